```python
import jax, jax.numpy as jnp
from jax import lax
import numpy as np

D_MODEL = 1024
BATCH = 16
SEQ = 2048
DEPTH = 1

LRU_WIDTH = D_MODEL
LRU_BLOCKS = 8
LRU_BLOCK_DIM = LRU_WIDTH // LRU_BLOCKS
CONV_WIDTH = 4
LRU_C = 8.0
HG_WIDTH = D_MODEL
HG_EXPAND = 128
HG_HEADS = HG_WIDTH // HG_EXPAND
HG_HEAD_V = HG_WIDTH // HG_HEADS
CHUNK = 64
HG_SCALE = HG_EXPAND ** -0.5
N_BRANCH = 2
EPS = 1e-6
IN_COLS = 2 * LRU_WIDTH + 4 * HG_WIDTH + N_BRANCH * D_MODEL
SPLITS = [LRU_WIDTH, 2 * LRU_WIDTH, 2 * LRU_WIDTH + HG_WIDTH, 2 * LRU_WIDTH + 2 * HG_WIDTH,
          2 * LRU_WIDTH + 3 * HG_WIDTH, 2 * LRU_WIDTH + 4 * HG_WIDTH]

kernel_name = "hybrid_hawk_hgrn2_gated_block"


def rms_norm(x, g):
    xf = x.astype(jnp.float32)
    y = xf * lax.rsqrt(jnp.mean(xf * xf, axis=-1, keepdims=True) + EPS)
    return y.astype(x.dtype) * g


def causal_depthwise_conv(x, w, b):
    s = x.shape[1]
    xp = jnp.pad(x, ((0, 0), (CONV_WIDTH - 1, 0), (0, 0)))
    return b + sum(xp[:, k:k + s] * w[k] for k in range(CONV_WIDTH))


def block_diag_linear(x, w, b):
    bsz, s, _ = x.shape
    xb = x.reshape(bsz, s, LRU_BLOCKS, LRU_BLOCK_DIM)
    y = jnp.einsum('bshi,hij->bshj', xb, w) + b
    return y.reshape(bsz, s, LRU_WIDTH)


def _linear_recurrence_combine(left, right):
    a_l, u_l = left
    a_r, u_r = right
    return a_l * a_r, a_r * u_l + u_r


def rg_lru(x, wx, bx, wa, ba, lam):
    xf = x.astype(jnp.float32)
    gate_i = jax.nn.sigmoid(block_diag_linear(xf, wx, bx))
    gate_r = jax.nn.sigmoid(block_diag_linear(xf, wa, ba))
    log_a = -LRU_C * gate_r * jax.nn.softplus(-lam.astype(jnp.float32))
    a = jnp.exp(log_a)
    mult = jnp.sqrt(-jnp.expm1(2.0 * log_a))
    u = mult * gate_i * xf
    _, h = lax.associative_scan(_linear_recurrence_combine, (a, u), axis=1)
    return h.astype(x.dtype)


def hgrn2_chunked(q, k, v, log_f):
    bsz, s, h, dk = q.shape
    dv = v.shape[-1]
    n = s // CHUNK

    def to_chunks(t):
        return t.reshape(bsz, n, CHUNK, h, t.shape[-1]).transpose(1, 0, 3, 2, 4)

    q, k, v, log_f = map(to_chunks, (q, k, v, log_f))
    b = jnp.cumsum(log_f, axis=3)
    b_mid = b[:, :, :, CHUNK // 2:CHUNK // 2 + 1]
    b_last = b[:, :, :, -1:]
    q_in = q * jnp.exp(b - b_mid) * HG_SCALE
    k_in = k * jnp.exp(b_mid - b)
    causal = jnp.tril(jnp.ones((CHUNK, CHUNK), dtype=bool))
    att = jnp.where(causal, jnp.einsum('nbhtd,nbhsd->nbhts', q_in, k_in), 0.0)
    o_intra = jnp.einsum('nbhts,nbhsv->nbhtv', att, v)
    q_inter = q * jnp.exp(b) * HG_SCALE
    k_state = k * jnp.exp(b_last - b)
    chunk_decay = jnp.exp(b_last[:, :, :, 0])

    def step(state, inp):
        qc, kc, vc, dc = inp
        o = jnp.einsum('bhtd,bhdv->bhtv', qc, state)
        state = state * dc[..., None] + jnp.einsum('bhsd,bhsv->bhdv', kc, vc)
        return state, o

    s0 = jnp.zeros((bsz, h, dk, dv), dtype=q.dtype)
    _, o_inter = lax.scan(step, s0, (q_inter, k_state, v, chunk_decay))
    o = o_intra + o_inter
    return o.transpose(1, 0, 3, 2, 4).reshape(bsz, s, h, dv)


def _fwd_setup_inputs(seed: int = 0) -> dict:
    key = jax.random.key(seed)
    ks = jax.random.split(key, 20)
    f32 = jnp.float32
    nrm = lambda k, shape, scale: jax.random.normal(k, shape, f32) * scale
    x = jax.random.normal(ks[0], (BATCH, SEQ, D_MODEL), f32)
    w_in = nrm(ks[1], (DEPTH, D_MODEL, IN_COLS), D_MODEL ** -0.5)
    b_merge = nrm(ks[2], (DEPTH, N_BRANCH * D_MODEL), 0.01)
    conv_w = nrm(ks[3], (DEPTH, CONV_WIDTH, LRU_WIDTH), CONV_WIDTH ** -0.5)
    conv_b = nrm(ks[4], (DEPTH, LRU_WIDTH), 0.01)
    rg_wx = nrm(ks[5], (DEPTH, LRU_BLOCKS, LRU_BLOCK_DIM, LRU_BLOCK_DIM), LRU_BLOCK_DIM ** -0.5)
    rg_bx = nrm(ks[6], (DEPTH, LRU_BLOCKS, LRU_BLOCK_DIM), 0.01)
    rg_wa = nrm(ks[7], (DEPTH, LRU_BLOCKS, LRU_BLOCK_DIM, LRU_BLOCK_DIM), LRU_BLOCK_DIM ** -0.5)
    rg_ba = nrm(ks[8], (DEPTH, LRU_BLOCKS, LRU_BLOCK_DIM), 0.01)
    u = jax.random.uniform(ks[9], (DEPTH, LRU_WIDTH), f32, minval=0.9, maxval=0.999)
    a0 = u ** (1.0 / LRU_C)
    rg_lambda = jnp.log(a0) - jnp.log1p(-a0)
    hg_lb_logits = nrm(ks[10], (DEPTH + 1, HG_WIDTH), 0.1)
    hg_norm_g = 1.0 + nrm(ks[11], (DEPTH, HG_HEAD_V), 0.05)
    proj_a = nrm(ks[12], (DEPTH, LRU_WIDTH, D_MODEL), LRU_WIDTH ** -0.5)
    proj_b = nrm(ks[13], (DEPTH, HG_WIDTH, D_MODEL), HG_WIDTH ** -0.5)
    w_out = nrm(ks[14], (DEPTH, D_MODEL, D_MODEL), D_MODEL ** -0.5)
    norm_g = 1.0 + nrm(ks[15], (DEPTH, D_MODEL), 0.05)
    final_norm_g = 1.0 + nrm(ks[16], (D_MODEL,), 0.05)
    return {"x": x, "w_in": w_in, "b_merge": b_merge, "conv_w": conv_w, "conv_b": conv_b,
            "rg_wx": rg_wx, "rg_bx": rg_bx, "rg_wa": rg_wa, "rg_ba": rg_ba, "rg_lambda": rg_lambda,
            "hg_lb_logits": hg_lb_logits, "hg_norm_g": hg_norm_g, "proj_a": proj_a, "proj_b": proj_b,
            "w_out": w_out, "norm_g": norm_g, "final_norm_g": final_norm_g}


def _fwd_reference(x, w_in, b_merge, conv_w, conv_b, rg_wx, rg_bx, rg_wa, rg_ba, rg_lambda,
              hg_lb_logits, hg_norm_g, proj_a, proj_b, w_out, norm_g, final_norm_g):
    bsz, s, _ = x.shape
    lb_all = jnp.cumsum(jax.nn.softmax(hg_lb_logits.astype(jnp.float32), axis=0), axis=0)
    for l in range(DEPTH):
        h = rms_norm(x, norm_g[l])
        z = h @ w_in[l]
        xa, ga, q, f_pre, i_in, gb, gm = jnp.split(z, SPLITS, axis=-1)

        xa = causal_depthwise_conv(xa, conv_w[l], conv_b[l])
        ya = rg_lru(xa, rg_wx[l], rg_bx[l], rg_wa[l], rg_ba[l], rg_lambda[l])
        ya = ya * jax.nn.silu(ga)
        out_a = ya @ proj_a[l]

        lb = lb_all[l]
        f = lb + (1.0 - lb) * jax.nn.sigmoid(f_pre.astype(jnp.float32))
        log_f = jnp.log(f)
        k = 1.0 - f
        qh = jax.nn.silu(q.astype(jnp.float32))
        heads = lambda t: t.reshape(bsz, s, HG_HEADS, t.shape[-1] // HG_HEADS)
        o = hgrn2_chunked(heads(qh), heads(k), heads(i_in.astype(jnp.float32)), heads(log_f))
        o = rms_norm(o, hg_norm_g[l]).reshape(bsz, s, HG_WIDTH).astype(x.dtype)
        yb = o * jax.nn.silu(gb)
        out_b = yb @ proj_b[l]

        gates = jax.nn.sigmoid(gm + b_merge[l])
        g_a, g_b = jnp.split(gates, N_BRANCH, axis=-1)
        mixed = g_a * out_a + g_b * out_b
        x = x + mixed @ w_out[l]
    return rms_norm(x, final_norm_g)


import jax as _jax
import jax.numpy as _jnp

TWIN_FORMAT = 'train_step'
FWD_PARAMS = ['x', 'w_in', 'b_merge', 'conv_w', 'conv_b', 'rg_wx', 'rg_bx', 'rg_wa', 'rg_ba', 'rg_lambda', 'hg_lb_logits', 'hg_norm_g', 'proj_a', 'proj_b', 'w_out', 'norm_g', 'final_norm_g']
TWIN_WEIGHTS = ['w_in', 'b_merge', 'conv_w', 'conv_b', 'rg_wx', 'rg_bx', 'rg_wa', 'rg_ba', 'rg_lambda', 'hg_lb_logits', 'hg_norm_g', 'proj_a', 'proj_b', 'w_out', 'norm_g', 'final_norm_g']
TWIN_DIFF_INPUT = 'x'
TWIN_INPUTS = ['x', 'w_in', 'b_merge', 'conv_w', 'conv_b', 'rg_wx', 'rg_bx', 'rg_wa', 'rg_ba', 'rg_lambda', 'hg_lb_logits', 'hg_norm_g', 'proj_a', 'proj_b', 'w_out', 'norm_g', 'final_norm_g', 'loss_target', 'm_w_in', 'm_b_merge', 'm_conv_w', 'm_conv_b', 'm_rg_wx', 'm_rg_bx', 'm_rg_wa', 'm_rg_ba', 'm_rg_lambda', 'm_hg_lb_logits', 'm_hg_norm_g', 'm_proj_a', 'm_proj_b', 'm_w_out', 'm_norm_g', 'm_final_norm_g', 'v_w_in', 'v_b_merge', 'v_conv_w', 'v_conv_b', 'v_rg_wx', 'v_rg_bx', 'v_rg_wa', 'v_rg_ba', 'v_rg_lambda', 'v_hg_lb_logits', 'v_hg_norm_g', 'v_proj_a', 'v_proj_b', 'v_w_out', 'v_norm_g', 'v_final_norm_g']
TWIN_OUTPUTS = ['loss', 'grad_x', 'grad_w_in', 'grad_b_merge', 'grad_conv_w', 'grad_conv_b', 'grad_rg_wx', 'grad_rg_bx', 'grad_rg_wa', 'grad_rg_ba', 'grad_rg_lambda', 'grad_hg_lb_logits', 'grad_hg_norm_g', 'grad_proj_a', 'grad_proj_b', 'grad_w_out', 'grad_norm_g', 'grad_final_norm_g', 'delta_w_in', 'delta_b_merge', 'delta_conv_w', 'delta_conv_b', 'delta_rg_wx', 'delta_rg_bx', 'delta_rg_wa', 'delta_rg_ba', 'delta_rg_lambda', 'delta_hg_lb_logits', 'delta_hg_norm_g', 'delta_proj_a', 'delta_proj_b', 'delta_w_out', 'delta_norm_g', 'delta_final_norm_g', 'new_m_w_in', 'new_m_b_merge', 'new_m_conv_w', 'new_m_conv_b', 'new_m_rg_wx', 'new_m_rg_bx', 'new_m_rg_wa', 'new_m_rg_ba', 'new_m_rg_lambda', 'new_m_hg_lb_logits', 'new_m_hg_norm_g', 'new_m_proj_a', 'new_m_proj_b', 'new_m_w_out', 'new_m_norm_g', 'new_m_final_norm_g', 'new_v_w_in', 'new_v_b_merge', 'new_v_conv_w', 'new_v_conv_b', 'new_v_rg_wx', 'new_v_rg_bx', 'new_v_rg_wa', 'new_v_rg_ba', 'new_v_rg_lambda', 'new_v_hg_lb_logits', 'new_v_hg_norm_g', 'new_v_proj_a', 'new_v_proj_b', 'new_v_w_out', 'new_v_norm_g', 'new_v_final_norm_g']
TWIN_LEAF_KINDS = {'loss': 'loss', 'grad_x': 'grad_x', 'grad_w_in': 'grad_w', 'grad_b_merge': 'grad_w', 'grad_conv_w': 'grad_w', 'grad_conv_b': 'grad_w', 'grad_rg_wx': 'grad_w', 'grad_rg_bx': 'grad_w', 'grad_rg_wa': 'grad_w', 'grad_rg_ba': 'grad_w', 'grad_rg_lambda': 'grad_w', 'grad_hg_lb_logits': 'grad_w', 'grad_hg_norm_g': 'grad_w', 'grad_proj_a': 'grad_w', 'grad_proj_b': 'grad_w', 'grad_w_out': 'grad_w', 'grad_norm_g': 'grad_w', 'grad_final_norm_g': 'grad_w', 'delta_w_in': 'delta_w', 'delta_b_merge': 'delta_w', 'delta_conv_w': 'delta_w', 'delta_conv_b': 'delta_w', 'delta_rg_wx': 'delta_w', 'delta_rg_bx': 'delta_w', 'delta_rg_wa': 'delta_w', 'delta_rg_ba': 'delta_w', 'delta_rg_lambda': 'delta_w', 'delta_hg_lb_logits': 'delta_w', 'delta_hg_norm_g': 'delta_w', 'delta_proj_a': 'delta_w', 'delta_proj_b': 'delta_w', 'delta_w_out': 'delta_w', 'delta_norm_g': 'delta_w', 'delta_final_norm_g': 'delta_w', 'new_m_w_in': 'new_m', 'new_m_b_merge': 'new_m', 'new_m_conv_w': 'new_m', 'new_m_conv_b': 'new_m', 'new_m_rg_wx': 'new_m', 'new_m_rg_bx': 'new_m', 'new_m_rg_wa': 'new_m', 'new_m_rg_ba': 'new_m', 'new_m_rg_lambda': 'new_m', 'new_m_hg_lb_logits': 'new_m', 'new_m_hg_norm_g': 'new_m', 'new_m_proj_a': 'new_m', 'new_m_proj_b': 'new_m', 'new_m_w_out': 'new_m', 'new_m_norm_g': 'new_m', 'new_m_final_norm_g': 'new_m', 'new_v_w_in': 'new_v', 'new_v_b_merge': 'new_v', 'new_v_conv_w': 'new_v', 'new_v_conv_b': 'new_v', 'new_v_rg_wx': 'new_v', 'new_v_rg_bx': 'new_v', 'new_v_rg_wa': 'new_v', 'new_v_rg_ba': 'new_v', 'new_v_rg_lambda': 'new_v', 'new_v_hg_lb_logits': 'new_v', 'new_v_hg_norm_g': 'new_v', 'new_v_proj_a': 'new_v', 'new_v_proj_b': 'new_v', 'new_v_w_out': 'new_v', 'new_v_norm_g': 'new_v', 'new_v_final_norm_g': 'new_v'}


def _forward(args):
    return _fwd_reference(*[args[k] for k in FWD_PARAMS])


def _output_shape():
    out = _jax.eval_shape(lambda: _forward(_fwd_setup_inputs(0)))
    return out.shape, out.dtype

N_MICROBATCH = 1
ADAM_LR = 0.001
ADAM_B1 = 0.9
ADAM_B2 = 0.999
ADAM_EPS = 1e-08
ADAM_WD = 0.01
ADAM_STEP = 10
PER_EXAMPLE_BATCH_AXIS = {'x': 0, 'loss_target': 0}
SHARED_INPUTS = []
_WEIGHT_DTYPES = {'w_in': _jnp.float32, 'b_merge': _jnp.float32, 'conv_w': _jnp.float32, 'conv_b': _jnp.float32, 'rg_wx': _jnp.float32, 'rg_bx': _jnp.float32, 'rg_wa': _jnp.float32, 'rg_ba': _jnp.float32, 'rg_lambda': _jnp.float32, 'hg_lb_logits': _jnp.float32, 'hg_norm_g': _jnp.float32, 'proj_a': _jnp.float32, 'proj_b': _jnp.float32, 'w_out': _jnp.float32, 'norm_g': _jnp.float32, 'final_norm_g': _jnp.float32}
MOMENT_SCALE = {'w_in': 3.374337e-02, 'b_merge': 1.919594e-02, 'conv_w': 3.729166e-02, 'conv_b': 3.977044e-01, 'rg_wx': 2.087048e-02, 'rg_bx': 1.283801e-02, 'rg_wa': 1.164579e-02, 'rg_ba': 9.467175e-03, 'rg_lambda': 1.795698e-02, 'hg_lb_logits': 5.093152e-03, 'hg_norm_g': 1.407552e-01, 'proj_a': 3.501109e-02, 'proj_b': 5.426529e-02, 'w_out': 6.444633e-02, 'norm_g': 9.114430e-02, 'final_norm_g': 3.206226e+01}


def _to_microbatches(a, axis):
    t = _jnp.moveaxis(a, axis, 0)
    t = t.reshape((N_MICROBATCH, t.shape[0] // N_MICROBATCH) + t.shape[1:])
    return _jnp.moveaxis(t, 1, axis + 1)


def setup_inputs(seed: int = 0) -> dict:
    inp = _fwd_setup_inputs(seed)
    key = _jax.random.fold_in(_jax.random.key(seed), 7919)
    shape, _ = _output_shape()
    out = dict(inp)
    out["loss_target"] = _jax.random.normal(_jax.random.fold_in(key, 0), shape, _jnp.float32)
    for i, name in enumerate(TWIN_WEIGHTS):
        w = inp[name].astype(_jnp.float32)
        if MOMENT_SCALE is None:
            s = _jnp.sqrt(_jnp.mean(_jnp.square(w)) + 1e-30)
        else:
            s = MOMENT_SCALE[name]
        km, kv = _jax.random.split(_jax.random.fold_in(key, i + 1))
        out[name] = w
        out["m_" + name] = s * _jax.random.normal(km, w.shape, _jnp.float32)
        out["v_" + name] = (s * s) * _jax.random.uniform(kv, w.shape, _jnp.float32, 0.5, 1.5)
    if N_MICROBATCH > 1:
        for name, axis in PER_EXAMPLE_BATCH_AXIS.items():
            out[name] = _to_microbatches(out[name], axis)
    return {'x': out['x'], 'w_in': out['w_in'], 'b_merge': out['b_merge'], 'conv_w': out['conv_w'], 'conv_b': out['conv_b'], 'rg_wx': out['rg_wx'], 'rg_bx': out['rg_bx'], 'rg_wa': out['rg_wa'], 'rg_ba': out['rg_ba'], 'rg_lambda': out['rg_lambda'], 'hg_lb_logits': out['hg_lb_logits'], 'hg_norm_g': out['hg_norm_g'], 'proj_a': out['proj_a'], 'proj_b': out['proj_b'], 'w_out': out['w_out'], 'norm_g': out['norm_g'], 'final_norm_g': out['final_norm_g'], 'loss_target': out['loss_target'], 'm_w_in': out['m_w_in'], 'm_b_merge': out['m_b_merge'], 'm_conv_w': out['m_conv_w'], 'm_conv_b': out['m_conv_b'], 'm_rg_wx': out['m_rg_wx'], 'm_rg_bx': out['m_rg_bx'], 'm_rg_wa': out['m_rg_wa'], 'm_rg_ba': out['m_rg_ba'], 'm_rg_lambda': out['m_rg_lambda'], 'm_hg_lb_logits': out['m_hg_lb_logits'], 'm_hg_norm_g': out['m_hg_norm_g'], 'm_proj_a': out['m_proj_a'], 'm_proj_b': out['m_proj_b'], 'm_w_out': out['m_w_out'], 'm_norm_g': out['m_norm_g'], 'm_final_norm_g': out['m_final_norm_g'], 'v_w_in': out['v_w_in'], 'v_b_merge': out['v_b_merge'], 'v_conv_w': out['v_conv_w'], 'v_conv_b': out['v_conv_b'], 'v_rg_wx': out['v_rg_wx'], 'v_rg_bx': out['v_rg_bx'], 'v_rg_wa': out['v_rg_wa'], 'v_rg_ba': out['v_rg_ba'], 'v_rg_lambda': out['v_rg_lambda'], 'v_hg_lb_logits': out['v_hg_lb_logits'], 'v_hg_norm_g': out['v_hg_norm_g'], 'v_proj_a': out['v_proj_a'], 'v_proj_b': out['v_proj_b'], 'v_w_out': out['v_w_out'], 'v_norm_g': out['v_norm_g'], 'v_final_norm_g': out['v_final_norm_g']}


def _loss(weights, diff, rest, loss_target):
    with _jax.named_scope("forward"):
        args = {**rest, TWIN_DIFF_INPUT: diff, **{k: w.astype(_WEIGHT_DTYPES[k]) for k, w in weights.items()}}
        y = _forward(args)
    with _jax.named_scope("loss_head"):
        err = _jnp.square(y.astype(_jnp.float32) - loss_target)
        return 0.5 * _jnp.sum(_jnp.mean(err, axis=-1)) if err.ndim else 0.5 * err


def _adamw(w, g, m, v):
    m = ADAM_B1 * m + (1.0 - ADAM_B1) * g
    v = ADAM_B2 * v + (1.0 - ADAM_B2) * _jnp.square(g)
    m_hat = m / (1.0 - ADAM_B1 ** ADAM_STEP)
    v_hat = v / (1.0 - ADAM_B2 ** ADAM_STEP)
    delta = -ADAM_LR * (m_hat / (_jnp.sqrt(v_hat) + ADAM_EPS) + ADAM_WD * w)
    return delta, m, v


def reference(x, w_in, b_merge, conv_w, conv_b, rg_wx, rg_bx, rg_wa, rg_ba, rg_lambda, hg_lb_logits, hg_norm_g, proj_a, proj_b, w_out, norm_g, final_norm_g, loss_target, m_w_in, m_b_merge, m_conv_w, m_conv_b, m_rg_wx, m_rg_bx, m_rg_wa, m_rg_ba, m_rg_lambda, m_hg_lb_logits, m_hg_norm_g, m_proj_a, m_proj_b, m_w_out, m_norm_g, m_final_norm_g, v_w_in, v_b_merge, v_conv_w, v_conv_b, v_rg_wx, v_rg_bx, v_rg_wa, v_rg_ba, v_rg_lambda, v_hg_lb_logits, v_hg_norm_g, v_proj_a, v_proj_b, v_w_out, v_norm_g, v_final_norm_g):
    given = dict(x=x, w_in=w_in, b_merge=b_merge, conv_w=conv_w, conv_b=conv_b, rg_wx=rg_wx, rg_bx=rg_bx, rg_wa=rg_wa, rg_ba=rg_ba, rg_lambda=rg_lambda, hg_lb_logits=hg_lb_logits, hg_norm_g=hg_norm_g, proj_a=proj_a, proj_b=proj_b, w_out=w_out, norm_g=norm_g, final_norm_g=final_norm_g, loss_target=loss_target, m_w_in=m_w_in, m_b_merge=m_b_merge, m_conv_w=m_conv_w, m_conv_b=m_conv_b, m_rg_wx=m_rg_wx, m_rg_bx=m_rg_bx, m_rg_wa=m_rg_wa, m_rg_ba=m_rg_ba, m_rg_lambda=m_rg_lambda, m_hg_lb_logits=m_hg_lb_logits, m_hg_norm_g=m_hg_norm_g, m_proj_a=m_proj_a, m_proj_b=m_proj_b, m_w_out=m_w_out, m_norm_g=m_norm_g, m_final_norm_g=m_final_norm_g, v_w_in=v_w_in, v_b_merge=v_b_merge, v_conv_w=v_conv_w, v_conv_b=v_conv_b, v_rg_wx=v_rg_wx, v_rg_bx=v_rg_bx, v_rg_wa=v_rg_wa, v_rg_ba=v_rg_ba, v_rg_lambda=v_rg_lambda, v_hg_lb_logits=v_hg_lb_logits, v_hg_norm_g=v_hg_norm_g, v_proj_a=v_proj_a, v_proj_b=v_proj_b, v_w_out=v_w_out, v_norm_g=v_norm_g, v_final_norm_g=v_final_norm_g)
    weights = {n: given[n] for n in TWIN_WEIGHTS}
    shared = {n: given[n] for n in SHARED_INPUTS}
    per_example = {n: given[n] for n in ['x']}
    grad_fn = _jax.value_and_grad(_loss, argnums=(0, 1))

    def one_microbatch(ex, loss_target):
        ex = dict(ex)
        diff = ex.pop(TWIN_DIFF_INPUT)
        return grad_fn(weights, diff, {**shared, **ex}, loss_target)

    if N_MICROBATCH == 1:
        loss, (grad_w, grad_x) = one_microbatch(per_example, given["loss_target"])
    else:
        def body(carry, xs):
            loss_sum, grad_sum = carry
            l_k, (gw_k, gx_k) = one_microbatch(xs[0], xs[1])
            with _jax.named_scope("update"):
                return (loss_sum + l_k, _jax.tree.map(_jnp.add, grad_sum, gw_k)), gx_k

        init = (_jnp.zeros((), _jnp.float32), _jax.tree.map(_jnp.zeros_like, weights))
        (loss, grad_w), grad_x = _jax.lax.scan(body, init, (per_example, given["loss_target"]))
    with _jax.named_scope("update"):
        delta_w, new_m, new_v = {}, {}, {}
        for n in TWIN_WEIGHTS:
            delta_w[n], new_m[n], new_v[n] = _adamw(weights[n], grad_w[n], given["m_" + n], given["v_" + n])
    return (loss, grad_x, *[grad_w[n] for n in TWIN_WEIGHTS], *[delta_w[n] for n in TWIN_WEIGHTS],
            *[new_m[n] for n in TWIN_WEIGHTS], *[new_v[n] for n in TWIN_WEIGHTS])
```

```python
import jax
import jax.numpy as jnp
from jax import lax
from jax.experimental import pallas as pl
from jax.experimental.pallas import tpu as pltpu

F32 = jnp.float32
BF16 = jnp.bfloat16
SDS = jax.ShapeDtypeStruct
MESH = pl.DeviceIdType.MESH
ANY = pl.BlockSpec(memory_space=pl.ANY)

D = 1024
NB = 8
BD = D // NB
CHUNK = 64
EPS = 1e-6
LRU_C = 8.0
HG_SCALE = BD ** -0.5
ADAM_LR, ADAM_B1, ADAM_B2, ADAM_EPS, ADAM_WD, ADAM_STEP = 0.001, 0.9, 0.999, 1e-08, 0.01, 10

NT_DIMS = (((1,), (1,)), ((), ()))
TN_DIMS = (((0,), (0,)), ((), ()))


def _params(vmem_mib):
    return pltpu.CompilerParams(vmem_limit_bytes=vmem_mib << 20)


def _row_tile(rows, most=256):
    assert rows % 8 == 0
    return max(t for t in range(8, min(rows, most) + 1, 8) if rows % t == 0)


def _sigmoid(v):
    return jax.nn.sigmoid(v)


def _softplus_neg(lam):
    t = -lam
    e = jnp.exp(-jnp.abs(t))
    w = 1.0 + e
    d = w - 1.0
    l1p = jnp.where(d == 0.0, e, jnp.log(w) * (e / jnp.where(d == 0.0, 1.0, d)))
    return jnp.maximum(t, 0.0) + l1p


def _inproj_fwd(x2, norm_g, w_all):
    n = x2.shape[0]
    tm = min(n, 1024)

    def body(x_ref, g_ref, w_ref, z_ref, h_ref):
        @pl.when(pl.program_id(1) == 0)
        def _():
            x = x_ref[...]
            r = lax.rsqrt(jnp.mean(x * x, axis=-1, keepdims=True) + EPS)
            h_ref[...] = ((x * r) * g_ref[...]).astype(BF16)

        z_ref[0] = jnp.dot(h_ref[...], w_ref[0], preferred_element_type=F32)

    return pl.pallas_call(
        body, name="inproj_fwd", grid=(n // tm, NB),
        in_specs=[pl.BlockSpec((tm, D), lambda i, j: (i, 0)),
                  pl.BlockSpec((1, D), lambda i, j: (0, 0)),
                  pl.BlockSpec((1, D, D), lambda i, j: (j, 0, 0))],
        out_specs=[pl.BlockSpec((1, tm, D), lambda i, j: (j, i, 0)),
                   pl.BlockSpec((tm, D), lambda i, j: (i, 0))],
        out_shape=[SDS((NB, n, D), F32), SDS((n, D), BF16)],
        compiler_params=_params(48),
    )(x2, norm_g, w_all)


LRU_T = 256


def _conv(ext, cw, cb):
    t = LRU_T
    acc = ext[5:5 + t, :] * cw[0:1, :] + ext[6:6 + t, :] * cw[1:2, :]
    acc = acc + ext[7:7 + t, :] * cw[2:3, :]
    acc = acc + ext[8:8 + t, :] * cw[3:4, :]
    return cb + acc


def _lru_gates(xa, wx_ref, wa_ref, bx, ba, lam):
    xab = xa.astype(BF16)
    pis, prs = [], []
    for h in range(NB):
        xs = xab[:, h * BD:(h + 1) * BD]
        pis.append(jnp.dot(xs, wx_ref[h], preferred_element_type=F32))
        prs.append(jnp.dot(xs, wa_ref[h], preferred_element_type=F32))
    gi = _sigmoid(jnp.concatenate(pis, axis=1) + bx)
    gr = _sigmoid(jnp.concatenate(prs, axis=1) + ba)
    sp = _softplus_neg(lam)
    log_a = (-LRU_C * gr) * sp
    a = jnp.exp(log_a)
    mult = jnp.sqrt(-jnp.tanh(log_a) * (a * a + 1.0))
    return xab, gi, gr, sp, a, mult


def _lru_fwd(z, cw8, cb, wx, wa, bx, ba, lam, nb, s_len):
    n = nb * s_len
    t = LRU_T
    ns = s_len // t

    def body(xp_ref, ga_ref, cw_ref, cb_ref, wx_ref, wa_ref, bx_ref, ba_ref, lam_ref,
             h_ref, ya_ref, ext, a_s, u_s, carry):
        @pl.when(pl.program_id(1) == 0)
        def _():
            ext[0:8, :] = jnp.zeros((8, D), F32)
            carry[...] = jnp.zeros((8, D), F32)

        ext[8:8 + t, :] = xp_ref[0]
        xa = _conv(ext, cw_ref[...], cb_ref[...])
        ext[0:8, :] = ext[t:t + 8, :]
        _, gi, _, _, a, mult = _lru_gates(xa, wx_ref, wa_ref, bx_ref[...], ba_ref[...], lam_ref[...])
        u = (mult * gi) * xa
        row = lax.broadcasted_iota(jnp.int32, (t, D), 0) & 7
        for sh in (1, 2, 4):
            a_sh = pltpu.roll(a, sh, 0)
            u_sh = pltpu.roll(u, sh, 0)
            m = row >= sh
            u = jnp.where(m, a * u_sh + u, u)
            a = jnp.where(m, a * a_sh, a)
        a_s[...] = a
        u_s[...] = u

        def step(g, c):
            r = pl.multiple_of(g * 8, 8)
            hg = u_s[pl.ds(r, 8), :] + a_s[pl.ds(r, 8), :] * c
            h_ref[pl.ds(r, 8), :] = hg
            return hg[7:8, :]

        c_out = lax.fori_loop(0, t // 8, step, carry[0:1, :], unroll=4)
        carry[0:1, :] = c_out
        ga = ga_ref[0]
        ya_ref[...] = (h_ref[...] * (ga * _sigmoid(ga))).astype(BF16)

    row_map = lambda b, s: (b * ns + s, 0)
    rep2 = lambda b, s: (0, 0)
    rep3 = lambda b, s: (0, 0, 0)
    return pl.pallas_call(
        body, name="lru_fwd", grid=(nb, ns),
        in_specs=[pl.BlockSpec((1, t, D), lambda b, s: (0, b * ns + s, 0)),
                  pl.BlockSpec((1, t, D), lambda b, s: (1, b * ns + s, 0)),
                  pl.BlockSpec((8, D), rep2), pl.BlockSpec((1, D), rep2),
                  pl.BlockSpec((NB, BD, BD), rep3), pl.BlockSpec((NB, BD, BD), rep3),
                  pl.BlockSpec((1, D), rep2), pl.BlockSpec((1, D), rep2), pl.BlockSpec((1, D), rep2)],
        out_specs=[pl.BlockSpec((t, D), row_map), pl.BlockSpec((t, D), row_map)],
        out_shape=[SDS((n, D), F32), SDS((n, D), BF16)],
        scratch_shapes=[pltpu.VMEM((t + 8, D), F32), pltpu.VMEM((t, D), F32), pltpu.VMEM((t, D), F32),
                        pltpu.VMEM((8, D), F32)],
        compiler_params=_params(48),
    )(z, z, cw8, cb, wx, wa, bx, ba, lam)


def _lru_bwd(z, h_all, dya, cw8, cb, wx, wa, bx, ba, lam, nb, s_len):
    n = nb * s_len
    t = LRU_T
    ns = s_len // t
    t8 = t // 8

    def body(xp_ref, xph_ref, ga_ref, h_ref, hh_ref, dya_ref, cw_ref, cb_ref, wx_ref, wa_ref, bx_ref, ba_ref,
             lam_ref, dz_ref, gcw_ref, gcb_ref, gwx_ref, gwa_ref, gbx_ref, gba_ref, glam_ref,
             ext, hext, dext, a_s, u_s, dh_s, carry):
        b, s = pl.program_id(0), pl.program_id(1)
        first_tile = s == ns - 1

        @pl.when((b == 0) & (s == 0))
        def _():
            for ref in (gcw_ref, gcb_ref, gwx_ref, gwa_ref, gbx_ref, gba_ref, glam_ref):
                ref[...] = jnp.zeros(ref.shape, F32)

        @pl.when(s == 0)
        def _():
            dext[t:t + 8, :] = jnp.zeros((8, D), F32)
            carry[...] = jnp.zeros((8, D), F32)

        keep = jnp.where(first_tile, 0.0, 1.0)
        ext[0:8, :] = xph_ref[0] * keep
        ext[8:8 + t, :] = xp_ref[0]
        hext[0:8, :] = hh_ref[...] * keep
        hext[8:8 + t, :] = h_ref[...]
        cw = cw_ref[...]
        lam = lam_ref[...]
        xa = _conv(ext, cw, cb_ref[...])
        xab, gi, gr, sp, a, mult = _lru_gates(xa, wx_ref, wa_ref, bx_ref[...], ba_ref[...], lam)
        h_prev = hext[7:7 + t, :]
        ga = ga_ref[0]
        sg = _sigmoid(ga)
        dya_v = dya_ref[...]
        d_ga = dya_v * h_ref[...] * (sg * (1.0 + ga * (1.0 - sg)))
        g_in = dya_v * (ga * sg)

        rows = lax.broadcasted_iota(jnp.int32, (t, D), 0)
        row = rows & 7
        an = jnp.where(rows == t - 1, 1.0, pltpu.roll(a, t - 1, 0))
        u = g_in
        for sh in (1, 2, 4):
            a_sh = pltpu.roll(an, t - sh, 0)
            u_sh = pltpu.roll(u, t - sh, 0)
            m = row < 8 - sh
            u = jnp.where(m, u + an * u_sh, u)
            an = jnp.where(m, an * a_sh, an)
        a_s[...] = an
        u_s[...] = u

        def step(i, c):
            r = pl.multiple_of((t8 - 1 - i) * 8, 8)
            dg = u_s[pl.ds(r, 8), :] + a_s[pl.ds(r, 8), :] * c
            dh_s[pl.ds(r, 8), :] = dg
            return dg[0:1, :]

        lax.fori_loop(0, t8, step, carry[0:1, :], unroll=4)
        dh = dh_s[...]
        carry[0:1, :] = a[0:1, :] * dh[0:1, :]

        d_a = dh * h_prev
        dux = dh * xa
        d_mult = dux * gi
        d_gi = dux * mult
        d_xa = dh * (mult * gi)
        d_loga = d_a * a - d_mult * ((a * a) / mult)
        d_gr = d_loga * (-LRU_C * sp)
        d_sp = jnp.sum(d_loga * (-LRU_C * gr), axis=0, keepdims=True)
        glam_ref[...] += d_sp * (-_sigmoid(-lam))
        d_pi = d_gi * gi * (1.0 - gi)
        d_pr = d_gr * gr * (1.0 - gr)
        gbx_ref[...] += jnp.sum(d_pi, axis=0, keepdims=True)
        gba_ref[...] += jnp.sum(d_pr, axis=0, keepdims=True)
        dpib = d_pi.astype(BF16)
        dprb = d_pr.astype(BF16)
        back = []
        for h in range(NB):
            cs = slice(h * BD, (h + 1) * BD)
            gwx_ref[h] += lax.dot_general(xab[:, cs], dpib[:, cs], TN_DIMS, preferred_element_type=F32)
            gwa_ref[h] += lax.dot_general(xab[:, cs], dprb[:, cs], TN_DIMS, preferred_element_type=F32)
            back.append(lax.dot_general(dpib[:, cs], wx_ref[h], NT_DIMS, preferred_element_type=F32)
                        + lax.dot_general(dprb[:, cs], wa_ref[h], NT_DIMS, preferred_element_type=F32))
        d_xa = d_xa + jnp.concatenate(back, axis=1)

        dext[0:t, :] = d_xa
        d_xp = dext[3:3 + t, :] * cw[0:1, :] + dext[2:2 + t, :] * cw[1:2, :]
        d_xp = d_xp + dext[1:1 + t, :] * cw[2:3, :]
        d_xp = d_xp + d_xa * cw[3:4, :]
        dext[t:t + 8, :] = d_xa[0:8, :]
        gcb_ref[...] += jnp.sum(d_xa, axis=0, keepdims=True)
        for k in range(4):
            gcw_ref[k:k + 1, :] += jnp.sum(d_xa * ext[5 + k:5 + k + t, :], axis=0, keepdims=True)
        dz_ref[0] = d_xp.astype(BF16)
        dz_ref[1] = d_ga.astype(BF16)

    rb = lambda b, s: b * ns + (ns - 1 - s)
    halo = lambda b, s: jnp.maximum(rb(b, s) * t8 - 1, 0)
    rep2 = lambda b, s: (0, 0)
    rep3 = lambda b, s: (0, 0, 0)
    return pl.pallas_call(
        body, name="lru_bwd", grid=(nb, ns),
        in_specs=[pl.BlockSpec((1, t, D), lambda b, s: (0, rb(b, s), 0)),
                  pl.BlockSpec((1, 8, D), lambda b, s: (0, halo(b, s), 0)),
                  pl.BlockSpec((1, t, D), lambda b, s: (1, rb(b, s), 0)),
                  pl.BlockSpec((t, D), lambda b, s: (rb(b, s), 0)),
                  pl.BlockSpec((8, D), lambda b, s: (halo(b, s), 0)),
                  pl.BlockSpec((t, D), lambda b, s: (rb(b, s), 0)),
                  pl.BlockSpec((8, D), rep2), pl.BlockSpec((1, D), rep2),
                  pl.BlockSpec((NB, BD, BD), rep3), pl.BlockSpec((NB, BD, BD), rep3),
                  pl.BlockSpec((1, D), rep2), pl.BlockSpec((1, D), rep2), pl.BlockSpec((1, D), rep2)],
        out_specs=[pl.BlockSpec((2, t, D), lambda b, s: (0, rb(b, s), 0)),
                   pl.BlockSpec((8, D), rep2), pl.BlockSpec((1, D), rep2),
                   pl.BlockSpec((NB, BD, BD), rep3), pl.BlockSpec((NB, BD, BD), rep3),
                   pl.BlockSpec((1, D), rep2), pl.BlockSpec((1, D), rep2), pl.BlockSpec((1, D), rep2)],
        out_shape=[SDS((2, n, D), BF16), SDS((8, D), F32), SDS((1, D), F32),
                   SDS((NB, BD, BD), F32), SDS((NB, BD, BD), F32),
                   SDS((1, D), F32), SDS((1, D), F32), SDS((1, D), F32)],
        scratch_shapes=[pltpu.VMEM((t + 8, D), F32), pltpu.VMEM((t + 8, D), F32), pltpu.VMEM((t + 8, D), F32),
                        pltpu.VMEM((t, D), F32), pltpu.VMEM((t, D), F32), pltpu.VMEM((t, D), F32),
                        pltpu.VMEM((8, D), F32)],
        compiler_params=_params(56),
    )(z, z, z, h_all, h_all, dya, cw8, cb, wx, wa, bx, ba, lam)


HG_T = 256
HG_NC = HG_T // CHUNK


def _lower_bound(lg):
    m = jnp.max(lg, axis=0, keepdims=True)
    e = jnp.exp(lg - m)
    return e[0:1, :] / jnp.sum(e, axis=0, keepdims=True)


def _tri(upper):
    r = lax.broadcasted_iota(jnp.int32, (CHUNK, CHUNK), 0)
    c = lax.broadcasted_iota(jnp.int32, (CHUNK, CHUNK), 1)
    return ((c >= r) if upper else (r >= c))


def _hg_chunk(q, fp, lb):
    sig = _sigmoid(fp)
    f = lb + (1.0 - lb) * sig
    log_f = jnp.log(f)
    k = 1.0 - f
    b = jnp.dot(_tri(False).astype(F32), log_f, precision=lax.Precision.HIGHEST, preferred_element_type=F32)
    b_mid = b[CHUNK // 2:CHUNK // 2 + 1, :]
    b_last = b[CHUNK - 1:CHUNK, :]
    sq = _sigmoid(q)
    qh = q * sq
    e_qi = jnp.exp(b - b_mid)
    e_ki = jnp.exp(b_mid - b)
    e_qs = jnp.exp(b)
    e_ks = jnp.exp(b_last - b)
    dc = jnp.exp(b_last)
    q_in = (qh * e_qi) * HG_SCALE
    k_in = k * e_ki
    q_st = (qh * e_qs) * HG_SCALE
    k_st = k * e_ks
    att = lax.dot_general(q_in.astype(BF16), k_in.astype(BF16), NT_DIMS, preferred_element_type=F32)
    att = jnp.where(_tri(False), att, 0.0)
    return dict(sig=sig, f=f, k=k, sq=sq, e_qi=e_qi, e_ki=e_ki, e_qs=e_qs, e_ks=e_ks, dc=dc,
                q_in=q_in, k_in=k_in, q_st=q_st, k_st=k_st, att=att)


def _hgrn_fwd(z, lb_logits, hg_g, nb, s_len):
    n = nb * s_len
    t = HG_T
    ns = s_len // t
    nchunk = s_len // CHUNK

    def body(q_ref, f_ref, v_ref, gb_ref, lg_ref, g_ref, o_ref, yb_ref, st_ref, st):
        @pl.when(pl.program_id(2) == 0)
        def _():
            st[...] = jnp.zeros((BD, BD), F32)

        lb = _lower_bound(lg_ref[...])
        for c in range(HG_NC):
            rows = slice(c * CHUNK, (c + 1) * CHUNK)
            ck = _hg_chunk(q_ref[0, rows, :], f_ref[0, rows, :], lb)
            vb = v_ref[0, rows, :].astype(BF16)
            st_prev = st[...]
            st_ref[c] = st_prev
            o = jnp.dot(ck["att"].astype(BF16), vb, preferred_element_type=F32)
            o = o + lax.dot_general(ck["q_st"].astype(BF16), st_prev.astype(BF16), NT_DIMS,
                                    preferred_element_type=F32)
            st[...] = st_prev * ck["dc"] + lax.dot_general(vb, ck["k_st"].astype(BF16), TN_DIMS,
                                                           preferred_element_type=F32)
            o_ref[rows, :] = o
            r = lax.rsqrt(jnp.mean(o * o, axis=-1, keepdims=True) + EPS)
            gb = gb_ref[0, rows, :]
            yb_ref[rows, :] = (((o * r) * g_ref[...]) * (gb * _sigmoid(gb))).astype(BF16)

    seg = lambda j: pl.BlockSpec((1, t, BD), lambda h, b, s: (j, b * ns + s, h))
    tile = pl.BlockSpec((t, BD), lambda h, b, s: (b * ns + s, h))
    return pl.pallas_call(
        body, name="hgrn_fwd", grid=(NB, nb, ns),
        in_specs=[seg(2), seg(3), seg(4), seg(5),
                  pl.BlockSpec((2, BD), lambda h, b, s: (0, h)),
                  pl.BlockSpec((1, BD), lambda h, b, s: (0, 0))],
        out_specs=[tile, tile,
                   pl.BlockSpec((HG_NC, BD, BD), lambda h, b, s: ((b * NB + h) * ns + s, 0, 0))],
        out_shape=[SDS((n, D), F32), SDS((n, D), BF16), SDS((nb * NB * nchunk, BD, BD), F32)],
        scratch_shapes=[pltpu.VMEM((BD, BD), F32)],
        compiler_params=_params(32),
    )(z, z, z, z, lb_logits, hg_g)


def _hgrn_bwd(z, o_all, st_all, dyb, lb_logits, hg_g, nb, s_len):
    n = nb * s_len
    t = HG_T
    ns = s_len // t

    def body(q_ref, f_ref, v_ref, gb_ref, o_ref, st_ref, dyb_ref, lg_ref, g_ref,
             dz_ref, glg_ref, ghg_ref, dst, dlb):
        h, b, s = pl.program_id(0), pl.program_id(1), pl.program_id(2)

        @pl.when((h == 0) & (b == 0) & (s == 0))
        def _():
            ghg_ref[...] = jnp.zeros((1, BD), F32)

        @pl.when((b == 0) & (s == 0))
        def _():
            dlb[...] = jnp.zeros((8, BD), F32)

        @pl.when(s == 0)
        def _():
            dst[...] = jnp.zeros((BD, BD), F32)

        lb = _lower_bound(lg_ref[...])
        g = g_ref[...]
        rowi = lax.broadcasted_iota(jnp.int32, (CHUNK, BD), 0)
        ghg = jnp.zeros((1, BD), F32)
        dlb_acc = jnp.zeros((1, BD), F32)
        for c in reversed(range(HG_NC)):
            rows = slice(c * CHUNK, (c + 1) * CHUNK)
            q = q_ref[0, rows, :]
            ck = _hg_chunk(q, f_ref[0, rows, :], lb)
            v = v_ref[0, rows, :]
            vb = v.astype(BF16)
            gb = gb_ref[0, rows, :]
            o = o_ref[rows, :]
            dyb_v = dyb_ref[rows, :]
            st_prev = st_ref[c]
            dst_new = dst[...]

            sgb = _sigmoid(gb)
            r = lax.rsqrt(jnp.mean(o * o, axis=-1, keepdims=True) + EPS)
            ohat = o * r
            d_on = dyb_v * (gb * sgb)
            d_gb = dyb_v * (ohat * g) * (sgb * (1.0 + gb * (1.0 - sgb)))
            ghg = ghg + jnp.sum(d_on * ohat, axis=0, keepdims=True)
            tt = d_on * g
            d_o = r * (tt - ohat * jnp.mean(tt * ohat, axis=-1, keepdims=True))
            dob = d_o.astype(BF16)

            attb = ck["att"].astype(BF16)
            q_inb, k_inb = ck["q_in"].astype(BF16), ck["k_in"].astype(BF16)
            q_stb, k_stb = ck["q_st"].astype(BF16), ck["k_st"].astype(BF16)
            dstb = dst_new.astype(BF16)
            d_att = lax.dot_general(dob, vb, NT_DIMS, preferred_element_type=F32)
            d_att = jnp.where(_tri(False), d_att, 0.0).astype(BF16)
            d_v = lax.dot_general(attb, dob, TN_DIMS, preferred_element_type=F32)
            d_v = d_v + lax.dot_general(k_stb, dstb, NT_DIMS, preferred_element_type=F32)
            d_q_in = jnp.dot(d_att, k_inb, preferred_element_type=F32)
            d_k_in = lax.dot_general(d_att, q_inb, TN_DIMS, preferred_element_type=F32)
            d_q_st = jnp.dot(dob, st_prev.astype(BF16), preferred_element_type=F32)
            d_k_st = jnp.dot(vb, dstb, preferred_element_type=F32)
            d_dc = jnp.sum(dst_new * st_prev, axis=0, keepdims=True)
            dst[...] = dst_new * ck["dc"] + lax.dot_general(dob, q_stb, TN_DIMS, preferred_element_type=F32)

            p_qi = d_q_in * ck["q_in"]
            p_ki = d_k_in * ck["k_in"]
            p_qs = d_q_st * ck["q_st"]
            p_ks = d_k_st * ck["k_st"]
            d_qh = (d_q_in * ck["e_qi"] + d_q_st * ck["e_qs"]) * HG_SCALE
            d_k = d_k_in * ck["e_ki"] + d_k_st * ck["e_ks"]
            d_b = (p_qi - p_ki) + (p_qs - p_ks)
            d_b_mid = jnp.sum(p_ki - p_qi, axis=0, keepdims=True)
            d_b_last = jnp.sum(p_ks, axis=0, keepdims=True) + d_dc * ck["dc"]
            d_b = d_b + jnp.where(rowi == CHUNK // 2, d_b_mid, 0.0) + jnp.where(rowi == CHUNK - 1, d_b_last, 0.0)
            d_logf = jnp.dot(_tri(True).astype(F32), d_b, precision=lax.Precision.HIGHEST,
                             preferred_element_type=F32)
            d_f = d_logf / ck["f"] - d_k
            sig, sq = ck["sig"], ck["sq"]
            d_fp = d_f * (1.0 - lb) * (sig * (1.0 - sig))
            dlb_acc = dlb_acc + jnp.sum(d_f * (1.0 - sig), axis=0, keepdims=True)
            d_q = d_qh * (sq * (1.0 + q * (1.0 - sq)))
            dz_ref[0, rows, :] = d_q.astype(BF16)
            dz_ref[1, rows, :] = d_fp.astype(BF16)
            dz_ref[2, rows, :] = d_v.astype(BF16)
            dz_ref[3, rows, :] = d_gb.astype(BF16)

        ghg_ref[...] += ghg
        dlb[0:1, :] += dlb_acc

        @pl.when((b == nb - 1) & (s == ns - 1))
        def _():
            dl = dlb[0:1, :] * (lb * (1.0 - lb))
            glg_ref[0:1, :] = dl
            glg_ref[1:2, :] = -dl

    rb = lambda b, s: b * ns + (ns - 1 - s)
    seg = lambda j: pl.BlockSpec((1, t, BD), lambda h, b, s: (j, rb(b, s), h))
    tile = pl.BlockSpec((t, BD), lambda h, b, s: (rb(b, s), h))
    return pl.pallas_call(
        body, name="hgrn_bwd", grid=(NB, nb, ns),
        in_specs=[seg(2), seg(3), seg(4), seg(5), tile,
                  pl.BlockSpec((HG_NC, BD, BD), lambda h, b, s: ((b * NB + h) * ns + (ns - 1 - s), 0, 0)),
                  tile,
                  pl.BlockSpec((2, BD), lambda h, b, s: (0, h)),
                  pl.BlockSpec((1, BD), lambda h, b, s: (0, 0))],
        out_specs=[pl.BlockSpec((4, t, BD), lambda h, b, s: (0, rb(b, s), h)),
                   pl.BlockSpec((2, BD), lambda h, b, s: (0, h)),
                   pl.BlockSpec((1, BD), lambda h, b, s: (0, 0))],
        out_shape=[SDS((4, n, D), BF16), SDS((2, D), F32), SDS((1, BD), F32)],
        scratch_shapes=[pltpu.VMEM((BD, BD), F32), pltpu.VMEM((8, BD), F32)],
        compiler_params=_params(32),
    )(z, z, z, z, o_all, st_all, dyb, lb_logits, hg_g)


def _mid(ya, yb, z, b_merge, x2, tgt, fin_g, pa, pb, wo):
    n = x2.shape[0]
    tm = 256
    ni = n // tm

    def body(ya_ref, yb_ref, gma_ref, gmb_ref, bm_ref, x_ref, t_ref, fg_ref, pa_hbm, pb_hbm, wo_hbm,
             dx2_ref, dya_ref, dyb_ref, dgm_ref, loss_ref, gfg_ref, gbm_ref, gpa_hbm, gpb_hbm, gwo_hbm,
             pa_v, pb_v, wo_v, gpa_v, gpb_v, gwo_v, sem):
        i = pl.program_id(0)
        loads = [pltpu.make_async_copy(src, dst, sem.at[k])
                 for k, (src, dst) in enumerate(((pa_hbm, pa_v), (pb_hbm, pb_v), (wo_hbm, wo_v)))]
        stores = [pltpu.make_async_copy(src, dst, sem.at[k])
                  for k, (src, dst) in enumerate(((gpa_v, gpa_hbm), (gpb_v, gpb_hbm), (gwo_v, gwo_hbm)))]

        @pl.when(i == 0)
        def _():
            for cp in loads:
                cp.start()
            for ref in (gpa_v, gpb_v, gwo_v, loss_ref, gfg_ref, gbm_ref):
                ref[...] = jnp.zeros(ref.shape, F32)
            for cp in loads:
                cp.wait()

        ya_v = ya_ref[...]
        yb_v = yb_ref[...]
        out_a = jnp.dot(ya_v, pa_v[...], preferred_element_type=F32)
        out_b = jnp.dot(yb_v, pb_v[...], preferred_element_type=F32)
        bm = bm_ref[...]
        g_a = _sigmoid(gma_ref[0] + bm[:, 0:D])
        g_b = _sigmoid(gmb_ref[0] + bm[:, D:2 * D])
        mixed = g_a * out_a + g_b * out_b
        mixb = mixed.astype(BF16)
        xo = x_ref[...] + jnp.dot(mixb, wo_v[...], preferred_element_type=F32)
        r = lax.rsqrt(jnp.mean(xo * xo, axis=-1, keepdims=True) + EPS)
        xn = xo * r
        fg = fg_ref[...]
        e = xn * fg - t_ref[...]
        loss_ref[...] += 0.5 * jnp.sum(jnp.mean(e * e, axis=-1, keepdims=True))
        dy = e * (1.0 / D)
        gfg_ref[...] += jnp.sum(dy * xn, axis=0, keepdims=True)
        dxn = dy * fg
        dx2 = r * (dxn - xn * jnp.mean(dxn * xn, axis=-1, keepdims=True))
        dx2_ref[...] = dx2
        dx2b = dx2.astype(BF16)
        d_mixed = lax.dot_general(dx2b, wo_v[...], NT_DIMS, preferred_element_type=F32)
        gwo_v[...] += lax.dot_general(mixb, dx2b, TN_DIMS, preferred_element_type=F32)
        d_oa = (d_mixed * g_a).astype(BF16)
        d_ob = (d_mixed * g_b).astype(BF16)
        dgm_a = (d_mixed * out_a) * (g_a * (1.0 - g_a))
        dgm_b = (d_mixed * out_b) * (g_b * (1.0 - g_b))
        gbm_ref[:, 0:D] += jnp.sum(dgm_a, axis=0, keepdims=True)
        gbm_ref[:, D:2 * D] += jnp.sum(dgm_b, axis=0, keepdims=True)
        dgm_ref[0] = dgm_a.astype(BF16)
        dgm_ref[1] = dgm_b.astype(BF16)
        dya_ref[...] = lax.dot_general(d_oa, pa_v[...], NT_DIMS, preferred_element_type=F32)
        dyb_ref[...] = lax.dot_general(d_ob, pb_v[...], NT_DIMS, preferred_element_type=F32)
        gpa_v[...] += lax.dot_general(ya_v, d_oa, TN_DIMS, preferred_element_type=F32)
        gpb_v[...] += lax.dot_general(yb_v, d_ob, TN_DIMS, preferred_element_type=F32)

        @pl.when(i == ni - 1)
        def _():
            for cp in stores:
                cp.start()
            for cp in stores:
                cp.wait()

    rows = pl.BlockSpec((tm, D), lambda i: (i, 0))
    rep = lambda shape: pl.BlockSpec(shape, lambda i: (0,) * len(shape))
    return pl.pallas_call(
        body, name="mid", grid=(ni,),
        in_specs=[rows, rows,
                  pl.BlockSpec((1, tm, D), lambda i: (6, i, 0)), pl.BlockSpec((1, tm, D), lambda i: (7, i, 0)),
                  rep((1, 2 * D)), rows, rows, rep((1, D)), ANY, ANY, ANY],
        out_specs=[rows, rows, rows, pl.BlockSpec((2, tm, D), lambda i: (0, i, 0)),
                   rep((8, BD)), rep((1, D)), rep((1, 2 * D)), ANY, ANY, ANY],
        out_shape=[SDS((n, D), F32), SDS((n, D), F32), SDS((n, D), F32), SDS((2, n, D), BF16),
                   SDS((8, BD), F32), SDS((1, D), F32), SDS((1, 2 * D), F32),
                   SDS((D, D), F32), SDS((D, D), F32), SDS((D, D), F32)],
        scratch_shapes=[pltpu.VMEM((D, D), BF16)] * 3 + [pltpu.VMEM((D, D), F32)] * 3 + [pltpu.SemaphoreType.DMA((3,))],
        compiler_params=_params(60),
    )(ya, yb, z, z, b_merge, x2, tgt, fin_g, pa, pb, wo)


def _dz_specs(tm, ni, row_major):
    if row_major:
        ia = lambda i, j: (jnp.minimum(j, 1), i, 0)
        ib = lambda i, j: (jnp.clip(j - 2, 0, 3), i, 0)
        im = lambda i, j: (jnp.clip(j - 6, 0, 1), i, 0)
    else:
        last = ni - 1
        ia = lambda j, i: (jnp.minimum(j, 1), jnp.where(j < 2, i, last), 0)
        ib = lambda j, i: (jnp.clip(j - 2, 0, 3), jnp.where(j < 2, 0, jnp.where(j < 6, i, last)), 0)
        im = lambda j, i: (jnp.clip(j - 6, 0, 1), jnp.where(j < 6, 0, i), 0)
    return [pl.BlockSpec((1, tm, D), f) for f in (ia, ib, im)]


def _inproj_bwd_x(dza, dzb, dzm, w_all, x2, dx2, norm_g):
    n = x2.shape[0]
    tm = 512
    ni = n // tm

    def body(dza_ref, dzb_ref, dzm_ref, w_ref, x_ref, dx2_ref, g_ref, gx_ref, gg_ref, acc):
        i, j = pl.program_id(0), pl.program_id(1)

        @pl.when((i == 0) & (j == 0))
        def _():
            gg_ref[...] = jnp.zeros((1, D), F32)

        @pl.when(j == 0)
        def _():
            acc[...] = jnp.zeros((tm, D), F32)

        def add(ref):
            acc[...] += lax.dot_general(ref[0], w_ref[0], NT_DIMS, preferred_element_type=F32)

        pl.when(j < 2)(lambda: add(dza_ref))
        pl.when((j >= 2) & (j < 6))(lambda: add(dzb_ref))
        pl.when(j >= 6)(lambda: add(dzm_ref))

        @pl.when(j == NB - 1)
        def _():
            x = x_ref[...]
            r = lax.rsqrt(jnp.mean(x * x, axis=-1, keepdims=True) + EPS)
            xn = x * r
            dh = acc[...]
            gg_ref[...] += jnp.sum(dh * xn, axis=0, keepdims=True)
            dxn = dh * g_ref[...]
            gx_ref[...] = dx2_ref[...] + r * (dxn - xn * jnp.mean(dxn * xn, axis=-1, keepdims=True))

    rows = pl.BlockSpec((tm, D), lambda i, j: (i, 0))
    return pl.pallas_call(
        body, name="inproj_bwd_x", grid=(ni, NB),
        in_specs=_dz_specs(tm, ni, True) + [pl.BlockSpec((1, D, D), lambda i, j: (j, 0, 0)), rows, rows,
                                             pl.BlockSpec((1, D), lambda i, j: (0, 0))],
        out_specs=[rows, pl.BlockSpec((1, D), lambda i, j: (0, 0))],
        out_shape=[SDS((n, D), F32), SDS((1, D), F32)],
        scratch_shapes=[pltpu.VMEM((tm, D), F32)],
        compiler_params=_params(48),
    )(dza, dzb, dzm, w_all, x2, dx2, norm_g)


def _inproj_bwd_w(dza, dzb, dzm, h_all):
    n = h_all.shape[0]
    tm = 512
    ni = n // tm

    def body(dza_ref, dzb_ref, dzm_ref, h_ref, gw_ref):
        j, i = pl.program_id(0), pl.program_id(1)

        @pl.when(i == 0)
        def _():
            gw_ref[...] = jnp.zeros((1, D, D), F32)

        def add(ref):
            gw_ref[0] += lax.dot_general(h_ref[...], ref[0], TN_DIMS, preferred_element_type=F32)

        pl.when(j < 2)(lambda: add(dza_ref))
        pl.when((j >= 2) & (j < 6))(lambda: add(dzb_ref))
        pl.when(j >= 6)(lambda: add(dzm_ref))

    return pl.pallas_call(
        body, name="inproj_bwd_w", grid=(NB, ni),
        in_specs=_dz_specs(tm, ni, False) + [pl.BlockSpec((tm, D), lambda j, i: (i, 0))],
        out_specs=pl.BlockSpec((1, D, D), lambda j, i: (j, 0, 0)),
        out_shape=SDS((NB, D, D), F32),
        compiler_params=_params(48),
    )(dza, dzb, dzm, h_all)


def _adamw(w, g, m, v):
    rows, cols = w.shape
    tr = _row_tile(rows)

    def body(w_ref, g_ref, m_ref, v_ref, d_ref, nm_ref, nv_ref):
        gv = g_ref[...]
        nm = ADAM_B1 * m_ref[...] + (1.0 - ADAM_B1) * gv
        nv = ADAM_B2 * v_ref[...] + (1.0 - ADAM_B2) * (gv * gv)
        m_hat = nm / (1.0 - ADAM_B1 ** ADAM_STEP)
        v_hat = nv / (1.0 - ADAM_B2 ** ADAM_STEP)
        d_ref[...] = -ADAM_LR * (m_hat / (jnp.sqrt(v_hat) + ADAM_EPS) + ADAM_WD * w_ref[...])
        nm_ref[...] = nm
        nv_ref[...] = nv

    spec = pl.BlockSpec((tr, cols), lambda i: (i, 0))
    return pl.pallas_call(
        body, name="adamw", grid=(rows // tr,), in_specs=[spec] * 4, out_specs=[spec] * 3,
        out_shape=[SDS((rows, cols), F32)] * 3, compiler_params=_params(32),
    )(w, g, m, v)


def _place():
    return lax.axis_index("x"), lax.axis_index("y"), lax.axis_index("c")


def _allgather(blocks, dtypes, name):
    na = len(blocks)

    def body(*refs):
        ins, outs, stages = refs[:na], refs[na:2 * na], refs[2 * na:3 * na]
        send_sems, recv_sems, local_sems = refs[3 * na:]
        x, y, c = _place()
        me, sibling = (x, y, c), (x, y, 1 - c)
        chips = [(1 - x, y), (x, 1 - y), (1 - x, 1 - y)]
        blk = lambda p: 4 * p[0] + 2 * p[1] + p[2]

        def copy(a, k, block, to, src=None):
            return pltpu.make_async_remote_copy(
                src_ref=outs[a].at[blk(block)] if src is None else src, dst_ref=outs[a].at[blk(block)],
                send_sem=send_sems.at[7 * a + k], recv_sem=recv_sems.at[7 * a + k],
                device_id=to, device_id_type=MESH)

        mine, first, passed = [], [], []
        for a in range(na):
            stages[a][...] = ins[a][...].astype(dtypes[a])
            mine.append(pltpu.make_async_copy(stages[a], outs[a].at[blk(me)], local_sems.at[a]))
            mine[-1].start()
            first.append(copy(a, 0, me, sibling, src=stages[a]))
            first += [copy(a, 1 + j, me, (*chip, c), src=stages[a]) for j, chip in enumerate(chips)]
        for cp in first:
            cp.start()
        for j, chip in enumerate(chips):
            for a in range(na):
                copy(a, 1 + j, (*chip, c), me).wait_recv()
                passed.append(copy(a, 4 + j, (*chip, c), sibling))
                passed[-1].start()
        for a in range(na):
            copy(a, 0, sibling, me).wait_recv()
            for j, chip in enumerate(chips):
                copy(a, 4 + j, (*chip, 1 - c), me).wait_recv()
        for cp in first + passed:
            cp.wait_send()
        for cp in mine:
            cp.wait()

    return pl.pallas_call(
        body, name=name,
        in_specs=[pl.BlockSpec(memory_space=pltpu.VMEM)] * na, out_specs=[ANY] * na,
        out_shape=[SDS((NB,) + b.shape, dt) for b, dt in zip(blocks, dtypes)],
        scratch_shapes=[pltpu.VMEM(b.shape, dt) for b, dt in zip(blocks, dtypes)]
        + [pltpu.SemaphoreType.DMA((7 * na,)), pltpu.SemaphoreType.DMA((7 * na,)), pltpu.SemaphoreType.DMA((na,))],
        compiler_params=_params(40),
    )(*blocks)


def _rs_sibling(gs):
    na = len(gs)

    def body(*refs):
        ins, outs = refs[:na], refs[na:2 * na]
        send_sems, recv_sems = refs[2 * na:]
        x, y, c = _place()
        copies = []
        for a in range(na):
            for q in range(4):
                copies.append(pltpu.make_async_remote_copy(
                    src_ref=ins[a].at[2 * q + (1 - c)], dst_ref=outs[a].at[q],
                    send_sem=send_sems.at[4 * a + q], recv_sem=recv_sems.at[4 * a + q],
                    device_id=(x, y, 1 - c), device_id_type=MESH))
        for cp in copies:
            cp.start()
        for cp in copies:
            cp.wait_recv()
        for cp in copies:
            cp.wait_send()

    return pl.pallas_call(
        body, name="rs_sibling", in_specs=[ANY] * na, out_specs=[ANY] * na,
        out_shape=[SDS((4,) + g.shape[1:], F32) for g in gs],
        scratch_shapes=[pltpu.SemaphoreType.DMA((4 * na,)), pltpu.SemaphoreType.DMA((4 * na,))],
    )(*gs)


def _rs_chips(ps):
    na = len(ps)

    def body(*refs):
        ins, outs = refs[:na], refs[na:2 * na]
        send_sems, recv_sems = refs[2 * na:]
        x, y, c = _place()
        chips = [(1 - x, y), (x, 1 - y), (1 - x, 1 - y)]
        copies = []
        for a in range(na):
            for slot, (px, py) in enumerate(chips):
                copies.append(pltpu.make_async_remote_copy(
                    src_ref=ins[a].at[2 * px + py], dst_ref=outs[a].at[slot],
                    send_sem=send_sems.at[3 * a + slot], recv_sem=recv_sems.at[3 * a + slot],
                    device_id=(px, py, c), device_id_type=MESH))
        for cp in copies:
            cp.start()
        for cp in copies:
            cp.wait_recv()
        for cp in copies:
            cp.wait_send()

    return pl.pallas_call(
        body, name="rs_chips", in_specs=[ANY] * na, out_specs=[ANY] * na,
        out_shape=[SDS((3,) + p.shape[1:], F32) for p in ps],
        scratch_shapes=[pltpu.SemaphoreType.DMA((3 * na,)), pltpu.SemaphoreType.DMA((3 * na,))],
    )(*ps)


def _add_sibling(place, g, a_in):
    _, r, cols = g.shape
    tr = _row_tile(r)

    def body(place_ref, g_ref, a_ref, p_ref):
        p_ref[...] = g_ref[...] + a_ref[...]

    return pl.pallas_call(
        body, name="add_sibling",
        grid_spec=pltpu.PrefetchScalarGridSpec(
            num_scalar_prefetch=1, grid=(4, r // tr),
            in_specs=[pl.BlockSpec((1, tr, cols), lambda q, i, pr: (2 * q + pr[2], i, 0)),
                      pl.BlockSpec((1, tr, cols), lambda q, i, pr: (q, i, 0))],
            out_specs=pl.BlockSpec((1, tr, cols), lambda q, i, pr: (q, i, 0))),
        out_shape=SDS((4, r, cols), F32), compiler_params=_params(32),
    )(place, g, a_in)


def _add_chips(place, p, b_in):
    _, r, cols = p.shape
    tr = _row_tile(r)

    def body(place_ref, p_ref, b0_ref, b1_ref, b2_ref, o_ref):
        o_ref[...] = ((p_ref[0] + b0_ref[0]) + b1_ref[0]) + b2_ref[0]

    slot = lambda k: pl.BlockSpec((1, tr, cols), lambda i, pr: (k, i, 0))
    return pl.pallas_call(
        body, name="add_chips",
        grid_spec=pltpu.PrefetchScalarGridSpec(
            num_scalar_prefetch=1, grid=(r // tr,),
            in_specs=[pl.BlockSpec((1, tr, cols), lambda i, pr: (2 * pr[0] + pr[1], i, 0)), slot(0), slot(1), slot(2)],
            out_specs=pl.BlockSpec((tr, cols), lambda i, pr: (i, 0))),
        out_shape=SDS((r, cols), F32), compiler_params=_params(32),
    )(place, p, b_in, b_in, b_in)


def _reduce_scatter(place, gs):
    a_in = _rs_sibling(gs)
    ps = [_add_sibling(place, g, a) for g, a in zip(gs, a_in)]
    b_in = _rs_chips(ps)
    return [_add_chips(place, p, b) for p, b in zip(ps, b_in)]


REP_NAMES = ("rg_wx", "rg_wa", "b_merge", "conv_b", "rg_bx", "rg_ba", "rg_lambda", "hg_lb_logits", "hg_norm_g",
             "norm_g", "final_norm_g")
VEC_NAMES = REP_NAMES[2:]


def _pack_rows(arrays, width, row_multiple=8):
    flat = jnp.concatenate([a.reshape(-1) for a in arrays])
    rows = -(-flat.shape[0] // width)
    rows = -(-rows // row_multiple) * row_multiple
    return jnp.pad(flat, (0, rows * width - flat.shape[0])).reshape(rows, width)


def _unpack(flat, like):
    out, off = [], 0
    for a in like:
        out.append(flat[off:off + a.size].reshape(a.shape))
        off += a.size
    return out


def kernel(x, w_in, b_merge, conv_w, conv_b, rg_wx, rg_bx, rg_wa, rg_ba, rg_lambda, hg_lb_logits, hg_norm_g, proj_a, proj_b, w_out, norm_g, final_norm_g, loss_target, m_w_in, m_b_merge, m_conv_w, m_conv_b, m_rg_wx, m_rg_bx, m_rg_wa, m_rg_ba, m_rg_lambda, m_hg_lb_logits, m_hg_norm_g, m_proj_a, m_proj_b, m_w_out, m_norm_g, m_final_norm_g, v_w_in, v_b_merge, v_conv_w, v_conv_b, v_rg_wx, v_rg_bx, v_rg_wa, v_rg_ba, v_rg_lambda, v_hg_lb_logits, v_hg_norm_g, v_proj_a, v_proj_b, v_w_out, v_norm_g, v_final_norm_g):
    weights = dict(w_in=w_in, b_merge=b_merge, conv_w=conv_w, conv_b=conv_b, rg_wx=rg_wx, rg_bx=rg_bx, rg_wa=rg_wa,
                   rg_ba=rg_ba, rg_lambda=rg_lambda, hg_lb_logits=hg_lb_logits, hg_norm_g=hg_norm_g, proj_a=proj_a,
                   proj_b=proj_b, w_out=w_out, norm_g=norm_g, final_norm_g=final_norm_g)
    mom1 = dict(w_in=m_w_in, b_merge=m_b_merge, conv_w=m_conv_w, conv_b=m_conv_b, rg_wx=m_rg_wx, rg_bx=m_rg_bx,
                rg_wa=m_rg_wa, rg_ba=m_rg_ba, rg_lambda=m_rg_lambda, hg_lb_logits=m_hg_lb_logits,
                hg_norm_g=m_hg_norm_g, proj_a=m_proj_a, proj_b=m_proj_b, w_out=m_w_out, norm_g=m_norm_g,
                final_norm_g=m_final_norm_g)
    mom2 = dict(w_in=v_w_in, b_merge=v_b_merge, conv_w=v_conv_w, conv_b=v_conv_b, rg_wx=v_rg_wx, rg_bx=v_rg_bx,
                rg_wa=v_rg_wa, rg_ba=v_rg_ba, rg_lambda=v_rg_lambda, hg_lb_logits=v_hg_lb_logits,
                hg_norm_g=v_hg_norm_g, proj_a=v_proj_a, proj_b=v_proj_b, w_out=v_w_out, norm_g=v_norm_g,
                final_norm_g=v_final_norm_g)
    order = list(weights)
    nb, s_len, _ = x.shape
    n = nb * s_len
    px, py, pc = _place()
    place = jnp.stack([px, py, pc]).astype(jnp.int32)

    cw_blk = jnp.pad(conv_w[0], ((0, 4), (0, 0)))
    w_all, pa_all, pb_all, wo_all, cw_all = _allgather(
        [w_in[0], proj_a[0], proj_b[0], w_out[0], cw_blk], [BF16, BF16, BF16, BF16, F32], "gather_weights")
    pa_full, pb_full, wo_full = (a.reshape(D, D) for a in (pa_all, pb_all, wo_all))
    cw8 = cw_all.transpose(1, 0, 2).reshape(8, D)
    wx_b, wa_b = rg_wx[0].astype(BF16), rg_wa[0].astype(BF16)
    cb, bx, ba = conv_b, rg_bx.reshape(1, D), rg_ba.reshape(1, D)
    fin_g = final_norm_g.reshape(1, D)

    x2 = x.reshape(n, D)
    z, h_all = _inproj_fwd(x2, norm_g, w_all)
    hlru, ya = _lru_fwd(z, cw8, cb, wx_b, wa_b, bx, ba, rg_lambda, nb, s_len)
    o_all, yb, st_all = _hgrn_fwd(z, hg_lb_logits, hg_norm_g, nb, s_len)

    (dx2, dya, dyb, dzm, loss_acc, g_fin, g_bm, g_pa, g_pb, g_wo) = _mid(
        ya, yb, z, b_merge, x2, loss_target.reshape(n, D), fin_g, pa_full, pb_full, wo_full)
    dzb, g_lg, g_hg = _hgrn_bwd(z, o_all, st_all, dyb, hg_lb_logits, hg_norm_g, nb, s_len)
    dza, g_cw8, g_cb, g_wx, g_wa, g_bx, g_ba, g_lam = _lru_bwd(
        z, hlru, dya, cw8, cb, wx_b, wa_b, bx, ba, rg_lambda, nb, s_len)
    grad_x, g_ng = _inproj_bwd_x(dza, dzb, dzm, w_all, x2, dx2, norm_g)
    g_w = _inproj_bwd_w(dza, dzb, dzm, h_all)

    part = dict(b_merge=g_bm, conv_b=g_cb, rg_bx=g_bx, rg_ba=g_ba, rg_lambda=g_lam, hg_lb_logits=g_lg,
                hg_norm_g=g_hg, norm_g=g_ng, final_norm_g=g_fin)
    vec = _pack_rows([part[k] for k in VEC_NAMES], BD)
    vec = jnp.pad(vec, ((0, 16 * NB - vec.shape[0]), (0, 0))).reshape(NB, 2, D)
    small = jnp.concatenate([g_wx.reshape(NB, 16, D), g_wa.reshape(NB, 16, D),
                             g_cw8.reshape(8, NB, BD).transpose(1, 0, 2).reshape(NB, 1, D), vec,
                             jnp.zeros((NB, 5, D), F32)], axis=1)
    mids = jnp.concatenate([g.reshape(NB, BD, D) for g in (g_pa, g_pb, g_wo)], axis=1)
    r_w, r_mid, r_small = _reduce_scatter(place, [g_w, mids, small])
    (small_all,) = _allgather([r_small], [F32], "gather_small_grads")

    grads = dict(w_in=r_w.reshape(1, D, D),
                 proj_a=r_mid[0:BD].reshape(1, BD, D), proj_b=r_mid[BD:2 * BD].reshape(1, BD, D),
                 w_out=r_mid[2 * BD:3 * BD].reshape(1, BD, D),
                 conv_w=r_small[32].reshape(8, BD)[0:4].reshape(1, 4, BD),
                 rg_wx=small_all[:, 0:16].reshape(1, NB, BD, BD), rg_wa=small_all[:, 16:32].reshape(1, NB, BD, BD))
    vec_all = small_all[:, 33:35].reshape(-1)
    for k, gk in zip(VEC_NAMES, _unpack(vec_all, [weights[k] for k in VEC_NAMES])):
        grads[k] = gk

    delta, new_m, new_v = {}, {}, {}
    for k in ("w_in", "proj_a", "proj_b", "w_out"):
        shp = weights[k].shape
        two = lambda a: a.reshape(shp[1], shp[2])
        d_k, m_k, v_k = _adamw(two(weights[k]), two(grads[k]), two(mom1[k]), two(mom2[k]))
        delta[k], new_m[k], new_v[k] = d_k.reshape(shp), m_k.reshape(shp), v_k.reshape(shp)
    rep = list(REP_NAMES) + ["conv_w"]
    packs = [_pack_rows([t[k] for k in rep], BD, 256) for t in (weights, grads, mom1, mom2)]
    outs = _adamw(*packs)
    for tgt, flat in zip((delta, new_m, new_v), outs):
        for k, a in zip(rep, _unpack(flat.reshape(-1), [weights[k] for k in rep])):
            tgt[k] = a

    loss = lax.psum(loss_acc[0, 0], ("x", "y", "c"))
    return (loss, grad_x.reshape(x.shape), *[grads[k] for k in order], *[delta[k] for k in order],
            *[new_m[k] for k in order], *[new_v[k] for k in order])
```

```python
import jax
import jax.numpy as jnp
from jax import lax
from jax.experimental import pallas as pl
from jax.experimental.pallas import tpu as pltpu

F32 = jnp.float32
BF16 = jnp.bfloat16
SDS = jax.ShapeDtypeStruct
MESH = pl.DeviceIdType.MESH
ANY = pl.BlockSpec(memory_space=pl.ANY)

D = 1024
NB = 8
BD = D // NB
CHUNK = 64
EPS = 1e-6
LRU_C = 8.0
HG_SCALE = BD ** -0.5
ADAM_LR, ADAM_B1, ADAM_B2, ADAM_EPS, ADAM_WD, ADAM_STEP = 0.001, 0.9, 0.999, 1e-08, 0.01, 10

NT_DIMS = (((1,), (1,)), ((), ()))
TN_DIMS = (((0,), (0,)), ((), ()))


def _params(vmem_mib):
    return pltpu.CompilerParams(vmem_limit_bytes=vmem_mib << 20)


def _row_tile(rows, most=256):
    assert rows % 8 == 0
    return max(t for t in range(8, min(rows, most) + 1, 8) if rows % t == 0)


def _sigmoid(v):
    return jax.nn.sigmoid(v)


def _softplus_neg(lam):
    t = -lam
    e = jnp.exp(-jnp.abs(t))
    w = 1.0 + e
    d = w - 1.0
    l1p = jnp.where(d == 0.0, e, jnp.log(w) * (e / jnp.where(d == 0.0, 1.0, d)))
    return jnp.maximum(t, 0.0) + l1p


def _inproj_fwd(x2, norm_g, w_all):
    n = x2.shape[0]
    tm = min(n, 1024)

    def body(x_ref, g_ref, w_ref, z_ref, h_ref):
        @pl.when(pl.program_id(1) == 0)
        def _():
            x = x_ref[...]
            r = lax.rsqrt(jnp.mean(x * x, axis=-1, keepdims=True) + EPS)
            h_ref[...] = ((x * r) * g_ref[...]).astype(BF16)

        z_ref[0] = jnp.dot(h_ref[...], w_ref[0], preferred_element_type=F32)

    return pl.pallas_call(
        body, name="inproj_fwd", grid=(n // tm, NB),
        in_specs=[pl.BlockSpec((tm, D), lambda i, j: (i, 0)),
                  pl.BlockSpec((1, D), lambda i, j: (0, 0)),
                  pl.BlockSpec((1, D, D), lambda i, j: (j, 0, 0))],
        out_specs=[pl.BlockSpec((1, tm, D), lambda i, j: (j, i, 0)),
                   pl.BlockSpec((tm, D), lambda i, j: (i, 0))],
        out_shape=[SDS((NB, n, D), F32), SDS((n, D), BF16)],
        compiler_params=_params(48),
    )(x2, norm_g, w_all)


LRU_T = 256


def _conv(ext, cw, cb):
    t = LRU_T
    acc = ext[5:5 + t, :] * cw[0:1, :] + ext[6:6 + t, :] * cw[1:2, :]
    acc = acc + ext[7:7 + t, :] * cw[2:3, :]
    acc = acc + ext[8:8 + t, :] * cw[3:4, :]
    return cb + acc


def _lru_gates(xa, wx_ref, wa_ref, bx, ba, lam):
    xab = xa.astype(BF16)
    pis, prs = [], []
    for h in range(NB):
        xs = xab[:, h * BD:(h + 1) * BD]
        pis.append(jnp.dot(xs, wx_ref[h], preferred_element_type=F32))
        prs.append(jnp.dot(xs, wa_ref[h], preferred_element_type=F32))
    gi = _sigmoid(jnp.concatenate(pis, axis=1) + bx)
    gr = _sigmoid(jnp.concatenate(prs, axis=1) + ba)
    sp = _softplus_neg(lam)
    log_a = (-LRU_C * gr) * sp
    a = jnp.exp(log_a)
    mult = jnp.sqrt(-jnp.tanh(log_a) * (a * a + 1.0))
    return xab, gi, gr, sp, a, mult


def _lru_fwd(z, cw8, cb, wx, wa, bx, ba, lam, nb, s_len):
    n = nb * s_len
    t = LRU_T
    ns = s_len // t

    def body(xp_ref, ga_ref, cw_ref, cb_ref, wx_ref, wa_ref, bx_ref, ba_ref, lam_ref,
             h_ref, ya_ref, ext, a_s, u_s, carry):
        @pl.when(pl.program_id(1) == 0)
        def _():
            ext[0:8, :] = jnp.zeros((8, D), F32)
            carry[...] = jnp.zeros((8, D), F32)

        ext[8:8 + t, :] = xp_ref[0]
        xa = _conv(ext, cw_ref[...], cb_ref[...])
        ext[0:8, :] = ext[t:t + 8, :]
        _, gi, _, _, a, mult = _lru_gates(xa, wx_ref, wa_ref, bx_ref[...], ba_ref[...], lam_ref[...])
        u = (mult * gi) * xa
        row = lax.broadcasted_iota(jnp.int32, (t, D), 0) & 7
        for sh in (1, 2, 4):
            a_sh = pltpu.roll(a, sh, 0)
            u_sh = pltpu.roll(u, sh, 0)
            m = row >= sh
            u = jnp.where(m, a * u_sh + u, u)
            a = jnp.where(m, a * a_sh, a)
        a_s[...] = a
        u_s[...] = u

        def step(g, c):
            r = pl.multiple_of(g * 8, 8)
            hg = u_s[pl.ds(r, 8), :] + a_s[pl.ds(r, 8), :] * c
            h_ref[pl.ds(r, 8), :] = hg
            return hg[7:8, :]

        c_out = lax.fori_loop(0, t // 8, step, carry[0:1, :], unroll=4)
        carry[0:1, :] = c_out
        ga = ga_ref[0]
        ya_ref[...] = (h_ref[...] * (ga * _sigmoid(ga))).astype(BF16)

    row_map = lambda b, s: (b * ns + s, 0)
    rep2 = lambda b, s: (0, 0)
    rep3 = lambda b, s: (0, 0, 0)
    return pl.pallas_call(
        body, name="lru_fwd", grid=(nb, ns),
        in_specs=[pl.BlockSpec((1, t, D), lambda b, s: (0, b * ns + s, 0)),
                  pl.BlockSpec((1, t, D), lambda b, s: (1, b * ns + s, 0)),
                  pl.BlockSpec((8, D), rep2), pl.BlockSpec((1, D), rep2),
                  pl.BlockSpec((NB, BD, BD), rep3), pl.BlockSpec((NB, BD, BD), rep3),
                  pl.BlockSpec((1, D), rep2), pl.BlockSpec((1, D), rep2), pl.BlockSpec((1, D), rep2)],
        out_specs=[pl.BlockSpec((t, D), row_map), pl.BlockSpec((t, D), row_map)],
        out_shape=[SDS((n, D), F32), SDS((n, D), BF16)],
        scratch_shapes=[pltpu.VMEM((t + 8, D), F32), pltpu.VMEM((t, D), F32), pltpu.VMEM((t, D), F32),
                        pltpu.VMEM((8, D), F32)],
        compiler_params=_params(48),
    )(z, z, cw8, cb, wx, wa, bx, ba, lam)


def _lru_bwd(z, h_all, dya, cw8, cb, wx, wa, bx, ba, lam, nb, s_len):
    n = nb * s_len
    t = LRU_T
    ns = s_len // t
    t8 = t // 8

    def body(xp_ref, xph_ref, ga_ref, h_ref, hh_ref, dya_ref, cw_ref, cb_ref, wx_ref, wa_ref, bx_ref, ba_ref,
             lam_ref, dz_ref, gcw_ref, gcb_ref, gwx_ref, gwa_ref, gbx_ref, gba_ref, glam_ref,
             ext, hext, dext, a_s, u_s, dh_s, carry):
        b, s = pl.program_id(0), pl.program_id(1)
        first_tile = s == ns - 1

        @pl.when((b == 0) & (s == 0))
        def _():
            for ref in (gcw_ref, gcb_ref, gwx_ref, gwa_ref, gbx_ref, gba_ref, glam_ref):
                ref[...] = jnp.zeros(ref.shape, F32)

        @pl.when(s == 0)
        def _():
            dext[t:t + 8, :] = jnp.zeros((8, D), F32)
            carry[...] = jnp.zeros((8, D), F32)

        keep = jnp.where(first_tile, 0.0, 1.0)
        ext[0:8, :] = xph_ref[0] * keep
        ext[8:8 + t, :] = xp_ref[0]
        hext[0:8, :] = hh_ref[...] * keep
        hext[8:8 + t, :] = h_ref[...]
        cw = cw_ref[...]
        lam = lam_ref[...]
        xa = _conv(ext, cw, cb_ref[...])
        xab, gi, gr, sp, a, mult = _lru_gates(xa, wx_ref, wa_ref, bx_ref[...], ba_ref[...], lam)
        h_prev = hext[7:7 + t, :]
        ga = ga_ref[0]
        sg = _sigmoid(ga)
        dya_v = dya_ref[...]
        d_ga = dya_v * h_ref[...] * (sg * (1.0 + ga * (1.0 - sg)))
        g_in = dya_v * (ga * sg)

        rows = lax.broadcasted_iota(jnp.int32, (t, D), 0)
        row = rows & 7
        an = jnp.where(rows == t - 1, 1.0, pltpu.roll(a, t - 1, 0))
        u = g_in
        for sh in (1, 2, 4):
            a_sh = pltpu.roll(an, t - sh, 0)
            u_sh = pltpu.roll(u, t - sh, 0)
            m = row < 8 - sh
            u = jnp.where(m, u + an * u_sh, u)
            an = jnp.where(m, an * a_sh, an)
        a_s[...] = an
        u_s[...] = u

        def step(i, c):
            r = pl.multiple_of((t8 - 1 - i) * 8, 8)
            dg = u_s[pl.ds(r, 8), :] + a_s[pl.ds(r, 8), :] * c
            dh_s[pl.ds(r, 8), :] = dg
            return dg[0:1, :]

        lax.fori_loop(0, t8, step, carry[0:1, :], unroll=4)
        dh = dh_s[...]
        carry[0:1, :] = a[0:1, :] * dh[0:1, :]

        d_a = dh * h_prev
        dux = dh * xa
        d_mult = dux * gi
        d_gi = dux * mult
        d_xa = dh * (mult * gi)
        d_loga = d_a * a - d_mult * ((a * a) / mult)
        d_gr = d_loga * (-LRU_C * sp)
        d_sp = jnp.sum(d_loga * (-LRU_C * gr), axis=0, keepdims=True)
        glam_ref[...] += d_sp * (-_sigmoid(-lam))
        d_pi = d_gi * gi * (1.0 - gi)
        d_pr = d_gr * gr * (1.0 - gr)
        gbx_ref[...] += jnp.sum(d_pi, axis=0, keepdims=True)
        gba_ref[...] += jnp.sum(d_pr, axis=0, keepdims=True)
        dpib = d_pi.astype(BF16)
        dprb = d_pr.astype(BF16)
        back = []
        for h in range(NB):
            cs = slice(h * BD, (h + 1) * BD)
            gwx_ref[h] += lax.dot_general(xab[:, cs], dpib[:, cs], TN_DIMS, preferred_element_type=F32)
            gwa_ref[h] += lax.dot_general(xab[:, cs], dprb[:, cs], TN_DIMS, preferred_element_type=F32)
            back.append(lax.dot_general(dpib[:, cs], wx_ref[h], NT_DIMS, preferred_element_type=F32)
                        + lax.dot_general(dprb[:, cs], wa_ref[h], NT_DIMS, preferred_element_type=F32))
        d_xa = d_xa + jnp.concatenate(back, axis=1)

        dext[0:t, :] = d_xa
        d_xp = dext[3:3 + t, :] * cw[0:1, :] + dext[2:2 + t, :] * cw[1:2, :]
        d_xp = d_xp + dext[1:1 + t, :] * cw[2:3, :]
        d_xp = d_xp + d_xa * cw[3:4, :]
        dext[t:t + 8, :] = d_xa[0:8, :]
        gcb_ref[...] += jnp.sum(d_xa, axis=0, keepdims=True)
        for k in range(4):
            gcw_ref[k:k + 1, :] += jnp.sum(d_xa * ext[5 + k:5 + k + t, :], axis=0, keepdims=True)
        dz_ref[0] = d_xp.astype(BF16)
        dz_ref[1] = d_ga.astype(BF16)

    rb = lambda b, s: b * ns + (ns - 1 - s)
    halo = lambda b, s: jnp.maximum(rb(b, s) * t8 - 1, 0)
    rep2 = lambda b, s: (0, 0)
    rep3 = lambda b, s: (0, 0, 0)
    return pl.pallas_call(
        body, name="lru_bwd", grid=(nb, ns),
        in_specs=[pl.BlockSpec((1, t, D), lambda b, s: (0, rb(b, s), 0)),
                  pl.BlockSpec((1, 8, D), lambda b, s: (0, halo(b, s), 0)),
                  pl.BlockSpec((1, t, D), lambda b, s: (1, rb(b, s), 0)),
                  pl.BlockSpec((t, D), lambda b, s: (rb(b, s), 0)),
                  pl.BlockSpec((8, D), lambda b, s: (halo(b, s), 0)),
                  pl.BlockSpec((t, D), lambda b, s: (rb(b, s), 0)),
                  pl.BlockSpec((8, D), rep2), pl.BlockSpec((1, D), rep2),
                  pl.BlockSpec((NB, BD, BD), rep3), pl.BlockSpec((NB, BD, BD), rep3),
                  pl.BlockSpec((1, D), rep2), pl.BlockSpec((1, D), rep2), pl.BlockSpec((1, D), rep2)],
        out_specs=[pl.BlockSpec((2, t, D), lambda b, s: (0, rb(b, s), 0)),
                   pl.BlockSpec((8, D), rep2), pl.BlockSpec((1, D), rep2),
                   pl.BlockSpec((NB, BD, BD), rep3), pl.BlockSpec((NB, BD, BD), rep3),
                   pl.BlockSpec((1, D), rep2), pl.BlockSpec((1, D), rep2), pl.BlockSpec((1, D), rep2)],
        out_shape=[SDS((2, n, D), BF16), SDS((8, D), F32), SDS((1, D), F32),
                   SDS((NB, BD, BD), F32), SDS((NB, BD, BD), F32),
                   SDS((1, D), F32), SDS((1, D), F32), SDS((1, D), F32)],
        scratch_shapes=[pltpu.VMEM((t + 8, D), F32), pltpu.VMEM((t + 8, D), F32), pltpu.VMEM((t + 8, D), F32),
                        pltpu.VMEM((t, D), F32), pltpu.VMEM((t, D), F32), pltpu.VMEM((t, D), F32),
                        pltpu.VMEM((8, D), F32)],
        compiler_params=_params(56),
    )(z, z, z, h_all, h_all, dya, cw8, cb, wx, wa, bx, ba, lam)


HG_T = 512
HG_NC = HG_T // CHUNK
BNT_DIMS = (((2,), (2,)), ((0,), (0,)))
BNN_DIMS = (((2,), (1,)), ((0,), (0,)))
BTN_DIMS = (((1,), (1,)), ((0,), (0,)))


def _lower_bound(lg):
    m = jnp.max(lg, axis=0, keepdims=True)
    e = jnp.exp(lg - m)
    return e[0:1, :] / jnp.sum(e, axis=0, keepdims=True)


def _tri(upper):
    r = lax.broadcasted_iota(jnp.int32, (HG_NC, CHUNK, CHUNK), 1)
    c = lax.broadcasted_iota(jnp.int32, (HG_NC, CHUNK, CHUNK), 2)
    return (c >= r) if upper else (r >= c)


def _bdot(a, b, dims, precision=None):
    return lax.dot_general(a, b, dims, precision=precision, preferred_element_type=F32)


def _chunks(a):
    return a.reshape(HG_NC, CHUNK, BD)


def _hg_tile(q, fp, lb):
    q, fp = _chunks(q), _chunks(fp)
    sig = _sigmoid(fp)
    f = lb + (1.0 - lb) * sig
    log_f = jnp.log(f)
    k = 1.0 - f
    b = _bdot(_tri(False).astype(F32), log_f, BNN_DIMS, lax.Precision.HIGHEST)
    b_mid = b[:, CHUNK // 2:CHUNK // 2 + 1, :]
    b_last = b[:, CHUNK - 1:CHUNK, :]
    sq = _sigmoid(q)
    qh = q * sq
    e_qi = jnp.exp(b - b_mid)
    e_ki = jnp.exp(b_mid - b)
    e_qs = jnp.exp(b)
    e_ks = jnp.exp(b_last - b)
    dc = jnp.exp(b_last)
    q_in = (qh * e_qi) * HG_SCALE
    k_in = k * e_ki
    q_st = (qh * e_qs) * HG_SCALE
    k_st = k * e_ks
    att = _bdot(q_in.astype(BF16), k_in.astype(BF16), BNT_DIMS)
    att = jnp.where(_tri(False), att, 0.0)
    return dict(q=q, sig=sig, f=f, k=k, sq=sq, e_qi=e_qi, e_ki=e_ki, e_qs=e_qs, e_ks=e_ks, dc=dc,
                q_in=q_in, k_in=k_in, q_st=q_st, k_st=k_st, att=att)


def _hgrn_fwd(z, lb_logits, hg_g, nb, s_len):
    n = nb * s_len
    t = HG_T
    ns = s_len // t
    nchunk = s_len // CHUNK

    def body(q_ref, f_ref, v_ref, gb_ref, lg_ref, g_ref, o_ref, yb_ref, st_ref, st):
        @pl.when(pl.program_id(2) == 0)
        def _():
            st[...] = jnp.zeros((BD, BD), F32)

        lb = _lower_bound(lg_ref[...])
        ck = _hg_tile(q_ref[0], f_ref[0], lb)
        vb = _chunks(v_ref[0]).astype(BF16)
        kv = _bdot(vb, ck["k_st"].astype(BF16), BTN_DIMS)
        states = [st[...]]
        for c in range(HG_NC):
            states.append(states[c] * ck["dc"][c] + kv[c])
        st[...] = states[HG_NC]
        s_in = jnp.stack(states[:HG_NC], axis=0)
        st_ref[...] = s_in
        o = _bdot(ck["att"].astype(BF16), vb, BNN_DIMS) + _bdot(ck["q_st"].astype(BF16), s_in.astype(BF16), BNT_DIMS)
        o_ref[...] = o.reshape(t, BD)
        r = lax.rsqrt(jnp.mean(o * o, axis=-1, keepdims=True) + EPS)
        gb = _chunks(gb_ref[0])
        yb_ref[...] = (((o * r) * g_ref[...]) * (gb * _sigmoid(gb))).astype(BF16).reshape(t, BD)

    seg = lambda j: pl.BlockSpec((1, t, BD), lambda h, b, s: (j, b * ns + s, h))
    tile = pl.BlockSpec((t, BD), lambda h, b, s: (b * ns + s, h))
    return pl.pallas_call(
        body, name="hgrn_fwd", grid=(NB, nb, ns),
        in_specs=[seg(2), seg(3), seg(4), seg(5),
                  pl.BlockSpec((2, BD), lambda h, b, s: (0, h)),
                  pl.BlockSpec((1, BD), lambda h, b, s: (0, 0))],
        out_specs=[tile, tile,
                   pl.BlockSpec((HG_NC, BD, BD), lambda h, b, s: ((b * NB + h) * ns + s, 0, 0))],
        out_shape=[SDS((n, D), F32), SDS((n, D), BF16), SDS((nb * NB * nchunk, BD, BD), F32)],
        scratch_shapes=[pltpu.VMEM((BD, BD), F32)],
        compiler_params=_params(40),
    )(z, z, z, z, lb_logits, hg_g)


def _hgrn_bwd(z, o_all, st_all, dyb, lb_logits, hg_g, nb, s_len):
    n = nb * s_len
    t = HG_T
    ns = s_len // t

    def body(q_ref, f_ref, v_ref, gb_ref, o_ref, st_ref, dyb_ref, lg_ref, g_ref,
             dz_ref, glg_ref, ghg_ref, dst, dlb):
        h, b, s = pl.program_id(0), pl.program_id(1), pl.program_id(2)

        @pl.when((h == 0) & (b == 0) & (s == 0))
        def _():
            ghg_ref[...] = jnp.zeros((1, BD), F32)

        @pl.when((b == 0) & (s == 0))
        def _():
            dlb[...] = jnp.zeros((8, BD), F32)

        @pl.when(s == 0)
        def _():
            dst[...] = jnp.zeros((BD, BD), F32)

        lb = _lower_bound(lg_ref[...])
        g = g_ref[...]
        ck = _hg_tile(q_ref[0], f_ref[0], lb)
        q = ck["q"]
        vb = _chunks(v_ref[0]).astype(BF16)
        gb = _chunks(gb_ref[0])
        o = _chunks(o_ref[...])
        dyb_v = _chunks(dyb_ref[...])
        s_in = st_ref[...]

        sgb = _sigmoid(gb)
        r = lax.rsqrt(jnp.mean(o * o, axis=-1, keepdims=True) + EPS)
        ohat = o * r
        d_on = dyb_v * (gb * sgb)
        d_gb = dyb_v * (ohat * g) * (sgb * (1.0 + gb * (1.0 - sgb)))
        ghg_ref[...] += jnp.sum(jnp.sum(d_on * ohat, axis=1), axis=0, keepdims=True)
        tt = d_on * g
        d_o = r * (tt - ohat * jnp.mean(tt * ohat, axis=-1, keepdims=True))
        dob = d_o.astype(BF16)

        attb = ck["att"].astype(BF16)
        q_inb, k_inb = ck["q_in"].astype(BF16), ck["k_in"].astype(BF16)
        q_stb, k_stb = ck["q_st"].astype(BF16), ck["k_st"].astype(BF16)
        d_att = jnp.where(_tri(False), _bdot(dob, vb, BNT_DIMS), 0.0).astype(BF16)
        d_q_in = _bdot(d_att, k_inb, BNN_DIMS)
        d_k_in = _bdot(d_att, q_inb, BTN_DIMS)
        d_q_st = _bdot(dob, s_in.astype(BF16), BNN_DIMS)
        qdo = _bdot(dob, q_stb, BTN_DIMS)
        d_states = [None] * HG_NC + [dst[...]]
        for c in reversed(range(HG_NC)):
            d_states[c] = d_states[c + 1] * ck["dc"][c] + qdo[c]
        dst[...] = d_states[0]
        ds_out = jnp.stack(d_states[1:], axis=0)
        dsb = ds_out.astype(BF16)
        d_v = _bdot(attb, dob, BTN_DIMS) + _bdot(k_stb, dsb, BNT_DIMS)
        d_k_st = _bdot(vb, dsb, BNN_DIMS)
        d_dc = jnp.sum(ds_out * s_in, axis=1, keepdims=True)

        p_qi = d_q_in * ck["q_in"]
        p_ki = d_k_in * ck["k_in"]
        p_qs = d_q_st * ck["q_st"]
        p_ks = d_k_st * ck["k_st"]
        d_qh = (d_q_in * ck["e_qi"] + d_q_st * ck["e_qs"]) * HG_SCALE
        d_k = d_k_in * ck["e_ki"] + d_k_st * ck["e_ks"]
        d_b = (p_qi - p_ki) + (p_qs - p_ks)
        d_b_mid = jnp.sum(p_ki - p_qi, axis=1, keepdims=True)
        d_b_last = jnp.sum(p_ks, axis=1, keepdims=True) + d_dc * ck["dc"]
        rowi = lax.broadcasted_iota(jnp.int32, (HG_NC, CHUNK, BD), 1)
        d_b = d_b + jnp.where(rowi == CHUNK // 2, d_b_mid, 0.0) + jnp.where(rowi == CHUNK - 1, d_b_last, 0.0)
        d_logf = _bdot(_tri(True).astype(F32), d_b, BNN_DIMS, lax.Precision.HIGHEST)
        d_f = d_logf / ck["f"] - d_k
        sig, sq = ck["sig"], ck["sq"]
        d_fp = d_f * (1.0 - lb) * (sig * (1.0 - sig))
        dlb[0:1, :] += jnp.sum(jnp.sum(d_f * (1.0 - sig), axis=1), axis=0, keepdims=True)
        d_q = d_qh * (sq * (1.0 + q * (1.0 - sq)))
        dz_ref[0] = d_q.astype(BF16).reshape(t, BD)
        dz_ref[1] = d_fp.astype(BF16).reshape(t, BD)
        dz_ref[2] = d_v.astype(BF16).reshape(t, BD)
        dz_ref[3] = d_gb.astype(BF16).reshape(t, BD)

        @pl.when((b == nb - 1) & (s == ns - 1))
        def _():
            dl = dlb[0:1, :] * (lb * (1.0 - lb))
            glg_ref[0:1, :] = dl
            glg_ref[1:2, :] = -dl

    rb = lambda b, s: b * ns + (ns - 1 - s)
    seg = lambda j: pl.BlockSpec((1, t, BD), lambda h, b, s: (j, rb(b, s), h))
    tile = pl.BlockSpec((t, BD), lambda h, b, s: (rb(b, s), h))
    return pl.pallas_call(
        body, name="hgrn_bwd", grid=(NB, nb, ns),
        in_specs=[seg(2), seg(3), seg(4), seg(5), tile,
                  pl.BlockSpec((HG_NC, BD, BD), lambda h, b, s: ((b * NB + h) * ns + (ns - 1 - s), 0, 0)),
                  tile,
                  pl.BlockSpec((2, BD), lambda h, b, s: (0, h)),
                  pl.BlockSpec((1, BD), lambda h, b, s: (0, 0))],
        out_specs=[pl.BlockSpec((4, t, BD), lambda h, b, s: (0, rb(b, s), h)),
                   pl.BlockSpec((2, BD), lambda h, b, s: (0, h)),
                   pl.BlockSpec((1, BD), lambda h, b, s: (0, 0))],
        out_shape=[SDS((4, n, D), BF16), SDS((2, D), F32), SDS((1, BD), F32)],
        scratch_shapes=[pltpu.VMEM((BD, BD), F32), pltpu.VMEM((8, BD), F32)],
        compiler_params=_params(48),
    )(z, z, z, z, o_all, st_all, dyb, lb_logits, hg_g)


def _mid(ya, yb, z, b_merge, x2, tgt, fin_g, pa, pb, wo):
    n = x2.shape[0]
    tm = 256
    ni = n // tm

    def body(ya_ref, yb_ref, gma_ref, gmb_ref, bm_ref, x_ref, t_ref, fg_ref, pa_hbm, pb_hbm, wo_hbm,
             dx2_ref, dya_ref, dyb_ref, dgm_ref, loss_ref, gfg_ref, gbm_ref, gpa_hbm, gpb_hbm, gwo_hbm,
             pa_v, pb_v, wo_v, gpa_v, gpb_v, gwo_v, sem):
        i = pl.program_id(0)
        loads = [pltpu.make_async_copy(src, dst, sem.at[k])
                 for k, (src, dst) in enumerate(((pa_hbm, pa_v), (pb_hbm, pb_v), (wo_hbm, wo_v)))]
        stores = [pltpu.make_async_copy(src, dst, sem.at[k])
                  for k, (src, dst) in enumerate(((gpa_v, gpa_hbm), (gpb_v, gpb_hbm), (gwo_v, gwo_hbm)))]

        @pl.when(i == 0)
        def _():
            for cp in loads:
                cp.start()
            for ref in (gpa_v, gpb_v, gwo_v, loss_ref, gfg_ref, gbm_ref):
                ref[...] = jnp.zeros(ref.shape, F32)
            for cp in loads:
                cp.wait()

        ya_v = ya_ref[...]
        yb_v = yb_ref[...]
        out_a = jnp.dot(ya_v, pa_v[...], preferred_element_type=F32)
        out_b = jnp.dot(yb_v, pb_v[...], preferred_element_type=F32)
        bm = bm_ref[...]
        g_a = _sigmoid(gma_ref[0] + bm[:, 0:D])
        g_b = _sigmoid(gmb_ref[0] + bm[:, D:2 * D])
        mixed = g_a * out_a + g_b * out_b
        mixb = mixed.astype(BF16)
        xo = x_ref[...] + jnp.dot(mixb, wo_v[...], preferred_element_type=F32)
        r = lax.rsqrt(jnp.mean(xo * xo, axis=-1, keepdims=True) + EPS)
        xn = xo * r
        fg = fg_ref[...]
        e = xn * fg - t_ref[...]
        loss_ref[...] += 0.5 * jnp.sum(jnp.mean(e * e, axis=-1, keepdims=True))
        dy = e * (1.0 / D)
        gfg_ref[...] += jnp.sum(dy * xn, axis=0, keepdims=True)
        dxn = dy * fg
        dx2 = r * (dxn - xn * jnp.mean(dxn * xn, axis=-1, keepdims=True))
        dx2_ref[...] = dx2
        dx2b = dx2.astype(BF16)
        d_mixed = lax.dot_general(dx2b, wo_v[...], NT_DIMS, preferred_element_type=F32)
        gwo_v[...] += lax.dot_general(mixb, dx2b, TN_DIMS, preferred_element_type=F32)
        d_oa = (d_mixed * g_a).astype(BF16)
        d_ob = (d_mixed * g_b).astype(BF16)
        dgm_a = (d_mixed * out_a) * (g_a * (1.0 - g_a))
        dgm_b = (d_mixed * out_b) * (g_b * (1.0 - g_b))
        gbm_ref[:, 0:D] += jnp.sum(dgm_a, axis=0, keepdims=True)
        gbm_ref[:, D:2 * D] += jnp.sum(dgm_b, axis=0, keepdims=True)
        dgm_ref[0] = dgm_a.astype(BF16)
        dgm_ref[1] = dgm_b.astype(BF16)
        dya_ref[...] = lax.dot_general(d_oa, pa_v[...], NT_DIMS, preferred_element_type=F32)
        dyb_ref[...] = lax.dot_general(d_ob, pb_v[...], NT_DIMS, preferred_element_type=F32)
        gpa_v[...] += lax.dot_general(ya_v, d_oa, TN_DIMS, preferred_element_type=F32)
        gpb_v[...] += lax.dot_general(yb_v, d_ob, TN_DIMS, preferred_element_type=F32)

        @pl.when(i == ni - 1)
        def _():
            for cp in stores:
                cp.start()
            for cp in stores:
                cp.wait()

    rows = pl.BlockSpec((tm, D), lambda i: (i, 0))
    rep = lambda shape: pl.BlockSpec(shape, lambda i: (0,) * len(shape))
    return pl.pallas_call(
        body, name="mid", grid=(ni,),
        in_specs=[rows, rows,
                  pl.BlockSpec((1, tm, D), lambda i: (6, i, 0)), pl.BlockSpec((1, tm, D), lambda i: (7, i, 0)),
                  rep((1, 2 * D)), rows, rows, rep((1, D)), ANY, ANY, ANY],
        out_specs=[rows, rows, rows, pl.BlockSpec((2, tm, D), lambda i: (0, i, 0)),
                   rep((8, BD)), rep((1, D)), rep((1, 2 * D)), ANY, ANY, ANY],
        out_shape=[SDS((n, D), F32), SDS((n, D), F32), SDS((n, D), F32), SDS((2, n, D), BF16),
                   SDS((8, BD), F32), SDS((1, D), F32), SDS((1, 2 * D), F32),
                   SDS((D, D), F32), SDS((D, D), F32), SDS((D, D), F32)],
        scratch_shapes=[pltpu.VMEM((D, D), BF16)] * 3 + [pltpu.VMEM((D, D), F32)] * 3 + [pltpu.SemaphoreType.DMA((3,))],
        compiler_params=_params(60),
    )(ya, yb, z, z, b_merge, x2, tgt, fin_g, pa, pb, wo)


def _dz_specs(tm, ni, row_major):
    if row_major:
        ia = lambda i, j: (jnp.minimum(j, 1), i, 0)
        ib = lambda i, j: (jnp.clip(j - 2, 0, 3), i, 0)
        im = lambda i, j: (jnp.clip(j - 6, 0, 1), i, 0)
    else:
        last = ni - 1
        ia = lambda j, i: (jnp.minimum(j, 1), jnp.where(j < 2, i, last), 0)
        ib = lambda j, i: (jnp.clip(j - 2, 0, 3), jnp.where(j < 2, 0, jnp.where(j < 6, i, last)), 0)
        im = lambda j, i: (jnp.clip(j - 6, 0, 1), jnp.where(j < 6, 0, i), 0)
    return [pl.BlockSpec((1, tm, D), f) for f in (ia, ib, im)]


def _inproj_bwd_x(dza, dzb, dzm, w_all, x2, dx2, norm_g, after):
    n = x2.shape[0]
    tm = 512
    ni = n // tm

    def body(dza_ref, dzb_ref, dzm_ref, w_ref, x_ref, dx2_ref, g_ref, after_ref, gx_ref, gg_ref, acc):
        i, j = pl.program_id(0), pl.program_id(1)

        @pl.when((i == 0) & (j == 0))
        def _():
            gg_ref[...] = jnp.zeros((1, D), F32)

        @pl.when(j == 0)
        def _():
            acc[...] = jnp.zeros((tm, D), F32)

        def add(ref):
            acc[...] += lax.dot_general(ref[0], w_ref[0], NT_DIMS, preferred_element_type=F32)

        pl.when(j < 2)(lambda: add(dza_ref))
        pl.when((j >= 2) & (j < 6))(lambda: add(dzb_ref))
        pl.when(j >= 6)(lambda: add(dzm_ref))

        @pl.when(j == NB - 1)
        def _():
            x = x_ref[...]
            r = lax.rsqrt(jnp.mean(x * x, axis=-1, keepdims=True) + EPS)
            xn = x * r
            dh = acc[...]
            gg_ref[...] += jnp.sum(dh * xn, axis=0, keepdims=True)
            dxn = dh * g_ref[...]
            gx_ref[...] = dx2_ref[...] + r * (dxn - xn * jnp.mean(dxn * xn, axis=-1, keepdims=True))

    rows = pl.BlockSpec((tm, D), lambda i, j: (i, 0))
    return pl.pallas_call(
        body, name="inproj_bwd_x", grid=(ni, NB),
        in_specs=_dz_specs(tm, ni, True) + [pl.BlockSpec((1, D, D), lambda i, j: (j, 0, 0)), rows, rows,
                                             pl.BlockSpec((1, D), lambda i, j: (0, 0)), ANY],
        out_specs=[rows, pl.BlockSpec((1, D), lambda i, j: (0, 0))],
        out_shape=[SDS((n, D), F32), SDS((1, D), F32)],
        scratch_shapes=[pltpu.VMEM((tm, D), F32)],
        compiler_params=_params(48),
    )(dza, dzb, dzm, w_all, x2, dx2, norm_g, after)


def _inproj_bwd_w(dza, dzb, dzm, h_all):
    n = h_all.shape[0]
    tm = 512
    ni = n // tm

    def body(dza_ref, dzb_ref, dzm_ref, h_ref, gw_ref):
        j, i = pl.program_id(0), pl.program_id(1)

        @pl.when(i == 0)
        def _():
            gw_ref[...] = jnp.zeros((1, D, D), F32)

        def add(ref):
            gw_ref[0] += lax.dot_general(h_ref[...], ref[0], TN_DIMS, preferred_element_type=F32)

        pl.when(j < 2)(lambda: add(dza_ref))
        pl.when((j >= 2) & (j < 6))(lambda: add(dzb_ref))
        pl.when(j >= 6)(lambda: add(dzm_ref))

    return pl.pallas_call(
        body, name="inproj_bwd_w", grid=(NB, ni),
        in_specs=_dz_specs(tm, ni, False) + [pl.BlockSpec((tm, D), lambda j, i: (i, 0))],
        out_specs=pl.BlockSpec((1, D, D), lambda j, i: (j, 0, 0)),
        out_shape=SDS((NB, D, D), F32),
        compiler_params=_params(48),
    )(dza, dzb, dzm, h_all)


def _adamw(w, g, m, v):
    rows, cols = w.shape
    tr = _row_tile(rows)

    def body(w_ref, g_ref, m_ref, v_ref, d_ref, nm_ref, nv_ref):
        gv = g_ref[...]
        nm = ADAM_B1 * m_ref[...] + (1.0 - ADAM_B1) * gv
        nv = ADAM_B2 * v_ref[...] + (1.0 - ADAM_B2) * (gv * gv)
        m_hat = nm / (1.0 - ADAM_B1 ** ADAM_STEP)
        v_hat = nv / (1.0 - ADAM_B2 ** ADAM_STEP)
        d_ref[...] = -ADAM_LR * (m_hat / (jnp.sqrt(v_hat) + ADAM_EPS) + ADAM_WD * w_ref[...])
        nm_ref[...] = nm
        nv_ref[...] = nv

    spec = pl.BlockSpec((tr, cols), lambda i: (i, 0))
    return pl.pallas_call(
        body, name="adamw", grid=(rows // tr,), in_specs=[spec] * 4, out_specs=[spec] * 3,
        out_shape=[SDS((rows, cols), F32)] * 3, compiler_params=_params(32),
    )(w, g, m, v)


def _place():
    return lax.axis_index("x"), lax.axis_index("y"), lax.axis_index("c")


def _allgather(blocks, dtypes, name):
    na = len(blocks)

    def body(*refs):
        ins, outs, stages = refs[:na], refs[na:2 * na], refs[2 * na:3 * na]
        send_sems, recv_sems, local_sems = refs[3 * na:]
        x, y, c = _place()
        me, sibling = (x, y, c), (x, y, 1 - c)
        chips = [(1 - x, y), (x, 1 - y), (1 - x, 1 - y)]
        blk = lambda p: 4 * p[0] + 2 * p[1] + p[2]

        def copy(a, k, block, to, src=None):
            return pltpu.make_async_remote_copy(
                src_ref=outs[a].at[blk(block)] if src is None else src, dst_ref=outs[a].at[blk(block)],
                send_sem=send_sems.at[7 * a + k], recv_sem=recv_sems.at[7 * a + k],
                device_id=to, device_id_type=MESH)

        mine, first, passed = [], [], []
        for a in range(na):
            stages[a][...] = ins[a][...].astype(dtypes[a])
            mine.append(pltpu.make_async_copy(stages[a], outs[a].at[blk(me)], local_sems.at[a]))
            mine[-1].start()
            first.append(copy(a, 0, me, sibling, src=stages[a]))
            first += [copy(a, 1 + j, me, (*chip, c), src=stages[a]) for j, chip in enumerate(chips)]
        for cp in first:
            cp.start()
        for j, chip in enumerate(chips):
            for a in range(na):
                copy(a, 1 + j, (*chip, c), me).wait_recv()
                passed.append(copy(a, 4 + j, (*chip, c), sibling))
                passed[-1].start()
        for a in range(na):
            copy(a, 0, sibling, me).wait_recv()
            for j, chip in enumerate(chips):
                copy(a, 4 + j, (*chip, 1 - c), me).wait_recv()
        for cp in first + passed:
            cp.wait_send()
        for cp in mine:
            cp.wait()

    return pl.pallas_call(
        body, name=name,
        in_specs=[pl.BlockSpec(memory_space=pltpu.VMEM)] * na, out_specs=[ANY] * na,
        out_shape=[SDS((NB,) + b.shape, dt) for b, dt in zip(blocks, dtypes)],
        scratch_shapes=[pltpu.VMEM(b.shape, dt) for b, dt in zip(blocks, dtypes)]
        + [pltpu.SemaphoreType.DMA((7 * na,)), pltpu.SemaphoreType.DMA((7 * na,)), pltpu.SemaphoreType.DMA((na,))],
        compiler_params=_params(40),
    )(*blocks)


def _rs_sibling(gs):
    na = len(gs)

    def body(*refs):
        ins, outs = refs[:na], refs[na:2 * na]
        send_sems, recv_sems = refs[2 * na:]
        x, y, c = _place()
        copies = []
        for a in range(na):
            for q in range(4):
                copies.append(pltpu.make_async_remote_copy(
                    src_ref=ins[a].at[2 * q + (1 - c)], dst_ref=outs[a].at[q],
                    send_sem=send_sems.at[4 * a + q], recv_sem=recv_sems.at[4 * a + q],
                    device_id=(x, y, 1 - c), device_id_type=MESH))
        for cp in copies:
            cp.start()
        for cp in copies:
            cp.wait_recv()
        for cp in copies:
            cp.wait_send()

    return pl.pallas_call(
        body, name="rs_sibling", in_specs=[ANY] * na, out_specs=[ANY] * na,
        out_shape=[SDS((4,) + g.shape[1:], F32) for g in gs],
        scratch_shapes=[pltpu.SemaphoreType.DMA((4 * na,)), pltpu.SemaphoreType.DMA((4 * na,))],
    )(*gs)


HBM = pl.BlockSpec(memory_space=pltpu.HBM)
SEMS = pl.BlockSpec(memory_space=pltpu.SEMAPHORE)
EFFECT = pltpu.SideEffectType.DATAFLOW_SIDE_EFFECTING


def _other_chips(x, y):
    return [(1 - x, y), (x, 1 - y), (1 - x, 1 - y)]


def _chip_copies(srcs, lands, send_sems, recv_sems):
    x, y, c = _place()
    return [pltpu.make_async_remote_copy(
        src_ref=srcs[a].at[slot], dst_ref=lands[a].at[slot],
        send_sem=send_sems.at[3 * a + slot], recv_sem=recv_sems.at[3 * a + slot],
        device_id=(px, py, c), device_id_type=MESH)
        for a in range(len(srcs)) for slot, (px, py) in enumerate(_other_chips(x, y))]


def _rs_chips_start(ps):
    na = len(ps)

    def body(*refs):
        srcs, lands = refs[:na], refs[na:2 * na]
        send_sems, recv_sems = refs[2 * na], refs[2 * na + 1]
        token = refs[-1]
        for cp in _chip_copies(srcs, lands, send_sems, recv_sems):
            cp.start()
        token[...] = jnp.zeros_like(token)

    hbm = lambda a: pltpu.HBM(a.shape, a.dtype)
    out = pl.pallas_call(
        body, name="rs_chips_start",
        out_shape=(pltpu.SemaphoreType.DMA((3 * na,)), pltpu.SemaphoreType.DMA((3 * na,)),
                   *[hbm(p) for p in ps], *[hbm(p) for p in ps], SDS((8, BD), F32)),
        in_specs=[HBM] * (2 * na),
        out_specs=(SEMS, SEMS, *[HBM] * (2 * na), pl.BlockSpec(memory_space=pltpu.VMEM)),
        input_output_aliases={i: 2 + i for i in range(2 * na)},
        compiler_params=pltpu.CompilerParams(has_side_effects=EFFECT),
    )(*[pltpu.with_memory_space_constraint(p, pltpu.HBM) for p in ps],
      *[pltpu.with_memory_space_constraint(lax.empty(p.shape, p.dtype), pltpu.HBM) for p in ps])
    return out[0], out[1], out[2:2 + na], out[2 + na:2 + 2 * na], out[-1]


def _rs_chips_wait(send_sems, recv_sems, srcs, lands, after):
    na = len(srcs)

    def body(*refs):
        srcs_r, lands_r = refs[:na], refs[na:2 * na]
        send_r, recv_r = refs[2 * na], refs[2 * na + 1]
        copies = _chip_copies(srcs_r, lands_r, send_r, recv_r)
        for cp in copies:
            cp.wait_send()
        for cp in copies:
            cp.wait_recv()

    hbm = lambda a: pltpu.HBM(a.shape, a.dtype)
    out = pl.pallas_call(
        body, name="rs_chips_wait",
        out_shape=(*[hbm(p) for p in srcs], *[hbm(p) for p in lands]),
        in_specs=[HBM] * (2 * na) + [SEMS, SEMS, ANY],
        out_specs=tuple([HBM] * (2 * na)),
        input_output_aliases={i: i for i in range(2 * na)},
        compiler_params=pltpu.CompilerParams(has_side_effects=EFFECT),
    )(*srcs, *lands, send_sems, recv_sems, after)
    return out[na:]


def _add_sibling(place, g, a_in):
    _, r, cols = g.shape
    tr = _row_tile(r)
    ni = r // tr

    def chip(k, pr):
        qx = jnp.where((k == 0) | (k == 2), 1 - pr[0], pr[0])
        qy = jnp.where((k == 1) | (k == 2), 1 - pr[1], pr[1])
        return 2 * qx + qy

    def body(place_ref, g_ref, a_ref, out_ref, own_ref):
        k = pl.program_id(0)
        total = g_ref[0] + a_ref[0]

        @pl.when(k < 3)
        def _():
            out_ref[0] = total.astype(BF16)

        @pl.when(k == 3)
        def _():
            own_ref[...] = total

    return pl.pallas_call(
        body, name="add_sibling",
        grid_spec=pltpu.PrefetchScalarGridSpec(
            num_scalar_prefetch=1, grid=(4, ni),
            in_specs=[pl.BlockSpec((1, tr, cols), lambda k, i, pr: (2 * chip(k, pr) + pr[2], i, 0)),
                      pl.BlockSpec((1, tr, cols), lambda k, i, pr: (chip(k, pr), i, 0))],
            out_specs=[pl.BlockSpec((1, tr, cols), lambda k, i, pr: (jnp.minimum(k, 2), jnp.where(k < 3, i, ni - 1), 0)),
                       pl.BlockSpec((tr, cols), lambda k, i, pr: (jnp.where(k < 3, 0, i), 0))]),
        out_shape=[SDS((3, r, cols), BF16), SDS((r, cols), F32)], compiler_params=_params(32),
    )(place, g, a_in)


def _add_chips(own, b_in):
    r, cols = own.shape
    tr = _row_tile(r)

    def body(p_ref, b0_ref, b1_ref, b2_ref, o_ref):
        o_ref[...] = ((p_ref[...] + b0_ref[0].astype(F32)) + b1_ref[0].astype(F32)) + b2_ref[0].astype(F32)

    slot = lambda k: pl.BlockSpec((1, tr, cols), lambda i: (k, i, 0))
    spec = pl.BlockSpec((tr, cols), lambda i: (i, 0))
    return pl.pallas_call(
        body, name="add_chips", grid=(r // tr,), in_specs=[spec, slot(0), slot(1), slot(2)], out_specs=spec,
        out_shape=SDS((r, cols), F32), compiler_params=_params(32),
    )(own, b_in, b_in, b_in)


VEC_NAMES = ("b_merge", "conv_b", "rg_bx", "rg_ba", "rg_lambda", "hg_lb_logits", "hg_norm_g", "final_norm_g")
REP_NAMES = ("rg_wx", "rg_wa", "norm_g") + VEC_NAMES
SMALL_AT = 3 * BD
SMALL_ROWS = 48
MID_ROWS = 448


def _sum_blocks(parts):
    def body(p_ref, o_ref):
        acc = p_ref[0, 0:1, :]
        for k in range(1, NB):
            acc = acc + p_ref[k, 0:1, :]
        o_ref[...] = acc

    return pl.pallas_call(body, name="sum_blocks", out_shape=SDS((1, parts.shape[2]), F32))(parts)


def _pack_rows(arrays, width, row_multiple=8):
    flat = jnp.concatenate([a.reshape(-1) for a in arrays])
    rows = -(-flat.shape[0] // width)
    rows = -(-rows // row_multiple) * row_multiple
    return jnp.pad(flat, (0, rows * width - flat.shape[0])).reshape(rows, width)


def _unpack(flat, like):
    out, off = [], 0
    for a in like:
        out.append(flat[off:off + a.size].reshape(a.shape))
        off += a.size
    return out


def kernel(x, w_in, b_merge, conv_w, conv_b, rg_wx, rg_bx, rg_wa, rg_ba, rg_lambda, hg_lb_logits, hg_norm_g, proj_a, proj_b, w_out, norm_g, final_norm_g, loss_target, m_w_in, m_b_merge, m_conv_w, m_conv_b, m_rg_wx, m_rg_bx, m_rg_wa, m_rg_ba, m_rg_lambda, m_hg_lb_logits, m_hg_norm_g, m_proj_a, m_proj_b, m_w_out, m_norm_g, m_final_norm_g, v_w_in, v_b_merge, v_conv_w, v_conv_b, v_rg_wx, v_rg_bx, v_rg_wa, v_rg_ba, v_rg_lambda, v_hg_lb_logits, v_hg_norm_g, v_proj_a, v_proj_b, v_w_out, v_norm_g, v_final_norm_g):
    weights = dict(w_in=w_in, b_merge=b_merge, conv_w=conv_w, conv_b=conv_b, rg_wx=rg_wx, rg_bx=rg_bx, rg_wa=rg_wa,
                   rg_ba=rg_ba, rg_lambda=rg_lambda, hg_lb_logits=hg_lb_logits, hg_norm_g=hg_norm_g, proj_a=proj_a,
                   proj_b=proj_b, w_out=w_out, norm_g=norm_g, final_norm_g=final_norm_g)
    mom1 = dict(w_in=m_w_in, b_merge=m_b_merge, conv_w=m_conv_w, conv_b=m_conv_b, rg_wx=m_rg_wx, rg_bx=m_rg_bx,
                rg_wa=m_rg_wa, rg_ba=m_rg_ba, rg_lambda=m_rg_lambda, hg_lb_logits=m_hg_lb_logits,
                hg_norm_g=m_hg_norm_g, proj_a=m_proj_a, proj_b=m_proj_b, w_out=m_w_out, norm_g=m_norm_g,
                final_norm_g=m_final_norm_g)
    mom2 = dict(w_in=v_w_in, b_merge=v_b_merge, conv_w=v_conv_w, conv_b=v_conv_b, rg_wx=v_rg_wx, rg_bx=v_rg_bx,
                rg_wa=v_rg_wa, rg_ba=v_rg_ba, rg_lambda=v_rg_lambda, hg_lb_logits=v_hg_lb_logits,
                hg_norm_g=v_hg_norm_g, proj_a=v_proj_a, proj_b=v_proj_b, w_out=v_w_out, norm_g=v_norm_g,
                final_norm_g=v_final_norm_g)
    order = list(weights)
    nb, s_len, _ = x.shape
    n = nb * s_len
    px, py, pc = _place()
    place = jnp.stack([px, py, pc]).astype(jnp.int32)

    cw_blk = jnp.pad(conv_w[0], ((0, 4), (0, 0)))
    w_all, pa_all, pb_all, wo_all, cw_all = _allgather(
        [w_in[0], proj_a[0], proj_b[0], w_out[0], cw_blk], [BF16, BF16, BF16, BF16, F32], "gather_weights")
    pa_full, pb_full, wo_full = (a.reshape(D, D) for a in (pa_all, pb_all, wo_all))
    cw8 = cw_all.transpose(1, 0, 2).reshape(8, D)
    wx_b, wa_b = rg_wx[0].astype(BF16), rg_wa[0].astype(BF16)
    cb, bx, ba = conv_b, rg_bx.reshape(1, D), rg_ba.reshape(1, D)
    fin_g = final_norm_g.reshape(1, D)

    x2 = x.reshape(n, D)
    z, h_all = _inproj_fwd(x2, norm_g, w_all)
    hlru, ya = _lru_fwd(z, cw8, cb, wx_b, wa_b, bx, ba, rg_lambda, nb, s_len)
    o_all, yb, st_all = _hgrn_fwd(z, hg_lb_logits, hg_norm_g, nb, s_len)

    (dx2, dya, dyb, dzm, loss_acc, g_fin, g_bm, g_pa, g_pb, g_wo) = _mid(
        ya, yb, z, b_merge, x2, loss_target.reshape(n, D), fin_g, pa_full, pb_full, wo_full)
    dzb, g_lg, g_hg = _hgrn_bwd(z, o_all, st_all, dyb, hg_lb_logits, hg_norm_g, nb, s_len)
    dza, g_cw8, g_cb, g_wx, g_wa, g_bx, g_ba, g_lam = _lru_bwd(
        z, hlru, dya, cw8, cb, wx_b, wa_b, bx, ba, rg_lambda, nb, s_len)
    g_w = _inproj_bwd_w(dza, dzb, dzm, h_all)

    part = dict(b_merge=g_bm, conv_b=g_cb, rg_bx=g_bx, rg_ba=g_ba, rg_lambda=g_lam, hg_lb_logits=g_lg,
                hg_norm_g=g_hg, final_norm_g=g_fin)
    vec = _pack_rows([part[k] for k in VEC_NAMES], BD)
    vec = jnp.pad(vec, ((0, 16 * NB - vec.shape[0]), (0, 0))).reshape(NB, 2, D)
    rows8 = lambda a: jnp.pad(a, ((0, 0), (0, 8 - a.shape[1]), (0, 0)))
    g_m = jnp.concatenate([g.reshape(NB, BD, D) for g in (g_pa, g_pb, g_wo)]
                          + [g_wx.reshape(NB, 16, D), g_wa.reshape(NB, 16, D),
                             rows8(g_cw8.reshape(8, NB, BD).transpose(1, 0, 2).reshape(NB, 1, D)), rows8(vec),
                             jnp.zeros((NB, MID_ROWS - SMALL_AT - SMALL_ROWS, D), F32)], axis=1)
    from_sibling = _rs_sibling([g_w, g_m])
    w_out_bf, w_own = _add_sibling(place, g_w, from_sibling[0])
    m_out_bf, m_own = _add_sibling(place, g_m, from_sibling[1])
    send_sems, recv_sems, srcs, lands, token = _rs_chips_start([w_out_bf, m_out_bf])
    grad_x, g_ng = _inproj_bwd_x(dza, dzb, dzm, w_all, x2, dx2, norm_g, token)
    from_chips = _rs_chips_wait(send_sems, recv_sems, srcs, lands, grad_x)
    r_w = _add_chips(w_own, from_chips[0])
    r_m = _add_chips(m_own, from_chips[1])
    tail = jnp.concatenate([r_m[SMALL_AT:SMALL_AT + SMALL_ROWS], jnp.pad(g_ng, ((0, 7), (0, 0)))], axis=0)
    (tail_all,) = _allgather([tail], [F32], "gather_small_grads")

    grads = dict(w_in=r_w.reshape(1, D, D),
                 proj_a=r_m[0:BD].reshape(1, BD, D), proj_b=r_m[BD:2 * BD].reshape(1, BD, D),
                 w_out=r_m[2 * BD:3 * BD].reshape(1, BD, D),
                 conv_w=r_m[SMALL_AT + 32].reshape(8, BD)[0:4].reshape(1, 4, BD),
                 rg_wx=tail_all[:, 0:16].reshape(1, NB, BD, BD), rg_wa=tail_all[:, 16:32].reshape(1, NB, BD, BD),
                 norm_g=_sum_blocks(tail_all[:, SMALL_ROWS:SMALL_ROWS + 8]))
    vec_all = tail_all[:, 40:42].reshape(-1)
    for k, gk in zip(VEC_NAMES, _unpack(vec_all, [weights[k] for k in VEC_NAMES])):
        grads[k] = gk

    delta, new_m, new_v = {}, {}, {}
    for k in ("w_in", "proj_a", "proj_b", "w_out"):
        shp = weights[k].shape
        two = lambda a: a.reshape(shp[1], shp[2])
        d_k, m_k, v_k = _adamw(two(weights[k]), two(grads[k]), two(mom1[k]), two(mom2[k]))
        delta[k], new_m[k], new_v[k] = d_k.reshape(shp), m_k.reshape(shp), v_k.reshape(shp)
    rep = list(REP_NAMES) + ["conv_w"]
    packs = [_pack_rows([t[k] for k in rep], BD, 256) for t in (weights, grads, mom1, mom2)]
    outs = _adamw(*packs)
    for tgt, flat in zip((delta, new_m, new_v), outs):
        for k, a in zip(rep, _unpack(flat.reshape(-1), [weights[k] for k in rep])):
            tgt[k] = a

    loss = lax.psum(loss_acc[0, 0], ("x", "y", "c"))
    return (loss, grad_x.reshape(x.shape), *[grads[k] for k in order], *[delta[k] for k in order],
            *[new_m[k] for k in order], *[new_v[k] for k in order])
```

```python
import jax
import jax.numpy as jnp
from jax import lax
from jax.experimental import pallas as pl
from jax.experimental.pallas import tpu as pltpu

F32 = jnp.float32
BF16 = jnp.bfloat16
SDS = jax.ShapeDtypeStruct
MESH = pl.DeviceIdType.MESH
ANY = pl.BlockSpec(memory_space=pl.ANY)

D = 1024
NB = 8
BD = D // NB
CHUNK = 64
EPS = 1e-6
LRU_C = 8.0
HG_SCALE = BD ** -0.5
ADAM_LR, ADAM_B1, ADAM_B2, ADAM_EPS, ADAM_WD, ADAM_STEP = 0.001, 0.9, 0.999, 1e-08, 0.01, 10

NT_DIMS = (((1,), (1,)), ((), ()))
TN_DIMS = (((0,), (0,)), ((), ()))


def _params(vmem_mib):
    return pltpu.CompilerParams(vmem_limit_bytes=vmem_mib << 20)


def _row_tile(rows, most=256):
    assert rows % 8 == 0
    return max(t for t in range(8, min(rows, most) + 1, 8) if rows % t == 0)


def _sigmoid(v):
    return jax.nn.sigmoid(v)


def _softplus_neg(lam):
    t = -lam
    e = jnp.exp(-jnp.abs(t))
    w = 1.0 + e
    d = w - 1.0
    l1p = jnp.where(d == 0.0, e, jnp.log(w) * (e / jnp.where(d == 0.0, 1.0, d)))
    return jnp.maximum(t, 0.0) + l1p


def _place():
    return lax.axis_index("x"), lax.axis_index("y"), lax.axis_index("c")


def _other_chips(x, y):
    return [(1 - x, y), (x, 1 - y), (1 - x, 1 - y)]


def _block_id(p):
    return 4 * p[0] + 2 * p[1] + p[2]


def _arrival_order(x, y, c):
    near, far, diag = _other_chips(x, y)
    return [(x, y, c), (x, y, 1 - c), (*near, c), (*far, c), (*near, 1 - c), (*far, 1 - c), (*diag, c), (*diag, 1 - c)]


def _gather_inproj(order_ids, x2, norm_g, blocks, dtypes):
    na = len(blocks)
    n = x2.shape[0]
    tm = min(n, 1024)
    ni = n // tm

    def body(order_ref, x_ref, g_ref, *refs):
        ins, (z_ref, h_ref), outs = refs[:na], refs[na:na + 2], refs[na + 2:2 * na + 2]
        stages = refs[2 * na + 2:3 * na + 2]
        h_full, wbuf, send_sems, recv_sems, local_sems, wsem, hsem = refs[3 * na + 2:]
        j, i = pl.program_id(0), pl.program_id(1)
        x, y, c = _place()
        me, sibling = (x, y, c), (x, y, 1 - c)
        chips = _other_chips(x, y)
        small = range(1, na)

        def copy(a, k, block, to, src=None):
            return pltpu.make_async_remote_copy(
                src_ref=outs[a].at[_block_id(block)] if src is None else src, dst_ref=outs[a].at[_block_id(block)],
                send_sem=send_sems.at[7 * a + k], recv_sem=recv_sems.at[7 * a + k],
                device_id=to, device_id_type=MESH)

        def local(a):
            return pltpu.make_async_copy(stages[a], outs[a].at[_block_id(me)], local_sems.at[a])

        def landed(a, slot):
            copy(a, 1 + slot, (*chips[slot], c), me).wait_recv()
            copy(a, 4 + slot, (*chips[slot], c), sibling).start()

        def passed_on(a, slot):
            copy(a, 4 + slot, (*chips[slot], 1 - c), me).wait_recv()

        @pl.when((j == 0) & (i == 0))
        def _():
            for a in range(na):
                stages[a][...] = ins[a][...].astype(dtypes[a])
                local(a).start()
            for a in range(na):
                copy(a, 0, me, sibling, src=stages[a]).start()
                for slot, chip in enumerate(chips):
                    copy(a, 1 + slot, me, (*chip, c), src=stages[a]).start()

        @pl.when(j == 0)
        def _():
            xv = x_ref[...]
            r = lax.rsqrt(jnp.mean(xv * xv, axis=-1, keepdims=True) + EPS)
            hb = ((xv * r) * g_ref[...]).astype(BF16)
            h_full[pl.ds(pl.multiple_of(i * tm, tm), tm), :] = hb

        save_h = pltpu.make_async_copy(h_full, h_ref, hsem)
        pl.when((j == 0) & (i == ni - 1))(save_h.start)

        steps = [
            [(lambda: local(0).wait(), ())],
            [(lambda: copy(0, 0, sibling, me).wait_recv(), ())],
            [(landed, (0, 0))],
            [(landed, (0, 1))] + [(landed, (a, 0)) for a in small],
            [(passed_on, (0, 0))] + [(landed, (a, 1)) for a in small],
            [(passed_on, (0, 1))],
            [(landed, (0, 2))],
            [(passed_on, (0, 2))] + [(landed, (a, 2)) for a in small],
        ]
        for k, todo in enumerate(steps):
            @pl.when((i == 0) & (j == k))
            def _(todo=todo):
                for fn, args in todo:
                    fn(*args)

        @pl.when(i == 0)
        def _():
            load = pltpu.make_async_copy(outs[0].at[order_ref[j]], wbuf, wsem)
            load.start()
            load.wait()

        z_ref[0] = jnp.dot(h_full[pl.ds(pl.multiple_of(i * tm, tm), tm), :], wbuf[...], preferred_element_type=F32)

        @pl.when((j == NB - 1) & (i == ni - 1))
        def _():
            save_h.wait()
            for a in small:
                local(a).wait()
                copy(a, 0, sibling, me).wait_recv()
                for slot in range(3):
                    passed_on(a, slot)
            for a in range(na):
                copy(a, 0, me, sibling, src=stages[a]).wait_send()
                for slot, chip in enumerate(chips):
                    copy(a, 1 + slot, me, (*chip, c), src=stages[a]).wait_send()
                    copy(a, 4 + slot, (*chip, c), sibling).wait_send()

    rows_once = lambda j, i, order: (jnp.where(j == 0, i, ni - 1), 0)
    vmem = pl.BlockSpec(memory_space=pltpu.VMEM)
    return pl.pallas_call(
        body, name="gather_inproj",
        grid_spec=pltpu.PrefetchScalarGridSpec(
            num_scalar_prefetch=1, grid=(NB, ni),
            in_specs=[pl.BlockSpec((tm, D), rows_once), pl.BlockSpec((1, D), lambda j, i, order: (0, 0))] + [vmem] * na,
            out_specs=[pl.BlockSpec((1, tm, D), lambda j, i, order: (order[j], i, 0)), ANY] + [ANY] * na,
            scratch_shapes=[pltpu.VMEM(b.shape, dt) for b, dt in zip(blocks, dtypes)]
            + [pltpu.VMEM((n, D), BF16), pltpu.VMEM((D, D), BF16),
               pltpu.SemaphoreType.DMA((7 * na,)), pltpu.SemaphoreType.DMA((7 * na,)),
               pltpu.SemaphoreType.DMA((na,)), pltpu.SemaphoreType.DMA(()), pltpu.SemaphoreType.DMA(())]),
        out_shape=[SDS((NB, n, D), F32), SDS((n, D), BF16)] + [SDS((NB,) + b.shape, dt) for b, dt in zip(blocks, dtypes)],
        compiler_params=_params(56),
    )(order_ids, x2, norm_g, *blocks)


LRU_T = 256


def _conv(ext, cw, cb):
    t = LRU_T
    acc = ext[5:5 + t, :] * cw[0:1, :] + ext[6:6 + t, :] * cw[1:2, :]
    acc = acc + ext[7:7 + t, :] * cw[2:3, :]
    acc = acc + ext[8:8 + t, :] * cw[3:4, :]
    return cb + acc


def _lru_gates(xa, wx_ref, wa_ref, bx, ba, lam):
    xab = xa.astype(BF16)
    pis, prs = [], []
    for h in range(NB):
        xs = xab[:, h * BD:(h + 1) * BD]
        pis.append(jnp.dot(xs, wx_ref[h], preferred_element_type=F32))
        prs.append(jnp.dot(xs, wa_ref[h], preferred_element_type=F32))
    gi = _sigmoid(jnp.concatenate(pis, axis=1) + bx)
    gr = _sigmoid(jnp.concatenate(prs, axis=1) + ba)
    sp = _softplus_neg(lam)
    log_a = (-LRU_C * gr) * sp
    a = jnp.exp(log_a)
    mult = jnp.sqrt(-jnp.tanh(log_a) * (a * a + 1.0))
    return xab, gi, gr, sp, a, mult


def _lru_fwd(z, cw8, cb, wx, wa, bx, ba, lam, nb, s_len):
    n = nb * s_len
    t = LRU_T
    ns = s_len // t

    def body(xp_ref, ga_ref, cw_ref, cb_ref, wx_ref, wa_ref, bx_ref, ba_ref, lam_ref,
             h_ref, ya_ref, ext, a_s, u_s, carry):
        @pl.when(pl.program_id(1) == 0)
        def _():
            ext[0:8, :] = jnp.zeros((8, D), F32)
            carry[...] = jnp.zeros((8, D), F32)

        ext[8:8 + t, :] = xp_ref[0]
        xa = _conv(ext, cw_ref[...], cb_ref[...])
        ext[0:8, :] = ext[t:t + 8, :]
        _, gi, _, _, a, mult = _lru_gates(xa, wx_ref, wa_ref, bx_ref[...], ba_ref[...], lam_ref[...])
        u = (mult * gi) * xa
        row = lax.broadcasted_iota(jnp.int32, (t, D), 0) & 7
        for sh in (1, 2, 4):
            a_sh = pltpu.roll(a, sh, 0)
            u_sh = pltpu.roll(u, sh, 0)
            m = row >= sh
            u = jnp.where(m, a * u_sh + u, u)
            a = jnp.where(m, a * a_sh, a)
        a_s[...] = a
        u_s[...] = u

        def step(g, c):
            r = pl.multiple_of(g * 8, 8)
            hg = u_s[pl.ds(r, 8), :] + a_s[pl.ds(r, 8), :] * c
            h_ref[pl.ds(r, 8), :] = hg
            return hg[7:8, :]

        c_out = lax.fori_loop(0, t // 8, step, carry[0:1, :], unroll=4)
        carry[0:1, :] = c_out
        ga = ga_ref[0]
        ya_ref[...] = (h_ref[...] * (ga * _sigmoid(ga))).astype(BF16)

    row_map = lambda b, s: (b * ns + s, 0)
    rep2 = lambda b, s: (0, 0)
    rep3 = lambda b, s: (0, 0, 0)
    return pl.pallas_call(
        body, name="lru_fwd", grid=(nb, ns),
        in_specs=[pl.BlockSpec((1, t, D), lambda b, s: (0, b * ns + s, 0)),
                  pl.BlockSpec((1, t, D), lambda b, s: (1, b * ns + s, 0)),
                  pl.BlockSpec((8, D), rep2), pl.BlockSpec((1, D), rep2),
                  pl.BlockSpec((NB, BD, BD), rep3), pl.BlockSpec((NB, BD, BD), rep3),
                  pl.BlockSpec((1, D), rep2), pl.BlockSpec((1, D), rep2), pl.BlockSpec((1, D), rep2)],
        out_specs=[pl.BlockSpec((t, D), row_map), pl.BlockSpec((t, D), row_map)],
        out_shape=[SDS((n, D), F32), SDS((n, D), BF16)],
        scratch_shapes=[pltpu.VMEM((t + 8, D), F32), pltpu.VMEM((t, D), F32), pltpu.VMEM((t, D), F32),
                        pltpu.VMEM((8, D), F32)],
        compiler_params=_params(48),
    )(z, z, cw8, cb, wx, wa, bx, ba, lam)


def _lru_bwd(z, h_all, dya, cw8, cb, wx, wa, bx, ba, lam, nb, s_len):
    n = nb * s_len
    t = LRU_T
    ns = s_len // t
    t8 = t // 8

    def body(xp_ref, xph_ref, ga_ref, h_ref, hh_ref, dya_ref, cw_ref, cb_ref, wx_ref, wa_ref, bx_ref, ba_ref,
             lam_ref, dz_ref, gcw_ref, gcb_ref, gwx_ref, gwa_ref, gbx_ref, gba_ref, glam_ref,
             ext, hext, dext, a_s, u_s, dh_s, carry):
        b, s = pl.program_id(0), pl.program_id(1)
        first_tile = s == ns - 1

        @pl.when((b == 0) & (s == 0))
        def _():
            for ref in (gcw_ref, gcb_ref, gwx_ref, gwa_ref, gbx_ref, gba_ref, glam_ref):
                ref[...] = jnp.zeros(ref.shape, F32)

        @pl.when(s == 0)
        def _():
            dext[t:t + 8, :] = jnp.zeros((8, D), F32)
            carry[...] = jnp.zeros((8, D), F32)

        keep = jnp.where(first_tile, 0.0, 1.0)
        ext[0:8, :] = xph_ref[0] * keep
        ext[8:8 + t, :] = xp_ref[0]
        hext[0:8, :] = hh_ref[...] * keep
        hext[8:8 + t, :] = h_ref[...]
        cw = cw_ref[...]
        lam = lam_ref[...]
        xa = _conv(ext, cw, cb_ref[...])
        xab, gi, gr, sp, a, mult = _lru_gates(xa, wx_ref, wa_ref, bx_ref[...], ba_ref[...], lam)
        h_prev = hext[7:7 + t, :]
        ga = ga_ref[0]
        sg = _sigmoid(ga)
        dya_v = dya_ref[...]
        d_ga = dya_v * h_ref[...] * (sg * (1.0 + ga * (1.0 - sg)))
        g_in = dya_v * (ga * sg)

        rows = lax.broadcasted_iota(jnp.int32, (t, D), 0)
        row = rows & 7
        an = jnp.where(rows == t - 1, 1.0, pltpu.roll(a, t - 1, 0))
        u = g_in
        for sh in (1, 2, 4):
            a_sh = pltpu.roll(an, t - sh, 0)
            u_sh = pltpu.roll(u, t - sh, 0)
            m = row < 8 - sh
            u = jnp.where(m, u + an * u_sh, u)
            an = jnp.where(m, an * a_sh, an)
        a_s[...] = an
        u_s[...] = u

        def step(i, c):
            r = pl.multiple_of((t8 - 1 - i) * 8, 8)
            dg = u_s[pl.ds(r, 8), :] + a_s[pl.ds(r, 8), :] * c
            dh_s[pl.ds(r, 8), :] = dg
            return dg[0:1, :]

        lax.fori_loop(0, t8, step, carry[0:1, :], unroll=4)
        dh = dh_s[...]
        carry[0:1, :] = a[0:1, :] * dh[0:1, :]

        d_a = dh * h_prev
        dux = dh * xa
        d_mult = dux * gi
        d_gi = dux * mult
        d_xa = dh * (mult * gi)
        d_loga = d_a * a - d_mult * ((a * a) / mult)
        d_gr = d_loga * (-LRU_C * sp)
        d_sp = jnp.sum(d_loga * (-LRU_C * gr), axis=0, keepdims=True)
        glam_ref[...] += d_sp * (-_sigmoid(-lam))
        d_pi = d_gi * gi * (1.0 - gi)
        d_pr = d_gr * gr * (1.0 - gr)
        gbx_ref[...] += jnp.sum(d_pi, axis=0, keepdims=True)
        gba_ref[...] += jnp.sum(d_pr, axis=0, keepdims=True)
        dpib = d_pi.astype(BF16)
        dprb = d_pr.astype(BF16)
        back = []
        for h in range(NB):
            cs = slice(h * BD, (h + 1) * BD)
            gwx_ref[h] += lax.dot_general(xab[:, cs], dpib[:, cs], TN_DIMS, preferred_element_type=F32)
            gwa_ref[h] += lax.dot_general(xab[:, cs], dprb[:, cs], TN_DIMS, preferred_element_type=F32)
            back.append(lax.dot_general(dpib[:, cs], wx_ref[h], NT_DIMS, preferred_element_type=F32)
                        + lax.dot_general(dprb[:, cs], wa_ref[h], NT_DIMS, preferred_element_type=F32))
        d_xa = d_xa + jnp.concatenate(back, axis=1)

        dext[0:t, :] = d_xa
        d_xp = dext[3:3 + t, :] * cw[0:1, :] + dext[2:2 + t, :] * cw[1:2, :]
        d_xp = d_xp + dext[1:1 + t, :] * cw[2:3, :]
        d_xp = d_xp + d_xa * cw[3:4, :]
        dext[t:t + 8, :] = d_xa[0:8, :]
        gcb_ref[...] += jnp.sum(d_xa, axis=0, keepdims=True)
        for k in range(4):
            gcw_ref[k:k + 1, :] += jnp.sum(d_xa * ext[5 + k:5 + k + t, :], axis=0, keepdims=True)
        dz_ref[0] = d_xp.astype(BF16)
        dz_ref[1] = d_ga.astype(BF16)

    rb = lambda b, s: b * ns + (ns - 1 - s)
    halo = lambda b, s: jnp.maximum(rb(b, s) * t8 - 1, 0)
    rep2 = lambda b, s: (0, 0)
    rep3 = lambda b, s: (0, 0, 0)
    return pl.pallas_call(
        body, name="lru_bwd", grid=(nb, ns),
        in_specs=[pl.BlockSpec((1, t, D), lambda b, s: (0, rb(b, s), 0)),
                  pl.BlockSpec((1, 8, D), lambda b, s: (0, halo(b, s), 0)),
                  pl.BlockSpec((1, t, D), lambda b, s: (1, rb(b, s), 0)),
                  pl.BlockSpec((t, D), lambda b, s: (rb(b, s), 0)),
                  pl.BlockSpec((8, D), lambda b, s: (halo(b, s), 0)),
                  pl.BlockSpec((t, D), lambda b, s: (rb(b, s), 0)),
                  pl.BlockSpec((8, D), rep2), pl.BlockSpec((1, D), rep2),
                  pl.BlockSpec((NB, BD, BD), rep3), pl.BlockSpec((NB, BD, BD), rep3),
                  pl.BlockSpec((1, D), rep2), pl.BlockSpec((1, D), rep2), pl.BlockSpec((1, D), rep2)],
        out_specs=[pl.BlockSpec((2, t, D), lambda b, s: (0, rb(b, s), 0)),
                   pl.BlockSpec((8, D), rep2), pl.BlockSpec((1, D), rep2),
                   pl.BlockSpec((NB, BD, BD), rep3), pl.BlockSpec((NB, BD, BD), rep3),
                   pl.BlockSpec((1, D), rep2), pl.BlockSpec((1, D), rep2), pl.BlockSpec((1, D), rep2)],
        out_shape=[SDS((2, n, D), BF16), SDS((8, D), F32), SDS((1, D), F32),
                   SDS((NB, BD, BD), F32), SDS((NB, BD, BD), F32),
                   SDS((1, D), F32), SDS((1, D), F32), SDS((1, D), F32)],
        scratch_shapes=[pltpu.VMEM((t + 8, D), F32), pltpu.VMEM((t + 8, D), F32), pltpu.VMEM((t + 8, D), F32),
                        pltpu.VMEM((t, D), F32), pltpu.VMEM((t, D), F32), pltpu.VMEM((t, D), F32),
                        pltpu.VMEM((8, D), F32)],
        compiler_params=_params(56),
    )(z, z, z, h_all, h_all, dya, cw8, cb, wx, wa, bx, ba, lam)


HG_T = 512
HG_NC = HG_T // CHUNK
BNT_DIMS = (((2,), (2,)), ((0,), (0,)))
BNN_DIMS = (((2,), (1,)), ((0,), (0,)))
BTN_DIMS = (((1,), (1,)), ((0,), (0,)))


def _lower_bound(lg):
    m = jnp.max(lg, axis=0, keepdims=True)
    e = jnp.exp(lg - m)
    return e[0:1, :] / jnp.sum(e, axis=0, keepdims=True)


def _tri(upper):
    r = lax.broadcasted_iota(jnp.int32, (HG_NC, CHUNK, CHUNK), 1)
    c = lax.broadcasted_iota(jnp.int32, (HG_NC, CHUNK, CHUNK), 2)
    return (c >= r) if upper else (r >= c)


def _bdot(a, b, dims, precision=None):
    return lax.dot_general(a, b, dims, precision=precision, preferred_element_type=F32)


def _chunks(a):
    return a.reshape(HG_NC, CHUNK, BD)


def _hg_tile(q, fp, lb):
    q, fp = _chunks(q), _chunks(fp)
    sig = _sigmoid(fp)
    f = lb + (1.0 - lb) * sig
    log_f = jnp.log(f)
    k = 1.0 - f
    b = _bdot(_tri(False).astype(F32), log_f, BNN_DIMS, lax.Precision.HIGHEST)
    b_mid = b[:, CHUNK // 2:CHUNK // 2 + 1, :]
    b_last = b[:, CHUNK - 1:CHUNK, :]
    sq = _sigmoid(q)
    qh = q * sq
    e_qi = jnp.exp(b - b_mid)
    e_ki = jnp.exp(b_mid - b)
    e_qs = jnp.exp(b)
    e_ks = jnp.exp(b_last - b)
    dc = jnp.exp(b_last)
    q_in = (qh * e_qi) * HG_SCALE
    k_in = k * e_ki
    q_st = (qh * e_qs) * HG_SCALE
    k_st = k * e_ks
    att = _bdot(q_in.astype(BF16), k_in.astype(BF16), BNT_DIMS)
    att = jnp.where(_tri(False), att, 0.0)
    return dict(q=q, sig=sig, f=f, k=k, sq=sq, e_qi=e_qi, e_ki=e_ki, e_qs=e_qs, e_ks=e_ks, dc=dc,
                q_in=q_in, k_in=k_in, q_st=q_st, k_st=k_st, att=att)


def _hgrn_fwd(z, lb_logits, hg_g, nb, s_len):
    n = nb * s_len
    t = HG_T
    ns = s_len // t
    nchunk = s_len // CHUNK

    def body(q_ref, f_ref, v_ref, gb_ref, lg_ref, g_ref, o_ref, yb_ref, st_ref, st):
        @pl.when(pl.program_id(2) == 0)
        def _():
            st[...] = jnp.zeros((BD, BD), F32)

        lb = _lower_bound(lg_ref[...])
        ck = _hg_tile(q_ref[0], f_ref[0], lb)
        vb = _chunks(v_ref[0]).astype(BF16)
        kv = _bdot(vb, ck["k_st"].astype(BF16), BTN_DIMS)
        states = [st[...]]
        for c in range(HG_NC):
            states.append(states[c] * ck["dc"][c] + kv[c])
        st[...] = states[HG_NC]
        s_in = jnp.stack(states[:HG_NC], axis=0)
        st_ref[...] = s_in
        o = _bdot(ck["att"].astype(BF16), vb, BNN_DIMS) + _bdot(ck["q_st"].astype(BF16), s_in.astype(BF16), BNT_DIMS)
        o_ref[...] = o.reshape(t, BD)
        r = lax.rsqrt(jnp.mean(o * o, axis=-1, keepdims=True) + EPS)
        gb = _chunks(gb_ref[0])
        yb_ref[...] = (((o * r) * g_ref[...]) * (gb * _sigmoid(gb))).astype(BF16).reshape(t, BD)

    seg = lambda j: pl.BlockSpec((1, t, BD), lambda h, b, s: (j, b * ns + s, h))
    tile = pl.BlockSpec((t, BD), lambda h, b, s: (b * ns + s, h))
    return pl.pallas_call(
        body, name="hgrn_fwd", grid=(NB, nb, ns),
        in_specs=[seg(2), seg(3), seg(4), seg(5),
                  pl.BlockSpec((2, BD), lambda h, b, s: (0, h)),
                  pl.BlockSpec((1, BD), lambda h, b, s: (0, 0))],
        out_specs=[tile, tile,
                   pl.BlockSpec((HG_NC, BD, BD), lambda h, b, s: ((b * NB + h) * ns + s, 0, 0))],
        out_shape=[SDS((n, D), F32), SDS((n, D), BF16), SDS((nb * NB * nchunk, BD, BD), F32)],
        scratch_shapes=[pltpu.VMEM((BD, BD), F32)],
        compiler_params=_params(40),
    )(z, z, z, z, lb_logits, hg_g)


def _hgrn_bwd(z, o_all, st_all, dyb, lb_logits, hg_g, nb, s_len):
    n = nb * s_len
    t = HG_T
    ns = s_len // t

    def body(q_ref, f_ref, v_ref, gb_ref, o_ref, st_ref, dyb_ref, lg_ref, g_ref,
             dz_ref, glg_ref, ghg_ref, dst, dlb):
        h, b, s = pl.program_id(0), pl.program_id(1), pl.program_id(2)

        @pl.when((h == 0) & (b == 0) & (s == 0))
        def _():
            ghg_ref[...] = jnp.zeros((1, BD), F32)

        @pl.when((b == 0) & (s == 0))
        def _():
            dlb[...] = jnp.zeros((8, BD), F32)

        @pl.when(s == 0)
        def _():
            dst[...] = jnp.zeros((BD, BD), F32)

        lb = _lower_bound(lg_ref[...])
        g = g_ref[...]
        ck = _hg_tile(q_ref[0], f_ref[0], lb)
        q = ck["q"]
        vb = _chunks(v_ref[0]).astype(BF16)
        gb = _chunks(gb_ref[0])
        o = _chunks(o_ref[...])
        dyb_v = _chunks(dyb_ref[...])
        s_in = st_ref[...]

        sgb = _sigmoid(gb)
        r = lax.rsqrt(jnp.mean(o * o, axis=-1, keepdims=True) + EPS)
        ohat = o * r
        d_on = dyb_v * (gb * sgb)
        d_gb = dyb_v * (ohat * g) * (sgb * (1.0 + gb * (1.0 - sgb)))
        ghg_ref[...] += jnp.sum(jnp.sum(d_on * ohat, axis=1), axis=0, keepdims=True)
        tt = d_on * g
        d_o = r * (tt - ohat * jnp.mean(tt * ohat, axis=-1, keepdims=True))
        dob = d_o.astype(BF16)

        attb = ck["att"].astype(BF16)
        q_inb, k_inb = ck["q_in"].astype(BF16), ck["k_in"].astype(BF16)
        q_stb, k_stb = ck["q_st"].astype(BF16), ck["k_st"].astype(BF16)
        d_att = jnp.where(_tri(False), _bdot(dob, vb, BNT_DIMS), 0.0).astype(BF16)
        d_q_in = _bdot(d_att, k_inb, BNN_DIMS)
        d_k_in = _bdot(d_att, q_inb, BTN_DIMS)
        d_q_st = _bdot(dob, s_in.astype(BF16), BNN_DIMS)
        qdo = _bdot(dob, q_stb, BTN_DIMS)
        d_states = [None] * HG_NC + [dst[...]]
        for c in reversed(range(HG_NC)):
            d_states[c] = d_states[c + 1] * ck["dc"][c] + qdo[c]
        dst[...] = d_states[0]
        ds_out = jnp.stack(d_states[1:], axis=0)
        dsb = ds_out.astype(BF16)
        d_v = _bdot(attb, dob, BTN_DIMS) + _bdot(k_stb, dsb, BNT_DIMS)
        d_k_st = _bdot(vb, dsb, BNN_DIMS)
        d_dc = jnp.sum(ds_out * s_in, axis=1, keepdims=True)

        p_qi = d_q_in * ck["q_in"]
        p_ki = d_k_in * ck["k_in"]
        p_qs = d_q_st * ck["q_st"]
        p_ks = d_k_st * ck["k_st"]
        d_qh = (d_q_in * ck["e_qi"] + d_q_st * ck["e_qs"]) * HG_SCALE
        d_k = d_k_in * ck["e_ki"] + d_k_st * ck["e_ks"]
        d_b = (p_qi - p_ki) + (p_qs - p_ks)
        d_b_mid = jnp.sum(p_ki - p_qi, axis=1, keepdims=True)
        d_b_last = jnp.sum(p_ks, axis=1, keepdims=True) + d_dc * ck["dc"]
        rowi = lax.broadcasted_iota(jnp.int32, (HG_NC, CHUNK, BD), 1)
        d_b = d_b + jnp.where(rowi == CHUNK // 2, d_b_mid, 0.0) + jnp.where(rowi == CHUNK - 1, d_b_last, 0.0)
        d_logf = _bdot(_tri(True).astype(F32), d_b, BNN_DIMS, lax.Precision.HIGHEST)
        d_f = d_logf / ck["f"] - d_k
        sig, sq = ck["sig"], ck["sq"]
        d_fp = d_f * (1.0 - lb) * (sig * (1.0 - sig))
        dlb[0:1, :] += jnp.sum(jnp.sum(d_f * (1.0 - sig), axis=1), axis=0, keepdims=True)
        d_q = d_qh * (sq * (1.0 + q * (1.0 - sq)))
        dz_ref[0] = d_q.astype(BF16).reshape(t, BD)
        dz_ref[1] = d_fp.astype(BF16).reshape(t, BD)
        dz_ref[2] = d_v.astype(BF16).reshape(t, BD)
        dz_ref[3] = d_gb.astype(BF16).reshape(t, BD)

        @pl.when((b == nb - 1) & (s == ns - 1))
        def _():
            dl = dlb[0:1, :] * (lb * (1.0 - lb))
            glg_ref[0:1, :] = dl
            glg_ref[1:2, :] = -dl

    rb = lambda b, s: b * ns + (ns - 1 - s)
    seg = lambda j: pl.BlockSpec((1, t, BD), lambda h, b, s: (j, rb(b, s), h))
    tile = pl.BlockSpec((t, BD), lambda h, b, s: (rb(b, s), h))
    return pl.pallas_call(
        body, name="hgrn_bwd", grid=(NB, nb, ns),
        in_specs=[seg(2), seg(3), seg(4), seg(5), tile,
                  pl.BlockSpec((HG_NC, BD, BD), lambda h, b, s: ((b * NB + h) * ns + (ns - 1 - s), 0, 0)),
                  tile,
                  pl.BlockSpec((2, BD), lambda h, b, s: (0, h)),
                  pl.BlockSpec((1, BD), lambda h, b, s: (0, 0))],
        out_specs=[pl.BlockSpec((4, t, BD), lambda h, b, s: (0, rb(b, s), h)),
                   pl.BlockSpec((2, BD), lambda h, b, s: (0, h)),
                   pl.BlockSpec((1, BD), lambda h, b, s: (0, 0))],
        out_shape=[SDS((4, n, D), BF16), SDS((2, D), F32), SDS((1, BD), F32)],
        scratch_shapes=[pltpu.VMEM((BD, BD), F32), pltpu.VMEM((8, BD), F32)],
        compiler_params=_params(48),
    )(z, z, z, z, o_all, st_all, dyb, lb_logits, hg_g)


def _mid(ya, yb, z, b_merge, x2, tgt, fin_g, pa, pb, wo):
    n = x2.shape[0]
    tm = 256
    ni = n // tm

    def body(ya_ref, yb_ref, gma_ref, gmb_ref, bm_ref, x_ref, t_ref, fg_ref, pa_hbm, pb_hbm, wo_hbm,
             dx2_ref, dya_ref, dyb_ref, dgm_ref, loss_ref, gfg_ref, gbm_ref, gpa_hbm, gpb_hbm, gwo_hbm,
             pa_v, pb_v, wo_v, gpa_v, gpb_v, gwo_v, sem):
        i = pl.program_id(0)
        loads = [pltpu.make_async_copy(src, dst, sem.at[k])
                 for k, (src, dst) in enumerate(((pa_hbm, pa_v), (pb_hbm, pb_v), (wo_hbm, wo_v)))]
        stores = [pltpu.make_async_copy(src, dst, sem.at[k])
                  for k, (src, dst) in enumerate(((gpa_v, gpa_hbm), (gpb_v, gpb_hbm), (gwo_v, gwo_hbm)))]

        @pl.when(i == 0)
        def _():
            for cp in loads:
                cp.start()
            for ref in (gpa_v, gpb_v, gwo_v, loss_ref, gfg_ref, gbm_ref):
                ref[...] = jnp.zeros(ref.shape, F32)
            for cp in loads:
                cp.wait()

        ya_v = ya_ref[...]
        yb_v = yb_ref[...]
        out_a = jnp.dot(ya_v, pa_v[...], preferred_element_type=F32)
        out_b = jnp.dot(yb_v, pb_v[...], preferred_element_type=F32)
        bm = bm_ref[...]
        g_a = _sigmoid(gma_ref[0] + bm[:, 0:D])
        g_b = _sigmoid(gmb_ref[0] + bm[:, D:2 * D])
        mixed = g_a * out_a + g_b * out_b
        mixb = mixed.astype(BF16)
        xo = x_ref[...] + jnp.dot(mixb, wo_v[...], preferred_element_type=F32)
        r = lax.rsqrt(jnp.mean(xo * xo, axis=-1, keepdims=True) + EPS)
        xn = xo * r
        fg = fg_ref[...]
        e = xn * fg - t_ref[...]
        loss_ref[...] += 0.5 * jnp.sum(jnp.mean(e * e, axis=-1, keepdims=True))
        dy = e * (1.0 / D)
        gfg_ref[...] += jnp.sum(dy * xn, axis=0, keepdims=True)
        dxn = dy * fg
        dx2 = r * (dxn - xn * jnp.mean(dxn * xn, axis=-1, keepdims=True))
        dx2_ref[...] = dx2
        dx2b = dx2.astype(BF16)
        d_mixed = lax.dot_general(dx2b, wo_v[...], NT_DIMS, preferred_element_type=F32)
        gwo_v[...] += lax.dot_general(mixb, dx2b, TN_DIMS, preferred_element_type=F32)
        d_oa = (d_mixed * g_a).astype(BF16)
        d_ob = (d_mixed * g_b).astype(BF16)
        dgm_a = (d_mixed * out_a) * (g_a * (1.0 - g_a))
        dgm_b = (d_mixed * out_b) * (g_b * (1.0 - g_b))
        gbm_ref[:, 0:D] += jnp.sum(dgm_a, axis=0, keepdims=True)
        gbm_ref[:, D:2 * D] += jnp.sum(dgm_b, axis=0, keepdims=True)
        dgm_ref[0] = dgm_a.astype(BF16)
        dgm_ref[1] = dgm_b.astype(BF16)
        dya_ref[...] = lax.dot_general(d_oa, pa_v[...], NT_DIMS, preferred_element_type=F32)
        dyb_ref[...] = lax.dot_general(d_ob, pb_v[...], NT_DIMS, preferred_element_type=F32)
        gpa_v[...] += lax.dot_general(ya_v, d_oa, TN_DIMS, preferred_element_type=F32)
        gpb_v[...] += lax.dot_general(yb_v, d_ob, TN_DIMS, preferred_element_type=F32)

        @pl.when(i == ni - 1)
        def _():
            for cp in stores:
                cp.start()
            for cp in stores:
                cp.wait()

    rows = pl.BlockSpec((tm, D), lambda i: (i, 0))
    rep = lambda shape: pl.BlockSpec(shape, lambda i: (0,) * len(shape))
    return pl.pallas_call(
        body, name="mid", grid=(ni,),
        in_specs=[rows, rows,
                  pl.BlockSpec((1, tm, D), lambda i: (6, i, 0)), pl.BlockSpec((1, tm, D), lambda i: (7, i, 0)),
                  rep((1, 2 * D)), rows, rows, rep((1, D)), ANY, ANY, ANY],
        out_specs=[rows, rows, rows, pl.BlockSpec((2, tm, D), lambda i: (0, i, 0)),
                   rep((8, BD)), rep((1, D)), rep((1, 2 * D)), ANY, ANY, ANY],
        out_shape=[SDS((n, D), F32), SDS((n, D), F32), SDS((n, D), F32), SDS((2, n, D), BF16),
                   SDS((8, BD), F32), SDS((1, D), F32), SDS((1, 2 * D), F32),
                   SDS((D, D), F32), SDS((D, D), F32), SDS((D, D), F32)],
        scratch_shapes=[pltpu.VMEM((D, D), BF16)] * 3 + [pltpu.VMEM((D, D), F32)] * 3 + [pltpu.SemaphoreType.DMA((3,))],
        compiler_params=_params(60),
    )(ya, yb, z, z, b_merge, x2, tgt, fin_g, pa, pb, wo)


def _dz_specs(tm, ni, row_major):
    if row_major:
        ia = lambda i, j: (jnp.minimum(j, 1), i, 0)
        ib = lambda i, j: (jnp.clip(j - 2, 0, 3), i, 0)
        im = lambda i, j: (jnp.clip(j - 6, 0, 1), i, 0)
    else:
        last = ni - 1
        ia = lambda j, i: (jnp.minimum(j, 1), jnp.where(j < 2, i, last), 0)
        ib = lambda j, i: (jnp.clip(j - 2, 0, 3), jnp.where(j < 2, 0, jnp.where(j < 6, i, last)), 0)
        im = lambda j, i: (jnp.clip(j - 6, 0, 1), jnp.where(j < 6, 0, i), 0)
    return [pl.BlockSpec((1, tm, D), f) for f in (ia, ib, im)]


def _inproj_bwd_x(dza, dzb, dzm, w_all, x2, dx2, norm_g, after):
    n = x2.shape[0]
    tm = 512
    ni = n // tm

    def body(dza_ref, dzb_ref, dzm_ref, w_ref, x_ref, dx2_ref, g_ref, after_ref, gx_ref, gg_ref, acc):
        i, j = pl.program_id(0), pl.program_id(1)

        @pl.when((i == 0) & (j == 0))
        def _():
            gg_ref[...] = jnp.zeros((1, D), F32)

        @pl.when(j == 0)
        def _():
            acc[...] = jnp.zeros((tm, D), F32)

        def add(ref):
            acc[...] += lax.dot_general(ref[0], w_ref[0], NT_DIMS, preferred_element_type=F32)

        pl.when(j < 2)(lambda: add(dza_ref))
        pl.when((j >= 2) & (j < 6))(lambda: add(dzb_ref))
        pl.when(j >= 6)(lambda: add(dzm_ref))

        @pl.when(j == NB - 1)
        def _():
            x = x_ref[...]
            r = lax.rsqrt(jnp.mean(x * x, axis=-1, keepdims=True) + EPS)
            xn = x * r
            dh = acc[...]
            gg_ref[...] += jnp.sum(dh * xn, axis=0, keepdims=True)
            dxn = dh * g_ref[...]
            gx_ref[...] = dx2_ref[...] + r * (dxn - xn * jnp.mean(dxn * xn, axis=-1, keepdims=True))

    rows = pl.BlockSpec((tm, D), lambda i, j: (i, 0))
    return pl.pallas_call(
        body, name="inproj_bwd_x", grid=(ni, NB),
        in_specs=_dz_specs(tm, ni, True) + [pl.BlockSpec((1, D, D), lambda i, j: (j, 0, 0)), rows, rows,
                                             pl.BlockSpec((1, D), lambda i, j: (0, 0)), ANY],
        out_specs=[rows, pl.BlockSpec((1, D), lambda i, j: (0, 0))],
        out_shape=[SDS((n, D), F32), SDS((1, D), F32)],
        scratch_shapes=[pltpu.VMEM((tm, D), F32)],
        compiler_params=_params(48),
    )(dza, dzb, dzm, w_all, x2, dx2, norm_g, after)


def _inproj_bwd_w(dza, dzb, dzm, h_all):
    n = h_all.shape[0]
    tm = 512
    ni = n // tm

    def body(dza_ref, dzb_ref, dzm_ref, h_ref, gw_ref):
        j, i = pl.program_id(0), pl.program_id(1)

        @pl.when(i == 0)
        def _():
            gw_ref[...] = jnp.zeros((1, D, D), F32)

        def add(ref):
            gw_ref[0] += lax.dot_general(h_ref[...], ref[0], TN_DIMS, preferred_element_type=F32)

        pl.when(j < 2)(lambda: add(dza_ref))
        pl.when((j >= 2) & (j < 6))(lambda: add(dzb_ref))
        pl.when(j >= 6)(lambda: add(dzm_ref))

    return pl.pallas_call(
        body, name="inproj_bwd_w", grid=(NB, ni),
        in_specs=_dz_specs(tm, ni, False) + [pl.BlockSpec((tm, D), lambda j, i: (i, 0))],
        out_specs=pl.BlockSpec((1, D, D), lambda j, i: (j, 0, 0)),
        out_shape=SDS((NB, D, D), F32),
        compiler_params=_params(48),
    )(dza, dzb, dzm, h_all)


def _adamw(w, g, m, v):
    rows, cols = w.shape
    tr = _row_tile(rows)

    def body(w_ref, g_ref, m_ref, v_ref, d_ref, nm_ref, nv_ref):
        gv = g_ref[...]
        nm = ADAM_B1 * m_ref[...] + (1.0 - ADAM_B1) * gv
        nv = ADAM_B2 * v_ref[...] + (1.0 - ADAM_B2) * (gv * gv)
        m_hat = nm / (1.0 - ADAM_B1 ** ADAM_STEP)
        v_hat = nv / (1.0 - ADAM_B2 ** ADAM_STEP)
        d_ref[...] = -ADAM_LR * (m_hat / (jnp.sqrt(v_hat) + ADAM_EPS) + ADAM_WD * w_ref[...])
        nm_ref[...] = nm
        nv_ref[...] = nv

    spec = pl.BlockSpec((tr, cols), lambda i: (i, 0))
    return pl.pallas_call(
        body, name="adamw", grid=(rows // tr,), in_specs=[spec] * 4, out_specs=[spec] * 3,
        out_shape=[SDS((rows, cols), F32)] * 3, compiler_params=_params(32),
    )(w, g, m, v)


def _allgather(blocks, dtypes, name):
    na = len(blocks)

    def body(*refs):
        ins, outs, stages = refs[:na], refs[na:2 * na], refs[2 * na:3 * na]
        send_sems, recv_sems, local_sems = refs[3 * na:]
        x, y, c = _place()
        me, sibling = (x, y, c), (x, y, 1 - c)
        chips = [(1 - x, y), (x, 1 - y), (1 - x, 1 - y)]
        blk = lambda p: 4 * p[0] + 2 * p[1] + p[2]

        def copy(a, k, block, to, src=None):
            return pltpu.make_async_remote_copy(
                src_ref=outs[a].at[blk(block)] if src is None else src, dst_ref=outs[a].at[blk(block)],
                send_sem=send_sems.at[7 * a + k], recv_sem=recv_sems.at[7 * a + k],
                device_id=to, device_id_type=MESH)

        mine, first, passed = [], [], []
        for a in range(na):
            stages[a][...] = ins[a][...].astype(dtypes[a])
            mine.append(pltpu.make_async_copy(stages[a], outs[a].at[blk(me)], local_sems.at[a]))
            mine[-1].start()
            first.append(copy(a, 0, me, sibling, src=stages[a]))
            first += [copy(a, 1 + j, me, (*chip, c), src=stages[a]) for j, chip in enumerate(chips)]
        for cp in first:
            cp.start()
        for j, chip in enumerate(chips):
            for a in range(na):
                copy(a, 1 + j, (*chip, c), me).wait_recv()
                passed.append(copy(a, 4 + j, (*chip, c), sibling))
                passed[-1].start()
        for a in range(na):
            copy(a, 0, sibling, me).wait_recv()
            for j, chip in enumerate(chips):
                copy(a, 4 + j, (*chip, 1 - c), me).wait_recv()
        for cp in first + passed:
            cp.wait_send()
        for cp in mine:
            cp.wait()

    return pl.pallas_call(
        body, name=name,
        in_specs=[pl.BlockSpec(memory_space=pltpu.VMEM)] * na, out_specs=[ANY] * na,
        out_shape=[SDS((NB,) + b.shape, dt) for b, dt in zip(blocks, dtypes)],
        scratch_shapes=[pltpu.VMEM(b.shape, dt) for b, dt in zip(blocks, dtypes)]
        + [pltpu.SemaphoreType.DMA((7 * na,)), pltpu.SemaphoreType.DMA((7 * na,)), pltpu.SemaphoreType.DMA((na,))],
        compiler_params=_params(40),
    )(*blocks)


def _rs_sibling(gs):
    na = len(gs)

    def body(*refs):
        ins, outs = refs[:na], refs[na:2 * na]
        send_sems, recv_sems = refs[2 * na:]
        x, y, c = _place()
        copies = []
        for a in range(na):
            for q in range(4):
                copies.append(pltpu.make_async_remote_copy(
                    src_ref=ins[a].at[2 * q + (1 - c)], dst_ref=outs[a].at[q],
                    send_sem=send_sems.at[4 * a + q], recv_sem=recv_sems.at[4 * a + q],
                    device_id=(x, y, 1 - c), device_id_type=MESH))
        for cp in copies:
            cp.start()
        for cp in copies:
            cp.wait_recv()
        for cp in copies:
            cp.wait_send()

    return pl.pallas_call(
        body, name="rs_sibling", in_specs=[ANY] * na, out_specs=[ANY] * na,
        out_shape=[SDS((4,) + g.shape[1:], F32) for g in gs],
        scratch_shapes=[pltpu.SemaphoreType.DMA((4 * na,)), pltpu.SemaphoreType.DMA((4 * na,))],
    )(*gs)


HBM = pl.BlockSpec(memory_space=pltpu.HBM)
SEMS = pl.BlockSpec(memory_space=pltpu.SEMAPHORE)
EFFECT = pltpu.SideEffectType.DATAFLOW_SIDE_EFFECTING


def _chip_copies(srcs, lands, send_sems, recv_sems):
    x, y, c = _place()
    return [pltpu.make_async_remote_copy(
        src_ref=srcs[a].at[slot], dst_ref=lands[a].at[slot],
        send_sem=send_sems.at[3 * a + slot], recv_sem=recv_sems.at[3 * a + slot],
        device_id=(px, py, c), device_id_type=MESH)
        for a in range(len(srcs)) for slot, (px, py) in enumerate(_other_chips(x, y))]


def _rs_chips_start(ps):
    na = len(ps)

    def body(*refs):
        srcs, lands = refs[:na], refs[na:2 * na]
        send_sems, recv_sems = refs[2 * na], refs[2 * na + 1]
        token = refs[-1]
        for cp in _chip_copies(srcs, lands, send_sems, recv_sems):
            cp.start()
        token[...] = jnp.zeros_like(token)

    hbm = lambda a: pltpu.HBM(a.shape, a.dtype)
    out = pl.pallas_call(
        body, name="rs_chips_start",
        out_shape=(pltpu.SemaphoreType.DMA((3 * na,)), pltpu.SemaphoreType.DMA((3 * na,)),
                   *[hbm(p) for p in ps], *[hbm(p) for p in ps], SDS((8, BD), F32)),
        in_specs=[HBM] * (2 * na),
        out_specs=(SEMS, SEMS, *[HBM] * (2 * na), pl.BlockSpec(memory_space=pltpu.VMEM)),
        input_output_aliases={i: 2 + i for i in range(2 * na)},
        compiler_params=pltpu.CompilerParams(has_side_effects=EFFECT),
    )(*[pltpu.with_memory_space_constraint(p, pltpu.HBM) for p in ps],
      *[pltpu.with_memory_space_constraint(lax.empty(p.shape, p.dtype), pltpu.HBM) for p in ps])
    return out[0], out[1], out[2:2 + na], out[2 + na:2 + 2 * na], out[-1]


def _rs_chips_wait(send_sems, recv_sems, srcs, lands, after):
    na = len(srcs)

    def body(*refs):
        srcs_r, lands_r = refs[:na], refs[na:2 * na]
        send_r, recv_r = refs[2 * na], refs[2 * na + 1]
        copies = _chip_copies(srcs_r, lands_r, send_r, recv_r)
        for cp in copies:
            cp.wait_send()
        for cp in copies:
            cp.wait_recv()

    hbm = lambda a: pltpu.HBM(a.shape, a.dtype)
    out = pl.pallas_call(
        body, name="rs_chips_wait",
        out_shape=(*[hbm(p) for p in srcs], *[hbm(p) for p in lands]),
        in_specs=[HBM] * (2 * na) + [SEMS, SEMS, ANY],
        out_specs=tuple([HBM] * (2 * na)),
        input_output_aliases={i: i for i in range(2 * na)},
        compiler_params=pltpu.CompilerParams(has_side_effects=EFFECT),
    )(*srcs, *lands, send_sems, recv_sems, after)
    return out[na:]


def _add_sibling(place, g, a_in):
    _, r, cols = g.shape
    tr = _row_tile(r)

    def chip(k, pr):
        qx = pr[0] if k in (1, 3) else 1 - pr[0]
        qy = pr[1] if k in (0, 3) else 1 - pr[1]
        return 2 * qx + qy

    def body(place_ref, *refs):
        g_refs, a_refs, (out_ref, own_ref) = refs[0:4], refs[4:8], refs[8:10]
        for k in range(3):
            out_ref[k] = (g_refs[k][0] + a_refs[k][0]).astype(BF16)
        own_ref[...] = g_refs[3][0] + a_refs[3][0]

    mine = lambda k: pl.BlockSpec((1, tr, cols), lambda i, pr: (2 * chip(k, pr) + pr[2], i, 0))
    theirs = lambda k: pl.BlockSpec((1, tr, cols), lambda i, pr: (chip(k, pr), i, 0))
    return pl.pallas_call(
        body, name="add_sibling",
        grid_spec=pltpu.PrefetchScalarGridSpec(
            num_scalar_prefetch=1, grid=(r // tr,),
            in_specs=[mine(k) for k in range(4)] + [theirs(k) for k in range(4)],
            out_specs=[pl.BlockSpec((3, tr, cols), lambda i, pr: (0, i, 0)),
                       pl.BlockSpec((tr, cols), lambda i, pr: (i, 0))]),
        out_shape=[SDS((3, r, cols), BF16), SDS((r, cols), F32)], compiler_params=_params(48),
    )(place, *[g] * 4, *[a_in] * 4)


def _add_chips(own, b_in):
    r, cols = own.shape
    tr = _row_tile(r)

    def body(p_ref, b0_ref, b1_ref, b2_ref, o_ref):
        o_ref[...] = ((p_ref[...] + b0_ref[0].astype(F32)) + b1_ref[0].astype(F32)) + b2_ref[0].astype(F32)

    slot = lambda k: pl.BlockSpec((1, tr, cols), lambda i: (k, i, 0))
    spec = pl.BlockSpec((tr, cols), lambda i: (i, 0))
    return pl.pallas_call(
        body, name="add_chips", grid=(r // tr,), in_specs=[spec, slot(0), slot(1), slot(2)], out_specs=spec,
        out_shape=SDS((r, cols), F32), compiler_params=_params(32),
    )(own, b_in, b_in, b_in)


VEC_NAMES = ("b_merge", "conv_b", "rg_bx", "rg_ba", "rg_lambda", "hg_lb_logits", "hg_norm_g", "final_norm_g")
REP_NAMES = ("rg_wx", "rg_wa", "norm_g") + VEC_NAMES
SMALL_AT = 3 * BD
SMALL_ROWS = 48
MID_ROWS = 448


def _sum_blocks(parts):
    def body(p_ref, o_ref):
        acc = p_ref[0, 0:1, :]
        for k in range(1, NB):
            acc = acc + p_ref[k, 0:1, :]
        o_ref[...] = acc

    return pl.pallas_call(body, name="sum_blocks", out_shape=SDS((1, parts.shape[2]), F32))(parts)


def _pack_rows(arrays, width, row_multiple=8):
    flat = jnp.concatenate([a.reshape(-1) for a in arrays])
    rows = -(-flat.shape[0] // width)
    rows = -(-rows // row_multiple) * row_multiple
    return jnp.pad(flat, (0, rows * width - flat.shape[0])).reshape(rows, width)


def _unpack(flat, like):
    out, off = [], 0
    for a in like:
        out.append(flat[off:off + a.size].reshape(a.shape))
        off += a.size
    return out


def kernel(x, w_in, b_merge, conv_w, conv_b, rg_wx, rg_bx, rg_wa, rg_ba, rg_lambda, hg_lb_logits, hg_norm_g, proj_a, proj_b, w_out, norm_g, final_norm_g, loss_target, m_w_in, m_b_merge, m_conv_w, m_conv_b, m_rg_wx, m_rg_bx, m_rg_wa, m_rg_ba, m_rg_lambda, m_hg_lb_logits, m_hg_norm_g, m_proj_a, m_proj_b, m_w_out, m_norm_g, m_final_norm_g, v_w_in, v_b_merge, v_conv_w, v_conv_b, v_rg_wx, v_rg_bx, v_rg_wa, v_rg_ba, v_rg_lambda, v_hg_lb_logits, v_hg_norm_g, v_proj_a, v_proj_b, v_w_out, v_norm_g, v_final_norm_g):
    weights = dict(w_in=w_in, b_merge=b_merge, conv_w=conv_w, conv_b=conv_b, rg_wx=rg_wx, rg_bx=rg_bx, rg_wa=rg_wa,
                   rg_ba=rg_ba, rg_lambda=rg_lambda, hg_lb_logits=hg_lb_logits, hg_norm_g=hg_norm_g, proj_a=proj_a,
                   proj_b=proj_b, w_out=w_out, norm_g=norm_g, final_norm_g=final_norm_g)
    mom1 = dict(w_in=m_w_in, b_merge=m_b_merge, conv_w=m_conv_w, conv_b=m_conv_b, rg_wx=m_rg_wx, rg_bx=m_rg_bx,
                rg_wa=m_rg_wa, rg_ba=m_rg_ba, rg_lambda=m_rg_lambda, hg_lb_logits=m_hg_lb_logits,
                hg_norm_g=m_hg_norm_g, proj_a=m_proj_a, proj_b=m_proj_b, w_out=m_w_out, norm_g=m_norm_g,
                final_norm_g=m_final_norm_g)
    mom2 = dict(w_in=v_w_in, b_merge=v_b_merge, conv_w=v_conv_w, conv_b=v_conv_b, rg_wx=v_rg_wx, rg_bx=v_rg_bx,
                rg_wa=v_rg_wa, rg_ba=v_rg_ba, rg_lambda=v_rg_lambda, hg_lb_logits=v_hg_lb_logits,
                hg_norm_g=v_hg_norm_g, proj_a=v_proj_a, proj_b=v_proj_b, w_out=v_w_out, norm_g=v_norm_g,
                final_norm_g=v_final_norm_g)
    order = list(weights)
    nb, s_len, _ = x.shape
    n = nb * s_len
    px, py, pc = _place()
    place = jnp.stack([px, py, pc]).astype(jnp.int32)

    x2 = x.reshape(n, D)
    cw_blk = jnp.pad(conv_w[0], ((0, 4), (0, 0)))
    order_ids = jnp.stack([_block_id(p) for p in _arrival_order(px, py, pc)]).astype(jnp.int32)
    z, h_all, w_all, pa_all, pb_all, wo_all, cw_all = _gather_inproj(
        order_ids, x2, norm_g, [w_in[0], proj_a[0], proj_b[0], w_out[0], cw_blk], [BF16, BF16, BF16, BF16, F32])
    pa_full, pb_full, wo_full = (a.reshape(D, D) for a in (pa_all, pb_all, wo_all))
    cw8 = cw_all.transpose(1, 0, 2).reshape(8, D)
    wx_b, wa_b = rg_wx[0].astype(BF16), rg_wa[0].astype(BF16)
    cb, bx, ba = conv_b, rg_bx.reshape(1, D), rg_ba.reshape(1, D)
    fin_g = final_norm_g.reshape(1, D)

    hlru, ya = _lru_fwd(z, cw8, cb, wx_b, wa_b, bx, ba, rg_lambda, nb, s_len)
    o_all, yb, st_all = _hgrn_fwd(z, hg_lb_logits, hg_norm_g, nb, s_len)

    (dx2, dya, dyb, dzm, loss_acc, g_fin, g_bm, g_pa, g_pb, g_wo) = _mid(
        ya, yb, z, b_merge, x2, loss_target.reshape(n, D), fin_g, pa_full, pb_full, wo_full)
    dzb, g_lg, g_hg = _hgrn_bwd(z, o_all, st_all, dyb, hg_lb_logits, hg_norm_g, nb, s_len)
    dza, g_cw8, g_cb, g_wx, g_wa, g_bx, g_ba, g_lam = _lru_bwd(
        z, hlru, dya, cw8, cb, wx_b, wa_b, bx, ba, rg_lambda, nb, s_len)
    g_w = _inproj_bwd_w(dza, dzb, dzm, h_all)

    part = dict(b_merge=g_bm, conv_b=g_cb, rg_bx=g_bx, rg_ba=g_ba, rg_lambda=g_lam, hg_lb_logits=g_lg,
                hg_norm_g=g_hg, final_norm_g=g_fin)
    vec = _pack_rows([part[k] for k in VEC_NAMES], BD)
    vec = jnp.pad(vec, ((0, 16 * NB - vec.shape[0]), (0, 0))).reshape(NB, 2, D)
    rows8 = lambda a: jnp.pad(a, ((0, 0), (0, 8 - a.shape[1]), (0, 0)))
    g_m = jnp.concatenate([g.reshape(NB, BD, D) for g in (g_pa, g_pb, g_wo)]
                          + [g_wx.reshape(NB, 16, D), g_wa.reshape(NB, 16, D),
                             rows8(g_cw8.reshape(8, NB, BD).transpose(1, 0, 2).reshape(NB, 1, D)), rows8(vec),
                             jnp.zeros((NB, MID_ROWS - SMALL_AT - SMALL_ROWS, D), F32)], axis=1)
    from_sibling = _rs_sibling([g_w, g_m])
    w_out_bf, w_own = _add_sibling(place, g_w, from_sibling[0])
    m_out_bf, m_own = _add_sibling(place, g_m, from_sibling[1])
    send_sems, recv_sems, srcs, lands, token = _rs_chips_start([w_out_bf, m_out_bf])
    grad_x, g_ng = _inproj_bwd_x(dza, dzb, dzm, w_all, x2, dx2, norm_g, token)
    from_chips = _rs_chips_wait(send_sems, recv_sems, srcs, lands, grad_x)
    r_w = _add_chips(w_own, from_chips[0])
    r_m = _add_chips(m_own, from_chips[1])
    tail = jnp.concatenate([r_m[SMALL_AT:SMALL_AT + SMALL_ROWS], jnp.pad(g_ng, ((0, 7), (0, 0)))], axis=0)
    (tail_all,) = _allgather([tail], [F32], "gather_small_grads")

    grads = dict(w_in=r_w.reshape(1, D, D),
                 proj_a=r_m[0:BD].reshape(1, BD, D), proj_b=r_m[BD:2 * BD].reshape(1, BD, D),
                 w_out=r_m[2 * BD:3 * BD].reshape(1, BD, D),
                 conv_w=r_m[SMALL_AT + 32].reshape(8, BD)[0:4].reshape(1, 4, BD),
                 rg_wx=tail_all[:, 0:16].reshape(1, NB, BD, BD), rg_wa=tail_all[:, 16:32].reshape(1, NB, BD, BD),
                 norm_g=_sum_blocks(tail_all[:, SMALL_ROWS:SMALL_ROWS + 8]))
    vec_all = tail_all[:, 40:42].reshape(-1)
    for k, gk in zip(VEC_NAMES, _unpack(vec_all, [weights[k] for k in VEC_NAMES])):
        grads[k] = gk

    delta, new_m, new_v = {}, {}, {}
    for k in ("w_in", "proj_a", "proj_b", "w_out"):
        shp = weights[k].shape
        two = lambda a: a.reshape(shp[1], shp[2])
        d_k, m_k, v_k = _adamw(two(weights[k]), two(grads[k]), two(mom1[k]), two(mom2[k]))
        delta[k], new_m[k], new_v[k] = d_k.reshape(shp), m_k.reshape(shp), v_k.reshape(shp)
    rep = list(REP_NAMES) + ["conv_w"]
    packs = [_pack_rows([t[k] for k in rep], BD, 256) for t in (weights, grads, mom1, mom2)]
    outs = _adamw(*packs)
    for tgt, flat in zip((delta, new_m, new_v), outs):
        for k, a in zip(rep, _unpack(flat.reshape(-1), [weights[k] for k in rep])):
            tgt[k] = a

    loss = lax.psum(loss_acc[0, 0], ("x", "y", "c"))
    return (loss, grad_x.reshape(x.shape), *[grads[k] for k in order], *[delta[k] for k in order],
            *[new_m[k] for k in order], *[new_v[k] for k in order])
```

```python
import jax
import jax.numpy as jnp
from jax import lax
from jax.experimental import pallas as pl
from jax.experimental.pallas import tpu as pltpu

F32 = jnp.float32
BF16 = jnp.bfloat16
SDS = jax.ShapeDtypeStruct
MESH = pl.DeviceIdType.MESH
ANY = pl.BlockSpec(memory_space=pl.ANY)

D = 1024
NB = 8
BD = D // NB
CHUNK = 64
EPS = 1e-6
LRU_C = 8.0
HG_SCALE = BD ** -0.5
ADAM_LR, ADAM_B1, ADAM_B2, ADAM_EPS, ADAM_WD, ADAM_STEP = 0.001, 0.9, 0.999, 1e-08, 0.01, 10

NT_DIMS = (((1,), (1,)), ((), ()))
TN_DIMS = (((0,), (0,)), ((), ()))


def _params(vmem_mib):
    return pltpu.CompilerParams(vmem_limit_bytes=vmem_mib << 20)


def _row_tile(rows, most=256):
    assert rows % 8 == 0
    return max(t for t in range(8, min(rows, most) + 1, 8) if rows % t == 0)


def _sigmoid(v):
    return jax.nn.sigmoid(v)


def _softplus_neg(lam):
    t = -lam
    e = jnp.exp(-jnp.abs(t))
    w = 1.0 + e
    d = w - 1.0
    l1p = jnp.where(d == 0.0, e, jnp.log(w) * (e / jnp.where(d == 0.0, 1.0, d)))
    return jnp.maximum(t, 0.0) + l1p


def _place():
    return lax.axis_index("x"), lax.axis_index("y"), lax.axis_index("c")


def _other_chips(x, y):
    return [(1 - x, y), (x, 1 - y), (1 - x, 1 - y)]


def _block_id(p):
    return 4 * p[0] + 2 * p[1] + p[2]


def _arrival_order(x, y, c):
    near, far, diag = _other_chips(x, y)
    return [(x, y, c), (x, y, 1 - c), (*near, c), (*far, c), (*near, 1 - c), (*far, 1 - c), (*diag, c), (*diag, 1 - c)]


def _gather_inproj(order_ids, x2, norm_g, blocks, dtypes):
    na = len(blocks)
    n = x2.shape[0]
    tm = min(n, 1024)
    ni = n // tm

    def body(order_ref, x_ref, g_ref, *refs):
        ins, (z_ref, h_ref), outs = refs[:na], refs[na:na + 2], refs[na + 2:2 * na + 2]
        stages = refs[2 * na + 2:3 * na + 2]
        h_full, wbuf, send_sems, recv_sems, local_sems, wsem, hsem = refs[3 * na + 2:]
        j, i = pl.program_id(0), pl.program_id(1)
        x, y, c = _place()
        me, sibling = (x, y, c), (x, y, 1 - c)
        chips = _other_chips(x, y)
        small = range(1, na)

        def copy(a, k, block, to, src=None):
            return pltpu.make_async_remote_copy(
                src_ref=outs[a].at[_block_id(block)] if src is None else src, dst_ref=outs[a].at[_block_id(block)],
                send_sem=send_sems.at[7 * a + k], recv_sem=recv_sems.at[7 * a + k],
                device_id=to, device_id_type=MESH)

        def local(a):
            return pltpu.make_async_copy(stages[a], outs[a].at[_block_id(me)], local_sems.at[a])

        def landed(a, slot):
            copy(a, 1 + slot, (*chips[slot], c), me).wait_recv()
            copy(a, 4 + slot, (*chips[slot], c), sibling).start()

        def passed_on(a, slot):
            copy(a, 4 + slot, (*chips[slot], 1 - c), me).wait_recv()

        @pl.when((j == 0) & (i == 0))
        def _():
            for a in range(na):
                stages[a][...] = ins[a][...].astype(dtypes[a])
                local(a).start()
            for a in range(na):
                copy(a, 0, me, sibling, src=stages[a]).start()
                for slot, chip in enumerate(chips):
                    copy(a, 1 + slot, me, (*chip, c), src=stages[a]).start()

        @pl.when(j == 0)
        def _():
            xv = x_ref[...]
            r = lax.rsqrt(jnp.mean(xv * xv, axis=-1, keepdims=True) + EPS)
            hb = ((xv * r) * g_ref[...]).astype(BF16)
            h_full[pl.ds(pl.multiple_of(i * tm, tm), tm), :] = hb

        save_h = pltpu.make_async_copy(h_full, h_ref, hsem)
        pl.when((j == 0) & (i == ni - 1))(save_h.start)

        steps = [
            lambda: local(0).wait(),
            lambda: copy(0, 0, sibling, me).wait_recv(),
            lambda: landed(0, 0),
            lambda: landed(0, 1),
            lambda: passed_on(0, 0),
            lambda: passed_on(0, 1),
            lambda: landed(0, 2),
            lambda: passed_on(0, 2),
        ]
        for k, step in enumerate(steps):
            pl.when((i == 0) & (j == k))(step)

        @pl.when(i == 0)
        def _():
            load = pltpu.make_async_copy(outs[0].at[order_ref[j]], wbuf, wsem)
            load.start()
            load.wait()

        z_ref[0] = jnp.dot(h_full[pl.ds(pl.multiple_of(i * tm, tm), tm), :], wbuf[...], preferred_element_type=F32)

        @pl.when((j == NB - 1) & (i == ni - 1))
        def _():
            save_h.wait()
            for slot in range(3):
                for a in small:
                    landed(a, slot)
            for a in small:
                local(a).wait()
                copy(a, 0, sibling, me).wait_recv()
                for slot in range(3):
                    passed_on(a, slot)
            for a in range(na):
                copy(a, 0, me, sibling, src=stages[a]).wait_send()
                for slot, chip in enumerate(chips):
                    copy(a, 1 + slot, me, (*chip, c), src=stages[a]).wait_send()
                    copy(a, 4 + slot, (*chip, c), sibling).wait_send()

    rows_once = lambda j, i, order: (jnp.where(j == 0, i, ni - 1), 0)
    vmem = pl.BlockSpec(memory_space=pltpu.VMEM)
    return pl.pallas_call(
        body, name="gather_inproj",
        grid_spec=pltpu.PrefetchScalarGridSpec(
            num_scalar_prefetch=1, grid=(NB, ni),
            in_specs=[pl.BlockSpec((tm, D), rows_once), pl.BlockSpec((1, D), lambda j, i, order: (0, 0))] + [vmem] * na,
            out_specs=[pl.BlockSpec((1, tm, D), lambda j, i, order: (order[j], i, 0)), ANY] + [ANY] * na,
            scratch_shapes=[pltpu.VMEM(b.shape, dt) for b, dt in zip(blocks, dtypes)]
            + [pltpu.VMEM((n, D), BF16), pltpu.VMEM((D, D), BF16),
               pltpu.SemaphoreType.DMA((7 * na,)), pltpu.SemaphoreType.DMA((7 * na,)),
               pltpu.SemaphoreType.DMA((na,)), pltpu.SemaphoreType.DMA(()), pltpu.SemaphoreType.DMA(())]),
        out_shape=[SDS((NB, n, D), F32), SDS((n, D), BF16)] + [SDS((NB,) + b.shape, dt) for b, dt in zip(blocks, dtypes)],
        compiler_params=_params(56),
    )(order_ids, x2, norm_g, *blocks)


LRU_T = 256


def _conv(ext, cw, cb):
    t = LRU_T
    acc = ext[5:5 + t, :] * cw[0:1, :] + ext[6:6 + t, :] * cw[1:2, :]
    acc = acc + ext[7:7 + t, :] * cw[2:3, :]
    acc = acc + ext[8:8 + t, :] * cw[3:4, :]
    return cb + acc


def _lru_gates(xa, wx_ref, wa_ref, bx, ba, lam):
    xab = xa.astype(BF16)
    pis, prs = [], []
    for h in range(NB):
        xs = xab[:, h * BD:(h + 1) * BD]
        pis.append(jnp.dot(xs, wx_ref[h], preferred_element_type=F32))
        prs.append(jnp.dot(xs, wa_ref[h], preferred_element_type=F32))
    gi = _sigmoid(jnp.concatenate(pis, axis=1) + bx)
    gr = _sigmoid(jnp.concatenate(prs, axis=1) + ba)
    sp = _softplus_neg(lam)
    log_a = (-LRU_C * gr) * sp
    a = jnp.exp(log_a)
    mult = jnp.sqrt(-jnp.tanh(log_a) * (a * a + 1.0))
    return xab, gi, gr, sp, a, mult


def _lru_fwd(z, cw8, cb, wx, wa, bx, ba, lam, nb, s_len):
    n = nb * s_len
    t = LRU_T
    ns = s_len // t

    def body(xp_ref, ga_ref, cw_ref, cb_ref, wx_ref, wa_ref, bx_ref, ba_ref, lam_ref,
             h_ref, ya_ref, ext, a_s, u_s, carry):
        @pl.when(pl.program_id(1) == 0)
        def _():
            ext[0:8, :] = jnp.zeros((8, D), F32)
            carry[...] = jnp.zeros((8, D), F32)

        ext[8:8 + t, :] = xp_ref[0]
        xa = _conv(ext, cw_ref[...], cb_ref[...])
        ext[0:8, :] = ext[t:t + 8, :]
        _, gi, _, _, a, mult = _lru_gates(xa, wx_ref, wa_ref, bx_ref[...], ba_ref[...], lam_ref[...])
        u = (mult * gi) * xa
        row = lax.broadcasted_iota(jnp.int32, (t, D), 0) & 7
        for sh in (1, 2, 4):
            a_sh = pltpu.roll(a, sh, 0)
            u_sh = pltpu.roll(u, sh, 0)
            m = row >= sh
            u = jnp.where(m, a * u_sh + u, u)
            a = jnp.where(m, a * a_sh, a)
        a_s[...] = a
        u_s[...] = u

        def step(g, c):
            r = pl.multiple_of(g * 8, 8)
            hg = u_s[pl.ds(r, 8), :] + a_s[pl.ds(r, 8), :] * c
            h_ref[pl.ds(r, 8), :] = hg
            return hg[7:8, :]

        c_out = lax.fori_loop(0, t // 8, step, carry[0:1, :], unroll=4)
        carry[0:1, :] = c_out
        ga = ga_ref[0]
        ya_ref[...] = (h_ref[...] * (ga * _sigmoid(ga))).astype(BF16)

    row_map = lambda b, s: (b * ns + s, 0)
    rep2 = lambda b, s: (0, 0)
    rep3 = lambda b, s: (0, 0, 0)
    return pl.pallas_call(
        body, name="lru_fwd", grid=(nb, ns),
        in_specs=[pl.BlockSpec((1, t, D), lambda b, s: (0, b * ns + s, 0)),
                  pl.BlockSpec((1, t, D), lambda b, s: (1, b * ns + s, 0)),
                  pl.BlockSpec((8, D), rep2), pl.BlockSpec((1, D), rep2),
                  pl.BlockSpec((NB, BD, BD), rep3), pl.BlockSpec((NB, BD, BD), rep3),
                  pl.BlockSpec((1, D), rep2), pl.BlockSpec((1, D), rep2), pl.BlockSpec((1, D), rep2)],
        out_specs=[pl.BlockSpec((t, D), row_map), pl.BlockSpec((t, D), row_map)],
        out_shape=[SDS((n, D), F32), SDS((n, D), BF16)],
        scratch_shapes=[pltpu.VMEM((t + 8, D), F32), pltpu.VMEM((t, D), F32), pltpu.VMEM((t, D), F32),
                        pltpu.VMEM((8, D), F32)],
        compiler_params=_params(48),
    )(z, z, cw8, cb, wx, wa, bx, ba, lam)


def _lru_bwd(z, h_all, dya, cw8, cb, wx, wa, bx, ba, lam, nb, s_len):
    n = nb * s_len
    t = LRU_T
    ns = s_len // t
    t8 = t // 8

    def body(xp_ref, xph_ref, ga_ref, h_ref, hh_ref, dya_ref, cw_ref, cb_ref, wx_ref, wa_ref, bx_ref, ba_ref,
             lam_ref, dz_ref, gcw_ref, gcb_ref, gwx_ref, gwa_ref, gbx_ref, gba_ref, glam_ref,
             ext, hext, dext, a_s, u_s, dh_s, carry):
        b, s = pl.program_id(0), pl.program_id(1)
        first_tile = s == ns - 1

        @pl.when((b == 0) & (s == 0))
        def _():
            for ref in (gcw_ref, gcb_ref, gwx_ref, gwa_ref, gbx_ref, gba_ref, glam_ref):
                ref[...] = jnp.zeros(ref.shape, F32)

        @pl.when(s == 0)
        def _():
            dext[t:t + 8, :] = jnp.zeros((8, D), F32)
            carry[...] = jnp.zeros((8, D), F32)

        keep = jnp.where(first_tile, 0.0, 1.0)
        ext[0:8, :] = xph_ref[0] * keep
        ext[8:8 + t, :] = xp_ref[0]
        hext[0:8, :] = hh_ref[...] * keep
        hext[8:8 + t, :] = h_ref[...]
        cw = cw_ref[...]
        lam = lam_ref[...]
        xa = _conv(ext, cw, cb_ref[...])
        xab, gi, gr, sp, a, mult = _lru_gates(xa, wx_ref, wa_ref, bx_ref[...], ba_ref[...], lam)
        h_prev = hext[7:7 + t, :]
        ga = ga_ref[0]
        sg = _sigmoid(ga)
        dya_v = dya_ref[...]
        d_ga = dya_v * h_ref[...] * (sg * (1.0 + ga * (1.0 - sg)))
        g_in = dya_v * (ga * sg)

        rows = lax.broadcasted_iota(jnp.int32, (t, D), 0)
        row = rows & 7
        an = jnp.where(rows == t - 1, 1.0, pltpu.roll(a, t - 1, 0))
        u = g_in
        for sh in (1, 2, 4):
            a_sh = pltpu.roll(an, t - sh, 0)
            u_sh = pltpu.roll(u, t - sh, 0)
            m = row < 8 - sh
            u = jnp.where(m, u + an * u_sh, u)
            an = jnp.where(m, an * a_sh, an)
        a_s[...] = an
        u_s[...] = u

        def step(i, c):
            r = pl.multiple_of((t8 - 1 - i) * 8, 8)
            dg = u_s[pl.ds(r, 8), :] + a_s[pl.ds(r, 8), :] * c
            dh_s[pl.ds(r, 8), :] = dg
            return dg[0:1, :]

        lax.fori_loop(0, t8, step, carry[0:1, :], unroll=4)
        dh = dh_s[...]
        carry[0:1, :] = a[0:1, :] * dh[0:1, :]

        d_a = dh * h_prev
        dux = dh * xa
        d_mult = dux * gi
        d_gi = dux * mult
        d_xa = dh * (mult * gi)
        d_loga = d_a * a - d_mult * ((a * a) / mult)
        d_gr = d_loga * (-LRU_C * sp)
        d_sp = jnp.sum(d_loga * (-LRU_C * gr), axis=0, keepdims=True)
        glam_ref[...] += d_sp * (-_sigmoid(-lam))
        d_pi = d_gi * gi * (1.0 - gi)
        d_pr = d_gr * gr * (1.0 - gr)
        gbx_ref[...] += jnp.sum(d_pi, axis=0, keepdims=True)
        gba_ref[...] += jnp.sum(d_pr, axis=0, keepdims=True)
        dpib = d_pi.astype(BF16)
        dprb = d_pr.astype(BF16)
        back = []
        for h in range(NB):
            cs = slice(h * BD, (h + 1) * BD)
            gwx_ref[h] += lax.dot_general(xab[:, cs], dpib[:, cs], TN_DIMS, preferred_element_type=F32)
            gwa_ref[h] += lax.dot_general(xab[:, cs], dprb[:, cs], TN_DIMS, preferred_element_type=F32)
            back.append(lax.dot_general(dpib[:, cs], wx_ref[h], NT_DIMS, preferred_element_type=F32)
                        + lax.dot_general(dprb[:, cs], wa_ref[h], NT_DIMS, preferred_element_type=F32))
        d_xa = d_xa + jnp.concatenate(back, axis=1)

        dext[0:t, :] = d_xa
        d_xp = dext[3:3 + t, :] * cw[0:1, :] + dext[2:2 + t, :] * cw[1:2, :]
        d_xp = d_xp + dext[1:1 + t, :] * cw[2:3, :]
        d_xp = d_xp + d_xa * cw[3:4, :]
        dext[t:t + 8, :] = d_xa[0:8, :]
        gcb_ref[...] += jnp.sum(d_xa, axis=0, keepdims=True)
        for k in range(4):
            gcw_ref[k:k + 1, :] += jnp.sum(d_xa * ext[5 + k:5 + k + t, :], axis=0, keepdims=True)
        dz_ref[0] = d_xp.astype(BF16)
        dz_ref[1] = d_ga.astype(BF16)

    rb = lambda b, s: b * ns + (ns - 1 - s)
    halo = lambda b, s: jnp.maximum(rb(b, s) * t8 - 1, 0)
    rep2 = lambda b, s: (0, 0)
    rep3 = lambda b, s: (0, 0, 0)
    return pl.pallas_call(
        body, name="lru_bwd", grid=(nb, ns),
        in_specs=[pl.BlockSpec((1, t, D), lambda b, s: (0, rb(b, s), 0)),
                  pl.BlockSpec((1, 8, D), lambda b, s: (0, halo(b, s), 0)),
                  pl.BlockSpec((1, t, D), lambda b, s: (1, rb(b, s), 0)),
                  pl.BlockSpec((t, D), lambda b, s: (rb(b, s), 0)),
                  pl.BlockSpec((8, D), lambda b, s: (halo(b, s), 0)),
                  pl.BlockSpec((t, D), lambda b, s: (rb(b, s), 0)),
                  pl.BlockSpec((8, D), rep2), pl.BlockSpec((1, D), rep2),
                  pl.BlockSpec((NB, BD, BD), rep3), pl.BlockSpec((NB, BD, BD), rep3),
                  pl.BlockSpec((1, D), rep2), pl.BlockSpec((1, D), rep2), pl.BlockSpec((1, D), rep2)],
        out_specs=[pl.BlockSpec((2, t, D), lambda b, s: (0, rb(b, s), 0)),
                   pl.BlockSpec((8, D), rep2), pl.BlockSpec((1, D), rep2),
                   pl.BlockSpec((NB, BD, BD), rep3), pl.BlockSpec((NB, BD, BD), rep3),
                   pl.BlockSpec((1, D), rep2), pl.BlockSpec((1, D), rep2), pl.BlockSpec((1, D), rep2)],
        out_shape=[SDS((2, n, D), BF16), SDS((8, D), F32), SDS((1, D), F32),
                   SDS((NB, BD, BD), F32), SDS((NB, BD, BD), F32),
                   SDS((1, D), F32), SDS((1, D), F32), SDS((1, D), F32)],
        scratch_shapes=[pltpu.VMEM((t + 8, D), F32), pltpu.VMEM((t + 8, D), F32), pltpu.VMEM((t + 8, D), F32),
                        pltpu.VMEM((t, D), F32), pltpu.VMEM((t, D), F32), pltpu.VMEM((t, D), F32),
                        pltpu.VMEM((8, D), F32)],
        compiler_params=_params(56),
    )(z, z, z, h_all, h_all, dya, cw8, cb, wx, wa, bx, ba, lam)


HG_T = 512
HG_NC = HG_T // CHUNK
BNT_DIMS = (((2,), (2,)), ((0,), (0,)))
BNN_DIMS = (((2,), (1,)), ((0,), (0,)))
BTN_DIMS = (((1,), (1,)), ((0,), (0,)))


def _lower_bound(lg):
    m = jnp.max(lg, axis=0, keepdims=True)
    e = jnp.exp(lg - m)
    return e[0:1, :] / jnp.sum(e, axis=0, keepdims=True)


def _tri(upper):
    r = lax.broadcasted_iota(jnp.int32, (HG_NC, CHUNK, CHUNK), 1)
    c = lax.broadcasted_iota(jnp.int32, (HG_NC, CHUNK, CHUNK), 2)
    return (c >= r) if upper else (r >= c)


def _bdot(a, b, dims, precision=None):
    return lax.dot_general(a, b, dims, precision=precision, preferred_element_type=F32)


def _chunks(a):
    return a.reshape(HG_NC, CHUNK, BD)


def _hg_tile(q, fp, lb):
    q, fp = _chunks(q), _chunks(fp)
    sig = _sigmoid(fp)
    f = lb + (1.0 - lb) * sig
    log_f = jnp.log(f)
    k = 1.0 - f
    b = _bdot(_tri(False).astype(F32), log_f, BNN_DIMS, lax.Precision.HIGHEST)
    b_mid = b[:, CHUNK // 2:CHUNK // 2 + 1, :]
    b_last = b[:, CHUNK - 1:CHUNK, :]
    sq = _sigmoid(q)
    qh = q * sq
    e_qi = jnp.exp(b - b_mid)
    e_ki = jnp.exp(b_mid - b)
    e_qs = jnp.exp(b)
    e_ks = jnp.exp(b_last - b)
    dc = jnp.exp(b_last)
    q_in = (qh * e_qi) * HG_SCALE
    k_in = k * e_ki
    q_st = (qh * e_qs) * HG_SCALE
    k_st = k * e_ks
    att = _bdot(q_in.astype(BF16), k_in.astype(BF16), BNT_DIMS)
    att = jnp.where(_tri(False), att, 0.0)
    return dict(q=q, sig=sig, f=f, k=k, sq=sq, e_qi=e_qi, e_ki=e_ki, e_qs=e_qs, e_ks=e_ks, dc=dc,
                q_in=q_in, k_in=k_in, q_st=q_st, k_st=k_st, att=att)


def _hgrn_fwd(z, lb_logits, hg_g, nb, s_len):
    n = nb * s_len
    t = HG_T
    ns = s_len // t
    nchunk = s_len // CHUNK

    def body(q_ref, f_ref, v_ref, gb_ref, lg_ref, g_ref, o_ref, yb_ref, st_ref, st):
        @pl.when(pl.program_id(2) == 0)
        def _():
            st[...] = jnp.zeros((BD, BD), F32)

        lb = _lower_bound(lg_ref[...])
        ck = _hg_tile(q_ref[0], f_ref[0], lb)
        vb = _chunks(v_ref[0]).astype(BF16)
        kv = _bdot(vb, ck["k_st"].astype(BF16), BTN_DIMS)
        states = [st[...]]
        for c in range(HG_NC):
            states.append(states[c] * ck["dc"][c] + kv[c])
        st[...] = states[HG_NC]
        s_in = jnp.stack(states[:HG_NC], axis=0)
        st_ref[...] = s_in
        o = _bdot(ck["att"].astype(BF16), vb, BNN_DIMS) + _bdot(ck["q_st"].astype(BF16), s_in.astype(BF16), BNT_DIMS)
        o_ref[...] = o.reshape(t, BD)
        r = lax.rsqrt(jnp.mean(o * o, axis=-1, keepdims=True) + EPS)
        gb = _chunks(gb_ref[0])
        yb_ref[...] = (((o * r) * g_ref[...]) * (gb * _sigmoid(gb))).astype(BF16).reshape(t, BD)

    seg = lambda j: pl.BlockSpec((1, t, BD), lambda h, b, s: (j, b * ns + s, h))
    tile = pl.BlockSpec((t, BD), lambda h, b, s: (b * ns + s, h))
    return pl.pallas_call(
        body, name="hgrn_fwd", grid=(NB, nb, ns),
        in_specs=[seg(2), seg(3), seg(4), seg(5),
                  pl.BlockSpec((2, BD), lambda h, b, s: (0, h)),
                  pl.BlockSpec((1, BD), lambda h, b, s: (0, 0))],
        out_specs=[tile, tile,
                   pl.BlockSpec((HG_NC, BD, BD), lambda h, b, s: ((b * NB + h) * ns + s, 0, 0))],
        out_shape=[SDS((n, D), F32), SDS((n, D), BF16), SDS((nb * NB * nchunk, BD, BD), F32)],
        scratch_shapes=[pltpu.VMEM((BD, BD), F32)],
        compiler_params=_params(40),
    )(z, z, z, z, lb_logits, hg_g)


def _hgrn_bwd(z, o_all, st_all, dyb, lb_logits, hg_g, nb, s_len):
    n = nb * s_len
    t = HG_T
    ns = s_len // t

    def body(q_ref, f_ref, v_ref, gb_ref, o_ref, st_ref, dyb_ref, lg_ref, g_ref,
             dz_ref, glg_ref, ghg_ref, dst, dlb):
        h, b, s = pl.program_id(0), pl.program_id(1), pl.program_id(2)

        @pl.when((h == 0) & (b == 0) & (s == 0))
        def _():
            ghg_ref[...] = jnp.zeros((1, BD), F32)

        @pl.when((b == 0) & (s == 0))
        def _():
            dlb[...] = jnp.zeros((8, BD), F32)

        @pl.when(s == 0)
        def _():
            dst[...] = jnp.zeros((BD, BD), F32)

        lb = _lower_bound(lg_ref[...])
        g = g_ref[...]
        ck = _hg_tile(q_ref[0], f_ref[0], lb)
        q = ck["q"]
        vb = _chunks(v_ref[0]).astype(BF16)
        gb = _chunks(gb_ref[0])
        o = _chunks(o_ref[...])
        dyb_v = _chunks(dyb_ref[...])
        s_in = st_ref[...]

        sgb = _sigmoid(gb)
        r = lax.rsqrt(jnp.mean(o * o, axis=-1, keepdims=True) + EPS)
        ohat = o * r
        d_on = dyb_v * (gb * sgb)
        d_gb = dyb_v * (ohat * g) * (sgb * (1.0 + gb * (1.0 - sgb)))
        ghg_ref[...] += jnp.sum(jnp.sum(d_on * ohat, axis=1), axis=0, keepdims=True)
        tt = d_on * g
        d_o = r * (tt - ohat * jnp.mean(tt * ohat, axis=-1, keepdims=True))
        dob = d_o.astype(BF16)

        attb = ck["att"].astype(BF16)
        q_inb, k_inb = ck["q_in"].astype(BF16), ck["k_in"].astype(BF16)
        q_stb, k_stb = ck["q_st"].astype(BF16), ck["k_st"].astype(BF16)
        d_att = jnp.where(_tri(False), _bdot(dob, vb, BNT_DIMS), 0.0).astype(BF16)
        d_q_in = _bdot(d_att, k_inb, BNN_DIMS)
        d_k_in = _bdot(d_att, q_inb, BTN_DIMS)
        d_q_st = _bdot(dob, s_in.astype(BF16), BNN_DIMS)
        qdo = _bdot(dob, q_stb, BTN_DIMS)
        d_states = [None] * HG_NC + [dst[...]]
        for c in reversed(range(HG_NC)):
            d_states[c] = d_states[c + 1] * ck["dc"][c] + qdo[c]
        dst[...] = d_states[0]
        ds_out = jnp.stack(d_states[1:], axis=0)
        dsb = ds_out.astype(BF16)
        d_v = _bdot(attb, dob, BTN_DIMS) + _bdot(k_stb, dsb, BNT_DIMS)
        d_k_st = _bdot(vb, dsb, BNN_DIMS)
        d_dc = jnp.sum(ds_out * s_in, axis=1, keepdims=True)

        p_qi = d_q_in * ck["q_in"]
        p_ki = d_k_in * ck["k_in"]
        p_qs = d_q_st * ck["q_st"]
        p_ks = d_k_st * ck["k_st"]
        d_qh = (d_q_in * ck["e_qi"] + d_q_st * ck["e_qs"]) * HG_SCALE
        d_k = d_k_in * ck["e_ki"] + d_k_st * ck["e_ks"]
        d_b = (p_qi - p_ki) + (p_qs - p_ks)
        d_b_mid = jnp.sum(p_ki - p_qi, axis=1, keepdims=True)
        d_b_last = jnp.sum(p_ks, axis=1, keepdims=True) + d_dc * ck["dc"]
        rowi = lax.broadcasted_iota(jnp.int32, (HG_NC, CHUNK, BD), 1)
        d_b = d_b + jnp.where(rowi == CHUNK // 2, d_b_mid, 0.0) + jnp.where(rowi == CHUNK - 1, d_b_last, 0.0)
        d_logf = _bdot(_tri(True).astype(F32), d_b, BNN_DIMS, lax.Precision.HIGHEST)
        d_f = d_logf / ck["f"] - d_k
        sig, sq = ck["sig"], ck["sq"]
        d_fp = d_f * (1.0 - lb) * (sig * (1.0 - sig))
        dlb[0:1, :] += jnp.sum(jnp.sum(d_f * (1.0 - sig), axis=1), axis=0, keepdims=True)
        d_q = d_qh * (sq * (1.0 + q * (1.0 - sq)))
        dz_ref[0] = d_q.astype(BF16).reshape(t, BD)
        dz_ref[1] = d_fp.astype(BF16).reshape(t, BD)
        dz_ref[2] = d_v.astype(BF16).reshape(t, BD)
        dz_ref[3] = d_gb.astype(BF16).reshape(t, BD)

        @pl.when((b == nb - 1) & (s == ns - 1))
        def _():
            dl = dlb[0:1, :] * (lb * (1.0 - lb))
            glg_ref[0:1, :] = dl
            glg_ref[1:2, :] = -dl

    rb = lambda b, s: b * ns + (ns - 1 - s)
    seg = lambda j: pl.BlockSpec((1, t, BD), lambda h, b, s: (j, rb(b, s), h))
    tile = pl.BlockSpec((t, BD), lambda h, b, s: (rb(b, s), h))
    return pl.pallas_call(
        body, name="hgrn_bwd", grid=(NB, nb, ns),
        in_specs=[seg(2), seg(3), seg(4), seg(5), tile,
                  pl.BlockSpec((HG_NC, BD, BD), lambda h, b, s: ((b * NB + h) * ns + (ns - 1 - s), 0, 0)),
                  tile,
                  pl.BlockSpec((2, BD), lambda h, b, s: (0, h)),
                  pl.BlockSpec((1, BD), lambda h, b, s: (0, 0))],
        out_specs=[pl.BlockSpec((4, t, BD), lambda h, b, s: (0, rb(b, s), h)),
                   pl.BlockSpec((2, BD), lambda h, b, s: (0, h)),
                   pl.BlockSpec((1, BD), lambda h, b, s: (0, 0))],
        out_shape=[SDS((4, n, D), BF16), SDS((2, D), F32), SDS((1, BD), F32)],
        scratch_shapes=[pltpu.VMEM((BD, BD), F32), pltpu.VMEM((8, BD), F32)],
        compiler_params=_params(48),
    )(z, z, z, z, o_all, st_all, dyb, lb_logits, hg_g)


def _mid(ya, yb, z, b_merge, x2, tgt, fin_g, pa, pb, wo):
    n = x2.shape[0]
    tm = 256
    ni = n // tm

    def body(ya_ref, yb_ref, gma_ref, gmb_ref, bm_ref, x_ref, t_ref, fg_ref, pa_hbm, pb_hbm, wo_hbm,
             dx2_ref, dya_ref, dyb_ref, dgm_ref, loss_ref, gfg_ref, gbm_ref, gpa_hbm, gpb_hbm, gwo_hbm,
             pa_v, pb_v, wo_v, gpa_v, gpb_v, gwo_v, sem):
        i = pl.program_id(0)
        loads = [pltpu.make_async_copy(src, dst, sem.at[k])
                 for k, (src, dst) in enumerate(((pa_hbm, pa_v), (pb_hbm, pb_v), (wo_hbm, wo_v)))]
        stores = [pltpu.make_async_copy(src, dst, sem.at[k])
                  for k, (src, dst) in enumerate(((gpa_v, gpa_hbm), (gpb_v, gpb_hbm), (gwo_v, gwo_hbm)))]

        @pl.when(i == 0)
        def _():
            for cp in loads:
                cp.start()
            for ref in (gpa_v, gpb_v, gwo_v, loss_ref, gfg_ref, gbm_ref):
                ref[...] = jnp.zeros(ref.shape, F32)
            for cp in loads:
                cp.wait()

        ya_v = ya_ref[...]
        yb_v = yb_ref[...]
        out_a = jnp.dot(ya_v, pa_v[...], preferred_element_type=F32)
        out_b = jnp.dot(yb_v, pb_v[...], preferred_element_type=F32)
        bm = bm_ref[...]
        g_a = _sigmoid(gma_ref[0] + bm[:, 0:D])
        g_b = _sigmoid(gmb_ref[0] + bm[:, D:2 * D])
        mixed = g_a * out_a + g_b * out_b
        mixb = mixed.astype(BF16)
        xo = x_ref[...] + jnp.dot(mixb, wo_v[...], preferred_element_type=F32)
        r = lax.rsqrt(jnp.mean(xo * xo, axis=-1, keepdims=True) + EPS)
        xn = xo * r
        fg = fg_ref[...]
        e = xn * fg - t_ref[...]
        loss_ref[...] += 0.5 * jnp.sum(jnp.mean(e * e, axis=-1, keepdims=True))
        dy = e * (1.0 / D)
        gfg_ref[...] += jnp.sum(dy * xn, axis=0, keepdims=True)
        dxn = dy * fg
        dx2 = r * (dxn - xn * jnp.mean(dxn * xn, axis=-1, keepdims=True))
        dx2_ref[...] = dx2
        dx2b = dx2.astype(BF16)
        d_mixed = lax.dot_general(dx2b, wo_v[...], NT_DIMS, preferred_element_type=F32)
        gwo_v[...] += lax.dot_general(mixb, dx2b, TN_DIMS, preferred_element_type=F32)
        d_oa = (d_mixed * g_a).astype(BF16)
        d_ob = (d_mixed * g_b).astype(BF16)
        dgm_a = (d_mixed * out_a) * (g_a * (1.0 - g_a))
        dgm_b = (d_mixed * out_b) * (g_b * (1.0 - g_b))
        gbm_ref[:, 0:D] += jnp.sum(dgm_a, axis=0, keepdims=True)
        gbm_ref[:, D:2 * D] += jnp.sum(dgm_b, axis=0, keepdims=True)
        dgm_ref[0] = dgm_a.astype(BF16)
        dgm_ref[1] = dgm_b.astype(BF16)
        dya_ref[...] = lax.dot_general(d_oa, pa_v[...], NT_DIMS, preferred_element_type=F32)
        dyb_ref[...] = lax.dot_general(d_ob, pb_v[...], NT_DIMS, preferred_element_type=F32)
        gpa_v[...] += lax.dot_general(ya_v, d_oa, TN_DIMS, preferred_element_type=F32)
        gpb_v[...] += lax.dot_general(yb_v, d_ob, TN_DIMS, preferred_element_type=F32)

        @pl.when(i == ni - 1)
        def _():
            for cp in stores:
                cp.start()
            for cp in stores:
                cp.wait()

    rows = pl.BlockSpec((tm, D), lambda i: (i, 0))
    rep = lambda shape: pl.BlockSpec(shape, lambda i: (0,) * len(shape))
    return pl.pallas_call(
        body, name="mid", grid=(ni,),
        in_specs=[rows, rows,
                  pl.BlockSpec((1, tm, D), lambda i: (6, i, 0)), pl.BlockSpec((1, tm, D), lambda i: (7, i, 0)),
                  rep((1, 2 * D)), rows, rows, rep((1, D)), ANY, ANY, ANY],
        out_specs=[rows, rows, rows, pl.BlockSpec((2, tm, D), lambda i: (0, i, 0)),
                   rep((8, BD)), rep((1, D)), rep((1, 2 * D)), ANY, ANY, ANY],
        out_shape=[SDS((n, D), F32), SDS((n, D), F32), SDS((n, D), F32), SDS((2, n, D), BF16),
                   SDS((8, BD), F32), SDS((1, D), F32), SDS((1, 2 * D), F32),
                   SDS((D, D), F32), SDS((D, D), F32), SDS((D, D), F32)],
        scratch_shapes=[pltpu.VMEM((D, D), BF16)] * 3 + [pltpu.VMEM((D, D), F32)] * 3 + [pltpu.SemaphoreType.DMA((3,))],
        compiler_params=_params(60),
    )(ya, yb, z, z, b_merge, x2, tgt, fin_g, pa, pb, wo)


def _dz_specs(tm, ni, row_major):
    if row_major:
        ia = lambda i, j: (jnp.minimum(j, 1), i, 0)
        ib = lambda i, j: (jnp.clip(j - 2, 0, 3), i, 0)
        im = lambda i, j: (jnp.clip(j - 6, 0, 1), i, 0)
    else:
        last = ni - 1
        ia = lambda j, i: (jnp.minimum(j, 1), jnp.where(j < 2, i, last), 0)
        ib = lambda j, i: (jnp.clip(j - 2, 0, 3), jnp.where(j < 2, 0, jnp.where(j < 6, i, last)), 0)
        im = lambda j, i: (jnp.clip(j - 6, 0, 1), jnp.where(j < 6, 0, i), 0)
    return [pl.BlockSpec((1, tm, D), f) for f in (ia, ib, im)]


def _inproj_bwd_x(dza, dzb, dzm, w_all, x2, dx2, norm_g, after):
    n = x2.shape[0]
    tm = 512
    ni = n // tm

    def body(dza_ref, dzb_ref, dzm_ref, w_ref, x_ref, dx2_ref, g_ref, after_ref, gx_ref, gg_ref, acc):
        i, j = pl.program_id(0), pl.program_id(1)

        @pl.when((i == 0) & (j == 0))
        def _():
            gg_ref[...] = jnp.zeros((1, D), F32)

        @pl.when(j == 0)
        def _():
            acc[...] = jnp.zeros((tm, D), F32)

        def add(ref):
            acc[...] += lax.dot_general(ref[0], w_ref[0], NT_DIMS, preferred_element_type=F32)

        pl.when(j < 2)(lambda: add(dza_ref))
        pl.when((j >= 2) & (j < 6))(lambda: add(dzb_ref))
        pl.when(j >= 6)(lambda: add(dzm_ref))

        @pl.when(j == NB - 1)
        def _():
            x = x_ref[...]
            r = lax.rsqrt(jnp.mean(x * x, axis=-1, keepdims=True) + EPS)
            xn = x * r
            dh = acc[...]
            gg_ref[...] += jnp.sum(dh * xn, axis=0, keepdims=True)
            dxn = dh * g_ref[...]
            gx_ref[...] = dx2_ref[...] + r * (dxn - xn * jnp.mean(dxn * xn, axis=-1, keepdims=True))

    rows = pl.BlockSpec((tm, D), lambda i, j: (i, 0))
    return pl.pallas_call(
        body, name="inproj_bwd_x", grid=(ni, NB),
        in_specs=_dz_specs(tm, ni, True) + [pl.BlockSpec((1, D, D), lambda i, j: (j, 0, 0)), rows, rows,
                                             pl.BlockSpec((1, D), lambda i, j: (0, 0)), ANY],
        out_specs=[rows, pl.BlockSpec((1, D), lambda i, j: (0, 0))],
        out_shape=[SDS((n, D), F32), SDS((1, D), F32)],
        scratch_shapes=[pltpu.VMEM((tm, D), F32)],
        compiler_params=_params(48),
    )(dza, dzb, dzm, w_all, x2, dx2, norm_g, after)


def _inproj_bwd_w(dza, dzb, dzm, h_all):
    n = h_all.shape[0]
    tm = min(n, 1024)
    ni = n // tm

    def body(dza_ref, dzb_ref, dzm_ref, h_ref, gw_ref):
        j, i = pl.program_id(0), pl.program_id(1)

        @pl.when(i == 0)
        def _():
            gw_ref[...] = jnp.zeros((1, D, D), F32)

        def add(ref):
            gw_ref[0] += lax.dot_general(h_ref[...], ref[0], TN_DIMS, preferred_element_type=F32)

        pl.when(j < 2)(lambda: add(dza_ref))
        pl.when((j >= 2) & (j < 6))(lambda: add(dzb_ref))
        pl.when(j >= 6)(lambda: add(dzm_ref))

    return pl.pallas_call(
        body, name="inproj_bwd_w", grid=(NB, ni),
        in_specs=_dz_specs(tm, ni, False) + [pl.BlockSpec((tm, D), lambda j, i: (i, 0))],
        out_specs=pl.BlockSpec((1, D, D), lambda j, i: (j, 0, 0)),
        out_shape=SDS((NB, D, D), F32),
        compiler_params=_params(48),
    )(dza, dzb, dzm, h_all)


def _adamw(w, g, m, v):
    rows, cols = w.shape
    tr = _row_tile(rows)

    def body(w_ref, g_ref, m_ref, v_ref, d_ref, nm_ref, nv_ref):
        gv = g_ref[...]
        nm = ADAM_B1 * m_ref[...] + (1.0 - ADAM_B1) * gv
        nv = ADAM_B2 * v_ref[...] + (1.0 - ADAM_B2) * (gv * gv)
        m_hat = nm / (1.0 - ADAM_B1 ** ADAM_STEP)
        v_hat = nv / (1.0 - ADAM_B2 ** ADAM_STEP)
        d_ref[...] = -ADAM_LR * (m_hat / (jnp.sqrt(v_hat) + ADAM_EPS) + ADAM_WD * w_ref[...])
        nm_ref[...] = nm
        nv_ref[...] = nv

    spec = pl.BlockSpec((tr, cols), lambda i: (i, 0))
    return pl.pallas_call(
        body, name="adamw", grid=(rows // tr,), in_specs=[spec] * 4, out_specs=[spec] * 3,
        out_shape=[SDS((rows, cols), F32)] * 3, compiler_params=_params(32),
    )(w, g, m, v)


def _allgather(blocks, dtypes, name):
    na = len(blocks)

    def body(*refs):
        ins, outs, stages = refs[:na], refs[na:2 * na], refs[2 * na:3 * na]
        send_sems, recv_sems, local_sems = refs[3 * na:]
        x, y, c = _place()
        me, sibling = (x, y, c), (x, y, 1 - c)
        chips = [(1 - x, y), (x, 1 - y), (1 - x, 1 - y)]
        blk = lambda p: 4 * p[0] + 2 * p[1] + p[2]

        def copy(a, k, block, to, src=None):
            return pltpu.make_async_remote_copy(
                src_ref=outs[a].at[blk(block)] if src is None else src, dst_ref=outs[a].at[blk(block)],
                send_sem=send_sems.at[7 * a + k], recv_sem=recv_sems.at[7 * a + k],
                device_id=to, device_id_type=MESH)

        mine, first, passed = [], [], []
        for a in range(na):
            stages[a][...] = ins[a][...].astype(dtypes[a])
            mine.append(pltpu.make_async_copy(stages[a], outs[a].at[blk(me)], local_sems.at[a]))
            mine[-1].start()
            first.append(copy(a, 0, me, sibling, src=stages[a]))
            first += [copy(a, 1 + j, me, (*chip, c), src=stages[a]) for j, chip in enumerate(chips)]
        for cp in first:
            cp.start()
        for j, chip in enumerate(chips):
            for a in range(na):
                copy(a, 1 + j, (*chip, c), me).wait_recv()
                passed.append(copy(a, 4 + j, (*chip, c), sibling))
                passed[-1].start()
        for a in range(na):
            copy(a, 0, sibling, me).wait_recv()
            for j, chip in enumerate(chips):
                copy(a, 4 + j, (*chip, 1 - c), me).wait_recv()
        for cp in first + passed:
            cp.wait_send()
        for cp in mine:
            cp.wait()

    return pl.pallas_call(
        body, name=name,
        in_specs=[pl.BlockSpec(memory_space=pltpu.VMEM)] * na, out_specs=[ANY] * na,
        out_shape=[SDS((NB,) + b.shape, dt) for b, dt in zip(blocks, dtypes)],
        scratch_shapes=[pltpu.VMEM(b.shape, dt) for b, dt in zip(blocks, dtypes)]
        + [pltpu.SemaphoreType.DMA((7 * na,)), pltpu.SemaphoreType.DMA((7 * na,)), pltpu.SemaphoreType.DMA((na,))],
        compiler_params=_params(40),
    )(*blocks)


def _rs_sibling(gs):
    na = len(gs)

    def body(*refs):
        ins, outs = refs[:na], refs[na:2 * na]
        send_sems, recv_sems = refs[2 * na:]
        x, y, c = _place()
        copies = []
        for a in range(na):
            for q in range(4):
                copies.append(pltpu.make_async_remote_copy(
                    src_ref=ins[a].at[2 * q + (1 - c)], dst_ref=outs[a].at[q],
                    send_sem=send_sems.at[4 * a + q], recv_sem=recv_sems.at[4 * a + q],
                    device_id=(x, y, 1 - c), device_id_type=MESH))
        for cp in copies:
            cp.start()
        for cp in copies:
            cp.wait_recv()
        for cp in copies:
            cp.wait_send()

    return pl.pallas_call(
        body, name="rs_sibling", in_specs=[ANY] * na, out_specs=[ANY] * na,
        out_shape=[SDS((4,) + g.shape[1:], F32) for g in gs],
        scratch_shapes=[pltpu.SemaphoreType.DMA((4 * na,)), pltpu.SemaphoreType.DMA((4 * na,))],
    )(*gs)


HBM = pl.BlockSpec(memory_space=pltpu.HBM)
SEMS = pl.BlockSpec(memory_space=pltpu.SEMAPHORE)
EFFECT = pltpu.SideEffectType.DATAFLOW_SIDE_EFFECTING


def _chip_copies(srcs, lands, send_sems, recv_sems):
    x, y, c = _place()
    return [pltpu.make_async_remote_copy(
        src_ref=srcs[a].at[slot], dst_ref=lands[a].at[slot],
        send_sem=send_sems.at[3 * a + slot], recv_sem=recv_sems.at[3 * a + slot],
        device_id=(px, py, c), device_id_type=MESH)
        for a in range(len(srcs)) for slot, (px, py) in enumerate(_other_chips(x, y))]


def _rs_chips_start(ps):
    na = len(ps)

    def body(*refs):
        srcs, lands = refs[:na], refs[na:2 * na]
        send_sems, recv_sems = refs[2 * na], refs[2 * na + 1]
        token = refs[-1]
        for cp in _chip_copies(srcs, lands, send_sems, recv_sems):
            cp.start()
        token[...] = jnp.zeros_like(token)

    hbm = lambda a: pltpu.HBM(a.shape, a.dtype)
    out = pl.pallas_call(
        body, name="rs_chips_start",
        out_shape=(pltpu.SemaphoreType.DMA((3 * na,)), pltpu.SemaphoreType.DMA((3 * na,)),
                   *[hbm(p) for p in ps], *[hbm(p) for p in ps], SDS((8, BD), F32)),
        in_specs=[HBM] * (2 * na),
        out_specs=(SEMS, SEMS, *[HBM] * (2 * na), pl.BlockSpec(memory_space=pltpu.VMEM)),
        input_output_aliases={i: 2 + i for i in range(2 * na)},
        compiler_params=pltpu.CompilerParams(has_side_effects=EFFECT),
    )(*[pltpu.with_memory_space_constraint(p, pltpu.HBM) for p in ps],
      *[pltpu.with_memory_space_constraint(lax.empty(p.shape, p.dtype), pltpu.HBM) for p in ps])
    return out[0], out[1], out[2:2 + na], out[2 + na:2 + 2 * na], out[-1]


def _rs_chips_wait(send_sems, recv_sems, srcs, lands, after):
    na = len(srcs)

    def body(*refs):
        srcs_r, lands_r = refs[:na], refs[na:2 * na]
        send_r, recv_r = refs[2 * na], refs[2 * na + 1]
        copies = _chip_copies(srcs_r, lands_r, send_r, recv_r)
        for cp in copies:
            cp.wait_send()
        for cp in copies:
            cp.wait_recv()

    hbm = lambda a: pltpu.HBM(a.shape, a.dtype)
    out = pl.pallas_call(
        body, name="rs_chips_wait",
        out_shape=(*[hbm(p) for p in srcs], *[hbm(p) for p in lands]),
        in_specs=[HBM] * (2 * na) + [SEMS, SEMS, ANY],
        out_specs=tuple([HBM] * (2 * na)),
        input_output_aliases={i: i for i in range(2 * na)},
        compiler_params=pltpu.CompilerParams(has_side_effects=EFFECT),
    )(*srcs, *lands, send_sems, recv_sems, after)
    return out[na:]


def _add_sibling(place, g, a_in):
    _, r, cols = g.shape
    tr = _row_tile(r)

    def chip(k, pr):
        qx = pr[0] if k in (1, 3) else 1 - pr[0]
        qy = pr[1] if k in (0, 3) else 1 - pr[1]
        return 2 * qx + qy

    def body(place_ref, *refs):
        g_refs, a_refs, (out_ref, own_ref) = refs[0:4], refs[4:8], refs[8:10]
        for k in range(3):
            out_ref[k] = (g_refs[k][0] + a_refs[k][0]).astype(BF16)
        own_ref[...] = g_refs[3][0] + a_refs[3][0]

    mine = lambda k: pl.BlockSpec((1, tr, cols), lambda i, pr: (2 * chip(k, pr) + pr[2], i, 0))
    theirs = lambda k: pl.BlockSpec((1, tr, cols), lambda i, pr: (chip(k, pr), i, 0))
    return pl.pallas_call(
        body, name="add_sibling",
        grid_spec=pltpu.PrefetchScalarGridSpec(
            num_scalar_prefetch=1, grid=(r // tr,),
            in_specs=[mine(k) for k in range(4)] + [theirs(k) for k in range(4)],
            out_specs=[pl.BlockSpec((3, tr, cols), lambda i, pr: (0, i, 0)),
                       pl.BlockSpec((tr, cols), lambda i, pr: (i, 0))]),
        out_shape=[SDS((3, r, cols), BF16), SDS((r, cols), F32)], compiler_params=_params(48),
    )(place, *[g] * 4, *[a_in] * 4)


def _add_chips(own, b_in):
    r, cols = own.shape
    tr = _row_tile(r)

    def body(p_ref, b0_ref, b1_ref, b2_ref, o_ref):
        o_ref[...] = ((p_ref[...] + b0_ref[0].astype(F32)) + b1_ref[0].astype(F32)) + b2_ref[0].astype(F32)

    slot = lambda k: pl.BlockSpec((1, tr, cols), lambda i: (k, i, 0))
    spec = pl.BlockSpec((tr, cols), lambda i: (i, 0))
    return pl.pallas_call(
        body, name="add_chips", grid=(r // tr,), in_specs=[spec, slot(0), slot(1), slot(2)], out_specs=spec,
        out_shape=SDS((r, cols), F32), compiler_params=_params(32),
    )(own, b_in, b_in, b_in)


VEC_NAMES = ("b_merge", "conv_b", "rg_bx", "rg_ba", "rg_lambda", "hg_lb_logits", "hg_norm_g", "final_norm_g")
REP_NAMES = ("rg_wx", "rg_wa", "norm_g") + VEC_NAMES
SMALL_AT = 3 * BD
SMALL_ROWS = 48
MID_ROWS = 448


def _sum_blocks(parts):
    def body(p_ref, o_ref):
        acc = p_ref[0, 0:1, :]
        for k in range(1, NB):
            acc = acc + p_ref[k, 0:1, :]
        o_ref[...] = acc

    return pl.pallas_call(body, name="sum_blocks", out_shape=SDS((1, parts.shape[2]), F32))(parts)


def _pack_rows(arrays, width, row_multiple=8):
    flat = jnp.concatenate([a.reshape(-1) for a in arrays])
    rows = -(-flat.shape[0] // width)
    rows = -(-rows // row_multiple) * row_multiple
    return jnp.pad(flat, (0, rows * width - flat.shape[0])).reshape(rows, width)


def _unpack(flat, like):
    out, off = [], 0
    for a in like:
        out.append(flat[off:off + a.size].reshape(a.shape))
        off += a.size
    return out


def kernel(x, w_in, b_merge, conv_w, conv_b, rg_wx, rg_bx, rg_wa, rg_ba, rg_lambda, hg_lb_logits, hg_norm_g, proj_a, proj_b, w_out, norm_g, final_norm_g, loss_target, m_w_in, m_b_merge, m_conv_w, m_conv_b, m_rg_wx, m_rg_bx, m_rg_wa, m_rg_ba, m_rg_lambda, m_hg_lb_logits, m_hg_norm_g, m_proj_a, m_proj_b, m_w_out, m_norm_g, m_final_norm_g, v_w_in, v_b_merge, v_conv_w, v_conv_b, v_rg_wx, v_rg_bx, v_rg_wa, v_rg_ba, v_rg_lambda, v_hg_lb_logits, v_hg_norm_g, v_proj_a, v_proj_b, v_w_out, v_norm_g, v_final_norm_g):
    weights = dict(w_in=w_in, b_merge=b_merge, conv_w=conv_w, conv_b=conv_b, rg_wx=rg_wx, rg_bx=rg_bx, rg_wa=rg_wa,
                   rg_ba=rg_ba, rg_lambda=rg_lambda, hg_lb_logits=hg_lb_logits, hg_norm_g=hg_norm_g, proj_a=proj_a,
                   proj_b=proj_b, w_out=w_out, norm_g=norm_g, final_norm_g=final_norm_g)
    mom1 = dict(w_in=m_w_in, b_merge=m_b_merge, conv_w=m_conv_w, conv_b=m_conv_b, rg_wx=m_rg_wx, rg_bx=m_rg_bx,
                rg_wa=m_rg_wa, rg_ba=m_rg_ba, rg_lambda=m_rg_lambda, hg_lb_logits=m_hg_lb_logits,
                hg_norm_g=m_hg_norm_g, proj_a=m_proj_a, proj_b=m_proj_b, w_out=m_w_out, norm_g=m_norm_g,
                final_norm_g=m_final_norm_g)
    mom2 = dict(w_in=v_w_in, b_merge=v_b_merge, conv_w=v_conv_w, conv_b=v_conv_b, rg_wx=v_rg_wx, rg_bx=v_rg_bx,
                rg_wa=v_rg_wa, rg_ba=v_rg_ba, rg_lambda=v_rg_lambda, hg_lb_logits=v_hg_lb_logits,
                hg_norm_g=v_hg_norm_g, proj_a=v_proj_a, proj_b=v_proj_b, w_out=v_w_out, norm_g=v_norm_g,
                final_norm_g=v_final_norm_g)
    order = list(weights)
    nb, s_len, _ = x.shape
    n = nb * s_len
    px, py, pc = _place()
    place = jnp.stack([px, py, pc]).astype(jnp.int32)

    x2 = x.reshape(n, D)
    cw_blk = jnp.pad(conv_w[0], ((0, 4), (0, 0)))
    order_ids = jnp.stack([_block_id(p) for p in _arrival_order(px, py, pc)]).astype(jnp.int32)
    z, h_all, w_all, pa_all, pb_all, wo_all, cw_all = _gather_inproj(
        order_ids, x2, norm_g, [w_in[0], proj_a[0], proj_b[0], w_out[0], cw_blk], [BF16, BF16, BF16, BF16, F32])
    pa_full, pb_full, wo_full = (a.reshape(D, D) for a in (pa_all, pb_all, wo_all))
    cw8 = cw_all.transpose(1, 0, 2).reshape(8, D)
    wx_b, wa_b = rg_wx[0].astype(BF16), rg_wa[0].astype(BF16)
    cb, bx, ba = conv_b, rg_bx.reshape(1, D), rg_ba.reshape(1, D)
    fin_g = final_norm_g.reshape(1, D)

    hlru, ya = _lru_fwd(z, cw8, cb, wx_b, wa_b, bx, ba, rg_lambda, nb, s_len)
    o_all, yb, st_all = _hgrn_fwd(z, hg_lb_logits, hg_norm_g, nb, s_len)

    (dx2, dya, dyb, dzm, loss_acc, g_fin, g_bm, g_pa, g_pb, g_wo) = _mid(
        ya, yb, z, b_merge, x2, loss_target.reshape(n, D), fin_g, pa_full, pb_full, wo_full)
    dzb, g_lg, g_hg = _hgrn_bwd(z, o_all, st_all, dyb, hg_lb_logits, hg_norm_g, nb, s_len)
    dza, g_cw8, g_cb, g_wx, g_wa, g_bx, g_ba, g_lam = _lru_bwd(
        z, hlru, dya, cw8, cb, wx_b, wa_b, bx, ba, rg_lambda, nb, s_len)
    g_w = _inproj_bwd_w(dza, dzb, dzm, h_all)

    part = dict(b_merge=g_bm, conv_b=g_cb, rg_bx=g_bx, rg_ba=g_ba, rg_lambda=g_lam, hg_lb_logits=g_lg,
                hg_norm_g=g_hg, final_norm_g=g_fin)
    vec = _pack_rows([part[k] for k in VEC_NAMES], BD)
    vec = jnp.pad(vec, ((0, 16 * NB - vec.shape[0]), (0, 0))).reshape(NB, 2, D)
    rows8 = lambda a: jnp.pad(a, ((0, 0), (0, 8 - a.shape[1]), (0, 0)))
    g_m = jnp.concatenate([g.reshape(NB, BD, D) for g in (g_pa, g_pb, g_wo)]
                          + [g_wx.reshape(NB, 16, D), g_wa.reshape(NB, 16, D),
                             rows8(g_cw8.reshape(8, NB, BD).transpose(1, 0, 2).reshape(NB, 1, D)), rows8(vec),
                             jnp.zeros((NB, MID_ROWS - SMALL_AT - SMALL_ROWS, D), F32)], axis=1)
    from_sibling = _rs_sibling([g_w, g_m])
    w_out_bf, w_own = _add_sibling(place, g_w, from_sibling[0])
    m_out_bf, m_own = _add_sibling(place, g_m, from_sibling[1])
    send_sems, recv_sems, srcs, lands, token = _rs_chips_start([w_out_bf, m_out_bf])
    grad_x, g_ng = _inproj_bwd_x(dza, dzb, dzm, w_all, x2, dx2, norm_g, token)
    from_chips = _rs_chips_wait(send_sems, recv_sems, srcs, lands, grad_x)
    r_w = _add_chips(w_own, from_chips[0])
    r_m = _add_chips(m_own, from_chips[1])
    tail = jnp.concatenate([r_m[SMALL_AT:SMALL_AT + SMALL_ROWS], jnp.pad(g_ng, ((0, 7), (0, 0)))], axis=0)
    (tail_all,) = _allgather([tail], [F32], "gather_small_grads")

    grads = dict(w_in=r_w.reshape(1, D, D),
                 proj_a=r_m[0:BD].reshape(1, BD, D), proj_b=r_m[BD:2 * BD].reshape(1, BD, D),
                 w_out=r_m[2 * BD:3 * BD].reshape(1, BD, D),
                 conv_w=r_m[SMALL_AT + 32].reshape(8, BD)[0:4].reshape(1, 4, BD),
                 rg_wx=tail_all[:, 0:16].reshape(1, NB, BD, BD), rg_wa=tail_all[:, 16:32].reshape(1, NB, BD, BD),
                 norm_g=_sum_blocks(tail_all[:, SMALL_ROWS:SMALL_ROWS + 8]))
    vec_all = tail_all[:, 40:42].reshape(-1)
    for k, gk in zip(VEC_NAMES, _unpack(vec_all, [weights[k] for k in VEC_NAMES])):
        grads[k] = gk

    delta, new_m, new_v = {}, {}, {}
    for k in ("w_in", "proj_a", "proj_b", "w_out"):
        shp = weights[k].shape
        two = lambda a: a.reshape(shp[1], shp[2])
        d_k, m_k, v_k = _adamw(two(weights[k]), two(grads[k]), two(mom1[k]), two(mom2[k]))
        delta[k], new_m[k], new_v[k] = d_k.reshape(shp), m_k.reshape(shp), v_k.reshape(shp)
    rep = list(REP_NAMES) + ["conv_w"]
    packs = [_pack_rows([t[k] for k in rep], BD, 256) for t in (weights, grads, mom1, mom2)]
    outs = _adamw(*packs)
    for tgt, flat in zip((delta, new_m, new_v), outs):
        for k, a in zip(rep, _unpack(flat.reshape(-1), [weights[k] for k in rep])):
            tgt[k] = a

    loss = lax.psum(loss_acc[0, 0], ("x", "y", "c"))
    return (loss, grad_x.reshape(x.shape), *[grads[k] for k in order], *[delta[k] for k in order],
            *[new_m[k] for k in order], *[new_v[k] for k in order])
```

```python
import jax
import jax.numpy as jnp
from jax import lax
from jax.experimental import pallas as pl
from jax.experimental.pallas import tpu as pltpu

F32 = jnp.float32
BF16 = jnp.bfloat16
SDS = jax.ShapeDtypeStruct
MESH = pl.DeviceIdType.MESH
ANY = pl.BlockSpec(memory_space=pl.ANY)

D = 1024
NB = 8
BD = D // NB
CHUNK = 64
EPS = 1e-6
LRU_C = 8.0
HG_SCALE = BD ** -0.5
ADAM_LR, ADAM_B1, ADAM_B2, ADAM_EPS, ADAM_WD, ADAM_STEP = 0.001, 0.9, 0.999, 1e-08, 0.01, 10

NT_DIMS = (((1,), (1,)), ((), ()))
TN_DIMS = (((0,), (0,)), ((), ()))


def _params(vmem_mib):
    return pltpu.CompilerParams(vmem_limit_bytes=vmem_mib << 20)


def _row_tile(rows, most=256):
    assert rows % 8 == 0
    return max(t for t in range(8, min(rows, most) + 1, 8) if rows % t == 0)


def _sigmoid(v):
    return jax.nn.sigmoid(v)


def _softplus_neg(lam):
    t = -lam
    e = jnp.exp(-jnp.abs(t))
    w = 1.0 + e
    d = w - 1.0
    l1p = jnp.where(d == 0.0, e, jnp.log(w) * (e / jnp.where(d == 0.0, 1.0, d)))
    return jnp.maximum(t, 0.0) + l1p


def _place():
    return lax.axis_index("x"), lax.axis_index("y"), lax.axis_index("c")


def _other_chips(x, y):
    return [(1 - x, y), (x, 1 - y), (1 - x, 1 - y)]


def _block_id(p):
    return 4 * p[0] + 2 * p[1] + p[2]


def _arrival_order(x, y, c):
    near, far, diag = _other_chips(x, y)
    return [(x, y, c), (x, y, 1 - c), (*near, c), (*far, c), (*near, 1 - c), (*far, 1 - c), (*diag, c), (*diag, 1 - c)]


def _gather_inproj(order_ids, x2, norm_g, blocks, dtypes):
    na = len(blocks)
    n = x2.shape[0]
    tm = min(n, 1024)
    ni = n // tm

    def body(order_ref, x_ref, g_ref, *refs):
        ins, (z_ref, h_ref), outs = refs[:na], refs[na:na + 2], refs[na + 2:2 * na + 2]
        stages = refs[2 * na + 2:3 * na + 2]
        h_full, wbuf, send_sems, recv_sems, local_sems, wsem, hsem = refs[3 * na + 2:]
        j, i = pl.program_id(0), pl.program_id(1)
        x, y, c = _place()
        me, sibling = (x, y, c), (x, y, 1 - c)
        chips = _other_chips(x, y)
        small = range(1, na)

        def copy(a, k, block, to, src=None):
            return pltpu.make_async_remote_copy(
                src_ref=outs[a].at[_block_id(block)] if src is None else src, dst_ref=outs[a].at[_block_id(block)],
                send_sem=send_sems.at[7 * a + k], recv_sem=recv_sems.at[7 * a + k],
                device_id=to, device_id_type=MESH)

        def local(a):
            return pltpu.make_async_copy(stages[a], outs[a].at[_block_id(me)], local_sems.at[a])

        def landed(a, slot):
            copy(a, 1 + slot, (*chips[slot], c), me).wait_recv()
            copy(a, 4 + slot, (*chips[slot], c), sibling).start()

        def passed_on(a, slot):
            copy(a, 4 + slot, (*chips[slot], 1 - c), me).wait_recv()

        @pl.when((j == 0) & (i == 0))
        def _():
            for a in range(na):
                stages[a][...] = ins[a][...].astype(dtypes[a])
                local(a).start()
            for a in range(na):
                copy(a, 0, me, sibling, src=stages[a]).start()
                for slot, chip in enumerate(chips):
                    copy(a, 1 + slot, me, (*chip, c), src=stages[a]).start()

        @pl.when(j == 0)
        def _():
            xv = x_ref[...]
            r = lax.rsqrt(jnp.mean(xv * xv, axis=-1, keepdims=True) + EPS)
            hb = ((xv * r) * g_ref[...]).astype(BF16)
            h_full[pl.ds(pl.multiple_of(i * tm, tm), tm), :] = hb

        save_h = pltpu.make_async_copy(h_full, h_ref, hsem)
        pl.when((j == 0) & (i == ni - 1))(save_h.start)

        steps = [
            lambda: local(0).wait(),
            lambda: copy(0, 0, sibling, me).wait_recv(),
            lambda: landed(0, 0),
            lambda: landed(0, 1),
            lambda: passed_on(0, 0),
            lambda: passed_on(0, 1),
            lambda: landed(0, 2),
            lambda: passed_on(0, 2),
        ]
        for k, step in enumerate(steps):
            pl.when((i == 0) & (j == k))(step)

        @pl.when(i == 0)
        def _():
            load = pltpu.make_async_copy(outs[0].at[order_ref[j]], wbuf, wsem)
            load.start()
            load.wait()

        z_ref[0] = jnp.dot(h_full[pl.ds(pl.multiple_of(i * tm, tm), tm), :], wbuf[...], preferred_element_type=F32)

        @pl.when((j == NB - 1) & (i == ni - 1))
        def _():
            save_h.wait()
            for slot in range(3):
                for a in small:
                    landed(a, slot)
            for a in small:
                local(a).wait()
                copy(a, 0, sibling, me).wait_recv()
                for slot in range(3):
                    passed_on(a, slot)
            for a in range(na):
                copy(a, 0, me, sibling, src=stages[a]).wait_send()
                for slot, chip in enumerate(chips):
                    copy(a, 1 + slot, me, (*chip, c), src=stages[a]).wait_send()
                    copy(a, 4 + slot, (*chip, c), sibling).wait_send()

    rows_once = lambda j, i, order: (jnp.where(j == 0, i, ni - 1), 0)
    vmem = pl.BlockSpec(memory_space=pltpu.VMEM)
    return pl.pallas_call(
        body, name="gather_inproj",
        grid_spec=pltpu.PrefetchScalarGridSpec(
            num_scalar_prefetch=1, grid=(NB, ni),
            in_specs=[pl.BlockSpec((tm, D), rows_once), pl.BlockSpec((1, D), lambda j, i, order: (0, 0))] + [vmem] * na,
            out_specs=[pl.BlockSpec((1, tm, D), lambda j, i, order: (order[j], i, 0)), ANY] + [ANY] * na,
            scratch_shapes=[pltpu.VMEM(b.shape, dt) for b, dt in zip(blocks, dtypes)]
            + [pltpu.VMEM((n, D), BF16), pltpu.VMEM((D, D), BF16),
               pltpu.SemaphoreType.DMA((7 * na,)), pltpu.SemaphoreType.DMA((7 * na,)),
               pltpu.SemaphoreType.DMA((na,)), pltpu.SemaphoreType.DMA(()), pltpu.SemaphoreType.DMA(())]),
        out_shape=[SDS((NB, n, D), F32), SDS((n, D), BF16)] + [SDS((NB,) + b.shape, dt) for b, dt in zip(blocks, dtypes)],
        compiler_params=_params(56),
    )(order_ids, x2, norm_g, *blocks)


LRU_T = 256


def _conv(ext, cw, cb):
    t = LRU_T
    acc = ext[5:5 + t, :] * cw[0:1, :] + ext[6:6 + t, :] * cw[1:2, :]
    acc = acc + ext[7:7 + t, :] * cw[2:3, :]
    acc = acc + ext[8:8 + t, :] * cw[3:4, :]
    return cb + acc


def _lru_gates(xa, wx_ref, wa_ref, bx, ba, lam):
    xab = xa.astype(BF16)
    pis, prs = [], []
    for h in range(NB):
        xs = xab[:, h * BD:(h + 1) * BD]
        pis.append(jnp.dot(xs, wx_ref[h], preferred_element_type=F32))
        prs.append(jnp.dot(xs, wa_ref[h], preferred_element_type=F32))
    gi = _sigmoid(jnp.concatenate(pis, axis=1) + bx)
    gr = _sigmoid(jnp.concatenate(prs, axis=1) + ba)
    sp = _softplus_neg(lam)
    log_a = (-LRU_C * gr) * sp
    a = jnp.exp(log_a)
    mult = jnp.sqrt(-jnp.tanh(log_a) * (a * a + 1.0))
    return xab, gi, gr, sp, a, mult


def _lru_fwd(z, cw8, cb, wx, wa, bx, ba, lam, nb, s_len):
    n = nb * s_len
    t = LRU_T
    ns = s_len // t

    def body(xp_ref, ga_ref, cw_ref, cb_ref, wx_ref, wa_ref, bx_ref, ba_ref, lam_ref,
             h_ref, ya_ref, ext, a_s, u_s, carry):
        @pl.when(pl.program_id(1) == 0)
        def _():
            ext[0:8, :] = jnp.zeros((8, D), F32)
            carry[...] = jnp.zeros((8, D), F32)

        ext[8:8 + t, :] = xp_ref[0]
        xa = _conv(ext, cw_ref[...], cb_ref[...])
        ext[0:8, :] = ext[t:t + 8, :]
        _, gi, _, _, a, mult = _lru_gates(xa, wx_ref, wa_ref, bx_ref[...], ba_ref[...], lam_ref[...])
        u = (mult * gi) * xa
        row = lax.broadcasted_iota(jnp.int32, (t, D), 0) & 7
        for sh in (1, 2, 4):
            a_sh = pltpu.roll(a, sh, 0)
            u_sh = pltpu.roll(u, sh, 0)
            m = row >= sh
            u = jnp.where(m, a * u_sh + u, u)
            a = jnp.where(m, a * a_sh, a)
        a_s[...] = a
        u_s[...] = u

        def step(g, c):
            r = pl.multiple_of(g * 8, 8)
            hg = u_s[pl.ds(r, 8), :] + a_s[pl.ds(r, 8), :] * c
            h_ref[pl.ds(r, 8), :] = hg
            return hg[7:8, :]

        c_out = lax.fori_loop(0, t // 8, step, carry[0:1, :], unroll=4)
        carry[0:1, :] = c_out
        ga = ga_ref[0]
        ya_ref[...] = (h_ref[...] * (ga * _sigmoid(ga))).astype(BF16)

    row_map = lambda b, s: (b * ns + s, 0)
    rep2 = lambda b, s: (0, 0)
    rep3 = lambda b, s: (0, 0, 0)
    return pl.pallas_call(
        body, name="lru_fwd", grid=(nb, ns),
        in_specs=[pl.BlockSpec((1, t, D), lambda b, s: (0, b * ns + s, 0)),
                  pl.BlockSpec((1, t, D), lambda b, s: (1, b * ns + s, 0)),
                  pl.BlockSpec((8, D), rep2), pl.BlockSpec((1, D), rep2),
                  pl.BlockSpec((NB, BD, BD), rep3), pl.BlockSpec((NB, BD, BD), rep3),
                  pl.BlockSpec((1, D), rep2), pl.BlockSpec((1, D), rep2), pl.BlockSpec((1, D), rep2)],
        out_specs=[pl.BlockSpec((t, D), row_map), pl.BlockSpec((t, D), row_map)],
        out_shape=[SDS((n, D), F32), SDS((n, D), BF16)],
        scratch_shapes=[pltpu.VMEM((t + 8, D), F32), pltpu.VMEM((t, D), F32), pltpu.VMEM((t, D), F32),
                        pltpu.VMEM((8, D), F32)],
        compiler_params=_params(48),
    )(z, z, cw8, cb, wx, wa, bx, ba, lam)


def _lru_bwd(z, h_all, dya, cw8, cb, wx, wa, bx, ba, lam, nb, s_len):
    n = nb * s_len
    t = LRU_T
    ns = s_len // t
    t8 = t // 8

    def body(xp_ref, xph_ref, ga_ref, h_ref, hh_ref, dya_ref, cw_ref, cb_ref, wx_ref, wa_ref, bx_ref, ba_ref,
             lam_ref, dz_ref, gcw_ref, gcb_ref, gwx_ref, gwa_ref, gbx_ref, gba_ref, glam_ref,
             ext, hext, dext, a_s, u_s, dh_s, carry):
        b, s = pl.program_id(0), pl.program_id(1)
        first_tile = s == ns - 1

        @pl.when((b == 0) & (s == 0))
        def _():
            for ref in (gcw_ref, gcb_ref, gwx_ref, gwa_ref, gbx_ref, gba_ref, glam_ref):
                ref[...] = jnp.zeros(ref.shape, F32)

        @pl.when(s == 0)
        def _():
            dext[t:t + 8, :] = jnp.zeros((8, D), F32)
            carry[...] = jnp.zeros((8, D), F32)

        keep = jnp.where(first_tile, 0.0, 1.0)
        ext[0:8, :] = xph_ref[0] * keep
        ext[8:8 + t, :] = xp_ref[0]
        hext[0:8, :] = hh_ref[...] * keep
        hext[8:8 + t, :] = h_ref[...]
        cw = cw_ref[...]
        lam = lam_ref[...]
        xa = _conv(ext, cw, cb_ref[...])
        xab, gi, gr, sp, a, mult = _lru_gates(xa, wx_ref, wa_ref, bx_ref[...], ba_ref[...], lam)
        h_prev = hext[7:7 + t, :]
        ga = ga_ref[0]
        sg = _sigmoid(ga)
        dya_v = dya_ref[...]
        d_ga = dya_v * h_ref[...] * (sg * (1.0 + ga * (1.0 - sg)))
        g_in = dya_v * (ga * sg)

        rows = lax.broadcasted_iota(jnp.int32, (t, D), 0)
        row = rows & 7
        an = jnp.where(rows == t - 1, 1.0, pltpu.roll(a, t - 1, 0))
        u = g_in
        for sh in (1, 2, 4):
            a_sh = pltpu.roll(an, t - sh, 0)
            u_sh = pltpu.roll(u, t - sh, 0)
            m = row < 8 - sh
            u = jnp.where(m, u + an * u_sh, u)
            an = jnp.where(m, an * a_sh, an)
        a_s[...] = an
        u_s[...] = u

        def step(i, c):
            r = pl.multiple_of((t8 - 1 - i) * 8, 8)
            dg = u_s[pl.ds(r, 8), :] + a_s[pl.ds(r, 8), :] * c
            dh_s[pl.ds(r, 8), :] = dg
            return dg[0:1, :]

        lax.fori_loop(0, t8, step, carry[0:1, :], unroll=4)
        dh = dh_s[...]
        carry[0:1, :] = a[0:1, :] * dh[0:1, :]

        d_a = dh * h_prev
        dux = dh * xa
        d_mult = dux * gi
        d_gi = dux * mult
        d_xa = dh * (mult * gi)
        d_loga = d_a * a - d_mult * ((a * a) / mult)
        d_gr = d_loga * (-LRU_C * sp)
        d_sp = jnp.sum(d_loga * (-LRU_C * gr), axis=0, keepdims=True)
        glam_ref[...] += d_sp * (-_sigmoid(-lam))
        d_pi = d_gi * gi * (1.0 - gi)
        d_pr = d_gr * gr * (1.0 - gr)
        gbx_ref[...] += jnp.sum(d_pi, axis=0, keepdims=True)
        gba_ref[...] += jnp.sum(d_pr, axis=0, keepdims=True)
        dpib = d_pi.astype(BF16)
        dprb = d_pr.astype(BF16)
        back = []
        for h in range(NB):
            cs = slice(h * BD, (h + 1) * BD)
            gwx_ref[h] += lax.dot_general(xab[:, cs], dpib[:, cs], TN_DIMS, preferred_element_type=F32)
            gwa_ref[h] += lax.dot_general(xab[:, cs], dprb[:, cs], TN_DIMS, preferred_element_type=F32)
            back.append(lax.dot_general(dpib[:, cs], wx_ref[h], NT_DIMS, preferred_element_type=F32)
                        + lax.dot_general(dprb[:, cs], wa_ref[h], NT_DIMS, preferred_element_type=F32))
        d_xa = d_xa + jnp.concatenate(back, axis=1)

        dext[0:t, :] = d_xa
        d_xp = dext[3:3 + t, :] * cw[0:1, :] + dext[2:2 + t, :] * cw[1:2, :]
        d_xp = d_xp + dext[1:1 + t, :] * cw[2:3, :]
        d_xp = d_xp + d_xa * cw[3:4, :]
        dext[t:t + 8, :] = d_xa[0:8, :]
        gcb_ref[...] += jnp.sum(d_xa, axis=0, keepdims=True)
        for k in range(4):
            gcw_ref[k:k + 1, :] += jnp.sum(d_xa * ext[5 + k:5 + k + t, :], axis=0, keepdims=True)
        dz_ref[0] = d_xp.astype(BF16)
        dz_ref[1] = d_ga.astype(BF16)

    rb = lambda b, s: b * ns + (ns - 1 - s)
    halo = lambda b, s: jnp.maximum(rb(b, s) * t8 - 1, 0)
    rep2 = lambda b, s: (0, 0)
    rep3 = lambda b, s: (0, 0, 0)
    return pl.pallas_call(
        body, name="lru_bwd", grid=(nb, ns),
        in_specs=[pl.BlockSpec((1, t, D), lambda b, s: (0, rb(b, s), 0)),
                  pl.BlockSpec((1, 8, D), lambda b, s: (0, halo(b, s), 0)),
                  pl.BlockSpec((1, t, D), lambda b, s: (1, rb(b, s), 0)),
                  pl.BlockSpec((t, D), lambda b, s: (rb(b, s), 0)),
                  pl.BlockSpec((8, D), lambda b, s: (halo(b, s), 0)),
                  pl.BlockSpec((t, D), lambda b, s: (rb(b, s), 0)),
                  pl.BlockSpec((8, D), rep2), pl.BlockSpec((1, D), rep2),
                  pl.BlockSpec((NB, BD, BD), rep3), pl.BlockSpec((NB, BD, BD), rep3),
                  pl.BlockSpec((1, D), rep2), pl.BlockSpec((1, D), rep2), pl.BlockSpec((1, D), rep2)],
        out_specs=[pl.BlockSpec((2, t, D), lambda b, s: (0, rb(b, s), 0)),
                   pl.BlockSpec((8, D), rep2), pl.BlockSpec((1, D), rep2),
                   pl.BlockSpec((NB, BD, BD), rep3), pl.BlockSpec((NB, BD, BD), rep3),
                   pl.BlockSpec((1, D), rep2), pl.BlockSpec((1, D), rep2), pl.BlockSpec((1, D), rep2)],
        out_shape=[SDS((2, n, D), BF16), SDS((8, D), F32), SDS((1, D), F32),
                   SDS((NB, BD, BD), F32), SDS((NB, BD, BD), F32),
                   SDS((1, D), F32), SDS((1, D), F32), SDS((1, D), F32)],
        scratch_shapes=[pltpu.VMEM((t + 8, D), F32), pltpu.VMEM((t + 8, D), F32), pltpu.VMEM((t + 8, D), F32),
                        pltpu.VMEM((t, D), F32), pltpu.VMEM((t, D), F32), pltpu.VMEM((t, D), F32),
                        pltpu.VMEM((8, D), F32)],
        compiler_params=_params(56),
    )(z, z, z, h_all, h_all, dya, cw8, cb, wx, wa, bx, ba, lam)


HG_T = 512
HG_NC = HG_T // CHUNK
BNT_DIMS = (((2,), (2,)), ((0,), (0,)))
BNN_DIMS = (((2,), (1,)), ((0,), (0,)))
BTN_DIMS = (((1,), (1,)), ((0,), (0,)))


def _lower_bound(lg):
    m = jnp.max(lg, axis=0, keepdims=True)
    e = jnp.exp(lg - m)
    return e[0:1, :] / jnp.sum(e, axis=0, keepdims=True)


def _tri(upper):
    r = lax.broadcasted_iota(jnp.int32, (HG_NC, CHUNK, CHUNK), 1)
    c = lax.broadcasted_iota(jnp.int32, (HG_NC, CHUNK, CHUNK), 2)
    return (c >= r) if upper else (r >= c)


def _bdot(a, b, dims, precision=None):
    return lax.dot_general(a, b, dims, precision=precision, preferred_element_type=F32)


def _chunks(a):
    return a.reshape(HG_NC, CHUNK, BD)


def _hg_tile(q, fp, lb):
    q, fp = _chunks(q), _chunks(fp)
    sig = _sigmoid(fp)
    f = lb + (1.0 - lb) * sig
    log_f = jnp.log(f)
    k = 1.0 - f
    b = _bdot(_tri(False).astype(F32), log_f, BNN_DIMS, lax.Precision.HIGHEST)
    b_mid = b[:, CHUNK // 2:CHUNK // 2 + 1, :]
    b_last = b[:, CHUNK - 1:CHUNK, :]
    sq = _sigmoid(q)
    qh = q * sq
    e_qi = jnp.exp(b - b_mid)
    e_ki = jnp.exp(b_mid - b)
    e_qs = jnp.exp(b)
    e_ks = jnp.exp(b_last - b)
    dc = jnp.exp(b_last)
    q_in = (qh * e_qi) * HG_SCALE
    k_in = k * e_ki
    q_st = (qh * e_qs) * HG_SCALE
    k_st = k * e_ks
    att = _bdot(q_in.astype(BF16), k_in.astype(BF16), BNT_DIMS)
    att = jnp.where(_tri(False), att, 0.0)
    return dict(q=q, sig=sig, f=f, k=k, sq=sq, e_qi=e_qi, e_ki=e_ki, e_qs=e_qs, e_ks=e_ks, dc=dc,
                q_in=q_in, k_in=k_in, q_st=q_st, k_st=k_st, att=att)


def _hgrn_fwd(z, lb_logits, hg_g, nb, s_len):
    n = nb * s_len
    t = HG_T
    ns = s_len // t
    nchunk = s_len // CHUNK

    def body(q_ref, f_ref, v_ref, gb_ref, lg_ref, g_ref, o_ref, yb_ref, st_ref, st):
        @pl.when(pl.program_id(2) == 0)
        def _():
            st[...] = jnp.zeros((BD, BD), F32)

        lb = _lower_bound(lg_ref[...])
        ck = _hg_tile(q_ref[0], f_ref[0], lb)
        vb = _chunks(v_ref[0]).astype(BF16)
        kv = _bdot(vb, ck["k_st"].astype(BF16), BTN_DIMS)
        states = [st[...]]
        for c in range(HG_NC):
            states.append(states[c] * ck["dc"][c] + kv[c])
        st[...] = states[HG_NC]
        s_in = jnp.stack(states[:HG_NC], axis=0)
        st_ref[...] = s_in
        o = _bdot(ck["att"].astype(BF16), vb, BNN_DIMS) + _bdot(ck["q_st"].astype(BF16), s_in.astype(BF16), BNT_DIMS)
        o_ref[...] = o.reshape(t, BD)
        r = lax.rsqrt(jnp.mean(o * o, axis=-1, keepdims=True) + EPS)
        gb = _chunks(gb_ref[0])
        yb_ref[...] = (((o * r) * g_ref[...]) * (gb * _sigmoid(gb))).astype(BF16).reshape(t, BD)

    seg = lambda j: pl.BlockSpec((1, t, BD), lambda h, b, s: (j, b * ns + s, h))
    tile = pl.BlockSpec((t, BD), lambda h, b, s: (b * ns + s, h))
    return pl.pallas_call(
        body, name="hgrn_fwd", grid=(NB, nb, ns),
        in_specs=[seg(2), seg(3), seg(4), seg(5),
                  pl.BlockSpec((2, BD), lambda h, b, s: (0, h)),
                  pl.BlockSpec((1, BD), lambda h, b, s: (0, 0))],
        out_specs=[tile, tile,
                   pl.BlockSpec((HG_NC, BD, BD), lambda h, b, s: ((b * NB + h) * ns + s, 0, 0))],
        out_shape=[SDS((n, D), F32), SDS((n, D), BF16), SDS((nb * NB * nchunk, BD, BD), F32)],
        scratch_shapes=[pltpu.VMEM((BD, BD), F32)],
        compiler_params=_params(40),
    )(z, z, z, z, lb_logits, hg_g)


def _hgrn_bwd(z, o_all, st_all, dyb, lb_logits, hg_g, nb, s_len):
    n = nb * s_len
    t = HG_T
    ns = s_len // t

    def body(q_ref, f_ref, v_ref, gb_ref, o_ref, st_ref, dyb_ref, lg_ref, g_ref,
             dz_ref, glg_ref, ghg_ref, dst, dlb):
        h, b, s = pl.program_id(0), pl.program_id(1), pl.program_id(2)

        @pl.when((h == 0) & (b == 0) & (s == 0))
        def _():
            ghg_ref[...] = jnp.zeros((1, BD), F32)

        @pl.when((b == 0) & (s == 0))
        def _():
            dlb[...] = jnp.zeros((8, BD), F32)

        @pl.when(s == 0)
        def _():
            dst[...] = jnp.zeros((BD, BD), F32)

        lb = _lower_bound(lg_ref[...])
        g = g_ref[...]
        ck = _hg_tile(q_ref[0], f_ref[0], lb)
        q = ck["q"]
        vb = _chunks(v_ref[0]).astype(BF16)
        gb = _chunks(gb_ref[0])
        o = _chunks(o_ref[...])
        dyb_v = _chunks(dyb_ref[...])
        s_in = st_ref[...]

        sgb = _sigmoid(gb)
        r = lax.rsqrt(jnp.mean(o * o, axis=-1, keepdims=True) + EPS)
        ohat = o * r
        d_on = dyb_v * (gb * sgb)
        d_gb = dyb_v * (ohat * g) * (sgb * (1.0 + gb * (1.0 - sgb)))
        ghg_ref[...] += jnp.sum(jnp.sum(d_on * ohat, axis=1), axis=0, keepdims=True)
        tt = d_on * g
        d_o = r * (tt - ohat * jnp.mean(tt * ohat, axis=-1, keepdims=True))
        dob = d_o.astype(BF16)

        attb = ck["att"].astype(BF16)
        q_inb, k_inb = ck["q_in"].astype(BF16), ck["k_in"].astype(BF16)
        q_stb, k_stb = ck["q_st"].astype(BF16), ck["k_st"].astype(BF16)
        d_att = jnp.where(_tri(False), _bdot(dob, vb, BNT_DIMS), 0.0).astype(BF16)
        d_q_in = _bdot(d_att, k_inb, BNN_DIMS)
        d_k_in = _bdot(d_att, q_inb, BTN_DIMS)
        d_q_st = _bdot(dob, s_in.astype(BF16), BNN_DIMS)
        qdo = _bdot(dob, q_stb, BTN_DIMS)
        d_states = [None] * HG_NC + [dst[...]]
        for c in reversed(range(HG_NC)):
            d_states[c] = d_states[c + 1] * ck["dc"][c] + qdo[c]
        dst[...] = d_states[0]
        ds_out = jnp.stack(d_states[1:], axis=0)
        dsb = ds_out.astype(BF16)
        d_v = _bdot(attb, dob, BTN_DIMS) + _bdot(k_stb, dsb, BNT_DIMS)
        d_k_st = _bdot(vb, dsb, BNN_DIMS)
        d_dc = jnp.sum(ds_out * s_in, axis=1, keepdims=True)

        p_qi = d_q_in * ck["q_in"]
        p_ki = d_k_in * ck["k_in"]
        p_qs = d_q_st * ck["q_st"]
        p_ks = d_k_st * ck["k_st"]
        d_qh = (d_q_in * ck["e_qi"] + d_q_st * ck["e_qs"]) * HG_SCALE
        d_k = d_k_in * ck["e_ki"] + d_k_st * ck["e_ks"]
        d_b = (p_qi - p_ki) + (p_qs - p_ks)
        d_b_mid = jnp.sum(p_ki - p_qi, axis=1, keepdims=True)
        d_b_last = jnp.sum(p_ks, axis=1, keepdims=True) + d_dc * ck["dc"]
        rowi = lax.broadcasted_iota(jnp.int32, (HG_NC, CHUNK, BD), 1)
        d_b = d_b + jnp.where(rowi == CHUNK // 2, d_b_mid, 0.0) + jnp.where(rowi == CHUNK - 1, d_b_last, 0.0)
        d_logf = _bdot(_tri(True).astype(F32), d_b, BNN_DIMS, lax.Precision.HIGHEST)
        d_f = d_logf / ck["f"] - d_k
        sig, sq = ck["sig"], ck["sq"]
        d_fp = d_f * (1.0 - lb) * (sig * (1.0 - sig))
        dlb[0:1, :] += jnp.sum(jnp.sum(d_f * (1.0 - sig), axis=1), axis=0, keepdims=True)
        d_q = d_qh * (sq * (1.0 + q * (1.0 - sq)))
        dz_ref[0] = d_q.astype(BF16).reshape(t, BD)
        dz_ref[1] = d_fp.astype(BF16).reshape(t, BD)
        dz_ref[2] = d_v.astype(BF16).reshape(t, BD)
        dz_ref[3] = d_gb.astype(BF16).reshape(t, BD)

        @pl.when((b == nb - 1) & (s == ns - 1))
        def _():
            dl = dlb[0:1, :] * (lb * (1.0 - lb))
            glg_ref[0:1, :] = dl
            glg_ref[1:2, :] = -dl

    rb = lambda b, s: b * ns + (ns - 1 - s)
    seg = lambda j: pl.BlockSpec((1, t, BD), lambda h, b, s: (j, rb(b, s), h))
    tile = pl.BlockSpec((t, BD), lambda h, b, s: (rb(b, s), h))
    return pl.pallas_call(
        body, name="hgrn_bwd", grid=(NB, nb, ns),
        in_specs=[seg(2), seg(3), seg(4), seg(5), tile,
                  pl.BlockSpec((HG_NC, BD, BD), lambda h, b, s: ((b * NB + h) * ns + (ns - 1 - s), 0, 0)),
                  tile,
                  pl.BlockSpec((2, BD), lambda h, b, s: (0, h)),
                  pl.BlockSpec((1, BD), lambda h, b, s: (0, 0))],
        out_specs=[pl.BlockSpec((4, t, BD), lambda h, b, s: (0, rb(b, s), h)),
                   pl.BlockSpec((2, BD), lambda h, b, s: (0, h)),
                   pl.BlockSpec((1, BD), lambda h, b, s: (0, 0))],
        out_shape=[SDS((4, n, D), BF16), SDS((2, D), F32), SDS((1, BD), F32)],
        scratch_shapes=[pltpu.VMEM((BD, BD), F32), pltpu.VMEM((8, BD), F32)],
        compiler_params=_params(48),
    )(z, z, z, z, o_all, st_all, dyb, lb_logits, hg_g)


def _mid(ya, yb, z, b_merge, x2, tgt, fin_g, pa, pb, wo):
    n = x2.shape[0]
    tm = 256
    ni = n // tm

    def body(ya_ref, yb_ref, gma_ref, gmb_ref, bm_ref, x_ref, t_ref, fg_ref, pa_hbm, pb_hbm, wo_hbm,
             dx2_ref, dya_ref, dyb_ref, dgm_ref, loss_ref, gfg_ref, gbm_ref, gpa_hbm, gpb_hbm, gwo_hbm,
             pa_v, pb_v, wo_v, gpa_v, gpb_v, gwo_v, sem):
        i = pl.program_id(0)
        loads = [pltpu.make_async_copy(src, dst, sem.at[k])
                 for k, (src, dst) in enumerate(((pa_hbm, pa_v), (pb_hbm, pb_v), (wo_hbm, wo_v)))]
        stores = [pltpu.make_async_copy(src, dst, sem.at[k])
                  for k, (src, dst) in enumerate(((gpa_v, gpa_hbm), (gpb_v, gpb_hbm), (gwo_v, gwo_hbm)))]

        @pl.when(i == 0)
        def _():
            for cp in loads:
                cp.start()
            for ref in (gpa_v, gpb_v, gwo_v, loss_ref, gfg_ref, gbm_ref):
                ref[...] = jnp.zeros(ref.shape, F32)
            for cp in loads:
                cp.wait()

        ya_v = ya_ref[...]
        yb_v = yb_ref[...]
        out_a = jnp.dot(ya_v, pa_v[...], preferred_element_type=F32)
        out_b = jnp.dot(yb_v, pb_v[...], preferred_element_type=F32)
        bm = bm_ref[...]
        g_a = _sigmoid(gma_ref[0] + bm[:, 0:D])
        g_b = _sigmoid(gmb_ref[0] + bm[:, D:2 * D])
        mixed = g_a * out_a + g_b * out_b
        mixb = mixed.astype(BF16)
        xo = x_ref[...] + jnp.dot(mixb, wo_v[...], preferred_element_type=F32)
        r = lax.rsqrt(jnp.mean(xo * xo, axis=-1, keepdims=True) + EPS)
        xn = xo * r
        fg = fg_ref[...]
        e = xn * fg - t_ref[...]
        loss_ref[...] += 0.5 * jnp.sum(jnp.mean(e * e, axis=-1, keepdims=True))
        dy = e * (1.0 / D)
        gfg_ref[...] += jnp.sum(dy * xn, axis=0, keepdims=True)
        dxn = dy * fg
        dx2 = r * (dxn - xn * jnp.mean(dxn * xn, axis=-1, keepdims=True))
        dx2_ref[...] = dx2
        dx2b = dx2.astype(BF16)
        d_mixed = lax.dot_general(dx2b, wo_v[...], NT_DIMS, preferred_element_type=F32)
        gwo_v[...] += lax.dot_general(mixb, dx2b, TN_DIMS, preferred_element_type=F32)
        d_oa = (d_mixed * g_a).astype(BF16)
        d_ob = (d_mixed * g_b).astype(BF16)
        dgm_a = (d_mixed * out_a) * (g_a * (1.0 - g_a))
        dgm_b = (d_mixed * out_b) * (g_b * (1.0 - g_b))
        gbm_ref[:, 0:D] += jnp.sum(dgm_a, axis=0, keepdims=True)
        gbm_ref[:, D:2 * D] += jnp.sum(dgm_b, axis=0, keepdims=True)
        dgm_ref[0] = dgm_a.astype(BF16)
        dgm_ref[1] = dgm_b.astype(BF16)
        dya_ref[...] = lax.dot_general(d_oa, pa_v[...], NT_DIMS, preferred_element_type=F32)
        dyb_ref[...] = lax.dot_general(d_ob, pb_v[...], NT_DIMS, preferred_element_type=F32)
        gpa_v[...] += lax.dot_general(ya_v, d_oa, TN_DIMS, preferred_element_type=F32)
        gpb_v[...] += lax.dot_general(yb_v, d_ob, TN_DIMS, preferred_element_type=F32)

        @pl.when(i == ni - 1)
        def _():
            for cp in stores:
                cp.start()
            for cp in stores:
                cp.wait()

    rows = pl.BlockSpec((tm, D), lambda i: (i, 0))
    rep = lambda shape: pl.BlockSpec(shape, lambda i: (0,) * len(shape))
    return pl.pallas_call(
        body, name="mid", grid=(ni,),
        in_specs=[rows, rows,
                  pl.BlockSpec((1, tm, D), lambda i: (6, i, 0)), pl.BlockSpec((1, tm, D), lambda i: (7, i, 0)),
                  rep((1, 2 * D)), rows, rows, rep((1, D)), ANY, ANY, ANY],
        out_specs=[rows, rows, rows, pl.BlockSpec((2, tm, D), lambda i: (0, i, 0)),
                   rep((8, BD)), rep((1, D)), rep((1, 2 * D)), ANY, ANY, ANY],
        out_shape=[SDS((n, D), F32), SDS((n, D), F32), SDS((n, D), F32), SDS((2, n, D), BF16),
                   SDS((8, BD), F32), SDS((1, D), F32), SDS((1, 2 * D), F32),
                   SDS((D, D), F32), SDS((D, D), F32), SDS((D, D), F32)],
        scratch_shapes=[pltpu.VMEM((D, D), BF16)] * 3 + [pltpu.VMEM((D, D), F32)] * 3 + [pltpu.SemaphoreType.DMA((3,))],
        compiler_params=_params(60),
    )(ya, yb, z, z, b_merge, x2, tgt, fin_g, pa, pb, wo)


def _dz_specs(tm, ni, row_major):
    if row_major:
        ia = lambda i, j: (jnp.minimum(j, 1), i, 0)
        ib = lambda i, j: (jnp.clip(j - 2, 0, 3), i, 0)
        im = lambda i, j: (jnp.clip(j - 6, 0, 1), i, 0)
    else:
        last = ni - 1
        ia = lambda j, i: (jnp.minimum(j, 1), jnp.where(j < 2, i, last), 0)
        ib = lambda j, i: (jnp.clip(j - 2, 0, 3), jnp.where(j < 2, 0, jnp.where(j < 6, i, last)), 0)
        im = lambda j, i: (jnp.clip(j - 6, 0, 1), jnp.where(j < 6, 0, i), 0)
    return [pl.BlockSpec((1, tm, D), f) for f in (ia, ib, im)]


def _inproj_bwd_x(dza, dzb, dzm, w_all, x2, dx2, norm_g, after):
    n = x2.shape[0]
    tm = 512
    ni = n // tm

    def body(dza_ref, dzb_ref, dzm_ref, w_ref, x_ref, dx2_ref, g_ref, after_ref, gx_ref, gg_ref, acc):
        i, j = pl.program_id(0), pl.program_id(1)

        @pl.when((i == 0) & (j == 0))
        def _():
            gg_ref[...] = jnp.zeros((1, D), F32)

        @pl.when(j == 0)
        def _():
            acc[...] = jnp.zeros((tm, D), F32)

        def add(ref):
            acc[...] += lax.dot_general(ref[0], w_ref[0], NT_DIMS, preferred_element_type=F32)

        pl.when(j < 2)(lambda: add(dza_ref))
        pl.when((j >= 2) & (j < 6))(lambda: add(dzb_ref))
        pl.when(j >= 6)(lambda: add(dzm_ref))

        @pl.when(j == NB - 1)
        def _():
            x = x_ref[...]
            r = lax.rsqrt(jnp.mean(x * x, axis=-1, keepdims=True) + EPS)
            xn = x * r
            dh = acc[...]
            gg_ref[...] += jnp.sum(dh * xn, axis=0, keepdims=True)
            dxn = dh * g_ref[...]
            gx_ref[...] = dx2_ref[...] + r * (dxn - xn * jnp.mean(dxn * xn, axis=-1, keepdims=True))

    rows = pl.BlockSpec((tm, D), lambda i, j: (i, 0))
    return pl.pallas_call(
        body, name="inproj_bwd_x", grid=(ni, NB),
        in_specs=_dz_specs(tm, ni, True) + [pl.BlockSpec((1, D, D), lambda i, j: (j, 0, 0)), rows, rows,
                                             pl.BlockSpec((1, D), lambda i, j: (0, 0)), ANY],
        out_specs=[rows, pl.BlockSpec((1, D), lambda i, j: (0, 0))],
        out_shape=[SDS((n, D), F32), SDS((1, D), F32)],
        scratch_shapes=[pltpu.VMEM((tm, D), F32)],
        compiler_params=_params(48),
    )(dza, dzb, dzm, w_all, x2, dx2, norm_g, after)


def _inproj_bwd_w(dza, dzb, dzm, h_all, g_m):
    n = h_all.shape[0]
    tm = min(n, 1024)
    ni = n // tm

    def body(dza_ref, dzb_ref, dzm_ref, h_ref, gm_hbm, gw_ref, got_w, got_m, stage, send_sems, recv_sems):
        j, i = pl.program_id(0), pl.program_id(1)
        x, y, c = _place()
        sibling = (x, y, 1 - c)

        def send_w(q):
            return pltpu.make_async_remote_copy(
                src_ref=stage.at[q % 2], dst_ref=got_w.at[q], send_sem=send_sems.at[q], recv_sem=recv_sems.at[q],
                device_id=sibling, device_id_type=MESH)

        def send_m(q):
            return pltpu.make_async_remote_copy(
                src_ref=gm_hbm.at[2 * q + (1 - c)], dst_ref=got_m.at[q], send_sem=send_sems.at[4 + q],
                recv_sem=recv_sems.at[4 + q], device_id=sibling, device_id_type=MESH)

        @pl.when((j == 0) & (i == 0))
        def _():
            for q in range(4):
                send_m(q).start()

        @pl.when(i == 0)
        def _():
            gw_ref[...] = jnp.zeros((1, D, D), F32)

        def add(ref):
            gw_ref[0] += lax.dot_general(h_ref[...], ref[0], TN_DIMS, preferred_element_type=F32)

        pl.when(j < 2)(lambda: add(dza_ref))
        pl.when((j >= 2) & (j < 6))(lambda: add(dzb_ref))
        pl.when(j >= 6)(lambda: add(dzm_ref))

        for q in range(4):
            @pl.when((i == ni - 1) & (j == 2 * q + 1 - c))
            def _(q=q):
                if q >= 2:
                    send_w(q - 2).wait_send()
                stage[q % 2] = gw_ref[0].astype(BF16)
                send_w(q).start()

        @pl.when((j == NB - 1) & (i == ni - 1))
        def _():
            for q in (2, 3):
                send_w(q).wait_send()
            for q in range(4):
                send_w(q).wait_recv()
                send_m(q).wait_send()
                send_m(q).wait_recv()

    return pl.pallas_call(
        body, name="inproj_bwd_w", grid=(NB, ni),
        in_specs=_dz_specs(tm, ni, False) + [pl.BlockSpec((tm, D), lambda j, i: (i, 0)), ANY],
        out_specs=[pl.BlockSpec((1, D, D), lambda j, i: (j, 0, 0)), ANY, ANY],
        out_shape=[SDS((NB, D, D), F32), SDS((4, D, D), BF16), SDS((4,) + g_m.shape[1:], F32)],
        scratch_shapes=[pltpu.VMEM((2, D, D), BF16), pltpu.SemaphoreType.DMA((8,)), pltpu.SemaphoreType.DMA((8,))],
        compiler_params=_params(48),
    )(dza, dzb, dzm, h_all, g_m)


def _adamw(w, g, m, v):
    rows, cols = w.shape
    tr = _row_tile(rows)

    def body(w_ref, g_ref, m_ref, v_ref, d_ref, nm_ref, nv_ref):
        gv = g_ref[...]
        nm = ADAM_B1 * m_ref[...] + (1.0 - ADAM_B1) * gv
        nv = ADAM_B2 * v_ref[...] + (1.0 - ADAM_B2) * (gv * gv)
        m_hat = nm / (1.0 - ADAM_B1 ** ADAM_STEP)
        v_hat = nv / (1.0 - ADAM_B2 ** ADAM_STEP)
        d_ref[...] = -ADAM_LR * (m_hat / (jnp.sqrt(v_hat) + ADAM_EPS) + ADAM_WD * w_ref[...])
        nm_ref[...] = nm
        nv_ref[...] = nv

    spec = pl.BlockSpec((tr, cols), lambda i: (i, 0))
    return pl.pallas_call(
        body, name="adamw", grid=(rows // tr,), in_specs=[spec] * 4, out_specs=[spec] * 3,
        out_shape=[SDS((rows, cols), F32)] * 3, compiler_params=_params(32),
    )(w, g, m, v)


def _allgather(blocks, dtypes, name):
    na = len(blocks)

    def body(*refs):
        ins, outs, stages = refs[:na], refs[na:2 * na], refs[2 * na:3 * na]
        send_sems, recv_sems, local_sems = refs[3 * na:]
        x, y, c = _place()
        me, sibling = (x, y, c), (x, y, 1 - c)
        chips = [(1 - x, y), (x, 1 - y), (1 - x, 1 - y)]
        blk = lambda p: 4 * p[0] + 2 * p[1] + p[2]

        def copy(a, k, block, to, src=None):
            return pltpu.make_async_remote_copy(
                src_ref=outs[a].at[blk(block)] if src is None else src, dst_ref=outs[a].at[blk(block)],
                send_sem=send_sems.at[7 * a + k], recv_sem=recv_sems.at[7 * a + k],
                device_id=to, device_id_type=MESH)

        mine, first, passed = [], [], []
        for a in range(na):
            stages[a][...] = ins[a][...].astype(dtypes[a])
            mine.append(pltpu.make_async_copy(stages[a], outs[a].at[blk(me)], local_sems.at[a]))
            mine[-1].start()
            first.append(copy(a, 0, me, sibling, src=stages[a]))
            first += [copy(a, 1 + j, me, (*chip, c), src=stages[a]) for j, chip in enumerate(chips)]
        for cp in first:
            cp.start()
        for j, chip in enumerate(chips):
            for a in range(na):
                copy(a, 1 + j, (*chip, c), me).wait_recv()
                passed.append(copy(a, 4 + j, (*chip, c), sibling))
                passed[-1].start()
        for a in range(na):
            copy(a, 0, sibling, me).wait_recv()
            for j, chip in enumerate(chips):
                copy(a, 4 + j, (*chip, 1 - c), me).wait_recv()
        for cp in first + passed:
            cp.wait_send()
        for cp in mine:
            cp.wait()

    return pl.pallas_call(
        body, name=name,
        in_specs=[pl.BlockSpec(memory_space=pltpu.VMEM)] * na, out_specs=[ANY] * na,
        out_shape=[SDS((NB,) + b.shape, dt) for b, dt in zip(blocks, dtypes)],
        scratch_shapes=[pltpu.VMEM(b.shape, dt) for b, dt in zip(blocks, dtypes)]
        + [pltpu.SemaphoreType.DMA((7 * na,)), pltpu.SemaphoreType.DMA((7 * na,)), pltpu.SemaphoreType.DMA((na,))],
        compiler_params=_params(40),
    )(*blocks)


HBM = pl.BlockSpec(memory_space=pltpu.HBM)
SEMS = pl.BlockSpec(memory_space=pltpu.SEMAPHORE)
EFFECT = pltpu.SideEffectType.DATAFLOW_SIDE_EFFECTING


def _chip_copies(srcs, lands, send_sems, recv_sems):
    x, y, c = _place()
    return [pltpu.make_async_remote_copy(
        src_ref=srcs[a].at[slot], dst_ref=lands[a].at[slot],
        send_sem=send_sems.at[3 * a + slot], recv_sem=recv_sems.at[3 * a + slot],
        device_id=(px, py, c), device_id_type=MESH)
        for a in range(len(srcs)) for slot, (px, py) in enumerate(_other_chips(x, y))]


def _rs_chips_start(ps):
    na = len(ps)

    def body(*refs):
        srcs, lands = refs[:na], refs[na:2 * na]
        send_sems, recv_sems = refs[2 * na], refs[2 * na + 1]
        token = refs[-1]
        for cp in _chip_copies(srcs, lands, send_sems, recv_sems):
            cp.start()
        token[...] = jnp.zeros_like(token)

    hbm = lambda a: pltpu.HBM(a.shape, a.dtype)
    out = pl.pallas_call(
        body, name="rs_chips_start",
        out_shape=(pltpu.SemaphoreType.DMA((3 * na,)), pltpu.SemaphoreType.DMA((3 * na,)),
                   *[hbm(p) for p in ps], *[hbm(p) for p in ps], SDS((8, BD), F32)),
        in_specs=[HBM] * (2 * na),
        out_specs=(SEMS, SEMS, *[HBM] * (2 * na), pl.BlockSpec(memory_space=pltpu.VMEM)),
        input_output_aliases={i: 2 + i for i in range(2 * na)},
        compiler_params=pltpu.CompilerParams(has_side_effects=EFFECT),
    )(*[pltpu.with_memory_space_constraint(p, pltpu.HBM) for p in ps],
      *[pltpu.with_memory_space_constraint(lax.empty(p.shape, p.dtype), pltpu.HBM) for p in ps])
    return out[0], out[1], out[2:2 + na], out[2 + na:2 + 2 * na], out[-1]


def _rs_chips_wait(send_sems, recv_sems, srcs, lands, after):
    na = len(srcs)

    def body(*refs):
        srcs_r, lands_r = refs[:na], refs[na:2 * na]
        send_r, recv_r = refs[2 * na], refs[2 * na + 1]
        copies = _chip_copies(srcs_r, lands_r, send_r, recv_r)
        for cp in copies:
            cp.wait_send()
        for cp in copies:
            cp.wait_recv()

    hbm = lambda a: pltpu.HBM(a.shape, a.dtype)
    out = pl.pallas_call(
        body, name="rs_chips_wait",
        out_shape=(*[hbm(p) for p in srcs], *[hbm(p) for p in lands]),
        in_specs=[HBM] * (2 * na) + [SEMS, SEMS, ANY],
        out_specs=tuple([HBM] * (2 * na)),
        input_output_aliases={i: i for i in range(2 * na)},
        compiler_params=pltpu.CompilerParams(has_side_effects=EFFECT),
    )(*srcs, *lands, send_sems, recv_sems, after)
    return out[na:]


def _add_sibling(place, g, a_in):
    _, r, cols = g.shape
    tr = _row_tile(r)

    def chip(k, pr):
        qx = pr[0] if k in (1, 3) else 1 - pr[0]
        qy = pr[1] if k in (0, 3) else 1 - pr[1]
        return 2 * qx + qy

    def body(place_ref, *refs):
        g_refs, a_refs, (out_ref, own_ref) = refs[0:4], refs[4:8], refs[8:10]
        for k in range(3):
            out_ref[k] = (g_refs[k][0] + a_refs[k][0].astype(F32)).astype(BF16)
        own_ref[...] = g_refs[3][0] + a_refs[3][0].astype(F32)

    mine = lambda k: pl.BlockSpec((1, tr, cols), lambda i, pr: (2 * chip(k, pr) + pr[2], i, 0))
    theirs = lambda k: pl.BlockSpec((1, tr, cols), lambda i, pr: (chip(k, pr), i, 0))
    return pl.pallas_call(
        body, name="add_sibling",
        grid_spec=pltpu.PrefetchScalarGridSpec(
            num_scalar_prefetch=1, grid=(r // tr,),
            in_specs=[mine(k) for k in range(4)] + [theirs(k) for k in range(4)],
            out_specs=[pl.BlockSpec((3, tr, cols), lambda i, pr: (0, i, 0)),
                       pl.BlockSpec((tr, cols), lambda i, pr: (i, 0))]),
        out_shape=[SDS((3, r, cols), BF16), SDS((r, cols), F32)], compiler_params=_params(48),
    )(place, *[g] * 4, *[a_in] * 4)


def _add_chips(own, b_in):
    r, cols = own.shape
    tr = _row_tile(r)

    def body(p_ref, b0_ref, b1_ref, b2_ref, o_ref):
        o_ref[...] = ((p_ref[...] + b0_ref[0].astype(F32)) + b1_ref[0].astype(F32)) + b2_ref[0].astype(F32)

    slot = lambda k: pl.BlockSpec((1, tr, cols), lambda i: (k, i, 0))
    spec = pl.BlockSpec((tr, cols), lambda i: (i, 0))
    return pl.pallas_call(
        body, name="add_chips", grid=(r // tr,), in_specs=[spec, slot(0), slot(1), slot(2)], out_specs=spec,
        out_shape=SDS((r, cols), F32), compiler_params=_params(32),
    )(own, b_in, b_in, b_in)


VEC_NAMES = ("b_merge", "conv_b", "rg_bx", "rg_ba", "rg_lambda", "hg_lb_logits", "hg_norm_g", "final_norm_g")
REP_NAMES = ("rg_wx", "rg_wa", "norm_g") + VEC_NAMES
SMALL_AT = 3 * BD
SMALL_ROWS = 48
MID_ROWS = 448


def _sum_blocks(parts):
    def body(p_ref, o_ref):
        acc = p_ref[0]
        for k in range(1, NB):
            acc = acc + p_ref[k]
        o_ref[...] = acc

    return pl.pallas_call(body, name="sum_blocks", out_shape=SDS(parts.shape[1:], F32))(parts)


def _pack_rows(arrays, width, row_multiple=8):
    flat = jnp.concatenate([a.reshape(-1) for a in arrays])
    rows = -(-flat.shape[0] // width)
    rows = -(-rows // row_multiple) * row_multiple
    return jnp.pad(flat, (0, rows * width - flat.shape[0])).reshape(rows, width)


def _unpack(flat, like):
    out, off = [], 0
    for a in like:
        out.append(flat[off:off + a.size].reshape(a.shape))
        off += a.size
    return out


def kernel(x, w_in, b_merge, conv_w, conv_b, rg_wx, rg_bx, rg_wa, rg_ba, rg_lambda, hg_lb_logits, hg_norm_g, proj_a, proj_b, w_out, norm_g, final_norm_g, loss_target, m_w_in, m_b_merge, m_conv_w, m_conv_b, m_rg_wx, m_rg_bx, m_rg_wa, m_rg_ba, m_rg_lambda, m_hg_lb_logits, m_hg_norm_g, m_proj_a, m_proj_b, m_w_out, m_norm_g, m_final_norm_g, v_w_in, v_b_merge, v_conv_w, v_conv_b, v_rg_wx, v_rg_bx, v_rg_wa, v_rg_ba, v_rg_lambda, v_hg_lb_logits, v_hg_norm_g, v_proj_a, v_proj_b, v_w_out, v_norm_g, v_final_norm_g):
    weights = dict(w_in=w_in, b_merge=b_merge, conv_w=conv_w, conv_b=conv_b, rg_wx=rg_wx, rg_bx=rg_bx, rg_wa=rg_wa,
                   rg_ba=rg_ba, rg_lambda=rg_lambda, hg_lb_logits=hg_lb_logits, hg_norm_g=hg_norm_g, proj_a=proj_a,
                   proj_b=proj_b, w_out=w_out, norm_g=norm_g, final_norm_g=final_norm_g)
    mom1 = dict(w_in=m_w_in, b_merge=m_b_merge, conv_w=m_conv_w, conv_b=m_conv_b, rg_wx=m_rg_wx, rg_bx=m_rg_bx,
                rg_wa=m_rg_wa, rg_ba=m_rg_ba, rg_lambda=m_rg_lambda, hg_lb_logits=m_hg_lb_logits,
                hg_norm_g=m_hg_norm_g, proj_a=m_proj_a, proj_b=m_proj_b, w_out=m_w_out, norm_g=m_norm_g,
                final_norm_g=m_final_norm_g)
    mom2 = dict(w_in=v_w_in, b_merge=v_b_merge, conv_w=v_conv_w, conv_b=v_conv_b, rg_wx=v_rg_wx, rg_bx=v_rg_bx,
                rg_wa=v_rg_wa, rg_ba=v_rg_ba, rg_lambda=v_rg_lambda, hg_lb_logits=v_hg_lb_logits,
                hg_norm_g=v_hg_norm_g, proj_a=v_proj_a, proj_b=v_proj_b, w_out=v_w_out, norm_g=v_norm_g,
                final_norm_g=v_final_norm_g)
    order = list(weights)
    nb, s_len, _ = x.shape
    n = nb * s_len
    px, py, pc = _place()
    place = jnp.stack([px, py, pc]).astype(jnp.int32)

    x2 = x.reshape(n, D)
    cw_blk = jnp.pad(conv_w[0], ((0, 4), (0, 0)))
    order_ids = jnp.stack([_block_id(p) for p in _arrival_order(px, py, pc)]).astype(jnp.int32)
    z, h_all, w_all, pa_all, pb_all, wo_all, cw_all = _gather_inproj(
        order_ids, x2, norm_g, [w_in[0], proj_a[0], proj_b[0], w_out[0], cw_blk], [BF16, BF16, BF16, BF16, F32])
    pa_full, pb_full, wo_full = (a.reshape(D, D) for a in (pa_all, pb_all, wo_all))
    cw8 = cw_all.transpose(1, 0, 2).reshape(8, D)
    wx_b, wa_b = rg_wx[0].astype(BF16), rg_wa[0].astype(BF16)
    cb, bx, ba = conv_b, rg_bx.reshape(1, D), rg_ba.reshape(1, D)
    fin_g = final_norm_g.reshape(1, D)

    hlru, ya = _lru_fwd(z, cw8, cb, wx_b, wa_b, bx, ba, rg_lambda, nb, s_len)
    o_all, yb, st_all = _hgrn_fwd(z, hg_lb_logits, hg_norm_g, nb, s_len)

    (dx2, dya, dyb, dzm, loss_acc, g_fin, g_bm, g_pa, g_pb, g_wo) = _mid(
        ya, yb, z, b_merge, x2, loss_target.reshape(n, D), fin_g, pa_full, pb_full, wo_full)
    dzb, g_lg, g_hg = _hgrn_bwd(z, o_all, st_all, dyb, hg_lb_logits, hg_norm_g, nb, s_len)
    dza, g_cw8, g_cb, g_wx, g_wa, g_bx, g_ba, g_lam = _lru_bwd(
        z, hlru, dya, cw8, cb, wx_b, wa_b, bx, ba, rg_lambda, nb, s_len)

    part = dict(b_merge=g_bm, conv_b=g_cb, rg_bx=g_bx, rg_ba=g_ba, rg_lambda=g_lam, hg_lb_logits=g_lg,
                hg_norm_g=g_hg, final_norm_g=g_fin)
    vec = _pack_rows([part[k] for k in VEC_NAMES], BD)
    vec = jnp.pad(vec, ((0, 16 * NB - vec.shape[0]), (0, 0))).reshape(NB, 2, D)
    rows8 = lambda a: jnp.pad(a, ((0, 0), (0, 8 - a.shape[1]), (0, 0)))
    g_m = jnp.concatenate([g.reshape(NB, BD, D) for g in (g_pa, g_pb, g_wo)]
                          + [g_wx.reshape(NB, 16, D), g_wa.reshape(NB, 16, D),
                             rows8(g_cw8.reshape(8, NB, BD).transpose(1, 0, 2).reshape(NB, 1, D)), rows8(vec),
                             jnp.zeros((NB, MID_ROWS - SMALL_AT - SMALL_ROWS, D), F32)], axis=1)
    g_w, w_from_sibling, m_from_sibling = _inproj_bwd_w(dza, dzb, dzm, h_all, g_m)
    w_out_bf, w_own = _add_sibling(place, g_w, w_from_sibling)
    m_out_bf, m_own = _add_sibling(place, g_m, m_from_sibling)
    send_sems, recv_sems, srcs, lands, token = _rs_chips_start([w_out_bf, m_out_bf])
    grad_x, g_ng = _inproj_bwd_x(dza, dzb, dzm, w_all, x2, dx2, norm_g, token)
    from_chips = _rs_chips_wait(send_sems, recv_sems, srcs, lands, grad_x)
    r_w = _add_chips(w_own, from_chips[0])
    r_m = _add_chips(m_own, from_chips[1])
    row = lax.broadcasted_iota(jnp.int32, (8, D), 0)
    mine = jnp.where(row == 0, g_ng, jnp.where(row == 1, loss_acc[0:1, 0:1], 0.0))
    tail = jnp.concatenate([r_m[SMALL_AT:SMALL_AT + SMALL_ROWS], mine], axis=0)
    (tail_all,) = _allgather([tail], [F32], "gather_small_grads")
    summed = _sum_blocks(tail_all[:, SMALL_ROWS:SMALL_ROWS + 8])

    grads = dict(w_in=r_w.reshape(1, D, D),
                 proj_a=r_m[0:BD].reshape(1, BD, D), proj_b=r_m[BD:2 * BD].reshape(1, BD, D),
                 w_out=r_m[2 * BD:3 * BD].reshape(1, BD, D),
                 conv_w=r_m[SMALL_AT + 32].reshape(8, BD)[0:4].reshape(1, 4, BD),
                 rg_wx=tail_all[:, 0:16].reshape(1, NB, BD, BD), rg_wa=tail_all[:, 16:32].reshape(1, NB, BD, BD),
                 norm_g=summed[0:1])
    vec_all = tail_all[:, 40:42].reshape(-1)
    for k, gk in zip(VEC_NAMES, _unpack(vec_all, [weights[k] for k in VEC_NAMES])):
        grads[k] = gk

    delta, new_m, new_v = {}, {}, {}
    for k in ("w_in", "proj_a", "proj_b", "w_out"):
        shp = weights[k].shape
        two = lambda a: a.reshape(shp[1], shp[2])
        d_k, m_k, v_k = _adamw(two(weights[k]), two(grads[k]), two(mom1[k]), two(mom2[k]))
        delta[k], new_m[k], new_v[k] = d_k.reshape(shp), m_k.reshape(shp), v_k.reshape(shp)
    rep = list(REP_NAMES) + ["conv_w"]
    packs = [_pack_rows([t[k] for k in rep], BD, 256) for t in (weights, grads, mom1, mom2)]
    outs = _adamw(*packs)
    for tgt, flat in zip((delta, new_m, new_v), outs):
        for k, a in zip(rep, _unpack(flat.reshape(-1), [weights[k] for k in rep])):
            tgt[k] = a

    return (summed[1, 0],grad_x.reshape(x.shape), *[grads[k] for k in order], *[delta[k] for k in order],
            *[new_m[k] for k in order], *[new_v[k] for k in order])
```

```python
import jax
import jax.numpy as jnp
from jax import lax
from jax.experimental import pallas as pl
from jax.experimental.pallas import tpu as pltpu

F32 = jnp.float32
BF16 = jnp.bfloat16
SDS = jax.ShapeDtypeStruct
MESH = pl.DeviceIdType.MESH
ANY = pl.BlockSpec(memory_space=pl.ANY)

D = 1024
NB = 8
BD = D // NB
CHUNK = 64
EPS = 1e-6
LRU_C = 8.0
HG_SCALE = BD ** -0.5
ADAM_LR, ADAM_B1, ADAM_B2, ADAM_EPS, ADAM_WD, ADAM_STEP = 0.001, 0.9, 0.999, 1e-08, 0.01, 10

NT_DIMS = (((1,), (1,)), ((), ()))
TN_DIMS = (((0,), (0,)), ((), ()))


def _params(vmem_mib):
    return pltpu.CompilerParams(vmem_limit_bytes=vmem_mib << 20)


def _row_tile(rows, most=256):
    assert rows % 8 == 0
    return max(t for t in range(8, min(rows, most) + 1, 8) if rows % t == 0)


def _sigmoid(v):
    return 0.5 * (jnp.tanh(0.5 * v) + 1.0)


def _groups(v):
    return v.reshape(v.shape[0] // 8, 8, v.shape[1])


def _softplus_neg(lam):
    t = -lam
    e = jnp.exp(-jnp.abs(t))
    w = 1.0 + e
    d = w - 1.0
    l1p = jnp.where(d == 0.0, e, jnp.log(w) * (e / jnp.where(d == 0.0, 1.0, d)))
    return jnp.maximum(t, 0.0) + l1p


def _place():
    return lax.axis_index("x"), lax.axis_index("y"), lax.axis_index("c")


def _other_chips(x, y):
    return [(1 - x, y), (x, 1 - y), (1 - x, 1 - y)]


def _block_id(p):
    return 4 * p[0] + 2 * p[1] + p[2]


def _arrival_order(x, y, c):
    near, far, diag = _other_chips(x, y)
    return [(x, y, c), (x, y, 1 - c), (*near, c), (*far, c), (*near, 1 - c), (*far, 1 - c), (*diag, c), (*diag, 1 - c)]


def _gather_inproj(order_ids, x2, norm_g, blocks, dtypes):
    na = len(blocks)
    n = x2.shape[0]
    tm = min(n, 1024)
    ni = n // tm

    def body(order_ref, x_ref, g_ref, *refs):
        ins, (z_ref, h_ref), outs = refs[:na], refs[na:na + 2], refs[na + 2:2 * na + 2]
        stages = refs[2 * na + 2:3 * na + 2]
        h_full, wbuf, send_sems, recv_sems, local_sems, wsem, hsem = refs[3 * na + 2:]
        j, i = pl.program_id(0), pl.program_id(1)
        x, y, c = _place()
        me, sibling = (x, y, c), (x, y, 1 - c)
        chips = _other_chips(x, y)
        small = range(1, na)

        def copy(a, k, block, to, src=None):
            return pltpu.make_async_remote_copy(
                src_ref=outs[a].at[_block_id(block)] if src is None else src, dst_ref=outs[a].at[_block_id(block)],
                send_sem=send_sems.at[7 * a + k], recv_sem=recv_sems.at[7 * a + k],
                device_id=to, device_id_type=MESH)

        def local(a):
            return pltpu.make_async_copy(stages[a], outs[a].at[_block_id(me)], local_sems.at[a])

        def landed(a, slot):
            copy(a, 1 + slot, (*chips[slot], c), me).wait_recv()
            copy(a, 4 + slot, (*chips[slot], c), sibling).start()

        def passed_on(a, slot):
            copy(a, 4 + slot, (*chips[slot], 1 - c), me).wait_recv()

        @pl.when((j == 0) & (i == 0))
        def _():
            for a in range(na):
                stages[a][...] = ins[a][...].astype(dtypes[a])
                local(a).start()
            for a in range(na):
                copy(a, 0, me, sibling, src=stages[a]).start()
                for slot, chip in enumerate(chips):
                    copy(a, 1 + slot, me, (*chip, c), src=stages[a]).start()

        @pl.when(j == 0)
        def _():
            xv = x_ref[...]
            r = lax.rsqrt(jnp.mean(xv * xv, axis=-1, keepdims=True) + EPS)
            hb = ((xv * r) * g_ref[...]).astype(BF16)
            h_full[pl.ds(pl.multiple_of(i * tm, tm), tm), :] = hb

        save_h = pltpu.make_async_copy(h_full, h_ref, hsem)
        pl.when((j == 0) & (i == ni - 1))(save_h.start)

        steps = [
            lambda: local(0).wait(),
            lambda: copy(0, 0, sibling, me).wait_recv(),
            lambda: landed(0, 0),
            lambda: landed(0, 1),
            lambda: passed_on(0, 0),
            lambda: passed_on(0, 1),
            lambda: landed(0, 2),
            lambda: passed_on(0, 2),
        ]
        for k, step in enumerate(steps):
            pl.when((i == 0) & (j == k))(step)

        @pl.when(i == 0)
        def _():
            load = pltpu.make_async_copy(outs[0].at[order_ref[j]], wbuf, wsem)
            load.start()
            load.wait()

        z_ref[0] = jnp.dot(h_full[pl.ds(pl.multiple_of(i * tm, tm), tm), :], wbuf[...], preferred_element_type=F32)

        @pl.when((j == NB - 1) & (i == ni - 1))
        def _():
            save_h.wait()
            for slot in range(3):
                for a in small:
                    landed(a, slot)
            for a in small:
                local(a).wait()
                copy(a, 0, sibling, me).wait_recv()
                for slot in range(3):
                    passed_on(a, slot)
            for a in range(na):
                copy(a, 0, me, sibling, src=stages[a]).wait_send()
                for slot, chip in enumerate(chips):
                    copy(a, 1 + slot, me, (*chip, c), src=stages[a]).wait_send()
                    copy(a, 4 + slot, (*chip, c), sibling).wait_send()

    rows_once = lambda j, i, order: (jnp.where(j == 0, i, ni - 1), 0)
    vmem = pl.BlockSpec(memory_space=pltpu.VMEM)
    return pl.pallas_call(
        body, name="gather_inproj",
        grid_spec=pltpu.PrefetchScalarGridSpec(
            num_scalar_prefetch=1, grid=(NB, ni),
            in_specs=[pl.BlockSpec((tm, D), rows_once), pl.BlockSpec((1, D), lambda j, i, order: (0, 0))] + [vmem] * na,
            out_specs=[pl.BlockSpec((1, tm, D), lambda j, i, order: (order[j], i, 0)), ANY] + [ANY] * na,
            scratch_shapes=[pltpu.VMEM(b.shape, dt) for b, dt in zip(blocks, dtypes)]
            + [pltpu.VMEM((n, D), BF16), pltpu.VMEM((D, D), BF16),
               pltpu.SemaphoreType.DMA((7 * na,)), pltpu.SemaphoreType.DMA((7 * na,)),
               pltpu.SemaphoreType.DMA((na,)), pltpu.SemaphoreType.DMA(()), pltpu.SemaphoreType.DMA(())]),
        out_shape=[SDS((NB, n, D), F32), SDS((n, D), BF16)] + [SDS((NB,) + b.shape, dt) for b, dt in zip(blocks, dtypes)],
        compiler_params=_params(56),
    )(order_ids, x2, norm_g, *blocks)


LRU_T = 256


def _conv(ext, cw, cb):
    t = LRU_T
    acc = ext[5:5 + t, :] * cw[0:1, :] + ext[6:6 + t, :] * cw[1:2, :]
    acc = acc + ext[7:7 + t, :] * cw[2:3, :]
    acc = acc + ext[8:8 + t, :] * cw[3:4, :]
    return cb + acc


def _lru_gates(xa, wx_ref, wa_ref, bx, ba, lam):
    xab = xa.astype(BF16)
    pis, prs = [], []
    for h in range(NB):
        xs = xab[:, h * BD:(h + 1) * BD]
        pis.append(jnp.dot(xs, wx_ref[h], preferred_element_type=F32))
        prs.append(jnp.dot(xs, wa_ref[h], preferred_element_type=F32))
    gi = _sigmoid(jnp.concatenate(pis, axis=1) + bx)
    gr = _sigmoid(jnp.concatenate(prs, axis=1) + ba)
    sp = _softplus_neg(lam)
    log_a = (-LRU_C * gr) * sp
    a = jnp.exp(log_a)
    mult = jnp.sqrt(-jnp.tanh(log_a) * (a * a + 1.0))
    return xab, gi, gr, sp, a, mult


def _lru_fwd(z, cw8, cb, wx, wa, bx, ba, lam, nb, s_len):
    n = nb * s_len
    t = LRU_T
    ns = s_len // t

    def body(xp_ref, ga_ref, cw_ref, cb_ref, wx_ref, wa_ref, bx_ref, ba_ref, lam_ref,
             h_ref, ya_ref, ext, a_s, u_s, carry):
        @pl.when(pl.program_id(1) == 0)
        def _():
            ext[0:8, :] = jnp.zeros((8, D), F32)
            carry[...] = jnp.zeros((8, D), F32)

        ext[8:8 + t, :] = xp_ref[0]
        xa = _conv(ext, cw_ref[...], cb_ref[...])
        ext[0:8, :] = ext[t:t + 8, :]
        _, gi, _, _, a, mult = _lru_gates(xa, wx_ref, wa_ref, bx_ref[...], ba_ref[...], lam_ref[...])
        u = (mult * gi) * xa
        a, u = _groups(a), _groups(u)
        row = lax.broadcasted_iota(jnp.int32, a.shape, 1)
        for sh in (1, 2, 4):
            a_sh = pltpu.roll(a, sh, 1)
            u_sh = pltpu.roll(u, sh, 1)
            m = row >= sh
            u = jnp.where(m, a * u_sh + u, u)
            a = jnp.where(m, a * a_sh, a)
        a_s[...] = a.reshape(t, D)
        u_s[...] = u.reshape(t, D)

        def step(g, c):
            r = pl.multiple_of(g * 8, 8)
            hg = u_s[pl.ds(r, 8), :] + a_s[pl.ds(r, 8), :] * c
            h_ref[pl.ds(r, 8), :] = hg
            return hg[7:8, :]

        c_out = lax.fori_loop(0, t // 8, step, carry[0:1, :], unroll=4)
        carry[0:1, :] = c_out
        ga = ga_ref[0]
        ya_ref[...] = (h_ref[...] * (ga * _sigmoid(ga))).astype(BF16)

    row_map = lambda b, s: (b * ns + s, 0)
    rep2 = lambda b, s: (0, 0)
    rep3 = lambda b, s: (0, 0, 0)
    return pl.pallas_call(
        body, name="lru_fwd", grid=(nb, ns),
        in_specs=[pl.BlockSpec((1, t, D), lambda b, s: (0, b * ns + s, 0)),
                  pl.BlockSpec((1, t, D), lambda b, s: (1, b * ns + s, 0)),
                  pl.BlockSpec((8, D), rep2), pl.BlockSpec((1, D), rep2),
                  pl.BlockSpec((NB, BD, BD), rep3), pl.BlockSpec((NB, BD, BD), rep3),
                  pl.BlockSpec((1, D), rep2), pl.BlockSpec((1, D), rep2), pl.BlockSpec((1, D), rep2)],
        out_specs=[pl.BlockSpec((t, D), row_map), pl.BlockSpec((t, D), row_map)],
        out_shape=[SDS((n, D), F32), SDS((n, D), BF16)],
        scratch_shapes=[pltpu.VMEM((t + 8, D), F32), pltpu.VMEM((t, D), F32), pltpu.VMEM((t, D), F32),
                        pltpu.VMEM((8, D), F32)],
        compiler_params=_params(48),
    )(z, z, cw8, cb, wx, wa, bx, ba, lam)


def _lru_bwd(z, h_all, dya, cw8, cb, wx, wa, bx, ba, lam, nb, s_len):
    n = nb * s_len
    t = LRU_T
    ns = s_len // t
    t8 = t // 8

    def body(xp_ref, xph_ref, ga_ref, h_ref, hh_ref, dya_ref, cw_ref, cb_ref, wx_ref, wa_ref, bx_ref, ba_ref,
             lam_ref, dz_ref, gcw_ref, gcb_ref, gwx_ref, gwa_ref, gbx_ref, gba_ref, glam_ref,
             ext, hext, dext, a_s, u_s, dh_s, carry):
        b, s = pl.program_id(0), pl.program_id(1)
        first_tile = s == ns - 1

        @pl.when((b == 0) & (s == 0))
        def _():
            for ref in (gcw_ref, gcb_ref, gwx_ref, gwa_ref, gbx_ref, gba_ref, glam_ref):
                ref[...] = jnp.zeros(ref.shape, F32)

        @pl.when(s == 0)
        def _():
            dext[t:t + 8, :] = jnp.zeros((8, D), F32)
            carry[...] = jnp.zeros((8, D), F32)

        keep = jnp.where(first_tile, 0.0, 1.0)
        ext[0:8, :] = xph_ref[0] * keep
        ext[8:8 + t, :] = xp_ref[0]
        hext[0:8, :] = hh_ref[...] * keep
        hext[8:8 + t, :] = h_ref[...]
        cw = cw_ref[...]
        lam = lam_ref[...]
        xa = _conv(ext, cw, cb_ref[...])
        xab, gi, gr, sp, a, mult = _lru_gates(xa, wx_ref, wa_ref, bx_ref[...], ba_ref[...], lam)
        h_prev = hext[7:7 + t, :]
        ga = ga_ref[0]
        sg = _sigmoid(ga)
        dya_v = dya_ref[...]
        d_ga = dya_v * h_ref[...] * (sg * (1.0 + ga * (1.0 - sg)))
        g_in = dya_v * (ga * sg)

        rows = lax.broadcasted_iota(jnp.int32, (t, D), 0)
        an = _groups(jnp.where(rows == t - 1, 1.0, pltpu.roll(a, t - 1, 0)))
        u = _groups(g_in)
        row = lax.broadcasted_iota(jnp.int32, an.shape, 1)
        for sh in (1, 2, 4):
            a_sh = pltpu.roll(an, 8 - sh, 1)
            u_sh = pltpu.roll(u, 8 - sh, 1)
            m = row < 8 - sh
            u = jnp.where(m, u + an * u_sh, u)
            an = jnp.where(m, an * a_sh, an)
        a_s[...] = an.reshape(t, D)
        u_s[...] = u.reshape(t, D)

        def step(i, c):
            r = pl.multiple_of((t8 - 1 - i) * 8, 8)
            dg = u_s[pl.ds(r, 8), :] + a_s[pl.ds(r, 8), :] * c
            dh_s[pl.ds(r, 8), :] = dg
            return dg[0:1, :]

        lax.fori_loop(0, t8, step, carry[0:1, :], unroll=4)
        dh = dh_s[...]
        carry[0:1, :] = a[0:1, :] * dh[0:1, :]

        d_a = dh * h_prev
        dux = dh * xa
        d_mult = dux * gi
        d_gi = dux * mult
        d_xa = dh * (mult * gi)
        d_loga = d_a * a - d_mult * ((a * a) / mult)
        d_gr = d_loga * (-LRU_C * sp)
        d_sp = jnp.sum(d_loga * (-LRU_C * gr), axis=0, keepdims=True)
        glam_ref[...] += d_sp * (-_sigmoid(-lam))
        d_pi = d_gi * gi * (1.0 - gi)
        d_pr = d_gr * gr * (1.0 - gr)
        gbx_ref[...] += jnp.sum(d_pi, axis=0, keepdims=True)
        gba_ref[...] += jnp.sum(d_pr, axis=0, keepdims=True)
        dpib = d_pi.astype(BF16)
        dprb = d_pr.astype(BF16)
        back = []
        for h in range(NB):
            cs = slice(h * BD, (h + 1) * BD)
            gwx_ref[h] += lax.dot_general(xab[:, cs], dpib[:, cs], TN_DIMS, preferred_element_type=F32)
            gwa_ref[h] += lax.dot_general(xab[:, cs], dprb[:, cs], TN_DIMS, preferred_element_type=F32)
            back.append(lax.dot_general(dpib[:, cs], wx_ref[h], NT_DIMS, preferred_element_type=F32)
                        + lax.dot_general(dprb[:, cs], wa_ref[h], NT_DIMS, preferred_element_type=F32))
        d_xa = d_xa + jnp.concatenate(back, axis=1)

        dext[0:t, :] = d_xa
        d_xp = dext[3:3 + t, :] * cw[0:1, :] + dext[2:2 + t, :] * cw[1:2, :]
        d_xp = d_xp + dext[1:1 + t, :] * cw[2:3, :]
        d_xp = d_xp + d_xa * cw[3:4, :]
        dext[t:t + 8, :] = d_xa[0:8, :]
        gcb_ref[...] += jnp.sum(d_xa, axis=0, keepdims=True)
        for k in range(4):
            gcw_ref[k:k + 1, :] += jnp.sum(d_xa * ext[5 + k:5 + k + t, :], axis=0, keepdims=True)
        dz_ref[0] = d_xp.astype(BF16)
        dz_ref[1] = d_ga.astype(BF16)

    rb = lambda b, s: b * ns + (ns - 1 - s)
    halo = lambda b, s: jnp.maximum(rb(b, s) * t8 - 1, 0)
    rep2 = lambda b, s: (0, 0)
    rep3 = lambda b, s: (0, 0, 0)
    return pl.pallas_call(
        body, name="lru_bwd", grid=(nb, ns),
        in_specs=[pl.BlockSpec((1, t, D), lambda b, s: (0, rb(b, s), 0)),
                  pl.BlockSpec((1, 8, D), lambda b, s: (0, halo(b, s), 0)),
                  pl.BlockSpec((1, t, D), lambda b, s: (1, rb(b, s), 0)),
                  pl.BlockSpec((t, D), lambda b, s: (rb(b, s), 0)),
                  pl.BlockSpec((8, D), lambda b, s: (halo(b, s), 0)),
                  pl.BlockSpec((t, D), lambda b, s: (rb(b, s), 0)),
                  pl.BlockSpec((8, D), rep2), pl.BlockSpec((1, D), rep2),
                  pl.BlockSpec((NB, BD, BD), rep3), pl.BlockSpec((NB, BD, BD), rep3),
                  pl.BlockSpec((1, D), rep2), pl.BlockSpec((1, D), rep2), pl.BlockSpec((1, D), rep2)],
        out_specs=[pl.BlockSpec((2, t, D), lambda b, s: (0, rb(b, s), 0)),
                   pl.BlockSpec((8, D), rep2), pl.BlockSpec((1, D), rep2),
                   pl.BlockSpec((NB, BD, BD), rep3), pl.BlockSpec((NB, BD, BD), rep3),
                   pl.BlockSpec((1, D), rep2), pl.BlockSpec((1, D), rep2), pl.BlockSpec((1, D), rep2)],
        out_shape=[SDS((2, n, D), BF16), SDS((8, D), F32), SDS((1, D), F32),
                   SDS((NB, BD, BD), F32), SDS((NB, BD, BD), F32),
                   SDS((1, D), F32), SDS((1, D), F32), SDS((1, D), F32)],
        scratch_shapes=[pltpu.VMEM((t + 8, D), F32), pltpu.VMEM((t + 8, D), F32), pltpu.VMEM((t + 8, D), F32),
                        pltpu.VMEM((t, D), F32), pltpu.VMEM((t, D), F32), pltpu.VMEM((t, D), F32),
                        pltpu.VMEM((8, D), F32)],
        compiler_params=_params(56),
    )(z, z, z, h_all, h_all, dya, cw8, cb, wx, wa, bx, ba, lam)


HG_T = 512
HG_NC = HG_T // CHUNK
BNT_DIMS = (((2,), (2,)), ((0,), (0,)))
BNN_DIMS = (((2,), (1,)), ((0,), (0,)))
BTN_DIMS = (((1,), (1,)), ((0,), (0,)))


def _lower_bound(lg):
    m = jnp.max(lg, axis=0, keepdims=True)
    e = jnp.exp(lg - m)
    return e[0:1, :] / jnp.sum(e, axis=0, keepdims=True)


def _tri(upper):
    r = lax.broadcasted_iota(jnp.int32, (HG_NC, CHUNK, CHUNK), 1)
    c = lax.broadcasted_iota(jnp.int32, (HG_NC, CHUNK, CHUNK), 2)
    return (c >= r) if upper else (r >= c)


def _bdot(a, b, dims, precision=None):
    return lax.dot_general(a, b, dims, precision=precision, preferred_element_type=F32)


def _chunks(a):
    return a.reshape(HG_NC, CHUNK, BD)


def _hg_tile(q, fp, lb):
    q, fp = _chunks(q), _chunks(fp)
    sig = _sigmoid(fp)
    f = lb + (1.0 - lb) * sig
    log_f = jnp.log(f)
    k = 1.0 - f
    b = _bdot(_tri(False).astype(F32), log_f, BNN_DIMS, lax.Precision.HIGHEST)
    b_mid = b[:, CHUNK // 2:CHUNK // 2 + 1, :]
    b_last = b[:, CHUNK - 1:CHUNK, :]
    sq = _sigmoid(q)
    qh = q * sq
    e_qi = jnp.exp(b - b_mid)
    e_ki = jnp.exp(b_mid - b)
    e_qs = jnp.exp(b)
    e_ks = jnp.exp(b_last - b)
    dc = jnp.exp(b_last)
    q_in = (qh * e_qi) * HG_SCALE
    k_in = k * e_ki
    q_st = (qh * e_qs) * HG_SCALE
    k_st = k * e_ks
    att = _bdot(q_in.astype(BF16), k_in.astype(BF16), BNT_DIMS)
    att = jnp.where(_tri(False), att, 0.0)
    return dict(q=q, sig=sig, f=f, k=k, sq=sq, e_qi=e_qi, e_ki=e_ki, e_qs=e_qs, e_ks=e_ks, dc=dc,
                q_in=q_in, k_in=k_in, q_st=q_st, k_st=k_st, att=att)


def _hgrn_fwd(z, lb_logits, hg_g, nb, s_len):
    n = nb * s_len
    t = HG_T
    ns = s_len // t
    nchunk = s_len // CHUNK

    def body(q_ref, f_ref, v_ref, gb_ref, lg_ref, g_ref, o_ref, yb_ref, st_ref, st):
        @pl.when(pl.program_id(1) == 0)
        def _():
            st[...] = jnp.zeros((NB, BD, BD), F32)

        def head(h, carry):
            cols = pl.ds(pl.multiple_of(h * BD, BD), BD)
            lb = _lower_bound(lg_ref[:, cols])
            ck = _hg_tile(q_ref[0, :, cols], f_ref[0, :, cols], lb)
            vb = _chunks(v_ref[0, :, cols]).astype(BF16)
            kv = _bdot(vb, ck["k_st"].astype(BF16), BTN_DIMS)
            states = [st[h]]
            for c in range(HG_NC):
                states.append(states[c] * ck["dc"][c] + kv[c])
            st[h] = states[HG_NC]
            s_in = jnp.stack(states[:HG_NC], axis=0)
            st_ref[h] = s_in
            o = (_bdot(ck["att"].astype(BF16), vb, BNN_DIMS)
                 + _bdot(ck["q_st"].astype(BF16), s_in.astype(BF16), BNT_DIMS))
            o_ref[:, cols] = o.reshape(t, BD)
            r = lax.rsqrt(jnp.mean(o * o, axis=-1, keepdims=True) + EPS)
            gb = _chunks(gb_ref[0, :, cols])
            yb_ref[:, cols] = (((o * r) * g_ref[...]) * (gb * _sigmoid(gb))).astype(BF16).reshape(t, BD)
            return carry

        lax.fori_loop(0, NB, head, 0)

    seg = lambda j: pl.BlockSpec((1, t, D), lambda b, s: (j, b * ns + s, 0))
    tile = pl.BlockSpec((t, D), lambda b, s: (b * ns + s, 0))
    return pl.pallas_call(
        body, name="hgrn_fwd", grid=(nb, ns),
        in_specs=[seg(2), seg(3), seg(4), seg(5),
                  pl.BlockSpec((2, D), lambda b, s: (0, 0)), pl.BlockSpec((1, BD), lambda b, s: (0, 0))],
        out_specs=[tile, tile, pl.BlockSpec((NB, HG_NC, BD, BD), lambda b, s: (b, s, 0, 0))],
        out_shape=[SDS((n, D), F32), SDS((n, D), BF16), SDS((nb * NB, nchunk, BD, BD), F32)],
        scratch_shapes=[pltpu.VMEM((NB, BD, BD), F32)],
        compiler_params=_params(56),
    )(z, z, z, z, lb_logits, hg_g)


def _hgrn_bwd(z, o_all, st_all, dyb, lb_logits, hg_g, nb, s_len):
    n = nb * s_len
    t = HG_T
    ns = s_len // t

    def body(q_ref, f_ref, v_ref, gb_ref, o_ref, st_ref, dyb_ref, lg_ref, g_ref,
             dz_ref, glg_ref, ghg_ref, dst, dlb):
        b, s = pl.program_id(0), pl.program_id(1)

        @pl.when((b == 0) & (s == 0))
        def _():
            ghg_ref[...] = jnp.zeros((1, BD), F32)
            dlb[...] = jnp.zeros((8, D), F32)

        @pl.when(s == 0)
        def _():
            dst[...] = jnp.zeros((NB, BD, BD), F32)

        g = g_ref[...]

        def head(h, carry):
            cols = pl.ds(pl.multiple_of(h * BD, BD), BD)
            lb = _lower_bound(lg_ref[:, cols])
            ck = _hg_tile(q_ref[0, :, cols], f_ref[0, :, cols], lb)
            q = ck["q"]
            vb = _chunks(v_ref[0, :, cols]).astype(BF16)
            gb = _chunks(gb_ref[0, :, cols])
            o = _chunks(o_ref[:, cols])
            dyb_v = _chunks(dyb_ref[:, cols])
            s_in = st_ref[h]

            sgb = _sigmoid(gb)
            r = lax.rsqrt(jnp.mean(o * o, axis=-1, keepdims=True) + EPS)
            ohat = o * r
            d_on = dyb_v * (gb * sgb)
            d_gb = dyb_v * (ohat * g) * (sgb * (1.0 + gb * (1.0 - sgb)))
            ghg_ref[...] += jnp.sum(jnp.sum(d_on * ohat, axis=1), axis=0, keepdims=True)
            tt = d_on * g
            d_o = r * (tt - ohat * jnp.mean(tt * ohat, axis=-1, keepdims=True))
            dob = d_o.astype(BF16)

            attb = ck["att"].astype(BF16)
            q_inb, k_inb = ck["q_in"].astype(BF16), ck["k_in"].astype(BF16)
            q_stb, k_stb = ck["q_st"].astype(BF16), ck["k_st"].astype(BF16)
            d_att = jnp.where(_tri(False), _bdot(dob, vb, BNT_DIMS), 0.0).astype(BF16)
            d_q_in = _bdot(d_att, k_inb, BNN_DIMS)
            d_k_in = _bdot(d_att, q_inb, BTN_DIMS)
            d_q_st = _bdot(dob, s_in.astype(BF16), BNN_DIMS)
            qdo = _bdot(dob, q_stb, BTN_DIMS)
            d_states = [None] * HG_NC + [dst[h]]
            for c in reversed(range(HG_NC)):
                d_states[c] = d_states[c + 1] * ck["dc"][c] + qdo[c]
            dst[h] = d_states[0]
            ds_out = jnp.stack(d_states[1:], axis=0)
            dsb = ds_out.astype(BF16)
            d_v = _bdot(attb, dob, BTN_DIMS) + _bdot(k_stb, dsb, BNT_DIMS)
            d_k_st = _bdot(vb, dsb, BNN_DIMS)
            d_dc = jnp.sum(ds_out * s_in, axis=1, keepdims=True)

            p_qi = d_q_in * ck["q_in"]
            p_ki = d_k_in * ck["k_in"]
            p_qs = d_q_st * ck["q_st"]
            p_ks = d_k_st * ck["k_st"]
            d_qh = (d_q_in * ck["e_qi"] + d_q_st * ck["e_qs"]) * HG_SCALE
            d_k = d_k_in * ck["e_ki"] + d_k_st * ck["e_ks"]
            d_b = (p_qi - p_ki) + (p_qs - p_ks)
            d_b_mid = jnp.sum(p_ki - p_qi, axis=1, keepdims=True)
            d_b_last = jnp.sum(p_ks, axis=1, keepdims=True) + d_dc * ck["dc"]
            rowi = lax.broadcasted_iota(jnp.int32, (HG_NC, CHUNK, BD), 1)
            d_b = d_b + jnp.where(rowi == CHUNK // 2, d_b_mid, 0.0) + jnp.where(rowi == CHUNK - 1, d_b_last, 0.0)
            d_logf = _bdot(_tri(True).astype(F32), d_b, BNN_DIMS, lax.Precision.HIGHEST)
            d_f = d_logf / ck["f"] - d_k
            sig, sq = ck["sig"], ck["sq"]
            d_fp = d_f * (1.0 - lb) * (sig * (1.0 - sig))
            dlb[0:1, cols] += jnp.sum(jnp.sum(d_f * (1.0 - sig), axis=1), axis=0, keepdims=True)
            d_q = d_qh * (sq * (1.0 + q * (1.0 - sq)))
            dz_ref[0, :, cols] = d_q.astype(BF16).reshape(t, BD)
            dz_ref[1, :, cols] = d_fp.astype(BF16).reshape(t, BD)
            dz_ref[2, :, cols] = d_v.astype(BF16).reshape(t, BD)
            dz_ref[3, :, cols] = d_gb.astype(BF16).reshape(t, BD)
            return carry

        lax.fori_loop(0, NB, head, 0)

        @pl.when((b == nb - 1) & (s == ns - 1))
        def _():
            lb = _lower_bound(lg_ref[...])
            dl = dlb[0:1, :] * (lb * (1.0 - lb))
            glg_ref[0:1, :] = dl
            glg_ref[1:2, :] = -dl

    rb = lambda b, s: b * ns + (ns - 1 - s)
    seg = lambda j: pl.BlockSpec((1, t, D), lambda b, s: (j, rb(b, s), 0))
    tile = pl.BlockSpec((t, D), lambda b, s: (rb(b, s), 0))
    return pl.pallas_call(
        body, name="hgrn_bwd", grid=(nb, ns),
        in_specs=[seg(2), seg(3), seg(4), seg(5), tile,
                  pl.BlockSpec((NB, HG_NC, BD, BD), lambda b, s: (b, ns - 1 - s, 0, 0)),
                  tile, pl.BlockSpec((2, D), lambda b, s: (0, 0)), pl.BlockSpec((1, BD), lambda b, s: (0, 0))],
        out_specs=[pl.BlockSpec((4, t, D), lambda b, s: (0, rb(b, s), 0)),
                   pl.BlockSpec((2, D), lambda b, s: (0, 0)), pl.BlockSpec((1, BD), lambda b, s: (0, 0))],
        out_shape=[SDS((4, n, D), BF16), SDS((2, D), F32), SDS((1, BD), F32)],
        scratch_shapes=[pltpu.VMEM((NB, BD, BD), F32), pltpu.VMEM((8, D), F32)],
        compiler_params=_params(60),
    )(z, z, z, z, o_all, st_all, dyb, lb_logits, hg_g)


def _mid(ya, yb, z, b_merge, x2, tgt, fin_g, pa, pb, wo):
    n = x2.shape[0]
    tm = 256
    ni = n // tm

    def body(ya_ref, yb_ref, gma_ref, gmb_ref, bm_ref, x_ref, t_ref, fg_ref, pa_hbm, pb_hbm, wo_hbm,
             dx2_ref, dya_ref, dyb_ref, dgm_ref, loss_ref, gfg_ref, gbm_ref, gpa_hbm, gpb_hbm, gwo_hbm,
             pa_v, pb_v, wo_v, gpa_v, gpb_v, gwo_v, sem):
        i = pl.program_id(0)
        loads = [pltpu.make_async_copy(src, dst, sem.at[k])
                 for k, (src, dst) in enumerate(((pa_hbm, pa_v), (pb_hbm, pb_v), (wo_hbm, wo_v)))]
        stores = [pltpu.make_async_copy(src, dst, sem.at[k])
                  for k, (src, dst) in enumerate(((gpa_v, gpa_hbm), (gpb_v, gpb_hbm), (gwo_v, gwo_hbm)))]

        @pl.when(i == 0)
        def _():
            for cp in loads:
                cp.start()
            for ref in (gpa_v, gpb_v, gwo_v, loss_ref, gfg_ref, gbm_ref):
                ref[...] = jnp.zeros(ref.shape, F32)
            for cp in loads:
                cp.wait()

        ya_v = ya_ref[...]
        yb_v = yb_ref[...]
        out_a = jnp.dot(ya_v, pa_v[...], preferred_element_type=F32)
        out_b = jnp.dot(yb_v, pb_v[...], preferred_element_type=F32)
        bm = bm_ref[...]
        g_a = _sigmoid(gma_ref[0] + bm[:, 0:D])
        g_b = _sigmoid(gmb_ref[0] + bm[:, D:2 * D])
        mixed = g_a * out_a + g_b * out_b
        mixb = mixed.astype(BF16)
        xo = x_ref[...] + jnp.dot(mixb, wo_v[...], preferred_element_type=F32)
        r = lax.rsqrt(jnp.mean(xo * xo, axis=-1, keepdims=True) + EPS)
        xn = xo * r
        fg = fg_ref[...]
        e = xn * fg - t_ref[...]
        loss_ref[...] += 0.5 * jnp.sum(jnp.mean(e * e, axis=-1, keepdims=True))
        dy = e * (1.0 / D)
        gfg_ref[...] += jnp.sum(dy * xn, axis=0, keepdims=True)
        dxn = dy * fg
        dx2 = r * (dxn - xn * jnp.mean(dxn * xn, axis=-1, keepdims=True))
        dx2_ref[...] = dx2
        dx2b = dx2.astype(BF16)
        d_mixed = lax.dot_general(dx2b, wo_v[...], NT_DIMS, preferred_element_type=F32)
        gwo_v[...] += lax.dot_general(mixb, dx2b, TN_DIMS, preferred_element_type=F32)
        d_oa = (d_mixed * g_a).astype(BF16)
        d_ob = (d_mixed * g_b).astype(BF16)
        dgm_a = (d_mixed * out_a) * (g_a * (1.0 - g_a))
        dgm_b = (d_mixed * out_b) * (g_b * (1.0 - g_b))
        gbm_ref[:, 0:D] += jnp.sum(dgm_a, axis=0, keepdims=True)
        gbm_ref[:, D:2 * D] += jnp.sum(dgm_b, axis=0, keepdims=True)
        dgm_ref[0] = dgm_a.astype(BF16)
        dgm_ref[1] = dgm_b.astype(BF16)
        dya_ref[...] = lax.dot_general(d_oa, pa_v[...], NT_DIMS, preferred_element_type=F32)
        dyb_ref[...] = lax.dot_general(d_ob, pb_v[...], NT_DIMS, preferred_element_type=F32)
        gpa_v[...] += lax.dot_general(ya_v, d_oa, TN_DIMS, preferred_element_type=F32)
        gpb_v[...] += lax.dot_general(yb_v, d_ob, TN_DIMS, preferred_element_type=F32)

        @pl.when(i == ni - 1)
        def _():
            for cp in stores:
                cp.start()
            for cp in stores:
                cp.wait()

    rows = pl.BlockSpec((tm, D), lambda i: (i, 0))
    rep = lambda shape: pl.BlockSpec(shape, lambda i: (0,) * len(shape))
    return pl.pallas_call(
        body, name="mid", grid=(ni,),
        in_specs=[rows, rows,
                  pl.BlockSpec((1, tm, D), lambda i: (6, i, 0)), pl.BlockSpec((1, tm, D), lambda i: (7, i, 0)),
                  rep((1, 2 * D)), rows, rows, rep((1, D)), ANY, ANY, ANY],
        out_specs=[rows, rows, rows, pl.BlockSpec((2, tm, D), lambda i: (0, i, 0)),
                   rep((8, BD)), rep((1, D)), rep((1, 2 * D)), ANY, ANY, ANY],
        out_shape=[SDS((n, D), F32), SDS((n, D), F32), SDS((n, D), F32), SDS((2, n, D), BF16),
                   SDS((8, BD), F32), SDS((1, D), F32), SDS((1, 2 * D), F32),
                   SDS((D, D), F32), SDS((D, D), F32), SDS((D, D), F32)],
        scratch_shapes=[pltpu.VMEM((D, D), BF16)] * 3 + [pltpu.VMEM((D, D), F32)] * 3 + [pltpu.SemaphoreType.DMA((3,))],
        compiler_params=_params(60),
    )(ya, yb, z, z, b_merge, x2, tgt, fin_g, pa, pb, wo)


def _dz_specs(tm, ni, row_major):
    if row_major:
        ia = lambda i, j: (jnp.minimum(j, 1), i, 0)
        ib = lambda i, j: (jnp.clip(j - 2, 0, 3), i, 0)
        im = lambda i, j: (jnp.clip(j - 6, 0, 1), i, 0)
    else:
        last = ni - 1
        ia = lambda j, i: (jnp.minimum(j, 1), jnp.where(j < 2, i, last), 0)
        ib = lambda j, i: (jnp.clip(j - 2, 0, 3), jnp.where(j < 2, 0, jnp.where(j < 6, i, last)), 0)
        im = lambda j, i: (jnp.clip(j - 6, 0, 1), jnp.where(j < 6, 0, i), 0)
    return [pl.BlockSpec((1, tm, D), f) for f in (ia, ib, im)]


def _inproj_bwd_x(dza, dzb, dzm, w_all, x2, dx2, norm_g, after):
    n = x2.shape[0]
    tm = 512
    ni = n // tm

    def body(dza_ref, dzb_ref, dzm_ref, w_ref, x_ref, dx2_ref, g_ref, after_ref, gx_ref, gg_ref, acc):
        i, j = pl.program_id(0), pl.program_id(1)

        @pl.when((i == 0) & (j == 0))
        def _():
            gg_ref[...] = jnp.zeros((1, D), F32)

        @pl.when(j == 0)
        def _():
            acc[...] = jnp.zeros((tm, D), F32)

        def add(ref):
            acc[...] += lax.dot_general(ref[0], w_ref[0], NT_DIMS, preferred_element_type=F32)

        pl.when(j < 2)(lambda: add(dza_ref))
        pl.when((j >= 2) & (j < 6))(lambda: add(dzb_ref))
        pl.when(j >= 6)(lambda: add(dzm_ref))

        @pl.when(j == NB - 1)
        def _():
            x = x_ref[...]
            r = lax.rsqrt(jnp.mean(x * x, axis=-1, keepdims=True) + EPS)
            xn = x * r
            dh = acc[...]
            gg_ref[...] += jnp.sum(dh * xn, axis=0, keepdims=True)
            dxn = dh * g_ref[...]
            gx_ref[...] = dx2_ref[...] + r * (dxn - xn * jnp.mean(dxn * xn, axis=-1, keepdims=True))

    rows = pl.BlockSpec((tm, D), lambda i, j: (i, 0))
    return pl.pallas_call(
        body, name="inproj_bwd_x", grid=(ni, NB),
        in_specs=_dz_specs(tm, ni, True) + [pl.BlockSpec((1, D, D), lambda i, j: (j, 0, 0)), rows, rows,
                                             pl.BlockSpec((1, D), lambda i, j: (0, 0)), ANY],
        out_specs=[rows, pl.BlockSpec((1, D), lambda i, j: (0, 0))],
        out_shape=[SDS((n, D), F32), SDS((1, D), F32)],
        scratch_shapes=[pltpu.VMEM((tm, D), F32)],
        compiler_params=_params(48),
    )(dza, dzb, dzm, w_all, x2, dx2, norm_g, after)


def _inproj_bwd_w(dza, dzb, dzm, h_all, g_m):
    n = h_all.shape[0]
    tm = min(n, 1024)
    ni = n // tm

    def body(dza_ref, dzb_ref, dzm_ref, h_ref, gm_hbm, gw_ref, got_w, got_m, stage, send_sems, recv_sems):
        j, i = pl.program_id(0), pl.program_id(1)
        x, y, c = _place()
        sibling = (x, y, 1 - c)

        def send_w(q):
            return pltpu.make_async_remote_copy(
                src_ref=stage.at[q % 2], dst_ref=got_w.at[q], send_sem=send_sems.at[q], recv_sem=recv_sems.at[q],
                device_id=sibling, device_id_type=MESH)

        def send_m(q):
            return pltpu.make_async_remote_copy(
                src_ref=gm_hbm.at[2 * q + (1 - c)], dst_ref=got_m.at[q], send_sem=send_sems.at[4 + q],
                recv_sem=recv_sems.at[4 + q], device_id=sibling, device_id_type=MESH)

        @pl.when((j == 0) & (i == 0))
        def _():
            for q in range(4):
                send_m(q).start()

        @pl.when(i == 0)
        def _():
            gw_ref[...] = jnp.zeros((1, D, D), F32)

        def add(ref):
            gw_ref[0] += lax.dot_general(h_ref[...], ref[0], TN_DIMS, preferred_element_type=F32)

        pl.when(j < 2)(lambda: add(dza_ref))
        pl.when((j >= 2) & (j < 6))(lambda: add(dzb_ref))
        pl.when(j >= 6)(lambda: add(dzm_ref))

        for q in range(4):
            @pl.when((i == ni - 1) & (j == 2 * q + 1 - c))
            def _(q=q):
                if q >= 2:
                    send_w(q - 2).wait_send()
                stage[q % 2] = gw_ref[0].astype(BF16)
                send_w(q).start()

        @pl.when((j == NB - 1) & (i == ni - 1))
        def _():
            for q in (2, 3):
                send_w(q).wait_send()
            for q in range(4):
                send_w(q).wait_recv()
                send_m(q).wait_send()
                send_m(q).wait_recv()

    return pl.pallas_call(
        body, name="inproj_bwd_w", grid=(NB, ni),
        in_specs=_dz_specs(tm, ni, False) + [pl.BlockSpec((tm, D), lambda j, i: (i, 0)), ANY],
        out_specs=[pl.BlockSpec((1, D, D), lambda j, i: (j, 0, 0)), ANY, ANY],
        out_shape=[SDS((NB, D, D), F32), SDS((4, D, D), BF16), SDS((4,) + g_m.shape[1:], F32)],
        scratch_shapes=[pltpu.VMEM((2, D, D), BF16), pltpu.SemaphoreType.DMA((8,)), pltpu.SemaphoreType.DMA((8,))],
        compiler_params=_params(48),
    )(dza, dzb, dzm, h_all, g_m)


def _adamw(w, g, m, v):
    rows, cols = w.shape
    tr = _row_tile(rows)

    def body(w_ref, g_ref, m_ref, v_ref, d_ref, nm_ref, nv_ref):
        gv = g_ref[...]
        nm = ADAM_B1 * m_ref[...] + (1.0 - ADAM_B1) * gv
        nv = ADAM_B2 * v_ref[...] + (1.0 - ADAM_B2) * (gv * gv)
        m_hat = nm / (1.0 - ADAM_B1 ** ADAM_STEP)
        v_hat = nv / (1.0 - ADAM_B2 ** ADAM_STEP)
        d_ref[...] = -ADAM_LR * (m_hat / (jnp.sqrt(v_hat) + ADAM_EPS) + ADAM_WD * w_ref[...])
        nm_ref[...] = nm
        nv_ref[...] = nv

    spec = pl.BlockSpec((tr, cols), lambda i: (i, 0))
    return pl.pallas_call(
        body, name="adamw", grid=(rows // tr,), in_specs=[spec] * 4, out_specs=[spec] * 3,
        out_shape=[SDS((rows, cols), F32)] * 3, compiler_params=_params(32),
    )(w, g, m, v)


def _allgather(blocks, dtypes, name):
    na = len(blocks)

    def body(*refs):
        ins, outs, stages = refs[:na], refs[na:2 * na], refs[2 * na:3 * na]
        send_sems, recv_sems, local_sems = refs[3 * na:]
        x, y, c = _place()
        me, sibling = (x, y, c), (x, y, 1 - c)
        chips = [(1 - x, y), (x, 1 - y), (1 - x, 1 - y)]
        blk = lambda p: 4 * p[0] + 2 * p[1] + p[2]

        def copy(a, k, block, to, src=None):
            return pltpu.make_async_remote_copy(
                src_ref=outs[a].at[blk(block)] if src is None else src, dst_ref=outs[a].at[blk(block)],
                send_sem=send_sems.at[7 * a + k], recv_sem=recv_sems.at[7 * a + k],
                device_id=to, device_id_type=MESH)

        mine, first, passed = [], [], []
        for a in range(na):
            stages[a][...] = ins[a][...].astype(dtypes[a])
            mine.append(pltpu.make_async_copy(stages[a], outs[a].at[blk(me)], local_sems.at[a]))
            mine[-1].start()
            first.append(copy(a, 0, me, sibling, src=stages[a]))
            first += [copy(a, 1 + j, me, (*chip, c), src=stages[a]) for j, chip in enumerate(chips)]
        for cp in first:
            cp.start()
        for j, chip in enumerate(chips):
            for a in range(na):
                copy(a, 1 + j, (*chip, c), me).wait_recv()
                passed.append(copy(a, 4 + j, (*chip, c), sibling))
                passed[-1].start()
        for a in range(na):
            copy(a, 0, sibling, me).wait_recv()
            for j, chip in enumerate(chips):
                copy(a, 4 + j, (*chip, 1 - c), me).wait_recv()
        for cp in first + passed:
            cp.wait_send()
        for cp in mine:
            cp.wait()

    return pl.pallas_call(
        body, name=name,
        in_specs=[pl.BlockSpec(memory_space=pltpu.VMEM)] * na, out_specs=[ANY] * na,
        out_shape=[SDS((NB,) + b.shape, dt) for b, dt in zip(blocks, dtypes)],
        scratch_shapes=[pltpu.VMEM(b.shape, dt) for b, dt in zip(blocks, dtypes)]
        + [pltpu.SemaphoreType.DMA((7 * na,)), pltpu.SemaphoreType.DMA((7 * na,)), pltpu.SemaphoreType.DMA((na,))],
        compiler_params=_params(40),
    )(*blocks)


HBM = pl.BlockSpec(memory_space=pltpu.HBM)
SEMS = pl.BlockSpec(memory_space=pltpu.SEMAPHORE)
EFFECT = pltpu.SideEffectType.DATAFLOW_SIDE_EFFECTING


def _chip_copies(srcs, lands, send_sems, recv_sems):
    x, y, c = _place()
    return [pltpu.make_async_remote_copy(
        src_ref=srcs[a].at[slot], dst_ref=lands[a].at[slot],
        send_sem=send_sems.at[3 * a + slot], recv_sem=recv_sems.at[3 * a + slot],
        device_id=(px, py, c), device_id_type=MESH)
        for a in range(len(srcs)) for slot, (px, py) in enumerate(_other_chips(x, y))]


def _rs_chips_start(ps):
    na = len(ps)

    def body(*refs):
        srcs, lands = refs[:na], refs[na:2 * na]
        send_sems, recv_sems = refs[2 * na], refs[2 * na + 1]
        token = refs[-1]
        for cp in _chip_copies(srcs, lands, send_sems, recv_sems):
            cp.start()
        token[...] = jnp.zeros_like(token)

    hbm = lambda a: pltpu.HBM(a.shape, a.dtype)
    out = pl.pallas_call(
        body, name="rs_chips_start",
        out_shape=(pltpu.SemaphoreType.DMA((3 * na,)), pltpu.SemaphoreType.DMA((3 * na,)),
                   *[hbm(p) for p in ps], *[hbm(p) for p in ps], SDS((8, BD), F32)),
        in_specs=[HBM] * (2 * na),
        out_specs=(SEMS, SEMS, *[HBM] * (2 * na), pl.BlockSpec(memory_space=pltpu.VMEM)),
        input_output_aliases={i: 2 + i for i in range(2 * na)},
        compiler_params=pltpu.CompilerParams(has_side_effects=EFFECT),
    )(*[pltpu.with_memory_space_constraint(p, pltpu.HBM) for p in ps],
      *[pltpu.with_memory_space_constraint(lax.empty(p.shape, p.dtype), pltpu.HBM) for p in ps])
    return out[0], out[1], out[2:2 + na], out[2 + na:2 + 2 * na], out[-1]


def _rs_chips_wait(send_sems, recv_sems, srcs, lands, after):
    na = len(srcs)

    def body(*refs):
        srcs_r, lands_r = refs[:na], refs[na:2 * na]
        send_r, recv_r = refs[2 * na], refs[2 * na + 1]
        copies = _chip_copies(srcs_r, lands_r, send_r, recv_r)
        for cp in copies:
            cp.wait_send()
        for cp in copies:
            cp.wait_recv()

    hbm = lambda a: pltpu.HBM(a.shape, a.dtype)
    out = pl.pallas_call(
        body, name="rs_chips_wait",
        out_shape=(*[hbm(p) for p in srcs], *[hbm(p) for p in lands]),
        in_specs=[HBM] * (2 * na) + [SEMS, SEMS, ANY],
        out_specs=tuple([HBM] * (2 * na)),
        input_output_aliases={i: i for i in range(2 * na)},
        compiler_params=pltpu.CompilerParams(has_side_effects=EFFECT),
    )(*srcs, *lands, send_sems, recv_sems, after)
    return out[na:]


def _add_sibling(place, g, a_in):
    _, r, cols = g.shape
    tr = _row_tile(r)

    def chip(k, pr):
        qx = pr[0] if k in (1, 3) else 1 - pr[0]
        qy = pr[1] if k in (0, 3) else 1 - pr[1]
        return 2 * qx + qy

    def body(place_ref, *refs):
        g_refs, a_refs, (out_ref, own_ref) = refs[0:4], refs[4:8], refs[8:10]
        for k in range(3):
            out_ref[k] = (g_refs[k][0] + a_refs[k][0].astype(F32)).astype(BF16)
        own_ref[...] = g_refs[3][0] + a_refs[3][0].astype(F32)

    mine = lambda k: pl.BlockSpec((1, tr, cols), lambda i, pr: (2 * chip(k, pr) + pr[2], i, 0))
    theirs = lambda k: pl.BlockSpec((1, tr, cols), lambda i, pr: (chip(k, pr), i, 0))
    return pl.pallas_call(
        body, name="add_sibling",
        grid_spec=pltpu.PrefetchScalarGridSpec(
            num_scalar_prefetch=1, grid=(r // tr,),
            in_specs=[mine(k) for k in range(4)] + [theirs(k) for k in range(4)],
            out_specs=[pl.BlockSpec((3, tr, cols), lambda i, pr: (0, i, 0)),
                       pl.BlockSpec((tr, cols), lambda i, pr: (i, 0))]),
        out_shape=[SDS((3, r, cols), BF16), SDS((r, cols), F32)], compiler_params=_params(48),
    )(place, *[g] * 4, *[a_in] * 4)


def _add_chips(own, b_in):
    r, cols = own.shape
    tr = _row_tile(r)

    def body(p_ref, b0_ref, b1_ref, b2_ref, o_ref):
        o_ref[...] = ((p_ref[...] + b0_ref[0].astype(F32)) + b1_ref[0].astype(F32)) + b2_ref[0].astype(F32)

    slot = lambda k: pl.BlockSpec((1, tr, cols), lambda i: (k, i, 0))
    spec = pl.BlockSpec((tr, cols), lambda i: (i, 0))
    return pl.pallas_call(
        body, name="add_chips", grid=(r // tr,), in_specs=[spec, slot(0), slot(1), slot(2)], out_specs=spec,
        out_shape=SDS((r, cols), F32), compiler_params=_params(32),
    )(own, b_in, b_in, b_in)


VEC_NAMES = ("b_merge", "conv_b", "rg_bx", "rg_ba", "rg_lambda", "hg_lb_logits", "hg_norm_g", "final_norm_g")
REP_NAMES = ("rg_wx", "rg_wa", "norm_g") + VEC_NAMES
SMALL_AT = 3 * BD
SMALL_ROWS = 48
MID_ROWS = 448


def _sum_blocks(parts):
    def body(p_ref, o_ref):
        acc = p_ref[0]
        for k in range(1, NB):
            acc = acc + p_ref[k]
        o_ref[...] = acc

    return pl.pallas_call(body, name="sum_blocks", out_shape=SDS(parts.shape[1:], F32))(parts)


def _pack_rows(arrays, width, row_multiple=8):
    flat = jnp.concatenate([a.reshape(-1) for a in arrays])
    rows = -(-flat.shape[0] // width)
    rows = -(-rows // row_multiple) * row_multiple
    return jnp.pad(flat, (0, rows * width - flat.shape[0])).reshape(rows, width)


def _unpack(flat, like):
    out, off = [], 0
    for a in like:
        out.append(flat[off:off + a.size].reshape(a.shape))
        off += a.size
    return out


def kernel(x, w_in, b_merge, conv_w, conv_b, rg_wx, rg_bx, rg_wa, rg_ba, rg_lambda, hg_lb_logits, hg_norm_g, proj_a, proj_b, w_out, norm_g, final_norm_g, loss_target, m_w_in, m_b_merge, m_conv_w, m_conv_b, m_rg_wx, m_rg_bx, m_rg_wa, m_rg_ba, m_rg_lambda, m_hg_lb_logits, m_hg_norm_g, m_proj_a, m_proj_b, m_w_out, m_norm_g, m_final_norm_g, v_w_in, v_b_merge, v_conv_w, v_conv_b, v_rg_wx, v_rg_bx, v_rg_wa, v_rg_ba, v_rg_lambda, v_hg_lb_logits, v_hg_norm_g, v_proj_a, v_proj_b, v_w_out, v_norm_g, v_final_norm_g):
    weights = dict(w_in=w_in, b_merge=b_merge, conv_w=conv_w, conv_b=conv_b, rg_wx=rg_wx, rg_bx=rg_bx, rg_wa=rg_wa,
                   rg_ba=rg_ba, rg_lambda=rg_lambda, hg_lb_logits=hg_lb_logits, hg_norm_g=hg_norm_g, proj_a=proj_a,
                   proj_b=proj_b, w_out=w_out, norm_g=norm_g, final_norm_g=final_norm_g)
    mom1 = dict(w_in=m_w_in, b_merge=m_b_merge, conv_w=m_conv_w, conv_b=m_conv_b, rg_wx=m_rg_wx, rg_bx=m_rg_bx,
                rg_wa=m_rg_wa, rg_ba=m_rg_ba, rg_lambda=m_rg_lambda, hg_lb_logits=m_hg_lb_logits,
                hg_norm_g=m_hg_norm_g, proj_a=m_proj_a, proj_b=m_proj_b, w_out=m_w_out, norm_g=m_norm_g,
                final_norm_g=m_final_norm_g)
    mom2 = dict(w_in=v_w_in, b_merge=v_b_merge, conv_w=v_conv_w, conv_b=v_conv_b, rg_wx=v_rg_wx, rg_bx=v_rg_bx,
                rg_wa=v_rg_wa, rg_ba=v_rg_ba, rg_lambda=v_rg_lambda, hg_lb_logits=v_hg_lb_logits,
                hg_norm_g=v_hg_norm_g, proj_a=v_proj_a, proj_b=v_proj_b, w_out=v_w_out, norm_g=v_norm_g,
                final_norm_g=v_final_norm_g)
    order = list(weights)
    nb, s_len, _ = x.shape
    n = nb * s_len
    px, py, pc = _place()
    place = jnp.stack([px, py, pc]).astype(jnp.int32)

    x2 = x.reshape(n, D)
    cw_blk = jnp.pad(conv_w[0], ((0, 4), (0, 0)))
    order_ids = jnp.stack([_block_id(p) for p in _arrival_order(px, py, pc)]).astype(jnp.int32)
    z, h_all, w_all, pa_all, pb_all, wo_all, cw_all = _gather_inproj(
        order_ids, x2, norm_g, [w_in[0], proj_a[0], proj_b[0], w_out[0], cw_blk], [BF16, BF16, BF16, BF16, F32])
    pa_full, pb_full, wo_full = (a.reshape(D, D) for a in (pa_all, pb_all, wo_all))
    cw8 = cw_all.transpose(1, 0, 2).reshape(8, D)
    wx_b, wa_b = rg_wx[0].astype(BF16), rg_wa[0].astype(BF16)
    cb, bx, ba = conv_b, rg_bx.reshape(1, D), rg_ba.reshape(1, D)
    fin_g = final_norm_g.reshape(1, D)

    hlru, ya = _lru_fwd(z, cw8, cb, wx_b, wa_b, bx, ba, rg_lambda, nb, s_len)
    o_all, yb, st_all = _hgrn_fwd(z, hg_lb_logits, hg_norm_g, nb, s_len)

    (dx2, dya, dyb, dzm, loss_acc, g_fin, g_bm, g_pa, g_pb, g_wo) = _mid(
        ya, yb, z, b_merge, x2, loss_target.reshape(n, D), fin_g, pa_full, pb_full, wo_full)
    dzb, g_lg, g_hg = _hgrn_bwd(z, o_all, st_all, dyb, hg_lb_logits, hg_norm_g, nb, s_len)
    dza, g_cw8, g_cb, g_wx, g_wa, g_bx, g_ba, g_lam = _lru_bwd(
        z, hlru, dya, cw8, cb, wx_b, wa_b, bx, ba, rg_lambda, nb, s_len)

    part = dict(b_merge=g_bm, conv_b=g_cb, rg_bx=g_bx, rg_ba=g_ba, rg_lambda=g_lam, hg_lb_logits=g_lg,
                hg_norm_g=g_hg, final_norm_g=g_fin)
    vec = _pack_rows([part[k] for k in VEC_NAMES], BD)
    vec = jnp.pad(vec, ((0, 16 * NB - vec.shape[0]), (0, 0))).reshape(NB, 2, D)
    rows8 = lambda a: jnp.pad(a, ((0, 0), (0, 8 - a.shape[1]), (0, 0)))
    g_m = jnp.concatenate([g.reshape(NB, BD, D) for g in (g_pa, g_pb, g_wo)]
                          + [g_wx.reshape(NB, 16, D), g_wa.reshape(NB, 16, D),
                             rows8(g_cw8.reshape(8, NB, BD).transpose(1, 0, 2).reshape(NB, 1, D)), rows8(vec),
                             jnp.zeros((NB, MID_ROWS - SMALL_AT - SMALL_ROWS, D), F32)], axis=1)
    g_w, w_from_sibling, m_from_sibling = _inproj_bwd_w(dza, dzb, dzm, h_all, g_m)
    w_out_bf, w_own = _add_sibling(place, g_w, w_from_sibling)
    m_out_bf, m_own = _add_sibling(place, g_m, m_from_sibling)
    send_sems, recv_sems, srcs, lands, token = _rs_chips_start([w_out_bf, m_out_bf])
    grad_x, g_ng = _inproj_bwd_x(dza, dzb, dzm, w_all, x2, dx2, norm_g, token)
    from_chips = _rs_chips_wait(send_sems, recv_sems, srcs, lands, grad_x)
    r_w = _add_chips(w_own, from_chips[0])
    r_m = _add_chips(m_own, from_chips[1])
    row = lax.broadcasted_iota(jnp.int32, (8, D), 0)
    mine = jnp.where(row == 0, g_ng, jnp.where(row == 1, loss_acc[0:1, 0:1], 0.0))
    tail = jnp.concatenate([r_m[SMALL_AT:SMALL_AT + SMALL_ROWS], mine], axis=0)
    (tail_all,) = _allgather([tail], [F32], "gather_small_grads")
    summed = _sum_blocks(tail_all[:, SMALL_ROWS:SMALL_ROWS + 8])

    grads = dict(w_in=r_w.reshape(1, D, D),
                 proj_a=r_m[0:BD].reshape(1, BD, D), proj_b=r_m[BD:2 * BD].reshape(1, BD, D),
                 w_out=r_m[2 * BD:3 * BD].reshape(1, BD, D),
                 conv_w=r_m[SMALL_AT + 32].reshape(8, BD)[0:4].reshape(1, 4, BD),
                 rg_wx=tail_all[:, 0:16].reshape(1, NB, BD, BD), rg_wa=tail_all[:, 16:32].reshape(1, NB, BD, BD),
                 norm_g=summed[0:1])
    vec_all = tail_all[:, 40:42].reshape(-1)
    for k, gk in zip(VEC_NAMES, _unpack(vec_all, [weights[k] for k in VEC_NAMES])):
        grads[k] = gk

    delta, new_m, new_v = {}, {}, {}
    for k in ("w_in", "proj_a", "proj_b", "w_out"):
        shp = weights[k].shape
        two = lambda a: a.reshape(shp[1], shp[2])
        d_k, m_k, v_k = _adamw(two(weights[k]), two(grads[k]), two(mom1[k]), two(mom2[k]))
        delta[k], new_m[k], new_v[k] = d_k.reshape(shp), m_k.reshape(shp), v_k.reshape(shp)
    rep = list(REP_NAMES) + ["conv_w"]
    packs = [_pack_rows([t[k] for k in rep], BD, 256) for t in (weights, grads, mom1, mom2)]
    outs = _adamw(*packs)
    for tgt, flat in zip((delta, new_m, new_v), outs):
        for k, a in zip(rep, _unpack(flat.reshape(-1), [weights[k] for k in rep])):
            tgt[k] = a

    return (summed[1, 0],grad_x.reshape(x.shape), *[grads[k] for k in order], *[delta[k] for k in order],
            *[new_m[k] for k in order], *[new_v[k] for k in order])
```

```python
import functools

import jax
import jax.numpy as jnp
from jax import lax
from jax.experimental import pallas as pl
from jax.experimental.pallas import tpu as pltpu

F32 = jnp.float32
BF16 = jnp.bfloat16
SDS = jax.ShapeDtypeStruct
MESH = pl.DeviceIdType.MESH
ANY = pl.BlockSpec(memory_space=pl.ANY)

D = 1024
NB = 8
BD = D // NB
CHUNK = 64
EPS = 1e-6
LRU_C = 8.0
HG_SCALE = BD ** -0.5
ADAM_LR, ADAM_B1, ADAM_B2, ADAM_EPS, ADAM_WD, ADAM_STEP = 0.001, 0.9, 0.999, 1e-08, 0.01, 10

NT_DIMS = (((1,), (1,)), ((), ()))
TN_DIMS = (((0,), (0,)), ((), ()))


def _params(vmem_mib):
    return pltpu.CompilerParams(vmem_limit_bytes=vmem_mib << 20)


def _row_tile(rows, most=256):
    assert rows % 8 == 0
    return max(t for t in range(8, min(rows, most) + 1, 8) if rows % t == 0)


def _sigmoid(v):
    return 0.5 * (jnp.tanh(0.5 * v) + 1.0)


def _groups(v):
    return v.reshape(v.shape[0] // 8, 8, v.shape[1])


def _softplus_neg(lam):
    t = -lam
    e = jnp.exp(-jnp.abs(t))
    w = 1.0 + e
    d = w - 1.0
    l1p = jnp.where(d == 0.0, e, jnp.log(w) * (e / jnp.where(d == 0.0, 1.0, d)))
    return jnp.maximum(t, 0.0) + l1p


def _place():
    return lax.axis_index("x"), lax.axis_index("y"), lax.axis_index("c")


def _other_chips(x, y):
    return [(1 - x, y), (x, 1 - y), (1 - x, 1 - y)]


def _block_id(p):
    return 4 * p[0] + 2 * p[1] + p[2]


def _arrival_order(x, y, c):
    near, far, diag = _other_chips(x, y)
    return [(x, y, c), (x, y, 1 - c), (*near, c), (*far, c), (*near, 1 - c), (*far, 1 - c), (*diag, c), (*diag, 1 - c)]


def _gather_inproj(order_ids, x2, norm_g, blocks, dtypes):
    na = len(blocks)
    n = x2.shape[0]
    tm = min(n, 1024)
    ni = n // tm

    def body(order_ref, x_ref, g_ref, *refs):
        ins, (z_ref, h_ref), outs = refs[:na], refs[na:na + 2], refs[na + 2:2 * na + 2]
        stages = refs[2 * na + 2:3 * na + 2]
        h_full, wbuf, send_sems, recv_sems, local_sems, wsem, hsem = refs[3 * na + 2:]
        j, i = pl.program_id(0), pl.program_id(1)
        x, y, c = _place()
        me, sibling = (x, y, c), (x, y, 1 - c)
        chips = _other_chips(x, y)
        small = range(1, na)

        def copy(a, k, block, to, src=None):
            return pltpu.make_async_remote_copy(
                src_ref=outs[a].at[_block_id(block)] if src is None else src, dst_ref=outs[a].at[_block_id(block)],
                send_sem=send_sems.at[7 * a + k], recv_sem=recv_sems.at[7 * a + k],
                device_id=to, device_id_type=MESH)

        def local(a):
            return pltpu.make_async_copy(stages[a], outs[a].at[_block_id(me)], local_sems.at[a])

        def landed(a, slot):
            copy(a, 1 + slot, (*chips[slot], c), me).wait_recv()
            copy(a, 4 + slot, (*chips[slot], c), sibling).start()

        def passed_on(a, slot):
            copy(a, 4 + slot, (*chips[slot], 1 - c), me).wait_recv()

        @pl.when((j == 0) & (i == 0))
        def _():
            for a in range(na):
                stages[a][...] = ins[a][...].astype(dtypes[a])
                local(a).start()
            for a in range(na):
                copy(a, 0, me, sibling, src=stages[a]).start()
                for slot, chip in enumerate(chips):
                    copy(a, 1 + slot, me, (*chip, c), src=stages[a]).start()

        @pl.when(j == 0)
        def _():
            xv = x_ref[...]
            r = lax.rsqrt(jnp.mean(xv * xv, axis=-1, keepdims=True) + EPS)
            hb = ((xv * r) * g_ref[...]).astype(BF16)
            h_full[pl.ds(pl.multiple_of(i * tm, tm), tm), :] = hb

        save_h = pltpu.make_async_copy(h_full, h_ref, hsem)
        pl.when((j == 0) & (i == ni - 1))(save_h.start)

        steps = [
            lambda: local(0).wait(),
            lambda: copy(0, 0, sibling, me).wait_recv(),
            lambda: landed(0, 0),
            lambda: landed(0, 1),
            lambda: passed_on(0, 0),
            lambda: passed_on(0, 1),
            lambda: landed(0, 2),
            lambda: passed_on(0, 2),
        ]
        for k, step in enumerate(steps):
            pl.when((i == 0) & (j == k))(step)

        @pl.when(i == 0)
        def _():
            load = pltpu.make_async_copy(outs[0].at[order_ref[j]], wbuf, wsem)
            load.start()
            load.wait()

        z_ref[0] = jnp.dot(h_full[pl.ds(pl.multiple_of(i * tm, tm), tm), :], wbuf[...], preferred_element_type=F32)

        @pl.when((j == NB - 1) & (i == ni - 1))
        def _():
            save_h.wait()
            for slot in range(3):
                for a in small:
                    landed(a, slot)
            for a in small:
                local(a).wait()
                copy(a, 0, sibling, me).wait_recv()
                for slot in range(3):
                    passed_on(a, slot)
            for a in range(na):
                copy(a, 0, me, sibling, src=stages[a]).wait_send()
                for slot, chip in enumerate(chips):
                    copy(a, 1 + slot, me, (*chip, c), src=stages[a]).wait_send()
                    copy(a, 4 + slot, (*chip, c), sibling).wait_send()

    rows_once = lambda j, i, order: (jnp.where(j == 0, i, ni - 1), 0)
    vmem = pl.BlockSpec(memory_space=pltpu.VMEM)
    return pl.pallas_call(
        body, name="gather_inproj",
        grid_spec=pltpu.PrefetchScalarGridSpec(
            num_scalar_prefetch=1, grid=(NB, ni),
            in_specs=[pl.BlockSpec((tm, D), rows_once), pl.BlockSpec((1, D), lambda j, i, order: (0, 0))] + [vmem] * na,
            out_specs=[pl.BlockSpec((1, tm, D), lambda j, i, order: (order[j], i, 0)), ANY] + [ANY] * na,
            scratch_shapes=[pltpu.VMEM(b.shape, dt) for b, dt in zip(blocks, dtypes)]
            + [pltpu.VMEM((n, D), BF16), pltpu.VMEM((D, D), BF16),
               pltpu.SemaphoreType.DMA((7 * na,)), pltpu.SemaphoreType.DMA((7 * na,)),
               pltpu.SemaphoreType.DMA((na,)), pltpu.SemaphoreType.DMA(()), pltpu.SemaphoreType.DMA(())]),
        out_shape=[SDS((NB, n, D), F32), SDS((n, D), BF16)] + [SDS((NB,) + b.shape, dt) for b, dt in zip(blocks, dtypes)],
        compiler_params=_params(56),
    )(order_ids, x2, norm_g, *blocks)


LRU_T = 256


def _conv(ext, cw, cb):
    t = LRU_T
    acc = ext[5:5 + t, :] * cw[0:1, :] + ext[6:6 + t, :] * cw[1:2, :]
    acc = acc + ext[7:7 + t, :] * cw[2:3, :]
    acc = acc + ext[8:8 + t, :] * cw[3:4, :]
    return cb + acc


def _lru_gates(xa, wx_ref, wa_ref, bx, ba, lam):
    xab = xa.astype(BF16)
    pis, prs = [], []
    for h in range(NB):
        xs = xab[:, h * BD:(h + 1) * BD]
        pis.append(jnp.dot(xs, wx_ref[h], preferred_element_type=F32))
        prs.append(jnp.dot(xs, wa_ref[h], preferred_element_type=F32))
    gi = _sigmoid(jnp.concatenate(pis, axis=1) + bx)
    gr = _sigmoid(jnp.concatenate(prs, axis=1) + ba)
    sp = _softplus_neg(lam)
    log_a = (-LRU_C * gr) * sp
    a = jnp.exp(log_a)
    mult = jnp.sqrt(-jnp.tanh(log_a) * (a * a + 1.0))
    return xab, gi, gr, sp, a, mult


def _lru_fwd(z, cw8, cb, wx, wa, bx, ba, lam, nb, s_len):
    n = nb * s_len
    t = LRU_T
    ns = s_len // t

    def body(xp_ref, ga_ref, cw_ref, cb_ref, wx_ref, wa_ref, bx_ref, ba_ref, lam_ref,
             h_ref, ya_ref, ext, a_s, u_s, carry):
        @pl.when(pl.program_id(1) == 0)
        def _():
            ext[0:8, :] = jnp.zeros((8, D), F32)
            carry[...] = jnp.zeros((8, D), F32)

        ext[8:8 + t, :] = xp_ref[0]
        xa = _conv(ext, cw_ref[...], cb_ref[...])
        ext[0:8, :] = ext[t:t + 8, :]
        _, gi, _, _, a, mult = _lru_gates(xa, wx_ref, wa_ref, bx_ref[...], ba_ref[...], lam_ref[...])
        u = (mult * gi) * xa
        a, u = _groups(a), _groups(u)
        row = lax.broadcasted_iota(jnp.int32, a.shape, 1)
        for sh in (1, 2, 4):
            a_sh = pltpu.roll(a, sh, 1)
            u_sh = pltpu.roll(u, sh, 1)
            m = row >= sh
            u = jnp.where(m, a * u_sh + u, u)
            a = jnp.where(m, a * a_sh, a)
        a_s[...] = a.reshape(t, D)
        u_s[...] = u.reshape(t, D)

        def step(g, c):
            r = pl.multiple_of(g * 8, 8)
            hg = u_s[pl.ds(r, 8), :] + a_s[pl.ds(r, 8), :] * c
            h_ref[pl.ds(r, 8), :] = hg
            return hg[7:8, :]

        c_out = lax.fori_loop(0, t // 8, step, carry[0:1, :], unroll=4)
        carry[0:1, :] = c_out
        ga = ga_ref[0]
        ya_ref[...] = (h_ref[...] * (ga * _sigmoid(ga))).astype(BF16)

    row_map = lambda b, s: (b * ns + s, 0)
    rep2 = lambda b, s: (0, 0)
    rep3 = lambda b, s: (0, 0, 0)
    return pl.pallas_call(
        body, name="lru_fwd", grid=(nb, ns),
        in_specs=[pl.BlockSpec((1, t, D), lambda b, s: (0, b * ns + s, 0)),
                  pl.BlockSpec((1, t, D), lambda b, s: (1, b * ns + s, 0)),
                  pl.BlockSpec((8, D), rep2), pl.BlockSpec((1, D), rep2),
                  pl.BlockSpec((NB, BD, BD), rep3), pl.BlockSpec((NB, BD, BD), rep3),
                  pl.BlockSpec((1, D), rep2), pl.BlockSpec((1, D), rep2), pl.BlockSpec((1, D), rep2)],
        out_specs=[pl.BlockSpec((t, D), row_map), pl.BlockSpec((t, D), row_map)],
        out_shape=[SDS((n, D), F32), SDS((n, D), BF16)],
        scratch_shapes=[pltpu.VMEM((t + 8, D), F32), pltpu.VMEM((t, D), F32), pltpu.VMEM((t, D), F32),
                        pltpu.VMEM((8, D), F32)],
        compiler_params=_params(48),
    )(z, z, cw8, cb, wx, wa, bx, ba, lam)


def _lru_bwd(z, h_all, dya, cw8, cb, wx, wa, bx, ba, lam, nb, s_len):
    n = nb * s_len
    t = LRU_T
    ns = s_len // t
    t8 = t // 8

    def body(xp_ref, xph_ref, ga_ref, h_ref, hh_ref, dya_ref, cw_ref, cb_ref, wx_ref, wa_ref, bx_ref, ba_ref,
             lam_ref, dz_ref, gcw_ref, gcb_ref, gwx_ref, gwa_ref, gbx_ref, gba_ref, glam_ref,
             ext, hext, dext, a_s, u_s, dh_s, carry):
        b, s = pl.program_id(0), pl.program_id(1)
        first_tile = s == ns - 1

        @pl.when((b == 0) & (s == 0))
        def _():
            for ref in (gcw_ref, gcb_ref, gwx_ref, gwa_ref, gbx_ref, gba_ref, glam_ref):
                ref[...] = jnp.zeros(ref.shape, F32)

        @pl.when(s == 0)
        def _():
            dext[t:t + 8, :] = jnp.zeros((8, D), F32)
            carry[...] = jnp.zeros((8, D), F32)

        keep = jnp.where(first_tile, 0.0, 1.0)
        ext[0:8, :] = xph_ref[0] * keep
        ext[8:8 + t, :] = xp_ref[0]
        hext[0:8, :] = hh_ref[...] * keep
        hext[8:8 + t, :] = h_ref[...]
        cw = cw_ref[...]
        lam = lam_ref[...]
        xa = _conv(ext, cw, cb_ref[...])
        xab, gi, gr, sp, a, mult = _lru_gates(xa, wx_ref, wa_ref, bx_ref[...], ba_ref[...], lam)
        h_prev = hext[7:7 + t, :]
        ga = ga_ref[0]
        sg = _sigmoid(ga)
        dya_v = dya_ref[...]
        d_ga = dya_v * h_ref[...] * (sg * (1.0 + ga * (1.0 - sg)))
        g_in = dya_v * (ga * sg)

        rows = lax.broadcasted_iota(jnp.int32, (t, D), 0)
        an = _groups(jnp.where(rows == t - 1, 1.0, pltpu.roll(a, t - 1, 0)))
        u = _groups(g_in)
        row = lax.broadcasted_iota(jnp.int32, an.shape, 1)
        for sh in (1, 2, 4):
            a_sh = pltpu.roll(an, 8 - sh, 1)
            u_sh = pltpu.roll(u, 8 - sh, 1)
            m = row < 8 - sh
            u = jnp.where(m, u + an * u_sh, u)
            an = jnp.where(m, an * a_sh, an)
        a_s[...] = an.reshape(t, D)
        u_s[...] = u.reshape(t, D)

        def step(i, c):
            r = pl.multiple_of((t8 - 1 - i) * 8, 8)
            dg = u_s[pl.ds(r, 8), :] + a_s[pl.ds(r, 8), :] * c
            dh_s[pl.ds(r, 8), :] = dg
            return dg[0:1, :]

        lax.fori_loop(0, t8, step, carry[0:1, :], unroll=4)
        dh = dh_s[...]
        carry[0:1, :] = a[0:1, :] * dh[0:1, :]

        d_a = dh * h_prev
        dux = dh * xa
        d_mult = dux * gi
        d_gi = dux * mult
        d_xa = dh * (mult * gi)
        d_loga = d_a * a - d_mult * ((a * a) / mult)
        d_gr = d_loga * (-LRU_C * sp)
        d_sp = jnp.sum(d_loga * (-LRU_C * gr), axis=0, keepdims=True)
        glam_ref[...] += d_sp * (-_sigmoid(-lam))
        d_pi = d_gi * gi * (1.0 - gi)
        d_pr = d_gr * gr * (1.0 - gr)
        gbx_ref[...] += jnp.sum(d_pi, axis=0, keepdims=True)
        gba_ref[...] += jnp.sum(d_pr, axis=0, keepdims=True)
        dpib = d_pi.astype(BF16)
        dprb = d_pr.astype(BF16)
        back = []
        for h in range(NB):
            cs = slice(h * BD, (h + 1) * BD)
            gwx_ref[h] += lax.dot_general(xab[:, cs], dpib[:, cs], TN_DIMS, preferred_element_type=F32)
            gwa_ref[h] += lax.dot_general(xab[:, cs], dprb[:, cs], TN_DIMS, preferred_element_type=F32)
            back.append(lax.dot_general(dpib[:, cs], wx_ref[h], NT_DIMS, preferred_element_type=F32)
                        + lax.dot_general(dprb[:, cs], wa_ref[h], NT_DIMS, preferred_element_type=F32))
        d_xa = d_xa + jnp.concatenate(back, axis=1)

        dext[0:t, :] = d_xa
        d_xp = dext[3:3 + t, :] * cw[0:1, :] + dext[2:2 + t, :] * cw[1:2, :]
        d_xp = d_xp + dext[1:1 + t, :] * cw[2:3, :]
        d_xp = d_xp + d_xa * cw[3:4, :]
        dext[t:t + 8, :] = d_xa[0:8, :]
        gcb_ref[...] += jnp.sum(d_xa, axis=0, keepdims=True)
        for k in range(4):
            gcw_ref[k:k + 1, :] += jnp.sum(d_xa * ext[5 + k:5 + k + t, :], axis=0, keepdims=True)
        dz_ref[0] = d_xp.astype(BF16)
        dz_ref[1] = d_ga.astype(BF16)

    rb = lambda b, s: b * ns + (ns - 1 - s)
    halo = lambda b, s: jnp.maximum(rb(b, s) * t8 - 1, 0)
    rep2 = lambda b, s: (0, 0)
    rep3 = lambda b, s: (0, 0, 0)
    return pl.pallas_call(
        body, name="lru_bwd", grid=(nb, ns),
        in_specs=[pl.BlockSpec((1, t, D), lambda b, s: (0, rb(b, s), 0)),
                  pl.BlockSpec((1, 8, D), lambda b, s: (0, halo(b, s), 0)),
                  pl.BlockSpec((1, t, D), lambda b, s: (1, rb(b, s), 0)),
                  pl.BlockSpec((t, D), lambda b, s: (rb(b, s), 0)),
                  pl.BlockSpec((8, D), lambda b, s: (halo(b, s), 0)),
                  pl.BlockSpec((t, D), lambda b, s: (rb(b, s), 0)),
                  pl.BlockSpec((8, D), rep2), pl.BlockSpec((1, D), rep2),
                  pl.BlockSpec((NB, BD, BD), rep3), pl.BlockSpec((NB, BD, BD), rep3),
                  pl.BlockSpec((1, D), rep2), pl.BlockSpec((1, D), rep2), pl.BlockSpec((1, D), rep2)],
        out_specs=[pl.BlockSpec((2, t, D), lambda b, s: (0, rb(b, s), 0)),
                   pl.BlockSpec((8, D), rep2), pl.BlockSpec((1, D), rep2),
                   pl.BlockSpec((NB, BD, BD), rep3), pl.BlockSpec((NB, BD, BD), rep3),
                   pl.BlockSpec((1, D), rep2), pl.BlockSpec((1, D), rep2), pl.BlockSpec((1, D), rep2)],
        out_shape=[SDS((2, n, D), BF16), SDS((8, D), F32), SDS((1, D), F32),
                   SDS((NB, BD, BD), F32), SDS((NB, BD, BD), F32),
                   SDS((1, D), F32), SDS((1, D), F32), SDS((1, D), F32)],
        scratch_shapes=[pltpu.VMEM((t + 8, D), F32), pltpu.VMEM((t + 8, D), F32), pltpu.VMEM((t + 8, D), F32),
                        pltpu.VMEM((t, D), F32), pltpu.VMEM((t, D), F32), pltpu.VMEM((t, D), F32),
                        pltpu.VMEM((8, D), F32)],
        compiler_params=_params(56),
    )(z, z, z, h_all, h_all, dya, cw8, cb, wx, wa, bx, ba, lam)


HG_T = 512
HG_NC = HG_T // CHUNK
BNT_DIMS = (((2,), (2,)), ((0,), (0,)))
BNN_DIMS = (((2,), (1,)), ((0,), (0,)))
BTN_DIMS = (((1,), (1,)), ((0,), (0,)))


def _lower_bound(lg):
    m = jnp.max(lg, axis=0, keepdims=True)
    e = jnp.exp(lg - m)
    return e[0:1, :] / jnp.sum(e, axis=0, keepdims=True)


def _tri(upper):
    r = lax.broadcasted_iota(jnp.int32, (HG_NC, CHUNK, CHUNK), 1)
    c = lax.broadcasted_iota(jnp.int32, (HG_NC, CHUNK, CHUNK), 2)
    return (c >= r) if upper else (r >= c)


def _bdot(a, b, dims):
    return lax.dot_general(a, b, dims, preferred_element_type=F32)


def _tri_sums(upper, a):
    tri = _tri(upper).astype(BF16)
    a1 = a.astype(BF16)
    r1 = a - a1.astype(F32)
    a2 = r1.astype(BF16)
    a3 = (r1 - a2.astype(F32)).astype(BF16)
    return _bdot(tri, a1, BNN_DIMS) + (_bdot(tri, a2, BNN_DIMS) + _bdot(tri, a3, BNN_DIMS))


def _chunks(a):
    return a.reshape(HG_NC, CHUNK, BD)


def _hg_tile(q, fp, lb):
    q, fp = _chunks(q), _chunks(fp)
    sig = _sigmoid(fp)
    f = lb + (1.0 - lb) * sig
    log_f = jnp.log(f)
    k = 1.0 - f
    b = _tri_sums(False, log_f)
    b_mid = b[:, CHUNK // 2:CHUNK // 2 + 1, :]
    b_last = b[:, CHUNK - 1:CHUNK, :]
    sq = _sigmoid(q)
    qh = q * sq
    e_qi = jnp.exp(b - b_mid)
    e_ki = jnp.exp(b_mid - b)
    e_qs = jnp.exp(b)
    e_ks = jnp.exp(b_last - b)
    dc = jnp.exp(b_last)
    q_in = (qh * e_qi) * HG_SCALE
    k_in = k * e_ki
    q_st = (qh * e_qs) * HG_SCALE
    k_st = k * e_ks
    att = _bdot(q_in.astype(BF16), k_in.astype(BF16), BNT_DIMS)
    att = jnp.where(_tri(False), att, 0.0)
    return dict(q=q, sig=sig, f=f, k=k, sq=sq, e_qi=e_qi, e_ki=e_ki, e_qs=e_qs, e_ks=e_ks, dc=dc,
                q_in=q_in, k_in=k_in, q_st=q_st, k_st=k_st, att=att)


def _hgrn_fwd(z, lb_logits, hg_g, nb, s_len):
    n = nb * s_len
    t = HG_T
    ns = s_len // t
    nchunk = s_len // CHUNK

    def body(q_ref, f_ref, v_ref, gb_ref, lg_ref, g_ref, o_ref, yb_ref, st_ref, st):
        @pl.when(pl.program_id(1) == 0)
        def _():
            st[...] = jnp.zeros((NB, BD, BD), F32)

        def head(h, carry):
            cols = pl.ds(pl.multiple_of(h * BD, BD), BD)
            lb = _lower_bound(lg_ref[:, cols])
            ck = _hg_tile(q_ref[0, :, cols], f_ref[0, :, cols], lb)
            vb = _chunks(v_ref[0, :, cols]).astype(BF16)
            kv = _bdot(vb, ck["k_st"].astype(BF16), BTN_DIMS)
            states = [st[h]]
            for c in range(HG_NC):
                states.append(states[c] * ck["dc"][c] + kv[c])
            st[h] = states[HG_NC]
            s_in = jnp.stack(states[:HG_NC], axis=0)
            st_ref[h] = s_in
            o = (_bdot(ck["att"].astype(BF16), vb, BNN_DIMS)
                 + _bdot(ck["q_st"].astype(BF16), s_in.astype(BF16), BNT_DIMS))
            o_ref[:, cols] = o.reshape(t, BD)
            r = lax.rsqrt(jnp.mean(o * o, axis=-1, keepdims=True) + EPS)
            gb = _chunks(gb_ref[0, :, cols])
            yb_ref[:, cols] = (((o * r) * g_ref[...]) * (gb * _sigmoid(gb))).astype(BF16).reshape(t, BD)
            return carry

        lax.fori_loop(0, NB, head, 0, unroll=2)

    seg = lambda j: pl.BlockSpec((1, t, D), lambda b, s: (j, b * ns + s, 0))
    tile = pl.BlockSpec((t, D), lambda b, s: (b * ns + s, 0))
    return pl.pallas_call(
        body, name="hgrn_fwd", grid=(nb, ns),
        in_specs=[seg(2), seg(3), seg(4), seg(5),
                  pl.BlockSpec((2, D), lambda b, s: (0, 0)), pl.BlockSpec((1, BD), lambda b, s: (0, 0))],
        out_specs=[tile, tile, pl.BlockSpec((NB, HG_NC, BD, BD), lambda b, s: (b, s, 0, 0))],
        out_shape=[SDS((n, D), F32), SDS((n, D), BF16), SDS((nb * NB, nchunk, BD, BD), F32)],
        scratch_shapes=[pltpu.VMEM((NB, BD, BD), F32)],
        compiler_params=_params(56),
    )(z, z, z, z, lb_logits, hg_g)


def _hgrn_bwd(z, o_all, st_all, dyb, lb_logits, hg_g, nb, s_len):
    n = nb * s_len
    t = HG_T
    ns = s_len // t

    def body(q_ref, f_ref, v_ref, gb_ref, o_ref, st_ref, dyb_ref, lg_ref, g_ref,
             dz_ref, glg_ref, ghg_ref, dst, dlb):
        b, s = pl.program_id(0), pl.program_id(1)

        @pl.when((b == 0) & (s == 0))
        def _():
            ghg_ref[...] = jnp.zeros((1, BD), F32)
            dlb[...] = jnp.zeros((8, D), F32)

        @pl.when(s == 0)
        def _():
            dst[...] = jnp.zeros((NB, BD, BD), F32)

        g = g_ref[...]

        def head(h, carry):
            cols = pl.ds(pl.multiple_of(h * BD, BD), BD)
            lb = _lower_bound(lg_ref[:, cols])
            ck = _hg_tile(q_ref[0, :, cols], f_ref[0, :, cols], lb)
            q = ck["q"]
            vb = _chunks(v_ref[0, :, cols]).astype(BF16)
            gb = _chunks(gb_ref[0, :, cols])
            o = _chunks(o_ref[:, cols])
            dyb_v = _chunks(dyb_ref[:, cols])
            s_in = st_ref[h]

            sgb = _sigmoid(gb)
            r = lax.rsqrt(jnp.mean(o * o, axis=-1, keepdims=True) + EPS)
            ohat = o * r
            d_on = dyb_v * (gb * sgb)
            d_gb = dyb_v * (ohat * g) * (sgb * (1.0 + gb * (1.0 - sgb)))
            ghg_ref[...] += jnp.sum(jnp.sum(d_on * ohat, axis=1), axis=0, keepdims=True)
            tt = d_on * g
            d_o = r * (tt - ohat * jnp.mean(tt * ohat, axis=-1, keepdims=True))
            dob = d_o.astype(BF16)

            attb = ck["att"].astype(BF16)
            q_inb, k_inb = ck["q_in"].astype(BF16), ck["k_in"].astype(BF16)
            q_stb, k_stb = ck["q_st"].astype(BF16), ck["k_st"].astype(BF16)
            d_att = jnp.where(_tri(False), _bdot(dob, vb, BNT_DIMS), 0.0).astype(BF16)
            d_q_in = _bdot(d_att, k_inb, BNN_DIMS)
            d_k_in = _bdot(d_att, q_inb, BTN_DIMS)
            d_q_st = _bdot(dob, s_in.astype(BF16), BNN_DIMS)
            qdo = _bdot(dob, q_stb, BTN_DIMS)
            d_states = [None] * HG_NC + [dst[h]]
            for c in reversed(range(HG_NC)):
                d_states[c] = d_states[c + 1] * ck["dc"][c] + qdo[c]
            dst[h] = d_states[0]
            ds_out = jnp.stack(d_states[1:], axis=0)
            dsb = ds_out.astype(BF16)
            d_v = _bdot(attb, dob, BTN_DIMS) + _bdot(k_stb, dsb, BNT_DIMS)
            d_k_st = _bdot(vb, dsb, BNN_DIMS)
            d_dc = jnp.sum(ds_out * s_in, axis=1, keepdims=True)

            p_qi = d_q_in * ck["q_in"]
            p_ki = d_k_in * ck["k_in"]
            p_qs = d_q_st * ck["q_st"]
            p_ks = d_k_st * ck["k_st"]
            d_qh = (d_q_in * ck["e_qi"] + d_q_st * ck["e_qs"]) * HG_SCALE
            d_k = d_k_in * ck["e_ki"] + d_k_st * ck["e_ks"]
            d_b = (p_qi - p_ki) + (p_qs - p_ks)
            d_b_mid = jnp.sum(p_ki - p_qi, axis=1, keepdims=True)
            d_b_last = jnp.sum(p_ks, axis=1, keepdims=True) + d_dc * ck["dc"]
            rowi = lax.broadcasted_iota(jnp.int32, (HG_NC, CHUNK, BD), 1)
            d_b = d_b + jnp.where(rowi == CHUNK // 2, d_b_mid, 0.0) + jnp.where(rowi == CHUNK - 1, d_b_last, 0.0)
            d_logf = _tri_sums(True, d_b)
            d_f = d_logf / ck["f"] - d_k
            sig, sq = ck["sig"], ck["sq"]
            d_fp = d_f * (1.0 - lb) * (sig * (1.0 - sig))
            dlb[0:1, cols] += jnp.sum(jnp.sum(d_f * (1.0 - sig), axis=1), axis=0, keepdims=True)
            d_q = d_qh * (sq * (1.0 + q * (1.0 - sq)))
            dz_ref[0, :, cols] = d_q.astype(BF16).reshape(t, BD)
            dz_ref[1, :, cols] = d_fp.astype(BF16).reshape(t, BD)
            dz_ref[2, :, cols] = d_v.astype(BF16).reshape(t, BD)
            dz_ref[3, :, cols] = d_gb.astype(BF16).reshape(t, BD)
            return carry

        lax.fori_loop(0, NB, head, 0, unroll=2)

        @pl.when((b == nb - 1) & (s == ns - 1))
        def _():
            lb = _lower_bound(lg_ref[...])
            dl = dlb[0:1, :] * (lb * (1.0 - lb))
            glg_ref[0:1, :] = dl
            glg_ref[1:2, :] = -dl

    rb = lambda b, s: b * ns + (ns - 1 - s)
    seg = lambda j: pl.BlockSpec((1, t, D), lambda b, s: (j, rb(b, s), 0))
    tile = pl.BlockSpec((t, D), lambda b, s: (rb(b, s), 0))
    return pl.pallas_call(
        body, name="hgrn_bwd", grid=(nb, ns),
        in_specs=[seg(2), seg(3), seg(4), seg(5), tile,
                  pl.BlockSpec((NB, HG_NC, BD, BD), lambda b, s: (b, ns - 1 - s, 0, 0)),
                  tile, pl.BlockSpec((2, D), lambda b, s: (0, 0)), pl.BlockSpec((1, BD), lambda b, s: (0, 0))],
        out_specs=[pl.BlockSpec((4, t, D), lambda b, s: (0, rb(b, s), 0)),
                   pl.BlockSpec((2, D), lambda b, s: (0, 0)), pl.BlockSpec((1, BD), lambda b, s: (0, 0))],
        out_shape=[SDS((4, n, D), BF16), SDS((2, D), F32), SDS((1, BD), F32)],
        scratch_shapes=[pltpu.VMEM((NB, BD, BD), F32), pltpu.VMEM((8, D), F32)],
        compiler_params=_params(60),
    )(z, z, z, z, o_all, st_all, dyb, lb_logits, hg_g)


def _mid(ya, yb, z, b_merge, x2, tgt, fin_g, pa, pb, wo):
    n = x2.shape[0]
    tm = 256
    ni = n // tm

    def body(ya_ref, yb_ref, gma_ref, gmb_ref, bm_ref, x_ref, t_ref, fg_ref, pa_hbm, pb_hbm, wo_hbm,
             dx2_ref, dya_ref, dyb_ref, dgm_ref, loss_ref, gfg_ref, gbm_ref, gm_hbm,
             pa_v, pb_v, wo_v, gpa_v, gpb_v, gwo_v, sem):
        i = pl.program_id(0)
        by_owner = lambda g: g.reshape(NB, BD, D)
        loads = [pltpu.make_async_copy(src, dst, sem.at[k])
                 for k, (src, dst) in enumerate(((pa_hbm, pa_v), (pb_hbm, pb_v), (wo_hbm, wo_v)))]
        stores = [pltpu.make_async_copy(src, dst, sem.at[k])
                  for k, (src, dst) in enumerate((g, gm_hbm.at[:, pl.ds(slot * BD, BD), :])
                                                 for slot, g in enumerate((gpa_v, gpb_v, gwo_v)))]

        @pl.when(i == 0)
        def _():
            for cp in loads:
                cp.start()
            for ref in (gpa_v, gpb_v, gwo_v, loss_ref, gfg_ref, gbm_ref):
                ref[...] = jnp.zeros(ref.shape, F32)
            for cp in loads:
                cp.wait()

        ya_v = ya_ref[...]
        yb_v = yb_ref[...]
        out_a = jnp.dot(ya_v, pa_v[...], preferred_element_type=F32)
        out_b = jnp.dot(yb_v, pb_v[...], preferred_element_type=F32)
        bm = bm_ref[...]
        g_a = _sigmoid(gma_ref[0] + bm[:, 0:D])
        g_b = _sigmoid(gmb_ref[0] + bm[:, D:2 * D])
        mixed = g_a * out_a + g_b * out_b
        mixb = mixed.astype(BF16)
        xo = x_ref[...] + jnp.dot(mixb, wo_v[...], preferred_element_type=F32)
        r = lax.rsqrt(jnp.mean(xo * xo, axis=-1, keepdims=True) + EPS)
        xn = xo * r
        fg = fg_ref[...]
        e = xn * fg - t_ref[...]
        loss_ref[...] += 0.5 * jnp.sum(jnp.mean(e * e, axis=-1, keepdims=True))
        dy = e * (1.0 / D)
        gfg_ref[...] += jnp.sum(dy * xn, axis=0, keepdims=True)
        dxn = dy * fg
        dx2 = r * (dxn - xn * jnp.mean(dxn * xn, axis=-1, keepdims=True))
        dx2_ref[...] = dx2
        dx2b = dx2.astype(BF16)
        d_mixed = lax.dot_general(dx2b, wo_v[...], NT_DIMS, preferred_element_type=F32)
        gwo_v[...] += by_owner(lax.dot_general(mixb, dx2b, TN_DIMS, preferred_element_type=F32))
        d_oa = (d_mixed * g_a).astype(BF16)
        d_ob = (d_mixed * g_b).astype(BF16)
        dgm_a = (d_mixed * out_a) * (g_a * (1.0 - g_a))
        dgm_b = (d_mixed * out_b) * (g_b * (1.0 - g_b))
        gbm_ref[:, 0:D] += jnp.sum(dgm_a, axis=0, keepdims=True)
        gbm_ref[:, D:2 * D] += jnp.sum(dgm_b, axis=0, keepdims=True)
        dgm_ref[0] = dgm_a.astype(BF16)
        dgm_ref[1] = dgm_b.astype(BF16)
        dya_ref[...] = lax.dot_general(d_oa, pa_v[...], NT_DIMS, preferred_element_type=F32)
        dyb_ref[...] = lax.dot_general(d_ob, pb_v[...], NT_DIMS, preferred_element_type=F32)
        gpa_v[...] += by_owner(lax.dot_general(ya_v, d_oa, TN_DIMS, preferred_element_type=F32))
        gpb_v[...] += by_owner(lax.dot_general(yb_v, d_ob, TN_DIMS, preferred_element_type=F32))

        @pl.when(i == ni - 1)
        def _():
            for cp in stores:
                cp.start()
            for cp in stores:
                cp.wait()

    rows = pl.BlockSpec((tm, D), lambda i: (i, 0))
    rep = lambda shape: pl.BlockSpec(shape, lambda i: (0,) * len(shape))
    return pl.pallas_call(
        body, name="mid", grid=(ni,),
        in_specs=[rows, rows,
                  pl.BlockSpec((1, tm, D), lambda i: (6, i, 0)), pl.BlockSpec((1, tm, D), lambda i: (7, i, 0)),
                  rep((1, 2 * D)), rows, rows, rep((1, D)), ANY, ANY, ANY],
        out_specs=[rows, rows, rows, pl.BlockSpec((2, tm, D), lambda i: (0, i, 0)),
                   rep((8, BD)), rep((1, D)), rep((1, 2 * D)), ANY],
        out_shape=[SDS((n, D), F32), SDS((n, D), F32), SDS((n, D), F32), SDS((2, n, D), BF16),
                   SDS((8, BD), F32), SDS((1, D), F32), SDS((1, 2 * D), F32),
                   SDS((NB, MID_ROWS, D), F32)],
        scratch_shapes=[pltpu.VMEM((D, D), BF16)] * 3 + [pltpu.VMEM((NB, BD, D), F32)] * 3 + [pltpu.SemaphoreType.DMA((3,))],
        compiler_params=_params(60),
    )(ya, yb, z, z, b_merge, x2, tgt, fin_g, pa, pb, wo)


def _dz_specs(tm, ni, row_major):
    if row_major:
        ia = lambda i, j: (jnp.minimum(j, 1), i, 0)
        ib = lambda i, j: (jnp.clip(j - 2, 0, 3), i, 0)
        im = lambda i, j: (jnp.clip(j - 6, 0, 1), i, 0)
    else:
        last = ni - 1
        ia = lambda j, i: (jnp.minimum(j, 1), jnp.where(j < 2, i, last), 0)
        ib = lambda j, i: (jnp.clip(j - 2, 0, 3), jnp.where(j < 2, 0, jnp.where(j < 6, i, last)), 0)
        im = lambda j, i: (jnp.clip(j - 6, 0, 1), jnp.where(j < 6, 0, i), 0)
    return [pl.BlockSpec((1, tm, D), f) for f in (ia, ib, im)]


def _inproj_bwd_x(dza, dzb, dzm, w_all, x2, dx2, norm_g, after):
    n = x2.shape[0]
    tm = 512
    ni = n // tm

    def body(dza_ref, dzb_ref, dzm_ref, w_ref, x_ref, dx2_ref, g_ref, after_ref, gx_ref, gg_ref, acc):
        i, j = pl.program_id(0), pl.program_id(1)

        @pl.when((i == 0) & (j == 0))
        def _():
            gg_ref[...] = jnp.zeros((1, D), F32)

        @pl.when(j == 0)
        def _():
            acc[...] = jnp.zeros((tm, D), F32)

        def add(ref):
            acc[...] += lax.dot_general(ref[0], w_ref[0], NT_DIMS, preferred_element_type=F32)

        pl.when(j < 2)(lambda: add(dza_ref))
        pl.when((j >= 2) & (j < 6))(lambda: add(dzb_ref))
        pl.when(j >= 6)(lambda: add(dzm_ref))

        @pl.when(j == NB - 1)
        def _():
            x = x_ref[...]
            r = lax.rsqrt(jnp.mean(x * x, axis=-1, keepdims=True) + EPS)
            xn = x * r
            dh = acc[...]
            gg_ref[...] += jnp.sum(dh * xn, axis=0, keepdims=True)
            dxn = dh * g_ref[...]
            gx_ref[...] = dx2_ref[...] + r * (dxn - xn * jnp.mean(dxn * xn, axis=-1, keepdims=True))

    rows = pl.BlockSpec((tm, D), lambda i, j: (i, 0))
    return pl.pallas_call(
        body, name="inproj_bwd_x", grid=(ni, NB),
        in_specs=_dz_specs(tm, ni, True) + [pl.BlockSpec((1, D, D), lambda i, j: (j, 0, 0)), rows, rows,
                                             pl.BlockSpec((1, D), lambda i, j: (0, 0)), ANY],
        out_specs=[rows, pl.BlockSpec((1, D), lambda i, j: (0, 0))],
        out_shape=[SDS((n, D), F32), SDS((1, D), F32)],
        scratch_shapes=[pltpu.VMEM((tm, D), F32)],
        compiler_params=_params(48),
    )(dza, dzb, dzm, w_all, x2, dx2, norm_g, after)


def _inproj_bwd_w(dza, dzb, dzm, h_all, g_m):
    n = h_all.shape[0]
    tm = min(n, 1024)
    ni = n // tm

    def body(dza_ref, dzb_ref, dzm_ref, h_ref, gm_hbm, gw_ref, got_w, got_m, stage, send_sems, recv_sems):
        j, i = pl.program_id(0), pl.program_id(1)
        x, y, c = _place()
        sibling = (x, y, 1 - c)

        def send_w(q):
            return pltpu.make_async_remote_copy(
                src_ref=stage.at[q % 2], dst_ref=got_w.at[q], send_sem=send_sems.at[q], recv_sem=recv_sems.at[q],
                device_id=sibling, device_id_type=MESH)

        def send_m(q):
            return pltpu.make_async_remote_copy(
                src_ref=gm_hbm.at[2 * q + (1 - c)], dst_ref=got_m.at[q], send_sem=send_sems.at[4 + q],
                recv_sem=recv_sems.at[4 + q], device_id=sibling, device_id_type=MESH)

        @pl.when((j == 0) & (i == 0))
        def _():
            for q in range(4):
                send_m(q).start()

        @pl.when(i == 0)
        def _():
            gw_ref[...] = jnp.zeros((1, D, D), F32)

        def add(ref):
            gw_ref[0] += lax.dot_general(h_ref[...], ref[0], TN_DIMS, preferred_element_type=F32)

        pl.when(j < 2)(lambda: add(dza_ref))
        pl.when((j >= 2) & (j < 6))(lambda: add(dzb_ref))
        pl.when(j >= 6)(lambda: add(dzm_ref))

        for q in range(4):
            @pl.when((i == ni - 1) & (j == 2 * q + 1 - c))
            def _(q=q):
                if q >= 2:
                    send_w(q - 2).wait_send()
                stage[q % 2] = gw_ref[0].astype(BF16)
                send_w(q).start()

        @pl.when((j == NB - 1) & (i == ni - 1))
        def _():
            for q in (2, 3):
                send_w(q).wait_send()
            for q in range(4):
                send_w(q).wait_recv()
                send_m(q).wait_send()
                send_m(q).wait_recv()

    return pl.pallas_call(
        body, name="inproj_bwd_w", grid=(NB, ni),
        in_specs=_dz_specs(tm, ni, False) + [pl.BlockSpec((tm, D), lambda j, i: (i, 0)), ANY],
        out_specs=[pl.BlockSpec((1, D, D), lambda j, i: (j, 0, 0)), ANY, ANY],
        out_shape=[SDS((NB, D, D), F32), SDS((4, D, D), BF16), SDS((4,) + g_m.shape[1:], F32)],
        scratch_shapes=[pltpu.VMEM((2, D, D), BF16), pltpu.SemaphoreType.DMA((8,)), pltpu.SemaphoreType.DMA((8,))],
        compiler_params=_params(48),
    )(dza, dzb, dzm, h_all, g_m)


def _adamw(w, g, m, v):
    rows, cols = w.shape
    tr = _row_tile(rows)

    spec = pl.BlockSpec((tr, cols), lambda i: (i, 0))
    return pl.pallas_call(
        functools.partial(_adam_refs), name="adamw", grid=(rows // tr,), in_specs=[spec] * 4, out_specs=[spec] * 3,
        out_shape=[SDS((rows, cols), F32)] * 3, compiler_params=_params(32),
    )(w, g, m, v)


def _adam_refs(w_ref, g_ref, m_ref, v_ref, d_ref, nm_ref, nv_ref):
    gv = g_ref[...]
    nm = ADAM_B1 * m_ref[...] + (1.0 - ADAM_B1) * gv
    nv = ADAM_B2 * v_ref[...] + (1.0 - ADAM_B2) * (gv * gv)
    m_hat = nm / (1.0 - ADAM_B1 ** ADAM_STEP)
    v_hat = nv / (1.0 - ADAM_B2 ** ADAM_STEP)
    d_ref[...] = -ADAM_LR * (m_hat / (jnp.sqrt(v_hat) + ADAM_EPS) + ADAM_WD * w_ref[...])
    nm_ref[...] = nm
    nv_ref[...] = nv


def _adamw_small(ws, gs, ms, vs):
    k = len(ws)

    def body(*refs):
        for i in range(k):
            _adam_refs(*[refs[part * k + i] for part in range(7)])

    shapes = [SDS(w.shape, F32) for w in ws]
    out = pl.pallas_call(body, name="adamw_small", out_shape=shapes * 3, compiler_params=_params(32))(*ws, *gs, *ms, *vs)
    return out[:k], out[k:2 * k], out[2 * k:]


def _allgather(blocks, dtypes, name):
    na = len(blocks)

    def body(*refs):
        ins, outs, stages = refs[:na], refs[na:2 * na], refs[2 * na:3 * na]
        send_sems, recv_sems, local_sems = refs[3 * na:]
        x, y, c = _place()
        me, sibling = (x, y, c), (x, y, 1 - c)
        chips = [(1 - x, y), (x, 1 - y), (1 - x, 1 - y)]
        blk = lambda p: 4 * p[0] + 2 * p[1] + p[2]

        def copy(a, k, block, to, src=None):
            return pltpu.make_async_remote_copy(
                src_ref=outs[a].at[blk(block)] if src is None else src, dst_ref=outs[a].at[blk(block)],
                send_sem=send_sems.at[7 * a + k], recv_sem=recv_sems.at[7 * a + k],
                device_id=to, device_id_type=MESH)

        mine, first, passed = [], [], []
        for a in range(na):
            stages[a][...] = ins[a][...].astype(dtypes[a])
            mine.append(pltpu.make_async_copy(stages[a], outs[a].at[blk(me)], local_sems.at[a]))
            mine[-1].start()
            first.append(copy(a, 0, me, sibling, src=stages[a]))
            first += [copy(a, 1 + j, me, (*chip, c), src=stages[a]) for j, chip in enumerate(chips)]
        for cp in first:
            cp.start()
        for j, chip in enumerate(chips):
            for a in range(na):
                copy(a, 1 + j, (*chip, c), me).wait_recv()
                passed.append(copy(a, 4 + j, (*chip, c), sibling))
                passed[-1].start()
        for a in range(na):
            copy(a, 0, sibling, me).wait_recv()
            for j, chip in enumerate(chips):
                copy(a, 4 + j, (*chip, 1 - c), me).wait_recv()
        for cp in first + passed:
            cp.wait_send()
        for cp in mine:
            cp.wait()

    return pl.pallas_call(
        body, name=name,
        in_specs=[pl.BlockSpec(memory_space=pltpu.VMEM)] * na, out_specs=[ANY] * na,
        out_shape=[SDS((NB,) + b.shape, dt) for b, dt in zip(blocks, dtypes)],
        scratch_shapes=[pltpu.VMEM(b.shape, dt) for b, dt in zip(blocks, dtypes)]
        + [pltpu.SemaphoreType.DMA((7 * na,)), pltpu.SemaphoreType.DMA((7 * na,)), pltpu.SemaphoreType.DMA((na,))],
        compiler_params=_params(40),
    )(*blocks)


HBM = pl.BlockSpec(memory_space=pltpu.HBM)
SEMS = pl.BlockSpec(memory_space=pltpu.SEMAPHORE)
EFFECT = pltpu.SideEffectType.DATAFLOW_SIDE_EFFECTING


def _chip_copies(srcs, lands, send_sems, recv_sems):
    x, y, c = _place()
    return [pltpu.make_async_remote_copy(
        src_ref=srcs[a].at[slot], dst_ref=lands[a].at[slot],
        send_sem=send_sems.at[3 * a + slot], recv_sem=recv_sems.at[3 * a + slot],
        device_id=(px, py, c), device_id_type=MESH)
        for a in range(len(srcs)) for slot, (px, py) in enumerate(_other_chips(x, y))]


def _rs_chips_start(ps):
    na = len(ps)

    def body(*refs):
        srcs, lands = refs[:na], refs[na:2 * na]
        send_sems, recv_sems = refs[2 * na], refs[2 * na + 1]
        token = refs[-1]
        for cp in _chip_copies(srcs, lands, send_sems, recv_sems):
            cp.start()
        token[...] = jnp.zeros_like(token)

    hbm = lambda a: pltpu.HBM(a.shape, a.dtype)
    out = pl.pallas_call(
        body, name="rs_chips_start",
        out_shape=(pltpu.SemaphoreType.DMA((3 * na,)), pltpu.SemaphoreType.DMA((3 * na,)),
                   *[hbm(p) for p in ps], *[hbm(p) for p in ps], SDS((8, BD), F32)),
        in_specs=[HBM] * (2 * na),
        out_specs=(SEMS, SEMS, *[HBM] * (2 * na), pl.BlockSpec(memory_space=pltpu.VMEM)),
        input_output_aliases={i: 2 + i for i in range(2 * na)},
        compiler_params=pltpu.CompilerParams(has_side_effects=EFFECT),
    )(*[pltpu.with_memory_space_constraint(p, pltpu.HBM) for p in ps],
      *[pltpu.with_memory_space_constraint(lax.empty(p.shape, p.dtype), pltpu.HBM) for p in ps])
    return out[0], out[1], out[2:2 + na], out[2 + na:2 + 2 * na], out[-1]


def _rs_chips_wait(send_sems, recv_sems, srcs, lands, after):
    na = len(srcs)

    def body(*refs):
        srcs_r, lands_r = refs[:na], refs[na:2 * na]
        send_r, recv_r = refs[2 * na], refs[2 * na + 1]
        copies = _chip_copies(srcs_r, lands_r, send_r, recv_r)
        for cp in copies:
            cp.wait_send()
        for cp in copies:
            cp.wait_recv()

    hbm = lambda a: pltpu.HBM(a.shape, a.dtype)
    out = pl.pallas_call(
        body, name="rs_chips_wait",
        out_shape=(*[hbm(p) for p in srcs], *[hbm(p) for p in lands]),
        in_specs=[HBM] * (2 * na) + [SEMS, SEMS, ANY],
        out_specs=tuple([HBM] * (2 * na)),
        input_output_aliases={i: i for i in range(2 * na)},
        compiler_params=pltpu.CompilerParams(has_side_effects=EFFECT),
    )(*srcs, *lands, send_sems, recv_sems, after)
    return out[na:]


def _add_sibling(place, g, a_in):
    _, r, cols = g.shape
    tr = _row_tile(r)

    def chip(k, pr):
        qx = pr[0] if k in (1, 3) else 1 - pr[0]
        qy = pr[1] if k in (0, 3) else 1 - pr[1]
        return 2 * qx + qy

    def body(place_ref, *refs):
        g_refs, a_refs, (out_ref, own_ref) = refs[0:4], refs[4:8], refs[8:10]
        for k in range(3):
            out_ref[k] = (g_refs[k][0] + a_refs[k][0].astype(F32)).astype(BF16)
        own_ref[...] = g_refs[3][0] + a_refs[3][0].astype(F32)

    mine = lambda k: pl.BlockSpec((1, tr, cols), lambda i, pr: (2 * chip(k, pr) + pr[2], i, 0))
    theirs = lambda k: pl.BlockSpec((1, tr, cols), lambda i, pr: (chip(k, pr), i, 0))
    return pl.pallas_call(
        body, name="add_sibling",
        grid_spec=pltpu.PrefetchScalarGridSpec(
            num_scalar_prefetch=1, grid=(r // tr,),
            in_specs=[mine(k) for k in range(4)] + [theirs(k) for k in range(4)],
            out_specs=[pl.BlockSpec((3, tr, cols), lambda i, pr: (0, i, 0)),
                       pl.BlockSpec((tr, cols), lambda i, pr: (i, 0))]),
        out_shape=[SDS((3, r, cols), BF16), SDS((r, cols), F32)], compiler_params=_params(48),
    )(place, *[g] * 4, *[a_in] * 4)


def _add_chips(own, b_in):
    r, cols = own.shape
    tr = _row_tile(r)

    def body(p_ref, b0_ref, b1_ref, b2_ref, o_ref):
        o_ref[...] = ((p_ref[...] + b0_ref[0].astype(F32)) + b1_ref[0].astype(F32)) + b2_ref[0].astype(F32)

    slot = lambda k: pl.BlockSpec((1, tr, cols), lambda i: (k, i, 0))
    spec = pl.BlockSpec((tr, cols), lambda i: (i, 0))
    return pl.pallas_call(
        body, name="add_chips", grid=(r // tr,), in_specs=[spec, slot(0), slot(1), slot(2)], out_specs=spec,
        out_shape=SDS((r, cols), F32), compiler_params=_params(32),
    )(own, b_in, b_in, b_in)


VEC_NAMES = ("b_merge", "conv_b", "rg_bx", "rg_ba", "rg_lambda", "hg_lb_logits", "hg_norm_g", "final_norm_g")
REP_NAMES = ("rg_wx", "rg_wa", "norm_g") + VEC_NAMES
SMALL_AT = 3 * BD
SMALL_ROWS = 48
MID_ROWS = 448


def _sum_blocks(parts):
    def body(p_ref, o_ref):
        acc = p_ref[0]
        for k in range(1, NB):
            acc = acc + p_ref[k]
        o_ref[...] = acc

    return pl.pallas_call(body, name="sum_blocks", out_shape=SDS(parts.shape[1:], F32))(parts)


def _pack_rows(arrays, width, row_multiple=8):
    flat = jnp.concatenate([a.reshape(-1) for a in arrays])
    rows = -(-flat.shape[0] // width)
    rows = -(-rows // row_multiple) * row_multiple
    return jnp.pad(flat, (0, rows * width - flat.shape[0])).reshape(rows, width)


def _unpack(flat, like):
    out, off = [], 0
    for a in like:
        out.append(flat[off:off + a.size].reshape(a.shape))
        off += a.size
    return out


def kernel(x, w_in, b_merge, conv_w, conv_b, rg_wx, rg_bx, rg_wa, rg_ba, rg_lambda, hg_lb_logits, hg_norm_g, proj_a, proj_b, w_out, norm_g, final_norm_g, loss_target, m_w_in, m_b_merge, m_conv_w, m_conv_b, m_rg_wx, m_rg_bx, m_rg_wa, m_rg_ba, m_rg_lambda, m_hg_lb_logits, m_hg_norm_g, m_proj_a, m_proj_b, m_w_out, m_norm_g, m_final_norm_g, v_w_in, v_b_merge, v_conv_w, v_conv_b, v_rg_wx, v_rg_bx, v_rg_wa, v_rg_ba, v_rg_lambda, v_hg_lb_logits, v_hg_norm_g, v_proj_a, v_proj_b, v_w_out, v_norm_g, v_final_norm_g):
    weights = dict(w_in=w_in, b_merge=b_merge, conv_w=conv_w, conv_b=conv_b, rg_wx=rg_wx, rg_bx=rg_bx, rg_wa=rg_wa,
                   rg_ba=rg_ba, rg_lambda=rg_lambda, hg_lb_logits=hg_lb_logits, hg_norm_g=hg_norm_g, proj_a=proj_a,
                   proj_b=proj_b, w_out=w_out, norm_g=norm_g, final_norm_g=final_norm_g)
    mom1 = dict(w_in=m_w_in, b_merge=m_b_merge, conv_w=m_conv_w, conv_b=m_conv_b, rg_wx=m_rg_wx, rg_bx=m_rg_bx,
                rg_wa=m_rg_wa, rg_ba=m_rg_ba, rg_lambda=m_rg_lambda, hg_lb_logits=m_hg_lb_logits,
                hg_norm_g=m_hg_norm_g, proj_a=m_proj_a, proj_b=m_proj_b, w_out=m_w_out, norm_g=m_norm_g,
                final_norm_g=m_final_norm_g)
    mom2 = dict(w_in=v_w_in, b_merge=v_b_merge, conv_w=v_conv_w, conv_b=v_conv_b, rg_wx=v_rg_wx, rg_bx=v_rg_bx,
                rg_wa=v_rg_wa, rg_ba=v_rg_ba, rg_lambda=v_rg_lambda, hg_lb_logits=v_hg_lb_logits,
                hg_norm_g=v_hg_norm_g, proj_a=v_proj_a, proj_b=v_proj_b, w_out=v_w_out, norm_g=v_norm_g,
                final_norm_g=v_final_norm_g)
    order = list(weights)
    nb, s_len, _ = x.shape
    n = nb * s_len
    px, py, pc = _place()
    place = jnp.stack([px, py, pc]).astype(jnp.int32)

    x2 = x.reshape(n, D)
    cw_blk = jnp.pad(conv_w[0], ((0, 4), (0, 0)))
    order_ids = jnp.stack([_block_id(p) for p in _arrival_order(px, py, pc)]).astype(jnp.int32)
    z, h_all, w_all, pa_all, pb_all, wo_all, cw_all = _gather_inproj(
        order_ids, x2, norm_g, [w_in[0], proj_a[0], proj_b[0], w_out[0], cw_blk], [BF16, BF16, BF16, BF16, F32])
    pa_full, pb_full, wo_full = (a.reshape(D, D) for a in (pa_all, pb_all, wo_all))
    cw8 = cw_all.transpose(1, 0, 2).reshape(8, D)
    wx_b, wa_b = rg_wx[0].astype(BF16), rg_wa[0].astype(BF16)
    cb, bx, ba = conv_b, rg_bx.reshape(1, D), rg_ba.reshape(1, D)
    fin_g = final_norm_g.reshape(1, D)

    hlru, ya = _lru_fwd(z, cw8, cb, wx_b, wa_b, bx, ba, rg_lambda, nb, s_len)
    o_all, yb, st_all = _hgrn_fwd(z, hg_lb_logits, hg_norm_g, nb, s_len)

    (dx2, dya, dyb, dzm, loss_acc, g_fin, g_bm, g_mid) = _mid(
        ya, yb, z, b_merge, x2, loss_target.reshape(n, D), fin_g, pa_full, pb_full, wo_full)
    dzb, g_lg, g_hg = _hgrn_bwd(z, o_all, st_all, dyb, hg_lb_logits, hg_norm_g, nb, s_len)
    dza, g_cw8, g_cb, g_wx, g_wa, g_bx, g_ba, g_lam = _lru_bwd(
        z, hlru, dya, cw8, cb, wx_b, wa_b, bx, ba, rg_lambda, nb, s_len)

    part = dict(b_merge=g_bm, conv_b=g_cb, rg_bx=g_bx, rg_ba=g_ba, rg_lambda=g_lam, hg_lb_logits=g_lg,
                hg_norm_g=g_hg, final_norm_g=g_fin)
    vec = _pack_rows([part[k] for k in VEC_NAMES], BD)
    vec = jnp.pad(vec, ((0, 16 * NB - vec.shape[0]), (0, 0))).reshape(NB, 2, D)
    rows8 = lambda a: jnp.pad(a, ((0, 0), (0, 8 - a.shape[1]), (0, 0)))
    small = jnp.concatenate([g_wx.reshape(NB, 16, D), g_wa.reshape(NB, 16, D),
                             rows8(g_cw8.reshape(8, NB, BD).transpose(1, 0, 2).reshape(NB, 1, D)), rows8(vec),
                             jnp.zeros((NB, MID_ROWS - SMALL_AT - SMALL_ROWS, D), F32)], axis=1)
    g_m = lax.dynamic_update_slice(g_mid, small, (0, SMALL_AT, 0))
    g_w, w_from_sibling, m_from_sibling = _inproj_bwd_w(dza, dzb, dzm, h_all, g_m)
    w_out_bf, w_own = _add_sibling(place, g_w, w_from_sibling)
    m_out_bf, m_own = _add_sibling(place, g_m, m_from_sibling)
    send_sems, recv_sems, srcs, lands, token = _rs_chips_start([w_out_bf, m_out_bf])
    grad_x, g_ng = _inproj_bwd_x(dza, dzb, dzm, w_all, x2, dx2, norm_g, token)
    from_chips = _rs_chips_wait(send_sems, recv_sems, srcs, lands, grad_x)
    r_w = _add_chips(w_own, from_chips[0])
    r_m = _add_chips(m_own, from_chips[1])
    row = lax.broadcasted_iota(jnp.int32, (8, D), 0)
    mine = jnp.where(row == 0, g_ng, jnp.where(row == 1, loss_acc[0:1, 0:1], 0.0))
    tail = jnp.concatenate([r_m[SMALL_AT:SMALL_AT + SMALL_ROWS], mine], axis=0)
    (tail_all,) = _allgather([tail], [F32], "gather_small_grads")
    summed = _sum_blocks(tail_all[:, SMALL_ROWS:SMALL_ROWS + 8])

    grads = dict(w_in=r_w.reshape(1, D, D),
                 proj_a=r_m[0:BD].reshape(1, BD, D), proj_b=r_m[BD:2 * BD].reshape(1, BD, D),
                 w_out=r_m[2 * BD:3 * BD].reshape(1, BD, D),
                 conv_w=r_m[SMALL_AT + 32].reshape(8, BD)[0:4].reshape(1, 4, BD),
                 rg_wx=tail_all[:, 0:16].reshape(1, NB, BD, BD), rg_wa=tail_all[:, 16:32].reshape(1, NB, BD, BD),
                 norm_g=summed[0:1])
    vec_all = tail_all[:, 40:42].reshape(-1)
    for k, gk in zip(VEC_NAMES, _unpack(vec_all, [weights[k] for k in VEC_NAMES])):
        grads[k] = gk

    delta, new_m, new_v = {}, {}, {}
    for k in ("w_in", "proj_a", "proj_b", "w_out"):
        shp = weights[k].shape
        two = lambda a: a.reshape(shp[1], shp[2])
        d_k, m_k, v_k = _adamw(two(weights[k]), two(grads[k]), two(mom1[k]), two(mom2[k]))
        delta[k], new_m[k], new_v[k] = d_k.reshape(shp), m_k.reshape(shp), v_k.reshape(shp)
    rep = list(REP_NAMES) + ["conv_w"]
    flat2 = lambda a: a.reshape(-1, a.shape[-1])
    outs = _adamw_small(*[[flat2(t[k]) for k in rep] for t in (weights, grads, mom1, mom2)])
    for tgt, arrays in zip((delta, new_m, new_v), outs):
        for k, a in zip(rep, arrays):
            tgt[k] = a.reshape(weights[k].shape)

    return (summed[1, 0],grad_x.reshape(x.shape), *[grads[k] for k in order], *[delta[k] for k in order],
            *[new_m[k] for k in order], *[new_v[k] for k in order])
```

```python
import functools

import jax
import jax.numpy as jnp
from jax import lax
from jax.experimental import pallas as pl
from jax.experimental.pallas import tpu as pltpu

F32 = jnp.float32
BF16 = jnp.bfloat16
SDS = jax.ShapeDtypeStruct
MESH = pl.DeviceIdType.MESH
ANY = pl.BlockSpec(memory_space=pl.ANY)

D = 1024
NB = 8
BD = D // NB
CHUNK = 64
EPS = 1e-6
LRU_C = 8.0
HG_SCALE = BD ** -0.5
ADAM_LR, ADAM_B1, ADAM_B2, ADAM_EPS, ADAM_WD, ADAM_STEP = 0.001, 0.9, 0.999, 1e-08, 0.01, 10

NT_DIMS = (((1,), (1,)), ((), ()))
TN_DIMS = (((0,), (0,)), ((), ()))


def _params(vmem_mib):
    return pltpu.CompilerParams(vmem_limit_bytes=vmem_mib << 20)


def _row_tile(rows, most=256):
    assert rows % 8 == 0
    return max(t for t in range(8, min(rows, most) + 1, 8) if rows % t == 0)


def _sigmoid(v):
    return 0.5 * (jnp.tanh(0.5 * v) + 1.0)


def _groups(v):
    return v.reshape(v.shape[0] // 8, 8, v.shape[1])


def _softplus_neg(lam):
    t = -lam
    e = jnp.exp(-jnp.abs(t))
    w = 1.0 + e
    d = w - 1.0
    l1p = jnp.where(d == 0.0, e, jnp.log(w) * (e / jnp.where(d == 0.0, 1.0, d)))
    return jnp.maximum(t, 0.0) + l1p


def _place():
    return lax.axis_index("x"), lax.axis_index("y"), lax.axis_index("c")


def _other_chips(x, y):
    return [(1 - x, y), (x, 1 - y), (1 - x, 1 - y)]


def _block_id(p):
    return 4 * p[0] + 2 * p[1] + p[2]


def _arrival_order(x, y, c):
    near, far, diag = _other_chips(x, y)
    return [(x, y, c), (x, y, 1 - c), (*near, c), (*far, c), (*near, 1 - c), (*far, 1 - c), (*diag, c), (*diag, 1 - c)]


def _gather_inproj(order_ids, x2, norm_g, blocks, dtypes):
    na = len(blocks)
    n = x2.shape[0]
    tm = min(n, 1024)
    ni = n // tm

    def body(order_ref, x_ref, g_ref, *refs):
        ins, (z_ref, h_ref), outs = refs[:na], refs[na:na + 2], refs[na + 2:2 * na + 2]
        stages = refs[2 * na + 2:3 * na + 2]
        h_full, wbuf, send_sems, recv_sems, local_sems, wsem, hsem = refs[3 * na + 2:]
        j, i = pl.program_id(0), pl.program_id(1)
        x, y, c = _place()
        me, sibling = (x, y, c), (x, y, 1 - c)
        chips = _other_chips(x, y)
        small = range(1, na)

        def copy(a, k, block, to, src=None):
            return pltpu.make_async_remote_copy(
                src_ref=outs[a].at[_block_id(block)] if src is None else src, dst_ref=outs[a].at[_block_id(block)],
                send_sem=send_sems.at[7 * a + k], recv_sem=recv_sems.at[7 * a + k],
                device_id=to, device_id_type=MESH)

        def local(a):
            return pltpu.make_async_copy(stages[a], outs[a].at[_block_id(me)], local_sems.at[a])

        def landed(a, slot):
            copy(a, 1 + slot, (*chips[slot], c), me).wait_recv()
            copy(a, 4 + slot, (*chips[slot], c), sibling).start()

        def passed_on(a, slot):
            copy(a, 4 + slot, (*chips[slot], 1 - c), me).wait_recv()

        @pl.when((j == 0) & (i == 0))
        def _():
            for a in range(na):
                stages[a][...] = ins[a][...].astype(dtypes[a])
                local(a).start()
            for a in range(na):
                copy(a, 0, me, sibling, src=stages[a]).start()
                for slot, chip in enumerate(chips):
                    copy(a, 1 + slot, me, (*chip, c), src=stages[a]).start()

        @pl.when(j == 0)
        def _():
            xv = x_ref[...]
            r = lax.rsqrt(jnp.mean(xv * xv, axis=-1, keepdims=True) + EPS)
            hb = ((xv * r) * g_ref[...]).astype(BF16)
            h_full[pl.ds(pl.multiple_of(i * tm, tm), tm), :] = hb

        save_h = pltpu.make_async_copy(h_full, h_ref, hsem)
        pl.when((j == 0) & (i == ni - 1))(save_h.start)

        steps = [
            lambda: local(0).wait(),
            lambda: copy(0, 0, sibling, me).wait_recv(),
            lambda: landed(0, 0),
            lambda: landed(0, 1),
            lambda: passed_on(0, 0),
            lambda: passed_on(0, 1),
            lambda: landed(0, 2),
            lambda: passed_on(0, 2),
        ]
        for k, step in enumerate(steps):
            pl.when((i == 0) & (j == k))(step)

        @pl.when(i == 0)
        def _():
            load = pltpu.make_async_copy(outs[0].at[order_ref[j]], wbuf, wsem)
            load.start()
            load.wait()

        z_ref[0] = jnp.dot(h_full[pl.ds(pl.multiple_of(i * tm, tm), tm), :], wbuf[...], preferred_element_type=F32)

        @pl.when((j == NB - 1) & (i == ni - 1))
        def _():
            save_h.wait()
            for slot in range(3):
                for a in small:
                    landed(a, slot)
            for a in small:
                local(a).wait()
                copy(a, 0, sibling, me).wait_recv()
                for slot in range(3):
                    passed_on(a, slot)
            for a in range(na):
                copy(a, 0, me, sibling, src=stages[a]).wait_send()
                for slot, chip in enumerate(chips):
                    copy(a, 1 + slot, me, (*chip, c), src=stages[a]).wait_send()
                    copy(a, 4 + slot, (*chip, c), sibling).wait_send()

    rows_once = lambda j, i, order: (jnp.where(j == 0, i, ni - 1), 0)
    vmem = pl.BlockSpec(memory_space=pltpu.VMEM)
    return pl.pallas_call(
        body, name="gather_inproj",
        grid_spec=pltpu.PrefetchScalarGridSpec(
            num_scalar_prefetch=1, grid=(NB, ni),
            in_specs=[pl.BlockSpec((tm, D), rows_once), pl.BlockSpec((1, D), lambda j, i, order: (0, 0))] + [vmem] * na,
            out_specs=[pl.BlockSpec((1, tm, D), lambda j, i, order: (order[j], i, 0)), ANY] + [ANY] * na,
            scratch_shapes=[pltpu.VMEM(b.shape, dt) for b, dt in zip(blocks, dtypes)]
            + [pltpu.VMEM((n, D), BF16), pltpu.VMEM((D, D), BF16),
               pltpu.SemaphoreType.DMA((7 * na,)), pltpu.SemaphoreType.DMA((7 * na,)),
               pltpu.SemaphoreType.DMA((na,)), pltpu.SemaphoreType.DMA(()), pltpu.SemaphoreType.DMA(())]),
        out_shape=[SDS((NB, n, D), F32), SDS((n, D), BF16)] + [SDS((NB,) + b.shape, dt) for b, dt in zip(blocks, dtypes)],
        compiler_params=_params(56),
    )(order_ids, x2, norm_g, *blocks)


LRU_T = 256


def _conv(ext, cw, cb):
    t = LRU_T
    acc = ext[5:5 + t, :] * cw[0:1, :] + ext[6:6 + t, :] * cw[1:2, :]
    acc = acc + ext[7:7 + t, :] * cw[2:3, :]
    acc = acc + ext[8:8 + t, :] * cw[3:4, :]
    return cb + acc


def _lru_gates(xa, wx_ref, wa_ref, bx, ba, lam):
    xab = xa.astype(BF16)
    pis, prs = [], []
    for h in range(NB):
        xs = xab[:, h * BD:(h + 1) * BD]
        pis.append(jnp.dot(xs, wx_ref[h], preferred_element_type=F32))
        prs.append(jnp.dot(xs, wa_ref[h], preferred_element_type=F32))
    gi = _sigmoid(jnp.concatenate(pis, axis=1) + bx)
    gr = _sigmoid(jnp.concatenate(prs, axis=1) + ba)
    sp = _softplus_neg(lam)
    log_a = (-LRU_C * gr) * sp
    a = jnp.exp(log_a)
    mult = jnp.sqrt(-jnp.tanh(log_a) * (a * a + 1.0))
    return xab, gi, gr, sp, a, mult


def _lru_fwd(z, cw8, cb, wx, wa, bx, ba, lam, nb, s_len, after):
    n = nb * s_len
    t = LRU_T
    ns = s_len // t

    def body(xp_ref, ga_ref, cw_ref, cb_ref, wx_ref, wa_ref, bx_ref, ba_ref, lam_ref, after_ref,
             h_ref, ya_ref, ext, a_s, u_s, carry):
        @pl.when(pl.program_id(1) == 0)
        def _():
            ext[0:8, :] = jnp.zeros((8, D), F32)
            carry[...] = jnp.zeros((8, D), F32)

        ext[8:8 + t, :] = xp_ref[0]
        xa = _conv(ext, cw_ref[...], cb_ref[...])
        ext[0:8, :] = ext[t:t + 8, :]
        _, gi, _, _, a, mult = _lru_gates(xa, wx_ref, wa_ref, bx_ref[...], ba_ref[...], lam_ref[...])
        u = (mult * gi) * xa
        a, u = _groups(a), _groups(u)
        row = lax.broadcasted_iota(jnp.int32, a.shape, 1)
        for sh in (1, 2, 4):
            a_sh = pltpu.roll(a, sh, 1)
            u_sh = pltpu.roll(u, sh, 1)
            m = row >= sh
            u = jnp.where(m, a * u_sh + u, u)
            a = jnp.where(m, a * a_sh, a)
        a_s[...] = a.reshape(t, D)
        u_s[...] = u.reshape(t, D)

        def step(g, c):
            r = pl.multiple_of(g * 8, 8)
            hg = u_s[pl.ds(r, 8), :] + a_s[pl.ds(r, 8), :] * c
            h_ref[pl.ds(r, 8), :] = hg
            return hg[7:8, :]

        c_out = lax.fori_loop(0, t // 8, step, carry[0:1, :], unroll=4)
        carry[0:1, :] = c_out
        ga = ga_ref[0]
        ya_ref[...] = (h_ref[...] * (ga * _sigmoid(ga))).astype(BF16)

    row_map = lambda b, s: (b * ns + s, 0)
    rep2 = lambda b, s: (0, 0)
    rep3 = lambda b, s: (0, 0, 0)
    return pl.pallas_call(
        body, name="lru_fwd", grid=(nb, ns),
        in_specs=[pl.BlockSpec((1, t, D), lambda b, s: (0, b * ns + s, 0)),
                  pl.BlockSpec((1, t, D), lambda b, s: (1, b * ns + s, 0)),
                  pl.BlockSpec((8, D), rep2), pl.BlockSpec((1, D), rep2),
                  pl.BlockSpec((NB, BD, BD), rep3), pl.BlockSpec((NB, BD, BD), rep3),
                  pl.BlockSpec((1, D), rep2), pl.BlockSpec((1, D), rep2), pl.BlockSpec((1, D), rep2), ANY],
        out_specs=[pl.BlockSpec((t, D), row_map), pl.BlockSpec((t, D), row_map)],
        out_shape=[SDS((n, D), F32), SDS((n, D), BF16)],
        scratch_shapes=[pltpu.VMEM((t + 8, D), F32), pltpu.VMEM((t, D), F32), pltpu.VMEM((t, D), F32),
                        pltpu.VMEM((8, D), F32)],
        compiler_params=_params(48),
    )(z, z, cw8, cb, wx, wa, bx, ba, lam, after)


def _lru_bwd(z, h_all, dya, cw8, cb, wx, wa, bx, ba, lam, nb, s_len):
    n = nb * s_len
    t = LRU_T
    ns = s_len // t
    t8 = t // 8

    def body(xp_ref, xph_ref, ga_ref, h_ref, hh_ref, dya_ref, cw_ref, cb_ref, wx_ref, wa_ref, bx_ref, ba_ref,
             lam_ref, dz_ref, gcw_ref, gcb_ref, gwx_ref, gwa_ref, gbx_ref, gba_ref, glam_ref,
             ext, hext, dext, a_s, u_s, dh_s, carry):
        b, s = pl.program_id(0), pl.program_id(1)
        first_tile = s == ns - 1

        @pl.when((b == 0) & (s == 0))
        def _():
            for ref in (gcw_ref, gcb_ref, gwx_ref, gwa_ref, gbx_ref, gba_ref, glam_ref):
                ref[...] = jnp.zeros(ref.shape, F32)

        @pl.when(s == 0)
        def _():
            dext[t:t + 8, :] = jnp.zeros((8, D), F32)
            carry[...] = jnp.zeros((8, D), F32)

        keep = jnp.where(first_tile, 0.0, 1.0)
        ext[0:8, :] = xph_ref[0] * keep
        ext[8:8 + t, :] = xp_ref[0]
        hext[0:8, :] = hh_ref[...] * keep
        hext[8:8 + t, :] = h_ref[...]
        cw = cw_ref[...]
        lam = lam_ref[...]
        xa = _conv(ext, cw, cb_ref[...])
        xab, gi, gr, sp, a, mult = _lru_gates(xa, wx_ref, wa_ref, bx_ref[...], ba_ref[...], lam)
        h_prev = hext[7:7 + t, :]
        ga = ga_ref[0]
        sg = _sigmoid(ga)
        dya_v = dya_ref[...]
        d_ga = dya_v * h_ref[...] * (sg * (1.0 + ga * (1.0 - sg)))
        g_in = dya_v * (ga * sg)

        rows = lax.broadcasted_iota(jnp.int32, (t, D), 0)
        an = _groups(jnp.where(rows == t - 1, 1.0, pltpu.roll(a, t - 1, 0)))
        u = _groups(g_in)
        row = lax.broadcasted_iota(jnp.int32, an.shape, 1)
        for sh in (1, 2, 4):
            a_sh = pltpu.roll(an, 8 - sh, 1)
            u_sh = pltpu.roll(u, 8 - sh, 1)
            m = row < 8 - sh
            u = jnp.where(m, u + an * u_sh, u)
            an = jnp.where(m, an * a_sh, an)
        a_s[...] = an.reshape(t, D)
        u_s[...] = u.reshape(t, D)

        def step(i, c):
            r = pl.multiple_of((t8 - 1 - i) * 8, 8)
            dg = u_s[pl.ds(r, 8), :] + a_s[pl.ds(r, 8), :] * c
            dh_s[pl.ds(r, 8), :] = dg
            return dg[0:1, :]

        lax.fori_loop(0, t8, step, carry[0:1, :], unroll=4)
        dh = dh_s[...]
        carry[0:1, :] = a[0:1, :] * dh[0:1, :]

        d_a = dh * h_prev
        dux = dh * xa
        d_mult = dux * gi
        d_gi = dux * mult
        d_xa = dh * (mult * gi)
        d_loga = d_a * a - d_mult * ((a * a) / mult)
        d_gr = d_loga * (-LRU_C * sp)
        d_sp = jnp.sum(d_loga * (-LRU_C * gr), axis=0, keepdims=True)
        glam_ref[...] += d_sp * (-_sigmoid(-lam))
        d_pi = d_gi * gi * (1.0 - gi)
        d_pr = d_gr * gr * (1.0 - gr)
        gbx_ref[...] += jnp.sum(d_pi, axis=0, keepdims=True)
        gba_ref[...] += jnp.sum(d_pr, axis=0, keepdims=True)
        dpib = d_pi.astype(BF16)
        dprb = d_pr.astype(BF16)
        back = []
        for h in range(NB):
            cs = slice(h * BD, (h + 1) * BD)
            gwx_ref[h] += lax.dot_general(xab[:, cs], dpib[:, cs], TN_DIMS, preferred_element_type=F32)
            gwa_ref[h] += lax.dot_general(xab[:, cs], dprb[:, cs], TN_DIMS, preferred_element_type=F32)
            back.append(lax.dot_general(dpib[:, cs], wx_ref[h], NT_DIMS, preferred_element_type=F32)
                        + lax.dot_general(dprb[:, cs], wa_ref[h], NT_DIMS, preferred_element_type=F32))
        d_xa = d_xa + jnp.concatenate(back, axis=1)

        dext[0:t, :] = d_xa
        d_xp = dext[3:3 + t, :] * cw[0:1, :] + dext[2:2 + t, :] * cw[1:2, :]
        d_xp = d_xp + dext[1:1 + t, :] * cw[2:3, :]
        d_xp = d_xp + d_xa * cw[3:4, :]
        dext[t:t + 8, :] = d_xa[0:8, :]
        gcb_ref[...] += jnp.sum(d_xa, axis=0, keepdims=True)
        for k in range(4):
            gcw_ref[k:k + 1, :] += jnp.sum(d_xa * ext[5 + k:5 + k + t, :], axis=0, keepdims=True)
        dz_ref[0] = d_xp.astype(BF16)
        dz_ref[1] = d_ga.astype(BF16)

    rb = lambda b, s: b * ns + (ns - 1 - s)
    halo = lambda b, s: jnp.maximum(rb(b, s) * t8 - 1, 0)
    rep2 = lambda b, s: (0, 0)
    rep3 = lambda b, s: (0, 0, 0)
    return pl.pallas_call(
        body, name="lru_bwd", grid=(nb, ns),
        in_specs=[pl.BlockSpec((1, t, D), lambda b, s: (0, rb(b, s), 0)),
                  pl.BlockSpec((1, 8, D), lambda b, s: (0, halo(b, s), 0)),
                  pl.BlockSpec((1, t, D), lambda b, s: (1, rb(b, s), 0)),
                  pl.BlockSpec((t, D), lambda b, s: (rb(b, s), 0)),
                  pl.BlockSpec((8, D), lambda b, s: (halo(b, s), 0)),
                  pl.BlockSpec((t, D), lambda b, s: (rb(b, s), 0)),
                  pl.BlockSpec((8, D), rep2), pl.BlockSpec((1, D), rep2),
                  pl.BlockSpec((NB, BD, BD), rep3), pl.BlockSpec((NB, BD, BD), rep3),
                  pl.BlockSpec((1, D), rep2), pl.BlockSpec((1, D), rep2), pl.BlockSpec((1, D), rep2)],
        out_specs=[pl.BlockSpec((2, t, D), lambda b, s: (0, rb(b, s), 0)),
                   pl.BlockSpec((8, D), rep2), pl.BlockSpec((1, D), rep2),
                   pl.BlockSpec((NB, BD, BD), rep3), pl.BlockSpec((NB, BD, BD), rep3),
                   pl.BlockSpec((1, D), rep2), pl.BlockSpec((1, D), rep2), pl.BlockSpec((1, D), rep2)],
        out_shape=[SDS((2, n, D), BF16), SDS((8, D), F32), SDS((1, D), F32),
                   SDS((NB, BD, BD), F32), SDS((NB, BD, BD), F32),
                   SDS((1, D), F32), SDS((1, D), F32), SDS((1, D), F32)],
        scratch_shapes=[pltpu.VMEM((t + 8, D), F32), pltpu.VMEM((t + 8, D), F32), pltpu.VMEM((t + 8, D), F32),
                        pltpu.VMEM((t, D), F32), pltpu.VMEM((t, D), F32), pltpu.VMEM((t, D), F32),
                        pltpu.VMEM((8, D), F32)],
        compiler_params=_params(56),
    )(z, z, z, h_all, h_all, dya, cw8, cb, wx, wa, bx, ba, lam)


HG_T = 512
HG_NC = HG_T // CHUNK
BNT_DIMS = (((2,), (2,)), ((0,), (0,)))
BNN_DIMS = (((2,), (1,)), ((0,), (0,)))
BTN_DIMS = (((1,), (1,)), ((0,), (0,)))


def _lower_bound(lg):
    m = jnp.max(lg, axis=0, keepdims=True)
    e = jnp.exp(lg - m)
    return e[0:1, :] / jnp.sum(e, axis=0, keepdims=True)


def _tri(upper):
    r = lax.broadcasted_iota(jnp.int32, (HG_NC, CHUNK, CHUNK), 1)
    c = lax.broadcasted_iota(jnp.int32, (HG_NC, CHUNK, CHUNK), 2)
    return (c >= r) if upper else (r >= c)


def _bdot(a, b, dims):
    return lax.dot_general(a, b, dims, preferred_element_type=F32)


def _tri_sums(upper, a):
    tri = _tri(upper).astype(BF16)
    a1 = a.astype(BF16)
    r1 = a - a1.astype(F32)
    a2 = r1.astype(BF16)
    a3 = (r1 - a2.astype(F32)).astype(BF16)
    return _bdot(tri, a1, BNN_DIMS) + (_bdot(tri, a2, BNN_DIMS) + _bdot(tri, a3, BNN_DIMS))


def _chunks(a):
    return a.reshape(HG_NC, CHUNK, BD)


def _hg_tile(q, fp, lb):
    q, fp = _chunks(q), _chunks(fp)
    sig = _sigmoid(fp)
    f = lb + (1.0 - lb) * sig
    log_f = jnp.log(f)
    k = 1.0 - f
    b = _tri_sums(False, log_f)
    b_mid = b[:, CHUNK // 2:CHUNK // 2 + 1, :]
    b_last = b[:, CHUNK - 1:CHUNK, :]
    sq = _sigmoid(q)
    qh = q * sq
    e_qi = jnp.exp(b - b_mid)
    e_ki = jnp.exp(b_mid - b)
    e_qs = jnp.exp(b)
    e_ks = jnp.exp(b_last - b)
    dc = jnp.exp(b_last)
    q_in = (qh * e_qi) * HG_SCALE
    k_in = k * e_ki
    q_st = (qh * e_qs) * HG_SCALE
    k_st = k * e_ks
    att = _bdot(q_in.astype(BF16), k_in.astype(BF16), BNT_DIMS)
    att = jnp.where(_tri(False), att, 0.0)
    return dict(q=q, sig=sig, f=f, k=k, sq=sq, e_qi=e_qi, e_ki=e_ki, e_qs=e_qs, e_ks=e_ks, dc=dc,
                q_in=q_in, k_in=k_in, q_st=q_st, k_st=k_st, att=att)


def _hgrn_fwd(z, lb_logits, hg_g, nb, s_len):
    n = nb * s_len
    t = HG_T
    ns = s_len // t
    nchunk = s_len // CHUNK

    def body(q_ref, f_ref, v_ref, gb_ref, lg_ref, g_ref, o_ref, yb_ref, st_ref, st):
        @pl.when(pl.program_id(1) == 0)
        def _():
            st[...] = jnp.zeros((NB, BD, BD), F32)

        def head(h, carry):
            cols = pl.ds(pl.multiple_of(h * BD, BD), BD)
            lb = _lower_bound(lg_ref[:, cols])
            ck = _hg_tile(q_ref[0, :, cols], f_ref[0, :, cols], lb)
            vb = _chunks(v_ref[0, :, cols]).astype(BF16)
            kv = _bdot(vb, ck["k_st"].astype(BF16), BTN_DIMS)
            states = [st[h]]
            for c in range(HG_NC):
                states.append(states[c] * ck["dc"][c] + kv[c])
            st[h] = states[HG_NC]
            s_in = jnp.stack(states[:HG_NC], axis=0)
            st_ref[h] = s_in
            o = (_bdot(ck["att"].astype(BF16), vb, BNN_DIMS)
                 + _bdot(ck["q_st"].astype(BF16), s_in.astype(BF16), BNT_DIMS))
            o_ref[:, cols] = o.reshape(t, BD)
            r = lax.rsqrt(jnp.mean(o * o, axis=-1, keepdims=True) + EPS)
            gb = _chunks(gb_ref[0, :, cols])
            yb_ref[:, cols] = (((o * r) * g_ref[...]) * (gb * _sigmoid(gb))).astype(BF16).reshape(t, BD)
            return carry

        lax.fori_loop(0, NB, head, 0, unroll=2)

    seg = lambda j: pl.BlockSpec((1, t, D), lambda b, s: (j, b * ns + s, 0))
    tile = pl.BlockSpec((t, D), lambda b, s: (b * ns + s, 0))
    return pl.pallas_call(
        body, name="hgrn_fwd", grid=(nb, ns),
        in_specs=[seg(2), seg(3), seg(4), seg(5),
                  pl.BlockSpec((2, D), lambda b, s: (0, 0)), pl.BlockSpec((1, BD), lambda b, s: (0, 0))],
        out_specs=[tile, tile, pl.BlockSpec((NB, HG_NC, BD, BD), lambda b, s: (b, s, 0, 0))],
        out_shape=[SDS((n, D), F32), SDS((n, D), BF16), SDS((nb * NB, nchunk, BD, BD), F32)],
        scratch_shapes=[pltpu.VMEM((NB, BD, BD), F32)],
        compiler_params=_params(56),
    )(z, z, z, z, lb_logits, hg_g)


def _hgrn_bwd(z, o_all, st_all, dyb, lb_logits, hg_g, nb, s_len):
    n = nb * s_len
    t = HG_T
    ns = s_len // t

    def body(q_ref, f_ref, v_ref, gb_ref, o_ref, st_ref, dyb_ref, lg_ref, g_ref,
             dz_ref, glg_ref, ghg_ref, dst, dlb):
        b, s = pl.program_id(0), pl.program_id(1)

        @pl.when((b == 0) & (s == 0))
        def _():
            ghg_ref[...] = jnp.zeros((1, BD), F32)
            dlb[...] = jnp.zeros((8, D), F32)

        @pl.when(s == 0)
        def _():
            dst[...] = jnp.zeros((NB, BD, BD), F32)

        g = g_ref[...]

        def head(h, carry):
            cols = pl.ds(pl.multiple_of(h * BD, BD), BD)
            lb = _lower_bound(lg_ref[:, cols])
            ck = _hg_tile(q_ref[0, :, cols], f_ref[0, :, cols], lb)
            q = ck["q"]
            vb = _chunks(v_ref[0, :, cols]).astype(BF16)
            gb = _chunks(gb_ref[0, :, cols])
            o = _chunks(o_ref[:, cols])
            dyb_v = _chunks(dyb_ref[:, cols])
            s_in = st_ref[h]

            sgb = _sigmoid(gb)
            r = lax.rsqrt(jnp.mean(o * o, axis=-1, keepdims=True) + EPS)
            ohat = o * r
            d_on = dyb_v * (gb * sgb)
            d_gb = dyb_v * (ohat * g) * (sgb * (1.0 + gb * (1.0 - sgb)))
            ghg_ref[...] += jnp.sum(jnp.sum(d_on * ohat, axis=1), axis=0, keepdims=True)
            tt = d_on * g
            d_o = r * (tt - ohat * jnp.mean(tt * ohat, axis=-1, keepdims=True))
            dob = d_o.astype(BF16)

            attb = ck["att"].astype(BF16)
            q_inb, k_inb = ck["q_in"].astype(BF16), ck["k_in"].astype(BF16)
            q_stb, k_stb = ck["q_st"].astype(BF16), ck["k_st"].astype(BF16)
            d_att = jnp.where(_tri(False), _bdot(dob, vb, BNT_DIMS), 0.0).astype(BF16)
            d_q_in = _bdot(d_att, k_inb, BNN_DIMS)
            d_k_in = _bdot(d_att, q_inb, BTN_DIMS)
            d_q_st = _bdot(dob, s_in.astype(BF16), BNN_DIMS)
            qdo = _bdot(dob, q_stb, BTN_DIMS)
            d_states = [None] * HG_NC + [dst[h]]
            for c in reversed(range(HG_NC)):
                d_states[c] = d_states[c + 1] * ck["dc"][c] + qdo[c]
            dst[h] = d_states[0]
            ds_out = jnp.stack(d_states[1:], axis=0)
            dsb = ds_out.astype(BF16)
            d_v = _bdot(attb, dob, BTN_DIMS) + _bdot(k_stb, dsb, BNT_DIMS)
            d_k_st = _bdot(vb, dsb, BNN_DIMS)
            d_dc = jnp.sum(ds_out * s_in, axis=1, keepdims=True)

            p_qi = d_q_in * ck["q_in"]
            p_ki = d_k_in * ck["k_in"]
            p_qs = d_q_st * ck["q_st"]
            p_ks = d_k_st * ck["k_st"]
            d_qh = (d_q_in * ck["e_qi"] + d_q_st * ck["e_qs"]) * HG_SCALE
            d_k = d_k_in * ck["e_ki"] + d_k_st * ck["e_ks"]
            d_b = (p_qi - p_ki) + (p_qs - p_ks)
            d_b_mid = jnp.sum(p_ki - p_qi, axis=1, keepdims=True)
            d_b_last = jnp.sum(p_ks, axis=1, keepdims=True) + d_dc * ck["dc"]
            rowi = lax.broadcasted_iota(jnp.int32, (HG_NC, CHUNK, BD), 1)
            d_b = d_b + jnp.where(rowi == CHUNK // 2, d_b_mid, 0.0) + jnp.where(rowi == CHUNK - 1, d_b_last, 0.0)
            d_logf = _tri_sums(True, d_b)
            d_f = d_logf / ck["f"] - d_k
            sig, sq = ck["sig"], ck["sq"]
            d_fp = d_f * (1.0 - lb) * (sig * (1.0 - sig))
            dlb[0:1, cols] += jnp.sum(jnp.sum(d_f * (1.0 - sig), axis=1), axis=0, keepdims=True)
            d_q = d_qh * (sq * (1.0 + q * (1.0 - sq)))
            dz_ref[0, :, cols] = d_q.astype(BF16).reshape(t, BD)
            dz_ref[1, :, cols] = d_fp.astype(BF16).reshape(t, BD)
            dz_ref[2, :, cols] = d_v.astype(BF16).reshape(t, BD)
            dz_ref[3, :, cols] = d_gb.astype(BF16).reshape(t, BD)
            return carry

        lax.fori_loop(0, NB, head, 0, unroll=2)

        @pl.when((b == nb - 1) & (s == ns - 1))
        def _():
            lb = _lower_bound(lg_ref[...])
            dl = dlb[0:1, :] * (lb * (1.0 - lb))
            glg_ref[0:1, :] = dl
            glg_ref[1:2, :] = -dl

    rb = lambda b, s: b * ns + (ns - 1 - s)
    seg = lambda j: pl.BlockSpec((1, t, D), lambda b, s: (j, rb(b, s), 0))
    tile = pl.BlockSpec((t, D), lambda b, s: (rb(b, s), 0))
    return pl.pallas_call(
        body, name="hgrn_bwd", grid=(nb, ns),
        in_specs=[seg(2), seg(3), seg(4), seg(5), tile,
                  pl.BlockSpec((NB, HG_NC, BD, BD), lambda b, s: (b, ns - 1 - s, 0, 0)),
                  tile, pl.BlockSpec((2, D), lambda b, s: (0, 0)), pl.BlockSpec((1, BD), lambda b, s: (0, 0))],
        out_specs=[pl.BlockSpec((4, t, D), lambda b, s: (0, rb(b, s), 0)),
                   pl.BlockSpec((2, D), lambda b, s: (0, 0)), pl.BlockSpec((1, BD), lambda b, s: (0, 0))],
        out_shape=[SDS((4, n, D), BF16), SDS((2, D), F32), SDS((1, BD), F32)],
        scratch_shapes=[pltpu.VMEM((NB, BD, BD), F32), pltpu.VMEM((8, D), F32)],
        compiler_params=_params(60),
    )(z, z, z, z, o_all, st_all, dyb, lb_logits, hg_g)


def _mid(ya, yb, z, b_merge, x2, tgt, fin_g, pa, pb, wo):
    n = x2.shape[0]
    tm = 256
    ni = n // tm

    def body(ya_ref, yb_ref, gma_ref, gmb_ref, bm_ref, x_ref, t_ref, fg_ref, pa_hbm, pb_hbm, wo_hbm,
             dx2_ref, dya_ref, dyb_ref, dgm_ref, loss_ref, gfg_ref, gbm_ref, gm_hbm,
             pa_v, pb_v, wo_v, gpa_v, gpb_v, gwo_v, sem):
        i = pl.program_id(0)
        by_owner = lambda g: g.reshape(NB, BD, D)
        loads = [pltpu.make_async_copy(src, dst, sem.at[k])
                 for k, (src, dst) in enumerate(((pa_hbm, pa_v), (pb_hbm, pb_v), (wo_hbm, wo_v)))]
        stores = [pltpu.make_async_copy(src, dst, sem.at[k])
                  for k, (src, dst) in enumerate((g, gm_hbm.at[:, pl.ds(slot * BD, BD), :])
                                                 for slot, g in enumerate((gpa_v, gpb_v, gwo_v)))]

        @pl.when(i == 0)
        def _():
            for cp in loads:
                cp.start()
            for ref in (gpa_v, gpb_v, gwo_v, loss_ref, gfg_ref, gbm_ref):
                ref[...] = jnp.zeros(ref.shape, F32)
            for cp in loads:
                cp.wait()

        ya_v = ya_ref[...]
        yb_v = yb_ref[...]
        out_a = jnp.dot(ya_v, pa_v[...], preferred_element_type=F32)
        out_b = jnp.dot(yb_v, pb_v[...], preferred_element_type=F32)
        bm = bm_ref[...]
        g_a = _sigmoid(gma_ref[0] + bm[:, 0:D])
        g_b = _sigmoid(gmb_ref[0] + bm[:, D:2 * D])
        mixed = g_a * out_a + g_b * out_b
        mixb = mixed.astype(BF16)
        xo = x_ref[...] + jnp.dot(mixb, wo_v[...], preferred_element_type=F32)
        r = lax.rsqrt(jnp.mean(xo * xo, axis=-1, keepdims=True) + EPS)
        xn = xo * r
        fg = fg_ref[...]
        e = xn * fg - t_ref[...]
        loss_ref[...] += 0.5 * jnp.sum(jnp.mean(e * e, axis=-1, keepdims=True))
        dy = e * (1.0 / D)
        gfg_ref[...] += jnp.sum(dy * xn, axis=0, keepdims=True)
        dxn = dy * fg
        dx2 = r * (dxn - xn * jnp.mean(dxn * xn, axis=-1, keepdims=True))
        dx2_ref[...] = dx2
        dx2b = dx2.astype(BF16)
        d_mixed = lax.dot_general(dx2b, wo_v[...], NT_DIMS, preferred_element_type=F32)
        gwo_v[...] += by_owner(lax.dot_general(mixb, dx2b, TN_DIMS, preferred_element_type=F32))
        d_oa = (d_mixed * g_a).astype(BF16)
        d_ob = (d_mixed * g_b).astype(BF16)
        dgm_a = (d_mixed * out_a) * (g_a * (1.0 - g_a))
        dgm_b = (d_mixed * out_b) * (g_b * (1.0 - g_b))
        gbm_ref[:, 0:D] += jnp.sum(dgm_a, axis=0, keepdims=True)
        gbm_ref[:, D:2 * D] += jnp.sum(dgm_b, axis=0, keepdims=True)
        dgm_ref[0] = dgm_a.astype(BF16)
        dgm_ref[1] = dgm_b.astype(BF16)
        dya_ref[...] = lax.dot_general(d_oa, pa_v[...], NT_DIMS, preferred_element_type=F32)
        dyb_ref[...] = lax.dot_general(d_ob, pb_v[...], NT_DIMS, preferred_element_type=F32)
        gpa_v[...] += by_owner(lax.dot_general(ya_v, d_oa, TN_DIMS, preferred_element_type=F32))
        gpb_v[...] += by_owner(lax.dot_general(yb_v, d_ob, TN_DIMS, preferred_element_type=F32))

        @pl.when(i == ni - 1)
        def _():
            for cp in stores:
                cp.start()
            for cp in stores:
                cp.wait()

    rows = pl.BlockSpec((tm, D), lambda i: (i, 0))
    rep = lambda shape: pl.BlockSpec(shape, lambda i: (0,) * len(shape))
    return pl.pallas_call(
        body, name="mid", grid=(ni,),
        in_specs=[rows, rows,
                  pl.BlockSpec((1, tm, D), lambda i: (6, i, 0)), pl.BlockSpec((1, tm, D), lambda i: (7, i, 0)),
                  rep((1, 2 * D)), rows, rows, rep((1, D)), ANY, ANY, ANY],
        out_specs=[rows, rows, rows, pl.BlockSpec((2, tm, D), lambda i: (0, i, 0)),
                   rep((8, BD)), rep((1, D)), rep((1, 2 * D)), ANY],
        out_shape=[SDS((n, D), F32), SDS((n, D), F32), SDS((n, D), F32), SDS((2, n, D), BF16),
                   SDS((8, BD), F32), SDS((1, D), F32), SDS((1, 2 * D), F32),
                   SDS((NB, MID_ROWS, D), F32)],
        scratch_shapes=[pltpu.VMEM((D, D), BF16)] * 3 + [pltpu.VMEM((NB, BD, D), F32)] * 3 + [pltpu.SemaphoreType.DMA((3,))],
        compiler_params=_params(60),
    )(ya, yb, z, z, b_merge, x2, tgt, fin_g, pa, pb, wo)


def _dz_specs(tm, ni, row_major):
    if row_major:
        ia = lambda i, j: (jnp.minimum(j, 1), i, 0)
        ib = lambda i, j: (jnp.clip(j - 2, 0, 3), i, 0)
        im = lambda i, j: (jnp.clip(j - 6, 0, 1), i, 0)
    else:
        last = ni - 1
        ia = lambda j, i: (jnp.minimum(j, 1), jnp.where(j < 2, i, last), 0)
        ib = lambda j, i: (jnp.clip(j - 2, 0, 3), jnp.where(j < 2, 0, jnp.where(j < 6, i, last)), 0)
        im = lambda j, i: (jnp.clip(j - 6, 0, 1), jnp.where(j < 6, 0, i), 0)
    return [pl.BlockSpec((1, tm, D), f) for f in (ia, ib, im)]


def _inproj_bwd_x(dza, dzb, dzm, w_all, x2, dx2, norm_g, after):
    n = x2.shape[0]
    tm = 512
    ni = n // tm

    def body(dza_ref, dzb_ref, dzm_ref, w_ref, x_ref, dx2_ref, g_ref, after_ref, gx_ref, gg_ref, acc):
        i, j = pl.program_id(0), pl.program_id(1)

        @pl.when((i == 0) & (j == 0))
        def _():
            gg_ref[...] = jnp.zeros((1, D), F32)

        @pl.when(j == 0)
        def _():
            acc[...] = jnp.zeros((tm, D), F32)

        def add(ref):
            acc[...] += lax.dot_general(ref[0], w_ref[0], NT_DIMS, preferred_element_type=F32)

        pl.when(j < 2)(lambda: add(dza_ref))
        pl.when((j >= 2) & (j < 6))(lambda: add(dzb_ref))
        pl.when(j >= 6)(lambda: add(dzm_ref))

        @pl.when(j == NB - 1)
        def _():
            x = x_ref[...]
            r = lax.rsqrt(jnp.mean(x * x, axis=-1, keepdims=True) + EPS)
            xn = x * r
            dh = acc[...]
            gg_ref[...] += jnp.sum(dh * xn, axis=0, keepdims=True)
            dxn = dh * g_ref[...]
            gx_ref[...] = dx2_ref[...] + r * (dxn - xn * jnp.mean(dxn * xn, axis=-1, keepdims=True))

    rows = pl.BlockSpec((tm, D), lambda i, j: (i, 0))
    return pl.pallas_call(
        body, name="inproj_bwd_x", grid=(ni, NB),
        in_specs=_dz_specs(tm, ni, True) + [pl.BlockSpec((1, D, D), lambda i, j: (j, 0, 0)), rows, rows,
                                             pl.BlockSpec((1, D), lambda i, j: (0, 0)), ANY],
        out_specs=[rows, pl.BlockSpec((1, D), lambda i, j: (0, 0))],
        out_shape=[SDS((n, D), F32), SDS((1, D), F32)],
        scratch_shapes=[pltpu.VMEM((tm, D), F32)],
        compiler_params=_params(48),
    )(dza, dzb, dzm, w_all, x2, dx2, norm_g, after)


def _inproj_bwd_w(dza, dzb, dzm, h_all, g_m):
    n = h_all.shape[0]
    tm = min(n, 1024)
    ni = n // tm

    def body(dza_ref, dzb_ref, dzm_ref, h_ref, gm_hbm, gw_ref, got_w, got_m, stage, send_sems, recv_sems):
        j, i = pl.program_id(0), pl.program_id(1)
        x, y, c = _place()
        sibling = (x, y, 1 - c)

        def send_w(q):
            return pltpu.make_async_remote_copy(
                src_ref=stage.at[q % 2], dst_ref=got_w.at[q], send_sem=send_sems.at[q], recv_sem=recv_sems.at[q],
                device_id=sibling, device_id_type=MESH)

        def send_m(q):
            return pltpu.make_async_remote_copy(
                src_ref=gm_hbm.at[2 * q + (1 - c)], dst_ref=got_m.at[q], send_sem=send_sems.at[4 + q],
                recv_sem=recv_sems.at[4 + q], device_id=sibling, device_id_type=MESH)

        @pl.when((j == 0) & (i == 0))
        def _():
            for q in range(4):
                send_m(q).start()

        @pl.when(i == 0)
        def _():
            gw_ref[...] = jnp.zeros((1, D, D), F32)

        def add(ref):
            gw_ref[0] += lax.dot_general(h_ref[...], ref[0], TN_DIMS, preferred_element_type=F32)

        pl.when(j < 2)(lambda: add(dza_ref))
        pl.when((j >= 2) & (j < 6))(lambda: add(dzb_ref))
        pl.when(j >= 6)(lambda: add(dzm_ref))

        for q in range(4):
            @pl.when((i == ni - 1) & (j == 2 * q + 1 - c))
            def _(q=q):
                if q >= 2:
                    send_w(q - 2).wait_send()
                stage[q % 2] = gw_ref[0].astype(BF16)
                send_w(q).start()

        @pl.when((j == NB - 1) & (i == ni - 1))
        def _():
            for q in (2, 3):
                send_w(q).wait_send()
            for q in range(4):
                send_w(q).wait_recv()
                send_m(q).wait_send()
                send_m(q).wait_recv()

    return pl.pallas_call(
        body, name="inproj_bwd_w", grid=(NB, ni),
        in_specs=_dz_specs(tm, ni, False) + [pl.BlockSpec((tm, D), lambda j, i: (i, 0)), ANY],
        out_specs=[pl.BlockSpec((1, D, D), lambda j, i: (j, 0, 0)), ANY, ANY],
        out_shape=[SDS((NB, D, D), F32), SDS((4, D, D), BF16), SDS((4,) + g_m.shape[1:], F32)],
        scratch_shapes=[pltpu.VMEM((2, D, D), BF16), pltpu.SemaphoreType.DMA((8,)), pltpu.SemaphoreType.DMA((8,))],
        compiler_params=_params(48),
    )(dza, dzb, dzm, h_all, g_m)


def _adamw(w, g, m, v):
    rows, cols = w.shape
    tr = _row_tile(rows)

    spec = pl.BlockSpec((tr, cols), lambda i: (i, 0))
    return pl.pallas_call(
        functools.partial(_adam_refs), name="adamw", grid=(rows // tr,), in_specs=[spec] * 4, out_specs=[spec] * 3,
        out_shape=[SDS((rows, cols), F32)] * 3, compiler_params=_params(32),
    )(w, g, m, v)


def _adam_refs(w_ref, g_ref, m_ref, v_ref, d_ref, nm_ref, nv_ref):
    gv = g_ref[...]
    nm = ADAM_B1 * m_ref[...] + (1.0 - ADAM_B1) * gv
    nv = ADAM_B2 * v_ref[...] + (1.0 - ADAM_B2) * (gv * gv)
    m_hat = nm / (1.0 - ADAM_B1 ** ADAM_STEP)
    v_hat = nv / (1.0 - ADAM_B2 ** ADAM_STEP)
    d_ref[...] = -ADAM_LR * (m_hat / (jnp.sqrt(v_hat) + ADAM_EPS) + ADAM_WD * w_ref[...])
    nm_ref[...] = nm
    nv_ref[...] = nv


def _adamw_small(ws, gs, ms, vs):
    k = len(ws)

    def body(*refs):
        for i in range(k):
            _adam_refs(*[refs[part * k + i] for part in range(7)])

    shapes = [SDS(w.shape, F32) for w in ws]
    out = pl.pallas_call(body, name="adamw_small", out_shape=shapes * 3, compiler_params=_params(32))(*ws, *gs, *ms, *vs)
    return out[:k], out[k:2 * k], out[2 * k:]


def _allgather(blocks, dtypes, name):
    na = len(blocks)

    def body(*refs):
        ins, outs, stages = refs[:na], refs[na:2 * na], refs[2 * na:3 * na]
        send_sems, recv_sems, local_sems = refs[3 * na:]
        x, y, c = _place()
        me, sibling = (x, y, c), (x, y, 1 - c)
        chips = [(1 - x, y), (x, 1 - y), (1 - x, 1 - y)]
        blk = lambda p: 4 * p[0] + 2 * p[1] + p[2]

        def copy(a, k, block, to, src=None):
            return pltpu.make_async_remote_copy(
                src_ref=outs[a].at[blk(block)] if src is None else src, dst_ref=outs[a].at[blk(block)],
                send_sem=send_sems.at[7 * a + k], recv_sem=recv_sems.at[7 * a + k],
                device_id=to, device_id_type=MESH)

        mine, first, passed = [], [], []
        for a in range(na):
            stages[a][...] = ins[a][...].astype(dtypes[a])
            mine.append(pltpu.make_async_copy(stages[a], outs[a].at[blk(me)], local_sems.at[a]))
            mine[-1].start()
            first.append(copy(a, 0, me, sibling, src=stages[a]))
            first += [copy(a, 1 + j, me, (*chip, c), src=stages[a]) for j, chip in enumerate(chips)]
        for cp in first:
            cp.start()
        for j, chip in enumerate(chips):
            for a in range(na):
                copy(a, 1 + j, (*chip, c), me).wait_recv()
                passed.append(copy(a, 4 + j, (*chip, c), sibling))
                passed[-1].start()
        for a in range(na):
            copy(a, 0, sibling, me).wait_recv()
            for j, chip in enumerate(chips):
                copy(a, 4 + j, (*chip, 1 - c), me).wait_recv()
        for cp in first + passed:
            cp.wait_send()
        for cp in mine:
            cp.wait()

    return pl.pallas_call(
        body, name=name,
        in_specs=[pl.BlockSpec(memory_space=pltpu.VMEM)] * na, out_specs=[ANY] * na,
        out_shape=[SDS((NB,) + b.shape, dt) for b, dt in zip(blocks, dtypes)],
        scratch_shapes=[pltpu.VMEM(b.shape, dt) for b, dt in zip(blocks, dtypes)]
        + [pltpu.SemaphoreType.DMA((7 * na,)), pltpu.SemaphoreType.DMA((7 * na,)), pltpu.SemaphoreType.DMA((na,))],
        compiler_params=_params(40),
    )(*blocks)


HBM = pl.BlockSpec(memory_space=pltpu.HBM)
SEMS = pl.BlockSpec(memory_space=pltpu.SEMAPHORE)
EFFECT = pltpu.SideEffectType.DATAFLOW_SIDE_EFFECTING


def _chip_copies(srcs, lands, send_sems, recv_sems):
    x, y, c = _place()
    return [pltpu.make_async_remote_copy(
        src_ref=srcs[a].at[slot], dst_ref=lands[a].at[slot],
        send_sem=send_sems.at[3 * a + slot], recv_sem=recv_sems.at[3 * a + slot],
        device_id=(px, py, c), device_id_type=MESH)
        for a in range(len(srcs)) for slot, (px, py) in enumerate(_other_chips(x, y))]


PEER_FLIPS = ((0, 0, 1), (1, 0, 0), (0, 1, 0), (1, 1, 0), (1, 0, 1), (0, 1, 1), (1, 1, 1))


def _peer_copies(srcs, lands, send_sems, recv_sems):
    x, y, c = _place()
    flip = lambda v, f: 1 - v if f else v
    return [pltpu.make_async_remote_copy(
        src_ref=srcs[a], dst_ref=lands[a].at[_block_id((x, y, c))],
        send_sem=send_sems.at[7 * a + k], recv_sem=recv_sems.at[7 * a + k],
        device_id=(flip(x, fx), flip(y, fy), flip(c, fc)), device_id_type=MESH)
        for a in range(len(srcs)) for k, (fx, fy, fc) in enumerate(PEER_FLIPS)]


def _split_start(name, copies, per_array, srcs, lands, after=None):
    na = len(srcs)

    def body(*refs):
        send_sems, recv_sems = refs[-2 * na - 3], refs[-2 * na - 2]
        for cp in copies(refs[:na], refs[na:2 * na], send_sems, recv_sems):
            cp.start()
        refs[-1][...] = jnp.zeros_like(refs[-1])

    hbm = lambda a: pltpu.HBM(a.shape, a.dtype)
    pin = lambda a: pltpu.with_memory_space_constraint(a, pltpu.HBM)
    out = pl.pallas_call(
        body, name=name,
        out_shape=(pltpu.SemaphoreType.DMA((per_array * na,)), pltpu.SemaphoreType.DMA((per_array * na,)),
                   *[hbm(a) for a in srcs], *[hbm(a) for a in lands], SDS((8, BD), F32)),
        in_specs=[HBM] * (2 * na) + ([] if after is None else [ANY]),
        out_specs=(SEMS, SEMS, *[HBM] * (2 * na), pl.BlockSpec(memory_space=pltpu.VMEM)),
        input_output_aliases={i: 2 + i for i in range(2 * na)},
        compiler_params=pltpu.CompilerParams(has_side_effects=EFFECT),
    )(*[pin(a) for a in srcs], *[pin(a) for a in lands], *([] if after is None else [after]))
    return out[0], out[1], out[2:2 + na], out[2 + na:2 + 2 * na], out[-1]


def _split_wait(name, copies, started, after):
    send_sems, recv_sems, srcs, lands, _ = started
    na = len(srcs)

    def body(*refs):
        waits = copies(refs[:na], refs[na:2 * na], refs[2 * na], refs[2 * na + 1])
        for cp in waits:
            cp.wait_send()
        for cp in waits:
            cp.wait_recv()

    hbm = lambda a: pltpu.HBM(a.shape, a.dtype)
    out = pl.pallas_call(
        body, name=name,
        out_shape=(*[hbm(a) for a in srcs], *[hbm(a) for a in lands]),
        in_specs=[HBM] * (2 * na) + [SEMS, SEMS, ANY],
        out_specs=tuple([HBM] * (2 * na)),
        input_output_aliases={i: i for i in range(2 * na)},
        compiler_params=pltpu.CompilerParams(has_side_effects=EFFECT),
    )(*srcs, *lands, send_sems, recv_sems, after)
    return out[na:]


def _add_sibling(place, g, a_in):
    _, r, cols = g.shape
    tr = _row_tile(r)

    def chip(k, pr):
        qx = pr[0] if k in (1, 3) else 1 - pr[0]
        qy = pr[1] if k in (0, 3) else 1 - pr[1]
        return 2 * qx + qy

    def body(place_ref, *refs):
        g_refs, a_refs, (out_ref, own_ref) = refs[0:4], refs[4:8], refs[8:10]
        for k in range(3):
            out_ref[k] = (g_refs[k][0] + a_refs[k][0].astype(F32)).astype(BF16)
        own_ref[...] = g_refs[3][0] + a_refs[3][0].astype(F32)

    mine = lambda k: pl.BlockSpec((1, tr, cols), lambda i, pr: (2 * chip(k, pr) + pr[2], i, 0))
    theirs = lambda k: pl.BlockSpec((1, tr, cols), lambda i, pr: (chip(k, pr), i, 0))
    return pl.pallas_call(
        body, name="add_sibling",
        grid_spec=pltpu.PrefetchScalarGridSpec(
            num_scalar_prefetch=1, grid=(r // tr,),
            in_specs=[mine(k) for k in range(4)] + [theirs(k) for k in range(4)],
            out_specs=[pl.BlockSpec((3, tr, cols), lambda i, pr: (0, i, 0)),
                       pl.BlockSpec((tr, cols), lambda i, pr: (i, 0))]),
        out_shape=[SDS((3, r, cols), BF16), SDS((r, cols), F32)], compiler_params=_params(48),
    )(place, *[g] * 4, *[a_in] * 4)


def _add_chips(own, b_in):
    r, cols = own.shape
    tr = _row_tile(r)

    def body(p_ref, b0_ref, b1_ref, b2_ref, o_ref):
        o_ref[...] = ((p_ref[...] + b0_ref[0].astype(F32)) + b1_ref[0].astype(F32)) + b2_ref[0].astype(F32)

    slot = lambda k: pl.BlockSpec((1, tr, cols), lambda i: (k, i, 0))
    spec = pl.BlockSpec((tr, cols), lambda i: (i, 0))
    return pl.pallas_call(
        body, name="add_chips", grid=(r // tr,), in_specs=[spec, slot(0), slot(1), slot(2)], out_specs=spec,
        out_shape=SDS((r, cols), F32), compiler_params=_params(32),
    )(own, b_in, b_in, b_in)


VEC_NAMES = ("b_merge", "conv_b", "rg_bx", "rg_ba", "rg_lambda", "hg_lb_logits", "hg_norm_g", "final_norm_g")
REP_NAMES = ("rg_wx", "rg_wa", "norm_g") + VEC_NAMES
SMALL_AT = 3 * BD
SMALL_ROWS = 48
MID_ROWS = 448


def _sum_blocks(parts):
    def body(p_ref, o_ref):
        acc = p_ref[0]
        for k in range(1, NB):
            acc = acc + p_ref[k]
        o_ref[...] = acc

    return pl.pallas_call(body, name="sum_blocks", out_shape=SDS(parts.shape[1:], F32))(parts)


def _pack_rows(arrays, width, row_multiple=8):
    flat = jnp.concatenate([a.reshape(-1) for a in arrays])
    rows = -(-flat.shape[0] // width)
    rows = -(-rows // row_multiple) * row_multiple
    return jnp.pad(flat, (0, rows * width - flat.shape[0])).reshape(rows, width)


def _unpack(flat, like):
    out, off = [], 0
    for a in like:
        out.append(flat[off:off + a.size].reshape(a.shape))
        off += a.size
    return out


def kernel(x, w_in, b_merge, conv_w, conv_b, rg_wx, rg_bx, rg_wa, rg_ba, rg_lambda, hg_lb_logits, hg_norm_g, proj_a, proj_b, w_out, norm_g, final_norm_g, loss_target, m_w_in, m_b_merge, m_conv_w, m_conv_b, m_rg_wx, m_rg_bx, m_rg_wa, m_rg_ba, m_rg_lambda, m_hg_lb_logits, m_hg_norm_g, m_proj_a, m_proj_b, m_w_out, m_norm_g, m_final_norm_g, v_w_in, v_b_merge, v_conv_w, v_conv_b, v_rg_wx, v_rg_bx, v_rg_wa, v_rg_ba, v_rg_lambda, v_hg_lb_logits, v_hg_norm_g, v_proj_a, v_proj_b, v_w_out, v_norm_g, v_final_norm_g):
    weights = dict(w_in=w_in, b_merge=b_merge, conv_w=conv_w, conv_b=conv_b, rg_wx=rg_wx, rg_bx=rg_bx, rg_wa=rg_wa,
                   rg_ba=rg_ba, rg_lambda=rg_lambda, hg_lb_logits=hg_lb_logits, hg_norm_g=hg_norm_g, proj_a=proj_a,
                   proj_b=proj_b, w_out=w_out, norm_g=norm_g, final_norm_g=final_norm_g)
    mom1 = dict(w_in=m_w_in, b_merge=m_b_merge, conv_w=m_conv_w, conv_b=m_conv_b, rg_wx=m_rg_wx, rg_bx=m_rg_bx,
                rg_wa=m_rg_wa, rg_ba=m_rg_ba, rg_lambda=m_rg_lambda, hg_lb_logits=m_hg_lb_logits,
                hg_norm_g=m_hg_norm_g, proj_a=m_proj_a, proj_b=m_proj_b, w_out=m_w_out, norm_g=m_norm_g,
                final_norm_g=m_final_norm_g)
    mom2 = dict(w_in=v_w_in, b_merge=v_b_merge, conv_w=v_conv_w, conv_b=v_conv_b, rg_wx=v_rg_wx, rg_bx=v_rg_bx,
                rg_wa=v_rg_wa, rg_ba=v_rg_ba, rg_lambda=v_rg_lambda, hg_lb_logits=v_hg_lb_logits,
                hg_norm_g=v_hg_norm_g, proj_a=v_proj_a, proj_b=v_proj_b, w_out=v_w_out, norm_g=v_norm_g,
                final_norm_g=v_final_norm_g)
    order = list(weights)
    nb, s_len, _ = x.shape
    n = nb * s_len
    px, py, pc = _place()
    place = jnp.stack([px, py, pc]).astype(jnp.int32)

    x2 = x.reshape(n, D)
    cw_blk = jnp.pad(conv_w[0], ((0, 4), (0, 0)))
    order_ids = jnp.stack([_block_id(p) for p in _arrival_order(px, py, pc)]).astype(jnp.int32)
    z, h_all, w_all, cw_all = _gather_inproj(order_ids, x2, norm_g, [w_in[0], cw_blk], [BF16, F32])
    cw8 = cw_all.transpose(1, 0, 2).reshape(8, D)
    wx_b, wa_b = rg_wx[0].astype(BF16), rg_wa[0].astype(BF16)
    cb, bx, ba = conv_b, rg_bx.reshape(1, D), rg_ba.reshape(1, D)
    fin_g = final_norm_g.reshape(1, D)

    proj_blocks = [w[0].astype(BF16) for w in (proj_a, proj_b, w_out)]
    my_id = _block_id((px, py, pc))
    lands = [lax.dynamic_update_slice(jnp.zeros((NB, BD, D), BF16), b[None], (my_id, 0, 0)) for b in proj_blocks]
    proj_gather = _split_start("gather_proj_start", _peer_copies, 7, proj_blocks, lands, after=h_all)

    hlru, ya = _lru_fwd(z, cw8, cb, wx_b, wa_b, bx, ba, rg_lambda, nb, s_len, proj_gather[-1])
    o_all, yb, st_all = _hgrn_fwd(z, hg_lb_logits, hg_norm_g, nb, s_len)
    pa_full, pb_full, wo_full = (a.reshape(D, D) for a in _split_wait("gather_proj_wait", _peer_copies, proj_gather, yb))

    (dx2, dya, dyb, dzm, loss_acc, g_fin, g_bm, g_mid) = _mid(
        ya, yb, z, b_merge, x2, loss_target.reshape(n, D), fin_g, pa_full, pb_full, wo_full)
    dzb, g_lg, g_hg = _hgrn_bwd(z, o_all, st_all, dyb, hg_lb_logits, hg_norm_g, nb, s_len)
    dza, g_cw8, g_cb, g_wx, g_wa, g_bx, g_ba, g_lam = _lru_bwd(
        z, hlru, dya, cw8, cb, wx_b, wa_b, bx, ba, rg_lambda, nb, s_len)

    part = dict(b_merge=g_bm, conv_b=g_cb, rg_bx=g_bx, rg_ba=g_ba, rg_lambda=g_lam, hg_lb_logits=g_lg,
                hg_norm_g=g_hg, final_norm_g=g_fin)
    vec = _pack_rows([part[k] for k in VEC_NAMES], BD)
    vec = jnp.pad(vec, ((0, 16 * NB - vec.shape[0]), (0, 0))).reshape(NB, 2, D)
    rows8 = lambda a: jnp.pad(a, ((0, 0), (0, 8 - a.shape[1]), (0, 0)))
    small = jnp.concatenate([g_wx.reshape(NB, 16, D), g_wa.reshape(NB, 16, D),
                             rows8(g_cw8.reshape(8, NB, BD).transpose(1, 0, 2).reshape(NB, 1, D)), rows8(vec),
                             jnp.zeros((NB, MID_ROWS - SMALL_AT - SMALL_ROWS, D), F32)], axis=1)
    g_m = lax.dynamic_update_slice(g_mid, small, (0, SMALL_AT, 0))
    g_w, w_from_sibling, m_from_sibling = _inproj_bwd_w(dza, dzb, dzm, h_all, g_m)
    w_out_bf, w_own = _add_sibling(place, g_w, w_from_sibling)
    m_out_bf, m_own = _add_sibling(place, g_m, m_from_sibling)
    outgoing = [w_out_bf, m_out_bf]
    chip_sums = _split_start("rs_chips_start", _chip_copies, 3, outgoing, [lax.empty(a.shape, a.dtype) for a in outgoing])
    grad_x, g_ng = _inproj_bwd_x(dza, dzb, dzm, w_all, x2, dx2, norm_g, chip_sums[-1])
    from_chips = _split_wait("rs_chips_wait", _chip_copies, chip_sums, grad_x)
    r_w = _add_chips(w_own, from_chips[0])
    r_m = _add_chips(m_own, from_chips[1])
    row = lax.broadcasted_iota(jnp.int32, (8, D), 0)
    mine = jnp.where(row == 0, g_ng, jnp.where(row == 1, loss_acc[0:1, 0:1], 0.0))
    tail = jnp.concatenate([r_m[SMALL_AT:SMALL_AT + SMALL_ROWS], mine], axis=0)
    (tail_all,) = _allgather([tail], [F32], "gather_small_grads")
    summed = _sum_blocks(tail_all[:, SMALL_ROWS:SMALL_ROWS + 8])

    grads = dict(w_in=r_w.reshape(1, D, D),
                 proj_a=r_m[0:BD].reshape(1, BD, D), proj_b=r_m[BD:2 * BD].reshape(1, BD, D),
                 w_out=r_m[2 * BD:3 * BD].reshape(1, BD, D),
                 conv_w=r_m[SMALL_AT + 32].reshape(8, BD)[0:4].reshape(1, 4, BD),
                 rg_wx=tail_all[:, 0:16].reshape(1, NB, BD, BD), rg_wa=tail_all[:, 16:32].reshape(1, NB, BD, BD),
                 norm_g=summed[0:1])
    vec_all = tail_all[:, 40:42].reshape(-1)
    for k, gk in zip(VEC_NAMES, _unpack(vec_all, [weights[k] for k in VEC_NAMES])):
        grads[k] = gk

    delta, new_m, new_v = {}, {}, {}
    for k in ("w_in", "proj_a", "proj_b", "w_out"):
        shp = weights[k].shape
        two = lambda a: a.reshape(shp[1], shp[2])
        d_k, m_k, v_k = _adamw(two(weights[k]), two(grads[k]), two(mom1[k]), two(mom2[k]))
        delta[k], new_m[k], new_v[k] = d_k.reshape(shp), m_k.reshape(shp), v_k.reshape(shp)
    rep = list(REP_NAMES) + ["conv_w"]
    flat2 = lambda a: a.reshape(-1, a.shape[-1])
    outs = _adamw_small(*[[flat2(t[k]) for k in rep] for t in (weights, grads, mom1, mom2)])
    for tgt, arrays in zip((delta, new_m, new_v), outs):
        for k, a in zip(rep, arrays):
            tgt[k] = a.reshape(weights[k].shape)

    return (summed[1, 0],grad_x.reshape(x.shape), *[grads[k] for k in order], *[delta[k] for k in order],
            *[new_m[k] for k in order], *[new_v[k] for k in order])
```

```python
import functools

import jax
import jax.numpy as jnp
from jax import lax
from jax.experimental import pallas as pl
from jax.experimental.pallas import tpu as pltpu

F32 = jnp.float32
BF16 = jnp.bfloat16
SDS = jax.ShapeDtypeStruct
MESH = pl.DeviceIdType.MESH
ANY = pl.BlockSpec(memory_space=pl.ANY)

D = 1024
NB = 8
BD = D // NB
CHUNK = 64
EPS = 1e-6
LRU_C = 8.0
HG_SCALE = BD ** -0.5
ADAM_LR, ADAM_B1, ADAM_B2, ADAM_EPS, ADAM_WD, ADAM_STEP = 0.001, 0.9, 0.999, 1e-08, 0.01, 10

NT_DIMS = (((1,), (1,)), ((), ()))
TN_DIMS = (((0,), (0,)), ((), ()))


def _params(vmem_mib):
    return pltpu.CompilerParams(vmem_limit_bytes=vmem_mib << 20)


def _row_tile(rows, most=256):
    assert rows % 8 == 0
    return max(t for t in range(8, min(rows, most) + 1, 8) if rows % t == 0)


def _sigmoid(v):
    return 0.5 * (jnp.tanh(0.5 * v) + 1.0)


def _groups(v):
    return v.reshape(v.shape[0] // 8, 8, v.shape[1])


def _softplus_neg(lam):
    t = -lam
    e = jnp.exp(-jnp.abs(t))
    w = 1.0 + e
    d = w - 1.0
    l1p = jnp.where(d == 0.0, e, jnp.log(w) * (e / jnp.where(d == 0.0, 1.0, d)))
    return jnp.maximum(t, 0.0) + l1p


def _place():
    return lax.axis_index("x"), lax.axis_index("y"), lax.axis_index("c")


def _other_chips(x, y):
    return [(1 - x, y), (x, 1 - y), (1 - x, 1 - y)]


def _block_id(p):
    return 4 * p[0] + 2 * p[1] + p[2]


def _arrival_order(x, y, c):
    near, far, diag = _other_chips(x, y)
    return [(x, y, c), (x, y, 1 - c), (*near, c), (*far, c), (*near, 1 - c), (*far, 1 - c), (*diag, c), (*diag, 1 - c)]


def _gather_inproj(order_ids, x2, norm_g, blocks, dtypes):
    na = len(blocks)
    n = x2.shape[0]
    tm = min(n, 1024)
    ni = n // tm

    def body(order_ref, x_ref, g_ref, *refs):
        ins, (z_ref, h_ref), outs = refs[:na], refs[na:na + 2], refs[na + 2:2 * na + 2]
        stages = refs[2 * na + 2:3 * na + 2]
        h_full, wbuf, send_sems, recv_sems, local_sems, wsems, hsem = refs[3 * na + 2:]
        j, i = pl.program_id(0), pl.program_id(1)
        x, y, c = _place()
        me, sibling = (x, y, c), (x, y, 1 - c)
        chips = _other_chips(x, y)
        small = range(1, na)

        def copy(a, k, block, to, src=None):
            return pltpu.make_async_remote_copy(
                src_ref=outs[a].at[_block_id(block)] if src is None else src, dst_ref=outs[a].at[_block_id(block)],
                send_sem=send_sems.at[7 * a + k], recv_sem=recv_sems.at[7 * a + k],
                device_id=to, device_id_type=MESH)

        def local(a):
            return pltpu.make_async_copy(stages[a], outs[a].at[_block_id(me)], local_sems.at[a])

        def landed(a, slot):
            copy(a, 1 + slot, (*chips[slot], c), me).wait_recv()
            copy(a, 4 + slot, (*chips[slot], c), sibling).start()

        def passed_on(a, slot):
            copy(a, 4 + slot, (*chips[slot], 1 - c), me).wait_recv()

        @pl.when((j == 0) & (i == 0))
        def _():
            for a in range(na):
                stages[a][...] = ins[a][...].astype(dtypes[a])
                local(a).start()
            for a in range(na):
                copy(a, 0, me, sibling, src=stages[a]).start()
                for slot, chip in enumerate(chips):
                    copy(a, 1 + slot, me, (*chip, c), src=stages[a]).start()

        @pl.when(j == 0)
        def _():
            xv = x_ref[...]
            r = lax.rsqrt(jnp.mean(xv * xv, axis=-1, keepdims=True) + EPS)
            hb = ((xv * r) * g_ref[...]).astype(BF16)
            h_full[pl.ds(pl.multiple_of(i * tm, tm), tm), :] = hb

        save_h = pltpu.make_async_copy(h_full, h_ref, hsem)
        pl.when((j == 0) & (i == ni - 1))(save_h.start)

        steps = [
            lambda: local(0).wait(),
            lambda: copy(0, 0, sibling, me).wait_recv(),
            lambda: landed(0, 0),
            lambda: landed(0, 1),
            lambda: passed_on(0, 0),
            lambda: passed_on(0, 1),
            lambda: landed(0, 2),
            lambda: passed_on(0, 2),
        ]
        def w_load(k):
            return pltpu.make_async_copy(outs[0].at[order_ref[k]], wbuf.at[k % 2], wsems.at[k % 2])

        for k, step in enumerate(steps):
            @pl.when((j == 0) & (i == 0) if k == 0 else (j == k - 1) & (i == ni - 1))
            def _(k=k, step=step):
                step()
                w_load(k).start()

        pl.when(i == 0)(lambda: w_load(j).wait())
        z_ref[0] = jnp.dot(h_full[pl.ds(pl.multiple_of(i * tm, tm), tm), :], wbuf[j % 2], preferred_element_type=F32)

        @pl.when((j == NB - 1) & (i == ni - 1))
        def _():
            save_h.wait()
            for slot in range(3):
                for a in small:
                    landed(a, slot)
            for a in small:
                local(a).wait()
                copy(a, 0, sibling, me).wait_recv()
                for slot in range(3):
                    passed_on(a, slot)
            for a in range(na):
                copy(a, 0, me, sibling, src=stages[a]).wait_send()
                for slot, chip in enumerate(chips):
                    copy(a, 1 + slot, me, (*chip, c), src=stages[a]).wait_send()
                    copy(a, 4 + slot, (*chip, c), sibling).wait_send()

    rows_once = lambda j, i, order: (jnp.where(j == 0, i, ni - 1), 0)
    vmem = pl.BlockSpec(memory_space=pltpu.VMEM)
    return pl.pallas_call(
        body, name="gather_inproj",
        grid_spec=pltpu.PrefetchScalarGridSpec(
            num_scalar_prefetch=1, grid=(NB, ni),
            in_specs=[pl.BlockSpec((tm, D), rows_once), pl.BlockSpec((1, D), lambda j, i, order: (0, 0))] + [vmem] * na,
            out_specs=[pl.BlockSpec((1, tm, D), lambda j, i, order: (order[j], i, 0)), ANY] + [ANY] * na,
            scratch_shapes=[pltpu.VMEM(b.shape, dt) for b, dt in zip(blocks, dtypes)]
            + [pltpu.VMEM((n, D), BF16), pltpu.VMEM((2, D, D), BF16),
               pltpu.SemaphoreType.DMA((7 * na,)), pltpu.SemaphoreType.DMA((7 * na,)),
               pltpu.SemaphoreType.DMA((na,)), pltpu.SemaphoreType.DMA((2,)), pltpu.SemaphoreType.DMA(())]),
        out_shape=[SDS((NB, n, D), F32), SDS((n, D), BF16)] + [SDS((NB,) + b.shape, dt) for b, dt in zip(blocks, dtypes)],
        compiler_params=_params(56),
    )(order_ids, x2, norm_g, *blocks)


LRU_T = 256


def _shifted(groups, shifts):
    row = lax.broadcasted_iota(jnp.int32, (groups.shape[0] - 1,) + groups.shape[1:], 1)
    out = []
    for s in shifts:
        y = pltpu.roll(groups, s % 8, 1)
        moved = jnp.where(row >= s, y[1:], y[:-1]) if s > 0 else jnp.where(row < 8 + s, y[:-1], y[1:])
        out.append(moved.reshape(-1, groups.shape[2]))
    return out


def _conv(taps, cw, cb):
    acc = taps[0] * cw[0:1, :] + taps[1] * cw[1:2, :]
    acc = acc + taps[2] * cw[2:3, :]
    acc = acc + taps[3] * cw[3:4, :]
    return cb + acc


def _lru_gates(xa, wx_ref, wa_ref, bx, ba, lam):
    xab = xa.astype(BF16)
    pis, prs = [], []
    for h in range(NB):
        xs = xab[:, h * BD:(h + 1) * BD]
        pis.append(jnp.dot(xs, wx_ref[h], preferred_element_type=F32))
        prs.append(jnp.dot(xs, wa_ref[h], preferred_element_type=F32))
    gi = _sigmoid(jnp.concatenate(pis, axis=1) + bx)
    gr = _sigmoid(jnp.concatenate(prs, axis=1) + ba)
    sp = _softplus_neg(lam)
    log_a = (-LRU_C * gr) * sp
    a = jnp.exp(log_a)
    mult = jnp.sqrt(-jnp.tanh(log_a) * (a * a + 1.0))
    return xab, gi, gr, sp, a, mult


def _lru_fwd(z, cw8, cb, wx, wa, bx, ba, lam, nb, s_len):
    n = nb * s_len
    t = LRU_T
    ns = s_len // t

    def body(xp_ref, ga_ref, cw_ref, cb_ref, wx_ref, wa_ref, bx_ref, ba_ref, lam_ref,
             h_ref, ya_ref, ext, a_s, u_s, carry):
        @pl.when(pl.program_id(1) == 0)
        def _():
            ext[0:8, :] = jnp.zeros((8, D), F32)
            carry[...] = jnp.zeros((8, D), F32)

        xp = xp_ref[0]
        ext[8:8 + t, :] = xp
        xa = _conv(_shifted(_groups(ext[...]), (3, 2, 1)) + [xp], cw_ref[...], cb_ref[...])
        ext[0:8, :] = xp[t - 8:t, :]
        _, gi, _, _, a, mult = _lru_gates(xa, wx_ref, wa_ref, bx_ref[...], ba_ref[...], lam_ref[...])
        u = (mult * gi) * xa
        a, u = _groups(a), _groups(u)
        row = lax.broadcasted_iota(jnp.int32, a.shape, 1)
        for sh in (1, 2, 4):
            a_sh = pltpu.roll(a, sh, 1)
            u_sh = pltpu.roll(u, sh, 1)
            m = row >= sh
            u = jnp.where(m, a * u_sh + u, u)
            a = jnp.where(m, a * a_sh, a)
        a_s[...] = a.reshape(t, D)
        u_s[...] = u.reshape(t, D)

        def step(g, c):
            r = pl.multiple_of(g * 8, 8)
            hg = u_s[pl.ds(r, 8), :] + a_s[pl.ds(r, 8), :] * c
            h_ref[pl.ds(r, 8), :] = hg
            return hg[7:8, :]

        c_out = lax.fori_loop(0, t // 8, step, carry[0:1, :], unroll=4)
        carry[0:1, :] = c_out
        ga = ga_ref[0]
        ya_ref[...] = (h_ref[...] * (ga * _sigmoid(ga))).astype(BF16)

    row_map = lambda b, s: (b * ns + s, 0)
    rep2 = lambda b, s: (0, 0)
    rep3 = lambda b, s: (0, 0, 0)
    return pl.pallas_call(
        body, name="lru_fwd", grid=(nb, ns),
        in_specs=[pl.BlockSpec((1, t, D), lambda b, s: (0, b * ns + s, 0)),
                  pl.BlockSpec((1, t, D), lambda b, s: (1, b * ns + s, 0)),
                  pl.BlockSpec((8, D), rep2), pl.BlockSpec((1, D), rep2),
                  pl.BlockSpec((NB, BD, BD), rep3), pl.BlockSpec((NB, BD, BD), rep3),
                  pl.BlockSpec((1, D), rep2), pl.BlockSpec((1, D), rep2), pl.BlockSpec((1, D), rep2)],
        out_specs=[pl.BlockSpec((t, D), row_map), pl.BlockSpec((t, D), row_map)],
        out_shape=[SDS((n, D), F32), SDS((n, D), BF16)],
        scratch_shapes=[pltpu.VMEM((t + 8, D), F32), pltpu.VMEM((t, D), F32), pltpu.VMEM((t, D), F32),
                        pltpu.VMEM((8, D), F32)],
        compiler_params=_params(48),
    )(z, z, cw8, cb, wx, wa, bx, ba, lam)


def _lru_bwd(z, h_all, dya, cw8, cb, wx, wa, bx, ba, lam, nb, s_len):
    n = nb * s_len
    t = LRU_T
    ns = s_len // t
    t8 = t // 8

    def body(xp_ref, xph_ref, ga_ref, h_ref, hh_ref, dya_ref, cw_ref, cb_ref, wx_ref, wa_ref, bx_ref, ba_ref,
             lam_ref, dz_ref, gcw_ref, gcb_ref, gwx_ref, gwa_ref, gbx_ref, gba_ref, glam_ref,
             ext, hext, dext, a_s, u_s, dh_s, carry):
        b, s = pl.program_id(0), pl.program_id(1)
        first_tile = s == ns - 1

        @pl.when((b == 0) & (s == 0))
        def _():
            for ref in (gcw_ref, gcb_ref, gwx_ref, gwa_ref, gbx_ref, gba_ref, glam_ref):
                ref[...] = jnp.zeros(ref.shape, F32)

        @pl.when(s == 0)
        def _():
            dext[t:t + 8, :] = jnp.zeros((8, D), F32)
            carry[...] = jnp.zeros((8, D), F32)

        keep = jnp.where(first_tile, 0.0, 1.0)
        xp = xp_ref[0]
        ext[0:8, :] = xph_ref[0] * keep
        ext[8:8 + t, :] = xp
        hext[0:8, :] = hh_ref[...] * keep
        hext[8:8 + t, :] = h_ref[...]
        cw = cw_ref[...]
        lam = lam_ref[...]
        taps = _shifted(_groups(ext[...]), (3, 2, 1)) + [xp]
        xa = _conv(taps, cw, cb_ref[...])
        xab, gi, gr, sp, a, mult = _lru_gates(xa, wx_ref, wa_ref, bx_ref[...], ba_ref[...], lam)
        (h_prev,) = _shifted(_groups(hext[...]), (1,))
        ga = ga_ref[0]
        sg = _sigmoid(ga)
        dya_v = dya_ref[...]
        d_ga = dya_v * h_ref[...] * (sg * (1.0 + ga * (1.0 - sg)))
        g_in = dya_v * (ga * sg)

        (an,) = _shifted(jnp.concatenate([_groups(a), jnp.ones((1, 8, D), F32)], axis=0), (-1,))
        an, u = _groups(an), _groups(g_in)
        row = lax.broadcasted_iota(jnp.int32, an.shape, 1)
        for sh in (1, 2, 4):
            a_sh = pltpu.roll(an, 8 - sh, 1)
            u_sh = pltpu.roll(u, 8 - sh, 1)
            m = row < 8 - sh
            u = jnp.where(m, u + an * u_sh, u)
            an = jnp.where(m, an * a_sh, an)
        a_s[...] = an.reshape(t, D)
        u_s[...] = u.reshape(t, D)

        def step(i, c):
            r = pl.multiple_of((t8 - 1 - i) * 8, 8)
            dg = u_s[pl.ds(r, 8), :] + a_s[pl.ds(r, 8), :] * c
            dh_s[pl.ds(r, 8), :] = dg
            return dg[0:1, :]

        lax.fori_loop(0, t8, step, carry[0:1, :], unroll=4)
        dh = dh_s[...]
        carry[0:1, :] = a[0:1, :] * dh[0:1, :]

        d_a = dh * h_prev
        dux = dh * xa
        d_mult = dux * gi
        d_gi = dux * mult
        d_xa = dh * (mult * gi)
        d_loga = d_a * a - d_mult * ((a * a) / mult)
        d_gr = d_loga * (-LRU_C * sp)
        d_sp = jnp.sum(d_loga * (-LRU_C * gr), axis=0, keepdims=True)
        glam_ref[...] += d_sp * (-_sigmoid(-lam))
        d_pi = d_gi * gi * (1.0 - gi)
        d_pr = d_gr * gr * (1.0 - gr)
        gbx_ref[...] += jnp.sum(d_pi, axis=0, keepdims=True)
        gba_ref[...] += jnp.sum(d_pr, axis=0, keepdims=True)
        dpib = d_pi.astype(BF16)
        dprb = d_pr.astype(BF16)
        back = []
        for h in range(NB):
            cs = slice(h * BD, (h + 1) * BD)
            gwx_ref[h] += lax.dot_general(xab[:, cs], dpib[:, cs], TN_DIMS, preferred_element_type=F32)
            gwa_ref[h] += lax.dot_general(xab[:, cs], dprb[:, cs], TN_DIMS, preferred_element_type=F32)
            back.append(lax.dot_general(dpib[:, cs], wx_ref[h], NT_DIMS, preferred_element_type=F32)
                        + lax.dot_general(dprb[:, cs], wa_ref[h], NT_DIMS, preferred_element_type=F32))
        d_xa = d_xa + jnp.concatenate(back, axis=1)

        dext[0:t, :] = d_xa
        later = _shifted(_groups(dext[...]), (-3, -2, -1))
        d_xp = later[0] * cw[0:1, :] + later[1] * cw[1:2, :]
        d_xp = d_xp + later[2] * cw[2:3, :]
        d_xp = d_xp + d_xa * cw[3:4, :]
        dext[t:t + 8, :] = d_xa[0:8, :]
        gcb_ref[...] += jnp.sum(d_xa, axis=0, keepdims=True)
        for k in range(4):
            gcw_ref[k:k + 1, :] += jnp.sum(d_xa * taps[k], axis=0, keepdims=True)
        dz_ref[0] = d_xp.astype(BF16)
        dz_ref[1] = d_ga.astype(BF16)

    rb = lambda b, s: b * ns + (ns - 1 - s)
    halo = lambda b, s: jnp.maximum(rb(b, s) * t8 - 1, 0)
    rep2 = lambda b, s: (0, 0)
    rep3 = lambda b, s: (0, 0, 0)
    return pl.pallas_call(
        body, name="lru_bwd", grid=(nb, ns),
        in_specs=[pl.BlockSpec((1, t, D), lambda b, s: (0, rb(b, s), 0)),
                  pl.BlockSpec((1, 8, D), lambda b, s: (0, halo(b, s), 0)),
                  pl.BlockSpec((1, t, D), lambda b, s: (1, rb(b, s), 0)),
                  pl.BlockSpec((t, D), lambda b, s: (rb(b, s), 0)),
                  pl.BlockSpec((8, D), lambda b, s: (halo(b, s), 0)),
                  pl.BlockSpec((t, D), lambda b, s: (rb(b, s), 0)),
                  pl.BlockSpec((8, D), rep2), pl.BlockSpec((1, D), rep2),
                  pl.BlockSpec((NB, BD, BD), rep3), pl.BlockSpec((NB, BD, BD), rep3),
                  pl.BlockSpec((1, D), rep2), pl.BlockSpec((1, D), rep2), pl.BlockSpec((1, D), rep2)],
        out_specs=[pl.BlockSpec((2, t, D), lambda b, s: (0, rb(b, s), 0)),
                   pl.BlockSpec((8, D), rep2), pl.BlockSpec((1, D), rep2),
                   pl.BlockSpec((NB, BD, BD), rep3), pl.BlockSpec((NB, BD, BD), rep3),
                   pl.BlockSpec((1, D), rep2), pl.BlockSpec((1, D), rep2), pl.BlockSpec((1, D), rep2)],
        out_shape=[SDS((2, n, D), BF16), SDS((8, D), F32), SDS((1, D), F32),
                   SDS((NB, BD, BD), F32), SDS((NB, BD, BD), F32),
                   SDS((1, D), F32), SDS((1, D), F32), SDS((1, D), F32)],
        scratch_shapes=[pltpu.VMEM((t + 8, D), F32), pltpu.VMEM((t + 8, D), F32), pltpu.VMEM((t + 8, D), F32),
                        pltpu.VMEM((t, D), F32), pltpu.VMEM((t, D), F32), pltpu.VMEM((t, D), F32),
                        pltpu.VMEM((8, D), F32)],
        compiler_params=_params(56),
    )(z, z, z, h_all, h_all, dya, cw8, cb, wx, wa, bx, ba, lam)


HG_T = 512
HG_NC = HG_T // CHUNK
BNT_DIMS = (((2,), (2,)), ((0,), (0,)))
BNN_DIMS = (((2,), (1,)), ((0,), (0,)))
BTN_DIMS = (((1,), (1,)), ((0,), (0,)))


def _lower_bound(lg):
    m = jnp.max(lg, axis=0, keepdims=True)
    e = jnp.exp(lg - m)
    return e[0:1, :] / jnp.sum(e, axis=0, keepdims=True)


def _tri(upper):
    r = lax.broadcasted_iota(jnp.int32, (HG_NC, CHUNK, CHUNK), 1)
    c = lax.broadcasted_iota(jnp.int32, (HG_NC, CHUNK, CHUNK), 2)
    return (c >= r) if upper else (r >= c)


def _bdot(a, b, dims):
    return lax.dot_general(a, b, dims, preferred_element_type=F32)


def _tri_sums(upper, a):
    tri = _tri(upper).astype(BF16)
    a1 = a.astype(BF16)
    r1 = a - a1.astype(F32)
    a2 = r1.astype(BF16)
    a3 = (r1 - a2.astype(F32)).astype(BF16)
    return _bdot(tri, a1, BNN_DIMS) + (_bdot(tri, a2, BNN_DIMS) + _bdot(tri, a3, BNN_DIMS))


def _chunks(a):
    return a.reshape(HG_NC, CHUNK, BD)


def _hg_tile(q, fp, lb):
    q, fp = _chunks(q), _chunks(fp)
    sig = _sigmoid(fp)
    f = lb + (1.0 - lb) * sig
    log_f = jnp.log(f)
    k = 1.0 - f
    b = _tri_sums(False, log_f)
    b_mid = b[:, CHUNK // 2:CHUNK // 2 + 1, :]
    b_last = b[:, CHUNK - 1:CHUNK, :]
    sq = _sigmoid(q)
    qh = q * sq
    e_qi = jnp.exp(b - b_mid)
    e_ki = jnp.exp(b_mid - b)
    e_qs = jnp.exp(b)
    e_ks = jnp.exp(b_last - b)
    dc = jnp.exp(b_last)
    q_in = (qh * e_qi) * HG_SCALE
    k_in = k * e_ki
    q_st = (qh * e_qs) * HG_SCALE
    k_st = k * e_ks
    att = _bdot(q_in.astype(BF16), k_in.astype(BF16), BNT_DIMS)
    att = jnp.where(_tri(False), att, 0.0)
    return dict(q=q, sig=sig, f=f, k=k, sq=sq, e_qi=e_qi, e_ki=e_ki, e_qs=e_qs, e_ks=e_ks, dc=dc,
                q_in=q_in, k_in=k_in, q_st=q_st, k_st=k_st, att=att)


def _hgrn_fwd(z, lb_logits, hg_g, nb, s_len):
    n = nb * s_len
    t = HG_T
    ns = s_len // t
    nchunk = s_len // CHUNK

    def body(q_ref, f_ref, v_ref, gb_ref, lg_ref, g_ref, o_ref, yb_ref, st_ref, st):
        @pl.when(pl.program_id(1) == 0)
        def _():
            st[...] = jnp.zeros((NB, BD, BD), F32)

        def head(h, carry):
            cols = pl.ds(pl.multiple_of(h * BD, BD), BD)
            lb = _lower_bound(lg_ref[:, cols])
            ck = _hg_tile(q_ref[0, :, cols], f_ref[0, :, cols], lb)
            vb = _chunks(v_ref[0, :, cols]).astype(BF16)
            kv = _bdot(vb, ck["k_st"].astype(BF16), BTN_DIMS)
            states = [st[h]]
            for c in range(HG_NC):
                states.append(states[c] * ck["dc"][c] + kv[c])
            st[h] = states[HG_NC]
            s_in = jnp.stack(states[:HG_NC], axis=0)
            st_ref[h] = s_in
            o = (_bdot(ck["att"].astype(BF16), vb, BNN_DIMS)
                 + _bdot(ck["q_st"].astype(BF16), s_in.astype(BF16), BNT_DIMS))
            o_ref[:, cols] = o.reshape(t, BD)
            r = lax.rsqrt(jnp.mean(o * o, axis=-1, keepdims=True) + EPS)
            gb = _chunks(gb_ref[0, :, cols])
            yb_ref[:, cols] = (((o * r) * g_ref[...]) * (gb * _sigmoid(gb))).astype(BF16).reshape(t, BD)
            return carry

        lax.fori_loop(0, NB, head, 0, unroll=2)

    seg = lambda j: pl.BlockSpec((1, t, D), lambda b, s: (j, b * ns + s, 0))
    tile = pl.BlockSpec((t, D), lambda b, s: (b * ns + s, 0))
    return pl.pallas_call(
        body, name="hgrn_fwd", grid=(nb, ns),
        in_specs=[seg(2), seg(3), seg(4), seg(5),
                  pl.BlockSpec((2, D), lambda b, s: (0, 0)), pl.BlockSpec((1, BD), lambda b, s: (0, 0))],
        out_specs=[tile, tile, pl.BlockSpec((NB, HG_NC, BD, BD), lambda b, s: (b, s, 0, 0))],
        out_shape=[SDS((n, D), F32), SDS((n, D), BF16), SDS((nb * NB, nchunk, BD, BD), F32)],
        scratch_shapes=[pltpu.VMEM((NB, BD, BD), F32)],
        compiler_params=_params(56),
    )(z, z, z, z, lb_logits, hg_g)


def _hgrn_bwd(z, o_all, st_all, dyb, lb_logits, hg_g, nb, s_len):
    n = nb * s_len
    t = HG_T
    ns = s_len // t

    def body(q_ref, f_ref, v_ref, gb_ref, o_ref, st_ref, dyb_ref, lg_ref, g_ref,
             dz_ref, glg_ref, ghg_ref, dst, dlb):
        b, s = pl.program_id(0), pl.program_id(1)

        @pl.when((b == 0) & (s == 0))
        def _():
            ghg_ref[...] = jnp.zeros((1, BD), F32)
            dlb[...] = jnp.zeros((8, D), F32)

        @pl.when(s == 0)
        def _():
            dst[...] = jnp.zeros((NB, BD, BD), F32)

        g = g_ref[...]

        def head(h, carry):
            cols = pl.ds(pl.multiple_of(h * BD, BD), BD)
            lb = _lower_bound(lg_ref[:, cols])
            ck = _hg_tile(q_ref[0, :, cols], f_ref[0, :, cols], lb)
            q = ck["q"]
            vb = _chunks(v_ref[0, :, cols]).astype(BF16)
            gb = _chunks(gb_ref[0, :, cols])
            o = _chunks(o_ref[:, cols])
            dyb_v = _chunks(dyb_ref[:, cols])
            s_in = st_ref[h]

            sgb = _sigmoid(gb)
            r = lax.rsqrt(jnp.mean(o * o, axis=-1, keepdims=True) + EPS)
            ohat = o * r
            d_on = dyb_v * (gb * sgb)
            d_gb = dyb_v * (ohat * g) * (sgb * (1.0 + gb * (1.0 - sgb)))
            ghg_ref[...] += jnp.sum(jnp.sum(d_on * ohat, axis=1), axis=0, keepdims=True)
            tt = d_on * g
            d_o = r * (tt - ohat * jnp.mean(tt * ohat, axis=-1, keepdims=True))
            dob = d_o.astype(BF16)

            attb = ck["att"].astype(BF16)
            q_inb, k_inb = ck["q_in"].astype(BF16), ck["k_in"].astype(BF16)
            q_stb, k_stb = ck["q_st"].astype(BF16), ck["k_st"].astype(BF16)
            d_att = jnp.where(_tri(False), _bdot(dob, vb, BNT_DIMS), 0.0).astype(BF16)
            d_q_in = _bdot(d_att, k_inb, BNN_DIMS)
            d_k_in = _bdot(d_att, q_inb, BTN_DIMS)
            d_q_st = _bdot(dob, s_in.astype(BF16), BNN_DIMS)
            qdo = _bdot(dob, q_stb, BTN_DIMS)
            d_states = [None] * HG_NC + [dst[h]]
            for c in reversed(range(HG_NC)):
                d_states[c] = d_states[c + 1] * ck["dc"][c] + qdo[c]
            dst[h] = d_states[0]
            ds_out = jnp.stack(d_states[1:], axis=0)
            dsb = ds_out.astype(BF16)
            d_v = _bdot(attb, dob, BTN_DIMS) + _bdot(k_stb, dsb, BNT_DIMS)
            d_k_st = _bdot(vb, dsb, BNN_DIMS)
            d_dc = jnp.sum(ds_out * s_in, axis=1, keepdims=True)

            p_qi = d_q_in * ck["q_in"]
            p_ki = d_k_in * ck["k_in"]
            p_qs = d_q_st * ck["q_st"]
            p_ks = d_k_st * ck["k_st"]
            d_qh = (d_q_in * ck["e_qi"] + d_q_st * ck["e_qs"]) * HG_SCALE
            d_k = d_k_in * ck["e_ki"] + d_k_st * ck["e_ks"]
            d_b = (p_qi - p_ki) + (p_qs - p_ks)
            d_b_mid = jnp.sum(p_ki - p_qi, axis=1, keepdims=True)
            d_b_last = jnp.sum(p_ks, axis=1, keepdims=True) + d_dc * ck["dc"]
            rowi = lax.broadcasted_iota(jnp.int32, (HG_NC, CHUNK, BD), 1)
            d_b = d_b + jnp.where(rowi == CHUNK // 2, d_b_mid, 0.0) + jnp.where(rowi == CHUNK - 1, d_b_last, 0.0)
            d_logf = _tri_sums(True, d_b)
            d_f = d_logf / ck["f"] - d_k
            sig, sq = ck["sig"], ck["sq"]
            d_fp = d_f * (1.0 - lb) * (sig * (1.0 - sig))
            dlb[0:1, cols] += jnp.sum(jnp.sum(d_f * (1.0 - sig), axis=1), axis=0, keepdims=True)
            d_q = d_qh * (sq * (1.0 + q * (1.0 - sq)))
            dz_ref[0, :, cols] = d_q.astype(BF16).reshape(t, BD)
            dz_ref[1, :, cols] = d_fp.astype(BF16).reshape(t, BD)
            dz_ref[2, :, cols] = d_v.astype(BF16).reshape(t, BD)
            dz_ref[3, :, cols] = d_gb.astype(BF16).reshape(t, BD)
            return carry

        lax.fori_loop(0, NB, head, 0, unroll=2)

        @pl.when((b == nb - 1) & (s == ns - 1))
        def _():
            lb = _lower_bound(lg_ref[...])
            dl = dlb[0:1, :] * (lb * (1.0 - lb))
            glg_ref[0:1, :] = dl
            glg_ref[1:2, :] = -dl

    rb = lambda b, s: b * ns + (ns - 1 - s)
    seg = lambda j: pl.BlockSpec((1, t, D), lambda b, s: (j, rb(b, s), 0))
    tile = pl.BlockSpec((t, D), lambda b, s: (rb(b, s), 0))
    return pl.pallas_call(
        body, name="hgrn_bwd", grid=(nb, ns),
        in_specs=[seg(2), seg(3), seg(4), seg(5), tile,
                  pl.BlockSpec((NB, HG_NC, BD, BD), lambda b, s: (b, ns - 1 - s, 0, 0)),
                  tile, pl.BlockSpec((2, D), lambda b, s: (0, 0)), pl.BlockSpec((1, BD), lambda b, s: (0, 0))],
        out_specs=[pl.BlockSpec((4, t, D), lambda b, s: (0, rb(b, s), 0)),
                   pl.BlockSpec((2, D), lambda b, s: (0, 0)), pl.BlockSpec((1, BD), lambda b, s: (0, 0))],
        out_shape=[SDS((4, n, D), BF16), SDS((2, D), F32), SDS((1, BD), F32)],
        scratch_shapes=[pltpu.VMEM((NB, BD, BD), F32), pltpu.VMEM((8, D), F32)],
        compiler_params=_params(60),
    )(z, z, z, z, o_all, st_all, dyb, lb_logits, hg_g)


def _mid(ya, yb, z, b_merge, x2, tgt, fin_g, pa, pb, wo):
    n = x2.shape[0]
    tm = 256
    ni = n // tm

    def body(ya_ref, yb_ref, gma_ref, gmb_ref, bm_ref, x_ref, t_ref, fg_ref, pa_hbm, pb_hbm, wo_hbm,
             dx2_ref, dya_ref, dyb_ref, dgm_ref, loss_ref, gfg_ref, gbm_ref, gm_hbm,
             pa_v, pb_v, wo_v, gpa_v, gpb_v, gwo_v, sem):
        i = pl.program_id(0)
        by_owner = lambda g: g.reshape(NB, BD, D)
        loads = [pltpu.make_async_copy(src, dst, sem.at[k])
                 for k, (src, dst) in enumerate(((pa_hbm, pa_v), (pb_hbm, pb_v), (wo_hbm, wo_v)))]
        stores = [pltpu.make_async_copy(src, dst, sem.at[k])
                  for k, (src, dst) in enumerate((g, gm_hbm.at[:, pl.ds(slot * BD, BD), :])
                                                 for slot, g in enumerate((gpa_v, gpb_v, gwo_v)))]

        @pl.when(i == 0)
        def _():
            for cp in loads:
                cp.start()
            for ref in (gpa_v, gpb_v, gwo_v, loss_ref, gfg_ref, gbm_ref):
                ref[...] = jnp.zeros(ref.shape, F32)
            for cp in loads:
                cp.wait()

        ya_v = ya_ref[...]
        yb_v = yb_ref[...]
        out_a = jnp.dot(ya_v, pa_v[...], preferred_element_type=F32)
        out_b = jnp.dot(yb_v, pb_v[...], preferred_element_type=F32)
        bm = bm_ref[...]
        g_a = _sigmoid(gma_ref[0] + bm[:, 0:D])
        g_b = _sigmoid(gmb_ref[0] + bm[:, D:2 * D])
        mixed = g_a * out_a + g_b * out_b
        mixb = mixed.astype(BF16)
        xo = x_ref[...] + jnp.dot(mixb, wo_v[...], preferred_element_type=F32)
        r = lax.rsqrt(jnp.mean(xo * xo, axis=-1, keepdims=True) + EPS)
        xn = xo * r
        fg = fg_ref[...]
        e = xn * fg - t_ref[...]
        loss_ref[...] += 0.5 * jnp.sum(jnp.mean(e * e, axis=-1, keepdims=True))
        dy = e * (1.0 / D)
        gfg_ref[...] += jnp.sum(dy * xn, axis=0, keepdims=True)
        dxn = dy * fg
        dx2 = r * (dxn - xn * jnp.mean(dxn * xn, axis=-1, keepdims=True))
        dx2_ref[...] = dx2
        dx2b = dx2.astype(BF16)
        d_mixed = lax.dot_general(dx2b, wo_v[...], NT_DIMS, preferred_element_type=F32)
        gwo_v[...] += by_owner(lax.dot_general(mixb, dx2b, TN_DIMS, preferred_element_type=F32))
        d_oa = (d_mixed * g_a).astype(BF16)
        d_ob = (d_mixed * g_b).astype(BF16)
        dgm_a = (d_mixed * out_a) * (g_a * (1.0 - g_a))
        dgm_b = (d_mixed * out_b) * (g_b * (1.0 - g_b))
        gbm_ref[:, 0:D] += jnp.sum(dgm_a, axis=0, keepdims=True)
        gbm_ref[:, D:2 * D] += jnp.sum(dgm_b, axis=0, keepdims=True)
        dgm_ref[0] = dgm_a.astype(BF16)
        dgm_ref[1] = dgm_b.astype(BF16)
        dya_ref[...] = lax.dot_general(d_oa, pa_v[...], NT_DIMS, preferred_element_type=F32)
        dyb_ref[...] = lax.dot_general(d_ob, pb_v[...], NT_DIMS, preferred_element_type=F32)
        gpa_v[...] += by_owner(lax.dot_general(ya_v, d_oa, TN_DIMS, preferred_element_type=F32))
        gpb_v[...] += by_owner(lax.dot_general(yb_v, d_ob, TN_DIMS, preferred_element_type=F32))

        @pl.when(i == ni - 1)
        def _():
            for cp in stores:
                cp.start()
            for cp in stores:
                cp.wait()

    rows = pl.BlockSpec((tm, D), lambda i: (i, 0))
    rep = lambda shape: pl.BlockSpec(shape, lambda i: (0,) * len(shape))
    return pl.pallas_call(
        body, name="mid", grid=(ni,),
        in_specs=[rows, rows,
                  pl.BlockSpec((1, tm, D), lambda i: (6, i, 0)), pl.BlockSpec((1, tm, D), lambda i: (7, i, 0)),
                  rep((1, 2 * D)), rows, rows, rep((1, D)), ANY, ANY, ANY],
        out_specs=[rows, rows, rows, pl.BlockSpec((2, tm, D), lambda i: (0, i, 0)),
                   rep((8, BD)), rep((1, D)), rep((1, 2 * D)), ANY],
        out_shape=[SDS((n, D), F32), SDS((n, D), F32), SDS((n, D), F32), SDS((2, n, D), BF16),
                   SDS((8, BD), F32), SDS((1, D), F32), SDS((1, 2 * D), F32),
                   SDS((NB, MID_ROWS, D), F32)],
        scratch_shapes=[pltpu.VMEM((D, D), BF16)] * 3 + [pltpu.VMEM((NB, BD, D), F32)] * 3 + [pltpu.SemaphoreType.DMA((3,))],
        compiler_params=_params(60),
    )(ya, yb, z, z, b_merge, x2, tgt, fin_g, pa, pb, wo)


def _dz_specs(tm, ni, row_major):
    if row_major:
        ia = lambda i, j: (jnp.minimum(j, 1), i, 0)
        ib = lambda i, j: (jnp.clip(j - 2, 0, 3), i, 0)
        im = lambda i, j: (jnp.clip(j - 6, 0, 1), i, 0)
    else:
        last = ni - 1
        ia = lambda j, i: (jnp.minimum(j, 1), jnp.where(j < 2, i, last), 0)
        ib = lambda j, i: (jnp.clip(j - 2, 0, 3), jnp.where(j < 2, 0, jnp.where(j < 6, i, last)), 0)
        im = lambda j, i: (jnp.clip(j - 6, 0, 1), jnp.where(j < 6, 0, i), 0)
    return [pl.BlockSpec((1, tm, D), f) for f in (ia, ib, im)]


def _inproj_bwd_x(dza, dzb, dzm, w_all, x2, dx2, norm_g, after):
    n = x2.shape[0]
    tm = 512
    ni = n // tm

    def body(dza_ref, dzb_ref, dzm_ref, w_ref, x_ref, dx2_ref, g_ref, after_ref, gx_ref, gg_ref, acc):
        i, j = pl.program_id(0), pl.program_id(1)

        @pl.when((i == 0) & (j == 0))
        def _():
            gg_ref[...] = jnp.zeros((1, D), F32)

        @pl.when(j == 0)
        def _():
            acc[...] = jnp.zeros((tm, D), F32)

        def add(ref):
            acc[...] += lax.dot_general(ref[0], w_ref[0], NT_DIMS, preferred_element_type=F32)

        pl.when(j < 2)(lambda: add(dza_ref))
        pl.when((j >= 2) & (j < 6))(lambda: add(dzb_ref))
        pl.when(j >= 6)(lambda: add(dzm_ref))

        @pl.when(j == NB - 1)
        def _():
            x = x_ref[...]
            r = lax.rsqrt(jnp.mean(x * x, axis=-1, keepdims=True) + EPS)
            xn = x * r
            dh = acc[...]
            gg_ref[...] += jnp.sum(dh * xn, axis=0, keepdims=True)
            dxn = dh * g_ref[...]
            gx_ref[...] = dx2_ref[...] + r * (dxn - xn * jnp.mean(dxn * xn, axis=-1, keepdims=True))

    rows = pl.BlockSpec((tm, D), lambda i, j: (i, 0))
    return pl.pallas_call(
        body, name="inproj_bwd_x", grid=(ni, NB),
        in_specs=_dz_specs(tm, ni, True) + [pl.BlockSpec((1, D, D), lambda i, j: (j, 0, 0)), rows, rows,
                                             pl.BlockSpec((1, D), lambda i, j: (0, 0)), ANY],
        out_specs=[rows, pl.BlockSpec((1, D), lambda i, j: (0, 0))],
        out_shape=[SDS((n, D), F32), SDS((1, D), F32)],
        scratch_shapes=[pltpu.VMEM((tm, D), F32)],
        compiler_params=_params(48),
    )(dza, dzb, dzm, w_all, x2, dx2, norm_g, after)


def _inproj_bwd_w(dza, dzb, dzm, h_all, g_m):
    n = h_all.shape[0]
    tm = min(n, 1024)
    ni = n // tm

    def body(dza_ref, dzb_ref, dzm_ref, h_ref, gm_hbm, gw_ref, got_w, got_m, stage, send_sems, recv_sems):
        j, i = pl.program_id(0), pl.program_id(1)
        x, y, c = _place()
        sibling = (x, y, 1 - c)

        def send_w(q):
            return pltpu.make_async_remote_copy(
                src_ref=stage.at[q % 2], dst_ref=got_w.at[q], send_sem=send_sems.at[q], recv_sem=recv_sems.at[q],
                device_id=sibling, device_id_type=MESH)

        def send_m(q):
            return pltpu.make_async_remote_copy(
                src_ref=gm_hbm.at[2 * q + (1 - c)], dst_ref=got_m.at[q], send_sem=send_sems.at[4 + q],
                recv_sem=recv_sems.at[4 + q], device_id=sibling, device_id_type=MESH)

        @pl.when((j == 0) & (i == 0))
        def _():
            for q in range(4):
                send_m(q).start()

        @pl.when(i == 0)
        def _():
            gw_ref[...] = jnp.zeros((1, D, D), F32)

        def add(ref):
            gw_ref[0] += lax.dot_general(h_ref[...], ref[0], TN_DIMS, preferred_element_type=F32)

        pl.when(j < 2)(lambda: add(dza_ref))
        pl.when((j >= 2) & (j < 6))(lambda: add(dzb_ref))
        pl.when(j >= 6)(lambda: add(dzm_ref))

        for q in range(4):
            @pl.when((i == ni - 1) & (j == 2 * q + 1 - c))
            def _(q=q):
                if q >= 2:
                    send_w(q - 2).wait_send()
                stage[q % 2] = gw_ref[0].astype(BF16)
                send_w(q).start()

        @pl.when((j == NB - 1) & (i == ni - 1))
        def _():
            for q in (2, 3):
                send_w(q).wait_send()
            for q in range(4):
                send_w(q).wait_recv()
                send_m(q).wait_send()
                send_m(q).wait_recv()

    return pl.pallas_call(
        body, name="inproj_bwd_w", grid=(NB, ni),
        in_specs=_dz_specs(tm, ni, False) + [pl.BlockSpec((tm, D), lambda j, i: (i, 0)), ANY],
        out_specs=[pl.BlockSpec((1, D, D), lambda j, i: (j, 0, 0)), ANY, ANY],
        out_shape=[SDS((NB, D, D), F32), SDS((4, D, D), BF16), SDS((4,) + g_m.shape[1:], F32)],
        scratch_shapes=[pltpu.VMEM((2, D, D), BF16), pltpu.SemaphoreType.DMA((8,)), pltpu.SemaphoreType.DMA((8,))],
        compiler_params=_params(48),
    )(dza, dzb, dzm, h_all, g_m)


def _adamw(w, g, m, v):
    rows, cols = w.shape
    tr = _row_tile(rows)

    spec = pl.BlockSpec((tr, cols), lambda i: (i, 0))
    return pl.pallas_call(
        functools.partial(_adam_refs), name="adamw", grid=(rows // tr,), in_specs=[spec] * 4, out_specs=[spec] * 3,
        out_shape=[SDS((rows, cols), F32)] * 3, compiler_params=_params(32),
    )(w, g, m, v)


def _adam_refs(w_ref, g_ref, m_ref, v_ref, d_ref, nm_ref, nv_ref):
    gv = g_ref[...]
    nm = ADAM_B1 * m_ref[...] + (1.0 - ADAM_B1) * gv
    nv = ADAM_B2 * v_ref[...] + (1.0 - ADAM_B2) * (gv * gv)
    m_hat = nm / (1.0 - ADAM_B1 ** ADAM_STEP)
    v_hat = nv / (1.0 - ADAM_B2 ** ADAM_STEP)
    d_ref[...] = -ADAM_LR * (m_hat / (jnp.sqrt(v_hat) + ADAM_EPS) + ADAM_WD * w_ref[...])
    nm_ref[...] = nm
    nv_ref[...] = nv


def _adamw_small(ws, gs, ms, vs):
    k = len(ws)

    def body(*refs):
        for i in range(k):
            _adam_refs(*[refs[part * k + i] for part in range(7)])

    shapes = [SDS(w.shape, F32) for w in ws]
    out = pl.pallas_call(body, name="adamw_small", out_shape=shapes * 3, compiler_params=_params(32))(*ws, *gs, *ms, *vs)
    return out[:k], out[k:2 * k], out[2 * k:]


def _allgather(blocks, dtypes, name):
    na = len(blocks)

    def body(*refs):
        ins, outs, stages = refs[:na], refs[na:2 * na], refs[2 * na:3 * na]
        send_sems, recv_sems, local_sems = refs[3 * na:]
        x, y, c = _place()
        me, sibling = (x, y, c), (x, y, 1 - c)
        chips = [(1 - x, y), (x, 1 - y), (1 - x, 1 - y)]
        blk = lambda p: 4 * p[0] + 2 * p[1] + p[2]

        def copy(a, k, block, to, src=None):
            return pltpu.make_async_remote_copy(
                src_ref=outs[a].at[blk(block)] if src is None else src, dst_ref=outs[a].at[blk(block)],
                send_sem=send_sems.at[7 * a + k], recv_sem=recv_sems.at[7 * a + k],
                device_id=to, device_id_type=MESH)

        mine, first, passed = [], [], []
        for a in range(na):
            stages[a][...] = ins[a][...].astype(dtypes[a])
            mine.append(pltpu.make_async_copy(stages[a], outs[a].at[blk(me)], local_sems.at[a]))
            mine[-1].start()
            first.append(copy(a, 0, me, sibling, src=stages[a]))
            first += [copy(a, 1 + j, me, (*chip, c), src=stages[a]) for j, chip in enumerate(chips)]
        for cp in first:
            cp.start()
        for j, chip in enumerate(chips):
            for a in range(na):
                copy(a, 1 + j, (*chip, c), me).wait_recv()
                passed.append(copy(a, 4 + j, (*chip, c), sibling))
                passed[-1].start()
        for a in range(na):
            copy(a, 0, sibling, me).wait_recv()
            for j, chip in enumerate(chips):
                copy(a, 4 + j, (*chip, 1 - c), me).wait_recv()
        for cp in first + passed:
            cp.wait_send()
        for cp in mine:
            cp.wait()

    return pl.pallas_call(
        body, name=name,
        in_specs=[pl.BlockSpec(memory_space=pltpu.VMEM)] * na, out_specs=[ANY] * na,
        out_shape=[SDS((NB,) + b.shape, dt) for b, dt in zip(blocks, dtypes)],
        scratch_shapes=[pltpu.VMEM(b.shape, dt) for b, dt in zip(blocks, dtypes)]
        + [pltpu.SemaphoreType.DMA((7 * na,)), pltpu.SemaphoreType.DMA((7 * na,)), pltpu.SemaphoreType.DMA((na,))],
        compiler_params=_params(40),
    )(*blocks)


HBM = pl.BlockSpec(memory_space=pltpu.HBM)
SEMS = pl.BlockSpec(memory_space=pltpu.SEMAPHORE)
EFFECT = pltpu.SideEffectType.DATAFLOW_SIDE_EFFECTING


def _chip_copies(srcs, lands, send_sems, recv_sems):
    x, y, c = _place()
    return [pltpu.make_async_remote_copy(
        src_ref=srcs[a].at[slot], dst_ref=lands[a].at[slot],
        send_sem=send_sems.at[3 * a + slot], recv_sem=recv_sems.at[3 * a + slot],
        device_id=(px, py, c), device_id_type=MESH)
        for a in range(len(srcs)) for slot, (px, py) in enumerate(_other_chips(x, y))]


def _split_start(name, copies, per_array, srcs, lands, after=None):
    na = len(srcs)

    def body(*refs):
        send_sems, recv_sems = refs[-2 * na - 3], refs[-2 * na - 2]
        for cp in copies(refs[:na], refs[na:2 * na], send_sems, recv_sems):
            cp.start()
        refs[-1][...] = jnp.zeros_like(refs[-1])

    hbm = lambda a: pltpu.HBM(a.shape, a.dtype)
    pin = lambda a: pltpu.with_memory_space_constraint(a, pltpu.HBM)
    out = pl.pallas_call(
        body, name=name,
        out_shape=(pltpu.SemaphoreType.DMA((per_array * na,)), pltpu.SemaphoreType.DMA((per_array * na,)),
                   *[hbm(a) for a in srcs], *[hbm(a) for a in lands], SDS((8, BD), F32)),
        in_specs=[HBM] * (2 * na) + ([] if after is None else [ANY]),
        out_specs=(SEMS, SEMS, *[HBM] * (2 * na), pl.BlockSpec(memory_space=pltpu.VMEM)),
        input_output_aliases={i: 2 + i for i in range(2 * na)},
        compiler_params=pltpu.CompilerParams(has_side_effects=EFFECT),
    )(*[pin(a) for a in srcs], *[pin(a) for a in lands], *([] if after is None else [after]))
    return out[0], out[1], out[2:2 + na], out[2 + na:2 + 2 * na], out[-1]


def _split_wait(name, copies, started, after):
    send_sems, recv_sems, srcs, lands, _ = started
    na = len(srcs)

    def body(*refs):
        waits = copies(refs[:na], refs[na:2 * na], refs[2 * na], refs[2 * na + 1])
        for cp in waits:
            cp.wait_send()
        for cp in waits:
            cp.wait_recv()

    hbm = lambda a: pltpu.HBM(a.shape, a.dtype)
    out = pl.pallas_call(
        body, name=name,
        out_shape=(*[hbm(a) for a in srcs], *[hbm(a) for a in lands]),
        in_specs=[HBM] * (2 * na) + [SEMS, SEMS, ANY],
        out_specs=tuple([HBM] * (2 * na)),
        input_output_aliases={i: i for i in range(2 * na)},
        compiler_params=pltpu.CompilerParams(has_side_effects=EFFECT),
    )(*srcs, *lands, send_sems, recv_sems, after)
    return out[na:]


def _add_sibling(place, g, a_in):
    _, r, cols = g.shape
    tr = _row_tile(r)

    def chip(k, pr):
        qx = pr[0] if k in (1, 3) else 1 - pr[0]
        qy = pr[1] if k in (0, 3) else 1 - pr[1]
        return 2 * qx + qy

    def body(place_ref, *refs):
        g_refs, a_refs, (out_ref, own_ref) = refs[0:4], refs[4:8], refs[8:10]
        for k in range(3):
            out_ref[k] = (g_refs[k][0] + a_refs[k][0].astype(F32)).astype(BF16)
        own_ref[...] = g_refs[3][0] + a_refs[3][0].astype(F32)

    mine = lambda k: pl.BlockSpec((1, tr, cols), lambda i, pr: (2 * chip(k, pr) + pr[2], i, 0))
    theirs = lambda k: pl.BlockSpec((1, tr, cols), lambda i, pr: (chip(k, pr), i, 0))
    return pl.pallas_call(
        body, name="add_sibling",
        grid_spec=pltpu.PrefetchScalarGridSpec(
            num_scalar_prefetch=1, grid=(r // tr,),
            in_specs=[mine(k) for k in range(4)] + [theirs(k) for k in range(4)],
            out_specs=[pl.BlockSpec((3, tr, cols), lambda i, pr: (0, i, 0)),
                       pl.BlockSpec((tr, cols), lambda i, pr: (i, 0))]),
        out_shape=[SDS((3, r, cols), BF16), SDS((r, cols), F32)], compiler_params=_params(48),
    )(place, *[g] * 4, *[a_in] * 4)


def _add_chips(own, b_in):
    r, cols = own.shape
    tr = _row_tile(r)

    def body(p_ref, b0_ref, b1_ref, b2_ref, o_ref):
        o_ref[...] = ((p_ref[...] + b0_ref[0].astype(F32)) + b1_ref[0].astype(F32)) + b2_ref[0].astype(F32)

    slot = lambda k: pl.BlockSpec((1, tr, cols), lambda i: (k, i, 0))
    spec = pl.BlockSpec((tr, cols), lambda i: (i, 0))
    return pl.pallas_call(
        body, name="add_chips", grid=(r // tr,), in_specs=[spec, slot(0), slot(1), slot(2)], out_specs=spec,
        out_shape=SDS((r, cols), F32), compiler_params=_params(32),
    )(own, b_in, b_in, b_in)


VEC_NAMES = ("b_merge", "conv_b", "rg_bx", "rg_ba", "rg_lambda", "hg_lb_logits", "hg_norm_g", "final_norm_g")
REP_NAMES = ("rg_wx", "rg_wa", "norm_g") + VEC_NAMES
SMALL_AT = 3 * BD
SMALL_ROWS = 48
MID_ROWS = 448


def _sum_blocks(parts):
    def body(p_ref, o_ref):
        acc = p_ref[0]
        for k in range(1, NB):
            acc = acc + p_ref[k]
        o_ref[...] = acc

    return pl.pallas_call(body, name="sum_blocks", out_shape=SDS(parts.shape[1:], F32))(parts)


def _pack_rows(arrays, width, row_multiple=8):
    flat = jnp.concatenate([a.reshape(-1) for a in arrays])
    rows = -(-flat.shape[0] // width)
    rows = -(-rows // row_multiple) * row_multiple
    return jnp.pad(flat, (0, rows * width - flat.shape[0])).reshape(rows, width)


def _unpack(flat, like):
    out, off = [], 0
    for a in like:
        out.append(flat[off:off + a.size].reshape(a.shape))
        off += a.size
    return out


def kernel(x, w_in, b_merge, conv_w, conv_b, rg_wx, rg_bx, rg_wa, rg_ba, rg_lambda, hg_lb_logits, hg_norm_g, proj_a, proj_b, w_out, norm_g, final_norm_g, loss_target, m_w_in, m_b_merge, m_conv_w, m_conv_b, m_rg_wx, m_rg_bx, m_rg_wa, m_rg_ba, m_rg_lambda, m_hg_lb_logits, m_hg_norm_g, m_proj_a, m_proj_b, m_w_out, m_norm_g, m_final_norm_g, v_w_in, v_b_merge, v_conv_w, v_conv_b, v_rg_wx, v_rg_bx, v_rg_wa, v_rg_ba, v_rg_lambda, v_hg_lb_logits, v_hg_norm_g, v_proj_a, v_proj_b, v_w_out, v_norm_g, v_final_norm_g):
    weights = dict(w_in=w_in, b_merge=b_merge, conv_w=conv_w, conv_b=conv_b, rg_wx=rg_wx, rg_bx=rg_bx, rg_wa=rg_wa,
                   rg_ba=rg_ba, rg_lambda=rg_lambda, hg_lb_logits=hg_lb_logits, hg_norm_g=hg_norm_g, proj_a=proj_a,
                   proj_b=proj_b, w_out=w_out, norm_g=norm_g, final_norm_g=final_norm_g)
    mom1 = dict(w_in=m_w_in, b_merge=m_b_merge, conv_w=m_conv_w, conv_b=m_conv_b, rg_wx=m_rg_wx, rg_bx=m_rg_bx,
                rg_wa=m_rg_wa, rg_ba=m_rg_ba, rg_lambda=m_rg_lambda, hg_lb_logits=m_hg_lb_logits,
                hg_norm_g=m_hg_norm_g, proj_a=m_proj_a, proj_b=m_proj_b, w_out=m_w_out, norm_g=m_norm_g,
                final_norm_g=m_final_norm_g)
    mom2 = dict(w_in=v_w_in, b_merge=v_b_merge, conv_w=v_conv_w, conv_b=v_conv_b, rg_wx=v_rg_wx, rg_bx=v_rg_bx,
                rg_wa=v_rg_wa, rg_ba=v_rg_ba, rg_lambda=v_rg_lambda, hg_lb_logits=v_hg_lb_logits,
                hg_norm_g=v_hg_norm_g, proj_a=v_proj_a, proj_b=v_proj_b, w_out=v_w_out, norm_g=v_norm_g,
                final_norm_g=v_final_norm_g)
    order = list(weights)
    nb, s_len, _ = x.shape
    n = nb * s_len
    px, py, pc = _place()
    place = jnp.stack([px, py, pc]).astype(jnp.int32)

    x2 = x.reshape(n, D)
    cw_blk = jnp.pad(conv_w[0], ((0, 4), (0, 0)))
    order_ids = jnp.stack([_block_id(p) for p in _arrival_order(px, py, pc)]).astype(jnp.int32)
    z, h_all, w_all, pa_all, pb_all, wo_all, cw_all = _gather_inproj(
        order_ids, x2, norm_g, [w_in[0], proj_a[0], proj_b[0], w_out[0], cw_blk], [BF16, BF16, BF16, BF16, F32])
    pa_full, pb_full, wo_full = (a.reshape(D, D) for a in (pa_all, pb_all, wo_all))
    cw8 = cw_all.transpose(1, 0, 2).reshape(8, D)
    wx_b, wa_b = rg_wx[0].astype(BF16), rg_wa[0].astype(BF16)
    cb, bx, ba = conv_b, rg_bx.reshape(1, D), rg_ba.reshape(1, D)
    fin_g = final_norm_g.reshape(1, D)

    hlru, ya = _lru_fwd(z, cw8, cb, wx_b, wa_b, bx, ba, rg_lambda, nb, s_len)
    o_all, yb, st_all = _hgrn_fwd(z, hg_lb_logits, hg_norm_g, nb, s_len)

    (dx2, dya, dyb, dzm, loss_acc, g_fin, g_bm, g_mid) = _mid(
        ya, yb, z, b_merge, x2, loss_target.reshape(n, D), fin_g, pa_full, pb_full, wo_full)
    dzb, g_lg, g_hg = _hgrn_bwd(z, o_all, st_all, dyb, hg_lb_logits, hg_norm_g, nb, s_len)
    dza, g_cw8, g_cb, g_wx, g_wa, g_bx, g_ba, g_lam = _lru_bwd(
        z, hlru, dya, cw8, cb, wx_b, wa_b, bx, ba, rg_lambda, nb, s_len)

    part = dict(b_merge=g_bm, conv_b=g_cb, rg_bx=g_bx, rg_ba=g_ba, rg_lambda=g_lam, hg_lb_logits=g_lg,
                hg_norm_g=g_hg, final_norm_g=g_fin)
    vec = _pack_rows([part[k] for k in VEC_NAMES], BD)
    vec = jnp.pad(vec, ((0, 16 * NB - vec.shape[0]), (0, 0))).reshape(NB, 2, D)
    rows8 = lambda a: jnp.pad(a, ((0, 0), (0, 8 - a.shape[1]), (0, 0)))
    small = jnp.concatenate([g_wx.reshape(NB, 16, D), g_wa.reshape(NB, 16, D),
                             rows8(g_cw8.reshape(8, NB, BD).transpose(1, 0, 2).reshape(NB, 1, D)), rows8(vec),
                             jnp.zeros((NB, MID_ROWS - SMALL_AT - SMALL_ROWS, D), F32)], axis=1)
    g_m = lax.dynamic_update_slice(g_mid, small, (0, SMALL_AT, 0))
    g_w, w_from_sibling, m_from_sibling = _inproj_bwd_w(dza, dzb, dzm, h_all, g_m)
    w_out_bf, w_own = _add_sibling(place, g_w, w_from_sibling)
    m_out_bf, m_own = _add_sibling(place, g_m, m_from_sibling)
    outgoing = [w_out_bf, m_out_bf]
    chip_sums = _split_start("rs_chips_start", _chip_copies, 3, outgoing, [lax.empty(a.shape, a.dtype) for a in outgoing])
    grad_x, g_ng = _inproj_bwd_x(dza, dzb, dzm, w_all, x2, dx2, norm_g, chip_sums[-1])
    from_chips = _split_wait("rs_chips_wait", _chip_copies, chip_sums, grad_x)
    r_w = _add_chips(w_own, from_chips[0])
    r_m = _add_chips(m_own, from_chips[1])
    row = lax.broadcasted_iota(jnp.int32, (8, D), 0)
    mine = jnp.where(row == 0, g_ng, jnp.where(row == 1, loss_acc[0:1, 0:1], 0.0))
    tail = jnp.concatenate([r_m[SMALL_AT:SMALL_AT + SMALL_ROWS], mine], axis=0)
    (tail_all,) = _allgather([tail], [F32], "gather_small_grads")
    summed = _sum_blocks(tail_all[:, SMALL_ROWS:SMALL_ROWS + 8])

    grads = dict(w_in=r_w.reshape(1, D, D),
                 proj_a=r_m[0:BD].reshape(1, BD, D), proj_b=r_m[BD:2 * BD].reshape(1, BD, D),
                 w_out=r_m[2 * BD:3 * BD].reshape(1, BD, D),
                 conv_w=r_m[SMALL_AT + 32].reshape(8, BD)[0:4].reshape(1, 4, BD),
                 rg_wx=tail_all[:, 0:16].reshape(1, NB, BD, BD), rg_wa=tail_all[:, 16:32].reshape(1, NB, BD, BD),
                 norm_g=summed[0:1])
    vec_all = tail_all[:, 40:42].reshape(-1)
    for k, gk in zip(VEC_NAMES, _unpack(vec_all, [weights[k] for k in VEC_NAMES])):
        grads[k] = gk

    delta, new_m, new_v = {}, {}, {}
    for k in ("w_in", "proj_a", "proj_b", "w_out"):
        shp = weights[k].shape
        two = lambda a: a.reshape(shp[1], shp[2])
        d_k, m_k, v_k = _adamw(two(weights[k]), two(grads[k]), two(mom1[k]), two(mom2[k]))
        delta[k], new_m[k], new_v[k] = d_k.reshape(shp), m_k.reshape(shp), v_k.reshape(shp)
    rep = list(REP_NAMES) + ["conv_w"]
    flat2 = lambda a: a.reshape(-1, a.shape[-1])
    outs = _adamw_small(*[[flat2(t[k]) for k in rep] for t in (weights, grads, mom1, mom2)])
    for tgt, arrays in zip((delta, new_m, new_v), outs):
        for k, a in zip(rep, arrays):
            tgt[k] = a.reshape(weights[k].shape)

    return (summed[1, 0],grad_x.reshape(x.shape), *[grads[k] for k in order], *[delta[k] for k in order],
            *[new_m[k] for k in order], *[new_v[k] for k in order])
```

```python
import functools

import jax
import jax.numpy as jnp
from jax import lax
from jax.experimental import pallas as pl
from jax.experimental.pallas import tpu as pltpu

F32 = jnp.float32
BF16 = jnp.bfloat16
SDS = jax.ShapeDtypeStruct
MESH = pl.DeviceIdType.MESH
ANY = pl.BlockSpec(memory_space=pl.ANY)

D = 1024
NB = 8
BD = D // NB
CHUNK = 64
EPS = 1e-6
LRU_C = 8.0
HG_SCALE = BD ** -0.5
ADAM_LR, ADAM_B1, ADAM_B2, ADAM_EPS, ADAM_WD, ADAM_STEP = 0.001, 0.9, 0.999, 1e-08, 0.01, 10

NT_DIMS = (((1,), (1,)), ((), ()))
TN_DIMS = (((0,), (0,)), ((), ()))


def _params(vmem_mib):
    return pltpu.CompilerParams(vmem_limit_bytes=vmem_mib << 20)


def _row_tile(rows, most=256):
    assert rows % 8 == 0
    return max(t for t in range(8, min(rows, most) + 1, 8) if rows % t == 0)


def _sigmoid(v):
    return 0.5 * (jnp.tanh(0.5 * v) + 1.0)


def _groups(v):
    return v.reshape(v.shape[0] // 8, 8, v.shape[1])


def _softplus_neg(lam):
    t = -lam
    e = jnp.exp(-jnp.abs(t))
    w = 1.0 + e
    d = w - 1.0
    l1p = jnp.where(d == 0.0, e, jnp.log(w) * (e / jnp.where(d == 0.0, 1.0, d)))
    return jnp.maximum(t, 0.0) + l1p


def _place():
    return lax.axis_index("x"), lax.axis_index("y"), lax.axis_index("c")


def _other_chips(x, y):
    return [(1 - x, y), (x, 1 - y), (1 - x, 1 - y)]


def _block_id(p):
    return 4 * p[0] + 2 * p[1] + p[2]


def _arrival_order(x, y, c):
    near, far, diag = _other_chips(x, y)
    return [(x, y, c), (x, y, 1 - c), (*far, c), (*near, c), (*far, 1 - c), (*near, 1 - c), (*diag, c), (*diag, 1 - c)]


def _gather_inproj(order_ids, x2, norm_g, blocks, dtypes):
    na = len(blocks)
    n = x2.shape[0]
    tm = min(n, 1024)
    ni = n // tm

    def body(order_ref, x_ref, g_ref, *refs):
        ins, (z_ref, h_ref), outs = refs[:na], refs[na:na + 2], refs[na + 2:2 * na + 2]
        stages = refs[2 * na + 2:3 * na + 2]
        h_full, wbuf, send_sems, recv_sems, local_sems, wsems, hsem = refs[3 * na + 2:]
        j, i = pl.program_id(0), pl.program_id(1)
        x, y, c = _place()
        me, sibling = (x, y, c), (x, y, 1 - c)
        chips = _other_chips(x, y)
        small = range(1, na)

        def copy(a, k, block, to, src=None):
            return pltpu.make_async_remote_copy(
                src_ref=outs[a].at[_block_id(block)] if src is None else src, dst_ref=outs[a].at[_block_id(block)],
                send_sem=send_sems.at[7 * a + k], recv_sem=recv_sems.at[7 * a + k],
                device_id=to, device_id_type=MESH)

        def local(a):
            return pltpu.make_async_copy(stages[a], outs[a].at[_block_id(me)], local_sems.at[a])

        def landed(a, slot):
            copy(a, 1 + slot, (*chips[slot], c), me).wait_recv()
            copy(a, 4 + slot, (*chips[slot], c), sibling).start()

        def passed_on(a, slot):
            copy(a, 4 + slot, (*chips[slot], 1 - c), me).wait_recv()

        @pl.when((j == 0) & (i == 0))
        def _():
            for a in range(na):
                stages[a][...] = ins[a][...].astype(dtypes[a])
                local(a).start()
            for a in range(na):
                copy(a, 0, me, sibling, src=stages[a]).start()
                for slot, chip in enumerate(chips):
                    copy(a, 1 + slot, me, (*chip, c), src=stages[a]).start()

        @pl.when(j == 0)
        def _():
            xv = x_ref[...]
            r = lax.rsqrt(jnp.mean(xv * xv, axis=-1, keepdims=True) + EPS)
            hb = ((xv * r) * g_ref[...]).astype(BF16)
            h_full[pl.ds(pl.multiple_of(i * tm, tm), tm), :] = hb

        save_h = pltpu.make_async_copy(h_full, h_ref, hsem)
        pl.when((j == 0) & (i == ni - 1))(save_h.start)

        steps = [
            lambda: local(0).wait(),
            lambda: copy(0, 0, sibling, me).wait_recv(),
            lambda: landed(0, 1),
            lambda: landed(0, 0),
            lambda: passed_on(0, 1),
            lambda: passed_on(0, 0),
            lambda: landed(0, 2),
            lambda: passed_on(0, 2),
        ]
        def w_load(k):
            return pltpu.make_async_copy(outs[0].at[order_ref[k]], wbuf.at[k % 2], wsems.at[k % 2])

        for k, step in enumerate(steps):
            @pl.when((j == 0) & (i == 0) if k == 0 else (j == k - 1) & (i == ni - 1))
            def _(k=k, step=step):
                step()
                w_load(k).start()

        pl.when(i == 0)(lambda: w_load(j).wait())
        z_ref[0] = jnp.dot(h_full[pl.ds(pl.multiple_of(i * tm, tm), tm), :], wbuf[j % 2], preferred_element_type=F32)

        @pl.when((j == NB - 1) & (i == ni - 1))
        def _():
            save_h.wait()
            for slot in range(3):
                for a in small:
                    landed(a, slot)
            for a in small:
                local(a).wait()
                copy(a, 0, sibling, me).wait_recv()
                for slot in range(3):
                    passed_on(a, slot)
            for a in range(na):
                copy(a, 0, me, sibling, src=stages[a]).wait_send()
                for slot, chip in enumerate(chips):
                    copy(a, 1 + slot, me, (*chip, c), src=stages[a]).wait_send()
                    copy(a, 4 + slot, (*chip, c), sibling).wait_send()

    rows_once = lambda j, i, order: (jnp.where(j == 0, i, ni - 1), 0)
    vmem = pl.BlockSpec(memory_space=pltpu.VMEM)
    return pl.pallas_call(
        body, name="gather_inproj",
        grid_spec=pltpu.PrefetchScalarGridSpec(
            num_scalar_prefetch=1, grid=(NB, ni),
            in_specs=[pl.BlockSpec((tm, D), rows_once), pl.BlockSpec((1, D), lambda j, i, order: (0, 0))] + [vmem] * na,
            out_specs=[pl.BlockSpec((1, tm, D), lambda j, i, order: (order[j], i, 0)), ANY] + [ANY] * na,
            scratch_shapes=[pltpu.VMEM(b.shape, dt) for b, dt in zip(blocks, dtypes)]
            + [pltpu.VMEM((n, D), BF16), pltpu.VMEM((2, D, D), BF16),
               pltpu.SemaphoreType.DMA((7 * na,)), pltpu.SemaphoreType.DMA((7 * na,)),
               pltpu.SemaphoreType.DMA((na,)), pltpu.SemaphoreType.DMA((2,)), pltpu.SemaphoreType.DMA(())]),
        out_shape=[SDS((NB, n, D), F32), SDS((n, D), BF16)] + [SDS((NB,) + b.shape, dt) for b, dt in zip(blocks, dtypes)],
        compiler_params=_params(56),
    )(order_ids, x2, norm_g, *blocks)


LRU_T = 256


def _shifted(groups, shifts):
    row = lax.broadcasted_iota(jnp.int32, (groups.shape[0] - 1,) + groups.shape[1:], 1)
    out = []
    for s in shifts:
        y = pltpu.roll(groups, s % 8, 1)
        moved = jnp.where(row >= s, y[1:], y[:-1]) if s > 0 else jnp.where(row < 8 + s, y[:-1], y[1:])
        out.append(moved.reshape(-1, groups.shape[2]))
    return out


def _conv(taps, cw, cb):
    acc = taps[0] * cw[0:1, :] + taps[1] * cw[1:2, :]
    acc = acc + taps[2] * cw[2:3, :]
    acc = acc + taps[3] * cw[3:4, :]
    return cb + acc


def _lru_gates(xa, wx_ref, wa_ref, bx, ba, lam):
    xab = xa.astype(BF16)
    pis, prs = [], []
    for h in range(NB):
        xs = xab[:, h * BD:(h + 1) * BD]
        pis.append(jnp.dot(xs, wx_ref[h], preferred_element_type=F32))
        prs.append(jnp.dot(xs, wa_ref[h], preferred_element_type=F32))
    gi = _sigmoid(jnp.concatenate(pis, axis=1) + bx)
    gr = _sigmoid(jnp.concatenate(prs, axis=1) + ba)
    sp = _softplus_neg(lam)
    log_a = (-LRU_C * gr) * sp
    a = jnp.exp(log_a)
    mult = jnp.sqrt(-jnp.tanh(log_a) * (a * a + 1.0))
    return xab, gi, gr, sp, a, mult


def _lru_fwd(z, cw8, cb, wx, wa, bx, ba, lam, nb, s_len):
    n = nb * s_len
    t = LRU_T
    ns = s_len // t

    def body(xp_ref, ga_ref, cw_ref, cb_ref, wx_ref, wa_ref, bx_ref, ba_ref, lam_ref,
             h_ref, ya_ref, ext, a_s, u_s, carry):
        @pl.when(pl.program_id(1) == 0)
        def _():
            ext[0:8, :] = jnp.zeros((8, D), F32)
            carry[...] = jnp.zeros((8, D), F32)

        xp = xp_ref[0]
        ext[8:8 + t, :] = xp
        xa = _conv(_shifted(_groups(ext[...]), (3, 2, 1)) + [xp], cw_ref[...], cb_ref[...])
        ext[0:8, :] = xp[t - 8:t, :]
        _, gi, _, _, a, mult = _lru_gates(xa, wx_ref, wa_ref, bx_ref[...], ba_ref[...], lam_ref[...])
        u = (mult * gi) * xa
        a, u = _groups(a), _groups(u)
        row = lax.broadcasted_iota(jnp.int32, a.shape, 1)
        for sh in (1, 2, 4):
            a_sh = pltpu.roll(a, sh, 1)
            u_sh = pltpu.roll(u, sh, 1)
            m = row >= sh
            u = jnp.where(m, a * u_sh + u, u)
            a = jnp.where(m, a * a_sh, a)
        a_s[...] = a.reshape(t, D)
        u_s[...] = u.reshape(t, D)

        def step(g, c):
            r = pl.multiple_of(g * 8, 8)
            hg = u_s[pl.ds(r, 8), :] + a_s[pl.ds(r, 8), :] * c
            h_ref[pl.ds(r, 8), :] = hg
            return hg[7:8, :]

        c_out = lax.fori_loop(0, t // 8, step, carry[0:1, :], unroll=4)
        carry[0:1, :] = c_out
        ga = ga_ref[0]
        ya_ref[...] = (h_ref[...] * (ga * _sigmoid(ga))).astype(BF16)

    row_map = lambda b, s: (b * ns + s, 0)
    rep2 = lambda b, s: (0, 0)
    rep3 = lambda b, s: (0, 0, 0)
    return pl.pallas_call(
        body, name="lru_fwd", grid=(nb, ns),
        in_specs=[pl.BlockSpec((1, t, D), lambda b, s: (0, b * ns + s, 0)),
                  pl.BlockSpec((1, t, D), lambda b, s: (1, b * ns + s, 0)),
                  pl.BlockSpec((8, D), rep2), pl.BlockSpec((1, D), rep2),
                  pl.BlockSpec((NB, BD, BD), rep3), pl.BlockSpec((NB, BD, BD), rep3),
                  pl.BlockSpec((1, D), rep2), pl.BlockSpec((1, D), rep2), pl.BlockSpec((1, D), rep2)],
        out_specs=[pl.BlockSpec((t, D), row_map), pl.BlockSpec((t, D), row_map)],
        out_shape=[SDS((n, D), F32), SDS((n, D), BF16)],
        scratch_shapes=[pltpu.VMEM((t + 8, D), F32), pltpu.VMEM((t, D), F32), pltpu.VMEM((t, D), F32),
                        pltpu.VMEM((8, D), F32)],
        compiler_params=_params(48),
    )(z, z, cw8, cb, wx, wa, bx, ba, lam)


def _lru_bwd(z, h_all, dya, cw8, cb, wx, wa, bx, ba, lam, nb, s_len):
    n = nb * s_len
    t = LRU_T
    ns = s_len // t
    t8 = t // 8

    def body(xp_ref, xph_ref, ga_ref, h_ref, hh_ref, dya_ref, cw_ref, cb_ref, wx_ref, wa_ref, bx_ref, ba_ref,
             lam_ref, dz_ref, gcw_ref, gcb_ref, gwx_ref, gwa_ref, gbx_ref, gba_ref, glam_ref,
             ext, hext, dext, a_s, u_s, dh_s, carry):
        b, s = pl.program_id(0), pl.program_id(1)
        first_tile = s == ns - 1

        @pl.when((b == 0) & (s == 0))
        def _():
            for ref in (gcw_ref, gcb_ref, gwx_ref, gwa_ref, gbx_ref, gba_ref, glam_ref):
                ref[...] = jnp.zeros(ref.shape, F32)

        @pl.when(s == 0)
        def _():
            dext[t:t + 8, :] = jnp.zeros((8, D), F32)
            carry[...] = jnp.zeros((8, D), F32)

        keep = jnp.where(first_tile, 0.0, 1.0)
        xp = xp_ref[0]
        ext[0:8, :] = xph_ref[0] * keep
        ext[8:8 + t, :] = xp
        hext[0:8, :] = hh_ref[...] * keep
        hext[8:8 + t, :] = h_ref[...]
        cw = cw_ref[...]
        lam = lam_ref[...]
        taps = _shifted(_groups(ext[...]), (3, 2, 1)) + [xp]
        xa = _conv(taps, cw, cb_ref[...])
        xab, gi, gr, sp, a, mult = _lru_gates(xa, wx_ref, wa_ref, bx_ref[...], ba_ref[...], lam)
        (h_prev,) = _shifted(_groups(hext[...]), (1,))
        ga = ga_ref[0]
        sg = _sigmoid(ga)
        dya_v = dya_ref[...]
        d_ga = dya_v * h_ref[...] * (sg * (1.0 + ga * (1.0 - sg)))
        g_in = dya_v * (ga * sg)

        (an,) = _shifted(jnp.concatenate([_groups(a), jnp.ones((1, 8, D), F32)], axis=0), (-1,))
        an, u = _groups(an), _groups(g_in)
        row = lax.broadcasted_iota(jnp.int32, an.shape, 1)
        for sh in (1, 2, 4):
            a_sh = pltpu.roll(an, 8 - sh, 1)
            u_sh = pltpu.roll(u, 8 - sh, 1)
            m = row < 8 - sh
            u = jnp.where(m, u + an * u_sh, u)
            an = jnp.where(m, an * a_sh, an)
        a_s[...] = an.reshape(t, D)
        u_s[...] = u.reshape(t, D)

        def step(i, c):
            r = pl.multiple_of((t8 - 1 - i) * 8, 8)
            dg = u_s[pl.ds(r, 8), :] + a_s[pl.ds(r, 8), :] * c
            dh_s[pl.ds(r, 8), :] = dg
            return dg[0:1, :]

        lax.fori_loop(0, t8, step, carry[0:1, :], unroll=4)
        dh = dh_s[...]
        carry[0:1, :] = a[0:1, :] * dh[0:1, :]

        d_a = dh * h_prev
        dux = dh * xa
        d_mult = dux * gi
        d_gi = dux * mult
        d_xa = dh * (mult * gi)
        d_loga = d_a * a - d_mult * ((a * a) / mult)
        d_gr = d_loga * (-LRU_C * sp)
        d_sp = jnp.sum(d_loga * (-LRU_C * gr), axis=0, keepdims=True)
        glam_ref[...] += d_sp * (-_sigmoid(-lam))
        d_pi = d_gi * gi * (1.0 - gi)
        d_pr = d_gr * gr * (1.0 - gr)
        gbx_ref[...] += jnp.sum(d_pi, axis=0, keepdims=True)
        gba_ref[...] += jnp.sum(d_pr, axis=0, keepdims=True)
        dpib = d_pi.astype(BF16)
        dprb = d_pr.astype(BF16)
        back = []
        for h in range(NB):
            cs = slice(h * BD, (h + 1) * BD)
            gwx_ref[h] += lax.dot_general(xab[:, cs], dpib[:, cs], TN_DIMS, preferred_element_type=F32)
            gwa_ref[h] += lax.dot_general(xab[:, cs], dprb[:, cs], TN_DIMS, preferred_element_type=F32)
            back.append(lax.dot_general(dpib[:, cs], wx_ref[h], NT_DIMS, preferred_element_type=F32)
                        + lax.dot_general(dprb[:, cs], wa_ref[h], NT_DIMS, preferred_element_type=F32))
        d_xa = d_xa + jnp.concatenate(back, axis=1)

        dext[0:t, :] = d_xa
        later = _shifted(_groups(dext[...]), (-3, -2, -1))
        d_xp = later[0] * cw[0:1, :] + later[1] * cw[1:2, :]
        d_xp = d_xp + later[2] * cw[2:3, :]
        d_xp = d_xp + d_xa * cw[3:4, :]
        dext[t:t + 8, :] = d_xa[0:8, :]
        gcb_ref[...] += jnp.sum(d_xa, axis=0, keepdims=True)
        for k in range(4):
            gcw_ref[k:k + 1, :] += jnp.sum(d_xa * taps[k], axis=0, keepdims=True)
        dz_ref[0] = d_xp.astype(BF16)
        dz_ref[1] = d_ga.astype(BF16)

    rb = lambda b, s: b * ns + (ns - 1 - s)
    halo = lambda b, s: jnp.maximum(rb(b, s) * t8 - 1, 0)
    rep2 = lambda b, s: (0, 0)
    rep3 = lambda b, s: (0, 0, 0)
    return pl.pallas_call(
        body, name="lru_bwd", grid=(nb, ns),
        in_specs=[pl.BlockSpec((1, t, D), lambda b, s: (0, rb(b, s), 0)),
                  pl.BlockSpec((1, 8, D), lambda b, s: (0, halo(b, s), 0)),
                  pl.BlockSpec((1, t, D), lambda b, s: (1, rb(b, s), 0)),
                  pl.BlockSpec((t, D), lambda b, s: (rb(b, s), 0)),
                  pl.BlockSpec((8, D), lambda b, s: (halo(b, s), 0)),
                  pl.BlockSpec((t, D), lambda b, s: (rb(b, s), 0)),
                  pl.BlockSpec((8, D), rep2), pl.BlockSpec((1, D), rep2),
                  pl.BlockSpec((NB, BD, BD), rep3), pl.BlockSpec((NB, BD, BD), rep3),
                  pl.BlockSpec((1, D), rep2), pl.BlockSpec((1, D), rep2), pl.BlockSpec((1, D), rep2)],
        out_specs=[pl.BlockSpec((2, t, D), lambda b, s: (0, rb(b, s), 0)),
                   pl.BlockSpec((8, D), rep2), pl.BlockSpec((1, D), rep2),
                   pl.BlockSpec((NB, BD, BD), rep3), pl.BlockSpec((NB, BD, BD), rep3),
                   pl.BlockSpec((1, D), rep2), pl.BlockSpec((1, D), rep2), pl.BlockSpec((1, D), rep2)],
        out_shape=[SDS((2, n, D), BF16), SDS((8, D), F32), SDS((1, D), F32),
                   SDS((NB, BD, BD), F32), SDS((NB, BD, BD), F32),
                   SDS((1, D), F32), SDS((1, D), F32), SDS((1, D), F32)],
        scratch_shapes=[pltpu.VMEM((t + 8, D), F32), pltpu.VMEM((t + 8, D), F32), pltpu.VMEM((t + 8, D), F32),
                        pltpu.VMEM((t, D), F32), pltpu.VMEM((t, D), F32), pltpu.VMEM((t, D), F32),
                        pltpu.VMEM((8, D), F32)],
        compiler_params=_params(56),
    )(z, z, z, h_all, h_all, dya, cw8, cb, wx, wa, bx, ba, lam)


HG_T = 512
HG_NC = HG_T // CHUNK
BNT_DIMS = (((2,), (2,)), ((0,), (0,)))
BNN_DIMS = (((2,), (1,)), ((0,), (0,)))
BTN_DIMS = (((1,), (1,)), ((0,), (0,)))


def _lower_bound(lg):
    m = jnp.max(lg, axis=0, keepdims=True)
    e = jnp.exp(lg - m)
    return e[0:1, :] / jnp.sum(e, axis=0, keepdims=True)


def _tri(upper):
    r = lax.broadcasted_iota(jnp.int32, (HG_NC, CHUNK, CHUNK), 1)
    c = lax.broadcasted_iota(jnp.int32, (HG_NC, CHUNK, CHUNK), 2)
    return (c >= r) if upper else (r >= c)


def _bdot(a, b, dims):
    return lax.dot_general(a, b, dims, preferred_element_type=F32)


def _tri_sums(upper, a):
    tri = _tri(upper).astype(BF16)
    a1 = a.astype(BF16)
    r1 = a - a1.astype(F32)
    a2 = r1.astype(BF16)
    a3 = (r1 - a2.astype(F32)).astype(BF16)
    return _bdot(tri, a1, BNN_DIMS) + (_bdot(tri, a2, BNN_DIMS) + _bdot(tri, a3, BNN_DIMS))


def _chunks(a):
    return a.reshape(HG_NC, CHUNK, BD)


def _hg_tile(q, fp, lb):
    q, fp = _chunks(q), _chunks(fp)
    sig = _sigmoid(fp)
    f = lb + (1.0 - lb) * sig
    log_f = jnp.log(f)
    k = 1.0 - f
    b = _tri_sums(False, log_f)
    b_mid = b[:, CHUNK // 2:CHUNK // 2 + 1, :]
    b_last = b[:, CHUNK - 1:CHUNK, :]
    sq = _sigmoid(q)
    qh = q * sq
    e_qi = jnp.exp(b - b_mid)
    e_ki = jnp.exp(b_mid - b)
    e_qs = jnp.exp(b)
    e_ks = jnp.exp(b_last - b)
    dc = jnp.exp(b_last)
    q_in = (qh * e_qi) * HG_SCALE
    k_in = k * e_ki
    q_st = (qh * e_qs) * HG_SCALE
    k_st = k * e_ks
    att = _bdot(q_in.astype(BF16), k_in.astype(BF16), BNT_DIMS)
    att = jnp.where(_tri(False), att, 0.0)
    return dict(q=q, sig=sig, f=f, k=k, sq=sq, e_qi=e_qi, e_ki=e_ki, e_qs=e_qs, e_ks=e_ks, dc=dc,
                q_in=q_in, k_in=k_in, q_st=q_st, k_st=k_st, att=att)


def _hgrn_fwd(z, lb_logits, hg_g, nb, s_len):
    n = nb * s_len
    t = HG_T
    ns = s_len // t
    nchunk = s_len // CHUNK

    def body(q_ref, f_ref, v_ref, gb_ref, lg_ref, g_ref, o_ref, yb_ref, st_ref, st):
        @pl.when(pl.program_id(1) == 0)
        def _():
            st[...] = jnp.zeros((NB, BD, BD), F32)

        def head(h, carry):
            cols = pl.ds(pl.multiple_of(h * BD, BD), BD)
            lb = _lower_bound(lg_ref[:, cols])
            ck = _hg_tile(q_ref[0, :, cols], f_ref[0, :, cols], lb)
            vb = _chunks(v_ref[0, :, cols]).astype(BF16)
            kv = _bdot(vb, ck["k_st"].astype(BF16), BTN_DIMS)
            states = [st[h]]
            for c in range(HG_NC):
                states.append(states[c] * ck["dc"][c] + kv[c])
            st[h] = states[HG_NC]
            s_in = jnp.stack(states[:HG_NC], axis=0)
            st_ref[h] = s_in
            o = (_bdot(ck["att"].astype(BF16), vb, BNN_DIMS)
                 + _bdot(ck["q_st"].astype(BF16), s_in.astype(BF16), BNT_DIMS))
            o_ref[:, cols] = o.reshape(t, BD)
            r = lax.rsqrt(jnp.mean(o * o, axis=-1, keepdims=True) + EPS)
            gb = _chunks(gb_ref[0, :, cols])
            yb_ref[:, cols] = (((o * r) * g_ref[...]) * (gb * _sigmoid(gb))).astype(BF16).reshape(t, BD)
            return carry

        lax.fori_loop(0, NB, head, 0, unroll=2)

    seg = lambda j: pl.BlockSpec((1, t, D), lambda b, s: (j, b * ns + s, 0))
    tile = pl.BlockSpec((t, D), lambda b, s: (b * ns + s, 0))
    return pl.pallas_call(
        body, name="hgrn_fwd", grid=(nb, ns),
        in_specs=[seg(2), seg(3), seg(4), seg(5),
                  pl.BlockSpec((2, D), lambda b, s: (0, 0)), pl.BlockSpec((1, BD), lambda b, s: (0, 0))],
        out_specs=[tile, tile, pl.BlockSpec((NB, HG_NC, BD, BD), lambda b, s: (b, s, 0, 0))],
        out_shape=[SDS((n, D), F32), SDS((n, D), BF16), SDS((nb * NB, nchunk, BD, BD), F32)],
        scratch_shapes=[pltpu.VMEM((NB, BD, BD), F32)],
        compiler_params=_params(56),
    )(z, z, z, z, lb_logits, hg_g)


def _hgrn_bwd(z, o_all, st_all, dyb, lb_logits, hg_g, nb, s_len):
    n = nb * s_len
    t = HG_T
    ns = s_len // t

    def body(q_ref, f_ref, v_ref, gb_ref, o_ref, st_ref, dyb_ref, lg_ref, g_ref,
             dz_ref, glg_ref, ghg_ref, dst, dlb):
        b, s = pl.program_id(0), pl.program_id(1)

        @pl.when((b == 0) & (s == 0))
        def _():
            ghg_ref[...] = jnp.zeros((1, BD), F32)
            dlb[...] = jnp.zeros((8, D), F32)

        @pl.when(s == 0)
        def _():
            dst[...] = jnp.zeros((NB, BD, BD), F32)

        g = g_ref[...]

        def head(h, carry):
            cols = pl.ds(pl.multiple_of(h * BD, BD), BD)
            lb = _lower_bound(lg_ref[:, cols])
            ck = _hg_tile(q_ref[0, :, cols], f_ref[0, :, cols], lb)
            q = ck["q"]
            vb = _chunks(v_ref[0, :, cols]).astype(BF16)
            gb = _chunks(gb_ref[0, :, cols])
            o = _chunks(o_ref[:, cols])
            dyb_v = _chunks(dyb_ref[:, cols])
            s_in = st_ref[h]

            sgb = _sigmoid(gb)
            r = lax.rsqrt(jnp.mean(o * o, axis=-1, keepdims=True) + EPS)
            ohat = o * r
            d_on = dyb_v * (gb * sgb)
            d_gb = dyb_v * (ohat * g) * (sgb * (1.0 + gb * (1.0 - sgb)))
            ghg_ref[...] += jnp.sum(jnp.sum(d_on * ohat, axis=1), axis=0, keepdims=True)
            tt = d_on * g
            d_o = r * (tt - ohat * jnp.mean(tt * ohat, axis=-1, keepdims=True))
            dob = d_o.astype(BF16)

            attb = ck["att"].astype(BF16)
            q_inb, k_inb = ck["q_in"].astype(BF16), ck["k_in"].astype(BF16)
            q_stb, k_stb = ck["q_st"].astype(BF16), ck["k_st"].astype(BF16)
            d_att = jnp.where(_tri(False), _bdot(dob, vb, BNT_DIMS), 0.0).astype(BF16)
            d_q_in = _bdot(d_att, k_inb, BNN_DIMS)
            d_k_in = _bdot(d_att, q_inb, BTN_DIMS)
            d_q_st = _bdot(dob, s_in.astype(BF16), BNN_DIMS)
            qdo = _bdot(dob, q_stb, BTN_DIMS)
            d_states = [None] * HG_NC + [dst[h]]
            for c in reversed(range(HG_NC)):
                d_states[c] = d_states[c + 1] * ck["dc"][c] + qdo[c]
            dst[h] = d_states[0]
            ds_out = jnp.stack(d_states[1:], axis=0)
            dsb = ds_out.astype(BF16)
            d_v = _bdot(attb, dob, BTN_DIMS) + _bdot(k_stb, dsb, BNT_DIMS)
            d_k_st = _bdot(vb, dsb, BNN_DIMS)
            d_dc = jnp.sum(ds_out * s_in, axis=1, keepdims=True)

            p_qi = d_q_in * ck["q_in"]
            p_ki = d_k_in * ck["k_in"]
            p_qs = d_q_st * ck["q_st"]
            p_ks = d_k_st * ck["k_st"]
            d_qh = (d_q_in * ck["e_qi"] + d_q_st * ck["e_qs"]) * HG_SCALE
            d_k = d_k_in * ck["e_ki"] + d_k_st * ck["e_ks"]
            d_b = (p_qi - p_ki) + (p_qs - p_ks)
            d_b_mid = jnp.sum(p_ki - p_qi, axis=1, keepdims=True)
            d_b_last = jnp.sum(p_ks, axis=1, keepdims=True) + d_dc * ck["dc"]
            rowi = lax.broadcasted_iota(jnp.int32, (HG_NC, CHUNK, BD), 1)
            d_b = d_b + jnp.where(rowi == CHUNK // 2, d_b_mid, 0.0) + jnp.where(rowi == CHUNK - 1, d_b_last, 0.0)
            d_logf = _tri_sums(True, d_b)
            d_f = d_logf / ck["f"] - d_k
            sig, sq = ck["sig"], ck["sq"]
            d_fp = d_f * (1.0 - lb) * (sig * (1.0 - sig))
            dlb[0:1, cols] += jnp.sum(jnp.sum(d_f * (1.0 - sig), axis=1), axis=0, keepdims=True)
            d_q = d_qh * (sq * (1.0 + q * (1.0 - sq)))
            dz_ref[0, :, cols] = d_q.astype(BF16).reshape(t, BD)
            dz_ref[1, :, cols] = d_fp.astype(BF16).reshape(t, BD)
            dz_ref[2, :, cols] = d_v.astype(BF16).reshape(t, BD)
            dz_ref[3, :, cols] = d_gb.astype(BF16).reshape(t, BD)
            return carry

        lax.fori_loop(0, NB, head, 0, unroll=2)

        @pl.when((b == nb - 1) & (s == ns - 1))
        def _():
            lb = _lower_bound(lg_ref[...])
            dl = dlb[0:1, :] * (lb * (1.0 - lb))
            glg_ref[0:1, :] = dl
            glg_ref[1:2, :] = -dl

    rb = lambda b, s: b * ns + (ns - 1 - s)
    seg = lambda j: pl.BlockSpec((1, t, D), lambda b, s: (j, rb(b, s), 0))
    tile = pl.BlockSpec((t, D), lambda b, s: (rb(b, s), 0))
    return pl.pallas_call(
        body, name="hgrn_bwd", grid=(nb, ns),
        in_specs=[seg(2), seg(3), seg(4), seg(5), tile,
                  pl.BlockSpec((NB, HG_NC, BD, BD), lambda b, s: (b, ns - 1 - s, 0, 0)),
                  tile, pl.BlockSpec((2, D), lambda b, s: (0, 0)), pl.BlockSpec((1, BD), lambda b, s: (0, 0))],
        out_specs=[pl.BlockSpec((4, t, D), lambda b, s: (0, rb(b, s), 0)),
                   pl.BlockSpec((2, D), lambda b, s: (0, 0)), pl.BlockSpec((1, BD), lambda b, s: (0, 0))],
        out_shape=[SDS((4, n, D), BF16), SDS((2, D), F32), SDS((1, BD), F32)],
        scratch_shapes=[pltpu.VMEM((NB, BD, BD), F32), pltpu.VMEM((8, D), F32)],
        compiler_params=_params(60),
    )(z, z, z, z, o_all, st_all, dyb, lb_logits, hg_g)


def _mid(ya, yb, z, b_merge, x2, tgt, fin_g, pa, pb, wo):
    n = x2.shape[0]
    tm = 256
    ni = n // tm

    def body(ya_ref, yb_ref, gma_ref, gmb_ref, bm_ref, x_ref, t_ref, fg_ref, pa_hbm, pb_hbm, wo_hbm,
             dx2_ref, dya_ref, dyb_ref, dgm_ref, loss_ref, gfg_ref, gbm_ref, gm_hbm,
             pa_v, pb_v, wo_v, gpa_v, gpb_v, gwo_v, sem):
        i = pl.program_id(0)
        by_owner = lambda g: g.reshape(NB, BD, D)
        loads = [pltpu.make_async_copy(src, dst, sem.at[k])
                 for k, (src, dst) in enumerate(((pa_hbm, pa_v), (pb_hbm, pb_v), (wo_hbm, wo_v)))]
        stores = [pltpu.make_async_copy(src, dst, sem.at[k])
                  for k, (src, dst) in enumerate((g, gm_hbm.at[:, pl.ds(slot * BD, BD), :])
                                                 for slot, g in enumerate((gpa_v, gpb_v, gwo_v)))]

        @pl.when(i == 0)
        def _():
            for cp in loads:
                cp.start()
            for ref in (gpa_v, gpb_v, gwo_v, loss_ref, gfg_ref, gbm_ref):
                ref[...] = jnp.zeros(ref.shape, F32)
            for cp in loads:
                cp.wait()

        ya_v = ya_ref[...]
        yb_v = yb_ref[...]
        out_a = jnp.dot(ya_v, pa_v[...], preferred_element_type=F32)
        out_b = jnp.dot(yb_v, pb_v[...], preferred_element_type=F32)
        bm = bm_ref[...]
        g_a = _sigmoid(gma_ref[0] + bm[:, 0:D])
        g_b = _sigmoid(gmb_ref[0] + bm[:, D:2 * D])
        mixed = g_a * out_a + g_b * out_b
        mixb = mixed.astype(BF16)
        xo = x_ref[...] + jnp.dot(mixb, wo_v[...], preferred_element_type=F32)
        r = lax.rsqrt(jnp.mean(xo * xo, axis=-1, keepdims=True) + EPS)
        xn = xo * r
        fg = fg_ref[...]
        e = xn * fg - t_ref[...]
        loss_ref[...] += 0.5 * jnp.sum(jnp.mean(e * e, axis=-1, keepdims=True))
        dy = e * (1.0 / D)
        gfg_ref[...] += jnp.sum(dy * xn, axis=0, keepdims=True)
        dxn = dy * fg
        dx2 = r * (dxn - xn * jnp.mean(dxn * xn, axis=-1, keepdims=True))
        dx2_ref[...] = dx2
        dx2b = dx2.astype(BF16)
        d_mixed = lax.dot_general(dx2b, wo_v[...], NT_DIMS, preferred_element_type=F32)
        gwo_v[...] += by_owner(lax.dot_general(mixb, dx2b, TN_DIMS, preferred_element_type=F32))
        d_oa = (d_mixed * g_a).astype(BF16)
        d_ob = (d_mixed * g_b).astype(BF16)
        dgm_a = (d_mixed * out_a) * (g_a * (1.0 - g_a))
        dgm_b = (d_mixed * out_b) * (g_b * (1.0 - g_b))
        gbm_ref[:, 0:D] += jnp.sum(dgm_a, axis=0, keepdims=True)
        gbm_ref[:, D:2 * D] += jnp.sum(dgm_b, axis=0, keepdims=True)
        dgm_ref[0] = dgm_a.astype(BF16)
        dgm_ref[1] = dgm_b.astype(BF16)
        dya_ref[...] = lax.dot_general(d_oa, pa_v[...], NT_DIMS, preferred_element_type=F32)
        dyb_ref[...] = lax.dot_general(d_ob, pb_v[...], NT_DIMS, preferred_element_type=F32)
        gpa_v[...] += by_owner(lax.dot_general(ya_v, d_oa, TN_DIMS, preferred_element_type=F32))
        gpb_v[...] += by_owner(lax.dot_general(yb_v, d_ob, TN_DIMS, preferred_element_type=F32))

        @pl.when(i == ni - 1)
        def _():
            for cp in stores:
                cp.start()
            for cp in stores:
                cp.wait()

    rows = pl.BlockSpec((tm, D), lambda i: (i, 0))
    rep = lambda shape: pl.BlockSpec(shape, lambda i: (0,) * len(shape))
    return pl.pallas_call(
        body, name="mid", grid=(ni,),
        in_specs=[rows, rows,
                  pl.BlockSpec((1, tm, D), lambda i: (6, i, 0)), pl.BlockSpec((1, tm, D), lambda i: (7, i, 0)),
                  rep((1, 2 * D)), rows, rows, rep((1, D)), ANY, ANY, ANY],
        out_specs=[rows, rows, rows, pl.BlockSpec((2, tm, D), lambda i: (0, i, 0)),
                   rep((8, BD)), rep((1, D)), rep((1, 2 * D)), ANY],
        out_shape=[SDS((n, D), F32), SDS((n, D), F32), SDS((n, D), F32), SDS((2, n, D), BF16),
                   SDS((8, BD), F32), SDS((1, D), F32), SDS((1, 2 * D), F32),
                   SDS((NB, MID_ROWS, D), F32)],
        scratch_shapes=[pltpu.VMEM((D, D), BF16)] * 3 + [pltpu.VMEM((NB, BD, D), F32)] * 3 + [pltpu.SemaphoreType.DMA((3,))],
        compiler_params=_params(60),
    )(ya, yb, z, z, b_merge, x2, tgt, fin_g, pa, pb, wo)


def _dz_specs(tm, ni, row_major):
    if row_major:
        ia = lambda i, j: (jnp.minimum(j, 1), i, 0)
        ib = lambda i, j: (jnp.clip(j - 2, 0, 3), i, 0)
        im = lambda i, j: (jnp.clip(j - 6, 0, 1), i, 0)
    else:
        last = ni - 1
        ia = lambda j, i: (jnp.minimum(j, 1), jnp.where(j < 2, i, last), 0)
        ib = lambda j, i: (jnp.clip(j - 2, 0, 3), jnp.where(j < 2, 0, jnp.where(j < 6, i, last)), 0)
        im = lambda j, i: (jnp.clip(j - 6, 0, 1), jnp.where(j < 6, 0, i), 0)
    return [pl.BlockSpec((1, tm, D), f) for f in (ia, ib, im)]


def _inproj_bwd_x(dza, dzb, dzm, w_all, x2, dx2, norm_g, after):
    n = x2.shape[0]
    tm = 512
    ni = n // tm

    def body(dza_ref, dzb_ref, dzm_ref, w_ref, x_ref, dx2_ref, g_ref, after_ref, gx_ref, gg_ref, acc):
        i, j = pl.program_id(0), pl.program_id(1)

        @pl.when((i == 0) & (j == 0))
        def _():
            gg_ref[...] = jnp.zeros((1, D), F32)

        @pl.when(j == 0)
        def _():
            acc[...] = jnp.zeros((tm, D), F32)

        def add(ref):
            acc[...] += lax.dot_general(ref[0], w_ref[0], NT_DIMS, preferred_element_type=F32)

        pl.when(j < 2)(lambda: add(dza_ref))
        pl.when((j >= 2) & (j < 6))(lambda: add(dzb_ref))
        pl.when(j >= 6)(lambda: add(dzm_ref))

        @pl.when(j == NB - 1)
        def _():
            x = x_ref[...]
            r = lax.rsqrt(jnp.mean(x * x, axis=-1, keepdims=True) + EPS)
            xn = x * r
            dh = acc[...]
            gg_ref[...] += jnp.sum(dh * xn, axis=0, keepdims=True)
            dxn = dh * g_ref[...]
            gx_ref[...] = dx2_ref[...] + r * (dxn - xn * jnp.mean(dxn * xn, axis=-1, keepdims=True))

    rows = pl.BlockSpec((tm, D), lambda i, j: (i, 0))
    return pl.pallas_call(
        body, name="inproj_bwd_x", grid=(ni, NB),
        in_specs=_dz_specs(tm, ni, True) + [pl.BlockSpec((1, D, D), lambda i, j: (j, 0, 0)), rows, rows,
                                             pl.BlockSpec((1, D), lambda i, j: (0, 0)), ANY],
        out_specs=[rows, pl.BlockSpec((1, D), lambda i, j: (0, 0))],
        out_shape=[SDS((n, D), F32), SDS((1, D), F32)],
        scratch_shapes=[pltpu.VMEM((tm, D), F32)],
        compiler_params=_params(48),
    )(dza, dzb, dzm, w_all, x2, dx2, norm_g, after)


def _inproj_bwd_w(dza, dzb, dzm, h_all, g_m):
    n = h_all.shape[0]
    tm = min(n, 1024)
    ni = n // tm

    def body(dza_ref, dzb_ref, dzm_ref, h_ref, gm_hbm, gw_ref, got_w, got_m, stage, send_sems, recv_sems):
        j, i = pl.program_id(0), pl.program_id(1)
        x, y, c = _place()
        sibling = (x, y, 1 - c)

        def send_w(q):
            return pltpu.make_async_remote_copy(
                src_ref=stage.at[q % 2], dst_ref=got_w.at[q], send_sem=send_sems.at[q], recv_sem=recv_sems.at[q],
                device_id=sibling, device_id_type=MESH)

        def send_m(q):
            return pltpu.make_async_remote_copy(
                src_ref=gm_hbm.at[2 * q + (1 - c)], dst_ref=got_m.at[q], send_sem=send_sems.at[4 + q],
                recv_sem=recv_sems.at[4 + q], device_id=sibling, device_id_type=MESH)

        @pl.when((j == 0) & (i == 0))
        def _():
            for q in range(4):
                send_m(q).start()

        @pl.when(i == 0)
        def _():
            gw_ref[...] = jnp.zeros((1, D, D), F32)

        def add(ref):
            gw_ref[0] += lax.dot_general(h_ref[...], ref[0], TN_DIMS, preferred_element_type=F32)

        pl.when(j < 2)(lambda: add(dza_ref))
        pl.when((j >= 2) & (j < 6))(lambda: add(dzb_ref))
        pl.when(j >= 6)(lambda: add(dzm_ref))

        for q in range(4):
            @pl.when((i == ni - 1) & (j == 2 * q + 1 - c))
            def _(q=q):
                if q >= 2:
                    send_w(q - 2).wait_send()
                stage[q % 2] = gw_ref[0].astype(BF16)
                send_w(q).start()

        @pl.when((j == NB - 1) & (i == ni - 1))
        def _():
            for q in (2, 3):
                send_w(q).wait_send()
            for q in range(4):
                send_w(q).wait_recv()
                send_m(q).wait_send()
                send_m(q).wait_recv()

    return pl.pallas_call(
        body, name="inproj_bwd_w", grid=(NB, ni),
        in_specs=_dz_specs(tm, ni, False) + [pl.BlockSpec((tm, D), lambda j, i: (i, 0)), ANY],
        out_specs=[pl.BlockSpec((1, D, D), lambda j, i: (j, 0, 0)), ANY, ANY],
        out_shape=[SDS((NB, D, D), F32), SDS((4, D, D), BF16), SDS((4,) + g_m.shape[1:], F32)],
        scratch_shapes=[pltpu.VMEM((2, D, D), BF16), pltpu.SemaphoreType.DMA((8,)), pltpu.SemaphoreType.DMA((8,))],
        compiler_params=_params(48),
    )(dza, dzb, dzm, h_all, g_m)


def _adamw(w, g, m, v):
    rows, cols = w.shape
    tr = _row_tile(rows)

    spec = pl.BlockSpec((tr, cols), lambda i: (i, 0))
    return pl.pallas_call(
        functools.partial(_adam_refs), name="adamw", grid=(rows // tr,), in_specs=[spec] * 4, out_specs=[spec] * 3,
        out_shape=[SDS((rows, cols), F32)] * 3, compiler_params=_params(32),
    )(w, g, m, v)


def _adam_refs(w_ref, g_ref, m_ref, v_ref, d_ref, nm_ref, nv_ref):
    gv = g_ref[...]
    nm = ADAM_B1 * m_ref[...] + (1.0 - ADAM_B1) * gv
    nv = ADAM_B2 * v_ref[...] + (1.0 - ADAM_B2) * (gv * gv)
    m_hat = nm / (1.0 - ADAM_B1 ** ADAM_STEP)
    v_hat = nv / (1.0 - ADAM_B2 ** ADAM_STEP)
    d_ref[...] = -ADAM_LR * (m_hat / (jnp.sqrt(v_hat) + ADAM_EPS) + ADAM_WD * w_ref[...])
    nm_ref[...] = nm
    nv_ref[...] = nv


def _adamw_small(ws, gs, ms, vs):
    k = len(ws)

    def body(*refs):
        for i in range(k):
            _adam_refs(*[refs[part * k + i] for part in range(7)])

    shapes = [SDS(w.shape, F32) for w in ws]
    out = pl.pallas_call(body, name="adamw_small", out_shape=shapes * 3, compiler_params=_params(32))(*ws, *gs, *ms, *vs)
    return out[:k], out[k:2 * k], out[2 * k:]


def _allgather(blocks, dtypes, name):
    na = len(blocks)

    def body(*refs):
        ins, outs, stages = refs[:na], refs[na:2 * na], refs[2 * na:3 * na]
        send_sems, recv_sems, local_sems = refs[3 * na:]
        x, y, c = _place()
        me, sibling = (x, y, c), (x, y, 1 - c)
        chips = [(1 - x, y), (x, 1 - y), (1 - x, 1 - y)]
        blk = lambda p: 4 * p[0] + 2 * p[1] + p[2]

        def copy(a, k, block, to, src=None):
            return pltpu.make_async_remote_copy(
                src_ref=outs[a].at[blk(block)] if src is None else src, dst_ref=outs[a].at[blk(block)],
                send_sem=send_sems.at[7 * a + k], recv_sem=recv_sems.at[7 * a + k],
                device_id=to, device_id_type=MESH)

        mine, first, passed = [], [], []
        for a in range(na):
            stages[a][...] = ins[a][...].astype(dtypes[a])
            mine.append(pltpu.make_async_copy(stages[a], outs[a].at[blk(me)], local_sems.at[a]))
            mine[-1].start()
            first.append(copy(a, 0, me, sibling, src=stages[a]))
            first += [copy(a, 1 + j, me, (*chip, c), src=stages[a]) for j, chip in enumerate(chips)]
        for cp in first:
            cp.start()
        for j, chip in enumerate(chips):
            for a in range(na):
                copy(a, 1 + j, (*chip, c), me).wait_recv()
                passed.append(copy(a, 4 + j, (*chip, c), sibling))
                passed[-1].start()
        for a in range(na):
            copy(a, 0, sibling, me).wait_recv()
            for j, chip in enumerate(chips):
                copy(a, 4 + j, (*chip, 1 - c), me).wait_recv()
        for cp in first + passed:
            cp.wait_send()
        for cp in mine:
            cp.wait()

    return pl.pallas_call(
        body, name=name,
        in_specs=[pl.BlockSpec(memory_space=pltpu.VMEM)] * na, out_specs=[ANY] * na,
        out_shape=[SDS((NB,) + b.shape, dt) for b, dt in zip(blocks, dtypes)],
        scratch_shapes=[pltpu.VMEM(b.shape, dt) for b, dt in zip(blocks, dtypes)]
        + [pltpu.SemaphoreType.DMA((7 * na,)), pltpu.SemaphoreType.DMA((7 * na,)), pltpu.SemaphoreType.DMA((na,))],
        compiler_params=_params(40),
    )(*blocks)


HBM = pl.BlockSpec(memory_space=pltpu.HBM)
SEMS = pl.BlockSpec(memory_space=pltpu.SEMAPHORE)
EFFECT = pltpu.SideEffectType.DATAFLOW_SIDE_EFFECTING


def _chip_copies(srcs, lands, send_sems, recv_sems):
    x, y, c = _place()
    return [pltpu.make_async_remote_copy(
        src_ref=srcs[a].at[slot], dst_ref=lands[a].at[slot],
        send_sem=send_sems.at[3 * a + slot], recv_sem=recv_sems.at[3 * a + slot],
        device_id=(px, py, c), device_id_type=MESH)
        for a in range(len(srcs)) for slot, (px, py) in enumerate(_other_chips(x, y))]


def _split_start(name, copies, per_array, srcs, lands, after=None):
    na = len(srcs)

    def body(*refs):
        send_sems, recv_sems = refs[-2 * na - 3], refs[-2 * na - 2]
        for cp in copies(refs[:na], refs[na:2 * na], send_sems, recv_sems):
            cp.start()
        refs[-1][...] = jnp.zeros_like(refs[-1])

    hbm = lambda a: pltpu.HBM(a.shape, a.dtype)
    pin = lambda a: pltpu.with_memory_space_constraint(a, pltpu.HBM)
    out = pl.pallas_call(
        body, name=name,
        out_shape=(pltpu.SemaphoreType.DMA((per_array * na,)), pltpu.SemaphoreType.DMA((per_array * na,)),
                   *[hbm(a) for a in srcs], *[hbm(a) for a in lands], SDS((8, BD), F32)),
        in_specs=[HBM] * (2 * na) + ([] if after is None else [ANY]),
        out_specs=(SEMS, SEMS, *[HBM] * (2 * na), pl.BlockSpec(memory_space=pltpu.VMEM)),
        input_output_aliases={i: 2 + i for i in range(2 * na)},
        compiler_params=pltpu.CompilerParams(has_side_effects=EFFECT),
    )(*[pin(a) for a in srcs], *[pin(a) for a in lands], *([] if after is None else [after]))
    return out[0], out[1], out[2:2 + na], out[2 + na:2 + 2 * na], out[-1]


def _split_wait(name, copies, started, after):
    send_sems, recv_sems, srcs, lands, _ = started
    na = len(srcs)

    def body(*refs):
        waits = copies(refs[:na], refs[na:2 * na], refs[2 * na], refs[2 * na + 1])
        for cp in waits:
            cp.wait_send()
        for cp in waits:
            cp.wait_recv()

    hbm = lambda a: pltpu.HBM(a.shape, a.dtype)
    out = pl.pallas_call(
        body, name=name,
        out_shape=(*[hbm(a) for a in srcs], *[hbm(a) for a in lands]),
        in_specs=[HBM] * (2 * na) + [SEMS, SEMS, ANY],
        out_specs=tuple([HBM] * (2 * na)),
        input_output_aliases={i: i for i in range(2 * na)},
        compiler_params=pltpu.CompilerParams(has_side_effects=EFFECT),
    )(*srcs, *lands, send_sems, recv_sems, after)
    return out[na:]


def _add_sibling(place, g, a_in):
    _, r, cols = g.shape
    tr = _row_tile(r)

    def chip(k, pr):
        qx = pr[0] if k in (1, 3) else 1 - pr[0]
        qy = pr[1] if k in (0, 3) else 1 - pr[1]
        return 2 * qx + qy

    def body(place_ref, *refs):
        g_refs, a_refs, (out_ref, own_ref) = refs[0:4], refs[4:8], refs[8:10]
        for k in range(3):
            out_ref[k] = (g_refs[k][0] + a_refs[k][0].astype(F32)).astype(BF16)
        own_ref[...] = g_refs[3][0] + a_refs[3][0].astype(F32)

    mine = lambda k: pl.BlockSpec((1, tr, cols), lambda i, pr: (2 * chip(k, pr) + pr[2], i, 0))
    theirs = lambda k: pl.BlockSpec((1, tr, cols), lambda i, pr: (chip(k, pr), i, 0))
    return pl.pallas_call(
        body, name="add_sibling",
        grid_spec=pltpu.PrefetchScalarGridSpec(
            num_scalar_prefetch=1, grid=(r // tr,),
            in_specs=[mine(k) for k in range(4)] + [theirs(k) for k in range(4)],
            out_specs=[pl.BlockSpec((3, tr, cols), lambda i, pr: (0, i, 0)),
                       pl.BlockSpec((tr, cols), lambda i, pr: (i, 0))]),
        out_shape=[SDS((3, r, cols), BF16), SDS((r, cols), F32)], compiler_params=_params(48),
    )(place, *[g] * 4, *[a_in] * 4)


def _add_chips(own, b_in):
    r, cols = own.shape
    tr = _row_tile(r)

    def body(p_ref, b0_ref, b1_ref, b2_ref, o_ref):
        o_ref[...] = ((p_ref[...] + b0_ref[0].astype(F32)) + b1_ref[0].astype(F32)) + b2_ref[0].astype(F32)

    slot = lambda k: pl.BlockSpec((1, tr, cols), lambda i: (k, i, 0))
    spec = pl.BlockSpec((tr, cols), lambda i: (i, 0))
    return pl.pallas_call(
        body, name="add_chips", grid=(r // tr,), in_specs=[spec, slot(0), slot(1), slot(2)], out_specs=spec,
        out_shape=SDS((r, cols), F32), compiler_params=_params(32),
    )(own, b_in, b_in, b_in)


VEC_NAMES = ("b_merge", "conv_b", "rg_bx", "rg_ba", "rg_lambda", "hg_lb_logits", "hg_norm_g", "final_norm_g")
REP_NAMES = ("rg_wx", "rg_wa", "norm_g") + VEC_NAMES
SMALL_AT = 3 * BD
SMALL_ROWS = 48
MID_ROWS = 448


def _sum_blocks(parts):
    def body(p_ref, o_ref):
        acc = p_ref[0]
        for k in range(1, NB):
            acc = acc + p_ref[k]
        o_ref[...] = acc

    return pl.pallas_call(body, name="sum_blocks", out_shape=SDS(parts.shape[1:], F32))(parts)


def _pack_rows(arrays, width, row_multiple=8):
    flat = jnp.concatenate([a.reshape(-1) for a in arrays])
    rows = -(-flat.shape[0] // width)
    rows = -(-rows // row_multiple) * row_multiple
    return jnp.pad(flat, (0, rows * width - flat.shape[0])).reshape(rows, width)


def _unpack(flat, like):
    out, off = [], 0
    for a in like:
        out.append(flat[off:off + a.size].reshape(a.shape))
        off += a.size
    return out


def kernel(x, w_in, b_merge, conv_w, conv_b, rg_wx, rg_bx, rg_wa, rg_ba, rg_lambda, hg_lb_logits, hg_norm_g, proj_a, proj_b, w_out, norm_g, final_norm_g, loss_target, m_w_in, m_b_merge, m_conv_w, m_conv_b, m_rg_wx, m_rg_bx, m_rg_wa, m_rg_ba, m_rg_lambda, m_hg_lb_logits, m_hg_norm_g, m_proj_a, m_proj_b, m_w_out, m_norm_g, m_final_norm_g, v_w_in, v_b_merge, v_conv_w, v_conv_b, v_rg_wx, v_rg_bx, v_rg_wa, v_rg_ba, v_rg_lambda, v_hg_lb_logits, v_hg_norm_g, v_proj_a, v_proj_b, v_w_out, v_norm_g, v_final_norm_g):
    weights = dict(w_in=w_in, b_merge=b_merge, conv_w=conv_w, conv_b=conv_b, rg_wx=rg_wx, rg_bx=rg_bx, rg_wa=rg_wa,
                   rg_ba=rg_ba, rg_lambda=rg_lambda, hg_lb_logits=hg_lb_logits, hg_norm_g=hg_norm_g, proj_a=proj_a,
                   proj_b=proj_b, w_out=w_out, norm_g=norm_g, final_norm_g=final_norm_g)
    mom1 = dict(w_in=m_w_in, b_merge=m_b_merge, conv_w=m_conv_w, conv_b=m_conv_b, rg_wx=m_rg_wx, rg_bx=m_rg_bx,
                rg_wa=m_rg_wa, rg_ba=m_rg_ba, rg_lambda=m_rg_lambda, hg_lb_logits=m_hg_lb_logits,
                hg_norm_g=m_hg_norm_g, proj_a=m_proj_a, proj_b=m_proj_b, w_out=m_w_out, norm_g=m_norm_g,
                final_norm_g=m_final_norm_g)
    mom2 = dict(w_in=v_w_in, b_merge=v_b_merge, conv_w=v_conv_w, conv_b=v_conv_b, rg_wx=v_rg_wx, rg_bx=v_rg_bx,
                rg_wa=v_rg_wa, rg_ba=v_rg_ba, rg_lambda=v_rg_lambda, hg_lb_logits=v_hg_lb_logits,
                hg_norm_g=v_hg_norm_g, proj_a=v_proj_a, proj_b=v_proj_b, w_out=v_w_out, norm_g=v_norm_g,
                final_norm_g=v_final_norm_g)
    order = list(weights)
    nb, s_len, _ = x.shape
    n = nb * s_len
    px, py, pc = _place()
    place = jnp.stack([px, py, pc]).astype(jnp.int32)

    x2 = x.reshape(n, D)
    cw_blk = jnp.pad(conv_w[0], ((0, 4), (0, 0)))
    order_ids = jnp.stack([_block_id(p) for p in _arrival_order(px, py, pc)]).astype(jnp.int32)
    z, h_all, w_all, pa_all, pb_all, wo_all, cw_all = _gather_inproj(
        order_ids, x2, norm_g, [w_in[0], proj_a[0], proj_b[0], w_out[0], cw_blk], [BF16, BF16, BF16, BF16, F32])
    pa_full, pb_full, wo_full = (a.reshape(D, D) for a in (pa_all, pb_all, wo_all))
    cw8 = cw_all.transpose(1, 0, 2).reshape(8, D)
    wx_b, wa_b = rg_wx[0].astype(BF16), rg_wa[0].astype(BF16)
    cb, bx, ba = conv_b, rg_bx.reshape(1, D), rg_ba.reshape(1, D)
    fin_g = final_norm_g.reshape(1, D)

    hlru, ya = _lru_fwd(z, cw8, cb, wx_b, wa_b, bx, ba, rg_lambda, nb, s_len)
    o_all, yb, st_all = _hgrn_fwd(z, hg_lb_logits, hg_norm_g, nb, s_len)

    (dx2, dya, dyb, dzm, loss_acc, g_fin, g_bm, g_mid) = _mid(
        ya, yb, z, b_merge, x2, loss_target.reshape(n, D), fin_g, pa_full, pb_full, wo_full)
    dzb, g_lg, g_hg = _hgrn_bwd(z, o_all, st_all, dyb, hg_lb_logits, hg_norm_g, nb, s_len)
    dza, g_cw8, g_cb, g_wx, g_wa, g_bx, g_ba, g_lam = _lru_bwd(
        z, hlru, dya, cw8, cb, wx_b, wa_b, bx, ba, rg_lambda, nb, s_len)

    part = dict(b_merge=g_bm, conv_b=g_cb, rg_bx=g_bx, rg_ba=g_ba, rg_lambda=g_lam, hg_lb_logits=g_lg,
                hg_norm_g=g_hg, final_norm_g=g_fin)
    vec = _pack_rows([part[k] for k in VEC_NAMES], BD)
    vec = jnp.pad(vec, ((0, 16 * NB - vec.shape[0]), (0, 0))).reshape(NB, 2, D)
    rows8 = lambda a: jnp.pad(a, ((0, 0), (0, 8 - a.shape[1]), (0, 0)))
    small = jnp.concatenate([g_wx.reshape(NB, 16, D), g_wa.reshape(NB, 16, D),
                             rows8(g_cw8.reshape(8, NB, BD).transpose(1, 0, 2).reshape(NB, 1, D)), rows8(vec),
                             jnp.zeros((NB, MID_ROWS - SMALL_AT - SMALL_ROWS, D), F32)], axis=1)
    g_m = lax.dynamic_update_slice(g_mid, small, (0, SMALL_AT, 0))
    g_w, w_from_sibling, m_from_sibling = _inproj_bwd_w(dza, dzb, dzm, h_all, g_m)
    w_out_bf, w_own = _add_sibling(place, g_w, w_from_sibling)
    m_out_bf, m_own = _add_sibling(place, g_m, m_from_sibling)
    outgoing = [w_out_bf, m_out_bf]
    chip_sums = _split_start("rs_chips_start", _chip_copies, 3, outgoing, [lax.empty(a.shape, a.dtype) for a in outgoing])
    grad_x, g_ng = _inproj_bwd_x(dza, dzb, dzm, w_all, x2, dx2, norm_g, chip_sums[-1])
    from_chips = _split_wait("rs_chips_wait", _chip_copies, chip_sums, grad_x)
    r_w = _add_chips(w_own, from_chips[0])
    r_m = _add_chips(m_own, from_chips[1])
    row = lax.broadcasted_iota(jnp.int32, (8, D), 0)
    mine = jnp.where(row == 0, g_ng, jnp.where(row == 1, loss_acc[0:1, 0:1], 0.0))
    tail = jnp.concatenate([r_m[SMALL_AT:SMALL_AT + SMALL_ROWS], mine], axis=0)
    (tail_all,) = _allgather([tail], [F32], "gather_small_grads")
    summed = _sum_blocks(tail_all[:, SMALL_ROWS:SMALL_ROWS + 8])

    grads = dict(w_in=r_w.reshape(1, D, D),
                 proj_a=r_m[0:BD].reshape(1, BD, D), proj_b=r_m[BD:2 * BD].reshape(1, BD, D),
                 w_out=r_m[2 * BD:3 * BD].reshape(1, BD, D),
                 conv_w=r_m[SMALL_AT + 32].reshape(8, BD)[0:4].reshape(1, 4, BD),
                 rg_wx=tail_all[:, 0:16].reshape(1, NB, BD, BD), rg_wa=tail_all[:, 16:32].reshape(1, NB, BD, BD),
                 norm_g=summed[0:1])
    vec_all = tail_all[:, 40:42].reshape(-1)
    for k, gk in zip(VEC_NAMES, _unpack(vec_all, [weights[k] for k in VEC_NAMES])):
        grads[k] = gk

    delta, new_m, new_v = {}, {}, {}
    for k in ("w_in", "proj_a", "proj_b", "w_out"):
        shp = weights[k].shape
        two = lambda a: a.reshape(shp[1], shp[2])
        d_k, m_k, v_k = _adamw(two(weights[k]), two(grads[k]), two(mom1[k]), two(mom2[k]))
        delta[k], new_m[k], new_v[k] = d_k.reshape(shp), m_k.reshape(shp), v_k.reshape(shp)
    rep = list(REP_NAMES) + ["conv_w"]
    flat2 = lambda a: a.reshape(-1, a.shape[-1])
    outs = _adamw_small(*[[flat2(t[k]) for k in rep] for t in (weights, grads, mom1, mom2)])
    for tgt, arrays in zip((delta, new_m, new_v), outs):
        for k, a in zip(rep, arrays):
            tgt[k] = a.reshape(weights[k].shape)

    return (summed[1, 0],grad_x.reshape(x.shape), *[grads[k] for k in order], *[delta[k] for k in order],
            *[new_m[k] for k in order], *[new_v[k] for k in order])
```

```python
import functools

import jax
import jax.numpy as jnp
from jax import lax
from jax.experimental import pallas as pl
from jax.experimental.pallas import tpu as pltpu

F32 = jnp.float32
BF16 = jnp.bfloat16
SDS = jax.ShapeDtypeStruct
MESH = pl.DeviceIdType.MESH
ANY = pl.BlockSpec(memory_space=pl.ANY)

D = 1024
NB = 8
BD = D // NB
CHUNK = 64
EPS = 1e-6
LRU_C = 8.0
HG_SCALE = BD ** -0.5
ADAM_LR, ADAM_B1, ADAM_B2, ADAM_EPS, ADAM_WD, ADAM_STEP = 0.001, 0.9, 0.999, 1e-08, 0.01, 10

NT_DIMS = (((1,), (1,)), ((), ()))
TN_DIMS = (((0,), (0,)), ((), ()))


def _params(vmem_mib):
    return pltpu.CompilerParams(vmem_limit_bytes=vmem_mib << 20)


def _row_tile(rows, most=256):
    assert rows % 8 == 0
    return max(t for t in range(8, min(rows, most) + 1, 8) if rows % t == 0)


def _sigmoid(v):
    return 0.5 * (jnp.tanh(0.5 * v) + 1.0)


def _groups(v):
    return v.reshape(v.shape[0] // 8, 8, v.shape[1])


def _softplus_neg(lam):
    t = -lam
    e = jnp.exp(-jnp.abs(t))
    w = 1.0 + e
    d = w - 1.0
    l1p = jnp.where(d == 0.0, e, jnp.log(w) * (e / jnp.where(d == 0.0, 1.0, d)))
    return jnp.maximum(t, 0.0) + l1p


def _place():
    return lax.axis_index("x"), lax.axis_index("y"), lax.axis_index("c")


def _other_chips(x, y):
    return [(1 - x, y), (x, 1 - y), (1 - x, 1 - y)]


def _block_id(p):
    return 4 * p[0] + 2 * p[1] + p[2]


def _arrival_order(x, y, c):
    near, far, diag = _other_chips(x, y)
    return [(x, y, c), (x, y, 1 - c), (*near, c), (*far, c), (*near, 1 - c), (*far, 1 - c), (*diag, c), (*diag, 1 - c)]


def _gather_inproj(order_ids, x2, norm_g, blocks, dtypes):
    na = len(blocks)
    n = x2.shape[0]
    tm = min(n, 1024)
    ni = n // tm

    def body(order_ref, x_ref, g_ref, *refs):
        ins, (z_ref, h_ref), outs = refs[:na], refs[na:na + 2], refs[na + 2:2 * na + 2]
        stages = refs[2 * na + 2:3 * na + 2]
        h_full, wbuf, send_sems, recv_sems, local_sems, wsems, hsem = refs[3 * na + 2:]
        j, i = pl.program_id(0), pl.program_id(1)
        x, y, c = _place()
        me, sibling = (x, y, c), (x, y, 1 - c)
        chips = _other_chips(x, y)
        small = range(1, na)

        def copy(a, k, block, to, src=None):
            return pltpu.make_async_remote_copy(
                src_ref=outs[a].at[_block_id(block)] if src is None else src, dst_ref=outs[a].at[_block_id(block)],
                send_sem=send_sems.at[7 * a + k], recv_sem=recv_sems.at[7 * a + k],
                device_id=to, device_id_type=MESH)

        def local(a):
            return pltpu.make_async_copy(stages[a], outs[a].at[_block_id(me)], local_sems.at[a])

        def landed(a, slot):
            copy(a, 1 + slot, (*chips[slot], c), me).wait_recv()
            copy(a, 4 + slot, (*chips[slot], c), sibling).start()
            if slot < 2:
                @pl.when(c == slot)
                def _():
                    copy(a, 3, (*chips[slot], c), (*chips[1 - slot], c)).start()

        def diagonal_and_small():
            landed(0, 2)
            for a in small:
                landed(a, 0)
                landed(a, 1)

        def passed_on(a, slot):
            copy(a, 4 + slot, (*chips[slot], 1 - c), me).wait_recv()

        @pl.when((j == 0) & (i == 0))
        def _():
            for a in range(na):
                stages[a][...] = ins[a][...].astype(dtypes[a])
                local(a).start()
            for a in range(na):
                copy(a, 0, me, sibling, src=stages[a]).start()
                for slot, chip in enumerate(chips[:2]):
                    copy(a, 1 + slot, me, (*chip, c), src=stages[a]).start()

        @pl.when(j == 0)
        def _():
            xv = x_ref[...]
            r = lax.rsqrt(jnp.mean(xv * xv, axis=-1, keepdims=True) + EPS)
            hb = ((xv * r) * g_ref[...]).astype(BF16)
            h_full[pl.ds(pl.multiple_of(i * tm, tm), tm), :] = hb

        save_h = pltpu.make_async_copy(h_full, h_ref, hsem)
        pl.when((j == 0) & (i == ni - 1))(save_h.start)

        steps = [
            lambda: local(0).wait(),
            lambda: copy(0, 0, sibling, me).wait_recv(),
            lambda: landed(0, 0),
            lambda: landed(0, 1),
            lambda: passed_on(0, 0),
            lambda: passed_on(0, 1),
            diagonal_and_small,
            lambda: passed_on(0, 2),
        ]
        def w_load(k):
            return pltpu.make_async_copy(outs[0].at[order_ref[k]], wbuf.at[k % 2], wsems.at[k % 2])

        for k, step in enumerate(steps):
            @pl.when((j == 0) & (i == 0) if k == 0 else (j == k - 1) & (i == ni - 1))
            def _(k=k, step=step):
                step()
                w_load(k).start()

        pl.when(i == 0)(lambda: w_load(j).wait())
        z_ref[0] = jnp.dot(h_full[pl.ds(pl.multiple_of(i * tm, tm), tm), :], wbuf[j % 2], preferred_element_type=F32)

        @pl.when((j == NB - 1) & (i == ni - 1))
        def _():
            save_h.wait()
            for a in small:
                landed(a, 2)
            for a in small:
                local(a).wait()
                copy(a, 0, sibling, me).wait_recv()
                for slot in range(3):
                    passed_on(a, slot)
            for a in range(na):
                copy(a, 0, me, sibling, src=stages[a]).wait_send()
                for slot, chip in enumerate(chips):
                    copy(a, 1 + slot, me, (*chip, c), src=stages[a]).wait_send()
                    copy(a, 4 + slot, (*chip, c), sibling).wait_send()

    rows_once = lambda j, i, order: (jnp.where(j == 0, i, ni - 1), 0)
    vmem = pl.BlockSpec(memory_space=pltpu.VMEM)
    return pl.pallas_call(
        body, name="gather_inproj",
        grid_spec=pltpu.PrefetchScalarGridSpec(
            num_scalar_prefetch=1, grid=(NB, ni),
            in_specs=[pl.BlockSpec((tm, D), rows_once), pl.BlockSpec((1, D), lambda j, i, order: (0, 0))] + [vmem] * na,
            out_specs=[pl.BlockSpec((1, tm, D), lambda j, i, order: (order[j], i, 0)), ANY] + [ANY] * na,
            scratch_shapes=[pltpu.VMEM(b.shape, dt) for b, dt in zip(blocks, dtypes)]
            + [pltpu.VMEM((n, D), BF16), pltpu.VMEM((2, D, D), BF16),
               pltpu.SemaphoreType.DMA((7 * na,)), pltpu.SemaphoreType.DMA((7 * na,)),
               pltpu.SemaphoreType.DMA((na,)), pltpu.SemaphoreType.DMA((2,)), pltpu.SemaphoreType.DMA(())]),
        out_shape=[SDS((NB, n, D), F32), SDS((n, D), BF16)] + [SDS((NB,) + b.shape, dt) for b, dt in zip(blocks, dtypes)],
        compiler_params=_params(56),
    )(order_ids, x2, norm_g, *blocks)


LRU_T = 256


def _shifted(groups, shifts):
    row = lax.broadcasted_iota(jnp.int32, (groups.shape[0] - 1,) + groups.shape[1:], 1)
    out = []
    for s in shifts:
        y = pltpu.roll(groups, s % 8, 1)
        moved = jnp.where(row >= s, y[1:], y[:-1]) if s > 0 else jnp.where(row < 8 + s, y[:-1], y[1:])
        out.append(moved.reshape(-1, groups.shape[2]))
    return out


def _conv(taps, cw, cb):
    acc = taps[0] * cw[0:1, :] + taps[1] * cw[1:2, :]
    acc = acc + taps[2] * cw[2:3, :]
    acc = acc + taps[3] * cw[3:4, :]
    return cb + acc


def _lru_gates(xa, wx_ref, wa_ref, bx, ba, lam):
    xab = xa.astype(BF16)
    pis, prs = [], []
    for h in range(NB):
        xs = xab[:, h * BD:(h + 1) * BD]
        pis.append(jnp.dot(xs, wx_ref[h], preferred_element_type=F32))
        prs.append(jnp.dot(xs, wa_ref[h], preferred_element_type=F32))
    gi = _sigmoid(jnp.concatenate(pis, axis=1) + bx)
    gr = _sigmoid(jnp.concatenate(prs, axis=1) + ba)
    sp = _softplus_neg(lam)
    log_a = (-LRU_C * gr) * sp
    a = jnp.exp(log_a)
    mult = jnp.sqrt(-jnp.tanh(log_a) * (a * a + 1.0))
    return xab, gi, gr, sp, a, mult


def _lru_fwd(z, cw8, cb, wx, wa, bx, ba, lam, nb, s_len):
    n = nb * s_len
    t = LRU_T
    ns = s_len // t

    def body(xp_ref, ga_ref, cw_ref, cb_ref, wx_ref, wa_ref, bx_ref, ba_ref, lam_ref,
             h_ref, ya_ref, ext, a_s, u_s, carry):
        @pl.when(pl.program_id(1) == 0)
        def _():
            ext[0:8, :] = jnp.zeros((8, D), F32)
            carry[...] = jnp.zeros((8, D), F32)

        xp = xp_ref[0]
        ext[8:8 + t, :] = xp
        xa = _conv(_shifted(_groups(ext[...]), (3, 2, 1)) + [xp], cw_ref[...], cb_ref[...])
        ext[0:8, :] = xp[t - 8:t, :]
        _, gi, _, _, a, mult = _lru_gates(xa, wx_ref, wa_ref, bx_ref[...], ba_ref[...], lam_ref[...])
        u = (mult * gi) * xa
        a, u = _groups(a), _groups(u)
        row = lax.broadcasted_iota(jnp.int32, a.shape, 1)
        for sh in (1, 2, 4):
            a_sh = pltpu.roll(a, sh, 1)
            u_sh = pltpu.roll(u, sh, 1)
            m = row >= sh
            u = jnp.where(m, a * u_sh + u, u)
            a = jnp.where(m, a * a_sh, a)
        a_s[...] = a.reshape(t, D)
        u_s[...] = u.reshape(t, D)

        def step(g, c):
            r = pl.multiple_of(g * 8, 8)
            hg = u_s[pl.ds(r, 8), :] + a_s[pl.ds(r, 8), :] * c
            h_ref[pl.ds(r, 8), :] = hg
            return hg[7:8, :]

        c_out = lax.fori_loop(0, t // 8, step, carry[0:1, :], unroll=4)
        carry[0:1, :] = c_out
        ga = ga_ref[0]
        ya_ref[...] = (h_ref[...] * (ga * _sigmoid(ga))).astype(BF16)

    row_map = lambda b, s: (b * ns + s, 0)
    rep2 = lambda b, s: (0, 0)
    rep3 = lambda b, s: (0, 0, 0)
    return pl.pallas_call(
        body, name="lru_fwd", grid=(nb, ns),
        in_specs=[pl.BlockSpec((1, t, D), lambda b, s: (0, b * ns + s, 0)),
                  pl.BlockSpec((1, t, D), lambda b, s: (1, b * ns + s, 0)),
                  pl.BlockSpec((8, D), rep2), pl.BlockSpec((1, D), rep2),
                  pl.BlockSpec((NB, BD, BD), rep3), pl.BlockSpec((NB, BD, BD), rep3),
                  pl.BlockSpec((1, D), rep2), pl.BlockSpec((1, D), rep2), pl.BlockSpec((1, D), rep2)],
        out_specs=[pl.BlockSpec((t, D), row_map), pl.BlockSpec((t, D), row_map)],
        out_shape=[SDS((n, D), F32), SDS((n, D), BF16)],
        scratch_shapes=[pltpu.VMEM((t + 8, D), F32), pltpu.VMEM((t, D), F32), pltpu.VMEM((t, D), F32),
                        pltpu.VMEM((8, D), F32)],
        compiler_params=_params(48),
    )(z, z, cw8, cb, wx, wa, bx, ba, lam)


def _lru_bwd(z, h_all, dya, cw8, cb, wx, wa, bx, ba, lam, nb, s_len):
    n = nb * s_len
    t = LRU_T
    ns = s_len // t
    t8 = t // 8

    def body(xp_ref, xph_ref, ga_ref, h_ref, hh_ref, dya_ref, cw_ref, cb_ref, wx_ref, wa_ref, bx_ref, ba_ref,
             lam_ref, dz_ref, gcw_ref, gcb_ref, gwx_ref, gwa_ref, gbx_ref, gba_ref, glam_ref,
             ext, hext, dext, a_s, u_s, dh_s, carry):
        b, s = pl.program_id(0), pl.program_id(1)
        first_tile = s == ns - 1

        @pl.when((b == 0) & (s == 0))
        def _():
            for ref in (gcw_ref, gcb_ref, gwx_ref, gwa_ref, gbx_ref, gba_ref, glam_ref):
                ref[...] = jnp.zeros(ref.shape, F32)

        @pl.when(s == 0)
        def _():
            dext[t:t + 8, :] = jnp.zeros((8, D), F32)
            carry[...] = jnp.zeros((8, D), F32)

        keep = jnp.where(first_tile, 0.0, 1.0)
        xp = xp_ref[0]
        ext[0:8, :] = xph_ref[0] * keep
        ext[8:8 + t, :] = xp
        hext[0:8, :] = hh_ref[...] * keep
        hext[8:8 + t, :] = h_ref[...]
        cw = cw_ref[...]
        lam = lam_ref[...]
        taps = _shifted(_groups(ext[...]), (3, 2, 1)) + [xp]
        xa = _conv(taps, cw, cb_ref[...])
        xab, gi, gr, sp, a, mult = _lru_gates(xa, wx_ref, wa_ref, bx_ref[...], ba_ref[...], lam)
        (h_prev,) = _shifted(_groups(hext[...]), (1,))
        ga = ga_ref[0]
        sg = _sigmoid(ga)
        dya_v = dya_ref[...]
        d_ga = dya_v * h_ref[...] * (sg * (1.0 + ga * (1.0 - sg)))
        g_in = dya_v * (ga * sg)

        (an,) = _shifted(jnp.concatenate([_groups(a), jnp.ones((1, 8, D), F32)], axis=0), (-1,))
        an, u = _groups(an), _groups(g_in)
        row = lax.broadcasted_iota(jnp.int32, an.shape, 1)
        for sh in (1, 2, 4):
            a_sh = pltpu.roll(an, 8 - sh, 1)
            u_sh = pltpu.roll(u, 8 - sh, 1)
            m = row < 8 - sh
            u = jnp.where(m, u + an * u_sh, u)
            an = jnp.where(m, an * a_sh, an)
        a_s[...] = an.reshape(t, D)
        u_s[...] = u.reshape(t, D)

        def step(i, c):
            r = pl.multiple_of((t8 - 1 - i) * 8, 8)
            dg = u_s[pl.ds(r, 8), :] + a_s[pl.ds(r, 8), :] * c
            dh_s[pl.ds(r, 8), :] = dg
            return dg[0:1, :]

        lax.fori_loop(0, t8, step, carry[0:1, :], unroll=4)
        dh = dh_s[...]
        carry[0:1, :] = a[0:1, :] * dh[0:1, :]

        d_a = dh * h_prev
        dux = dh * xa
        d_mult = dux * gi
        d_gi = dux * mult
        d_xa = dh * (mult * gi)
        d_loga = d_a * a - d_mult * ((a * a) / mult)
        d_gr = d_loga * (-LRU_C * sp)
        d_sp = jnp.sum(d_loga * (-LRU_C * gr), axis=0, keepdims=True)
        glam_ref[...] += d_sp * (-_sigmoid(-lam))
        d_pi = d_gi * gi * (1.0 - gi)
        d_pr = d_gr * gr * (1.0 - gr)
        gbx_ref[...] += jnp.sum(d_pi, axis=0, keepdims=True)
        gba_ref[...] += jnp.sum(d_pr, axis=0, keepdims=True)
        dpib = d_pi.astype(BF16)
        dprb = d_pr.astype(BF16)
        back = []
        for h in range(NB):
            cs = slice(h * BD, (h + 1) * BD)
            gwx_ref[h] += lax.dot_general(xab[:, cs], dpib[:, cs], TN_DIMS, preferred_element_type=F32)
            gwa_ref[h] += lax.dot_general(xab[:, cs], dprb[:, cs], TN_DIMS, preferred_element_type=F32)
            back.append(lax.dot_general(dpib[:, cs], wx_ref[h], NT_DIMS, preferred_element_type=F32)
                        + lax.dot_general(dprb[:, cs], wa_ref[h], NT_DIMS, preferred_element_type=F32))
        d_xa = d_xa + jnp.concatenate(back, axis=1)

        dext[0:t, :] = d_xa
        later = _shifted(_groups(dext[...]), (-3, -2, -1))
        d_xp = later[0] * cw[0:1, :] + later[1] * cw[1:2, :]
        d_xp = d_xp + later[2] * cw[2:3, :]
        d_xp = d_xp + d_xa * cw[3:4, :]
        dext[t:t + 8, :] = d_xa[0:8, :]
        gcb_ref[...] += jnp.sum(d_xa, axis=0, keepdims=True)
        for k in range(4):
            gcw_ref[k:k + 1, :] += jnp.sum(d_xa * taps[k], axis=0, keepdims=True)
        dz_ref[0] = d_xp.astype(BF16)
        dz_ref[1] = d_ga.astype(BF16)

    rb = lambda b, s: b * ns + (ns - 1 - s)
    halo = lambda b, s: jnp.maximum(rb(b, s) * t8 - 1, 0)
    rep2 = lambda b, s: (0, 0)
    rep3 = lambda b, s: (0, 0, 0)
    return pl.pallas_call(
        body, name="lru_bwd", grid=(nb, ns),
        in_specs=[pl.BlockSpec((1, t, D), lambda b, s: (0, rb(b, s), 0)),
                  pl.BlockSpec((1, 8, D), lambda b, s: (0, halo(b, s), 0)),
                  pl.BlockSpec((1, t, D), lambda b, s: (1, rb(b, s), 0)),
                  pl.BlockSpec((t, D), lambda b, s: (rb(b, s), 0)),
                  pl.BlockSpec((8, D), lambda b, s: (halo(b, s), 0)),
                  pl.BlockSpec((t, D), lambda b, s: (rb(b, s), 0)),
                  pl.BlockSpec((8, D), rep2), pl.BlockSpec((1, D), rep2),
                  pl.BlockSpec((NB, BD, BD), rep3), pl.BlockSpec((NB, BD, BD), rep3),
                  pl.BlockSpec((1, D), rep2), pl.BlockSpec((1, D), rep2), pl.BlockSpec((1, D), rep2)],
        out_specs=[pl.BlockSpec((2, t, D), lambda b, s: (0, rb(b, s), 0)),
                   pl.BlockSpec((8, D), rep2), pl.BlockSpec((1, D), rep2),
                   pl.BlockSpec((NB, BD, BD), rep3), pl.BlockSpec((NB, BD, BD), rep3),
                   pl.BlockSpec((1, D), rep2), pl.BlockSpec((1, D), rep2), pl.BlockSpec((1, D), rep2)],
        out_shape=[SDS((2, n, D), BF16), SDS((8, D), F32), SDS((1, D), F32),
                   SDS((NB, BD, BD), F32), SDS((NB, BD, BD), F32),
                   SDS((1, D), F32), SDS((1, D), F32), SDS((1, D), F32)],
        scratch_shapes=[pltpu.VMEM((t + 8, D), F32), pltpu.VMEM((t + 8, D), F32), pltpu.VMEM((t + 8, D), F32),
                        pltpu.VMEM((t, D), F32), pltpu.VMEM((t, D), F32), pltpu.VMEM((t, D), F32),
                        pltpu.VMEM((8, D), F32)],
        compiler_params=_params(56),
    )(z, z, z, h_all, h_all, dya, cw8, cb, wx, wa, bx, ba, lam)


HG_T = 512
HG_NC = HG_T // CHUNK
BNT_DIMS = (((2,), (2,)), ((0,), (0,)))
BNN_DIMS = (((2,), (1,)), ((0,), (0,)))
BTN_DIMS = (((1,), (1,)), ((0,), (0,)))


def _lower_bound(lg):
    m = jnp.max(lg, axis=0, keepdims=True)
    e = jnp.exp(lg - m)
    return e[0:1, :] / jnp.sum(e, axis=0, keepdims=True)


def _tri(upper):
    r = lax.broadcasted_iota(jnp.int32, (HG_NC, CHUNK, CHUNK), 1)
    c = lax.broadcasted_iota(jnp.int32, (HG_NC, CHUNK, CHUNK), 2)
    return (c >= r) if upper else (r >= c)


def _bdot(a, b, dims):
    return lax.dot_general(a, b, dims, preferred_element_type=F32)


def _tri_sums(upper, a):
    tri = _tri(upper).astype(BF16)
    a1 = a.astype(BF16)
    r1 = a - a1.astype(F32)
    a2 = r1.astype(BF16)
    a3 = (r1 - a2.astype(F32)).astype(BF16)
    return _bdot(tri, a1, BNN_DIMS) + (_bdot(tri, a2, BNN_DIMS) + _bdot(tri, a3, BNN_DIMS))


def _chunks(a):
    return a.reshape(HG_NC, CHUNK, BD)


def _hg_tile(q, fp, lb):
    q, fp = _chunks(q), _chunks(fp)
    sig = _sigmoid(fp)
    f = lb + (1.0 - lb) * sig
    log_f = jnp.log(f)
    k = 1.0 - f
    b = _tri_sums(False, log_f)
    b_mid = b[:, CHUNK // 2:CHUNK // 2 + 1, :]
    b_last = b[:, CHUNK - 1:CHUNK, :]
    sq = _sigmoid(q)
    qh = q * sq
    e_qi = jnp.exp(b - b_mid)
    e_ki = jnp.exp(b_mid - b)
    e_qs = jnp.exp(b)
    e_ks = jnp.exp(b_last - b)
    dc = jnp.exp(b_last)
    q_in = (qh * e_qi) * HG_SCALE
    k_in = k * e_ki
    q_st = (qh * e_qs) * HG_SCALE
    k_st = k * e_ks
    att = _bdot(q_in.astype(BF16), k_in.astype(BF16), BNT_DIMS)
    att = jnp.where(_tri(False), att, 0.0)
    return dict(q=q, sig=sig, f=f, k=k, sq=sq, e_qi=e_qi, e_ki=e_ki, e_qs=e_qs, e_ks=e_ks, dc=dc,
                q_in=q_in, k_in=k_in, q_st=q_st, k_st=k_st, att=att)


def _hgrn_fwd(z, lb_logits, hg_g, nb, s_len):
    n = nb * s_len
    t = HG_T
    ns = s_len // t
    nchunk = s_len // CHUNK

    def body(q_ref, f_ref, v_ref, gb_ref, lg_ref, g_ref, o_ref, yb_ref, st_ref, st):
        @pl.when(pl.program_id(1) == 0)
        def _():
            st[...] = jnp.zeros((NB, BD, BD), F32)

        def head(h, carry):
            cols = pl.ds(pl.multiple_of(h * BD, BD), BD)
            lb = _lower_bound(lg_ref[:, cols])
            ck = _hg_tile(q_ref[0, :, cols], f_ref[0, :, cols], lb)
            vb = _chunks(v_ref[0, :, cols]).astype(BF16)
            kv = _bdot(vb, ck["k_st"].astype(BF16), BTN_DIMS)
            states = [st[h]]
            for c in range(HG_NC):
                states.append(states[c] * ck["dc"][c] + kv[c])
            st[h] = states[HG_NC]
            s_in = jnp.stack(states[:HG_NC], axis=0)
            st_ref[h] = s_in
            o = (_bdot(ck["att"].astype(BF16), vb, BNN_DIMS)
                 + _bdot(ck["q_st"].astype(BF16), s_in.astype(BF16), BNT_DIMS))
            o_ref[:, cols] = o.reshape(t, BD)
            r = lax.rsqrt(jnp.mean(o * o, axis=-1, keepdims=True) + EPS)
            gb = _chunks(gb_ref[0, :, cols])
            yb_ref[:, cols] = (((o * r) * g_ref[...]) * (gb * _sigmoid(gb))).astype(BF16).reshape(t, BD)
            return carry

        lax.fori_loop(0, NB, head, 0, unroll=2)

    seg = lambda j: pl.BlockSpec((1, t, D), lambda b, s: (j, b * ns + s, 0))
    tile = pl.BlockSpec((t, D), lambda b, s: (b * ns + s, 0))
    return pl.pallas_call(
        body, name="hgrn_fwd", grid=(nb, ns),
        in_specs=[seg(2), seg(3), seg(4), seg(5),
                  pl.BlockSpec((2, D), lambda b, s: (0, 0)), pl.BlockSpec((1, BD), lambda b, s: (0, 0))],
        out_specs=[tile, tile, pl.BlockSpec((NB, HG_NC, BD, BD), lambda b, s: (b, s, 0, 0))],
        out_shape=[SDS((n, D), F32), SDS((n, D), BF16), SDS((nb * NB, nchunk, BD, BD), F32)],
        scratch_shapes=[pltpu.VMEM((NB, BD, BD), F32)],
        compiler_params=_params(56),
    )(z, z, z, z, lb_logits, hg_g)


def _hgrn_bwd(z, o_all, st_all, dyb, lb_logits, hg_g, nb, s_len):
    n = nb * s_len
    t = HG_T
    ns = s_len // t

    def body(q_ref, f_ref, v_ref, gb_ref, o_ref, st_ref, dyb_ref, lg_ref, g_ref,
             dz_ref, glg_ref, ghg_ref, dst, dlb):
        b, s = pl.program_id(0), pl.program_id(1)

        @pl.when((b == 0) & (s == 0))
        def _():
            ghg_ref[...] = jnp.zeros((1, BD), F32)
            dlb[...] = jnp.zeros((8, D), F32)

        @pl.when(s == 0)
        def _():
            dst[...] = jnp.zeros((NB, BD, BD), F32)

        g = g_ref[...]

        def head(h, carry):
            cols = pl.ds(pl.multiple_of(h * BD, BD), BD)
            lb = _lower_bound(lg_ref[:, cols])
            ck = _hg_tile(q_ref[0, :, cols], f_ref[0, :, cols], lb)
            q = ck["q"]
            vb = _chunks(v_ref[0, :, cols]).astype(BF16)
            gb = _chunks(gb_ref[0, :, cols])
            o = _chunks(o_ref[:, cols])
            dyb_v = _chunks(dyb_ref[:, cols])
            s_in = st_ref[h]

            sgb = _sigmoid(gb)
            r = lax.rsqrt(jnp.mean(o * o, axis=-1, keepdims=True) + EPS)
            ohat = o * r
            d_on = dyb_v * (gb * sgb)
            d_gb = dyb_v * (ohat * g) * (sgb * (1.0 + gb * (1.0 - sgb)))
            ghg_ref[...] += jnp.sum(jnp.sum(d_on * ohat, axis=1), axis=0, keepdims=True)
            tt = d_on * g
            d_o = r * (tt - ohat * jnp.mean(tt * ohat, axis=-1, keepdims=True))
            dob = d_o.astype(BF16)

            attb = ck["att"].astype(BF16)
            q_inb, k_inb = ck["q_in"].astype(BF16), ck["k_in"].astype(BF16)
            q_stb, k_stb = ck["q_st"].astype(BF16), ck["k_st"].astype(BF16)
            d_att = jnp.where(_tri(False), _bdot(dob, vb, BNT_DIMS), 0.0).astype(BF16)
            d_q_in = _bdot(d_att, k_inb, BNN_DIMS)
            d_k_in = _bdot(d_att, q_inb, BTN_DIMS)
            d_q_st = _bdot(dob, s_in.astype(BF16), BNN_DIMS)
            qdo = _bdot(dob, q_stb, BTN_DIMS)
            d_states = [None] * HG_NC + [dst[h]]
            for c in reversed(range(HG_NC)):
                d_states[c] = d_states[c + 1] * ck["dc"][c] + qdo[c]
            dst[h] = d_states[0]
            ds_out = jnp.stack(d_states[1:], axis=0)
            dsb = ds_out.astype(BF16)
            d_v = _bdot(attb, dob, BTN_DIMS) + _bdot(k_stb, dsb, BNT_DIMS)
            d_k_st = _bdot(vb, dsb, BNN_DIMS)
            d_dc = jnp.sum(ds_out * s_in, axis=1, keepdims=True)

            p_qi = d_q_in * ck["q_in"]
            p_ki = d_k_in * ck["k_in"]
            p_qs = d_q_st * ck["q_st"]
            p_ks = d_k_st * ck["k_st"]
            d_qh = (d_q_in * ck["e_qi"] + d_q_st * ck["e_qs"]) * HG_SCALE
            d_k = d_k_in * ck["e_ki"] + d_k_st * ck["e_ks"]
            d_b = (p_qi - p_ki) + (p_qs - p_ks)
            d_b_mid = jnp.sum(p_ki - p_qi, axis=1, keepdims=True)
            d_b_last = jnp.sum(p_ks, axis=1, keepdims=True) + d_dc * ck["dc"]
            rowi = lax.broadcasted_iota(jnp.int32, (HG_NC, CHUNK, BD), 1)
            d_b = d_b + jnp.where(rowi == CHUNK // 2, d_b_mid, 0.0) + jnp.where(rowi == CHUNK - 1, d_b_last, 0.0)
            d_logf = _tri_sums(True, d_b)
            d_f = d_logf / ck["f"] - d_k
            sig, sq = ck["sig"], ck["sq"]
            d_fp = d_f * (1.0 - lb) * (sig * (1.0 - sig))
            dlb[0:1, cols] += jnp.sum(jnp.sum(d_f * (1.0 - sig), axis=1), axis=0, keepdims=True)
            d_q = d_qh * (sq * (1.0 + q * (1.0 - sq)))
            dz_ref[0, :, cols] = d_q.astype(BF16).reshape(t, BD)
            dz_ref[1, :, cols] = d_fp.astype(BF16).reshape(t, BD)
            dz_ref[2, :, cols] = d_v.astype(BF16).reshape(t, BD)
            dz_ref[3, :, cols] = d_gb.astype(BF16).reshape(t, BD)
            return carry

        lax.fori_loop(0, NB, head, 0, unroll=2)

        @pl.when((b == nb - 1) & (s == ns - 1))
        def _():
            lb = _lower_bound(lg_ref[...])
            dl = dlb[0:1, :] * (lb * (1.0 - lb))
            glg_ref[0:1, :] = dl
            glg_ref[1:2, :] = -dl

    rb = lambda b, s: b * ns + (ns - 1 - s)
    seg = lambda j: pl.BlockSpec((1, t, D), lambda b, s: (j, rb(b, s), 0))
    tile = pl.BlockSpec((t, D), lambda b, s: (rb(b, s), 0))
    return pl.pallas_call(
        body, name="hgrn_bwd", grid=(nb, ns),
        in_specs=[seg(2), seg(3), seg(4), seg(5), tile,
                  pl.BlockSpec((NB, HG_NC, BD, BD), lambda b, s: (b, ns - 1 - s, 0, 0)),
                  tile, pl.BlockSpec((2, D), lambda b, s: (0, 0)), pl.BlockSpec((1, BD), lambda b, s: (0, 0))],
        out_specs=[pl.BlockSpec((4, t, D), lambda b, s: (0, rb(b, s), 0)),
                   pl.BlockSpec((2, D), lambda b, s: (0, 0)), pl.BlockSpec((1, BD), lambda b, s: (0, 0))],
        out_shape=[SDS((4, n, D), BF16), SDS((2, D), F32), SDS((1, BD), F32)],
        scratch_shapes=[pltpu.VMEM((NB, BD, BD), F32), pltpu.VMEM((8, D), F32)],
        compiler_params=_params(60),
    )(z, z, z, z, o_all, st_all, dyb, lb_logits, hg_g)


def _mid(ya, yb, z, b_merge, x2, tgt, fin_g, pa, pb, wo):
    n = x2.shape[0]
    tm = 256
    ni = n // tm

    def body(ya_ref, yb_ref, gma_ref, gmb_ref, bm_ref, x_ref, t_ref, fg_ref, pa_hbm, pb_hbm, wo_hbm,
             dx2_ref, dya_ref, dyb_ref, dgm_ref, loss_ref, gfg_ref, gbm_ref, gm_hbm,
             pa_v, pb_v, wo_v, gpa_v, gpb_v, gwo_v, sem):
        i = pl.program_id(0)
        by_owner = lambda g: g.reshape(NB, BD, D)
        loads = [pltpu.make_async_copy(src, dst, sem.at[k])
                 for k, (src, dst) in enumerate(((pa_hbm, pa_v), (pb_hbm, pb_v), (wo_hbm, wo_v)))]
        stores = [pltpu.make_async_copy(src, dst, sem.at[k])
                  for k, (src, dst) in enumerate((g, gm_hbm.at[:, pl.ds(slot * BD, BD), :])
                                                 for slot, g in enumerate((gpa_v, gpb_v, gwo_v)))]

        @pl.when(i == 0)
        def _():
            for cp in loads:
                cp.start()
            for ref in (gpa_v, gpb_v, gwo_v, loss_ref, gfg_ref, gbm_ref):
                ref[...] = jnp.zeros(ref.shape, F32)
            for cp in loads:
                cp.wait()

        ya_v = ya_ref[...]
        yb_v = yb_ref[...]
        out_a = jnp.dot(ya_v, pa_v[...], preferred_element_type=F32)
        out_b = jnp.dot(yb_v, pb_v[...], preferred_element_type=F32)
        bm = bm_ref[...]
        g_a = _sigmoid(gma_ref[0] + bm[:, 0:D])
        g_b = _sigmoid(gmb_ref[0] + bm[:, D:2 * D])
        mixed = g_a * out_a + g_b * out_b
        mixb = mixed.astype(BF16)
        xo = x_ref[...] + jnp.dot(mixb, wo_v[...], preferred_element_type=F32)
        r = lax.rsqrt(jnp.mean(xo * xo, axis=-1, keepdims=True) + EPS)
        xn = xo * r
        fg = fg_ref[...]
        e = xn * fg - t_ref[...]
        loss_ref[...] += 0.5 * jnp.sum(jnp.mean(e * e, axis=-1, keepdims=True))
        dy = e * (1.0 / D)
        gfg_ref[...] += jnp.sum(dy * xn, axis=0, keepdims=True)
        dxn = dy * fg
        dx2 = r * (dxn - xn * jnp.mean(dxn * xn, axis=-1, keepdims=True))
        dx2_ref[...] = dx2
        dx2b = dx2.astype(BF16)
        d_mixed = lax.dot_general(dx2b, wo_v[...], NT_DIMS, preferred_element_type=F32)
        gwo_v[...] += by_owner(lax.dot_general(mixb, dx2b, TN_DIMS, preferred_element_type=F32))
        d_oa = (d_mixed * g_a).astype(BF16)
        d_ob = (d_mixed * g_b).astype(BF16)
        dgm_a = (d_mixed * out_a) * (g_a * (1.0 - g_a))
        dgm_b = (d_mixed * out_b) * (g_b * (1.0 - g_b))
        gbm_ref[:, 0:D] += jnp.sum(dgm_a, axis=0, keepdims=True)
        gbm_ref[:, D:2 * D] += jnp.sum(dgm_b, axis=0, keepdims=True)
        dgm_ref[0] = dgm_a.astype(BF16)
        dgm_ref[1] = dgm_b.astype(BF16)
        dya_ref[...] = lax.dot_general(d_oa, pa_v[...], NT_DIMS, preferred_element_type=F32)
        dyb_ref[...] = lax.dot_general(d_ob, pb_v[...], NT_DIMS, preferred_element_type=F32)
        gpa_v[...] += by_owner(lax.dot_general(ya_v, d_oa, TN_DIMS, preferred_element_type=F32))
        gpb_v[...] += by_owner(lax.dot_general(yb_v, d_ob, TN_DIMS, preferred_element_type=F32))

        @pl.when(i == ni - 1)
        def _():
            for cp in stores:
                cp.start()
            for cp in stores:
                cp.wait()

    rows = pl.BlockSpec((tm, D), lambda i: (i, 0))
    rep = lambda shape: pl.BlockSpec(shape, lambda i: (0,) * len(shape))
    return pl.pallas_call(
        body, name="mid", grid=(ni,),
        in_specs=[rows, rows,
                  pl.BlockSpec((1, tm, D), lambda i: (6, i, 0)), pl.BlockSpec((1, tm, D), lambda i: (7, i, 0)),
                  rep((1, 2 * D)), rows, rows, rep((1, D)), ANY, ANY, ANY],
        out_specs=[rows, rows, rows, pl.BlockSpec((2, tm, D), lambda i: (0, i, 0)),
                   rep((8, BD)), rep((1, D)), rep((1, 2 * D)), ANY],
        out_shape=[SDS((n, D), F32), SDS((n, D), F32), SDS((n, D), F32), SDS((2, n, D), BF16),
                   SDS((8, BD), F32), SDS((1, D), F32), SDS((1, 2 * D), F32),
                   SDS((NB, MID_ROWS, D), F32)],
        scratch_shapes=[pltpu.VMEM((D, D), BF16)] * 3 + [pltpu.VMEM((NB, BD, D), F32)] * 3 + [pltpu.SemaphoreType.DMA((3,))],
        compiler_params=_params(60),
    )(ya, yb, z, z, b_merge, x2, tgt, fin_g, pa, pb, wo)


def _dz_specs(tm, ni, row_major):
    if row_major:
        ia = lambda i, j: (jnp.minimum(j, 1), i, 0)
        ib = lambda i, j: (jnp.clip(j - 2, 0, 3), i, 0)
        im = lambda i, j: (jnp.clip(j - 6, 0, 1), i, 0)
    else:
        last = ni - 1
        ia = lambda j, i: (jnp.minimum(j, 1), jnp.where(j < 2, i, last), 0)
        ib = lambda j, i: (jnp.clip(j - 2, 0, 3), jnp.where(j < 2, 0, jnp.where(j < 6, i, last)), 0)
        im = lambda j, i: (jnp.clip(j - 6, 0, 1), jnp.where(j < 6, 0, i), 0)
    return [pl.BlockSpec((1, tm, D), f) for f in (ia, ib, im)]


def _inproj_bwd_x(dza, dzb, dzm, w_all, x2, dx2, norm_g, after):
    n = x2.shape[0]
    tm = 512
    ni = n // tm

    def body(dza_ref, dzb_ref, dzm_ref, w_ref, x_ref, dx2_ref, g_ref, after_ref, gx_ref, gg_ref, acc):
        i, j = pl.program_id(0), pl.program_id(1)

        @pl.when((i == 0) & (j == 0))
        def _():
            gg_ref[...] = jnp.zeros((1, D), F32)

        @pl.when(j == 0)
        def _():
            acc[...] = jnp.zeros((tm, D), F32)

        def add(ref):
            acc[...] += lax.dot_general(ref[0], w_ref[0], NT_DIMS, preferred_element_type=F32)

        pl.when(j < 2)(lambda: add(dza_ref))
        pl.when((j >= 2) & (j < 6))(lambda: add(dzb_ref))
        pl.when(j >= 6)(lambda: add(dzm_ref))

        @pl.when(j == NB - 1)
        def _():
            x = x_ref[...]
            r = lax.rsqrt(jnp.mean(x * x, axis=-1, keepdims=True) + EPS)
            xn = x * r
            dh = acc[...]
            gg_ref[...] += jnp.sum(dh * xn, axis=0, keepdims=True)
            dxn = dh * g_ref[...]
            gx_ref[...] = dx2_ref[...] + r * (dxn - xn * jnp.mean(dxn * xn, axis=-1, keepdims=True))

    rows = pl.BlockSpec((tm, D), lambda i, j: (i, 0))
    return pl.pallas_call(
        body, name="inproj_bwd_x", grid=(ni, NB),
        in_specs=_dz_specs(tm, ni, True) + [pl.BlockSpec((1, D, D), lambda i, j: (j, 0, 0)), rows, rows,
                                             pl.BlockSpec((1, D), lambda i, j: (0, 0)), ANY],
        out_specs=[rows, pl.BlockSpec((1, D), lambda i, j: (0, 0))],
        out_shape=[SDS((n, D), F32), SDS((1, D), F32)],
        scratch_shapes=[pltpu.VMEM((tm, D), F32)],
        compiler_params=_params(48),
    )(dza, dzb, dzm, w_all, x2, dx2, norm_g, after)


def _inproj_bwd_w(dza, dzb, dzm, h_all, g_m):
    n = h_all.shape[0]
    tm = min(n, 1024)
    ni = n // tm

    def body(dza_ref, dzb_ref, dzm_ref, h_ref, gm_hbm, gw_ref, got_w, got_m, stage, send_sems, recv_sems):
        j, i = pl.program_id(0), pl.program_id(1)
        x, y, c = _place()
        sibling = (x, y, 1 - c)

        def send_w(q):
            return pltpu.make_async_remote_copy(
                src_ref=stage.at[q % 2], dst_ref=got_w.at[q], send_sem=send_sems.at[q], recv_sem=recv_sems.at[q],
                device_id=sibling, device_id_type=MESH)

        def send_m(q):
            return pltpu.make_async_remote_copy(
                src_ref=gm_hbm.at[2 * q + (1 - c)], dst_ref=got_m.at[q], send_sem=send_sems.at[4 + q],
                recv_sem=recv_sems.at[4 + q], device_id=sibling, device_id_type=MESH)

        @pl.when((j == 0) & (i == 0))
        def _():
            for q in range(4):
                send_m(q).start()

        @pl.when(i == 0)
        def _():
            gw_ref[...] = jnp.zeros((1, D, D), F32)

        def add(ref):
            gw_ref[0] += lax.dot_general(h_ref[...], ref[0], TN_DIMS, preferred_element_type=F32)

        pl.when(j < 2)(lambda: add(dza_ref))
        pl.when((j >= 2) & (j < 6))(lambda: add(dzb_ref))
        pl.when(j >= 6)(lambda: add(dzm_ref))

        for q in range(4):
            @pl.when((i == ni - 1) & (j == 2 * q + 1 - c))
            def _(q=q):
                if q >= 2:
                    send_w(q - 2).wait_send()
                stage[q % 2] = gw_ref[0].astype(BF16)
                send_w(q).start()

        @pl.when((j == NB - 1) & (i == ni - 1))
        def _():
            for q in (2, 3):
                send_w(q).wait_send()
            for q in range(4):
                send_w(q).wait_recv()
                send_m(q).wait_send()
                send_m(q).wait_recv()

    return pl.pallas_call(
        body, name="inproj_bwd_w", grid=(NB, ni),
        in_specs=_dz_specs(tm, ni, False) + [pl.BlockSpec((tm, D), lambda j, i: (i, 0)), ANY],
        out_specs=[pl.BlockSpec((1, D, D), lambda j, i: (j, 0, 0)), ANY, ANY],
        out_shape=[SDS((NB, D, D), F32), SDS((4, D, D), BF16), SDS((4,) + g_m.shape[1:], F32)],
        scratch_shapes=[pltpu.VMEM((2, D, D), BF16), pltpu.SemaphoreType.DMA((8,)), pltpu.SemaphoreType.DMA((8,))],
        compiler_params=_params(48),
    )(dza, dzb, dzm, h_all, g_m)


def _adamw(w, g, m, v):
    rows, cols = w.shape
    tr = _row_tile(rows)

    spec = pl.BlockSpec((tr, cols), lambda i: (i, 0))
    return pl.pallas_call(
        functools.partial(_adam_refs), name="adamw", grid=(rows // tr,), in_specs=[spec] * 4, out_specs=[spec] * 3,
        out_shape=[SDS((rows, cols), F32)] * 3, compiler_params=_params(32),
    )(w, g, m, v)


def _adam_refs(w_ref, g_ref, m_ref, v_ref, d_ref, nm_ref, nv_ref):
    gv = g_ref[...]
    nm = ADAM_B1 * m_ref[...] + (1.0 - ADAM_B1) * gv
    nv = ADAM_B2 * v_ref[...] + (1.0 - ADAM_B2) * (gv * gv)
    m_hat = nm / (1.0 - ADAM_B1 ** ADAM_STEP)
    v_hat = nv / (1.0 - ADAM_B2 ** ADAM_STEP)
    d_ref[...] = -ADAM_LR * (m_hat / (jnp.sqrt(v_hat) + ADAM_EPS) + ADAM_WD * w_ref[...])
    nm_ref[...] = nm
    nv_ref[...] = nv


def _adamw_small(ws, gs, ms, vs):
    k = len(ws)

    def body(*refs):
        for i in range(k):
            _adam_refs(*[refs[part * k + i] for part in range(7)])

    shapes = [SDS(w.shape, F32) for w in ws]
    out = pl.pallas_call(body, name="adamw_small", out_shape=shapes * 3, compiler_params=_params(32))(*ws, *gs, *ms, *vs)
    return out[:k], out[k:2 * k], out[2 * k:]


def _allgather(blocks, dtypes, name):
    na = len(blocks)

    def body(*refs):
        ins, outs, stages = refs[:na], refs[na:2 * na], refs[2 * na:3 * na]
        send_sems, recv_sems, local_sems = refs[3 * na:]
        x, y, c = _place()
        me, sibling = (x, y, c), (x, y, 1 - c)
        chips = [(1 - x, y), (x, 1 - y), (1 - x, 1 - y)]
        blk = lambda p: 4 * p[0] + 2 * p[1] + p[2]

        def copy(a, k, block, to, src=None):
            return pltpu.make_async_remote_copy(
                src_ref=outs[a].at[blk(block)] if src is None else src, dst_ref=outs[a].at[blk(block)],
                send_sem=send_sems.at[7 * a + k], recv_sem=recv_sems.at[7 * a + k],
                device_id=to, device_id_type=MESH)

        mine, first, passed = [], [], []
        for a in range(na):
            stages[a][...] = ins[a][...].astype(dtypes[a])
            mine.append(pltpu.make_async_copy(stages[a], outs[a].at[blk(me)], local_sems.at[a]))
            mine[-1].start()
            first.append(copy(a, 0, me, sibling, src=stages[a]))
            first += [copy(a, 1 + j, me, (*chip, c), src=stages[a]) for j, chip in enumerate(chips)]
        for cp in first:
            cp.start()
        for j, chip in enumerate(chips):
            for a in range(na):
                copy(a, 1 + j, (*chip, c), me).wait_recv()
                passed.append(copy(a, 4 + j, (*chip, c), sibling))
                passed[-1].start()
        for a in range(na):
            copy(a, 0, sibling, me).wait_recv()
            for j, chip in enumerate(chips):
                copy(a, 4 + j, (*chip, 1 - c), me).wait_recv()
        for cp in first + passed:
            cp.wait_send()
        for cp in mine:
            cp.wait()

    return pl.pallas_call(
        body, name=name,
        in_specs=[pl.BlockSpec(memory_space=pltpu.VMEM)] * na, out_specs=[ANY] * na,
        out_shape=[SDS((NB,) + b.shape, dt) for b, dt in zip(blocks, dtypes)],
        scratch_shapes=[pltpu.VMEM(b.shape, dt) for b, dt in zip(blocks, dtypes)]
        + [pltpu.SemaphoreType.DMA((7 * na,)), pltpu.SemaphoreType.DMA((7 * na,)), pltpu.SemaphoreType.DMA((na,))],
        compiler_params=_params(40),
    )(*blocks)


HBM = pl.BlockSpec(memory_space=pltpu.HBM)
SEMS = pl.BlockSpec(memory_space=pltpu.SEMAPHORE)
EFFECT = pltpu.SideEffectType.DATAFLOW_SIDE_EFFECTING


def _chip_copies(srcs, lands, send_sems, recv_sems):
    x, y, c = _place()
    return [pltpu.make_async_remote_copy(
        src_ref=srcs[a].at[slot], dst_ref=lands[a].at[slot],
        send_sem=send_sems.at[3 * a + slot], recv_sem=recv_sems.at[3 * a + slot],
        device_id=(px, py, c), device_id_type=MESH)
        for a in range(len(srcs)) for slot, (px, py) in enumerate(_other_chips(x, y))]


def _split_start(name, copies, per_array, srcs, lands, after=None):
    na = len(srcs)

    def body(*refs):
        send_sems, recv_sems = refs[-2 * na - 3], refs[-2 * na - 2]
        for cp in copies(refs[:na], refs[na:2 * na], send_sems, recv_sems):
            cp.start()
        refs[-1][...] = jnp.zeros_like(refs[-1])

    hbm = lambda a: pltpu.HBM(a.shape, a.dtype)
    pin = lambda a: pltpu.with_memory_space_constraint(a, pltpu.HBM)
    out = pl.pallas_call(
        body, name=name,
        out_shape=(pltpu.SemaphoreType.DMA((per_array * na,)), pltpu.SemaphoreType.DMA((per_array * na,)),
                   *[hbm(a) for a in srcs], *[hbm(a) for a in lands], SDS((8, BD), F32)),
        in_specs=[HBM] * (2 * na) + ([] if after is None else [ANY]),
        out_specs=(SEMS, SEMS, *[HBM] * (2 * na), pl.BlockSpec(memory_space=pltpu.VMEM)),
        input_output_aliases={i: 2 + i for i in range(2 * na)},
        compiler_params=pltpu.CompilerParams(has_side_effects=EFFECT),
    )(*[pin(a) for a in srcs], *[pin(a) for a in lands], *([] if after is None else [after]))
    return out[0], out[1], out[2:2 + na], out[2 + na:2 + 2 * na], out[-1]


def _split_wait(name, copies, started, after):
    send_sems, recv_sems, srcs, lands, _ = started
    na = len(srcs)

    def body(*refs):
        waits = copies(refs[:na], refs[na:2 * na], refs[2 * na], refs[2 * na + 1])
        for cp in waits:
            cp.wait_send()
        for cp in waits:
            cp.wait_recv()

    hbm = lambda a: pltpu.HBM(a.shape, a.dtype)
    out = pl.pallas_call(
        body, name=name,
        out_shape=(*[hbm(a) for a in srcs], *[hbm(a) for a in lands]),
        in_specs=[HBM] * (2 * na) + [SEMS, SEMS, ANY],
        out_specs=tuple([HBM] * (2 * na)),
        input_output_aliases={i: i for i in range(2 * na)},
        compiler_params=pltpu.CompilerParams(has_side_effects=EFFECT),
    )(*srcs, *lands, send_sems, recv_sems, after)
    return out[na:]


def _add_sibling(place, g, a_in):
    _, r, cols = g.shape
    tr = _row_tile(r)

    def chip(k, pr):
        qx = pr[0] if k in (1, 3) else 1 - pr[0]
        qy = pr[1] if k in (0, 3) else 1 - pr[1]
        return 2 * qx + qy

    def body(place_ref, *refs):
        g_refs, a_refs, (out_ref, own_ref) = refs[0:4], refs[4:8], refs[8:10]
        for k in range(3):
            out_ref[k] = (g_refs[k][0] + a_refs[k][0].astype(F32)).astype(BF16)
        own_ref[...] = g_refs[3][0] + a_refs[3][0].astype(F32)

    mine = lambda k: pl.BlockSpec((1, tr, cols), lambda i, pr: (2 * chip(k, pr) + pr[2], i, 0))
    theirs = lambda k: pl.BlockSpec((1, tr, cols), lambda i, pr: (chip(k, pr), i, 0))
    return pl.pallas_call(
        body, name="add_sibling",
        grid_spec=pltpu.PrefetchScalarGridSpec(
            num_scalar_prefetch=1, grid=(r // tr,),
            in_specs=[mine(k) for k in range(4)] + [theirs(k) for k in range(4)],
            out_specs=[pl.BlockSpec((3, tr, cols), lambda i, pr: (0, i, 0)),
                       pl.BlockSpec((tr, cols), lambda i, pr: (i, 0))]),
        out_shape=[SDS((3, r, cols), BF16), SDS((r, cols), F32)], compiler_params=_params(48),
    )(place, *[g] * 4, *[a_in] * 4)


def _add_chips(own, b_in):
    r, cols = own.shape
    tr = _row_tile(r)

    def body(p_ref, b0_ref, b1_ref, b2_ref, o_ref):
        o_ref[...] = ((p_ref[...] + b0_ref[0].astype(F32)) + b1_ref[0].astype(F32)) + b2_ref[0].astype(F32)

    slot = lambda k: pl.BlockSpec((1, tr, cols), lambda i: (k, i, 0))
    spec = pl.BlockSpec((tr, cols), lambda i: (i, 0))
    return pl.pallas_call(
        body, name="add_chips", grid=(r // tr,), in_specs=[spec, slot(0), slot(1), slot(2)], out_specs=spec,
        out_shape=SDS((r, cols), F32), compiler_params=_params(32),
    )(own, b_in, b_in, b_in)


VEC_NAMES = ("b_merge", "conv_b", "rg_bx", "rg_ba", "rg_lambda", "hg_lb_logits", "hg_norm_g", "final_norm_g")
REP_NAMES = ("rg_wx", "rg_wa", "norm_g") + VEC_NAMES
SMALL_AT = 3 * BD
SMALL_ROWS = 48
MID_ROWS = 448


def _sum_blocks(parts):
    def body(p_ref, o_ref):
        acc = p_ref[0]
        for k in range(1, NB):
            acc = acc + p_ref[k]
        o_ref[...] = acc

    return pl.pallas_call(body, name="sum_blocks", out_shape=SDS(parts.shape[1:], F32))(parts)


def _pack_rows(arrays, width, row_multiple=8):
    flat = jnp.concatenate([a.reshape(-1) for a in arrays])
    rows = -(-flat.shape[0] // width)
    rows = -(-rows // row_multiple) * row_multiple
    return jnp.pad(flat, (0, rows * width - flat.shape[0])).reshape(rows, width)


def _unpack(flat, like):
    out, off = [], 0
    for a in like:
        out.append(flat[off:off + a.size].reshape(a.shape))
        off += a.size
    return out


def kernel(x, w_in, b_merge, conv_w, conv_b, rg_wx, rg_bx, rg_wa, rg_ba, rg_lambda, hg_lb_logits, hg_norm_g, proj_a, proj_b, w_out, norm_g, final_norm_g, loss_target, m_w_in, m_b_merge, m_conv_w, m_conv_b, m_rg_wx, m_rg_bx, m_rg_wa, m_rg_ba, m_rg_lambda, m_hg_lb_logits, m_hg_norm_g, m_proj_a, m_proj_b, m_w_out, m_norm_g, m_final_norm_g, v_w_in, v_b_merge, v_conv_w, v_conv_b, v_rg_wx, v_rg_bx, v_rg_wa, v_rg_ba, v_rg_lambda, v_hg_lb_logits, v_hg_norm_g, v_proj_a, v_proj_b, v_w_out, v_norm_g, v_final_norm_g):
    weights = dict(w_in=w_in, b_merge=b_merge, conv_w=conv_w, conv_b=conv_b, rg_wx=rg_wx, rg_bx=rg_bx, rg_wa=rg_wa,
                   rg_ba=rg_ba, rg_lambda=rg_lambda, hg_lb_logits=hg_lb_logits, hg_norm_g=hg_norm_g, proj_a=proj_a,
                   proj_b=proj_b, w_out=w_out, norm_g=norm_g, final_norm_g=final_norm_g)
    mom1 = dict(w_in=m_w_in, b_merge=m_b_merge, conv_w=m_conv_w, conv_b=m_conv_b, rg_wx=m_rg_wx, rg_bx=m_rg_bx,
                rg_wa=m_rg_wa, rg_ba=m_rg_ba, rg_lambda=m_rg_lambda, hg_lb_logits=m_hg_lb_logits,
                hg_norm_g=m_hg_norm_g, proj_a=m_proj_a, proj_b=m_proj_b, w_out=m_w_out, norm_g=m_norm_g,
                final_norm_g=m_final_norm_g)
    mom2 = dict(w_in=v_w_in, b_merge=v_b_merge, conv_w=v_conv_w, conv_b=v_conv_b, rg_wx=v_rg_wx, rg_bx=v_rg_bx,
                rg_wa=v_rg_wa, rg_ba=v_rg_ba, rg_lambda=v_rg_lambda, hg_lb_logits=v_hg_lb_logits,
                hg_norm_g=v_hg_norm_g, proj_a=v_proj_a, proj_b=v_proj_b, w_out=v_w_out, norm_g=v_norm_g,
                final_norm_g=v_final_norm_g)
    order = list(weights)
    nb, s_len, _ = x.shape
    n = nb * s_len
    px, py, pc = _place()
    place = jnp.stack([px, py, pc]).astype(jnp.int32)

    x2 = x.reshape(n, D)
    cw_blk = jnp.pad(conv_w[0], ((0, 4), (0, 0)))
    order_ids = jnp.stack([_block_id(p) for p in _arrival_order(px, py, pc)]).astype(jnp.int32)
    z, h_all, w_all, pa_all, pb_all, wo_all, cw_all = _gather_inproj(
        order_ids, x2, norm_g, [w_in[0], proj_a[0], proj_b[0], w_out[0], cw_blk], [BF16, BF16, BF16, BF16, F32])
    pa_full, pb_full, wo_full = (a.reshape(D, D) for a in (pa_all, pb_all, wo_all))
    cw8 = cw_all.transpose(1, 0, 2).reshape(8, D)
    wx_b, wa_b = rg_wx[0].astype(BF16), rg_wa[0].astype(BF16)
    cb, bx, ba = conv_b, rg_bx.reshape(1, D), rg_ba.reshape(1, D)
    fin_g = final_norm_g.reshape(1, D)

    hlru, ya = _lru_fwd(z, cw8, cb, wx_b, wa_b, bx, ba, rg_lambda, nb, s_len)
    o_all, yb, st_all = _hgrn_fwd(z, hg_lb_logits, hg_norm_g, nb, s_len)

    (dx2, dya, dyb, dzm, loss_acc, g_fin, g_bm, g_mid) = _mid(
        ya, yb, z, b_merge, x2, loss_target.reshape(n, D), fin_g, pa_full, pb_full, wo_full)
    dzb, g_lg, g_hg = _hgrn_bwd(z, o_all, st_all, dyb, hg_lb_logits, hg_norm_g, nb, s_len)
    dza, g_cw8, g_cb, g_wx, g_wa, g_bx, g_ba, g_lam = _lru_bwd(
        z, hlru, dya, cw8, cb, wx_b, wa_b, bx, ba, rg_lambda, nb, s_len)

    part = dict(b_merge=g_bm, conv_b=g_cb, rg_bx=g_bx, rg_ba=g_ba, rg_lambda=g_lam, hg_lb_logits=g_lg,
                hg_norm_g=g_hg, final_norm_g=g_fin)
    vec = _pack_rows([part[k] for k in VEC_NAMES], BD)
    vec = jnp.pad(vec, ((0, 16 * NB - vec.shape[0]), (0, 0))).reshape(NB, 2, D)
    rows8 = lambda a: jnp.pad(a, ((0, 0), (0, 8 - a.shape[1]), (0, 0)))
    small = jnp.concatenate([g_wx.reshape(NB, 16, D), g_wa.reshape(NB, 16, D),
                             rows8(g_cw8.reshape(8, NB, BD).transpose(1, 0, 2).reshape(NB, 1, D)), rows8(vec),
                             jnp.zeros((NB, MID_ROWS - SMALL_AT - SMALL_ROWS, D), F32)], axis=1)
    g_m = lax.dynamic_update_slice(g_mid, small, (0, SMALL_AT, 0))
    g_w, w_from_sibling, m_from_sibling = _inproj_bwd_w(dza, dzb, dzm, h_all, g_m)
    w_out_bf, w_own = _add_sibling(place, g_w, w_from_sibling)
    m_out_bf, m_own = _add_sibling(place, g_m, m_from_sibling)
    outgoing = [w_out_bf, m_out_bf]
    chip_sums = _split_start("rs_chips_start", _chip_copies, 3, outgoing, [lax.empty(a.shape, a.dtype) for a in outgoing])
    grad_x, g_ng = _inproj_bwd_x(dza, dzb, dzm, w_all, x2, dx2, norm_g, chip_sums[-1])
    from_chips = _split_wait("rs_chips_wait", _chip_copies, chip_sums, grad_x)
    r_w = _add_chips(w_own, from_chips[0])
    r_m = _add_chips(m_own, from_chips[1])
    row = lax.broadcasted_iota(jnp.int32, (8, D), 0)
    mine = jnp.where(row == 0, g_ng, jnp.where(row == 1, loss_acc[0:1, 0:1], 0.0))
    tail = jnp.concatenate([r_m[SMALL_AT:SMALL_AT + SMALL_ROWS], mine], axis=0)
    (tail_all,) = _allgather([tail], [F32], "gather_small_grads")
    summed = _sum_blocks(tail_all[:, SMALL_ROWS:SMALL_ROWS + 8])

    grads = dict(w_in=r_w.reshape(1, D, D),
                 proj_a=r_m[0:BD].reshape(1, BD, D), proj_b=r_m[BD:2 * BD].reshape(1, BD, D),
                 w_out=r_m[2 * BD:3 * BD].reshape(1, BD, D),
                 conv_w=r_m[SMALL_AT + 32].reshape(8, BD)[0:4].reshape(1, 4, BD),
                 rg_wx=tail_all[:, 0:16].reshape(1, NB, BD, BD), rg_wa=tail_all[:, 16:32].reshape(1, NB, BD, BD),
                 norm_g=summed[0:1])
    vec_all = tail_all[:, 40:42].reshape(-1)
    for k, gk in zip(VEC_NAMES, _unpack(vec_all, [weights[k] for k in VEC_NAMES])):
        grads[k] = gk

    delta, new_m, new_v = {}, {}, {}
    for k in ("w_in", "proj_a", "proj_b", "w_out"):
        shp = weights[k].shape
        two = lambda a: a.reshape(shp[1], shp[2])
        d_k, m_k, v_k = _adamw(two(weights[k]), two(grads[k]), two(mom1[k]), two(mom2[k]))
        delta[k], new_m[k], new_v[k] = d_k.reshape(shp), m_k.reshape(shp), v_k.reshape(shp)
    rep = list(REP_NAMES) + ["conv_w"]
    flat2 = lambda a: a.reshape(-1, a.shape[-1])
    outs = _adamw_small(*[[flat2(t[k]) for k in rep] for t in (weights, grads, mom1, mom2)])
    for tgt, arrays in zip((delta, new_m, new_v), outs):
        for k, a in zip(rep, arrays):
            tgt[k] = a.reshape(weights[k].shape)

    return (summed[1, 0],grad_x.reshape(x.shape), *[grads[k] for k in order], *[delta[k] for k in order],
            *[new_m[k] for k in order], *[new_v[k] for k in order])
```

```python
import functools

import jax
import jax.numpy as jnp
from jax import lax
from jax.experimental import pallas as pl
from jax.experimental.pallas import tpu as pltpu

F32 = jnp.float32
BF16 = jnp.bfloat16
SDS = jax.ShapeDtypeStruct
MESH = pl.DeviceIdType.MESH
ANY = pl.BlockSpec(memory_space=pl.ANY)

D = 1024
NB = 8
BD = D // NB
CHUNK = 64
EPS = 1e-6
LRU_C = 8.0
HG_SCALE = BD ** -0.5
ADAM_LR, ADAM_B1, ADAM_B2, ADAM_EPS, ADAM_WD, ADAM_STEP = 0.001, 0.9, 0.999, 1e-08, 0.01, 10

NT_DIMS = (((1,), (1,)), ((), ()))
TN_DIMS = (((0,), (0,)), ((), ()))


def _params(vmem_mib):
    return pltpu.CompilerParams(vmem_limit_bytes=vmem_mib << 20)


def _row_tile(rows, most=256):
    assert rows % 8 == 0
    return max(t for t in range(8, min(rows, most) + 1, 8) if rows % t == 0)


def _sigmoid(v):
    return 0.5 * (jnp.tanh(0.5 * v) + 1.0)


def _groups(v):
    return v.reshape(v.shape[0] // 8, 8, v.shape[1])


def _softplus_neg(lam):
    t = -lam
    e = jnp.exp(-jnp.abs(t))
    w = 1.0 + e
    d = w - 1.0
    l1p = jnp.where(d == 0.0, e, jnp.log(w) * (e / jnp.where(d == 0.0, 1.0, d)))
    return jnp.maximum(t, 0.0) + l1p


def _place():
    return lax.axis_index("x"), lax.axis_index("y"), lax.axis_index("c")


def _other_chips(x, y):
    return [(1 - x, y), (x, 1 - y), (1 - x, 1 - y)]


def _block_id(p):
    return 4 * p[0] + 2 * p[1] + p[2]


def _arrival_order(x, y, c):
    near, far, diag = _other_chips(x, y)
    return [(x, y, c), (x, y, 1 - c), (*near, c), (*far, c), (*near, 1 - c), (*far, 1 - c), (*diag, c), (*diag, 1 - c)]


def _gather_inproj(order_ids, x2, norm_g, blocks, dtypes):
    na = len(blocks)
    n = x2.shape[0]
    tm = min(n, 1024)
    ni = n // tm

    def body(order_ref, x_ref, g_ref, *refs):
        ins, (z_ref, h_ref), outs = refs[:na], refs[na:na + 2], refs[na + 2:2 * na + 2]
        stages = refs[2 * na + 2:3 * na + 2]
        h_full, wbuf, send_sems, recv_sems, local_sems, wsems, hsem = refs[3 * na + 2:]
        j, i = pl.program_id(0), pl.program_id(1)
        x, y, c = _place()
        me, sibling = (x, y, c), (x, y, 1 - c)
        chips = _other_chips(x, y)
        small = range(1, na)

        def copy(a, k, block, to, src=None):
            return pltpu.make_async_remote_copy(
                src_ref=outs[a].at[_block_id(block)] if src is None else src, dst_ref=outs[a].at[_block_id(block)],
                send_sem=send_sems.at[7 * a + k], recv_sem=recv_sems.at[7 * a + k],
                device_id=to, device_id_type=MESH)

        def local(a):
            return pltpu.make_async_copy(stages[a], outs[a].at[_block_id(me)], local_sems.at[a])

        def landed(a, slot):
            copy(a, 1 + slot, (*chips[slot], c), me).wait_recv()
            copy(a, 4 + slot, (*chips[slot], c), sibling).start()
            if slot < 2:
                @pl.when(c == slot)
                def _():
                    copy(a, 3, (*chips[slot], c), (*chips[1 - slot], c)).start()

        def diagonal_and_small():
            landed(0, 2)
            for a in small:
                landed(a, 0)
                landed(a, 1)

        def passed_on(a, slot):
            copy(a, 4 + slot, (*chips[slot], 1 - c), me).wait_recv()

        @pl.when((j == 0) & (i == 0))
        def _():
            for a in range(na):
                stages[a][...] = ins[a][...].astype(dtypes[a])
                local(a).start()
            for a in range(na):
                copy(a, 0, me, sibling, src=stages[a]).start()
                for slot, chip in enumerate(chips[:2]):
                    copy(a, 1 + slot, me, (*chip, c), src=stages[a]).start()

        @pl.when(j == 0)
        def _():
            xv = x_ref[...]
            r = lax.rsqrt(jnp.mean(xv * xv, axis=-1, keepdims=True) + EPS)
            hb = ((xv * r) * g_ref[...]).astype(BF16)
            h_full[pl.ds(pl.multiple_of(i * tm, tm), tm), :] = hb

        save_h = pltpu.make_async_copy(h_full, h_ref, hsem)
        pl.when((j == 0) & (i == ni - 1))(save_h.start)

        steps = [
            lambda: local(0).wait(),
            lambda: copy(0, 0, sibling, me).wait_recv(),
            lambda: landed(0, 0),
            lambda: landed(0, 1),
            lambda: passed_on(0, 0),
            lambda: passed_on(0, 1),
            diagonal_and_small,
            lambda: passed_on(0, 2),
        ]
        def w_load(k):
            return pltpu.make_async_copy(outs[0].at[order_ref[k]], wbuf.at[k % 2], wsems.at[k % 2])

        for k, step in enumerate(steps):
            @pl.when((j == 0) & (i == 0) if k == 0 else (j == k - 1) & (i == ni - 1))
            def _(k=k, step=step):
                step()
                w_load(k).start()

        pl.when(i == 0)(lambda: w_load(j).wait())
        z_ref[0] = jnp.dot(h_full[pl.ds(pl.multiple_of(i * tm, tm), tm), :], wbuf[j % 2], preferred_element_type=F32)

        @pl.when((j == NB - 1) & (i == ni - 1))
        def _():
            save_h.wait()
            for a in small:
                landed(a, 2)
            for a in small:
                local(a).wait()
                copy(a, 0, sibling, me).wait_recv()
                for slot in range(3):
                    passed_on(a, slot)
            for a in range(na):
                copy(a, 0, me, sibling, src=stages[a]).wait_send()
                for slot, chip in enumerate(chips):
                    copy(a, 1 + slot, me, (*chip, c), src=stages[a]).wait_send()
                    copy(a, 4 + slot, (*chip, c), sibling).wait_send()

    rows_once = lambda j, i, order: (jnp.where(j == 0, i, ni - 1), 0)
    vmem = pl.BlockSpec(memory_space=pltpu.VMEM)
    return pl.pallas_call(
        body, name="gather_inproj",
        grid_spec=pltpu.PrefetchScalarGridSpec(
            num_scalar_prefetch=1, grid=(NB, ni),
            in_specs=[pl.BlockSpec((tm, D), rows_once), pl.BlockSpec((1, D), lambda j, i, order: (0, 0))] + [vmem] * na,
            out_specs=[pl.BlockSpec((1, tm, D), lambda j, i, order: (order[j], i, 0)), ANY] + [ANY] * na,
            scratch_shapes=[pltpu.VMEM(b.shape, dt) for b, dt in zip(blocks, dtypes)]
            + [pltpu.VMEM((n, D), BF16), pltpu.VMEM((2, D, D), BF16),
               pltpu.SemaphoreType.DMA((7 * na,)), pltpu.SemaphoreType.DMA((7 * na,)),
               pltpu.SemaphoreType.DMA((na,)), pltpu.SemaphoreType.DMA((2,)), pltpu.SemaphoreType.DMA(())]),
        out_shape=[SDS((NB, n, D), F32), SDS((n, D), BF16)] + [SDS((NB,) + b.shape, dt) for b, dt in zip(blocks, dtypes)],
        compiler_params=_params(56),
    )(order_ids, x2, norm_g, *blocks)


LRU_T = 256


def _shifted(groups, shifts):
    row = lax.broadcasted_iota(jnp.int32, (groups.shape[0] - 1,) + groups.shape[1:], 1)
    out = []
    for s in shifts:
        y = pltpu.roll(groups, s % 8, 1)
        moved = jnp.where(row >= s, y[1:], y[:-1]) if s > 0 else jnp.where(row < 8 + s, y[:-1], y[1:])
        out.append(moved.reshape(-1, groups.shape[2]))
    return out


def _conv(taps, cw, cb):
    acc = taps[0] * cw[0:1, :] + taps[1] * cw[1:2, :]
    acc = acc + taps[2] * cw[2:3, :]
    acc = acc + taps[3] * cw[3:4, :]
    return cb + acc


def _lru_gates(xa, wx_ref, wa_ref, bx, ba, lam):
    xab = xa.astype(BF16)
    pis, prs = [], []
    for h in range(NB):
        xs = xab[:, h * BD:(h + 1) * BD]
        pis.append(jnp.dot(xs, wx_ref[h], preferred_element_type=F32))
        prs.append(jnp.dot(xs, wa_ref[h], preferred_element_type=F32))
    gi = _sigmoid(jnp.concatenate(pis, axis=1) + bx)
    gr = _sigmoid(jnp.concatenate(prs, axis=1) + ba)
    sp = _softplus_neg(lam)
    log_a = (-LRU_C * gr) * sp
    a = jnp.exp(log_a)
    mult = jnp.sqrt(-jnp.tanh(log_a) * (a * a + 1.0))
    return xab, gi, gr, sp, a, mult


def _lru_fwd(z, cw8, cb, wx, wa, bx, ba, lam, nb, s_len):
    n = nb * s_len
    t = LRU_T
    ns = s_len // t

    def body(xp_ref, ga_ref, cw_ref, cb_ref, wx_ref, wa_ref, bx_ref, ba_ref, lam_ref,
             h_ref, ya_ref, ext, a_s, u_s, carry):
        @pl.when(pl.program_id(1) == 0)
        def _():
            ext[0:8, :] = jnp.zeros((8, D), F32)
            carry[...] = jnp.zeros((8, D), F32)

        xp = xp_ref[0]
        ext[8:8 + t, :] = xp
        xa = _conv(_shifted(_groups(ext[...]), (3, 2, 1)) + [xp], cw_ref[...], cb_ref[...])
        ext[0:8, :] = xp[t - 8:t, :]
        _, gi, _, _, a, mult = _lru_gates(xa, wx_ref, wa_ref, bx_ref[...], ba_ref[...], lam_ref[...])
        u = (mult * gi) * xa
        a, u = _groups(a), _groups(u)
        row = lax.broadcasted_iota(jnp.int32, a.shape, 1)
        for sh in (1, 2, 4):
            a_sh = pltpu.roll(a, sh, 1)
            u_sh = pltpu.roll(u, sh, 1)
            m = row >= sh
            u = jnp.where(m, a * u_sh + u, u)
            a = jnp.where(m, a * a_sh, a)
        a_s[...] = a.reshape(t, D)
        u_s[...] = u.reshape(t, D)

        def step(g, c):
            r = pl.multiple_of(g * 8, 8)
            hg = u_s[pl.ds(r, 8), :] + a_s[pl.ds(r, 8), :] * c
            h_ref[pl.ds(r, 8), :] = hg
            return hg[7:8, :]

        c_out = lax.fori_loop(0, t // 8, step, carry[0:1, :], unroll=4)
        carry[0:1, :] = c_out
        ga = ga_ref[0]
        ya_ref[...] = (h_ref[...] * (ga * _sigmoid(ga))).astype(BF16)

    row_map = lambda b, s: (b * ns + s, 0)
    rep2 = lambda b, s: (0, 0)
    rep3 = lambda b, s: (0, 0, 0)
    return pl.pallas_call(
        body, name="lru_fwd", grid=(nb, ns),
        in_specs=[pl.BlockSpec((1, t, D), lambda b, s: (0, b * ns + s, 0)),
                  pl.BlockSpec((1, t, D), lambda b, s: (1, b * ns + s, 0)),
                  pl.BlockSpec((8, D), rep2), pl.BlockSpec((1, D), rep2),
                  pl.BlockSpec((NB, BD, BD), rep3), pl.BlockSpec((NB, BD, BD), rep3),
                  pl.BlockSpec((1, D), rep2), pl.BlockSpec((1, D), rep2), pl.BlockSpec((1, D), rep2)],
        out_specs=[pl.BlockSpec((t, D), row_map), pl.BlockSpec((t, D), row_map)],
        out_shape=[SDS((n, D), F32), SDS((n, D), BF16)],
        scratch_shapes=[pltpu.VMEM((t + 8, D), F32), pltpu.VMEM((t, D), F32), pltpu.VMEM((t, D), F32),
                        pltpu.VMEM((8, D), F32)],
        compiler_params=_params(48),
    )(z, z, cw8, cb, wx, wa, bx, ba, lam)


def _lru_bwd(z, h_all, dya, cw8, cb, wx, wa, bx, ba, lam, nb, s_len):
    n = nb * s_len
    t = LRU_T
    ns = s_len // t
    t8 = t // 8

    def body(xp_ref, xph_ref, ga_ref, h_ref, hh_ref, dya_ref, cw_ref, cb_ref, wx_ref, wa_ref, bx_ref, ba_ref,
             lam_ref, dz_ref, gcw_ref, gcb_ref, gwx_ref, gwa_ref, gbx_ref, gba_ref, glam_ref,
             ext, hext, dext, a_s, u_s, dh_s, carry):
        b, s = pl.program_id(0), pl.program_id(1)
        first_tile = s == ns - 1

        @pl.when((b == 0) & (s == 0))
        def _():
            for ref in (gcw_ref, gcb_ref, gwx_ref, gwa_ref, gbx_ref, gba_ref, glam_ref):
                ref[...] = jnp.zeros(ref.shape, F32)

        @pl.when(s == 0)
        def _():
            dext[t:t + 8, :] = jnp.zeros((8, D), F32)
            carry[...] = jnp.zeros((8, D), F32)

        keep = jnp.where(first_tile, 0.0, 1.0)
        xp = xp_ref[0]
        ext[0:8, :] = xph_ref[0] * keep
        ext[8:8 + t, :] = xp
        hext[0:8, :] = hh_ref[...] * keep
        hext[8:8 + t, :] = h_ref[...]
        cw = cw_ref[...]
        lam = lam_ref[...]
        taps = _shifted(_groups(ext[...]), (3, 2, 1)) + [xp]
        xa = _conv(taps, cw, cb_ref[...])
        xab, gi, gr, sp, a, mult = _lru_gates(xa, wx_ref, wa_ref, bx_ref[...], ba_ref[...], lam)
        (h_prev,) = _shifted(_groups(hext[...]), (1,))
        ga = ga_ref[0]
        sg = _sigmoid(ga)
        dya_v = dya_ref[...]
        d_ga = dya_v * h_ref[...] * (sg * (1.0 + ga * (1.0 - sg)))
        g_in = dya_v * (ga * sg)

        (an,) = _shifted(jnp.concatenate([_groups(a), jnp.ones((1, 8, D), F32)], axis=0), (-1,))
        an, u = _groups(an), _groups(g_in)
        row = lax.broadcasted_iota(jnp.int32, an.shape, 1)
        for sh in (1, 2, 4):
            a_sh = pltpu.roll(an, 8 - sh, 1)
            u_sh = pltpu.roll(u, 8 - sh, 1)
            m = row < 8 - sh
            u = jnp.where(m, u + an * u_sh, u)
            an = jnp.where(m, an * a_sh, an)
        a_s[...] = an.reshape(t, D)
        u_s[...] = u.reshape(t, D)

        def step(i, c):
            r = pl.multiple_of((t8 - 1 - i) * 8, 8)
            dg = u_s[pl.ds(r, 8), :] + a_s[pl.ds(r, 8), :] * c
            dh_s[pl.ds(r, 8), :] = dg
            return dg[0:1, :]

        lax.fori_loop(0, t8, step, carry[0:1, :], unroll=4)
        dh = dh_s[...]
        carry[0:1, :] = a[0:1, :] * dh[0:1, :]

        d_a = dh * h_prev
        dux = dh * xa
        d_mult = dux * gi
        d_gi = dux * mult
        d_xa = dh * (mult * gi)
        d_loga = d_a * a - d_mult * ((a * a) / mult)
        d_gr = d_loga * (-LRU_C * sp)
        d_sp = jnp.sum(d_loga * (-LRU_C * gr), axis=0, keepdims=True)
        glam_ref[...] += d_sp * (-_sigmoid(-lam))
        d_pi = d_gi * gi * (1.0 - gi)
        d_pr = d_gr * gr * (1.0 - gr)
        gbx_ref[...] += jnp.sum(d_pi, axis=0, keepdims=True)
        gba_ref[...] += jnp.sum(d_pr, axis=0, keepdims=True)
        dpib = d_pi.astype(BF16)
        dprb = d_pr.astype(BF16)
        back = []
        for h in range(NB):
            cs = slice(h * BD, (h + 1) * BD)
            gwx_ref[h] += lax.dot_general(xab[:, cs], dpib[:, cs], TN_DIMS, preferred_element_type=F32)
            gwa_ref[h] += lax.dot_general(xab[:, cs], dprb[:, cs], TN_DIMS, preferred_element_type=F32)
            back.append(lax.dot_general(dpib[:, cs], wx_ref[h], NT_DIMS, preferred_element_type=F32)
                        + lax.dot_general(dprb[:, cs], wa_ref[h], NT_DIMS, preferred_element_type=F32))
        d_xa = d_xa + jnp.concatenate(back, axis=1)

        dext[0:t, :] = d_xa
        later = _shifted(_groups(dext[...]), (-3, -2, -1))
        d_xp = later[0] * cw[0:1, :] + later[1] * cw[1:2, :]
        d_xp = d_xp + later[2] * cw[2:3, :]
        d_xp = d_xp + d_xa * cw[3:4, :]
        dext[t:t + 8, :] = d_xa[0:8, :]
        gcb_ref[...] += jnp.sum(d_xa, axis=0, keepdims=True)
        for k in range(4):
            gcw_ref[k:k + 1, :] += jnp.sum(d_xa * taps[k], axis=0, keepdims=True)
        dz_ref[0] = d_xp.astype(BF16)
        dz_ref[1] = d_ga.astype(BF16)

    rb = lambda b, s: b * ns + (ns - 1 - s)
    halo = lambda b, s: jnp.maximum(rb(b, s) * t8 - 1, 0)
    rep2 = lambda b, s: (0, 0)
    rep3 = lambda b, s: (0, 0, 0)
    return pl.pallas_call(
        body, name="lru_bwd", grid=(nb, ns),
        in_specs=[pl.BlockSpec((1, t, D), lambda b, s: (0, rb(b, s), 0)),
                  pl.BlockSpec((1, 8, D), lambda b, s: (0, halo(b, s), 0)),
                  pl.BlockSpec((1, t, D), lambda b, s: (1, rb(b, s), 0)),
                  pl.BlockSpec((t, D), lambda b, s: (rb(b, s), 0)),
                  pl.BlockSpec((8, D), lambda b, s: (halo(b, s), 0)),
                  pl.BlockSpec((t, D), lambda b, s: (rb(b, s), 0)),
                  pl.BlockSpec((8, D), rep2), pl.BlockSpec((1, D), rep2),
                  pl.BlockSpec((NB, BD, BD), rep3), pl.BlockSpec((NB, BD, BD), rep3),
                  pl.BlockSpec((1, D), rep2), pl.BlockSpec((1, D), rep2), pl.BlockSpec((1, D), rep2)],
        out_specs=[pl.BlockSpec((2, t, D), lambda b, s: (0, rb(b, s), 0)),
                   pl.BlockSpec((8, D), rep2), pl.BlockSpec((1, D), rep2),
                   pl.BlockSpec((NB, BD, BD), rep3), pl.BlockSpec((NB, BD, BD), rep3),
                   pl.BlockSpec((1, D), rep2), pl.BlockSpec((1, D), rep2), pl.BlockSpec((1, D), rep2)],
        out_shape=[SDS((2, n, D), BF16), SDS((8, D), F32), SDS((1, D), F32),
                   SDS((NB, BD, BD), F32), SDS((NB, BD, BD), F32),
                   SDS((1, D), F32), SDS((1, D), F32), SDS((1, D), F32)],
        scratch_shapes=[pltpu.VMEM((t + 8, D), F32), pltpu.VMEM((t + 8, D), F32), pltpu.VMEM((t + 8, D), F32),
                        pltpu.VMEM((t, D), F32), pltpu.VMEM((t, D), F32), pltpu.VMEM((t, D), F32),
                        pltpu.VMEM((8, D), F32)],
        compiler_params=_params(56),
    )(z, z, z, h_all, h_all, dya, cw8, cb, wx, wa, bx, ba, lam)


HG_T = 512
HG_NC = HG_T // CHUNK
BNT_DIMS = (((2,), (2,)), ((0,), (0,)))
BNN_DIMS = (((2,), (1,)), ((0,), (0,)))
BTN_DIMS = (((1,), (1,)), ((0,), (0,)))


def _lower_bound(lg):
    m = jnp.max(lg, axis=0, keepdims=True)
    e = jnp.exp(lg - m)
    return e[0:1, :] / jnp.sum(e, axis=0, keepdims=True)


def _tri(upper):
    r = lax.broadcasted_iota(jnp.int32, (HG_NC, CHUNK, CHUNK), 1)
    c = lax.broadcasted_iota(jnp.int32, (HG_NC, CHUNK, CHUNK), 2)
    return (c >= r) if upper else (r >= c)


def _bdot(a, b, dims):
    return lax.dot_general(a, b, dims, preferred_element_type=F32)


def _tri_sums(upper, a):
    tri = _tri(upper).astype(BF16)
    a1 = a.astype(BF16)
    r1 = a - a1.astype(F32)
    a2 = r1.astype(BF16)
    a3 = (r1 - a2.astype(F32)).astype(BF16)
    return _bdot(tri, a1, BNN_DIMS) + (_bdot(tri, a2, BNN_DIMS) + _bdot(tri, a3, BNN_DIMS))


def _chunks(a):
    return a.reshape(HG_NC, CHUNK, BD)


def _hg_tile(q, fp, lb):
    q, fp = _chunks(q), _chunks(fp)
    sig = _sigmoid(fp)
    f = lb + (1.0 - lb) * sig
    log_f = jnp.log(f)
    k = 1.0 - f
    b = _tri_sums(False, log_f)
    b_mid = b[:, CHUNK // 2:CHUNK // 2 + 1, :]
    b_last = b[:, CHUNK - 1:CHUNK, :]
    sq = _sigmoid(q)
    qh = q * sq
    e_qi = jnp.exp(b - b_mid)
    e_ki = jnp.exp(b_mid - b)
    e_qs = jnp.exp(b)
    e_ks = jnp.exp(b_last - b)
    dc = jnp.exp(b_last)
    q_in = (qh * e_qi) * HG_SCALE
    k_in = k * e_ki
    q_st = (qh * e_qs) * HG_SCALE
    k_st = k * e_ks
    att = _bdot(q_in.astype(BF16), k_in.astype(BF16), BNT_DIMS)
    att = jnp.where(_tri(False), att, 0.0)
    return dict(q=q, sig=sig, f=f, k=k, sq=sq, e_qi=e_qi, e_ki=e_ki, e_qs=e_qs, e_ks=e_ks, dc=dc,
                q_in=q_in, k_in=k_in, q_st=q_st, k_st=k_st, att=att)


def _hgrn_fwd(z, lb_logits, hg_g, nb, s_len):
    n = nb * s_len
    t = HG_T
    ns = s_len // t
    nchunk = s_len // CHUNK

    def body(q_ref, f_ref, v_ref, gb_ref, lg_ref, g_ref, o_ref, yb_ref, st_ref, st):
        @pl.when(pl.program_id(1) == 0)
        def _():
            st[...] = jnp.zeros((NB, BD, BD), F32)

        def head(h, carry):
            cols = pl.ds(pl.multiple_of(h * BD, BD), BD)
            lb = _lower_bound(lg_ref[:, cols])
            ck = _hg_tile(q_ref[0, :, cols], f_ref[0, :, cols], lb)
            vb = _chunks(v_ref[0, :, cols]).astype(BF16)
            kv = _bdot(vb, ck["k_st"].astype(BF16), BTN_DIMS)
            states = [st[h]]
            for c in range(HG_NC):
                states.append(states[c] * ck["dc"][c] + kv[c])
            st[h] = states[HG_NC]
            s_in = jnp.stack(states[:HG_NC], axis=0)
            st_ref[h] = s_in
            o = (_bdot(ck["att"].astype(BF16), vb, BNN_DIMS)
                 + _bdot(ck["q_st"].astype(BF16), s_in.astype(BF16), BNT_DIMS))
            o_ref[:, cols] = o.reshape(t, BD)
            r = lax.rsqrt(jnp.mean(o * o, axis=-1, keepdims=True) + EPS)
            gb = _chunks(gb_ref[0, :, cols])
            yb_ref[:, cols] = (((o * r) * g_ref[...]) * (gb * _sigmoid(gb))).astype(BF16).reshape(t, BD)
            return carry

        lax.fori_loop(0, NB, head, 0, unroll=4)

    seg = lambda j: pl.BlockSpec((1, t, D), lambda b, s: (j, b * ns + s, 0))
    tile = pl.BlockSpec((t, D), lambda b, s: (b * ns + s, 0))
    return pl.pallas_call(
        body, name="hgrn_fwd", grid=(nb, ns),
        in_specs=[seg(2), seg(3), seg(4), seg(5),
                  pl.BlockSpec((2, D), lambda b, s: (0, 0)), pl.BlockSpec((1, BD), lambda b, s: (0, 0))],
        out_specs=[tile, tile, pl.BlockSpec((NB, HG_NC, BD, BD), lambda b, s: (b, s, 0, 0))],
        out_shape=[SDS((n, D), F32), SDS((n, D), BF16), SDS((nb * NB, nchunk, BD, BD), F32)],
        scratch_shapes=[pltpu.VMEM((NB, BD, BD), F32)],
        compiler_params=_params(56),
    )(z, z, z, z, lb_logits, hg_g)


def _hgrn_bwd(z, o_all, st_all, dyb, lb_logits, hg_g, nb, s_len):
    n = nb * s_len
    t = HG_T
    ns = s_len // t

    def body(q_ref, f_ref, v_ref, gb_ref, o_ref, st_ref, dyb_ref, lg_ref, g_ref,
             dz_ref, glg_ref, ghg_ref, dst, dlb):
        b, s = pl.program_id(0), pl.program_id(1)

        @pl.when((b == 0) & (s == 0))
        def _():
            ghg_ref[...] = jnp.zeros((1, BD), F32)
            dlb[...] = jnp.zeros((8, D), F32)

        @pl.when(s == 0)
        def _():
            dst[...] = jnp.zeros((NB, BD, BD), F32)

        g = g_ref[...]

        def head(h, carry):
            cols = pl.ds(pl.multiple_of(h * BD, BD), BD)
            lb = _lower_bound(lg_ref[:, cols])
            ck = _hg_tile(q_ref[0, :, cols], f_ref[0, :, cols], lb)
            q = ck["q"]
            vb = _chunks(v_ref[0, :, cols]).astype(BF16)
            gb = _chunks(gb_ref[0, :, cols])
            o = _chunks(o_ref[:, cols])
            dyb_v = _chunks(dyb_ref[:, cols])
            s_in = st_ref[h]

            sgb = _sigmoid(gb)
            r = lax.rsqrt(jnp.mean(o * o, axis=-1, keepdims=True) + EPS)
            ohat = o * r
            d_on = dyb_v * (gb * sgb)
            d_gb = dyb_v * (ohat * g) * (sgb * (1.0 + gb * (1.0 - sgb)))
            ghg_ref[...] += jnp.sum(jnp.sum(d_on * ohat, axis=1), axis=0, keepdims=True)
            tt = d_on * g
            d_o = r * (tt - ohat * jnp.mean(tt * ohat, axis=-1, keepdims=True))
            dob = d_o.astype(BF16)

            attb = ck["att"].astype(BF16)
            q_inb, k_inb = ck["q_in"].astype(BF16), ck["k_in"].astype(BF16)
            q_stb, k_stb = ck["q_st"].astype(BF16), ck["k_st"].astype(BF16)
            d_att = jnp.where(_tri(False), _bdot(dob, vb, BNT_DIMS), 0.0).astype(BF16)
            d_q_in = _bdot(d_att, k_inb, BNN_DIMS)
            d_k_in = _bdot(d_att, q_inb, BTN_DIMS)
            d_q_st = _bdot(dob, s_in.astype(BF16), BNN_DIMS)
            qdo = _bdot(dob, q_stb, BTN_DIMS)
            d_states = [None] * HG_NC + [dst[h]]
            for c in reversed(range(HG_NC)):
                d_states[c] = d_states[c + 1] * ck["dc"][c] + qdo[c]
            dst[h] = d_states[0]
            ds_out = jnp.stack(d_states[1:], axis=0)
            dsb = ds_out.astype(BF16)
            d_v = _bdot(attb, dob, BTN_DIMS) + _bdot(k_stb, dsb, BNT_DIMS)
            d_k_st = _bdot(vb, dsb, BNN_DIMS)
            d_dc = jnp.sum(ds_out * s_in, axis=1, keepdims=True)

            p_qi = d_q_in * ck["q_in"]
            p_ki = d_k_in * ck["k_in"]
            p_qs = d_q_st * ck["q_st"]
            p_ks = d_k_st * ck["k_st"]
            d_qh = (d_q_in * ck["e_qi"] + d_q_st * ck["e_qs"]) * HG_SCALE
            d_k = d_k_in * ck["e_ki"] + d_k_st * ck["e_ks"]
            d_b = (p_qi - p_ki) + (p_qs - p_ks)
            d_b_mid = jnp.sum(p_ki - p_qi, axis=1, keepdims=True)
            d_b_last = jnp.sum(p_ks, axis=1, keepdims=True) + d_dc * ck["dc"]
            rowi = lax.broadcasted_iota(jnp.int32, (HG_NC, CHUNK, BD), 1)
            d_b = d_b + jnp.where(rowi == CHUNK // 2, d_b_mid, 0.0) + jnp.where(rowi == CHUNK - 1, d_b_last, 0.0)
            d_logf = _tri_sums(True, d_b)
            d_f = d_logf / ck["f"] - d_k
            sig, sq = ck["sig"], ck["sq"]
            d_fp = d_f * (1.0 - lb) * (sig * (1.0 - sig))
            dlb[0:1, cols] += jnp.sum(jnp.sum(d_f * (1.0 - sig), axis=1), axis=0, keepdims=True)
            d_q = d_qh * (sq * (1.0 + q * (1.0 - sq)))
            dz_ref[0, :, cols] = d_q.astype(BF16).reshape(t, BD)
            dz_ref[1, :, cols] = d_fp.astype(BF16).reshape(t, BD)
            dz_ref[2, :, cols] = d_v.astype(BF16).reshape(t, BD)
            dz_ref[3, :, cols] = d_gb.astype(BF16).reshape(t, BD)
            return carry

        lax.fori_loop(0, NB, head, 0, unroll=2)

        @pl.when((b == nb - 1) & (s == ns - 1))
        def _():
            lb = _lower_bound(lg_ref[...])
            dl = dlb[0:1, :] * (lb * (1.0 - lb))
            glg_ref[0:1, :] = dl
            glg_ref[1:2, :] = -dl

    rb = lambda b, s: b * ns + (ns - 1 - s)
    seg = lambda j: pl.BlockSpec((1, t, D), lambda b, s: (j, rb(b, s), 0))
    tile = pl.BlockSpec((t, D), lambda b, s: (rb(b, s), 0))
    return pl.pallas_call(
        body, name="hgrn_bwd", grid=(nb, ns),
        in_specs=[seg(2), seg(3), seg(4), seg(5), tile,
                  pl.BlockSpec((NB, HG_NC, BD, BD), lambda b, s: (b, ns - 1 - s, 0, 0)),
                  tile, pl.BlockSpec((2, D), lambda b, s: (0, 0)), pl.BlockSpec((1, BD), lambda b, s: (0, 0))],
        out_specs=[pl.BlockSpec((4, t, D), lambda b, s: (0, rb(b, s), 0)),
                   pl.BlockSpec((2, D), lambda b, s: (0, 0)), pl.BlockSpec((1, BD), lambda b, s: (0, 0))],
        out_shape=[SDS((4, n, D), BF16), SDS((2, D), F32), SDS((1, BD), F32)],
        scratch_shapes=[pltpu.VMEM((NB, BD, BD), F32), pltpu.VMEM((8, D), F32)],
        compiler_params=_params(60),
    )(z, z, z, z, o_all, st_all, dyb, lb_logits, hg_g)


def _mid(ya, yb, z, b_merge, x2, tgt, fin_g, pa, pb, wo):
    n = x2.shape[0]
    tm = 256
    ni = n // tm

    def body(ya_ref, yb_ref, gma_ref, gmb_ref, bm_ref, x_ref, t_ref, fg_ref, pa_hbm, pb_hbm, wo_hbm,
             dx2_ref, dya_ref, dyb_ref, dgm_ref, loss_ref, gfg_ref, gbm_ref, gm_hbm,
             pa_v, pb_v, wo_v, gpa_v, gpb_v, gwo_v, sem):
        i = pl.program_id(0)
        by_owner = lambda g: g.reshape(NB, BD, D)
        loads = [pltpu.make_async_copy(src, dst, sem.at[k])
                 for k, (src, dst) in enumerate(((pa_hbm, pa_v), (pb_hbm, pb_v), (wo_hbm, wo_v)))]
        stores = [pltpu.make_async_copy(src, dst, sem.at[k])
                  for k, (src, dst) in enumerate((g, gm_hbm.at[:, pl.ds(slot * BD, BD), :])
                                                 for slot, g in enumerate((gpa_v, gpb_v, gwo_v)))]

        @pl.when(i == 0)
        def _():
            for cp in loads:
                cp.start()
            for ref in (gpa_v, gpb_v, gwo_v, loss_ref, gfg_ref, gbm_ref):
                ref[...] = jnp.zeros(ref.shape, F32)
            for cp in loads:
                cp.wait()

        ya_v = ya_ref[...]
        yb_v = yb_ref[...]
        out_a = jnp.dot(ya_v, pa_v[...], preferred_element_type=F32)
        out_b = jnp.dot(yb_v, pb_v[...], preferred_element_type=F32)
        bm = bm_ref[...]
        g_a = _sigmoid(gma_ref[0] + bm[:, 0:D])
        g_b = _sigmoid(gmb_ref[0] + bm[:, D:2 * D])
        mixed = g_a * out_a + g_b * out_b
        mixb = mixed.astype(BF16)
        xo = x_ref[...] + jnp.dot(mixb, wo_v[...], preferred_element_type=F32)
        r = lax.rsqrt(jnp.mean(xo * xo, axis=-1, keepdims=True) + EPS)
        xn = xo * r
        fg = fg_ref[...]
        e = xn * fg - t_ref[...]
        loss_ref[...] += 0.5 * jnp.sum(jnp.mean(e * e, axis=-1, keepdims=True))
        dy = e * (1.0 / D)
        gfg_ref[...] += jnp.sum(dy * xn, axis=0, keepdims=True)
        dxn = dy * fg
        dx2 = r * (dxn - xn * jnp.mean(dxn * xn, axis=-1, keepdims=True))
        dx2_ref[...] = dx2
        dx2b = dx2.astype(BF16)
        d_mixed = lax.dot_general(dx2b, wo_v[...], NT_DIMS, preferred_element_type=F32)
        gwo_v[...] += by_owner(lax.dot_general(mixb, dx2b, TN_DIMS, preferred_element_type=F32))
        d_oa = (d_mixed * g_a).astype(BF16)
        d_ob = (d_mixed * g_b).astype(BF16)
        dgm_a = (d_mixed * out_a) * (g_a * (1.0 - g_a))
        dgm_b = (d_mixed * out_b) * (g_b * (1.0 - g_b))
        gbm_ref[:, 0:D] += jnp.sum(dgm_a, axis=0, keepdims=True)
        gbm_ref[:, D:2 * D] += jnp.sum(dgm_b, axis=0, keepdims=True)
        dgm_ref[0] = dgm_a.astype(BF16)
        dgm_ref[1] = dgm_b.astype(BF16)
        dya_ref[...] = lax.dot_general(d_oa, pa_v[...], NT_DIMS, preferred_element_type=F32)
        dyb_ref[...] = lax.dot_general(d_ob, pb_v[...], NT_DIMS, preferred_element_type=F32)
        gpa_v[...] += by_owner(lax.dot_general(ya_v, d_oa, TN_DIMS, preferred_element_type=F32))
        gpb_v[...] += by_owner(lax.dot_general(yb_v, d_ob, TN_DIMS, preferred_element_type=F32))

        @pl.when(i == ni - 1)
        def _():
            for cp in stores:
                cp.start()
            for cp in stores:
                cp.wait()

    rows = pl.BlockSpec((tm, D), lambda i: (i, 0))
    rep = lambda shape: pl.BlockSpec(shape, lambda i: (0,) * len(shape))
    return pl.pallas_call(
        body, name="mid", grid=(ni,),
        in_specs=[rows, rows,
                  pl.BlockSpec((1, tm, D), lambda i: (6, i, 0)), pl.BlockSpec((1, tm, D), lambda i: (7, i, 0)),
                  rep((1, 2 * D)), rows, rows, rep((1, D)), ANY, ANY, ANY],
        out_specs=[rows, rows, rows, pl.BlockSpec((2, tm, D), lambda i: (0, i, 0)),
                   rep((8, BD)), rep((1, D)), rep((1, 2 * D)), ANY],
        out_shape=[SDS((n, D), F32), SDS((n, D), F32), SDS((n, D), F32), SDS((2, n, D), BF16),
                   SDS((8, BD), F32), SDS((1, D), F32), SDS((1, 2 * D), F32),
                   SDS((NB, MID_ROWS, D), F32)],
        scratch_shapes=[pltpu.VMEM((D, D), BF16)] * 3 + [pltpu.VMEM((NB, BD, D), F32)] * 3 + [pltpu.SemaphoreType.DMA((3,))],
        compiler_params=_params(60),
    )(ya, yb, z, z, b_merge, x2, tgt, fin_g, pa, pb, wo)


def _dz_specs(tm, ni, row_major):
    if row_major:
        ia = lambda i, j: (jnp.minimum(j, 1), i, 0)
        ib = lambda i, j: (jnp.clip(j - 2, 0, 3), i, 0)
        im = lambda i, j: (jnp.clip(j - 6, 0, 1), i, 0)
    else:
        last = ni - 1
        ia = lambda j, i: (jnp.minimum(j, 1), jnp.where(j < 2, i, last), 0)
        ib = lambda j, i: (jnp.clip(j - 2, 0, 3), jnp.where(j < 2, 0, jnp.where(j < 6, i, last)), 0)
        im = lambda j, i: (jnp.clip(j - 6, 0, 1), jnp.where(j < 6, 0, i), 0)
    return [pl.BlockSpec((1, tm, D), f) for f in (ia, ib, im)]


def _inproj_bwd_x(dza, dzb, dzm, w_all, x2, dx2, norm_g, after):
    n = x2.shape[0]
    tm = 512
    ni = n // tm

    def body(dza_ref, dzb_ref, dzm_ref, w_ref, x_ref, dx2_ref, g_ref, after_ref, gx_ref, gg_ref, acc):
        i, j = pl.program_id(0), pl.program_id(1)

        @pl.when((i == 0) & (j == 0))
        def _():
            gg_ref[...] = jnp.zeros((1, D), F32)

        @pl.when(j == 0)
        def _():
            acc[...] = jnp.zeros((tm, D), F32)

        def add(ref):
            acc[...] += lax.dot_general(ref[0], w_ref[0], NT_DIMS, preferred_element_type=F32)

        pl.when(j < 2)(lambda: add(dza_ref))
        pl.when((j >= 2) & (j < 6))(lambda: add(dzb_ref))
        pl.when(j >= 6)(lambda: add(dzm_ref))

        @pl.when(j == NB - 1)
        def _():
            x = x_ref[...]
            r = lax.rsqrt(jnp.mean(x * x, axis=-1, keepdims=True) + EPS)
            xn = x * r
            dh = acc[...]
            gg_ref[...] += jnp.sum(dh * xn, axis=0, keepdims=True)
            dxn = dh * g_ref[...]
            gx_ref[...] = dx2_ref[...] + r * (dxn - xn * jnp.mean(dxn * xn, axis=-1, keepdims=True))

    rows = pl.BlockSpec((tm, D), lambda i, j: (i, 0))
    return pl.pallas_call(
        body, name="inproj_bwd_x", grid=(ni, NB),
        in_specs=_dz_specs(tm, ni, True) + [pl.BlockSpec((1, D, D), lambda i, j: (j, 0, 0)), rows, rows,
                                             pl.BlockSpec((1, D), lambda i, j: (0, 0)), ANY],
        out_specs=[rows, pl.BlockSpec((1, D), lambda i, j: (0, 0))],
        out_shape=[SDS((n, D), F32), SDS((1, D), F32)],
        scratch_shapes=[pltpu.VMEM((tm, D), F32)],
        compiler_params=_params(48),
    )(dza, dzb, dzm, w_all, x2, dx2, norm_g, after)


def _inproj_bwd_w(dza, dzb, dzm, h_all, g_m):
    n = h_all.shape[0]
    tm = min(n, 2048)
    ni = n // tm

    def body(dza_ref, dzb_ref, dzm_ref, h_ref, gm_hbm, gw_ref, got_w, got_m, stage, send_sems, recv_sems):
        j, i = pl.program_id(0), pl.program_id(1)
        x, y, c = _place()
        sibling = (x, y, 1 - c)

        def send_w(q):
            return pltpu.make_async_remote_copy(
                src_ref=stage.at[q % 2], dst_ref=got_w.at[q], send_sem=send_sems.at[q], recv_sem=recv_sems.at[q],
                device_id=sibling, device_id_type=MESH)

        def send_m(q):
            return pltpu.make_async_remote_copy(
                src_ref=gm_hbm.at[2 * q + (1 - c)], dst_ref=got_m.at[q], send_sem=send_sems.at[4 + q],
                recv_sem=recv_sems.at[4 + q], device_id=sibling, device_id_type=MESH)

        @pl.when((j == 0) & (i == 0))
        def _():
            for q in range(4):
                send_m(q).start()

        @pl.when(i == 0)
        def _():
            gw_ref[...] = jnp.zeros((1, D, D), F32)

        def add(ref):
            gw_ref[0] += lax.dot_general(h_ref[...], ref[0], TN_DIMS, preferred_element_type=F32)

        pl.when(j < 2)(lambda: add(dza_ref))
        pl.when((j >= 2) & (j < 6))(lambda: add(dzb_ref))
        pl.when(j >= 6)(lambda: add(dzm_ref))

        for q in range(4):
            @pl.when((i == ni - 1) & (j == 2 * q + 1 - c))
            def _(q=q):
                if q >= 2:
                    send_w(q - 2).wait_send()
                stage[q % 2] = gw_ref[0].astype(BF16)
                send_w(q).start()

        @pl.when((j == NB - 1) & (i == ni - 1))
        def _():
            for q in (2, 3):
                send_w(q).wait_send()
            for q in range(4):
                send_w(q).wait_recv()
                send_m(q).wait_send()
                send_m(q).wait_recv()

    return pl.pallas_call(
        body, name="inproj_bwd_w", grid=(NB, ni),
        in_specs=_dz_specs(tm, ni, False) + [pl.BlockSpec((tm, D), lambda j, i: (i, 0)), ANY],
        out_specs=[pl.BlockSpec((1, D, D), lambda j, i: (j, 0, 0)), ANY, ANY],
        out_shape=[SDS((NB, D, D), F32), SDS((4, D, D), BF16), SDS((4,) + g_m.shape[1:], F32)],
        scratch_shapes=[pltpu.VMEM((2, D, D), BF16), pltpu.SemaphoreType.DMA((8,)), pltpu.SemaphoreType.DMA((8,))],
        compiler_params=_params(58),
    )(dza, dzb, dzm, h_all, g_m)


def _adamw(w, g, m, v):
    rows, cols = w.shape
    tr = _row_tile(rows)

    spec = pl.BlockSpec((tr, cols), lambda i: (i, 0))
    return pl.pallas_call(
        functools.partial(_adam_refs), name="adamw", grid=(rows // tr,), in_specs=[spec] * 4, out_specs=[spec] * 3,
        out_shape=[SDS((rows, cols), F32)] * 3, compiler_params=_params(32),
    )(w, g, m, v)


def _adam_refs(w_ref, g_ref, m_ref, v_ref, d_ref, nm_ref, nv_ref):
    gv = g_ref[...]
    nm = ADAM_B1 * m_ref[...] + (1.0 - ADAM_B1) * gv
    nv = ADAM_B2 * v_ref[...] + (1.0 - ADAM_B2) * (gv * gv)
    m_hat = nm / (1.0 - ADAM_B1 ** ADAM_STEP)
    v_hat = nv / (1.0 - ADAM_B2 ** ADAM_STEP)
    d_ref[...] = -ADAM_LR * (m_hat / (jnp.sqrt(v_hat) + ADAM_EPS) + ADAM_WD * w_ref[...])
    nm_ref[...] = nm
    nv_ref[...] = nv


def _adamw_small(ws, gs, ms, vs):
    k = len(ws)

    def body(*refs):
        for i in range(k):
            _adam_refs(*[refs[part * k + i] for part in range(7)])

    shapes = [SDS(w.shape, F32) for w in ws]
    out = pl.pallas_call(body, name="adamw_small", out_shape=shapes * 3, compiler_params=_params(32))(*ws, *gs, *ms, *vs)
    return out[:k], out[k:2 * k], out[2 * k:]


def _allgather(blocks, dtypes, name):
    na = len(blocks)

    def body(*refs):
        ins, outs, stages = refs[:na], refs[na:2 * na], refs[2 * na:3 * na]
        send_sems, recv_sems, local_sems = refs[3 * na:]
        x, y, c = _place()
        me, sibling = (x, y, c), (x, y, 1 - c)
        chips = [(1 - x, y), (x, 1 - y), (1 - x, 1 - y)]
        blk = lambda p: 4 * p[0] + 2 * p[1] + p[2]

        def copy(a, k, block, to, src=None):
            return pltpu.make_async_remote_copy(
                src_ref=outs[a].at[blk(block)] if src is None else src, dst_ref=outs[a].at[blk(block)],
                send_sem=send_sems.at[7 * a + k], recv_sem=recv_sems.at[7 * a + k],
                device_id=to, device_id_type=MESH)

        mine, first, passed = [], [], []
        for a in range(na):
            stages[a][...] = ins[a][...].astype(dtypes[a])
            mine.append(pltpu.make_async_copy(stages[a], outs[a].at[blk(me)], local_sems.at[a]))
            mine[-1].start()
            first.append(copy(a, 0, me, sibling, src=stages[a]))
            first += [copy(a, 1 + j, me, (*chip, c), src=stages[a]) for j, chip in enumerate(chips)]
        for cp in first:
            cp.start()
        for j, chip in enumerate(chips):
            for a in range(na):
                copy(a, 1 + j, (*chip, c), me).wait_recv()
                passed.append(copy(a, 4 + j, (*chip, c), sibling))
                passed[-1].start()
        for a in range(na):
            copy(a, 0, sibling, me).wait_recv()
            for j, chip in enumerate(chips):
                copy(a, 4 + j, (*chip, 1 - c), me).wait_recv()
        for cp in first + passed:
            cp.wait_send()
        for cp in mine:
            cp.wait()

    return pl.pallas_call(
        body, name=name,
        in_specs=[pl.BlockSpec(memory_space=pltpu.VMEM)] * na, out_specs=[ANY] * na,
        out_shape=[SDS((NB,) + b.shape, dt) for b, dt in zip(blocks, dtypes)],
        scratch_shapes=[pltpu.VMEM(b.shape, dt) for b, dt in zip(blocks, dtypes)]
        + [pltpu.SemaphoreType.DMA((7 * na,)), pltpu.SemaphoreType.DMA((7 * na,)), pltpu.SemaphoreType.DMA((na,))],
        compiler_params=_params(40),
    )(*blocks)


HBM = pl.BlockSpec(memory_space=pltpu.HBM)
SEMS = pl.BlockSpec(memory_space=pltpu.SEMAPHORE)
EFFECT = pltpu.SideEffectType.DATAFLOW_SIDE_EFFECTING


def _chip_copies(srcs, lands, send_sems, recv_sems):
    x, y, c = _place()
    return [pltpu.make_async_remote_copy(
        src_ref=srcs[a].at[slot], dst_ref=lands[a].at[slot],
        send_sem=send_sems.at[3 * a + slot], recv_sem=recv_sems.at[3 * a + slot],
        device_id=(px, py, c), device_id_type=MESH)
        for a in range(len(srcs)) for slot, (px, py) in enumerate(_other_chips(x, y))]


def _split_start(name, copies, per_array, srcs, lands, after=None):
    na = len(srcs)

    def body(*refs):
        send_sems, recv_sems = refs[-2 * na - 3], refs[-2 * na - 2]
        for cp in copies(refs[:na], refs[na:2 * na], send_sems, recv_sems):
            cp.start()
        refs[-1][...] = jnp.zeros_like(refs[-1])

    hbm = lambda a: pltpu.HBM(a.shape, a.dtype)
    pin = lambda a: pltpu.with_memory_space_constraint(a, pltpu.HBM)
    out = pl.pallas_call(
        body, name=name,
        out_shape=(pltpu.SemaphoreType.DMA((per_array * na,)), pltpu.SemaphoreType.DMA((per_array * na,)),
                   *[hbm(a) for a in srcs], *[hbm(a) for a in lands], SDS((8, BD), F32)),
        in_specs=[HBM] * (2 * na) + ([] if after is None else [ANY]),
        out_specs=(SEMS, SEMS, *[HBM] * (2 * na), pl.BlockSpec(memory_space=pltpu.VMEM)),
        input_output_aliases={i: 2 + i for i in range(2 * na)},
        compiler_params=pltpu.CompilerParams(has_side_effects=EFFECT),
    )(*[pin(a) for a in srcs], *[pin(a) for a in lands], *([] if after is None else [after]))
    return out[0], out[1], out[2:2 + na], out[2 + na:2 + 2 * na], out[-1]


def _split_wait(name, copies, started, after):
    send_sems, recv_sems, srcs, lands, _ = started
    na = len(srcs)

    def body(*refs):
        waits = copies(refs[:na], refs[na:2 * na], refs[2 * na], refs[2 * na + 1])
        for cp in waits:
            cp.wait_send()
        for cp in waits:
            cp.wait_recv()

    hbm = lambda a: pltpu.HBM(a.shape, a.dtype)
    out = pl.pallas_call(
        body, name=name,
        out_shape=(*[hbm(a) for a in srcs], *[hbm(a) for a in lands]),
        in_specs=[HBM] * (2 * na) + [SEMS, SEMS, ANY],
        out_specs=tuple([HBM] * (2 * na)),
        input_output_aliases={i: i for i in range(2 * na)},
        compiler_params=pltpu.CompilerParams(has_side_effects=EFFECT),
    )(*srcs, *lands, send_sems, recv_sems, after)
    return out[na:]


def _add_sibling(place, g, a_in):
    _, r, cols = g.shape
    tr = _row_tile(r)

    def chip(k, pr):
        qx = pr[0] if k in (1, 3) else 1 - pr[0]
        qy = pr[1] if k in (0, 3) else 1 - pr[1]
        return 2 * qx + qy

    def body(place_ref, *refs):
        g_refs, a_refs, (out_ref, own_ref) = refs[0:4], refs[4:8], refs[8:10]
        for k in range(3):
            out_ref[k] = (g_refs[k][0] + a_refs[k][0].astype(F32)).astype(BF16)
        own_ref[...] = g_refs[3][0] + a_refs[3][0].astype(F32)

    mine = lambda k: pl.BlockSpec((1, tr, cols), lambda i, pr: (2 * chip(k, pr) + pr[2], i, 0))
    theirs = lambda k: pl.BlockSpec((1, tr, cols), lambda i, pr: (chip(k, pr), i, 0))
    return pl.pallas_call(
        body, name="add_sibling",
        grid_spec=pltpu.PrefetchScalarGridSpec(
            num_scalar_prefetch=1, grid=(r // tr,),
            in_specs=[mine(k) for k in range(4)] + [theirs(k) for k in range(4)],
            out_specs=[pl.BlockSpec((3, tr, cols), lambda i, pr: (0, i, 0)),
                       pl.BlockSpec((tr, cols), lambda i, pr: (i, 0))]),
        out_shape=[SDS((3, r, cols), BF16), SDS((r, cols), F32)], compiler_params=_params(48),
    )(place, *[g] * 4, *[a_in] * 4)


def _add_chips(own, b_in):
    r, cols = own.shape
    tr = _row_tile(r)

    def body(p_ref, b0_ref, b1_ref, b2_ref, o_ref):
        o_ref[...] = ((p_ref[...] + b0_ref[0].astype(F32)) + b1_ref[0].astype(F32)) + b2_ref[0].astype(F32)

    slot = lambda k: pl.BlockSpec((1, tr, cols), lambda i: (k, i, 0))
    spec = pl.BlockSpec((tr, cols), lambda i: (i, 0))
    return pl.pallas_call(
        body, name="add_chips", grid=(r // tr,), in_specs=[spec, slot(0), slot(1), slot(2)], out_specs=spec,
        out_shape=SDS((r, cols), F32), compiler_params=_params(32),
    )(own, b_in, b_in, b_in)


VEC_NAMES = ("b_merge", "conv_b", "rg_bx", "rg_ba", "rg_lambda", "hg_lb_logits", "hg_norm_g", "final_norm_g")
REP_NAMES = ("rg_wx", "rg_wa", "norm_g") + VEC_NAMES
SMALL_AT = 3 * BD
SMALL_ROWS = 48
MID_ROWS = 448


def _sum_blocks(parts):
    def body(p_ref, o_ref):
        acc = p_ref[0]
        for k in range(1, NB):
            acc = acc + p_ref[k]
        o_ref[...] = acc

    return pl.pallas_call(body, name="sum_blocks", out_shape=SDS(parts.shape[1:], F32))(parts)


def _pack_rows(arrays, width, row_multiple=8):
    flat = jnp.concatenate([a.reshape(-1) for a in arrays])
    rows = -(-flat.shape[0] // width)
    rows = -(-rows // row_multiple) * row_multiple
    return jnp.pad(flat, (0, rows * width - flat.shape[0])).reshape(rows, width)


def _unpack(flat, like):
    out, off = [], 0
    for a in like:
        out.append(flat[off:off + a.size].reshape(a.shape))
        off += a.size
    return out


def kernel(x, w_in, b_merge, conv_w, conv_b, rg_wx, rg_bx, rg_wa, rg_ba, rg_lambda, hg_lb_logits, hg_norm_g, proj_a, proj_b, w_out, norm_g, final_norm_g, loss_target, m_w_in, m_b_merge, m_conv_w, m_conv_b, m_rg_wx, m_rg_bx, m_rg_wa, m_rg_ba, m_rg_lambda, m_hg_lb_logits, m_hg_norm_g, m_proj_a, m_proj_b, m_w_out, m_norm_g, m_final_norm_g, v_w_in, v_b_merge, v_conv_w, v_conv_b, v_rg_wx, v_rg_bx, v_rg_wa, v_rg_ba, v_rg_lambda, v_hg_lb_logits, v_hg_norm_g, v_proj_a, v_proj_b, v_w_out, v_norm_g, v_final_norm_g):
    weights = dict(w_in=w_in, b_merge=b_merge, conv_w=conv_w, conv_b=conv_b, rg_wx=rg_wx, rg_bx=rg_bx, rg_wa=rg_wa,
                   rg_ba=rg_ba, rg_lambda=rg_lambda, hg_lb_logits=hg_lb_logits, hg_norm_g=hg_norm_g, proj_a=proj_a,
                   proj_b=proj_b, w_out=w_out, norm_g=norm_g, final_norm_g=final_norm_g)
    mom1 = dict(w_in=m_w_in, b_merge=m_b_merge, conv_w=m_conv_w, conv_b=m_conv_b, rg_wx=m_rg_wx, rg_bx=m_rg_bx,
                rg_wa=m_rg_wa, rg_ba=m_rg_ba, rg_lambda=m_rg_lambda, hg_lb_logits=m_hg_lb_logits,
                hg_norm_g=m_hg_norm_g, proj_a=m_proj_a, proj_b=m_proj_b, w_out=m_w_out, norm_g=m_norm_g,
                final_norm_g=m_final_norm_g)
    mom2 = dict(w_in=v_w_in, b_merge=v_b_merge, conv_w=v_conv_w, conv_b=v_conv_b, rg_wx=v_rg_wx, rg_bx=v_rg_bx,
                rg_wa=v_rg_wa, rg_ba=v_rg_ba, rg_lambda=v_rg_lambda, hg_lb_logits=v_hg_lb_logits,
                hg_norm_g=v_hg_norm_g, proj_a=v_proj_a, proj_b=v_proj_b, w_out=v_w_out, norm_g=v_norm_g,
                final_norm_g=v_final_norm_g)
    order = list(weights)
    nb, s_len, _ = x.shape
    n = nb * s_len
    px, py, pc = _place()
    place = jnp.stack([px, py, pc]).astype(jnp.int32)

    x2 = x.reshape(n, D)
    cw_blk = jnp.pad(conv_w[0], ((0, 4), (0, 0)))
    order_ids = jnp.stack([_block_id(p) for p in _arrival_order(px, py, pc)]).astype(jnp.int32)
    z, h_all, w_all, pa_all, pb_all, wo_all, cw_all = _gather_inproj(
        order_ids, x2, norm_g, [w_in[0], proj_a[0], proj_b[0], w_out[0], cw_blk], [BF16, BF16, BF16, BF16, F32])
    pa_full, pb_full, wo_full = (a.reshape(D, D) for a in (pa_all, pb_all, wo_all))
    cw8 = cw_all.transpose(1, 0, 2).reshape(8, D)
    wx_b, wa_b = rg_wx[0].astype(BF16), rg_wa[0].astype(BF16)
    cb, bx, ba = conv_b, rg_bx.reshape(1, D), rg_ba.reshape(1, D)
    fin_g = final_norm_g.reshape(1, D)

    hlru, ya = _lru_fwd(z, cw8, cb, wx_b, wa_b, bx, ba, rg_lambda, nb, s_len)
    o_all, yb, st_all = _hgrn_fwd(z, hg_lb_logits, hg_norm_g, nb, s_len)

    (dx2, dya, dyb, dzm, loss_acc, g_fin, g_bm, g_mid) = _mid(
        ya, yb, z, b_merge, x2, loss_target.reshape(n, D), fin_g, pa_full, pb_full, wo_full)
    dzb, g_lg, g_hg = _hgrn_bwd(z, o_all, st_all, dyb, hg_lb_logits, hg_norm_g, nb, s_len)
    dza, g_cw8, g_cb, g_wx, g_wa, g_bx, g_ba, g_lam = _lru_bwd(
        z, hlru, dya, cw8, cb, wx_b, wa_b, bx, ba, rg_lambda, nb, s_len)

    part = dict(b_merge=g_bm, conv_b=g_cb, rg_bx=g_bx, rg_ba=g_ba, rg_lambda=g_lam, hg_lb_logits=g_lg,
                hg_norm_g=g_hg, final_norm_g=g_fin)
    vec = _pack_rows([part[k] for k in VEC_NAMES], BD)
    vec = jnp.pad(vec, ((0, 16 * NB - vec.shape[0]), (0, 0))).reshape(NB, 2, D)
    rows8 = lambda a: jnp.pad(a, ((0, 0), (0, 8 - a.shape[1]), (0, 0)))
    small = jnp.concatenate([g_wx.reshape(NB, 16, D), g_wa.reshape(NB, 16, D),
                             rows8(g_cw8.reshape(8, NB, BD).transpose(1, 0, 2).reshape(NB, 1, D)), rows8(vec),
                             jnp.zeros((NB, MID_ROWS - SMALL_AT - SMALL_ROWS, D), F32)], axis=1)
    g_m = lax.dynamic_update_slice(g_mid, small, (0, SMALL_AT, 0))
    g_w, w_from_sibling, m_from_sibling = _inproj_bwd_w(dza, dzb, dzm, h_all, g_m)
    w_out_bf, w_own = _add_sibling(place, g_w, w_from_sibling)
    m_out_bf, m_own = _add_sibling(place, g_m, m_from_sibling)
    outgoing = [w_out_bf, m_out_bf]
    chip_sums = _split_start("rs_chips_start", _chip_copies, 3, outgoing, [lax.empty(a.shape, a.dtype) for a in outgoing])
    grad_x, g_ng = _inproj_bwd_x(dza, dzb, dzm, w_all, x2, dx2, norm_g, chip_sums[-1])
    from_chips = _split_wait("rs_chips_wait", _chip_copies, chip_sums, grad_x)
    r_w = _add_chips(w_own, from_chips[0])
    r_m = _add_chips(m_own, from_chips[1])
    row = lax.broadcasted_iota(jnp.int32, (8, D), 0)
    mine = jnp.where(row == 0, g_ng, jnp.where(row == 1, loss_acc[0:1, 0:1], 0.0))
    tail = jnp.concatenate([r_m[SMALL_AT:SMALL_AT + SMALL_ROWS], mine], axis=0)
    (tail_all,) = _allgather([tail], [F32], "gather_small_grads")
    summed = _sum_blocks(tail_all[:, SMALL_ROWS:SMALL_ROWS + 8])

    grads = dict(w_in=r_w.reshape(1, D, D),
                 proj_a=r_m[0:BD].reshape(1, BD, D), proj_b=r_m[BD:2 * BD].reshape(1, BD, D),
                 w_out=r_m[2 * BD:3 * BD].reshape(1, BD, D),
                 conv_w=r_m[SMALL_AT + 32].reshape(8, BD)[0:4].reshape(1, 4, BD),
                 rg_wx=tail_all[:, 0:16].reshape(1, NB, BD, BD), rg_wa=tail_all[:, 16:32].reshape(1, NB, BD, BD),
                 norm_g=summed[0:1])
    vec_all = tail_all[:, 40:42].reshape(-1)
    for k, gk in zip(VEC_NAMES, _unpack(vec_all, [weights[k] for k in VEC_NAMES])):
        grads[k] = gk

    delta, new_m, new_v = {}, {}, {}
    flat2 = lambda a: a.reshape(-1, a.shape[-1])
    d_k, m_k, v_k = _adamw(*[flat2(t["w_in"]) for t in (weights, grads, mom1, mom2)])
    delta["w_in"], new_m["w_in"], new_v["w_in"] = (a.reshape(w_in.shape) for a in (d_k, m_k, v_k))
    rep = list(REP_NAMES) + ["conv_w", "proj_a", "proj_b", "w_out"]
    outs = _adamw_small(*[[flat2(t[k]) for k in rep] for t in (weights, grads, mom1, mom2)])
    for tgt, arrays in zip((delta, new_m, new_v), outs):
        for k, a in zip(rep, arrays):
            tgt[k] = a.reshape(weights[k].shape)

    return (summed[1, 0], grad_x.reshape(x.shape), *[grads[k] for k in order], *[delta[k] for k in order],
            *[new_m[k] for k in order], *[new_v[k] for k in order])
```

```python
import functools

import jax
import jax.numpy as jnp
from jax import lax
from jax.experimental import pallas as pl
from jax.experimental.pallas import tpu as pltpu

F32 = jnp.float32
BF16 = jnp.bfloat16
SDS = jax.ShapeDtypeStruct
MESH = pl.DeviceIdType.MESH
ANY = pl.BlockSpec(memory_space=pl.ANY)

D = 1024
NB = 8
BD = D // NB
CHUNK = 64
EPS = 1e-6
LRU_C = 8.0
HG_SCALE = BD ** -0.5
ADAM_LR, ADAM_B1, ADAM_B2, ADAM_EPS, ADAM_WD, ADAM_STEP = 0.001, 0.9, 0.999, 1e-08, 0.01, 10

NT_DIMS = (((1,), (1,)), ((), ()))
TN_DIMS = (((0,), (0,)), ((), ()))


def _params(vmem_mib):
    return pltpu.CompilerParams(vmem_limit_bytes=vmem_mib << 20)


def _row_tile(rows, most=256):
    assert rows % 8 == 0
    return max(t for t in range(8, min(rows, most) + 1, 8) if rows % t == 0)


def _sigmoid(v):
    return 0.5 * (jnp.tanh(0.5 * v) + 1.0)


def _groups(v):
    return v.reshape(v.shape[0] // 8, 8, v.shape[1])


def _softplus_neg(lam):
    t = -lam
    e = jnp.exp(-jnp.abs(t))
    w = 1.0 + e
    d = w - 1.0
    l1p = jnp.where(d == 0.0, e, jnp.log(w) * (e / jnp.where(d == 0.0, 1.0, d)))
    return jnp.maximum(t, 0.0) + l1p


def _place():
    return lax.axis_index("x"), lax.axis_index("y"), lax.axis_index("c")


def _other_chips(x, y):
    return [(1 - x, y), (x, 1 - y), (1 - x, 1 - y)]


def _block_id(p):
    return 4 * p[0] + 2 * p[1] + p[2]


def _arrival_order(x, y, c):
    near, far, diag = _other_chips(x, y)
    return [(x, y, c), (x, y, 1 - c), (*near, c), (*far, c), (*near, 1 - c), (*far, 1 - c), (*diag, c), (*diag, 1 - c)]


def _gather_inproj(order_ids, x2, norm_g, blocks, dtypes):
    na = len(blocks)
    n = x2.shape[0]
    tm = min(n, 1024)
    ni = n // tm

    def body(order_ref, x_ref, g_ref, *refs):
        ins, (z_ref, h_ref), outs = refs[:na], refs[na:na + 2], refs[na + 2:2 * na + 2]
        stages = refs[2 * na + 2:3 * na + 2]
        h_full, wbuf, send_sems, recv_sems, local_sems, wsems, hsem = refs[3 * na + 2:]
        j, i = pl.program_id(0), pl.program_id(1)
        x, y, c = _place()
        me, sibling = (x, y, c), (x, y, 1 - c)
        chips = _other_chips(x, y)
        small = range(1, na)

        def copy(a, k, block, to, src=None):
            return pltpu.make_async_remote_copy(
                src_ref=outs[a].at[_block_id(block)] if src is None else src, dst_ref=outs[a].at[_block_id(block)],
                send_sem=send_sems.at[7 * a + k], recv_sem=recv_sems.at[7 * a + k],
                device_id=to, device_id_type=MESH)

        def local(a):
            return pltpu.make_async_copy(stages[a], outs[a].at[_block_id(me)], local_sems.at[a])

        def landed(a, slot):
            copy(a, 1 + slot, (*chips[slot], c), me).wait_recv()
            copy(a, 4 + slot, (*chips[slot], c), sibling).start()
            if slot < 2:
                @pl.when(c == slot)
                def _():
                    copy(a, 3, (*chips[slot], c), (*chips[1 - slot], c)).start()

        def diagonal_and_small():
            landed(0, 2)
            for a in small:
                landed(a, 0)
                landed(a, 1)

        def passed_on(a, slot):
            copy(a, 4 + slot, (*chips[slot], 1 - c), me).wait_recv()

        @pl.when((j == 0) & (i == 0))
        def _():
            for a in range(na):
                stages[a][...] = ins[a][...].astype(dtypes[a])
                local(a).start()
            for a in range(na):
                copy(a, 0, me, sibling, src=stages[a]).start()
                for slot, chip in enumerate(chips[:2]):
                    copy(a, 1 + slot, me, (*chip, c), src=stages[a]).start()

        @pl.when(j == 0)
        def _():
            xv = x_ref[...]
            r = lax.rsqrt(jnp.mean(xv * xv, axis=-1, keepdims=True) + EPS)
            hb = ((xv * r) * g_ref[...]).astype(BF16)
            h_full[pl.ds(pl.multiple_of(i * tm, tm), tm), :] = hb

        save_h = pltpu.make_async_copy(h_full, h_ref, hsem)
        pl.when((j == 0) & (i == ni - 1))(save_h.start)

        steps = [
            lambda: local(0).wait(),
            lambda: copy(0, 0, sibling, me).wait_recv(),
            lambda: landed(0, 0),
            lambda: landed(0, 1),
            lambda: passed_on(0, 0),
            lambda: passed_on(0, 1),
            diagonal_and_small,
            lambda: passed_on(0, 2),
        ]
        def w_load(k):
            return pltpu.make_async_copy(outs[0].at[order_ref[k]], wbuf.at[k % 2], wsems.at[k % 2])

        for k, step in enumerate(steps):
            @pl.when((j == 0) & (i == 0) if k == 0 else (j == k - 1) & (i == ni - 1))
            def _(k=k, step=step):
                step()
                w_load(k).start()

        pl.when(i == 0)(lambda: w_load(j).wait())
        z_ref[0] = jnp.dot(h_full[pl.ds(pl.multiple_of(i * tm, tm), tm), :], wbuf[j % 2], preferred_element_type=F32)

        @pl.when((j == NB - 1) & (i == ni - 1))
        def _():
            save_h.wait()
            for a in small:
                landed(a, 2)
            for a in small:
                local(a).wait()
                copy(a, 0, sibling, me).wait_recv()
                for slot in range(3):
                    passed_on(a, slot)
            for a in range(na):
                copy(a, 0, me, sibling, src=stages[a]).wait_send()
                for slot, chip in enumerate(chips):
                    copy(a, 1 + slot, me, (*chip, c), src=stages[a]).wait_send()
                    copy(a, 4 + slot, (*chip, c), sibling).wait_send()

    rows_once = lambda j, i, order: (jnp.where(j == 0, i, ni - 1), 0)
    vmem = pl.BlockSpec(memory_space=pltpu.VMEM)
    return pl.pallas_call(
        body, name="gather_inproj",
        grid_spec=pltpu.PrefetchScalarGridSpec(
            num_scalar_prefetch=1, grid=(NB, ni),
            in_specs=[pl.BlockSpec((tm, D), rows_once), pl.BlockSpec((1, D), lambda j, i, order: (0, 0))] + [vmem] * na,
            out_specs=[pl.BlockSpec((1, tm, D), lambda j, i, order: (order[j], i, 0)), ANY] + [ANY] * na,
            scratch_shapes=[pltpu.VMEM(b.shape, dt) for b, dt in zip(blocks, dtypes)]
            + [pltpu.VMEM((n, D), BF16), pltpu.VMEM((2, D, D), BF16),
               pltpu.SemaphoreType.DMA((7 * na,)), pltpu.SemaphoreType.DMA((7 * na,)),
               pltpu.SemaphoreType.DMA((na,)), pltpu.SemaphoreType.DMA((2,)), pltpu.SemaphoreType.DMA(())]),
        out_shape=[SDS((NB, n, D), F32), SDS((n, D), BF16)] + [SDS((NB,) + b.shape, dt) for b, dt in zip(blocks, dtypes)],
        compiler_params=_params(56),
    )(order_ids, x2, norm_g, *blocks)


LRU_T = 256


def _shifted(groups, shifts):
    row = lax.broadcasted_iota(jnp.int32, (groups.shape[0] - 1,) + groups.shape[1:], 1)
    out = []
    for s in shifts:
        y = pltpu.roll(groups, s % 8, 1)
        moved = jnp.where(row >= s, y[1:], y[:-1]) if s > 0 else jnp.where(row < 8 + s, y[:-1], y[1:])
        out.append(moved.reshape(-1, groups.shape[2]))
    return out


def _conv(taps, cw, cb):
    acc = taps[0] * cw[0:1, :] + taps[1] * cw[1:2, :]
    acc = acc + taps[2] * cw[2:3, :]
    acc = acc + taps[3] * cw[3:4, :]
    return cb + acc


def _lru_gates(xa, wx_ref, wa_ref, bx, ba, lam):
    xab = xa.astype(BF16)
    pis, prs = [], []
    for h in range(NB):
        xs = xab[:, h * BD:(h + 1) * BD]
        pis.append(jnp.dot(xs, wx_ref[h], preferred_element_type=F32))
        prs.append(jnp.dot(xs, wa_ref[h], preferred_element_type=F32))
    gi = _sigmoid(jnp.concatenate(pis, axis=1) + bx)
    gr = _sigmoid(jnp.concatenate(prs, axis=1) + ba)
    sp = _softplus_neg(lam)
    log_a = (-LRU_C * gr) * sp
    a = jnp.exp(log_a)
    mult = jnp.sqrt(-jnp.tanh(log_a) * (a * a + 1.0))
    return xab, gi, gr, sp, a, mult


def _lru_fwd(z, cw8, cb, wx, wa, bx, ba, lam, nb, s_len):
    n = nb * s_len
    t = LRU_T
    ns = s_len // t

    def body(xp_ref, ga_ref, cw_ref, cb_ref, wx_ref, wa_ref, bx_ref, ba_ref, lam_ref,
             h_ref, ya_ref, ext, a_s, u_s, carry):
        @pl.when(pl.program_id(1) == 0)
        def _():
            ext[0:8, :] = jnp.zeros((8, D), F32)
            carry[...] = jnp.zeros((8, D), F32)

        xp = xp_ref[0]
        ext[8:8 + t, :] = xp
        xa = _conv(_shifted(_groups(ext[...]), (3, 2, 1)) + [xp], cw_ref[...], cb_ref[...])
        ext[0:8, :] = xp[t - 8:t, :]
        _, gi, _, _, a, mult = _lru_gates(xa, wx_ref, wa_ref, bx_ref[...], ba_ref[...], lam_ref[...])
        u = (mult * gi) * xa
        a, u = _groups(a), _groups(u)
        row = lax.broadcasted_iota(jnp.int32, a.shape, 1)
        for sh in (1, 2, 4):
            a_sh = pltpu.roll(a, sh, 1)
            u_sh = pltpu.roll(u, sh, 1)
            m = row >= sh
            u = jnp.where(m, a * u_sh + u, u)
            a = jnp.where(m, a * a_sh, a)
        a_s[...] = a.reshape(t, D)
        u_s[...] = u.reshape(t, D)

        def step(g, c):
            r = pl.multiple_of(g * 8, 8)
            hg = u_s[pl.ds(r, 8), :] + a_s[pl.ds(r, 8), :] * c
            h_ref[pl.ds(r, 8), :] = hg
            return hg[7:8, :]

        c_out = lax.fori_loop(0, t // 8, step, carry[0:1, :], unroll=4)
        carry[0:1, :] = c_out
        ga = ga_ref[0]
        ya_ref[...] = (h_ref[...] * (ga * _sigmoid(ga))).astype(BF16)

    row_map = lambda b, s: (b * ns + s, 0)
    rep2 = lambda b, s: (0, 0)
    rep3 = lambda b, s: (0, 0, 0)
    return pl.pallas_call(
        body, name="lru_fwd", grid=(nb, ns),
        in_specs=[pl.BlockSpec((1, t, D), lambda b, s: (0, b * ns + s, 0)),
                  pl.BlockSpec((1, t, D), lambda b, s: (1, b * ns + s, 0)),
                  pl.BlockSpec((8, D), rep2), pl.BlockSpec((1, D), rep2),
                  pl.BlockSpec((NB, BD, BD), rep3), pl.BlockSpec((NB, BD, BD), rep3),
                  pl.BlockSpec((1, D), rep2), pl.BlockSpec((1, D), rep2), pl.BlockSpec((1, D), rep2)],
        out_specs=[pl.BlockSpec((t, D), row_map), pl.BlockSpec((t, D), row_map)],
        out_shape=[SDS((n, D), F32), SDS((n, D), BF16)],
        scratch_shapes=[pltpu.VMEM((t + 8, D), F32), pltpu.VMEM((t, D), F32), pltpu.VMEM((t, D), F32),
                        pltpu.VMEM((8, D), F32)],
        compiler_params=_params(48),
    )(z, z, cw8, cb, wx, wa, bx, ba, lam)


def _lru_bwd(z, h_all, dya, cw8, cb, wx, wa, bx, ba, lam, nb, s_len):
    n = nb * s_len
    t = LRU_T
    ns = s_len // t
    t8 = t // 8

    def body(xp_ref, xph_ref, ga_ref, h_ref, hh_ref, dya_ref, cw_ref, cb_ref, wx_ref, wa_ref, bx_ref, ba_ref,
             lam_ref, dz_ref, gcw_ref, gcb_ref, gwx_ref, gwa_ref, gbx_ref, gba_ref, glam_ref,
             ext, hext, dext, a_s, u_s, dh_s, carry):
        b, s = pl.program_id(0), pl.program_id(1)
        first_tile = s == ns - 1

        @pl.when((b == 0) & (s == 0))
        def _():
            for ref in (gcw_ref, gcb_ref, gwx_ref, gwa_ref, gbx_ref, gba_ref, glam_ref):
                ref[...] = jnp.zeros(ref.shape, F32)

        @pl.when(s == 0)
        def _():
            dext[t:t + 8, :] = jnp.zeros((8, D), F32)
            carry[...] = jnp.zeros((8, D), F32)

        keep = jnp.where(first_tile, 0.0, 1.0)
        xp = xp_ref[0]
        ext[0:8, :] = xph_ref[0] * keep
        ext[8:8 + t, :] = xp
        hext[0:8, :] = hh_ref[...] * keep
        hext[8:8 + t, :] = h_ref[...]
        cw = cw_ref[...]
        lam = lam_ref[...]
        taps = _shifted(_groups(ext[...]), (3, 2, 1)) + [xp]
        xa = _conv(taps, cw, cb_ref[...])
        xab, gi, gr, sp, a, mult = _lru_gates(xa, wx_ref, wa_ref, bx_ref[...], ba_ref[...], lam)
        (h_prev,) = _shifted(_groups(hext[...]), (1,))
        ga = ga_ref[0]
        sg = _sigmoid(ga)
        dya_v = dya_ref[...]
        d_ga = dya_v * h_ref[...] * (sg * (1.0 + ga * (1.0 - sg)))
        g_in = dya_v * (ga * sg)

        (an,) = _shifted(jnp.concatenate([_groups(a), jnp.ones((1, 8, D), F32)], axis=0), (-1,))
        an, u = _groups(an), _groups(g_in)
        row = lax.broadcasted_iota(jnp.int32, an.shape, 1)
        for sh in (1, 2, 4):
            a_sh = pltpu.roll(an, 8 - sh, 1)
            u_sh = pltpu.roll(u, 8 - sh, 1)
            m = row < 8 - sh
            u = jnp.where(m, u + an * u_sh, u)
            an = jnp.where(m, an * a_sh, an)
        a_s[...] = an.reshape(t, D)
        u_s[...] = u.reshape(t, D)

        def step(i, c):
            r = pl.multiple_of((t8 - 1 - i) * 8, 8)
            dg = u_s[pl.ds(r, 8), :] + a_s[pl.ds(r, 8), :] * c
            dh_s[pl.ds(r, 8), :] = dg
            return dg[0:1, :]

        lax.fori_loop(0, t8, step, carry[0:1, :], unroll=4)
        dh = dh_s[...]
        carry[0:1, :] = a[0:1, :] * dh[0:1, :]

        d_a = dh * h_prev
        dux = dh * xa
        d_mult = dux * gi
        d_gi = dux * mult
        d_xa = dh * (mult * gi)
        d_loga = d_a * a - d_mult * ((a * a) / mult)
        d_gr = d_loga * (-LRU_C * sp)
        d_sp = jnp.sum(d_loga * (-LRU_C * gr), axis=0, keepdims=True)
        glam_ref[...] += d_sp * (-_sigmoid(-lam))
        d_pi = d_gi * gi * (1.0 - gi)
        d_pr = d_gr * gr * (1.0 - gr)
        gbx_ref[...] += jnp.sum(d_pi, axis=0, keepdims=True)
        gba_ref[...] += jnp.sum(d_pr, axis=0, keepdims=True)
        dpib = d_pi.astype(BF16)
        dprb = d_pr.astype(BF16)
        back = []
        for h in range(NB):
            cs = slice(h * BD, (h + 1) * BD)
            gwx_ref[h] += lax.dot_general(xab[:, cs], dpib[:, cs], TN_DIMS, preferred_element_type=F32)
            gwa_ref[h] += lax.dot_general(xab[:, cs], dprb[:, cs], TN_DIMS, preferred_element_type=F32)
            back.append(lax.dot_general(dpib[:, cs], wx_ref[h], NT_DIMS, preferred_element_type=F32)
                        + lax.dot_general(dprb[:, cs], wa_ref[h], NT_DIMS, preferred_element_type=F32))
        d_xa = d_xa + jnp.concatenate(back, axis=1)

        dext[0:t, :] = d_xa
        later = _shifted(_groups(dext[...]), (-3, -2, -1))
        d_xp = later[0] * cw[0:1, :] + later[1] * cw[1:2, :]
        d_xp = d_xp + later[2] * cw[2:3, :]
        d_xp = d_xp + d_xa * cw[3:4, :]
        dext[t:t + 8, :] = d_xa[0:8, :]
        gcb_ref[...] += jnp.sum(d_xa, axis=0, keepdims=True)
        for k in range(4):
            gcw_ref[k:k + 1, :] += jnp.sum(d_xa * taps[k], axis=0, keepdims=True)
        dz_ref[0] = d_xp.astype(BF16)
        dz_ref[1] = d_ga.astype(BF16)

    rb = lambda b, s: b * ns + (ns - 1 - s)
    halo = lambda b, s: jnp.maximum(rb(b, s) * t8 - 1, 0)
    rep2 = lambda b, s: (0, 0)
    rep3 = lambda b, s: (0, 0, 0)
    return pl.pallas_call(
        body, name="lru_bwd", grid=(nb, ns),
        in_specs=[pl.BlockSpec((1, t, D), lambda b, s: (0, rb(b, s), 0)),
                  pl.BlockSpec((1, 8, D), lambda b, s: (0, halo(b, s), 0)),
                  pl.BlockSpec((1, t, D), lambda b, s: (1, rb(b, s), 0)),
                  pl.BlockSpec((t, D), lambda b, s: (rb(b, s), 0)),
                  pl.BlockSpec((8, D), lambda b, s: (halo(b, s), 0)),
                  pl.BlockSpec((t, D), lambda b, s: (rb(b, s), 0)),
                  pl.BlockSpec((8, D), rep2), pl.BlockSpec((1, D), rep2),
                  pl.BlockSpec((NB, BD, BD), rep3), pl.BlockSpec((NB, BD, BD), rep3),
                  pl.BlockSpec((1, D), rep2), pl.BlockSpec((1, D), rep2), pl.BlockSpec((1, D), rep2)],
        out_specs=[pl.BlockSpec((2, t, D), lambda b, s: (0, rb(b, s), 0)),
                   pl.BlockSpec((8, D), rep2), pl.BlockSpec((1, D), rep2),
                   pl.BlockSpec((NB, BD, BD), rep3), pl.BlockSpec((NB, BD, BD), rep3),
                   pl.BlockSpec((1, D), rep2), pl.BlockSpec((1, D), rep2), pl.BlockSpec((1, D), rep2)],
        out_shape=[SDS((2, n, D), BF16), SDS((8, D), F32), SDS((1, D), F32),
                   SDS((NB, BD, BD), F32), SDS((NB, BD, BD), F32),
                   SDS((1, D), F32), SDS((1, D), F32), SDS((1, D), F32)],
        scratch_shapes=[pltpu.VMEM((t + 8, D), F32), pltpu.VMEM((t + 8, D), F32), pltpu.VMEM((t + 8, D), F32),
                        pltpu.VMEM((t, D), F32), pltpu.VMEM((t, D), F32), pltpu.VMEM((t, D), F32),
                        pltpu.VMEM((8, D), F32)],
        compiler_params=_params(56),
    )(z, z, z, h_all, h_all, dya, cw8, cb, wx, wa, bx, ba, lam)


HG_T = 512
HG_NC = HG_T // CHUNK
BNT_DIMS = (((2,), (2,)), ((0,), (0,)))
BNN_DIMS = (((2,), (1,)), ((0,), (0,)))
BTN_DIMS = (((1,), (1,)), ((0,), (0,)))


def _lower_bound(lg):
    m = jnp.max(lg, axis=0, keepdims=True)
    e = jnp.exp(lg - m)
    return e[0:1, :] / jnp.sum(e, axis=0, keepdims=True)


def _tri(upper):
    r = lax.broadcasted_iota(jnp.int32, (HG_NC, CHUNK, CHUNK), 1)
    c = lax.broadcasted_iota(jnp.int32, (HG_NC, CHUNK, CHUNK), 2)
    return (c >= r) if upper else (r >= c)


def _bdot(a, b, dims):
    return lax.dot_general(a, b, dims, preferred_element_type=F32)


def _tri_sums(upper, a):
    tri = _tri(upper).astype(BF16)
    a1 = a.astype(BF16)
    r1 = a - a1.astype(F32)
    a2 = r1.astype(BF16)
    a3 = (r1 - a2.astype(F32)).astype(BF16)
    return _bdot(tri, a1, BNN_DIMS) + (_bdot(tri, a2, BNN_DIMS) + _bdot(tri, a3, BNN_DIMS))


def _chunks(a):
    return a.reshape(HG_NC, CHUNK, BD)


def _hg_tile(q, fp, lb):
    q, fp = _chunks(q), _chunks(fp)
    sig = _sigmoid(fp)
    f = lb + (1.0 - lb) * sig
    log_f = jnp.log(f)
    k = 1.0 - f
    b = _tri_sums(False, log_f)
    b_mid = b[:, CHUNK // 2:CHUNK // 2 + 1, :]
    b_last = b[:, CHUNK - 1:CHUNK, :]
    sq = _sigmoid(q)
    qh = q * sq
    e_qi = jnp.exp(b - b_mid)
    e_ki = jnp.exp(b_mid - b)
    e_qs = jnp.exp(b)
    e_ks = jnp.exp(b_last - b)
    dc = jnp.exp(b_last)
    q_in = (qh * e_qi) * HG_SCALE
    k_in = k * e_ki
    q_st = (qh * e_qs) * HG_SCALE
    k_st = k * e_ks
    att = _bdot(q_in.astype(BF16), k_in.astype(BF16), BNT_DIMS)
    att = jnp.where(_tri(False), att, 0.0)
    return dict(q=q, sig=sig, f=f, k=k, sq=sq, e_qi=e_qi, e_ki=e_ki, e_qs=e_qs, e_ks=e_ks, dc=dc,
                q_in=q_in, k_in=k_in, q_st=q_st, k_st=k_st, att=att)


def _hgrn_fwd(z, lb_logits, hg_g, nb, s_len):
    n = nb * s_len
    t = HG_T
    ns = s_len // t
    nchunk = s_len // CHUNK

    def body(q_ref, f_ref, v_ref, gb_ref, lg_ref, g_ref, o_ref, yb_ref, st_ref, st):
        @pl.when(pl.program_id(1) == 0)
        def _():
            st[...] = jnp.zeros((NB, BD, BD), F32)

        def head(h, carry):
            cols = pl.ds(pl.multiple_of(h * BD, BD), BD)
            lb = _lower_bound(lg_ref[:, cols])
            ck = _hg_tile(q_ref[0, :, cols], f_ref[0, :, cols], lb)
            vb = _chunks(v_ref[0, :, cols]).astype(BF16)
            kv = _bdot(vb, ck["k_st"].astype(BF16), BTN_DIMS)
            states = [st[h]]
            for c in range(HG_NC):
                states.append(states[c] * ck["dc"][c] + kv[c])
            st[h] = states[HG_NC]
            s_in = jnp.stack(states[:HG_NC], axis=0)
            st_ref[h] = s_in
            o = (_bdot(ck["att"].astype(BF16), vb, BNN_DIMS)
                 + _bdot(ck["q_st"].astype(BF16), s_in.astype(BF16), BNT_DIMS))
            o_ref[:, cols] = o.reshape(t, BD)
            r = lax.rsqrt(jnp.mean(o * o, axis=-1, keepdims=True) + EPS)
            gb = _chunks(gb_ref[0, :, cols])
            yb_ref[:, cols] = (((o * r) * g_ref[...]) * (gb * _sigmoid(gb))).astype(BF16).reshape(t, BD)
            return carry

        lax.fori_loop(0, NB, head, 0, unroll=4)

    seg = lambda j: pl.BlockSpec((1, t, D), lambda b, s: (j, b * ns + s, 0))
    tile = pl.BlockSpec((t, D), lambda b, s: (b * ns + s, 0))
    return pl.pallas_call(
        body, name="hgrn_fwd", grid=(nb, ns),
        in_specs=[seg(2), seg(3), seg(4), seg(5),
                  pl.BlockSpec((2, D), lambda b, s: (0, 0)), pl.BlockSpec((1, BD), lambda b, s: (0, 0))],
        out_specs=[tile, tile, pl.BlockSpec((NB, HG_NC, BD, BD), lambda b, s: (b, s, 0, 0))],
        out_shape=[SDS((n, D), F32), SDS((n, D), BF16), SDS((nb * NB, nchunk, BD, BD), F32)],
        scratch_shapes=[pltpu.VMEM((NB, BD, BD), F32)],
        compiler_params=_params(56),
    )(z, z, z, z, lb_logits, hg_g)


def _hgrn_bwd(z, o_all, st_all, dyb, lb_logits, hg_g, nb, s_len):
    n = nb * s_len
    t = HG_T
    ns = s_len // t

    def body(q_ref, f_ref, v_ref, gb_ref, o_ref, st_ref, dyb_ref, lg_ref, g_ref,
             dz_ref, glg_ref, ghg_ref, dst, dlb):
        b, s = pl.program_id(0), pl.program_id(1)

        @pl.when((b == 0) & (s == 0))
        def _():
            ghg_ref[...] = jnp.zeros((1, BD), F32)
            dlb[...] = jnp.zeros((8, D), F32)

        @pl.when(s == 0)
        def _():
            dst[...] = jnp.zeros((NB, BD, BD), F32)

        g = g_ref[...]

        def head(h, carry):
            cols = pl.ds(pl.multiple_of(h * BD, BD), BD)
            lb = _lower_bound(lg_ref[:, cols])
            ck = _hg_tile(q_ref[0, :, cols], f_ref[0, :, cols], lb)
            q = ck["q"]
            vb = _chunks(v_ref[0, :, cols]).astype(BF16)
            gb = _chunks(gb_ref[0, :, cols])
            o = _chunks(o_ref[:, cols])
            dyb_v = _chunks(dyb_ref[:, cols])
            s_in = st_ref[h]

            sgb = _sigmoid(gb)
            r = lax.rsqrt(jnp.mean(o * o, axis=-1, keepdims=True) + EPS)
            ohat = o * r
            d_on = dyb_v * (gb * sgb)
            d_gb = dyb_v * (ohat * g) * (sgb * (1.0 + gb * (1.0 - sgb)))
            ghg_ref[...] += jnp.sum(jnp.sum(d_on * ohat, axis=1), axis=0, keepdims=True)
            tt = d_on * g
            d_o = r * (tt - ohat * jnp.mean(tt * ohat, axis=-1, keepdims=True))
            dob = d_o.astype(BF16)

            attb = ck["att"].astype(BF16)
            q_inb, k_inb = ck["q_in"].astype(BF16), ck["k_in"].astype(BF16)
            q_stb, k_stb = ck["q_st"].astype(BF16), ck["k_st"].astype(BF16)
            d_att = jnp.where(_tri(False), _bdot(dob, vb, BNT_DIMS), 0.0).astype(BF16)
            d_q_in = _bdot(d_att, k_inb, BNN_DIMS)
            d_k_in = _bdot(d_att, q_inb, BTN_DIMS)
            d_q_st = _bdot(dob, s_in.astype(BF16), BNN_DIMS)
            qdo = _bdot(dob, q_stb, BTN_DIMS)
            d_states = [None] * HG_NC + [dst[h]]
            for c in reversed(range(HG_NC)):
                d_states[c] = d_states[c + 1] * ck["dc"][c] + qdo[c]
            dst[h] = d_states[0]
            ds_out = jnp.stack(d_states[1:], axis=0)
            dsb = ds_out.astype(BF16)
            d_v = _bdot(attb, dob, BTN_DIMS) + _bdot(k_stb, dsb, BNT_DIMS)
            d_k_st = _bdot(vb, dsb, BNN_DIMS)
            d_dc = jnp.sum(ds_out * s_in, axis=1, keepdims=True)

            p_qi = d_q_in * ck["q_in"]
            p_ki = d_k_in * ck["k_in"]
            p_qs = d_q_st * ck["q_st"]
            p_ks = d_k_st * ck["k_st"]
            d_qh = (d_q_in * ck["e_qi"] + d_q_st * ck["e_qs"]) * HG_SCALE
            d_k = d_k_in * ck["e_ki"] + d_k_st * ck["e_ks"]
            d_b = (p_qi - p_ki) + (p_qs - p_ks)
            d_b_mid = jnp.sum(p_ki - p_qi, axis=1, keepdims=True)
            d_b_last = jnp.sum(p_ks, axis=1, keepdims=True) + d_dc * ck["dc"]
            rowi = lax.broadcasted_iota(jnp.int32, (HG_NC, CHUNK, BD), 1)
            d_b = d_b + jnp.where(rowi == CHUNK // 2, d_b_mid, 0.0) + jnp.where(rowi == CHUNK - 1, d_b_last, 0.0)
            d_logf = _tri_sums(True, d_b)
            d_f = d_logf / ck["f"] - d_k
            sig, sq = ck["sig"], ck["sq"]
            d_fp = d_f * (1.0 - lb) * (sig * (1.0 - sig))
            dlb[0:1, cols] += jnp.sum(jnp.sum(d_f * (1.0 - sig), axis=1), axis=0, keepdims=True)
            d_q = d_qh * (sq * (1.0 + q * (1.0 - sq)))
            dz_ref[0, :, cols] = d_q.astype(BF16).reshape(t, BD)
            dz_ref[1, :, cols] = d_fp.astype(BF16).reshape(t, BD)
            dz_ref[2, :, cols] = d_v.astype(BF16).reshape(t, BD)
            dz_ref[3, :, cols] = d_gb.astype(BF16).reshape(t, BD)
            return carry

        lax.fori_loop(0, NB, head, 0, unroll=2)

        @pl.when((b == nb - 1) & (s == ns - 1))
        def _():
            lb = _lower_bound(lg_ref[...])
            dl = dlb[0:1, :] * (lb * (1.0 - lb))
            glg_ref[0:1, :] = dl
            glg_ref[1:2, :] = -dl

    rb = lambda b, s: b * ns + (ns - 1 - s)
    seg = lambda j: pl.BlockSpec((1, t, D), lambda b, s: (j, rb(b, s), 0))
    tile = pl.BlockSpec((t, D), lambda b, s: (rb(b, s), 0))
    return pl.pallas_call(
        body, name="hgrn_bwd", grid=(nb, ns),
        in_specs=[seg(2), seg(3), seg(4), seg(5), tile,
                  pl.BlockSpec((NB, HG_NC, BD, BD), lambda b, s: (b, ns - 1 - s, 0, 0)),
                  tile, pl.BlockSpec((2, D), lambda b, s: (0, 0)), pl.BlockSpec((1, BD), lambda b, s: (0, 0))],
        out_specs=[pl.BlockSpec((4, t, D), lambda b, s: (0, rb(b, s), 0)),
                   pl.BlockSpec((2, D), lambda b, s: (0, 0)), pl.BlockSpec((1, BD), lambda b, s: (0, 0))],
        out_shape=[SDS((4, n, D), BF16), SDS((2, D), F32), SDS((1, BD), F32)],
        scratch_shapes=[pltpu.VMEM((NB, BD, BD), F32), pltpu.VMEM((8, D), F32)],
        compiler_params=_params(60),
    )(z, z, z, z, o_all, st_all, dyb, lb_logits, hg_g)


def _mid(ya, yb, z, b_merge, x2, tgt, fin_g, pa, pb, wo):
    n = x2.shape[0]
    tm = 256
    ni = n // tm

    def body(ya_ref, yb_ref, gma_ref, gmb_ref, bm_ref, x_ref, t_ref, fg_ref, pa_hbm, pb_hbm, wo_hbm,
             dx2_ref, dya_ref, dyb_ref, dgm_ref, loss_ref, gfg_ref, gbm_ref, gm_hbm,
             pa_v, pb_v, wo_v, gpa_v, gpb_v, gwo_v, sem):
        i = pl.program_id(0)
        by_owner = lambda g: g.reshape(NB, BD, D)
        loads = [pltpu.make_async_copy(src, dst, sem.at[k])
                 for k, (src, dst) in enumerate(((pa_hbm, pa_v), (pb_hbm, pb_v), (wo_hbm, wo_v)))]
        stores = [pltpu.make_async_copy(src, dst, sem.at[k])
                  for k, (src, dst) in enumerate((g, gm_hbm.at[:, pl.ds(slot * BD, BD), :])
                                                 for slot, g in enumerate((gpa_v, gpb_v, gwo_v)))]

        @pl.when(i == 0)
        def _():
            for cp in loads:
                cp.start()
            for ref in (gpa_v, gpb_v, gwo_v, loss_ref, gfg_ref, gbm_ref):
                ref[...] = jnp.zeros(ref.shape, F32)
            for cp in loads:
                cp.wait()

        ya_v = ya_ref[...]
        yb_v = yb_ref[...]
        out_a = jnp.dot(ya_v, pa_v[...], preferred_element_type=F32)
        out_b = jnp.dot(yb_v, pb_v[...], preferred_element_type=F32)
        bm = bm_ref[...]
        g_a = _sigmoid(gma_ref[0] + bm[:, 0:D])
        g_b = _sigmoid(gmb_ref[0] + bm[:, D:2 * D])
        mixed = g_a * out_a + g_b * out_b
        mixb = mixed.astype(BF16)
        xo = x_ref[...] + jnp.dot(mixb, wo_v[...], preferred_element_type=F32)
        r = lax.rsqrt(jnp.mean(xo * xo, axis=-1, keepdims=True) + EPS)
        xn = xo * r
        fg = fg_ref[...]
        e = xn * fg - t_ref[...]
        loss_ref[...] += 0.5 * jnp.sum(jnp.mean(e * e, axis=-1, keepdims=True))
        dy = e * (1.0 / D)
        gfg_ref[...] += jnp.sum(dy * xn, axis=0, keepdims=True)
        dxn = dy * fg
        dx2 = r * (dxn - xn * jnp.mean(dxn * xn, axis=-1, keepdims=True))
        dx2_ref[...] = dx2
        dx2b = dx2.astype(BF16)
        d_mixed = lax.dot_general(dx2b, wo_v[...], NT_DIMS, preferred_element_type=F32)
        gwo_v[...] += by_owner(lax.dot_general(mixb, dx2b, TN_DIMS, preferred_element_type=F32))
        d_oa = (d_mixed * g_a).astype(BF16)
        d_ob = (d_mixed * g_b).astype(BF16)
        dgm_a = (d_mixed * out_a) * (g_a * (1.0 - g_a))
        dgm_b = (d_mixed * out_b) * (g_b * (1.0 - g_b))
        gbm_ref[:, 0:D] += jnp.sum(dgm_a, axis=0, keepdims=True)
        gbm_ref[:, D:2 * D] += jnp.sum(dgm_b, axis=0, keepdims=True)
        dgm_ref[0] = dgm_a.astype(BF16)
        dgm_ref[1] = dgm_b.astype(BF16)
        dya_ref[...] = lax.dot_general(d_oa, pa_v[...], NT_DIMS, preferred_element_type=F32)
        dyb_ref[...] = lax.dot_general(d_ob, pb_v[...], NT_DIMS, preferred_element_type=F32)
        gpa_v[...] += by_owner(lax.dot_general(ya_v, d_oa, TN_DIMS, preferred_element_type=F32))
        gpb_v[...] += by_owner(lax.dot_general(yb_v, d_ob, TN_DIMS, preferred_element_type=F32))

        @pl.when(i == ni - 1)
        def _():
            for cp in stores:
                cp.start()
            for cp in stores:
                cp.wait()

    rows = pl.BlockSpec((tm, D), lambda i: (i, 0))
    rep = lambda shape: pl.BlockSpec(shape, lambda i: (0,) * len(shape))
    return pl.pallas_call(
        body, name="mid", grid=(ni,),
        in_specs=[rows, rows,
                  pl.BlockSpec((1, tm, D), lambda i: (6, i, 0)), pl.BlockSpec((1, tm, D), lambda i: (7, i, 0)),
                  rep((1, 2 * D)), rows, rows, rep((1, D)), ANY, ANY, ANY],
        out_specs=[rows, rows, rows, pl.BlockSpec((2, tm, D), lambda i: (0, i, 0)),
                   rep((8, BD)), rep((1, D)), rep((1, 2 * D)), ANY],
        out_shape=[SDS((n, D), F32), SDS((n, D), F32), SDS((n, D), F32), SDS((2, n, D), BF16),
                   SDS((8, BD), F32), SDS((1, D), F32), SDS((1, 2 * D), F32),
                   SDS((NB, MID_ROWS, D), F32)],
        scratch_shapes=[pltpu.VMEM((D, D), BF16)] * 3 + [pltpu.VMEM((NB, BD, D), F32)] * 3 + [pltpu.SemaphoreType.DMA((3,))],
        compiler_params=_params(60),
    )(ya, yb, z, z, b_merge, x2, tgt, fin_g, pa, pb, wo)


def _dz_specs(tm, ni, row_major):
    if row_major:
        ia = lambda i, j: (jnp.minimum(j, 1), i, 0)
        ib = lambda i, j: (jnp.clip(j - 2, 0, 3), i, 0)
        im = lambda i, j: (jnp.clip(j - 6, 0, 1), i, 0)
    else:
        last = ni - 1
        ia = lambda j, i: (jnp.minimum(j, 1), jnp.where(j < 2, i, last), 0)
        ib = lambda j, i: (jnp.clip(j - 2, 0, 3), jnp.where(j < 2, 0, jnp.where(j < 6, i, last)), 0)
        im = lambda j, i: (jnp.clip(j - 6, 0, 1), jnp.where(j < 6, 0, i), 0)
    return [pl.BlockSpec((1, tm, D), f) for f in (ia, ib, im)]


def _inproj_bwd_x(dza, dzb, dzm, w_all, x2, dx2, norm_g, after):
    n = x2.shape[0]
    tm = 512
    ni = n // tm

    def body(dza_ref, dzb_ref, dzm_ref, w_ref, x_ref, dx2_ref, g_ref, after_ref, gx_ref, gg_ref, acc):
        i, j = pl.program_id(0), pl.program_id(1)

        @pl.when((i == 0) & (j == 0))
        def _():
            gg_ref[...] = jnp.zeros((1, D), F32)

        @pl.when(j == 0)
        def _():
            acc[...] = jnp.zeros((tm, D), F32)

        def add(ref):
            acc[...] += lax.dot_general(ref[0], w_ref[0], NT_DIMS, preferred_element_type=F32)

        pl.when(j < 2)(lambda: add(dza_ref))
        pl.when((j >= 2) & (j < 6))(lambda: add(dzb_ref))
        pl.when(j >= 6)(lambda: add(dzm_ref))

        @pl.when(j == NB - 1)
        def _():
            x = x_ref[...]
            r = lax.rsqrt(jnp.mean(x * x, axis=-1, keepdims=True) + EPS)
            xn = x * r
            dh = acc[...]
            gg_ref[...] += jnp.sum(dh * xn, axis=0, keepdims=True)
            dxn = dh * g_ref[...]
            gx_ref[...] = dx2_ref[...] + r * (dxn - xn * jnp.mean(dxn * xn, axis=-1, keepdims=True))

    rows = pl.BlockSpec((tm, D), lambda i, j: (i, 0))
    return pl.pallas_call(
        body, name="inproj_bwd_x", grid=(ni, NB),
        in_specs=_dz_specs(tm, ni, True) + [pl.BlockSpec((1, D, D), lambda i, j: (j, 0, 0)), rows, rows,
                                             pl.BlockSpec((1, D), lambda i, j: (0, 0)), ANY],
        out_specs=[rows, pl.BlockSpec((1, D), lambda i, j: (0, 0))],
        out_shape=[SDS((n, D), F32), SDS((1, D), F32)],
        scratch_shapes=[pltpu.VMEM((tm, D), F32)],
        compiler_params=_params(48),
    )(dza, dzb, dzm, w_all, x2, dx2, norm_g, after)


def _inproj_bwd_w(dza, dzb, dzm, h_all, g_m):
    n = h_all.shape[0]
    tm = min(n, 2048)
    ni = n // tm

    def body(dza_ref, dzb_ref, dzm_ref, h_ref, gm_hbm, gw_ref, got_w, got_m, stage, send_sems, recv_sems):
        j, i = pl.program_id(0), pl.program_id(1)
        x, y, c = _place()
        sibling = (x, y, 1 - c)

        def send_w(q):
            return pltpu.make_async_remote_copy(
                src_ref=stage.at[q % 2], dst_ref=got_w.at[q], send_sem=send_sems.at[q], recv_sem=recv_sems.at[q],
                device_id=sibling, device_id_type=MESH)

        def send_m(q):
            return pltpu.make_async_remote_copy(
                src_ref=gm_hbm.at[2 * q + (1 - c)], dst_ref=got_m.at[q], send_sem=send_sems.at[4 + q],
                recv_sem=recv_sems.at[4 + q], device_id=sibling, device_id_type=MESH)

        @pl.when((j == 0) & (i == 0))
        def _():
            for q in range(4):
                send_m(q).start()

        @pl.when(i == 0)
        def _():
            gw_ref[...] = jnp.zeros((1, D, D), F32)

        def add(ref):
            gw_ref[0] += lax.dot_general(h_ref[...], ref[0], TN_DIMS, preferred_element_type=F32)

        pl.when(j < 2)(lambda: add(dza_ref))
        pl.when((j >= 2) & (j < 6))(lambda: add(dzb_ref))
        pl.when(j >= 6)(lambda: add(dzm_ref))

        for q in range(4):
            @pl.when((i == ni - 1) & (j == 2 * q + 1 - c))
            def _(q=q):
                if q >= 2:
                    send_w(q - 2).wait_send()
                stage[q % 2] = gw_ref[0].astype(BF16)
                send_w(q).start()

        @pl.when((j == NB - 1) & (i == ni - 1))
        def _():
            for q in (2, 3):
                send_w(q).wait_send()
            for q in range(4):
                send_w(q).wait_recv()
                send_m(q).wait_send()
                send_m(q).wait_recv()

    return pl.pallas_call(
        body, name="inproj_bwd_w", grid=(NB, ni),
        in_specs=_dz_specs(tm, ni, False) + [pl.BlockSpec((tm, D), lambda j, i: (i, 0)), ANY],
        out_specs=[pl.BlockSpec((1, D, D), lambda j, i: (j, 0, 0)), ANY, ANY],
        out_shape=[SDS((NB, D, D), F32), SDS((4, D, D), BF16), SDS((4,) + g_m.shape[1:], F32)],
        scratch_shapes=[pltpu.VMEM((2, D, D), BF16), pltpu.SemaphoreType.DMA((8,)), pltpu.SemaphoreType.DMA((8,))],
        compiler_params=_params(58),
    )(dza, dzb, dzm, h_all, g_m)


def _adamw(w, g, m, v):
    rows, cols = w.shape
    tr = _row_tile(rows)

    spec = pl.BlockSpec((tr, cols), lambda i: (i, 0))
    return pl.pallas_call(
        functools.partial(_adam_refs), name="adamw", grid=(rows // tr,), in_specs=[spec] * 4, out_specs=[spec] * 3,
        out_shape=[SDS((rows, cols), F32)] * 3, compiler_params=_params(32),
    )(w, g, m, v)


def _adam_refs(w_ref, g_ref, m_ref, v_ref, d_ref, nm_ref, nv_ref):
    gv = g_ref[...]
    nm = ADAM_B1 * m_ref[...] + (1.0 - ADAM_B1) * gv
    nv = ADAM_B2 * v_ref[...] + (1.0 - ADAM_B2) * (gv * gv)
    m_hat = nm / (1.0 - ADAM_B1 ** ADAM_STEP)
    v_hat = nv / (1.0 - ADAM_B2 ** ADAM_STEP)
    d_ref[...] = -ADAM_LR * (m_hat / (jnp.sqrt(v_hat) + ADAM_EPS) + ADAM_WD * w_ref[...])
    nm_ref[...] = nm
    nv_ref[...] = nv


def _adamw_small(ws, gs, ms, vs):
    k = len(ws)

    def body(*refs):
        ins, outs = refs[:4 * k], refs[4 * k:7 * k]
        vin, vout = refs[7 * k:11 * k], refs[11 * k:14 * k]
        load_sems, store_sems = refs[14 * k:]
        loads = [pltpu.make_async_copy(ins[i], vin[i], load_sems.at[i]) for i in range(4 * k)]
        for cp in loads:
            cp.start()
        for cp in loads:
            cp.wait()
        for i in range(k):
            _adam_refs(*[vin[part * k + i] for part in range(4)], *[vout[part * k + i] for part in range(3)])
        stores = [pltpu.make_async_copy(vout[i], outs[i], store_sems.at[i]) for i in range(3 * k)]
        for cp in stores:
            cp.start()
        for cp in stores:
            cp.wait()

    shapes = [SDS(w.shape, F32) for w in ws]
    vmem = [pltpu.VMEM(w.shape, F32) for w in ws]
    out = pl.pallas_call(
        body, name="adamw_small", out_shape=shapes * 3, in_specs=[HBM] * (4 * k), out_specs=[HBM] * (3 * k),
        scratch_shapes=vmem * 7 + [pltpu.SemaphoreType.DMA((4 * k,)), pltpu.SemaphoreType.DMA((3 * k,))],
        compiler_params=_params(32),
    )(*ws, *gs, *ms, *vs)
    return out[:k], out[k:2 * k], out[2 * k:]


def _allgather(blocks, dtypes, name):
    na = len(blocks)

    def body(*refs):
        ins, outs, stages = refs[:na], refs[na:2 * na], refs[2 * na:3 * na]
        send_sems, recv_sems, local_sems = refs[3 * na:]
        x, y, c = _place()
        me, sibling = (x, y, c), (x, y, 1 - c)
        chips = [(1 - x, y), (x, 1 - y), (1 - x, 1 - y)]
        blk = lambda p: 4 * p[0] + 2 * p[1] + p[2]

        def copy(a, k, block, to, src=None):
            return pltpu.make_async_remote_copy(
                src_ref=outs[a].at[blk(block)] if src is None else src, dst_ref=outs[a].at[blk(block)],
                send_sem=send_sems.at[7 * a + k], recv_sem=recv_sems.at[7 * a + k],
                device_id=to, device_id_type=MESH)

        mine, first, passed = [], [], []
        for a in range(na):
            stages[a][...] = ins[a][...].astype(dtypes[a])
            mine.append(pltpu.make_async_copy(stages[a], outs[a].at[blk(me)], local_sems.at[a]))
            mine[-1].start()
            first.append(copy(a, 0, me, sibling, src=stages[a]))
            first += [copy(a, 1 + j, me, (*chip, c), src=stages[a]) for j, chip in enumerate(chips)]
        for cp in first:
            cp.start()
        for j, chip in enumerate(chips):
            for a in range(na):
                copy(a, 1 + j, (*chip, c), me).wait_recv()
                passed.append(copy(a, 4 + j, (*chip, c), sibling))
                passed[-1].start()
        for a in range(na):
            copy(a, 0, sibling, me).wait_recv()
            for j, chip in enumerate(chips):
                copy(a, 4 + j, (*chip, 1 - c), me).wait_recv()
        for cp in first + passed:
            cp.wait_send()
        for cp in mine:
            cp.wait()

    return pl.pallas_call(
        body, name=name,
        in_specs=[pl.BlockSpec(memory_space=pltpu.VMEM)] * na, out_specs=[ANY] * na,
        out_shape=[SDS((NB,) + b.shape, dt) for b, dt in zip(blocks, dtypes)],
        scratch_shapes=[pltpu.VMEM(b.shape, dt) for b, dt in zip(blocks, dtypes)]
        + [pltpu.SemaphoreType.DMA((7 * na,)), pltpu.SemaphoreType.DMA((7 * na,)), pltpu.SemaphoreType.DMA((na,))],
        compiler_params=_params(40),
    )(*blocks)


HBM = pl.BlockSpec(memory_space=pltpu.HBM)
SEMS = pl.BlockSpec(memory_space=pltpu.SEMAPHORE)
EFFECT = pltpu.SideEffectType.DATAFLOW_SIDE_EFFECTING


def _chip_copies(srcs, lands, send_sems, recv_sems):
    x, y, c = _place()
    return [pltpu.make_async_remote_copy(
        src_ref=srcs[a].at[slot], dst_ref=lands[a].at[slot],
        send_sem=send_sems.at[3 * a + slot], recv_sem=recv_sems.at[3 * a + slot],
        device_id=(px, py, c), device_id_type=MESH)
        for a in range(len(srcs)) for slot, (px, py) in enumerate(_other_chips(x, y))]


def _split_start(name, copies, per_array, srcs, lands, after=None):
    na = len(srcs)

    def body(*refs):
        send_sems, recv_sems = refs[-2 * na - 3], refs[-2 * na - 2]
        for cp in copies(refs[:na], refs[na:2 * na], send_sems, recv_sems):
            cp.start()
        refs[-1][...] = jnp.zeros_like(refs[-1])

    hbm = lambda a: pltpu.HBM(a.shape, a.dtype)
    pin = lambda a: pltpu.with_memory_space_constraint(a, pltpu.HBM)
    out = pl.pallas_call(
        body, name=name,
        out_shape=(pltpu.SemaphoreType.DMA((per_array * na,)), pltpu.SemaphoreType.DMA((per_array * na,)),
                   *[hbm(a) for a in srcs], *[hbm(a) for a in lands], SDS((8, BD), F32)),
        in_specs=[HBM] * (2 * na) + ([] if after is None else [ANY]),
        out_specs=(SEMS, SEMS, *[HBM] * (2 * na), pl.BlockSpec(memory_space=pltpu.VMEM)),
        input_output_aliases={i: 2 + i for i in range(2 * na)},
        compiler_params=pltpu.CompilerParams(has_side_effects=EFFECT),
    )(*[pin(a) for a in srcs], *[pin(a) for a in lands], *([] if after is None else [after]))
    return out[0], out[1], out[2:2 + na], out[2 + na:2 + 2 * na], out[-1]


def _split_wait(name, copies, started, after):
    send_sems, recv_sems, srcs, lands, _ = started
    na = len(srcs)

    def body(*refs):
        waits = copies(refs[:na], refs[na:2 * na], refs[2 * na], refs[2 * na + 1])
        for cp in waits:
            cp.wait_send()
        for cp in waits:
            cp.wait_recv()

    hbm = lambda a: pltpu.HBM(a.shape, a.dtype)
    out = pl.pallas_call(
        body, name=name,
        out_shape=(*[hbm(a) for a in srcs], *[hbm(a) for a in lands]),
        in_specs=[HBM] * (2 * na) + [SEMS, SEMS, ANY],
        out_specs=tuple([HBM] * (2 * na)),
        input_output_aliases={i: i for i in range(2 * na)},
        compiler_params=pltpu.CompilerParams(has_side_effects=EFFECT),
    )(*srcs, *lands, send_sems, recv_sems, after)
    return out[na:]


def _add_sibling(place, g, a_in):
    _, r, cols = g.shape
    tr = _row_tile(r)

    def chip(k, pr):
        qx = pr[0] if k in (1, 3) else 1 - pr[0]
        qy = pr[1] if k in (0, 3) else 1 - pr[1]
        return 2 * qx + qy

    def body(place_ref, *refs):
        g_refs, a_refs, (out_ref, own_ref) = refs[0:4], refs[4:8], refs[8:10]
        for k in range(3):
            out_ref[k] = (g_refs[k][0] + a_refs[k][0].astype(F32)).astype(BF16)
        own_ref[...] = g_refs[3][0] + a_refs[3][0].astype(F32)

    mine = lambda k: pl.BlockSpec((1, tr, cols), lambda i, pr: (2 * chip(k, pr) + pr[2], i, 0))
    theirs = lambda k: pl.BlockSpec((1, tr, cols), lambda i, pr: (chip(k, pr), i, 0))
    return pl.pallas_call(
        body, name="add_sibling",
        grid_spec=pltpu.PrefetchScalarGridSpec(
            num_scalar_prefetch=1, grid=(r // tr,),
            in_specs=[mine(k) for k in range(4)] + [theirs(k) for k in range(4)],
            out_specs=[pl.BlockSpec((3, tr, cols), lambda i, pr: (0, i, 0)),
                       pl.BlockSpec((tr, cols), lambda i, pr: (i, 0))]),
        out_shape=[SDS((3, r, cols), BF16), SDS((r, cols), F32)], compiler_params=_params(48),
    )(place, *[g] * 4, *[a_in] * 4)


def _add_chips(own, b_in):
    r, cols = own.shape
    tr = _row_tile(r)

    def body(p_ref, b0_ref, b1_ref, b2_ref, o_ref):
        o_ref[...] = ((p_ref[...] + b0_ref[0].astype(F32)) + b1_ref[0].astype(F32)) + b2_ref[0].astype(F32)

    slot = lambda k: pl.BlockSpec((1, tr, cols), lambda i: (k, i, 0))
    spec = pl.BlockSpec((tr, cols), lambda i: (i, 0))
    return pl.pallas_call(
        body, name="add_chips", grid=(r // tr,), in_specs=[spec, slot(0), slot(1), slot(2)], out_specs=spec,
        out_shape=SDS((r, cols), F32), compiler_params=_params(32),
    )(own, b_in, b_in, b_in)


VEC_NAMES = ("b_merge", "conv_b", "rg_bx", "rg_ba", "rg_lambda", "hg_lb_logits", "hg_norm_g", "final_norm_g")
REP_NAMES = ("rg_wx", "rg_wa", "norm_g") + VEC_NAMES
SMALL_AT = 3 * BD
SMALL_ROWS = 48
MID_ROWS = 448


def _sum_blocks(parts):
    def body(p_ref, o_ref):
        acc = p_ref[0]
        for k in range(1, NB):
            acc = acc + p_ref[k]
        o_ref[...] = acc

    return pl.pallas_call(body, name="sum_blocks", out_shape=SDS(parts.shape[1:], F32))(parts)


def _pack_rows(arrays, width, row_multiple=8):
    flat = jnp.concatenate([a.reshape(-1) for a in arrays])
    rows = -(-flat.shape[0] // width)
    rows = -(-rows // row_multiple) * row_multiple
    return jnp.pad(flat, (0, rows * width - flat.shape[0])).reshape(rows, width)


def _unpack(flat, like):
    out, off = [], 0
    for a in like:
        out.append(flat[off:off + a.size].reshape(a.shape))
        off += a.size
    return out


def kernel(x, w_in, b_merge, conv_w, conv_b, rg_wx, rg_bx, rg_wa, rg_ba, rg_lambda, hg_lb_logits, hg_norm_g, proj_a, proj_b, w_out, norm_g, final_norm_g, loss_target, m_w_in, m_b_merge, m_conv_w, m_conv_b, m_rg_wx, m_rg_bx, m_rg_wa, m_rg_ba, m_rg_lambda, m_hg_lb_logits, m_hg_norm_g, m_proj_a, m_proj_b, m_w_out, m_norm_g, m_final_norm_g, v_w_in, v_b_merge, v_conv_w, v_conv_b, v_rg_wx, v_rg_bx, v_rg_wa, v_rg_ba, v_rg_lambda, v_hg_lb_logits, v_hg_norm_g, v_proj_a, v_proj_b, v_w_out, v_norm_g, v_final_norm_g):
    weights = dict(w_in=w_in, b_merge=b_merge, conv_w=conv_w, conv_b=conv_b, rg_wx=rg_wx, rg_bx=rg_bx, rg_wa=rg_wa,
                   rg_ba=rg_ba, rg_lambda=rg_lambda, hg_lb_logits=hg_lb_logits, hg_norm_g=hg_norm_g, proj_a=proj_a,
                   proj_b=proj_b, w_out=w_out, norm_g=norm_g, final_norm_g=final_norm_g)
    mom1 = dict(w_in=m_w_in, b_merge=m_b_merge, conv_w=m_conv_w, conv_b=m_conv_b, rg_wx=m_rg_wx, rg_bx=m_rg_bx,
                rg_wa=m_rg_wa, rg_ba=m_rg_ba, rg_lambda=m_rg_lambda, hg_lb_logits=m_hg_lb_logits,
                hg_norm_g=m_hg_norm_g, proj_a=m_proj_a, proj_b=m_proj_b, w_out=m_w_out, norm_g=m_norm_g,
                final_norm_g=m_final_norm_g)
    mom2 = dict(w_in=v_w_in, b_merge=v_b_merge, conv_w=v_conv_w, conv_b=v_conv_b, rg_wx=v_rg_wx, rg_bx=v_rg_bx,
                rg_wa=v_rg_wa, rg_ba=v_rg_ba, rg_lambda=v_rg_lambda, hg_lb_logits=v_hg_lb_logits,
                hg_norm_g=v_hg_norm_g, proj_a=v_proj_a, proj_b=v_proj_b, w_out=v_w_out, norm_g=v_norm_g,
                final_norm_g=v_final_norm_g)
    order = list(weights)
    nb, s_len, _ = x.shape
    n = nb * s_len
    px, py, pc = _place()
    place = jnp.stack([px, py, pc]).astype(jnp.int32)

    x2 = x.reshape(n, D)
    cw_blk = jnp.pad(conv_w[0], ((0, 4), (0, 0)))
    order_ids = jnp.stack([_block_id(p) for p in _arrival_order(px, py, pc)]).astype(jnp.int32)
    z, h_all, w_all, pa_all, pb_all, wo_all, cw_all = _gather_inproj(
        order_ids, x2, norm_g, [w_in[0], proj_a[0], proj_b[0], w_out[0], cw_blk], [BF16, BF16, BF16, BF16, F32])
    pa_full, pb_full, wo_full = (a.reshape(D, D) for a in (pa_all, pb_all, wo_all))
    cw8 = cw_all.transpose(1, 0, 2).reshape(8, D)
    wx_b, wa_b = rg_wx[0].astype(BF16), rg_wa[0].astype(BF16)
    cb, bx, ba = conv_b, rg_bx.reshape(1, D), rg_ba.reshape(1, D)
    fin_g = final_norm_g.reshape(1, D)

    hlru, ya = _lru_fwd(z, cw8, cb, wx_b, wa_b, bx, ba, rg_lambda, nb, s_len)
    o_all, yb, st_all = _hgrn_fwd(z, hg_lb_logits, hg_norm_g, nb, s_len)

    (dx2, dya, dyb, dzm, loss_acc, g_fin, g_bm, g_mid) = _mid(
        ya, yb, z, b_merge, x2, loss_target.reshape(n, D), fin_g, pa_full, pb_full, wo_full)
    dzb, g_lg, g_hg = _hgrn_bwd(z, o_all, st_all, dyb, hg_lb_logits, hg_norm_g, nb, s_len)
    dza, g_cw8, g_cb, g_wx, g_wa, g_bx, g_ba, g_lam = _lru_bwd(
        z, hlru, dya, cw8, cb, wx_b, wa_b, bx, ba, rg_lambda, nb, s_len)

    part = dict(b_merge=g_bm, conv_b=g_cb, rg_bx=g_bx, rg_ba=g_ba, rg_lambda=g_lam, hg_lb_logits=g_lg,
                hg_norm_g=g_hg, final_norm_g=g_fin)
    vec = _pack_rows([part[k] for k in VEC_NAMES], BD)
    vec = jnp.pad(vec, ((0, 16 * NB - vec.shape[0]), (0, 0))).reshape(NB, 2, D)
    rows8 = lambda a: jnp.pad(a, ((0, 0), (0, 8 - a.shape[1]), (0, 0)))
    small = jnp.concatenate([g_wx.reshape(NB, 16, D), g_wa.reshape(NB, 16, D),
                             rows8(g_cw8.reshape(8, NB, BD).transpose(1, 0, 2).reshape(NB, 1, D)), rows8(vec),
                             jnp.zeros((NB, MID_ROWS - SMALL_AT - SMALL_ROWS, D), F32)], axis=1)
    g_m = lax.dynamic_update_slice(g_mid, small, (0, SMALL_AT, 0))
    g_w, w_from_sibling, m_from_sibling = _inproj_bwd_w(dza, dzb, dzm, h_all, g_m)
    w_out_bf, w_own = _add_sibling(place, g_w, w_from_sibling)
    m_out_bf, m_own = _add_sibling(place, g_m, m_from_sibling)
    outgoing = [w_out_bf, m_out_bf]
    chip_sums = _split_start("rs_chips_start", _chip_copies, 3, outgoing, [lax.empty(a.shape, a.dtype) for a in outgoing])
    grad_x, g_ng = _inproj_bwd_x(dza, dzb, dzm, w_all, x2, dx2, norm_g, chip_sums[-1])
    from_chips = _split_wait("rs_chips_wait", _chip_copies, chip_sums, grad_x)
    r_w = _add_chips(w_own, from_chips[0])
    r_m = _add_chips(m_own, from_chips[1])
    row = lax.broadcasted_iota(jnp.int32, (8, D), 0)
    mine = jnp.where(row == 0, g_ng, jnp.where(row == 1, loss_acc[0:1, 0:1], 0.0))
    tail = jnp.concatenate([r_m[SMALL_AT:SMALL_AT + SMALL_ROWS], mine], axis=0)
    (tail_all,) = _allgather([tail], [F32], "gather_small_grads")
    summed = _sum_blocks(tail_all[:, SMALL_ROWS:SMALL_ROWS + 8])

    grads = dict(w_in=r_w.reshape(1, D, D),
                 proj_a=r_m[0:BD].reshape(1, BD, D), proj_b=r_m[BD:2 * BD].reshape(1, BD, D),
                 w_out=r_m[2 * BD:3 * BD].reshape(1, BD, D),
                 conv_w=r_m[SMALL_AT + 32].reshape(8, BD)[0:4].reshape(1, 4, BD),
                 rg_wx=tail_all[:, 0:16].reshape(1, NB, BD, BD), rg_wa=tail_all[:, 16:32].reshape(1, NB, BD, BD),
                 norm_g=summed[0:1])
    vec_all = tail_all[:, 40:42].reshape(-1)
    for k, gk in zip(VEC_NAMES, _unpack(vec_all, [weights[k] for k in VEC_NAMES])):
        grads[k] = gk

    delta, new_m, new_v = {}, {}, {}
    flat2 = lambda a: a.reshape(-1, a.shape[-1])
    for k in ("w_in", "proj_a", "proj_b", "w_out"):
        outs = _adamw(*[flat2(t[k]) for t in (weights, grads, mom1, mom2)])
        delta[k], new_m[k], new_v[k] = (a.reshape(weights[k].shape) for a in outs)
    rep = list(REP_NAMES) + ["conv_w"]
    outs = _adamw_small(*[[flat2(t[k]) for k in rep] for t in (weights, grads, mom1, mom2)])
    for tgt, arrays in zip((delta, new_m, new_v), outs):
        for k, a in zip(rep, arrays):
            tgt[k] = a.reshape(weights[k].shape)

    return (summed[1, 0], grad_x.reshape(x.shape), *[grads[k] for k in order], *[delta[k] for k in order],
            *[new_m[k] for k in order], *[new_v[k] for k in order])
```

```python
import functools

import jax
import jax.numpy as jnp
from jax import lax
from jax.experimental import pallas as pl
from jax.experimental.pallas import tpu as pltpu

F32 = jnp.float32
BF16 = jnp.bfloat16
SDS = jax.ShapeDtypeStruct
MESH = pl.DeviceIdType.MESH
ANY = pl.BlockSpec(memory_space=pl.ANY)

D = 1024
NB = 8
BD = D // NB
CHUNK = 64
EPS = 1e-6
LRU_C = 8.0
HG_SCALE = BD ** -0.5
ADAM_LR, ADAM_B1, ADAM_B2, ADAM_EPS, ADAM_WD, ADAM_STEP = 0.001, 0.9, 0.999, 1e-08, 0.01, 10

NT_DIMS = (((1,), (1,)), ((), ()))
TN_DIMS = (((0,), (0,)), ((), ()))


def _params(vmem_mib):
    return pltpu.CompilerParams(vmem_limit_bytes=vmem_mib << 20)


def _row_tile(rows, most=256):
    assert rows % 8 == 0
    return max(t for t in range(8, min(rows, most) + 1, 8) if rows % t == 0)


def _sigmoid(v):
    return 0.5 * (jnp.tanh(0.5 * v) + 1.0)


def _groups(v):
    return v.reshape(v.shape[0] // 8, 8, v.shape[1])


def _softplus_neg(lam):
    t = -lam
    e = jnp.exp(-jnp.abs(t))
    w = 1.0 + e
    d = w - 1.0
    l1p = jnp.where(d == 0.0, e, jnp.log(w) * (e / jnp.where(d == 0.0, 1.0, d)))
    return jnp.maximum(t, 0.0) + l1p


def _place():
    return lax.axis_index("x"), lax.axis_index("y"), lax.axis_index("c")


def _other_chips(x, y):
    return [(1 - x, y), (x, 1 - y), (1 - x, 1 - y)]


def _block_id(p):
    return 4 * p[0] + 2 * p[1] + p[2]


def _arrival_order(x, y, c):
    near, far, diag = _other_chips(x, y)
    return [(x, y, c), (x, y, 1 - c), (*near, c), (*far, c), (*near, 1 - c), (*far, 1 - c), (*diag, c), (*diag, 1 - c)]


def _gather_inproj(order_ids, x2, norm_g, blocks, dtypes):
    na = len(blocks)
    n = x2.shape[0]
    tm = min(n, 1024)
    ni = n // tm

    def body(order_ref, x_ref, g_ref, *refs):
        ins, (z_ref, h_ref), outs = refs[:na], refs[na:na + 2], refs[na + 2:2 * na + 2]
        stages = refs[2 * na + 2:3 * na + 2]
        h_full, wbuf, send_sems, recv_sems, local_sems, wsems, hsem = refs[3 * na + 2:]
        j, i = pl.program_id(0), pl.program_id(1)
        x, y, c = _place()
        me, sibling = (x, y, c), (x, y, 1 - c)
        chips = _other_chips(x, y)
        small = range(1, na)

        def copy(a, k, block, to, src=None):
            return pltpu.make_async_remote_copy(
                src_ref=outs[a].at[_block_id(block)] if src is None else src, dst_ref=outs[a].at[_block_id(block)],
                send_sem=send_sems.at[7 * a + k], recv_sem=recv_sems.at[7 * a + k],
                device_id=to, device_id_type=MESH)

        def local(a):
            return pltpu.make_async_copy(stages[a], outs[a].at[_block_id(me)], local_sems.at[a])

        def landed(a, slot):
            copy(a, 1 + slot, (*chips[slot], c), me).wait_recv()
            copy(a, 4 + slot, (*chips[slot], c), sibling).start()
            if slot < 2:
                @pl.when(c == slot)
                def _():
                    copy(a, 3, (*chips[slot], c), (*chips[1 - slot], c)).start()

        def diagonal_and_small():
            landed(0, 2)
            for a in small:
                landed(a, 0)
                landed(a, 1)

        def passed_on(a, slot):
            copy(a, 4 + slot, (*chips[slot], 1 - c), me).wait_recv()

        @pl.when((j == 0) & (i == 0))
        def _():
            for a in range(na):
                stages[a][...] = ins[a][...].astype(dtypes[a])
                local(a).start()
            for a in range(na):
                copy(a, 0, me, sibling, src=stages[a]).start()
                for slot, chip in enumerate(chips[:2]):
                    copy(a, 1 + slot, me, (*chip, c), src=stages[a]).start()

        @pl.when(j == 0)
        def _():
            xv = x_ref[...]
            r = lax.rsqrt(jnp.mean(xv * xv, axis=-1, keepdims=True) + EPS)
            hb = ((xv * r) * g_ref[...]).astype(BF16)
            h_full[pl.ds(pl.multiple_of(i * tm, tm), tm), :] = hb

        save_h = pltpu.make_async_copy(h_full, h_ref, hsem)
        pl.when((j == 0) & (i == ni - 1))(save_h.start)

        steps = [
            lambda: local(0).wait(),
            lambda: copy(0, 0, sibling, me).wait_recv(),
            lambda: landed(0, 0),
            lambda: landed(0, 1),
            lambda: passed_on(0, 0),
            lambda: passed_on(0, 1),
            diagonal_and_small,
            lambda: passed_on(0, 2),
        ]
        def w_load(k):
            return pltpu.make_async_copy(outs[0].at[order_ref[k]], wbuf.at[k % 2], wsems.at[k % 2])

        for k, step in enumerate(steps):
            @pl.when((j == 0) & (i == 0) if k == 0 else (j == k - 1) & (i == ni - 1))
            def _(k=k, step=step):
                step()
                w_load(k).start()

        pl.when(i == 0)(lambda: w_load(j).wait())
        z_ref[0] = jnp.dot(h_full[pl.ds(pl.multiple_of(i * tm, tm), tm), :], wbuf[j % 2], preferred_element_type=F32)

        @pl.when((j == NB - 1) & (i == ni - 1))
        def _():
            save_h.wait()
            for a in small:
                landed(a, 2)
            for a in small:
                local(a).wait()
                copy(a, 0, sibling, me).wait_recv()
                for slot in range(3):
                    passed_on(a, slot)
            for a in range(na):
                copy(a, 0, me, sibling, src=stages[a]).wait_send()
                for slot, chip in enumerate(chips):
                    copy(a, 1 + slot, me, (*chip, c), src=stages[a]).wait_send()
                    copy(a, 4 + slot, (*chip, c), sibling).wait_send()

    rows_once = lambda j, i, order: (jnp.where(j == 0, i, ni - 1), 0)
    vmem = pl.BlockSpec(memory_space=pltpu.VMEM)
    return pl.pallas_call(
        body, name="gather_inproj",
        grid_spec=pltpu.PrefetchScalarGridSpec(
            num_scalar_prefetch=1, grid=(NB, ni),
            in_specs=[pl.BlockSpec((tm, D), rows_once), pl.BlockSpec((1, D), lambda j, i, order: (0, 0))] + [vmem] * na,
            out_specs=[pl.BlockSpec((1, tm, D), lambda j, i, order: (order[j], i, 0)), ANY] + [ANY] * na,
            scratch_shapes=[pltpu.VMEM(b.shape, dt) for b, dt in zip(blocks, dtypes)]
            + [pltpu.VMEM((n, D), BF16), pltpu.VMEM((2, D, D), BF16),
               pltpu.SemaphoreType.DMA((7 * na,)), pltpu.SemaphoreType.DMA((7 * na,)),
               pltpu.SemaphoreType.DMA((na,)), pltpu.SemaphoreType.DMA((2,)), pltpu.SemaphoreType.DMA(())]),
        out_shape=[SDS((NB, n, D), F32), SDS((n, D), BF16)] + [SDS((NB,) + b.shape, dt) for b, dt in zip(blocks, dtypes)],
        compiler_params=_params(56),
    )(order_ids, x2, norm_g, *blocks)


LRU_T = 256


def _shifted(groups, shifts):
    row = lax.broadcasted_iota(jnp.int32, (groups.shape[0] - 1,) + groups.shape[1:], 1)
    out = []
    for s in shifts:
        y = pltpu.roll(groups, s % 8, 1)
        moved = jnp.where(row >= s, y[1:], y[:-1]) if s > 0 else jnp.where(row < 8 + s, y[:-1], y[1:])
        out.append(moved.reshape(-1, groups.shape[2]))
    return out


def _conv(taps, cw, cb):
    acc = taps[0] * cw[0:1, :] + taps[1] * cw[1:2, :]
    acc = acc + taps[2] * cw[2:3, :]
    acc = acc + taps[3] * cw[3:4, :]
    return cb + acc


def _lru_gates(xa, wx_ref, wa_ref, bx, ba, lam):
    xab = xa.astype(BF16)
    pis, prs = [], []
    for h in range(NB):
        xs = xab[:, h * BD:(h + 1) * BD]
        pis.append(jnp.dot(xs, wx_ref[h], preferred_element_type=F32))
        prs.append(jnp.dot(xs, wa_ref[h], preferred_element_type=F32))
    gi = _sigmoid(jnp.concatenate(pis, axis=1) + bx)
    gr = _sigmoid(jnp.concatenate(prs, axis=1) + ba)
    sp = _softplus_neg(lam)
    log_a = (-LRU_C * gr) * sp
    a = jnp.exp(log_a)
    mult = jnp.sqrt(-jnp.tanh(log_a) * (a * a + 1.0))
    return xab, gi, gr, sp, a, mult


def _lru_fwd(z, cw8, cb, wx, wa, bx, ba, lam, nb, s_len):
    n = nb * s_len
    t = LRU_T
    ns = s_len // t

    def body(xp_ref, ga_ref, cw_ref, cb_ref, wx_ref, wa_ref, bx_ref, ba_ref, lam_ref,
             h_ref, ya_ref, ext, a_s, u_s, carry):
        @pl.when(pl.program_id(1) == 0)
        def _():
            ext[0:8, :] = jnp.zeros((8, D), F32)
            carry[...] = jnp.zeros((8, D), F32)

        xp = xp_ref[0]
        ext[8:8 + t, :] = xp
        xa = _conv(_shifted(_groups(ext[...]), (3, 2, 1)) + [xp], cw_ref[...], cb_ref[...])
        ext[0:8, :] = xp[t - 8:t, :]
        _, gi, _, _, a, mult = _lru_gates(xa, wx_ref, wa_ref, bx_ref[...], ba_ref[...], lam_ref[...])
        u = (mult * gi) * xa
        a, u = _groups(a), _groups(u)
        row = lax.broadcasted_iota(jnp.int32, a.shape, 1)
        for sh in (1, 2, 4):
            a_sh = pltpu.roll(a, sh, 1)
            u_sh = pltpu.roll(u, sh, 1)
            m = row >= sh
            u = jnp.where(m, a * u_sh + u, u)
            a = jnp.where(m, a * a_sh, a)
        a_s[...] = a.reshape(t, D)
        u_s[...] = u.reshape(t, D)

        def step(g, c):
            r = pl.multiple_of(g * 8, 8)
            hg = u_s[pl.ds(r, 8), :] + a_s[pl.ds(r, 8), :] * c
            h_ref[pl.ds(r, 8), :] = hg
            return hg[7:8, :]

        c_out = lax.fori_loop(0, t // 8, step, carry[0:1, :], unroll=4)
        carry[0:1, :] = c_out
        ga = ga_ref[0]
        ya_ref[...] = (h_ref[...] * (ga * _sigmoid(ga))).astype(BF16)

    row_map = lambda b, s: (b * ns + s, 0)
    rep2 = lambda b, s: (0, 0)
    rep3 = lambda b, s: (0, 0, 0)
    return pl.pallas_call(
        body, name="lru_fwd", grid=(nb, ns),
        in_specs=[pl.BlockSpec((1, t, D), lambda b, s: (0, b * ns + s, 0)),
                  pl.BlockSpec((1, t, D), lambda b, s: (1, b * ns + s, 0)),
                  pl.BlockSpec((8, D), rep2), pl.BlockSpec((1, D), rep2),
                  pl.BlockSpec((NB, BD, BD), rep3), pl.BlockSpec((NB, BD, BD), rep3),
                  pl.BlockSpec((1, D), rep2), pl.BlockSpec((1, D), rep2), pl.BlockSpec((1, D), rep2)],
        out_specs=[pl.BlockSpec((t, D), row_map), pl.BlockSpec((t, D), row_map)],
        out_shape=[SDS((n, D), F32), SDS((n, D), BF16)],
        scratch_shapes=[pltpu.VMEM((t + 8, D), F32), pltpu.VMEM((t, D), F32), pltpu.VMEM((t, D), F32),
                        pltpu.VMEM((8, D), F32)],
        compiler_params=_params(48),
    )(z, z, cw8, cb, wx, wa, bx, ba, lam)


def _lru_bwd(z, h_all, dya, cw8, cb, wx, wa, bx, ba, lam, nb, s_len):
    n = nb * s_len
    t = LRU_T
    ns = s_len // t
    t8 = t // 8

    def body(xp_ref, xph_ref, ga_ref, h_ref, hh_ref, dya_ref, cw_ref, cb_ref, wx_ref, wa_ref, bx_ref, ba_ref,
             lam_ref, dz_ref, gcw_ref, gcb_ref, gwx_ref, gwa_ref, gbx_ref, gba_ref, glam_ref,
             ext, hext, dext, a_s, u_s, dh_s, carry):
        b, s = pl.program_id(0), pl.program_id(1)
        first_tile = s == ns - 1

        @pl.when((b == 0) & (s == 0))
        def _():
            for ref in (gcw_ref, gcb_ref, gwx_ref, gwa_ref, gbx_ref, gba_ref, glam_ref):
                ref[...] = jnp.zeros(ref.shape, F32)

        @pl.when(s == 0)
        def _():
            dext[t:t + 8, :] = jnp.zeros((8, D), F32)
            carry[...] = jnp.zeros((8, D), F32)

        keep = jnp.where(first_tile, 0.0, 1.0)
        xp = xp_ref[0]
        ext[0:8, :] = xph_ref[0] * keep
        ext[8:8 + t, :] = xp
        hext[0:8, :] = hh_ref[...] * keep
        hext[8:8 + t, :] = h_ref[...]
        cw = cw_ref[...]
        lam = lam_ref[...]
        taps = _shifted(_groups(ext[...]), (3, 2, 1)) + [xp]
        xa = _conv(taps, cw, cb_ref[...])
        xab, gi, gr, sp, a, mult = _lru_gates(xa, wx_ref, wa_ref, bx_ref[...], ba_ref[...], lam)
        (h_prev,) = _shifted(_groups(hext[...]), (1,))
        ga = ga_ref[0]
        sg = _sigmoid(ga)
        dya_v = dya_ref[...]
        d_ga = dya_v * h_ref[...] * (sg * (1.0 + ga * (1.0 - sg)))
        g_in = dya_v * (ga * sg)

        (an,) = _shifted(jnp.concatenate([_groups(a), jnp.ones((1, 8, D), F32)], axis=0), (-1,))
        an, u = _groups(an), _groups(g_in)
        row = lax.broadcasted_iota(jnp.int32, an.shape, 1)
        for sh in (1, 2, 4):
            a_sh = pltpu.roll(an, 8 - sh, 1)
            u_sh = pltpu.roll(u, 8 - sh, 1)
            m = row < 8 - sh
            u = jnp.where(m, u + an * u_sh, u)
            an = jnp.where(m, an * a_sh, an)
        a_s[...] = an.reshape(t, D)
        u_s[...] = u.reshape(t, D)

        def step(i, c):
            r = pl.multiple_of((t8 - 1 - i) * 8, 8)
            dg = u_s[pl.ds(r, 8), :] + a_s[pl.ds(r, 8), :] * c
            dh_s[pl.ds(r, 8), :] = dg
            return dg[0:1, :]

        lax.fori_loop(0, t8, step, carry[0:1, :], unroll=4)
        dh = dh_s[...]
        carry[0:1, :] = a[0:1, :] * dh[0:1, :]

        d_a = dh * h_prev
        dux = dh * xa
        d_mult = dux * gi
        d_gi = dux * mult
        d_xa = dh * (mult * gi)
        d_loga = d_a * a - d_mult * ((a * a) / mult)
        d_gr = d_loga * (-LRU_C * sp)
        d_sp = jnp.sum(d_loga * (-LRU_C * gr), axis=0, keepdims=True)
        glam_ref[...] += d_sp * (-_sigmoid(-lam))
        d_pi = d_gi * gi * (1.0 - gi)
        d_pr = d_gr * gr * (1.0 - gr)
        gbx_ref[...] += jnp.sum(d_pi, axis=0, keepdims=True)
        gba_ref[...] += jnp.sum(d_pr, axis=0, keepdims=True)
        dpib = d_pi.astype(BF16)
        dprb = d_pr.astype(BF16)
        back = []
        for h in range(NB):
            cs = slice(h * BD, (h + 1) * BD)
            gwx_ref[h] += lax.dot_general(xab[:, cs], dpib[:, cs], TN_DIMS, preferred_element_type=F32)
            gwa_ref[h] += lax.dot_general(xab[:, cs], dprb[:, cs], TN_DIMS, preferred_element_type=F32)
            back.append(lax.dot_general(dpib[:, cs], wx_ref[h], NT_DIMS, preferred_element_type=F32)
                        + lax.dot_general(dprb[:, cs], wa_ref[h], NT_DIMS, preferred_element_type=F32))
        d_xa = d_xa + jnp.concatenate(back, axis=1)

        dext[0:t, :] = d_xa
        later = _shifted(_groups(dext[...]), (-3, -2, -1))
        d_xp = later[0] * cw[0:1, :] + later[1] * cw[1:2, :]
        d_xp = d_xp + later[2] * cw[2:3, :]
        d_xp = d_xp + d_xa * cw[3:4, :]
        dext[t:t + 8, :] = d_xa[0:8, :]
        gcb_ref[...] += jnp.sum(d_xa, axis=0, keepdims=True)
        for k in range(4):
            gcw_ref[k:k + 1, :] += jnp.sum(d_xa * taps[k], axis=0, keepdims=True)
        dz_ref[0] = d_xp.astype(BF16)
        dz_ref[1] = d_ga.astype(BF16)

    rb = lambda b, s: b * ns + (ns - 1 - s)
    halo = lambda b, s: jnp.maximum(rb(b, s) * t8 - 1, 0)
    rep2 = lambda b, s: (0, 0)
    rep3 = lambda b, s: (0, 0, 0)
    return pl.pallas_call(
        body, name="lru_bwd", grid=(nb, ns),
        in_specs=[pl.BlockSpec((1, t, D), lambda b, s: (0, rb(b, s), 0)),
                  pl.BlockSpec((1, 8, D), lambda b, s: (0, halo(b, s), 0)),
                  pl.BlockSpec((1, t, D), lambda b, s: (1, rb(b, s), 0)),
                  pl.BlockSpec((t, D), lambda b, s: (rb(b, s), 0)),
                  pl.BlockSpec((8, D), lambda b, s: (halo(b, s), 0)),
                  pl.BlockSpec((t, D), lambda b, s: (rb(b, s), 0)),
                  pl.BlockSpec((8, D), rep2), pl.BlockSpec((1, D), rep2),
                  pl.BlockSpec((NB, BD, BD), rep3), pl.BlockSpec((NB, BD, BD), rep3),
                  pl.BlockSpec((1, D), rep2), pl.BlockSpec((1, D), rep2), pl.BlockSpec((1, D), rep2)],
        out_specs=[pl.BlockSpec((2, t, D), lambda b, s: (0, rb(b, s), 0)),
                   pl.BlockSpec((8, D), rep2), pl.BlockSpec((1, D), rep2),
                   pl.BlockSpec((NB, BD, BD), rep3), pl.BlockSpec((NB, BD, BD), rep3),
                   pl.BlockSpec((1, D), rep2), pl.BlockSpec((1, D), rep2), pl.BlockSpec((1, D), rep2)],
        out_shape=[SDS((2, n, D), BF16), SDS((8, D), F32), SDS((1, D), F32),
                   SDS((NB, BD, BD), F32), SDS((NB, BD, BD), F32),
                   SDS((1, D), F32), SDS((1, D), F32), SDS((1, D), F32)],
        scratch_shapes=[pltpu.VMEM((t + 8, D), F32), pltpu.VMEM((t + 8, D), F32), pltpu.VMEM((t + 8, D), F32),
                        pltpu.VMEM((t, D), F32), pltpu.VMEM((t, D), F32), pltpu.VMEM((t, D), F32),
                        pltpu.VMEM((8, D), F32)],
        compiler_params=_params(56),
    )(z, z, z, h_all, h_all, dya, cw8, cb, wx, wa, bx, ba, lam)


HG_T = 512
HG_NC = HG_T // CHUNK
BNT_DIMS = (((2,), (2,)), ((0,), (0,)))
BNN_DIMS = (((2,), (1,)), ((0,), (0,)))
BTN_DIMS = (((1,), (1,)), ((0,), (0,)))


def _lower_bound(lg):
    m = jnp.max(lg, axis=0, keepdims=True)
    e = jnp.exp(lg - m)
    return e[0:1, :] / jnp.sum(e, axis=0, keepdims=True)


def _tri(upper):
    r = lax.broadcasted_iota(jnp.int32, (HG_NC, CHUNK, CHUNK), 1)
    c = lax.broadcasted_iota(jnp.int32, (HG_NC, CHUNK, CHUNK), 2)
    return (c >= r) if upper else (r >= c)


def _bdot(a, b, dims):
    return lax.dot_general(a, b, dims, preferred_element_type=F32)


def _tri_sums(upper, a):
    tri = _tri(upper).astype(BF16)
    a1 = a.astype(BF16)
    r1 = a - a1.astype(F32)
    a2 = r1.astype(BF16)
    a3 = (r1 - a2.astype(F32)).astype(BF16)
    return _bdot(tri, a1, BNN_DIMS) + (_bdot(tri, a2, BNN_DIMS) + _bdot(tri, a3, BNN_DIMS))


def _chunks(a):
    return a.reshape(HG_NC, CHUNK, BD)


def _hg_tile(q, fp, lb):
    q, fp = _chunks(q), _chunks(fp)
    sig = _sigmoid(fp)
    f = lb + (1.0 - lb) * sig
    log_f = jnp.log(f)
    k = 1.0 - f
    b = _tri_sums(False, log_f)
    b_mid = b[:, CHUNK // 2:CHUNK // 2 + 1, :]
    b_last = b[:, CHUNK - 1:CHUNK, :]
    sq = _sigmoid(q)
    qh = q * sq
    e_qi = jnp.exp(b - b_mid)
    e_ki = jnp.exp(b_mid - b)
    e_qs = jnp.exp(b)
    e_ks = jnp.exp(b_last - b)
    dc = jnp.exp(b_last)
    q_in = (qh * e_qi) * HG_SCALE
    k_in = k * e_ki
    q_st = (qh * e_qs) * HG_SCALE
    k_st = k * e_ks
    att = _bdot(q_in.astype(BF16), k_in.astype(BF16), BNT_DIMS)
    att = jnp.where(_tri(False), att, 0.0)
    return dict(q=q, sig=sig, f=f, k=k, sq=sq, e_qi=e_qi, e_ki=e_ki, e_qs=e_qs, e_ks=e_ks, dc=dc,
                q_in=q_in, k_in=k_in, q_st=q_st, k_st=k_st, att=att)


def _hgrn_fwd(z, lb_logits, hg_g, nb, s_len):
    n = nb * s_len
    t = HG_T
    ns = s_len // t
    nchunk = s_len // CHUNK

    def body(q_ref, f_ref, v_ref, gb_ref, lg_ref, g_ref, o_ref, yb_ref, st_ref, st):
        @pl.when(pl.program_id(1) == 0)
        def _():
            st[...] = jnp.zeros((NB, BD, BD), F32)

        def head(h, carry):
            cols = pl.ds(pl.multiple_of(h * BD, BD), BD)
            lb = _lower_bound(lg_ref[:, cols])
            ck = _hg_tile(q_ref[0, :, cols], f_ref[0, :, cols], lb)
            vb = _chunks(v_ref[0, :, cols]).astype(BF16)
            kv = _bdot(vb, ck["k_st"].astype(BF16), BTN_DIMS)
            states = [st[h]]
            for c in range(HG_NC):
                states.append(states[c] * ck["dc"][c] + kv[c])
            st[h] = states[HG_NC]
            s_in = jnp.stack(states[:HG_NC], axis=0)
            st_ref[h] = s_in
            o = (_bdot(ck["att"].astype(BF16), vb, BNN_DIMS)
                 + _bdot(ck["q_st"].astype(BF16), s_in.astype(BF16), BNT_DIMS))
            o_ref[:, cols] = o.reshape(t, BD)
            r = lax.rsqrt(jnp.mean(o * o, axis=-1, keepdims=True) + EPS)
            gb = _chunks(gb_ref[0, :, cols])
            yb_ref[:, cols] = (((o * r) * g_ref[...]) * (gb * _sigmoid(gb))).astype(BF16).reshape(t, BD)
            return carry

        lax.fori_loop(0, NB, head, 0, unroll=4)

    seg = lambda j: pl.BlockSpec((1, t, D), lambda b, s: (j, b * ns + s, 0))
    tile = pl.BlockSpec((t, D), lambda b, s: (b * ns + s, 0))
    return pl.pallas_call(
        body, name="hgrn_fwd", grid=(nb, ns),
        in_specs=[seg(2), seg(3), seg(4), seg(5),
                  pl.BlockSpec((2, D), lambda b, s: (0, 0)), pl.BlockSpec((1, BD), lambda b, s: (0, 0))],
        out_specs=[tile, tile, pl.BlockSpec((NB, HG_NC, BD, BD), lambda b, s: (b, s, 0, 0))],
        out_shape=[SDS((n, D), F32), SDS((n, D), BF16), SDS((nb * NB, nchunk, BD, BD), F32)],
        scratch_shapes=[pltpu.VMEM((NB, BD, BD), F32)],
        compiler_params=_params(56),
    )(z, z, z, z, lb_logits, hg_g)


def _hgrn_bwd(z, o_all, st_all, dyb, lb_logits, hg_g, nb, s_len):
    n = nb * s_len
    t = HG_T
    ns = s_len // t

    def body(q_ref, f_ref, v_ref, gb_ref, o_ref, st_ref, dyb_ref, lg_ref, g_ref,
             dz_ref, glg_ref, ghg_ref, dst, dlb):
        b, s = pl.program_id(0), pl.program_id(1)

        @pl.when((b == 0) & (s == 0))
        def _():
            ghg_ref[...] = jnp.zeros((1, BD), F32)
            dlb[...] = jnp.zeros((8, D), F32)

        @pl.when(s == 0)
        def _():
            dst[...] = jnp.zeros((NB, BD, BD), F32)

        g = g_ref[...]

        def head(h, carry):
            cols = pl.ds(pl.multiple_of(h * BD, BD), BD)
            lb = _lower_bound(lg_ref[:, cols])
            ck = _hg_tile(q_ref[0, :, cols], f_ref[0, :, cols], lb)
            q = ck["q"]
            vb = _chunks(v_ref[0, :, cols]).astype(BF16)
            gb = _chunks(gb_ref[0, :, cols])
            o = _chunks(o_ref[:, cols])
            dyb_v = _chunks(dyb_ref[:, cols])
            s_in = st_ref[h]

            sgb = _sigmoid(gb)
            r = lax.rsqrt(jnp.mean(o * o, axis=-1, keepdims=True) + EPS)
            ohat = o * r
            d_on = dyb_v * (gb * sgb)
            d_gb = dyb_v * (ohat * g) * (sgb * (1.0 + gb * (1.0 - sgb)))
            ghg_ref[...] += jnp.sum(jnp.sum(d_on * ohat, axis=1), axis=0, keepdims=True)
            tt = d_on * g
            d_o = r * (tt - ohat * jnp.mean(tt * ohat, axis=-1, keepdims=True))
            dob = d_o.astype(BF16)

            attb = ck["att"].astype(BF16)
            q_inb, k_inb = ck["q_in"].astype(BF16), ck["k_in"].astype(BF16)
            q_stb, k_stb = ck["q_st"].astype(BF16), ck["k_st"].astype(BF16)
            d_att = jnp.where(_tri(False), _bdot(dob, vb, BNT_DIMS), 0.0).astype(BF16)
            d_q_in = _bdot(d_att, k_inb, BNN_DIMS)
            d_k_in = _bdot(d_att, q_inb, BTN_DIMS)
            d_q_st = _bdot(dob, s_in.astype(BF16), BNN_DIMS)
            qdo = _bdot(dob, q_stb, BTN_DIMS)
            d_states = [None] * HG_NC + [dst[h]]
            for c in reversed(range(HG_NC)):
                d_states[c] = d_states[c + 1] * ck["dc"][c] + qdo[c]
            dst[h] = d_states[0]
            ds_out = jnp.stack(d_states[1:], axis=0)
            dsb = ds_out.astype(BF16)
            d_v = _bdot(attb, dob, BTN_DIMS) + _bdot(k_stb, dsb, BNT_DIMS)
            d_k_st = _bdot(vb, dsb, BNN_DIMS)
            d_dc = jnp.sum(ds_out * s_in, axis=1, keepdims=True)

            p_qi = d_q_in * ck["q_in"]
            p_ki = d_k_in * ck["k_in"]
            p_qs = d_q_st * ck["q_st"]
            p_ks = d_k_st * ck["k_st"]
            d_qh = (d_q_in * ck["e_qi"] + d_q_st * ck["e_qs"]) * HG_SCALE
            d_k = d_k_in * ck["e_ki"] + d_k_st * ck["e_ks"]
            d_b = (p_qi - p_ki) + (p_qs - p_ks)
            d_b_mid = jnp.sum(p_ki - p_qi, axis=1, keepdims=True)
            d_b_last = jnp.sum(p_ks, axis=1, keepdims=True) + d_dc * ck["dc"]
            rowi = lax.broadcasted_iota(jnp.int32, (HG_NC, CHUNK, BD), 1)
            d_b = d_b + jnp.where(rowi == CHUNK // 2, d_b_mid, 0.0) + jnp.where(rowi == CHUNK - 1, d_b_last, 0.0)
            d_logf = _tri_sums(True, d_b)
            d_f = d_logf / ck["f"] - d_k
            sig, sq = ck["sig"], ck["sq"]
            d_fp = d_f * (1.0 - lb) * (sig * (1.0 - sig))
            dlb[0:1, cols] += jnp.sum(jnp.sum(d_f * (1.0 - sig), axis=1), axis=0, keepdims=True)
            d_q = d_qh * (sq * (1.0 + q * (1.0 - sq)))
            dz_ref[0, :, cols] = d_q.astype(BF16).reshape(t, BD)
            dz_ref[1, :, cols] = d_fp.astype(BF16).reshape(t, BD)
            dz_ref[2, :, cols] = d_v.astype(BF16).reshape(t, BD)
            dz_ref[3, :, cols] = d_gb.astype(BF16).reshape(t, BD)
            return carry

        lax.fori_loop(0, NB, head, 0, unroll=2)

        @pl.when((b == nb - 1) & (s == ns - 1))
        def _():
            lb = _lower_bound(lg_ref[...])
            dl = dlb[0:1, :] * (lb * (1.0 - lb))
            glg_ref[0:1, :] = dl
            glg_ref[1:2, :] = -dl

    rb = lambda b, s: b * ns + (ns - 1 - s)
    seg = lambda j: pl.BlockSpec((1, t, D), lambda b, s: (j, rb(b, s), 0))
    tile = pl.BlockSpec((t, D), lambda b, s: (rb(b, s), 0))
    return pl.pallas_call(
        body, name="hgrn_bwd", grid=(nb, ns),
        in_specs=[seg(2), seg(3), seg(4), seg(5), tile,
                  pl.BlockSpec((NB, HG_NC, BD, BD), lambda b, s: (b, ns - 1 - s, 0, 0)),
                  tile, pl.BlockSpec((2, D), lambda b, s: (0, 0)), pl.BlockSpec((1, BD), lambda b, s: (0, 0))],
        out_specs=[pl.BlockSpec((4, t, D), lambda b, s: (0, rb(b, s), 0)),
                   pl.BlockSpec((2, D), lambda b, s: (0, 0)), pl.BlockSpec((1, BD), lambda b, s: (0, 0))],
        out_shape=[SDS((4, n, D), BF16), SDS((2, D), F32), SDS((1, BD), F32)],
        scratch_shapes=[pltpu.VMEM((NB, BD, BD), F32), pltpu.VMEM((8, D), F32)],
        compiler_params=_params(60),
    )(z, z, z, z, o_all, st_all, dyb, lb_logits, hg_g)


def _mid(ya, yb, z, b_merge, x2, tgt, fin_g, pa, pb, wo):
    n = x2.shape[0]
    tm = 256
    ni = n // tm

    def body(ya_ref, yb_ref, gma_ref, gmb_ref, bm_ref, x_ref, t_ref, fg_ref, pa_hbm, pb_hbm, wo_hbm,
             dx2_ref, dya_ref, dyb_ref, dgm_ref, loss_ref, gfg_ref, gbm_ref, gm_hbm,
             pa_v, pb_v, wo_v, gpa_v, gpb_v, gwo_v, sem):
        i = pl.program_id(0)
        by_owner = lambda g: g.reshape(NB, BD, D)
        loads = [pltpu.make_async_copy(src, dst, sem.at[k])
                 for k, (src, dst) in enumerate(((pa_hbm, pa_v), (pb_hbm, pb_v), (wo_hbm, wo_v)))]
        stores = [pltpu.make_async_copy(src, dst, sem.at[k])
                  for k, (src, dst) in enumerate((g, gm_hbm.at[:, pl.ds(slot * BD, BD), :])
                                                 for slot, g in enumerate((gpa_v, gpb_v, gwo_v)))]

        @pl.when(i == 0)
        def _():
            for cp in loads:
                cp.start()
            for ref in (gpa_v, gpb_v, gwo_v, loss_ref, gfg_ref, gbm_ref):
                ref[...] = jnp.zeros(ref.shape, F32)
            for cp in loads:
                cp.wait()

        ya_v = ya_ref[...]
        yb_v = yb_ref[...]
        out_a = jnp.dot(ya_v, pa_v[...], preferred_element_type=F32)
        out_b = jnp.dot(yb_v, pb_v[...], preferred_element_type=F32)
        bm = bm_ref[...]
        g_a = _sigmoid(gma_ref[0] + bm[:, 0:D])
        g_b = _sigmoid(gmb_ref[0] + bm[:, D:2 * D])
        mixed = g_a * out_a + g_b * out_b
        mixb = mixed.astype(BF16)
        xo = x_ref[...] + jnp.dot(mixb, wo_v[...], preferred_element_type=F32)
        r = lax.rsqrt(jnp.mean(xo * xo, axis=-1, keepdims=True) + EPS)
        xn = xo * r
        fg = fg_ref[...]
        e = xn * fg - t_ref[...]
        loss_ref[...] += 0.5 * jnp.sum(jnp.mean(e * e, axis=-1, keepdims=True))
        dy = e * (1.0 / D)
        gfg_ref[...] += jnp.sum(dy * xn, axis=0, keepdims=True)
        dxn = dy * fg
        dx2 = r * (dxn - xn * jnp.mean(dxn * xn, axis=-1, keepdims=True))
        dx2_ref[...] = dx2
        dx2b = dx2.astype(BF16)
        d_mixed = lax.dot_general(dx2b, wo_v[...], NT_DIMS, preferred_element_type=F32)
        gwo_v[...] += by_owner(lax.dot_general(mixb, dx2b, TN_DIMS, preferred_element_type=F32))
        d_oa = (d_mixed * g_a).astype(BF16)
        d_ob = (d_mixed * g_b).astype(BF16)
        dgm_a = (d_mixed * out_a) * (g_a * (1.0 - g_a))
        dgm_b = (d_mixed * out_b) * (g_b * (1.0 - g_b))
        gbm_ref[:, 0:D] += jnp.sum(dgm_a, axis=0, keepdims=True)
        gbm_ref[:, D:2 * D] += jnp.sum(dgm_b, axis=0, keepdims=True)
        dgm_ref[0] = dgm_a.astype(BF16)
        dgm_ref[1] = dgm_b.astype(BF16)
        dya_ref[...] = lax.dot_general(d_oa, pa_v[...], NT_DIMS, preferred_element_type=F32)
        dyb_ref[...] = lax.dot_general(d_ob, pb_v[...], NT_DIMS, preferred_element_type=F32)
        gpa_v[...] += by_owner(lax.dot_general(ya_v, d_oa, TN_DIMS, preferred_element_type=F32))
        gpb_v[...] += by_owner(lax.dot_general(yb_v, d_ob, TN_DIMS, preferred_element_type=F32))

        @pl.when(i == ni - 1)
        def _():
            for cp in stores:
                cp.start()
            for cp in stores:
                cp.wait()

    rows = pl.BlockSpec((tm, D), lambda i: (i, 0))
    rep = lambda shape: pl.BlockSpec(shape, lambda i: (0,) * len(shape))
    return pl.pallas_call(
        body, name="mid", grid=(ni,),
        in_specs=[rows, rows,
                  pl.BlockSpec((1, tm, D), lambda i: (6, i, 0)), pl.BlockSpec((1, tm, D), lambda i: (7, i, 0)),
                  rep((1, 2 * D)), rows, rows, rep((1, D)), ANY, ANY, ANY],
        out_specs=[rows, rows, rows, pl.BlockSpec((2, tm, D), lambda i: (0, i, 0)),
                   rep((8, BD)), rep((1, D)), rep((1, 2 * D)), ANY],
        out_shape=[SDS((n, D), F32), SDS((n, D), F32), SDS((n, D), F32), SDS((2, n, D), BF16),
                   SDS((8, BD), F32), SDS((1, D), F32), SDS((1, 2 * D), F32),
                   SDS((NB, MID_ROWS, D), F32)],
        scratch_shapes=[pltpu.VMEM((D, D), BF16)] * 3 + [pltpu.VMEM((NB, BD, D), F32)] * 3 + [pltpu.SemaphoreType.DMA((3,))],
        compiler_params=_params(60),
    )(ya, yb, z, z, b_merge, x2, tgt, fin_g, pa, pb, wo)


def _dz_specs(tm, ni, row_major):
    if row_major:
        ia = lambda i, j: (jnp.minimum(j, 1), i, 0)
        ib = lambda i, j: (jnp.clip(j - 2, 0, 3), i, 0)
        im = lambda i, j: (jnp.clip(j - 6, 0, 1), i, 0)
    else:
        last = ni - 1
        ia = lambda j, i: (jnp.minimum(j, 1), jnp.where(j < 2, i, last), 0)
        ib = lambda j, i: (jnp.clip(j - 2, 0, 3), jnp.where(j < 2, 0, jnp.where(j < 6, i, last)), 0)
        im = lambda j, i: (jnp.clip(j - 6, 0, 1), jnp.where(j < 6, 0, i), 0)
    return [pl.BlockSpec((1, tm, D), f) for f in (ia, ib, im)]


def _inproj_bwd_x(dza, dzb, dzm, w_all, x2, dx2, norm_g, after):
    n = x2.shape[0]
    tm = 512
    ni = n // tm

    def body(dza_ref, dzb_ref, dzm_ref, w_ref, x_ref, dx2_ref, g_ref, after_ref, gx_ref, gg_ref, acc):
        i, j = pl.program_id(0), pl.program_id(1)

        @pl.when((i == 0) & (j == 0))
        def _():
            gg_ref[...] = jnp.zeros((1, D), F32)

        @pl.when(j == 0)
        def _():
            acc[...] = jnp.zeros((tm, D), F32)

        def add(ref):
            acc[...] += lax.dot_general(ref[0], w_ref[0], NT_DIMS, preferred_element_type=F32)

        pl.when(j < 2)(lambda: add(dza_ref))
        pl.when((j >= 2) & (j < 6))(lambda: add(dzb_ref))
        pl.when(j >= 6)(lambda: add(dzm_ref))

        @pl.when(j == NB - 1)
        def _():
            x = x_ref[...]
            r = lax.rsqrt(jnp.mean(x * x, axis=-1, keepdims=True) + EPS)
            xn = x * r
            dh = acc[...]
            gg_ref[...] += jnp.sum(dh * xn, axis=0, keepdims=True)
            dxn = dh * g_ref[...]
            gx_ref[...] = dx2_ref[...] + r * (dxn - xn * jnp.mean(dxn * xn, axis=-1, keepdims=True))

    rows = pl.BlockSpec((tm, D), lambda i, j: (i, 0))
    return pl.pallas_call(
        body, name="inproj_bwd_x", grid=(ni, NB),
        in_specs=_dz_specs(tm, ni, True) + [pl.BlockSpec((1, D, D), lambda i, j: (j, 0, 0)), rows, rows,
                                             pl.BlockSpec((1, D), lambda i, j: (0, 0)), ANY],
        out_specs=[rows, pl.BlockSpec((1, D), lambda i, j: (0, 0))],
        out_shape=[SDS((n, D), F32), SDS((1, D), F32)],
        scratch_shapes=[pltpu.VMEM((tm, D), F32)],
        compiler_params=_params(48),
    )(dza, dzb, dzm, w_all, x2, dx2, norm_g, after)


def _inproj_bwd_w(dza, dzb, dzm, h_all, g_m):
    n = h_all.shape[0]
    tm = min(n, 2048)
    ni = n // tm

    def body(dza_ref, dzb_ref, dzm_ref, h_ref, gm_hbm, gw_ref, got_w, got_m, stage, send_sems, recv_sems):
        j, i = pl.program_id(0), pl.program_id(1)
        x, y, c = _place()
        sibling = (x, y, 1 - c)

        def send_w(q):
            return pltpu.make_async_remote_copy(
                src_ref=stage.at[q % 2], dst_ref=got_w.at[q], send_sem=send_sems.at[q], recv_sem=recv_sems.at[q],
                device_id=sibling, device_id_type=MESH)

        def send_m(q):
            return pltpu.make_async_remote_copy(
                src_ref=gm_hbm.at[2 * q + (1 - c)], dst_ref=got_m.at[q], send_sem=send_sems.at[4 + q],
                recv_sem=recv_sems.at[4 + q], device_id=sibling, device_id_type=MESH)

        @pl.when((j == 0) & (i == 0))
        def _():
            for q in range(4):
                send_m(q).start()

        @pl.when(i == 0)
        def _():
            gw_ref[...] = jnp.zeros((1, D, D), F32)

        def add(ref):
            gw_ref[0] += lax.dot_general(h_ref[...], ref[0], TN_DIMS, preferred_element_type=F32)

        pl.when(j < 2)(lambda: add(dza_ref))
        pl.when((j >= 2) & (j < 6))(lambda: add(dzb_ref))
        pl.when(j >= 6)(lambda: add(dzm_ref))

        for q in range(4):
            @pl.when((i == ni - 1) & (j == 2 * q + 1 - c))
            def _(q=q):
                if q >= 2:
                    send_w(q - 2).wait_send()
                stage[q % 2] = gw_ref[0].astype(BF16)
                send_w(q).start()

        @pl.when((j == NB - 1) & (i == ni - 1))
        def _():
            for q in (2, 3):
                send_w(q).wait_send()
            for q in range(4):
                send_w(q).wait_recv()
                send_m(q).wait_send()
                send_m(q).wait_recv()

    return pl.pallas_call(
        body, name="inproj_bwd_w", grid=(NB, ni),
        in_specs=_dz_specs(tm, ni, False) + [pl.BlockSpec((tm, D), lambda j, i: (i, 0)), ANY],
        out_specs=[pl.BlockSpec((1, D, D), lambda j, i: (j, 0, 0)), ANY, ANY],
        out_shape=[SDS((NB, D, D), F32), SDS((4, D, D), BF16), SDS((4,) + g_m.shape[1:], F32)],
        scratch_shapes=[pltpu.VMEM((2, D, D), BF16), pltpu.SemaphoreType.DMA((8,)), pltpu.SemaphoreType.DMA((8,))],
        compiler_params=_params(58),
    )(dza, dzb, dzm, h_all, g_m)


def _adamw(w, g, m, v):
    rows, cols = w.shape
    tr = _row_tile(rows)

    spec = pl.BlockSpec((tr, cols), lambda i: (i, 0))
    return pl.pallas_call(
        functools.partial(_adam_refs), name="adamw", grid=(rows // tr,), in_specs=[spec] * 4, out_specs=[spec] * 3,
        out_shape=[SDS((rows, cols), F32)] * 3, compiler_params=_params(32),
    )(w, g, m, v)


def _adam_refs(w_ref, g_ref, m_ref, v_ref, d_ref, nm_ref, nv_ref):
    gv = g_ref[...]
    nm = ADAM_B1 * m_ref[...] + (1.0 - ADAM_B1) * gv
    nv = ADAM_B2 * v_ref[...] + (1.0 - ADAM_B2) * (gv * gv)
    m_hat = nm / (1.0 - ADAM_B1 ** ADAM_STEP)
    v_hat = nv / (1.0 - ADAM_B2 ** ADAM_STEP)
    d_ref[...] = -ADAM_LR * (m_hat / (jnp.sqrt(v_hat) + ADAM_EPS) + ADAM_WD * w_ref[...])
    nm_ref[...] = nm
    nv_ref[...] = nv


def _adamw_small(ws, gs, ms, vs):
    k = len(ws)

    def body(*refs):
        ins, outs = refs[:4 * k], refs[4 * k:7 * k]
        vin, vout = refs[7 * k:11 * k], refs[11 * k:14 * k]
        load_sems, store_sems = refs[14 * k:]
        loads = [pltpu.make_async_copy(ins[i], vin[i], load_sems.at[i]) for i in range(4 * k)]
        for cp in loads:
            cp.start()
        for cp in loads:
            cp.wait()
        for i in range(k):
            _adam_refs(*[vin[part * k + i] for part in range(4)], *[vout[part * k + i] for part in range(3)])
        stores = [pltpu.make_async_copy(vout[i], outs[i], store_sems.at[i]) for i in range(3 * k)]
        for cp in stores:
            cp.start()
        for cp in stores:
            cp.wait()

    shapes = [SDS(w.shape, F32) for w in ws]
    vmem = [pltpu.VMEM(w.shape, F32) for w in ws]
    out = pl.pallas_call(
        body, name="adamw_small", out_shape=shapes * 3, in_specs=[HBM] * (4 * k), out_specs=[HBM] * (3 * k),
        scratch_shapes=vmem * 7 + [pltpu.SemaphoreType.DMA((4 * k,)), pltpu.SemaphoreType.DMA((3 * k,))],
        compiler_params=_params(32),
    )(*ws, *gs, *ms, *vs)
    return out[:k], out[k:2 * k], out[2 * k:]


def _allgather(blocks, dtypes, name):
    na = len(blocks)

    def body(*refs):
        ins, outs, stages = refs[:na], refs[na:2 * na], refs[2 * na:3 * na]
        send_sems, recv_sems, local_sems = refs[3 * na:]
        x, y, c = _place()
        me, sibling = (x, y, c), (x, y, 1 - c)
        chips = [(1 - x, y), (x, 1 - y), (1 - x, 1 - y)]
        blk = lambda p: 4 * p[0] + 2 * p[1] + p[2]

        def copy(a, k, block, to, src=None):
            return pltpu.make_async_remote_copy(
                src_ref=outs[a].at[blk(block)] if src is None else src, dst_ref=outs[a].at[blk(block)],
                send_sem=send_sems.at[7 * a + k], recv_sem=recv_sems.at[7 * a + k],
                device_id=to, device_id_type=MESH)

        mine, first, passed = [], [], []
        for a in range(na):
            stages[a][...] = ins[a][...].astype(dtypes[a])
            mine.append(pltpu.make_async_copy(stages[a], outs[a].at[blk(me)], local_sems.at[a]))
            mine[-1].start()
            first.append(copy(a, 0, me, sibling, src=stages[a]))
            first += [copy(a, 1 + j, me, (*chip, c), src=stages[a]) for j, chip in enumerate(chips)]
        for cp in first:
            cp.start()
        for j, chip in enumerate(chips):
            for a in range(na):
                copy(a, 1 + j, (*chip, c), me).wait_recv()
                passed.append(copy(a, 4 + j, (*chip, c), sibling))
                passed[-1].start()
        for a in range(na):
            copy(a, 0, sibling, me).wait_recv()
            for j, chip in enumerate(chips):
                copy(a, 4 + j, (*chip, 1 - c), me).wait_recv()
        for cp in first + passed:
            cp.wait_send()
        for cp in mine:
            cp.wait()

    return pl.pallas_call(
        body, name=name,
        in_specs=[pl.BlockSpec(memory_space=pltpu.VMEM)] * na, out_specs=[ANY] * na,
        out_shape=[SDS((NB,) + b.shape, dt) for b, dt in zip(blocks, dtypes)],
        scratch_shapes=[pltpu.VMEM(b.shape, dt) for b, dt in zip(blocks, dtypes)]
        + [pltpu.SemaphoreType.DMA((7 * na,)), pltpu.SemaphoreType.DMA((7 * na,)), pltpu.SemaphoreType.DMA((na,))],
        compiler_params=_params(40),
    )(*blocks)


HBM = pl.BlockSpec(memory_space=pltpu.HBM)
SEMS = pl.BlockSpec(memory_space=pltpu.SEMAPHORE)
EFFECT = pltpu.SideEffectType.DATAFLOW_SIDE_EFFECTING


def _chip_copies(srcs, lands, send_sems, recv_sems):
    x, y, c = _place()
    return [pltpu.make_async_remote_copy(
        src_ref=srcs[a].at[slot], dst_ref=lands[a].at[slot],
        send_sem=send_sems.at[3 * a + slot], recv_sem=recv_sems.at[3 * a + slot],
        device_id=(px, py, c), device_id_type=MESH)
        for a in range(len(srcs)) for slot, (px, py) in enumerate(_other_chips(x, y))]


def _split_start(name, copies, per_array, srcs, lands, after=None):
    na = len(srcs)

    def body(*refs):
        send_sems, recv_sems = refs[-2 * na - 3], refs[-2 * na - 2]
        for cp in copies(refs[:na], refs[na:2 * na], send_sems, recv_sems):
            cp.start()
        refs[-1][...] = jnp.zeros_like(refs[-1])

    hbm = lambda a: pltpu.HBM(a.shape, a.dtype)
    pin = lambda a: pltpu.with_memory_space_constraint(a, pltpu.HBM)
    out = pl.pallas_call(
        body, name=name,
        out_shape=(pltpu.SemaphoreType.DMA((per_array * na,)), pltpu.SemaphoreType.DMA((per_array * na,)),
                   *[hbm(a) for a in srcs], *[hbm(a) for a in lands], SDS((8, BD), F32)),
        in_specs=[HBM] * (2 * na) + ([] if after is None else [ANY]),
        out_specs=(SEMS, SEMS, *[HBM] * (2 * na), pl.BlockSpec(memory_space=pltpu.VMEM)),
        input_output_aliases={i: 2 + i for i in range(2 * na)},
        compiler_params=pltpu.CompilerParams(has_side_effects=EFFECT),
    )(*[pin(a) for a in srcs], *[pin(a) for a in lands], *([] if after is None else [after]))
    return out[0], out[1], out[2:2 + na], out[2 + na:2 + 2 * na], out[-1]


def _split_wait(name, copies, started, after):
    send_sems, recv_sems, srcs, lands, _ = started
    na = len(srcs)

    def body(*refs):
        waits = copies(refs[:na], refs[na:2 * na], refs[2 * na], refs[2 * na + 1])
        for cp in waits:
            cp.wait_send()
        for cp in waits:
            cp.wait_recv()

    hbm = lambda a: pltpu.HBM(a.shape, a.dtype)
    out = pl.pallas_call(
        body, name=name,
        out_shape=(*[hbm(a) for a in srcs], *[hbm(a) for a in lands]),
        in_specs=[HBM] * (2 * na) + [SEMS, SEMS, ANY],
        out_specs=tuple([HBM] * (2 * na)),
        input_output_aliases={i: i for i in range(2 * na)},
        compiler_params=pltpu.CompilerParams(has_side_effects=EFFECT),
    )(*srcs, *lands, send_sems, recv_sems, after)
    return out[na:]


def _add_sibling(place, g, a_in):
    _, r, cols = g.shape
    tr = _row_tile(r)

    def chip(k, pr):
        qx = pr[0] if k in (1, 3) else 1 - pr[0]
        qy = pr[1] if k in (0, 3) else 1 - pr[1]
        return 2 * qx + qy

    def body(place_ref, *refs):
        g_refs, a_refs, (out_ref, own_ref) = refs[0:4], refs[4:8], refs[8:10]
        for k in range(3):
            out_ref[k] = (g_refs[k][0] + a_refs[k][0].astype(F32)).astype(BF16)
        own_ref[...] = g_refs[3][0] + a_refs[3][0].astype(F32)

    mine = lambda k: pl.BlockSpec((1, tr, cols), lambda i, pr: (2 * chip(k, pr) + pr[2], i, 0))
    theirs = lambda k: pl.BlockSpec((1, tr, cols), lambda i, pr: (chip(k, pr), i, 0))
    return pl.pallas_call(
        body, name="add_sibling",
        grid_spec=pltpu.PrefetchScalarGridSpec(
            num_scalar_prefetch=1, grid=(r // tr,),
            in_specs=[mine(k) for k in range(4)] + [theirs(k) for k in range(4)],
            out_specs=[pl.BlockSpec((3, tr, cols), lambda i, pr: (0, i, 0)),
                       pl.BlockSpec((tr, cols), lambda i, pr: (i, 0))]),
        out_shape=[SDS((3, r, cols), BF16), SDS((r, cols), F32)], compiler_params=_params(48),
    )(place, *[g] * 4, *[a_in] * 4)


def _add_chips(own, b_in):
    r, cols = own.shape
    tr = _row_tile(r)

    def body(p_ref, b0_ref, b1_ref, b2_ref, o_ref):
        o_ref[...] = ((p_ref[...] + b0_ref[0].astype(F32)) + b1_ref[0].astype(F32)) + b2_ref[0].astype(F32)

    slot = lambda k: pl.BlockSpec((1, tr, cols), lambda i: (k, i, 0))
    spec = pl.BlockSpec((tr, cols), lambda i: (i, 0))
    return pl.pallas_call(
        body, name="add_chips", grid=(r // tr,), in_specs=[spec, slot(0), slot(1), slot(2)], out_specs=spec,
        out_shape=SDS((r, cols), F32), compiler_params=_params(32),
    )(own, b_in, b_in, b_in)


VEC_NAMES = ("b_merge", "conv_b", "rg_bx", "rg_ba", "rg_lambda", "hg_lb_logits", "hg_norm_g", "final_norm_g")
REP_NAMES = ("rg_wx", "rg_wa", "norm_g") + VEC_NAMES
SMALL_AT = 3 * BD
SMALL_ROWS = 48
MID_ROWS = 448


def _sum_blocks(parts):
    def body(p_ref, o_ref):
        acc = p_ref[0]
        for k in range(1, NB):
            acc = acc + p_ref[k]
        o_ref[...] = acc

    return pl.pallas_call(body, name="sum_blocks", out_shape=SDS(parts.shape[1:], F32))(parts)


def _pack_rows(arrays, width, row_multiple=8):
    flat = jnp.concatenate([a.reshape(-1) for a in arrays])
    rows = -(-flat.shape[0] // width)
    rows = -(-rows // row_multiple) * row_multiple
    return jnp.pad(flat, (0, rows * width - flat.shape[0])).reshape(rows, width)


def _unpack(flat, like):
    out, off = [], 0
    for a in like:
        out.append(flat[off:off + a.size].reshape(a.shape))
        off += a.size
    return out


def kernel(x, w_in, b_merge, conv_w, conv_b, rg_wx, rg_bx, rg_wa, rg_ba, rg_lambda, hg_lb_logits, hg_norm_g, proj_a, proj_b, w_out, norm_g, final_norm_g, loss_target, m_w_in, m_b_merge, m_conv_w, m_conv_b, m_rg_wx, m_rg_bx, m_rg_wa, m_rg_ba, m_rg_lambda, m_hg_lb_logits, m_hg_norm_g, m_proj_a, m_proj_b, m_w_out, m_norm_g, m_final_norm_g, v_w_in, v_b_merge, v_conv_w, v_conv_b, v_rg_wx, v_rg_bx, v_rg_wa, v_rg_ba, v_rg_lambda, v_hg_lb_logits, v_hg_norm_g, v_proj_a, v_proj_b, v_w_out, v_norm_g, v_final_norm_g):
    weights = dict(w_in=w_in, b_merge=b_merge, conv_w=conv_w, conv_b=conv_b, rg_wx=rg_wx, rg_bx=rg_bx, rg_wa=rg_wa,
                   rg_ba=rg_ba, rg_lambda=rg_lambda, hg_lb_logits=hg_lb_logits, hg_norm_g=hg_norm_g, proj_a=proj_a,
                   proj_b=proj_b, w_out=w_out, norm_g=norm_g, final_norm_g=final_norm_g)
    mom1 = dict(w_in=m_w_in, b_merge=m_b_merge, conv_w=m_conv_w, conv_b=m_conv_b, rg_wx=m_rg_wx, rg_bx=m_rg_bx,
                rg_wa=m_rg_wa, rg_ba=m_rg_ba, rg_lambda=m_rg_lambda, hg_lb_logits=m_hg_lb_logits,
                hg_norm_g=m_hg_norm_g, proj_a=m_proj_a, proj_b=m_proj_b, w_out=m_w_out, norm_g=m_norm_g,
                final_norm_g=m_final_norm_g)
    mom2 = dict(w_in=v_w_in, b_merge=v_b_merge, conv_w=v_conv_w, conv_b=v_conv_b, rg_wx=v_rg_wx, rg_bx=v_rg_bx,
                rg_wa=v_rg_wa, rg_ba=v_rg_ba, rg_lambda=v_rg_lambda, hg_lb_logits=v_hg_lb_logits,
                hg_norm_g=v_hg_norm_g, proj_a=v_proj_a, proj_b=v_proj_b, w_out=v_w_out, norm_g=v_norm_g,
                final_norm_g=v_final_norm_g)
    order = list(weights)
    nb, s_len, _ = x.shape
    n = nb * s_len
    px, py, pc = _place()
    place = jnp.stack([px, py, pc]).astype(jnp.int32)

    in_hbm = lambda a: pltpu.with_memory_space_constraint(a, pltpu.HBM)
    norm_gain = in_hbm(norm_g)

    x2 = x.reshape(n, D)
    cw_blk = jnp.pad(conv_w[0], ((0, 4), (0, 0)))
    order_ids = jnp.stack([_block_id(p) for p in _arrival_order(px, py, pc)]).astype(jnp.int32)
    z, h_all, w_all, pa_all, pb_all, wo_all, cw_all = _gather_inproj(
        order_ids, x2, norm_gain, [w_in[0], proj_a[0], proj_b[0], w_out[0], cw_blk], [BF16, BF16, BF16, BF16, F32])
    pa_full, pb_full, wo_full = (a.reshape(D, D) for a in (pa_all, pb_all, wo_all))
    cw8 = in_hbm(cw_all.transpose(1, 0, 2).reshape(8, D))
    wx_b, wa_b = in_hbm(rg_wx[0].astype(BF16)), in_hbm(rg_wa[0].astype(BF16))
    cb, bx, ba, lam = (in_hbm(a.reshape(1, D)) for a in (conv_b, rg_bx, rg_ba, rg_lambda))
    fin_g, b_mrg = in_hbm(final_norm_g.reshape(1, D)), in_hbm(b_merge)
    lb_lg, hg_g = in_hbm(hg_lb_logits), in_hbm(hg_norm_g)

    hlru, ya = _lru_fwd(z, cw8, cb, wx_b, wa_b, bx, ba, lam, nb, s_len)
    o_all, yb, st_all = _hgrn_fwd(z, lb_lg, hg_g, nb, s_len)

    (dx2, dya, dyb, dzm, loss_acc, g_fin, g_bm, g_mid) = _mid(
        ya, yb, z, b_mrg, x2, loss_target.reshape(n, D), fin_g, pa_full, pb_full, wo_full)
    dzb, g_lg, g_hg = _hgrn_bwd(z, o_all, st_all, dyb, lb_lg, hg_g, nb, s_len)
    dza, g_cw8, g_cb, g_wx, g_wa, g_bx, g_ba, g_lam = _lru_bwd(
        z, hlru, dya, cw8, cb, wx_b, wa_b, bx, ba, lam, nb, s_len)

    part = dict(b_merge=g_bm, conv_b=g_cb, rg_bx=g_bx, rg_ba=g_ba, rg_lambda=g_lam, hg_lb_logits=g_lg,
                hg_norm_g=g_hg, final_norm_g=g_fin)
    vec = _pack_rows([part[k] for k in VEC_NAMES], BD)
    vec = jnp.pad(vec, ((0, 16 * NB - vec.shape[0]), (0, 0))).reshape(NB, 2, D)
    rows8 = lambda a: jnp.pad(a, ((0, 0), (0, 8 - a.shape[1]), (0, 0)))
    small = jnp.concatenate([g_wx.reshape(NB, 16, D), g_wa.reshape(NB, 16, D),
                             rows8(g_cw8.reshape(8, NB, BD).transpose(1, 0, 2).reshape(NB, 1, D)), rows8(vec),
                             jnp.zeros((NB, MID_ROWS - SMALL_AT - SMALL_ROWS, D), F32)], axis=1)
    g_m = lax.dynamic_update_slice(g_mid, small, (0, SMALL_AT, 0))
    g_w, w_from_sibling, m_from_sibling = _inproj_bwd_w(dza, dzb, dzm, h_all, g_m)
    w_out_bf, w_own = _add_sibling(place, g_w, w_from_sibling)
    m_out_bf, m_own = _add_sibling(place, g_m, m_from_sibling)
    outgoing = [w_out_bf, m_out_bf]
    chip_sums = _split_start("rs_chips_start", _chip_copies, 3, outgoing, [lax.empty(a.shape, a.dtype) for a in outgoing])
    grad_x, g_ng = _inproj_bwd_x(dza, dzb, dzm, w_all, x2, dx2, norm_gain, chip_sums[-1])
    from_chips = _split_wait("rs_chips_wait", _chip_copies, chip_sums, grad_x)
    r_w = _add_chips(w_own, from_chips[0])
    r_m = _add_chips(m_own, from_chips[1])
    row = lax.broadcasted_iota(jnp.int32, (8, D), 0)
    mine = jnp.where(row == 0, g_ng, jnp.where(row == 1, loss_acc[0:1, 0:1], 0.0))
    tail = jnp.concatenate([r_m[SMALL_AT:SMALL_AT + SMALL_ROWS], mine], axis=0)
    (tail_all,) = _allgather([tail], [F32], "gather_small_grads")
    summed = _sum_blocks(tail_all[:, SMALL_ROWS:SMALL_ROWS + 8])

    grads = dict(w_in=r_w.reshape(1, D, D),
                 proj_a=r_m[0:BD].reshape(1, BD, D), proj_b=r_m[BD:2 * BD].reshape(1, BD, D),
                 w_out=r_m[2 * BD:3 * BD].reshape(1, BD, D),
                 conv_w=r_m[SMALL_AT + 32].reshape(8, BD)[0:4].reshape(1, 4, BD),
                 rg_wx=tail_all[:, 0:16].reshape(1, NB, BD, BD), rg_wa=tail_all[:, 16:32].reshape(1, NB, BD, BD),
                 norm_g=summed[0:1])
    vec_all = tail_all[:, 40:42].reshape(-1)
    for k, gk in zip(VEC_NAMES, _unpack(vec_all, [weights[k] for k in VEC_NAMES])):
        grads[k] = gk

    delta, new_m, new_v = {}, {}, {}
    flat2 = lambda a: a.reshape(-1, a.shape[-1])
    for k in ("w_in", "proj_a", "proj_b", "w_out"):
        outs = _adamw(*[flat2(t[k]) for t in (weights, grads, mom1, mom2)])
        delta[k], new_m[k], new_v[k] = (a.reshape(weights[k].shape) for a in outs)
    rep = list(REP_NAMES) + ["conv_w"]
    outs = _adamw_small(*[[flat2(t[k]) for k in rep] for t in (weights, grads, mom1, mom2)])
    for tgt, arrays in zip((delta, new_m, new_v), outs):
        for k, a in zip(rep, arrays):
            tgt[k] = a.reshape(weights[k].shape)

    return (summed[1, 0], grad_x.reshape(x.shape), *[grads[k] for k in order], *[delta[k] for k in order],
            *[new_m[k] for k in order], *[new_v[k] for k in order])
```

```python
import functools

import jax
import jax.numpy as jnp
from jax import lax
from jax.experimental import pallas as pl
from jax.experimental.pallas import tpu as pltpu

F32 = jnp.float32
BF16 = jnp.bfloat16
SDS = jax.ShapeDtypeStruct
MESH = pl.DeviceIdType.MESH
ANY = pl.BlockSpec(memory_space=pl.ANY)

D = 1024
NB = 8
BD = D // NB
CHUNK = 64
EPS = 1e-6
LRU_C = 8.0
HG_SCALE = BD ** -0.5
ADAM_LR, ADAM_B1, ADAM_B2, ADAM_EPS, ADAM_WD, ADAM_STEP = 0.001, 0.9, 0.999, 1e-08, 0.01, 10

NT_DIMS = (((1,), (1,)), ((), ()))
TN_DIMS = (((0,), (0,)), ((), ()))


def _params(vmem_mib):
    return pltpu.CompilerParams(vmem_limit_bytes=vmem_mib << 20)


def _row_tile(rows, most=256):
    assert rows % 8 == 0
    return max(t for t in range(8, min(rows, most) + 1, 8) if rows % t == 0)


def _sigmoid(v):
    return 0.5 * (jnp.tanh(0.5 * v) + 1.0)


def _groups(v):
    return v.reshape(v.shape[0] // 8, 8, v.shape[1])


def _softplus_neg(lam):
    t = -lam
    e = jnp.exp(-jnp.abs(t))
    w = 1.0 + e
    d = w - 1.0
    l1p = jnp.where(d == 0.0, e, jnp.log(w) * (e / jnp.where(d == 0.0, 1.0, d)))
    return jnp.maximum(t, 0.0) + l1p


def _place():
    return lax.axis_index("x"), lax.axis_index("y"), lax.axis_index("c")


def _other_chips(x, y):
    return [(1 - x, y), (x, 1 - y), (1 - x, 1 - y)]


def _block_id(p):
    return 4 * p[0] + 2 * p[1] + p[2]


def _core_chips(x, y, c):
    near, far, diag = _other_chips(x, y)
    pick = lambda a, b: (jnp.where(c == 0, a[0], b[0]), jnp.where(c == 0, a[1], b[1]))
    return [pick(near, far), pick(far, near), diag]


def _arrival_order(x, y, c):
    first, second, diag = _core_chips(x, y, c)
    return [(x, y, c), (x, y, 1 - c), (*first, c), (*second, 1 - c), (*second, c), (*first, 1 - c),
            (*diag, c), (*diag, 1 - c)]


def _gather_inproj(order_ids, x2, norm_g, blocks, dtypes):
    na = len(blocks)
    n = x2.shape[0]
    tm = min(n, 1024)
    ni = n // tm

    def body(order_ref, x_ref, g_ref, *refs):
        ins, (z_ref, h_ref), outs = refs[:na], refs[na:na + 2], refs[na + 2:2 * na + 2]
        stages = refs[2 * na + 2:3 * na + 2]
        h_full, wbuf, send_sems, recv_sems, local_sems, wsems, hsem = refs[3 * na + 2:]
        j, i = pl.program_id(0), pl.program_id(1)
        x, y, c = _place()
        me, sibling = (x, y, c), (x, y, 1 - c)
        chips = _core_chips(x, y, c)
        sibling_chips = [chips[1], chips[0], chips[2]]
        small = range(1, na)

        def copy(a, k, block, to, src=None):
            return pltpu.make_async_remote_copy(
                src_ref=outs[a].at[_block_id(block)] if src is None else src, dst_ref=outs[a].at[_block_id(block)],
                send_sem=send_sems.at[7 * a + k], recv_sem=recv_sems.at[7 * a + k],
                device_id=to, device_id_type=MESH)

        def local(a):
            return pltpu.make_async_copy(stages[a], outs[a].at[_block_id(me)], local_sems.at[a])

        def landed(a, slot):
            copy(a, 1 + slot, (*chips[slot], c), me).wait_recv()
            copy(a, 4 + slot, (*chips[slot], c), sibling).start()
            if slot == 0:
                copy(a, 3, (*chips[0], c), (*chips[1], c)).start()

        def diagonal_and_small():
            landed(0, 2)
            for a in small:
                landed(a, 0)
                landed(a, 1)

        def passed_on(a, slot):
            copy(a, 4 + slot, (*sibling_chips[slot], 1 - c), me).wait_recv()

        def sibling_here_send_second():
            copy(0, 0, sibling, me).wait_recv()
            for a in range(na):
                copy(a, 2, me, (*chips[1], c), src=stages[a]).start()

        @pl.when((j == 0) & (i == 0))
        def _():
            for a in range(na):
                stages[a][...] = ins[a][...].astype(dtypes[a])
                local(a).start()
            for a in range(na):
                copy(a, 0, me, sibling, src=stages[a]).start()
                copy(a, 1, me, (*chips[0], c), src=stages[a]).start()

        @pl.when(j == 0)
        def _():
            xv = x_ref[...]
            r = lax.rsqrt(jnp.mean(xv * xv, axis=-1, keepdims=True) + EPS)
            hb = ((xv * r) * g_ref[...]).astype(BF16)
            h_full[pl.ds(pl.multiple_of(i * tm, tm), tm), :] = hb

        save_h = pltpu.make_async_copy(h_full, h_ref, hsem)
        pl.when((j == 0) & (i == ni - 1))(save_h.start)

        steps = [
            lambda: local(0).wait(),
            sibling_here_send_second,
            lambda: landed(0, 0),
            lambda: passed_on(0, 0),
            lambda: landed(0, 1),
            lambda: passed_on(0, 1),
            diagonal_and_small,
            lambda: passed_on(0, 2),
        ]
        def w_load(k):
            return pltpu.make_async_copy(outs[0].at[order_ref[k]], wbuf.at[k % 2], wsems.at[k % 2])

        for k, step in enumerate(steps):
            @pl.when((j == 0) & (i == 0) if k == 0 else (j == k - 1) & (i == ni - 1))
            def _(k=k, step=step):
                step()
                w_load(k).start()

        pl.when(i == 0)(lambda: w_load(j).wait())
        z_ref[0] = jnp.dot(h_full[pl.ds(pl.multiple_of(i * tm, tm), tm), :], wbuf[j % 2], preferred_element_type=F32)

        @pl.when((j == NB - 1) & (i == ni - 1))
        def _():
            save_h.wait()
            for a in small:
                landed(a, 2)
            for a in small:
                local(a).wait()
                copy(a, 0, sibling, me).wait_recv()
                for slot in range(3):
                    passed_on(a, slot)
            for a in range(na):
                copy(a, 0, me, sibling, src=stages[a]).wait_send()
                for slot, chip in enumerate(chips):
                    copy(a, 1 + slot, me, (*chip, c), src=stages[a]).wait_send()
                    copy(a, 4 + slot, (*chip, c), sibling).wait_send()

    rows_once = lambda j, i, order: (jnp.where(j == 0, i, ni - 1), 0)
    vmem = pl.BlockSpec(memory_space=pltpu.VMEM)
    return pl.pallas_call(
        body, name="gather_inproj",
        grid_spec=pltpu.PrefetchScalarGridSpec(
            num_scalar_prefetch=1, grid=(NB, ni),
            in_specs=[pl.BlockSpec((tm, D), rows_once), pl.BlockSpec((1, D), lambda j, i, order: (0, 0))] + [vmem] * na,
            out_specs=[pl.BlockSpec((1, tm, D), lambda j, i, order: (order[j], i, 0)), ANY] + [ANY] * na,
            scratch_shapes=[pltpu.VMEM(b.shape, dt) for b, dt in zip(blocks, dtypes)]
            + [pltpu.VMEM((n, D), BF16), pltpu.VMEM((2, D, D), BF16),
               pltpu.SemaphoreType.DMA((7 * na,)), pltpu.SemaphoreType.DMA((7 * na,)),
               pltpu.SemaphoreType.DMA((na,)), pltpu.SemaphoreType.DMA((2,)), pltpu.SemaphoreType.DMA(())]),
        out_shape=[SDS((NB, n, D), F32), SDS((n, D), BF16)] + [SDS((NB,) + b.shape, dt) for b, dt in zip(blocks, dtypes)],
        compiler_params=_params(56),
    )(order_ids, x2, norm_g, *blocks)


LRU_T = 256


def _shifted(groups, shifts):
    row = lax.broadcasted_iota(jnp.int32, (groups.shape[0] - 1,) + groups.shape[1:], 1)
    out = []
    for s in shifts:
        y = pltpu.roll(groups, s % 8, 1)
        moved = jnp.where(row >= s, y[1:], y[:-1]) if s > 0 else jnp.where(row < 8 + s, y[:-1], y[1:])
        out.append(moved.reshape(-1, groups.shape[2]))
    return out


def _conv(taps, cw, cb):
    acc = taps[0] * cw[0:1, :] + taps[1] * cw[1:2, :]
    acc = acc + taps[2] * cw[2:3, :]
    acc = acc + taps[3] * cw[3:4, :]
    return cb + acc


def _lru_gates(xa, wx_ref, wa_ref, bx, ba, lam):
    xab = xa.astype(BF16)
    pis, prs = [], []
    for h in range(NB):
        xs = xab[:, h * BD:(h + 1) * BD]
        pis.append(jnp.dot(xs, wx_ref[h], preferred_element_type=F32))
        prs.append(jnp.dot(xs, wa_ref[h], preferred_element_type=F32))
    gi = _sigmoid(jnp.concatenate(pis, axis=1) + bx)
    gr = _sigmoid(jnp.concatenate(prs, axis=1) + ba)
    sp = _softplus_neg(lam)
    log_a = (-LRU_C * gr) * sp
    a = jnp.exp(log_a)
    mult = jnp.sqrt(-jnp.tanh(log_a) * (a * a + 1.0))
    return xab, gi, gr, sp, a, mult


def _lru_fwd(z, cw8, cb, wx, wa, bx, ba, lam, nb, s_len):
    n = nb * s_len
    t = LRU_T
    ns = s_len // t

    def body(xp_ref, ga_ref, cw_ref, cb_ref, wx_ref, wa_ref, bx_ref, ba_ref, lam_ref,
             h_ref, ya_ref, ext, a_s, u_s, carry):
        @pl.when(pl.program_id(1) == 0)
        def _():
            ext[0:8, :] = jnp.zeros((8, D), F32)
            carry[...] = jnp.zeros((8, D), F32)

        xp = xp_ref[0]
        ext[8:8 + t, :] = xp
        xa = _conv(_shifted(_groups(ext[...]), (3, 2, 1)) + [xp], cw_ref[...], cb_ref[...])
        ext[0:8, :] = xp[t - 8:t, :]
        _, gi, _, _, a, mult = _lru_gates(xa, wx_ref, wa_ref, bx_ref[...], ba_ref[...], lam_ref[...])
        u = (mult * gi) * xa
        a, u = _groups(a), _groups(u)
        row = lax.broadcasted_iota(jnp.int32, a.shape, 1)
        for sh in (1, 2, 4):
            a_sh = pltpu.roll(a, sh, 1)
            u_sh = pltpu.roll(u, sh, 1)
            m = row >= sh
            u = jnp.where(m, a * u_sh + u, u)
            a = jnp.where(m, a * a_sh, a)
        a_s[...] = a.reshape(t, D)
        u_s[...] = u.reshape(t, D)

        def step(g, c):
            r = pl.multiple_of(g * 8, 8)
            hg = u_s[pl.ds(r, 8), :] + a_s[pl.ds(r, 8), :] * c
            h_ref[pl.ds(r, 8), :] = hg
            return hg[7:8, :]

        c_out = lax.fori_loop(0, t // 8, step, carry[0:1, :], unroll=4)
        carry[0:1, :] = c_out
        ga = ga_ref[0]
        ya_ref[...] = (h_ref[...] * (ga * _sigmoid(ga))).astype(BF16)

    row_map = lambda b, s: (b * ns + s, 0)
    rep2 = lambda b, s: (0, 0)
    rep3 = lambda b, s: (0, 0, 0)
    return pl.pallas_call(
        body, name="lru_fwd", grid=(nb, ns),
        in_specs=[pl.BlockSpec((1, t, D), lambda b, s: (0, b * ns + s, 0)),
                  pl.BlockSpec((1, t, D), lambda b, s: (1, b * ns + s, 0)),
                  pl.BlockSpec((8, D), rep2), pl.BlockSpec((1, D), rep2),
                  pl.BlockSpec((NB, BD, BD), rep3), pl.BlockSpec((NB, BD, BD), rep3),
                  pl.BlockSpec((1, D), rep2), pl.BlockSpec((1, D), rep2), pl.BlockSpec((1, D), rep2)],
        out_specs=[pl.BlockSpec((t, D), row_map), pl.BlockSpec((t, D), row_map)],
        out_shape=[SDS((n, D), F32), SDS((n, D), BF16)],
        scratch_shapes=[pltpu.VMEM((t + 8, D), F32), pltpu.VMEM((t, D), F32), pltpu.VMEM((t, D), F32),
                        pltpu.VMEM((8, D), F32)],
        compiler_params=_params(48),
    )(z, z, cw8, cb, wx, wa, bx, ba, lam)


def _lru_bwd(z, h_all, dya, cw8, cb, wx, wa, bx, ba, lam, nb, s_len):
    n = nb * s_len
    t = LRU_T
    ns = s_len // t
    t8 = t // 8

    def body(xp_ref, xph_ref, ga_ref, h_ref, hh_ref, dya_ref, cw_ref, cb_ref, wx_ref, wa_ref, bx_ref, ba_ref,
             lam_ref, dz_ref, gcw_ref, gcb_ref, gwx_ref, gwa_ref, gbx_ref, gba_ref, glam_ref,
             ext, hext, dext, a_s, u_s, dh_s, carry):
        b, s = pl.program_id(0), pl.program_id(1)
        first_tile = s == ns - 1

        @pl.when((b == 0) & (s == 0))
        def _():
            for ref in (gcw_ref, gcb_ref, gwx_ref, gwa_ref, gbx_ref, gba_ref, glam_ref):
                ref[...] = jnp.zeros(ref.shape, F32)

        @pl.when(s == 0)
        def _():
            dext[t:t + 8, :] = jnp.zeros((8, D), F32)
            carry[...] = jnp.zeros((8, D), F32)

        keep = jnp.where(first_tile, 0.0, 1.0)
        xp = xp_ref[0]
        ext[0:8, :] = xph_ref[0] * keep
        ext[8:8 + t, :] = xp
        hext[0:8, :] = hh_ref[...] * keep
        hext[8:8 + t, :] = h_ref[...]
        cw = cw_ref[...]
        lam = lam_ref[...]
        taps = _shifted(_groups(ext[...]), (3, 2, 1)) + [xp]
        xa = _conv(taps, cw, cb_ref[...])
        xab, gi, gr, sp, a, mult = _lru_gates(xa, wx_ref, wa_ref, bx_ref[...], ba_ref[...], lam)
        (h_prev,) = _shifted(_groups(hext[...]), (1,))
        ga = ga_ref[0]
        sg = _sigmoid(ga)
        dya_v = dya_ref[...]
        d_ga = dya_v * h_ref[...] * (sg * (1.0 + ga * (1.0 - sg)))
        g_in = dya_v * (ga * sg)

        (an,) = _shifted(jnp.concatenate([_groups(a), jnp.ones((1, 8, D), F32)], axis=0), (-1,))
        an, u = _groups(an), _groups(g_in)
        row = lax.broadcasted_iota(jnp.int32, an.shape, 1)
        for sh in (1, 2, 4):
            a_sh = pltpu.roll(an, 8 - sh, 1)
            u_sh = pltpu.roll(u, 8 - sh, 1)
            m = row < 8 - sh
            u = jnp.where(m, u + an * u_sh, u)
            an = jnp.where(m, an * a_sh, an)
        a_s[...] = an.reshape(t, D)
        u_s[...] = u.reshape(t, D)

        def step(i, c):
            r = pl.multiple_of((t8 - 1 - i) * 8, 8)
            dg = u_s[pl.ds(r, 8), :] + a_s[pl.ds(r, 8), :] * c
            dh_s[pl.ds(r, 8), :] = dg
            return dg[0:1, :]

        lax.fori_loop(0, t8, step, carry[0:1, :], unroll=4)
        dh = dh_s[...]
        carry[0:1, :] = a[0:1, :] * dh[0:1, :]

        d_a = dh * h_prev
        dux = dh * xa
        d_mult = dux * gi
        d_gi = dux * mult
        d_xa = dh * (mult * gi)
        d_loga = d_a * a - d_mult * ((a * a) / mult)
        d_gr = d_loga * (-LRU_C * sp)
        d_sp = jnp.sum(d_loga * (-LRU_C * gr), axis=0, keepdims=True)
        glam_ref[...] += d_sp * (-_sigmoid(-lam))
        d_pi = d_gi * gi * (1.0 - gi)
        d_pr = d_gr * gr * (1.0 - gr)
        gbx_ref[...] += jnp.sum(d_pi, axis=0, keepdims=True)
        gba_ref[...] += jnp.sum(d_pr, axis=0, keepdims=True)
        dpib = d_pi.astype(BF16)
        dprb = d_pr.astype(BF16)
        back = []
        for h in range(NB):
            cs = slice(h * BD, (h + 1) * BD)
            gwx_ref[h] += lax.dot_general(xab[:, cs], dpib[:, cs], TN_DIMS, preferred_element_type=F32)
            gwa_ref[h] += lax.dot_general(xab[:, cs], dprb[:, cs], TN_DIMS, preferred_element_type=F32)
            back.append(lax.dot_general(dpib[:, cs], wx_ref[h], NT_DIMS, preferred_element_type=F32)
                        + lax.dot_general(dprb[:, cs], wa_ref[h], NT_DIMS, preferred_element_type=F32))
        d_xa = d_xa + jnp.concatenate(back, axis=1)

        dext[0:t, :] = d_xa
        later = _shifted(_groups(dext[...]), (-3, -2, -1))
        d_xp = later[0] * cw[0:1, :] + later[1] * cw[1:2, :]
        d_xp = d_xp + later[2] * cw[2:3, :]
        d_xp = d_xp + d_xa * cw[3:4, :]
        dext[t:t + 8, :] = d_xa[0:8, :]
        gcb_ref[...] += jnp.sum(d_xa, axis=0, keepdims=True)
        for k in range(4):
            gcw_ref[k:k + 1, :] += jnp.sum(d_xa * taps[k], axis=0, keepdims=True)
        dz_ref[0] = d_xp.astype(BF16)
        dz_ref[1] = d_ga.astype(BF16)

    rb = lambda b, s: b * ns + (ns - 1 - s)
    halo = lambda b, s: jnp.maximum(rb(b, s) * t8 - 1, 0)
    rep2 = lambda b, s: (0, 0)
    rep3 = lambda b, s: (0, 0, 0)
    return pl.pallas_call(
        body, name="lru_bwd", grid=(nb, ns),
        in_specs=[pl.BlockSpec((1, t, D), lambda b, s: (0, rb(b, s), 0)),
                  pl.BlockSpec((1, 8, D), lambda b, s: (0, halo(b, s), 0)),
                  pl.BlockSpec((1, t, D), lambda b, s: (1, rb(b, s), 0)),
                  pl.BlockSpec((t, D), lambda b, s: (rb(b, s), 0)),
                  pl.BlockSpec((8, D), lambda b, s: (halo(b, s), 0)),
                  pl.BlockSpec((t, D), lambda b, s: (rb(b, s), 0)),
                  pl.BlockSpec((8, D), rep2), pl.BlockSpec((1, D), rep2),
                  pl.BlockSpec((NB, BD, BD), rep3), pl.BlockSpec((NB, BD, BD), rep3),
                  pl.BlockSpec((1, D), rep2), pl.BlockSpec((1, D), rep2), pl.BlockSpec((1, D), rep2)],
        out_specs=[pl.BlockSpec((2, t, D), lambda b, s: (0, rb(b, s), 0)),
                   pl.BlockSpec((8, D), rep2), pl.BlockSpec((1, D), rep2),
                   pl.BlockSpec((NB, BD, BD), rep3), pl.BlockSpec((NB, BD, BD), rep3),
                   pl.BlockSpec((1, D), rep2), pl.BlockSpec((1, D), rep2), pl.BlockSpec((1, D), rep2)],
        out_shape=[SDS((2, n, D), BF16), SDS((8, D), F32), SDS((1, D), F32),
                   SDS((NB, BD, BD), F32), SDS((NB, BD, BD), F32),
                   SDS((1, D), F32), SDS((1, D), F32), SDS((1, D), F32)],
        scratch_shapes=[pltpu.VMEM((t + 8, D), F32), pltpu.VMEM((t + 8, D), F32), pltpu.VMEM((t + 8, D), F32),
                        pltpu.VMEM((t, D), F32), pltpu.VMEM((t, D), F32), pltpu.VMEM((t, D), F32),
                        pltpu.VMEM((8, D), F32)],
        compiler_params=_params(56),
    )(z, z, z, h_all, h_all, dya, cw8, cb, wx, wa, bx, ba, lam)


HG_T = 512
HG_NC = HG_T // CHUNK
BNT_DIMS = (((2,), (2,)), ((0,), (0,)))
BNN_DIMS = (((2,), (1,)), ((0,), (0,)))
BTN_DIMS = (((1,), (1,)), ((0,), (0,)))


def _lower_bound(lg):
    m = jnp.max(lg, axis=0, keepdims=True)
    e = jnp.exp(lg - m)
    return e[0:1, :] / jnp.sum(e, axis=0, keepdims=True)


def _tri(upper):
    r = lax.broadcasted_iota(jnp.int32, (HG_NC, CHUNK, CHUNK), 1)
    c = lax.broadcasted_iota(jnp.int32, (HG_NC, CHUNK, CHUNK), 2)
    return (c >= r) if upper else (r >= c)


def _bdot(a, b, dims):
    return lax.dot_general(a, b, dims, preferred_element_type=F32)


def _tri_sums(upper, a):
    tri = _tri(upper).astype(BF16)
    a1 = a.astype(BF16)
    r1 = a - a1.astype(F32)
    a2 = r1.astype(BF16)
    a3 = (r1 - a2.astype(F32)).astype(BF16)
    return _bdot(tri, a1, BNN_DIMS) + (_bdot(tri, a2, BNN_DIMS) + _bdot(tri, a3, BNN_DIMS))


def _chunks(a):
    return a.reshape(HG_NC, CHUNK, BD)


def _hg_tile(q, fp, lb):
    q, fp = _chunks(q), _chunks(fp)
    sig = _sigmoid(fp)
    f = lb + (1.0 - lb) * sig
    log_f = jnp.log(f)
    k = 1.0 - f
    b = _tri_sums(False, log_f)
    b_mid = b[:, CHUNK // 2:CHUNK // 2 + 1, :]
    b_last = b[:, CHUNK - 1:CHUNK, :]
    sq = _sigmoid(q)
    qh = q * sq
    e_qi = jnp.exp(b - b_mid)
    e_ki = jnp.exp(b_mid - b)
    e_qs = jnp.exp(b)
    e_ks = jnp.exp(b_last - b)
    dc = jnp.exp(b_last)
    q_in = (qh * e_qi) * HG_SCALE
    k_in = k * e_ki
    q_st = (qh * e_qs) * HG_SCALE
    k_st = k * e_ks
    att = _bdot(q_in.astype(BF16), k_in.astype(BF16), BNT_DIMS)
    att = jnp.where(_tri(False), att, 0.0)
    return dict(q=q, sig=sig, f=f, k=k, sq=sq, e_qi=e_qi, e_ki=e_ki, e_qs=e_qs, e_ks=e_ks, dc=dc,
                q_in=q_in, k_in=k_in, q_st=q_st, k_st=k_st, att=att)


def _hgrn_fwd(z, lb_logits, hg_g, nb, s_len):
    n = nb * s_len
    t = HG_T
    ns = s_len // t
    nchunk = s_len // CHUNK

    def body(q_ref, f_ref, v_ref, gb_ref, lg_ref, g_ref, o_ref, yb_ref, st_ref, st):
        @pl.when(pl.program_id(1) == 0)
        def _():
            st[...] = jnp.zeros((NB, BD, BD), F32)

        def head(h, carry):
            cols = pl.ds(pl.multiple_of(h * BD, BD), BD)
            lb = _lower_bound(lg_ref[:, cols])
            ck = _hg_tile(q_ref[0, :, cols], f_ref[0, :, cols], lb)
            vb = _chunks(v_ref[0, :, cols]).astype(BF16)
            kv = _bdot(vb, ck["k_st"].astype(BF16), BTN_DIMS)
            states = [st[h]]
            for c in range(HG_NC):
                states.append(states[c] * ck["dc"][c] + kv[c])
            st[h] = states[HG_NC]
            s_in = jnp.stack(states[:HG_NC], axis=0)
            st_ref[h] = s_in
            o = (_bdot(ck["att"].astype(BF16), vb, BNN_DIMS)
                 + _bdot(ck["q_st"].astype(BF16), s_in.astype(BF16), BNT_DIMS))
            o_ref[:, cols] = o.reshape(t, BD)
            r = lax.rsqrt(jnp.mean(o * o, axis=-1, keepdims=True) + EPS)
            gb = _chunks(gb_ref[0, :, cols])
            yb_ref[:, cols] = (((o * r) * g_ref[...]) * (gb * _sigmoid(gb))).astype(BF16).reshape(t, BD)
            return carry

        lax.fori_loop(0, NB, head, 0, unroll=4)

    seg = lambda j: pl.BlockSpec((1, t, D), lambda b, s: (j, b * ns + s, 0))
    tile = pl.BlockSpec((t, D), lambda b, s: (b * ns + s, 0))
    return pl.pallas_call(
        body, name="hgrn_fwd", grid=(nb, ns),
        in_specs=[seg(2), seg(3), seg(4), seg(5),
                  pl.BlockSpec((2, D), lambda b, s: (0, 0)), pl.BlockSpec((1, BD), lambda b, s: (0, 0))],
        out_specs=[tile, tile, pl.BlockSpec((NB, HG_NC, BD, BD), lambda b, s: (b, s, 0, 0))],
        out_shape=[SDS((n, D), F32), SDS((n, D), BF16), SDS((nb * NB, nchunk, BD, BD), F32)],
        scratch_shapes=[pltpu.VMEM((NB, BD, BD), F32)],
        compiler_params=_params(56),
    )(z, z, z, z, lb_logits, hg_g)


def _hgrn_bwd(z, o_all, st_all, dyb, lb_logits, hg_g, nb, s_len):
    n = nb * s_len
    t = HG_T
    ns = s_len // t

    def body(q_ref, f_ref, v_ref, gb_ref, o_ref, st_ref, dyb_ref, lg_ref, g_ref,
             dz_ref, glg_ref, ghg_ref, dst, dlb):
        b, s = pl.program_id(0), pl.program_id(1)

        @pl.when((b == 0) & (s == 0))
        def _():
            ghg_ref[...] = jnp.zeros((1, BD), F32)
            dlb[...] = jnp.zeros((8, D), F32)

        @pl.when(s == 0)
        def _():
            dst[...] = jnp.zeros((NB, BD, BD), F32)

        g = g_ref[...]

        def head(h, carry):
            cols = pl.ds(pl.multiple_of(h * BD, BD), BD)
            lb = _lower_bound(lg_ref[:, cols])
            ck = _hg_tile(q_ref[0, :, cols], f_ref[0, :, cols], lb)
            q = ck["q"]
            vb = _chunks(v_ref[0, :, cols]).astype(BF16)
            gb = _chunks(gb_ref[0, :, cols])
            o = _chunks(o_ref[:, cols])
            dyb_v = _chunks(dyb_ref[:, cols])
            s_in = st_ref[h]

            sgb = _sigmoid(gb)
            r = lax.rsqrt(jnp.mean(o * o, axis=-1, keepdims=True) + EPS)
            ohat = o * r
            d_on = dyb_v * (gb * sgb)
            d_gb = dyb_v * (ohat * g) * (sgb * (1.0 + gb * (1.0 - sgb)))
            ghg_ref[...] += jnp.sum(jnp.sum(d_on * ohat, axis=1), axis=0, keepdims=True)
            tt = d_on * g
            d_o = r * (tt - ohat * jnp.mean(tt * ohat, axis=-1, keepdims=True))
            dob = d_o.astype(BF16)

            attb = ck["att"].astype(BF16)
            q_inb, k_inb = ck["q_in"].astype(BF16), ck["k_in"].astype(BF16)
            q_stb, k_stb = ck["q_st"].astype(BF16), ck["k_st"].astype(BF16)
            d_att = jnp.where(_tri(False), _bdot(dob, vb, BNT_DIMS), 0.0).astype(BF16)
            d_q_in = _bdot(d_att, k_inb, BNN_DIMS)
            d_k_in = _bdot(d_att, q_inb, BTN_DIMS)
            d_q_st = _bdot(dob, s_in.astype(BF16), BNN_DIMS)
            qdo = _bdot(dob, q_stb, BTN_DIMS)
            d_states = [None] * HG_NC + [dst[h]]
            for c in reversed(range(HG_NC)):
                d_states[c] = d_states[c + 1] * ck["dc"][c] + qdo[c]
            dst[h] = d_states[0]
            ds_out = jnp.stack(d_states[1:], axis=0)
            dsb = ds_out.astype(BF16)
            d_v = _bdot(attb, dob, BTN_DIMS) + _bdot(k_stb, dsb, BNT_DIMS)
            d_k_st = _bdot(vb, dsb, BNN_DIMS)
            d_dc = jnp.sum(ds_out * s_in, axis=1, keepdims=True)

            p_qi = d_q_in * ck["q_in"]
            p_ki = d_k_in * ck["k_in"]
            p_qs = d_q_st * ck["q_st"]
            p_ks = d_k_st * ck["k_st"]
            d_qh = (d_q_in * ck["e_qi"] + d_q_st * ck["e_qs"]) * HG_SCALE
            d_k = d_k_in * ck["e_ki"] + d_k_st * ck["e_ks"]
            d_b = (p_qi - p_ki) + (p_qs - p_ks)
            d_b_mid = jnp.sum(p_ki - p_qi, axis=1, keepdims=True)
            d_b_last = jnp.sum(p_ks, axis=1, keepdims=True) + d_dc * ck["dc"]
            rowi = lax.broadcasted_iota(jnp.int32, (HG_NC, CHUNK, BD), 1)
            d_b = d_b + jnp.where(rowi == CHUNK // 2, d_b_mid, 0.0) + jnp.where(rowi == CHUNK - 1, d_b_last, 0.0)
            d_logf = _tri_sums(True, d_b)
            d_f = d_logf / ck["f"] - d_k
            sig, sq = ck["sig"], ck["sq"]
            d_fp = d_f * (1.0 - lb) * (sig * (1.0 - sig))
            dlb[0:1, cols] += jnp.sum(jnp.sum(d_f * (1.0 - sig), axis=1), axis=0, keepdims=True)
            d_q = d_qh * (sq * (1.0 + q * (1.0 - sq)))
            dz_ref[0, :, cols] = d_q.astype(BF16).reshape(t, BD)
            dz_ref[1, :, cols] = d_fp.astype(BF16).reshape(t, BD)
            dz_ref[2, :, cols] = d_v.astype(BF16).reshape(t, BD)
            dz_ref[3, :, cols] = d_gb.astype(BF16).reshape(t, BD)
            return carry

        lax.fori_loop(0, NB, head, 0, unroll=2)

        @pl.when((b == nb - 1) & (s == ns - 1))
        def _():
            lb = _lower_bound(lg_ref[...])
            dl = dlb[0:1, :] * (lb * (1.0 - lb))
            glg_ref[0:1, :] = dl
            glg_ref[1:2, :] = -dl

    rb = lambda b, s: b * ns + (ns - 1 - s)
    seg = lambda j: pl.BlockSpec((1, t, D), lambda b, s: (j, rb(b, s), 0))
    tile = pl.BlockSpec((t, D), lambda b, s: (rb(b, s), 0))
    return pl.pallas_call(
        body, name="hgrn_bwd", grid=(nb, ns),
        in_specs=[seg(2), seg(3), seg(4), seg(5), tile,
                  pl.BlockSpec((NB, HG_NC, BD, BD), lambda b, s: (b, ns - 1 - s, 0, 0)),
                  tile, pl.BlockSpec((2, D), lambda b, s: (0, 0)), pl.BlockSpec((1, BD), lambda b, s: (0, 0))],
        out_specs=[pl.BlockSpec((4, t, D), lambda b, s: (0, rb(b, s), 0)),
                   pl.BlockSpec((2, D), lambda b, s: (0, 0)), pl.BlockSpec((1, BD), lambda b, s: (0, 0))],
        out_shape=[SDS((4, n, D), BF16), SDS((2, D), F32), SDS((1, BD), F32)],
        scratch_shapes=[pltpu.VMEM((NB, BD, BD), F32), pltpu.VMEM((8, D), F32)],
        compiler_params=_params(60),
    )(z, z, z, z, o_all, st_all, dyb, lb_logits, hg_g)


def _mid(ya, yb, z, b_merge, x2, tgt, fin_g, pa, pb, wo):
    n = x2.shape[0]
    tm = 256
    ni = n // tm

    def body(ya_ref, yb_ref, gma_ref, gmb_ref, bm_ref, x_ref, t_ref, fg_ref, pa_hbm, pb_hbm, wo_hbm,
             dx2_ref, dya_ref, dyb_ref, dgm_ref, loss_ref, gfg_ref, gbm_ref, gm_hbm,
             pa_v, pb_v, wo_v, gpa_v, gpb_v, gwo_v, sem):
        i = pl.program_id(0)
        by_owner = lambda g: g.reshape(NB, BD, D)
        loads = [pltpu.make_async_copy(src, dst, sem.at[k])
                 for k, (src, dst) in enumerate(((pa_hbm, pa_v), (pb_hbm, pb_v), (wo_hbm, wo_v)))]
        stores = [pltpu.make_async_copy(src, dst, sem.at[k])
                  for k, (src, dst) in enumerate((g, gm_hbm.at[:, pl.ds(slot * BD, BD), :])
                                                 for slot, g in enumerate((gpa_v, gpb_v, gwo_v)))]

        @pl.when(i == 0)
        def _():
            for cp in loads:
                cp.start()
            for ref in (gpa_v, gpb_v, gwo_v, loss_ref, gfg_ref, gbm_ref):
                ref[...] = jnp.zeros(ref.shape, F32)
            for cp in loads:
                cp.wait()

        ya_v = ya_ref[...]
        yb_v = yb_ref[...]
        out_a = jnp.dot(ya_v, pa_v[...], preferred_element_type=F32)
        out_b = jnp.dot(yb_v, pb_v[...], preferred_element_type=F32)
        bm = bm_ref[...]
        g_a = _sigmoid(gma_ref[0] + bm[:, 0:D])
        g_b = _sigmoid(gmb_ref[0] + bm[:, D:2 * D])
        mixed = g_a * out_a + g_b * out_b
        mixb = mixed.astype(BF16)
        xo = x_ref[...] + jnp.dot(mixb, wo_v[...], preferred_element_type=F32)
        r = lax.rsqrt(jnp.mean(xo * xo, axis=-1, keepdims=True) + EPS)
        xn = xo * r
        fg = fg_ref[...]
        e = xn * fg - t_ref[...]
        loss_ref[...] += 0.5 * jnp.sum(jnp.mean(e * e, axis=-1, keepdims=True))
        dy = e * (1.0 / D)
        gfg_ref[...] += jnp.sum(dy * xn, axis=0, keepdims=True)
        dxn = dy * fg
        dx2 = r * (dxn - xn * jnp.mean(dxn * xn, axis=-1, keepdims=True))
        dx2_ref[...] = dx2
        dx2b = dx2.astype(BF16)
        d_mixed = lax.dot_general(dx2b, wo_v[...], NT_DIMS, preferred_element_type=F32)
        gwo_v[...] += by_owner(lax.dot_general(mixb, dx2b, TN_DIMS, preferred_element_type=F32))
        d_oa = (d_mixed * g_a).astype(BF16)
        d_ob = (d_mixed * g_b).astype(BF16)
        dgm_a = (d_mixed * out_a) * (g_a * (1.0 - g_a))
        dgm_b = (d_mixed * out_b) * (g_b * (1.0 - g_b))
        gbm_ref[:, 0:D] += jnp.sum(dgm_a, axis=0, keepdims=True)
        gbm_ref[:, D:2 * D] += jnp.sum(dgm_b, axis=0, keepdims=True)
        dgm_ref[0] = dgm_a.astype(BF16)
        dgm_ref[1] = dgm_b.astype(BF16)
        dya_ref[...] = lax.dot_general(d_oa, pa_v[...], NT_DIMS, preferred_element_type=F32)
        dyb_ref[...] = lax.dot_general(d_ob, pb_v[...], NT_DIMS, preferred_element_type=F32)
        gpa_v[...] += by_owner(lax.dot_general(ya_v, d_oa, TN_DIMS, preferred_element_type=F32))
        gpb_v[...] += by_owner(lax.dot_general(yb_v, d_ob, TN_DIMS, preferred_element_type=F32))

        @pl.when(i == ni - 1)
        def _():
            for cp in stores:
                cp.start()
            for cp in stores:
                cp.wait()

    rows = pl.BlockSpec((tm, D), lambda i: (i, 0))
    rep = lambda shape: pl.BlockSpec(shape, lambda i: (0,) * len(shape))
    return pl.pallas_call(
        body, name="mid", grid=(ni,),
        in_specs=[rows, rows,
                  pl.BlockSpec((1, tm, D), lambda i: (6, i, 0)), pl.BlockSpec((1, tm, D), lambda i: (7, i, 0)),
                  rep((1, 2 * D)), rows, rows, rep((1, D)), ANY, ANY, ANY],
        out_specs=[rows, rows, rows, pl.BlockSpec((2, tm, D), lambda i: (0, i, 0)),
                   rep((8, BD)), rep((1, D)), rep((1, 2 * D)), ANY],
        out_shape=[SDS((n, D), F32), SDS((n, D), F32), SDS((n, D), F32), SDS((2, n, D), BF16),
                   SDS((8, BD), F32), SDS((1, D), F32), SDS((1, 2 * D), F32),
                   SDS((NB, MID_ROWS, D), F32)],
        scratch_shapes=[pltpu.VMEM((D, D), BF16)] * 3 + [pltpu.VMEM((NB, BD, D), F32)] * 3 + [pltpu.SemaphoreType.DMA((3,))],
        compiler_params=_params(60),
    )(ya, yb, z, z, b_merge, x2, tgt, fin_g, pa, pb, wo)


def _dz_specs(tm, ni, row_major):
    if row_major:
        ia = lambda i, j: (jnp.minimum(j, 1), i, 0)
        ib = lambda i, j: (jnp.clip(j - 2, 0, 3), i, 0)
        im = lambda i, j: (jnp.clip(j - 6, 0, 1), i, 0)
    else:
        last = ni - 1
        ia = lambda j, i: (jnp.minimum(j, 1), jnp.where(j < 2, i, last), 0)
        ib = lambda j, i: (jnp.clip(j - 2, 0, 3), jnp.where(j < 2, 0, jnp.where(j < 6, i, last)), 0)
        im = lambda j, i: (jnp.clip(j - 6, 0, 1), jnp.where(j < 6, 0, i), 0)
    return [pl.BlockSpec((1, tm, D), f) for f in (ia, ib, im)]


def _inproj_bwd_x(dza, dzb, dzm, w_all, x2, dx2, norm_g, after):
    n = x2.shape[0]
    tm = 512
    ni = n // tm

    def body(dza_ref, dzb_ref, dzm_ref, w_ref, x_ref, dx2_ref, g_ref, after_ref, gx_ref, gg_ref, acc):
        i, j = pl.program_id(0), pl.program_id(1)

        @pl.when((i == 0) & (j == 0))
        def _():
            gg_ref[...] = jnp.zeros((1, D), F32)

        @pl.when(j == 0)
        def _():
            acc[...] = jnp.zeros((tm, D), F32)

        def add(ref):
            acc[...] += lax.dot_general(ref[0], w_ref[0], NT_DIMS, preferred_element_type=F32)

        pl.when(j < 2)(lambda: add(dza_ref))
        pl.when((j >= 2) & (j < 6))(lambda: add(dzb_ref))
        pl.when(j >= 6)(lambda: add(dzm_ref))

        @pl.when(j == NB - 1)
        def _():
            x = x_ref[...]
            r = lax.rsqrt(jnp.mean(x * x, axis=-1, keepdims=True) + EPS)
            xn = x * r
            dh = acc[...]
            gg_ref[...] += jnp.sum(dh * xn, axis=0, keepdims=True)
            dxn = dh * g_ref[...]
            gx_ref[...] = dx2_ref[...] + r * (dxn - xn * jnp.mean(dxn * xn, axis=-1, keepdims=True))

    rows = pl.BlockSpec((tm, D), lambda i, j: (i, 0))
    return pl.pallas_call(
        body, name="inproj_bwd_x", grid=(ni, NB),
        in_specs=_dz_specs(tm, ni, True) + [pl.BlockSpec((1, D, D), lambda i, j: (j, 0, 0)), rows, rows,
                                             pl.BlockSpec((1, D), lambda i, j: (0, 0)), ANY],
        out_specs=[rows, pl.BlockSpec((1, D), lambda i, j: (0, 0))],
        out_shape=[SDS((n, D), F32), SDS((1, D), F32)],
        scratch_shapes=[pltpu.VMEM((tm, D), F32)],
        compiler_params=_params(48),
    )(dza, dzb, dzm, w_all, x2, dx2, norm_g, after)


def _inproj_bwd_w(dza, dzb, dzm, h_all, g_m):
    n = h_all.shape[0]
    tm = min(n, 2048)
    ni = n // tm

    def body(dza_ref, dzb_ref, dzm_ref, h_ref, gm_hbm, gw_ref, got_w, got_m, stage, send_sems, recv_sems):
        j, i = pl.program_id(0), pl.program_id(1)
        x, y, c = _place()
        sibling = (x, y, 1 - c)

        def send_w(q):
            return pltpu.make_async_remote_copy(
                src_ref=stage.at[q % 2], dst_ref=got_w.at[q], send_sem=send_sems.at[q], recv_sem=recv_sems.at[q],
                device_id=sibling, device_id_type=MESH)

        def send_m(q):
            return pltpu.make_async_remote_copy(
                src_ref=gm_hbm.at[2 * q + (1 - c)], dst_ref=got_m.at[q], send_sem=send_sems.at[4 + q],
                recv_sem=recv_sems.at[4 + q], device_id=sibling, device_id_type=MESH)

        @pl.when((j == 0) & (i == 0))
        def _():
            for q in range(4):
                send_m(q).start()

        @pl.when(i == 0)
        def _():
            gw_ref[...] = jnp.zeros((1, D, D), F32)

        def add(ref):
            gw_ref[0] += lax.dot_general(h_ref[...], ref[0], TN_DIMS, preferred_element_type=F32)

        pl.when(j < 2)(lambda: add(dza_ref))
        pl.when((j >= 2) & (j < 6))(lambda: add(dzb_ref))
        pl.when(j >= 6)(lambda: add(dzm_ref))

        for q in range(4):
            @pl.when((i == ni - 1) & (j == 2 * q + 1 - c))
            def _(q=q):
                if q >= 2:
                    send_w(q - 2).wait_send()
                stage[q % 2] = gw_ref[0].astype(BF16)
                send_w(q).start()

        @pl.when((j == NB - 1) & (i == ni - 1))
        def _():
            for q in (2, 3):
                send_w(q).wait_send()
            for q in range(4):
                send_w(q).wait_recv()
                send_m(q).wait_send()
                send_m(q).wait_recv()

    return pl.pallas_call(
        body, name="inproj_bwd_w", grid=(NB, ni),
        in_specs=_dz_specs(tm, ni, False) + [pl.BlockSpec((tm, D), lambda j, i: (i, 0)), ANY],
        out_specs=[pl.BlockSpec((1, D, D), lambda j, i: (j, 0, 0)), ANY, ANY],
        out_shape=[SDS((NB, D, D), F32), SDS((4, D, D), BF16), SDS((4,) + g_m.shape[1:], F32)],
        scratch_shapes=[pltpu.VMEM((2, D, D), BF16), pltpu.SemaphoreType.DMA((8,)), pltpu.SemaphoreType.DMA((8,))],
        compiler_params=_params(58),
    )(dza, dzb, dzm, h_all, g_m)


def _adamw(w, g, m, v):
    rows, cols = w.shape
    tr = _row_tile(rows)

    spec = pl.BlockSpec((tr, cols), lambda i: (i, 0))
    return pl.pallas_call(
        functools.partial(_adam_refs), name="adamw", grid=(rows // tr,), in_specs=[spec] * 4, out_specs=[spec] * 3,
        out_shape=[SDS((rows, cols), F32)] * 3, compiler_params=_params(32),
    )(w, g, m, v)


def _adam_refs(w_ref, g_ref, m_ref, v_ref, d_ref, nm_ref, nv_ref):
    gv = g_ref[...]
    nm = ADAM_B1 * m_ref[...] + (1.0 - ADAM_B1) * gv
    nv = ADAM_B2 * v_ref[...] + (1.0 - ADAM_B2) * (gv * gv)
    m_hat = nm / (1.0 - ADAM_B1 ** ADAM_STEP)
    v_hat = nv / (1.0 - ADAM_B2 ** ADAM_STEP)
    d_ref[...] = -ADAM_LR * (m_hat / (jnp.sqrt(v_hat) + ADAM_EPS) + ADAM_WD * w_ref[...])
    nm_ref[...] = nm
    nv_ref[...] = nv


def _adamw_small(ws, gs, ms, vs):
    k = len(ws)

    def body(*refs):
        ins, outs = refs[:4 * k], refs[4 * k:7 * k]
        vin, vout = refs[7 * k:11 * k], refs[11 * k:14 * k]
        load_sems, store_sems = refs[14 * k:]
        loads = [pltpu.make_async_copy(ins[i], vin[i], load_sems.at[i]) for i in range(4 * k)]
        for cp in loads:
            cp.start()
        for cp in loads:
            cp.wait()
        for i in range(k):
            _adam_refs(*[vin[part * k + i] for part in range(4)], *[vout[part * k + i] for part in range(3)])
        stores = [pltpu.make_async_copy(vout[i], outs[i], store_sems.at[i]) for i in range(3 * k)]
        for cp in stores:
            cp.start()
        for cp in stores:
            cp.wait()

    shapes = [SDS(w.shape, F32) for w in ws]
    vmem = [pltpu.VMEM(w.shape, F32) for w in ws]
    out = pl.pallas_call(
        body, name="adamw_small", out_shape=shapes * 3, in_specs=[HBM] * (4 * k), out_specs=[HBM] * (3 * k),
        scratch_shapes=vmem * 7 + [pltpu.SemaphoreType.DMA((4 * k,)), pltpu.SemaphoreType.DMA((3 * k,))],
        compiler_params=_params(32),
    )(*ws, *gs, *ms, *vs)
    return out[:k], out[k:2 * k], out[2 * k:]


def _allgather(blocks, dtypes, name):
    na = len(blocks)

    def body(*refs):
        ins, outs, stages = refs[:na], refs[na:2 * na], refs[2 * na:3 * na]
        send_sems, recv_sems, local_sems = refs[3 * na:]
        x, y, c = _place()
        me, sibling = (x, y, c), (x, y, 1 - c)
        chips = [(1 - x, y), (x, 1 - y), (1 - x, 1 - y)]
        blk = lambda p: 4 * p[0] + 2 * p[1] + p[2]

        def copy(a, k, block, to, src=None):
            return pltpu.make_async_remote_copy(
                src_ref=outs[a].at[blk(block)] if src is None else src, dst_ref=outs[a].at[blk(block)],
                send_sem=send_sems.at[7 * a + k], recv_sem=recv_sems.at[7 * a + k],
                device_id=to, device_id_type=MESH)

        mine, first, passed = [], [], []
        for a in range(na):
            stages[a][...] = ins[a][...].astype(dtypes[a])
            mine.append(pltpu.make_async_copy(stages[a], outs[a].at[blk(me)], local_sems.at[a]))
            mine[-1].start()
            first.append(copy(a, 0, me, sibling, src=stages[a]))
            first += [copy(a, 1 + j, me, (*chip, c), src=stages[a]) for j, chip in enumerate(chips)]
        for cp in first:
            cp.start()
        for j, chip in enumerate(chips):
            for a in range(na):
                copy(a, 1 + j, (*chip, c), me).wait_recv()
                passed.append(copy(a, 4 + j, (*chip, c), sibling))
                passed[-1].start()
        for a in range(na):
            copy(a, 0, sibling, me).wait_recv()
            for j, chip in enumerate(chips):
                copy(a, 4 + j, (*chip, 1 - c), me).wait_recv()
        for cp in first + passed:
            cp.wait_send()
        for cp in mine:
            cp.wait()

    return pl.pallas_call(
        body, name=name,
        in_specs=[pl.BlockSpec(memory_space=pltpu.VMEM)] * na, out_specs=[ANY] * na,
        out_shape=[SDS((NB,) + b.shape, dt) for b, dt in zip(blocks, dtypes)],
        scratch_shapes=[pltpu.VMEM(b.shape, dt) for b, dt in zip(blocks, dtypes)]
        + [pltpu.SemaphoreType.DMA((7 * na,)), pltpu.SemaphoreType.DMA((7 * na,)), pltpu.SemaphoreType.DMA((na,))],
        compiler_params=_params(40),
    )(*blocks)


HBM = pl.BlockSpec(memory_space=pltpu.HBM)
SEMS = pl.BlockSpec(memory_space=pltpu.SEMAPHORE)
EFFECT = pltpu.SideEffectType.DATAFLOW_SIDE_EFFECTING


def _chip_copies(srcs, lands, send_sems, recv_sems):
    x, y, c = _place()
    return [pltpu.make_async_remote_copy(
        src_ref=srcs[a].at[slot], dst_ref=lands[a].at[slot],
        send_sem=send_sems.at[3 * a + slot], recv_sem=recv_sems.at[3 * a + slot],
        device_id=(px, py, c), device_id_type=MESH)
        for a in range(len(srcs)) for slot, (px, py) in enumerate(_other_chips(x, y))]


def _split_start(name, copies, per_array, srcs, lands, after=None):
    na = len(srcs)

    def body(*refs):
        send_sems, recv_sems = refs[-2 * na - 3], refs[-2 * na - 2]
        for cp in copies(refs[:na], refs[na:2 * na], send_sems, recv_sems):
            cp.start()
        refs[-1][...] = jnp.zeros_like(refs[-1])

    hbm = lambda a: pltpu.HBM(a.shape, a.dtype)
    pin = lambda a: pltpu.with_memory_space_constraint(a, pltpu.HBM)
    out = pl.pallas_call(
        body, name=name,
        out_shape=(pltpu.SemaphoreType.DMA((per_array * na,)), pltpu.SemaphoreType.DMA((per_array * na,)),
                   *[hbm(a) for a in srcs], *[hbm(a) for a in lands], SDS((8, BD), F32)),
        in_specs=[HBM] * (2 * na) + ([] if after is None else [ANY]),
        out_specs=(SEMS, SEMS, *[HBM] * (2 * na), pl.BlockSpec(memory_space=pltpu.VMEM)),
        input_output_aliases={i: 2 + i for i in range(2 * na)},
        compiler_params=pltpu.CompilerParams(has_side_effects=EFFECT),
    )(*[pin(a) for a in srcs], *[pin(a) for a in lands], *([] if after is None else [after]))
    return out[0], out[1], out[2:2 + na], out[2 + na:2 + 2 * na], out[-1]


def _split_wait(name, copies, started, after):
    send_sems, recv_sems, srcs, lands, _ = started
    na = len(srcs)

    def body(*refs):
        waits = copies(refs[:na], refs[na:2 * na], refs[2 * na], refs[2 * na + 1])
        for cp in waits:
            cp.wait_send()
        for cp in waits:
            cp.wait_recv()

    hbm = lambda a: pltpu.HBM(a.shape, a.dtype)
    out = pl.pallas_call(
        body, name=name,
        out_shape=(*[hbm(a) for a in srcs], *[hbm(a) for a in lands]),
        in_specs=[HBM] * (2 * na) + [SEMS, SEMS, ANY],
        out_specs=tuple([HBM] * (2 * na)),
        input_output_aliases={i: i for i in range(2 * na)},
        compiler_params=pltpu.CompilerParams(has_side_effects=EFFECT),
    )(*srcs, *lands, send_sems, recv_sems, after)
    return out[na:]


def _add_sibling(place, g, a_in):
    _, r, cols = g.shape
    tr = _row_tile(r)

    def chip(k, pr):
        qx = pr[0] if k in (1, 3) else 1 - pr[0]
        qy = pr[1] if k in (0, 3) else 1 - pr[1]
        return 2 * qx + qy

    def body(place_ref, *refs):
        g_refs, a_refs, (out_ref, own_ref) = refs[0:4], refs[4:8], refs[8:10]
        for k in range(3):
            out_ref[k] = (g_refs[k][0] + a_refs[k][0].astype(F32)).astype(BF16)
        own_ref[...] = g_refs[3][0] + a_refs[3][0].astype(F32)

    mine = lambda k: pl.BlockSpec((1, tr, cols), lambda i, pr: (2 * chip(k, pr) + pr[2], i, 0))
    theirs = lambda k: pl.BlockSpec((1, tr, cols), lambda i, pr: (chip(k, pr), i, 0))
    return pl.pallas_call(
        body, name="add_sibling",
        grid_spec=pltpu.PrefetchScalarGridSpec(
            num_scalar_prefetch=1, grid=(r // tr,),
            in_specs=[mine(k) for k in range(4)] + [theirs(k) for k in range(4)],
            out_specs=[pl.BlockSpec((3, tr, cols), lambda i, pr: (0, i, 0)),
                       pl.BlockSpec((tr, cols), lambda i, pr: (i, 0))]),
        out_shape=[SDS((3, r, cols), BF16), SDS((r, cols), F32)], compiler_params=_params(48),
    )(place, *[g] * 4, *[a_in] * 4)


def _add_chips(own, b_in):
    r, cols = own.shape
    tr = _row_tile(r)

    def body(p_ref, b0_ref, b1_ref, b2_ref, o_ref):
        o_ref[...] = ((p_ref[...] + b0_ref[0].astype(F32)) + b1_ref[0].astype(F32)) + b2_ref[0].astype(F32)

    slot = lambda k: pl.BlockSpec((1, tr, cols), lambda i: (k, i, 0))
    spec = pl.BlockSpec((tr, cols), lambda i: (i, 0))
    return pl.pallas_call(
        body, name="add_chips", grid=(r // tr,), in_specs=[spec, slot(0), slot(1), slot(2)], out_specs=spec,
        out_shape=SDS((r, cols), F32), compiler_params=_params(32),
    )(own, b_in, b_in, b_in)


VEC_NAMES = ("b_merge", "conv_b", "rg_bx", "rg_ba", "rg_lambda", "hg_lb_logits", "hg_norm_g", "final_norm_g")
REP_NAMES = ("rg_wx", "rg_wa", "norm_g") + VEC_NAMES
SMALL_AT = 3 * BD
SMALL_ROWS = 48
MID_ROWS = 448


def _sum_blocks(parts):
    def body(p_ref, o_ref):
        acc = p_ref[0]
        for k in range(1, NB):
            acc = acc + p_ref[k]
        o_ref[...] = acc

    return pl.pallas_call(body, name="sum_blocks", out_shape=SDS(parts.shape[1:], F32))(parts)


def _pack_rows(arrays, width, row_multiple=8):
    flat = jnp.concatenate([a.reshape(-1) for a in arrays])
    rows = -(-flat.shape[0] // width)
    rows = -(-rows // row_multiple) * row_multiple
    return jnp.pad(flat, (0, rows * width - flat.shape[0])).reshape(rows, width)


def _unpack(flat, like):
    out, off = [], 0
    for a in like:
        out.append(flat[off:off + a.size].reshape(a.shape))
        off += a.size
    return out


def kernel(x, w_in, b_merge, conv_w, conv_b, rg_wx, rg_bx, rg_wa, rg_ba, rg_lambda, hg_lb_logits, hg_norm_g, proj_a, proj_b, w_out, norm_g, final_norm_g, loss_target, m_w_in, m_b_merge, m_conv_w, m_conv_b, m_rg_wx, m_rg_bx, m_rg_wa, m_rg_ba, m_rg_lambda, m_hg_lb_logits, m_hg_norm_g, m_proj_a, m_proj_b, m_w_out, m_norm_g, m_final_norm_g, v_w_in, v_b_merge, v_conv_w, v_conv_b, v_rg_wx, v_rg_bx, v_rg_wa, v_rg_ba, v_rg_lambda, v_hg_lb_logits, v_hg_norm_g, v_proj_a, v_proj_b, v_w_out, v_norm_g, v_final_norm_g):
    weights = dict(w_in=w_in, b_merge=b_merge, conv_w=conv_w, conv_b=conv_b, rg_wx=rg_wx, rg_bx=rg_bx, rg_wa=rg_wa,
                   rg_ba=rg_ba, rg_lambda=rg_lambda, hg_lb_logits=hg_lb_logits, hg_norm_g=hg_norm_g, proj_a=proj_a,
                   proj_b=proj_b, w_out=w_out, norm_g=norm_g, final_norm_g=final_norm_g)
    mom1 = dict(w_in=m_w_in, b_merge=m_b_merge, conv_w=m_conv_w, conv_b=m_conv_b, rg_wx=m_rg_wx, rg_bx=m_rg_bx,
                rg_wa=m_rg_wa, rg_ba=m_rg_ba, rg_lambda=m_rg_lambda, hg_lb_logits=m_hg_lb_logits,
                hg_norm_g=m_hg_norm_g, proj_a=m_proj_a, proj_b=m_proj_b, w_out=m_w_out, norm_g=m_norm_g,
                final_norm_g=m_final_norm_g)
    mom2 = dict(w_in=v_w_in, b_merge=v_b_merge, conv_w=v_conv_w, conv_b=v_conv_b, rg_wx=v_rg_wx, rg_bx=v_rg_bx,
                rg_wa=v_rg_wa, rg_ba=v_rg_ba, rg_lambda=v_rg_lambda, hg_lb_logits=v_hg_lb_logits,
                hg_norm_g=v_hg_norm_g, proj_a=v_proj_a, proj_b=v_proj_b, w_out=v_w_out, norm_g=v_norm_g,
                final_norm_g=v_final_norm_g)
    order = list(weights)
    nb, s_len, _ = x.shape
    n = nb * s_len
    px, py, pc = _place()
    place = jnp.stack([px, py, pc]).astype(jnp.int32)

    in_hbm = lambda a: pltpu.with_memory_space_constraint(a, pltpu.HBM)
    norm_gain = in_hbm(norm_g)

    x2 = x.reshape(n, D)
    cw_blk = jnp.pad(conv_w[0], ((0, 4), (0, 0)))
    order_ids = jnp.stack([_block_id(p) for p in _arrival_order(px, py, pc)]).astype(jnp.int32)
    z, h_all, w_all, pa_all, pb_all, wo_all, cw_all = _gather_inproj(
        order_ids, x2, norm_gain, [w_in[0], proj_a[0], proj_b[0], w_out[0], cw_blk], [BF16, BF16, BF16, BF16, F32])
    pa_full, pb_full, wo_full = (a.reshape(D, D) for a in (pa_all, pb_all, wo_all))
    cw8 = in_hbm(cw_all.transpose(1, 0, 2).reshape(8, D))
    wx_b, wa_b = in_hbm(rg_wx[0].astype(BF16)), in_hbm(rg_wa[0].astype(BF16))
    cb, bx, ba, lam = (in_hbm(a.reshape(1, D)) for a in (conv_b, rg_bx, rg_ba, rg_lambda))
    fin_g, b_mrg = in_hbm(final_norm_g.reshape(1, D)), in_hbm(b_merge)
    lb_lg, hg_g = in_hbm(hg_lb_logits), in_hbm(hg_norm_g)

    hlru, ya = _lru_fwd(z, cw8, cb, wx_b, wa_b, bx, ba, lam, nb, s_len)
    o_all, yb, st_all = _hgrn_fwd(z, lb_lg, hg_g, nb, s_len)

    (dx2, dya, dyb, dzm, loss_acc, g_fin, g_bm, g_mid) = _mid(
        ya, yb, z, b_mrg, x2, loss_target.reshape(n, D), fin_g, pa_full, pb_full, wo_full)
    dzb, g_lg, g_hg = _hgrn_bwd(z, o_all, st_all, dyb, lb_lg, hg_g, nb, s_len)
    dza, g_cw8, g_cb, g_wx, g_wa, g_bx, g_ba, g_lam = _lru_bwd(
        z, hlru, dya, cw8, cb, wx_b, wa_b, bx, ba, lam, nb, s_len)

    part = dict(b_merge=g_bm, conv_b=g_cb, rg_bx=g_bx, rg_ba=g_ba, rg_lambda=g_lam, hg_lb_logits=g_lg,
                hg_norm_g=g_hg, final_norm_g=g_fin)
    vec = _pack_rows([part[k] for k in VEC_NAMES], BD)
    vec = jnp.pad(vec, ((0, 16 * NB - vec.shape[0]), (0, 0))).reshape(NB, 2, D)
    rows8 = lambda a: jnp.pad(a, ((0, 0), (0, 8 - a.shape[1]), (0, 0)))
    small = jnp.concatenate([g_wx.reshape(NB, 16, D), g_wa.reshape(NB, 16, D),
                             rows8(g_cw8.reshape(8, NB, BD).transpose(1, 0, 2).reshape(NB, 1, D)), rows8(vec),
                             jnp.zeros((NB, MID_ROWS - SMALL_AT - SMALL_ROWS, D), F32)], axis=1)
    g_m = lax.dynamic_update_slice(g_mid, small, (0, SMALL_AT, 0))
    g_w, w_from_sibling, m_from_sibling = _inproj_bwd_w(dza, dzb, dzm, h_all, g_m)
    w_out_bf, w_own = _add_sibling(place, g_w, w_from_sibling)
    m_out_bf, m_own = _add_sibling(place, g_m, m_from_sibling)
    outgoing = [w_out_bf, m_out_bf]
    chip_sums = _split_start("rs_chips_start", _chip_copies, 3, outgoing, [lax.empty(a.shape, a.dtype) for a in outgoing])
    grad_x, g_ng = _inproj_bwd_x(dza, dzb, dzm, w_all, x2, dx2, norm_gain, chip_sums[-1])
    from_chips = _split_wait("rs_chips_wait", _chip_copies, chip_sums, grad_x)
    r_w = _add_chips(w_own, from_chips[0])
    r_m = _add_chips(m_own, from_chips[1])
    row = lax.broadcasted_iota(jnp.int32, (8, D), 0)
    mine = jnp.where(row == 0, g_ng, jnp.where(row == 1, loss_acc[0:1, 0:1], 0.0))
    tail = jnp.concatenate([r_m[SMALL_AT:SMALL_AT + SMALL_ROWS], mine], axis=0)
    (tail_all,) = _allgather([tail], [F32], "gather_small_grads")
    summed = _sum_blocks(tail_all[:, SMALL_ROWS:SMALL_ROWS + 8])

    grads = dict(w_in=r_w.reshape(1, D, D),
                 proj_a=r_m[0:BD].reshape(1, BD, D), proj_b=r_m[BD:2 * BD].reshape(1, BD, D),
                 w_out=r_m[2 * BD:3 * BD].reshape(1, BD, D),
                 conv_w=r_m[SMALL_AT + 32].reshape(8, BD)[0:4].reshape(1, 4, BD),
                 rg_wx=tail_all[:, 0:16].reshape(1, NB, BD, BD), rg_wa=tail_all[:, 16:32].reshape(1, NB, BD, BD),
                 norm_g=summed[0:1])
    vec_all = tail_all[:, 40:42].reshape(-1)
    for k, gk in zip(VEC_NAMES, _unpack(vec_all, [weights[k] for k in VEC_NAMES])):
        grads[k] = gk

    delta, new_m, new_v = {}, {}, {}
    flat2 = lambda a: a.reshape(-1, a.shape[-1])
    for k in ("w_in", "proj_a", "proj_b", "w_out"):
        outs = _adamw(*[flat2(t[k]) for t in (weights, grads, mom1, mom2)])
        delta[k], new_m[k], new_v[k] = (a.reshape(weights[k].shape) for a in outs)
    rep = list(REP_NAMES) + ["conv_w"]
    outs = _adamw_small(*[[flat2(t[k]) for k in rep] for t in (weights, grads, mom1, mom2)])
    for tgt, arrays in zip((delta, new_m, new_v), outs):
        for k, a in zip(rep, arrays):
            tgt[k] = a.reshape(weights[k].shape)

    return (summed[1, 0], grad_x.reshape(x.shape), *[grads[k] for k in order], *[delta[k] for k in order],
            *[new_m[k] for k in order], *[new_v[k] for k in order])
```

```python
import functools

import jax
import jax.numpy as jnp
from jax import lax
from jax.experimental import pallas as pl
from jax.experimental.pallas import tpu as pltpu

F32 = jnp.float32
BF16 = jnp.bfloat16
SDS = jax.ShapeDtypeStruct
MESH = pl.DeviceIdType.MESH
ANY = pl.BlockSpec(memory_space=pl.ANY)

D = 1024
NB = 8
BD = D // NB
CHUNK = 64
EPS = 1e-6
LRU_C = 8.0
HG_SCALE = BD ** -0.5
ADAM_LR, ADAM_B1, ADAM_B2, ADAM_EPS, ADAM_WD, ADAM_STEP = 0.001, 0.9, 0.999, 1e-08, 0.01, 10

NT_DIMS = (((1,), (1,)), ((), ()))
TN_DIMS = (((0,), (0,)), ((), ()))


def _params(vmem_mib):
    return pltpu.CompilerParams(vmem_limit_bytes=vmem_mib << 20)


def _row_tile(rows, most=256):
    assert rows % 8 == 0
    return max(t for t in range(8, min(rows, most) + 1, 8) if rows % t == 0)


def _sigmoid(v):
    return 0.5 * (jnp.tanh(0.5 * v) + 1.0)


def _groups(v):
    return v.reshape(v.shape[0] // 8, 8, v.shape[1])


def _softplus_neg(lam):
    t = -lam
    e = jnp.exp(-jnp.abs(t))
    w = 1.0 + e
    d = w - 1.0
    l1p = jnp.where(d == 0.0, e, jnp.log(w) * (e / jnp.where(d == 0.0, 1.0, d)))
    return jnp.maximum(t, 0.0) + l1p


def _place():
    return lax.axis_index("x"), lax.axis_index("y"), lax.axis_index("c")


def _other_chips(x, y):
    return [(1 - x, y), (x, 1 - y), (1 - x, 1 - y)]


def _block_id(p):
    return 4 * p[0] + 2 * p[1] + p[2]


def _core_chips(x, y, c):
    near, far, diag = _other_chips(x, y)
    pick = lambda a, b: (jnp.where(c == 0, a[0], b[0]), jnp.where(c == 0, a[1], b[1]))
    return [pick(near, far), pick(far, near), diag]


def _arrival_order(x, y, c):
    first, second, diag = _core_chips(x, y, c)
    return [(x, y, c), (x, y, 1 - c), (*first, c), (*second, 1 - c), (*second, c), (*first, 1 - c),
            (*diag, c), (*diag, 1 - c)]


def _gather_inproj(order_ids, x2, norm_g, blocks, dtypes):
    na = len(blocks)
    n = x2.shape[0]
    tm = min(n, 1024)
    ni = n // tm

    def body(order_ref, x_ref, g_ref, *refs):
        ins, (z_ref, h_ref), outs = refs[:na], refs[na:na + 2], refs[na + 2:2 * na + 2]
        stages = refs[2 * na + 2:3 * na + 2]
        h_full, wbuf, send_sems, recv_sems, local_sems, wsems, hsem = refs[3 * na + 2:]
        j, i = pl.program_id(0), pl.program_id(1)
        x, y, c = _place()
        me, sibling = (x, y, c), (x, y, 1 - c)
        chips = _core_chips(x, y, c)
        sibling_chips = [chips[1], chips[0], chips[2]]
        small = range(1, na)

        def copy(a, k, block, to, src=None):
            return pltpu.make_async_remote_copy(
                src_ref=outs[a].at[_block_id(block)] if src is None else src, dst_ref=outs[a].at[_block_id(block)],
                send_sem=send_sems.at[7 * a + k], recv_sem=recv_sems.at[7 * a + k],
                device_id=to, device_id_type=MESH)

        def local(a):
            return pltpu.make_async_copy(stages[a], outs[a].at[_block_id(me)], local_sems.at[a])

        def landed(a, slot):
            copy(a, 1 + slot, (*chips[slot], c), me).wait_recv()
            copy(a, 4 + slot, (*chips[slot], c), sibling).start()
            if slot == 0:
                copy(a, 3, (*chips[0], c), (*chips[1], c)).start()

        def diagonal_and_small():
            landed(0, 2)
            for a in small:
                landed(a, 0)
                landed(a, 1)

        def passed_on(a, slot):
            copy(a, 4 + slot, (*sibling_chips[slot], 1 - c), me).wait_recv()

        def sibling_here_send_second():
            copy(0, 0, sibling, me).wait_recv()
            for a in range(na):
                copy(a, 2, me, (*chips[1], c), src=stages[a]).start()

        @pl.when((j == 0) & (i == 0))
        def _():
            for a in range(na):
                stages[a][...] = ins[a][...].astype(dtypes[a])
                local(a).start()
            for a in range(na):
                copy(a, 0, me, sibling, src=stages[a]).start()
                copy(a, 1, me, (*chips[0], c), src=stages[a]).start()

        @pl.when(j == 0)
        def _():
            xv = x_ref[...]
            r = lax.rsqrt(jnp.mean(xv * xv, axis=-1, keepdims=True) + EPS)
            hb = ((xv * r) * g_ref[...]).astype(BF16)
            h_full[pl.ds(pl.multiple_of(i * tm, tm), tm), :] = hb

        save_h = pltpu.make_async_copy(h_full, h_ref, hsem)
        pl.when((j == 0) & (i == ni - 1))(save_h.start)

        steps = [
            lambda: local(0).wait(),
            sibling_here_send_second,
            lambda: landed(0, 0),
            lambda: passed_on(0, 0),
            lambda: landed(0, 1),
            lambda: passed_on(0, 1),
            diagonal_and_small,
            lambda: passed_on(0, 2),
        ]
        def w_load(k):
            return pltpu.make_async_copy(outs[0].at[order_ref[k]], wbuf.at[k % 2], wsems.at[k % 2])

        for k, step in enumerate(steps):
            @pl.when((j == 0) & (i == 0) if k == 0 else (j == k - 1) & (i == ni - 1))
            def _(k=k, step=step):
                step()
                w_load(k).start()

        pl.when(i == 0)(lambda: w_load(j).wait())
        z_ref[0] = jnp.dot(h_full[pl.ds(pl.multiple_of(i * tm, tm), tm), :], wbuf[j % 2], preferred_element_type=F32)

        @pl.when((j == NB - 1) & (i == ni - 1))
        def _():
            save_h.wait()
            for a in small:
                landed(a, 2)
            for a in small:
                local(a).wait()
                copy(a, 0, sibling, me).wait_recv()
                for slot in range(3):
                    passed_on(a, slot)
            for a in range(na):
                copy(a, 0, me, sibling, src=stages[a]).wait_send()
                for slot, chip in enumerate(chips):
                    copy(a, 1 + slot, me, (*chip, c), src=stages[a]).wait_send()
                    copy(a, 4 + slot, (*chip, c), sibling).wait_send()

    rows_once = lambda j, i, order: (jnp.where(j == 0, i, ni - 1), 0)
    vmem = pl.BlockSpec(memory_space=pltpu.VMEM)
    return pl.pallas_call(
        body, name="gather_inproj",
        grid_spec=pltpu.PrefetchScalarGridSpec(
            num_scalar_prefetch=1, grid=(NB, ni),
            in_specs=[pl.BlockSpec((tm, D), rows_once), pl.BlockSpec((1, D), lambda j, i, order: (0, 0))] + [vmem] * na,
            out_specs=[pl.BlockSpec((1, tm, D), lambda j, i, order: (order[j], i, 0)), ANY] + [ANY] * na,
            scratch_shapes=[pltpu.VMEM(b.shape, dt) for b, dt in zip(blocks, dtypes)]
            + [pltpu.VMEM((n, D), BF16), pltpu.VMEM((2, D, D), BF16),
               pltpu.SemaphoreType.DMA((7 * na,)), pltpu.SemaphoreType.DMA((7 * na,)),
               pltpu.SemaphoreType.DMA((na,)), pltpu.SemaphoreType.DMA((2,)), pltpu.SemaphoreType.DMA(())]),
        out_shape=[SDS((NB, n, D), F32), SDS((n, D), BF16)] + [SDS((NB,) + b.shape, dt) for b, dt in zip(blocks, dtypes)],
        compiler_params=_params(56),
    )(order_ids, x2, norm_g, *blocks)


LRU_T = 256


def _shifted(groups, shifts):
    row = lax.broadcasted_iota(jnp.int32, (groups.shape[0] - 1,) + groups.shape[1:], 1)
    out = []
    for s in shifts:
        y = pltpu.roll(groups, s % 8, 1)
        moved = jnp.where(row >= s, y[1:], y[:-1]) if s > 0 else jnp.where(row < 8 + s, y[:-1], y[1:])
        out.append(moved.reshape(-1, groups.shape[2]))
    return out


def _conv(taps, cw, cb):
    acc = taps[0] * cw[0:1, :] + taps[1] * cw[1:2, :]
    acc = acc + taps[2] * cw[2:3, :]
    acc = acc + taps[3] * cw[3:4, :]
    return cb + acc


def _lru_gates(xa, wx_ref, wa_ref, bx, ba, lam):
    xab = xa.astype(BF16)
    pis, prs = [], []
    for h in range(NB):
        xs = xab[:, h * BD:(h + 1) * BD]
        pis.append(jnp.dot(xs, wx_ref[h], preferred_element_type=F32))
        prs.append(jnp.dot(xs, wa_ref[h], preferred_element_type=F32))
    gi = _sigmoid(jnp.concatenate(pis, axis=1) + bx)
    gr = _sigmoid(jnp.concatenate(prs, axis=1) + ba)
    sp = _softplus_neg(lam)
    log_a = (-LRU_C * gr) * sp
    a = jnp.exp(log_a)
    mult = jnp.sqrt(-jnp.tanh(log_a) * (a * a + 1.0))
    return xab, gi, gr, sp, a, mult


def _lru_fwd(z, cw8, cb, wx, wa, bx, ba, lam, nb, s_len):
    n = nb * s_len
    t = LRU_T
    ns = s_len // t

    def body(xp_ref, ga_ref, cw_ref, cb_ref, wx_ref, wa_ref, bx_ref, ba_ref, lam_ref,
             h_ref, ya_ref, ext, a_s, u_s, carry):
        @pl.when(pl.program_id(1) == 0)
        def _():
            ext[0:8, :] = jnp.zeros((8, D), F32)
            carry[...] = jnp.zeros((8, D), F32)

        xp = xp_ref[0]
        ext[8:8 + t, :] = xp
        xa = _conv(_shifted(_groups(ext[...]), (3, 2, 1)) + [xp], cw_ref[...], cb_ref[...])
        ext[0:8, :] = xp[t - 8:t, :]
        _, gi, _, _, a, mult = _lru_gates(xa, wx_ref, wa_ref, bx_ref[...], ba_ref[...], lam_ref[...])
        u = (mult * gi) * xa
        a, u = _groups(a), _groups(u)
        row = lax.broadcasted_iota(jnp.int32, a.shape, 1)
        for sh in (1, 2, 4):
            a_sh = pltpu.roll(a, sh, 1)
            u_sh = pltpu.roll(u, sh, 1)
            m = row >= sh
            u = jnp.where(m, a * u_sh + u, u)
            a = jnp.where(m, a * a_sh, a)
        a_s[...] = a.reshape(t, D)
        u_s[...] = u.reshape(t, D)

        def step(g, c):
            r = pl.multiple_of(g * 8, 8)
            hg = u_s[pl.ds(r, 8), :] + a_s[pl.ds(r, 8), :] * c
            h_ref[pl.ds(r, 8), :] = hg
            return hg[7:8, :]

        c_out = lax.fori_loop(0, t // 8, step, carry[0:1, :], unroll=4)
        carry[0:1, :] = c_out
        ga = ga_ref[0]
        ya_ref[...] = (h_ref[...] * (ga * _sigmoid(ga))).astype(BF16)

    row_map = lambda b, s: (b * ns + s, 0)
    rep2 = lambda b, s: (0, 0)
    rep3 = lambda b, s: (0, 0, 0)
    return pl.pallas_call(
        body, name="lru_fwd", grid=(nb, ns),
        in_specs=[pl.BlockSpec((1, t, D), lambda b, s: (0, b * ns + s, 0)),
                  pl.BlockSpec((1, t, D), lambda b, s: (1, b * ns + s, 0)),
                  pl.BlockSpec((8, D), rep2), pl.BlockSpec((1, D), rep2),
                  pl.BlockSpec((NB, BD, BD), rep3), pl.BlockSpec((NB, BD, BD), rep3),
                  pl.BlockSpec((1, D), rep2), pl.BlockSpec((1, D), rep2), pl.BlockSpec((1, D), rep2)],
        out_specs=[pl.BlockSpec((t, D), row_map), pl.BlockSpec((t, D), row_map)],
        out_shape=[SDS((n, D), F32), SDS((n, D), BF16)],
        scratch_shapes=[pltpu.VMEM((t + 8, D), F32), pltpu.VMEM((t, D), F32), pltpu.VMEM((t, D), F32),
                        pltpu.VMEM((8, D), F32)],
        compiler_params=_params(48),
    )(z, z, cw8, cb, wx, wa, bx, ba, lam)


def _lru_bwd(z, h_all, dya, cw8, cb, wx, wa, bx, ba, lam, nb, s_len):
    n = nb * s_len
    t = LRU_T
    ns = s_len // t
    t8 = t // 8

    def body(xp_ref, xph_ref, ga_ref, h_ref, hh_ref, dya_ref, cw_ref, cb_ref, wx_ref, wa_ref, bx_ref, ba_ref,
             lam_ref, dz_ref, gcw_ref, gcb_ref, gwx_ref, gwa_ref, gbx_ref, gba_ref, glam_ref,
             ext, hext, dext, a_s, u_s, dh_s, carry):
        b, s = pl.program_id(0), pl.program_id(1)
        first_tile = s == ns - 1

        @pl.when((b == 0) & (s == 0))
        def _():
            for ref in (gcw_ref, gcb_ref, gwx_ref, gwa_ref, gbx_ref, gba_ref, glam_ref):
                ref[...] = jnp.zeros(ref.shape, F32)

        @pl.when(s == 0)
        def _():
            dext[t:t + 8, :] = jnp.zeros((8, D), F32)
            carry[...] = jnp.zeros((8, D), F32)

        keep = jnp.where(first_tile, 0.0, 1.0)
        xp = xp_ref[0]
        ext[0:8, :] = xph_ref[0] * keep
        ext[8:8 + t, :] = xp
        hext[0:8, :] = hh_ref[...] * keep
        hext[8:8 + t, :] = h_ref[...]
        cw = cw_ref[...]
        lam = lam_ref[...]
        taps = _shifted(_groups(ext[...]), (3, 2, 1)) + [xp]
        xa = _conv(taps, cw, cb_ref[...])
        xab, gi, gr, sp, a, mult = _lru_gates(xa, wx_ref, wa_ref, bx_ref[...], ba_ref[...], lam)
        (h_prev,) = _shifted(_groups(hext[...]), (1,))
        ga = ga_ref[0]
        sg = _sigmoid(ga)
        dya_v = dya_ref[...]
        d_ga = dya_v * h_ref[...] * (sg * (1.0 + ga * (1.0 - sg)))
        g_in = dya_v * (ga * sg)

        (an,) = _shifted(jnp.concatenate([_groups(a), jnp.ones((1, 8, D), F32)], axis=0), (-1,))
        an, u = _groups(an), _groups(g_in)
        row = lax.broadcasted_iota(jnp.int32, an.shape, 1)
        for sh in (1, 2, 4):
            a_sh = pltpu.roll(an, 8 - sh, 1)
            u_sh = pltpu.roll(u, 8 - sh, 1)
            m = row < 8 - sh
            u = jnp.where(m, u + an * u_sh, u)
            an = jnp.where(m, an * a_sh, an)
        a_s[...] = an.reshape(t, D)
        u_s[...] = u.reshape(t, D)

        def step(i, c):
            r = pl.multiple_of((t8 - 1 - i) * 8, 8)
            dg = u_s[pl.ds(r, 8), :] + a_s[pl.ds(r, 8), :] * c
            dh_s[pl.ds(r, 8), :] = dg
            return dg[0:1, :]

        lax.fori_loop(0, t8, step, carry[0:1, :], unroll=4)
        dh = dh_s[...]
        carry[0:1, :] = a[0:1, :] * dh[0:1, :]

        d_a = dh * h_prev
        dux = dh * xa
        d_mult = dux * gi
        d_gi = dux * mult
        d_xa = dh * (mult * gi)
        d_loga = d_a * a - d_mult * ((a * a) / mult)
        d_gr = d_loga * (-LRU_C * sp)
        d_sp = jnp.sum(d_loga * (-LRU_C * gr), axis=0, keepdims=True)
        glam_ref[...] += d_sp * (-_sigmoid(-lam))
        d_pi = d_gi * gi * (1.0 - gi)
        d_pr = d_gr * gr * (1.0 - gr)
        gbx_ref[...] += jnp.sum(d_pi, axis=0, keepdims=True)
        gba_ref[...] += jnp.sum(d_pr, axis=0, keepdims=True)
        dpib = d_pi.astype(BF16)
        dprb = d_pr.astype(BF16)
        back = []
        for h in range(NB):
            cs = slice(h * BD, (h + 1) * BD)
            gwx_ref[h] += lax.dot_general(xab[:, cs], dpib[:, cs], TN_DIMS, preferred_element_type=F32)
            gwa_ref[h] += lax.dot_general(xab[:, cs], dprb[:, cs], TN_DIMS, preferred_element_type=F32)
            back.append(lax.dot_general(dpib[:, cs], wx_ref[h], NT_DIMS, preferred_element_type=F32)
                        + lax.dot_general(dprb[:, cs], wa_ref[h], NT_DIMS, preferred_element_type=F32))
        d_xa = d_xa + jnp.concatenate(back, axis=1)

        dext[0:t, :] = d_xa
        later = _shifted(_groups(dext[...]), (-3, -2, -1))
        d_xp = later[0] * cw[0:1, :] + later[1] * cw[1:2, :]
        d_xp = d_xp + later[2] * cw[2:3, :]
        d_xp = d_xp + d_xa * cw[3:4, :]
        dext[t:t + 8, :] = d_xa[0:8, :]
        gcb_ref[...] += jnp.sum(d_xa, axis=0, keepdims=True)
        for k in range(4):
            gcw_ref[k:k + 1, :] += jnp.sum(d_xa * taps[k], axis=0, keepdims=True)
        dz_ref[0] = d_xp.astype(BF16)
        dz_ref[1] = d_ga.astype(BF16)

    rb = lambda b, s: b * ns + (ns - 1 - s)
    halo = lambda b, s: jnp.maximum(rb(b, s) * t8 - 1, 0)
    rep2 = lambda b, s: (0, 0)
    rep3 = lambda b, s: (0, 0, 0)
    return pl.pallas_call(
        body, name="lru_bwd", grid=(nb, ns),
        in_specs=[pl.BlockSpec((1, t, D), lambda b, s: (0, rb(b, s), 0)),
                  pl.BlockSpec((1, 8, D), lambda b, s: (0, halo(b, s), 0)),
                  pl.BlockSpec((1, t, D), lambda b, s: (1, rb(b, s), 0)),
                  pl.BlockSpec((t, D), lambda b, s: (rb(b, s), 0)),
                  pl.BlockSpec((8, D), lambda b, s: (halo(b, s), 0)),
                  pl.BlockSpec((t, D), lambda b, s: (rb(b, s), 0)),
                  pl.BlockSpec((8, D), rep2), pl.BlockSpec((1, D), rep2),
                  pl.BlockSpec((NB, BD, BD), rep3), pl.BlockSpec((NB, BD, BD), rep3),
                  pl.BlockSpec((1, D), rep2), pl.BlockSpec((1, D), rep2), pl.BlockSpec((1, D), rep2)],
        out_specs=[pl.BlockSpec((2, t, D), lambda b, s: (0, rb(b, s), 0)),
                   pl.BlockSpec((8, D), rep2), pl.BlockSpec((1, D), rep2),
                   pl.BlockSpec((NB, BD, BD), rep3), pl.BlockSpec((NB, BD, BD), rep3),
                   pl.BlockSpec((1, D), rep2), pl.BlockSpec((1, D), rep2), pl.BlockSpec((1, D), rep2)],
        out_shape=[SDS((2, n, D), BF16), SDS((8, D), F32), SDS((1, D), F32),
                   SDS((NB, BD, BD), F32), SDS((NB, BD, BD), F32),
                   SDS((1, D), F32), SDS((1, D), F32), SDS((1, D), F32)],
        scratch_shapes=[pltpu.VMEM((t + 8, D), F32), pltpu.VMEM((t + 8, D), F32), pltpu.VMEM((t + 8, D), F32),
                        pltpu.VMEM((t, D), F32), pltpu.VMEM((t, D), F32), pltpu.VMEM((t, D), F32),
                        pltpu.VMEM((8, D), F32)],
        compiler_params=_params(56),
    )(z, z, z, h_all, h_all, dya, cw8, cb, wx, wa, bx, ba, lam)


HG_T = 512
HG_NC = HG_T // CHUNK
BNT_DIMS = (((2,), (2,)), ((0,), (0,)))
BNN_DIMS = (((2,), (1,)), ((0,), (0,)))
BTN_DIMS = (((1,), (1,)), ((0,), (0,)))


def _lower_bound(lg):
    m = jnp.max(lg, axis=0, keepdims=True)
    e = jnp.exp(lg - m)
    return e[0:1, :] / jnp.sum(e, axis=0, keepdims=True)


def _tri(upper):
    r = lax.broadcasted_iota(jnp.int32, (HG_NC, CHUNK, CHUNK), 1)
    c = lax.broadcasted_iota(jnp.int32, (HG_NC, CHUNK, CHUNK), 2)
    return (c >= r) if upper else (r >= c)


def _bdot(a, b, dims):
    return lax.dot_general(a, b, dims, preferred_element_type=F32)


def _tri_sums(upper, a):
    tri = _tri(upper).astype(BF16)
    a1 = a.astype(BF16)
    r1 = a - a1.astype(F32)
    a2 = r1.astype(BF16)
    a3 = (r1 - a2.astype(F32)).astype(BF16)
    return _bdot(tri, a1, BNN_DIMS) + (_bdot(tri, a2, BNN_DIMS) + _bdot(tri, a3, BNN_DIMS))


def _chunks(a):
    return a.reshape(HG_NC, CHUNK, BD)


def _hg_tile(q, fp, lb):
    q, fp = _chunks(q), _chunks(fp)
    sig = _sigmoid(fp)
    f = lb + (1.0 - lb) * sig
    log_f = jnp.log(f)
    k = 1.0 - f
    b = _tri_sums(False, log_f)
    b_mid = b[:, CHUNK // 2:CHUNK // 2 + 1, :]
    b_last = b[:, CHUNK - 1:CHUNK, :]
    sq = _sigmoid(q)
    qh = q * sq
    e_qi = jnp.exp(b - b_mid)
    e_ki = jnp.exp(b_mid - b)
    e_qs = jnp.exp(b)
    e_ks = jnp.exp(b_last - b)
    dc = jnp.exp(b_last)
    q_in = (qh * e_qi) * HG_SCALE
    k_in = k * e_ki
    q_st = (qh * e_qs) * HG_SCALE
    k_st = k * e_ks
    att = _bdot(q_in.astype(BF16), k_in.astype(BF16), BNT_DIMS)
    att = jnp.where(_tri(False), att, 0.0)
    return dict(q=q, sig=sig, f=f, k=k, sq=sq, e_qi=e_qi, e_ki=e_ki, e_qs=e_qs, e_ks=e_ks, dc=dc,
                q_in=q_in, k_in=k_in, q_st=q_st, k_st=k_st, att=att)


def _hgrn_fwd(z, lb_logits, hg_g, nb, s_len):
    n = nb * s_len
    t = HG_T
    ns = s_len // t
    nchunk = s_len // CHUNK

    def body(q_ref, f_ref, v_ref, gb_ref, lg_ref, g_ref, o_ref, yb_ref, st_ref, st):
        @pl.when(pl.program_id(1) == 0)
        def _():
            st[...] = jnp.zeros((NB, BD, BD), F32)

        def head(h, carry):
            cols = pl.ds(pl.multiple_of(h * BD, BD), BD)
            lb = _lower_bound(lg_ref[:, cols])
            ck = _hg_tile(q_ref[0, :, cols], f_ref[0, :, cols], lb)
            vb = _chunks(v_ref[0, :, cols]).astype(BF16)
            kv = _bdot(vb, ck["k_st"].astype(BF16), BTN_DIMS)
            states = [st[h]]
            for c in range(HG_NC):
                states.append(states[c] * ck["dc"][c] + kv[c])
            st[h] = states[HG_NC]
            s_in = jnp.stack(states[:HG_NC], axis=0)
            st_ref[h] = s_in
            o = (_bdot(ck["att"].astype(BF16), vb, BNN_DIMS)
                 + _bdot(ck["q_st"].astype(BF16), s_in.astype(BF16), BNT_DIMS))
            o_ref[:, cols] = o.reshape(t, BD)
            r = lax.rsqrt(jnp.mean(o * o, axis=-1, keepdims=True) + EPS)
            gb = _chunks(gb_ref[0, :, cols])
            yb_ref[:, cols] = (((o * r) * g_ref[...]) * (gb * _sigmoid(gb))).astype(BF16).reshape(t, BD)
            return carry

        lax.fori_loop(0, NB, head, 0, unroll=4)

    seg = lambda j: pl.BlockSpec((1, t, D), lambda b, s: (j, b * ns + s, 0))
    tile = pl.BlockSpec((t, D), lambda b, s: (b * ns + s, 0))
    return pl.pallas_call(
        body, name="hgrn_fwd", grid=(nb, ns),
        in_specs=[seg(2), seg(3), seg(4), seg(5),
                  pl.BlockSpec((2, D), lambda b, s: (0, 0)), pl.BlockSpec((1, BD), lambda b, s: (0, 0))],
        out_specs=[tile, tile, pl.BlockSpec((NB, HG_NC, BD, BD), lambda b, s: (b, s, 0, 0))],
        out_shape=[SDS((n, D), F32), SDS((n, D), BF16), SDS((nb * NB, nchunk, BD, BD), F32)],
        scratch_shapes=[pltpu.VMEM((NB, BD, BD), F32)],
        compiler_params=_params(56),
    )(z, z, z, z, lb_logits, hg_g)


def _hgrn_bwd(z, o_all, st_all, dyb, lb_logits, hg_g, nb, s_len):
    n = nb * s_len
    t = HG_T
    ns = s_len // t

    def body(q_ref, f_ref, v_ref, gb_ref, o_ref, st_ref, dyb_ref, lg_ref, g_ref,
             dz_ref, glg_ref, ghg_ref, dst, dlb):
        b, s = pl.program_id(0), pl.program_id(1)

        @pl.when((b == 0) & (s == 0))
        def _():
            ghg_ref[...] = jnp.zeros((1, BD), F32)
            dlb[...] = jnp.zeros((8, D), F32)

        @pl.when(s == 0)
        def _():
            dst[...] = jnp.zeros((NB, BD, BD), F32)

        g = g_ref[...]

        def head(h, carry):
            cols = pl.ds(pl.multiple_of(h * BD, BD), BD)
            lb = _lower_bound(lg_ref[:, cols])
            ck = _hg_tile(q_ref[0, :, cols], f_ref[0, :, cols], lb)
            q = ck["q"]
            vb = _chunks(v_ref[0, :, cols]).astype(BF16)
            gb = _chunks(gb_ref[0, :, cols])
            o = _chunks(o_ref[:, cols])
            dyb_v = _chunks(dyb_ref[:, cols])
            s_in = st_ref[h]

            sgb = _sigmoid(gb)
            r = lax.rsqrt(jnp.mean(o * o, axis=-1, keepdims=True) + EPS)
            ohat = o * r
            d_on = dyb_v * (gb * sgb)
            d_gb = dyb_v * (ohat * g) * (sgb * (1.0 + gb * (1.0 - sgb)))
            ghg_ref[...] += jnp.sum(jnp.sum(d_on * ohat, axis=1), axis=0, keepdims=True)
            tt = d_on * g
            d_o = r * (tt - ohat * jnp.mean(tt * ohat, axis=-1, keepdims=True))
            dob = d_o.astype(BF16)

            attb = ck["att"].astype(BF16)
            q_inb, k_inb = ck["q_in"].astype(BF16), ck["k_in"].astype(BF16)
            q_stb, k_stb = ck["q_st"].astype(BF16), ck["k_st"].astype(BF16)
            d_att = jnp.where(_tri(False), _bdot(dob, vb, BNT_DIMS), 0.0).astype(BF16)
            d_q_in = _bdot(d_att, k_inb, BNN_DIMS)
            d_k_in = _bdot(d_att, q_inb, BTN_DIMS)
            d_q_st = _bdot(dob, s_in.astype(BF16), BNN_DIMS)
            qdo = _bdot(dob, q_stb, BTN_DIMS)
            d_states = [None] * HG_NC + [dst[h]]
            for c in reversed(range(HG_NC)):
                d_states[c] = d_states[c + 1] * ck["dc"][c] + qdo[c]
            dst[h] = d_states[0]
            ds_out = jnp.stack(d_states[1:], axis=0)
            dsb = ds_out.astype(BF16)
            d_v = _bdot(attb, dob, BTN_DIMS) + _bdot(k_stb, dsb, BNT_DIMS)
            d_k_st = _bdot(vb, dsb, BNN_DIMS)
            d_dc = jnp.sum(ds_out * s_in, axis=1, keepdims=True)

            p_qi = d_q_in * ck["q_in"]
            p_ki = d_k_in * ck["k_in"]
            p_qs = d_q_st * ck["q_st"]
            p_ks = d_k_st * ck["k_st"]
            d_qh = (d_q_in * ck["e_qi"] + d_q_st * ck["e_qs"]) * HG_SCALE
            d_k = d_k_in * ck["e_ki"] + d_k_st * ck["e_ks"]
            d_b = (p_qi - p_ki) + (p_qs - p_ks)
            d_b_mid = jnp.sum(p_ki - p_qi, axis=1, keepdims=True)
            d_b_last = jnp.sum(p_ks, axis=1, keepdims=True) + d_dc * ck["dc"]
            rowi = lax.broadcasted_iota(jnp.int32, (HG_NC, CHUNK, BD), 1)
            d_b = d_b + jnp.where(rowi == CHUNK // 2, d_b_mid, 0.0) + jnp.where(rowi == CHUNK - 1, d_b_last, 0.0)
            d_logf = _tri_sums(True, d_b)
            d_f = d_logf / ck["f"] - d_k
            sig, sq = ck["sig"], ck["sq"]
            d_fp = d_f * (1.0 - lb) * (sig * (1.0 - sig))
            dlb[0:1, cols] += jnp.sum(jnp.sum(d_f * (1.0 - sig), axis=1), axis=0, keepdims=True)
            d_q = d_qh * (sq * (1.0 + q * (1.0 - sq)))
            dz_ref[0, :, cols] = d_q.astype(BF16).reshape(t, BD)
            dz_ref[1, :, cols] = d_fp.astype(BF16).reshape(t, BD)
            dz_ref[2, :, cols] = d_v.astype(BF16).reshape(t, BD)
            dz_ref[3, :, cols] = d_gb.astype(BF16).reshape(t, BD)
            return carry

        lax.fori_loop(0, NB, head, 0, unroll=2)

        @pl.when((b == nb - 1) & (s == ns - 1))
        def _():
            lb = _lower_bound(lg_ref[...])
            dl = dlb[0:1, :] * (lb * (1.0 - lb))
            glg_ref[0:1, :] = dl
            glg_ref[1:2, :] = -dl

    rb = lambda b, s: b * ns + (ns - 1 - s)
    seg = lambda j: pl.BlockSpec((1, t, D), lambda b, s: (j, rb(b, s), 0))
    tile = pl.BlockSpec((t, D), lambda b, s: (rb(b, s), 0))
    return pl.pallas_call(
        body, name="hgrn_bwd", grid=(nb, ns),
        in_specs=[seg(2), seg(3), seg(4), seg(5), tile,
                  pl.BlockSpec((NB, HG_NC, BD, BD), lambda b, s: (b, ns - 1 - s, 0, 0)),
                  tile, pl.BlockSpec((2, D), lambda b, s: (0, 0)), pl.BlockSpec((1, BD), lambda b, s: (0, 0))],
        out_specs=[pl.BlockSpec((4, t, D), lambda b, s: (0, rb(b, s), 0)),
                   pl.BlockSpec((2, D), lambda b, s: (0, 0)), pl.BlockSpec((1, BD), lambda b, s: (0, 0))],
        out_shape=[SDS((4, n, D), BF16), SDS((2, D), F32), SDS((1, BD), F32)],
        scratch_shapes=[pltpu.VMEM((NB, BD, BD), F32), pltpu.VMEM((8, D), F32)],
        compiler_params=_params(60),
    )(z, z, z, z, o_all, st_all, dyb, lb_logits, hg_g)


def _mid(ya, yb, z, b_merge, x2, tgt, fin_g, pa, pb, wo):
    n = x2.shape[0]
    tm = 256
    ni = n // tm

    def body(ya_ref, yb_ref, gma_ref, gmb_ref, bm_ref, x_ref, t_ref, fg_ref, pa_hbm, pb_hbm, wo_hbm,
             dx2_ref, dya_ref, dyb_ref, dgm_ref, loss_ref, gfg_ref, gbm_ref, gm_hbm,
             pa_v, pb_v, wo_v, gpa_v, gpb_v, gwo_v, sem):
        i = pl.program_id(0)
        by_owner = lambda g: g.reshape(NB, BD, D)
        loads = [pltpu.make_async_copy(src, dst, sem.at[k])
                 for k, (src, dst) in enumerate(((pa_hbm, pa_v), (pb_hbm, pb_v), (wo_hbm, wo_v)))]
        stores = [pltpu.make_async_copy(src, dst, sem.at[k])
                  for k, (src, dst) in enumerate((g, gm_hbm.at[:, pl.ds(slot * BD, BD), :])
                                                 for slot, g in enumerate((gpa_v, gpb_v, gwo_v)))]

        @pl.when(i == 0)
        def _():
            for cp in loads:
                cp.start()
            for ref in (gpa_v, gpb_v, gwo_v, loss_ref, gfg_ref, gbm_ref):
                ref[...] = jnp.zeros(ref.shape, F32)
            for cp in loads:
                cp.wait()

        ya_v = ya_ref[...]
        yb_v = yb_ref[...]
        out_a = jnp.dot(ya_v, pa_v[...], preferred_element_type=F32)
        out_b = jnp.dot(yb_v, pb_v[...], preferred_element_type=F32)
        bm = bm_ref[...]
        g_a = _sigmoid(gma_ref[0] + bm[:, 0:D])
        g_b = _sigmoid(gmb_ref[0] + bm[:, D:2 * D])
        mixed = g_a * out_a + g_b * out_b
        mixb = mixed.astype(BF16)
        xo = x_ref[...] + jnp.dot(mixb, wo_v[...], preferred_element_type=F32)
        r = lax.rsqrt(jnp.mean(xo * xo, axis=-1, keepdims=True) + EPS)
        xn = xo * r
        fg = fg_ref[...]
        e = xn * fg - t_ref[...]
        loss_ref[...] += 0.5 * jnp.sum(jnp.mean(e * e, axis=-1, keepdims=True))
        dy = e * (1.0 / D)
        gfg_ref[...] += jnp.sum(dy * xn, axis=0, keepdims=True)
        dxn = dy * fg
        dx2 = r * (dxn - xn * jnp.mean(dxn * xn, axis=-1, keepdims=True))
        dx2_ref[...] = dx2
        dx2b = dx2.astype(BF16)
        d_mixed = lax.dot_general(dx2b, wo_v[...], NT_DIMS, preferred_element_type=F32)
        gwo_v[...] += by_owner(lax.dot_general(mixb, dx2b, TN_DIMS, preferred_element_type=F32))
        d_oa = (d_mixed * g_a).astype(BF16)
        d_ob = (d_mixed * g_b).astype(BF16)
        dgm_a = (d_mixed * out_a) * (g_a * (1.0 - g_a))
        dgm_b = (d_mixed * out_b) * (g_b * (1.0 - g_b))
        gbm_ref[:, 0:D] += jnp.sum(dgm_a, axis=0, keepdims=True)
        gbm_ref[:, D:2 * D] += jnp.sum(dgm_b, axis=0, keepdims=True)
        dgm_ref[0] = dgm_a.astype(BF16)
        dgm_ref[1] = dgm_b.astype(BF16)
        dya_ref[...] = lax.dot_general(d_oa, pa_v[...], NT_DIMS, preferred_element_type=F32)
        dyb_ref[...] = lax.dot_general(d_ob, pb_v[...], NT_DIMS, preferred_element_type=F32)
        gpa_v[...] += by_owner(lax.dot_general(ya_v, d_oa, TN_DIMS, preferred_element_type=F32))
        gpb_v[...] += by_owner(lax.dot_general(yb_v, d_ob, TN_DIMS, preferred_element_type=F32))

        @pl.when(i == ni - 1)
        def _():
            for cp in stores:
                cp.start()
            for cp in stores:
                cp.wait()

    rows = pl.BlockSpec((tm, D), lambda i: (i, 0))
    rep = lambda shape: pl.BlockSpec(shape, lambda i: (0,) * len(shape))
    return pl.pallas_call(
        body, name="mid", grid=(ni,),
        in_specs=[rows, rows,
                  pl.BlockSpec((1, tm, D), lambda i: (6, i, 0)), pl.BlockSpec((1, tm, D), lambda i: (7, i, 0)),
                  rep((1, 2 * D)), rows, rows, rep((1, D)), ANY, ANY, ANY],
        out_specs=[rows, rows, rows, pl.BlockSpec((2, tm, D), lambda i: (0, i, 0)),
                   rep((8, BD)), rep((1, D)), rep((1, 2 * D)), ANY],
        out_shape=[SDS((n, D), F32), SDS((n, D), F32), SDS((n, D), F32), SDS((2, n, D), BF16),
                   SDS((8, BD), F32), SDS((1, D), F32), SDS((1, 2 * D), F32),
                   SDS((NB, MID_ROWS, D), F32)],
        scratch_shapes=[pltpu.VMEM((D, D), BF16)] * 3 + [pltpu.VMEM((NB, BD, D), F32)] * 3 + [pltpu.SemaphoreType.DMA((3,))],
        compiler_params=_params(60),
    )(ya, yb, z, z, b_merge, x2, tgt, fin_g, pa, pb, wo)


def _dz_specs(tm, ni, row_major):
    if row_major:
        ia = lambda i, j: (jnp.minimum(j, 1), i, 0)
        ib = lambda i, j: (jnp.clip(j - 2, 0, 3), i, 0)
        im = lambda i, j: (jnp.clip(j - 6, 0, 1), i, 0)
    else:
        last = ni - 1
        ia = lambda j, i: (jnp.minimum(j, 1), jnp.where(j < 2, i, last), 0)
        ib = lambda j, i: (jnp.clip(j - 2, 0, 3), jnp.where(j < 2, 0, jnp.where(j < 6, i, last)), 0)
        im = lambda j, i: (jnp.clip(j - 6, 0, 1), jnp.where(j < 6, 0, i), 0)
    return [pl.BlockSpec((1, tm, D), f) for f in (ia, ib, im)]


def _inproj_bwd_x(dza, dzb, dzm, w_all, x2, dx2, norm_g, after):
    n = x2.shape[0]
    tm = 512
    ni = n // tm

    def body(dza_ref, dzb_ref, dzm_ref, w_ref, x_ref, dx2_ref, g_ref, after_ref, gx_ref, gg_ref, acc):
        i, j = pl.program_id(0), pl.program_id(1)

        @pl.when((i == 0) & (j == 0))
        def _():
            gg_ref[...] = jnp.zeros((1, D), F32)

        @pl.when(j == 0)
        def _():
            acc[...] = jnp.zeros((tm, D), F32)

        def add(ref):
            acc[...] += lax.dot_general(ref[0], w_ref[0], NT_DIMS, preferred_element_type=F32)

        pl.when(j < 2)(lambda: add(dza_ref))
        pl.when((j >= 2) & (j < 6))(lambda: add(dzb_ref))
        pl.when(j >= 6)(lambda: add(dzm_ref))

        @pl.when(j == NB - 1)
        def _():
            x = x_ref[...]
            r = lax.rsqrt(jnp.mean(x * x, axis=-1, keepdims=True) + EPS)
            xn = x * r
            dh = acc[...]
            gg_ref[...] += jnp.sum(dh * xn, axis=0, keepdims=True)
            dxn = dh * g_ref[...]
            gx_ref[...] = dx2_ref[...] + r * (dxn - xn * jnp.mean(dxn * xn, axis=-1, keepdims=True))

    rows = pl.BlockSpec((tm, D), lambda i, j: (i, 0))
    return pl.pallas_call(
        body, name="inproj_bwd_x", grid=(ni, NB),
        in_specs=_dz_specs(tm, ni, True) + [pl.BlockSpec((1, D, D), lambda i, j: (j, 0, 0)), rows, rows,
                                             pl.BlockSpec((1, D), lambda i, j: (0, 0)), ANY],
        out_specs=[rows, pl.BlockSpec((1, D), lambda i, j: (0, 0))],
        out_shape=[SDS((n, D), F32), SDS((1, D), F32)],
        scratch_shapes=[pltpu.VMEM((tm, D), F32)],
        compiler_params=_params(48),
    )(dza, dzb, dzm, w_all, x2, dx2, norm_g, after)


def _inproj_bwd_w(dza, dzb, dzm, h_all, g_m):
    n = h_all.shape[0]
    tm = min(n, 2048)
    ni = n // tm

    def body(dza_ref, dzb_ref, dzm_ref, h_ref, gm_hbm, gw_ref, got_w, got_m, stage, send_sems, recv_sems):
        j, i = pl.program_id(0), pl.program_id(1)
        x, y, c = _place()
        sibling = (x, y, 1 - c)

        def send_w(q):
            return pltpu.make_async_remote_copy(
                src_ref=stage.at[q % 2], dst_ref=got_w.at[q], send_sem=send_sems.at[q], recv_sem=recv_sems.at[q],
                device_id=sibling, device_id_type=MESH)

        def send_m(q):
            return pltpu.make_async_remote_copy(
                src_ref=gm_hbm.at[2 * q + (1 - c)], dst_ref=got_m.at[q], send_sem=send_sems.at[4 + q],
                recv_sem=recv_sems.at[4 + q], device_id=sibling, device_id_type=MESH)

        @pl.when((j == 0) & (i == 0))
        def _():
            for q in range(4):
                send_m(q).start()

        @pl.when(i == 0)
        def _():
            gw_ref[...] = jnp.zeros((1, D, D), F32)

        def add(ref):
            gw_ref[0] += lax.dot_general(h_ref[...], ref[0], TN_DIMS, preferred_element_type=F32)

        pl.when(j < 2)(lambda: add(dza_ref))
        pl.when((j >= 2) & (j < 6))(lambda: add(dzb_ref))
        pl.when(j >= 6)(lambda: add(dzm_ref))

        for q in range(4):
            @pl.when((i == ni - 1) & (j == 2 * q + 1 - c))
            def _(q=q):
                if q >= 2:
                    send_w(q - 2).wait_send()
                stage[q % 2] = gw_ref[0].astype(BF16)
                send_w(q).start()

        @pl.when((j == NB - 1) & (i == ni - 1))
        def _():
            for q in (2, 3):
                send_w(q).wait_send()
            for q in range(4):
                send_w(q).wait_recv()
                send_m(q).wait_send()
                send_m(q).wait_recv()

    return pl.pallas_call(
        body, name="inproj_bwd_w", grid=(NB, ni),
        in_specs=_dz_specs(tm, ni, False) + [pl.BlockSpec((tm, D), lambda j, i: (i, 0)), ANY],
        out_specs=[pl.BlockSpec((1, D, D), lambda j, i: (j, 0, 0)), ANY, ANY],
        out_shape=[SDS((NB, D, D), F32), SDS((4, D, D), BF16), SDS((4,) + g_m.shape[1:], F32)],
        scratch_shapes=[pltpu.VMEM((2, D, D), BF16), pltpu.SemaphoreType.DMA((8,)), pltpu.SemaphoreType.DMA((8,))],
        compiler_params=_params(58),
    )(dza, dzb, dzm, h_all, g_m)


def _adam_refs(w_ref, g_ref, m_ref, v_ref, d_ref, nm_ref, nv_ref):
    gv = g_ref[...]
    nm = ADAM_B1 * m_ref[...] + (1.0 - ADAM_B1) * gv
    nv = ADAM_B2 * v_ref[...] + (1.0 - ADAM_B2) * (gv * gv)
    m_hat = nm / (1.0 - ADAM_B1 ** ADAM_STEP)
    v_hat = nv / (1.0 - ADAM_B2 ** ADAM_STEP)
    d_ref[...] = -ADAM_LR * (m_hat / (jnp.sqrt(v_hat) + ADAM_EPS) + ADAM_WD * w_ref[...])
    nm_ref[...] = nm
    nv_ref[...] = nv


def _adamw_small(ws, gs, ms, vs):
    k = len(ws)

    def body(*refs):
        ins, outs = refs[:4 * k], refs[4 * k:7 * k]
        vin, vout = refs[7 * k:11 * k], refs[11 * k:14 * k]
        load_sems, store_sems = refs[14 * k:]
        loads = [pltpu.make_async_copy(ins[i], vin[i], load_sems.at[i]) for i in range(4 * k)]
        for cp in loads:
            cp.start()
        for cp in loads:
            cp.wait()
        for i in range(k):
            _adam_refs(*[vin[part * k + i] for part in range(4)], *[vout[part * k + i] for part in range(3)])
        stores = [pltpu.make_async_copy(vout[i], outs[i], store_sems.at[i]) for i in range(3 * k)]
        for cp in stores:
            cp.start()
        for cp in stores:
            cp.wait()

    shapes = [SDS(w.shape, F32) for w in ws]
    vmem = [pltpu.VMEM(w.shape, F32) for w in ws]
    out = pl.pallas_call(
        body, name="adamw_small", out_shape=shapes * 3, in_specs=[HBM] * (4 * k), out_specs=[HBM] * (3 * k),
        scratch_shapes=vmem * 7 + [pltpu.SemaphoreType.DMA((4 * k,)), pltpu.SemaphoreType.DMA((3 * k,))],
        compiler_params=_params(32),
    )(*ws, *gs, *ms, *vs)
    return out[:k], out[k:2 * k], out[2 * k:]


def _gather_tail_adamw(tail, big, proj):
    rows = big[0].shape[0]
    steps = 4
    tr = rows // steps
    assert len(proj) == steps - 1

    def body(*refs):
        tail_ref, big_in = refs[0], refs[1:5]
        proj_in = [refs[5 + 4 * p:9 + 4 * p] for p in range(steps - 1)]
        out_ref, big_out = refs[17], refs[18:21]
        proj_out = [refs[21 + 3 * p:24 + 3 * p] for p in range(steps - 1)]
        stage, send_sems, recv_sems, local_sem = refs[30:]
        i = pl.program_id(0)
        x, y, c = _place()
        me, sibling = (x, y, c), (x, y, 1 - c)
        chips = _other_chips(x, y)

        def copy(k, block, to, src=None):
            return pltpu.make_async_remote_copy(
                src_ref=out_ref.at[_block_id(block)] if src is None else src, dst_ref=out_ref.at[_block_id(block)],
                send_sem=send_sems.at[k], recv_sem=recv_sems.at[k], device_id=to, device_id_type=MESH)

        mine = pltpu.make_async_copy(stage, out_ref.at[_block_id(me)], local_sem)

        def landed(slot):
            copy(1 + slot, (*chips[slot], c), me).wait_recv()
            copy(4 + slot, (*chips[slot], c), sibling).start()

        @pl.when(i == 0)
        def _():
            stage[...] = tail_ref[...]
            mine.start()
            copy(0, me, sibling, src=stage).start()
            for slot, chip in enumerate(chips):
                copy(1 + slot, me, (*chip, c), src=stage).start()

        @pl.when(i == 1)
        def _():
            landed(0)
            landed(1)

        pl.when(i == 2)(lambda: landed(2))

        @pl.when(i == steps - 1)
        def _():
            copy(0, sibling, me).wait_recv()
            for slot, chip in enumerate(chips):
                copy(4 + slot, (*chip, 1 - c), me).wait_recv()
            copy(0, me, sibling, src=stage).wait_send()
            for slot, chip in enumerate(chips):
                copy(1 + slot, me, (*chip, c), src=stage).wait_send()
                copy(4 + slot, (*chip, c), sibling).wait_send()
            mine.wait()

        _adam_refs(*big_in, *big_out)
        for p in range(steps - 1):
            pl.when(i == p + 1)(functools.partial(_adam_refs, *proj_in[p], *proj_out[p]))

    tile = pl.BlockSpec((tr, big[0].shape[1]), lambda i: (i, 0))
    whole = lambda a: pl.BlockSpec(a.shape, lambda i: (0, 0))
    flat = [a for quad in proj for a in quad]
    out = pl.pallas_call(
        body, name="gather_tail_adamw", grid=(steps,),
        in_specs=[pl.BlockSpec(memory_space=pltpu.VMEM)] + [tile] * 4 + [whole(a) for a in flat],
        out_specs=[ANY] + [tile] * 3 + [whole(quad[0]) for quad in proj for _ in range(3)],
        out_shape=[SDS((NB,) + tail.shape, F32)] + [SDS(big[0].shape, F32)] * 3
        + [SDS(quad[0].shape, F32) for quad in proj for _ in range(3)],
        scratch_shapes=[pltpu.VMEM(tail.shape, F32), pltpu.SemaphoreType.DMA((7,)), pltpu.SemaphoreType.DMA((7,)),
                        pltpu.SemaphoreType.DMA(())],
        compiler_params=_params(40),
    )(tail, *big, *flat)
    return out[0], out[1:4], [out[4 + 3 * p:7 + 3 * p] for p in range(steps - 1)]


HBM = pl.BlockSpec(memory_space=pltpu.HBM)
SEMS = pl.BlockSpec(memory_space=pltpu.SEMAPHORE)
EFFECT = pltpu.SideEffectType.DATAFLOW_SIDE_EFFECTING


def _chip_copies(srcs, lands, send_sems, recv_sems):
    x, y, c = _place()
    return [pltpu.make_async_remote_copy(
        src_ref=srcs[a].at[slot], dst_ref=lands[a].at[slot],
        send_sem=send_sems.at[3 * a + slot], recv_sem=recv_sems.at[3 * a + slot],
        device_id=(px, py, c), device_id_type=MESH)
        for a in range(len(srcs)) for slot, (px, py) in enumerate(_other_chips(x, y))]


def _split_start(name, copies, per_array, srcs, lands, after=None):
    na = len(srcs)

    def body(*refs):
        send_sems, recv_sems = refs[-2 * na - 3], refs[-2 * na - 2]
        for cp in copies(refs[:na], refs[na:2 * na], send_sems, recv_sems):
            cp.start()
        refs[-1][...] = jnp.zeros_like(refs[-1])

    hbm = lambda a: pltpu.HBM(a.shape, a.dtype)
    pin = lambda a: pltpu.with_memory_space_constraint(a, pltpu.HBM)
    out = pl.pallas_call(
        body, name=name,
        out_shape=(pltpu.SemaphoreType.DMA((per_array * na,)), pltpu.SemaphoreType.DMA((per_array * na,)),
                   *[hbm(a) for a in srcs], *[hbm(a) for a in lands], SDS((8, BD), F32)),
        in_specs=[HBM] * (2 * na) + ([] if after is None else [ANY]),
        out_specs=(SEMS, SEMS, *[HBM] * (2 * na), pl.BlockSpec(memory_space=pltpu.VMEM)),
        input_output_aliases={i: 2 + i for i in range(2 * na)},
        compiler_params=pltpu.CompilerParams(has_side_effects=EFFECT),
    )(*[pin(a) for a in srcs], *[pin(a) for a in lands], *([] if after is None else [after]))
    return out[0], out[1], out[2:2 + na], out[2 + na:2 + 2 * na], out[-1]


def _split_wait(name, copies, started, after):
    send_sems, recv_sems, srcs, lands, _ = started
    na = len(srcs)

    def body(*refs):
        waits = copies(refs[:na], refs[na:2 * na], refs[2 * na], refs[2 * na + 1])
        for cp in waits:
            cp.wait_send()
        for cp in waits:
            cp.wait_recv()

    hbm = lambda a: pltpu.HBM(a.shape, a.dtype)
    out = pl.pallas_call(
        body, name=name,
        out_shape=(*[hbm(a) for a in srcs], *[hbm(a) for a in lands]),
        in_specs=[HBM] * (2 * na) + [SEMS, SEMS, ANY],
        out_specs=tuple([HBM] * (2 * na)),
        input_output_aliases={i: i for i in range(2 * na)},
        compiler_params=pltpu.CompilerParams(has_side_effects=EFFECT),
    )(*srcs, *lands, send_sems, recv_sems, after)
    return out[na:]


def _add_sibling(place, g, a_in):
    _, r, cols = g.shape
    tr = _row_tile(r)

    def chip(k, pr):
        qx = pr[0] if k in (1, 3) else 1 - pr[0]
        qy = pr[1] if k in (0, 3) else 1 - pr[1]
        return 2 * qx + qy

    def body(place_ref, *refs):
        g_refs, a_refs, (out_ref, own_ref) = refs[0:4], refs[4:8], refs[8:10]
        for k in range(3):
            out_ref[k] = (g_refs[k][0] + a_refs[k][0].astype(F32)).astype(BF16)
        own_ref[...] = g_refs[3][0] + a_refs[3][0].astype(F32)

    mine = lambda k: pl.BlockSpec((1, tr, cols), lambda i, pr: (2 * chip(k, pr) + pr[2], i, 0))
    theirs = lambda k: pl.BlockSpec((1, tr, cols), lambda i, pr: (chip(k, pr), i, 0))
    return pl.pallas_call(
        body, name="add_sibling",
        grid_spec=pltpu.PrefetchScalarGridSpec(
            num_scalar_prefetch=1, grid=(r // tr,),
            in_specs=[mine(k) for k in range(4)] + [theirs(k) for k in range(4)],
            out_specs=[pl.BlockSpec((3, tr, cols), lambda i, pr: (0, i, 0)),
                       pl.BlockSpec((tr, cols), lambda i, pr: (i, 0))]),
        out_shape=[SDS((3, r, cols), BF16), SDS((r, cols), F32)], compiler_params=_params(48),
    )(place, *[g] * 4, *[a_in] * 4)


def _add_chips(own, b_in):
    r, cols = own.shape
    tr = _row_tile(r)

    def body(p_ref, b0_ref, b1_ref, b2_ref, o_ref):
        o_ref[...] = ((p_ref[...] + b0_ref[0].astype(F32)) + b1_ref[0].astype(F32)) + b2_ref[0].astype(F32)

    slot = lambda k: pl.BlockSpec((1, tr, cols), lambda i: (k, i, 0))
    spec = pl.BlockSpec((tr, cols), lambda i: (i, 0))
    return pl.pallas_call(
        body, name="add_chips", grid=(r // tr,), in_specs=[spec, slot(0), slot(1), slot(2)], out_specs=spec,
        out_shape=SDS((r, cols), F32), compiler_params=_params(32),
    )(own, b_in, b_in, b_in)


VEC_NAMES = ("b_merge", "conv_b", "rg_bx", "rg_ba", "rg_lambda", "hg_lb_logits", "hg_norm_g", "final_norm_g")
REP_NAMES = ("rg_wx", "rg_wa", "norm_g") + VEC_NAMES
SMALL_AT = 3 * BD
SMALL_ROWS = 48
MID_ROWS = 448


def _sum_blocks(parts):
    def body(p_ref, o_ref):
        acc = p_ref[0]
        for k in range(1, NB):
            acc = acc + p_ref[k]
        o_ref[...] = acc

    return pl.pallas_call(body, name="sum_blocks", out_shape=SDS(parts.shape[1:], F32))(parts)


def _pack_rows(arrays, width, row_multiple=8):
    flat = jnp.concatenate([a.reshape(-1) for a in arrays])
    rows = -(-flat.shape[0] // width)
    rows = -(-rows // row_multiple) * row_multiple
    return jnp.pad(flat, (0, rows * width - flat.shape[0])).reshape(rows, width)


def _unpack(flat, like):
    out, off = [], 0
    for a in like:
        out.append(flat[off:off + a.size].reshape(a.shape))
        off += a.size
    return out


def kernel(x, w_in, b_merge, conv_w, conv_b, rg_wx, rg_bx, rg_wa, rg_ba, rg_lambda, hg_lb_logits, hg_norm_g, proj_a, proj_b, w_out, norm_g, final_norm_g, loss_target, m_w_in, m_b_merge, m_conv_w, m_conv_b, m_rg_wx, m_rg_bx, m_rg_wa, m_rg_ba, m_rg_lambda, m_hg_lb_logits, m_hg_norm_g, m_proj_a, m_proj_b, m_w_out, m_norm_g, m_final_norm_g, v_w_in, v_b_merge, v_conv_w, v_conv_b, v_rg_wx, v_rg_bx, v_rg_wa, v_rg_ba, v_rg_lambda, v_hg_lb_logits, v_hg_norm_g, v_proj_a, v_proj_b, v_w_out, v_norm_g, v_final_norm_g):
    weights = dict(w_in=w_in, b_merge=b_merge, conv_w=conv_w, conv_b=conv_b, rg_wx=rg_wx, rg_bx=rg_bx, rg_wa=rg_wa,
                   rg_ba=rg_ba, rg_lambda=rg_lambda, hg_lb_logits=hg_lb_logits, hg_norm_g=hg_norm_g, proj_a=proj_a,
                   proj_b=proj_b, w_out=w_out, norm_g=norm_g, final_norm_g=final_norm_g)
    mom1 = dict(w_in=m_w_in, b_merge=m_b_merge, conv_w=m_conv_w, conv_b=m_conv_b, rg_wx=m_rg_wx, rg_bx=m_rg_bx,
                rg_wa=m_rg_wa, rg_ba=m_rg_ba, rg_lambda=m_rg_lambda, hg_lb_logits=m_hg_lb_logits,
                hg_norm_g=m_hg_norm_g, proj_a=m_proj_a, proj_b=m_proj_b, w_out=m_w_out, norm_g=m_norm_g,
                final_norm_g=m_final_norm_g)
    mom2 = dict(w_in=v_w_in, b_merge=v_b_merge, conv_w=v_conv_w, conv_b=v_conv_b, rg_wx=v_rg_wx, rg_bx=v_rg_bx,
                rg_wa=v_rg_wa, rg_ba=v_rg_ba, rg_lambda=v_rg_lambda, hg_lb_logits=v_hg_lb_logits,
                hg_norm_g=v_hg_norm_g, proj_a=v_proj_a, proj_b=v_proj_b, w_out=v_w_out, norm_g=v_norm_g,
                final_norm_g=v_final_norm_g)
    order = list(weights)
    nb, s_len, _ = x.shape
    n = nb * s_len
    px, py, pc = _place()
    place = jnp.stack([px, py, pc]).astype(jnp.int32)

    in_hbm = lambda a: pltpu.with_memory_space_constraint(a, pltpu.HBM)
    norm_gain = in_hbm(norm_g)

    x2 = x.reshape(n, D)
    cw_blk = jnp.pad(conv_w[0], ((0, 4), (0, 0)))
    order_ids = jnp.stack([_block_id(p) for p in _arrival_order(px, py, pc)]).astype(jnp.int32)
    z, h_all, w_all, pa_all, pb_all, wo_all, cw_all = _gather_inproj(
        order_ids, x2, norm_gain, [w_in[0], proj_a[0], proj_b[0], w_out[0], cw_blk], [BF16, BF16, BF16, BF16, F32])
    pa_full, pb_full, wo_full = (a.reshape(D, D) for a in (pa_all, pb_all, wo_all))
    cw8 = in_hbm(cw_all.transpose(1, 0, 2).reshape(8, D))
    wx_b, wa_b = in_hbm(rg_wx[0].astype(BF16)), in_hbm(rg_wa[0].astype(BF16))
    cb, bx, ba, lam = (in_hbm(a.reshape(1, D)) for a in (conv_b, rg_bx, rg_ba, rg_lambda))
    fin_g, b_mrg = in_hbm(final_norm_g.reshape(1, D)), in_hbm(b_merge)
    lb_lg, hg_g = in_hbm(hg_lb_logits), in_hbm(hg_norm_g)

    hlru, ya = _lru_fwd(z, cw8, cb, wx_b, wa_b, bx, ba, lam, nb, s_len)
    o_all, yb, st_all = _hgrn_fwd(z, lb_lg, hg_g, nb, s_len)

    (dx2, dya, dyb, dzm, loss_acc, g_fin, g_bm, g_mid) = _mid(
        ya, yb, z, b_mrg, x2, loss_target.reshape(n, D), fin_g, pa_full, pb_full, wo_full)
    dzb, g_lg, g_hg = _hgrn_bwd(z, o_all, st_all, dyb, lb_lg, hg_g, nb, s_len)
    dza, g_cw8, g_cb, g_wx, g_wa, g_bx, g_ba, g_lam = _lru_bwd(
        z, hlru, dya, cw8, cb, wx_b, wa_b, bx, ba, lam, nb, s_len)

    part = dict(b_merge=g_bm, conv_b=g_cb, rg_bx=g_bx, rg_ba=g_ba, rg_lambda=g_lam, hg_lb_logits=g_lg,
                hg_norm_g=g_hg, final_norm_g=g_fin)
    vec = _pack_rows([part[k] for k in VEC_NAMES], BD)
    vec = jnp.pad(vec, ((0, 16 * NB - vec.shape[0]), (0, 0))).reshape(NB, 2, D)
    rows8 = lambda a: jnp.pad(a, ((0, 0), (0, 8 - a.shape[1]), (0, 0)))
    small = jnp.concatenate([g_wx.reshape(NB, 16, D), g_wa.reshape(NB, 16, D),
                             rows8(g_cw8.reshape(8, NB, BD).transpose(1, 0, 2).reshape(NB, 1, D)), rows8(vec),
                             jnp.zeros((NB, MID_ROWS - SMALL_AT - SMALL_ROWS, D), F32)], axis=1)
    g_m = lax.dynamic_update_slice(g_mid, small, (0, SMALL_AT, 0))
    g_w, w_from_sibling, m_from_sibling = _inproj_bwd_w(dza, dzb, dzm, h_all, g_m)
    w_out_bf, w_own = _add_sibling(place, g_w, w_from_sibling)
    m_out_bf, m_own = _add_sibling(place, g_m, m_from_sibling)
    outgoing = [w_out_bf, m_out_bf]
    chip_sums = _split_start("rs_chips_start", _chip_copies, 3, outgoing, [lax.empty(a.shape, a.dtype) for a in outgoing])
    grad_x, g_ng = _inproj_bwd_x(dza, dzb, dzm, w_all, x2, dx2, norm_gain, chip_sums[-1])
    from_chips = _split_wait("rs_chips_wait", _chip_copies, chip_sums, grad_x)
    r_w = _add_chips(w_own, from_chips[0])
    r_m = _add_chips(m_own, from_chips[1])
    row = lax.broadcasted_iota(jnp.int32, (8, D), 0)
    mine = jnp.where(row == 0, g_ng, jnp.where(row == 1, loss_acc[0:1, 0:1], 0.0))
    tail = jnp.concatenate([r_m[SMALL_AT:SMALL_AT + SMALL_ROWS], mine], axis=0)
    sharded = ("w_in", "proj_a", "proj_b", "w_out")
    flat2 = lambda a: a.reshape(-1, a.shape[-1])
    own_grads = dict(w_in=r_w, proj_a=r_m[0:BD], proj_b=r_m[BD:2 * BD], w_out=r_m[2 * BD:3 * BD])
    quads = [[flat2(weights[k]), own_grads[k], flat2(mom1[k]), flat2(mom2[k])] for k in sharded]
    tail_all, big_out, proj_out = _gather_tail_adamw(tail, quads[0], quads[1:])
    summed = _sum_blocks(tail_all[:, SMALL_ROWS:SMALL_ROWS + 8])

    grads = {k: own_grads[k].reshape(weights[k].shape) for k in sharded}
    grads.update(conv_w=r_m[SMALL_AT + 32].reshape(8, BD)[0:4].reshape(1, 4, BD),
                 rg_wx=tail_all[:, 0:16].reshape(1, NB, BD, BD), rg_wa=tail_all[:, 16:32].reshape(1, NB, BD, BD),
                 norm_g=summed[0:1])
    vec_all = tail_all[:, 40:42].reshape(-1)
    for k, gk in zip(VEC_NAMES, _unpack(vec_all, [weights[k] for k in VEC_NAMES])):
        grads[k] = gk

    delta, new_m, new_v = {}, {}, {}
    for k, outs in zip(sharded, [big_out] + proj_out):
        delta[k], new_m[k], new_v[k] = (a.reshape(weights[k].shape) for a in outs)
    rep = list(REP_NAMES) + ["conv_w"]
    outs = _adamw_small(*[[flat2(t[k]) for k in rep] for t in (weights, grads, mom1, mom2)])
    for tgt, arrays in zip((delta, new_m, new_v), outs):
        for k, a in zip(rep, arrays):
            tgt[k] = a.reshape(weights[k].shape)

    return (summed[1, 0], grad_x.reshape(x.shape), *[grads[k] for k in order], *[delta[k] for k in order],
            *[new_m[k] for k in order], *[new_v[k] for k in order])
```

```python
import functools

import jax
import jax.numpy as jnp
from jax import lax
from jax.experimental import pallas as pl
from jax.experimental.pallas import tpu as pltpu

F32 = jnp.float32
BF16 = jnp.bfloat16
SDS = jax.ShapeDtypeStruct
MESH = pl.DeviceIdType.MESH
ANY = pl.BlockSpec(memory_space=pl.ANY)

D = 1024
NB = 8
BD = D // NB
CHUNK = 64
EPS = 1e-6
LRU_C = 8.0
HG_SCALE = BD ** -0.5
ADAM_LR, ADAM_B1, ADAM_B2, ADAM_EPS, ADAM_WD, ADAM_STEP = 0.001, 0.9, 0.999, 1e-08, 0.01, 10

NT_DIMS = (((1,), (1,)), ((), ()))
TN_DIMS = (((0,), (0,)), ((), ()))


def _params(vmem_mib):
    return pltpu.CompilerParams(vmem_limit_bytes=vmem_mib << 20)


def _row_tile(rows, most=256):
    assert rows % 8 == 0
    return max(t for t in range(8, min(rows, most) + 1, 8) if rows % t == 0)


def _sigmoid(v):
    return 0.5 * (jnp.tanh(0.5 * v) + 1.0)


def _groups(v):
    return v.reshape(v.shape[0] // 8, 8, v.shape[1])


def _softplus_neg(lam):
    t = -lam
    e = jnp.exp(-jnp.abs(t))
    w = 1.0 + e
    d = w - 1.0
    l1p = jnp.where(d == 0.0, e, jnp.log(w) * (e / jnp.where(d == 0.0, 1.0, d)))
    return jnp.maximum(t, 0.0) + l1p


def _place():
    return lax.axis_index("x"), lax.axis_index("y"), lax.axis_index("c")


def _other_chips(x, y):
    return [(1 - x, y), (x, 1 - y), (1 - x, 1 - y)]


def _block_id(p):
    return 4 * p[0] + 2 * p[1] + p[2]


def _core_chips(x, y, c):
    near, far, diag = _other_chips(x, y)
    pick = lambda a, b: (jnp.where(c == 0, a[0], b[0]), jnp.where(c == 0, a[1], b[1]))
    return [pick(near, far), pick(far, near), diag]


def _arrival_order(x, y, c):
    first, second, diag = _core_chips(x, y, c)
    return [(x, y, c), (x, y, 1 - c), (*first, c), (*second, 1 - c), (*second, c), (*first, 1 - c),
            (*diag, c), (*diag, 1 - c)]


def _gather_inproj(order_ids, x2, norm_g, blocks, dtypes):
    na = len(blocks)
    n = x2.shape[0]
    tm = min(n, 1024)
    ni = n // tm

    def body(order_ref, x_ref, g_ref, *refs):
        ins, (z_ref, h_ref), outs = refs[:na], refs[na:na + 2], refs[na + 2:2 * na + 2]
        stages = refs[2 * na + 2:3 * na + 2]
        h_full, wbuf, send_sems, recv_sems, local_sems, wsems, hsem = refs[3 * na + 2:]
        j, i = pl.program_id(0), pl.program_id(1)
        x, y, c = _place()
        me, sibling = (x, y, c), (x, y, 1 - c)
        chips = _core_chips(x, y, c)
        sibling_chips = [chips[1], chips[0], chips[2]]
        small = range(1, na)

        def copy(a, k, block, to, src=None):
            return pltpu.make_async_remote_copy(
                src_ref=outs[a].at[_block_id(block)] if src is None else src, dst_ref=outs[a].at[_block_id(block)],
                send_sem=send_sems.at[7 * a + k], recv_sem=recv_sems.at[7 * a + k],
                device_id=to, device_id_type=MESH)

        def local(a):
            return pltpu.make_async_copy(stages[a], outs[a].at[_block_id(me)], local_sems.at[a])

        def landed(a, slot):
            copy(a, 1 + slot, (*chips[slot], c), me).wait_recv()
            copy(a, 4 + slot, (*chips[slot], c), sibling).start()
            if slot == 0:
                copy(a, 3, (*chips[0], c), (*chips[1], c)).start()

        def diagonal_and_small():
            landed(0, 2)
            for a in small:
                landed(a, 0)
                landed(a, 1)

        def passed_on(a, slot):
            copy(a, 4 + slot, (*sibling_chips[slot], 1 - c), me).wait_recv()

        def sibling_here_send_second():
            copy(0, 0, sibling, me).wait_recv()
            for a in range(na):
                copy(a, 2, me, (*chips[1], c), src=stages[a]).start()

        @pl.when((j == 0) & (i == 0))
        def _():
            for a in range(na):
                stages[a][...] = ins[a][...].astype(dtypes[a])
                local(a).start()
            for a in range(na):
                copy(a, 0, me, sibling, src=stages[a]).start()
                copy(a, 1, me, (*chips[0], c), src=stages[a]).start()

        @pl.when(j == 0)
        def _():
            xv = x_ref[...]
            r = lax.rsqrt(jnp.mean(xv * xv, axis=-1, keepdims=True) + EPS)
            hb = ((xv * r) * g_ref[...]).astype(BF16)
            h_full[pl.ds(pl.multiple_of(i * tm, tm), tm), :] = hb

        save_h = pltpu.make_async_copy(h_full, h_ref, hsem)
        pl.when((j == 0) & (i == ni - 1))(save_h.start)

        steps = [
            lambda: local(0).wait(),
            sibling_here_send_second,
            lambda: landed(0, 0),
            lambda: passed_on(0, 0),
            lambda: landed(0, 1),
            lambda: passed_on(0, 1),
            diagonal_and_small,
            lambda: passed_on(0, 2),
        ]
        def w_load(k):
            return pltpu.make_async_copy(outs[0].at[order_ref[k]], wbuf.at[k % 2], wsems.at[k % 2])

        for k, step in enumerate(steps):
            @pl.when((j == 0) & (i == 0) if k == 0 else (j == k - 1) & (i == ni - 1))
            def _(k=k, step=step):
                step()
                w_load(k).start()

        pl.when(i == 0)(lambda: w_load(j).wait())
        z_ref[0] = jnp.dot(h_full[pl.ds(pl.multiple_of(i * tm, tm), tm), :], wbuf[j % 2], preferred_element_type=F32)

        @pl.when((j == NB - 1) & (i == ni - 1))
        def _():
            save_h.wait()
            for a in small:
                landed(a, 2)
            for a in small:
                local(a).wait()
                copy(a, 0, sibling, me).wait_recv()
                for slot in range(3):
                    passed_on(a, slot)
            for a in range(na):
                copy(a, 0, me, sibling, src=stages[a]).wait_send()
                for slot, chip in enumerate(chips):
                    copy(a, 1 + slot, me, (*chip, c), src=stages[a]).wait_send()
                    copy(a, 4 + slot, (*chip, c), sibling).wait_send()

    rows_once = lambda j, i, order: (jnp.where(j == 0, i, ni - 1), 0)
    vmem = pl.BlockSpec(memory_space=pltpu.VMEM)
    return pl.pallas_call(
        body, name="gather_inproj",
        grid_spec=pltpu.PrefetchScalarGridSpec(
            num_scalar_prefetch=1, grid=(NB, ni),
            in_specs=[pl.BlockSpec((tm, D), rows_once), pl.BlockSpec((1, D), lambda j, i, order: (0, 0))] + [vmem] * na,
            out_specs=[pl.BlockSpec((1, tm, D), lambda j, i, order: (order[j], i, 0)), ANY] + [ANY] * na,
            scratch_shapes=[pltpu.VMEM(b.shape, dt) for b, dt in zip(blocks, dtypes)]
            + [pltpu.VMEM((n, D), BF16), pltpu.VMEM((2, D, D), BF16),
               pltpu.SemaphoreType.DMA((7 * na,)), pltpu.SemaphoreType.DMA((7 * na,)),
               pltpu.SemaphoreType.DMA((na,)), pltpu.SemaphoreType.DMA((2,)), pltpu.SemaphoreType.DMA(())]),
        out_shape=[SDS((NB, n, D), F32), SDS((n, D), BF16)] + [SDS((NB,) + b.shape, dt) for b, dt in zip(blocks, dtypes)],
        compiler_params=_params(56),
    )(order_ids, x2, norm_g, *blocks)


LRU_T = 256


def _shifted(groups, shifts):
    row = lax.broadcasted_iota(jnp.int32, (groups.shape[0] - 1,) + groups.shape[1:], 1)
    out = []
    for s in shifts:
        y = pltpu.roll(groups, s % 8, 1)
        moved = jnp.where(row >= s, y[1:], y[:-1]) if s > 0 else jnp.where(row < 8 + s, y[:-1], y[1:])
        out.append(moved.reshape(-1, groups.shape[2]))
    return out


def _conv(taps, cw, cb):
    acc = taps[0] * cw[0:1, :] + taps[1] * cw[1:2, :]
    acc = acc + taps[2] * cw[2:3, :]
    acc = acc + taps[3] * cw[3:4, :]
    return cb + acc


def _lru_gates(xa, wx_ref, wa_ref, bx, ba, lam):
    xab = xa.astype(BF16)
    pis, prs = [], []
    for h in range(NB):
        xs = xab[:, h * BD:(h + 1) * BD]
        pis.append(jnp.dot(xs, wx_ref[h], preferred_element_type=F32))
        prs.append(jnp.dot(xs, wa_ref[h], preferred_element_type=F32))
    gi = _sigmoid(jnp.concatenate(pis, axis=1) + bx)
    gr = _sigmoid(jnp.concatenate(prs, axis=1) + ba)
    sp = _softplus_neg(lam)
    log_a = (-LRU_C * gr) * sp
    a = jnp.exp(log_a)
    mult = jnp.sqrt(-jnp.tanh(log_a) * (a * a + 1.0))
    return xab, gi, gr, sp, a, mult


def _lru_fwd(z, cw8, cb, wx, wa, bx, ba, lam, nb, s_len):
    n = nb * s_len
    t = LRU_T
    ns = s_len // t

    def body(xp_ref, ga_ref, cw_ref, cb_ref, wx_ref, wa_ref, bx_ref, ba_ref, lam_ref,
             h_ref, ya_ref, ext, a_s, u_s, carry):
        @pl.when(pl.program_id(1) == 0)
        def _():
            ext[0:8, :] = jnp.zeros((8, D), F32)
            carry[...] = jnp.zeros((8, D), F32)

        xp = xp_ref[0]
        ext[8:8 + t, :] = xp
        xa = _conv(_shifted(_groups(ext[...]), (3, 2, 1)) + [xp], cw_ref[...], cb_ref[...])
        ext[0:8, :] = xp[t - 8:t, :]
        _, gi, _, _, a, mult = _lru_gates(xa, wx_ref, wa_ref, bx_ref[...], ba_ref[...], lam_ref[...])
        u = (mult * gi) * xa
        a, u = _groups(a), _groups(u)
        row = lax.broadcasted_iota(jnp.int32, a.shape, 1)
        for sh in (1, 2, 4):
            a_sh = pltpu.roll(a, sh, 1)
            u_sh = pltpu.roll(u, sh, 1)
            m = row >= sh
            u = jnp.where(m, a * u_sh + u, u)
            a = jnp.where(m, a * a_sh, a)
        a_s[...] = a.reshape(t, D)
        u_s[...] = u.reshape(t, D)

        def step(g, c):
            r = pl.multiple_of(g * 8, 8)
            hg = u_s[pl.ds(r, 8), :] + a_s[pl.ds(r, 8), :] * c
            h_ref[pl.ds(r, 8), :] = hg
            return hg[7:8, :]

        c_out = lax.fori_loop(0, t // 8, step, carry[0:1, :], unroll=4)
        carry[0:1, :] = c_out
        ga = ga_ref[0]
        ya_ref[...] = (h_ref[...] * (ga * _sigmoid(ga))).astype(BF16)

    row_map = lambda b, s: (b * ns + s, 0)
    rep2 = lambda b, s: (0, 0)
    rep3 = lambda b, s: (0, 0, 0)
    return pl.pallas_call(
        body, name="lru_fwd", grid=(nb, ns),
        in_specs=[pl.BlockSpec((1, t, D), lambda b, s: (0, b * ns + s, 0)),
                  pl.BlockSpec((1, t, D), lambda b, s: (1, b * ns + s, 0)),
                  pl.BlockSpec((8, D), rep2), pl.BlockSpec((1, D), rep2),
                  pl.BlockSpec((NB, BD, BD), rep3), pl.BlockSpec((NB, BD, BD), rep3),
                  pl.BlockSpec((1, D), rep2), pl.BlockSpec((1, D), rep2), pl.BlockSpec((1, D), rep2)],
        out_specs=[pl.BlockSpec((t, D), row_map), pl.BlockSpec((t, D), row_map)],
        out_shape=[SDS((n, D), F32), SDS((n, D), BF16)],
        scratch_shapes=[pltpu.VMEM((t + 8, D), F32), pltpu.VMEM((t, D), F32), pltpu.VMEM((t, D), F32),
                        pltpu.VMEM((8, D), F32)],
        compiler_params=_params(48),
    )(z, z, cw8, cb, wx, wa, bx, ba, lam)


def _lru_bwd(z, h_all, dya, cw8, cb, wx, wa, bx, ba, lam, after, nb, s_len):
    n = nb * s_len
    t = LRU_T
    ns = s_len // t
    t8 = t // 8

    def body(xp_ref, xph_ref, ga_ref, h_ref, hh_ref, dya_ref, cw_ref, cb_ref, wx_ref, wa_ref, bx_ref, ba_ref,
             lam_ref, after_ref, dz_ref, gcw_ref, gcb_ref, gwx_ref, gwa_ref, gbx_ref, gba_ref, glam_ref,
             ext, hext, dext, a_s, u_s, dh_s, carry):
        b, s = pl.program_id(0), pl.program_id(1)
        first_tile = s == ns - 1

        @pl.when((b == 0) & (s == 0))
        def _():
            for ref in (gcw_ref, gcb_ref, gwx_ref, gwa_ref, gbx_ref, gba_ref, glam_ref):
                ref[...] = jnp.zeros(ref.shape, F32)

        @pl.when(s == 0)
        def _():
            dext[t:t + 8, :] = jnp.zeros((8, D), F32)
            carry[...] = jnp.zeros((8, D), F32)

        keep = jnp.where(first_tile, 0.0, 1.0)
        xp = xp_ref[0]
        ext[0:8, :] = xph_ref[0] * keep
        ext[8:8 + t, :] = xp
        hext[0:8, :] = hh_ref[...] * keep
        hext[8:8 + t, :] = h_ref[...]
        cw = cw_ref[...]
        lam = lam_ref[...]
        taps = _shifted(_groups(ext[...]), (3, 2, 1)) + [xp]
        xa = _conv(taps, cw, cb_ref[...])
        xab, gi, gr, sp, a, mult = _lru_gates(xa, wx_ref, wa_ref, bx_ref[...], ba_ref[...], lam)
        (h_prev,) = _shifted(_groups(hext[...]), (1,))
        ga = ga_ref[0]
        sg = _sigmoid(ga)
        dya_v = dya_ref[...]
        d_ga = dya_v * h_ref[...] * (sg * (1.0 + ga * (1.0 - sg)))
        g_in = dya_v * (ga * sg)

        (an,) = _shifted(jnp.concatenate([_groups(a), jnp.ones((1, 8, D), F32)], axis=0), (-1,))
        an, u = _groups(an), _groups(g_in)
        row = lax.broadcasted_iota(jnp.int32, an.shape, 1)
        for sh in (1, 2, 4):
            a_sh = pltpu.roll(an, 8 - sh, 1)
            u_sh = pltpu.roll(u, 8 - sh, 1)
            m = row < 8 - sh
            u = jnp.where(m, u + an * u_sh, u)
            an = jnp.where(m, an * a_sh, an)
        a_s[...] = an.reshape(t, D)
        u_s[...] = u.reshape(t, D)

        def step(i, c):
            r = pl.multiple_of((t8 - 1 - i) * 8, 8)
            dg = u_s[pl.ds(r, 8), :] + a_s[pl.ds(r, 8), :] * c
            dh_s[pl.ds(r, 8), :] = dg
            return dg[0:1, :]

        lax.fori_loop(0, t8, step, carry[0:1, :], unroll=4)
        dh = dh_s[...]
        carry[0:1, :] = a[0:1, :] * dh[0:1, :]

        d_a = dh * h_prev
        dux = dh * xa
        d_mult = dux * gi
        d_gi = dux * mult
        d_xa = dh * (mult * gi)
        d_loga = d_a * a - d_mult * ((a * a) / mult)
        d_gr = d_loga * (-LRU_C * sp)
        d_sp = jnp.sum(d_loga * (-LRU_C * gr), axis=0, keepdims=True)
        glam_ref[...] += d_sp * (-_sigmoid(-lam))
        d_pi = d_gi * gi * (1.0 - gi)
        d_pr = d_gr * gr * (1.0 - gr)
        gbx_ref[...] += jnp.sum(d_pi, axis=0, keepdims=True)
        gba_ref[...] += jnp.sum(d_pr, axis=0, keepdims=True)
        dpib = d_pi.astype(BF16)
        dprb = d_pr.astype(BF16)
        back = []
        for h in range(NB):
            cs = slice(h * BD, (h + 1) * BD)
            gwx_ref[h] += lax.dot_general(xab[:, cs], dpib[:, cs], TN_DIMS, preferred_element_type=F32)
            gwa_ref[h] += lax.dot_general(xab[:, cs], dprb[:, cs], TN_DIMS, preferred_element_type=F32)
            back.append(lax.dot_general(dpib[:, cs], wx_ref[h], NT_DIMS, preferred_element_type=F32)
                        + lax.dot_general(dprb[:, cs], wa_ref[h], NT_DIMS, preferred_element_type=F32))
        d_xa = d_xa + jnp.concatenate(back, axis=1)

        dext[0:t, :] = d_xa
        later = _shifted(_groups(dext[...]), (-3, -2, -1))
        d_xp = later[0] * cw[0:1, :] + later[1] * cw[1:2, :]
        d_xp = d_xp + later[2] * cw[2:3, :]
        d_xp = d_xp + d_xa * cw[3:4, :]
        dext[t:t + 8, :] = d_xa[0:8, :]
        gcb_ref[...] += jnp.sum(d_xa, axis=0, keepdims=True)
        for k in range(4):
            gcw_ref[k:k + 1, :] += jnp.sum(d_xa * taps[k], axis=0, keepdims=True)
        dz_ref[0] = d_xp.astype(BF16)
        dz_ref[1] = d_ga.astype(BF16)

    rb = lambda b, s: b * ns + (ns - 1 - s)
    halo = lambda b, s: jnp.maximum(rb(b, s) * t8 - 1, 0)
    rep2 = lambda b, s: (0, 0)
    rep3 = lambda b, s: (0, 0, 0)
    return pl.pallas_call(
        body, name="lru_bwd", grid=(nb, ns),
        in_specs=[pl.BlockSpec((1, t, D), lambda b, s: (0, rb(b, s), 0)),
                  pl.BlockSpec((1, 8, D), lambda b, s: (0, halo(b, s), 0)),
                  pl.BlockSpec((1, t, D), lambda b, s: (1, rb(b, s), 0)),
                  pl.BlockSpec((t, D), lambda b, s: (rb(b, s), 0)),
                  pl.BlockSpec((8, D), lambda b, s: (halo(b, s), 0)),
                  pl.BlockSpec((t, D), lambda b, s: (rb(b, s), 0)),
                  pl.BlockSpec((8, D), rep2), pl.BlockSpec((1, D), rep2),
                  pl.BlockSpec((NB, BD, BD), rep3), pl.BlockSpec((NB, BD, BD), rep3),
                  pl.BlockSpec((1, D), rep2), pl.BlockSpec((1, D), rep2), pl.BlockSpec((1, D), rep2), ANY],
        out_specs=[pl.BlockSpec((2, t, D), lambda b, s: (0, rb(b, s), 0)),
                   pl.BlockSpec((8, D), rep2), pl.BlockSpec((1, D), rep2),
                   pl.BlockSpec((NB, BD, BD), rep3), pl.BlockSpec((NB, BD, BD), rep3),
                   pl.BlockSpec((1, D), rep2), pl.BlockSpec((1, D), rep2), pl.BlockSpec((1, D), rep2)],
        out_shape=[SDS((2, n, D), BF16), SDS((8, D), F32), SDS((1, D), F32),
                   SDS((NB, BD, BD), F32), SDS((NB, BD, BD), F32),
                   SDS((1, D), F32), SDS((1, D), F32), SDS((1, D), F32)],
        scratch_shapes=[pltpu.VMEM((t + 8, D), F32), pltpu.VMEM((t + 8, D), F32), pltpu.VMEM((t + 8, D), F32),
                        pltpu.VMEM((t, D), F32), pltpu.VMEM((t, D), F32), pltpu.VMEM((t, D), F32),
                        pltpu.VMEM((8, D), F32)],
        compiler_params=_params(56),
    )(z, z, z, h_all, h_all, dya, cw8, cb, wx, wa, bx, ba, lam, after)


HG_T = 512
HG_NC = HG_T // CHUNK
BNT_DIMS = (((2,), (2,)), ((0,), (0,)))
BNN_DIMS = (((2,), (1,)), ((0,), (0,)))
BTN_DIMS = (((1,), (1,)), ((0,), (0,)))


def _lower_bound(lg):
    m = jnp.max(lg, axis=0, keepdims=True)
    e = jnp.exp(lg - m)
    return e[0:1, :] / jnp.sum(e, axis=0, keepdims=True)


def _tri(upper):
    r = lax.broadcasted_iota(jnp.int32, (HG_NC, CHUNK, CHUNK), 1)
    c = lax.broadcasted_iota(jnp.int32, (HG_NC, CHUNK, CHUNK), 2)
    return (c >= r) if upper else (r >= c)


def _bdot(a, b, dims):
    return lax.dot_general(a, b, dims, preferred_element_type=F32)


def _tri_sums(upper, a):
    tri = _tri(upper).astype(BF16)
    a1 = a.astype(BF16)
    r1 = a - a1.astype(F32)
    a2 = r1.astype(BF16)
    a3 = (r1 - a2.astype(F32)).astype(BF16)
    return _bdot(tri, a1, BNN_DIMS) + (_bdot(tri, a2, BNN_DIMS) + _bdot(tri, a3, BNN_DIMS))


def _chunks(a):
    return a.reshape(HG_NC, CHUNK, BD)


def _hg_tile(q, fp, lb):
    q, fp = _chunks(q), _chunks(fp)
    sig = _sigmoid(fp)
    f = lb + (1.0 - lb) * sig
    log_f = jnp.log(f)
    k = 1.0 - f
    b = _tri_sums(False, log_f)
    b_mid = b[:, CHUNK // 2:CHUNK // 2 + 1, :]
    b_last = b[:, CHUNK - 1:CHUNK, :]
    sq = _sigmoid(q)
    qh = q * sq
    e_qi = jnp.exp(b - b_mid)
    e_ki = jnp.exp(b_mid - b)
    e_qs = jnp.exp(b)
    e_ks = jnp.exp(b_last - b)
    dc = jnp.exp(b_last)
    q_in = (qh * e_qi) * HG_SCALE
    k_in = k * e_ki
    q_st = (qh * e_qs) * HG_SCALE
    k_st = k * e_ks
    att = _bdot(q_in.astype(BF16), k_in.astype(BF16), BNT_DIMS)
    att = jnp.where(_tri(False), att, 0.0)
    return dict(q=q, sig=sig, f=f, k=k, sq=sq, e_qi=e_qi, e_ki=e_ki, e_qs=e_qs, e_ks=e_ks, dc=dc,
                q_in=q_in, k_in=k_in, q_st=q_st, k_st=k_st, att=att)


def _hgrn_fwd(z, lb_logits, hg_g, nb, s_len):
    n = nb * s_len
    t = HG_T
    ns = s_len // t
    nchunk = s_len // CHUNK

    def body(q_ref, f_ref, v_ref, gb_ref, lg_ref, g_ref, o_ref, yb_ref, st_ref, st):
        @pl.when(pl.program_id(1) == 0)
        def _():
            st[...] = jnp.zeros((NB, BD, BD), F32)

        def head(h, carry):
            cols = pl.ds(pl.multiple_of(h * BD, BD), BD)
            lb = _lower_bound(lg_ref[:, cols])
            ck = _hg_tile(q_ref[0, :, cols], f_ref[0, :, cols], lb)
            vb = _chunks(v_ref[0, :, cols]).astype(BF16)
            kv = _bdot(vb, ck["k_st"].astype(BF16), BTN_DIMS)
            states = [st[h]]
            for c in range(HG_NC):
                states.append(states[c] * ck["dc"][c] + kv[c])
            st[h] = states[HG_NC]
            s_in = jnp.stack(states[:HG_NC], axis=0)
            st_ref[h] = s_in
            o = (_bdot(ck["att"].astype(BF16), vb, BNN_DIMS)
                 + _bdot(ck["q_st"].astype(BF16), s_in.astype(BF16), BNT_DIMS))
            o_ref[:, cols] = o.reshape(t, BD)
            r = lax.rsqrt(jnp.mean(o * o, axis=-1, keepdims=True) + EPS)
            gb = _chunks(gb_ref[0, :, cols])
            yb_ref[:, cols] = (((o * r) * g_ref[...]) * (gb * _sigmoid(gb))).astype(BF16).reshape(t, BD)
            return carry

        lax.fori_loop(0, NB, head, 0, unroll=4)

    seg = lambda j: pl.BlockSpec((1, t, D), lambda b, s: (j, b * ns + s, 0))
    tile = pl.BlockSpec((t, D), lambda b, s: (b * ns + s, 0))
    return pl.pallas_call(
        body, name="hgrn_fwd", grid=(nb, ns),
        in_specs=[seg(2), seg(3), seg(4), seg(5),
                  pl.BlockSpec((2, D), lambda b, s: (0, 0)), pl.BlockSpec((1, BD), lambda b, s: (0, 0))],
        out_specs=[tile, tile, pl.BlockSpec((NB, HG_NC, BD, BD), lambda b, s: (b, s, 0, 0))],
        out_shape=[SDS((n, D), F32), SDS((n, D), BF16), SDS((nb * NB, nchunk, BD, BD), F32)],
        scratch_shapes=[pltpu.VMEM((NB, BD, BD), F32)],
        compiler_params=_params(56),
    )(z, z, z, z, lb_logits, hg_g)


def _hgrn_bwd(z, o_all, st_all, dyb, lb_logits, hg_g, g_mid, nb, s_len):
    n = nb * s_len
    t = HG_T
    ns = s_len // t

    def body(q_ref, f_ref, v_ref, gb_ref, o_ref, st_ref, dyb_ref, lg_ref, g_ref, gmid_hbm,
             dz_ref, glg_ref, ghg_ref, got_mid, dst, dlb, send_sems, recv_sems):
        b, s = pl.program_id(0), pl.program_id(1)
        x, y, c = _place()
        exchange = [pltpu.make_async_remote_copy(
            src_ref=gmid_hbm.at[2 * q + (1 - c)], dst_ref=got_mid.at[q], send_sem=send_sems.at[q],
            recv_sem=recv_sems.at[q], device_id=(x, y, 1 - c), device_id_type=MESH) for q in range(4)]

        @pl.when((b == 0) & (s == 0))
        def _():
            for cp in exchange:
                cp.start()

        @pl.when((b == 0) & (s == 0))
        def _():
            ghg_ref[...] = jnp.zeros((1, BD), F32)
            dlb[...] = jnp.zeros((8, D), F32)

        @pl.when(s == 0)
        def _():
            dst[...] = jnp.zeros((NB, BD, BD), F32)

        g = g_ref[...]

        def head(h, carry):
            cols = pl.ds(pl.multiple_of(h * BD, BD), BD)
            lb = _lower_bound(lg_ref[:, cols])
            ck = _hg_tile(q_ref[0, :, cols], f_ref[0, :, cols], lb)
            q = ck["q"]
            vb = _chunks(v_ref[0, :, cols]).astype(BF16)
            gb = _chunks(gb_ref[0, :, cols])
            o = _chunks(o_ref[:, cols])
            dyb_v = _chunks(dyb_ref[:, cols])
            s_in = st_ref[h]

            sgb = _sigmoid(gb)
            r = lax.rsqrt(jnp.mean(o * o, axis=-1, keepdims=True) + EPS)
            ohat = o * r
            d_on = dyb_v * (gb * sgb)
            d_gb = dyb_v * (ohat * g) * (sgb * (1.0 + gb * (1.0 - sgb)))
            ghg_ref[...] += jnp.sum(jnp.sum(d_on * ohat, axis=1), axis=0, keepdims=True)
            tt = d_on * g
            d_o = r * (tt - ohat * jnp.mean(tt * ohat, axis=-1, keepdims=True))
            dob = d_o.astype(BF16)

            attb = ck["att"].astype(BF16)
            q_inb, k_inb = ck["q_in"].astype(BF16), ck["k_in"].astype(BF16)
            q_stb, k_stb = ck["q_st"].astype(BF16), ck["k_st"].astype(BF16)
            d_att = jnp.where(_tri(False), _bdot(dob, vb, BNT_DIMS), 0.0).astype(BF16)
            d_q_in = _bdot(d_att, k_inb, BNN_DIMS)
            d_k_in = _bdot(d_att, q_inb, BTN_DIMS)
            d_q_st = _bdot(dob, s_in.astype(BF16), BNN_DIMS)
            qdo = _bdot(dob, q_stb, BTN_DIMS)
            d_states = [None] * HG_NC + [dst[h]]
            for c in reversed(range(HG_NC)):
                d_states[c] = d_states[c + 1] * ck["dc"][c] + qdo[c]
            dst[h] = d_states[0]
            ds_out = jnp.stack(d_states[1:], axis=0)
            dsb = ds_out.astype(BF16)
            d_v = _bdot(attb, dob, BTN_DIMS) + _bdot(k_stb, dsb, BNT_DIMS)
            d_k_st = _bdot(vb, dsb, BNN_DIMS)
            d_dc = jnp.sum(ds_out * s_in, axis=1, keepdims=True)

            p_qi = d_q_in * ck["q_in"]
            p_ki = d_k_in * ck["k_in"]
            p_qs = d_q_st * ck["q_st"]
            p_ks = d_k_st * ck["k_st"]
            d_qh = (d_q_in * ck["e_qi"] + d_q_st * ck["e_qs"]) * HG_SCALE
            d_k = d_k_in * ck["e_ki"] + d_k_st * ck["e_ks"]
            d_b = (p_qi - p_ki) + (p_qs - p_ks)
            d_b_mid = jnp.sum(p_ki - p_qi, axis=1, keepdims=True)
            d_b_last = jnp.sum(p_ks, axis=1, keepdims=True) + d_dc * ck["dc"]
            rowi = lax.broadcasted_iota(jnp.int32, (HG_NC, CHUNK, BD), 1)
            d_b = d_b + jnp.where(rowi == CHUNK // 2, d_b_mid, 0.0) + jnp.where(rowi == CHUNK - 1, d_b_last, 0.0)
            d_logf = _tri_sums(True, d_b)
            d_f = d_logf / ck["f"] - d_k
            sig, sq = ck["sig"], ck["sq"]
            d_fp = d_f * (1.0 - lb) * (sig * (1.0 - sig))
            dlb[0:1, cols] += jnp.sum(jnp.sum(d_f * (1.0 - sig), axis=1), axis=0, keepdims=True)
            d_q = d_qh * (sq * (1.0 + q * (1.0 - sq)))
            dz_ref[0, :, cols] = d_q.astype(BF16).reshape(t, BD)
            dz_ref[1, :, cols] = d_fp.astype(BF16).reshape(t, BD)
            dz_ref[2, :, cols] = d_v.astype(BF16).reshape(t, BD)
            dz_ref[3, :, cols] = d_gb.astype(BF16).reshape(t, BD)
            return carry

        lax.fori_loop(0, NB, head, 0, unroll=2)

        @pl.when((b == nb - 1) & (s == ns - 1))
        def _():
            lb = _lower_bound(lg_ref[...])
            dl = dlb[0:1, :] * (lb * (1.0 - lb))
            glg_ref[0:1, :] = dl
            glg_ref[1:2, :] = -dl
            for cp in exchange:
                cp.wait_send()
            for cp in exchange:
                cp.wait_recv()

    rb = lambda b, s: b * ns + (ns - 1 - s)
    seg = lambda j: pl.BlockSpec((1, t, D), lambda b, s: (j, rb(b, s), 0))
    tile = pl.BlockSpec((t, D), lambda b, s: (rb(b, s), 0))
    return pl.pallas_call(
        body, name="hgrn_bwd", grid=(nb, ns),
        in_specs=[seg(2), seg(3), seg(4), seg(5), tile,
                  pl.BlockSpec((NB, HG_NC, BD, BD), lambda b, s: (b, ns - 1 - s, 0, 0)),
                  tile, pl.BlockSpec((2, D), lambda b, s: (0, 0)), pl.BlockSpec((1, BD), lambda b, s: (0, 0)), ANY],
        out_specs=[pl.BlockSpec((4, t, D), lambda b, s: (0, rb(b, s), 0)),
                   pl.BlockSpec((2, D), lambda b, s: (0, 0)), pl.BlockSpec((1, BD), lambda b, s: (0, 0)), ANY],
        out_shape=[SDS((4, n, D), BF16), SDS((2, D), F32), SDS((1, BD), F32), SDS((4,) + g_mid.shape[1:], F32)],
        scratch_shapes=[pltpu.VMEM((NB, BD, BD), F32), pltpu.VMEM((8, D), F32),
                        pltpu.SemaphoreType.DMA((4,)), pltpu.SemaphoreType.DMA((4,))],
        compiler_params=_params(60),
    )(z, z, z, z, o_all, st_all, dyb, lb_logits, hg_g, g_mid)


def _mid(ya, yb, z, b_merge, x2, tgt, fin_g, pa, pb, wo):
    n = x2.shape[0]
    tm = 256
    ni = n // tm

    def body(ya_ref, yb_ref, gma_ref, gmb_ref, bm_ref, x_ref, t_ref, fg_ref, pa_hbm, pb_hbm, wo_hbm,
             dx2_ref, dya_ref, dyb_ref, dgm_ref, loss_ref, gfg_ref, gbm_ref, gm_hbm,
             pa_v, pb_v, wo_v, gpa_v, gpb_v, gwo_v, sem):
        i = pl.program_id(0)
        by_owner = lambda g: g.reshape(NB, BD, D)
        loads = [pltpu.make_async_copy(src, dst, sem.at[k])
                 for k, (src, dst) in enumerate(((pa_hbm, pa_v), (pb_hbm, pb_v), (wo_hbm, wo_v)))]
        stores = [pltpu.make_async_copy(src, dst, sem.at[k])
                  for k, (src, dst) in enumerate((g, gm_hbm.at[:, pl.ds(slot * BD, BD), :])
                                                 for slot, g in enumerate((gpa_v, gpb_v, gwo_v)))]

        @pl.when(i == 0)
        def _():
            for cp in loads:
                cp.start()
            for ref in (gpa_v, gpb_v, gwo_v, loss_ref, gfg_ref, gbm_ref):
                ref[...] = jnp.zeros(ref.shape, F32)
            for cp in loads:
                cp.wait()

        ya_v = ya_ref[...]
        yb_v = yb_ref[...]
        out_a = jnp.dot(ya_v, pa_v[...], preferred_element_type=F32)
        out_b = jnp.dot(yb_v, pb_v[...], preferred_element_type=F32)
        bm = bm_ref[...]
        g_a = _sigmoid(gma_ref[0] + bm[:, 0:D])
        g_b = _sigmoid(gmb_ref[0] + bm[:, D:2 * D])
        mixed = g_a * out_a + g_b * out_b
        mixb = mixed.astype(BF16)
        xo = x_ref[...] + jnp.dot(mixb, wo_v[...], preferred_element_type=F32)
        r = lax.rsqrt(jnp.mean(xo * xo, axis=-1, keepdims=True) + EPS)
        xn = xo * r
        fg = fg_ref[...]
        e = xn * fg - t_ref[...]
        loss_ref[...] += 0.5 * jnp.sum(jnp.mean(e * e, axis=-1, keepdims=True))
        dy = e * (1.0 / D)
        gfg_ref[...] += jnp.sum(dy * xn, axis=0, keepdims=True)
        dxn = dy * fg
        dx2 = r * (dxn - xn * jnp.mean(dxn * xn, axis=-1, keepdims=True))
        dx2_ref[...] = dx2
        dx2b = dx2.astype(BF16)
        d_mixed = lax.dot_general(dx2b, wo_v[...], NT_DIMS, preferred_element_type=F32)
        gwo_v[...] += by_owner(lax.dot_general(mixb, dx2b, TN_DIMS, preferred_element_type=F32))
        d_oa = (d_mixed * g_a).astype(BF16)
        d_ob = (d_mixed * g_b).astype(BF16)
        dgm_a = (d_mixed * out_a) * (g_a * (1.0 - g_a))
        dgm_b = (d_mixed * out_b) * (g_b * (1.0 - g_b))
        gbm_ref[:, 0:D] += jnp.sum(dgm_a, axis=0, keepdims=True)
        gbm_ref[:, D:2 * D] += jnp.sum(dgm_b, axis=0, keepdims=True)
        dgm_ref[0] = dgm_a.astype(BF16)
        dgm_ref[1] = dgm_b.astype(BF16)
        dya_ref[...] = lax.dot_general(d_oa, pa_v[...], NT_DIMS, preferred_element_type=F32)
        dyb_ref[...] = lax.dot_general(d_ob, pb_v[...], NT_DIMS, preferred_element_type=F32)
        gpa_v[...] += by_owner(lax.dot_general(ya_v, d_oa, TN_DIMS, preferred_element_type=F32))
        gpb_v[...] += by_owner(lax.dot_general(yb_v, d_ob, TN_DIMS, preferred_element_type=F32))

        @pl.when(i == ni - 1)
        def _():
            for cp in stores:
                cp.start()
            for cp in stores:
                cp.wait()

    rows = pl.BlockSpec((tm, D), lambda i: (i, 0))
    rep = lambda shape: pl.BlockSpec(shape, lambda i: (0,) * len(shape))
    return pl.pallas_call(
        body, name="mid", grid=(ni,),
        in_specs=[rows, rows,
                  pl.BlockSpec((1, tm, D), lambda i: (6, i, 0)), pl.BlockSpec((1, tm, D), lambda i: (7, i, 0)),
                  rep((1, 2 * D)), rows, rows, rep((1, D)), ANY, ANY, ANY],
        out_specs=[rows, rows, rows, pl.BlockSpec((2, tm, D), lambda i: (0, i, 0)),
                   rep((8, BD)), rep((1, D)), rep((1, 2 * D)), ANY],
        out_shape=[SDS((n, D), F32), SDS((n, D), F32), SDS((n, D), F32), SDS((2, n, D), BF16),
                   SDS((8, BD), F32), SDS((1, D), F32), SDS((1, 2 * D), F32),
                   SDS((NB, 3 * BD, D), F32)],
        scratch_shapes=[pltpu.VMEM((D, D), BF16)] * 3 + [pltpu.VMEM((NB, BD, D), F32)] * 3 + [pltpu.SemaphoreType.DMA((3,))],
        compiler_params=_params(60),
    )(ya, yb, z, z, b_merge, x2, tgt, fin_g, pa, pb, wo)


def _dz_specs(tm, ni, row_major):
    if row_major:
        ia = lambda i, j: (jnp.minimum(j, 1), i, 0)
        ib = lambda i, j: (jnp.clip(j - 2, 0, 3), i, 0)
        im = lambda i, j: (jnp.clip(j - 6, 0, 1), i, 0)
    else:
        last = ni - 1
        ia = lambda j, i: (jnp.minimum(j, 1), jnp.where(j < 2, i, last), 0)
        ib = lambda j, i: (jnp.clip(j - 2, 0, 3), jnp.where(j < 2, 0, jnp.where(j < 6, i, last)), 0)
        im = lambda j, i: (jnp.clip(j - 6, 0, 1), jnp.where(j < 6, 0, i), 0)
    return [pl.BlockSpec((1, tm, D), f) for f in (ia, ib, im)]


def _inproj_bwd_x(dza, dzb, dzm, w_all, x2, dx2, norm_g, after):
    n = x2.shape[0]
    tm = 512
    ni = n // tm

    def body(dza_ref, dzb_ref, dzm_ref, w_ref, x_ref, dx2_ref, g_ref, after_ref, gx_hbm, gg_ref, acc, stage, sems):
        j, i = pl.program_id(0), pl.program_id(1)
        rows = pl.ds(pl.multiple_of(i * tm, tm), tm)

        @pl.when((i == 0) & (j == 0))
        def _():
            gg_ref[...] = jnp.zeros((1, D), F32)

        @pl.when(j == 0)
        def _():
            acc[rows, :] = jnp.zeros((tm, D), F32)

        def add(ref):
            acc[rows, :] += lax.dot_general(ref[0], w_ref[0], NT_DIMS, preferred_element_type=F32)

        pl.when(j < 2)(lambda: add(dza_ref))
        pl.when((j >= 2) & (j < 6))(lambda: add(dzb_ref))
        pl.when(j >= 6)(lambda: add(dzm_ref))

        def store(tile):
            return pltpu.make_async_copy(stage.at[tile % 2], gx_hbm.at[pl.ds(pl.multiple_of(tile * tm, tm), tm), :],
                                         sems.at[tile % 2])

        @pl.when(j == NB - 1)
        def _():
            x = x_ref[...]
            r = lax.rsqrt(jnp.mean(x * x, axis=-1, keepdims=True) + EPS)
            xn = x * r
            dh = acc[rows, :]
            gg_ref[...] += jnp.sum(dh * xn, axis=0, keepdims=True)
            dxn = dh * g_ref[...]
            pl.when(i >= 2)(lambda: store(i - 2).wait())
            stage[i % 2] = dx2_ref[...] + r * (dxn - xn * jnp.mean(dxn * xn, axis=-1, keepdims=True))
            store(i).start()

            @pl.when(i == ni - 1)
            def _():
                for tile in range(max(ni - 2, 0), ni):
                    store(tile).wait()

    last = lambda j, i: (jnp.where(j == NB - 1, i, 0), 0)
    return pl.pallas_call(
        body, name="inproj_bwd_x", grid=(NB, ni),
        in_specs=_dz_specs(tm, ni, False) + [pl.BlockSpec((1, D, D), lambda j, i: (j, 0, 0)),
                                              pl.BlockSpec((tm, D), last), pl.BlockSpec((tm, D), last),
                                              pl.BlockSpec((1, D), lambda j, i: (0, 0)), ANY],
        out_specs=[ANY, pl.BlockSpec((1, D), lambda j, i: (0, 0))],
        out_shape=[SDS((n, D), F32), SDS((1, D), F32)],
        scratch_shapes=[pltpu.VMEM((n, D), F32), pltpu.VMEM((2, tm, D), F32), pltpu.SemaphoreType.DMA((2,))],
        compiler_params=_params(56),
    )(dza, dzb, dzm, w_all, x2, dx2, norm_g, after)


def _inproj_bwd_w(dza, dzb, dzm, h_all, g_m):
    n = h_all.shape[0]
    tm = min(n, 2048)
    ni = n // tm

    def body(dza_ref, dzb_ref, dzm_ref, h_ref, gm_hbm, gw_ref, got_w, got_m, stage, send_sems, recv_sems):
        j, i = pl.program_id(0), pl.program_id(1)
        x, y, c = _place()
        sibling = (x, y, 1 - c)

        def send_w(q):
            return pltpu.make_async_remote_copy(
                src_ref=stage.at[q % 2], dst_ref=got_w.at[q], send_sem=send_sems.at[q], recv_sem=recv_sems.at[q],
                device_id=sibling, device_id_type=MESH)

        def send_m(q):
            return pltpu.make_async_remote_copy(
                src_ref=gm_hbm.at[2 * q + (1 - c)], dst_ref=got_m.at[q], send_sem=send_sems.at[4 + q],
                recv_sem=recv_sems.at[4 + q], device_id=sibling, device_id_type=MESH)

        @pl.when((j == 0) & (i == 0))
        def _():
            for q in range(4):
                send_m(q).start()

        @pl.when(i == 0)
        def _():
            gw_ref[...] = jnp.zeros((1, D, D), F32)

        def add(ref):
            gw_ref[0] += lax.dot_general(h_ref[...], ref[0], TN_DIMS, preferred_element_type=F32)

        pl.when(j < 2)(lambda: add(dza_ref))
        pl.when((j >= 2) & (j < 6))(lambda: add(dzb_ref))
        pl.when(j >= 6)(lambda: add(dzm_ref))

        for q in range(4):
            @pl.when((i == ni - 1) & (j == 2 * q + 1 - c))
            def _(q=q):
                if q >= 2:
                    send_w(q - 2).wait_send()
                stage[q % 2] = gw_ref[0].astype(BF16)
                send_w(q).start()

        @pl.when((j == NB - 1) & (i == ni - 1))
        def _():
            for q in (2, 3):
                send_w(q).wait_send()
            for q in range(4):
                send_w(q).wait_recv()
                send_m(q).wait_send()
                send_m(q).wait_recv()

    return pl.pallas_call(
        body, name="inproj_bwd_w", grid=(NB, ni),
        in_specs=_dz_specs(tm, ni, False) + [pl.BlockSpec((tm, D), lambda j, i: (i, 0)), ANY],
        out_specs=[pl.BlockSpec((1, D, D), lambda j, i: (j, 0, 0)), ANY, ANY],
        out_shape=[SDS((NB, D, D), F32), SDS((4, D, D), BF16), SDS((4,) + g_m.shape[1:], F32)],
        scratch_shapes=[pltpu.VMEM((2, D, D), BF16), pltpu.SemaphoreType.DMA((8,)), pltpu.SemaphoreType.DMA((8,))],
        compiler_params=_params(58),
    )(dza, dzb, dzm, h_all, g_m)


def _adamw(w, g, m, v):
    rows, cols = w.shape
    tr = _row_tile(rows)

    spec = pl.BlockSpec((tr, cols), lambda i: (i, 0))
    return pl.pallas_call(
        functools.partial(_adam_refs), name="adamw", grid=(rows // tr,), in_specs=[spec] * 4, out_specs=[spec] * 3,
        out_shape=[SDS((rows, cols), F32)] * 3, compiler_params=_params(32),
    )(w, g, m, v)


def _adam_refs(w_ref, g_ref, m_ref, v_ref, d_ref, nm_ref, nv_ref):
    gv = g_ref[...]
    nm = ADAM_B1 * m_ref[...] + (1.0 - ADAM_B1) * gv
    nv = ADAM_B2 * v_ref[...] + (1.0 - ADAM_B2) * (gv * gv)
    m_hat = nm / (1.0 - ADAM_B1 ** ADAM_STEP)
    v_hat = nv / (1.0 - ADAM_B2 ** ADAM_STEP)
    d_ref[...] = -ADAM_LR * (m_hat / (jnp.sqrt(v_hat) + ADAM_EPS) + ADAM_WD * w_ref[...])
    nm_ref[...] = nm
    nv_ref[...] = nv


def _adamw_small(ws, gs, ms, vs):
    k = len(ws)

    def body(*refs):
        ins, outs = refs[:4 * k], refs[4 * k:7 * k]
        vin, vout = refs[7 * k:11 * k], refs[11 * k:14 * k]
        load_sems, store_sems = refs[14 * k:]
        loads = [pltpu.make_async_copy(ins[i], vin[i], load_sems.at[i]) for i in range(4 * k)]
        for cp in loads:
            cp.start()
        for cp in loads:
            cp.wait()
        for i in range(k):
            _adam_refs(*[vin[part * k + i] for part in range(4)], *[vout[part * k + i] for part in range(3)])
        stores = [pltpu.make_async_copy(vout[i], outs[i], store_sems.at[i]) for i in range(3 * k)]
        for cp in stores:
            cp.start()
        for cp in stores:
            cp.wait()

    shapes = [SDS(w.shape, F32) for w in ws]
    vmem = [pltpu.VMEM(w.shape, F32) for w in ws]
    out = pl.pallas_call(
        body, name="adamw_small", out_shape=shapes * 3, in_specs=[HBM] * (4 * k), out_specs=[HBM] * (3 * k),
        scratch_shapes=vmem * 7 + [pltpu.SemaphoreType.DMA((4 * k,)), pltpu.SemaphoreType.DMA((3 * k,))],
        compiler_params=_params(32),
    )(*ws, *gs, *ms, *vs)
    return out[:k], out[k:2 * k], out[2 * k:]


def _gather_tail(tail):
    def body(tail_ref, out_ref, stage, send_sems, recv_sems, local_sem):
        x, y, c = _place()
        me, sibling = (x, y, c), (x, y, 1 - c)
        chips = _other_chips(x, y)

        def copy(k, block, to, src=None):
            return pltpu.make_async_remote_copy(
                src_ref=out_ref.at[_block_id(block)] if src is None else src, dst_ref=out_ref.at[_block_id(block)],
                send_sem=send_sems.at[k], recv_sem=recv_sems.at[k], device_id=to, device_id_type=MESH)

        stage[...] = tail_ref[...]
        mine = pltpu.make_async_copy(stage, out_ref.at[_block_id(me)], local_sem)
        mine.start()
        sends = [copy(0, me, sibling, src=stage)] + [copy(1 + slot, me, (*chip, c), src=stage)
                                                     for slot, chip in enumerate(chips)]
        for cp in sends:
            cp.start()
        for slot, chip in enumerate(chips):
            copy(1 + slot, (*chip, c), me).wait_recv()
            sends.append(copy(4 + slot, (*chip, c), sibling))
            sends[-1].start()
        copy(0, sibling, me).wait_recv()
        for slot, chip in enumerate(chips):
            copy(4 + slot, (*chip, 1 - c), me).wait_recv()
        for cp in sends:
            cp.wait_send()
        mine.wait()

    return pl.pallas_call(
        body, name="gather_small_grads", in_specs=[pl.BlockSpec(memory_space=pltpu.VMEM)], out_specs=ANY,
        out_shape=SDS((NB,) + tail.shape, F32),
        scratch_shapes=[pltpu.VMEM(tail.shape, F32), pltpu.SemaphoreType.DMA((7,)), pltpu.SemaphoreType.DMA((7,)),
                        pltpu.SemaphoreType.DMA(())],
    )(tail)


HBM = pl.BlockSpec(memory_space=pltpu.HBM)
SEMS = pl.BlockSpec(memory_space=pltpu.SEMAPHORE)
EFFECT = pltpu.SideEffectType.DATAFLOW_SIDE_EFFECTING


def _chip_copies(srcs, lands, send_sems, recv_sems):
    x, y, c = _place()
    return [pltpu.make_async_remote_copy(
        src_ref=srcs[a].at[slot], dst_ref=lands[a].at[slot],
        send_sem=send_sems.at[3 * a + slot], recv_sem=recv_sems.at[3 * a + slot],
        device_id=(px, py, c), device_id_type=MESH)
        for a in range(len(srcs)) for slot, (px, py) in enumerate(_other_chips(x, y))]


def _split_start(name, copies, per_array, srcs, lands, after=None):
    na = len(srcs)

    def body(*refs):
        send_sems, recv_sems = refs[-2 * na - 3], refs[-2 * na - 2]
        for cp in copies(refs[:na], refs[na:2 * na], send_sems, recv_sems):
            cp.start()
        refs[-1][...] = jnp.zeros_like(refs[-1])

    hbm = lambda a: pltpu.HBM(a.shape, a.dtype)
    pin = lambda a: pltpu.with_memory_space_constraint(a, pltpu.HBM)
    out = pl.pallas_call(
        body, name=name,
        out_shape=(pltpu.SemaphoreType.DMA((per_array * na,)), pltpu.SemaphoreType.DMA((per_array * na,)),
                   *[hbm(a) for a in srcs], *[hbm(a) for a in lands], SDS((8, BD), F32)),
        in_specs=[HBM] * (2 * na) + ([] if after is None else [ANY]),
        out_specs=(SEMS, SEMS, *[HBM] * (2 * na), pl.BlockSpec(memory_space=pltpu.VMEM)),
        input_output_aliases={i: 2 + i for i in range(2 * na)},
        compiler_params=pltpu.CompilerParams(has_side_effects=EFFECT),
    )(*[pin(a) for a in srcs], *[pin(a) for a in lands], *([] if after is None else [after]))
    return out[0], out[1], out[2:2 + na], out[2 + na:2 + 2 * na], out[-1]


def _split_wait(name, copies, started, after):
    send_sems, recv_sems, srcs, lands, _ = started
    na = len(srcs)

    def body(*refs):
        waits = copies(refs[:na], refs[na:2 * na], refs[2 * na], refs[2 * na + 1])
        for cp in waits:
            cp.wait_send()
        for cp in waits:
            cp.wait_recv()

    hbm = lambda a: pltpu.HBM(a.shape, a.dtype)
    out = pl.pallas_call(
        body, name=name,
        out_shape=(*[hbm(a) for a in srcs], *[hbm(a) for a in lands]),
        in_specs=[HBM] * (2 * na) + [SEMS, SEMS, ANY],
        out_specs=tuple([HBM] * (2 * na)),
        input_output_aliases={i: i for i in range(2 * na)},
        compiler_params=pltpu.CompilerParams(has_side_effects=EFFECT),
    )(*srcs, *lands, send_sems, recv_sems, after)
    return out[na:]


def _add_sibling(place, g, a_in):
    _, r, cols = g.shape
    tr = _row_tile(r)

    def chip(k, pr):
        qx = pr[0] if k in (1, 3) else 1 - pr[0]
        qy = pr[1] if k in (0, 3) else 1 - pr[1]
        return 2 * qx + qy

    def body(place_ref, *refs):
        g_refs, a_refs, (out_ref, own_ref) = refs[0:4], refs[4:8], refs[8:10]
        for k in range(3):
            out_ref[k] = (g_refs[k][0] + a_refs[k][0].astype(F32)).astype(BF16)
        own_ref[...] = g_refs[3][0] + a_refs[3][0].astype(F32)

    mine = lambda k: pl.BlockSpec((1, tr, cols), lambda i, pr: (2 * chip(k, pr) + pr[2], i, 0))
    theirs = lambda k: pl.BlockSpec((1, tr, cols), lambda i, pr: (chip(k, pr), i, 0))
    return pl.pallas_call(
        body, name="add_sibling",
        grid_spec=pltpu.PrefetchScalarGridSpec(
            num_scalar_prefetch=1, grid=(r // tr,),
            in_specs=[mine(k) for k in range(4)] + [theirs(k) for k in range(4)],
            out_specs=[pl.BlockSpec((3, tr, cols), lambda i, pr: (0, i, 0)),
                       pl.BlockSpec((tr, cols), lambda i, pr: (i, 0))]),
        out_shape=[SDS((3, r, cols), BF16), SDS((r, cols), F32)], compiler_params=_params(48),
    )(place, *[g] * 4, *[a_in] * 4)


def _add_chips(own, b_in):
    r, cols = own.shape
    tr = _row_tile(r)

    def body(p_ref, b0_ref, b1_ref, b2_ref, o_ref):
        o_ref[...] = ((p_ref[...] + b0_ref[0].astype(F32)) + b1_ref[0].astype(F32)) + b2_ref[0].astype(F32)

    slot = lambda k: pl.BlockSpec((1, tr, cols), lambda i: (k, i, 0))
    spec = pl.BlockSpec((tr, cols), lambda i: (i, 0))
    return pl.pallas_call(
        body, name="add_chips", grid=(r // tr,), in_specs=[spec, slot(0), slot(1), slot(2)], out_specs=spec,
        out_shape=SDS((r, cols), F32), compiler_params=_params(32),
    )(own, b_in, b_in, b_in)


VEC_NAMES = ("b_merge", "conv_b", "rg_bx", "rg_ba", "rg_lambda", "hg_lb_logits", "hg_norm_g", "final_norm_g")
REP_NAMES = ("rg_wx", "rg_wa", "norm_g") + VEC_NAMES
SMALL_ROWS = 48


def _sum_blocks(parts):
    def body(p_ref, o_ref):
        acc = p_ref[0]
        for k in range(1, NB):
            acc = acc + p_ref[k]
        o_ref[...] = acc

    return pl.pallas_call(body, name="sum_blocks", out_shape=SDS(parts.shape[1:], F32))(parts)


def _pack_rows(arrays, width, row_multiple=8):
    flat = jnp.concatenate([a.reshape(-1) for a in arrays])
    rows = -(-flat.shape[0] // width)
    rows = -(-rows // row_multiple) * row_multiple
    return jnp.pad(flat, (0, rows * width - flat.shape[0])).reshape(rows, width)


def _unpack(flat, like):
    out, off = [], 0
    for a in like:
        out.append(flat[off:off + a.size].reshape(a.shape))
        off += a.size
    return out


def kernel(x, w_in, b_merge, conv_w, conv_b, rg_wx, rg_bx, rg_wa, rg_ba, rg_lambda, hg_lb_logits, hg_norm_g, proj_a, proj_b, w_out, norm_g, final_norm_g, loss_target, m_w_in, m_b_merge, m_conv_w, m_conv_b, m_rg_wx, m_rg_bx, m_rg_wa, m_rg_ba, m_rg_lambda, m_hg_lb_logits, m_hg_norm_g, m_proj_a, m_proj_b, m_w_out, m_norm_g, m_final_norm_g, v_w_in, v_b_merge, v_conv_w, v_conv_b, v_rg_wx, v_rg_bx, v_rg_wa, v_rg_ba, v_rg_lambda, v_hg_lb_logits, v_hg_norm_g, v_proj_a, v_proj_b, v_w_out, v_norm_g, v_final_norm_g):
    weights = dict(w_in=w_in, b_merge=b_merge, conv_w=conv_w, conv_b=conv_b, rg_wx=rg_wx, rg_bx=rg_bx, rg_wa=rg_wa,
                   rg_ba=rg_ba, rg_lambda=rg_lambda, hg_lb_logits=hg_lb_logits, hg_norm_g=hg_norm_g, proj_a=proj_a,
                   proj_b=proj_b, w_out=w_out, norm_g=norm_g, final_norm_g=final_norm_g)
    mom1 = dict(w_in=m_w_in, b_merge=m_b_merge, conv_w=m_conv_w, conv_b=m_conv_b, rg_wx=m_rg_wx, rg_bx=m_rg_bx,
                rg_wa=m_rg_wa, rg_ba=m_rg_ba, rg_lambda=m_rg_lambda, hg_lb_logits=m_hg_lb_logits,
                hg_norm_g=m_hg_norm_g, proj_a=m_proj_a, proj_b=m_proj_b, w_out=m_w_out, norm_g=m_norm_g,
                final_norm_g=m_final_norm_g)
    mom2 = dict(w_in=v_w_in, b_merge=v_b_merge, conv_w=v_conv_w, conv_b=v_conv_b, rg_wx=v_rg_wx, rg_bx=v_rg_bx,
                rg_wa=v_rg_wa, rg_ba=v_rg_ba, rg_lambda=v_rg_lambda, hg_lb_logits=v_hg_lb_logits,
                hg_norm_g=v_hg_norm_g, proj_a=v_proj_a, proj_b=v_proj_b, w_out=v_w_out, norm_g=v_norm_g,
                final_norm_g=v_final_norm_g)
    order = list(weights)
    nb, s_len, _ = x.shape
    n = nb * s_len
    px, py, pc = _place()
    place = jnp.stack([px, py, pc]).astype(jnp.int32)

    in_hbm = lambda a: pltpu.with_memory_space_constraint(a, pltpu.HBM)
    norm_gain = in_hbm(norm_g)

    x2 = x.reshape(n, D)
    cw_blk = jnp.pad(conv_w[0], ((0, 4), (0, 0)))
    order_ids = jnp.stack([_block_id(p) for p in _arrival_order(px, py, pc)]).astype(jnp.int32)
    z, h_all, w_all, pa_all, pb_all, wo_all, cw_all = _gather_inproj(
        order_ids, x2, norm_gain, [w_in[0], proj_a[0], proj_b[0], w_out[0], cw_blk], [BF16, BF16, BF16, BF16, F32])
    pa_full, pb_full, wo_full = (a.reshape(D, D) for a in (pa_all, pb_all, wo_all))
    cw8 = in_hbm(cw_all.transpose(1, 0, 2).reshape(8, D))
    wx_b, wa_b = in_hbm(rg_wx[0].astype(BF16)), in_hbm(rg_wa[0].astype(BF16))
    cb, bx, ba, lam = (in_hbm(a.reshape(1, D)) for a in (conv_b, rg_bx, rg_ba, rg_lambda))
    fin_g, b_mrg = in_hbm(final_norm_g.reshape(1, D)), in_hbm(b_merge)
    lb_lg, hg_g = in_hbm(hg_lb_logits), in_hbm(hg_norm_g)

    hlru, ya = _lru_fwd(z, cw8, cb, wx_b, wa_b, bx, ba, lam, nb, s_len)
    o_all, yb, st_all = _hgrn_fwd(z, lb_lg, hg_g, nb, s_len)

    (dx2, dya, dyb, dzm, loss_acc, g_fin, g_bm, g_mid) = _mid(
        ya, yb, z, b_mrg, x2, loss_target.reshape(n, D), fin_g, pa_full, pb_full, wo_full)
    dzb, g_lg, g_hg, mid_from_sibling = _hgrn_bwd(z, o_all, st_all, dyb, lb_lg, hg_g, g_mid, nb, s_len)

    mid_out_bf, mid_own = _add_sibling(place, g_mid, mid_from_sibling)
    mid_sums = _split_start("rs_mid_start", _chip_copies, 3, [mid_out_bf], [lax.empty(mid_out_bf.shape, BF16)])
    dza, g_cw8, g_cb, g_wx, g_wa, g_bx, g_ba, g_lam = _lru_bwd(
        z, hlru, dya, cw8, cb, wx_b, wa_b, bx, ba, lam, mid_sums[-1], nb, s_len)
    part = dict(b_merge=g_bm, conv_b=g_cb, rg_bx=g_bx, rg_ba=g_ba, rg_lambda=g_lam, hg_lb_logits=g_lg,
                hg_norm_g=g_hg, final_norm_g=g_fin)
    vec = _pack_rows([part[k] for k in VEC_NAMES], BD)
    vec = jnp.pad(vec, ((0, 16 * NB - vec.shape[0]), (0, 0))).reshape(NB, 2, D)
    rows8 = lambda a: jnp.pad(a, ((0, 0), (0, 8 - a.shape[1]), (0, 0)))
    g_small = jnp.concatenate([g_wx.reshape(NB, 16, D), g_wa.reshape(NB, 16, D),
                               rows8(g_cw8.reshape(8, NB, BD).transpose(1, 0, 2).reshape(NB, 1, D)), rows8(vec)],
                              axis=1)
    g_w, w_from_sibling, small_from_sibling = _inproj_bwd_w(dza, dzb, dzm, h_all, g_small)
    w_out_bf, w_own = _add_sibling(place, g_w, w_from_sibling)
    small_out_bf, small_own = _add_sibling(place, g_small, small_from_sibling)
    outgoing = [w_out_bf, small_out_bf]
    chip_sums = _split_start("rs_chips_start", _chip_copies, 3, outgoing, [lax.empty(a.shape, a.dtype) for a in outgoing])
    grad_x, g_ng = _inproj_bwd_x(dza, dzb, dzm, w_all, x2, dx2, norm_gain, chip_sums[-1])
    (mid_from_chips,) = _split_wait("rs_mid_wait", _chip_copies, mid_sums, grad_x)
    w_from_chips, small_from_chips = _split_wait("rs_chips_wait", _chip_copies, chip_sums, grad_x)
    r_w = _add_chips(w_own, w_from_chips)
    r_mid = _add_chips(mid_own, mid_from_chips)
    r_small = _add_chips(small_own, small_from_chips)
    row = lax.broadcasted_iota(jnp.int32, (8, D), 0)
    mine = jnp.where(row == 0, g_ng, jnp.where(row == 1, loss_acc[0:1, 0:1], 0.0))
    tail_all = _gather_tail(jnp.concatenate([r_small, mine], axis=0))
    summed = _sum_blocks(tail_all[:, SMALL_ROWS:SMALL_ROWS + 8])

    sharded = ("w_in", "proj_a", "proj_b", "w_out")
    own_grads = dict(w_in=r_w, proj_a=r_mid[0:BD], proj_b=r_mid[BD:2 * BD], w_out=r_mid[2 * BD:3 * BD])
    grads = {k: own_grads[k].reshape(weights[k].shape) for k in sharded}
    grads.update(conv_w=r_small[32].reshape(8, BD)[0:4].reshape(1, 4, BD),
                 rg_wx=tail_all[:, 0:16].reshape(1, NB, BD, BD), rg_wa=tail_all[:, 16:32].reshape(1, NB, BD, BD),
                 norm_g=summed[0:1])
    vec_all = tail_all[:, 40:42].reshape(-1)
    for k, gk in zip(VEC_NAMES, _unpack(vec_all, [weights[k] for k in VEC_NAMES])):
        grads[k] = gk

    delta, new_m, new_v = {}, {}, {}
    flat2 = lambda a: a.reshape(-1, a.shape[-1])
    for k in sharded:
        outs = _adamw(flat2(weights[k]), own_grads[k], flat2(mom1[k]), flat2(mom2[k]))
        delta[k], new_m[k], new_v[k] = (a.reshape(weights[k].shape) for a in outs)
    rep = list(REP_NAMES) + ["conv_w"]
    outs = _adamw_small(*[[flat2(t[k]) for k in rep] for t in (weights, grads, mom1, mom2)])
    for tgt, arrays in zip((delta, new_m, new_v), outs):
        for k, a in zip(rep, arrays):
            tgt[k] = a.reshape(weights[k].shape)

    return (summed[1, 0], grad_x.reshape(x.shape), *[grads[k] for k in order], *[delta[k] for k in order],
            *[new_m[k] for k in order], *[new_v[k] for k in order])
```

```python
import functools

import jax
import jax.numpy as jnp
from jax import lax
from jax.experimental import pallas as pl
from jax.experimental.pallas import tpu as pltpu

F32 = jnp.float32
BF16 = jnp.bfloat16
SDS = jax.ShapeDtypeStruct
MESH = pl.DeviceIdType.MESH
ANY = pl.BlockSpec(memory_space=pl.ANY)

D = 1024
NB = 8
BD = D // NB
CHUNK = 64
EPS = 1e-6
LRU_C = 8.0
HG_SCALE = BD ** -0.5
ADAM_LR, ADAM_B1, ADAM_B2, ADAM_EPS, ADAM_WD, ADAM_STEP = 0.001, 0.9, 0.999, 1e-08, 0.01, 10

NT_DIMS = (((1,), (1,)), ((), ()))
TN_DIMS = (((0,), (0,)), ((), ()))


def _params(vmem_mib):
    return pltpu.CompilerParams(vmem_limit_bytes=vmem_mib << 20)


def _row_tile(rows, most=256):
    assert rows % 8 == 0
    return max(t for t in range(8, min(rows, most) + 1, 8) if rows % t == 0)


def _sigmoid(v):
    return 0.5 * (jnp.tanh(0.5 * v) + 1.0)


def _groups(v):
    return v.reshape(v.shape[0] // 8, 8, v.shape[1])


def _softplus_neg(lam):
    t = -lam
    e = jnp.exp(-jnp.abs(t))
    w = 1.0 + e
    d = w - 1.0
    l1p = jnp.where(d == 0.0, e, jnp.log(w) * (e / jnp.where(d == 0.0, 1.0, d)))
    return jnp.maximum(t, 0.0) + l1p


def _place():
    return lax.axis_index("x"), lax.axis_index("y"), lax.axis_index("c")


def _other_chips(x, y):
    return [(1 - x, y), (x, 1 - y), (1 - x, 1 - y)]


def _block_id(p):
    return 4 * p[0] + 2 * p[1] + p[2]


def _core_chips(x, y, c):
    near, far, diag = _other_chips(x, y)
    pick = lambda a, b: (jnp.where(c == 0, a[0], b[0]), jnp.where(c == 0, a[1], b[1]))
    return [pick(near, far), pick(far, near), diag]


def _arrival_order(x, y, c):
    first, second, diag = _core_chips(x, y, c)
    return [(x, y, c), (x, y, 1 - c), (*first, c), (*second, 1 - c), (*second, c), (*first, 1 - c),
            (*diag, c), (*diag, 1 - c)]


def _gather_inproj(order_ids, x2, norm_g, blocks, dtypes):
    na = len(blocks)
    n = x2.shape[0]
    tm = min(n, 1024)
    ni = n // tm

    def body(order_ref, x_ref, g_ref, *refs):
        ins, (z_ref, h_ref), outs = refs[:na], refs[na:na + 2], refs[na + 2:2 * na + 2]
        stages = refs[2 * na + 2:3 * na + 2]
        h_full, wbuf, send_sems, recv_sems, local_sems, wsems, hsem = refs[3 * na + 2:]
        j, i = pl.program_id(0), pl.program_id(1)
        x, y, c = _place()
        me, sibling = (x, y, c), (x, y, 1 - c)
        chips = _core_chips(x, y, c)
        sibling_chips = [chips[1], chips[0], chips[2]]
        small = range(1, na)

        def copy(a, k, block, to, src=None):
            return pltpu.make_async_remote_copy(
                src_ref=outs[a].at[_block_id(block)] if src is None else src, dst_ref=outs[a].at[_block_id(block)],
                send_sem=send_sems.at[7 * a + k], recv_sem=recv_sems.at[7 * a + k],
                device_id=to, device_id_type=MESH)

        def local(a):
            return pltpu.make_async_copy(stages[a], outs[a].at[_block_id(me)], local_sems.at[a])

        def landed(a, slot):
            copy(a, 1 + slot, (*chips[slot], c), me).wait_recv()
            copy(a, 4 + slot, (*chips[slot], c), sibling).start()
            if slot == 0:
                copy(a, 3, (*chips[0], c), (*chips[1], c)).start()

        def diagonal_and_small():
            landed(0, 2)
            for a in small:
                landed(a, 0)
                landed(a, 1)

        def passed_on(a, slot):
            copy(a, 4 + slot, (*sibling_chips[slot], 1 - c), me).wait_recv()

        def sibling_here_send_second():
            copy(0, 0, sibling, me).wait_recv()
            for a in range(na):
                copy(a, 2, me, (*chips[1], c), src=stages[a]).start()

        @pl.when((j == 0) & (i == 0))
        def _():
            for a in range(na):
                stages[a][...] = ins[a][...].astype(dtypes[a])
                local(a).start()
            for a in range(na):
                copy(a, 0, me, sibling, src=stages[a]).start()
                copy(a, 1, me, (*chips[0], c), src=stages[a]).start()

        @pl.when(j == 0)
        def _():
            xv = x_ref[...]
            r = lax.rsqrt(jnp.mean(xv * xv, axis=-1, keepdims=True) + EPS)
            hb = ((xv * r) * g_ref[...]).astype(BF16)
            h_full[pl.ds(pl.multiple_of(i * tm, tm), tm), :] = hb

        save_h = pltpu.make_async_copy(h_full, h_ref, hsem)
        pl.when((j == 0) & (i == ni - 1))(save_h.start)

        steps = [
            lambda: local(0).wait(),
            sibling_here_send_second,
            lambda: landed(0, 0),
            lambda: passed_on(0, 0),
            lambda: landed(0, 1),
            lambda: passed_on(0, 1),
            diagonal_and_small,
            lambda: passed_on(0, 2),
        ]
        def w_load(k):
            return pltpu.make_async_copy(outs[0].at[order_ref[k]], wbuf.at[k % 2], wsems.at[k % 2])

        for k, step in enumerate(steps):
            @pl.when((j == 0) & (i == 0) if k == 0 else (j == k - 1) & (i == ni - 1))
            def _(k=k, step=step):
                step()
                w_load(k).start()

        pl.when(i == 0)(lambda: w_load(j).wait())
        z_ref[0] = jnp.dot(h_full[pl.ds(pl.multiple_of(i * tm, tm), tm), :], wbuf[j % 2], preferred_element_type=F32)

        @pl.when((j == NB - 1) & (i == ni - 1))
        def _():
            save_h.wait()
            for a in small:
                landed(a, 2)
            for a in small:
                local(a).wait()
                copy(a, 0, sibling, me).wait_recv()
                for slot in range(3):
                    passed_on(a, slot)
            for a in range(na):
                copy(a, 0, me, sibling, src=stages[a]).wait_send()
                for slot, chip in enumerate(chips):
                    copy(a, 1 + slot, me, (*chip, c), src=stages[a]).wait_send()
                    copy(a, 4 + slot, (*chip, c), sibling).wait_send()

    rows_once = lambda j, i, order: (jnp.where(j == 0, i, ni - 1), 0)
    vmem = pl.BlockSpec(memory_space=pltpu.VMEM)
    return pl.pallas_call(
        body, name="gather_inproj",
        grid_spec=pltpu.PrefetchScalarGridSpec(
            num_scalar_prefetch=1, grid=(NB, ni),
            in_specs=[pl.BlockSpec((tm, D), rows_once), pl.BlockSpec((1, D), lambda j, i, order: (0, 0))] + [vmem] * na,
            out_specs=[pl.BlockSpec((1, tm, D), lambda j, i, order: (order[j], i, 0)), ANY] + [ANY] * na,
            scratch_shapes=[pltpu.VMEM(b.shape, dt) for b, dt in zip(blocks, dtypes)]
            + [pltpu.VMEM((n, D), BF16), pltpu.VMEM((2, D, D), BF16),
               pltpu.SemaphoreType.DMA((7 * na,)), pltpu.SemaphoreType.DMA((7 * na,)),
               pltpu.SemaphoreType.DMA((na,)), pltpu.SemaphoreType.DMA((2,)), pltpu.SemaphoreType.DMA(())]),
        out_shape=[SDS((NB, n, D), F32), SDS((n, D), BF16)] + [SDS((NB,) + b.shape, dt) for b, dt in zip(blocks, dtypes)],
        compiler_params=_params(56),
    )(order_ids, x2, norm_g, *blocks)


LRU_T = 256


def _shifted(groups, shifts):
    row = lax.broadcasted_iota(jnp.int32, (groups.shape[0] - 1,) + groups.shape[1:], 1)
    out = []
    for s in shifts:
        y = pltpu.roll(groups, s % 8, 1)
        moved = jnp.where(row >= s, y[1:], y[:-1]) if s > 0 else jnp.where(row < 8 + s, y[:-1], y[1:])
        out.append(moved.reshape(-1, groups.shape[2]))
    return out


def _conv(taps, cw, cb):
    acc = taps[0] * cw[0:1, :] + taps[1] * cw[1:2, :]
    acc = acc + taps[2] * cw[2:3, :]
    acc = acc + taps[3] * cw[3:4, :]
    return cb + acc


def _lru_gates(xa, wx_ref, wa_ref, bx, ba, lam):
    xab = xa.astype(BF16)
    pis, prs = [], []
    for h in range(NB):
        xs = xab[:, h * BD:(h + 1) * BD]
        pis.append(jnp.dot(xs, wx_ref[h], preferred_element_type=F32))
        prs.append(jnp.dot(xs, wa_ref[h], preferred_element_type=F32))
    gi = _sigmoid(jnp.concatenate(pis, axis=1) + bx)
    gr = _sigmoid(jnp.concatenate(prs, axis=1) + ba)
    sp = _softplus_neg(lam)
    log_a = (-LRU_C * gr) * sp
    a = jnp.exp(log_a)
    mult = jnp.sqrt(-jnp.tanh(log_a) * (a * a + 1.0))
    return xab, gi, gr, sp, a, mult


def _lru_fwd(z, cw8, cb, wx, wa, bx, ba, lam, nb, s_len):
    n = nb * s_len
    t = LRU_T
    ns = s_len // t

    def body(xp_ref, ga_ref, cw_ref, cb_ref, wx_ref, wa_ref, bx_ref, ba_ref, lam_ref,
             h_ref, ya_ref, ext, a_s, u_s, carry):
        @pl.when(pl.program_id(1) == 0)
        def _():
            ext[0:8, :] = jnp.zeros((8, D), F32)
            carry[...] = jnp.zeros((8, D), F32)

        xp = xp_ref[0]
        ext[8:8 + t, :] = xp
        xa = _conv(_shifted(_groups(ext[...]), (3, 2, 1)) + [xp], cw_ref[...], cb_ref[...])
        ext[0:8, :] = xp[t - 8:t, :]
        _, gi, _, _, a, mult = _lru_gates(xa, wx_ref, wa_ref, bx_ref[...], ba_ref[...], lam_ref[...])
        u = (mult * gi) * xa
        a, u = _groups(a), _groups(u)
        row = lax.broadcasted_iota(jnp.int32, a.shape, 1)
        for sh in (1, 2, 4):
            a_sh = pltpu.roll(a, sh, 1)
            u_sh = pltpu.roll(u, sh, 1)
            m = row >= sh
            u = jnp.where(m, a * u_sh + u, u)
            a = jnp.where(m, a * a_sh, a)
        a_s[...] = a.reshape(t, D)
        u_s[...] = u.reshape(t, D)

        def step(g, c):
            r = pl.multiple_of(g * 8, 8)
            hg = u_s[pl.ds(r, 8), :] + a_s[pl.ds(r, 8), :] * c
            h_ref[pl.ds(r, 8), :] = hg
            return hg[7:8, :]

        c_out = lax.fori_loop(0, t // 8, step, carry[0:1, :], unroll=4)
        carry[0:1, :] = c_out
        ga = ga_ref[0]
        ya_ref[...] = (h_ref[...] * (ga * _sigmoid(ga))).astype(BF16)

    row_map = lambda b, s: (b * ns + s, 0)
    rep2 = lambda b, s: (0, 0)
    rep3 = lambda b, s: (0, 0, 0)
    return pl.pallas_call(
        body, name="lru_fwd", grid=(nb, ns),
        in_specs=[pl.BlockSpec((1, t, D), lambda b, s: (0, b * ns + s, 0)),
                  pl.BlockSpec((1, t, D), lambda b, s: (1, b * ns + s, 0)),
                  pl.BlockSpec((8, D), rep2), pl.BlockSpec((1, D), rep2),
                  pl.BlockSpec((NB, BD, BD), rep3), pl.BlockSpec((NB, BD, BD), rep3),
                  pl.BlockSpec((1, D), rep2), pl.BlockSpec((1, D), rep2), pl.BlockSpec((1, D), rep2)],
        out_specs=[pl.BlockSpec((t, D), row_map), pl.BlockSpec((t, D), row_map)],
        out_shape=[SDS((n, D), F32), SDS((n, D), BF16)],
        scratch_shapes=[pltpu.VMEM((t + 8, D), F32), pltpu.VMEM((t, D), F32), pltpu.VMEM((t, D), F32),
                        pltpu.VMEM((8, D), F32)],
        compiler_params=_params(48),
    )(z, z, cw8, cb, wx, wa, bx, ba, lam)


def _lru_bwd(z, h_all, dya, cw8, cb, wx, wa, bx, ba, lam, nb, s_len):
    n = nb * s_len
    t = LRU_T
    ns = s_len // t
    t8 = t // 8

    def body(xp_ref, xph_ref, ga_ref, h_ref, hh_ref, dya_ref, cw_ref, cb_ref, wx_ref, wa_ref, bx_ref, ba_ref,
             lam_ref, dz_ref, gcw_ref, gcb_ref, gwx_ref, gwa_ref, gbx_ref, gba_ref, glam_ref,
             ext, hext, dext, a_s, u_s, dh_s, carry):
        b, s = pl.program_id(0), pl.program_id(1)
        first_tile = s == ns - 1

        @pl.when((b == 0) & (s == 0))
        def _():
            for ref in (gcw_ref, gcb_ref, gwx_ref, gwa_ref, gbx_ref, gba_ref, glam_ref):
                ref[...] = jnp.zeros(ref.shape, F32)

        @pl.when(s == 0)
        def _():
            dext[t:t + 8, :] = jnp.zeros((8, D), F32)
            carry[...] = jnp.zeros((8, D), F32)

        keep = jnp.where(first_tile, 0.0, 1.0)
        xp = xp_ref[0]
        ext[0:8, :] = xph_ref[0] * keep
        ext[8:8 + t, :] = xp
        hext[0:8, :] = hh_ref[...] * keep
        hext[8:8 + t, :] = h_ref[...]
        cw = cw_ref[...]
        lam = lam_ref[...]
        taps = _shifted(_groups(ext[...]), (3, 2, 1)) + [xp]
        xa = _conv(taps, cw, cb_ref[...])
        xab, gi, gr, sp, a, mult = _lru_gates(xa, wx_ref, wa_ref, bx_ref[...], ba_ref[...], lam)
        (h_prev,) = _shifted(_groups(hext[...]), (1,))
        ga = ga_ref[0]
        sg = _sigmoid(ga)
        dya_v = dya_ref[...]
        d_ga = dya_v * h_ref[...] * (sg * (1.0 + ga * (1.0 - sg)))
        g_in = dya_v * (ga * sg)

        (an,) = _shifted(jnp.concatenate([_groups(a), jnp.ones((1, 8, D), F32)], axis=0), (-1,))
        an, u = _groups(an), _groups(g_in)
        row = lax.broadcasted_iota(jnp.int32, an.shape, 1)
        for sh in (1, 2, 4):
            a_sh = pltpu.roll(an, 8 - sh, 1)
            u_sh = pltpu.roll(u, 8 - sh, 1)
            m = row < 8 - sh
            u = jnp.where(m, u + an * u_sh, u)
            an = jnp.where(m, an * a_sh, an)
        a_s[...] = an.reshape(t, D)
        u_s[...] = u.reshape(t, D)

        def step(i, c):
            r = pl.multiple_of((t8 - 1 - i) * 8, 8)
            dg = u_s[pl.ds(r, 8), :] + a_s[pl.ds(r, 8), :] * c
            dh_s[pl.ds(r, 8), :] = dg
            return dg[0:1, :]

        lax.fori_loop(0, t8, step, carry[0:1, :], unroll=4)
        dh = dh_s[...]
        carry[0:1, :] = a[0:1, :] * dh[0:1, :]

        d_a = dh * h_prev
        dux = dh * xa
        d_mult = dux * gi
        d_gi = dux * mult
        d_xa = dh * (mult * gi)
        d_loga = d_a * a - d_mult * ((a * a) / mult)
        d_gr = d_loga * (-LRU_C * sp)
        d_sp = jnp.sum(d_loga * (-LRU_C * gr), axis=0, keepdims=True)
        glam_ref[...] += d_sp * (-_sigmoid(-lam))
        d_pi = d_gi * gi * (1.0 - gi)
        d_pr = d_gr * gr * (1.0 - gr)
        gbx_ref[...] += jnp.sum(d_pi, axis=0, keepdims=True)
        gba_ref[...] += jnp.sum(d_pr, axis=0, keepdims=True)
        dpib = d_pi.astype(BF16)
        dprb = d_pr.astype(BF16)
        back = []
        for h in range(NB):
            cs = slice(h * BD, (h + 1) * BD)
            gwx_ref[h] += lax.dot_general(xab[:, cs], dpib[:, cs], TN_DIMS, preferred_element_type=F32)
            gwa_ref[h] += lax.dot_general(xab[:, cs], dprb[:, cs], TN_DIMS, preferred_element_type=F32)
            back.append(lax.dot_general(dpib[:, cs], wx_ref[h], NT_DIMS, preferred_element_type=F32)
                        + lax.dot_general(dprb[:, cs], wa_ref[h], NT_DIMS, preferred_element_type=F32))
        d_xa = d_xa + jnp.concatenate(back, axis=1)

        dext[0:t, :] = d_xa
        later = _shifted(_groups(dext[...]), (-3, -2, -1))
        d_xp = later[0] * cw[0:1, :] + later[1] * cw[1:2, :]
        d_xp = d_xp + later[2] * cw[2:3, :]
        d_xp = d_xp + d_xa * cw[3:4, :]
        dext[t:t + 8, :] = d_xa[0:8, :]
        gcb_ref[...] += jnp.sum(d_xa, axis=0, keepdims=True)
        for k in range(4):
            gcw_ref[k:k + 1, :] += jnp.sum(d_xa * taps[k], axis=0, keepdims=True)
        dz_ref[0] = d_xp.astype(BF16)
        dz_ref[1] = d_ga.astype(BF16)

    rb = lambda b, s: b * ns + (ns - 1 - s)
    halo = lambda b, s: jnp.maximum(rb(b, s) * t8 - 1, 0)
    rep2 = lambda b, s: (0, 0)
    rep3 = lambda b, s: (0, 0, 0)
    return pl.pallas_call(
        body, name="lru_bwd", grid=(nb, ns),
        in_specs=[pl.BlockSpec((1, t, D), lambda b, s: (0, rb(b, s), 0)),
                  pl.BlockSpec((1, 8, D), lambda b, s: (0, halo(b, s), 0)),
                  pl.BlockSpec((1, t, D), lambda b, s: (1, rb(b, s), 0)),
                  pl.BlockSpec((t, D), lambda b, s: (rb(b, s), 0)),
                  pl.BlockSpec((8, D), lambda b, s: (halo(b, s), 0)),
                  pl.BlockSpec((t, D), lambda b, s: (rb(b, s), 0)),
                  pl.BlockSpec((8, D), rep2), pl.BlockSpec((1, D), rep2),
                  pl.BlockSpec((NB, BD, BD), rep3), pl.BlockSpec((NB, BD, BD), rep3),
                  pl.BlockSpec((1, D), rep2), pl.BlockSpec((1, D), rep2), pl.BlockSpec((1, D), rep2)],
        out_specs=[pl.BlockSpec((2, t, D), lambda b, s: (0, rb(b, s), 0)),
                   pl.BlockSpec((8, D), rep2), pl.BlockSpec((1, D), rep2),
                   pl.BlockSpec((NB, BD, BD), rep3), pl.BlockSpec((NB, BD, BD), rep3),
                   pl.BlockSpec((1, D), rep2), pl.BlockSpec((1, D), rep2), pl.BlockSpec((1, D), rep2)],
        out_shape=[SDS((2, n, D), BF16), SDS((8, D), F32), SDS((1, D), F32),
                   SDS((NB, BD, BD), F32), SDS((NB, BD, BD), F32),
                   SDS((1, D), F32), SDS((1, D), F32), SDS((1, D), F32)],
        scratch_shapes=[pltpu.VMEM((t + 8, D), F32), pltpu.VMEM((t + 8, D), F32), pltpu.VMEM((t + 8, D), F32),
                        pltpu.VMEM((t, D), F32), pltpu.VMEM((t, D), F32), pltpu.VMEM((t, D), F32),
                        pltpu.VMEM((8, D), F32)],
        compiler_params=_params(56),
    )(z, z, z, h_all, h_all, dya, cw8, cb, wx, wa, bx, ba, lam)


HG_T = 512
HG_NC = HG_T // CHUNK
BNT_DIMS = (((2,), (2,)), ((0,), (0,)))
BNN_DIMS = (((2,), (1,)), ((0,), (0,)))
BTN_DIMS = (((1,), (1,)), ((0,), (0,)))


def _lower_bound(lg):
    m = jnp.max(lg, axis=0, keepdims=True)
    e = jnp.exp(lg - m)
    return e[0:1, :] / jnp.sum(e, axis=0, keepdims=True)


def _tri(upper):
    r = lax.broadcasted_iota(jnp.int32, (HG_NC, CHUNK, CHUNK), 1)
    c = lax.broadcasted_iota(jnp.int32, (HG_NC, CHUNK, CHUNK), 2)
    return (c >= r) if upper else (r >= c)


def _bdot(a, b, dims):
    return lax.dot_general(a, b, dims, preferred_element_type=F32)


def _tri_sums(upper, a):
    tri = _tri(upper).astype(BF16)
    a1 = a.astype(BF16)
    r1 = a - a1.astype(F32)
    a2 = r1.astype(BF16)
    a3 = (r1 - a2.astype(F32)).astype(BF16)
    return _bdot(tri, a1, BNN_DIMS) + (_bdot(tri, a2, BNN_DIMS) + _bdot(tri, a3, BNN_DIMS))


def _chunks(a):
    return a.reshape(HG_NC, CHUNK, BD)


def _hg_tile(q, fp, lb):
    q, fp = _chunks(q), _chunks(fp)
    sig = _sigmoid(fp)
    f = lb + (1.0 - lb) * sig
    log_f = jnp.log(f)
    k = 1.0 - f
    b = _tri_sums(False, log_f)
    b_mid = b[:, CHUNK // 2:CHUNK // 2 + 1, :]
    b_last = b[:, CHUNK - 1:CHUNK, :]
    sq = _sigmoid(q)
    qh = q * sq
    e_qi = jnp.exp(b - b_mid)
    e_ki = jnp.exp(b_mid - b)
    e_qs = jnp.exp(b)
    e_ks = jnp.exp(b_last - b)
    dc = jnp.exp(b_last)
    q_in = (qh * e_qi) * HG_SCALE
    k_in = k * e_ki
    q_st = (qh * e_qs) * HG_SCALE
    k_st = k * e_ks
    att = _bdot(q_in.astype(BF16), k_in.astype(BF16), BNT_DIMS)
    att = jnp.where(_tri(False), att, 0.0)
    return dict(q=q, sig=sig, f=f, k=k, sq=sq, e_qi=e_qi, e_ki=e_ki, e_qs=e_qs, e_ks=e_ks, dc=dc,
                q_in=q_in, k_in=k_in, q_st=q_st, k_st=k_st, att=att)


def _hgrn_fwd(z, lb_logits, hg_g, nb, s_len):
    n = nb * s_len
    t = HG_T
    ns = s_len // t
    nchunk = s_len // CHUNK

    def body(q_ref, f_ref, v_ref, gb_ref, lg_ref, g_ref, o_ref, yb_ref, st_ref, st):
        @pl.when(pl.program_id(1) == 0)
        def _():
            st[...] = jnp.zeros((NB, BD, BD), F32)

        def head(h, carry):
            cols = pl.ds(pl.multiple_of(h * BD, BD), BD)
            lb = _lower_bound(lg_ref[:, cols])
            ck = _hg_tile(q_ref[0, :, cols], f_ref[0, :, cols], lb)
            vb = _chunks(v_ref[0, :, cols]).astype(BF16)
            kv = _bdot(vb, ck["k_st"].astype(BF16), BTN_DIMS)
            states = [st[h]]
            for c in range(HG_NC):
                states.append(states[c] * ck["dc"][c] + kv[c])
            st[h] = states[HG_NC]
            s_in = jnp.stack(states[:HG_NC], axis=0)
            st_ref[h] = s_in
            o = (_bdot(ck["att"].astype(BF16), vb, BNN_DIMS)
                 + _bdot(ck["q_st"].astype(BF16), s_in.astype(BF16), BNT_DIMS))
            o_ref[:, cols] = o.reshape(t, BD)
            r = lax.rsqrt(jnp.mean(o * o, axis=-1, keepdims=True) + EPS)
            gb = _chunks(gb_ref[0, :, cols])
            yb_ref[:, cols] = (((o * r) * g_ref[...]) * (gb * _sigmoid(gb))).astype(BF16).reshape(t, BD)
            return carry

        lax.fori_loop(0, NB, head, 0, unroll=4)

    seg = lambda j: pl.BlockSpec((1, t, D), lambda b, s: (j, b * ns + s, 0))
    tile = pl.BlockSpec((t, D), lambda b, s: (b * ns + s, 0))
    return pl.pallas_call(
        body, name="hgrn_fwd", grid=(nb, ns),
        in_specs=[seg(2), seg(3), seg(4), seg(5),
                  pl.BlockSpec((2, D), lambda b, s: (0, 0)), pl.BlockSpec((1, BD), lambda b, s: (0, 0))],
        out_specs=[tile, tile, pl.BlockSpec((NB, HG_NC, BD, BD), lambda b, s: (b, s, 0, 0))],
        out_shape=[SDS((n, D), F32), SDS((n, D), BF16), SDS((nb * NB, nchunk, BD, BD), F32)],
        scratch_shapes=[pltpu.VMEM((NB, BD, BD), F32)],
        compiler_params=_params(56),
    )(z, z, z, z, lb_logits, hg_g)


def _hgrn_bwd(z, o_all, st_all, dyb, lb_logits, hg_g, nb, s_len):
    n = nb * s_len
    t = HG_T
    ns = s_len // t

    def body(q_ref, f_ref, v_ref, gb_ref, o_ref, st_ref, dyb_ref, lg_ref, g_ref,
             dz_ref, glg_ref, ghg_ref, dst, dlb):
        b, s = pl.program_id(0), pl.program_id(1)

        @pl.when((b == 0) & (s == 0))
        def _():
            ghg_ref[...] = jnp.zeros((1, BD), F32)
            dlb[...] = jnp.zeros((8, D), F32)

        @pl.when(s == 0)
        def _():
            dst[...] = jnp.zeros((NB, BD, BD), F32)

        g = g_ref[...]

        def head(h, carry):
            cols = pl.ds(pl.multiple_of(h * BD, BD), BD)
            lb = _lower_bound(lg_ref[:, cols])
            ck = _hg_tile(q_ref[0, :, cols], f_ref[0, :, cols], lb)
            q = ck["q"]
            vb = _chunks(v_ref[0, :, cols]).astype(BF16)
            gb = _chunks(gb_ref[0, :, cols])
            o = _chunks(o_ref[:, cols])
            dyb_v = _chunks(dyb_ref[:, cols])
            s_in = st_ref[h]

            sgb = _sigmoid(gb)
            r = lax.rsqrt(jnp.mean(o * o, axis=-1, keepdims=True) + EPS)
            ohat = o * r
            d_on = dyb_v * (gb * sgb)
            d_gb = dyb_v * (ohat * g) * (sgb * (1.0 + gb * (1.0 - sgb)))
            ghg_ref[...] += jnp.sum(jnp.sum(d_on * ohat, axis=1), axis=0, keepdims=True)
            tt = d_on * g
            d_o = r * (tt - ohat * jnp.mean(tt * ohat, axis=-1, keepdims=True))
            dob = d_o.astype(BF16)

            attb = ck["att"].astype(BF16)
            q_inb, k_inb = ck["q_in"].astype(BF16), ck["k_in"].astype(BF16)
            q_stb, k_stb = ck["q_st"].astype(BF16), ck["k_st"].astype(BF16)
            d_att = jnp.where(_tri(False), _bdot(dob, vb, BNT_DIMS), 0.0).astype(BF16)
            d_q_in = _bdot(d_att, k_inb, BNN_DIMS)
            d_k_in = _bdot(d_att, q_inb, BTN_DIMS)
            d_q_st = _bdot(dob, s_in.astype(BF16), BNN_DIMS)
            qdo = _bdot(dob, q_stb, BTN_DIMS)
            d_states = [None] * HG_NC + [dst[h]]
            for c in reversed(range(HG_NC)):
                d_states[c] = d_states[c + 1] * ck["dc"][c] + qdo[c]
            dst[h] = d_states[0]
            ds_out = jnp.stack(d_states[1:], axis=0)
            dsb = ds_out.astype(BF16)
            d_v = _bdot(attb, dob, BTN_DIMS) + _bdot(k_stb, dsb, BNT_DIMS)
            d_k_st = _bdot(vb, dsb, BNN_DIMS)
            d_dc = jnp.sum(ds_out * s_in, axis=1, keepdims=True)

            p_qi = d_q_in * ck["q_in"]
            p_ki = d_k_in * ck["k_in"]
            p_qs = d_q_st * ck["q_st"]
            p_ks = d_k_st * ck["k_st"]
            d_qh = (d_q_in * ck["e_qi"] + d_q_st * ck["e_qs"]) * HG_SCALE
            d_k = d_k_in * ck["e_ki"] + d_k_st * ck["e_ks"]
            d_b = (p_qi - p_ki) + (p_qs - p_ks)
            d_b_mid = jnp.sum(p_ki - p_qi, axis=1, keepdims=True)
            d_b_last = jnp.sum(p_ks, axis=1, keepdims=True) + d_dc * ck["dc"]
            rowi = lax.broadcasted_iota(jnp.int32, (HG_NC, CHUNK, BD), 1)
            d_b = d_b + jnp.where(rowi == CHUNK // 2, d_b_mid, 0.0) + jnp.where(rowi == CHUNK - 1, d_b_last, 0.0)
            d_logf = _tri_sums(True, d_b)
            d_f = d_logf / ck["f"] - d_k
            sig, sq = ck["sig"], ck["sq"]
            d_fp = d_f * (1.0 - lb) * (sig * (1.0 - sig))
            dlb[0:1, cols] += jnp.sum(jnp.sum(d_f * (1.0 - sig), axis=1), axis=0, keepdims=True)
            d_q = d_qh * (sq * (1.0 + q * (1.0 - sq)))
            dz_ref[0, :, cols] = d_q.astype(BF16).reshape(t, BD)
            dz_ref[1, :, cols] = d_fp.astype(BF16).reshape(t, BD)
            dz_ref[2, :, cols] = d_v.astype(BF16).reshape(t, BD)
            dz_ref[3, :, cols] = d_gb.astype(BF16).reshape(t, BD)
            return carry

        lax.fori_loop(0, NB, head, 0, unroll=2)

        @pl.when((b == nb - 1) & (s == ns - 1))
        def _():
            lb = _lower_bound(lg_ref[...])
            dl = dlb[0:1, :] * (lb * (1.0 - lb))
            glg_ref[0:1, :] = dl
            glg_ref[1:2, :] = -dl

    rb = lambda b, s: b * ns + (ns - 1 - s)
    seg = lambda j: pl.BlockSpec((1, t, D), lambda b, s: (j, rb(b, s), 0))
    tile = pl.BlockSpec((t, D), lambda b, s: (rb(b, s), 0))
    return pl.pallas_call(
        body, name="hgrn_bwd", grid=(nb, ns),
        in_specs=[seg(2), seg(3), seg(4), seg(5), tile,
                  pl.BlockSpec((NB, HG_NC, BD, BD), lambda b, s: (b, ns - 1 - s, 0, 0)),
                  tile, pl.BlockSpec((2, D), lambda b, s: (0, 0)), pl.BlockSpec((1, BD), lambda b, s: (0, 0))],
        out_specs=[pl.BlockSpec((4, t, D), lambda b, s: (0, rb(b, s), 0)),
                   pl.BlockSpec((2, D), lambda b, s: (0, 0)), pl.BlockSpec((1, BD), lambda b, s: (0, 0))],
        out_shape=[SDS((4, n, D), BF16), SDS((2, D), F32), SDS((1, BD), F32)],
        scratch_shapes=[pltpu.VMEM((NB, BD, BD), F32), pltpu.VMEM((8, D), F32)],
        compiler_params=_params(60),
    )(z, z, z, z, o_all, st_all, dyb, lb_logits, hg_g)


def _mid(ya, yb, z, b_merge, x2, tgt, fin_g, pa, pb, wo):
    n = x2.shape[0]
    tm = 256
    ni = n // tm

    def body(ya_ref, yb_ref, gma_ref, gmb_ref, bm_ref, x_ref, t_ref, fg_ref, pa_hbm, pb_hbm, wo_hbm,
             dx2_ref, dya_ref, dyb_ref, dgm_ref, loss_ref, gfg_ref, gbm_ref, gm_hbm,
             pa_v, pb_v, wo_v, gpa_v, gpb_v, gwo_v, sem):
        i = pl.program_id(0)
        by_owner = lambda g: g.reshape(NB, BD, D)
        loads = [pltpu.make_async_copy(src, dst, sem.at[k])
                 for k, (src, dst) in enumerate(((pa_hbm, pa_v), (pb_hbm, pb_v), (wo_hbm, wo_v)))]
        stores = [pltpu.make_async_copy(src, dst, sem.at[k])
                  for k, (src, dst) in enumerate((g, gm_hbm.at[:, pl.ds(slot * BD, BD), :])
                                                 for slot, g in enumerate((gpa_v, gpb_v, gwo_v)))]

        @pl.when(i == 0)
        def _():
            for cp in loads:
                cp.start()
            for ref in (gpa_v, gpb_v, gwo_v, loss_ref, gfg_ref, gbm_ref):
                ref[...] = jnp.zeros(ref.shape, F32)
            for cp in loads:
                cp.wait()

        ya_v = ya_ref[...]
        yb_v = yb_ref[...]
        out_a = jnp.dot(ya_v, pa_v[...], preferred_element_type=F32)
        out_b = jnp.dot(yb_v, pb_v[...], preferred_element_type=F32)
        bm = bm_ref[...]
        g_a = _sigmoid(gma_ref[0] + bm[:, 0:D])
        g_b = _sigmoid(gmb_ref[0] + bm[:, D:2 * D])
        mixed = g_a * out_a + g_b * out_b
        mixb = mixed.astype(BF16)
        xo = x_ref[...] + jnp.dot(mixb, wo_v[...], preferred_element_type=F32)
        r = lax.rsqrt(jnp.mean(xo * xo, axis=-1, keepdims=True) + EPS)
        xn = xo * r
        fg = fg_ref[...]
        e = xn * fg - t_ref[...]
        loss_ref[...] += 0.5 * jnp.sum(jnp.mean(e * e, axis=-1, keepdims=True))
        dy = e * (1.0 / D)
        gfg_ref[...] += jnp.sum(dy * xn, axis=0, keepdims=True)
        dxn = dy * fg
        dx2 = r * (dxn - xn * jnp.mean(dxn * xn, axis=-1, keepdims=True))
        dx2_ref[...] = dx2
        dx2b = dx2.astype(BF16)
        d_mixed = lax.dot_general(dx2b, wo_v[...], NT_DIMS, preferred_element_type=F32)
        gwo_v[...] += by_owner(lax.dot_general(mixb, dx2b, TN_DIMS, preferred_element_type=F32))
        d_oa = (d_mixed * g_a).astype(BF16)
        d_ob = (d_mixed * g_b).astype(BF16)
        dgm_a = (d_mixed * out_a) * (g_a * (1.0 - g_a))
        dgm_b = (d_mixed * out_b) * (g_b * (1.0 - g_b))
        gbm_ref[:, 0:D] += jnp.sum(dgm_a, axis=0, keepdims=True)
        gbm_ref[:, D:2 * D] += jnp.sum(dgm_b, axis=0, keepdims=True)
        dgm_ref[0] = dgm_a.astype(BF16)
        dgm_ref[1] = dgm_b.astype(BF16)
        dya_ref[...] = lax.dot_general(d_oa, pa_v[...], NT_DIMS, preferred_element_type=F32)
        dyb_ref[...] = lax.dot_general(d_ob, pb_v[...], NT_DIMS, preferred_element_type=F32)
        gpa_v[...] += by_owner(lax.dot_general(ya_v, d_oa, TN_DIMS, preferred_element_type=F32))
        gpb_v[...] += by_owner(lax.dot_general(yb_v, d_ob, TN_DIMS, preferred_element_type=F32))

        @pl.when(i == ni - 1)
        def _():
            for cp in stores:
                cp.start()
            for cp in stores:
                cp.wait()

    rows = pl.BlockSpec((tm, D), lambda i: (i, 0))
    rep = lambda shape: pl.BlockSpec(shape, lambda i: (0,) * len(shape))
    return pl.pallas_call(
        body, name="mid", grid=(ni,),
        in_specs=[rows, rows,
                  pl.BlockSpec((1, tm, D), lambda i: (6, i, 0)), pl.BlockSpec((1, tm, D), lambda i: (7, i, 0)),
                  rep((1, 2 * D)), rows, rows, rep((1, D)), ANY, ANY, ANY],
        out_specs=[rows, rows, rows, pl.BlockSpec((2, tm, D), lambda i: (0, i, 0)),
                   rep((8, BD)), rep((1, D)), rep((1, 2 * D)), ANY],
        out_shape=[SDS((n, D), F32), SDS((n, D), F32), SDS((n, D), F32), SDS((2, n, D), BF16),
                   SDS((8, BD), F32), SDS((1, D), F32), SDS((1, 2 * D), F32),
                   SDS((NB, MID_ROWS, D), F32)],
        scratch_shapes=[pltpu.VMEM((D, D), BF16)] * 3 + [pltpu.VMEM((NB, BD, D), F32)] * 3 + [pltpu.SemaphoreType.DMA((3,))],
        compiler_params=_params(60),
    )(ya, yb, z, z, b_merge, x2, tgt, fin_g, pa, pb, wo)


def _dz_specs(tm, ni, row_major):
    if row_major:
        ia = lambda i, j: (jnp.minimum(j, 1), i, 0)
        ib = lambda i, j: (jnp.clip(j - 2, 0, 3), i, 0)
        im = lambda i, j: (jnp.clip(j - 6, 0, 1), i, 0)
    else:
        last = ni - 1
        ia = lambda j, i: (jnp.minimum(j, 1), jnp.where(j < 2, i, last), 0)
        ib = lambda j, i: (jnp.clip(j - 2, 0, 3), jnp.where(j < 2, 0, jnp.where(j < 6, i, last)), 0)
        im = lambda j, i: (jnp.clip(j - 6, 0, 1), jnp.where(j < 6, 0, i), 0)
    return [pl.BlockSpec((1, tm, D), f) for f in (ia, ib, im)]


def _inproj_bwd_x(dza, dzb, dzm, w_all, x2, dx2, norm_g, after):
    n = x2.shape[0]
    tm = 512
    ni = n // tm

    def body(dza_ref, dzb_ref, dzm_ref, w_ref, x_ref, dx2_ref, g_ref, after_ref, gx_hbm, gg_ref, acc, stage, sems):
        j, i = pl.program_id(0), pl.program_id(1)
        rows = pl.ds(pl.multiple_of(i * tm, tm), tm)

        @pl.when((i == 0) & (j == 0))
        def _():
            gg_ref[...] = jnp.zeros((1, D), F32)

        @pl.when(j == 0)
        def _():
            acc[rows, :] = jnp.zeros((tm, D), F32)

        def add(ref):
            acc[rows, :] += lax.dot_general(ref[0], w_ref[0], NT_DIMS, preferred_element_type=F32)

        pl.when(j < 2)(lambda: add(dza_ref))
        pl.when((j >= 2) & (j < 6))(lambda: add(dzb_ref))
        pl.when(j >= 6)(lambda: add(dzm_ref))

        def store(tile):
            return pltpu.make_async_copy(stage.at[tile % 2], gx_hbm.at[pl.ds(pl.multiple_of(tile * tm, tm), tm), :],
                                         sems.at[tile % 2])

        @pl.when(j == NB - 1)
        def _():
            x = x_ref[...]
            r = lax.rsqrt(jnp.mean(x * x, axis=-1, keepdims=True) + EPS)
            xn = x * r
            dh = acc[rows, :]
            gg_ref[...] += jnp.sum(dh * xn, axis=0, keepdims=True)
            dxn = dh * g_ref[...]
            pl.when(i >= 2)(lambda: store(i - 2).wait())
            stage[i % 2] = dx2_ref[...] + r * (dxn - xn * jnp.mean(dxn * xn, axis=-1, keepdims=True))
            store(i).start()

            @pl.when(i == ni - 1)
            def _():
                for tile in range(max(ni - 2, 0), ni):
                    store(tile).wait()

    last = lambda j, i: (jnp.where(j == NB - 1, i, 0), 0)
    return pl.pallas_call(
        body, name="inproj_bwd_x", grid=(NB, ni),
        in_specs=_dz_specs(tm, ni, False) + [pl.BlockSpec((1, D, D), lambda j, i: (j, 0, 0)),
                                              pl.BlockSpec((tm, D), last), pl.BlockSpec((tm, D), last),
                                              pl.BlockSpec((1, D), lambda j, i: (0, 0)), ANY],
        out_specs=[ANY, pl.BlockSpec((1, D), lambda j, i: (0, 0))],
        out_shape=[SDS((n, D), F32), SDS((1, D), F32)],
        scratch_shapes=[pltpu.VMEM((n, D), F32), pltpu.VMEM((2, tm, D), F32), pltpu.SemaphoreType.DMA((2,))],
        compiler_params=_params(56),
    )(dza, dzb, dzm, w_all, x2, dx2, norm_g, after)


def _inproj_bwd_w(dza, dzb, dzm, h_all, g_m):
    n = h_all.shape[0]
    tm = min(n, 2048)
    ni = n // tm

    def body(dza_ref, dzb_ref, dzm_ref, h_ref, gm_hbm, gw_ref, got_w, got_m, stage, send_sems, recv_sems):
        j, i = pl.program_id(0), pl.program_id(1)
        x, y, c = _place()
        sibling = (x, y, 1 - c)

        def send_w(q):
            return pltpu.make_async_remote_copy(
                src_ref=stage.at[q % 2], dst_ref=got_w.at[q], send_sem=send_sems.at[q], recv_sem=recv_sems.at[q],
                device_id=sibling, device_id_type=MESH)

        def send_m(q):
            return pltpu.make_async_remote_copy(
                src_ref=gm_hbm.at[2 * q + (1 - c)], dst_ref=got_m.at[q], send_sem=send_sems.at[4 + q],
                recv_sem=recv_sems.at[4 + q], device_id=sibling, device_id_type=MESH)

        @pl.when((j == 0) & (i == 0))
        def _():
            for q in range(4):
                send_m(q).start()

        @pl.when(i == 0)
        def _():
            gw_ref[...] = jnp.zeros((1, D, D), F32)

        def add(ref):
            gw_ref[0] += lax.dot_general(h_ref[...], ref[0], TN_DIMS, preferred_element_type=F32)

        pl.when(j < 2)(lambda: add(dza_ref))
        pl.when((j >= 2) & (j < 6))(lambda: add(dzb_ref))
        pl.when(j >= 6)(lambda: add(dzm_ref))

        for q in range(4):
            @pl.when((i == ni - 1) & (j == 2 * q + 1 - c))
            def _(q=q):
                if q >= 2:
                    send_w(q - 2).wait_send()
                stage[q % 2] = gw_ref[0].astype(BF16)
                send_w(q).start()

        @pl.when((j == NB - 1) & (i == ni - 1))
        def _():
            for q in (2, 3):
                send_w(q).wait_send()
            for q in range(4):
                send_w(q).wait_recv()
                send_m(q).wait_send()
                send_m(q).wait_recv()

    return pl.pallas_call(
        body, name="inproj_bwd_w", grid=(NB, ni),
        in_specs=_dz_specs(tm, ni, False) + [pl.BlockSpec((tm, D), lambda j, i: (i, 0)), ANY],
        out_specs=[pl.BlockSpec((1, D, D), lambda j, i: (j, 0, 0)), ANY, ANY],
        out_shape=[SDS((NB, D, D), F32), SDS((4, D, D), BF16), SDS((4,) + g_m.shape[1:], F32)],
        scratch_shapes=[pltpu.VMEM((2, D, D), BF16), pltpu.SemaphoreType.DMA((8,)), pltpu.SemaphoreType.DMA((8,))],
        compiler_params=_params(58),
    )(dza, dzb, dzm, h_all, g_m)


def _adamw(w, g, m, v):
    rows, cols = w.shape
    tr = _row_tile(rows)

    spec = pl.BlockSpec((tr, cols), lambda i: (i, 0))
    return pl.pallas_call(
        functools.partial(_adam_refs), name="adamw", grid=(rows // tr,), in_specs=[spec] * 4, out_specs=[spec] * 3,
        out_shape=[SDS((rows, cols), F32)] * 3, compiler_params=_params(32),
    )(w, g, m, v)


def _adam_refs(w_ref, g_ref, m_ref, v_ref, d_ref, nm_ref, nv_ref):
    gv = g_ref[...]
    nm = ADAM_B1 * m_ref[...] + (1.0 - ADAM_B1) * gv
    nv = ADAM_B2 * v_ref[...] + (1.0 - ADAM_B2) * (gv * gv)
    m_hat = nm / (1.0 - ADAM_B1 ** ADAM_STEP)
    v_hat = nv / (1.0 - ADAM_B2 ** ADAM_STEP)
    d_ref[...] = -ADAM_LR * (m_hat / (jnp.sqrt(v_hat) + ADAM_EPS) + ADAM_WD * w_ref[...])
    nm_ref[...] = nm
    nv_ref[...] = nv


def _adamw_small(ws, gs, ms, vs):
    k = len(ws)

    def body(*refs):
        ins, outs = refs[:4 * k], refs[4 * k:7 * k]
        vin, vout = refs[7 * k:11 * k], refs[11 * k:14 * k]
        load_sems, store_sems = refs[14 * k:]
        loads = [pltpu.make_async_copy(ins[i], vin[i], load_sems.at[i]) for i in range(4 * k)]
        for cp in loads:
            cp.start()
        for cp in loads:
            cp.wait()
        for i in range(k):
            _adam_refs(*[vin[part * k + i] for part in range(4)], *[vout[part * k + i] for part in range(3)])
        stores = [pltpu.make_async_copy(vout[i], outs[i], store_sems.at[i]) for i in range(3 * k)]
        for cp in stores:
            cp.start()
        for cp in stores:
            cp.wait()

    shapes = [SDS(w.shape, F32) for w in ws]
    vmem = [pltpu.VMEM(w.shape, F32) for w in ws]
    out = pl.pallas_call(
        body, name="adamw_small", out_shape=shapes * 3, in_specs=[HBM] * (4 * k), out_specs=[HBM] * (3 * k),
        scratch_shapes=vmem * 7 + [pltpu.SemaphoreType.DMA((4 * k,)), pltpu.SemaphoreType.DMA((3 * k,))],
        compiler_params=_params(32),
    )(*ws, *gs, *ms, *vs)
    return out[:k], out[k:2 * k], out[2 * k:]


def _gather_tail(tail):
    def body(tail_ref, out_ref, stage, send_sems, recv_sems, local_sem):
        x, y, c = _place()
        me, sibling = (x, y, c), (x, y, 1 - c)
        chips = _other_chips(x, y)

        def copy(k, block, to, src=None):
            return pltpu.make_async_remote_copy(
                src_ref=out_ref.at[_block_id(block)] if src is None else src, dst_ref=out_ref.at[_block_id(block)],
                send_sem=send_sems.at[k], recv_sem=recv_sems.at[k], device_id=to, device_id_type=MESH)

        stage[...] = tail_ref[...]
        mine = pltpu.make_async_copy(stage, out_ref.at[_block_id(me)], local_sem)
        mine.start()
        sends = [copy(0, me, sibling, src=stage)] + [copy(1 + slot, me, (*chip, c), src=stage)
                                                     for slot, chip in enumerate(chips)]
        for cp in sends:
            cp.start()
        for slot, chip in enumerate(chips):
            copy(1 + slot, (*chip, c), me).wait_recv()
            sends.append(copy(4 + slot, (*chip, c), sibling))
            sends[-1].start()
        copy(0, sibling, me).wait_recv()
        for slot, chip in enumerate(chips):
            copy(4 + slot, (*chip, 1 - c), me).wait_recv()
        for cp in sends:
            cp.wait_send()
        mine.wait()

    return pl.pallas_call(
        body, name="gather_small_grads", in_specs=[pl.BlockSpec(memory_space=pltpu.VMEM)], out_specs=ANY,
        out_shape=SDS((NB,) + tail.shape, F32),
        scratch_shapes=[pltpu.VMEM(tail.shape, F32), pltpu.SemaphoreType.DMA((7,)), pltpu.SemaphoreType.DMA((7,)),
                        pltpu.SemaphoreType.DMA(())],
    )(tail)


HBM = pl.BlockSpec(memory_space=pltpu.HBM)
SEMS = pl.BlockSpec(memory_space=pltpu.SEMAPHORE)
EFFECT = pltpu.SideEffectType.DATAFLOW_SIDE_EFFECTING


def _chip_copies(srcs, lands, send_sems, recv_sems):
    x, y, c = _place()
    return [pltpu.make_async_remote_copy(
        src_ref=srcs[a].at[slot], dst_ref=lands[a].at[slot],
        send_sem=send_sems.at[3 * a + slot], recv_sem=recv_sems.at[3 * a + slot],
        device_id=(px, py, c), device_id_type=MESH)
        for a in range(len(srcs)) for slot, (px, py) in enumerate(_other_chips(x, y))]


def _split_start(name, copies, per_array, srcs, lands, after=None):
    na = len(srcs)

    def body(*refs):
        send_sems, recv_sems = refs[-2 * na - 3], refs[-2 * na - 2]
        for cp in copies(refs[:na], refs[na:2 * na], send_sems, recv_sems):
            cp.start()
        refs[-1][...] = jnp.zeros_like(refs[-1])

    hbm = lambda a: pltpu.HBM(a.shape, a.dtype)
    pin = lambda a: pltpu.with_memory_space_constraint(a, pltpu.HBM)
    out = pl.pallas_call(
        body, name=name,
        out_shape=(pltpu.SemaphoreType.DMA((per_array * na,)), pltpu.SemaphoreType.DMA((per_array * na,)),
                   *[hbm(a) for a in srcs], *[hbm(a) for a in lands], SDS((8, BD), F32)),
        in_specs=[HBM] * (2 * na) + ([] if after is None else [ANY]),
        out_specs=(SEMS, SEMS, *[HBM] * (2 * na), pl.BlockSpec(memory_space=pltpu.VMEM)),
        input_output_aliases={i: 2 + i for i in range(2 * na)},
        compiler_params=pltpu.CompilerParams(has_side_effects=EFFECT),
    )(*[pin(a) for a in srcs], *[pin(a) for a in lands], *([] if after is None else [after]))
    return out[0], out[1], out[2:2 + na], out[2 + na:2 + 2 * na], out[-1]


def _split_wait(name, copies, started, after):
    send_sems, recv_sems, srcs, lands, _ = started
    na = len(srcs)

    def body(*refs):
        waits = copies(refs[:na], refs[na:2 * na], refs[2 * na], refs[2 * na + 1])
        for cp in waits:
            cp.wait_send()
        for cp in waits:
            cp.wait_recv()

    hbm = lambda a: pltpu.HBM(a.shape, a.dtype)
    out = pl.pallas_call(
        body, name=name,
        out_shape=(*[hbm(a) for a in srcs], *[hbm(a) for a in lands]),
        in_specs=[HBM] * (2 * na) + [SEMS, SEMS, ANY],
        out_specs=tuple([HBM] * (2 * na)),
        input_output_aliases={i: i for i in range(2 * na)},
        compiler_params=pltpu.CompilerParams(has_side_effects=EFFECT),
    )(*srcs, *lands, send_sems, recv_sems, after)
    return out[na:]


def _add_sibling(place, g, a_in):
    _, r, cols = g.shape
    tr = _row_tile(r)

    def chip(k, pr):
        qx = pr[0] if k in (1, 3) else 1 - pr[0]
        qy = pr[1] if k in (0, 3) else 1 - pr[1]
        return 2 * qx + qy

    def body(place_ref, *refs):
        g_refs, a_refs, (out_ref, own_ref) = refs[0:4], refs[4:8], refs[8:10]
        for k in range(3):
            out_ref[k] = (g_refs[k][0] + a_refs[k][0].astype(F32)).astype(BF16)
        own_ref[...] = g_refs[3][0] + a_refs[3][0].astype(F32)

    mine = lambda k: pl.BlockSpec((1, tr, cols), lambda i, pr: (2 * chip(k, pr) + pr[2], i, 0))
    theirs = lambda k: pl.BlockSpec((1, tr, cols), lambda i, pr: (chip(k, pr), i, 0))
    return pl.pallas_call(
        body, name="add_sibling",
        grid_spec=pltpu.PrefetchScalarGridSpec(
            num_scalar_prefetch=1, grid=(r // tr,),
            in_specs=[mine(k) for k in range(4)] + [theirs(k) for k in range(4)],
            out_specs=[pl.BlockSpec((3, tr, cols), lambda i, pr: (0, i, 0)),
                       pl.BlockSpec((tr, cols), lambda i, pr: (i, 0))]),
        out_shape=[SDS((3, r, cols), BF16), SDS((r, cols), F32)], compiler_params=_params(48),
    )(place, *[g] * 4, *[a_in] * 4)


def _add_chips(own, b_in):
    r, cols = own.shape
    tr = _row_tile(r)

    def body(p_ref, b0_ref, b1_ref, b2_ref, o_ref):
        o_ref[...] = ((p_ref[...] + b0_ref[0].astype(F32)) + b1_ref[0].astype(F32)) + b2_ref[0].astype(F32)

    slot = lambda k: pl.BlockSpec((1, tr, cols), lambda i: (k, i, 0))
    spec = pl.BlockSpec((tr, cols), lambda i: (i, 0))
    return pl.pallas_call(
        body, name="add_chips", grid=(r // tr,), in_specs=[spec, slot(0), slot(1), slot(2)], out_specs=spec,
        out_shape=SDS((r, cols), F32), compiler_params=_params(32),
    )(own, b_in, b_in, b_in)


VEC_NAMES = ("b_merge", "conv_b", "rg_bx", "rg_ba", "rg_lambda", "hg_lb_logits", "hg_norm_g", "final_norm_g")
REP_NAMES = ("rg_wx", "rg_wa", "norm_g") + VEC_NAMES
SMALL_AT = 3 * BD
SMALL_ROWS = 48
MID_ROWS = 448


def _sum_blocks(parts):
    def body(p_ref, o_ref):
        acc = p_ref[0]
        for k in range(1, NB):
            acc = acc + p_ref[k]
        o_ref[...] = acc

    return pl.pallas_call(body, name="sum_blocks", out_shape=SDS(parts.shape[1:], F32))(parts)


def _pack_rows(arrays, width, row_multiple=8):
    flat = jnp.concatenate([a.reshape(-1) for a in arrays])
    rows = -(-flat.shape[0] // width)
    rows = -(-rows // row_multiple) * row_multiple
    return jnp.pad(flat, (0, rows * width - flat.shape[0])).reshape(rows, width)


def _unpack(flat, like):
    out, off = [], 0
    for a in like:
        out.append(flat[off:off + a.size].reshape(a.shape))
        off += a.size
    return out


def kernel(x, w_in, b_merge, conv_w, conv_b, rg_wx, rg_bx, rg_wa, rg_ba, rg_lambda, hg_lb_logits, hg_norm_g, proj_a, proj_b, w_out, norm_g, final_norm_g, loss_target, m_w_in, m_b_merge, m_conv_w, m_conv_b, m_rg_wx, m_rg_bx, m_rg_wa, m_rg_ba, m_rg_lambda, m_hg_lb_logits, m_hg_norm_g, m_proj_a, m_proj_b, m_w_out, m_norm_g, m_final_norm_g, v_w_in, v_b_merge, v_conv_w, v_conv_b, v_rg_wx, v_rg_bx, v_rg_wa, v_rg_ba, v_rg_lambda, v_hg_lb_logits, v_hg_norm_g, v_proj_a, v_proj_b, v_w_out, v_norm_g, v_final_norm_g):
    weights = dict(w_in=w_in, b_merge=b_merge, conv_w=conv_w, conv_b=conv_b, rg_wx=rg_wx, rg_bx=rg_bx, rg_wa=rg_wa,
                   rg_ba=rg_ba, rg_lambda=rg_lambda, hg_lb_logits=hg_lb_logits, hg_norm_g=hg_norm_g, proj_a=proj_a,
                   proj_b=proj_b, w_out=w_out, norm_g=norm_g, final_norm_g=final_norm_g)
    mom1 = dict(w_in=m_w_in, b_merge=m_b_merge, conv_w=m_conv_w, conv_b=m_conv_b, rg_wx=m_rg_wx, rg_bx=m_rg_bx,
                rg_wa=m_rg_wa, rg_ba=m_rg_ba, rg_lambda=m_rg_lambda, hg_lb_logits=m_hg_lb_logits,
                hg_norm_g=m_hg_norm_g, proj_a=m_proj_a, proj_b=m_proj_b, w_out=m_w_out, norm_g=m_norm_g,
                final_norm_g=m_final_norm_g)
    mom2 = dict(w_in=v_w_in, b_merge=v_b_merge, conv_w=v_conv_w, conv_b=v_conv_b, rg_wx=v_rg_wx, rg_bx=v_rg_bx,
                rg_wa=v_rg_wa, rg_ba=v_rg_ba, rg_lambda=v_rg_lambda, hg_lb_logits=v_hg_lb_logits,
                hg_norm_g=v_hg_norm_g, proj_a=v_proj_a, proj_b=v_proj_b, w_out=v_w_out, norm_g=v_norm_g,
                final_norm_g=v_final_norm_g)
    order = list(weights)
    nb, s_len, _ = x.shape
    n = nb * s_len
    px, py, pc = _place()
    place = jnp.stack([px, py, pc]).astype(jnp.int32)

    in_hbm = lambda a: pltpu.with_memory_space_constraint(a, pltpu.HBM)
    norm_gain = in_hbm(norm_g)

    x2 = x.reshape(n, D)
    cw_blk = jnp.pad(conv_w[0], ((0, 4), (0, 0)))
    order_ids = jnp.stack([_block_id(p) for p in _arrival_order(px, py, pc)]).astype(jnp.int32)
    z, h_all, w_all, pa_all, pb_all, wo_all, cw_all = _gather_inproj(
        order_ids, x2, norm_gain, [w_in[0], proj_a[0], proj_b[0], w_out[0], cw_blk], [BF16, BF16, BF16, BF16, F32])
    pa_full, pb_full, wo_full = (a.reshape(D, D) for a in (pa_all, pb_all, wo_all))
    cw8 = in_hbm(cw_all.transpose(1, 0, 2).reshape(8, D))
    wx_b, wa_b = in_hbm(rg_wx[0].astype(BF16)), in_hbm(rg_wa[0].astype(BF16))
    cb, bx, ba, lam = (in_hbm(a.reshape(1, D)) for a in (conv_b, rg_bx, rg_ba, rg_lambda))
    fin_g, b_mrg = in_hbm(final_norm_g.reshape(1, D)), in_hbm(b_merge)
    lb_lg, hg_g = in_hbm(hg_lb_logits), in_hbm(hg_norm_g)

    hlru, ya = _lru_fwd(z, cw8, cb, wx_b, wa_b, bx, ba, lam, nb, s_len)
    o_all, yb, st_all = _hgrn_fwd(z, lb_lg, hg_g, nb, s_len)

    (dx2, dya, dyb, dzm, loss_acc, g_fin, g_bm, g_mid) = _mid(
        ya, yb, z, b_mrg, x2, loss_target.reshape(n, D), fin_g, pa_full, pb_full, wo_full)
    dzb, g_lg, g_hg = _hgrn_bwd(z, o_all, st_all, dyb, lb_lg, hg_g, nb, s_len)
    dza, g_cw8, g_cb, g_wx, g_wa, g_bx, g_ba, g_lam = _lru_bwd(
        z, hlru, dya, cw8, cb, wx_b, wa_b, bx, ba, lam, nb, s_len)

    part = dict(b_merge=g_bm, conv_b=g_cb, rg_bx=g_bx, rg_ba=g_ba, rg_lambda=g_lam, hg_lb_logits=g_lg,
                hg_norm_g=g_hg, final_norm_g=g_fin)
    vec = _pack_rows([part[k] for k in VEC_NAMES], BD)
    vec = jnp.pad(vec, ((0, 16 * NB - vec.shape[0]), (0, 0))).reshape(NB, 2, D)
    rows8 = lambda a: jnp.pad(a, ((0, 0), (0, 8 - a.shape[1]), (0, 0)))
    small = jnp.concatenate([g_wx.reshape(NB, 16, D), g_wa.reshape(NB, 16, D),
                             rows8(g_cw8.reshape(8, NB, BD).transpose(1, 0, 2).reshape(NB, 1, D)), rows8(vec),
                             jnp.zeros((NB, MID_ROWS - SMALL_AT - SMALL_ROWS, D), F32)], axis=1)
    g_m = lax.dynamic_update_slice(g_mid, small, (0, SMALL_AT, 0))
    g_w, w_from_sibling, m_from_sibling = _inproj_bwd_w(dza, dzb, dzm, h_all, g_m)
    w_out_bf, w_own = _add_sibling(place, g_w, w_from_sibling)
    m_out_bf, m_own = _add_sibling(place, g_m, m_from_sibling)
    outgoing = [w_out_bf, m_out_bf]
    chip_sums = _split_start("rs_chips_start", _chip_copies, 3, outgoing, [lax.empty(a.shape, a.dtype) for a in outgoing])
    grad_x, g_ng = _inproj_bwd_x(dza, dzb, dzm, w_all, x2, dx2, norm_gain, chip_sums[-1])
    from_chips = _split_wait("rs_chips_wait", _chip_copies, chip_sums, grad_x)
    r_w = _add_chips(w_own, from_chips[0])
    r_m = _add_chips(m_own, from_chips[1])
    row = lax.broadcasted_iota(jnp.int32, (8, D), 0)
    mine = jnp.where(row == 0, g_ng, jnp.where(row == 1, loss_acc[0:1, 0:1], 0.0))
    tail = jnp.concatenate([r_m[SMALL_AT:SMALL_AT + SMALL_ROWS], mine], axis=0)
    tail_all = _gather_tail(tail)
    summed = _sum_blocks(tail_all[:, SMALL_ROWS:SMALL_ROWS + 8])

    sharded = ("w_in", "proj_a", "proj_b", "w_out")
    own_grads = dict(w_in=r_w, proj_a=r_m[0:BD], proj_b=r_m[BD:2 * BD], w_out=r_m[2 * BD:3 * BD])
    grads = {k: own_grads[k].reshape(weights[k].shape) for k in sharded}
    grads.update(conv_w=r_m[SMALL_AT + 32].reshape(8, BD)[0:4].reshape(1, 4, BD),
                 rg_wx=tail_all[:, 0:16].reshape(1, NB, BD, BD), rg_wa=tail_all[:, 16:32].reshape(1, NB, BD, BD),
                 norm_g=summed[0:1])
    vec_all = tail_all[:, 40:42].reshape(-1)
    for k, gk in zip(VEC_NAMES, _unpack(vec_all, [weights[k] for k in VEC_NAMES])):
        grads[k] = gk

    delta, new_m, new_v = {}, {}, {}
    flat2 = lambda a: a.reshape(-1, a.shape[-1])
    for k in sharded:
        outs = _adamw(flat2(weights[k]), own_grads[k], flat2(mom1[k]), flat2(mom2[k]))
        delta[k], new_m[k], new_v[k] = (a.reshape(weights[k].shape) for a in outs)
    rep = list(REP_NAMES) + ["conv_w"]
    outs = _adamw_small(*[[flat2(t[k]) for k in rep] for t in (weights, grads, mom1, mom2)])
    for tgt, arrays in zip((delta, new_m, new_v), outs):
        for k, a in zip(rep, arrays):
            tgt[k] = a.reshape(weights[k].shape)

    return (summed[1, 0], grad_x.reshape(x.shape), *[grads[k] for k in order], *[delta[k] for k in order],
            *[new_m[k] for k in order], *[new_v[k] for k in order])
```

```python
import functools

import jax
import jax.numpy as jnp
from jax import lax
from jax.experimental import pallas as pl
from jax.experimental.pallas import tpu as pltpu

F32 = jnp.float32
BF16 = jnp.bfloat16
SDS = jax.ShapeDtypeStruct
MESH = pl.DeviceIdType.MESH
ANY = pl.BlockSpec(memory_space=pl.ANY)

D = 1024
NB = 8
BD = D // NB
CHUNK = 64
EPS = 1e-6
LRU_C = 8.0
HG_SCALE = BD ** -0.5
ADAM_LR, ADAM_B1, ADAM_B2, ADAM_EPS, ADAM_WD, ADAM_STEP = 0.001, 0.9, 0.999, 1e-08, 0.01, 10

NT_DIMS = (((1,), (1,)), ((), ()))
TN_DIMS = (((0,), (0,)), ((), ()))


def _params(vmem_mib):
    return pltpu.CompilerParams(vmem_limit_bytes=vmem_mib << 20)


def _row_tile(rows, most=256):
    assert rows % 8 == 0
    return max(t for t in range(8, min(rows, most) + 1, 8) if rows % t == 0)


def _sigmoid(v):
    return 0.5 * (jnp.tanh(0.5 * v) + 1.0)


def _groups(v):
    return v.reshape(v.shape[0] // 8, 8, v.shape[1])


def _softplus_neg(lam):
    t = -lam
    e = jnp.exp(-jnp.abs(t))
    w = 1.0 + e
    d = w - 1.0
    l1p = jnp.where(d == 0.0, e, jnp.log(w) * (e / jnp.where(d == 0.0, 1.0, d)))
    return jnp.maximum(t, 0.0) + l1p


def _place():
    return lax.axis_index("x"), lax.axis_index("y"), lax.axis_index("c")


def _other_chips(x, y):
    return [(1 - x, y), (x, 1 - y), (1 - x, 1 - y)]


def _block_id(p):
    return 4 * p[0] + 2 * p[1] + p[2]


def _core_chips(x, y, c):
    near, far, diag = _other_chips(x, y)
    pick = lambda a, b: (jnp.where(c == 0, a[0], b[0]), jnp.where(c == 0, a[1], b[1]))
    return [pick(near, far), pick(far, near), diag]


def _arrival_order(x, y, c):
    first, second, diag = _core_chips(x, y, c)
    return [(x, y, c), (x, y, 1 - c), (*first, c), (*second, 1 - c), (*second, c), (*first, 1 - c),
            (*diag, c), (*diag, 1 - c)]


def _gather_inproj(order_ids, x2, norm_g, blocks, dtypes):
    na = len(blocks)
    n = x2.shape[0]
    tm = min(n, 1024)
    ni = n // tm

    def body(order_ref, x_ref, g_ref, *refs):
        ins, (z_ref, h_ref), outs = refs[:na], refs[na:na + 2], refs[na + 2:2 * na + 2]
        stages = refs[2 * na + 2:3 * na + 2]
        h_full, wbuf, send_sems, recv_sems, local_sems, wsems, hsem = refs[3 * na + 2:]
        j, i = pl.program_id(0), pl.program_id(1)
        x, y, c = _place()
        me, sibling = (x, y, c), (x, y, 1 - c)
        chips = _core_chips(x, y, c)
        sibling_chips = [chips[1], chips[0], chips[2]]
        small = range(1, na)

        def copy(a, k, block, to, src=None):
            return pltpu.make_async_remote_copy(
                src_ref=outs[a].at[_block_id(block)] if src is None else src, dst_ref=outs[a].at[_block_id(block)],
                send_sem=send_sems.at[7 * a + k], recv_sem=recv_sems.at[7 * a + k],
                device_id=to, device_id_type=MESH)

        def local(a):
            return pltpu.make_async_copy(stages[a], outs[a].at[_block_id(me)], local_sems.at[a])

        def landed(a, slot):
            copy(a, 1 + slot, (*chips[slot], c), me).wait_recv()
            copy(a, 4 + slot, (*chips[slot], c), sibling).start()
            if slot == 0:
                copy(a, 3, (*chips[0], c), (*chips[1], c)).start()

        def second_and_small():
            landed(0, 1)
            for a in small:
                landed(a, 0)
                landed(a, 1)

        def diagonals():
            for a in range(na):
                landed(a, 2)

        def passed_on(a, slot):
            copy(a, 4 + slot, (*sibling_chips[slot], 1 - c), me).wait_recv()

        def sibling_here_send_second():
            copy(0, 0, sibling, me).wait_recv()
            for a in range(na):
                copy(a, 2, me, (*chips[1], c), src=stages[a]).start()

        @pl.when((j == 0) & (i == 0))
        def _():
            for a in range(na):
                stages[a][...] = ins[a][...].astype(dtypes[a])
                local(a).start()
            for a in range(na):
                copy(a, 0, me, sibling, src=stages[a]).start()
                copy(a, 1, me, (*chips[0], c), src=stages[a]).start()

        @pl.when(j == 0)
        def _():
            xv = x_ref[...]
            r = lax.rsqrt(jnp.mean(xv * xv, axis=-1, keepdims=True) + EPS)
            hb = ((xv * r) * g_ref[...]).astype(BF16)
            h_full[pl.ds(pl.multiple_of(i * tm, tm), tm), :] = hb

        save_h = pltpu.make_async_copy(h_full, h_ref, hsem)
        pl.when((j == 0) & (i == ni - 1))(save_h.start)

        steps = [
            lambda: local(0).wait(),
            sibling_here_send_second,
            lambda: landed(0, 0),
            lambda: passed_on(0, 0),
            second_and_small,
            lambda: passed_on(0, 1),
            diagonals,
            lambda: passed_on(0, 2),
        ]
        def w_load(k):
            return pltpu.make_async_copy(outs[0].at[order_ref[k]], wbuf.at[k % 2], wsems.at[k % 2])

        for k, step in enumerate(steps):
            @pl.when((j == 0) & (i == 0) if k == 0 else (j == k - 1) & (i == ni - 1))
            def _(k=k, step=step):
                step()
                w_load(k).start()

        pl.when(i == 0)(lambda: w_load(j).wait())
        z_ref[0] = jnp.dot(h_full[pl.ds(pl.multiple_of(i * tm, tm), tm), :], wbuf[j % 2], preferred_element_type=F32)

        @pl.when((j == NB - 1) & (i == ni - 1))
        def _():
            save_h.wait()
            for a in small:
                local(a).wait()
                copy(a, 0, sibling, me).wait_recv()
                for slot in range(3):
                    passed_on(a, slot)
            for a in range(na):
                copy(a, 0, me, sibling, src=stages[a]).wait_send()
                for slot, chip in enumerate(chips):
                    copy(a, 1 + slot, me, (*chip, c), src=stages[a]).wait_send()
                    copy(a, 4 + slot, (*chip, c), sibling).wait_send()

    rows_once = lambda j, i, order: (jnp.where(j == 0, i, ni - 1), 0)
    vmem = pl.BlockSpec(memory_space=pltpu.VMEM)
    return pl.pallas_call(
        body, name="gather_inproj",
        grid_spec=pltpu.PrefetchScalarGridSpec(
            num_scalar_prefetch=1, grid=(NB, ni),
            in_specs=[pl.BlockSpec((tm, D), rows_once), pl.BlockSpec((1, D), lambda j, i, order: (0, 0))] + [vmem] * na,
            out_specs=[pl.BlockSpec((1, tm, D), lambda j, i, order: (order[j], i, 0)), ANY] + [ANY] * na,
            scratch_shapes=[pltpu.VMEM(b.shape, dt) for b, dt in zip(blocks, dtypes)]
            + [pltpu.VMEM((n, D), BF16), pltpu.VMEM((2, D, D), BF16),
               pltpu.SemaphoreType.DMA((7 * na,)), pltpu.SemaphoreType.DMA((7 * na,)),
               pltpu.SemaphoreType.DMA((na,)), pltpu.SemaphoreType.DMA((2,)), pltpu.SemaphoreType.DMA(())]),
        out_shape=[SDS((NB, n, D), F32), SDS((n, D), BF16)] + [SDS((NB,) + b.shape, dt) for b, dt in zip(blocks, dtypes)],
        compiler_params=_params(56),
    )(order_ids, x2, norm_g, *blocks)


LRU_T = 256


def _shifted(groups, shifts):
    row = lax.broadcasted_iota(jnp.int32, (groups.shape[0] - 1,) + groups.shape[1:], 1)
    out = []
    for s in shifts:
        y = pltpu.roll(groups, s % 8, 1)
        moved = jnp.where(row >= s, y[1:], y[:-1]) if s > 0 else jnp.where(row < 8 + s, y[:-1], y[1:])
        out.append(moved.reshape(-1, groups.shape[2]))
    return out


def _conv(taps, cw, cb):
    acc = taps[0] * cw[0:1, :] + taps[1] * cw[1:2, :]
    acc = acc + taps[2] * cw[2:3, :]
    acc = acc + taps[3] * cw[3:4, :]
    return cb + acc


def _lru_gates(xa, wx_ref, wa_ref, bx, ba, lam):
    xab = xa.astype(BF16)
    pis, prs = [], []
    for h in range(NB):
        xs = xab[:, h * BD:(h + 1) * BD]
        pis.append(jnp.dot(xs, wx_ref[h], preferred_element_type=F32))
        prs.append(jnp.dot(xs, wa_ref[h], preferred_element_type=F32))
    gi = _sigmoid(jnp.concatenate(pis, axis=1) + bx)
    gr = _sigmoid(jnp.concatenate(prs, axis=1) + ba)
    sp = _softplus_neg(lam)
    log_a = (-LRU_C * gr) * sp
    a = jnp.exp(log_a)
    mult = jnp.sqrt(-jnp.tanh(log_a) * (a * a + 1.0))
    return xab, gi, gr, sp, a, mult


def _lru_fwd(z, cw8, cb, wx, wa, bx, ba, lam, nb, s_len):
    n = nb * s_len
    t = LRU_T
    ns = s_len // t

    def body(xp_ref, ga_ref, cw_ref, cb_ref, wx_ref, wa_ref, bx_ref, ba_ref, lam_ref,
             h_ref, ya_ref, ext, a_s, u_s, carry):
        @pl.when(pl.program_id(1) == 0)
        def _():
            ext[0:8, :] = jnp.zeros((8, D), F32)
            carry[...] = jnp.zeros((8, D), F32)

        xp = xp_ref[0]
        ext[8:8 + t, :] = xp
        xa = _conv(_shifted(_groups(ext[...]), (3, 2, 1)) + [xp], cw_ref[...], cb_ref[...])
        ext[0:8, :] = xp[t - 8:t, :]
        _, gi, _, _, a, mult = _lru_gates(xa, wx_ref, wa_ref, bx_ref[...], ba_ref[...], lam_ref[...])
        u = (mult * gi) * xa
        a, u = _groups(a), _groups(u)
        row = lax.broadcasted_iota(jnp.int32, a.shape, 1)
        for sh in (1, 2, 4):
            a_sh = pltpu.roll(a, sh, 1)
            u_sh = pltpu.roll(u, sh, 1)
            m = row >= sh
            u = jnp.where(m, a * u_sh + u, u)
            a = jnp.where(m, a * a_sh, a)
        a_s[...] = a.reshape(t, D)
        u_s[...] = u.reshape(t, D)

        def step(g, c):
            r = pl.multiple_of(g * 8, 8)
            hg = u_s[pl.ds(r, 8), :] + a_s[pl.ds(r, 8), :] * c
            h_ref[pl.ds(r, 8), :] = hg
            return hg[7:8, :]

        c_out = lax.fori_loop(0, t // 8, step, carry[0:1, :], unroll=4)
        carry[0:1, :] = c_out
        ga = ga_ref[0]
        ya_ref[...] = (h_ref[...] * (ga * _sigmoid(ga))).astype(BF16)

    row_map = lambda b, s: (b * ns + s, 0)
    rep2 = lambda b, s: (0, 0)
    rep3 = lambda b, s: (0, 0, 0)
    return pl.pallas_call(
        body, name="lru_fwd", grid=(nb, ns),
        in_specs=[pl.BlockSpec((1, t, D), lambda b, s: (0, b * ns + s, 0)),
                  pl.BlockSpec((1, t, D), lambda b, s: (1, b * ns + s, 0)),
                  pl.BlockSpec((8, D), rep2), pl.BlockSpec((1, D), rep2),
                  pl.BlockSpec((NB, BD, BD), rep3), pl.BlockSpec((NB, BD, BD), rep3),
                  pl.BlockSpec((1, D), rep2), pl.BlockSpec((1, D), rep2), pl.BlockSpec((1, D), rep2)],
        out_specs=[pl.BlockSpec((t, D), row_map), pl.BlockSpec((t, D), row_map)],
        out_shape=[SDS((n, D), F32), SDS((n, D), BF16)],
        scratch_shapes=[pltpu.VMEM((t + 8, D), F32), pltpu.VMEM((t, D), F32), pltpu.VMEM((t, D), F32),
                        pltpu.VMEM((8, D), F32)],
        compiler_params=_params(48),
    )(z, z, cw8, cb, wx, wa, bx, ba, lam)


def _lru_bwd(z, h_all, dya, cw8, cb, wx, wa, bx, ba, lam, nb, s_len):
    n = nb * s_len
    t = LRU_T
    ns = s_len // t
    t8 = t // 8

    def body(xp_ref, xph_ref, ga_ref, h_ref, hh_ref, dya_ref, cw_ref, cb_ref, wx_ref, wa_ref, bx_ref, ba_ref,
             lam_ref, dz_ref, gcw_ref, gcb_ref, gwx_ref, gwa_ref, gbx_ref, gba_ref, glam_ref,
             ext, hext, dext, a_s, u_s, dh_s, carry):
        b, s = pl.program_id(0), pl.program_id(1)
        first_tile = s == ns - 1

        @pl.when((b == 0) & (s == 0))
        def _():
            for ref in (gcw_ref, gcb_ref, gwx_ref, gwa_ref, gbx_ref, gba_ref, glam_ref):
                ref[...] = jnp.zeros(ref.shape, F32)

        @pl.when(s == 0)
        def _():
            dext[t:t + 8, :] = jnp.zeros((8, D), F32)
            carry[...] = jnp.zeros((8, D), F32)

        keep = jnp.where(first_tile, 0.0, 1.0)
        xp = xp_ref[0]
        ext[0:8, :] = xph_ref[0] * keep
        ext[8:8 + t, :] = xp
        hext[0:8, :] = hh_ref[...] * keep
        hext[8:8 + t, :] = h_ref[...]
        cw = cw_ref[...]
        lam = lam_ref[...]
        taps = _shifted(_groups(ext[...]), (3, 2, 1)) + [xp]
        xa = _conv(taps, cw, cb_ref[...])
        xab, gi, gr, sp, a, mult = _lru_gates(xa, wx_ref, wa_ref, bx_ref[...], ba_ref[...], lam)
        (h_prev,) = _shifted(_groups(hext[...]), (1,))
        ga = ga_ref[0]
        sg = _sigmoid(ga)
        dya_v = dya_ref[...]
        d_ga = dya_v * h_ref[...] * (sg * (1.0 + ga * (1.0 - sg)))
        g_in = dya_v * (ga * sg)

        (an,) = _shifted(jnp.concatenate([_groups(a), jnp.ones((1, 8, D), F32)], axis=0), (-1,))
        an, u = _groups(an), _groups(g_in)
        row = lax.broadcasted_iota(jnp.int32, an.shape, 1)
        for sh in (1, 2, 4):
            a_sh = pltpu.roll(an, 8 - sh, 1)
            u_sh = pltpu.roll(u, 8 - sh, 1)
            m = row < 8 - sh
            u = jnp.where(m, u + an * u_sh, u)
            an = jnp.where(m, an * a_sh, an)
        a_s[...] = an.reshape(t, D)
        u_s[...] = u.reshape(t, D)

        def step(i, c):
            r = pl.multiple_of((t8 - 1 - i) * 8, 8)
            dg = u_s[pl.ds(r, 8), :] + a_s[pl.ds(r, 8), :] * c
            dh_s[pl.ds(r, 8), :] = dg
            return dg[0:1, :]

        lax.fori_loop(0, t8, step, carry[0:1, :], unroll=4)
        dh = dh_s[...]
        carry[0:1, :] = a[0:1, :] * dh[0:1, :]

        d_a = dh * h_prev
        dux = dh * xa
        d_mult = dux * gi
        d_gi = dux * mult
        d_xa = dh * (mult * gi)
        d_loga = d_a * a - d_mult * ((a * a) / mult)
        d_gr = d_loga * (-LRU_C * sp)
        d_sp = jnp.sum(d_loga * (-LRU_C * gr), axis=0, keepdims=True)
        glam_ref[...] += d_sp * (-_sigmoid(-lam))
        d_pi = d_gi * gi * (1.0 - gi)
        d_pr = d_gr * gr * (1.0 - gr)
        gbx_ref[...] += jnp.sum(d_pi, axis=0, keepdims=True)
        gba_ref[...] += jnp.sum(d_pr, axis=0, keepdims=True)
        dpib = d_pi.astype(BF16)
        dprb = d_pr.astype(BF16)
        back = []
        for h in range(NB):
            cs = slice(h * BD, (h + 1) * BD)
            gwx_ref[h] += lax.dot_general(xab[:, cs], dpib[:, cs], TN_DIMS, preferred_element_type=F32)
            gwa_ref[h] += lax.dot_general(xab[:, cs], dprb[:, cs], TN_DIMS, preferred_element_type=F32)
            back.append(lax.dot_general(dpib[:, cs], wx_ref[h], NT_DIMS, preferred_element_type=F32)
                        + lax.dot_general(dprb[:, cs], wa_ref[h], NT_DIMS, preferred_element_type=F32))
        d_xa = d_xa + jnp.concatenate(back, axis=1)

        dext[0:t, :] = d_xa
        later = _shifted(_groups(dext[...]), (-3, -2, -1))
        d_xp = later[0] * cw[0:1, :] + later[1] * cw[1:2, :]
        d_xp = d_xp + later[2] * cw[2:3, :]
        d_xp = d_xp + d_xa * cw[3:4, :]
        dext[t:t + 8, :] = d_xa[0:8, :]
        gcb_ref[...] += jnp.sum(d_xa, axis=0, keepdims=True)
        for k in range(4):
            gcw_ref[k:k + 1, :] += jnp.sum(d_xa * taps[k], axis=0, keepdims=True)
        dz_ref[0] = d_xp.astype(BF16)
        dz_ref[1] = d_ga.astype(BF16)

    rb = lambda b, s: b * ns + (ns - 1 - s)
    halo = lambda b, s: jnp.maximum(rb(b, s) * t8 - 1, 0)
    rep2 = lambda b, s: (0, 0)
    rep3 = lambda b, s: (0, 0, 0)
    return pl.pallas_call(
        body, name="lru_bwd", grid=(nb, ns),
        in_specs=[pl.BlockSpec((1, t, D), lambda b, s: (0, rb(b, s), 0)),
                  pl.BlockSpec((1, 8, D), lambda b, s: (0, halo(b, s), 0)),
                  pl.BlockSpec((1, t, D), lambda b, s: (1, rb(b, s), 0)),
                  pl.BlockSpec((t, D), lambda b, s: (rb(b, s), 0)),
                  pl.BlockSpec((8, D), lambda b, s: (halo(b, s), 0)),
                  pl.BlockSpec((t, D), lambda b, s: (rb(b, s), 0)),
                  pl.BlockSpec((8, D), rep2), pl.BlockSpec((1, D), rep2),
                  pl.BlockSpec((NB, BD, BD), rep3), pl.BlockSpec((NB, BD, BD), rep3),
                  pl.BlockSpec((1, D), rep2), pl.BlockSpec((1, D), rep2), pl.BlockSpec((1, D), rep2)],
        out_specs=[pl.BlockSpec((2, t, D), lambda b, s: (0, rb(b, s), 0)),
                   pl.BlockSpec((8, D), rep2), pl.BlockSpec((1, D), rep2),
                   pl.BlockSpec((NB, BD, BD), rep3), pl.BlockSpec((NB, BD, BD), rep3),
                   pl.BlockSpec((1, D), rep2), pl.BlockSpec((1, D), rep2), pl.BlockSpec((1, D), rep2)],
        out_shape=[SDS((2, n, D), BF16), SDS((8, D), F32), SDS((1, D), F32),
                   SDS((NB, BD, BD), F32), SDS((NB, BD, BD), F32),
                   SDS((1, D), F32), SDS((1, D), F32), SDS((1, D), F32)],
        scratch_shapes=[pltpu.VMEM((t + 8, D), F32), pltpu.VMEM((t + 8, D), F32), pltpu.VMEM((t + 8, D), F32),
                        pltpu.VMEM((t, D), F32), pltpu.VMEM((t, D), F32), pltpu.VMEM((t, D), F32),
                        pltpu.VMEM((8, D), F32)],
        compiler_params=_params(56),
    )(z, z, z, h_all, h_all, dya, cw8, cb, wx, wa, bx, ba, lam)


HG_T = 512
HG_NC = HG_T // CHUNK
BNT_DIMS = (((2,), (2,)), ((0,), (0,)))
BNN_DIMS = (((2,), (1,)), ((0,), (0,)))
BTN_DIMS = (((1,), (1,)), ((0,), (0,)))


def _lower_bound(lg):
    m = jnp.max(lg, axis=0, keepdims=True)
    e = jnp.exp(lg - m)
    return e[0:1, :] / jnp.sum(e, axis=0, keepdims=True)


def _tri(upper):
    r = lax.broadcasted_iota(jnp.int32, (HG_NC, CHUNK, CHUNK), 1)
    c = lax.broadcasted_iota(jnp.int32, (HG_NC, CHUNK, CHUNK), 2)
    return (c >= r) if upper else (r >= c)


def _bdot(a, b, dims):
    return lax.dot_general(a, b, dims, preferred_element_type=F32)


def _tri_sums(upper, a):
    tri = _tri(upper).astype(BF16)
    a1 = a.astype(BF16)
    r1 = a - a1.astype(F32)
    a2 = r1.astype(BF16)
    a3 = (r1 - a2.astype(F32)).astype(BF16)
    return _bdot(tri, a1, BNN_DIMS) + (_bdot(tri, a2, BNN_DIMS) + _bdot(tri, a3, BNN_DIMS))


def _chunks(a):
    return a.reshape(HG_NC, CHUNK, BD)


def _hg_tile(q, fp, lb):
    q, fp = _chunks(q), _chunks(fp)
    sig = _sigmoid(fp)
    f = lb + (1.0 - lb) * sig
    log_f = jnp.log(f)
    k = 1.0 - f
    b = _tri_sums(False, log_f)
    b_mid = b[:, CHUNK // 2:CHUNK // 2 + 1, :]
    b_last = b[:, CHUNK - 1:CHUNK, :]
    sq = _sigmoid(q)
    qh = q * sq
    e_qi = jnp.exp(b - b_mid)
    e_ki = jnp.exp(b_mid - b)
    e_qs = jnp.exp(b)
    e_ks = jnp.exp(b_last - b)
    dc = jnp.exp(b_last)
    q_in = (qh * e_qi) * HG_SCALE
    k_in = k * e_ki
    q_st = (qh * e_qs) * HG_SCALE
    k_st = k * e_ks
    att = _bdot(q_in.astype(BF16), k_in.astype(BF16), BNT_DIMS)
    att = jnp.where(_tri(False), att, 0.0)
    return dict(q=q, sig=sig, f=f, k=k, sq=sq, e_qi=e_qi, e_ki=e_ki, e_qs=e_qs, e_ks=e_ks, dc=dc,
                q_in=q_in, k_in=k_in, q_st=q_st, k_st=k_st, att=att)


def _hgrn_fwd(z, lb_logits, hg_g, nb, s_len):
    n = nb * s_len
    t = HG_T
    ns = s_len // t
    nchunk = s_len // CHUNK

    def body(q_ref, f_ref, v_ref, gb_ref, lg_ref, g_ref, o_ref, yb_ref, st_ref, st):
        @pl.when(pl.program_id(1) == 0)
        def _():
            st[...] = jnp.zeros((NB, BD, BD), F32)

        def head(h, carry):
            cols = pl.ds(pl.multiple_of(h * BD, BD), BD)
            lb = _lower_bound(lg_ref[:, cols])
            ck = _hg_tile(q_ref[0, :, cols], f_ref[0, :, cols], lb)
            vb = _chunks(v_ref[0, :, cols]).astype(BF16)
            kv = _bdot(vb, ck["k_st"].astype(BF16), BTN_DIMS)
            states = [st[h]]
            for c in range(HG_NC):
                states.append(states[c] * ck["dc"][c] + kv[c])
            st[h] = states[HG_NC]
            s_in = jnp.stack(states[:HG_NC], axis=0)
            st_ref[h] = s_in
            o = (_bdot(ck["att"].astype(BF16), vb, BNN_DIMS)
                 + _bdot(ck["q_st"].astype(BF16), s_in.astype(BF16), BNT_DIMS))
            o_ref[:, cols] = o.reshape(t, BD)
            r = lax.rsqrt(jnp.mean(o * o, axis=-1, keepdims=True) + EPS)
            gb = _chunks(gb_ref[0, :, cols])
            yb_ref[:, cols] = (((o * r) * g_ref[...]) * (gb * _sigmoid(gb))).astype(BF16).reshape(t, BD)
            return carry

        lax.fori_loop(0, NB, head, 0, unroll=4)

    seg = lambda j: pl.BlockSpec((1, t, D), lambda b, s: (j, b * ns + s, 0))
    tile = pl.BlockSpec((t, D), lambda b, s: (b * ns + s, 0))
    return pl.pallas_call(
        body, name="hgrn_fwd", grid=(nb, ns),
        in_specs=[seg(2), seg(3), seg(4), seg(5),
                  pl.BlockSpec((2, D), lambda b, s: (0, 0)), pl.BlockSpec((1, BD), lambda b, s: (0, 0))],
        out_specs=[tile, tile, pl.BlockSpec((NB, HG_NC, BD, BD), lambda b, s: (b, s, 0, 0))],
        out_shape=[SDS((n, D), F32), SDS((n, D), BF16), SDS((nb * NB, nchunk, BD, BD), F32)],
        scratch_shapes=[pltpu.VMEM((NB, BD, BD), F32)],
        compiler_params=_params(56),
    )(z, z, z, z, lb_logits, hg_g)


def _hgrn_bwd(z, o_all, st_all, dyb, lb_logits, hg_g, nb, s_len):
    n = nb * s_len
    t = HG_T
    ns = s_len // t

    def body(q_ref, f_ref, v_ref, gb_ref, o_ref, st_ref, dyb_ref, lg_ref, g_ref,
             dz_ref, glg_ref, ghg_ref, dst, dlb):
        b, s = pl.program_id(0), pl.program_id(1)

        @pl.when((b == 0) & (s == 0))
        def _():
            ghg_ref[...] = jnp.zeros((1, BD), F32)
            dlb[...] = jnp.zeros((8, D), F32)

        @pl.when(s == 0)
        def _():
            dst[...] = jnp.zeros((NB, BD, BD), F32)

        g = g_ref[...]

        def head(h, carry):
            cols = pl.ds(pl.multiple_of(h * BD, BD), BD)
            lb = _lower_bound(lg_ref[:, cols])
            ck = _hg_tile(q_ref[0, :, cols], f_ref[0, :, cols], lb)
            q = ck["q"]
            vb = _chunks(v_ref[0, :, cols]).astype(BF16)
            gb = _chunks(gb_ref[0, :, cols])
            o = _chunks(o_ref[:, cols])
            dyb_v = _chunks(dyb_ref[:, cols])
            s_in = st_ref[h]

            sgb = _sigmoid(gb)
            r = lax.rsqrt(jnp.mean(o * o, axis=-1, keepdims=True) + EPS)
            ohat = o * r
            d_on = dyb_v * (gb * sgb)
            d_gb = dyb_v * (ohat * g) * (sgb * (1.0 + gb * (1.0 - sgb)))
            ghg_ref[...] += jnp.sum(jnp.sum(d_on * ohat, axis=1), axis=0, keepdims=True)
            tt = d_on * g
            d_o = r * (tt - ohat * jnp.mean(tt * ohat, axis=-1, keepdims=True))
            dob = d_o.astype(BF16)

            attb = ck["att"].astype(BF16)
            q_inb, k_inb = ck["q_in"].astype(BF16), ck["k_in"].astype(BF16)
            q_stb, k_stb = ck["q_st"].astype(BF16), ck["k_st"].astype(BF16)
            d_att = jnp.where(_tri(False), _bdot(dob, vb, BNT_DIMS), 0.0).astype(BF16)
            d_q_in = _bdot(d_att, k_inb, BNN_DIMS)
            d_k_in = _bdot(d_att, q_inb, BTN_DIMS)
            d_q_st = _bdot(dob, s_in.astype(BF16), BNN_DIMS)
            qdo = _bdot(dob, q_stb, BTN_DIMS)
            d_states = [None] * HG_NC + [dst[h]]
            for c in reversed(range(HG_NC)):
                d_states[c] = d_states[c + 1] * ck["dc"][c] + qdo[c]
            dst[h] = d_states[0]
            ds_out = jnp.stack(d_states[1:], axis=0)
            dsb = ds_out.astype(BF16)
            d_v = _bdot(attb, dob, BTN_DIMS) + _bdot(k_stb, dsb, BNT_DIMS)
            d_k_st = _bdot(vb, dsb, BNN_DIMS)
            d_dc = jnp.sum(ds_out * s_in, axis=1, keepdims=True)

            p_qi = d_q_in * ck["q_in"]
            p_ki = d_k_in * ck["k_in"]
            p_qs = d_q_st * ck["q_st"]
            p_ks = d_k_st * ck["k_st"]
            d_qh = (d_q_in * ck["e_qi"] + d_q_st * ck["e_qs"]) * HG_SCALE
            d_k = d_k_in * ck["e_ki"] + d_k_st * ck["e_ks"]
            d_b = (p_qi - p_ki) + (p_qs - p_ks)
            d_b_mid = jnp.sum(p_ki - p_qi, axis=1, keepdims=True)
            d_b_last = jnp.sum(p_ks, axis=1, keepdims=True) + d_dc * ck["dc"]
            rowi = lax.broadcasted_iota(jnp.int32, (HG_NC, CHUNK, BD), 1)
            d_b = d_b + jnp.where(rowi == CHUNK // 2, d_b_mid, 0.0) + jnp.where(rowi == CHUNK - 1, d_b_last, 0.0)
            d_logf = _tri_sums(True, d_b)
            d_f = d_logf / ck["f"] - d_k
            sig, sq = ck["sig"], ck["sq"]
            d_fp = d_f * (1.0 - lb) * (sig * (1.0 - sig))
            dlb[0:1, cols] += jnp.sum(jnp.sum(d_f * (1.0 - sig), axis=1), axis=0, keepdims=True)
            d_q = d_qh * (sq * (1.0 + q * (1.0 - sq)))
            dz_ref[0, :, cols] = d_q.astype(BF16).reshape(t, BD)
            dz_ref[1, :, cols] = d_fp.astype(BF16).reshape(t, BD)
            dz_ref[2, :, cols] = d_v.astype(BF16).reshape(t, BD)
            dz_ref[3, :, cols] = d_gb.astype(BF16).reshape(t, BD)
            return carry

        lax.fori_loop(0, NB, head, 0, unroll=2)

        @pl.when((b == nb - 1) & (s == ns - 1))
        def _():
            lb = _lower_bound(lg_ref[...])
            dl = dlb[0:1, :] * (lb * (1.0 - lb))
            glg_ref[0:1, :] = dl
            glg_ref[1:2, :] = -dl

    rb = lambda b, s: b * ns + (ns - 1 - s)
    seg = lambda j: pl.BlockSpec((1, t, D), lambda b, s: (j, rb(b, s), 0))
    tile = pl.BlockSpec((t, D), lambda b, s: (rb(b, s), 0))
    return pl.pallas_call(
        body, name="hgrn_bwd", grid=(nb, ns),
        in_specs=[seg(2), seg(3), seg(4), seg(5), tile,
                  pl.BlockSpec((NB, HG_NC, BD, BD), lambda b, s: (b, ns - 1 - s, 0, 0)),
                  tile, pl.BlockSpec((2, D), lambda b, s: (0, 0)), pl.BlockSpec((1, BD), lambda b, s: (0, 0))],
        out_specs=[pl.BlockSpec((4, t, D), lambda b, s: (0, rb(b, s), 0)),
                   pl.BlockSpec((2, D), lambda b, s: (0, 0)), pl.BlockSpec((1, BD), lambda b, s: (0, 0))],
        out_shape=[SDS((4, n, D), BF16), SDS((2, D), F32), SDS((1, BD), F32)],
        scratch_shapes=[pltpu.VMEM((NB, BD, BD), F32), pltpu.VMEM((8, D), F32)],
        compiler_params=_params(60),
    )(z, z, z, z, o_all, st_all, dyb, lb_logits, hg_g)


def _mid(ya, yb, z, b_merge, x2, tgt, fin_g, pa, pb, wo):
    n = x2.shape[0]
    tm = 256
    ni = n // tm

    def body(ya_ref, yb_ref, gma_ref, gmb_ref, bm_ref, x_ref, t_ref, fg_ref, pa_hbm, pb_hbm, wo_hbm,
             dx2_ref, dya_ref, dyb_ref, dgm_ref, loss_ref, gfg_ref, gbm_ref, gm_hbm,
             pa_v, pb_v, wo_v, gpa_v, gpb_v, gwo_v, sem):
        i = pl.program_id(0)
        by_owner = lambda g: g.reshape(NB, BD, D)
        loads = [pltpu.make_async_copy(src, dst, sem.at[k])
                 for k, (src, dst) in enumerate(((pa_hbm, pa_v), (pb_hbm, pb_v), (wo_hbm, wo_v)))]
        stores = [pltpu.make_async_copy(src, dst, sem.at[k])
                  for k, (src, dst) in enumerate((g, gm_hbm.at[:, pl.ds(slot * BD, BD), :])
                                                 for slot, g in enumerate((gpa_v, gpb_v, gwo_v)))]

        @pl.when(i == 0)
        def _():
            for cp in loads:
                cp.start()
            for ref in (gpa_v, gpb_v, gwo_v, loss_ref, gfg_ref, gbm_ref):
                ref[...] = jnp.zeros(ref.shape, F32)
            for cp in loads:
                cp.wait()

        ya_v = ya_ref[...]
        yb_v = yb_ref[...]
        out_a = jnp.dot(ya_v, pa_v[...], preferred_element_type=F32)
        out_b = jnp.dot(yb_v, pb_v[...], preferred_element_type=F32)
        bm = bm_ref[...]
        g_a = _sigmoid(gma_ref[0] + bm[:, 0:D])
        g_b = _sigmoid(gmb_ref[0] + bm[:, D:2 * D])
        mixed = g_a * out_a + g_b * out_b
        mixb = mixed.astype(BF16)
        xo = x_ref[...] + jnp.dot(mixb, wo_v[...], preferred_element_type=F32)
        r = lax.rsqrt(jnp.mean(xo * xo, axis=-1, keepdims=True) + EPS)
        xn = xo * r
        fg = fg_ref[...]
        e = xn * fg - t_ref[...]
        loss_ref[...] += 0.5 * jnp.sum(jnp.mean(e * e, axis=-1, keepdims=True))
        dy = e * (1.0 / D)
        gfg_ref[...] += jnp.sum(dy * xn, axis=0, keepdims=True)
        dxn = dy * fg
        dx2 = r * (dxn - xn * jnp.mean(dxn * xn, axis=-1, keepdims=True))
        dx2_ref[...] = dx2
        dx2b = dx2.astype(BF16)
        d_mixed = lax.dot_general(dx2b, wo_v[...], NT_DIMS, preferred_element_type=F32)
        gwo_v[...] += by_owner(lax.dot_general(mixb, dx2b, TN_DIMS, preferred_element_type=F32))
        d_oa = (d_mixed * g_a).astype(BF16)
        d_ob = (d_mixed * g_b).astype(BF16)
        dgm_a = (d_mixed * out_a) * (g_a * (1.0 - g_a))
        dgm_b = (d_mixed * out_b) * (g_b * (1.0 - g_b))
        gbm_ref[:, 0:D] += jnp.sum(dgm_a, axis=0, keepdims=True)
        gbm_ref[:, D:2 * D] += jnp.sum(dgm_b, axis=0, keepdims=True)
        dgm_ref[0] = dgm_a.astype(BF16)
        dgm_ref[1] = dgm_b.astype(BF16)
        dya_ref[...] = lax.dot_general(d_oa, pa_v[...], NT_DIMS, preferred_element_type=F32)
        dyb_ref[...] = lax.dot_general(d_ob, pb_v[...], NT_DIMS, preferred_element_type=F32)
        gpa_v[...] += by_owner(lax.dot_general(ya_v, d_oa, TN_DIMS, preferred_element_type=F32))
        gpb_v[...] += by_owner(lax.dot_general(yb_v, d_ob, TN_DIMS, preferred_element_type=F32))

        @pl.when(i == ni - 1)
        def _():
            for cp in stores:
                cp.start()
            for cp in stores:
                cp.wait()

    rows = pl.BlockSpec((tm, D), lambda i: (i, 0))
    rep = lambda shape: pl.BlockSpec(shape, lambda i: (0,) * len(shape))
    return pl.pallas_call(
        body, name="mid", grid=(ni,),
        in_specs=[rows, rows,
                  pl.BlockSpec((1, tm, D), lambda i: (6, i, 0)), pl.BlockSpec((1, tm, D), lambda i: (7, i, 0)),
                  rep((1, 2 * D)), rows, rows, rep((1, D)), ANY, ANY, ANY],
        out_specs=[rows, rows, rows, pl.BlockSpec((2, tm, D), lambda i: (0, i, 0)),
                   rep((8, BD)), rep((1, D)), rep((1, 2 * D)), ANY],
        out_shape=[SDS((n, D), F32), SDS((n, D), F32), SDS((n, D), F32), SDS((2, n, D), BF16),
                   SDS((8, BD), F32), SDS((1, D), F32), SDS((1, 2 * D), F32),
                   SDS((NB, MID_ROWS, D), F32)],
        scratch_shapes=[pltpu.VMEM((D, D), BF16)] * 3 + [pltpu.VMEM((NB, BD, D), F32)] * 3 + [pltpu.SemaphoreType.DMA((3,))],
        compiler_params=_params(60),
    )(ya, yb, z, z, b_merge, x2, tgt, fin_g, pa, pb, wo)


def _dz_specs(tm, ni, row_major):
    if row_major:
        ia = lambda i, j: (jnp.minimum(j, 1), i, 0)
        ib = lambda i, j: (jnp.clip(j - 2, 0, 3), i, 0)
        im = lambda i, j: (jnp.clip(j - 6, 0, 1), i, 0)
    else:
        last = ni - 1
        ia = lambda j, i: (jnp.minimum(j, 1), jnp.where(j < 2, i, last), 0)
        ib = lambda j, i: (jnp.clip(j - 2, 0, 3), jnp.where(j < 2, 0, jnp.where(j < 6, i, last)), 0)
        im = lambda j, i: (jnp.clip(j - 6, 0, 1), jnp.where(j < 6, 0, i), 0)
    return [pl.BlockSpec((1, tm, D), f) for f in (ia, ib, im)]


def _inproj_bwd_x(dza, dzb, dzm, w_all, x2, dx2, norm_g, after):
    n = x2.shape[0]
    tm = 512
    ni = n // tm

    def body(dza_ref, dzb_ref, dzm_ref, w_ref, x_ref, dx2_ref, g_ref, after_ref, gx_ref, gg_ref, acc):
        i, j = pl.program_id(0), pl.program_id(1)

        @pl.when((i == 0) & (j == 0))
        def _():
            gg_ref[...] = jnp.zeros((1, D), F32)

        @pl.when(j == 0)
        def _():
            acc[...] = jnp.zeros((tm, D), F32)

        def add(ref):
            acc[...] += lax.dot_general(ref[0], w_ref[0], NT_DIMS, preferred_element_type=F32)

        pl.when(j < 2)(lambda: add(dza_ref))
        pl.when((j >= 2) & (j < 6))(lambda: add(dzb_ref))
        pl.when(j >= 6)(lambda: add(dzm_ref))

        @pl.when(j == NB - 1)
        def _():
            x = x_ref[...]
            r = lax.rsqrt(jnp.mean(x * x, axis=-1, keepdims=True) + EPS)
            xn = x * r
            dh = acc[...]
            gg_ref[...] += jnp.sum(dh * xn, axis=0, keepdims=True)
            dxn = dh * g_ref[...]
            gx_ref[...] = dx2_ref[...] + r * (dxn - xn * jnp.mean(dxn * xn, axis=-1, keepdims=True))

    rows = pl.BlockSpec((tm, D), lambda i, j: (i, 0))
    return pl.pallas_call(
        body, name="inproj_bwd_x", grid=(ni, NB),
        in_specs=_dz_specs(tm, ni, True) + [pl.BlockSpec((1, D, D), lambda i, j: (j, 0, 0)), rows, rows,
                                             pl.BlockSpec((1, D), lambda i, j: (0, 0)), ANY],
        out_specs=[rows, pl.BlockSpec((1, D), lambda i, j: (0, 0))],
        out_shape=[SDS((n, D), F32), SDS((1, D), F32)],
        scratch_shapes=[pltpu.VMEM((tm, D), F32)],
        compiler_params=_params(48),
    )(dza, dzb, dzm, w_all, x2, dx2, norm_g, after)


def _inproj_bwd_w(dza, dzb, dzm, h_all, g_m):
    n = h_all.shape[0]
    tm = min(n, 2048)
    ni = n // tm

    def body(dza_ref, dzb_ref, dzm_ref, h_ref, gm_hbm, gw_ref, got_w, got_m, stage, send_sems, recv_sems):
        j, i = pl.program_id(0), pl.program_id(1)
        x, y, c = _place()
        sibling = (x, y, 1 - c)

        def send_w(q):
            return pltpu.make_async_remote_copy(
                src_ref=stage.at[q % 2], dst_ref=got_w.at[q], send_sem=send_sems.at[q], recv_sem=recv_sems.at[q],
                device_id=sibling, device_id_type=MESH)

        def send_m(q):
            return pltpu.make_async_remote_copy(
                src_ref=gm_hbm.at[2 * q + (1 - c)], dst_ref=got_m.at[q], send_sem=send_sems.at[4 + q],
                recv_sem=recv_sems.at[4 + q], device_id=sibling, device_id_type=MESH)

        @pl.when((j == 0) & (i == 0))
        def _():
            for q in range(4):
                send_m(q).start()

        @pl.when(i == 0)
        def _():
            gw_ref[...] = jnp.zeros((1, D, D), F32)

        def add(ref):
            gw_ref[0] += lax.dot_general(h_ref[...], ref[0], TN_DIMS, preferred_element_type=F32)

        pl.when(j < 2)(lambda: add(dza_ref))
        pl.when((j >= 2) & (j < 6))(lambda: add(dzb_ref))
        pl.when(j >= 6)(lambda: add(dzm_ref))

        for q in range(4):
            @pl.when((i == ni - 1) & (j == 2 * q + 1 - c))
            def _(q=q):
                if q >= 2:
                    send_w(q - 2).wait_send()
                stage[q % 2] = gw_ref[0].astype(BF16)
                send_w(q).start()

        @pl.when((j == NB - 1) & (i == ni - 1))
        def _():
            for q in (2, 3):
                send_w(q).wait_send()
            for q in range(4):
                send_w(q).wait_recv()
                send_m(q).wait_send()
                send_m(q).wait_recv()

    return pl.pallas_call(
        body, name="inproj_bwd_w", grid=(NB, ni),
        in_specs=_dz_specs(tm, ni, False) + [pl.BlockSpec((tm, D), lambda j, i: (i, 0)), ANY],
        out_specs=[pl.BlockSpec((1, D, D), lambda j, i: (j, 0, 0)), ANY, ANY],
        out_shape=[SDS((NB, D, D), F32), SDS((4, D, D), BF16), SDS((4,) + g_m.shape[1:], F32)],
        scratch_shapes=[pltpu.VMEM((2, D, D), BF16), pltpu.SemaphoreType.DMA((8,)), pltpu.SemaphoreType.DMA((8,))],
        compiler_params=_params(58),
    )(dza, dzb, dzm, h_all, g_m)


def _adamw(w, g, m, v):
    rows, cols = w.shape
    tr = _row_tile(rows)

    spec = pl.BlockSpec((tr, cols), lambda i: (i, 0))
    return pl.pallas_call(
        functools.partial(_adam_refs), name="adamw", grid=(rows // tr,), in_specs=[spec] * 4, out_specs=[spec] * 3,
        out_shape=[SDS((rows, cols), F32)] * 3, compiler_params=_params(32),
    )(w, g, m, v)


def _adam_refs(w_ref, g_ref, m_ref, v_ref, d_ref, nm_ref, nv_ref):
    gv = g_ref[...]
    nm = ADAM_B1 * m_ref[...] + (1.0 - ADAM_B1) * gv
    nv = ADAM_B2 * v_ref[...] + (1.0 - ADAM_B2) * (gv * gv)
    m_hat = nm / (1.0 - ADAM_B1 ** ADAM_STEP)
    v_hat = nv / (1.0 - ADAM_B2 ** ADAM_STEP)
    d_ref[...] = -ADAM_LR * (m_hat / (jnp.sqrt(v_hat) + ADAM_EPS) + ADAM_WD * w_ref[...])
    nm_ref[...] = nm
    nv_ref[...] = nv


def _adamw_small(ws, gs, ms, vs):
    k = len(ws)

    def body(*refs):
        ins, outs = refs[:4 * k], refs[4 * k:7 * k]
        vin, vout = refs[7 * k:11 * k], refs[11 * k:14 * k]
        load_sems, store_sems = refs[14 * k:]
        loads = [pltpu.make_async_copy(ins[i], vin[i], load_sems.at[i]) for i in range(4 * k)]
        for cp in loads:
            cp.start()
        for cp in loads:
            cp.wait()
        for i in range(k):
            _adam_refs(*[vin[part * k + i] for part in range(4)], *[vout[part * k + i] for part in range(3)])
        stores = [pltpu.make_async_copy(vout[i], outs[i], store_sems.at[i]) for i in range(3 * k)]
        for cp in stores:
            cp.start()
        for cp in stores:
            cp.wait()

    shapes = [SDS(w.shape, F32) for w in ws]
    vmem = [pltpu.VMEM(w.shape, F32) for w in ws]
    out = pl.pallas_call(
        body, name="adamw_small", out_shape=shapes * 3, in_specs=[HBM] * (4 * k), out_specs=[HBM] * (3 * k),
        scratch_shapes=vmem * 7 + [pltpu.SemaphoreType.DMA((4 * k,)), pltpu.SemaphoreType.DMA((3 * k,))],
        compiler_params=_params(32),
    )(*ws, *gs, *ms, *vs)
    return out[:k], out[k:2 * k], out[2 * k:]


def _allgather(blocks, dtypes, name):
    na = len(blocks)

    def body(*refs):
        ins, outs, stages = refs[:na], refs[na:2 * na], refs[2 * na:3 * na]
        send_sems, recv_sems, local_sems = refs[3 * na:]
        x, y, c = _place()
        me, sibling = (x, y, c), (x, y, 1 - c)
        chips = [(1 - x, y), (x, 1 - y), (1 - x, 1 - y)]
        blk = lambda p: 4 * p[0] + 2 * p[1] + p[2]

        def copy(a, k, block, to, src=None):
            return pltpu.make_async_remote_copy(
                src_ref=outs[a].at[blk(block)] if src is None else src, dst_ref=outs[a].at[blk(block)],
                send_sem=send_sems.at[7 * a + k], recv_sem=recv_sems.at[7 * a + k],
                device_id=to, device_id_type=MESH)

        mine, first, passed = [], [], []
        for a in range(na):
            stages[a][...] = ins[a][...].astype(dtypes[a])
            mine.append(pltpu.make_async_copy(stages[a], outs[a].at[blk(me)], local_sems.at[a]))
            mine[-1].start()
            first.append(copy(a, 0, me, sibling, src=stages[a]))
            first += [copy(a, 1 + j, me, (*chip, c), src=stages[a]) for j, chip in enumerate(chips)]
        for cp in first:
            cp.start()
        for j, chip in enumerate(chips):
            for a in range(na):
                copy(a, 1 + j, (*chip, c), me).wait_recv()
                passed.append(copy(a, 4 + j, (*chip, c), sibling))
                passed[-1].start()
        for a in range(na):
            copy(a, 0, sibling, me).wait_recv()
            for j, chip in enumerate(chips):
                copy(a, 4 + j, (*chip, 1 - c), me).wait_recv()
        for cp in first + passed:
            cp.wait_send()
        for cp in mine:
            cp.wait()

    return pl.pallas_call(
        body, name=name,
        in_specs=[pl.BlockSpec(memory_space=pltpu.VMEM)] * na, out_specs=[ANY] * na,
        out_shape=[SDS((NB,) + b.shape, dt) for b, dt in zip(blocks, dtypes)],
        scratch_shapes=[pltpu.VMEM(b.shape, dt) for b, dt in zip(blocks, dtypes)]
        + [pltpu.SemaphoreType.DMA((7 * na,)), pltpu.SemaphoreType.DMA((7 * na,)), pltpu.SemaphoreType.DMA((na,))],
        compiler_params=_params(40),
    )(*blocks)


HBM = pl.BlockSpec(memory_space=pltpu.HBM)
SEMS = pl.BlockSpec(memory_space=pltpu.SEMAPHORE)
EFFECT = pltpu.SideEffectType.DATAFLOW_SIDE_EFFECTING


def _chip_copies(srcs, lands, send_sems, recv_sems):
    x, y, c = _place()
    return [pltpu.make_async_remote_copy(
        src_ref=srcs[a].at[slot], dst_ref=lands[a].at[slot],
        send_sem=send_sems.at[3 * a + slot], recv_sem=recv_sems.at[3 * a + slot],
        device_id=(px, py, c), device_id_type=MESH)
        for a in range(len(srcs)) for slot, (px, py) in enumerate(_other_chips(x, y))]


def _split_start(name, copies, per_array, srcs, lands, after=None):
    na = len(srcs)

    def body(*refs):
        send_sems, recv_sems = refs[-2 * na - 3], refs[-2 * na - 2]
        for cp in copies(refs[:na], refs[na:2 * na], send_sems, recv_sems):
            cp.start()
        refs[-1][...] = jnp.zeros_like(refs[-1])

    hbm = lambda a: pltpu.HBM(a.shape, a.dtype)
    pin = lambda a: pltpu.with_memory_space_constraint(a, pltpu.HBM)
    out = pl.pallas_call(
        body, name=name,
        out_shape=(pltpu.SemaphoreType.DMA((per_array * na,)), pltpu.SemaphoreType.DMA((per_array * na,)),
                   *[hbm(a) for a in srcs], *[hbm(a) for a in lands], SDS((8, BD), F32)),
        in_specs=[HBM] * (2 * na) + ([] if after is None else [ANY]),
        out_specs=(SEMS, SEMS, *[HBM] * (2 * na), pl.BlockSpec(memory_space=pltpu.VMEM)),
        input_output_aliases={i: 2 + i for i in range(2 * na)},
        compiler_params=pltpu.CompilerParams(has_side_effects=EFFECT),
    )(*[pin(a) for a in srcs], *[pin(a) for a in lands], *([] if after is None else [after]))
    return out[0], out[1], out[2:2 + na], out[2 + na:2 + 2 * na], out[-1]


def _split_wait(name, copies, started, after):
    send_sems, recv_sems, srcs, lands, _ = started
    na = len(srcs)

    def body(*refs):
        waits = copies(refs[:na], refs[na:2 * na], refs[2 * na], refs[2 * na + 1])
        for cp in waits:
            cp.wait_send()
        for cp in waits:
            cp.wait_recv()

    hbm = lambda a: pltpu.HBM(a.shape, a.dtype)
    out = pl.pallas_call(
        body, name=name,
        out_shape=(*[hbm(a) for a in srcs], *[hbm(a) for a in lands]),
        in_specs=[HBM] * (2 * na) + [SEMS, SEMS, ANY],
        out_specs=tuple([HBM] * (2 * na)),
        input_output_aliases={i: i for i in range(2 * na)},
        compiler_params=pltpu.CompilerParams(has_side_effects=EFFECT),
    )(*srcs, *lands, send_sems, recv_sems, after)
    return out[na:]


def _add_sibling(place, g, a_in):
    _, r, cols = g.shape
    tr = _row_tile(r)

    def chip(k, pr):
        qx = pr[0] if k in (1, 3) else 1 - pr[0]
        qy = pr[1] if k in (0, 3) else 1 - pr[1]
        return 2 * qx + qy

    def body(place_ref, *refs):
        g_refs, a_refs, (out_ref, own_ref) = refs[0:4], refs[4:8], refs[8:10]
        for k in range(3):
            out_ref[k] = (g_refs[k][0] + a_refs[k][0].astype(F32)).astype(BF16)
        own_ref[...] = g_refs[3][0] + a_refs[3][0].astype(F32)

    mine = lambda k: pl.BlockSpec((1, tr, cols), lambda i, pr: (2 * chip(k, pr) + pr[2], i, 0))
    theirs = lambda k: pl.BlockSpec((1, tr, cols), lambda i, pr: (chip(k, pr), i, 0))
    return pl.pallas_call(
        body, name="add_sibling",
        grid_spec=pltpu.PrefetchScalarGridSpec(
            num_scalar_prefetch=1, grid=(r // tr,),
            in_specs=[mine(k) for k in range(4)] + [theirs(k) for k in range(4)],
            out_specs=[pl.BlockSpec((3, tr, cols), lambda i, pr: (0, i, 0)),
                       pl.BlockSpec((tr, cols), lambda i, pr: (i, 0))]),
        out_shape=[SDS((3, r, cols), BF16), SDS((r, cols), F32)], compiler_params=_params(48),
    )(place, *[g] * 4, *[a_in] * 4)


def _add_chips(own, b_in):
    r, cols = own.shape
    tr = _row_tile(r)

    def body(p_ref, b0_ref, b1_ref, b2_ref, o_ref):
        o_ref[...] = ((p_ref[...] + b0_ref[0].astype(F32)) + b1_ref[0].astype(F32)) + b2_ref[0].astype(F32)

    slot = lambda k: pl.BlockSpec((1, tr, cols), lambda i: (k, i, 0))
    spec = pl.BlockSpec((tr, cols), lambda i: (i, 0))
    return pl.pallas_call(
        body, name="add_chips", grid=(r // tr,), in_specs=[spec, slot(0), slot(1), slot(2)], out_specs=spec,
        out_shape=SDS((r, cols), F32), compiler_params=_params(32),
    )(own, b_in, b_in, b_in)


VEC_NAMES = ("b_merge", "conv_b", "rg_bx", "rg_ba", "rg_lambda", "hg_lb_logits", "hg_norm_g", "final_norm_g")
REP_NAMES = ("rg_wx", "rg_wa", "norm_g") + VEC_NAMES
SMALL_AT = 3 * BD
SMALL_ROWS = 48
MID_ROWS = 448


def _sum_blocks(parts):
    def body(p_ref, o_ref):
        acc = p_ref[0]
        for k in range(1, NB):
            acc = acc + p_ref[k]
        o_ref[...] = acc

    return pl.pallas_call(body, name="sum_blocks", out_shape=SDS(parts.shape[1:], F32))(parts)


def _pack_rows(arrays, width, row_multiple=8):
    flat = jnp.concatenate([a.reshape(-1) for a in arrays])
    rows = -(-flat.shape[0] // width)
    rows = -(-rows // row_multiple) * row_multiple
    return jnp.pad(flat, (0, rows * width - flat.shape[0])).reshape(rows, width)


def _unpack(flat, like):
    out, off = [], 0
    for a in like:
        out.append(flat[off:off + a.size].reshape(a.shape))
        off += a.size
    return out


def kernel(x, w_in, b_merge, conv_w, conv_b, rg_wx, rg_bx, rg_wa, rg_ba, rg_lambda, hg_lb_logits, hg_norm_g, proj_a, proj_b, w_out, norm_g, final_norm_g, loss_target, m_w_in, m_b_merge, m_conv_w, m_conv_b, m_rg_wx, m_rg_bx, m_rg_wa, m_rg_ba, m_rg_lambda, m_hg_lb_logits, m_hg_norm_g, m_proj_a, m_proj_b, m_w_out, m_norm_g, m_final_norm_g, v_w_in, v_b_merge, v_conv_w, v_conv_b, v_rg_wx, v_rg_bx, v_rg_wa, v_rg_ba, v_rg_lambda, v_hg_lb_logits, v_hg_norm_g, v_proj_a, v_proj_b, v_w_out, v_norm_g, v_final_norm_g):
    weights = dict(w_in=w_in, b_merge=b_merge, conv_w=conv_w, conv_b=conv_b, rg_wx=rg_wx, rg_bx=rg_bx, rg_wa=rg_wa,
                   rg_ba=rg_ba, rg_lambda=rg_lambda, hg_lb_logits=hg_lb_logits, hg_norm_g=hg_norm_g, proj_a=proj_a,
                   proj_b=proj_b, w_out=w_out, norm_g=norm_g, final_norm_g=final_norm_g)
    mom1 = dict(w_in=m_w_in, b_merge=m_b_merge, conv_w=m_conv_w, conv_b=m_conv_b, rg_wx=m_rg_wx, rg_bx=m_rg_bx,
                rg_wa=m_rg_wa, rg_ba=m_rg_ba, rg_lambda=m_rg_lambda, hg_lb_logits=m_hg_lb_logits,
                hg_norm_g=m_hg_norm_g, proj_a=m_proj_a, proj_b=m_proj_b, w_out=m_w_out, norm_g=m_norm_g,
                final_norm_g=m_final_norm_g)
    mom2 = dict(w_in=v_w_in, b_merge=v_b_merge, conv_w=v_conv_w, conv_b=v_conv_b, rg_wx=v_rg_wx, rg_bx=v_rg_bx,
                rg_wa=v_rg_wa, rg_ba=v_rg_ba, rg_lambda=v_rg_lambda, hg_lb_logits=v_hg_lb_logits,
                hg_norm_g=v_hg_norm_g, proj_a=v_proj_a, proj_b=v_proj_b, w_out=v_w_out, norm_g=v_norm_g,
                final_norm_g=v_final_norm_g)
    order = list(weights)
    nb, s_len, _ = x.shape
    n = nb * s_len
    px, py, pc = _place()
    place = jnp.stack([px, py, pc]).astype(jnp.int32)

    in_hbm = lambda a: pltpu.with_memory_space_constraint(a, pltpu.HBM)
    norm_gain = in_hbm(norm_g)

    x2 = x.reshape(n, D)
    cw_blk = jnp.pad(conv_w[0], ((0, 4), (0, 0)))
    order_ids = jnp.stack([_block_id(p) for p in _arrival_order(px, py, pc)]).astype(jnp.int32)
    z, h_all, w_all, pa_all, pb_all, wo_all, cw_all = _gather_inproj(
        order_ids, x2, norm_gain, [w_in[0], proj_a[0], proj_b[0], w_out[0], cw_blk], [BF16, BF16, BF16, BF16, F32])
    pa_full, pb_full, wo_full = (a.reshape(D, D) for a in (pa_all, pb_all, wo_all))
    cw8 = in_hbm(cw_all.transpose(1, 0, 2).reshape(8, D))
    wx_b, wa_b = in_hbm(rg_wx[0].astype(BF16)), in_hbm(rg_wa[0].astype(BF16))
    cb, bx, ba, lam = (in_hbm(a.reshape(1, D)) for a in (conv_b, rg_bx, rg_ba, rg_lambda))
    fin_g, b_mrg = in_hbm(final_norm_g.reshape(1, D)), in_hbm(b_merge)
    lb_lg, hg_g = in_hbm(hg_lb_logits), in_hbm(hg_norm_g)

    hlru, ya = _lru_fwd(z, cw8, cb, wx_b, wa_b, bx, ba, lam, nb, s_len)
    o_all, yb, st_all = _hgrn_fwd(z, lb_lg, hg_g, nb, s_len)

    (dx2, dya, dyb, dzm, loss_acc, g_fin, g_bm, g_mid) = _mid(
        ya, yb, z, b_mrg, x2, loss_target.reshape(n, D), fin_g, pa_full, pb_full, wo_full)
    dzb, g_lg, g_hg = _hgrn_bwd(z, o_all, st_all, dyb, lb_lg, hg_g, nb, s_len)
    dza, g_cw8, g_cb, g_wx, g_wa, g_bx, g_ba, g_lam = _lru_bwd(
        z, hlru, dya, cw8, cb, wx_b, wa_b, bx, ba, lam, nb, s_len)

    part = dict(b_merge=g_bm, conv_b=g_cb, rg_bx=g_bx, rg_ba=g_ba, rg_lambda=g_lam, hg_lb_logits=g_lg,
                hg_norm_g=g_hg, final_norm_g=g_fin)
    vec = _pack_rows([part[k] for k in VEC_NAMES], BD)
    vec = jnp.pad(vec, ((0, 16 * NB - vec.shape[0]), (0, 0))).reshape(NB, 2, D)
    rows8 = lambda a: jnp.pad(a, ((0, 0), (0, 8 - a.shape[1]), (0, 0)))
    small = jnp.concatenate([g_wx.reshape(NB, 16, D), g_wa.reshape(NB, 16, D),
                             rows8(g_cw8.reshape(8, NB, BD).transpose(1, 0, 2).reshape(NB, 1, D)), rows8(vec),
                             jnp.zeros((NB, MID_ROWS - SMALL_AT - SMALL_ROWS, D), F32)], axis=1)
    g_m = lax.dynamic_update_slice(g_mid, small, (0, SMALL_AT, 0))
    g_w, w_from_sibling, m_from_sibling = _inproj_bwd_w(dza, dzb, dzm, h_all, g_m)
    w_out_bf, w_own = _add_sibling(place, g_w, w_from_sibling)
    m_out_bf, m_own = _add_sibling(place, g_m, m_from_sibling)
    outgoing = [w_out_bf, m_out_bf]
    chip_sums = _split_start("rs_chips_start", _chip_copies, 3, outgoing, [lax.empty(a.shape, a.dtype) for a in outgoing])
    grad_x, g_ng = _inproj_bwd_x(dza, dzb, dzm, w_all, x2, dx2, norm_gain, chip_sums[-1])
    from_chips = _split_wait("rs_chips_wait", _chip_copies, chip_sums, grad_x)
    r_w = _add_chips(w_own, from_chips[0])
    r_m = _add_chips(m_own, from_chips[1])
    row = lax.broadcasted_iota(jnp.int32, (8, D), 0)
    mine = jnp.where(row == 0, g_ng, jnp.where(row == 1, loss_acc[0:1, 0:1], 0.0))
    tail = jnp.concatenate([r_m[SMALL_AT:SMALL_AT + SMALL_ROWS], mine], axis=0)
    (tail_all,) = _allgather([tail], [F32], "gather_small_grads")
    summed = _sum_blocks(tail_all[:, SMALL_ROWS:SMALL_ROWS + 8])

    grads = dict(w_in=r_w.reshape(1, D, D),
                 proj_a=r_m[0:BD].reshape(1, BD, D), proj_b=r_m[BD:2 * BD].reshape(1, BD, D),
                 w_out=r_m[2 * BD:3 * BD].reshape(1, BD, D),
                 conv_w=r_m[SMALL_AT + 32].reshape(8, BD)[0:4].reshape(1, 4, BD),
                 rg_wx=tail_all[:, 0:16].reshape(1, NB, BD, BD), rg_wa=tail_all[:, 16:32].reshape(1, NB, BD, BD),
                 norm_g=summed[0:1])
    vec_all = tail_all[:, 40:42].reshape(-1)
    for k, gk in zip(VEC_NAMES, _unpack(vec_all, [weights[k] for k in VEC_NAMES])):
        grads[k] = gk

    delta, new_m, new_v = {}, {}, {}
    flat2 = lambda a: a.reshape(-1, a.shape[-1])
    for k in ("w_in", "proj_a", "proj_b", "w_out"):
        outs = _adamw(*[flat2(t[k]) for t in (weights, grads, mom1, mom2)])
        delta[k], new_m[k], new_v[k] = (a.reshape(weights[k].shape) for a in outs)
    rep = list(REP_NAMES) + ["conv_w"]
    outs = _adamw_small(*[[flat2(t[k]) for k in rep] for t in (weights, grads, mom1, mom2)])
    for tgt, arrays in zip((delta, new_m, new_v), outs):
        for k, a in zip(rep, arrays):
            tgt[k] = a.reshape(weights[k].shape)

    return (summed[1, 0], grad_x.reshape(x.shape), *[grads[k] for k in order], *[delta[k] for k in order],
            *[new_m[k] for k in order], *[new_v[k] for k in order])
```

```python
import functools

import jax
import jax.numpy as jnp
from jax import lax
from jax.experimental import pallas as pl
from jax.experimental.pallas import tpu as pltpu

F32 = jnp.float32
BF16 = jnp.bfloat16
SDS = jax.ShapeDtypeStruct
MESH = pl.DeviceIdType.MESH
ANY = pl.BlockSpec(memory_space=pl.ANY)

D = 1024
NB = 8
BD = D // NB
CHUNK = 64
EPS = 1e-6
LRU_C = 8.0
HG_SCALE = BD ** -0.5
ADAM_LR, ADAM_B1, ADAM_B2, ADAM_EPS, ADAM_WD, ADAM_STEP = 0.001, 0.9, 0.999, 1e-08, 0.01, 10

NT_DIMS = (((1,), (1,)), ((), ()))
TN_DIMS = (((0,), (0,)), ((), ()))


def _params(vmem_mib):
    return pltpu.CompilerParams(vmem_limit_bytes=vmem_mib << 20)


def _row_tile(rows, most=256):
    assert rows % 8 == 0
    return max(t for t in range(8, min(rows, most) + 1, 8) if rows % t == 0)


def _sigmoid(v):
    return 0.5 * (jnp.tanh(0.5 * v) + 1.0)


def _groups(v):
    return v.reshape(v.shape[0] // 8, 8, v.shape[1])


def _softplus_neg(lam):
    t = -lam
    e = jnp.exp(-jnp.abs(t))
    w = 1.0 + e
    d = w - 1.0
    l1p = jnp.where(d == 0.0, e, jnp.log(w) * (e / jnp.where(d == 0.0, 1.0, d)))
    return jnp.maximum(t, 0.0) + l1p


def _place():
    return lax.axis_index("x"), lax.axis_index("y"), lax.axis_index("c")


def _other_chips(x, y):
    return [(1 - x, y), (x, 1 - y), (1 - x, 1 - y)]


def _block_id(p):
    return 4 * p[0] + 2 * p[1] + p[2]


def _core_chips(x, y, c):
    near, far, diag = _other_chips(x, y)
    pick = lambda a, b: (jnp.where(c == 0, a[0], b[0]), jnp.where(c == 0, a[1], b[1]))
    return [pick(near, far), pick(far, near), diag]


def _arrival_order(x, y, c):
    first, second, diag = _core_chips(x, y, c)
    return [(x, y, c), (x, y, 1 - c), (*first, c), (*second, 1 - c), (*second, c), (*first, 1 - c),
            (*diag, c), (*diag, 1 - c)]


def _prenorm(x2, norm_g):
    n = x2.shape[0]
    tm = min(n, 1024)

    def body(x_ref, g_ref, h_ref):
        xv = x_ref[...]
        r = lax.rsqrt(jnp.mean(xv * xv, axis=-1, keepdims=True) + EPS)
        h_ref[...] = ((xv * r) * g_ref[...]).astype(BF16)

    rows = pl.BlockSpec((tm, D), lambda i: (i, 0))
    return pl.pallas_call(
        body, name="prenorm", grid=(n // tm,), in_specs=[rows, pl.BlockSpec((1, D), lambda i: (0, 0))], out_specs=rows,
        out_shape=SDS((n, D), BF16), compiler_params=_params(32),
    )(x2, norm_g)


def _gather_inproj(order_ids, h_all, blocks, dtypes):
    na = len(blocks)
    n = h_all.shape[0]
    tm = min(n, 2048)
    ni = n // tm

    def body(order_ref, h_hbm, *refs):
        ins, z_ref, outs = refs[:na], refs[na], refs[na + 1:2 * na + 1]
        stages = refs[2 * na + 1:3 * na + 1]
        h_full, wbuf, send_sems, recv_sems, local_sems, wsems, hsem = refs[3 * na + 1:]
        j, i = pl.program_id(0), pl.program_id(1)
        x, y, c = _place()
        me, sibling = (x, y, c), (x, y, 1 - c)
        chips = _core_chips(x, y, c)
        sibling_chips = [chips[1], chips[0], chips[2]]
        small = range(1, na)

        def copy(a, k, block, to, src=None):
            return pltpu.make_async_remote_copy(
                src_ref=outs[a].at[_block_id(block)] if src is None else src, dst_ref=outs[a].at[_block_id(block)],
                send_sem=send_sems.at[7 * a + k], recv_sem=recv_sems.at[7 * a + k],
                device_id=to, device_id_type=MESH)

        def local(a):
            return pltpu.make_async_copy(stages[a], outs[a].at[_block_id(me)], local_sems.at[a])

        def landed(a, slot):
            copy(a, 1 + slot, (*chips[slot], c), me).wait_recv()
            copy(a, 4 + slot, (*chips[slot], c), sibling).start()
            if slot == 0:
                copy(a, 3, (*chips[0], c), (*chips[1], c)).start()

        def diagonal_and_small():
            landed(0, 2)
            for a in small:
                landed(a, 0)
                landed(a, 1)

        def passed_on(a, slot):
            copy(a, 4 + slot, (*sibling_chips[slot], 1 - c), me).wait_recv()

        def sibling_here_send_second():
            copy(0, 0, sibling, me).wait_recv()
            for a in range(na):
                copy(a, 2, me, (*chips[1], c), src=stages[a]).start()

        @pl.when((j == 0) & (i == 0))
        def _():
            for a in range(na):
                stages[a][...] = ins[a][...].astype(dtypes[a])
                local(a).start()
            for a in range(na):
                copy(a, 0, me, sibling, src=stages[a]).start()
                copy(a, 1, me, (*chips[0], c), src=stages[a]).start()

            load_h = pltpu.make_async_copy(h_hbm, h_full, hsem)
            load_h.start()
            load_h.wait()

        steps = [
            lambda: local(0).wait(),
            sibling_here_send_second,
            lambda: landed(0, 0),
            lambda: passed_on(0, 0),
            lambda: landed(0, 1),
            lambda: passed_on(0, 1),
            diagonal_and_small,
            lambda: passed_on(0, 2),
        ]
        def w_load(k):
            return pltpu.make_async_copy(outs[0].at[order_ref[k]], wbuf.at[k % 2], wsems.at[k % 2])

        for k, step in enumerate(steps):
            @pl.when((j == 0) & (i == 0) if k == 0 else (j == k - 1) & (i == ni - 1))
            def _(k=k, step=step):
                step()
                w_load(k).start()

        pl.when(i == 0)(lambda: w_load(j).wait())
        z_ref[0] = jnp.dot(h_full[pl.ds(pl.multiple_of(i * tm, tm), tm), :], wbuf[j % 2], preferred_element_type=F32)

        @pl.when((j == NB - 1) & (i == ni - 1))
        def _():
            for a in small:
                landed(a, 2)
            for a in small:
                local(a).wait()
                copy(a, 0, sibling, me).wait_recv()
                for slot in range(3):
                    passed_on(a, slot)
            for a in range(na):
                copy(a, 0, me, sibling, src=stages[a]).wait_send()
                for slot, chip in enumerate(chips):
                    copy(a, 1 + slot, me, (*chip, c), src=stages[a]).wait_send()
                    copy(a, 4 + slot, (*chip, c), sibling).wait_send()

    vmem = pl.BlockSpec(memory_space=pltpu.VMEM)
    return pl.pallas_call(
        body, name="gather_inproj",
        grid_spec=pltpu.PrefetchScalarGridSpec(
            num_scalar_prefetch=1, grid=(NB, ni),
            in_specs=[ANY] + [vmem] * na,
            out_specs=[pl.BlockSpec((1, tm, D), lambda j, i, order: (order[j], i, 0))] + [ANY] * na,
            scratch_shapes=[pltpu.VMEM(b.shape, dt) for b, dt in zip(blocks, dtypes)]
            + [pltpu.VMEM((n, D), BF16), pltpu.VMEM((2, D, D), BF16),
               pltpu.SemaphoreType.DMA((7 * na,)), pltpu.SemaphoreType.DMA((7 * na,)),
               pltpu.SemaphoreType.DMA((na,)), pltpu.SemaphoreType.DMA((2,)), pltpu.SemaphoreType.DMA(())]),
        out_shape=[SDS((NB, n, D), F32)] + [SDS((NB,) + b.shape, dt) for b, dt in zip(blocks, dtypes)],
        compiler_params=_params(56),
    )(order_ids, h_all, *blocks)


LRU_T = 256


def _shifted(groups, shifts):
    row = lax.broadcasted_iota(jnp.int32, (groups.shape[0] - 1,) + groups.shape[1:], 1)
    out = []
    for s in shifts:
        y = pltpu.roll(groups, s % 8, 1)
        moved = jnp.where(row >= s, y[1:], y[:-1]) if s > 0 else jnp.where(row < 8 + s, y[:-1], y[1:])
        out.append(moved.reshape(-1, groups.shape[2]))
    return out


def _conv(taps, cw, cb):
    acc = taps[0] * cw[0:1, :] + taps[1] * cw[1:2, :]
    acc = acc + taps[2] * cw[2:3, :]
    acc = acc + taps[3] * cw[3:4, :]
    return cb + acc


def _lru_gates(xa, wx_ref, wa_ref, bx, ba, lam):
    xab = xa.astype(BF16)
    pis, prs = [], []
    for h in range(NB):
        xs = xab[:, h * BD:(h + 1) * BD]
        pis.append(jnp.dot(xs, wx_ref[h], preferred_element_type=F32))
        prs.append(jnp.dot(xs, wa_ref[h], preferred_element_type=F32))
    gi = _sigmoid(jnp.concatenate(pis, axis=1) + bx)
    gr = _sigmoid(jnp.concatenate(prs, axis=1) + ba)
    sp = _softplus_neg(lam)
    log_a = (-LRU_C * gr) * sp
    a = jnp.exp(log_a)
    mult = jnp.sqrt(-jnp.tanh(log_a) * (a * a + 1.0))
    return xab, gi, gr, sp, a, mult


def _lru_fwd(z, cw8, cb, wx, wa, bx, ba, lam, nb, s_len):
    n = nb * s_len
    t = LRU_T
    ns = s_len // t

    def body(xp_ref, ga_ref, cw_ref, cb_ref, wx_ref, wa_ref, bx_ref, ba_ref, lam_ref,
             h_ref, ya_ref, ext, a_s, u_s, carry):
        @pl.when(pl.program_id(1) == 0)
        def _():
            ext[0:8, :] = jnp.zeros((8, D), F32)
            carry[...] = jnp.zeros((8, D), F32)

        xp = xp_ref[0]
        ext[8:8 + t, :] = xp
        xa = _conv(_shifted(_groups(ext[...]), (3, 2, 1)) + [xp], cw_ref[...], cb_ref[...])
        ext[0:8, :] = xp[t - 8:t, :]
        _, gi, _, _, a, mult = _lru_gates(xa, wx_ref, wa_ref, bx_ref[...], ba_ref[...], lam_ref[...])
        u = (mult * gi) * xa
        a, u = _groups(a), _groups(u)
        row = lax.broadcasted_iota(jnp.int32, a.shape, 1)
        for sh in (1, 2, 4):
            a_sh = pltpu.roll(a, sh, 1)
            u_sh = pltpu.roll(u, sh, 1)
            m = row >= sh
            u = jnp.where(m, a * u_sh + u, u)
            a = jnp.where(m, a * a_sh, a)
        a_s[...] = a.reshape(t, D)
        u_s[...] = u.reshape(t, D)

        def step(g, c):
            r = pl.multiple_of(g * 8, 8)
            hg = u_s[pl.ds(r, 8), :] + a_s[pl.ds(r, 8), :] * c
            h_ref[pl.ds(r, 8), :] = hg
            return hg[7:8, :]

        c_out = lax.fori_loop(0, t // 8, step, carry[0:1, :], unroll=4)
        carry[0:1, :] = c_out
        ga = ga_ref[0]
        ya_ref[...] = (h_ref[...] * (ga * _sigmoid(ga))).astype(BF16)

    row_map = lambda b, s: (b * ns + s, 0)
    rep2 = lambda b, s: (0, 0)
    rep3 = lambda b, s: (0, 0, 0)
    return pl.pallas_call(
        body, name="lru_fwd", grid=(nb, ns),
        in_specs=[pl.BlockSpec((1, t, D), lambda b, s: (0, b * ns + s, 0)),
                  pl.BlockSpec((1, t, D), lambda b, s: (1, b * ns + s, 0)),
                  pl.BlockSpec((8, D), rep2), pl.BlockSpec((1, D), rep2),
                  pl.BlockSpec((NB, BD, BD), rep3), pl.BlockSpec((NB, BD, BD), rep3),
                  pl.BlockSpec((1, D), rep2), pl.BlockSpec((1, D), rep2), pl.BlockSpec((1, D), rep2)],
        out_specs=[pl.BlockSpec((t, D), row_map), pl.BlockSpec((t, D), row_map)],
        out_shape=[SDS((n, D), F32), SDS((n, D), BF16)],
        scratch_shapes=[pltpu.VMEM((t + 8, D), F32), pltpu.VMEM((t, D), F32), pltpu.VMEM((t, D), F32),
                        pltpu.VMEM((8, D), F32)],
        compiler_params=_params(48),
    )(z, z, cw8, cb, wx, wa, bx, ba, lam)


def _lru_bwd(z, h_all, dya, cw8, cb, wx, wa, bx, ba, lam, nb, s_len):
    n = nb * s_len
    t = LRU_T
    ns = s_len // t
    t8 = t // 8

    def body(xp_ref, xph_ref, ga_ref, h_ref, hh_ref, dya_ref, cw_ref, cb_ref, wx_ref, wa_ref, bx_ref, ba_ref,
             lam_ref, dz_ref, gcw_ref, gcb_ref, gwx_ref, gwa_ref, gbx_ref, gba_ref, glam_ref,
             ext, hext, dext, a_s, u_s, dh_s, carry):
        b, s = pl.program_id(0), pl.program_id(1)
        first_tile = s == ns - 1

        @pl.when((b == 0) & (s == 0))
        def _():
            for ref in (gcw_ref, gcb_ref, gwx_ref, gwa_ref, gbx_ref, gba_ref, glam_ref):
                ref[...] = jnp.zeros(ref.shape, F32)

        @pl.when(s == 0)
        def _():
            dext[t:t + 8, :] = jnp.zeros((8, D), F32)
            carry[...] = jnp.zeros((8, D), F32)

        keep = jnp.where(first_tile, 0.0, 1.0)
        xp = xp_ref[0]
        ext[0:8, :] = xph_ref[0] * keep
        ext[8:8 + t, :] = xp
        hext[0:8, :] = hh_ref[...] * keep
        hext[8:8 + t, :] = h_ref[...]
        cw = cw_ref[...]
        lam = lam_ref[...]
        taps = _shifted(_groups(ext[...]), (3, 2, 1)) + [xp]
        xa = _conv(taps, cw, cb_ref[...])
        xab, gi, gr, sp, a, mult = _lru_gates(xa, wx_ref, wa_ref, bx_ref[...], ba_ref[...], lam)
        (h_prev,) = _shifted(_groups(hext[...]), (1,))
        ga = ga_ref[0]
        sg = _sigmoid(ga)
        dya_v = dya_ref[...]
        d_ga = dya_v * h_ref[...] * (sg * (1.0 + ga * (1.0 - sg)))
        g_in = dya_v * (ga * sg)

        (an,) = _shifted(jnp.concatenate([_groups(a), jnp.ones((1, 8, D), F32)], axis=0), (-1,))
        an, u = _groups(an), _groups(g_in)
        row = lax.broadcasted_iota(jnp.int32, an.shape, 1)
        for sh in (1, 2, 4):
            a_sh = pltpu.roll(an, 8 - sh, 1)
            u_sh = pltpu.roll(u, 8 - sh, 1)
            m = row < 8 - sh
            u = jnp.where(m, u + an * u_sh, u)
            an = jnp.where(m, an * a_sh, an)
        a_s[...] = an.reshape(t, D)
        u_s[...] = u.reshape(t, D)

        def step(i, c):
            r = pl.multiple_of((t8 - 1 - i) * 8, 8)
            dg = u_s[pl.ds(r, 8), :] + a_s[pl.ds(r, 8), :] * c
            dh_s[pl.ds(r, 8), :] = dg
            return dg[0:1, :]

        lax.fori_loop(0, t8, step, carry[0:1, :], unroll=4)
        dh = dh_s[...]
        carry[0:1, :] = a[0:1, :] * dh[0:1, :]

        d_a = dh * h_prev
        dux = dh * xa
        d_mult = dux * gi
        d_gi = dux * mult
        d_xa = dh * (mult * gi)
        d_loga = d_a * a - d_mult * ((a * a) / mult)
        d_gr = d_loga * (-LRU_C * sp)
        d_sp = jnp.sum(d_loga * (-LRU_C * gr), axis=0, keepdims=True)
        glam_ref[...] += d_sp * (-_sigmoid(-lam))
        d_pi = d_gi * gi * (1.0 - gi)
        d_pr = d_gr * gr * (1.0 - gr)
        gbx_ref[...] += jnp.sum(d_pi, axis=0, keepdims=True)
        gba_ref[...] += jnp.sum(d_pr, axis=0, keepdims=True)
        dpib = d_pi.astype(BF16)
        dprb = d_pr.astype(BF16)
        back = []
        for h in range(NB):
            cs = slice(h * BD, (h + 1) * BD)
            gwx_ref[h] += lax.dot_general(xab[:, cs], dpib[:, cs], TN_DIMS, preferred_element_type=F32)
            gwa_ref[h] += lax.dot_general(xab[:, cs], dprb[:, cs], TN_DIMS, preferred_element_type=F32)
            back.append(lax.dot_general(dpib[:, cs], wx_ref[h], NT_DIMS, preferred_element_type=F32)
                        + lax.dot_general(dprb[:, cs], wa_ref[h], NT_DIMS, preferred_element_type=F32))
        d_xa = d_xa + jnp.concatenate(back, axis=1)

        dext[0:t, :] = d_xa
        later = _shifted(_groups(dext[...]), (-3, -2, -1))
        d_xp = later[0] * cw[0:1, :] + later[1] * cw[1:2, :]
        d_xp = d_xp + later[2] * cw[2:3, :]
        d_xp = d_xp + d_xa * cw[3:4, :]
        dext[t:t + 8, :] = d_xa[0:8, :]
        gcb_ref[...] += jnp.sum(d_xa, axis=0, keepdims=True)
        for k in range(4):
            gcw_ref[k:k + 1, :] += jnp.sum(d_xa * taps[k], axis=0, keepdims=True)
        dz_ref[0] = d_xp.astype(BF16)
        dz_ref[1] = d_ga.astype(BF16)

    rb = lambda b, s: b * ns + (ns - 1 - s)
    halo = lambda b, s: jnp.maximum(rb(b, s) * t8 - 1, 0)
    rep2 = lambda b, s: (0, 0)
    rep3 = lambda b, s: (0, 0, 0)
    return pl.pallas_call(
        body, name="lru_bwd", grid=(nb, ns),
        in_specs=[pl.BlockSpec((1, t, D), lambda b, s: (0, rb(b, s), 0)),
                  pl.BlockSpec((1, 8, D), lambda b, s: (0, halo(b, s), 0)),
                  pl.BlockSpec((1, t, D), lambda b, s: (1, rb(b, s), 0)),
                  pl.BlockSpec((t, D), lambda b, s: (rb(b, s), 0)),
                  pl.BlockSpec((8, D), lambda b, s: (halo(b, s), 0)),
                  pl.BlockSpec((t, D), lambda b, s: (rb(b, s), 0)),
                  pl.BlockSpec((8, D), rep2), pl.BlockSpec((1, D), rep2),
                  pl.BlockSpec((NB, BD, BD), rep3), pl.BlockSpec((NB, BD, BD), rep3),
                  pl.BlockSpec((1, D), rep2), pl.BlockSpec((1, D), rep2), pl.BlockSpec((1, D), rep2)],
        out_specs=[pl.BlockSpec((2, t, D), lambda b, s: (0, rb(b, s), 0)),
                   pl.BlockSpec((8, D), rep2), pl.BlockSpec((1, D), rep2),
                   pl.BlockSpec((NB, BD, BD), rep3), pl.BlockSpec((NB, BD, BD), rep3),
                   pl.BlockSpec((1, D), rep2), pl.BlockSpec((1, D), rep2), pl.BlockSpec((1, D), rep2)],
        out_shape=[SDS((2, n, D), BF16), SDS((8, D), F32), SDS((1, D), F32),
                   SDS((NB, BD, BD), F32), SDS((NB, BD, BD), F32),
                   SDS((1, D), F32), SDS((1, D), F32), SDS((1, D), F32)],
        scratch_shapes=[pltpu.VMEM((t + 8, D), F32), pltpu.VMEM((t + 8, D), F32), pltpu.VMEM((t + 8, D), F32),
                        pltpu.VMEM((t, D), F32), pltpu.VMEM((t, D), F32), pltpu.VMEM((t, D), F32),
                        pltpu.VMEM((8, D), F32)],
        compiler_params=_params(56),
    )(z, z, z, h_all, h_all, dya, cw8, cb, wx, wa, bx, ba, lam)


HG_T = 512
HG_NC = HG_T // CHUNK
BNT_DIMS = (((2,), (2,)), ((0,), (0,)))
BNN_DIMS = (((2,), (1,)), ((0,), (0,)))
BTN_DIMS = (((1,), (1,)), ((0,), (0,)))


def _lower_bound(lg):
    m = jnp.max(lg, axis=0, keepdims=True)
    e = jnp.exp(lg - m)
    return e[0:1, :] / jnp.sum(e, axis=0, keepdims=True)


def _tri(upper):
    r = lax.broadcasted_iota(jnp.int32, (HG_NC, CHUNK, CHUNK), 1)
    c = lax.broadcasted_iota(jnp.int32, (HG_NC, CHUNK, CHUNK), 2)
    return (c >= r) if upper else (r >= c)


def _bdot(a, b, dims):
    return lax.dot_general(a, b, dims, preferred_element_type=F32)


def _tri_sums(upper, a):
    tri = _tri(upper).astype(BF16)
    a1 = a.astype(BF16)
    r1 = a - a1.astype(F32)
    a2 = r1.astype(BF16)
    a3 = (r1 - a2.astype(F32)).astype(BF16)
    return _bdot(tri, a1, BNN_DIMS) + (_bdot(tri, a2, BNN_DIMS) + _bdot(tri, a3, BNN_DIMS))


def _chunks(a):
    return a.reshape(HG_NC, CHUNK, BD)


def _hg_tile(q, fp, lb):
    q, fp = _chunks(q), _chunks(fp)
    sig = _sigmoid(fp)
    f = lb + (1.0 - lb) * sig
    log_f = jnp.log(f)
    k = 1.0 - f
    b = _tri_sums(False, log_f)
    b_mid = b[:, CHUNK // 2:CHUNK // 2 + 1, :]
    b_last = b[:, CHUNK - 1:CHUNK, :]
    sq = _sigmoid(q)
    qh = q * sq
    e_qi = jnp.exp(b - b_mid)
    e_ki = jnp.exp(b_mid - b)
    e_qs = jnp.exp(b)
    e_ks = jnp.exp(b_last - b)
    dc = jnp.exp(b_last)
    q_in = (qh * e_qi) * HG_SCALE
    k_in = k * e_ki
    q_st = (qh * e_qs) * HG_SCALE
    k_st = k * e_ks
    att = _bdot(q_in.astype(BF16), k_in.astype(BF16), BNT_DIMS)
    att = jnp.where(_tri(False), att, 0.0)
    return dict(q=q, sig=sig, f=f, k=k, sq=sq, e_qi=e_qi, e_ki=e_ki, e_qs=e_qs, e_ks=e_ks, dc=dc,
                q_in=q_in, k_in=k_in, q_st=q_st, k_st=k_st, att=att)


def _hgrn_fwd(z, lb_logits, hg_g, nb, s_len):
    n = nb * s_len
    t = HG_T
    ns = s_len // t
    nchunk = s_len // CHUNK

    def body(q_ref, f_ref, v_ref, gb_ref, lg_ref, g_ref, o_ref, yb_ref, st_ref, st):
        @pl.when(pl.program_id(1) == 0)
        def _():
            st[...] = jnp.zeros((NB, BD, BD), F32)

        def head(h, carry):
            cols = pl.ds(pl.multiple_of(h * BD, BD), BD)
            lb = _lower_bound(lg_ref[:, cols])
            ck = _hg_tile(q_ref[0, :, cols], f_ref[0, :, cols], lb)
            vb = _chunks(v_ref[0, :, cols]).astype(BF16)
            kv = _bdot(vb, ck["k_st"].astype(BF16), BTN_DIMS)
            states = [st[h]]
            for c in range(HG_NC):
                states.append(states[c] * ck["dc"][c] + kv[c])
            st[h] = states[HG_NC]
            s_in = jnp.stack(states[:HG_NC], axis=0)
            st_ref[h] = s_in
            o = (_bdot(ck["att"].astype(BF16), vb, BNN_DIMS)
                 + _bdot(ck["q_st"].astype(BF16), s_in.astype(BF16), BNT_DIMS))
            o_ref[:, cols] = o.reshape(t, BD)
            r = lax.rsqrt(jnp.mean(o * o, axis=-1, keepdims=True) + EPS)
            gb = _chunks(gb_ref[0, :, cols])
            yb_ref[:, cols] = (((o * r) * g_ref[...]) * (gb * _sigmoid(gb))).astype(BF16).reshape(t, BD)
            return carry

        lax.fori_loop(0, NB, head, 0, unroll=4)

    seg = lambda j: pl.BlockSpec((1, t, D), lambda b, s: (j, b * ns + s, 0))
    tile = pl.BlockSpec((t, D), lambda b, s: (b * ns + s, 0))
    return pl.pallas_call(
        body, name="hgrn_fwd", grid=(nb, ns),
        in_specs=[seg(2), seg(3), seg(4), seg(5),
                  pl.BlockSpec((2, D), lambda b, s: (0, 0)), pl.BlockSpec((1, BD), lambda b, s: (0, 0))],
        out_specs=[tile, tile, pl.BlockSpec((NB, HG_NC, BD, BD), lambda b, s: (b, s, 0, 0))],
        out_shape=[SDS((n, D), F32), SDS((n, D), BF16), SDS((nb * NB, nchunk, BD, BD), F32)],
        scratch_shapes=[pltpu.VMEM((NB, BD, BD), F32)],
        compiler_params=_params(56),
    )(z, z, z, z, lb_logits, hg_g)


def _hgrn_bwd(z, o_all, st_all, dyb, lb_logits, hg_g, nb, s_len):
    n = nb * s_len
    t = HG_T
    ns = s_len // t

    def body(q_ref, f_ref, v_ref, gb_ref, o_ref, st_ref, dyb_ref, lg_ref, g_ref,
             dz_ref, glg_ref, ghg_ref, dst, dlb):
        b, s = pl.program_id(0), pl.program_id(1)

        @pl.when((b == 0) & (s == 0))
        def _():
            ghg_ref[...] = jnp.zeros((1, BD), F32)
            dlb[...] = jnp.zeros((8, D), F32)

        @pl.when(s == 0)
        def _():
            dst[...] = jnp.zeros((NB, BD, BD), F32)

        g = g_ref[...]

        def head(h, carry):
            cols = pl.ds(pl.multiple_of(h * BD, BD), BD)
            lb = _lower_bound(lg_ref[:, cols])
            ck = _hg_tile(q_ref[0, :, cols], f_ref[0, :, cols], lb)
            q = ck["q"]
            vb = _chunks(v_ref[0, :, cols]).astype(BF16)
            gb = _chunks(gb_ref[0, :, cols])
            o = _chunks(o_ref[:, cols])
            dyb_v = _chunks(dyb_ref[:, cols])
            s_in = st_ref[h]

            sgb = _sigmoid(gb)
            r = lax.rsqrt(jnp.mean(o * o, axis=-1, keepdims=True) + EPS)
            ohat = o * r
            d_on = dyb_v * (gb * sgb)
            d_gb = dyb_v * (ohat * g) * (sgb * (1.0 + gb * (1.0 - sgb)))
            ghg_ref[...] += jnp.sum(jnp.sum(d_on * ohat, axis=1), axis=0, keepdims=True)
            tt = d_on * g
            d_o = r * (tt - ohat * jnp.mean(tt * ohat, axis=-1, keepdims=True))
            dob = d_o.astype(BF16)

            attb = ck["att"].astype(BF16)
            q_inb, k_inb = ck["q_in"].astype(BF16), ck["k_in"].astype(BF16)
            q_stb, k_stb = ck["q_st"].astype(BF16), ck["k_st"].astype(BF16)
            d_att = jnp.where(_tri(False), _bdot(dob, vb, BNT_DIMS), 0.0).astype(BF16)
            d_q_in = _bdot(d_att, k_inb, BNN_DIMS)
            d_k_in = _bdot(d_att, q_inb, BTN_DIMS)
            d_q_st = _bdot(dob, s_in.astype(BF16), BNN_DIMS)
            qdo = _bdot(dob, q_stb, BTN_DIMS)
            d_states = [None] * HG_NC + [dst[h]]
            for c in reversed(range(HG_NC)):
                d_states[c] = d_states[c + 1] * ck["dc"][c] + qdo[c]
            dst[h] = d_states[0]
            ds_out = jnp.stack(d_states[1:], axis=0)
            dsb = ds_out.astype(BF16)
            d_v = _bdot(attb, dob, BTN_DIMS) + _bdot(k_stb, dsb, BNT_DIMS)
            d_k_st = _bdot(vb, dsb, BNN_DIMS)
            d_dc = jnp.sum(ds_out * s_in, axis=1, keepdims=True)

            p_qi = d_q_in * ck["q_in"]
            p_ki = d_k_in * ck["k_in"]
            p_qs = d_q_st * ck["q_st"]
            p_ks = d_k_st * ck["k_st"]
            d_qh = (d_q_in * ck["e_qi"] + d_q_st * ck["e_qs"]) * HG_SCALE
            d_k = d_k_in * ck["e_ki"] + d_k_st * ck["e_ks"]
            d_b = (p_qi - p_ki) + (p_qs - p_ks)
            d_b_mid = jnp.sum(p_ki - p_qi, axis=1, keepdims=True)
            d_b_last = jnp.sum(p_ks, axis=1, keepdims=True) + d_dc * ck["dc"]
            rowi = lax.broadcasted_iota(jnp.int32, (HG_NC, CHUNK, BD), 1)
            d_b = d_b + jnp.where(rowi == CHUNK // 2, d_b_mid, 0.0) + jnp.where(rowi == CHUNK - 1, d_b_last, 0.0)
            d_logf = _tri_sums(True, d_b)
            d_f = d_logf / ck["f"] - d_k
            sig, sq = ck["sig"], ck["sq"]
            d_fp = d_f * (1.0 - lb) * (sig * (1.0 - sig))
            dlb[0:1, cols] += jnp.sum(jnp.sum(d_f * (1.0 - sig), axis=1), axis=0, keepdims=True)
            d_q = d_qh * (sq * (1.0 + q * (1.0 - sq)))
            dz_ref[0, :, cols] = d_q.astype(BF16).reshape(t, BD)
            dz_ref[1, :, cols] = d_fp.astype(BF16).reshape(t, BD)
            dz_ref[2, :, cols] = d_v.astype(BF16).reshape(t, BD)
            dz_ref[3, :, cols] = d_gb.astype(BF16).reshape(t, BD)
            return carry

        lax.fori_loop(0, NB, head, 0, unroll=2)

        @pl.when((b == nb - 1) & (s == ns - 1))
        def _():
            lb = _lower_bound(lg_ref[...])
            dl = dlb[0:1, :] * (lb * (1.0 - lb))
            glg_ref[0:1, :] = dl
            glg_ref[1:2, :] = -dl

    rb = lambda b, s: b * ns + (ns - 1 - s)
    seg = lambda j: pl.BlockSpec((1, t, D), lambda b, s: (j, rb(b, s), 0))
    tile = pl.BlockSpec((t, D), lambda b, s: (rb(b, s), 0))
    return pl.pallas_call(
        body, name="hgrn_bwd", grid=(nb, ns),
        in_specs=[seg(2), seg(3), seg(4), seg(5), tile,
                  pl.BlockSpec((NB, HG_NC, BD, BD), lambda b, s: (b, ns - 1 - s, 0, 0)),
                  tile, pl.BlockSpec((2, D), lambda b, s: (0, 0)), pl.BlockSpec((1, BD), lambda b, s: (0, 0))],
        out_specs=[pl.BlockSpec((4, t, D), lambda b, s: (0, rb(b, s), 0)),
                   pl.BlockSpec((2, D), lambda b, s: (0, 0)), pl.BlockSpec((1, BD), lambda b, s: (0, 0))],
        out_shape=[SDS((4, n, D), BF16), SDS((2, D), F32), SDS((1, BD), F32)],
        scratch_shapes=[pltpu.VMEM((NB, BD, BD), F32), pltpu.VMEM((8, D), F32)],
        compiler_params=_params(60),
    )(z, z, z, z, o_all, st_all, dyb, lb_logits, hg_g)


def _mid(ya, yb, z, b_merge, x2, tgt, fin_g, pa, pb, wo):
    n = x2.shape[0]
    tm = 256
    ni = n // tm

    def body(ya_ref, yb_ref, gma_ref, gmb_ref, bm_ref, x_ref, t_ref, fg_ref, pa_hbm, pb_hbm, wo_hbm,
             dx2_ref, dya_ref, dyb_ref, dgm_ref, loss_ref, gfg_ref, gbm_ref, gm_hbm,
             pa_v, pb_v, wo_v, gpa_v, gpb_v, gwo_v, sem):
        i = pl.program_id(0)
        by_owner = lambda g: g.reshape(NB, BD, D)
        loads = [pltpu.make_async_copy(src, dst, sem.at[k])
                 for k, (src, dst) in enumerate(((pa_hbm, pa_v), (pb_hbm, pb_v), (wo_hbm, wo_v)))]
        stores = [pltpu.make_async_copy(src, dst, sem.at[k])
                  for k, (src, dst) in enumerate((g, gm_hbm.at[:, pl.ds(slot * BD, BD), :])
                                                 for slot, g in enumerate((gpa_v, gpb_v, gwo_v)))]

        @pl.when(i == 0)
        def _():
            for cp in loads:
                cp.start()
            for ref in (gpa_v, gpb_v, gwo_v, loss_ref, gfg_ref, gbm_ref):
                ref[...] = jnp.zeros(ref.shape, F32)
            for cp in loads:
                cp.wait()

        ya_v = ya_ref[...]
        yb_v = yb_ref[...]
        out_a = jnp.dot(ya_v, pa_v[...], preferred_element_type=F32)
        out_b = jnp.dot(yb_v, pb_v[...], preferred_element_type=F32)
        bm = bm_ref[...]
        g_a = _sigmoid(gma_ref[0] + bm[:, 0:D])
        g_b = _sigmoid(gmb_ref[0] + bm[:, D:2 * D])
        mixed = g_a * out_a + g_b * out_b
        mixb = mixed.astype(BF16)
        xo = x_ref[...] + jnp.dot(mixb, wo_v[...], preferred_element_type=F32)
        r = lax.rsqrt(jnp.mean(xo * xo, axis=-1, keepdims=True) + EPS)
        xn = xo * r
        fg = fg_ref[...]
        e = xn * fg - t_ref[...]
        loss_ref[...] += 0.5 * jnp.sum(jnp.mean(e * e, axis=-1, keepdims=True))
        dy = e * (1.0 / D)
        gfg_ref[...] += jnp.sum(dy * xn, axis=0, keepdims=True)
        dxn = dy * fg
        dx2 = r * (dxn - xn * jnp.mean(dxn * xn, axis=-1, keepdims=True))
        dx2_ref[...] = dx2
        dx2b = dx2.astype(BF16)
        d_mixed = lax.dot_general(dx2b, wo_v[...], NT_DIMS, preferred_element_type=F32)
        gwo_v[...] += by_owner(lax.dot_general(mixb, dx2b, TN_DIMS, preferred_element_type=F32))
        d_oa = (d_mixed * g_a).astype(BF16)
        d_ob = (d_mixed * g_b).astype(BF16)
        dgm_a = (d_mixed * out_a) * (g_a * (1.0 - g_a))
        dgm_b = (d_mixed * out_b) * (g_b * (1.0 - g_b))
        gbm_ref[:, 0:D] += jnp.sum(dgm_a, axis=0, keepdims=True)
        gbm_ref[:, D:2 * D] += jnp.sum(dgm_b, axis=0, keepdims=True)
        dgm_ref[0] = dgm_a.astype(BF16)
        dgm_ref[1] = dgm_b.astype(BF16)
        dya_ref[...] = lax.dot_general(d_oa, pa_v[...], NT_DIMS, preferred_element_type=F32)
        dyb_ref[...] = lax.dot_general(d_ob, pb_v[...], NT_DIMS, preferred_element_type=F32)
        gpa_v[...] += by_owner(lax.dot_general(ya_v, d_oa, TN_DIMS, preferred_element_type=F32))
        gpb_v[...] += by_owner(lax.dot_general(yb_v, d_ob, TN_DIMS, preferred_element_type=F32))

        @pl.when(i == ni - 1)
        def _():
            for cp in stores:
                cp.start()
            for cp in stores:
                cp.wait()

    rows = pl.BlockSpec((tm, D), lambda i: (i, 0))
    rep = lambda shape: pl.BlockSpec(shape, lambda i: (0,) * len(shape))
    return pl.pallas_call(
        body, name="mid", grid=(ni,),
        in_specs=[rows, rows,
                  pl.BlockSpec((1, tm, D), lambda i: (6, i, 0)), pl.BlockSpec((1, tm, D), lambda i: (7, i, 0)),
                  rep((1, 2 * D)), rows, rows, rep((1, D)), ANY, ANY, ANY],
        out_specs=[rows, rows, rows, pl.BlockSpec((2, tm, D), lambda i: (0, i, 0)),
                   rep((8, BD)), rep((1, D)), rep((1, 2 * D)), ANY],
        out_shape=[SDS((n, D), F32), SDS((n, D), F32), SDS((n, D), F32), SDS((2, n, D), BF16),
                   SDS((8, BD), F32), SDS((1, D), F32), SDS((1, 2 * D), F32),
                   SDS((NB, MID_ROWS, D), F32)],
        scratch_shapes=[pltpu.VMEM((D, D), BF16)] * 3 + [pltpu.VMEM((NB, BD, D), F32)] * 3 + [pltpu.SemaphoreType.DMA((3,))],
        compiler_params=_params(60),
    )(ya, yb, z, z, b_merge, x2, tgt, fin_g, pa, pb, wo)


def _dz_specs(tm, ni, row_major):
    if row_major:
        ia = lambda i, j: (jnp.minimum(j, 1), i, 0)
        ib = lambda i, j: (jnp.clip(j - 2, 0, 3), i, 0)
        im = lambda i, j: (jnp.clip(j - 6, 0, 1), i, 0)
    else:
        last = ni - 1
        ia = lambda j, i: (jnp.minimum(j, 1), jnp.where(j < 2, i, last), 0)
        ib = lambda j, i: (jnp.clip(j - 2, 0, 3), jnp.where(j < 2, 0, jnp.where(j < 6, i, last)), 0)
        im = lambda j, i: (jnp.clip(j - 6, 0, 1), jnp.where(j < 6, 0, i), 0)
    return [pl.BlockSpec((1, tm, D), f) for f in (ia, ib, im)]


def _inproj_bwd_x(dza, dzb, dzm, w_all, x2, dx2, norm_g, after):
    n = x2.shape[0]
    tm = 512
    ni = n // tm

    def body(dza_ref, dzb_ref, dzm_ref, w_ref, x_ref, dx2_ref, g_ref, after_ref, gx_ref, gg_ref, acc):
        i, j = pl.program_id(0), pl.program_id(1)

        @pl.when((i == 0) & (j == 0))
        def _():
            gg_ref[...] = jnp.zeros((1, D), F32)

        @pl.when(j == 0)
        def _():
            acc[...] = jnp.zeros((tm, D), F32)

        def add(ref):
            acc[...] += lax.dot_general(ref[0], w_ref[0], NT_DIMS, preferred_element_type=F32)

        pl.when(j < 2)(lambda: add(dza_ref))
        pl.when((j >= 2) & (j < 6))(lambda: add(dzb_ref))
        pl.when(j >= 6)(lambda: add(dzm_ref))

        @pl.when(j == NB - 1)
        def _():
            x = x_ref[...]
            r = lax.rsqrt(jnp.mean(x * x, axis=-1, keepdims=True) + EPS)
            xn = x * r
            dh = acc[...]
            gg_ref[...] += jnp.sum(dh * xn, axis=0, keepdims=True)
            dxn = dh * g_ref[...]
            gx_ref[...] = dx2_ref[...] + r * (dxn - xn * jnp.mean(dxn * xn, axis=-1, keepdims=True))

    rows = pl.BlockSpec((tm, D), lambda i, j: (i, 0))
    return pl.pallas_call(
        body, name="inproj_bwd_x", grid=(ni, NB),
        in_specs=_dz_specs(tm, ni, True) + [pl.BlockSpec((1, D, D), lambda i, j: (j, 0, 0)), rows, rows,
                                             pl.BlockSpec((1, D), lambda i, j: (0, 0)), ANY],
        out_specs=[rows, pl.BlockSpec((1, D), lambda i, j: (0, 0))],
        out_shape=[SDS((n, D), F32), SDS((1, D), F32)],
        scratch_shapes=[pltpu.VMEM((tm, D), F32)],
        compiler_params=_params(48),
    )(dza, dzb, dzm, w_all, x2, dx2, norm_g, after)


def _inproj_bwd_w(dza, dzb, dzm, h_all, g_m):
    n = h_all.shape[0]
    tm = min(n, 2048)
    ni = n // tm

    def body(dza_ref, dzb_ref, dzm_ref, h_ref, gm_hbm, gw_ref, got_w, got_m, stage, send_sems, recv_sems):
        j, i = pl.program_id(0), pl.program_id(1)
        x, y, c = _place()
        sibling = (x, y, 1 - c)

        def send_w(q):
            return pltpu.make_async_remote_copy(
                src_ref=stage.at[q % 2], dst_ref=got_w.at[q], send_sem=send_sems.at[q], recv_sem=recv_sems.at[q],
                device_id=sibling, device_id_type=MESH)

        def send_m(q):
            return pltpu.make_async_remote_copy(
                src_ref=gm_hbm.at[2 * q + (1 - c)], dst_ref=got_m.at[q], send_sem=send_sems.at[4 + q],
                recv_sem=recv_sems.at[4 + q], device_id=sibling, device_id_type=MESH)

        @pl.when((j == 0) & (i == 0))
        def _():
            for q in range(4):
                send_m(q).start()

        @pl.when(i == 0)
        def _():
            gw_ref[...] = jnp.zeros((1, D, D), F32)

        def add(ref):
            gw_ref[0] += lax.dot_general(h_ref[...], ref[0], TN_DIMS, preferred_element_type=F32)

        pl.when(j < 2)(lambda: add(dza_ref))
        pl.when((j >= 2) & (j < 6))(lambda: add(dzb_ref))
        pl.when(j >= 6)(lambda: add(dzm_ref))

        for q in range(4):
            @pl.when((i == ni - 1) & (j == 2 * q + 1 - c))
            def _(q=q):
                if q >= 2:
                    send_w(q - 2).wait_send()
                stage[q % 2] = gw_ref[0].astype(BF16)
                send_w(q).start()

        @pl.when((j == NB - 1) & (i == ni - 1))
        def _():
            for q in (2, 3):
                send_w(q).wait_send()
            for q in range(4):
                send_w(q).wait_recv()
                send_m(q).wait_send()
                send_m(q).wait_recv()

    return pl.pallas_call(
        body, name="inproj_bwd_w", grid=(NB, ni),
        in_specs=_dz_specs(tm, ni, False) + [pl.BlockSpec((tm, D), lambda j, i: (i, 0)), ANY],
        out_specs=[pl.BlockSpec((1, D, D), lambda j, i: (j, 0, 0)), ANY, ANY],
        out_shape=[SDS((NB, D, D), F32), SDS((4, D, D), BF16), SDS((4,) + g_m.shape[1:], F32)],
        scratch_shapes=[pltpu.VMEM((2, D, D), BF16), pltpu.SemaphoreType.DMA((8,)), pltpu.SemaphoreType.DMA((8,))],
        compiler_params=_params(58),
    )(dza, dzb, dzm, h_all, g_m)


def _adamw(w, g, m, v):
    rows, cols = w.shape
    tr = _row_tile(rows)

    spec = pl.BlockSpec((tr, cols), lambda i: (i, 0))
    return pl.pallas_call(
        functools.partial(_adam_refs), name="adamw", grid=(rows // tr,), in_specs=[spec] * 4, out_specs=[spec] * 3,
        out_shape=[SDS((rows, cols), F32)] * 3, compiler_params=_params(32),
    )(w, g, m, v)


def _adam_refs(w_ref, g_ref, m_ref, v_ref, d_ref, nm_ref, nv_ref):
    gv = g_ref[...]
    nm = ADAM_B1 * m_ref[...] + (1.0 - ADAM_B1) * gv
    nv = ADAM_B2 * v_ref[...] + (1.0 - ADAM_B2) * (gv * gv)
    m_hat = nm / (1.0 - ADAM_B1 ** ADAM_STEP)
    v_hat = nv / (1.0 - ADAM_B2 ** ADAM_STEP)
    d_ref[...] = -ADAM_LR * (m_hat / (jnp.sqrt(v_hat) + ADAM_EPS) + ADAM_WD * w_ref[...])
    nm_ref[...] = nm
    nv_ref[...] = nv


def _adamw_small(ws, gs, ms, vs):
    k = len(ws)

    def body(*refs):
        ins, outs = refs[:4 * k], refs[4 * k:7 * k]
        vin, vout = refs[7 * k:11 * k], refs[11 * k:14 * k]
        load_sems, store_sems = refs[14 * k:]
        loads = [pltpu.make_async_copy(ins[i], vin[i], load_sems.at[i]) for i in range(4 * k)]
        for cp in loads:
            cp.start()
        for cp in loads:
            cp.wait()
        for i in range(k):
            _adam_refs(*[vin[part * k + i] for part in range(4)], *[vout[part * k + i] for part in range(3)])
        stores = [pltpu.make_async_copy(vout[i], outs[i], store_sems.at[i]) for i in range(3 * k)]
        for cp in stores:
            cp.start()
        for cp in stores:
            cp.wait()

    shapes = [SDS(w.shape, F32) for w in ws]
    vmem = [pltpu.VMEM(w.shape, F32) for w in ws]
    out = pl.pallas_call(
        body, name="adamw_small", out_shape=shapes * 3, in_specs=[HBM] * (4 * k), out_specs=[HBM] * (3 * k),
        scratch_shapes=vmem * 7 + [pltpu.SemaphoreType.DMA((4 * k,)), pltpu.SemaphoreType.DMA((3 * k,))],
        compiler_params=_params(32),
    )(*ws, *gs, *ms, *vs)
    return out[:k], out[k:2 * k], out[2 * k:]


def _allgather(blocks, dtypes, name):
    na = len(blocks)

    def body(*refs):
        ins, outs, stages = refs[:na], refs[na:2 * na], refs[2 * na:3 * na]
        send_sems, recv_sems, local_sems = refs[3 * na:]
        x, y, c = _place()
        me, sibling = (x, y, c), (x, y, 1 - c)
        chips = [(1 - x, y), (x, 1 - y), (1 - x, 1 - y)]
        blk = lambda p: 4 * p[0] + 2 * p[1] + p[2]

        def copy(a, k, block, to, src=None):
            return pltpu.make_async_remote_copy(
                src_ref=outs[a].at[blk(block)] if src is None else src, dst_ref=outs[a].at[blk(block)],
                send_sem=send_sems.at[7 * a + k], recv_sem=recv_sems.at[7 * a + k],
                device_id=to, device_id_type=MESH)

        mine, first, passed = [], [], []
        for a in range(na):
            stages[a][...] = ins[a][...].astype(dtypes[a])
            mine.append(pltpu.make_async_copy(stages[a], outs[a].at[blk(me)], local_sems.at[a]))
            mine[-1].start()
            first.append(copy(a, 0, me, sibling, src=stages[a]))
            first += [copy(a, 1 + j, me, (*chip, c), src=stages[a]) for j, chip in enumerate(chips)]
        for cp in first:
            cp.start()
        for j, chip in enumerate(chips):
            for a in range(na):
                copy(a, 1 + j, (*chip, c), me).wait_recv()
                passed.append(copy(a, 4 + j, (*chip, c), sibling))
                passed[-1].start()
        for a in range(na):
            copy(a, 0, sibling, me).wait_recv()
            for j, chip in enumerate(chips):
                copy(a, 4 + j, (*chip, 1 - c), me).wait_recv()
        for cp in first + passed:
            cp.wait_send()
        for cp in mine:
            cp.wait()

    return pl.pallas_call(
        body, name=name,
        in_specs=[pl.BlockSpec(memory_space=pltpu.VMEM)] * na, out_specs=[ANY] * na,
        out_shape=[SDS((NB,) + b.shape, dt) for b, dt in zip(blocks, dtypes)],
        scratch_shapes=[pltpu.VMEM(b.shape, dt) for b, dt in zip(blocks, dtypes)]
        + [pltpu.SemaphoreType.DMA((7 * na,)), pltpu.SemaphoreType.DMA((7 * na,)), pltpu.SemaphoreType.DMA((na,))],
        compiler_params=_params(40),
    )(*blocks)


HBM = pl.BlockSpec(memory_space=pltpu.HBM)
SEMS = pl.BlockSpec(memory_space=pltpu.SEMAPHORE)
EFFECT = pltpu.SideEffectType.DATAFLOW_SIDE_EFFECTING


def _chip_copies(srcs, lands, send_sems, recv_sems):
    x, y, c = _place()
    return [pltpu.make_async_remote_copy(
        src_ref=srcs[a].at[slot], dst_ref=lands[a].at[slot],
        send_sem=send_sems.at[3 * a + slot], recv_sem=recv_sems.at[3 * a + slot],
        device_id=(px, py, c), device_id_type=MESH)
        for a in range(len(srcs)) for slot, (px, py) in enumerate(_other_chips(x, y))]


def _split_start(name, copies, per_array, srcs, lands, after=None):
    na = len(srcs)

    def body(*refs):
        send_sems, recv_sems = refs[-2 * na - 3], refs[-2 * na - 2]
        for cp in copies(refs[:na], refs[na:2 * na], send_sems, recv_sems):
            cp.start()
        refs[-1][...] = jnp.zeros_like(refs[-1])

    hbm = lambda a: pltpu.HBM(a.shape, a.dtype)
    pin = lambda a: pltpu.with_memory_space_constraint(a, pltpu.HBM)
    out = pl.pallas_call(
        body, name=name,
        out_shape=(pltpu.SemaphoreType.DMA((per_array * na,)), pltpu.SemaphoreType.DMA((per_array * na,)),
                   *[hbm(a) for a in srcs], *[hbm(a) for a in lands], SDS((8, BD), F32)),
        in_specs=[HBM] * (2 * na) + ([] if after is None else [ANY]),
        out_specs=(SEMS, SEMS, *[HBM] * (2 * na), pl.BlockSpec(memory_space=pltpu.VMEM)),
        input_output_aliases={i: 2 + i for i in range(2 * na)},
        compiler_params=pltpu.CompilerParams(has_side_effects=EFFECT),
    )(*[pin(a) for a in srcs], *[pin(a) for a in lands], *([] if after is None else [after]))
    return out[0], out[1], out[2:2 + na], out[2 + na:2 + 2 * na], out[-1]


def _split_wait(name, copies, started, after):
    send_sems, recv_sems, srcs, lands, _ = started
    na = len(srcs)

    def body(*refs):
        waits = copies(refs[:na], refs[na:2 * na], refs[2 * na], refs[2 * na + 1])
        for cp in waits:
            cp.wait_send()
        for cp in waits:
            cp.wait_recv()

    hbm = lambda a: pltpu.HBM(a.shape, a.dtype)
    out = pl.pallas_call(
        body, name=name,
        out_shape=(*[hbm(a) for a in srcs], *[hbm(a) for a in lands]),
        in_specs=[HBM] * (2 * na) + [SEMS, SEMS, ANY],
        out_specs=tuple([HBM] * (2 * na)),
        input_output_aliases={i: i for i in range(2 * na)},
        compiler_params=pltpu.CompilerParams(has_side_effects=EFFECT),
    )(*srcs, *lands, send_sems, recv_sems, after)
    return out[na:]


def _add_sibling(place, g, a_in):
    _, r, cols = g.shape
    tr = _row_tile(r)

    def chip(k, pr):
        qx = pr[0] if k in (1, 3) else 1 - pr[0]
        qy = pr[1] if k in (0, 3) else 1 - pr[1]
        return 2 * qx + qy

    def body(place_ref, *refs):
        g_refs, a_refs, (out_ref, own_ref) = refs[0:4], refs[4:8], refs[8:10]
        for k in range(3):
            out_ref[k] = (g_refs[k][0] + a_refs[k][0].astype(F32)).astype(BF16)
        own_ref[...] = g_refs[3][0] + a_refs[3][0].astype(F32)

    mine = lambda k: pl.BlockSpec((1, tr, cols), lambda i, pr: (2 * chip(k, pr) + pr[2], i, 0))
    theirs = lambda k: pl.BlockSpec((1, tr, cols), lambda i, pr: (chip(k, pr), i, 0))
    return pl.pallas_call(
        body, name="add_sibling",
        grid_spec=pltpu.PrefetchScalarGridSpec(
            num_scalar_prefetch=1, grid=(r // tr,),
            in_specs=[mine(k) for k in range(4)] + [theirs(k) for k in range(4)],
            out_specs=[pl.BlockSpec((3, tr, cols), lambda i, pr: (0, i, 0)),
                       pl.BlockSpec((tr, cols), lambda i, pr: (i, 0))]),
        out_shape=[SDS((3, r, cols), BF16), SDS((r, cols), F32)], compiler_params=_params(48),
    )(place, *[g] * 4, *[a_in] * 4)


def _add_chips(own, b_in):
    r, cols = own.shape
    tr = _row_tile(r)

    def body(p_ref, b0_ref, b1_ref, b2_ref, o_ref):
        o_ref[...] = ((p_ref[...] + b0_ref[0].astype(F32)) + b1_ref[0].astype(F32)) + b2_ref[0].astype(F32)

    slot = lambda k: pl.BlockSpec((1, tr, cols), lambda i: (k, i, 0))
    spec = pl.BlockSpec((tr, cols), lambda i: (i, 0))
    return pl.pallas_call(
        body, name="add_chips", grid=(r // tr,), in_specs=[spec, slot(0), slot(1), slot(2)], out_specs=spec,
        out_shape=SDS((r, cols), F32), compiler_params=_params(32),
    )(own, b_in, b_in, b_in)


VEC_NAMES = ("b_merge", "conv_b", "rg_bx", "rg_ba", "rg_lambda", "hg_lb_logits", "hg_norm_g", "final_norm_g")
REP_NAMES = ("rg_wx", "rg_wa", "norm_g") + VEC_NAMES
SMALL_AT = 3 * BD
SMALL_ROWS = 48
MID_ROWS = 448


def _sum_blocks(parts):
    def body(p_ref, o_ref):
        acc = p_ref[0]
        for k in range(1, NB):
            acc = acc + p_ref[k]
        o_ref[...] = acc

    return pl.pallas_call(body, name="sum_blocks", out_shape=SDS(parts.shape[1:], F32))(parts)


def _pack_rows(arrays, width, row_multiple=8):
    flat = jnp.concatenate([a.reshape(-1) for a in arrays])
    rows = -(-flat.shape[0] // width)
    rows = -(-rows // row_multiple) * row_multiple
    return jnp.pad(flat, (0, rows * width - flat.shape[0])).reshape(rows, width)


def _unpack(flat, like):
    out, off = [], 0
    for a in like:
        out.append(flat[off:off + a.size].reshape(a.shape))
        off += a.size
    return out


def kernel(x, w_in, b_merge, conv_w, conv_b, rg_wx, rg_bx, rg_wa, rg_ba, rg_lambda, hg_lb_logits, hg_norm_g, proj_a, proj_b, w_out, norm_g, final_norm_g, loss_target, m_w_in, m_b_merge, m_conv_w, m_conv_b, m_rg_wx, m_rg_bx, m_rg_wa, m_rg_ba, m_rg_lambda, m_hg_lb_logits, m_hg_norm_g, m_proj_a, m_proj_b, m_w_out, m_norm_g, m_final_norm_g, v_w_in, v_b_merge, v_conv_w, v_conv_b, v_rg_wx, v_rg_bx, v_rg_wa, v_rg_ba, v_rg_lambda, v_hg_lb_logits, v_hg_norm_g, v_proj_a, v_proj_b, v_w_out, v_norm_g, v_final_norm_g):
    weights = dict(w_in=w_in, b_merge=b_merge, conv_w=conv_w, conv_b=conv_b, rg_wx=rg_wx, rg_bx=rg_bx, rg_wa=rg_wa,
                   rg_ba=rg_ba, rg_lambda=rg_lambda, hg_lb_logits=hg_lb_logits, hg_norm_g=hg_norm_g, proj_a=proj_a,
                   proj_b=proj_b, w_out=w_out, norm_g=norm_g, final_norm_g=final_norm_g)
    mom1 = dict(w_in=m_w_in, b_merge=m_b_merge, conv_w=m_conv_w, conv_b=m_conv_b, rg_wx=m_rg_wx, rg_bx=m_rg_bx,
                rg_wa=m_rg_wa, rg_ba=m_rg_ba, rg_lambda=m_rg_lambda, hg_lb_logits=m_hg_lb_logits,
                hg_norm_g=m_hg_norm_g, proj_a=m_proj_a, proj_b=m_proj_b, w_out=m_w_out, norm_g=m_norm_g,
                final_norm_g=m_final_norm_g)
    mom2 = dict(w_in=v_w_in, b_merge=v_b_merge, conv_w=v_conv_w, conv_b=v_conv_b, rg_wx=v_rg_wx, rg_bx=v_rg_bx,
                rg_wa=v_rg_wa, rg_ba=v_rg_ba, rg_lambda=v_rg_lambda, hg_lb_logits=v_hg_lb_logits,
                hg_norm_g=v_hg_norm_g, proj_a=v_proj_a, proj_b=v_proj_b, w_out=v_w_out, norm_g=v_norm_g,
                final_norm_g=v_final_norm_g)
    order = list(weights)
    nb, s_len, _ = x.shape
    n = nb * s_len
    px, py, pc = _place()
    place = jnp.stack([px, py, pc]).astype(jnp.int32)

    in_hbm = lambda a: pltpu.with_memory_space_constraint(a, pltpu.HBM)
    norm_gain = in_hbm(norm_g)

    x2 = x.reshape(n, D)
    cw_blk = jnp.pad(conv_w[0], ((0, 4), (0, 0)))
    order_ids = jnp.stack([_block_id(p) for p in _arrival_order(px, py, pc)]).astype(jnp.int32)
    h_all = _prenorm(x2, norm_gain)
    z, w_all, pa_all, pb_all, wo_all, cw_all = _gather_inproj(
        order_ids, h_all, [w_in[0], proj_a[0], proj_b[0], w_out[0], cw_blk], [BF16, BF16, BF16, BF16, F32])
    pa_full, pb_full, wo_full = (a.reshape(D, D) for a in (pa_all, pb_all, wo_all))
    cw8 = in_hbm(cw_all.transpose(1, 0, 2).reshape(8, D))
    wx_b, wa_b = in_hbm(rg_wx[0].astype(BF16)), in_hbm(rg_wa[0].astype(BF16))
    cb, bx, ba, lam = (in_hbm(a.reshape(1, D)) for a in (conv_b, rg_bx, rg_ba, rg_lambda))
    fin_g, b_mrg = in_hbm(final_norm_g.reshape(1, D)), in_hbm(b_merge)
    lb_lg, hg_g = in_hbm(hg_lb_logits), in_hbm(hg_norm_g)

    hlru, ya = _lru_fwd(z, cw8, cb, wx_b, wa_b, bx, ba, lam, nb, s_len)
    o_all, yb, st_all = _hgrn_fwd(z, lb_lg, hg_g, nb, s_len)

    (dx2, dya, dyb, dzm, loss_acc, g_fin, g_bm, g_mid) = _mid(
        ya, yb, z, b_mrg, x2, loss_target.reshape(n, D), fin_g, pa_full, pb_full, wo_full)
    dzb, g_lg, g_hg = _hgrn_bwd(z, o_all, st_all, dyb, lb_lg, hg_g, nb, s_len)
    dza, g_cw8, g_cb, g_wx, g_wa, g_bx, g_ba, g_lam = _lru_bwd(
        z, hlru, dya, cw8, cb, wx_b, wa_b, bx, ba, lam, nb, s_len)

    part = dict(b_merge=g_bm, conv_b=g_cb, rg_bx=g_bx, rg_ba=g_ba, rg_lambda=g_lam, hg_lb_logits=g_lg,
                hg_norm_g=g_hg, final_norm_g=g_fin)
    vec = _pack_rows([part[k] for k in VEC_NAMES], BD)
    vec = jnp.pad(vec, ((0, 16 * NB - vec.shape[0]), (0, 0))).reshape(NB, 2, D)
    rows8 = lambda a: jnp.pad(a, ((0, 0), (0, 8 - a.shape[1]), (0, 0)))
    small = jnp.concatenate([g_wx.reshape(NB, 16, D), g_wa.reshape(NB, 16, D),
                             rows8(g_cw8.reshape(8, NB, BD).transpose(1, 0, 2).reshape(NB, 1, D)), rows8(vec),
                             jnp.zeros((NB, MID_ROWS - SMALL_AT - SMALL_ROWS, D), F32)], axis=1)
    g_m = lax.dynamic_update_slice(g_mid, small, (0, SMALL_AT, 0))
    g_w, w_from_sibling, m_from_sibling = _inproj_bwd_w(dza, dzb, dzm, h_all, g_m)
    w_out_bf, w_own = _add_sibling(place, g_w, w_from_sibling)
    m_out_bf, m_own = _add_sibling(place, g_m, m_from_sibling)
    outgoing = [w_out_bf, m_out_bf]
    chip_sums = _split_start("rs_chips_start", _chip_copies, 3, outgoing, [lax.empty(a.shape, a.dtype) for a in outgoing])
    grad_x, g_ng = _inproj_bwd_x(dza, dzb, dzm, w_all, x2, dx2, norm_gain, chip_sums[-1])
    from_chips = _split_wait("rs_chips_wait", _chip_copies, chip_sums, grad_x)
    r_w = _add_chips(w_own, from_chips[0])
    r_m = _add_chips(m_own, from_chips[1])
    row = lax.broadcasted_iota(jnp.int32, (8, D), 0)
    mine = jnp.where(row == 0, g_ng, jnp.where(row == 1, loss_acc[0:1, 0:1], 0.0))
    tail = jnp.concatenate([r_m[SMALL_AT:SMALL_AT + SMALL_ROWS], mine], axis=0)
    (tail_all,) = _allgather([tail], [F32], "gather_small_grads")
    summed = _sum_blocks(tail_all[:, SMALL_ROWS:SMALL_ROWS + 8])

    grads = dict(w_in=r_w.reshape(1, D, D),
                 proj_a=r_m[0:BD].reshape(1, BD, D), proj_b=r_m[BD:2 * BD].reshape(1, BD, D),
                 w_out=r_m[2 * BD:3 * BD].reshape(1, BD, D),
                 conv_w=r_m[SMALL_AT + 32].reshape(8, BD)[0:4].reshape(1, 4, BD),
                 rg_wx=tail_all[:, 0:16].reshape(1, NB, BD, BD), rg_wa=tail_all[:, 16:32].reshape(1, NB, BD, BD),
                 norm_g=summed[0:1])
    vec_all = tail_all[:, 40:42].reshape(-1)
    for k, gk in zip(VEC_NAMES, _unpack(vec_all, [weights[k] for k in VEC_NAMES])):
        grads[k] = gk

    delta, new_m, new_v = {}, {}, {}
    flat2 = lambda a: a.reshape(-1, a.shape[-1])
    for k in ("w_in", "proj_a", "proj_b", "w_out"):
        outs = _adamw(*[flat2(t[k]) for t in (weights, grads, mom1, mom2)])
        delta[k], new_m[k], new_v[k] = (a.reshape(weights[k].shape) for a in outs)
    rep = list(REP_NAMES) + ["conv_w"]
    outs = _adamw_small(*[[flat2(t[k]) for k in rep] for t in (weights, grads, mom1, mom2)])
    for tgt, arrays in zip((delta, new_m, new_v), outs):
        for k, a in zip(rep, arrays):
            tgt[k] = a.reshape(weights[k].shape)

    return (summed[1, 0], grad_x.reshape(x.shape), *[grads[k] for k in order], *[delta[k] for k in order],
            *[new_m[k] for k in order], *[new_v[k] for k in order])
```

```python
import functools

import jax
import jax.numpy as jnp
from jax import lax
from jax.experimental import pallas as pl
from jax.experimental.pallas import tpu as pltpu

F32 = jnp.float32
BF16 = jnp.bfloat16
SDS = jax.ShapeDtypeStruct
MESH = pl.DeviceIdType.MESH
ANY = pl.BlockSpec(memory_space=pl.ANY)

D = 1024
NB = 8
BD = D // NB
CHUNK = 64
EPS = 1e-6
LRU_C = 8.0
HG_SCALE = BD ** -0.5
ADAM_LR, ADAM_B1, ADAM_B2, ADAM_EPS, ADAM_WD, ADAM_STEP = 0.001, 0.9, 0.999, 1e-08, 0.01, 10

NT_DIMS = (((1,), (1,)), ((), ()))
TN_DIMS = (((0,), (0,)), ((), ()))


def _params(vmem_mib):
    return pltpu.CompilerParams(vmem_limit_bytes=vmem_mib << 20)


def _row_tile(rows, most=256):
    assert rows % 8 == 0
    return max(t for t in range(8, min(rows, most) + 1, 8) if rows % t == 0)


def _sigmoid(v):
    return 0.5 * (jnp.tanh(0.5 * v) + 1.0)


def _groups(v):
    return v.reshape(v.shape[0] // 8, 8, v.shape[1])


def _softplus_neg(lam):
    t = -lam
    e = jnp.exp(-jnp.abs(t))
    w = 1.0 + e
    d = w - 1.0
    l1p = jnp.where(d == 0.0, e, jnp.log(w) * (e / jnp.where(d == 0.0, 1.0, d)))
    return jnp.maximum(t, 0.0) + l1p


def _place():
    return lax.axis_index("x"), lax.axis_index("y"), lax.axis_index("c")


def _other_chips(x, y):
    return [(1 - x, y), (x, 1 - y), (1 - x, 1 - y)]


def _block_id(p):
    return 4 * p[0] + 2 * p[1] + p[2]


def _core_chips(x, y, c):
    near, far, diag = _other_chips(x, y)
    pick = lambda a, b: (jnp.where(c == 0, a[0], b[0]), jnp.where(c == 0, a[1], b[1]))
    return [pick(near, far), pick(far, near), diag]


def _arrival_order(x, y, c):
    first, second, diag = _core_chips(x, y, c)
    return [(x, y, c), (x, y, 1 - c), (*first, c), (*second, 1 - c), (*second, c), (*first, 1 - c),
            (*diag, c), (*diag, 1 - c)]


def _gather_inproj(order_ids, x2, norm_g, blocks, dtypes):
    na = len(blocks)
    n = x2.shape[0]
    tm = min(n, 1024)
    ni = n // tm

    def body(order_ref, x_ref, g_ref, *refs):
        ins, (z_ref, h_ref), outs = refs[:na], refs[na:na + 2], refs[na + 2:2 * na + 2]
        stages = refs[2 * na + 2:3 * na + 2]
        h_full, wbuf, send_sems, recv_sems, local_sems, wsems, hsem = refs[3 * na + 2:]
        j, i = pl.program_id(0), pl.program_id(1)
        x, y, c = _place()
        me, sibling = (x, y, c), (x, y, 1 - c)
        chips = _core_chips(x, y, c)
        sibling_chips = [chips[1], chips[0], chips[2]]
        small = range(1, na)

        def copy(a, k, block, to, src=None):
            return pltpu.make_async_remote_copy(
                src_ref=outs[a].at[_block_id(block)] if src is None else src, dst_ref=outs[a].at[_block_id(block)],
                send_sem=send_sems.at[7 * a + k], recv_sem=recv_sems.at[7 * a + k],
                device_id=to, device_id_type=MESH)

        def local(a):
            return pltpu.make_async_copy(stages[a], outs[a].at[_block_id(me)], local_sems.at[a])

        def landed(a, slot):
            copy(a, 1 + slot, (*chips[slot], c), me).wait_recv()
            copy(a, 4 + slot, (*chips[slot], c), sibling).start()
            if slot == 0:
                copy(a, 3, (*chips[0], c), (*chips[1], c)).start()

        def second_and_first_small():
            landed(0, 1)
            for a in small:
                landed(a, 0)

        def diagonal_and_second_small():
            landed(0, 2)
            for a in small:
                landed(a, 1)

        def passed_on(a, slot):
            copy(a, 4 + slot, (*sibling_chips[slot], 1 - c), me).wait_recv()

        def sibling_here_send_second():
            copy(0, 0, sibling, me).wait_recv()
            for a in range(na):
                copy(a, 2, me, (*chips[1], c), src=stages[a]).start()

        @pl.when((j == 0) & (i == 0))
        def _():
            for a in range(na):
                stages[a][...] = ins[a][...].astype(dtypes[a])
                local(a).start()
            for a in range(na):
                copy(a, 0, me, sibling, src=stages[a]).start()
                copy(a, 1, me, (*chips[0], c), src=stages[a]).start()

        @pl.when(j == 0)
        def _():
            xv = x_ref[...]
            r = lax.rsqrt(jnp.mean(xv * xv, axis=-1, keepdims=True) + EPS)
            hb = ((xv * r) * g_ref[...]).astype(BF16)
            h_full[pl.ds(pl.multiple_of(i * tm, tm), tm), :] = hb

        save_h = pltpu.make_async_copy(h_full, h_ref, hsem)
        pl.when((j == 0) & (i == ni - 1))(save_h.start)

        steps = [
            lambda: local(0).wait(),
            sibling_here_send_second,
            lambda: landed(0, 0),
            lambda: passed_on(0, 0),
            second_and_first_small,
            lambda: passed_on(0, 1),
            diagonal_and_second_small,
            lambda: passed_on(0, 2),
        ]
        def w_load(k):
            return pltpu.make_async_copy(outs[0].at[order_ref[k]], wbuf.at[k % 2], wsems.at[k % 2])

        for k, step in enumerate(steps):
            @pl.when((j == 0) & (i == 0) if k == 0 else (j == k - 1) & (i == ni - 1))
            def _(k=k, step=step):
                step()
                w_load(k).start()

        pl.when(i == 0)(lambda: w_load(j).wait())
        z_ref[0] = jnp.dot(h_full[pl.ds(pl.multiple_of(i * tm, tm), tm), :], wbuf[j % 2], preferred_element_type=F32)

        @pl.when((j == NB - 1) & (i == ni - 1))
        def _():
            save_h.wait()
            for a in small:
                landed(a, 2)
            for a in small:
                local(a).wait()
                copy(a, 0, sibling, me).wait_recv()
                for slot in range(3):
                    passed_on(a, slot)
            for a in range(na):
                copy(a, 0, me, sibling, src=stages[a]).wait_send()
                for slot, chip in enumerate(chips):
                    copy(a, 1 + slot, me, (*chip, c), src=stages[a]).wait_send()
                    copy(a, 4 + slot, (*chip, c), sibling).wait_send()

    rows_once = lambda j, i, order: (jnp.where(j == 0, i, ni - 1), 0)
    vmem = pl.BlockSpec(memory_space=pltpu.VMEM)
    return pl.pallas_call(
        body, name="gather_inproj",
        grid_spec=pltpu.PrefetchScalarGridSpec(
            num_scalar_prefetch=1, grid=(NB, ni),
            in_specs=[pl.BlockSpec((tm, D), rows_once), pl.BlockSpec((1, D), lambda j, i, order: (0, 0))] + [vmem] * na,
            out_specs=[pl.BlockSpec((1, tm, D), lambda j, i, order: (order[j], i, 0)), ANY] + [ANY] * na,
            scratch_shapes=[pltpu.VMEM(b.shape, dt) for b, dt in zip(blocks, dtypes)]
            + [pltpu.VMEM((n, D), BF16), pltpu.VMEM((2, D, D), BF16),
               pltpu.SemaphoreType.DMA((7 * na,)), pltpu.SemaphoreType.DMA((7 * na,)),
               pltpu.SemaphoreType.DMA((na,)), pltpu.SemaphoreType.DMA((2,)), pltpu.SemaphoreType.DMA(())]),
        out_shape=[SDS((NB, n, D), F32), SDS((n, D), BF16)] + [SDS((NB,) + b.shape, dt) for b, dt in zip(blocks, dtypes)],
        compiler_params=_params(56),
    )(order_ids, x2, norm_g, *blocks)


LRU_T = 256


def _shifted(groups, shifts):
    row = lax.broadcasted_iota(jnp.int32, (groups.shape[0] - 1,) + groups.shape[1:], 1)
    out = []
    for s in shifts:
        y = pltpu.roll(groups, s % 8, 1)
        moved = jnp.where(row >= s, y[1:], y[:-1]) if s > 0 else jnp.where(row < 8 + s, y[:-1], y[1:])
        out.append(moved.reshape(-1, groups.shape[2]))
    return out


def _conv(taps, cw, cb):
    acc = taps[0] * cw[0:1, :] + taps[1] * cw[1:2, :]
    acc = acc + taps[2] * cw[2:3, :]
    acc = acc + taps[3] * cw[3:4, :]
    return cb + acc


def _lru_gates(xa, wx_ref, wa_ref, bx, ba, lam):
    xab = xa.astype(BF16)
    pis, prs = [], []
    for h in range(NB):
        xs = xab[:, h * BD:(h + 1) * BD]
        pis.append(jnp.dot(xs, wx_ref[h], preferred_element_type=F32))
        prs.append(jnp.dot(xs, wa_ref[h], preferred_element_type=F32))
    gi = _sigmoid(jnp.concatenate(pis, axis=1) + bx)
    gr = _sigmoid(jnp.concatenate(prs, axis=1) + ba)
    sp = _softplus_neg(lam)
    log_a = (-LRU_C * gr) * sp
    a = jnp.exp(log_a)
    mult = jnp.sqrt(-jnp.tanh(log_a) * (a * a + 1.0))
    return xab, gi, gr, sp, a, mult


def _lru_fwd(z, cw8, cb, wx, wa, bx, ba, lam, nb, s_len):
    n = nb * s_len
    t = LRU_T
    ns = s_len // t

    def body(xp_ref, ga_ref, cw_ref, cb_ref, wx_ref, wa_ref, bx_ref, ba_ref, lam_ref,
             h_ref, ya_ref, ext, a_s, u_s, carry):
        @pl.when(pl.program_id(1) == 0)
        def _():
            ext[0:8, :] = jnp.zeros((8, D), F32)
            carry[...] = jnp.zeros((8, D), F32)

        xp = xp_ref[0]
        ext[8:8 + t, :] = xp
        xa = _conv(_shifted(_groups(ext[...]), (3, 2, 1)) + [xp], cw_ref[...], cb_ref[...])
        ext[0:8, :] = xp[t - 8:t, :]
        _, gi, _, _, a, mult = _lru_gates(xa, wx_ref, wa_ref, bx_ref[...], ba_ref[...], lam_ref[...])
        u = (mult * gi) * xa
        a, u = _groups(a), _groups(u)
        row = lax.broadcasted_iota(jnp.int32, a.shape, 1)
        for sh in (1, 2, 4):
            a_sh = pltpu.roll(a, sh, 1)
            u_sh = pltpu.roll(u, sh, 1)
            m = row >= sh
            u = jnp.where(m, a * u_sh + u, u)
            a = jnp.where(m, a * a_sh, a)
        a_s[...] = a.reshape(t, D)
        u_s[...] = u.reshape(t, D)

        def step(g, c):
            r = pl.multiple_of(g * 8, 8)
            hg = u_s[pl.ds(r, 8), :] + a_s[pl.ds(r, 8), :] * c
            h_ref[pl.ds(r, 8), :] = hg
            return hg[7:8, :]

        c_out = lax.fori_loop(0, t // 8, step, carry[0:1, :], unroll=4)
        carry[0:1, :] = c_out
        ga = ga_ref[0]
        ya_ref[...] = (h_ref[...] * (ga * _sigmoid(ga))).astype(BF16)

    row_map = lambda b, s: (b * ns + s, 0)
    rep2 = lambda b, s: (0, 0)
    rep3 = lambda b, s: (0, 0, 0)
    return pl.pallas_call(
        body, name="lru_fwd", grid=(nb, ns),
        in_specs=[pl.BlockSpec((1, t, D), lambda b, s: (0, b * ns + s, 0)),
                  pl.BlockSpec((1, t, D), lambda b, s: (1, b * ns + s, 0)),
                  pl.BlockSpec((8, D), rep2), pl.BlockSpec((1, D), rep2),
                  pl.BlockSpec((NB, BD, BD), rep3), pl.BlockSpec((NB, BD, BD), rep3),
                  pl.BlockSpec((1, D), rep2), pl.BlockSpec((1, D), rep2), pl.BlockSpec((1, D), rep2)],
        out_specs=[pl.BlockSpec((t, D), row_map), pl.BlockSpec((t, D), row_map)],
        out_shape=[SDS((n, D), F32), SDS((n, D), BF16)],
        scratch_shapes=[pltpu.VMEM((t + 8, D), F32), pltpu.VMEM((t, D), F32), pltpu.VMEM((t, D), F32),
                        pltpu.VMEM((8, D), F32)],
        compiler_params=_params(48),
    )(z, z, cw8, cb, wx, wa, bx, ba, lam)


def _lru_bwd(z, h_all, dya, cw8, cb, wx, wa, bx, ba, lam, nb, s_len):
    n = nb * s_len
    t = LRU_T
    ns = s_len // t
    t8 = t // 8

    def body(xp_ref, xph_ref, ga_ref, h_ref, hh_ref, dya_ref, cw_ref, cb_ref, wx_ref, wa_ref, bx_ref, ba_ref,
             lam_ref, dz_ref, gcw_ref, gcb_ref, gwx_ref, gwa_ref, gbx_ref, gba_ref, glam_ref,
             ext, hext, dext, a_s, u_s, dh_s, carry):
        b, s = pl.program_id(0), pl.program_id(1)
        first_tile = s == ns - 1

        @pl.when((b == 0) & (s == 0))
        def _():
            for ref in (gcw_ref, gcb_ref, gwx_ref, gwa_ref, gbx_ref, gba_ref, glam_ref):
                ref[...] = jnp.zeros(ref.shape, F32)

        @pl.when(s == 0)
        def _():
            dext[t:t + 8, :] = jnp.zeros((8, D), F32)
            carry[...] = jnp.zeros((8, D), F32)

        keep = jnp.where(first_tile, 0.0, 1.0)
        xp = xp_ref[0]
        ext[0:8, :] = xph_ref[0] * keep
        ext[8:8 + t, :] = xp
        hext[0:8, :] = hh_ref[...] * keep
        hext[8:8 + t, :] = h_ref[...]
        cw = cw_ref[...]
        lam = lam_ref[...]
        taps = _shifted(_groups(ext[...]), (3, 2, 1)) + [xp]
        xa = _conv(taps, cw, cb_ref[...])
        xab, gi, gr, sp, a, mult = _lru_gates(xa, wx_ref, wa_ref, bx_ref[...], ba_ref[...], lam)
        (h_prev,) = _shifted(_groups(hext[...]), (1,))
        ga = ga_ref[0]
        sg = _sigmoid(ga)
        dya_v = dya_ref[...]
        d_ga = dya_v * h_ref[...] * (sg * (1.0 + ga * (1.0 - sg)))
        g_in = dya_v * (ga * sg)

        (an,) = _shifted(jnp.concatenate([_groups(a), jnp.ones((1, 8, D), F32)], axis=0), (-1,))
        an, u = _groups(an), _groups(g_in)
        row = lax.broadcasted_iota(jnp.int32, an.shape, 1)
        for sh in (1, 2, 4):
            a_sh = pltpu.roll(an, 8 - sh, 1)
            u_sh = pltpu.roll(u, 8 - sh, 1)
            m = row < 8 - sh
            u = jnp.where(m, u + an * u_sh, u)
            an = jnp.where(m, an * a_sh, an)
        a_s[...] = an.reshape(t, D)
        u_s[...] = u.reshape(t, D)

        def step(i, c):
            r = pl.multiple_of((t8 - 1 - i) * 8, 8)
            dg = u_s[pl.ds(r, 8), :] + a_s[pl.ds(r, 8), :] * c
            dh_s[pl.ds(r, 8), :] = dg
            return dg[0:1, :]

        lax.fori_loop(0, t8, step, carry[0:1, :], unroll=4)
        dh = dh_s[...]
        carry[0:1, :] = a[0:1, :] * dh[0:1, :]

        d_a = dh * h_prev
        dux = dh * xa
        d_mult = dux * gi
        d_gi = dux * mult
        d_xa = dh * (mult * gi)
        d_loga = d_a * a - d_mult * ((a * a) / mult)
        d_gr = d_loga * (-LRU_C * sp)
        d_sp = jnp.sum(d_loga * (-LRU_C * gr), axis=0, keepdims=True)
        glam_ref[...] += d_sp * (-_sigmoid(-lam))
        d_pi = d_gi * gi * (1.0 - gi)
        d_pr = d_gr * gr * (1.0 - gr)
        gbx_ref[...] += jnp.sum(d_pi, axis=0, keepdims=True)
        gba_ref[...] += jnp.sum(d_pr, axis=0, keepdims=True)
        dpib = d_pi.astype(BF16)
        dprb = d_pr.astype(BF16)
        back = []
        for h in range(NB):
            cs = slice(h * BD, (h + 1) * BD)
            gwx_ref[h] += lax.dot_general(xab[:, cs], dpib[:, cs], TN_DIMS, preferred_element_type=F32)
            gwa_ref[h] += lax.dot_general(xab[:, cs], dprb[:, cs], TN_DIMS, preferred_element_type=F32)
            back.append(lax.dot_general(dpib[:, cs], wx_ref[h], NT_DIMS, preferred_element_type=F32)
                        + lax.dot_general(dprb[:, cs], wa_ref[h], NT_DIMS, preferred_element_type=F32))
        d_xa = d_xa + jnp.concatenate(back, axis=1)

        dext[0:t, :] = d_xa
        later = _shifted(_groups(dext[...]), (-3, -2, -1))
        d_xp = later[0] * cw[0:1, :] + later[1] * cw[1:2, :]
        d_xp = d_xp + later[2] * cw[2:3, :]
        d_xp = d_xp + d_xa * cw[3:4, :]
        dext[t:t + 8, :] = d_xa[0:8, :]
        gcb_ref[...] += jnp.sum(d_xa, axis=0, keepdims=True)
        for k in range(4):
            gcw_ref[k:k + 1, :] += jnp.sum(d_xa * taps[k], axis=0, keepdims=True)
        dz_ref[0] = d_xp.astype(BF16)
        dz_ref[1] = d_ga.astype(BF16)

    rb = lambda b, s: b * ns + (ns - 1 - s)
    halo = lambda b, s: jnp.maximum(rb(b, s) * t8 - 1, 0)
    rep2 = lambda b, s: (0, 0)
    rep3 = lambda b, s: (0, 0, 0)
    return pl.pallas_call(
        body, name="lru_bwd", grid=(nb, ns),
        in_specs=[pl.BlockSpec((1, t, D), lambda b, s: (0, rb(b, s), 0)),
                  pl.BlockSpec((1, 8, D), lambda b, s: (0, halo(b, s), 0)),
                  pl.BlockSpec((1, t, D), lambda b, s: (1, rb(b, s), 0)),
                  pl.BlockSpec((t, D), lambda b, s: (rb(b, s), 0)),
                  pl.BlockSpec((8, D), lambda b, s: (halo(b, s), 0)),
                  pl.BlockSpec((t, D), lambda b, s: (rb(b, s), 0)),
                  pl.BlockSpec((8, D), rep2), pl.BlockSpec((1, D), rep2),
                  pl.BlockSpec((NB, BD, BD), rep3), pl.BlockSpec((NB, BD, BD), rep3),
                  pl.BlockSpec((1, D), rep2), pl.BlockSpec((1, D), rep2), pl.BlockSpec((1, D), rep2)],
        out_specs=[pl.BlockSpec((2, t, D), lambda b, s: (0, rb(b, s), 0)),
                   pl.BlockSpec((8, D), rep2), pl.BlockSpec((1, D), rep2),
                   pl.BlockSpec((NB, BD, BD), rep3), pl.BlockSpec((NB, BD, BD), rep3),
                   pl.BlockSpec((1, D), rep2), pl.BlockSpec((1, D), rep2), pl.BlockSpec((1, D), rep2)],
        out_shape=[SDS((2, n, D), BF16), SDS((8, D), F32), SDS((1, D), F32),
                   SDS((NB, BD, BD), F32), SDS((NB, BD, BD), F32),
                   SDS((1, D), F32), SDS((1, D), F32), SDS((1, D), F32)],
        scratch_shapes=[pltpu.VMEM((t + 8, D), F32), pltpu.VMEM((t + 8, D), F32), pltpu.VMEM((t + 8, D), F32),
                        pltpu.VMEM((t, D), F32), pltpu.VMEM((t, D), F32), pltpu.VMEM((t, D), F32),
                        pltpu.VMEM((8, D), F32)],
        compiler_params=_params(56),
    )(z, z, z, h_all, h_all, dya, cw8, cb, wx, wa, bx, ba, lam)


HG_T = 512
HG_NC = HG_T // CHUNK
BNT_DIMS = (((2,), (2,)), ((0,), (0,)))
BNN_DIMS = (((2,), (1,)), ((0,), (0,)))
BTN_DIMS = (((1,), (1,)), ((0,), (0,)))


def _lower_bound(lg):
    m = jnp.max(lg, axis=0, keepdims=True)
    e = jnp.exp(lg - m)
    return e[0:1, :] / jnp.sum(e, axis=0, keepdims=True)


def _tri(upper):
    r = lax.broadcasted_iota(jnp.int32, (HG_NC, CHUNK, CHUNK), 1)
    c = lax.broadcasted_iota(jnp.int32, (HG_NC, CHUNK, CHUNK), 2)
    return (c >= r) if upper else (r >= c)


def _bdot(a, b, dims):
    return lax.dot_general(a, b, dims, preferred_element_type=F32)


def _tri_sums(upper, a):
    tri = _tri(upper).astype(BF16)
    a1 = a.astype(BF16)
    r1 = a - a1.astype(F32)
    a2 = r1.astype(BF16)
    a3 = (r1 - a2.astype(F32)).astype(BF16)
    return _bdot(tri, a1, BNN_DIMS) + (_bdot(tri, a2, BNN_DIMS) + _bdot(tri, a3, BNN_DIMS))


def _chunks(a):
    return a.reshape(HG_NC, CHUNK, BD)


def _hg_tile(q, fp, lb):
    q, fp = _chunks(q), _chunks(fp)
    sig = _sigmoid(fp)
    f = lb + (1.0 - lb) * sig
    log_f = jnp.log(f)
    k = 1.0 - f
    b = _tri_sums(False, log_f)
    b_mid = b[:, CHUNK // 2:CHUNK // 2 + 1, :]
    b_last = b[:, CHUNK - 1:CHUNK, :]
    sq = _sigmoid(q)
    qh = q * sq
    e_qi = jnp.exp(b - b_mid)
    e_ki = jnp.exp(b_mid - b)
    e_qs = jnp.exp(b)
    e_ks = jnp.exp(b_last - b)
    dc = jnp.exp(b_last)
    q_in = (qh * e_qi) * HG_SCALE
    k_in = k * e_ki
    q_st = (qh * e_qs) * HG_SCALE
    k_st = k * e_ks
    att = _bdot(q_in.astype(BF16), k_in.astype(BF16), BNT_DIMS)
    att = jnp.where(_tri(False), att, 0.0)
    return dict(q=q, sig=sig, f=f, k=k, sq=sq, e_qi=e_qi, e_ki=e_ki, e_qs=e_qs, e_ks=e_ks, dc=dc,
                q_in=q_in, k_in=k_in, q_st=q_st, k_st=k_st, att=att)


def _hgrn_fwd(z, lb_logits, hg_g, nb, s_len):
    n = nb * s_len
    t = HG_T
    ns = s_len // t
    nchunk = s_len // CHUNK

    def body(q_ref, f_ref, v_ref, gb_ref, lg_ref, g_ref, o_ref, yb_ref, st_ref, st):
        @pl.when(pl.program_id(1) == 0)
        def _():
            st[...] = jnp.zeros((NB, BD, BD), F32)

        def head(h, carry):
            cols = pl.ds(pl.multiple_of(h * BD, BD), BD)
            lb = _lower_bound(lg_ref[:, cols])
            ck = _hg_tile(q_ref[0, :, cols], f_ref[0, :, cols], lb)
            vb = _chunks(v_ref[0, :, cols]).astype(BF16)
            kv = _bdot(vb, ck["k_st"].astype(BF16), BTN_DIMS)
            states = [st[h]]
            for c in range(HG_NC):
                states.append(states[c] * ck["dc"][c] + kv[c])
            st[h] = states[HG_NC]
            s_in = jnp.stack(states[:HG_NC], axis=0)
            st_ref[h] = s_in
            o = (_bdot(ck["att"].astype(BF16), vb, BNN_DIMS)
                 + _bdot(ck["q_st"].astype(BF16), s_in.astype(BF16), BNT_DIMS))
            o_ref[:, cols] = o.reshape(t, BD)
            r = lax.rsqrt(jnp.mean(o * o, axis=-1, keepdims=True) + EPS)
            gb = _chunks(gb_ref[0, :, cols])
            yb_ref[:, cols] = (((o * r) * g_ref[...]) * (gb * _sigmoid(gb))).astype(BF16).reshape(t, BD)
            return carry

        lax.fori_loop(0, NB, head, 0, unroll=4)

    seg = lambda j: pl.BlockSpec((1, t, D), lambda b, s: (j, b * ns + s, 0))
    tile = pl.BlockSpec((t, D), lambda b, s: (b * ns + s, 0))
    return pl.pallas_call(
        body, name="hgrn_fwd", grid=(nb, ns),
        in_specs=[seg(2), seg(3), seg(4), seg(5),
                  pl.BlockSpec((2, D), lambda b, s: (0, 0)), pl.BlockSpec((1, BD), lambda b, s: (0, 0))],
        out_specs=[tile, tile, pl.BlockSpec((NB, HG_NC, BD, BD), lambda b, s: (b, s, 0, 0))],
        out_shape=[SDS((n, D), F32), SDS((n, D), BF16), SDS((nb * NB, nchunk, BD, BD), F32)],
        scratch_shapes=[pltpu.VMEM((NB, BD, BD), F32)],
        compiler_params=_params(56),
    )(z, z, z, z, lb_logits, hg_g)


def _hgrn_bwd(z, o_all, st_all, dyb, lb_logits, hg_g, nb, s_len):
    n = nb * s_len
    t = HG_T
    ns = s_len // t

    def body(q_ref, f_ref, v_ref, gb_ref, o_ref, st_ref, dyb_ref, lg_ref, g_ref,
             dz_ref, glg_ref, ghg_ref, dst, dlb):
        b, s = pl.program_id(0), pl.program_id(1)

        @pl.when((b == 0) & (s == 0))
        def _():
            ghg_ref[...] = jnp.zeros((1, BD), F32)
            dlb[...] = jnp.zeros((8, D), F32)

        @pl.when(s == 0)
        def _():
            dst[...] = jnp.zeros((NB, BD, BD), F32)

        g = g_ref[...]

        def head(h, carry):
            cols = pl.ds(pl.multiple_of(h * BD, BD), BD)
            lb = _lower_bound(lg_ref[:, cols])
            ck = _hg_tile(q_ref[0, :, cols], f_ref[0, :, cols], lb)
            q = ck["q"]
            vb = _chunks(v_ref[0, :, cols]).astype(BF16)
            gb = _chunks(gb_ref[0, :, cols])
            o = _chunks(o_ref[:, cols])
            dyb_v = _chunks(dyb_ref[:, cols])
            s_in = st_ref[h]

            sgb = _sigmoid(gb)
            r = lax.rsqrt(jnp.mean(o * o, axis=-1, keepdims=True) + EPS)
            ohat = o * r
            d_on = dyb_v * (gb * sgb)
            d_gb = dyb_v * (ohat * g) * (sgb * (1.0 + gb * (1.0 - sgb)))
            ghg_ref[...] += jnp.sum(jnp.sum(d_on * ohat, axis=1), axis=0, keepdims=True)
            tt = d_on * g
            d_o = r * (tt - ohat * jnp.mean(tt * ohat, axis=-1, keepdims=True))
            dob = d_o.astype(BF16)

            attb = ck["att"].astype(BF16)
            q_inb, k_inb = ck["q_in"].astype(BF16), ck["k_in"].astype(BF16)
            q_stb, k_stb = ck["q_st"].astype(BF16), ck["k_st"].astype(BF16)
            d_att = jnp.where(_tri(False), _bdot(dob, vb, BNT_DIMS), 0.0).astype(BF16)
            d_q_in = _bdot(d_att, k_inb, BNN_DIMS)
            d_k_in = _bdot(d_att, q_inb, BTN_DIMS)
            d_q_st = _bdot(dob, s_in.astype(BF16), BNN_DIMS)
            qdo = _bdot(dob, q_stb, BTN_DIMS)
            d_states = [None] * HG_NC + [dst[h]]
            for c in reversed(range(HG_NC)):
                d_states[c] = d_states[c + 1] * ck["dc"][c] + qdo[c]
            dst[h] = d_states[0]
            ds_out = jnp.stack(d_states[1:], axis=0)
            dsb = ds_out.astype(BF16)
            d_v = _bdot(attb, dob, BTN_DIMS) + _bdot(k_stb, dsb, BNT_DIMS)
            d_k_st = _bdot(vb, dsb, BNN_DIMS)
            d_dc = jnp.sum(ds_out * s_in, axis=1, keepdims=True)

            p_qi = d_q_in * ck["q_in"]
            p_ki = d_k_in * ck["k_in"]
            p_qs = d_q_st * ck["q_st"]
            p_ks = d_k_st * ck["k_st"]
            d_qh = (d_q_in * ck["e_qi"] + d_q_st * ck["e_qs"]) * HG_SCALE
            d_k = d_k_in * ck["e_ki"] + d_k_st * ck["e_ks"]
            d_b = (p_qi - p_ki) + (p_qs - p_ks)
            d_b_mid = jnp.sum(p_ki - p_qi, axis=1, keepdims=True)
            d_b_last = jnp.sum(p_ks, axis=1, keepdims=True) + d_dc * ck["dc"]
            rowi = lax.broadcasted_iota(jnp.int32, (HG_NC, CHUNK, BD), 1)
            d_b = d_b + jnp.where(rowi == CHUNK // 2, d_b_mid, 0.0) + jnp.where(rowi == CHUNK - 1, d_b_last, 0.0)
            d_logf = _tri_sums(True, d_b)
            d_f = d_logf / ck["f"] - d_k
            sig, sq = ck["sig"], ck["sq"]
            d_fp = d_f * (1.0 - lb) * (sig * (1.0 - sig))
            dlb[0:1, cols] += jnp.sum(jnp.sum(d_f * (1.0 - sig), axis=1), axis=0, keepdims=True)
            d_q = d_qh * (sq * (1.0 + q * (1.0 - sq)))
            dz_ref[0, :, cols] = d_q.astype(BF16).reshape(t, BD)
            dz_ref[1, :, cols] = d_fp.astype(BF16).reshape(t, BD)
            dz_ref[2, :, cols] = d_v.astype(BF16).reshape(t, BD)
            dz_ref[3, :, cols] = d_gb.astype(BF16).reshape(t, BD)
            return carry

        lax.fori_loop(0, NB, head, 0, unroll=2)

        @pl.when((b == nb - 1) & (s == ns - 1))
        def _():
            lb = _lower_bound(lg_ref[...])
            dl = dlb[0:1, :] * (lb * (1.0 - lb))
            glg_ref[0:1, :] = dl
            glg_ref[1:2, :] = -dl

    rb = lambda b, s: b * ns + (ns - 1 - s)
    seg = lambda j: pl.BlockSpec((1, t, D), lambda b, s: (j, rb(b, s), 0))
    tile = pl.BlockSpec((t, D), lambda b, s: (rb(b, s), 0))
    return pl.pallas_call(
        body, name="hgrn_bwd", grid=(nb, ns),
        in_specs=[seg(2), seg(3), seg(4), seg(5), tile,
                  pl.BlockSpec((NB, HG_NC, BD, BD), lambda b, s: (b, ns - 1 - s, 0, 0)),
                  tile, pl.BlockSpec((2, D), lambda b, s: (0, 0)), pl.BlockSpec((1, BD), lambda b, s: (0, 0))],
        out_specs=[pl.BlockSpec((4, t, D), lambda b, s: (0, rb(b, s), 0)),
                   pl.BlockSpec((2, D), lambda b, s: (0, 0)), pl.BlockSpec((1, BD), lambda b, s: (0, 0))],
        out_shape=[SDS((4, n, D), BF16), SDS((2, D), F32), SDS((1, BD), F32)],
        scratch_shapes=[pltpu.VMEM((NB, BD, BD), F32), pltpu.VMEM((8, D), F32)],
        compiler_params=_params(60),
    )(z, z, z, z, o_all, st_all, dyb, lb_logits, hg_g)


def _mid(ya, yb, z, b_merge, x2, tgt, fin_g, pa, pb, wo):
    n = x2.shape[0]
    tm = 256
    ni = n // tm

    def body(ya_ref, yb_ref, gma_ref, gmb_ref, bm_ref, x_ref, t_ref, fg_ref, pa_hbm, pb_hbm, wo_hbm,
             dx2_ref, dya_ref, dyb_ref, dgm_ref, loss_ref, gfg_ref, gbm_ref, gm_hbm,
             pa_v, pb_v, wo_v, gpa_v, gpb_v, gwo_v, sem):
        i = pl.program_id(0)
        by_owner = lambda g: g.reshape(NB, BD, D)
        loads = [pltpu.make_async_copy(src, dst, sem.at[k])
                 for k, (src, dst) in enumerate(((pa_hbm, pa_v), (pb_hbm, pb_v), (wo_hbm, wo_v)))]
        stores = [pltpu.make_async_copy(src, dst, sem.at[k])
                  for k, (src, dst) in enumerate((g, gm_hbm.at[:, pl.ds(slot * BD, BD), :])
                                                 for slot, g in enumerate((gpa_v, gpb_v, gwo_v)))]

        @pl.when(i == 0)
        def _():
            for cp in loads:
                cp.start()
            for ref in (gpa_v, gpb_v, gwo_v, loss_ref, gfg_ref, gbm_ref):
                ref[...] = jnp.zeros(ref.shape, F32)
            for cp in loads:
                cp.wait()

        ya_v = ya_ref[...]
        yb_v = yb_ref[...]
        out_a = jnp.dot(ya_v, pa_v[...], preferred_element_type=F32)
        out_b = jnp.dot(yb_v, pb_v[...], preferred_element_type=F32)
        bm = bm_ref[...]
        g_a = _sigmoid(gma_ref[0] + bm[:, 0:D])
        g_b = _sigmoid(gmb_ref[0] + bm[:, D:2 * D])
        mixed = g_a * out_a + g_b * out_b
        mixb = mixed.astype(BF16)
        xo = x_ref[...] + jnp.dot(mixb, wo_v[...], preferred_element_type=F32)
        r = lax.rsqrt(jnp.mean(xo * xo, axis=-1, keepdims=True) + EPS)
        xn = xo * r
        fg = fg_ref[...]
        e = xn * fg - t_ref[...]
        loss_ref[...] += 0.5 * jnp.sum(jnp.mean(e * e, axis=-1, keepdims=True))
        dy = e * (1.0 / D)
        gfg_ref[...] += jnp.sum(dy * xn, axis=0, keepdims=True)
        dxn = dy * fg
        dx2 = r * (dxn - xn * jnp.mean(dxn * xn, axis=-1, keepdims=True))
        dx2_ref[...] = dx2
        dx2b = dx2.astype(BF16)
        d_mixed = lax.dot_general(dx2b, wo_v[...], NT_DIMS, preferred_element_type=F32)
        gwo_v[...] += by_owner(lax.dot_general(mixb, dx2b, TN_DIMS, preferred_element_type=F32))
        d_oa = (d_mixed * g_a).astype(BF16)
        d_ob = (d_mixed * g_b).astype(BF16)
        dgm_a = (d_mixed * out_a) * (g_a * (1.0 - g_a))
        dgm_b = (d_mixed * out_b) * (g_b * (1.0 - g_b))
        gbm_ref[:, 0:D] += jnp.sum(dgm_a, axis=0, keepdims=True)
        gbm_ref[:, D:2 * D] += jnp.sum(dgm_b, axis=0, keepdims=True)
        dgm_ref[0] = dgm_a.astype(BF16)
        dgm_ref[1] = dgm_b.astype(BF16)
        dya_ref[...] = lax.dot_general(d_oa, pa_v[...], NT_DIMS, preferred_element_type=F32)
        dyb_ref[...] = lax.dot_general(d_ob, pb_v[...], NT_DIMS, preferred_element_type=F32)
        gpa_v[...] += by_owner(lax.dot_general(ya_v, d_oa, TN_DIMS, preferred_element_type=F32))
        gpb_v[...] += by_owner(lax.dot_general(yb_v, d_ob, TN_DIMS, preferred_element_type=F32))

        @pl.when(i == ni - 1)
        def _():
            for cp in stores:
                cp.start()
            for cp in stores:
                cp.wait()

    rows = pl.BlockSpec((tm, D), lambda i: (i, 0))
    rep = lambda shape: pl.BlockSpec(shape, lambda i: (0,) * len(shape))
    return pl.pallas_call(
        body, name="mid", grid=(ni,),
        in_specs=[rows, rows,
                  pl.BlockSpec((1, tm, D), lambda i: (6, i, 0)), pl.BlockSpec((1, tm, D), lambda i: (7, i, 0)),
                  rep((1, 2 * D)), rows, rows, rep((1, D)), ANY, ANY, ANY],
        out_specs=[rows, rows, rows, pl.BlockSpec((2, tm, D), lambda i: (0, i, 0)),
                   rep((8, BD)), rep((1, D)), rep((1, 2 * D)), ANY],
        out_shape=[SDS((n, D), F32), SDS((n, D), F32), SDS((n, D), F32), SDS((2, n, D), BF16),
                   SDS((8, BD), F32), SDS((1, D), F32), SDS((1, 2 * D), F32),
                   SDS((NB, MID_ROWS, D), F32)],
        scratch_shapes=[pltpu.VMEM((D, D), BF16)] * 3 + [pltpu.VMEM((NB, BD, D), F32)] * 3 + [pltpu.SemaphoreType.DMA((3,))],
        compiler_params=_params(60),
    )(ya, yb, z, z, b_merge, x2, tgt, fin_g, pa, pb, wo)


def _dz_specs(tm, ni, row_major):
    if row_major:
        ia = lambda i, j: (jnp.minimum(j, 1), i, 0)
        ib = lambda i, j: (jnp.clip(j - 2, 0, 3), i, 0)
        im = lambda i, j: (jnp.clip(j - 6, 0, 1), i, 0)
    else:
        last = ni - 1
        ia = lambda j, i: (jnp.minimum(j, 1), jnp.where(j < 2, i, last), 0)
        ib = lambda j, i: (jnp.clip(j - 2, 0, 3), jnp.where(j < 2, 0, jnp.where(j < 6, i, last)), 0)
        im = lambda j, i: (jnp.clip(j - 6, 0, 1), jnp.where(j < 6, 0, i), 0)
    return [pl.BlockSpec((1, tm, D), f) for f in (ia, ib, im)]


def _inproj_bwd_x(dza, dzb, dzm, w_all, x2, dx2, norm_g, after):
    n = x2.shape[0]
    tm = 512
    ni = n // tm

    def body(dza_ref, dzb_ref, dzm_ref, w_ref, x_ref, dx2_ref, g_ref, after_ref, gx_ref, gg_ref, acc):
        i, j = pl.program_id(0), pl.program_id(1)

        @pl.when((i == 0) & (j == 0))
        def _():
            gg_ref[...] = jnp.zeros((1, D), F32)

        @pl.when(j == 0)
        def _():
            acc[...] = jnp.zeros((tm, D), F32)

        def add(ref):
            acc[...] += lax.dot_general(ref[0], w_ref[0], NT_DIMS, preferred_element_type=F32)

        pl.when(j < 2)(lambda: add(dza_ref))
        pl.when((j >= 2) & (j < 6))(lambda: add(dzb_ref))
        pl.when(j >= 6)(lambda: add(dzm_ref))

        @pl.when(j == NB - 1)
        def _():
            x = x_ref[...]
            r = lax.rsqrt(jnp.mean(x * x, axis=-1, keepdims=True) + EPS)
            xn = x * r
            dh = acc[...]
            gg_ref[...] += jnp.sum(dh * xn, axis=0, keepdims=True)
            dxn = dh * g_ref[...]
            gx_ref[...] = dx2_ref[...] + r * (dxn - xn * jnp.mean(dxn * xn, axis=-1, keepdims=True))

    rows = pl.BlockSpec((tm, D), lambda i, j: (i, 0))
    return pl.pallas_call(
        body, name="inproj_bwd_x", grid=(ni, NB),
        in_specs=_dz_specs(tm, ni, True) + [pl.BlockSpec((1, D, D), lambda i, j: (j, 0, 0)), rows, rows,
                                             pl.BlockSpec((1, D), lambda i, j: (0, 0)), ANY],
        out_specs=[rows, pl.BlockSpec((1, D), lambda i, j: (0, 0))],
        out_shape=[SDS((n, D), F32), SDS((1, D), F32)],
        scratch_shapes=[pltpu.VMEM((tm, D), F32)],
        compiler_params=_params(48),
    )(dza, dzb, dzm, w_all, x2, dx2, norm_g, after)


def _inproj_bwd_w(dza, dzb, dzm, h_all, g_m):
    n = h_all.shape[0]
    tm = min(n, 2048)
    ni = n // tm

    def body(dza_ref, dzb_ref, dzm_ref, h_ref, gm_hbm, gw_ref, got_w, got_m, stage, send_sems, recv_sems):
        j, i = pl.program_id(0), pl.program_id(1)
        x, y, c = _place()
        sibling = (x, y, 1 - c)

        def send_w(q):
            return pltpu.make_async_remote_copy(
                src_ref=stage.at[q % 2], dst_ref=got_w.at[q], send_sem=send_sems.at[q], recv_sem=recv_sems.at[q],
                device_id=sibling, device_id_type=MESH)

        def send_m(q):
            return pltpu.make_async_remote_copy(
                src_ref=gm_hbm.at[2 * q + (1 - c)], dst_ref=got_m.at[q], send_sem=send_sems.at[4 + q],
                recv_sem=recv_sems.at[4 + q], device_id=sibling, device_id_type=MESH)

        @pl.when((j == 0) & (i == 0))
        def _():
            for q in range(4):
                send_m(q).start()

        @pl.when(i == 0)
        def _():
            gw_ref[...] = jnp.zeros((1, D, D), F32)

        def add(ref):
            gw_ref[0] += lax.dot_general(h_ref[...], ref[0], TN_DIMS, preferred_element_type=F32)

        pl.when(j < 2)(lambda: add(dza_ref))
        pl.when((j >= 2) & (j < 6))(lambda: add(dzb_ref))
        pl.when(j >= 6)(lambda: add(dzm_ref))

        for q in range(4):
            @pl.when((i == ni - 1) & (j == 2 * q + 1 - c))
            def _(q=q):
                if q >= 2:
                    send_w(q - 2).wait_send()
                stage[q % 2] = gw_ref[0].astype(BF16)
                send_w(q).start()

        @pl.when((j == NB - 1) & (i == ni - 1))
        def _():
            for q in (2, 3):
                send_w(q).wait_send()
            for q in range(4):
                send_w(q).wait_recv()
                send_m(q).wait_send()
                send_m(q).wait_recv()

    return pl.pallas_call(
        body, name="inproj_bwd_w", grid=(NB, ni),
        in_specs=_dz_specs(tm, ni, False) + [pl.BlockSpec((tm, D), lambda j, i: (i, 0)), ANY],
        out_specs=[pl.BlockSpec((1, D, D), lambda j, i: (j, 0, 0)), ANY, ANY],
        out_shape=[SDS((NB, D, D), F32), SDS((4, D, D), BF16), SDS((4,) + g_m.shape[1:], F32)],
        scratch_shapes=[pltpu.VMEM((2, D, D), BF16), pltpu.SemaphoreType.DMA((8,)), pltpu.SemaphoreType.DMA((8,))],
        compiler_params=_params(58),
    )(dza, dzb, dzm, h_all, g_m)


def _adamw(w, g, m, v):
    rows, cols = w.shape
    tr = _row_tile(rows)

    spec = pl.BlockSpec((tr, cols), lambda i: (i, 0))
    return pl.pallas_call(
        functools.partial(_adam_refs), name="adamw", grid=(rows // tr,), in_specs=[spec] * 4, out_specs=[spec] * 3,
        out_shape=[SDS((rows, cols), F32)] * 3, compiler_params=_params(32),
    )(w, g, m, v)


def _adam_refs(w_ref, g_ref, m_ref, v_ref, d_ref, nm_ref, nv_ref):
    gv = g_ref[...]
    nm = ADAM_B1 * m_ref[...] + (1.0 - ADAM_B1) * gv
    nv = ADAM_B2 * v_ref[...] + (1.0 - ADAM_B2) * (gv * gv)
    m_hat = nm / (1.0 - ADAM_B1 ** ADAM_STEP)
    v_hat = nv / (1.0 - ADAM_B2 ** ADAM_STEP)
    d_ref[...] = -ADAM_LR * (m_hat / (jnp.sqrt(v_hat) + ADAM_EPS) + ADAM_WD * w_ref[...])
    nm_ref[...] = nm
    nv_ref[...] = nv


def _adamw_small(ws, gs, ms, vs):
    k = len(ws)

    def body(*refs):
        ins, outs = refs[:4 * k], refs[4 * k:7 * k]
        vin, vout = refs[7 * k:11 * k], refs[11 * k:14 * k]
        load_sems, store_sems = refs[14 * k:]
        loads = [pltpu.make_async_copy(ins[i], vin[i], load_sems.at[i]) for i in range(4 * k)]
        for cp in loads:
            cp.start()
        for cp in loads:
            cp.wait()
        for i in range(k):
            _adam_refs(*[vin[part * k + i] for part in range(4)], *[vout[part * k + i] for part in range(3)])
        stores = [pltpu.make_async_copy(vout[i], outs[i], store_sems.at[i]) for i in range(3 * k)]
        for cp in stores:
            cp.start()
        for cp in stores:
            cp.wait()

    shapes = [SDS(w.shape, F32) for w in ws]
    vmem = [pltpu.VMEM(w.shape, F32) for w in ws]
    out = pl.pallas_call(
        body, name="adamw_small", out_shape=shapes * 3, in_specs=[HBM] * (4 * k), out_specs=[HBM] * (3 * k),
        scratch_shapes=vmem * 7 + [pltpu.SemaphoreType.DMA((4 * k,)), pltpu.SemaphoreType.DMA((3 * k,))],
        compiler_params=_params(32),
    )(*ws, *gs, *ms, *vs)
    return out[:k], out[k:2 * k], out[2 * k:]


def _allgather(blocks, dtypes, name):
    na = len(blocks)

    def body(*refs):
        ins, outs, stages = refs[:na], refs[na:2 * na], refs[2 * na:3 * na]
        send_sems, recv_sems, local_sems = refs[3 * na:]
        x, y, c = _place()
        me, sibling = (x, y, c), (x, y, 1 - c)
        chips = [(1 - x, y), (x, 1 - y), (1 - x, 1 - y)]
        blk = lambda p: 4 * p[0] + 2 * p[1] + p[2]

        def copy(a, k, block, to, src=None):
            return pltpu.make_async_remote_copy(
                src_ref=outs[a].at[blk(block)] if src is None else src, dst_ref=outs[a].at[blk(block)],
                send_sem=send_sems.at[7 * a + k], recv_sem=recv_sems.at[7 * a + k],
                device_id=to, device_id_type=MESH)

        mine, first, passed = [], [], []
        for a in range(na):
            stages[a][...] = ins[a][...].astype(dtypes[a])
            mine.append(pltpu.make_async_copy(stages[a], outs[a].at[blk(me)], local_sems.at[a]))
            mine[-1].start()
            first.append(copy(a, 0, me, sibling, src=stages[a]))
            first += [copy(a, 1 + j, me, (*chip, c), src=stages[a]) for j, chip in enumerate(chips)]
        for cp in first:
            cp.start()
        for j, chip in enumerate(chips):
            for a in range(na):
                copy(a, 1 + j, (*chip, c), me).wait_recv()
                passed.append(copy(a, 4 + j, (*chip, c), sibling))
                passed[-1].start()
        for a in range(na):
            copy(a, 0, sibling, me).wait_recv()
            for j, chip in enumerate(chips):
                copy(a, 4 + j, (*chip, 1 - c), me).wait_recv()
        for cp in first + passed:
            cp.wait_send()
        for cp in mine:
            cp.wait()

    return pl.pallas_call(
        body, name=name,
        in_specs=[pl.BlockSpec(memory_space=pltpu.VMEM)] * na, out_specs=[ANY] * na,
        out_shape=[SDS((NB,) + b.shape, dt) for b, dt in zip(blocks, dtypes)],
        scratch_shapes=[pltpu.VMEM(b.shape, dt) for b, dt in zip(blocks, dtypes)]
        + [pltpu.SemaphoreType.DMA((7 * na,)), pltpu.SemaphoreType.DMA((7 * na,)), pltpu.SemaphoreType.DMA((na,))],
        compiler_params=_params(40),
    )(*blocks)


HBM = pl.BlockSpec(memory_space=pltpu.HBM)
SEMS = pl.BlockSpec(memory_space=pltpu.SEMAPHORE)
EFFECT = pltpu.SideEffectType.DATAFLOW_SIDE_EFFECTING


def _chip_copies(srcs, lands, send_sems, recv_sems):
    x, y, c = _place()
    return [pltpu.make_async_remote_copy(
        src_ref=srcs[a].at[slot], dst_ref=lands[a].at[slot],
        send_sem=send_sems.at[3 * a + slot], recv_sem=recv_sems.at[3 * a + slot],
        device_id=(px, py, c), device_id_type=MESH)
        for a in range(len(srcs)) for slot, (px, py) in enumerate(_other_chips(x, y))]


def _split_start(name, copies, per_array, srcs, lands, after=None):
    na = len(srcs)

    def body(*refs):
        send_sems, recv_sems = refs[-2 * na - 3], refs[-2 * na - 2]
        for cp in copies(refs[:na], refs[na:2 * na], send_sems, recv_sems):
            cp.start()
        refs[-1][...] = jnp.zeros_like(refs[-1])

    hbm = lambda a: pltpu.HBM(a.shape, a.dtype)
    pin = lambda a: pltpu.with_memory_space_constraint(a, pltpu.HBM)
    out = pl.pallas_call(
        body, name=name,
        out_shape=(pltpu.SemaphoreType.DMA((per_array * na,)), pltpu.SemaphoreType.DMA((per_array * na,)),
                   *[hbm(a) for a in srcs], *[hbm(a) for a in lands], SDS((8, BD), F32)),
        in_specs=[HBM] * (2 * na) + ([] if after is None else [ANY]),
        out_specs=(SEMS, SEMS, *[HBM] * (2 * na), pl.BlockSpec(memory_space=pltpu.VMEM)),
        input_output_aliases={i: 2 + i for i in range(2 * na)},
        compiler_params=pltpu.CompilerParams(has_side_effects=EFFECT),
    )(*[pin(a) for a in srcs], *[pin(a) for a in lands], *([] if after is None else [after]))
    return out[0], out[1], out[2:2 + na], out[2 + na:2 + 2 * na], out[-1]


def _split_wait(name, copies, started, after):
    send_sems, recv_sems, srcs, lands, _ = started
    na = len(srcs)

    def body(*refs):
        waits = copies(refs[:na], refs[na:2 * na], refs[2 * na], refs[2 * na + 1])
        for cp in waits:
            cp.wait_send()
        for cp in waits:
            cp.wait_recv()

    hbm = lambda a: pltpu.HBM(a.shape, a.dtype)
    out = pl.pallas_call(
        body, name=name,
        out_shape=(*[hbm(a) for a in srcs], *[hbm(a) for a in lands]),
        in_specs=[HBM] * (2 * na) + [SEMS, SEMS, ANY],
        out_specs=tuple([HBM] * (2 * na)),
        input_output_aliases={i: i for i in range(2 * na)},
        compiler_params=pltpu.CompilerParams(has_side_effects=EFFECT),
    )(*srcs, *lands, send_sems, recv_sems, after)
    return out[na:]


def _add_sibling(place, g, a_in):
    _, r, cols = g.shape
    tr = _row_tile(r)

    def chip(k, pr):
        qx = pr[0] if k in (1, 3) else 1 - pr[0]
        qy = pr[1] if k in (0, 3) else 1 - pr[1]
        return 2 * qx + qy

    def body(place_ref, *refs):
        g_refs, a_refs, (out_ref, own_ref) = refs[0:4], refs[4:8], refs[8:10]
        for k in range(3):
            out_ref[k] = (g_refs[k][0] + a_refs[k][0].astype(F32)).astype(BF16)
        own_ref[...] = g_refs[3][0] + a_refs[3][0].astype(F32)

    mine = lambda k: pl.BlockSpec((1, tr, cols), lambda i, pr: (2 * chip(k, pr) + pr[2], i, 0))
    theirs = lambda k: pl.BlockSpec((1, tr, cols), lambda i, pr: (chip(k, pr), i, 0))
    return pl.pallas_call(
        body, name="add_sibling",
        grid_spec=pltpu.PrefetchScalarGridSpec(
            num_scalar_prefetch=1, grid=(r // tr,),
            in_specs=[mine(k) for k in range(4)] + [theirs(k) for k in range(4)],
            out_specs=[pl.BlockSpec((3, tr, cols), lambda i, pr: (0, i, 0)),
                       pl.BlockSpec((tr, cols), lambda i, pr: (i, 0))]),
        out_shape=[SDS((3, r, cols), BF16), SDS((r, cols), F32)], compiler_params=_params(48),
    )(place, *[g] * 4, *[a_in] * 4)


def _add_chips(own, b_in):
    r, cols = own.shape
    tr = _row_tile(r)

    def body(p_ref, b0_ref, b1_ref, b2_ref, o_ref):
        o_ref[...] = ((p_ref[...] + b0_ref[0].astype(F32)) + b1_ref[0].astype(F32)) + b2_ref[0].astype(F32)

    slot = lambda k: pl.BlockSpec((1, tr, cols), lambda i: (k, i, 0))
    spec = pl.BlockSpec((tr, cols), lambda i: (i, 0))
    return pl.pallas_call(
        body, name="add_chips", grid=(r // tr,), in_specs=[spec, slot(0), slot(1), slot(2)], out_specs=spec,
        out_shape=SDS((r, cols), F32), compiler_params=_params(32),
    )(own, b_in, b_in, b_in)


VEC_NAMES = ("b_merge", "conv_b", "rg_bx", "rg_ba", "rg_lambda", "hg_lb_logits", "hg_norm_g", "final_norm_g")
REP_NAMES = ("rg_wx", "rg_wa", "norm_g") + VEC_NAMES
SMALL_AT = 3 * BD
SMALL_ROWS = 48
MID_ROWS = 448


def _sum_blocks(parts):
    def body(p_ref, o_ref):
        acc = p_ref[0]
        for k in range(1, NB):
            acc = acc + p_ref[k]
        o_ref[...] = acc

    return pl.pallas_call(body, name="sum_blocks", out_shape=SDS(parts.shape[1:], F32))(parts)


def _pack_rows(arrays, width, row_multiple=8):
    flat = jnp.concatenate([a.reshape(-1) for a in arrays])
    rows = -(-flat.shape[0] // width)
    rows = -(-rows // row_multiple) * row_multiple
    return jnp.pad(flat, (0, rows * width - flat.shape[0])).reshape(rows, width)


def _unpack(flat, like):
    out, off = [], 0
    for a in like:
        out.append(flat[off:off + a.size].reshape(a.shape))
        off += a.size
    return out


def kernel(x, w_in, b_merge, conv_w, conv_b, rg_wx, rg_bx, rg_wa, rg_ba, rg_lambda, hg_lb_logits, hg_norm_g, proj_a, proj_b, w_out, norm_g, final_norm_g, loss_target, m_w_in, m_b_merge, m_conv_w, m_conv_b, m_rg_wx, m_rg_bx, m_rg_wa, m_rg_ba, m_rg_lambda, m_hg_lb_logits, m_hg_norm_g, m_proj_a, m_proj_b, m_w_out, m_norm_g, m_final_norm_g, v_w_in, v_b_merge, v_conv_w, v_conv_b, v_rg_wx, v_rg_bx, v_rg_wa, v_rg_ba, v_rg_lambda, v_hg_lb_logits, v_hg_norm_g, v_proj_a, v_proj_b, v_w_out, v_norm_g, v_final_norm_g):
    weights = dict(w_in=w_in, b_merge=b_merge, conv_w=conv_w, conv_b=conv_b, rg_wx=rg_wx, rg_bx=rg_bx, rg_wa=rg_wa,
                   rg_ba=rg_ba, rg_lambda=rg_lambda, hg_lb_logits=hg_lb_logits, hg_norm_g=hg_norm_g, proj_a=proj_a,
                   proj_b=proj_b, w_out=w_out, norm_g=norm_g, final_norm_g=final_norm_g)
    mom1 = dict(w_in=m_w_in, b_merge=m_b_merge, conv_w=m_conv_w, conv_b=m_conv_b, rg_wx=m_rg_wx, rg_bx=m_rg_bx,
                rg_wa=m_rg_wa, rg_ba=m_rg_ba, rg_lambda=m_rg_lambda, hg_lb_logits=m_hg_lb_logits,
                hg_norm_g=m_hg_norm_g, proj_a=m_proj_a, proj_b=m_proj_b, w_out=m_w_out, norm_g=m_norm_g,
                final_norm_g=m_final_norm_g)
    mom2 = dict(w_in=v_w_in, b_merge=v_b_merge, conv_w=v_conv_w, conv_b=v_conv_b, rg_wx=v_rg_wx, rg_bx=v_rg_bx,
                rg_wa=v_rg_wa, rg_ba=v_rg_ba, rg_lambda=v_rg_lambda, hg_lb_logits=v_hg_lb_logits,
                hg_norm_g=v_hg_norm_g, proj_a=v_proj_a, proj_b=v_proj_b, w_out=v_w_out, norm_g=v_norm_g,
                final_norm_g=v_final_norm_g)
    order = list(weights)
    nb, s_len, _ = x.shape
    n = nb * s_len
    px, py, pc = _place()
    place = jnp.stack([px, py, pc]).astype(jnp.int32)

    in_hbm = lambda a: pltpu.with_memory_space_constraint(a, pltpu.HBM)
    norm_gain = in_hbm(norm_g)

    x2 = x.reshape(n, D)
    cw_blk = jnp.pad(conv_w[0], ((0, 4), (0, 0)))
    order_ids = jnp.stack([_block_id(p) for p in _arrival_order(px, py, pc)]).astype(jnp.int32)
    z, h_all, w_all, pa_all, pb_all, wo_all, cw_all = _gather_inproj(
        order_ids, x2, norm_gain, [w_in[0], proj_a[0], proj_b[0], w_out[0], cw_blk], [BF16, BF16, BF16, BF16, F32])
    pa_full, pb_full, wo_full = (a.reshape(D, D) for a in (pa_all, pb_all, wo_all))
    cw8 = in_hbm(cw_all.transpose(1, 0, 2).reshape(8, D))
    wx_b, wa_b = in_hbm(rg_wx[0].astype(BF16)), in_hbm(rg_wa[0].astype(BF16))
    cb, bx, ba, lam = (in_hbm(a.reshape(1, D)) for a in (conv_b, rg_bx, rg_ba, rg_lambda))
    fin_g, b_mrg = in_hbm(final_norm_g.reshape(1, D)), in_hbm(b_merge)
    lb_lg, hg_g = in_hbm(hg_lb_logits), in_hbm(hg_norm_g)

    hlru, ya = _lru_fwd(z, cw8, cb, wx_b, wa_b, bx, ba, lam, nb, s_len)
    o_all, yb, st_all = _hgrn_fwd(z, lb_lg, hg_g, nb, s_len)

    (dx2, dya, dyb, dzm, loss_acc, g_fin, g_bm, g_mid) = _mid(
        ya, yb, z, b_mrg, x2, loss_target.reshape(n, D), fin_g, pa_full, pb_full, wo_full)
    dzb, g_lg, g_hg = _hgrn_bwd(z, o_all, st_all, dyb, lb_lg, hg_g, nb, s_len)
    dza, g_cw8, g_cb, g_wx, g_wa, g_bx, g_ba, g_lam = _lru_bwd(
        z, hlru, dya, cw8, cb, wx_b, wa_b, bx, ba, lam, nb, s_len)

    part = dict(b_merge=g_bm, conv_b=g_cb, rg_bx=g_bx, rg_ba=g_ba, rg_lambda=g_lam, hg_lb_logits=g_lg,
                hg_norm_g=g_hg, final_norm_g=g_fin)
    vec = _pack_rows([part[k] for k in VEC_NAMES], BD)
    vec = jnp.pad(vec, ((0, 16 * NB - vec.shape[0]), (0, 0))).reshape(NB, 2, D)
    rows8 = lambda a: jnp.pad(a, ((0, 0), (0, 8 - a.shape[1]), (0, 0)))
    small = jnp.concatenate([g_wx.reshape(NB, 16, D), g_wa.reshape(NB, 16, D),
                             rows8(g_cw8.reshape(8, NB, BD).transpose(1, 0, 2).reshape(NB, 1, D)), rows8(vec),
                             jnp.zeros((NB, MID_ROWS - SMALL_AT - SMALL_ROWS, D), F32)], axis=1)
    g_m = lax.dynamic_update_slice(g_mid, small, (0, SMALL_AT, 0))
    g_w, w_from_sibling, m_from_sibling = _inproj_bwd_w(dza, dzb, dzm, h_all, g_m)
    w_out_bf, w_own = _add_sibling(place, g_w, w_from_sibling)
    m_out_bf, m_own = _add_sibling(place, g_m, m_from_sibling)
    outgoing = [w_out_bf, m_out_bf]
    chip_sums = _split_start("rs_chips_start", _chip_copies, 3, outgoing, [lax.empty(a.shape, a.dtype) for a in outgoing])
    grad_x, g_ng = _inproj_bwd_x(dza, dzb, dzm, w_all, x2, dx2, norm_gain, chip_sums[-1])
    from_chips = _split_wait("rs_chips_wait", _chip_copies, chip_sums, grad_x)
    r_w = _add_chips(w_own, from_chips[0])
    r_m = _add_chips(m_own, from_chips[1])
    row = lax.broadcasted_iota(jnp.int32, (8, D), 0)
    mine = jnp.where(row == 0, g_ng, jnp.where(row == 1, loss_acc[0:1, 0:1], 0.0))
    tail = jnp.concatenate([r_m[SMALL_AT:SMALL_AT + SMALL_ROWS], mine], axis=0)
    (tail_all,) = _allgather([tail], [F32], "gather_small_grads")
    summed = _sum_blocks(tail_all[:, SMALL_ROWS:SMALL_ROWS + 8])

    grads = dict(w_in=r_w.reshape(1, D, D),
                 proj_a=r_m[0:BD].reshape(1, BD, D), proj_b=r_m[BD:2 * BD].reshape(1, BD, D),
                 w_out=r_m[2 * BD:3 * BD].reshape(1, BD, D),
                 conv_w=r_m[SMALL_AT + 32].reshape(8, BD)[0:4].reshape(1, 4, BD),
                 rg_wx=tail_all[:, 0:16].reshape(1, NB, BD, BD), rg_wa=tail_all[:, 16:32].reshape(1, NB, BD, BD),
                 norm_g=summed[0:1])
    vec_all = tail_all[:, 40:42].reshape(-1)
    for k, gk in zip(VEC_NAMES, _unpack(vec_all, [weights[k] for k in VEC_NAMES])):
        grads[k] = gk

    delta, new_m, new_v = {}, {}, {}
    flat2 = lambda a: a.reshape(-1, a.shape[-1])
    for k in ("w_in", "proj_a", "proj_b", "w_out"):
        outs = _adamw(*[flat2(t[k]) for t in (weights, grads, mom1, mom2)])
        delta[k], new_m[k], new_v[k] = (a.reshape(weights[k].shape) for a in outs)
    rep = list(REP_NAMES) + ["conv_w"]
    outs = _adamw_small(*[[flat2(t[k]) for k in rep] for t in (weights, grads, mom1, mom2)])
    for tgt, arrays in zip((delta, new_m, new_v), outs):
        for k, a in zip(rep, arrays):
            tgt[k] = a.reshape(weights[k].shape)

    return (summed[1, 0], grad_x.reshape(x.shape), *[grads[k] for k in order], *[delta[k] for k in order],
            *[new_m[k] for k in order], *[new_v[k] for k in order])
```

```python
import functools

import jax
import jax.numpy as jnp
from jax import lax
from jax.experimental import pallas as pl
from jax.experimental.pallas import tpu as pltpu

F32 = jnp.float32
BF16 = jnp.bfloat16
SDS = jax.ShapeDtypeStruct
MESH = pl.DeviceIdType.MESH
ANY = pl.BlockSpec(memory_space=pl.ANY)

D = 1024
NB = 8
BD = D // NB
CHUNK = 64
EPS = 1e-6
LRU_C = 8.0
HG_SCALE = BD ** -0.5
ADAM_LR, ADAM_B1, ADAM_B2, ADAM_EPS, ADAM_WD, ADAM_STEP = 0.001, 0.9, 0.999, 1e-08, 0.01, 10

NT_DIMS = (((1,), (1,)), ((), ()))
TN_DIMS = (((0,), (0,)), ((), ()))


def _params(vmem_mib):
    return pltpu.CompilerParams(vmem_limit_bytes=vmem_mib << 20)


def _row_tile(rows, most=256):
    assert rows % 8 == 0
    return max(t for t in range(8, min(rows, most) + 1, 8) if rows % t == 0)


def _sigmoid(v):
    return 0.5 * (jnp.tanh(0.5 * v) + 1.0)


def _groups(v):
    return v.reshape(v.shape[0] // 8, 8, v.shape[1])


def _softplus_neg(lam):
    t = -lam
    e = jnp.exp(-jnp.abs(t))
    w = 1.0 + e
    d = w - 1.0
    l1p = jnp.where(d == 0.0, e, jnp.log(w) * (e / jnp.where(d == 0.0, 1.0, d)))
    return jnp.maximum(t, 0.0) + l1p


def _place():
    return lax.axis_index("x"), lax.axis_index("y"), lax.axis_index("c")


def _other_chips(x, y):
    return [(1 - x, y), (x, 1 - y), (1 - x, 1 - y)]


def _block_id(p):
    return 4 * p[0] + 2 * p[1] + p[2]


def _core_chips(x, y, c):
    near, far, diag = _other_chips(x, y)
    pick = lambda a, b: (jnp.where(c == 0, a[0], b[0]), jnp.where(c == 0, a[1], b[1]))
    return [pick(near, far), pick(far, near), diag]


def _arrival_order(x, y, c):
    first, second, diag = _core_chips(x, y, c)
    return [(x, y, c), (x, y, 1 - c), (*first, c), (*second, 1 - c), (*second, c), (*first, 1 - c),
            (*diag, c), (*diag, 1 - c)]


def _gather_inproj(order_ids, x2, norm_g, blocks, dtypes):
    na = len(blocks)
    n = x2.shape[0]
    tm = min(n, 1024)
    ni = n // tm

    def body(order_ref, x_ref, g_ref, *refs):
        ins, (z_ref, h_ref), outs = refs[:na], refs[na:na + 2], refs[na + 2:2 * na + 2]
        stages = refs[2 * na + 2:3 * na + 2]
        h_full, wbuf, send_sems, recv_sems, local_sems, wsems, hsem = refs[3 * na + 2:]
        j, i = pl.program_id(0), pl.program_id(1)
        x, y, c = _place()
        me, sibling = (x, y, c), (x, y, 1 - c)
        chips = _core_chips(x, y, c)
        sibling_chips = [chips[1], chips[0], chips[2]]
        small = range(1, na)

        def copy(a, k, block, to, src=None):
            return pltpu.make_async_remote_copy(
                src_ref=outs[a].at[_block_id(block)] if src is None else src, dst_ref=outs[a].at[_block_id(block)],
                send_sem=send_sems.at[7 * a + k], recv_sem=recv_sems.at[7 * a + k],
                device_id=to, device_id_type=MESH)

        def local(a):
            return pltpu.make_async_copy(stages[a], outs[a].at[_block_id(me)], local_sems.at[a])

        def landed(a, slot):
            copy(a, 1 + slot, (*chips[slot], c), me).wait_recv()
            copy(a, 4 + slot, (*chips[slot], c), sibling).start()
            if slot == 0:
                copy(a, 3, (*chips[0], c), (*chips[1], c)).start()

        def diagonal_and_small():
            landed(0, 2)
            for a in small:
                landed(a, 0)
                landed(a, 1)

        def passed_on(a, slot):
            copy(a, 4 + slot, (*sibling_chips[slot], 1 - c), me).wait_recv()

        def sibling_here_send_second():
            copy(0, 0, sibling, me).wait_recv()
            for a in range(na):
                copy(a, 2, me, (*chips[1], c), src=stages[a]).start()

        @pl.when((j == 0) & (i == 0))
        def _():
            for a in range(na):
                stages[a][...] = ins[a][...].astype(dtypes[a])
                local(a).start()
            for a in range(na):
                copy(a, 0, me, sibling, src=stages[a]).start()
                copy(a, 1, me, (*chips[0], c), src=stages[a]).start()

        @pl.when(j == 0)
        def _():
            xv = x_ref[...]
            r = lax.rsqrt(jnp.mean(xv * xv, axis=-1, keepdims=True) + EPS)
            hb = ((xv * r) * g_ref[...]).astype(BF16)
            h_full[pl.ds(pl.multiple_of(i * tm, tm), tm), :] = hb

        save_h = pltpu.make_async_copy(h_full, h_ref, hsem)
        pl.when((j == 0) & (i == ni - 1))(save_h.start)

        steps = [
            lambda: local(0).wait(),
            sibling_here_send_second,
            lambda: landed(0, 0),
            lambda: passed_on(0, 0),
            lambda: landed(0, 1),
            lambda: passed_on(0, 1),
            diagonal_and_small,
            lambda: passed_on(0, 2),
        ]
        def w_load(k):
            return pltpu.make_async_copy(outs[0].at[order_ref[k]], wbuf.at[k % 2], wsems.at[k % 2])

        for k, step in enumerate(steps):
            @pl.when((j == 0) & (i == 0) if k == 0 else (j == k - 1) & (i == ni - 1))
            def _(k=k, step=step):
                step()
                w_load(k).start()

        pl.when(i == 0)(lambda: w_load(j).wait())
        z_ref[0] = jnp.dot(h_full[pl.ds(pl.multiple_of(i * tm, tm), tm), :], wbuf[j % 2], preferred_element_type=F32)

        @pl.when((j == NB - 1) & (i == ni - 1))
        def _():
            save_h.wait()
            for a in small:
                landed(a, 2)
            for a in small:
                local(a).wait()
                copy(a, 0, sibling, me).wait_recv()
                for slot in range(3):
                    passed_on(a, slot)
            for a in range(na):
                copy(a, 0, me, sibling, src=stages[a]).wait_send()
                for slot, chip in enumerate(chips):
                    copy(a, 1 + slot, me, (*chip, c), src=stages[a]).wait_send()
                    copy(a, 4 + slot, (*chip, c), sibling).wait_send()

    rows_once = lambda j, i, order: (jnp.where(j == 0, i, ni - 1), 0)
    vmem = pl.BlockSpec(memory_space=pltpu.VMEM)
    return pl.pallas_call(
        body, name="gather_inproj",
        grid_spec=pltpu.PrefetchScalarGridSpec(
            num_scalar_prefetch=1, grid=(NB, ni),
            in_specs=[pl.BlockSpec((tm, D), rows_once), pl.BlockSpec((1, D), lambda j, i, order: (0, 0))] + [vmem] * na,
            out_specs=[pl.BlockSpec((1, tm, D), lambda j, i, order: (order[j], i, 0)), ANY] + [ANY] * na,
            scratch_shapes=[pltpu.VMEM(b.shape, dt) for b, dt in zip(blocks, dtypes)]
            + [pltpu.VMEM((n, D), BF16), pltpu.VMEM((2, D, D), BF16),
               pltpu.SemaphoreType.DMA((7 * na,)), pltpu.SemaphoreType.DMA((7 * na,)),
               pltpu.SemaphoreType.DMA((na,)), pltpu.SemaphoreType.DMA((2,)), pltpu.SemaphoreType.DMA(())]),
        out_shape=[SDS((NB, n, D), F32), SDS((n, D), BF16)] + [SDS((NB,) + b.shape, dt) for b, dt in zip(blocks, dtypes)],
        compiler_params=_params(56),
    )(order_ids, x2, norm_g, *blocks)


LRU_T = 256


def _shifted(groups, shifts):
    row = lax.broadcasted_iota(jnp.int32, (groups.shape[0] - 1,) + groups.shape[1:], 1)
    out = []
    for s in shifts:
        y = pltpu.roll(groups, s % 8, 1)
        moved = jnp.where(row >= s, y[1:], y[:-1]) if s > 0 else jnp.where(row < 8 + s, y[:-1], y[1:])
        out.append(moved.reshape(-1, groups.shape[2]))
    return out


def _conv(taps, cw, cb):
    acc = taps[0] * cw[0:1, :] + taps[1] * cw[1:2, :]
    acc = acc + taps[2] * cw[2:3, :]
    acc = acc + taps[3] * cw[3:4, :]
    return cb + acc


def _lru_gates(xa, wx_ref, wa_ref, bx, ba, lam):
    xab = xa.astype(BF16)
    pis, prs = [], []
    for h in range(NB):
        xs = xab[:, h * BD:(h + 1) * BD]
        pis.append(jnp.dot(xs, wx_ref[h], preferred_element_type=F32))
        prs.append(jnp.dot(xs, wa_ref[h], preferred_element_type=F32))
    gi = _sigmoid(jnp.concatenate(pis, axis=1) + bx)
    gr = _sigmoid(jnp.concatenate(prs, axis=1) + ba)
    sp = _softplus_neg(lam)
    log_a = (-LRU_C * gr) * sp
    a = jnp.exp(log_a)
    mult = jnp.sqrt(-jnp.tanh(log_a) * (a * a + 1.0))
    return xab, gi, gr, sp, a, mult


def _lru_fwd(z, cw8, cb, wx, wa, bx, ba, lam, nb, s_len):
    n = nb * s_len
    t = LRU_T
    ns = s_len // t

    def body(xp_ref, ga_ref, cw_ref, cb_ref, wx_ref, wa_ref, bx_ref, ba_ref, lam_ref,
             h_ref, ya_ref, ext, a_s, u_s, carry):
        @pl.when(pl.program_id(1) == 0)
        def _():
            ext[0:8, :] = jnp.zeros((8, D), F32)
            carry[...] = jnp.zeros((8, D), F32)

        xp = xp_ref[0]
        ext[8:8 + t, :] = xp
        xa = _conv(_shifted(_groups(ext[...]), (3, 2, 1)) + [xp], cw_ref[...], cb_ref[...])
        ext[0:8, :] = xp[t - 8:t, :]
        _, gi, _, _, a, mult = _lru_gates(xa, wx_ref, wa_ref, bx_ref[...], ba_ref[...], lam_ref[...])
        u = (mult * gi) * xa
        a, u = _groups(a), _groups(u)
        row = lax.broadcasted_iota(jnp.int32, a.shape, 1)
        for sh in (1, 2, 4):
            a_sh = pltpu.roll(a, sh, 1)
            u_sh = pltpu.roll(u, sh, 1)
            m = row >= sh
            u = jnp.where(m, a * u_sh + u, u)
            a = jnp.where(m, a * a_sh, a)
        a_s[...] = a.reshape(t, D)
        u_s[...] = u.reshape(t, D)

        def step(g, c):
            r = pl.multiple_of(g * 8, 8)
            hg = u_s[pl.ds(r, 8), :] + a_s[pl.ds(r, 8), :] * c
            h_ref[pl.ds(r, 8), :] = hg
            return hg[7:8, :]

        c_out = lax.fori_loop(0, t // 8, step, carry[0:1, :], unroll=4)
        carry[0:1, :] = c_out
        ga = ga_ref[0]
        ya_ref[...] = (h_ref[...] * (ga * _sigmoid(ga))).astype(BF16)

    row_map = lambda b, s: (b * ns + s, 0)
    rep2 = lambda b, s: (0, 0)
    rep3 = lambda b, s: (0, 0, 0)
    return pl.pallas_call(
        body, name="lru_fwd", grid=(nb, ns),
        in_specs=[pl.BlockSpec((1, t, D), lambda b, s: (0, b * ns + s, 0)),
                  pl.BlockSpec((1, t, D), lambda b, s: (1, b * ns + s, 0)),
                  pl.BlockSpec((8, D), rep2), pl.BlockSpec((1, D), rep2),
                  pl.BlockSpec((NB, BD, BD), rep3), pl.BlockSpec((NB, BD, BD), rep3),
                  pl.BlockSpec((1, D), rep2), pl.BlockSpec((1, D), rep2), pl.BlockSpec((1, D), rep2)],
        out_specs=[pl.BlockSpec((t, D), row_map), pl.BlockSpec((t, D), row_map)],
        out_shape=[SDS((n, D), F32), SDS((n, D), BF16)],
        scratch_shapes=[pltpu.VMEM((t + 8, D), F32), pltpu.VMEM((t, D), F32), pltpu.VMEM((t, D), F32),
                        pltpu.VMEM((8, D), F32)],
        compiler_params=_params(48),
    )(z, z, cw8, cb, wx, wa, bx, ba, lam)


def _lru_bwd(z, h_all, dya, cw8, cb, wx, wa, bx, ba, lam, nb, s_len):
    n = nb * s_len
    t = LRU_T
    ns = s_len // t
    t8 = t // 8

    def body(xp_ref, xph_ref, ga_ref, h_ref, hh_ref, dya_ref, cw_ref, cb_ref, wx_ref, wa_ref, bx_ref, ba_ref,
             lam_ref, dz_ref, gcw_ref, gcb_ref, gwx_ref, gwa_ref, gbx_ref, gba_ref, glam_ref,
             ext, hext, dext, a_s, u_s, dh_s, carry):
        b, s = pl.program_id(0), pl.program_id(1)
        first_tile = s == ns - 1

        @pl.when((b == 0) & (s == 0))
        def _():
            for ref in (gcw_ref, gcb_ref, gwx_ref, gwa_ref, gbx_ref, gba_ref, glam_ref):
                ref[...] = jnp.zeros(ref.shape, F32)

        @pl.when(s == 0)
        def _():
            dext[t:t + 8, :] = jnp.zeros((8, D), F32)
            carry[...] = jnp.zeros((8, D), F32)

        keep = jnp.where(first_tile, 0.0, 1.0)
        xp = xp_ref[0]
        ext[0:8, :] = xph_ref[0] * keep
        ext[8:8 + t, :] = xp
        hext[0:8, :] = hh_ref[...] * keep
        hext[8:8 + t, :] = h_ref[...]
        cw = cw_ref[...]
        lam = lam_ref[...]
        taps = _shifted(_groups(ext[...]), (3, 2, 1)) + [xp]
        xa = _conv(taps, cw, cb_ref[...])
        xab, gi, gr, sp, a, mult = _lru_gates(xa, wx_ref, wa_ref, bx_ref[...], ba_ref[...], lam)
        (h_prev,) = _shifted(_groups(hext[...]), (1,))
        ga = ga_ref[0]
        sg = _sigmoid(ga)
        dya_v = dya_ref[...]
        d_ga = dya_v * h_ref[...] * (sg * (1.0 + ga * (1.0 - sg)))
        g_in = dya_v * (ga * sg)

        (an,) = _shifted(jnp.concatenate([_groups(a), jnp.ones((1, 8, D), F32)], axis=0), (-1,))
        an, u = _groups(an), _groups(g_in)
        row = lax.broadcasted_iota(jnp.int32, an.shape, 1)
        for sh in (1, 2, 4):
            a_sh = pltpu.roll(an, 8 - sh, 1)
            u_sh = pltpu.roll(u, 8 - sh, 1)
            m = row < 8 - sh
            u = jnp.where(m, u + an * u_sh, u)
            an = jnp.where(m, an * a_sh, an)
        a_s[...] = an.reshape(t, D)
        u_s[...] = u.reshape(t, D)

        def step(i, c):
            r = pl.multiple_of((t8 - 1 - i) * 8, 8)
            dg = u_s[pl.ds(r, 8), :] + a_s[pl.ds(r, 8), :] * c
            dh_s[pl.ds(r, 8), :] = dg
            return dg[0:1, :]

        lax.fori_loop(0, t8, step, carry[0:1, :], unroll=4)
        dh = dh_s[...]
        carry[0:1, :] = a[0:1, :] * dh[0:1, :]

        d_a = dh * h_prev
        dux = dh * xa
        d_mult = dux * gi
        d_gi = dux * mult
        d_xa = dh * (mult * gi)
        d_loga = d_a * a - d_mult * ((a * a) / mult)
        d_gr = d_loga * (-LRU_C * sp)
        d_sp = jnp.sum(d_loga * (-LRU_C * gr), axis=0, keepdims=True)
        glam_ref[...] += d_sp * (-_sigmoid(-lam))
        d_pi = d_gi * gi * (1.0 - gi)
        d_pr = d_gr * gr * (1.0 - gr)
        gbx_ref[...] += jnp.sum(d_pi, axis=0, keepdims=True)
        gba_ref[...] += jnp.sum(d_pr, axis=0, keepdims=True)
        dpib = d_pi.astype(BF16)
        dprb = d_pr.astype(BF16)
        back = []
        for h in range(NB):
            cs = slice(h * BD, (h + 1) * BD)
            gwx_ref[h] += lax.dot_general(xab[:, cs], dpib[:, cs], TN_DIMS, preferred_element_type=F32)
            gwa_ref[h] += lax.dot_general(xab[:, cs], dprb[:, cs], TN_DIMS, preferred_element_type=F32)
            back.append(lax.dot_general(dpib[:, cs], wx_ref[h], NT_DIMS, preferred_element_type=F32)
                        + lax.dot_general(dprb[:, cs], wa_ref[h], NT_DIMS, preferred_element_type=F32))
        d_xa = d_xa + jnp.concatenate(back, axis=1)

        dext[0:t, :] = d_xa
        later = _shifted(_groups(dext[...]), (-3, -2, -1))
        d_xp = later[0] * cw[0:1, :] + later[1] * cw[1:2, :]
        d_xp = d_xp + later[2] * cw[2:3, :]
        d_xp = d_xp + d_xa * cw[3:4, :]
        dext[t:t + 8, :] = d_xa[0:8, :]
        gcb_ref[...] += jnp.sum(d_xa, axis=0, keepdims=True)
        for k in range(4):
            gcw_ref[k:k + 1, :] += jnp.sum(d_xa * taps[k], axis=0, keepdims=True)
        dz_ref[0] = d_xp.astype(BF16)
        dz_ref[1] = d_ga.astype(BF16)

    rb = lambda b, s: b * ns + (ns - 1 - s)
    halo = lambda b, s: jnp.maximum(rb(b, s) * t8 - 1, 0)
    rep2 = lambda b, s: (0, 0)
    rep3 = lambda b, s: (0, 0, 0)
    return pl.pallas_call(
        body, name="lru_bwd", grid=(nb, ns),
        in_specs=[pl.BlockSpec((1, t, D), lambda b, s: (0, rb(b, s), 0)),
                  pl.BlockSpec((1, 8, D), lambda b, s: (0, halo(b, s), 0)),
                  pl.BlockSpec((1, t, D), lambda b, s: (1, rb(b, s), 0)),
                  pl.BlockSpec((t, D), lambda b, s: (rb(b, s), 0)),
                  pl.BlockSpec((8, D), lambda b, s: (halo(b, s), 0)),
                  pl.BlockSpec((t, D), lambda b, s: (rb(b, s), 0)),
                  pl.BlockSpec((8, D), rep2), pl.BlockSpec((1, D), rep2),
                  pl.BlockSpec((NB, BD, BD), rep3), pl.BlockSpec((NB, BD, BD), rep3),
                  pl.BlockSpec((1, D), rep2), pl.BlockSpec((1, D), rep2), pl.BlockSpec((1, D), rep2)],
        out_specs=[pl.BlockSpec((2, t, D), lambda b, s: (0, rb(b, s), 0)),
                   pl.BlockSpec((8, D), rep2), pl.BlockSpec((1, D), rep2),
                   pl.BlockSpec((NB, BD, BD), rep3), pl.BlockSpec((NB, BD, BD), rep3),
                   pl.BlockSpec((1, D), rep2), pl.BlockSpec((1, D), rep2), pl.BlockSpec((1, D), rep2)],
        out_shape=[SDS((2, n, D), BF16), SDS((8, D), F32), SDS((1, D), F32),
                   SDS((NB, BD, BD), F32), SDS((NB, BD, BD), F32),
                   SDS((1, D), F32), SDS((1, D), F32), SDS((1, D), F32)],
        scratch_shapes=[pltpu.VMEM((t + 8, D), F32), pltpu.VMEM((t + 8, D), F32), pltpu.VMEM((t + 8, D), F32),
                        pltpu.VMEM((t, D), F32), pltpu.VMEM((t, D), F32), pltpu.VMEM((t, D), F32),
                        pltpu.VMEM((8, D), F32)],
        compiler_params=_params(56),
    )(z, z, z, h_all, h_all, dya, cw8, cb, wx, wa, bx, ba, lam)


HG_T = 512
HG_NC = HG_T // CHUNK
BNT_DIMS = (((2,), (2,)), ((0,), (0,)))
BNN_DIMS = (((2,), (1,)), ((0,), (0,)))
BTN_DIMS = (((1,), (1,)), ((0,), (0,)))


def _lower_bound(lg):
    m = jnp.max(lg, axis=0, keepdims=True)
    e = jnp.exp(lg - m)
    return e[0:1, :] / jnp.sum(e, axis=0, keepdims=True)


def _tri(upper):
    r = lax.broadcasted_iota(jnp.int32, (HG_NC, CHUNK, CHUNK), 1)
    c = lax.broadcasted_iota(jnp.int32, (HG_NC, CHUNK, CHUNK), 2)
    return (c >= r) if upper else (r >= c)


def _bdot(a, b, dims):
    return lax.dot_general(a, b, dims, preferred_element_type=F32)


def _tri_sums(upper, a):
    tri = _tri(upper).astype(BF16)
    a1 = a.astype(BF16)
    r1 = a - a1.astype(F32)
    a2 = r1.astype(BF16)
    a3 = (r1 - a2.astype(F32)).astype(BF16)
    return _bdot(tri, a1, BNN_DIMS) + (_bdot(tri, a2, BNN_DIMS) + _bdot(tri, a3, BNN_DIMS))


def _chunks(a):
    return a.reshape(HG_NC, CHUNK, BD)


def _hg_tile(q, fp, lb):
    q, fp = _chunks(q), _chunks(fp)
    sig = _sigmoid(fp)
    f = lb + (1.0 - lb) * sig
    log_f = jnp.log(f)
    k = 1.0 - f
    b = _tri_sums(False, log_f)
    b_mid = b[:, CHUNK // 2:CHUNK // 2 + 1, :]
    b_last = b[:, CHUNK - 1:CHUNK, :]
    sq = _sigmoid(q)
    qh = q * sq
    e_qi = jnp.exp(b - b_mid)
    e_ki = jnp.exp(b_mid - b)
    e_qs = jnp.exp(b)
    e_ks = jnp.exp(b_last - b)
    dc = jnp.exp(b_last)
    q_in = (qh * e_qi) * HG_SCALE
    k_in = k * e_ki
    q_st = (qh * e_qs) * HG_SCALE
    k_st = k * e_ks
    att = _bdot(q_in.astype(BF16), k_in.astype(BF16), BNT_DIMS)
    att = jnp.where(_tri(False), att, 0.0)
    return dict(q=q, sig=sig, f=f, k=k, sq=sq, e_qi=e_qi, e_ki=e_ki, e_qs=e_qs, e_ks=e_ks, dc=dc,
                q_in=q_in, k_in=k_in, q_st=q_st, k_st=k_st, att=att)


def _hgrn_fwd(z, lb_logits, hg_g, nb, s_len):
    n = nb * s_len
    t = HG_T
    ns = s_len // t
    nchunk = s_len // CHUNK

    def body(q_ref, f_ref, v_ref, gb_ref, lg_ref, g_ref, o_ref, yb_ref, st_ref, st):
        @pl.when(pl.program_id(1) == 0)
        def _():
            st[...] = jnp.zeros((NB, BD, BD), F32)

        def head(h, carry):
            cols = pl.ds(pl.multiple_of(h * BD, BD), BD)
            lb = _lower_bound(lg_ref[:, cols])
            ck = _hg_tile(q_ref[0, :, cols], f_ref[0, :, cols], lb)
            vb = _chunks(v_ref[0, :, cols]).astype(BF16)
            kv = _bdot(vb, ck["k_st"].astype(BF16), BTN_DIMS)
            states = [st[h]]
            for c in range(HG_NC):
                states.append(states[c] * ck["dc"][c] + kv[c])
            st[h] = states[HG_NC]
            s_in = jnp.stack(states[:HG_NC], axis=0)
            st_ref[h] = s_in
            o = (_bdot(ck["att"].astype(BF16), vb, BNN_DIMS)
                 + _bdot(ck["q_st"].astype(BF16), s_in.astype(BF16), BNT_DIMS))
            o_ref[:, cols] = o.reshape(t, BD)
            r = lax.rsqrt(jnp.mean(o * o, axis=-1, keepdims=True) + EPS)
            gb = _chunks(gb_ref[0, :, cols])
            yb_ref[:, cols] = (((o * r) * g_ref[...]) * (gb * _sigmoid(gb))).astype(BF16).reshape(t, BD)
            return carry

        lax.fori_loop(0, NB, head, 0, unroll=4)

    seg = lambda j: pl.BlockSpec((1, t, D), lambda b, s: (j, b * ns + s, 0))
    tile = pl.BlockSpec((t, D), lambda b, s: (b * ns + s, 0))
    return pl.pallas_call(
        body, name="hgrn_fwd", grid=(nb, ns),
        in_specs=[seg(2), seg(3), seg(4), seg(5),
                  pl.BlockSpec((2, D), lambda b, s: (0, 0)), pl.BlockSpec((1, BD), lambda b, s: (0, 0))],
        out_specs=[tile, tile, pl.BlockSpec((NB, HG_NC, BD, BD), lambda b, s: (b, s, 0, 0))],
        out_shape=[SDS((n, D), F32), SDS((n, D), BF16), SDS((nb * NB, nchunk, BD, BD), F32)],
        scratch_shapes=[pltpu.VMEM((NB, BD, BD), F32)],
        compiler_params=_params(56),
    )(z, z, z, z, lb_logits, hg_g)


def _hgrn_bwd(z, o_all, st_all, dyb, lb_logits, hg_g, nb, s_len):
    n = nb * s_len
    t = HG_T
    ns = s_len // t

    def body(q_ref, f_ref, v_ref, gb_ref, o_ref, st_ref, dyb_ref, lg_ref, g_ref,
             dz_ref, glg_ref, ghg_ref, dst, dlb):
        b, s = pl.program_id(0), pl.program_id(1)

        @pl.when((b == 0) & (s == 0))
        def _():
            ghg_ref[...] = jnp.zeros((1, BD), F32)
            dlb[...] = jnp.zeros((8, D), F32)

        @pl.when(s == 0)
        def _():
            dst[...] = jnp.zeros((NB, BD, BD), F32)

        g = g_ref[...]

        def head(h, carry):
            cols = pl.ds(pl.multiple_of(h * BD, BD), BD)
            lb = _lower_bound(lg_ref[:, cols])
            ck = _hg_tile(q_ref[0, :, cols], f_ref[0, :, cols], lb)
            q = ck["q"]
            vb = _chunks(v_ref[0, :, cols]).astype(BF16)
            gb = _chunks(gb_ref[0, :, cols])
            o = _chunks(o_ref[:, cols])
            dyb_v = _chunks(dyb_ref[:, cols])
            s_in = st_ref[h]

            sgb = _sigmoid(gb)
            r = lax.rsqrt(jnp.mean(o * o, axis=-1, keepdims=True) + EPS)
            ohat = o * r
            d_on = dyb_v * (gb * sgb)
            d_gb = dyb_v * (ohat * g) * (sgb * (1.0 + gb * (1.0 - sgb)))
            ghg_ref[...] += jnp.sum(jnp.sum(d_on * ohat, axis=1), axis=0, keepdims=True)
            tt = d_on * g
            d_o = r * (tt - ohat * jnp.mean(tt * ohat, axis=-1, keepdims=True))
            dob = d_o.astype(BF16)

            attb = ck["att"].astype(BF16)
            q_inb, k_inb = ck["q_in"].astype(BF16), ck["k_in"].astype(BF16)
            q_stb, k_stb = ck["q_st"].astype(BF16), ck["k_st"].astype(BF16)
            d_att = jnp.where(_tri(False), _bdot(dob, vb, BNT_DIMS), 0.0).astype(BF16)
            d_q_in = _bdot(d_att, k_inb, BNN_DIMS)
            d_k_in = _bdot(d_att, q_inb, BTN_DIMS)
            d_q_st = _bdot(dob, s_in.astype(BF16), BNN_DIMS)
            qdo = _bdot(dob, q_stb, BTN_DIMS)
            d_states = [None] * HG_NC + [dst[h]]
            for c in reversed(range(HG_NC)):
                d_states[c] = d_states[c + 1] * ck["dc"][c] + qdo[c]
            dst[h] = d_states[0]
            ds_out = jnp.stack(d_states[1:], axis=0)
            dsb = ds_out.astype(BF16)
            d_v = _bdot(attb, dob, BTN_DIMS) + _bdot(k_stb, dsb, BNT_DIMS)
            d_k_st = _bdot(vb, dsb, BNN_DIMS)
            d_dc = jnp.sum(ds_out * s_in, axis=1, keepdims=True)

            p_qi = d_q_in * ck["q_in"]
            p_ki = d_k_in * ck["k_in"]
            p_qs = d_q_st * ck["q_st"]
            p_ks = d_k_st * ck["k_st"]
            d_qh = (d_q_in * ck["e_qi"] + d_q_st * ck["e_qs"]) * HG_SCALE
            d_k = d_k_in * ck["e_ki"] + d_k_st * ck["e_ks"]
            d_b = (p_qi - p_ki) + (p_qs - p_ks)
            d_b_mid = jnp.sum(p_ki - p_qi, axis=1, keepdims=True)
            d_b_last = jnp.sum(p_ks, axis=1, keepdims=True) + d_dc * ck["dc"]
            rowi = lax.broadcasted_iota(jnp.int32, (HG_NC, CHUNK, BD), 1)
            d_b = d_b + jnp.where(rowi == CHUNK // 2, d_b_mid, 0.0) + jnp.where(rowi == CHUNK - 1, d_b_last, 0.0)
            d_logf = _tri_sums(True, d_b)
            d_f = d_logf / ck["f"] - d_k
            sig, sq = ck["sig"], ck["sq"]
            d_fp = d_f * (1.0 - lb) * (sig * (1.0 - sig))
            dlb[0:1, cols] += jnp.sum(jnp.sum(d_f * (1.0 - sig), axis=1), axis=0, keepdims=True)
            d_q = d_qh * (sq * (1.0 + q * (1.0 - sq)))
            dz_ref[0, :, cols] = d_q.astype(BF16).reshape(t, BD)
            dz_ref[1, :, cols] = d_fp.astype(BF16).reshape(t, BD)
            dz_ref[2, :, cols] = d_v.astype(BF16).reshape(t, BD)
            dz_ref[3, :, cols] = d_gb.astype(BF16).reshape(t, BD)
            return carry

        lax.fori_loop(0, NB, head, 0, unroll=2)

        @pl.when((b == nb - 1) & (s == ns - 1))
        def _():
            lb = _lower_bound(lg_ref[...])
            dl = dlb[0:1, :] * (lb * (1.0 - lb))
            glg_ref[0:1, :] = dl
            glg_ref[1:2, :] = -dl

    rb = lambda b, s: b * ns + (ns - 1 - s)
    seg = lambda j: pl.BlockSpec((1, t, D), lambda b, s: (j, rb(b, s), 0))
    tile = pl.BlockSpec((t, D), lambda b, s: (rb(b, s), 0))
    return pl.pallas_call(
        body, name="hgrn_bwd", grid=(nb, ns),
        in_specs=[seg(2), seg(3), seg(4), seg(5), tile,
                  pl.BlockSpec((NB, HG_NC, BD, BD), lambda b, s: (b, ns - 1 - s, 0, 0)),
                  tile, pl.BlockSpec((2, D), lambda b, s: (0, 0)), pl.BlockSpec((1, BD), lambda b, s: (0, 0))],
        out_specs=[pl.BlockSpec((4, t, D), lambda b, s: (0, rb(b, s), 0)),
                   pl.BlockSpec((2, D), lambda b, s: (0, 0)), pl.BlockSpec((1, BD), lambda b, s: (0, 0))],
        out_shape=[SDS((4, n, D), BF16), SDS((2, D), F32), SDS((1, BD), F32)],
        scratch_shapes=[pltpu.VMEM((NB, BD, BD), F32), pltpu.VMEM((8, D), F32)],
        compiler_params=_params(60),
    )(z, z, z, z, o_all, st_all, dyb, lb_logits, hg_g)


def _mid(ya, yb, z, b_merge, x2, tgt, fin_g, pa, pb, wo):
    n = x2.shape[0]
    tm = 256
    ni = n // tm

    def body(ya_ref, yb_ref, gma_ref, gmb_ref, bm_ref, x_ref, t_ref, fg_ref, pa_hbm, pb_hbm, wo_hbm,
             dx2_ref, dya_ref, dyb_ref, dgm_ref, loss_ref, gfg_ref, gbm_ref, gm_hbm,
             pa_v, pb_v, wo_v, gpa_v, gpb_v, gwo_v, sem):
        i = pl.program_id(0)
        by_owner = lambda g: g.reshape(NB, BD, D)
        loads = [pltpu.make_async_copy(src, dst, sem.at[k])
                 for k, (src, dst) in enumerate(((pa_hbm, pa_v), (pb_hbm, pb_v), (wo_hbm, wo_v)))]
        stores = [pltpu.make_async_copy(src, dst, sem.at[k])
                  for k, (src, dst) in enumerate((g, gm_hbm.at[:, pl.ds(slot * BD, BD), :])
                                                 for slot, g in enumerate((gpa_v, gpb_v, gwo_v)))]

        @pl.when(i == 0)
        def _():
            for cp in loads:
                cp.start()
            for ref in (gpa_v, gpb_v, gwo_v, loss_ref, gfg_ref, gbm_ref):
                ref[...] = jnp.zeros(ref.shape, F32)
            for cp in loads:
                cp.wait()

        ya_v = ya_ref[...]
        yb_v = yb_ref[...]
        out_a = jnp.dot(ya_v, pa_v[...], preferred_element_type=F32)
        out_b = jnp.dot(yb_v, pb_v[...], preferred_element_type=F32)
        bm = bm_ref[...]
        g_a = _sigmoid(gma_ref[0] + bm[:, 0:D])
        g_b = _sigmoid(gmb_ref[0] + bm[:, D:2 * D])
        mixed = g_a * out_a + g_b * out_b
        mixb = mixed.astype(BF16)
        xo = x_ref[...] + jnp.dot(mixb, wo_v[...], preferred_element_type=F32)
        r = lax.rsqrt(jnp.mean(xo * xo, axis=-1, keepdims=True) + EPS)
        xn = xo * r
        fg = fg_ref[...]
        e = xn * fg - t_ref[...]
        loss_ref[...] += 0.5 * jnp.sum(jnp.mean(e * e, axis=-1, keepdims=True))
        dy = e * (1.0 / D)
        gfg_ref[...] += jnp.sum(dy * xn, axis=0, keepdims=True)
        dxn = dy * fg
        dx2 = r * (dxn - xn * jnp.mean(dxn * xn, axis=-1, keepdims=True))
        dx2_ref[...] = dx2
        dx2b = dx2.astype(BF16)
        d_mixed = lax.dot_general(dx2b, wo_v[...], NT_DIMS, preferred_element_type=F32)
        gwo_v[...] += by_owner(lax.dot_general(mixb, dx2b, TN_DIMS, preferred_element_type=F32))
        d_oa = (d_mixed * g_a).astype(BF16)
        d_ob = (d_mixed * g_b).astype(BF16)
        dgm_a = (d_mixed * out_a) * (g_a * (1.0 - g_a))
        dgm_b = (d_mixed * out_b) * (g_b * (1.0 - g_b))
        gbm_ref[:, 0:D] += jnp.sum(dgm_a, axis=0, keepdims=True)
        gbm_ref[:, D:2 * D] += jnp.sum(dgm_b, axis=0, keepdims=True)
        dgm_ref[0] = dgm_a.astype(BF16)
        dgm_ref[1] = dgm_b.astype(BF16)
        dya_ref[...] = lax.dot_general(d_oa, pa_v[...], NT_DIMS, preferred_element_type=F32)
        dyb_ref[...] = lax.dot_general(d_ob, pb_v[...], NT_DIMS, preferred_element_type=F32)
        gpa_v[...] += by_owner(lax.dot_general(ya_v, d_oa, TN_DIMS, preferred_element_type=F32))
        gpb_v[...] += by_owner(lax.dot_general(yb_v, d_ob, TN_DIMS, preferred_element_type=F32))

        @pl.when(i == ni - 1)
        def _():
            for cp in stores:
                cp.start()
            for cp in stores:
                cp.wait()

    rows = pl.BlockSpec((tm, D), lambda i: (i, 0))
    rep = lambda shape: pl.BlockSpec(shape, lambda i: (0,) * len(shape))
    return pl.pallas_call(
        body, name="mid", grid=(ni,),
        in_specs=[rows, rows,
                  pl.BlockSpec((1, tm, D), lambda i: (6, i, 0)), pl.BlockSpec((1, tm, D), lambda i: (7, i, 0)),
                  rep((1, 2 * D)), rows, rows, rep((1, D)), ANY, ANY, ANY],
        out_specs=[rows, rows, rows, pl.BlockSpec((2, tm, D), lambda i: (0, i, 0)),
                   rep((8, BD)), rep((1, D)), rep((1, 2 * D)), ANY],
        out_shape=[SDS((n, D), F32), SDS((n, D), F32), SDS((n, D), F32), SDS((2, n, D), BF16),
                   SDS((8, BD), F32), SDS((1, D), F32), SDS((1, 2 * D), F32),
                   SDS((NB, MID_ROWS, D), F32)],
        scratch_shapes=[pltpu.VMEM((D, D), BF16)] * 3 + [pltpu.VMEM((NB, BD, D), F32)] * 3 + [pltpu.SemaphoreType.DMA((3,))],
        compiler_params=_params(60),
    )(ya, yb, z, z, b_merge, x2, tgt, fin_g, pa, pb, wo)


def _dz_specs(tm, ni, row_major):
    if row_major:
        ia = lambda i, j: (jnp.minimum(j, 1), i, 0)
        ib = lambda i, j: (jnp.clip(j - 2, 0, 3), i, 0)
        im = lambda i, j: (jnp.clip(j - 6, 0, 1), i, 0)
    else:
        last = ni - 1
        ia = lambda j, i: (jnp.minimum(j, 1), jnp.where(j < 2, i, last), 0)
        ib = lambda j, i: (jnp.clip(j - 2, 0, 3), jnp.where(j < 2, 0, jnp.where(j < 6, i, last)), 0)
        im = lambda j, i: (jnp.clip(j - 6, 0, 1), jnp.where(j < 6, 0, i), 0)
    return [pl.BlockSpec((1, tm, D), f) for f in (ia, ib, im)]


def _inproj_bwd_x(dza, dzb, dzm, w_all, x2, dx2, norm_g, after):
    n = x2.shape[0]
    tm = 512
    ni = n // tm

    def body(dza_ref, dzb_ref, dzm_ref, w_ref, x_ref, dx2_ref, g_ref, after_ref, gx_ref, gg_ref, acc):
        i, j = pl.program_id(0), pl.program_id(1)

        @pl.when((i == 0) & (j == 0))
        def _():
            gg_ref[...] = jnp.zeros((1, D), F32)

        @pl.when(j == 0)
        def _():
            acc[...] = jnp.zeros((tm, D), F32)

        def add(ref):
            acc[...] += lax.dot_general(ref[0], w_ref[0], NT_DIMS, preferred_element_type=F32)

        pl.when(j < 2)(lambda: add(dza_ref))
        pl.when((j >= 2) & (j < 6))(lambda: add(dzb_ref))
        pl.when(j >= 6)(lambda: add(dzm_ref))

        @pl.when(j == NB - 1)
        def _():
            x = x_ref[...]
            r = lax.rsqrt(jnp.mean(x * x, axis=-1, keepdims=True) + EPS)
            xn = x * r
            dh = acc[...]
            gg_ref[...] += jnp.sum(dh * xn, axis=0, keepdims=True)
            dxn = dh * g_ref[...]
            gx_ref[...] = dx2_ref[...] + r * (dxn - xn * jnp.mean(dxn * xn, axis=-1, keepdims=True))

    rows = pl.BlockSpec((tm, D), lambda i, j: (i, 0))
    return pl.pallas_call(
        body, name="inproj_bwd_x", grid=(ni, NB),
        in_specs=_dz_specs(tm, ni, True) + [pl.BlockSpec((1, D, D), lambda i, j: (j, 0, 0)), rows, rows,
                                             pl.BlockSpec((1, D), lambda i, j: (0, 0)), ANY],
        out_specs=[rows, pl.BlockSpec((1, D), lambda i, j: (0, 0))],
        out_shape=[SDS((n, D), F32), SDS((1, D), F32)],
        scratch_shapes=[pltpu.VMEM((tm, D), F32)],
        compiler_params=_params(48),
    )(dza, dzb, dzm, w_all, x2, dx2, norm_g, after)


def _walk_tables(order, ni):
    rows = []
    for lo, hi in ((0, 2), (2, 6), (6, 8)):
        active = [j for j, g in enumerate(order) if lo <= g < hi]
        block, tile = [], []
        for j, g in enumerate(order):
            before = [a for a in active if a < j]
            if lo <= g < hi:
                block.append(g - lo), tile.append(-1)
            elif before:
                block.append(order[before[-1]] - lo), tile.append(ni - 1)
            else:
                block.append(order[active[0]] - lo), tile.append(0)
        rows += [block, tile]
    return rows


def _inproj_bwd_w(core, dza, dzb, dzm, h_all, g_m):
    n = h_all.shape[0]
    tm = min(n, 2048)
    ni = n // tm
    orders = [[2 * q + 1 - c for q in range(4)] + [2 * q + c for q in range(4)] for c in (0, 1)]
    tables = jnp.asarray([[order] + _walk_tables(order, ni) for order in orders], jnp.int32)
    walk = jnp.where(core == 0, tables[0], tables[1])

    def body(walk_ref, dza_ref, dzb_ref, dzm_ref, h_ref, gm_hbm, gw_ref, got_w, got_m, stage, send_sems, recv_sems):
        j, i = pl.program_id(0), pl.program_id(1)
        group = walk_ref[0, j]
        x, y, c = _place()
        sibling = (x, y, 1 - c)

        def send_w(q):
            return pltpu.make_async_remote_copy(
                src_ref=stage.at[q % 2], dst_ref=got_w.at[q], send_sem=send_sems.at[q], recv_sem=recv_sems.at[q],
                device_id=sibling, device_id_type=MESH)

        def send_m(q):
            return pltpu.make_async_remote_copy(
                src_ref=gm_hbm.at[2 * q + (1 - c)], dst_ref=got_m.at[q], send_sem=send_sems.at[4 + q],
                recv_sem=recv_sems.at[4 + q], device_id=sibling, device_id_type=MESH)

        @pl.when((j == 0) & (i == 0))
        def _():
            for q in range(4):
                send_m(q).start()

        @pl.when(i == 0)
        def _():
            gw_ref[...] = jnp.zeros((1, D, D), F32)

        def add(ref):
            gw_ref[0] += lax.dot_general(h_ref[...], ref[0], TN_DIMS, preferred_element_type=F32)

        pl.when(group < 2)(lambda: add(dza_ref))
        pl.when((group >= 2) & (group < 6))(lambda: add(dzb_ref))
        pl.when(group >= 6)(lambda: add(dzm_ref))

        for q in range(4):
            @pl.when((i == ni - 1) & (j == q))
            def _(q=q):
                if q >= 2:
                    send_w(q - 2).wait_send()
                stage[q % 2] = gw_ref[0].astype(BF16)
                send_w(q).start()

        @pl.when((j == NB - 1) & (i == ni - 1))
        def _():
            for q in (2, 3):
                send_w(q).wait_send()
            for q in range(4):
                send_w(q).wait_recv()
                send_m(q).wait_send()
                send_m(q).wait_recv()

    def dz_spec(k):
        return pl.BlockSpec((1, tm, D), lambda j, i, w: (w[1 + 2 * k, j], jnp.where(w[2 + 2 * k, j] < 0, i, w[2 + 2 * k, j]), 0))

    return pl.pallas_call(
        body, name="inproj_bwd_w",
        grid_spec=pltpu.PrefetchScalarGridSpec(
            num_scalar_prefetch=1, grid=(NB, ni),
            in_specs=[dz_spec(0), dz_spec(1), dz_spec(2), pl.BlockSpec((tm, D), lambda j, i, w: (i, 0)), ANY],
            out_specs=[pl.BlockSpec((1, D, D), lambda j, i, w: (w[0, j], 0, 0)), ANY, ANY],
            scratch_shapes=[pltpu.VMEM((2, D, D), BF16), pltpu.SemaphoreType.DMA((8,)), pltpu.SemaphoreType.DMA((8,))]),
        out_shape=[SDS((NB, D, D), F32), SDS((4, D, D), BF16), SDS((4,) + g_m.shape[1:], F32)],
        compiler_params=_params(58),
    )(walk, dza, dzb, dzm, h_all, g_m)


def _adamw(w, g, m, v):
    rows, cols = w.shape
    tr = _row_tile(rows)

    spec = pl.BlockSpec((tr, cols), lambda i: (i, 0))
    return pl.pallas_call(
        functools.partial(_adam_refs), name="adamw", grid=(rows // tr,), in_specs=[spec] * 4, out_specs=[spec] * 3,
        out_shape=[SDS((rows, cols), F32)] * 3, compiler_params=_params(32),
    )(w, g, m, v)


def _adam_refs(w_ref, g_ref, m_ref, v_ref, d_ref, nm_ref, nv_ref):
    gv = g_ref[...]
    nm = ADAM_B1 * m_ref[...] + (1.0 - ADAM_B1) * gv
    nv = ADAM_B2 * v_ref[...] + (1.0 - ADAM_B2) * (gv * gv)
    m_hat = nm / (1.0 - ADAM_B1 ** ADAM_STEP)
    v_hat = nv / (1.0 - ADAM_B2 ** ADAM_STEP)
    d_ref[...] = -ADAM_LR * (m_hat / (jnp.sqrt(v_hat) + ADAM_EPS) + ADAM_WD * w_ref[...])
    nm_ref[...] = nm
    nv_ref[...] = nv


def _adamw_small(ws, gs, ms, vs):
    k = len(ws)

    def body(*refs):
        ins, outs = refs[:4 * k], refs[4 * k:7 * k]
        vin, vout = refs[7 * k:11 * k], refs[11 * k:14 * k]
        load_sems, store_sems = refs[14 * k:]
        loads = [pltpu.make_async_copy(ins[i], vin[i], load_sems.at[i]) for i in range(4 * k)]
        for cp in loads:
            cp.start()
        for cp in loads:
            cp.wait()
        for i in range(k):
            _adam_refs(*[vin[part * k + i] for part in range(4)], *[vout[part * k + i] for part in range(3)])
        stores = [pltpu.make_async_copy(vout[i], outs[i], store_sems.at[i]) for i in range(3 * k)]
        for cp in stores:
            cp.start()
        for cp in stores:
            cp.wait()

    shapes = [SDS(w.shape, F32) for w in ws]
    vmem = [pltpu.VMEM(w.shape, F32) for w in ws]
    out = pl.pallas_call(
        body, name="adamw_small", out_shape=shapes * 3, in_specs=[HBM] * (4 * k), out_specs=[HBM] * (3 * k),
        scratch_shapes=vmem * 7 + [pltpu.SemaphoreType.DMA((4 * k,)), pltpu.SemaphoreType.DMA((3 * k,))],
        compiler_params=_params(32),
    )(*ws, *gs, *ms, *vs)
    return out[:k], out[k:2 * k], out[2 * k:]


def _allgather(blocks, dtypes, name):
    na = len(blocks)

    def body(*refs):
        ins, outs, stages = refs[:na], refs[na:2 * na], refs[2 * na:3 * na]
        send_sems, recv_sems, local_sems = refs[3 * na:]
        x, y, c = _place()
        me, sibling = (x, y, c), (x, y, 1 - c)
        chips = [(1 - x, y), (x, 1 - y), (1 - x, 1 - y)]
        blk = lambda p: 4 * p[0] + 2 * p[1] + p[2]

        def copy(a, k, block, to, src=None):
            return pltpu.make_async_remote_copy(
                src_ref=outs[a].at[blk(block)] if src is None else src, dst_ref=outs[a].at[blk(block)],
                send_sem=send_sems.at[7 * a + k], recv_sem=recv_sems.at[7 * a + k],
                device_id=to, device_id_type=MESH)

        mine, first, passed = [], [], []
        for a in range(na):
            stages[a][...] = ins[a][...].astype(dtypes[a])
            mine.append(pltpu.make_async_copy(stages[a], outs[a].at[blk(me)], local_sems.at[a]))
            mine[-1].start()
            first.append(copy(a, 0, me, sibling, src=stages[a]))
            first += [copy(a, 1 + j, me, (*chip, c), src=stages[a]) for j, chip in enumerate(chips)]
        for cp in first:
            cp.start()
        for j, chip in enumerate(chips):
            for a in range(na):
                copy(a, 1 + j, (*chip, c), me).wait_recv()
                passed.append(copy(a, 4 + j, (*chip, c), sibling))
                passed[-1].start()
        for a in range(na):
            copy(a, 0, sibling, me).wait_recv()
            for j, chip in enumerate(chips):
                copy(a, 4 + j, (*chip, 1 - c), me).wait_recv()
        for cp in first + passed:
            cp.wait_send()
        for cp in mine:
            cp.wait()

    return pl.pallas_call(
        body, name=name,
        in_specs=[pl.BlockSpec(memory_space=pltpu.VMEM)] * na, out_specs=[ANY] * na,
        out_shape=[SDS((NB,) + b.shape, dt) for b, dt in zip(blocks, dtypes)],
        scratch_shapes=[pltpu.VMEM(b.shape, dt) for b, dt in zip(blocks, dtypes)]
        + [pltpu.SemaphoreType.DMA((7 * na,)), pltpu.SemaphoreType.DMA((7 * na,)), pltpu.SemaphoreType.DMA((na,))],
        compiler_params=_params(40),
    )(*blocks)


HBM = pl.BlockSpec(memory_space=pltpu.HBM)
SEMS = pl.BlockSpec(memory_space=pltpu.SEMAPHORE)
EFFECT = pltpu.SideEffectType.DATAFLOW_SIDE_EFFECTING


def _chip_copies(srcs, lands, send_sems, recv_sems):
    x, y, c = _place()
    return [pltpu.make_async_remote_copy(
        src_ref=srcs[a].at[slot], dst_ref=lands[a].at[slot],
        send_sem=send_sems.at[3 * a + slot], recv_sem=recv_sems.at[3 * a + slot],
        device_id=(px, py, c), device_id_type=MESH)
        for a in range(len(srcs)) for slot, (px, py) in enumerate(_other_chips(x, y))]


def _split_start(name, copies, per_array, srcs, lands, after=None):
    na = len(srcs)

    def body(*refs):
        send_sems, recv_sems = refs[-2 * na - 3], refs[-2 * na - 2]
        for cp in copies(refs[:na], refs[na:2 * na], send_sems, recv_sems):
            cp.start()
        refs[-1][...] = jnp.zeros_like(refs[-1])

    hbm = lambda a: pltpu.HBM(a.shape, a.dtype)
    pin = lambda a: pltpu.with_memory_space_constraint(a, pltpu.HBM)
    out = pl.pallas_call(
        body, name=name,
        out_shape=(pltpu.SemaphoreType.DMA((per_array * na,)), pltpu.SemaphoreType.DMA((per_array * na,)),
                   *[hbm(a) for a in srcs], *[hbm(a) for a in lands], SDS((8, BD), F32)),
        in_specs=[HBM] * (2 * na) + ([] if after is None else [ANY]),
        out_specs=(SEMS, SEMS, *[HBM] * (2 * na), pl.BlockSpec(memory_space=pltpu.VMEM)),
        input_output_aliases={i: 2 + i for i in range(2 * na)},
        compiler_params=pltpu.CompilerParams(has_side_effects=EFFECT),
    )(*[pin(a) for a in srcs], *[pin(a) for a in lands], *([] if after is None else [after]))
    return out[0], out[1], out[2:2 + na], out[2 + na:2 + 2 * na], out[-1]


def _split_wait(name, copies, started, after):
    send_sems, recv_sems, srcs, lands, _ = started
    na = len(srcs)

    def body(*refs):
        waits = copies(refs[:na], refs[na:2 * na], refs[2 * na], refs[2 * na + 1])
        for cp in waits:
            cp.wait_send()
        for cp in waits:
            cp.wait_recv()

    hbm = lambda a: pltpu.HBM(a.shape, a.dtype)
    out = pl.pallas_call(
        body, name=name,
        out_shape=(*[hbm(a) for a in srcs], *[hbm(a) for a in lands]),
        in_specs=[HBM] * (2 * na) + [SEMS, SEMS, ANY],
        out_specs=tuple([HBM] * (2 * na)),
        input_output_aliases={i: i for i in range(2 * na)},
        compiler_params=pltpu.CompilerParams(has_side_effects=EFFECT),
    )(*srcs, *lands, send_sems, recv_sems, after)
    return out[na:]


def _add_sibling(place, g, a_in):
    _, r, cols = g.shape
    tr = _row_tile(r)

    def chip(k, pr):
        qx = pr[0] if k in (1, 3) else 1 - pr[0]
        qy = pr[1] if k in (0, 3) else 1 - pr[1]
        return 2 * qx + qy

    def body(place_ref, *refs):
        g_refs, a_refs, (out_ref, own_ref) = refs[0:4], refs[4:8], refs[8:10]
        for k in range(3):
            out_ref[k] = (g_refs[k][0] + a_refs[k][0].astype(F32)).astype(BF16)
        own_ref[...] = g_refs[3][0] + a_refs[3][0].astype(F32)

    mine = lambda k: pl.BlockSpec((1, tr, cols), lambda i, pr: (2 * chip(k, pr) + pr[2], i, 0))
    theirs = lambda k: pl.BlockSpec((1, tr, cols), lambda i, pr: (chip(k, pr), i, 0))
    return pl.pallas_call(
        body, name="add_sibling",
        grid_spec=pltpu.PrefetchScalarGridSpec(
            num_scalar_prefetch=1, grid=(r // tr,),
            in_specs=[mine(k) for k in range(4)] + [theirs(k) for k in range(4)],
            out_specs=[pl.BlockSpec((3, tr, cols), lambda i, pr: (0, i, 0)),
                       pl.BlockSpec((tr, cols), lambda i, pr: (i, 0))]),
        out_shape=[SDS((3, r, cols), BF16), SDS((r, cols), F32)], compiler_params=_params(48),
    )(place, *[g] * 4, *[a_in] * 4)


def _add_chips(own, b_in):
    r, cols = own.shape
    tr = _row_tile(r)

    def body(p_ref, b0_ref, b1_ref, b2_ref, o_ref):
        o_ref[...] = ((p_ref[...] + b0_ref[0].astype(F32)) + b1_ref[0].astype(F32)) + b2_ref[0].astype(F32)

    slot = lambda k: pl.BlockSpec((1, tr, cols), lambda i: (k, i, 0))
    spec = pl.BlockSpec((tr, cols), lambda i: (i, 0))
    return pl.pallas_call(
        body, name="add_chips", grid=(r // tr,), in_specs=[spec, slot(0), slot(1), slot(2)], out_specs=spec,
        out_shape=SDS((r, cols), F32), compiler_params=_params(32),
    )(own, b_in, b_in, b_in)


VEC_NAMES = ("b_merge", "conv_b", "rg_bx", "rg_ba", "rg_lambda", "hg_lb_logits", "hg_norm_g", "final_norm_g")
REP_NAMES = ("rg_wx", "rg_wa", "norm_g") + VEC_NAMES
SMALL_AT = 3 * BD
SMALL_ROWS = 48
MID_ROWS = 448


def _sum_blocks(parts):
    def body(p_ref, o_ref):
        acc = p_ref[0]
        for k in range(1, NB):
            acc = acc + p_ref[k]
        o_ref[...] = acc

    return pl.pallas_call(body, name="sum_blocks", out_shape=SDS(parts.shape[1:], F32))(parts)


def _pack_rows(arrays, width, row_multiple=8):
    flat = jnp.concatenate([a.reshape(-1) for a in arrays])
    rows = -(-flat.shape[0] // width)
    rows = -(-rows // row_multiple) * row_multiple
    return jnp.pad(flat, (0, rows * width - flat.shape[0])).reshape(rows, width)


def _unpack(flat, like):
    out, off = [], 0
    for a in like:
        out.append(flat[off:off + a.size].reshape(a.shape))
        off += a.size
    return out


def kernel(x, w_in, b_merge, conv_w, conv_b, rg_wx, rg_bx, rg_wa, rg_ba, rg_lambda, hg_lb_logits, hg_norm_g, proj_a, proj_b, w_out, norm_g, final_norm_g, loss_target, m_w_in, m_b_merge, m_conv_w, m_conv_b, m_rg_wx, m_rg_bx, m_rg_wa, m_rg_ba, m_rg_lambda, m_hg_lb_logits, m_hg_norm_g, m_proj_a, m_proj_b, m_w_out, m_norm_g, m_final_norm_g, v_w_in, v_b_merge, v_conv_w, v_conv_b, v_rg_wx, v_rg_bx, v_rg_wa, v_rg_ba, v_rg_lambda, v_hg_lb_logits, v_hg_norm_g, v_proj_a, v_proj_b, v_w_out, v_norm_g, v_final_norm_g):
    weights = dict(w_in=w_in, b_merge=b_merge, conv_w=conv_w, conv_b=conv_b, rg_wx=rg_wx, rg_bx=rg_bx, rg_wa=rg_wa,
                   rg_ba=rg_ba, rg_lambda=rg_lambda, hg_lb_logits=hg_lb_logits, hg_norm_g=hg_norm_g, proj_a=proj_a,
                   proj_b=proj_b, w_out=w_out, norm_g=norm_g, final_norm_g=final_norm_g)
    mom1 = dict(w_in=m_w_in, b_merge=m_b_merge, conv_w=m_conv_w, conv_b=m_conv_b, rg_wx=m_rg_wx, rg_bx=m_rg_bx,
                rg_wa=m_rg_wa, rg_ba=m_rg_ba, rg_lambda=m_rg_lambda, hg_lb_logits=m_hg_lb_logits,
                hg_norm_g=m_hg_norm_g, proj_a=m_proj_a, proj_b=m_proj_b, w_out=m_w_out, norm_g=m_norm_g,
                final_norm_g=m_final_norm_g)
    mom2 = dict(w_in=v_w_in, b_merge=v_b_merge, conv_w=v_conv_w, conv_b=v_conv_b, rg_wx=v_rg_wx, rg_bx=v_rg_bx,
                rg_wa=v_rg_wa, rg_ba=v_rg_ba, rg_lambda=v_rg_lambda, hg_lb_logits=v_hg_lb_logits,
                hg_norm_g=v_hg_norm_g, proj_a=v_proj_a, proj_b=v_proj_b, w_out=v_w_out, norm_g=v_norm_g,
                final_norm_g=v_final_norm_g)
    order = list(weights)
    nb, s_len, _ = x.shape
    n = nb * s_len
    px, py, pc = _place()
    place = jnp.stack([px, py, pc]).astype(jnp.int32)

    in_hbm = lambda a: pltpu.with_memory_space_constraint(a, pltpu.HBM)
    norm_gain = in_hbm(norm_g)

    x2 = x.reshape(n, D)
    cw_blk = jnp.pad(conv_w[0], ((0, 4), (0, 0)))
    order_ids = jnp.stack([_block_id(p) for p in _arrival_order(px, py, pc)]).astype(jnp.int32)
    z, h_all, w_all, pa_all, pb_all, wo_all, cw_all = _gather_inproj(
        order_ids, x2, norm_gain, [w_in[0], proj_a[0], proj_b[0], w_out[0], cw_blk], [BF16, BF16, BF16, BF16, F32])
    pa_full, pb_full, wo_full = (a.reshape(D, D) for a in (pa_all, pb_all, wo_all))
    cw8 = in_hbm(cw_all.transpose(1, 0, 2).reshape(8, D))
    wx_b, wa_b = in_hbm(rg_wx[0].astype(BF16)), in_hbm(rg_wa[0].astype(BF16))
    cb, bx, ba, lam = (in_hbm(a.reshape(1, D)) for a in (conv_b, rg_bx, rg_ba, rg_lambda))
    fin_g, b_mrg = in_hbm(final_norm_g.reshape(1, D)), in_hbm(b_merge)
    lb_lg, hg_g = in_hbm(hg_lb_logits), in_hbm(hg_norm_g)

    hlru, ya = _lru_fwd(z, cw8, cb, wx_b, wa_b, bx, ba, lam, nb, s_len)
    o_all, yb, st_all = _hgrn_fwd(z, lb_lg, hg_g, nb, s_len)

    (dx2, dya, dyb, dzm, loss_acc, g_fin, g_bm, g_mid) = _mid(
        ya, yb, z, b_mrg, x2, loss_target.reshape(n, D), fin_g, pa_full, pb_full, wo_full)
    dzb, g_lg, g_hg = _hgrn_bwd(z, o_all, st_all, dyb, lb_lg, hg_g, nb, s_len)
    dza, g_cw8, g_cb, g_wx, g_wa, g_bx, g_ba, g_lam = _lru_bwd(
        z, hlru, dya, cw8, cb, wx_b, wa_b, bx, ba, lam, nb, s_len)

    part = dict(b_merge=g_bm, conv_b=g_cb, rg_bx=g_bx, rg_ba=g_ba, rg_lambda=g_lam, hg_lb_logits=g_lg,
                hg_norm_g=g_hg, final_norm_g=g_fin)
    vec = _pack_rows([part[k] for k in VEC_NAMES], BD)
    vec = jnp.pad(vec, ((0, 16 * NB - vec.shape[0]), (0, 0))).reshape(NB, 2, D)
    rows8 = lambda a: jnp.pad(a, ((0, 0), (0, 8 - a.shape[1]), (0, 0)))
    small = jnp.concatenate([g_wx.reshape(NB, 16, D), g_wa.reshape(NB, 16, D),
                             rows8(g_cw8.reshape(8, NB, BD).transpose(1, 0, 2).reshape(NB, 1, D)), rows8(vec),
                             jnp.zeros((NB, MID_ROWS - SMALL_AT - SMALL_ROWS, D), F32)], axis=1)
    g_m = lax.dynamic_update_slice(g_mid, small, (0, SMALL_AT, 0))
    g_w, w_from_sibling, m_from_sibling = _inproj_bwd_w(pc, dza, dzb, dzm, h_all, g_m)
    w_out_bf, w_own = _add_sibling(place, g_w, w_from_sibling)
    m_out_bf, m_own = _add_sibling(place, g_m, m_from_sibling)
    outgoing = [w_out_bf, m_out_bf]
    chip_sums = _split_start("rs_chips_start", _chip_copies, 3, outgoing, [lax.empty(a.shape, a.dtype) for a in outgoing])
    grad_x, g_ng = _inproj_bwd_x(dza, dzb, dzm, w_all, x2, dx2, norm_gain, chip_sums[-1])
    from_chips = _split_wait("rs_chips_wait", _chip_copies, chip_sums, grad_x)
    r_w = _add_chips(w_own, from_chips[0])
    r_m = _add_chips(m_own, from_chips[1])
    row = lax.broadcasted_iota(jnp.int32, (8, D), 0)
    mine = jnp.where(row == 0, g_ng, jnp.where(row == 1, loss_acc[0:1, 0:1], 0.0))
    tail = jnp.concatenate([r_m[SMALL_AT:SMALL_AT + SMALL_ROWS], mine], axis=0)
    (tail_all,) = _allgather([tail], [F32], "gather_small_grads")
    summed = _sum_blocks(tail_all[:, SMALL_ROWS:SMALL_ROWS + 8])

    grads = dict(w_in=r_w.reshape(1, D, D),
                 proj_a=r_m[0:BD].reshape(1, BD, D), proj_b=r_m[BD:2 * BD].reshape(1, BD, D),
                 w_out=r_m[2 * BD:3 * BD].reshape(1, BD, D),
                 conv_w=r_m[SMALL_AT + 32].reshape(8, BD)[0:4].reshape(1, 4, BD),
                 rg_wx=tail_all[:, 0:16].reshape(1, NB, BD, BD), rg_wa=tail_all[:, 16:32].reshape(1, NB, BD, BD),
                 norm_g=summed[0:1])
    vec_all = tail_all[:, 40:42].reshape(-1)
    for k, gk in zip(VEC_NAMES, _unpack(vec_all, [weights[k] for k in VEC_NAMES])):
        grads[k] = gk

    delta, new_m, new_v = {}, {}, {}
    flat2 = lambda a: a.reshape(-1, a.shape[-1])
    for k in ("w_in", "proj_a", "proj_b", "w_out"):
        outs = _adamw(*[flat2(t[k]) for t in (weights, grads, mom1, mom2)])
        delta[k], new_m[k], new_v[k] = (a.reshape(weights[k].shape) for a in outs)
    rep = list(REP_NAMES) + ["conv_w"]
    outs = _adamw_small(*[[flat2(t[k]) for k in rep] for t in (weights, grads, mom1, mom2)])
    for tgt, arrays in zip((delta, new_m, new_v), outs):
        for k, a in zip(rep, arrays):
            tgt[k] = a.reshape(weights[k].shape)

    return (summed[1, 0], grad_x.reshape(x.shape), *[grads[k] for k in order], *[delta[k] for k in order],
            *[new_m[k] for k in order], *[new_v[k] for k in order])
```

```python
import functools

import jax
import jax.numpy as jnp
from jax import lax
from jax.experimental import pallas as pl
from jax.experimental.pallas import tpu as pltpu

F32 = jnp.float32
BF16 = jnp.bfloat16
SDS = jax.ShapeDtypeStruct
MESH = pl.DeviceIdType.MESH
ANY = pl.BlockSpec(memory_space=pl.ANY)

D = 1024
NB = 8
BD = D // NB
CHUNK = 64
EPS = 1e-6
LRU_C = 8.0
HG_SCALE = BD ** -0.5
ADAM_LR, ADAM_B1, ADAM_B2, ADAM_EPS, ADAM_WD, ADAM_STEP = 0.001, 0.9, 0.999, 1e-08, 0.01, 10

NT_DIMS = (((1,), (1,)), ((), ()))
TN_DIMS = (((0,), (0,)), ((), ()))


def _params(vmem_mib):
    return pltpu.CompilerParams(vmem_limit_bytes=vmem_mib << 20)


def _row_tile(rows, most=256):
    assert rows % 8 == 0
    return max(t for t in range(8, min(rows, most) + 1, 8) if rows % t == 0)


def _sigmoid(v):
    return 0.5 * (jnp.tanh(0.5 * v) + 1.0)


def _groups(v):
    return v.reshape(v.shape[0] // 8, 8, v.shape[1])


def _softplus_neg(lam):
    t = -lam
    e = jnp.exp(-jnp.abs(t))
    w = 1.0 + e
    d = w - 1.0
    l1p = jnp.where(d == 0.0, e, jnp.log(w) * (e / jnp.where(d == 0.0, 1.0, d)))
    return jnp.maximum(t, 0.0) + l1p


def _place():
    return lax.axis_index("x"), lax.axis_index("y"), lax.axis_index("c")


def _other_chips(x, y):
    return [(1 - x, y), (x, 1 - y), (1 - x, 1 - y)]


def _block_id(p):
    return 4 * p[0] + 2 * p[1] + p[2]


def _core_chips(x, y, c):
    near, far, diag = _other_chips(x, y)
    pick = lambda a, b: (jnp.where(c == 0, a[0], b[0]), jnp.where(c == 0, a[1], b[1]))
    return [pick(near, far), pick(far, near), diag]


def _arrival_order(x, y, c):
    first, second, diag = _core_chips(x, y, c)
    return [(x, y, c), (x, y, 1 - c), (*first, c), (*second, 1 - c), (*second, c), (*first, 1 - c),
            (*diag, c), (*diag, 1 - c)]


def _gather_inproj(order_ids, x2, norm_g, blocks, dtypes):
    na = len(blocks)
    n = x2.shape[0]
    tm = min(n, 1024)
    ni = n // tm

    def body(order_ref, x_ref, g_ref, *refs):
        ins, (z_ref, h_ref), outs = refs[:na], refs[na:na + 2], refs[na + 2:2 * na + 2]
        stages = refs[2 * na + 2:3 * na + 2]
        h_full, wbuf, send_sems, recv_sems, local_sems, wsems, hsem = refs[3 * na + 2:]
        j, i = pl.program_id(0), pl.program_id(1)
        x, y, c = _place()
        me, sibling = (x, y, c), (x, y, 1 - c)
        chips = _core_chips(x, y, c)
        sibling_chips = [chips[1], chips[0], chips[2]]
        small = range(1, na)

        def copy(a, k, block, to, src=None):
            return pltpu.make_async_remote_copy(
                src_ref=outs[a].at[_block_id(block)] if src is None else src, dst_ref=outs[a].at[_block_id(block)],
                send_sem=send_sems.at[7 * a + k], recv_sem=recv_sems.at[7 * a + k],
                device_id=to, device_id_type=MESH)

        def local(a):
            return pltpu.make_async_copy(stages[a], outs[a].at[_block_id(me)], local_sems.at[a])

        def landed(a, slot):
            copy(a, 1 + slot, (*chips[slot], c), me).wait_recv()
            copy(a, 4 + slot, (*chips[slot], c), sibling).start()
            if slot == 0:
                copy(a, 3, (*chips[0], c), (*chips[1], c)).start()

        def diagonal_and_small():
            landed(0, 2)
            for a in small:
                landed(a, 0)
                landed(a, 1)

        def passed_on(a, slot):
            copy(a, 4 + slot, (*sibling_chips[slot], 1 - c), me).wait_recv()

        def sibling_here_send_second():
            copy(0, 0, sibling, me).wait_recv()
            for a in range(na):
                copy(a, 2, me, (*chips[1], c), src=stages[a]).start()

        @pl.when((j == 0) & (i == 0))
        def _():
            for a in range(na):
                stages[a][...] = ins[a][...].astype(dtypes[a])
                local(a).start()
            for a in range(na):
                copy(a, 0, me, sibling, src=stages[a]).start()
                copy(a, 1, me, (*chips[0], c), src=stages[a]).start()

        @pl.when(j == 0)
        def _():
            xv = x_ref[...]
            r = lax.rsqrt(jnp.mean(xv * xv, axis=-1, keepdims=True) + EPS)
            hb = ((xv * r) * g_ref[...]).astype(BF16)
            h_full[pl.ds(pl.multiple_of(i * tm, tm), tm), :] = hb

        save_h = pltpu.make_async_copy(h_full, h_ref, hsem)
        pl.when((j == 0) & (i == ni - 1))(save_h.start)

        steps = [
            lambda: local(0).wait(),
            sibling_here_send_second,
            lambda: landed(0, 0),
            lambda: passed_on(0, 0),
            lambda: landed(0, 1),
            lambda: passed_on(0, 1),
            diagonal_and_small,
            lambda: passed_on(0, 2),
        ]
        def w_load(k):
            return pltpu.make_async_copy(outs[0].at[order_ref[k]], wbuf.at[k % 2], wsems.at[k % 2])

        for k, step in enumerate(steps):
            @pl.when((j == 0) & (i == 0) if k == 0 else (j == k - 1) & (i == ni - 1))
            def _(k=k, step=step):
                step()
                w_load(k).start()

        pl.when(i == 0)(lambda: w_load(j).wait())
        z_ref[0] = jnp.dot(h_full[pl.ds(pl.multiple_of(i * tm, tm), tm), :], wbuf[j % 2], preferred_element_type=F32)

        @pl.when((j == NB - 1) & (i == ni - 1))
        def _():
            save_h.wait()
            for a in small:
                landed(a, 2)
            for a in small:
                local(a).wait()
                copy(a, 0, sibling, me).wait_recv()
                for slot in range(3):
                    passed_on(a, slot)
            for a in range(na):
                copy(a, 0, me, sibling, src=stages[a]).wait_send()
                for slot, chip in enumerate(chips):
                    copy(a, 1 + slot, me, (*chip, c), src=stages[a]).wait_send()
                    copy(a, 4 + slot, (*chip, c), sibling).wait_send()

    rows_once = lambda j, i, order: (jnp.where(j == 0, i, ni - 1), 0)
    vmem = pl.BlockSpec(memory_space=pltpu.VMEM)
    return pl.pallas_call(
        body, name="gather_inproj",
        grid_spec=pltpu.PrefetchScalarGridSpec(
            num_scalar_prefetch=1, grid=(NB, ni),
            in_specs=[pl.BlockSpec((tm, D), rows_once), pl.BlockSpec((1, D), lambda j, i, order: (0, 0))] + [vmem] * na,
            out_specs=[pl.BlockSpec((1, tm, D), lambda j, i, order: (order[j], i, 0)), ANY] + [ANY] * na,
            scratch_shapes=[pltpu.VMEM(b.shape, dt) for b, dt in zip(blocks, dtypes)]
            + [pltpu.VMEM((n, D), BF16), pltpu.VMEM((2, D, D), BF16),
               pltpu.SemaphoreType.DMA((7 * na,)), pltpu.SemaphoreType.DMA((7 * na,)),
               pltpu.SemaphoreType.DMA((na,)), pltpu.SemaphoreType.DMA((2,)), pltpu.SemaphoreType.DMA(())]),
        out_shape=[SDS((NB, n, D), F32), SDS((n, D), BF16)] + [SDS((NB,) + b.shape, dt) for b, dt in zip(blocks, dtypes)],
        compiler_params=_params(56),
    )(order_ids, x2, norm_g, *blocks)


LRU_T = 256


def _shifted(groups, shifts):
    row = lax.broadcasted_iota(jnp.int32, (groups.shape[0] - 1,) + groups.shape[1:], 1)
    out = []
    for s in shifts:
        y = pltpu.roll(groups, s % 8, 1)
        moved = jnp.where(row >= s, y[1:], y[:-1]) if s > 0 else jnp.where(row < 8 + s, y[:-1], y[1:])
        out.append(moved.reshape(-1, groups.shape[2]))
    return out


def _conv(taps, cw, cb):
    acc = taps[0] * cw[0:1, :] + taps[1] * cw[1:2, :]
    acc = acc + taps[2] * cw[2:3, :]
    acc = acc + taps[3] * cw[3:4, :]
    return cb + acc


def _lru_gates(xa, wx_ref, wa_ref, bx, ba, lam):
    xab = xa.astype(BF16)
    pis, prs = [], []
    for h in range(NB):
        xs = xab[:, h * BD:(h + 1) * BD]
        pis.append(jnp.dot(xs, wx_ref[h], preferred_element_type=F32))
        prs.append(jnp.dot(xs, wa_ref[h], preferred_element_type=F32))
    gi = _sigmoid(jnp.concatenate(pis, axis=1) + bx)
    gr = _sigmoid(jnp.concatenate(prs, axis=1) + ba)
    sp = _softplus_neg(lam)
    log_a = (-LRU_C * gr) * sp
    a = jnp.exp(log_a)
    mult = jnp.sqrt(-jnp.tanh(log_a) * (a * a + 1.0))
    return xab, gi, gr, sp, a, mult


def _lru_fwd(z, cw8, cb, wx, wa, bx, ba, lam, nb, s_len):
    n = nb * s_len
    t = LRU_T
    ns = s_len // t

    def body(xp_ref, ga_ref, cw_ref, cb_ref, wx_ref, wa_ref, bx_ref, ba_ref, lam_ref,
             h_ref, ya_ref, ext, a_s, u_s, carry):
        @pl.when(pl.program_id(1) == 0)
        def _():
            ext[0:8, :] = jnp.zeros((8, D), F32)
            carry[...] = jnp.zeros((8, D), F32)

        xp = xp_ref[0]
        ext[8:8 + t, :] = xp
        xa = _conv(_shifted(_groups(ext[...]), (3, 2, 1)) + [xp], cw_ref[...], cb_ref[...])
        ext[0:8, :] = xp[t - 8:t, :]
        _, gi, _, _, a, mult = _lru_gates(xa, wx_ref, wa_ref, bx_ref[...], ba_ref[...], lam_ref[...])
        u = (mult * gi) * xa
        a, u = _groups(a), _groups(u)
        row = lax.broadcasted_iota(jnp.int32, a.shape, 1)
        for sh in (1, 2, 4):
            a_sh = pltpu.roll(a, sh, 1)
            u_sh = pltpu.roll(u, sh, 1)
            m = row >= sh
            u = jnp.where(m, a * u_sh + u, u)
            a = jnp.where(m, a * a_sh, a)
        a_s[...] = a.reshape(t, D)
        u_s[...] = u.reshape(t, D)

        def step(g, c):
            r = pl.multiple_of(g * 8, 8)
            hg = u_s[pl.ds(r, 8), :] + a_s[pl.ds(r, 8), :] * c
            h_ref[pl.ds(r, 8), :] = hg
            return hg[7:8, :]

        c_out = lax.fori_loop(0, t // 8, step, carry[0:1, :], unroll=4)
        carry[0:1, :] = c_out
        ga = ga_ref[0]
        ya_ref[...] = (h_ref[...] * (ga * _sigmoid(ga))).astype(BF16)

    row_map = lambda b, s: (b * ns + s, 0)
    rep2 = lambda b, s: (0, 0)
    rep3 = lambda b, s: (0, 0, 0)
    return pl.pallas_call(
        body, name="lru_fwd", grid=(nb, ns),
        in_specs=[pl.BlockSpec((1, t, D), lambda b, s: (0, b * ns + s, 0)),
                  pl.BlockSpec((1, t, D), lambda b, s: (1, b * ns + s, 0)),
                  pl.BlockSpec((8, D), rep2), pl.BlockSpec((1, D), rep2),
                  pl.BlockSpec((NB, BD, BD), rep3), pl.BlockSpec((NB, BD, BD), rep3),
                  pl.BlockSpec((1, D), rep2), pl.BlockSpec((1, D), rep2), pl.BlockSpec((1, D), rep2)],
        out_specs=[pl.BlockSpec((t, D), row_map), pl.BlockSpec((t, D), row_map)],
        out_shape=[SDS((n, D), F32), SDS((n, D), BF16)],
        scratch_shapes=[pltpu.VMEM((t + 8, D), F32), pltpu.VMEM((t, D), F32), pltpu.VMEM((t, D), F32),
                        pltpu.VMEM((8, D), F32)],
        compiler_params=_params(48),
    )(z, z, cw8, cb, wx, wa, bx, ba, lam)


def _lru_bwd(z, h_all, dya, cw8, cb, wx, wa, bx, ba, lam, nb, s_len):
    n = nb * s_len
    t = LRU_T
    ns = s_len // t
    t8 = t // 8

    def body(xp_ref, xph_ref, ga_ref, h_ref, hh_ref, dya_ref, cw_ref, cb_ref, wx_ref, wa_ref, bx_ref, ba_ref,
             lam_ref, dz_ref, gcw_ref, gcb_ref, gwx_ref, gwa_ref, gbx_ref, gba_ref, glam_ref,
             ext, hext, dext, a_s, u_s, dh_s, carry):
        b, s = pl.program_id(0), pl.program_id(1)
        first_tile = s == ns - 1

        @pl.when((b == 0) & (s == 0))
        def _():
            for ref in (gcw_ref, gcb_ref, gwx_ref, gwa_ref, gbx_ref, gba_ref, glam_ref):
                ref[...] = jnp.zeros(ref.shape, F32)

        @pl.when(s == 0)
        def _():
            dext[t:t + 8, :] = jnp.zeros((8, D), F32)
            carry[...] = jnp.zeros((8, D), F32)

        keep = jnp.where(first_tile, 0.0, 1.0)
        xp = xp_ref[0]
        ext[0:8, :] = xph_ref[0] * keep
        ext[8:8 + t, :] = xp
        hext[0:8, :] = hh_ref[...] * keep
        hext[8:8 + t, :] = h_ref[...]
        cw = cw_ref[...]
        lam = lam_ref[...]
        taps = _shifted(_groups(ext[...]), (3, 2, 1)) + [xp]
        xa = _conv(taps, cw, cb_ref[...])
        xab, gi, gr, sp, a, mult = _lru_gates(xa, wx_ref, wa_ref, bx_ref[...], ba_ref[...], lam)
        (h_prev,) = _shifted(_groups(hext[...]), (1,))
        ga = ga_ref[0]
        sg = _sigmoid(ga)
        dya_v = dya_ref[...]
        d_ga = dya_v * h_ref[...] * (sg * (1.0 + ga * (1.0 - sg)))
        g_in = dya_v * (ga * sg)

        (an,) = _shifted(jnp.concatenate([_groups(a), jnp.ones((1, 8, D), F32)], axis=0), (-1,))
        an, u = _groups(an), _groups(g_in)
        row = lax.broadcasted_iota(jnp.int32, an.shape, 1)
        for sh in (1, 2, 4):
            a_sh = pltpu.roll(an, 8 - sh, 1)
            u_sh = pltpu.roll(u, 8 - sh, 1)
            m = row < 8 - sh
            u = jnp.where(m, u + an * u_sh, u)
            an = jnp.where(m, an * a_sh, an)
        a_s[...] = an.reshape(t, D)
        u_s[...] = u.reshape(t, D)

        def step(i, c):
            r = pl.multiple_of((t8 - 1 - i) * 8, 8)
            dg = u_s[pl.ds(r, 8), :] + a_s[pl.ds(r, 8), :] * c
            dh_s[pl.ds(r, 8), :] = dg
            return dg[0:1, :]

        lax.fori_loop(0, t8, step, carry[0:1, :], unroll=4)
        dh = dh_s[...]
        carry[0:1, :] = a[0:1, :] * dh[0:1, :]

        d_a = dh * h_prev
        dux = dh * xa
        d_mult = dux * gi
        d_gi = dux * mult
        d_xa = dh * (mult * gi)
        d_loga = d_a * a - d_mult * ((a * a) / mult)
        d_gr = d_loga * (-LRU_C * sp)
        d_sp = jnp.sum(d_loga * (-LRU_C * gr), axis=0, keepdims=True)
        glam_ref[...] += d_sp * (-_sigmoid(-lam))
        d_pi = d_gi * gi * (1.0 - gi)
        d_pr = d_gr * gr * (1.0 - gr)
        gbx_ref[...] += jnp.sum(d_pi, axis=0, keepdims=True)
        gba_ref[...] += jnp.sum(d_pr, axis=0, keepdims=True)
        dpib = d_pi.astype(BF16)
        dprb = d_pr.astype(BF16)
        back = []
        for h in range(NB):
            cs = slice(h * BD, (h + 1) * BD)
            gwx_ref[h] += lax.dot_general(xab[:, cs], dpib[:, cs], TN_DIMS, preferred_element_type=F32)
            gwa_ref[h] += lax.dot_general(xab[:, cs], dprb[:, cs], TN_DIMS, preferred_element_type=F32)
            back.append(lax.dot_general(dpib[:, cs], wx_ref[h], NT_DIMS, preferred_element_type=F32)
                        + lax.dot_general(dprb[:, cs], wa_ref[h], NT_DIMS, preferred_element_type=F32))
        d_xa = d_xa + jnp.concatenate(back, axis=1)

        dext[0:t, :] = d_xa
        later = _shifted(_groups(dext[...]), (-3, -2, -1))
        d_xp = later[0] * cw[0:1, :] + later[1] * cw[1:2, :]
        d_xp = d_xp + later[2] * cw[2:3, :]
        d_xp = d_xp + d_xa * cw[3:4, :]
        dext[t:t + 8, :] = d_xa[0:8, :]
        gcb_ref[...] += jnp.sum(d_xa, axis=0, keepdims=True)
        for k in range(4):
            gcw_ref[k:k + 1, :] += jnp.sum(d_xa * taps[k], axis=0, keepdims=True)
        dz_ref[0] = d_xp.astype(BF16)
        dz_ref[1] = d_ga.astype(BF16)

    rb = lambda b, s: b * ns + (ns - 1 - s)
    halo = lambda b, s: jnp.maximum(rb(b, s) * t8 - 1, 0)
    rep2 = lambda b, s: (0, 0)
    rep3 = lambda b, s: (0, 0, 0)
    return pl.pallas_call(
        body, name="lru_bwd", grid=(nb, ns),
        in_specs=[pl.BlockSpec((1, t, D), lambda b, s: (0, rb(b, s), 0)),
                  pl.BlockSpec((1, 8, D), lambda b, s: (0, halo(b, s), 0)),
                  pl.BlockSpec((1, t, D), lambda b, s: (1, rb(b, s), 0)),
                  pl.BlockSpec((t, D), lambda b, s: (rb(b, s), 0)),
                  pl.BlockSpec((8, D), lambda b, s: (halo(b, s), 0)),
                  pl.BlockSpec((t, D), lambda b, s: (rb(b, s), 0)),
                  pl.BlockSpec((8, D), rep2), pl.BlockSpec((1, D), rep2),
                  pl.BlockSpec((NB, BD, BD), rep3), pl.BlockSpec((NB, BD, BD), rep3),
                  pl.BlockSpec((1, D), rep2), pl.BlockSpec((1, D), rep2), pl.BlockSpec((1, D), rep2)],
        out_specs=[pl.BlockSpec((2, t, D), lambda b, s: (0, rb(b, s), 0)),
                   pl.BlockSpec((8, D), rep2), pl.BlockSpec((1, D), rep2),
                   pl.BlockSpec((NB, BD, BD), rep3), pl.BlockSpec((NB, BD, BD), rep3),
                   pl.BlockSpec((1, D), rep2), pl.BlockSpec((1, D), rep2), pl.BlockSpec((1, D), rep2)],
        out_shape=[SDS((2, n, D), BF16), SDS((8, D), F32), SDS((1, D), F32),
                   SDS((NB, BD, BD), F32), SDS((NB, BD, BD), F32),
                   SDS((1, D), F32), SDS((1, D), F32), SDS((1, D), F32)],
        scratch_shapes=[pltpu.VMEM((t + 8, D), F32), pltpu.VMEM((t + 8, D), F32), pltpu.VMEM((t + 8, D), F32),
                        pltpu.VMEM((t, D), F32), pltpu.VMEM((t, D), F32), pltpu.VMEM((t, D), F32),
                        pltpu.VMEM((8, D), F32)],
        compiler_params=_params(56),
    )(z, z, z, h_all, h_all, dya, cw8, cb, wx, wa, bx, ba, lam)


HG_T = 512
HG_NC = HG_T // CHUNK
BNT_DIMS = (((2,), (2,)), ((0,), (0,)))
BNN_DIMS = (((2,), (1,)), ((0,), (0,)))
BTN_DIMS = (((1,), (1,)), ((0,), (0,)))


def _lower_bound(lg):
    m = jnp.max(lg, axis=0, keepdims=True)
    e = jnp.exp(lg - m)
    return e[0:1, :] / jnp.sum(e, axis=0, keepdims=True)


def _tri(upper):
    r = lax.broadcasted_iota(jnp.int32, (HG_NC, CHUNK, CHUNK), 1)
    c = lax.broadcasted_iota(jnp.int32, (HG_NC, CHUNK, CHUNK), 2)
    return (c >= r) if upper else (r >= c)


def _bdot(a, b, dims):
    return lax.dot_general(a, b, dims, preferred_element_type=F32)


def _tri_sums(upper, a):
    tri = _tri(upper).astype(BF16)
    a1 = a.astype(BF16)
    r1 = a - a1.astype(F32)
    a2 = r1.astype(BF16)
    a3 = (r1 - a2.astype(F32)).astype(BF16)
    return _bdot(tri, a1, BNN_DIMS) + (_bdot(tri, a2, BNN_DIMS) + _bdot(tri, a3, BNN_DIMS))


def _chunks(a):
    return a.reshape(HG_NC, CHUNK, BD)


def _hg_tile(q, fp, lb):
    q, fp = _chunks(q), _chunks(fp)
    sig = _sigmoid(fp)
    f = lb + (1.0 - lb) * sig
    log_f = jnp.log(f)
    k = 1.0 - f
    b = _tri_sums(False, log_f)
    b_mid = b[:, CHUNK // 2:CHUNK // 2 + 1, :]
    b_last = b[:, CHUNK - 1:CHUNK, :]
    sq = _sigmoid(q)
    qh = q * sq
    e_qi = jnp.exp(b - b_mid)
    e_ki = jnp.exp(b_mid - b)
    e_qs = jnp.exp(b)
    e_ks = jnp.exp(b_last - b)
    dc = jnp.exp(b_last)
    q_in = (qh * e_qi) * HG_SCALE
    k_in = k * e_ki
    q_st = (qh * e_qs) * HG_SCALE
    k_st = k * e_ks
    att = _bdot(q_in.astype(BF16), k_in.astype(BF16), BNT_DIMS)
    att = jnp.where(_tri(False), att, 0.0)
    return dict(q=q, sig=sig, f=f, k=k, sq=sq, e_qi=e_qi, e_ki=e_ki, e_qs=e_qs, e_ks=e_ks, dc=dc,
                q_in=q_in, k_in=k_in, q_st=q_st, k_st=k_st, att=att)


def _hgrn_fwd(z, lb_logits, hg_g, nb, s_len):
    n = nb * s_len
    t = HG_T
    ns = s_len // t
    nchunk = s_len // CHUNK

    def body(q_ref, f_ref, v_ref, gb_ref, lg_ref, g_ref, o_ref, yb_ref, st_ref, st):
        @pl.when(pl.program_id(1) == 0)
        def _():
            st[...] = jnp.zeros((NB, BD, BD), F32)

        def head(h, carry):
            cols = pl.ds(pl.multiple_of(h * BD, BD), BD)
            lb = _lower_bound(lg_ref[:, cols])
            ck = _hg_tile(q_ref[0, :, cols], f_ref[0, :, cols], lb)
            vb = _chunks(v_ref[0, :, cols]).astype(BF16)
            kv = _bdot(vb, ck["k_st"].astype(BF16), BTN_DIMS)
            states = [st[h]]
            for c in range(HG_NC):
                states.append(states[c] * ck["dc"][c] + kv[c])
            st[h] = states[HG_NC]
            s_in = jnp.stack(states[:HG_NC], axis=0)
            st_ref[h] = s_in
            o = (_bdot(ck["att"].astype(BF16), vb, BNN_DIMS)
                 + _bdot(ck["q_st"].astype(BF16), s_in.astype(BF16), BNT_DIMS))
            o_ref[:, cols] = o.reshape(t, BD)
            r = lax.rsqrt(jnp.mean(o * o, axis=-1, keepdims=True) + EPS)
            gb = _chunks(gb_ref[0, :, cols])
            yb_ref[:, cols] = (((o * r) * g_ref[...]) * (gb * _sigmoid(gb))).astype(BF16).reshape(t, BD)
            return carry

        lax.fori_loop(0, NB, head, 0, unroll=4)

    seg = lambda j: pl.BlockSpec((1, t, D), lambda b, s: (j, b * ns + s, 0))
    tile = pl.BlockSpec((t, D), lambda b, s: (b * ns + s, 0))
    return pl.pallas_call(
        body, name="hgrn_fwd", grid=(nb, ns),
        in_specs=[seg(2), seg(3), seg(4), seg(5),
                  pl.BlockSpec((2, D), lambda b, s: (0, 0)), pl.BlockSpec((1, BD), lambda b, s: (0, 0))],
        out_specs=[tile, tile, pl.BlockSpec((NB, HG_NC, BD, BD), lambda b, s: (b, s, 0, 0))],
        out_shape=[SDS((n, D), F32), SDS((n, D), BF16), SDS((nb * NB, nchunk, BD, BD), F32)],
        scratch_shapes=[pltpu.VMEM((NB, BD, BD), F32)],
        compiler_params=_params(56),
    )(z, z, z, z, lb_logits, hg_g)


def _hgrn_bwd(z, o_all, st_all, dyb, lb_logits, hg_g, nb, s_len):
    n = nb * s_len
    t = HG_T
    ns = s_len // t

    def body(q_ref, f_ref, v_ref, gb_ref, o_ref, st_ref, dyb_ref, lg_ref, g_ref,
             dz_ref, glg_ref, ghg_ref, dst, dlb):
        b, s = pl.program_id(0), pl.program_id(1)

        @pl.when((b == 0) & (s == 0))
        def _():
            ghg_ref[...] = jnp.zeros((1, BD), F32)
            dlb[...] = jnp.zeros((8, D), F32)

        @pl.when(s == 0)
        def _():
            dst[...] = jnp.zeros((NB, BD, BD), F32)

        g = g_ref[...]

        def head(h, carry):
            cols = pl.ds(pl.multiple_of(h * BD, BD), BD)
            lb = _lower_bound(lg_ref[:, cols])
            ck = _hg_tile(q_ref[0, :, cols], f_ref[0, :, cols], lb)
            q = ck["q"]
            vb = _chunks(v_ref[0, :, cols]).astype(BF16)
            gb = _chunks(gb_ref[0, :, cols])
            o = _chunks(o_ref[:, cols])
            dyb_v = _chunks(dyb_ref[:, cols])
            s_in = st_ref[h]

            sgb = _sigmoid(gb)
            r = lax.rsqrt(jnp.mean(o * o, axis=-1, keepdims=True) + EPS)
            ohat = o * r
            d_on = dyb_v * (gb * sgb)
            d_gb = dyb_v * (ohat * g) * (sgb * (1.0 + gb * (1.0 - sgb)))
            ghg_ref[...] += jnp.sum(jnp.sum(d_on * ohat, axis=1), axis=0, keepdims=True)
            tt = d_on * g
            d_o = r * (tt - ohat * jnp.mean(tt * ohat, axis=-1, keepdims=True))
            dob = d_o.astype(BF16)

            attb = ck["att"].astype(BF16)
            q_inb, k_inb = ck["q_in"].astype(BF16), ck["k_in"].astype(BF16)
            q_stb, k_stb = ck["q_st"].astype(BF16), ck["k_st"].astype(BF16)
            d_att = jnp.where(_tri(False), _bdot(dob, vb, BNT_DIMS), 0.0).astype(BF16)
            d_q_in = _bdot(d_att, k_inb, BNN_DIMS)
            d_k_in = _bdot(d_att, q_inb, BTN_DIMS)
            d_q_st = _bdot(dob, s_in.astype(BF16), BNN_DIMS)
            qdo = _bdot(dob, q_stb, BTN_DIMS)
            d_states = [None] * HG_NC + [dst[h]]
            for c in reversed(range(HG_NC)):
                d_states[c] = d_states[c + 1] * ck["dc"][c] + qdo[c]
            dst[h] = d_states[0]
            ds_out = jnp.stack(d_states[1:], axis=0)
            dsb = ds_out.astype(BF16)
            d_v = _bdot(attb, dob, BTN_DIMS) + _bdot(k_stb, dsb, BNT_DIMS)
            d_k_st = _bdot(vb, dsb, BNN_DIMS)
            d_dc = jnp.sum(ds_out * s_in, axis=1, keepdims=True)

            p_qi = d_q_in * ck["q_in"]
            p_ki = d_k_in * ck["k_in"]
            p_qs = d_q_st * ck["q_st"]
            p_ks = d_k_st * ck["k_st"]
            d_qh = (d_q_in * ck["e_qi"] + d_q_st * ck["e_qs"]) * HG_SCALE
            d_k = d_k_in * ck["e_ki"] + d_k_st * ck["e_ks"]
            d_b = (p_qi - p_ki) + (p_qs - p_ks)
            d_b_mid = jnp.sum(p_ki - p_qi, axis=1, keepdims=True)
            d_b_last = jnp.sum(p_ks, axis=1, keepdims=True) + d_dc * ck["dc"]
            rowi = lax.broadcasted_iota(jnp.int32, (HG_NC, CHUNK, BD), 1)
            d_b = d_b + jnp.where(rowi == CHUNK // 2, d_b_mid, 0.0) + jnp.where(rowi == CHUNK - 1, d_b_last, 0.0)
            d_logf = _tri_sums(True, d_b)
            d_f = d_logf / ck["f"] - d_k
            sig, sq = ck["sig"], ck["sq"]
            d_fp = d_f * (1.0 - lb) * (sig * (1.0 - sig))
            dlb[0:1, cols] += jnp.sum(jnp.sum(d_f * (1.0 - sig), axis=1), axis=0, keepdims=True)
            d_q = d_qh * (sq * (1.0 + q * (1.0 - sq)))
            dz_ref[0, :, cols] = d_q.astype(BF16).reshape(t, BD)
            dz_ref[1, :, cols] = d_fp.astype(BF16).reshape(t, BD)
            dz_ref[2, :, cols] = d_v.astype(BF16).reshape(t, BD)
            dz_ref[3, :, cols] = d_gb.astype(BF16).reshape(t, BD)
            return carry

        lax.fori_loop(0, NB, head, 0, unroll=2)

        @pl.when((b == nb - 1) & (s == ns - 1))
        def _():
            lb = _lower_bound(lg_ref[...])
            dl = dlb[0:1, :] * (lb * (1.0 - lb))
            glg_ref[0:1, :] = dl
            glg_ref[1:2, :] = -dl

    rb = lambda b, s: b * ns + (ns - 1 - s)
    seg = lambda j: pl.BlockSpec((1, t, D), lambda b, s: (j, rb(b, s), 0))
    tile = pl.BlockSpec((t, D), lambda b, s: (rb(b, s), 0))
    return pl.pallas_call(
        body, name="hgrn_bwd", grid=(nb, ns),
        in_specs=[seg(2), seg(3), seg(4), seg(5), tile,
                  pl.BlockSpec((NB, HG_NC, BD, BD), lambda b, s: (b, ns - 1 - s, 0, 0)),
                  tile, pl.BlockSpec((2, D), lambda b, s: (0, 0)), pl.BlockSpec((1, BD), lambda b, s: (0, 0))],
        out_specs=[pl.BlockSpec((4, t, D), lambda b, s: (0, rb(b, s), 0)),
                   pl.BlockSpec((2, D), lambda b, s: (0, 0)), pl.BlockSpec((1, BD), lambda b, s: (0, 0))],
        out_shape=[SDS((4, n, D), BF16), SDS((2, D), F32), SDS((1, BD), F32)],
        scratch_shapes=[pltpu.VMEM((NB, BD, BD), F32), pltpu.VMEM((8, D), F32)],
        compiler_params=_params(60),
    )(z, z, z, z, o_all, st_all, dyb, lb_logits, hg_g)


def _mid(ya, yb, z, b_merge, x2, tgt, fin_g, pa, pb, wo):
    n = x2.shape[0]
    tm = 256
    ni = n // tm

    def body(ya_ref, yb_ref, gma_ref, gmb_ref, bm_ref, x_ref, t_ref, fg_ref, pa_hbm, pb_hbm, wo_hbm,
             dx2_ref, dya_ref, dyb_ref, dgm_ref, loss_ref, gfg_ref, gbm_ref, gm_hbm,
             pa_v, pb_v, wo_v, gpa_v, gpb_v, gwo_v, sem):
        i = pl.program_id(0)
        by_owner = lambda g: g.reshape(NB, BD, D)
        loads = [pltpu.make_async_copy(src, dst, sem.at[k])
                 for k, (src, dst) in enumerate(((pa_hbm, pa_v), (pb_hbm, pb_v), (wo_hbm, wo_v)))]
        stores = [pltpu.make_async_copy(src, dst, sem.at[k])
                  for k, (src, dst) in enumerate((g, gm_hbm.at[:, pl.ds(slot * BD, BD), :])
                                                 for slot, g in enumerate((gpa_v, gpb_v, gwo_v)))]

        @pl.when(i == 0)
        def _():
            for cp in loads:
                cp.start()
            for ref in (gpa_v, gpb_v, gwo_v, loss_ref, gfg_ref, gbm_ref):
                ref[...] = jnp.zeros(ref.shape, F32)
            for cp in loads:
                cp.wait()

        ya_v = ya_ref[...]
        yb_v = yb_ref[...]
        out_a = jnp.dot(ya_v, pa_v[...], preferred_element_type=F32)
        out_b = jnp.dot(yb_v, pb_v[...], preferred_element_type=F32)
        bm = bm_ref[...]
        g_a = _sigmoid(gma_ref[0] + bm[:, 0:D])
        g_b = _sigmoid(gmb_ref[0] + bm[:, D:2 * D])
        mixed = g_a * out_a + g_b * out_b
        mixb = mixed.astype(BF16)
        xo = x_ref[...] + jnp.dot(mixb, wo_v[...], preferred_element_type=F32)
        r = lax.rsqrt(jnp.mean(xo * xo, axis=-1, keepdims=True) + EPS)
        xn = xo * r
        fg = fg_ref[...]
        e = xn * fg - t_ref[...]
        loss_ref[...] += 0.5 * jnp.sum(jnp.mean(e * e, axis=-1, keepdims=True))
        dy = e * (1.0 / D)
        gfg_ref[...] += jnp.sum(dy * xn, axis=0, keepdims=True)
        dxn = dy * fg
        dx2 = r * (dxn - xn * jnp.mean(dxn * xn, axis=-1, keepdims=True))
        dx2_ref[...] = dx2
        dx2b = dx2.astype(BF16)
        d_mixed = lax.dot_general(dx2b, wo_v[...], NT_DIMS, preferred_element_type=F32)
        gwo_v[...] += by_owner(lax.dot_general(mixb, dx2b, TN_DIMS, preferred_element_type=F32))
        d_oa = (d_mixed * g_a).astype(BF16)
        d_ob = (d_mixed * g_b).astype(BF16)
        dgm_a = (d_mixed * out_a) * (g_a * (1.0 - g_a))
        dgm_b = (d_mixed * out_b) * (g_b * (1.0 - g_b))
        gbm_ref[:, 0:D] += jnp.sum(dgm_a, axis=0, keepdims=True)
        gbm_ref[:, D:2 * D] += jnp.sum(dgm_b, axis=0, keepdims=True)
        dgm_ref[0] = dgm_a.astype(BF16)
        dgm_ref[1] = dgm_b.astype(BF16)
        dya_ref[...] = lax.dot_general(d_oa, pa_v[...], NT_DIMS, preferred_element_type=F32)
        dyb_ref[...] = lax.dot_general(d_ob, pb_v[...], NT_DIMS, preferred_element_type=F32)
        gpa_v[...] += by_owner(lax.dot_general(ya_v, d_oa, TN_DIMS, preferred_element_type=F32))
        gpb_v[...] += by_owner(lax.dot_general(yb_v, d_ob, TN_DIMS, preferred_element_type=F32))

        @pl.when(i == ni - 1)
        def _():
            for cp in stores:
                cp.start()
            for cp in stores:
                cp.wait()

    rows = pl.BlockSpec((tm, D), lambda i: (i, 0))
    rep = lambda shape: pl.BlockSpec(shape, lambda i: (0,) * len(shape))
    return pl.pallas_call(
        body, name="mid", grid=(ni,),
        in_specs=[rows, rows,
                  pl.BlockSpec((1, tm, D), lambda i: (6, i, 0)), pl.BlockSpec((1, tm, D), lambda i: (7, i, 0)),
                  rep((1, 2 * D)), rows, rows, rep((1, D)), ANY, ANY, ANY],
        out_specs=[rows, rows, rows, pl.BlockSpec((2, tm, D), lambda i: (0, i, 0)),
                   rep((8, BD)), rep((1, D)), rep((1, 2 * D)), ANY],
        out_shape=[SDS((n, D), F32), SDS((n, D), F32), SDS((n, D), F32), SDS((2, n, D), BF16),
                   SDS((8, BD), F32), SDS((1, D), F32), SDS((1, 2 * D), F32),
                   SDS((NB, MID_ROWS, D), F32)],
        scratch_shapes=[pltpu.VMEM((D, D), BF16)] * 3 + [pltpu.VMEM((NB, BD, D), F32)] * 3 + [pltpu.SemaphoreType.DMA((3,))],
        compiler_params=_params(60),
    )(ya, yb, z, z, b_merge, x2, tgt, fin_g, pa, pb, wo)


def _dz_specs(tm, ni, row_major):
    if row_major:
        ia = lambda i, j: (jnp.minimum(j, 1), i, 0)
        ib = lambda i, j: (jnp.clip(j - 2, 0, 3), i, 0)
        im = lambda i, j: (jnp.clip(j - 6, 0, 1), i, 0)
    else:
        last = ni - 1
        ia = lambda j, i: (jnp.minimum(j, 1), jnp.where(j < 2, i, last), 0)
        ib = lambda j, i: (jnp.clip(j - 2, 0, 3), jnp.where(j < 2, 0, jnp.where(j < 6, i, last)), 0)
        im = lambda j, i: (jnp.clip(j - 6, 0, 1), jnp.where(j < 6, 0, i), 0)
    return [pl.BlockSpec((1, tm, D), f) for f in (ia, ib, im)]


def _inproj_bwd_x(dza, dzb, dzm, w_all, x2, dx2, norm_g, after):
    n = x2.shape[0]
    tm = 512
    ni = n // tm

    def body(dza_ref, dzb_ref, dzm_ref, w_ref, x_ref, dx2_ref, g_ref, after_ref, gx_ref, gg_ref, acc):
        i, j = pl.program_id(0), pl.program_id(1)

        @pl.when((i == 0) & (j == 0))
        def _():
            gg_ref[...] = jnp.zeros((1, D), F32)

        @pl.when(j == 0)
        def _():
            acc[...] = jnp.zeros((tm, D), F32)

        def add(ref):
            acc[...] += lax.dot_general(ref[0], w_ref[0], NT_DIMS, preferred_element_type=F32)

        pl.when(j < 2)(lambda: add(dza_ref))
        pl.when((j >= 2) & (j < 6))(lambda: add(dzb_ref))
        pl.when(j >= 6)(lambda: add(dzm_ref))

        @pl.when(j == NB - 1)
        def _():
            x = x_ref[...]
            r = lax.rsqrt(jnp.mean(x * x, axis=-1, keepdims=True) + EPS)
            xn = x * r
            dh = acc[...]
            gg_ref[...] += jnp.sum(dh * xn, axis=0, keepdims=True)
            dxn = dh * g_ref[...]
            gx_ref[...] = dx2_ref[...] + r * (dxn - xn * jnp.mean(dxn * xn, axis=-1, keepdims=True))

    rows = pl.BlockSpec((tm, D), lambda i, j: (i, 0))
    return pl.pallas_call(
        body, name="inproj_bwd_x", grid=(ni, NB),
        in_specs=_dz_specs(tm, ni, True) + [pl.BlockSpec((1, D, D), lambda i, j: (j, 0, 0)), rows, rows,
                                             pl.BlockSpec((1, D), lambda i, j: (0, 0)), ANY],
        out_specs=[rows, pl.BlockSpec((1, D), lambda i, j: (0, 0))],
        out_shape=[SDS((n, D), F32), SDS((1, D), F32)],
        scratch_shapes=[pltpu.VMEM((tm, D), F32)],
        compiler_params=_params(48),
    )(dza, dzb, dzm, w_all, x2, dx2, norm_g, after)


def _walk_tables(order, ni):
    rows = []
    for lo, hi in ((0, 2), (2, 6), (6, 8)):
        active = [j for j, g in enumerate(order) if lo <= g < hi]
        block, tile = [], []
        for j, g in enumerate(order):
            before = [a for a in active if a < j]
            if lo <= g < hi:
                block.append(g - lo), tile.append(-1)
            elif before:
                block.append(order[before[-1]] - lo), tile.append(ni - 1)
            else:
                block.append(order[active[0]] - lo), tile.append(0)
        rows += [block, tile]
    return rows


def _inproj_bwd_w(core, dza, dzb, dzm, h_all, g_m):
    n = h_all.shape[0]
    tm = min(n, 2048)
    ni = n // tm
    orders = [[2 * q + 1 - c for q in range(4)] + [2 * q + c for q in range(4)] for c in (0, 1)]
    tables = jnp.asarray([[order] + _walk_tables(order, ni) for order in orders], jnp.int32)
    walk = jnp.where(core == 0, tables[0], tables[1])

    def body(walk_ref, dza_ref, dzb_ref, dzm_ref, h_ref, gm_hbm, out_bf, own_f32, got_w, got_m,
             acc, stage, theirs, send_sems, recv_sems, local_sems):
        j, i = pl.program_id(0), pl.program_id(1)
        group = walk_ref[0, j]
        x, y, c = _place()
        sibling = (x, y, 1 - c)

        def send_w(q):
            return pltpu.make_async_remote_copy(
                src_ref=stage.at[q % 2], dst_ref=got_w.at[q], send_sem=send_sems.at[q], recv_sem=recv_sems.at[q],
                device_id=sibling, device_id_type=MESH)

        def send_m(q):
            return pltpu.make_async_remote_copy(
                src_ref=gm_hbm.at[2 * q + (1 - c)], dst_ref=got_m.at[q], send_sem=send_sems.at[4 + q],
                recv_sem=recv_sems.at[4 + q], device_id=sibling, device_id_type=MESH)

        def fetch(q):
            return pltpu.make_async_copy(got_w.at[q], theirs, local_sems.at[0])

        @pl.when((j == 0) & (i == 0))
        def _():
            for q in range(4):
                send_m(q).start()

        @pl.when(i == 0)
        def _():
            acc[...] = jnp.zeros((D, D), F32)

        def add(ref):
            acc[...] += lax.dot_general(h_ref[...], ref[0], TN_DIMS, preferred_element_type=F32)

        pl.when(group < 2)(lambda: add(dza_ref))
        pl.when((group >= 2) & (group < 6))(lambda: add(dzb_ref))
        pl.when(group >= 6)(lambda: add(dzm_ref))

        for q in range(4):
            @pl.when((i == ni - 1) & (j == q))
            def _(q=q):
                if q >= 2:
                    send_w(q - 2).wait_send()
                stage[q % 2] = acc[...].astype(BF16)
                send_w(q).start()

        for q in range(4):
            @pl.when((j == 4 + q) & (i == 0))
            def _(q=q):
                send_w(q).wait_recv()
                fetch(q).start()

            @pl.when((j == 4 + q) & (i == ni - 1))
            def _(q=q):
                if q == 0:
                    send_w(2).wait_send()
                    send_w(3).wait_send()
                fetch(q).wait()
                total = acc[...] + theirs[...].astype(F32)
                other_x, other_y = x != q // 2, y != q % 2
                slot = jnp.where(other_x & other_y, 2, jnp.where(other_x, 0, 1))

                @pl.when(other_x | other_y)
                def _():
                    stage[0] = total.astype(BF16)
                    out = pltpu.make_async_copy(stage.at[0], out_bf.at[slot], local_sems.at[1])
                    out.start()
                    out.wait()

                @pl.when(jnp.logical_not(other_x | other_y))
                def _():
                    acc[...] = total
                    out = pltpu.make_async_copy(acc, own_f32, local_sems.at[1])
                    out.start()
                    out.wait()

        @pl.when((j == NB - 1) & (i == ni - 1))
        def _():
            for q in range(4):
                send_m(q).wait_send()
                send_m(q).wait_recv()

    def dz_spec(k):
        return pl.BlockSpec((1, tm, D), lambda j, i, w: (w[1 + 2 * k, j], jnp.where(w[2 + 2 * k, j] < 0, i, w[2 + 2 * k, j]), 0))

    return pl.pallas_call(
        body, name="inproj_bwd_w",
        grid_spec=pltpu.PrefetchScalarGridSpec(
            num_scalar_prefetch=1, grid=(NB, ni),
            in_specs=[dz_spec(0), dz_spec(1), dz_spec(2), pl.BlockSpec((tm, D), lambda j, i, w: (i, 0)), ANY],
            out_specs=[ANY, ANY, ANY, ANY],
            scratch_shapes=[pltpu.VMEM((D, D), F32), pltpu.VMEM((2, D, D), BF16), pltpu.VMEM((D, D), BF16),
                            pltpu.SemaphoreType.DMA((8,)), pltpu.SemaphoreType.DMA((8,)), pltpu.SemaphoreType.DMA((2,))]),
        out_shape=[SDS((3, D, D), BF16), SDS((D, D), F32), SDS((4, D, D), BF16), SDS((4,) + g_m.shape[1:], F32)],
        compiler_params=_params(58),
    )(walk, dza, dzb, dzm, h_all, g_m)


def _adamw(w, g, m, v):
    rows, cols = w.shape
    tr = _row_tile(rows)

    spec = pl.BlockSpec((tr, cols), lambda i: (i, 0))
    return pl.pallas_call(
        functools.partial(_adam_refs), name="adamw", grid=(rows // tr,), in_specs=[spec] * 4, out_specs=[spec] * 3,
        out_shape=[SDS((rows, cols), F32)] * 3, compiler_params=_params(32),
    )(w, g, m, v)


def _adam_refs(w_ref, g_ref, m_ref, v_ref, d_ref, nm_ref, nv_ref):
    gv = g_ref[...]
    nm = ADAM_B1 * m_ref[...] + (1.0 - ADAM_B1) * gv
    nv = ADAM_B2 * v_ref[...] + (1.0 - ADAM_B2) * (gv * gv)
    m_hat = nm / (1.0 - ADAM_B1 ** ADAM_STEP)
    v_hat = nv / (1.0 - ADAM_B2 ** ADAM_STEP)
    d_ref[...] = -ADAM_LR * (m_hat / (jnp.sqrt(v_hat) + ADAM_EPS) + ADAM_WD * w_ref[...])
    nm_ref[...] = nm
    nv_ref[...] = nv


def _adamw_small(ws, gs, ms, vs):
    k = len(ws)

    def body(*refs):
        ins, outs = refs[:4 * k], refs[4 * k:7 * k]
        vin, vout = refs[7 * k:11 * k], refs[11 * k:14 * k]
        load_sems, store_sems = refs[14 * k:]
        loads = [pltpu.make_async_copy(ins[i], vin[i], load_sems.at[i]) for i in range(4 * k)]
        for cp in loads:
            cp.start()
        for cp in loads:
            cp.wait()
        for i in range(k):
            _adam_refs(*[vin[part * k + i] for part in range(4)], *[vout[part * k + i] for part in range(3)])
        stores = [pltpu.make_async_copy(vout[i], outs[i], store_sems.at[i]) for i in range(3 * k)]
        for cp in stores:
            cp.start()
        for cp in stores:
            cp.wait()

    shapes = [SDS(w.shape, F32) for w in ws]
    vmem = [pltpu.VMEM(w.shape, F32) for w in ws]
    out = pl.pallas_call(
        body, name="adamw_small", out_shape=shapes * 3, in_specs=[HBM] * (4 * k), out_specs=[HBM] * (3 * k),
        scratch_shapes=vmem * 7 + [pltpu.SemaphoreType.DMA((4 * k,)), pltpu.SemaphoreType.DMA((3 * k,))],
        compiler_params=_params(32),
    )(*ws, *gs, *ms, *vs)
    return out[:k], out[k:2 * k], out[2 * k:]


def _allgather(blocks, dtypes, name):
    na = len(blocks)

    def body(*refs):
        ins, outs, stages = refs[:na], refs[na:2 * na], refs[2 * na:3 * na]
        send_sems, recv_sems, local_sems = refs[3 * na:]
        x, y, c = _place()
        me, sibling = (x, y, c), (x, y, 1 - c)
        chips = [(1 - x, y), (x, 1 - y), (1 - x, 1 - y)]
        blk = lambda p: 4 * p[0] + 2 * p[1] + p[2]

        def copy(a, k, block, to, src=None):
            return pltpu.make_async_remote_copy(
                src_ref=outs[a].at[blk(block)] if src is None else src, dst_ref=outs[a].at[blk(block)],
                send_sem=send_sems.at[7 * a + k], recv_sem=recv_sems.at[7 * a + k],
                device_id=to, device_id_type=MESH)

        mine, first, passed = [], [], []
        for a in range(na):
            stages[a][...] = ins[a][...].astype(dtypes[a])
            mine.append(pltpu.make_async_copy(stages[a], outs[a].at[blk(me)], local_sems.at[a]))
            mine[-1].start()
            first.append(copy(a, 0, me, sibling, src=stages[a]))
            first += [copy(a, 1 + j, me, (*chip, c), src=stages[a]) for j, chip in enumerate(chips)]
        for cp in first:
            cp.start()
        for j, chip in enumerate(chips):
            for a in range(na):
                copy(a, 1 + j, (*chip, c), me).wait_recv()
                passed.append(copy(a, 4 + j, (*chip, c), sibling))
                passed[-1].start()
        for a in range(na):
            copy(a, 0, sibling, me).wait_recv()
            for j, chip in enumerate(chips):
                copy(a, 4 + j, (*chip, 1 - c), me).wait_recv()
        for cp in first + passed:
            cp.wait_send()
        for cp in mine:
            cp.wait()

    return pl.pallas_call(
        body, name=name,
        in_specs=[pl.BlockSpec(memory_space=pltpu.VMEM)] * na, out_specs=[ANY] * na,
        out_shape=[SDS((NB,) + b.shape, dt) for b, dt in zip(blocks, dtypes)],
        scratch_shapes=[pltpu.VMEM(b.shape, dt) for b, dt in zip(blocks, dtypes)]
        + [pltpu.SemaphoreType.DMA((7 * na,)), pltpu.SemaphoreType.DMA((7 * na,)), pltpu.SemaphoreType.DMA((na,))],
        compiler_params=_params(40),
    )(*blocks)


HBM = pl.BlockSpec(memory_space=pltpu.HBM)
SEMS = pl.BlockSpec(memory_space=pltpu.SEMAPHORE)
EFFECT = pltpu.SideEffectType.DATAFLOW_SIDE_EFFECTING


def _chip_copies(srcs, lands, send_sems, recv_sems):
    x, y, c = _place()
    return [pltpu.make_async_remote_copy(
        src_ref=srcs[a].at[slot], dst_ref=lands[a].at[slot],
        send_sem=send_sems.at[3 * a + slot], recv_sem=recv_sems.at[3 * a + slot],
        device_id=(px, py, c), device_id_type=MESH)
        for a in range(len(srcs)) for slot, (px, py) in enumerate(_other_chips(x, y))]


def _split_start(name, copies, per_array, srcs, lands, after=None):
    na = len(srcs)

    def body(*refs):
        send_sems, recv_sems = refs[-2 * na - 3], refs[-2 * na - 2]
        for cp in copies(refs[:na], refs[na:2 * na], send_sems, recv_sems):
            cp.start()
        refs[-1][...] = jnp.zeros_like(refs[-1])

    hbm = lambda a: pltpu.HBM(a.shape, a.dtype)
    pin = lambda a: pltpu.with_memory_space_constraint(a, pltpu.HBM)
    out = pl.pallas_call(
        body, name=name,
        out_shape=(pltpu.SemaphoreType.DMA((per_array * na,)), pltpu.SemaphoreType.DMA((per_array * na,)),
                   *[hbm(a) for a in srcs], *[hbm(a) for a in lands], SDS((8, BD), F32)),
        in_specs=[HBM] * (2 * na) + ([] if after is None else [ANY]),
        out_specs=(SEMS, SEMS, *[HBM] * (2 * na), pl.BlockSpec(memory_space=pltpu.VMEM)),
        input_output_aliases={i: 2 + i for i in range(2 * na)},
        compiler_params=pltpu.CompilerParams(has_side_effects=EFFECT),
    )(*[pin(a) for a in srcs], *[pin(a) for a in lands], *([] if after is None else [after]))
    return out[0], out[1], out[2:2 + na], out[2 + na:2 + 2 * na], out[-1]


def _split_wait(name, copies, started, after):
    send_sems, recv_sems, srcs, lands, _ = started
    na = len(srcs)

    def body(*refs):
        waits = copies(refs[:na], refs[na:2 * na], refs[2 * na], refs[2 * na + 1])
        for cp in waits:
            cp.wait_send()
        for cp in waits:
            cp.wait_recv()

    hbm = lambda a: pltpu.HBM(a.shape, a.dtype)
    out = pl.pallas_call(
        body, name=name,
        out_shape=(*[hbm(a) for a in srcs], *[hbm(a) for a in lands]),
        in_specs=[HBM] * (2 * na) + [SEMS, SEMS, ANY],
        out_specs=tuple([HBM] * (2 * na)),
        input_output_aliases={i: i for i in range(2 * na)},
        compiler_params=pltpu.CompilerParams(has_side_effects=EFFECT),
    )(*srcs, *lands, send_sems, recv_sems, after)
    return out[na:]


def _add_sibling(place, g, a_in):
    _, r, cols = g.shape
    tr = _row_tile(r)

    def chip(k, pr):
        qx = pr[0] if k in (1, 3) else 1 - pr[0]
        qy = pr[1] if k in (0, 3) else 1 - pr[1]
        return 2 * qx + qy

    def body(place_ref, *refs):
        g_refs, a_refs, (out_ref, own_ref) = refs[0:4], refs[4:8], refs[8:10]
        for k in range(3):
            out_ref[k] = (g_refs[k][0] + a_refs[k][0].astype(F32)).astype(BF16)
        own_ref[...] = g_refs[3][0] + a_refs[3][0].astype(F32)

    mine = lambda k: pl.BlockSpec((1, tr, cols), lambda i, pr: (2 * chip(k, pr) + pr[2], i, 0))
    theirs = lambda k: pl.BlockSpec((1, tr, cols), lambda i, pr: (chip(k, pr), i, 0))
    return pl.pallas_call(
        body, name="add_sibling",
        grid_spec=pltpu.PrefetchScalarGridSpec(
            num_scalar_prefetch=1, grid=(r // tr,),
            in_specs=[mine(k) for k in range(4)] + [theirs(k) for k in range(4)],
            out_specs=[pl.BlockSpec((3, tr, cols), lambda i, pr: (0, i, 0)),
                       pl.BlockSpec((tr, cols), lambda i, pr: (i, 0))]),
        out_shape=[SDS((3, r, cols), BF16), SDS((r, cols), F32)], compiler_params=_params(48),
    )(place, *[g] * 4, *[a_in] * 4)


def _add_chips(own, b_in):
    r, cols = own.shape
    tr = _row_tile(r)

    def body(p_ref, b0_ref, b1_ref, b2_ref, o_ref):
        o_ref[...] = ((p_ref[...] + b0_ref[0].astype(F32)) + b1_ref[0].astype(F32)) + b2_ref[0].astype(F32)

    slot = lambda k: pl.BlockSpec((1, tr, cols), lambda i: (k, i, 0))
    spec = pl.BlockSpec((tr, cols), lambda i: (i, 0))
    return pl.pallas_call(
        body, name="add_chips", grid=(r // tr,), in_specs=[spec, slot(0), slot(1), slot(2)], out_specs=spec,
        out_shape=SDS((r, cols), F32), compiler_params=_params(32),
    )(own, b_in, b_in, b_in)


VEC_NAMES = ("b_merge", "conv_b", "rg_bx", "rg_ba", "rg_lambda", "hg_lb_logits", "hg_norm_g", "final_norm_g")
REP_NAMES = ("rg_wx", "rg_wa", "norm_g") + VEC_NAMES
SMALL_AT = 3 * BD
SMALL_ROWS = 48
MID_ROWS = 448


def _sum_blocks(parts):
    def body(p_ref, o_ref):
        acc = p_ref[0]
        for k in range(1, NB):
            acc = acc + p_ref[k]
        o_ref[...] = acc

    return pl.pallas_call(body, name="sum_blocks", out_shape=SDS(parts.shape[1:], F32))(parts)


def _pack_rows(arrays, width, row_multiple=8):
    flat = jnp.concatenate([a.reshape(-1) for a in arrays])
    rows = -(-flat.shape[0] // width)
    rows = -(-rows // row_multiple) * row_multiple
    return jnp.pad(flat, (0, rows * width - flat.shape[0])).reshape(rows, width)


def _unpack(flat, like):
    out, off = [], 0
    for a in like:
        out.append(flat[off:off + a.size].reshape(a.shape))
        off += a.size
    return out


def kernel(x, w_in, b_merge, conv_w, conv_b, rg_wx, rg_bx, rg_wa, rg_ba, rg_lambda, hg_lb_logits, hg_norm_g, proj_a, proj_b, w_out, norm_g, final_norm_g, loss_target, m_w_in, m_b_merge, m_conv_w, m_conv_b, m_rg_wx, m_rg_bx, m_rg_wa, m_rg_ba, m_rg_lambda, m_hg_lb_logits, m_hg_norm_g, m_proj_a, m_proj_b, m_w_out, m_norm_g, m_final_norm_g, v_w_in, v_b_merge, v_conv_w, v_conv_b, v_rg_wx, v_rg_bx, v_rg_wa, v_rg_ba, v_rg_lambda, v_hg_lb_logits, v_hg_norm_g, v_proj_a, v_proj_b, v_w_out, v_norm_g, v_final_norm_g):
    weights = dict(w_in=w_in, b_merge=b_merge, conv_w=conv_w, conv_b=conv_b, rg_wx=rg_wx, rg_bx=rg_bx, rg_wa=rg_wa,
                   rg_ba=rg_ba, rg_lambda=rg_lambda, hg_lb_logits=hg_lb_logits, hg_norm_g=hg_norm_g, proj_a=proj_a,
                   proj_b=proj_b, w_out=w_out, norm_g=norm_g, final_norm_g=final_norm_g)
    mom1 = dict(w_in=m_w_in, b_merge=m_b_merge, conv_w=m_conv_w, conv_b=m_conv_b, rg_wx=m_rg_wx, rg_bx=m_rg_bx,
                rg_wa=m_rg_wa, rg_ba=m_rg_ba, rg_lambda=m_rg_lambda, hg_lb_logits=m_hg_lb_logits,
                hg_norm_g=m_hg_norm_g, proj_a=m_proj_a, proj_b=m_proj_b, w_out=m_w_out, norm_g=m_norm_g,
                final_norm_g=m_final_norm_g)
    mom2 = dict(w_in=v_w_in, b_merge=v_b_merge, conv_w=v_conv_w, conv_b=v_conv_b, rg_wx=v_rg_wx, rg_bx=v_rg_bx,
                rg_wa=v_rg_wa, rg_ba=v_rg_ba, rg_lambda=v_rg_lambda, hg_lb_logits=v_hg_lb_logits,
                hg_norm_g=v_hg_norm_g, proj_a=v_proj_a, proj_b=v_proj_b, w_out=v_w_out, norm_g=v_norm_g,
                final_norm_g=v_final_norm_g)
    order = list(weights)
    nb, s_len, _ = x.shape
    n = nb * s_len
    px, py, pc = _place()
    place = jnp.stack([px, py, pc]).astype(jnp.int32)

    in_hbm = lambda a: pltpu.with_memory_space_constraint(a, pltpu.HBM)
    norm_gain = in_hbm(norm_g)

    x2 = x.reshape(n, D)
    cw_blk = jnp.pad(conv_w[0], ((0, 4), (0, 0)))
    order_ids = jnp.stack([_block_id(p) for p in _arrival_order(px, py, pc)]).astype(jnp.int32)
    z, h_all, w_all, pa_all, pb_all, wo_all, cw_all = _gather_inproj(
        order_ids, x2, norm_gain, [w_in[0], proj_a[0], proj_b[0], w_out[0], cw_blk], [BF16, BF16, BF16, BF16, F32])
    pa_full, pb_full, wo_full = (a.reshape(D, D) for a in (pa_all, pb_all, wo_all))
    cw8 = in_hbm(cw_all.transpose(1, 0, 2).reshape(8, D))
    wx_b, wa_b = in_hbm(rg_wx[0].astype(BF16)), in_hbm(rg_wa[0].astype(BF16))
    cb, bx, ba, lam = (in_hbm(a.reshape(1, D)) for a in (conv_b, rg_bx, rg_ba, rg_lambda))
    fin_g, b_mrg = in_hbm(final_norm_g.reshape(1, D)), in_hbm(b_merge)
    lb_lg, hg_g = in_hbm(hg_lb_logits), in_hbm(hg_norm_g)

    hlru, ya = _lru_fwd(z, cw8, cb, wx_b, wa_b, bx, ba, lam, nb, s_len)
    o_all, yb, st_all = _hgrn_fwd(z, lb_lg, hg_g, nb, s_len)

    (dx2, dya, dyb, dzm, loss_acc, g_fin, g_bm, g_mid) = _mid(
        ya, yb, z, b_mrg, x2, loss_target.reshape(n, D), fin_g, pa_full, pb_full, wo_full)
    dzb, g_lg, g_hg = _hgrn_bwd(z, o_all, st_all, dyb, lb_lg, hg_g, nb, s_len)
    dza, g_cw8, g_cb, g_wx, g_wa, g_bx, g_ba, g_lam = _lru_bwd(
        z, hlru, dya, cw8, cb, wx_b, wa_b, bx, ba, lam, nb, s_len)

    part = dict(b_merge=g_bm, conv_b=g_cb, rg_bx=g_bx, rg_ba=g_ba, rg_lambda=g_lam, hg_lb_logits=g_lg,
                hg_norm_g=g_hg, final_norm_g=g_fin)
    vec = _pack_rows([part[k] for k in VEC_NAMES], BD)
    vec = jnp.pad(vec, ((0, 16 * NB - vec.shape[0]), (0, 0))).reshape(NB, 2, D)
    rows8 = lambda a: jnp.pad(a, ((0, 0), (0, 8 - a.shape[1]), (0, 0)))
    small = jnp.concatenate([g_wx.reshape(NB, 16, D), g_wa.reshape(NB, 16, D),
                             rows8(g_cw8.reshape(8, NB, BD).transpose(1, 0, 2).reshape(NB, 1, D)), rows8(vec),
                             jnp.zeros((NB, MID_ROWS - SMALL_AT - SMALL_ROWS, D), F32)], axis=1)
    g_m = lax.dynamic_update_slice(g_mid, small, (0, SMALL_AT, 0))
    w_out_bf, w_own, _, m_from_sibling = _inproj_bwd_w(pc, dza, dzb, dzm, h_all, g_m)
    m_out_bf, m_own = _add_sibling(place, g_m, m_from_sibling)
    outgoing = [w_out_bf, m_out_bf]
    chip_sums = _split_start("rs_chips_start", _chip_copies, 3, outgoing, [lax.empty(a.shape, a.dtype) for a in outgoing])
    grad_x, g_ng = _inproj_bwd_x(dza, dzb, dzm, w_all, x2, dx2, norm_gain, chip_sums[-1])
    from_chips = _split_wait("rs_chips_wait", _chip_copies, chip_sums, grad_x)
    r_w = _add_chips(w_own, from_chips[0])
    r_m = _add_chips(m_own, from_chips[1])
    row = lax.broadcasted_iota(jnp.int32, (8, D), 0)
    mine = jnp.where(row == 0, g_ng, jnp.where(row == 1, loss_acc[0:1, 0:1], 0.0))
    tail = jnp.concatenate([r_m[SMALL_AT:SMALL_AT + SMALL_ROWS], mine], axis=0)
    (tail_all,) = _allgather([tail], [F32], "gather_small_grads")
    summed = _sum_blocks(tail_all[:, SMALL_ROWS:SMALL_ROWS + 8])

    grads = dict(w_in=r_w.reshape(1, D, D),
                 proj_a=r_m[0:BD].reshape(1, BD, D), proj_b=r_m[BD:2 * BD].reshape(1, BD, D),
                 w_out=r_m[2 * BD:3 * BD].reshape(1, BD, D),
                 conv_w=r_m[SMALL_AT + 32].reshape(8, BD)[0:4].reshape(1, 4, BD),
                 rg_wx=tail_all[:, 0:16].reshape(1, NB, BD, BD), rg_wa=tail_all[:, 16:32].reshape(1, NB, BD, BD),
                 norm_g=summed[0:1])
    vec_all = tail_all[:, 40:42].reshape(-1)
    for k, gk in zip(VEC_NAMES, _unpack(vec_all, [weights[k] for k in VEC_NAMES])):
        grads[k] = gk

    delta, new_m, new_v = {}, {}, {}
    flat2 = lambda a: a.reshape(-1, a.shape[-1])
    for k in ("w_in", "proj_a", "proj_b", "w_out"):
        outs = _adamw(*[flat2(t[k]) for t in (weights, grads, mom1, mom2)])
        delta[k], new_m[k], new_v[k] = (a.reshape(weights[k].shape) for a in outs)
    rep = list(REP_NAMES) + ["conv_w"]
    outs = _adamw_small(*[[flat2(t[k]) for k in rep] for t in (weights, grads, mom1, mom2)])
    for tgt, arrays in zip((delta, new_m, new_v), outs):
        for k, a in zip(rep, arrays):
            tgt[k] = a.reshape(weights[k].shape)

    return (summed[1, 0], grad_x.reshape(x.shape), *[grads[k] for k in order], *[delta[k] for k in order],
            *[new_m[k] for k in order], *[new_v[k] for k in order])
```

```python
import functools

import jax
import jax.numpy as jnp
from jax import lax
from jax.experimental import pallas as pl
from jax.experimental.pallas import tpu as pltpu

F32 = jnp.float32
BF16 = jnp.bfloat16
SDS = jax.ShapeDtypeStruct
MESH = pl.DeviceIdType.MESH
ANY = pl.BlockSpec(memory_space=pl.ANY)

D = 1024
NB = 8
BD = D // NB
CHUNK = 64
EPS = 1e-6
LRU_C = 8.0
HG_SCALE = BD ** -0.5
ADAM_LR, ADAM_B1, ADAM_B2, ADAM_EPS, ADAM_WD, ADAM_STEP = 0.001, 0.9, 0.999, 1e-08, 0.01, 10

NT_DIMS = (((1,), (1,)), ((), ()))
TN_DIMS = (((0,), (0,)), ((), ()))


def _params(vmem_mib):
    return pltpu.CompilerParams(vmem_limit_bytes=vmem_mib << 20)


def _row_tile(rows, most=256):
    assert rows % 8 == 0
    return max(t for t in range(8, min(rows, most) + 1, 8) if rows % t == 0)


def _sigmoid(v):
    return 0.5 * (jnp.tanh(0.5 * v) + 1.0)


def _groups(v):
    return v.reshape(v.shape[0] // 8, 8, v.shape[1])


def _softplus_neg(lam):
    t = -lam
    e = jnp.exp(-jnp.abs(t))
    w = 1.0 + e
    d = w - 1.0
    l1p = jnp.where(d == 0.0, e, jnp.log(w) * (e / jnp.where(d == 0.0, 1.0, d)))
    return jnp.maximum(t, 0.0) + l1p


def _place():
    return lax.axis_index("x"), lax.axis_index("y"), lax.axis_index("c")


def _other_chips(x, y):
    return [(1 - x, y), (x, 1 - y), (1 - x, 1 - y)]


def _block_id(p):
    return 4 * p[0] + 2 * p[1] + p[2]


def _core_chips(x, y, c):
    near, far, diag = _other_chips(x, y)
    pick = lambda a, b: (jnp.where(c == 0, a[0], b[0]), jnp.where(c == 0, a[1], b[1]))
    return [pick(near, far), pick(far, near), diag]


def _arrival_order(x, y, c):
    first, second, diag = _core_chips(x, y, c)
    return [(x, y, c), (x, y, 1 - c), (*first, c), (*second, 1 - c), (*second, c), (*first, 1 - c),
            (*diag, c), (*diag, 1 - c)]


def _gather_inproj(order_ids, x2, norm_g, blocks, dtypes):
    na = len(blocks)
    n = x2.shape[0]
    tm = min(n, 1024)
    ni = n // tm

    def body(order_ref, x_ref, g_ref, *refs):
        ins, (z_ref, h_ref), outs = refs[:na], refs[na:na + 2], refs[na + 2:2 * na + 2]
        stages = refs[2 * na + 2:3 * na + 2]
        h_full, wbuf, send_sems, recv_sems, local_sems, wsems, hsem = refs[3 * na + 2:]
        j, i = pl.program_id(0), pl.program_id(1)
        x, y, c = _place()
        me, sibling = (x, y, c), (x, y, 1 - c)
        chips = _core_chips(x, y, c)
        sibling_chips = [chips[1], chips[0], chips[2]]
        small = range(1, na)

        def copy(a, k, block, to, src=None):
            return pltpu.make_async_remote_copy(
                src_ref=outs[a].at[_block_id(block)] if src is None else src, dst_ref=outs[a].at[_block_id(block)],
                send_sem=send_sems.at[7 * a + k], recv_sem=recv_sems.at[7 * a + k],
                device_id=to, device_id_type=MESH)

        def local(a):
            return pltpu.make_async_copy(stages[a], outs[a].at[_block_id(me)], local_sems.at[a])

        def landed(a, slot):
            copy(a, 1 + slot, (*chips[slot], c), me).wait_recv()
            copy(a, 4 + slot, (*chips[slot], c), sibling).start()
            if slot == 0:
                copy(a, 3, (*chips[0], c), (*chips[1], c)).start()

        def diagonal_and_small():
            landed(0, 2)
            for a in small:
                landed(a, 0)
                landed(a, 1)

        def passed_on(a, slot):
            copy(a, 4 + slot, (*sibling_chips[slot], 1 - c), me).wait_recv()

        def sibling_here_send_second():
            copy(0, 0, sibling, me).wait_recv()
            for a in range(na):
                copy(a, 2, me, (*chips[1], c), src=stages[a]).start()

        @pl.when((j == 0) & (i == 0))
        def _():
            for a in range(na):
                stages[a][...] = ins[a][...].astype(dtypes[a])
                local(a).start()
            for a in range(na):
                copy(a, 0, me, sibling, src=stages[a]).start()
                copy(a, 1, me, (*chips[0], c), src=stages[a]).start()

        @pl.when(j == 0)
        def _():
            xv = x_ref[...]
            r = lax.rsqrt(jnp.mean(xv * xv, axis=-1, keepdims=True) + EPS)
            hb = ((xv * r) * g_ref[...]).astype(BF16)
            h_full[pl.ds(pl.multiple_of(i * tm, tm), tm), :] = hb

        save_h = pltpu.make_async_copy(h_full, h_ref, hsem)
        pl.when((j == 0) & (i == ni - 1))(save_h.start)

        steps = [
            lambda: local(0).wait(),
            sibling_here_send_second,
            lambda: landed(0, 0),
            lambda: passed_on(0, 0),
            lambda: landed(0, 1),
            lambda: passed_on(0, 1),
            diagonal_and_small,
            lambda: passed_on(0, 2),
        ]
        def w_load(k):
            return pltpu.make_async_copy(outs[0].at[order_ref[k]], wbuf.at[k % 2], wsems.at[k % 2])

        for k, step in enumerate(steps):
            @pl.when((j == 0) & (i == 0) if k == 0 else (j == k - 1) & (i == ni - 1))
            def _(k=k, step=step):
                step()
                w_load(k).start()

        pl.when(i == 0)(lambda: w_load(j).wait())
        z_ref[0] = jnp.dot(h_full[pl.ds(pl.multiple_of(i * tm, tm), tm), :], wbuf[j % 2], preferred_element_type=F32)

        @pl.when((j == NB - 1) & (i == ni - 1))
        def _():
            save_h.wait()
            for a in small:
                landed(a, 2)
            for a in small:
                local(a).wait()
                copy(a, 0, sibling, me).wait_recv()
                for slot in range(3):
                    passed_on(a, slot)
            for a in range(na):
                copy(a, 0, me, sibling, src=stages[a]).wait_send()
                for slot, chip in enumerate(chips):
                    copy(a, 1 + slot, me, (*chip, c), src=stages[a]).wait_send()
                    copy(a, 4 + slot, (*chip, c), sibling).wait_send()

    rows_once = lambda j, i, order: (jnp.where(j == 0, i, ni - 1), 0)
    vmem = pl.BlockSpec(memory_space=pltpu.VMEM)
    return pl.pallas_call(
        body, name="gather_inproj",
        grid_spec=pltpu.PrefetchScalarGridSpec(
            num_scalar_prefetch=1, grid=(NB, ni),
            in_specs=[pl.BlockSpec((tm, D), rows_once), pl.BlockSpec((1, D), lambda j, i, order: (0, 0))] + [vmem] * na,
            out_specs=[pl.BlockSpec((1, tm, D), lambda j, i, order: (order[j], i, 0)), ANY] + [ANY] * na,
            scratch_shapes=[pltpu.VMEM(b.shape, dt) for b, dt in zip(blocks, dtypes)]
            + [pltpu.VMEM((n, D), BF16), pltpu.VMEM((2, D, D), BF16),
               pltpu.SemaphoreType.DMA((7 * na,)), pltpu.SemaphoreType.DMA((7 * na,)),
               pltpu.SemaphoreType.DMA((na,)), pltpu.SemaphoreType.DMA((2,)), pltpu.SemaphoreType.DMA(())]),
        out_shape=[SDS((NB, n, D), F32), SDS((n, D), BF16)] + [SDS((NB,) + b.shape, dt) for b, dt in zip(blocks, dtypes)],
        compiler_params=_params(56),
    )(order_ids, x2, norm_g, *blocks)


LRU_T = 256


def _shifted(groups, shifts):
    row = lax.broadcasted_iota(jnp.int32, (groups.shape[0] - 1,) + groups.shape[1:], 1)
    out = []
    for s in shifts:
        y = pltpu.roll(groups, s % 8, 1)
        moved = jnp.where(row >= s, y[1:], y[:-1]) if s > 0 else jnp.where(row < 8 + s, y[:-1], y[1:])
        out.append(moved.reshape(-1, groups.shape[2]))
    return out


def _conv(taps, cw, cb):
    acc = taps[0] * cw[0:1, :] + taps[1] * cw[1:2, :]
    acc = acc + taps[2] * cw[2:3, :]
    acc = acc + taps[3] * cw[3:4, :]
    return cb + acc


def _lru_gates(xa, wx_ref, wa_ref, bx, ba, lam):
    xab = xa.astype(BF16)
    pis, prs = [], []
    for h in range(NB):
        xs = xab[:, h * BD:(h + 1) * BD]
        pis.append(jnp.dot(xs, wx_ref[h], preferred_element_type=F32))
        prs.append(jnp.dot(xs, wa_ref[h], preferred_element_type=F32))
    gi = _sigmoid(jnp.concatenate(pis, axis=1) + bx)
    gr = _sigmoid(jnp.concatenate(prs, axis=1) + ba)
    sp = _softplus_neg(lam)
    log_a = (-LRU_C * gr) * sp
    a = jnp.exp(log_a)
    mult = jnp.sqrt(-jnp.tanh(log_a) * (a * a + 1.0))
    return xab, gi, gr, sp, a, mult


def _lru_fwd(z, cw8, cb, wx, wa, bx, ba, lam, nb, s_len):
    n = nb * s_len
    t = LRU_T
    ns = s_len // t

    def body(xp_ref, ga_ref, cw_ref, cb_ref, wx_ref, wa_ref, bx_ref, ba_ref, lam_ref,
             h_ref, ya_ref, ext, a_s, u_s, carry):
        @pl.when(pl.program_id(1) == 0)
        def _():
            ext[0:8, :] = jnp.zeros((8, D), F32)
            carry[...] = jnp.zeros((8, D), F32)

        xp = xp_ref[0]
        ext[8:8 + t, :] = xp
        xa = _conv(_shifted(_groups(ext[...]), (3, 2, 1)) + [xp], cw_ref[...], cb_ref[...])
        ext[0:8, :] = xp[t - 8:t, :]
        _, gi, _, _, a, mult = _lru_gates(xa, wx_ref, wa_ref, bx_ref[...], ba_ref[...], lam_ref[...])
        u = (mult * gi) * xa
        a, u = _groups(a), _groups(u)
        row = lax.broadcasted_iota(jnp.int32, a.shape, 1)
        for sh in (1, 2, 4):
            a_sh = pltpu.roll(a, sh, 1)
            u_sh = pltpu.roll(u, sh, 1)
            m = row >= sh
            u = jnp.where(m, a * u_sh + u, u)
            a = jnp.where(m, a * a_sh, a)
        a_s[...] = a.reshape(t, D)
        u_s[...] = u.reshape(t, D)

        def step(g, c):
            r = pl.multiple_of(g * 8, 8)
            hg = u_s[pl.ds(r, 8), :] + a_s[pl.ds(r, 8), :] * c
            h_ref[pl.ds(r, 8), :] = hg
            return hg[7:8, :]

        c_out = lax.fori_loop(0, t // 8, step, carry[0:1, :], unroll=4)
        carry[0:1, :] = c_out
        ga = ga_ref[0]
        ya_ref[...] = (h_ref[...] * (ga * _sigmoid(ga))).astype(BF16)

    row_map = lambda b, s: (b * ns + s, 0)
    rep2 = lambda b, s: (0, 0)
    rep3 = lambda b, s: (0, 0, 0)
    return pl.pallas_call(
        body, name="lru_fwd", grid=(nb, ns),
        in_specs=[pl.BlockSpec((1, t, D), lambda b, s: (0, b * ns + s, 0)),
                  pl.BlockSpec((1, t, D), lambda b, s: (1, b * ns + s, 0)),
                  pl.BlockSpec((8, D), rep2), pl.BlockSpec((1, D), rep2),
                  pl.BlockSpec((NB, BD, BD), rep3), pl.BlockSpec((NB, BD, BD), rep3),
                  pl.BlockSpec((1, D), rep2), pl.BlockSpec((1, D), rep2), pl.BlockSpec((1, D), rep2)],
        out_specs=[pl.BlockSpec((t, D), row_map), pl.BlockSpec((t, D), row_map)],
        out_shape=[SDS((n, D), F32), SDS((n, D), BF16)],
        scratch_shapes=[pltpu.VMEM((t + 8, D), F32), pltpu.VMEM((t, D), F32), pltpu.VMEM((t, D), F32),
                        pltpu.VMEM((8, D), F32)],
        compiler_params=_params(48),
    )(z, z, cw8, cb, wx, wa, bx, ba, lam)


def _lru_bwd(z, h_all, dya, cw8, cb, wx, wa, bx, ba, lam, nb, s_len):
    n = nb * s_len
    t = LRU_T
    ns = s_len // t
    t8 = t // 8

    def body(xp_ref, xph_ref, ga_ref, h_ref, hh_ref, dya_ref, cw_ref, cb_ref, wx_ref, wa_ref, bx_ref, ba_ref,
             lam_ref, dz_ref, gcw_ref, gcb_ref, gwx_ref, gwa_ref, gbx_ref, gba_ref, glam_ref,
             ext, hext, dext, a_s, u_s, dh_s, carry):
        b, s = pl.program_id(0), pl.program_id(1)
        first_tile = s == ns - 1

        @pl.when((b == 0) & (s == 0))
        def _():
            for ref in (gcw_ref, gcb_ref, gwx_ref, gwa_ref, gbx_ref, gba_ref, glam_ref):
                ref[...] = jnp.zeros(ref.shape, F32)

        @pl.when(s == 0)
        def _():
            dext[t:t + 8, :] = jnp.zeros((8, D), F32)
            carry[...] = jnp.zeros((8, D), F32)

        keep = jnp.where(first_tile, 0.0, 1.0)
        xp = xp_ref[0]
        ext[0:8, :] = xph_ref[0] * keep
        ext[8:8 + t, :] = xp
        hext[0:8, :] = hh_ref[...] * keep
        hext[8:8 + t, :] = h_ref[...]
        cw = cw_ref[...]
        lam = lam_ref[...]
        taps = _shifted(_groups(ext[...]), (3, 2, 1)) + [xp]
        xa = _conv(taps, cw, cb_ref[...])
        xab, gi, gr, sp, a, mult = _lru_gates(xa, wx_ref, wa_ref, bx_ref[...], ba_ref[...], lam)
        (h_prev,) = _shifted(_groups(hext[...]), (1,))
        ga = ga_ref[0]
        sg = _sigmoid(ga)
        dya_v = dya_ref[...]
        d_ga = dya_v * h_ref[...] * (sg * (1.0 + ga * (1.0 - sg)))
        g_in = dya_v * (ga * sg)

        (an,) = _shifted(jnp.concatenate([_groups(a), jnp.ones((1, 8, D), F32)], axis=0), (-1,))
        an, u = _groups(an), _groups(g_in)
        row = lax.broadcasted_iota(jnp.int32, an.shape, 1)
        for sh in (1, 2, 4):
            a_sh = pltpu.roll(an, 8 - sh, 1)
            u_sh = pltpu.roll(u, 8 - sh, 1)
            m = row < 8 - sh
            u = jnp.where(m, u + an * u_sh, u)
            an = jnp.where(m, an * a_sh, an)
        a_s[...] = an.reshape(t, D)
        u_s[...] = u.reshape(t, D)

        def step(i, c):
            r = pl.multiple_of((t8 - 1 - i) * 8, 8)
            dg = u_s[pl.ds(r, 8), :] + a_s[pl.ds(r, 8), :] * c
            dh_s[pl.ds(r, 8), :] = dg
            return dg[0:1, :]

        lax.fori_loop(0, t8, step, carry[0:1, :], unroll=4)
        dh = dh_s[...]
        carry[0:1, :] = a[0:1, :] * dh[0:1, :]

        d_a = dh * h_prev
        dux = dh * xa
        d_mult = dux * gi
        d_gi = dux * mult
        d_xa = dh * (mult * gi)
        d_loga = d_a * a - d_mult * ((a * a) / mult)
        d_gr = d_loga * (-LRU_C * sp)
        d_sp = jnp.sum(d_loga * (-LRU_C * gr), axis=0, keepdims=True)
        glam_ref[...] += d_sp * (-_sigmoid(-lam))
        d_pi = d_gi * gi * (1.0 - gi)
        d_pr = d_gr * gr * (1.0 - gr)
        gbx_ref[...] += jnp.sum(d_pi, axis=0, keepdims=True)
        gba_ref[...] += jnp.sum(d_pr, axis=0, keepdims=True)
        dpib = d_pi.astype(BF16)
        dprb = d_pr.astype(BF16)
        back = []
        for h in range(NB):
            cs = slice(h * BD, (h + 1) * BD)
            gwx_ref[h] += lax.dot_general(xab[:, cs], dpib[:, cs], TN_DIMS, preferred_element_type=F32)
            gwa_ref[h] += lax.dot_general(xab[:, cs], dprb[:, cs], TN_DIMS, preferred_element_type=F32)
            back.append(lax.dot_general(dpib[:, cs], wx_ref[h], NT_DIMS, preferred_element_type=F32)
                        + lax.dot_general(dprb[:, cs], wa_ref[h], NT_DIMS, preferred_element_type=F32))
        d_xa = d_xa + jnp.concatenate(back, axis=1)

        dext[0:t, :] = d_xa
        later = _shifted(_groups(dext[...]), (-3, -2, -1))
        d_xp = later[0] * cw[0:1, :] + later[1] * cw[1:2, :]
        d_xp = d_xp + later[2] * cw[2:3, :]
        d_xp = d_xp + d_xa * cw[3:4, :]
        dext[t:t + 8, :] = d_xa[0:8, :]
        gcb_ref[...] += jnp.sum(d_xa, axis=0, keepdims=True)
        for k in range(4):
            gcw_ref[k:k + 1, :] += jnp.sum(d_xa * taps[k], axis=0, keepdims=True)
        dz_ref[0] = d_xp.astype(BF16)
        dz_ref[1] = d_ga.astype(BF16)

    rb = lambda b, s: b * ns + (ns - 1 - s)
    halo = lambda b, s: jnp.maximum(rb(b, s) * t8 - 1, 0)
    rep2 = lambda b, s: (0, 0)
    rep3 = lambda b, s: (0, 0, 0)
    return pl.pallas_call(
        body, name="lru_bwd", grid=(nb, ns),
        in_specs=[pl.BlockSpec((1, t, D), lambda b, s: (0, rb(b, s), 0)),
                  pl.BlockSpec((1, 8, D), lambda b, s: (0, halo(b, s), 0)),
                  pl.BlockSpec((1, t, D), lambda b, s: (1, rb(b, s), 0)),
                  pl.BlockSpec((t, D), lambda b, s: (rb(b, s), 0)),
                  pl.BlockSpec((8, D), lambda b, s: (halo(b, s), 0)),
                  pl.BlockSpec((t, D), lambda b, s: (rb(b, s), 0)),
                  pl.BlockSpec((8, D), rep2), pl.BlockSpec((1, D), rep2),
                  pl.BlockSpec((NB, BD, BD), rep3), pl.BlockSpec((NB, BD, BD), rep3),
                  pl.BlockSpec((1, D), rep2), pl.BlockSpec((1, D), rep2), pl.BlockSpec((1, D), rep2)],
        out_specs=[pl.BlockSpec((2, t, D), lambda b, s: (0, rb(b, s), 0)),
                   pl.BlockSpec((8, D), rep2), pl.BlockSpec((1, D), rep2),
                   pl.BlockSpec((NB, BD, BD), rep3), pl.BlockSpec((NB, BD, BD), rep3),
                   pl.BlockSpec((1, D), rep2), pl.BlockSpec((1, D), rep2), pl.BlockSpec((1, D), rep2)],
        out_shape=[SDS((2, n, D), BF16), SDS((8, D), F32), SDS((1, D), F32),
                   SDS((NB, BD, BD), F32), SDS((NB, BD, BD), F32),
                   SDS((1, D), F32), SDS((1, D), F32), SDS((1, D), F32)],
        scratch_shapes=[pltpu.VMEM((t + 8, D), F32), pltpu.VMEM((t + 8, D), F32), pltpu.VMEM((t + 8, D), F32),
                        pltpu.VMEM((t, D), F32), pltpu.VMEM((t, D), F32), pltpu.VMEM((t, D), F32),
                        pltpu.VMEM((8, D), F32)],
        compiler_params=_params(56),
    )(z, z, z, h_all, h_all, dya, cw8, cb, wx, wa, bx, ba, lam)


HG_T = 512
HG_NC = HG_T // CHUNK
BNT_DIMS = (((2,), (2,)), ((0,), (0,)))
BNN_DIMS = (((2,), (1,)), ((0,), (0,)))
BTN_DIMS = (((1,), (1,)), ((0,), (0,)))


def _lower_bound(lg):
    m = jnp.max(lg, axis=0, keepdims=True)
    e = jnp.exp(lg - m)
    return e[0:1, :] / jnp.sum(e, axis=0, keepdims=True)


def _tri(upper):
    r = lax.broadcasted_iota(jnp.int32, (HG_NC, CHUNK, CHUNK), 1)
    c = lax.broadcasted_iota(jnp.int32, (HG_NC, CHUNK, CHUNK), 2)
    return (c >= r) if upper else (r >= c)


def _bdot(a, b, dims):
    return lax.dot_general(a, b, dims, preferred_element_type=F32)


def _tri_sums(upper, a):
    tri = _tri(upper).astype(BF16)
    a1 = a.astype(BF16)
    r1 = a - a1.astype(F32)
    a2 = r1.astype(BF16)
    a3 = (r1 - a2.astype(F32)).astype(BF16)
    return _bdot(tri, a1, BNN_DIMS) + (_bdot(tri, a2, BNN_DIMS) + _bdot(tri, a3, BNN_DIMS))


def _chunks(a):
    return a.reshape(HG_NC, CHUNK, BD)


def _hg_tile(q, fp, lb):
    q, fp = _chunks(q), _chunks(fp)
    sig = _sigmoid(fp)
    f = lb + (1.0 - lb) * sig
    log_f = jnp.log(f)
    k = 1.0 - f
    b = _tri_sums(False, log_f)
    b_mid = b[:, CHUNK // 2:CHUNK // 2 + 1, :]
    b_last = b[:, CHUNK - 1:CHUNK, :]
    sq = _sigmoid(q)
    qh = q * sq
    e_qi = jnp.exp(b - b_mid)
    e_ki = jnp.exp(b_mid - b)
    e_qs = jnp.exp(b)
    e_ks = jnp.exp(b_last - b)
    dc = jnp.exp(b_last)
    q_in = (qh * e_qi) * HG_SCALE
    k_in = k * e_ki
    q_st = (qh * e_qs) * HG_SCALE
    k_st = k * e_ks
    att = _bdot(q_in.astype(BF16), k_in.astype(BF16), BNT_DIMS)
    att = jnp.where(_tri(False), att, 0.0)
    return dict(q=q, sig=sig, f=f, k=k, sq=sq, e_qi=e_qi, e_ki=e_ki, e_qs=e_qs, e_ks=e_ks, dc=dc,
                q_in=q_in, k_in=k_in, q_st=q_st, k_st=k_st, att=att)


def _hgrn_fwd(z, lb_logits, hg_g, nb, s_len):
    n = nb * s_len
    t = HG_T
    ns = s_len // t
    nchunk = s_len // CHUNK

    def body(q_ref, f_ref, v_ref, gb_ref, lg_ref, g_ref, o_ref, yb_ref, st_ref, st):
        @pl.when(pl.program_id(1) == 0)
        def _():
            st[...] = jnp.zeros((NB, BD, BD), F32)

        def head(h, carry):
            cols = pl.ds(pl.multiple_of(h * BD, BD), BD)
            lb = _lower_bound(lg_ref[:, cols])
            ck = _hg_tile(q_ref[0, :, cols], f_ref[0, :, cols], lb)
            vb = _chunks(v_ref[0, :, cols]).astype(BF16)
            kv = _bdot(vb, ck["k_st"].astype(BF16), BTN_DIMS)
            states = [st[h]]
            for c in range(HG_NC):
                states.append(states[c] * ck["dc"][c] + kv[c])
            st[h] = states[HG_NC]
            s_in = jnp.stack(states[:HG_NC], axis=0)
            st_ref[h] = s_in
            o = (_bdot(ck["att"].astype(BF16), vb, BNN_DIMS)
                 + _bdot(ck["q_st"].astype(BF16), s_in.astype(BF16), BNT_DIMS))
            o_ref[:, cols] = o.reshape(t, BD)
            r = lax.rsqrt(jnp.mean(o * o, axis=-1, keepdims=True) + EPS)
            gb = _chunks(gb_ref[0, :, cols])
            yb_ref[:, cols] = (((o * r) * g_ref[...]) * (gb * _sigmoid(gb))).astype(BF16).reshape(t, BD)
            return carry

        lax.fori_loop(0, NB, head, 0, unroll=4)

    seg = lambda j: pl.BlockSpec((1, t, D), lambda b, s: (j, b * ns + s, 0))
    tile = pl.BlockSpec((t, D), lambda b, s: (b * ns + s, 0))
    return pl.pallas_call(
        body, name="hgrn_fwd", grid=(nb, ns),
        in_specs=[seg(2), seg(3), seg(4), seg(5),
                  pl.BlockSpec((2, D), lambda b, s: (0, 0)), pl.BlockSpec((1, BD), lambda b, s: (0, 0))],
        out_specs=[tile, tile, pl.BlockSpec((NB, HG_NC, BD, BD), lambda b, s: (b, s, 0, 0))],
        out_shape=[SDS((n, D), F32), SDS((n, D), BF16), SDS((nb * NB, nchunk, BD, BD), F32)],
        scratch_shapes=[pltpu.VMEM((NB, BD, BD), F32)],
        compiler_params=_params(56),
    )(z, z, z, z, lb_logits, hg_g)


def _hgrn_bwd(z, o_all, st_all, dyb, lb_logits, hg_g, nb, s_len):
    n = nb * s_len
    t = HG_T
    ns = s_len // t

    def body(q_ref, f_ref, v_ref, gb_ref, o_ref, st_ref, dyb_ref, lg_ref, g_ref,
             dz_ref, glg_ref, ghg_ref, dst, dlb):
        b, s = pl.program_id(0), pl.program_id(1)

        @pl.when((b == 0) & (s == 0))
        def _():
            ghg_ref[...] = jnp.zeros((1, BD), F32)
            dlb[...] = jnp.zeros((8, D), F32)

        @pl.when(s == 0)
        def _():
            dst[...] = jnp.zeros((NB, BD, BD), F32)

        g = g_ref[...]

        def head(h, carry):
            cols = pl.ds(pl.multiple_of(h * BD, BD), BD)
            lb = _lower_bound(lg_ref[:, cols])
            ck = _hg_tile(q_ref[0, :, cols], f_ref[0, :, cols], lb)
            q = ck["q"]
            vb = _chunks(v_ref[0, :, cols]).astype(BF16)
            gb = _chunks(gb_ref[0, :, cols])
            o = _chunks(o_ref[:, cols])
            dyb_v = _chunks(dyb_ref[:, cols])
            s_in = st_ref[h]

            sgb = _sigmoid(gb)
            r = lax.rsqrt(jnp.mean(o * o, axis=-1, keepdims=True) + EPS)
            ohat = o * r
            d_on = dyb_v * (gb * sgb)
            d_gb = dyb_v * (ohat * g) * (sgb * (1.0 + gb * (1.0 - sgb)))
            ghg_ref[...] += jnp.sum(jnp.sum(d_on * ohat, axis=1), axis=0, keepdims=True)
            tt = d_on * g
            d_o = r * (tt - ohat * jnp.mean(tt * ohat, axis=-1, keepdims=True))
            dob = d_o.astype(BF16)

            attb = ck["att"].astype(BF16)
            q_inb, k_inb = ck["q_in"].astype(BF16), ck["k_in"].astype(BF16)
            q_stb, k_stb = ck["q_st"].astype(BF16), ck["k_st"].astype(BF16)
            d_att = jnp.where(_tri(False), _bdot(dob, vb, BNT_DIMS), 0.0).astype(BF16)
            d_q_in = _bdot(d_att, k_inb, BNN_DIMS)
            d_k_in = _bdot(d_att, q_inb, BTN_DIMS)
            d_q_st = _bdot(dob, s_in.astype(BF16), BNN_DIMS)
            qdo = _bdot(dob, q_stb, BTN_DIMS)
            d_states = [None] * HG_NC + [dst[h]]
            for c in reversed(range(HG_NC)):
                d_states[c] = d_states[c + 1] * ck["dc"][c] + qdo[c]
            dst[h] = d_states[0]
            ds_out = jnp.stack(d_states[1:], axis=0)
            dsb = ds_out.astype(BF16)
            d_v = _bdot(attb, dob, BTN_DIMS) + _bdot(k_stb, dsb, BNT_DIMS)
            d_k_st = _bdot(vb, dsb, BNN_DIMS)
            d_dc = jnp.sum(ds_out * s_in, axis=1, keepdims=True)

            p_qi = d_q_in * ck["q_in"]
            p_ki = d_k_in * ck["k_in"]
            p_qs = d_q_st * ck["q_st"]
            p_ks = d_k_st * ck["k_st"]
            d_qh = (d_q_in * ck["e_qi"] + d_q_st * ck["e_qs"]) * HG_SCALE
            d_k = d_k_in * ck["e_ki"] + d_k_st * ck["e_ks"]
            d_b = (p_qi - p_ki) + (p_qs - p_ks)
            d_b_mid = jnp.sum(p_ki - p_qi, axis=1, keepdims=True)
            d_b_last = jnp.sum(p_ks, axis=1, keepdims=True) + d_dc * ck["dc"]
            rowi = lax.broadcasted_iota(jnp.int32, (HG_NC, CHUNK, BD), 1)
            d_b = d_b + jnp.where(rowi == CHUNK // 2, d_b_mid, 0.0) + jnp.where(rowi == CHUNK - 1, d_b_last, 0.0)
            d_logf = _tri_sums(True, d_b)
            d_f = d_logf / ck["f"] - d_k
            sig, sq = ck["sig"], ck["sq"]
            d_fp = d_f * (1.0 - lb) * (sig * (1.0 - sig))
            dlb[0:1, cols] += jnp.sum(jnp.sum(d_f * (1.0 - sig), axis=1), axis=0, keepdims=True)
            d_q = d_qh * (sq * (1.0 + q * (1.0 - sq)))
            dz_ref[0, :, cols] = d_q.astype(BF16).reshape(t, BD)
            dz_ref[1, :, cols] = d_fp.astype(BF16).reshape(t, BD)
            dz_ref[2, :, cols] = d_v.astype(BF16).reshape(t, BD)
            dz_ref[3, :, cols] = d_gb.astype(BF16).reshape(t, BD)
            return carry

        lax.fori_loop(0, NB, head, 0, unroll=2)

        @pl.when((b == nb - 1) & (s == ns - 1))
        def _():
            lb = _lower_bound(lg_ref[...])
            dl = dlb[0:1, :] * (lb * (1.0 - lb))
            glg_ref[0:1, :] = dl
            glg_ref[1:2, :] = -dl

    rb = lambda b, s: b * ns + (ns - 1 - s)
    seg = lambda j: pl.BlockSpec((1, t, D), lambda b, s: (j, rb(b, s), 0))
    tile = pl.BlockSpec((t, D), lambda b, s: (rb(b, s), 0))
    return pl.pallas_call(
        body, name="hgrn_bwd", grid=(nb, ns),
        in_specs=[seg(2), seg(3), seg(4), seg(5), tile,
                  pl.BlockSpec((NB, HG_NC, BD, BD), lambda b, s: (b, ns - 1 - s, 0, 0)),
                  tile, pl.BlockSpec((2, D), lambda b, s: (0, 0)), pl.BlockSpec((1, BD), lambda b, s: (0, 0))],
        out_specs=[pl.BlockSpec((4, t, D), lambda b, s: (0, rb(b, s), 0)),
                   pl.BlockSpec((2, D), lambda b, s: (0, 0)), pl.BlockSpec((1, BD), lambda b, s: (0, 0))],
        out_shape=[SDS((4, n, D), BF16), SDS((2, D), F32), SDS((1, BD), F32)],
        scratch_shapes=[pltpu.VMEM((NB, BD, BD), F32), pltpu.VMEM((8, D), F32)],
        compiler_params=_params(60),
    )(z, z, z, z, o_all, st_all, dyb, lb_logits, hg_g)


def _mid(ya, yb, z, b_merge, x2, tgt, fin_g, pa, pb, wo):
    n = x2.shape[0]
    tm = 256
    ni = n // tm

    def body(ya_ref, yb_ref, gma_ref, gmb_ref, bm_ref, x_ref, t_ref, fg_ref, pa_hbm, pb_hbm, wo_hbm,
             dx2_ref, dya_ref, dyb_ref, dgm_ref, loss_ref, gfg_ref, gbm_ref, gm_hbm,
             pa_v, pb_v, wo_v, gpa_v, gpb_v, gwo_v, sem):
        i = pl.program_id(0)
        by_owner = lambda g: g.reshape(NB, BD, D)
        loads = [pltpu.make_async_copy(src, dst, sem.at[k])
                 for k, (src, dst) in enumerate(((pa_hbm, pa_v), (pb_hbm, pb_v), (wo_hbm, wo_v)))]
        stores = [pltpu.make_async_copy(src, dst, sem.at[k])
                  for k, (src, dst) in enumerate((g, gm_hbm.at[:, pl.ds(slot * BD, BD), :])
                                                 for slot, g in enumerate((gpa_v, gpb_v, gwo_v)))]

        @pl.when(i == 0)
        def _():
            for cp in loads:
                cp.start()
            for ref in (gpa_v, gpb_v, gwo_v, loss_ref, gfg_ref, gbm_ref):
                ref[...] = jnp.zeros(ref.shape, F32)
            for cp in loads:
                cp.wait()

        ya_v = ya_ref[...]
        yb_v = yb_ref[...]
        out_a = jnp.dot(ya_v, pa_v[...], preferred_element_type=F32)
        out_b = jnp.dot(yb_v, pb_v[...], preferred_element_type=F32)
        bm = bm_ref[...]
        g_a = _sigmoid(gma_ref[0] + bm[:, 0:D])
        g_b = _sigmoid(gmb_ref[0] + bm[:, D:2 * D])
        mixed = g_a * out_a + g_b * out_b
        mixb = mixed.astype(BF16)
        xo = x_ref[...] + jnp.dot(mixb, wo_v[...], preferred_element_type=F32)
        r = lax.rsqrt(jnp.mean(xo * xo, axis=-1, keepdims=True) + EPS)
        xn = xo * r
        fg = fg_ref[...]
        e = xn * fg - t_ref[...]
        loss_ref[...] += 0.5 * jnp.sum(jnp.mean(e * e, axis=-1, keepdims=True))
        dy = e * (1.0 / D)
        gfg_ref[...] += jnp.sum(dy * xn, axis=0, keepdims=True)
        dxn = dy * fg
        dx2 = r * (dxn - xn * jnp.mean(dxn * xn, axis=-1, keepdims=True))
        dx2_ref[...] = dx2
        dx2b = dx2.astype(BF16)
        d_mixed = lax.dot_general(dx2b, wo_v[...], NT_DIMS, preferred_element_type=F32)
        gwo_v[...] += by_owner(lax.dot_general(mixb, dx2b, TN_DIMS, preferred_element_type=F32))
        d_oa = (d_mixed * g_a).astype(BF16)
        d_ob = (d_mixed * g_b).astype(BF16)
        dgm_a = (d_mixed * out_a) * (g_a * (1.0 - g_a))
        dgm_b = (d_mixed * out_b) * (g_b * (1.0 - g_b))
        gbm_ref[:, 0:D] += jnp.sum(dgm_a, axis=0, keepdims=True)
        gbm_ref[:, D:2 * D] += jnp.sum(dgm_b, axis=0, keepdims=True)
        dgm_ref[0] = dgm_a.astype(BF16)
        dgm_ref[1] = dgm_b.astype(BF16)
        dya_ref[...] = lax.dot_general(d_oa, pa_v[...], NT_DIMS, preferred_element_type=F32)
        dyb_ref[...] = lax.dot_general(d_ob, pb_v[...], NT_DIMS, preferred_element_type=F32)
        gpa_v[...] += by_owner(lax.dot_general(ya_v, d_oa, TN_DIMS, preferred_element_type=F32))
        gpb_v[...] += by_owner(lax.dot_general(yb_v, d_ob, TN_DIMS, preferred_element_type=F32))

        @pl.when(i == ni - 1)
        def _():
            for cp in stores:
                cp.start()
            for cp in stores:
                cp.wait()

    rows = pl.BlockSpec((tm, D), lambda i: (i, 0))
    rep = lambda shape: pl.BlockSpec(shape, lambda i: (0,) * len(shape))
    return pl.pallas_call(
        body, name="mid", grid=(ni,),
        in_specs=[rows, rows,
                  pl.BlockSpec((1, tm, D), lambda i: (6, i, 0)), pl.BlockSpec((1, tm, D), lambda i: (7, i, 0)),
                  rep((1, 2 * D)), rows, rows, rep((1, D)), ANY, ANY, ANY],
        out_specs=[rows, rows, rows, pl.BlockSpec((2, tm, D), lambda i: (0, i, 0)),
                   rep((8, BD)), rep((1, D)), rep((1, 2 * D)), ANY],
        out_shape=[SDS((n, D), F32), SDS((n, D), F32), SDS((n, D), F32), SDS((2, n, D), BF16),
                   SDS((8, BD), F32), SDS((1, D), F32), SDS((1, 2 * D), F32),
                   SDS((NB, MID_ROWS, D), F32)],
        scratch_shapes=[pltpu.VMEM((D, D), BF16)] * 3 + [pltpu.VMEM((NB, BD, D), F32)] * 3 + [pltpu.SemaphoreType.DMA((3,))],
        compiler_params=_params(60),
    )(ya, yb, z, z, b_merge, x2, tgt, fin_g, pa, pb, wo)


def _dz_specs(tm, ni, row_major):
    if row_major:
        ia = lambda i, j: (jnp.minimum(j, 1), i, 0)
        ib = lambda i, j: (jnp.clip(j - 2, 0, 3), i, 0)
        im = lambda i, j: (jnp.clip(j - 6, 0, 1), i, 0)
    else:
        last = ni - 1
        ia = lambda j, i: (jnp.minimum(j, 1), jnp.where(j < 2, i, last), 0)
        ib = lambda j, i: (jnp.clip(j - 2, 0, 3), jnp.where(j < 2, 0, jnp.where(j < 6, i, last)), 0)
        im = lambda j, i: (jnp.clip(j - 6, 0, 1), jnp.where(j < 6, 0, i), 0)
    return [pl.BlockSpec((1, tm, D), f) for f in (ia, ib, im)]


def _inproj_bwd_x(dza, dzb, dzm, w_all, x2, dx2, norm_g, after):
    n = x2.shape[0]
    tm = 512
    ni = n // tm

    def body(dza_ref, dzb_ref, dzm_ref, w_ref, x_ref, dx2_ref, g_ref, after_ref, gx_ref, gg_ref, acc):
        i, j = pl.program_id(0), pl.program_id(1)

        @pl.when((i == 0) & (j == 0))
        def _():
            gg_ref[...] = jnp.zeros((1, D), F32)

        @pl.when(j == 0)
        def _():
            acc[...] = jnp.zeros((tm, D), F32)

        def add(ref):
            acc[...] += lax.dot_general(ref[0], w_ref[0], NT_DIMS, preferred_element_type=F32)

        pl.when(j < 2)(lambda: add(dza_ref))
        pl.when((j >= 2) & (j < 6))(lambda: add(dzb_ref))
        pl.when(j >= 6)(lambda: add(dzm_ref))

        @pl.when(j == NB - 1)
        def _():
            x = x_ref[...]
            r = lax.rsqrt(jnp.mean(x * x, axis=-1, keepdims=True) + EPS)
            xn = x * r
            dh = acc[...]
            gg_ref[...] += jnp.sum(dh * xn, axis=0, keepdims=True)
            dxn = dh * g_ref[...]
            gx_ref[...] = dx2_ref[...] + r * (dxn - xn * jnp.mean(dxn * xn, axis=-1, keepdims=True))

    rows = pl.BlockSpec((tm, D), lambda i, j: (i, 0))
    return pl.pallas_call(
        body, name="inproj_bwd_x", grid=(ni, NB),
        in_specs=_dz_specs(tm, ni, True) + [pl.BlockSpec((1, D, D), lambda i, j: (j, 0, 0)), rows, rows,
                                             pl.BlockSpec((1, D), lambda i, j: (0, 0)), ANY],
        out_specs=[rows, pl.BlockSpec((1, D), lambda i, j: (0, 0))],
        out_shape=[SDS((n, D), F32), SDS((1, D), F32)],
        scratch_shapes=[pltpu.VMEM((tm, D), F32)],
        compiler_params=_params(48),
    )(dza, dzb, dzm, w_all, x2, dx2, norm_g, after)


def _walk_tables(order, ni):
    rows = []
    for lo, hi in ((0, 2), (2, 6), (6, 8)):
        active = [j for j, g in enumerate(order) if lo <= g < hi]
        block, tile = [], []
        for j, g in enumerate(order):
            before = [a for a in active if a < j]
            if lo <= g < hi:
                block.append(g - lo), tile.append(-1)
            elif before:
                block.append(order[before[-1]] - lo), tile.append(ni - 1)
            else:
                block.append(order[active[0]] - lo), tile.append(0)
        rows += [block, tile]
    return rows


def _inproj_bwd_w(core, dza, dzb, dzm, h_all, g_m):
    n = h_all.shape[0]
    tm = min(n, 2048)
    ni = n // tm
    packed = g_m.shape[1:]
    orders = [[2 * q + 1 - c for q in range(4)] + [2 * q + c for q in range(4)] for c in (0, 1)]
    tables = jnp.asarray([[order] + _walk_tables(order, ni) for order in orders], jnp.int32)
    walk = jnp.where(core == 0, tables[0], tables[1])

    def body(walk_ref, dza_ref, dzb_ref, dzm_ref, h_ref, gm_hbm, out_bf, own_f32, m_out_bf, m_own_f32, got_w, got_m,
             acc, stage, theirs, m_mine, m_theirs, m_stage, send_sems, recv_sems, local_sems):
        j, i = pl.program_id(0), pl.program_id(1)
        group = walk_ref[0, j]
        x, y, c = _place()
        sibling = (x, y, 1 - c)

        def send_w(q):
            return pltpu.make_async_remote_copy(
                src_ref=stage.at[q % 2], dst_ref=got_w.at[q], send_sem=send_sems.at[q], recv_sem=recv_sems.at[q],
                device_id=sibling, device_id_type=MESH)

        def send_m(q):
            return pltpu.make_async_remote_copy(
                src_ref=gm_hbm.at[2 * q + (1 - c)], dst_ref=got_m.at[q], send_sem=send_sems.at[4 + q],
                recv_sem=recv_sems.at[4 + q], device_id=sibling, device_id_type=MESH)

        def fetch(q):
            return pltpu.make_async_copy(got_w.at[q], theirs, local_sems.at[0])

        def fetch_m(q):
            return (pltpu.make_async_copy(gm_hbm.at[2 * q + c], m_mine, local_sems.at[2]),
                    pltpu.make_async_copy(got_m.at[q], m_theirs, local_sems.at[3]))

        @pl.when((j == 0) & (i == 0))
        def _():
            for q in range(4):
                send_m(q).start()

        @pl.when(i == 0)
        def _():
            acc[...] = jnp.zeros((D, D), F32)

        def add(ref):
            acc[...] += lax.dot_general(h_ref[...], ref[0], TN_DIMS, preferred_element_type=F32)

        pl.when(group < 2)(lambda: add(dza_ref))
        pl.when((group >= 2) & (group < 6))(lambda: add(dzb_ref))
        pl.when(group >= 6)(lambda: add(dzm_ref))

        for q in range(4):
            @pl.when((i == ni - 1) & (j == q))
            def _(q=q):
                if q >= 2:
                    send_w(q - 2).wait_send()
                stage[q % 2] = acc[...].astype(BF16)
                send_w(q).start()

        for q in range(4):
            @pl.when((j == 4 + q) & (i == 0))
            def _(q=q):
                send_w(q).wait_recv()
                send_m(q).wait_recv()
                fetch(q).start()
                for cp in fetch_m(q):
                    cp.start()

            @pl.when((j == 4 + q) & (i == ni - 1))
            def _(q=q):
                if q == 0:
                    send_w(2).wait_send()
                    send_w(3).wait_send()
                other_x, other_y = x != q // 2, y != q % 2
                other = other_x | other_y
                slot = jnp.where(other_x & other_y, 2, jnp.where(other_x, 0, 1))
                m_out = pltpu.make_async_copy(m_stage, m_out_bf.at[slot], local_sems.at[4])
                m_own = pltpu.make_async_copy(m_mine, m_own_f32, local_sems.at[4])

                for cp in fetch_m(q):
                    cp.wait()
                total_m = m_mine[...] + m_theirs[...]

                @pl.when(other)
                def _():
                    m_stage[...] = total_m.astype(BF16)
                    m_out.start()

                @pl.when(jnp.logical_not(other))
                def _():
                    m_mine[...] = total_m
                    m_own.start()

                fetch(q).wait()
                total = acc[...] + theirs[...].astype(F32)

                @pl.when(other)
                def _():
                    stage[0] = total.astype(BF16)
                    out = pltpu.make_async_copy(stage.at[0], out_bf.at[slot], local_sems.at[1])
                    out.start()
                    out.wait()
                    m_out.wait()

                @pl.when(jnp.logical_not(other))
                def _():
                    acc[...] = total
                    out = pltpu.make_async_copy(acc, own_f32, local_sems.at[1])
                    out.start()
                    out.wait()
                    m_own.wait()

        @pl.when((j == NB - 1) & (i == ni - 1))
        def _():
            for q in range(4):
                send_m(q).wait_send()

    def dz_spec(k):
        return pl.BlockSpec((1, tm, D), lambda j, i, w: (w[1 + 2 * k, j], jnp.where(w[2 + 2 * k, j] < 0, i, w[2 + 2 * k, j]), 0))

    return pl.pallas_call(
        body, name="inproj_bwd_w",
        grid_spec=pltpu.PrefetchScalarGridSpec(
            num_scalar_prefetch=1, grid=(NB, ni),
            in_specs=[dz_spec(0), dz_spec(1), dz_spec(2), pl.BlockSpec((tm, D), lambda j, i, w: (i, 0)), ANY],
            out_specs=[ANY] * 6,
            scratch_shapes=[pltpu.VMEM((D, D), F32), pltpu.VMEM((2, D, D), BF16), pltpu.VMEM((D, D), BF16),
                            pltpu.VMEM(packed, F32), pltpu.VMEM(packed, F32), pltpu.VMEM(packed, BF16),
                            pltpu.SemaphoreType.DMA((8,)), pltpu.SemaphoreType.DMA((8,)), pltpu.SemaphoreType.DMA((5,))]),
        out_shape=[SDS((3, D, D), BF16), SDS((D, D), F32), SDS((3,) + packed, BF16), SDS(packed, F32),
                   SDS((4, D, D), BF16), SDS((4,) + packed, F32)],
        compiler_params=_params(58),
    )(walk, dza, dzb, dzm, h_all, g_m)


def _adamw(w, g, m, v):
    rows, cols = w.shape
    tr = _row_tile(rows)

    spec = pl.BlockSpec((tr, cols), lambda i: (i, 0))
    return pl.pallas_call(
        functools.partial(_adam_refs), name="adamw", grid=(rows // tr,), in_specs=[spec] * 4, out_specs=[spec] * 3,
        out_shape=[SDS((rows, cols), F32)] * 3, compiler_params=_params(32),
    )(w, g, m, v)


def _adam_refs(w_ref, g_ref, m_ref, v_ref, d_ref, nm_ref, nv_ref):
    gv = g_ref[...]
    nm = ADAM_B1 * m_ref[...] + (1.0 - ADAM_B1) * gv
    nv = ADAM_B2 * v_ref[...] + (1.0 - ADAM_B2) * (gv * gv)
    m_hat = nm / (1.0 - ADAM_B1 ** ADAM_STEP)
    v_hat = nv / (1.0 - ADAM_B2 ** ADAM_STEP)
    d_ref[...] = -ADAM_LR * (m_hat / (jnp.sqrt(v_hat) + ADAM_EPS) + ADAM_WD * w_ref[...])
    nm_ref[...] = nm
    nv_ref[...] = nv


def _adamw_small(ws, gs, ms, vs):
    k = len(ws)

    def body(*refs):
        ins, outs = refs[:4 * k], refs[4 * k:7 * k]
        vin, vout = refs[7 * k:11 * k], refs[11 * k:14 * k]
        load_sems, store_sems = refs[14 * k:]
        loads = [pltpu.make_async_copy(ins[i], vin[i], load_sems.at[i]) for i in range(4 * k)]
        for cp in loads:
            cp.start()
        for cp in loads:
            cp.wait()
        for i in range(k):
            _adam_refs(*[vin[part * k + i] for part in range(4)], *[vout[part * k + i] for part in range(3)])
        stores = [pltpu.make_async_copy(vout[i], outs[i], store_sems.at[i]) for i in range(3 * k)]
        for cp in stores:
            cp.start()
        for cp in stores:
            cp.wait()

    shapes = [SDS(w.shape, F32) for w in ws]
    vmem = [pltpu.VMEM(w.shape, F32) for w in ws]
    out = pl.pallas_call(
        body, name="adamw_small", out_shape=shapes * 3, in_specs=[HBM] * (4 * k), out_specs=[HBM] * (3 * k),
        scratch_shapes=vmem * 7 + [pltpu.SemaphoreType.DMA((4 * k,)), pltpu.SemaphoreType.DMA((3 * k,))],
        compiler_params=_params(32),
    )(*ws, *gs, *ms, *vs)
    return out[:k], out[k:2 * k], out[2 * k:]


def _allgather(blocks, dtypes, name):
    na = len(blocks)

    def body(*refs):
        ins, outs, stages = refs[:na], refs[na:2 * na], refs[2 * na:3 * na]
        send_sems, recv_sems, local_sems = refs[3 * na:]
        x, y, c = _place()
        me, sibling = (x, y, c), (x, y, 1 - c)
        chips = [(1 - x, y), (x, 1 - y), (1 - x, 1 - y)]
        blk = lambda p: 4 * p[0] + 2 * p[1] + p[2]

        def copy(a, k, block, to, src=None):
            return pltpu.make_async_remote_copy(
                src_ref=outs[a].at[blk(block)] if src is None else src, dst_ref=outs[a].at[blk(block)],
                send_sem=send_sems.at[7 * a + k], recv_sem=recv_sems.at[7 * a + k],
                device_id=to, device_id_type=MESH)

        mine, first, passed = [], [], []
        for a in range(na):
            stages[a][...] = ins[a][...].astype(dtypes[a])
            mine.append(pltpu.make_async_copy(stages[a], outs[a].at[blk(me)], local_sems.at[a]))
            mine[-1].start()
            first.append(copy(a, 0, me, sibling, src=stages[a]))
            first += [copy(a, 1 + j, me, (*chip, c), src=stages[a]) for j, chip in enumerate(chips)]
        for cp in first:
            cp.start()
        for j, chip in enumerate(chips):
            for a in range(na):
                copy(a, 1 + j, (*chip, c), me).wait_recv()
                passed.append(copy(a, 4 + j, (*chip, c), sibling))
                passed[-1].start()
        for a in range(na):
            copy(a, 0, sibling, me).wait_recv()
            for j, chip in enumerate(chips):
                copy(a, 4 + j, (*chip, 1 - c), me).wait_recv()
        for cp in first + passed:
            cp.wait_send()
        for cp in mine:
            cp.wait()

    return pl.pallas_call(
        body, name=name,
        in_specs=[pl.BlockSpec(memory_space=pltpu.VMEM)] * na, out_specs=[ANY] * na,
        out_shape=[SDS((NB,) + b.shape, dt) for b, dt in zip(blocks, dtypes)],
        scratch_shapes=[pltpu.VMEM(b.shape, dt) for b, dt in zip(blocks, dtypes)]
        + [pltpu.SemaphoreType.DMA((7 * na,)), pltpu.SemaphoreType.DMA((7 * na,)), pltpu.SemaphoreType.DMA((na,))],
        compiler_params=_params(40),
    )(*blocks)


HBM = pl.BlockSpec(memory_space=pltpu.HBM)
SEMS = pl.BlockSpec(memory_space=pltpu.SEMAPHORE)
EFFECT = pltpu.SideEffectType.DATAFLOW_SIDE_EFFECTING


def _chip_copies(srcs, lands, send_sems, recv_sems):
    x, y, c = _place()
    return [pltpu.make_async_remote_copy(
        src_ref=srcs[a].at[slot], dst_ref=lands[a].at[slot],
        send_sem=send_sems.at[3 * a + slot], recv_sem=recv_sems.at[3 * a + slot],
        device_id=(px, py, c), device_id_type=MESH)
        for a in range(len(srcs)) for slot, (px, py) in enumerate(_other_chips(x, y))]


def _split_start(name, copies, per_array, srcs, lands, after=None):
    na = len(srcs)

    def body(*refs):
        send_sems, recv_sems = refs[-2 * na - 3], refs[-2 * na - 2]
        for cp in copies(refs[:na], refs[na:2 * na], send_sems, recv_sems):
            cp.start()
        refs[-1][...] = jnp.zeros_like(refs[-1])

    hbm = lambda a: pltpu.HBM(a.shape, a.dtype)
    pin = lambda a: pltpu.with_memory_space_constraint(a, pltpu.HBM)
    out = pl.pallas_call(
        body, name=name,
        out_shape=(pltpu.SemaphoreType.DMA((per_array * na,)), pltpu.SemaphoreType.DMA((per_array * na,)),
                   *[hbm(a) for a in srcs], *[hbm(a) for a in lands], SDS((8, BD), F32)),
        in_specs=[HBM] * (2 * na) + ([] if after is None else [ANY]),
        out_specs=(SEMS, SEMS, *[HBM] * (2 * na), pl.BlockSpec(memory_space=pltpu.VMEM)),
        input_output_aliases={i: 2 + i for i in range(2 * na)},
        compiler_params=pltpu.CompilerParams(has_side_effects=EFFECT),
    )(*[pin(a) for a in srcs], *[pin(a) for a in lands], *([] if after is None else [after]))
    return out[0], out[1], out[2:2 + na], out[2 + na:2 + 2 * na], out[-1]


def _split_wait(name, copies, started, after):
    send_sems, recv_sems, srcs, lands, _ = started
    na = len(srcs)

    def body(*refs):
        waits = copies(refs[:na], refs[na:2 * na], refs[2 * na], refs[2 * na + 1])
        for cp in waits:
            cp.wait_send()
        for cp in waits:
            cp.wait_recv()

    hbm = lambda a: pltpu.HBM(a.shape, a.dtype)
    out = pl.pallas_call(
        body, name=name,
        out_shape=(*[hbm(a) for a in srcs], *[hbm(a) for a in lands]),
        in_specs=[HBM] * (2 * na) + [SEMS, SEMS, ANY],
        out_specs=tuple([HBM] * (2 * na)),
        input_output_aliases={i: i for i in range(2 * na)},
        compiler_params=pltpu.CompilerParams(has_side_effects=EFFECT),
    )(*srcs, *lands, send_sems, recv_sems, after)
    return out[na:]


def _add_chips(own, b_in):
    r, cols = own.shape
    tr = _row_tile(r)

    def body(p_ref, b0_ref, b1_ref, b2_ref, o_ref):
        o_ref[...] = ((p_ref[...] + b0_ref[0].astype(F32)) + b1_ref[0].astype(F32)) + b2_ref[0].astype(F32)

    slot = lambda k: pl.BlockSpec((1, tr, cols), lambda i: (k, i, 0))
    spec = pl.BlockSpec((tr, cols), lambda i: (i, 0))
    return pl.pallas_call(
        body, name="add_chips", grid=(r // tr,), in_specs=[spec, slot(0), slot(1), slot(2)], out_specs=spec,
        out_shape=SDS((r, cols), F32), compiler_params=_params(32),
    )(own, b_in, b_in, b_in)


VEC_NAMES = ("b_merge", "conv_b", "rg_bx", "rg_ba", "rg_lambda", "hg_lb_logits", "hg_norm_g", "final_norm_g")
REP_NAMES = ("rg_wx", "rg_wa", "norm_g") + VEC_NAMES
SMALL_AT = 3 * BD
SMALL_ROWS = 48
MID_ROWS = 448


def _sum_blocks(parts):
    def body(p_ref, o_ref):
        acc = p_ref[0]
        for k in range(1, NB):
            acc = acc + p_ref[k]
        o_ref[...] = acc

    return pl.pallas_call(body, name="sum_blocks", out_shape=SDS(parts.shape[1:], F32))(parts)


def _pack_rows(arrays, width, row_multiple=8):
    flat = jnp.concatenate([a.reshape(-1) for a in arrays])
    rows = -(-flat.shape[0] // width)
    rows = -(-rows // row_multiple) * row_multiple
    return jnp.pad(flat, (0, rows * width - flat.shape[0])).reshape(rows, width)


def _unpack(flat, like):
    out, off = [], 0
    for a in like:
        out.append(flat[off:off + a.size].reshape(a.shape))
        off += a.size
    return out


def kernel(x, w_in, b_merge, conv_w, conv_b, rg_wx, rg_bx, rg_wa, rg_ba, rg_lambda, hg_lb_logits, hg_norm_g, proj_a, proj_b, w_out, norm_g, final_norm_g, loss_target, m_w_in, m_b_merge, m_conv_w, m_conv_b, m_rg_wx, m_rg_bx, m_rg_wa, m_rg_ba, m_rg_lambda, m_hg_lb_logits, m_hg_norm_g, m_proj_a, m_proj_b, m_w_out, m_norm_g, m_final_norm_g, v_w_in, v_b_merge, v_conv_w, v_conv_b, v_rg_wx, v_rg_bx, v_rg_wa, v_rg_ba, v_rg_lambda, v_hg_lb_logits, v_hg_norm_g, v_proj_a, v_proj_b, v_w_out, v_norm_g, v_final_norm_g):
    weights = dict(w_in=w_in, b_merge=b_merge, conv_w=conv_w, conv_b=conv_b, rg_wx=rg_wx, rg_bx=rg_bx, rg_wa=rg_wa,
                   rg_ba=rg_ba, rg_lambda=rg_lambda, hg_lb_logits=hg_lb_logits, hg_norm_g=hg_norm_g, proj_a=proj_a,
                   proj_b=proj_b, w_out=w_out, norm_g=norm_g, final_norm_g=final_norm_g)
    mom1 = dict(w_in=m_w_in, b_merge=m_b_merge, conv_w=m_conv_w, conv_b=m_conv_b, rg_wx=m_rg_wx, rg_bx=m_rg_bx,
                rg_wa=m_rg_wa, rg_ba=m_rg_ba, rg_lambda=m_rg_lambda, hg_lb_logits=m_hg_lb_logits,
                hg_norm_g=m_hg_norm_g, proj_a=m_proj_a, proj_b=m_proj_b, w_out=m_w_out, norm_g=m_norm_g,
                final_norm_g=m_final_norm_g)
    mom2 = dict(w_in=v_w_in, b_merge=v_b_merge, conv_w=v_conv_w, conv_b=v_conv_b, rg_wx=v_rg_wx, rg_bx=v_rg_bx,
                rg_wa=v_rg_wa, rg_ba=v_rg_ba, rg_lambda=v_rg_lambda, hg_lb_logits=v_hg_lb_logits,
                hg_norm_g=v_hg_norm_g, proj_a=v_proj_a, proj_b=v_proj_b, w_out=v_w_out, norm_g=v_norm_g,
                final_norm_g=v_final_norm_g)
    order = list(weights)
    nb, s_len, _ = x.shape
    n = nb * s_len
    px, py, pc = _place()

    in_hbm = lambda a: pltpu.with_memory_space_constraint(a, pltpu.HBM)
    norm_gain = in_hbm(norm_g)

    x2 = x.reshape(n, D)
    cw_blk = jnp.pad(conv_w[0], ((0, 4), (0, 0)))
    order_ids = jnp.stack([_block_id(p) for p in _arrival_order(px, py, pc)]).astype(jnp.int32)
    z, h_all, w_all, pa_all, pb_all, wo_all, cw_all = _gather_inproj(
        order_ids, x2, norm_gain, [w_in[0], proj_a[0], proj_b[0], w_out[0], cw_blk], [BF16, BF16, BF16, BF16, F32])
    pa_full, pb_full, wo_full = (a.reshape(D, D) for a in (pa_all, pb_all, wo_all))
    cw8 = in_hbm(cw_all.transpose(1, 0, 2).reshape(8, D))
    wx_b, wa_b = in_hbm(rg_wx[0].astype(BF16)), in_hbm(rg_wa[0].astype(BF16))
    cb, bx, ba, lam = (in_hbm(a.reshape(1, D)) for a in (conv_b, rg_bx, rg_ba, rg_lambda))
    fin_g, b_mrg = in_hbm(final_norm_g.reshape(1, D)), in_hbm(b_merge)
    lb_lg, hg_g = in_hbm(hg_lb_logits), in_hbm(hg_norm_g)

    hlru, ya = _lru_fwd(z, cw8, cb, wx_b, wa_b, bx, ba, lam, nb, s_len)
    o_all, yb, st_all = _hgrn_fwd(z, lb_lg, hg_g, nb, s_len)

    (dx2, dya, dyb, dzm, loss_acc, g_fin, g_bm, g_mid) = _mid(
        ya, yb, z, b_mrg, x2, loss_target.reshape(n, D), fin_g, pa_full, pb_full, wo_full)
    dzb, g_lg, g_hg = _hgrn_bwd(z, o_all, st_all, dyb, lb_lg, hg_g, nb, s_len)
    dza, g_cw8, g_cb, g_wx, g_wa, g_bx, g_ba, g_lam = _lru_bwd(
        z, hlru, dya, cw8, cb, wx_b, wa_b, bx, ba, lam, nb, s_len)

    part = dict(b_merge=g_bm, conv_b=g_cb, rg_bx=g_bx, rg_ba=g_ba, rg_lambda=g_lam, hg_lb_logits=g_lg,
                hg_norm_g=g_hg, final_norm_g=g_fin)
    vec = _pack_rows([part[k] for k in VEC_NAMES], BD)
    vec = jnp.pad(vec, ((0, 16 * NB - vec.shape[0]), (0, 0))).reshape(NB, 2, D)
    rows8 = lambda a: jnp.pad(a, ((0, 0), (0, 8 - a.shape[1]), (0, 0)))
    small = jnp.concatenate([g_wx.reshape(NB, 16, D), g_wa.reshape(NB, 16, D),
                             rows8(g_cw8.reshape(8, NB, BD).transpose(1, 0, 2).reshape(NB, 1, D)), rows8(vec),
                             jnp.zeros((NB, MID_ROWS - SMALL_AT - SMALL_ROWS, D), F32)], axis=1)
    g_m = lax.dynamic_update_slice(g_mid, small, (0, SMALL_AT, 0))
    w_out_bf, w_own, m_out_bf, m_own, _, _ = _inproj_bwd_w(pc, dza, dzb, dzm, h_all, g_m)
    outgoing = [w_out_bf, m_out_bf]
    chip_sums = _split_start("rs_chips_start", _chip_copies, 3, outgoing, [lax.empty(a.shape, a.dtype) for a in outgoing])
    grad_x, g_ng = _inproj_bwd_x(dza, dzb, dzm, w_all, x2, dx2, norm_gain, chip_sums[-1])
    from_chips = _split_wait("rs_chips_wait", _chip_copies, chip_sums, grad_x)
    r_w = _add_chips(w_own, from_chips[0])
    r_m = _add_chips(m_own, from_chips[1])
    row = lax.broadcasted_iota(jnp.int32, (8, D), 0)
    mine = jnp.where(row == 0, g_ng, jnp.where(row == 1, loss_acc[0:1, 0:1], 0.0))
    tail = jnp.concatenate([r_m[SMALL_AT:SMALL_AT + SMALL_ROWS], mine], axis=0)
    (tail_all,) = _allgather([tail], [F32], "gather_small_grads")
    summed = _sum_blocks(tail_all[:, SMALL_ROWS:SMALL_ROWS + 8])

    grads = dict(w_in=r_w.reshape(1, D, D),
                 proj_a=r_m[0:BD].reshape(1, BD, D), proj_b=r_m[BD:2 * BD].reshape(1, BD, D),
                 w_out=r_m[2 * BD:3 * BD].reshape(1, BD, D),
                 conv_w=r_m[SMALL_AT + 32].reshape(8, BD)[0:4].reshape(1, 4, BD),
                 rg_wx=tail_all[:, 0:16].reshape(1, NB, BD, BD), rg_wa=tail_all[:, 16:32].reshape(1, NB, BD, BD),
                 norm_g=summed[0:1])
    vec_all = tail_all[:, 40:42].reshape(-1)
    for k, gk in zip(VEC_NAMES, _unpack(vec_all, [weights[k] for k in VEC_NAMES])):
        grads[k] = gk

    delta, new_m, new_v = {}, {}, {}
    flat2 = lambda a: a.reshape(-1, a.shape[-1])
    for k in ("w_in", "proj_a", "proj_b", "w_out"):
        outs = _adamw(*[flat2(t[k]) for t in (weights, grads, mom1, mom2)])
        delta[k], new_m[k], new_v[k] = (a.reshape(weights[k].shape) for a in outs)
    rep = list(REP_NAMES) + ["conv_w"]
    outs = _adamw_small(*[[flat2(t[k]) for k in rep] for t in (weights, grads, mom1, mom2)])
    for tgt, arrays in zip((delta, new_m, new_v), outs):
        for k, a in zip(rep, arrays):
            tgt[k] = a.reshape(weights[k].shape)

    return (summed[1, 0], grad_x.reshape(x.shape), *[grads[k] for k in order], *[delta[k] for k in order],
            *[new_m[k] for k in order], *[new_v[k] for k in order])
```

```python
import functools

import jax
import jax.numpy as jnp
from jax import lax
from jax.experimental import pallas as pl
from jax.experimental.pallas import tpu as pltpu

F32 = jnp.float32
BF16 = jnp.bfloat16
SDS = jax.ShapeDtypeStruct
MESH = pl.DeviceIdType.MESH
ANY = pl.BlockSpec(memory_space=pl.ANY)

D = 1024
NB = 8
BD = D // NB
CHUNK = 64
EPS = 1e-6
LRU_C = 8.0
HG_SCALE = BD ** -0.5
ADAM_LR, ADAM_B1, ADAM_B2, ADAM_EPS, ADAM_WD, ADAM_STEP = 0.001, 0.9, 0.999, 1e-08, 0.01, 10

NT_DIMS = (((1,), (1,)), ((), ()))
TN_DIMS = (((0,), (0,)), ((), ()))


def _params(vmem_mib):
    return pltpu.CompilerParams(vmem_limit_bytes=vmem_mib << 20)


def _row_tile(rows, most=256):
    assert rows % 8 == 0
    return max(t for t in range(8, min(rows, most) + 1, 8) if rows % t == 0)


def _sigmoid(v):
    return 0.5 * (jnp.tanh(0.5 * v) + 1.0)


def _groups(v):
    return v.reshape(v.shape[0] // 8, 8, v.shape[1])


def _softplus_neg(lam):
    t = -lam
    e = jnp.exp(-jnp.abs(t))
    w = 1.0 + e
    d = w - 1.0
    l1p = jnp.where(d == 0.0, e, jnp.log(w) * (e / jnp.where(d == 0.0, 1.0, d)))
    return jnp.maximum(t, 0.0) + l1p


def _place():
    return lax.axis_index("x"), lax.axis_index("y"), lax.axis_index("c")


def _other_chips(x, y):
    return [(1 - x, y), (x, 1 - y), (1 - x, 1 - y)]


def _block_id(p):
    return 4 * p[0] + 2 * p[1] + p[2]


def _core_chips(x, y, c):
    near, far, diag = _other_chips(x, y)
    pick = lambda a, b: (jnp.where(c == 0, a[0], b[0]), jnp.where(c == 0, a[1], b[1]))
    return [pick(near, far), pick(far, near), diag]


def _arrival_order(x, y, c):
    first, second, diag = _core_chips(x, y, c)
    return [(x, y, c), (x, y, 1 - c), (*first, c), (*second, 1 - c), (*second, c), (*first, 1 - c),
            (*diag, c), (*diag, 1 - c)]


def _gather_inproj(order_ids, x2, norm_g, blocks, dtypes):
    na = len(blocks)
    n = x2.shape[0]
    tm = min(n, 1024)
    ni = n // tm

    def body(order_ref, x_ref, g_ref, *refs):
        ins, (z_ref, h_ref), outs = refs[:na], refs[na:na + 2], refs[na + 2:2 * na + 2]
        stages = refs[2 * na + 2:3 * na + 2]
        h_full, wbuf, send_sems, recv_sems, local_sems, wsems, hsem = refs[3 * na + 2:]
        j, i = pl.program_id(0), pl.program_id(1)
        x, y, c = _place()
        me, sibling = (x, y, c), (x, y, 1 - c)
        chips = _core_chips(x, y, c)
        sibling_chips = [chips[1], chips[0], chips[2]]
        small = range(1, na)

        def copy(a, k, block, to, src=None):
            return pltpu.make_async_remote_copy(
                src_ref=outs[a].at[_block_id(block)] if src is None else src, dst_ref=outs[a].at[_block_id(block)],
                send_sem=send_sems.at[7 * a + k], recv_sem=recv_sems.at[7 * a + k],
                device_id=to, device_id_type=MESH)

        def local(a):
            return pltpu.make_async_copy(stages[a], outs[a].at[_block_id(me)], local_sems.at[a])

        def landed(a, slot):
            copy(a, 1 + slot, (*chips[slot], c), me).wait_recv()
            copy(a, 4 + slot, (*chips[slot], c), sibling).start()
            if slot == 0:
                copy(a, 3, (*chips[0], c), (*chips[1], c)).start()

        def diagonal_and_small():
            landed(0, 2)
            for a in small:
                landed(a, 0)
                landed(a, 1)

        def passed_on(a, slot):
            copy(a, 4 + slot, (*sibling_chips[slot], 1 - c), me).wait_recv()

        def sibling_here_send_second():
            copy(0, 0, sibling, me).wait_recv()
            for a in range(na):
                copy(a, 2, me, (*chips[1], c), src=stages[a]).start()

        @pl.when((j == 0) & (i == 0))
        def _():
            for a in range(na):
                stages[a][...] = ins[a][...].astype(dtypes[a])
                local(a).start()
            for a in range(na):
                copy(a, 0, me, sibling, src=stages[a]).start()
                copy(a, 1, me, (*chips[0], c), src=stages[a]).start()

        @pl.when(j == 0)
        def _():
            xv = x_ref[...]
            r = lax.rsqrt(jnp.mean(xv * xv, axis=-1, keepdims=True) + EPS)
            hb = ((xv * r) * g_ref[...]).astype(BF16)
            h_full[pl.ds(pl.multiple_of(i * tm, tm), tm), :] = hb

        save_h = pltpu.make_async_copy(h_full, h_ref, hsem)
        pl.when((j == 0) & (i == ni - 1))(save_h.start)

        steps = [
            lambda: local(0).wait(),
            sibling_here_send_second,
            lambda: landed(0, 0),
            lambda: passed_on(0, 0),
            lambda: landed(0, 1),
            lambda: passed_on(0, 1),
            diagonal_and_small,
            lambda: passed_on(0, 2),
        ]
        def w_load(k):
            return pltpu.make_async_copy(outs[0].at[order_ref[k]], wbuf.at[k % 2], wsems.at[k % 2])

        for k, step in enumerate(steps):
            @pl.when((j == 0) & (i == 0) if k == 0 else (j == k - 1) & (i == ni - 1))
            def _(k=k, step=step):
                step()
                w_load(k).start()

        pl.when(i == 0)(lambda: w_load(j).wait())
        z_ref[0] = jnp.dot(h_full[pl.ds(pl.multiple_of(i * tm, tm), tm), :], wbuf[j % 2], preferred_element_type=F32)

        @pl.when((j == NB - 1) & (i == ni - 1))
        def _():
            save_h.wait()
            for a in small:
                landed(a, 2)
            for a in small:
                local(a).wait()
                copy(a, 0, sibling, me).wait_recv()
                for slot in range(3):
                    passed_on(a, slot)
            for a in range(na):
                copy(a, 0, me, sibling, src=stages[a]).wait_send()
                for slot, chip in enumerate(chips):
                    copy(a, 1 + slot, me, (*chip, c), src=stages[a]).wait_send()
                    copy(a, 4 + slot, (*chip, c), sibling).wait_send()

    rows_once = lambda j, i, order: (jnp.where(j == 0, i, ni - 1), 0)
    vmem = pl.BlockSpec(memory_space=pltpu.VMEM)
    return pl.pallas_call(
        body, name="gather_inproj",
        grid_spec=pltpu.PrefetchScalarGridSpec(
            num_scalar_prefetch=1, grid=(NB, ni),
            in_specs=[pl.BlockSpec((tm, D), rows_once), pl.BlockSpec((1, D), lambda j, i, order: (0, 0))] + [vmem] * na,
            out_specs=[pl.BlockSpec((1, tm, D), lambda j, i, order: (order[j], i, 0)), ANY] + [ANY] * na,
            scratch_shapes=[pltpu.VMEM(b.shape, dt) for b, dt in zip(blocks, dtypes)]
            + [pltpu.VMEM((n, D), BF16), pltpu.VMEM((2, D, D), BF16),
               pltpu.SemaphoreType.DMA((7 * na,)), pltpu.SemaphoreType.DMA((7 * na,)),
               pltpu.SemaphoreType.DMA((na,)), pltpu.SemaphoreType.DMA((2,)), pltpu.SemaphoreType.DMA(())]),
        out_shape=[SDS((NB, n, D), F32), SDS((n, D), BF16)] + [SDS((NB,) + b.shape, dt) for b, dt in zip(blocks, dtypes)],
        compiler_params=_params(56),
    )(order_ids, x2, norm_g, *blocks)


LRU_T = 256


def _shifted(groups, shifts):
    row = lax.broadcasted_iota(jnp.int32, (groups.shape[0] - 1,) + groups.shape[1:], 1)
    out = []
    for s in shifts:
        y = pltpu.roll(groups, s % 8, 1)
        moved = jnp.where(row >= s, y[1:], y[:-1]) if s > 0 else jnp.where(row < 8 + s, y[:-1], y[1:])
        out.append(moved.reshape(-1, groups.shape[2]))
    return out


def _conv(taps, cw, cb):
    acc = taps[0] * cw[0:1, :] + taps[1] * cw[1:2, :]
    acc = acc + taps[2] * cw[2:3, :]
    acc = acc + taps[3] * cw[3:4, :]
    return cb + acc


def _lru_gates(xa, wx_ref, wa_ref, bx, ba, lam):
    xab = xa.astype(BF16)
    pis, prs = [], []
    for h in range(NB):
        xs = xab[:, h * BD:(h + 1) * BD]
        pis.append(jnp.dot(xs, wx_ref[h], preferred_element_type=F32))
        prs.append(jnp.dot(xs, wa_ref[h], preferred_element_type=F32))
    gi = _sigmoid(jnp.concatenate(pis, axis=1) + bx)
    gr = _sigmoid(jnp.concatenate(prs, axis=1) + ba)
    sp = _softplus_neg(lam)
    log_a = (-LRU_C * gr) * sp
    a = jnp.exp(log_a)
    mult = jnp.sqrt(-jnp.tanh(log_a) * (a * a + 1.0))
    return xab, gi, gr, sp, a, mult


def _lru_fwd(z, cw8, cb, wx, wa, bx, ba, lam, nb, s_len):
    n = nb * s_len
    t = LRU_T
    ns = s_len // t

    def body(xp_ref, ga_ref, cw_ref, cb_ref, wx_ref, wa_ref, bx_ref, ba_ref, lam_ref,
             h_ref, ya_ref, ext, a_s, u_s, carry):
        @pl.when(pl.program_id(1) == 0)
        def _():
            ext[0:8, :] = jnp.zeros((8, D), F32)
            carry[...] = jnp.zeros((8, D), F32)

        xp = xp_ref[0]
        ext[8:8 + t, :] = xp
        xa = _conv(_shifted(_groups(ext[...]), (3, 2, 1)) + [xp], cw_ref[...], cb_ref[...])
        ext[0:8, :] = xp[t - 8:t, :]
        _, gi, _, _, a, mult = _lru_gates(xa, wx_ref, wa_ref, bx_ref[...], ba_ref[...], lam_ref[...])
        u = (mult * gi) * xa
        a, u = _groups(a), _groups(u)
        row = lax.broadcasted_iota(jnp.int32, a.shape, 1)
        for sh in (1, 2, 4):
            a_sh = pltpu.roll(a, sh, 1)
            u_sh = pltpu.roll(u, sh, 1)
            m = row >= sh
            u = jnp.where(m, a * u_sh + u, u)
            a = jnp.where(m, a * a_sh, a)
        a_s[...] = a.reshape(t, D)
        u_s[...] = u.reshape(t, D)

        def step(g, c):
            r = pl.multiple_of(g * 8, 8)
            hg = u_s[pl.ds(r, 8), :] + a_s[pl.ds(r, 8), :] * c
            h_ref[pl.ds(r, 8), :] = hg
            return hg[7:8, :]

        c_out = lax.fori_loop(0, t // 8, step, carry[0:1, :], unroll=4)
        carry[0:1, :] = c_out
        ga = ga_ref[0]
        ya_ref[...] = (h_ref[...] * (ga * _sigmoid(ga))).astype(BF16)

    row_map = lambda b, s: (b * ns + s, 0)
    rep2 = lambda b, s: (0, 0)
    rep3 = lambda b, s: (0, 0, 0)
    return pl.pallas_call(
        body, name="lru_fwd", grid=(nb, ns),
        in_specs=[pl.BlockSpec((1, t, D), lambda b, s: (0, b * ns + s, 0)),
                  pl.BlockSpec((1, t, D), lambda b, s: (1, b * ns + s, 0)),
                  pl.BlockSpec((8, D), rep2), pl.BlockSpec((1, D), rep2),
                  pl.BlockSpec((NB, BD, BD), rep3), pl.BlockSpec((NB, BD, BD), rep3),
                  pl.BlockSpec((1, D), rep2), pl.BlockSpec((1, D), rep2), pl.BlockSpec((1, D), rep2)],
        out_specs=[pl.BlockSpec((t, D), row_map), pl.BlockSpec((t, D), row_map)],
        out_shape=[SDS((n, D), F32), SDS((n, D), BF16)],
        scratch_shapes=[pltpu.VMEM((t + 8, D), F32), pltpu.VMEM((t, D), F32), pltpu.VMEM((t, D), F32),
                        pltpu.VMEM((8, D), F32)],
        compiler_params=_params(48),
    )(z, z, cw8, cb, wx, wa, bx, ba, lam)


def _lru_bwd(z, h_all, dya, cw8, cb, wx, wa, bx, ba, lam, nb, s_len):
    n = nb * s_len
    t = LRU_T
    ns = s_len // t
    t8 = t // 8

    def body(xp_ref, xph_ref, ga_ref, h_ref, hh_ref, dya_ref, cw_ref, cb_ref, wx_ref, wa_ref, bx_ref, ba_ref,
             lam_ref, dz_ref, gcw_ref, gcb_ref, gwx_ref, gwa_ref, gbx_ref, gba_ref, glam_ref,
             ext, hext, dext, a_s, u_s, dh_s, carry):
        b, s = pl.program_id(0), pl.program_id(1)
        first_tile = s == ns - 1

        @pl.when((b == 0) & (s == 0))
        def _():
            for ref in (gcw_ref, gcb_ref, gwx_ref, gwa_ref, gbx_ref, gba_ref, glam_ref):
                ref[...] = jnp.zeros(ref.shape, F32)

        @pl.when(s == 0)
        def _():
            dext[t:t + 8, :] = jnp.zeros((8, D), F32)
            carry[...] = jnp.zeros((8, D), F32)

        keep = jnp.where(first_tile, 0.0, 1.0)
        xp = xp_ref[0]
        ext[0:8, :] = xph_ref[0] * keep
        ext[8:8 + t, :] = xp
        hext[0:8, :] = hh_ref[...] * keep
        hext[8:8 + t, :] = h_ref[...]
        cw = cw_ref[...]
        lam = lam_ref[...]
        taps = _shifted(_groups(ext[...]), (3, 2, 1)) + [xp]
        xa = _conv(taps, cw, cb_ref[...])
        xab, gi, gr, sp, a, mult = _lru_gates(xa, wx_ref, wa_ref, bx_ref[...], ba_ref[...], lam)
        (h_prev,) = _shifted(_groups(hext[...]), (1,))
        ga = ga_ref[0]
        sg = _sigmoid(ga)
        dya_v = dya_ref[...]
        d_ga = dya_v * h_ref[...] * (sg * (1.0 + ga * (1.0 - sg)))
        g_in = dya_v * (ga * sg)

        (an,) = _shifted(jnp.concatenate([_groups(a), jnp.ones((1, 8, D), F32)], axis=0), (-1,))
        an, u = _groups(an), _groups(g_in)
        row = lax.broadcasted_iota(jnp.int32, an.shape, 1)
        for sh in (1, 2, 4):
            a_sh = pltpu.roll(an, 8 - sh, 1)
            u_sh = pltpu.roll(u, 8 - sh, 1)
            m = row < 8 - sh
            u = jnp.where(m, u + an * u_sh, u)
            an = jnp.where(m, an * a_sh, an)
        a_s[...] = an.reshape(t, D)
        u_s[...] = u.reshape(t, D)

        def step(i, c):
            r = pl.multiple_of((t8 - 1 - i) * 8, 8)
            dg = u_s[pl.ds(r, 8), :] + a_s[pl.ds(r, 8), :] * c
            dh_s[pl.ds(r, 8), :] = dg
            return dg[0:1, :]

        lax.fori_loop(0, t8, step, carry[0:1, :], unroll=4)
        dh = dh_s[...]
        carry[0:1, :] = a[0:1, :] * dh[0:1, :]

        d_a = dh * h_prev
        dux = dh * xa
        d_mult = dux * gi
        d_gi = dux * mult
        d_xa = dh * (mult * gi)
        d_loga = d_a * a - d_mult * ((a * a) / mult)
        d_gr = d_loga * (-LRU_C * sp)
        d_sp = jnp.sum(d_loga * (-LRU_C * gr), axis=0, keepdims=True)
        glam_ref[...] += d_sp * (-_sigmoid(-lam))
        d_pi = d_gi * gi * (1.0 - gi)
        d_pr = d_gr * gr * (1.0 - gr)
        gbx_ref[...] += jnp.sum(d_pi, axis=0, keepdims=True)
        gba_ref[...] += jnp.sum(d_pr, axis=0, keepdims=True)
        dpib = d_pi.astype(BF16)
        dprb = d_pr.astype(BF16)
        back = []
        for h in range(NB):
            cs = slice(h * BD, (h + 1) * BD)
            gwx_ref[h] += lax.dot_general(xab[:, cs], dpib[:, cs], TN_DIMS, preferred_element_type=F32)
            gwa_ref[h] += lax.dot_general(xab[:, cs], dprb[:, cs], TN_DIMS, preferred_element_type=F32)
            back.append(lax.dot_general(dpib[:, cs], wx_ref[h], NT_DIMS, preferred_element_type=F32)
                        + lax.dot_general(dprb[:, cs], wa_ref[h], NT_DIMS, preferred_element_type=F32))
        d_xa = d_xa + jnp.concatenate(back, axis=1)

        dext[0:t, :] = d_xa
        later = _shifted(_groups(dext[...]), (-3, -2, -1))
        d_xp = later[0] * cw[0:1, :] + later[1] * cw[1:2, :]
        d_xp = d_xp + later[2] * cw[2:3, :]
        d_xp = d_xp + d_xa * cw[3:4, :]
        dext[t:t + 8, :] = d_xa[0:8, :]
        gcb_ref[...] += jnp.sum(d_xa, axis=0, keepdims=True)
        for k in range(4):
            gcw_ref[k:k + 1, :] += jnp.sum(d_xa * taps[k], axis=0, keepdims=True)
        dz_ref[0] = d_xp.astype(BF16)
        dz_ref[1] = d_ga.astype(BF16)

    rb = lambda b, s: b * ns + (ns - 1 - s)
    halo = lambda b, s: jnp.maximum(rb(b, s) * t8 - 1, 0)
    rep2 = lambda b, s: (0, 0)
    rep3 = lambda b, s: (0, 0, 0)
    return pl.pallas_call(
        body, name="lru_bwd", grid=(nb, ns),
        in_specs=[pl.BlockSpec((1, t, D), lambda b, s: (0, rb(b, s), 0)),
                  pl.BlockSpec((1, 8, D), lambda b, s: (0, halo(b, s), 0)),
                  pl.BlockSpec((1, t, D), lambda b, s: (1, rb(b, s), 0)),
                  pl.BlockSpec((t, D), lambda b, s: (rb(b, s), 0)),
                  pl.BlockSpec((8, D), lambda b, s: (halo(b, s), 0)),
                  pl.BlockSpec((t, D), lambda b, s: (rb(b, s), 0)),
                  pl.BlockSpec((8, D), rep2), pl.BlockSpec((1, D), rep2),
                  pl.BlockSpec((NB, BD, BD), rep3), pl.BlockSpec((NB, BD, BD), rep3),
                  pl.BlockSpec((1, D), rep2), pl.BlockSpec((1, D), rep2), pl.BlockSpec((1, D), rep2)],
        out_specs=[pl.BlockSpec((2, t, D), lambda b, s: (0, rb(b, s), 0)),
                   pl.BlockSpec((8, D), rep2), pl.BlockSpec((1, D), rep2),
                   pl.BlockSpec((NB, BD, BD), rep3), pl.BlockSpec((NB, BD, BD), rep3),
                   pl.BlockSpec((1, D), rep2), pl.BlockSpec((1, D), rep2), pl.BlockSpec((1, D), rep2)],
        out_shape=[SDS((2, n, D), BF16), SDS((8, D), F32), SDS((1, D), F32),
                   SDS((NB, BD, BD), F32), SDS((NB, BD, BD), F32),
                   SDS((1, D), F32), SDS((1, D), F32), SDS((1, D), F32)],
        scratch_shapes=[pltpu.VMEM((t + 8, D), F32), pltpu.VMEM((t + 8, D), F32), pltpu.VMEM((t + 8, D), F32),
                        pltpu.VMEM((t, D), F32), pltpu.VMEM((t, D), F32), pltpu.VMEM((t, D), F32),
                        pltpu.VMEM((8, D), F32)],
        compiler_params=_params(56),
    )(z, z, z, h_all, h_all, dya, cw8, cb, wx, wa, bx, ba, lam)


HG_T = 512
HG_NC = HG_T // CHUNK
BNT_DIMS = (((2,), (2,)), ((0,), (0,)))
BNN_DIMS = (((2,), (1,)), ((0,), (0,)))
BTN_DIMS = (((1,), (1,)), ((0,), (0,)))


def _lower_bound(lg):
    m = jnp.max(lg, axis=0, keepdims=True)
    e = jnp.exp(lg - m)
    return e[0:1, :] / jnp.sum(e, axis=0, keepdims=True)


def _tri(upper):
    r = lax.broadcasted_iota(jnp.int32, (HG_NC, CHUNK, CHUNK), 1)
    c = lax.broadcasted_iota(jnp.int32, (HG_NC, CHUNK, CHUNK), 2)
    return (c >= r) if upper else (r >= c)


def _bdot(a, b, dims):
    return lax.dot_general(a, b, dims, preferred_element_type=F32)


def _tri_sums(upper, a):
    tri = _tri(upper).astype(BF16)
    a1 = a.astype(BF16)
    r1 = a - a1.astype(F32)
    a2 = r1.astype(BF16)
    a3 = (r1 - a2.astype(F32)).astype(BF16)
    return _bdot(tri, a1, BNN_DIMS) + (_bdot(tri, a2, BNN_DIMS) + _bdot(tri, a3, BNN_DIMS))


def _chunks(a):
    return a.reshape(HG_NC, CHUNK, BD)


def _hg_tile(q, fp, lb):
    q, fp = _chunks(q), _chunks(fp)
    sig = _sigmoid(fp)
    f = lb + (1.0 - lb) * sig
    log_f = jnp.log(f)
    k = 1.0 - f
    b = _tri_sums(False, log_f)
    b_mid = b[:, CHUNK // 2:CHUNK // 2 + 1, :]
    b_last = b[:, CHUNK - 1:CHUNK, :]
    sq = _sigmoid(q)
    qh = q * sq
    e_qi = jnp.exp(b - b_mid)
    e_ki = jnp.exp(b_mid - b)
    e_qs = jnp.exp(b)
    e_ks = jnp.exp(b_last - b)
    dc = jnp.exp(b_last)
    q_in = (qh * e_qi) * HG_SCALE
    k_in = k * e_ki
    q_st = (qh * e_qs) * HG_SCALE
    k_st = k * e_ks
    att = _bdot(q_in.astype(BF16), k_in.astype(BF16), BNT_DIMS)
    att = jnp.where(_tri(False), att, 0.0)
    return dict(q=q, sig=sig, f=f, k=k, sq=sq, e_qi=e_qi, e_ki=e_ki, e_qs=e_qs, e_ks=e_ks, dc=dc,
                q_in=q_in, k_in=k_in, q_st=q_st, k_st=k_st, att=att)


def _hgrn_fwd(z, lb_logits, hg_g, nb, s_len):
    n = nb * s_len
    t = HG_T
    ns = s_len // t
    nchunk = s_len // CHUNK

    def body(q_ref, f_ref, v_ref, gb_ref, lg_ref, g_ref, o_ref, yb_ref, st_ref, st):
        @pl.when(pl.program_id(1) == 0)
        def _():
            st[...] = jnp.zeros((NB, BD, BD), F32)

        def head(h, carry):
            cols = pl.ds(pl.multiple_of(h * BD, BD), BD)
            lb = _lower_bound(lg_ref[:, cols])
            ck = _hg_tile(q_ref[0, :, cols], f_ref[0, :, cols], lb)
            vb = _chunks(v_ref[0, :, cols]).astype(BF16)
            kv = _bdot(vb, ck["k_st"].astype(BF16), BTN_DIMS)
            states = [st[h]]
            for c in range(HG_NC):
                states.append(states[c] * ck["dc"][c] + kv[c])
            st[h] = states[HG_NC]
            s_in = jnp.stack(states[:HG_NC], axis=0)
            st_ref[h] = s_in
            o = (_bdot(ck["att"].astype(BF16), vb, BNN_DIMS)
                 + _bdot(ck["q_st"].astype(BF16), s_in.astype(BF16), BNT_DIMS))
            o_ref[:, cols] = o.reshape(t, BD)
            r = lax.rsqrt(jnp.mean(o * o, axis=-1, keepdims=True) + EPS)
            gb = _chunks(gb_ref[0, :, cols])
            yb_ref[:, cols] = (((o * r) * g_ref[...]) * (gb * _sigmoid(gb))).astype(BF16).reshape(t, BD)
            return carry

        lax.fori_loop(0, NB, head, 0, unroll=4)

    seg = lambda j: pl.BlockSpec((1, t, D), lambda b, s: (j, b * ns + s, 0))
    tile = pl.BlockSpec((t, D), lambda b, s: (b * ns + s, 0))
    return pl.pallas_call(
        body, name="hgrn_fwd", grid=(nb, ns),
        in_specs=[seg(2), seg(3), seg(4), seg(5),
                  pl.BlockSpec((2, D), lambda b, s: (0, 0)), pl.BlockSpec((1, BD), lambda b, s: (0, 0))],
        out_specs=[tile, tile, pl.BlockSpec((NB, HG_NC, BD, BD), lambda b, s: (b, s, 0, 0))],
        out_shape=[SDS((n, D), F32), SDS((n, D), BF16), SDS((nb * NB, nchunk, BD, BD), F32)],
        scratch_shapes=[pltpu.VMEM((NB, BD, BD), F32)],
        compiler_params=_params(56),
    )(z, z, z, z, lb_logits, hg_g)


def _hgrn_bwd(z, o_all, st_all, dyb, lb_logits, hg_g, nb, s_len):
    n = nb * s_len
    t = HG_T
    ns = s_len // t

    def body(q_ref, f_ref, v_ref, gb_ref, o_ref, st_ref, dyb_ref, lg_ref, g_ref,
             dz_ref, glg_ref, ghg_ref, dst, dlb):
        b, s = pl.program_id(0), pl.program_id(1)

        @pl.when((b == 0) & (s == 0))
        def _():
            ghg_ref[...] = jnp.zeros((1, BD), F32)
            dlb[...] = jnp.zeros((8, D), F32)

        @pl.when(s == 0)
        def _():
            dst[...] = jnp.zeros((NB, BD, BD), F32)

        g = g_ref[...]

        def head(h, carry):
            cols = pl.ds(pl.multiple_of(h * BD, BD), BD)
            lb = _lower_bound(lg_ref[:, cols])
            ck = _hg_tile(q_ref[0, :, cols], f_ref[0, :, cols], lb)
            q = ck["q"]
            vb = _chunks(v_ref[0, :, cols]).astype(BF16)
            gb = _chunks(gb_ref[0, :, cols])
            o = _chunks(o_ref[:, cols])
            dyb_v = _chunks(dyb_ref[:, cols])
            s_in = st_ref[h]

            sgb = _sigmoid(gb)
            r = lax.rsqrt(jnp.mean(o * o, axis=-1, keepdims=True) + EPS)
            ohat = o * r
            d_on = dyb_v * (gb * sgb)
            d_gb = dyb_v * (ohat * g) * (sgb * (1.0 + gb * (1.0 - sgb)))
            ghg_ref[...] += jnp.sum(jnp.sum(d_on * ohat, axis=1), axis=0, keepdims=True)
            tt = d_on * g
            d_o = r * (tt - ohat * jnp.mean(tt * ohat, axis=-1, keepdims=True))
            dob = d_o.astype(BF16)

            attb = ck["att"].astype(BF16)
            q_inb, k_inb = ck["q_in"].astype(BF16), ck["k_in"].astype(BF16)
            q_stb, k_stb = ck["q_st"].astype(BF16), ck["k_st"].astype(BF16)
            d_att = jnp.where(_tri(False), _bdot(dob, vb, BNT_DIMS), 0.0).astype(BF16)
            d_q_in = _bdot(d_att, k_inb, BNN_DIMS)
            d_k_in = _bdot(d_att, q_inb, BTN_DIMS)
            d_q_st = _bdot(dob, s_in.astype(BF16), BNN_DIMS)
            qdo = _bdot(dob, q_stb, BTN_DIMS)
            d_states = [None] * HG_NC + [dst[h]]
            for c in reversed(range(HG_NC)):
                d_states[c] = d_states[c + 1] * ck["dc"][c] + qdo[c]
            dst[h] = d_states[0]
            ds_out = jnp.stack(d_states[1:], axis=0)
            dsb = ds_out.astype(BF16)
            d_v = _bdot(attb, dob, BTN_DIMS) + _bdot(k_stb, dsb, BNT_DIMS)
            d_k_st = _bdot(vb, dsb, BNN_DIMS)
            d_dc = jnp.sum(ds_out * s_in, axis=1, keepdims=True)

            p_qi = d_q_in * ck["q_in"]
            p_ki = d_k_in * ck["k_in"]
            p_qs = d_q_st * ck["q_st"]
            p_ks = d_k_st * ck["k_st"]
            d_qh = (d_q_in * ck["e_qi"] + d_q_st * ck["e_qs"]) * HG_SCALE
            d_k = d_k_in * ck["e_ki"] + d_k_st * ck["e_ks"]
            d_b = (p_qi - p_ki) + (p_qs - p_ks)
            d_b_mid = jnp.sum(p_ki - p_qi, axis=1, keepdims=True)
            d_b_last = jnp.sum(p_ks, axis=1, keepdims=True) + d_dc * ck["dc"]
            rowi = lax.broadcasted_iota(jnp.int32, (HG_NC, CHUNK, BD), 1)
            d_b = d_b + jnp.where(rowi == CHUNK // 2, d_b_mid, 0.0) + jnp.where(rowi == CHUNK - 1, d_b_last, 0.0)
            d_logf = _tri_sums(True, d_b)
            d_f = d_logf / ck["f"] - d_k
            sig, sq = ck["sig"], ck["sq"]
            d_fp = d_f * (1.0 - lb) * (sig * (1.0 - sig))
            dlb[0:1, cols] += jnp.sum(jnp.sum(d_f * (1.0 - sig), axis=1), axis=0, keepdims=True)
            d_q = d_qh * (sq * (1.0 + q * (1.0 - sq)))
            dz_ref[0, :, cols] = d_q.astype(BF16).reshape(t, BD)
            dz_ref[1, :, cols] = d_fp.astype(BF16).reshape(t, BD)
            dz_ref[2, :, cols] = d_v.astype(BF16).reshape(t, BD)
            dz_ref[3, :, cols] = d_gb.astype(BF16).reshape(t, BD)
            return carry

        lax.fori_loop(0, NB, head, 0, unroll=2)

        @pl.when((b == nb - 1) & (s == ns - 1))
        def _():
            lb = _lower_bound(lg_ref[...])
            dl = dlb[0:1, :] * (lb * (1.0 - lb))
            glg_ref[0:1, :] = dl
            glg_ref[1:2, :] = -dl

    rb = lambda b, s: b * ns + (ns - 1 - s)
    seg = lambda j: pl.BlockSpec((1, t, D), lambda b, s: (j, rb(b, s), 0))
    tile = pl.BlockSpec((t, D), lambda b, s: (rb(b, s), 0))
    return pl.pallas_call(
        body, name="hgrn_bwd", grid=(nb, ns),
        in_specs=[seg(2), seg(3), seg(4), seg(5), tile,
                  pl.BlockSpec((NB, HG_NC, BD, BD), lambda b, s: (b, ns - 1 - s, 0, 0)),
                  tile, pl.BlockSpec((2, D), lambda b, s: (0, 0)), pl.BlockSpec((1, BD), lambda b, s: (0, 0))],
        out_specs=[pl.BlockSpec((4, t, D), lambda b, s: (0, rb(b, s), 0)),
                   pl.BlockSpec((2, D), lambda b, s: (0, 0)), pl.BlockSpec((1, BD), lambda b, s: (0, 0))],
        out_shape=[SDS((4, n, D), BF16), SDS((2, D), F32), SDS((1, BD), F32)],
        scratch_shapes=[pltpu.VMEM((NB, BD, BD), F32), pltpu.VMEM((8, D), F32)],
        compiler_params=_params(60),
    )(z, z, z, z, o_all, st_all, dyb, lb_logits, hg_g)


def _mid(ya, yb, z, b_merge, x2, tgt, fin_g, pa, pb, wo):
    n = x2.shape[0]
    tm = 256
    ni = n // tm

    def body(ya_ref, yb_ref, gma_ref, gmb_ref, bm_ref, x_ref, t_ref, fg_ref, pa_hbm, pb_hbm, wo_hbm,
             dx2_ref, dya_ref, dyb_ref, dgm_ref, loss_ref, gfg_ref, gbm_ref, gm_hbm,
             pa_v, pb_v, wo_v, gpa_v, gpb_v, gwo_v, sem):
        i = pl.program_id(0)
        by_owner = lambda g: g.reshape(NB, BD, D)
        loads = [pltpu.make_async_copy(src, dst, sem.at[k])
                 for k, (src, dst) in enumerate(((pa_hbm, pa_v), (pb_hbm, pb_v), (wo_hbm, wo_v)))]
        stores = [pltpu.make_async_copy(src, dst, sem.at[k])
                  for k, (src, dst) in enumerate((g, gm_hbm.at[:, pl.ds(slot * BD, BD), :])
                                                 for slot, g in enumerate((gpa_v, gpb_v, gwo_v)))]

        @pl.when(i == 0)
        def _():
            for cp in loads:
                cp.start()
            for ref in (gpa_v, gpb_v, gwo_v, loss_ref, gfg_ref, gbm_ref):
                ref[...] = jnp.zeros(ref.shape, F32)
            for cp in loads:
                cp.wait()

        ya_v = ya_ref[...]
        yb_v = yb_ref[...]
        out_a = jnp.dot(ya_v, pa_v[...], preferred_element_type=F32)
        out_b = jnp.dot(yb_v, pb_v[...], preferred_element_type=F32)
        bm = bm_ref[...]
        g_a = _sigmoid(gma_ref[0] + bm[:, 0:D])
        g_b = _sigmoid(gmb_ref[0] + bm[:, D:2 * D])
        mixed = g_a * out_a + g_b * out_b
        mixb = mixed.astype(BF16)
        xo = x_ref[...] + jnp.dot(mixb, wo_v[...], preferred_element_type=F32)
        r = lax.rsqrt(jnp.mean(xo * xo, axis=-1, keepdims=True) + EPS)
        xn = xo * r
        fg = fg_ref[...]
        e = xn * fg - t_ref[...]
        loss_ref[...] += 0.5 * jnp.sum(jnp.mean(e * e, axis=-1, keepdims=True))
        dy = e * (1.0 / D)
        gfg_ref[...] += jnp.sum(dy * xn, axis=0, keepdims=True)
        dxn = dy * fg
        dx2 = r * (dxn - xn * jnp.mean(dxn * xn, axis=-1, keepdims=True))
        dx2_ref[...] = dx2
        dx2b = dx2.astype(BF16)
        d_mixed = lax.dot_general(dx2b, wo_v[...], NT_DIMS, preferred_element_type=F32)
        gwo_v[...] += by_owner(lax.dot_general(mixb, dx2b, TN_DIMS, preferred_element_type=F32))
        d_oa = (d_mixed * g_a).astype(BF16)
        d_ob = (d_mixed * g_b).astype(BF16)
        dgm_a = (d_mixed * out_a) * (g_a * (1.0 - g_a))
        dgm_b = (d_mixed * out_b) * (g_b * (1.0 - g_b))
        gbm_ref[:, 0:D] += jnp.sum(dgm_a, axis=0, keepdims=True)
        gbm_ref[:, D:2 * D] += jnp.sum(dgm_b, axis=0, keepdims=True)
        dgm_ref[0] = dgm_a.astype(BF16)
        dgm_ref[1] = dgm_b.astype(BF16)
        dya_ref[...] = lax.dot_general(d_oa, pa_v[...], NT_DIMS, preferred_element_type=F32)
        dyb_ref[...] = lax.dot_general(d_ob, pb_v[...], NT_DIMS, preferred_element_type=F32)
        gpa_v[...] += by_owner(lax.dot_general(ya_v, d_oa, TN_DIMS, preferred_element_type=F32))
        gpb_v[...] += by_owner(lax.dot_general(yb_v, d_ob, TN_DIMS, preferred_element_type=F32))

        @pl.when(i == ni - 1)
        def _():
            for cp in stores:
                cp.start()
            for cp in stores:
                cp.wait()

    rows = pl.BlockSpec((tm, D), lambda i: (i, 0))
    rep = lambda shape: pl.BlockSpec(shape, lambda i: (0,) * len(shape))
    return pl.pallas_call(
        body, name="mid", grid=(ni,),
        in_specs=[rows, rows,
                  pl.BlockSpec((1, tm, D), lambda i: (6, i, 0)), pl.BlockSpec((1, tm, D), lambda i: (7, i, 0)),
                  rep((1, 2 * D)), rows, rows, rep((1, D)), ANY, ANY, ANY],
        out_specs=[rows, rows, rows, pl.BlockSpec((2, tm, D), lambda i: (0, i, 0)),
                   rep((8, BD)), rep((1, D)), rep((1, 2 * D)), ANY],
        out_shape=[SDS((n, D), F32), SDS((n, D), F32), SDS((n, D), F32), SDS((2, n, D), BF16),
                   SDS((8, BD), F32), SDS((1, D), F32), SDS((1, 2 * D), F32),
                   SDS((NB, MID_ROWS, D), F32)],
        scratch_shapes=[pltpu.VMEM((D, D), BF16)] * 3 + [pltpu.VMEM((NB, BD, D), F32)] * 3 + [pltpu.SemaphoreType.DMA((3,))],
        compiler_params=_params(60),
    )(ya, yb, z, z, b_merge, x2, tgt, fin_g, pa, pb, wo)


def _dz_specs(tm, ni, row_major):
    if row_major:
        ia = lambda i, j: (jnp.minimum(j, 1), i, 0)
        ib = lambda i, j: (jnp.clip(j - 2, 0, 3), i, 0)
        im = lambda i, j: (jnp.clip(j - 6, 0, 1), i, 0)
    else:
        last = ni - 1
        ia = lambda j, i: (jnp.minimum(j, 1), jnp.where(j < 2, i, last), 0)
        ib = lambda j, i: (jnp.clip(j - 2, 0, 3), jnp.where(j < 2, 0, jnp.where(j < 6, i, last)), 0)
        im = lambda j, i: (jnp.clip(j - 6, 0, 1), jnp.where(j < 6, 0, i), 0)
    return [pl.BlockSpec((1, tm, D), f) for f in (ia, ib, im)]


def _inproj_bwd_x(dza, dzb, dzm, w_all, x2, dx2, norm_g, after):
    n = x2.shape[0]
    tm = 512
    ni = n // tm

    def body(dza_ref, dzb_ref, dzm_ref, w_ref, x_ref, dx2_ref, g_ref, after_ref, gx_ref, gg_ref, acc):
        i, j = pl.program_id(0), pl.program_id(1)

        @pl.when((i == 0) & (j == 0))
        def _():
            gg_ref[...] = jnp.zeros((1, D), F32)

        @pl.when(j == 0)
        def _():
            acc[...] = jnp.zeros((tm, D), F32)

        def add(ref):
            acc[...] += lax.dot_general(ref[0], w_ref[0], NT_DIMS, preferred_element_type=F32)

        pl.when(j < 2)(lambda: add(dza_ref))
        pl.when((j >= 2) & (j < 6))(lambda: add(dzb_ref))
        pl.when(j >= 6)(lambda: add(dzm_ref))

        @pl.when(j == NB - 1)
        def _():
            x = x_ref[...]
            r = lax.rsqrt(jnp.mean(x * x, axis=-1, keepdims=True) + EPS)
            xn = x * r
            dh = acc[...]
            gg_ref[...] += jnp.sum(dh * xn, axis=0, keepdims=True)
            dxn = dh * g_ref[...]
            gx_ref[...] = dx2_ref[...] + r * (dxn - xn * jnp.mean(dxn * xn, axis=-1, keepdims=True))

    rows = pl.BlockSpec((tm, D), lambda i, j: (i, 0))
    return pl.pallas_call(
        body, name="inproj_bwd_x", grid=(ni, NB),
        in_specs=_dz_specs(tm, ni, True) + [pl.BlockSpec((1, D, D), lambda i, j: (j, 0, 0)), rows, rows,
                                             pl.BlockSpec((1, D), lambda i, j: (0, 0)), ANY],
        out_specs=[rows, pl.BlockSpec((1, D), lambda i, j: (0, 0))],
        out_shape=[SDS((n, D), F32), SDS((1, D), F32)],
        scratch_shapes=[pltpu.VMEM((tm, D), F32)],
        compiler_params=_params(48),
    )(dza, dzb, dzm, w_all, x2, dx2, norm_g, after)


def _walk_tables(order, ni):
    rows = []
    for lo, hi in ((0, 2), (2, 6), (6, 8)):
        active = [j for j, g in enumerate(order) if lo <= g < hi]
        block, tile = [], []
        for j, g in enumerate(order):
            before = [a for a in active if a < j]
            if lo <= g < hi:
                block.append(g - lo), tile.append(-1)
            elif before:
                block.append(order[before[-1]] - lo), tile.append(ni - 1)
            else:
                block.append(order[active[0]] - lo), tile.append(0)
        rows += [block, tile]
    return rows


def _inproj_bwd_w(core, dza, dzb, dzm, h_all, g_m):
    n = h_all.shape[0]
    tm = min(n, 2048)
    ni = n // tm
    packed = g_m.shape[1:]
    orders = [[2 * q + 1 - c for q in range(4)] + [2 * q + c for q in range(4)] for c in (0, 1)]
    tables = jnp.asarray([[order] + _walk_tables(order, ni) for order in orders], jnp.int32)
    walk = jnp.where(core == 0, tables[0], tables[1])

    def body(walk_ref, dza_ref, dzb_ref, dzm_ref, h_ref, gm_hbm, out_bf, own_f32, m_out_bf, m_own_f32, got_w, got_m,
             acc, stage, theirs, m_mine, m_theirs, m_stage, send_sems, recv_sems, local_sems):
        j, i = pl.program_id(0), pl.program_id(1)
        group = walk_ref[0, j]
        x, y, c = _place()
        sibling = (x, y, 1 - c)

        def send_w(q):
            return pltpu.make_async_remote_copy(
                src_ref=stage.at[q % 2], dst_ref=got_w.at[q], send_sem=send_sems.at[q], recv_sem=recv_sems.at[q],
                device_id=sibling, device_id_type=MESH)

        def send_m(q):
            return pltpu.make_async_remote_copy(
                src_ref=gm_hbm.at[2 * q + (1 - c)], dst_ref=got_m.at[q], send_sem=send_sems.at[4 + q],
                recv_sem=recv_sems.at[4 + q], device_id=sibling, device_id_type=MESH)

        def fetch(q):
            return pltpu.make_async_copy(got_w.at[q], theirs, local_sems.at[0])

        def fetch_m(q):
            return (pltpu.make_async_copy(gm_hbm.at[2 * q + c], m_mine, local_sems.at[2]),
                    pltpu.make_async_copy(got_m.at[q], m_theirs, local_sems.at[3]))

        @pl.when((j == 0) & (i == 0))
        def _():
            for q in range(4):
                send_m(q).start()

        @pl.when(i == 0)
        def _():
            acc[...] = jnp.zeros((D, D), F32)

        def add(ref):
            acc[...] += lax.dot_general(h_ref[...], ref[0], TN_DIMS, preferred_element_type=F32)

        pl.when(group < 2)(lambda: add(dza_ref))
        pl.when((group >= 2) & (group < 6))(lambda: add(dzb_ref))
        pl.when(group >= 6)(lambda: add(dzm_ref))

        for q in range(4):
            @pl.when((i == ni - 1) & (j == q))
            def _(q=q):
                if q >= 2:
                    send_w(q - 2).wait_send()
                stage[q % 2] = acc[...].astype(BF16)
                send_w(q).start()

        for q in range(4):
            @pl.when((j == 4 + q) & (i == 0))
            def _(q=q):
                send_w(q).wait_recv()
                send_m(q).wait_recv()
                fetch(q).start()
                for cp in fetch_m(q):
                    cp.start()

            @pl.when((j == 4 + q) & (i == ni - 1))
            def _(q=q):
                if q == 0:
                    send_w(2).wait_send()
                    send_w(3).wait_send()
                other_x, other_y = x != q // 2, y != q % 2
                other = other_x | other_y
                slot = jnp.where(other_x & other_y, 2, jnp.where(other_x, 0, 1))
                m_out = pltpu.make_async_copy(m_stage, m_out_bf.at[slot], local_sems.at[4])
                m_own = pltpu.make_async_copy(m_mine, m_own_f32, local_sems.at[4])

                for cp in fetch_m(q):
                    cp.wait()

                @pl.when(other)
                def _():
                    m_stage[...] = (m_mine[...] + m_theirs[...]).astype(BF16)
                    m_out.start()

                @pl.when(jnp.logical_not(other))
                def _():
                    m_mine[...] += m_theirs[...]
                    m_own.start()

                fetch(q).wait()

                @pl.when(other)
                def _():
                    stage[0] = (acc[...] + theirs[...].astype(F32)).astype(BF16)
                    out = pltpu.make_async_copy(stage.at[0], out_bf.at[slot], local_sems.at[1])
                    out.start()
                    out.wait()
                    m_out.wait()

                @pl.when(jnp.logical_not(other))
                def _():
                    acc[...] += theirs[...].astype(F32)
                    out = pltpu.make_async_copy(acc, own_f32, local_sems.at[1])
                    out.start()
                    out.wait()
                    m_own.wait()

        @pl.when((j == NB - 1) & (i == ni - 1))
        def _():
            for q in range(4):
                send_m(q).wait_send()

    def dz_spec(k):
        return pl.BlockSpec((1, tm, D), lambda j, i, w: (w[1 + 2 * k, j], jnp.where(w[2 + 2 * k, j] < 0, i, w[2 + 2 * k, j]), 0))

    return pl.pallas_call(
        body, name="inproj_bwd_w",
        grid_spec=pltpu.PrefetchScalarGridSpec(
            num_scalar_prefetch=1, grid=(NB, ni),
            in_specs=[dz_spec(0), dz_spec(1), dz_spec(2), pl.BlockSpec((tm, D), lambda j, i, w: (i, 0)), ANY],
            out_specs=[ANY] * 6,
            scratch_shapes=[pltpu.VMEM((D, D), F32), pltpu.VMEM((2, D, D), BF16), pltpu.VMEM((D, D), BF16),
                            pltpu.VMEM(packed, F32), pltpu.VMEM(packed, F32), pltpu.VMEM(packed, BF16),
                            pltpu.SemaphoreType.DMA((8,)), pltpu.SemaphoreType.DMA((8,)), pltpu.SemaphoreType.DMA((5,))]),
        out_shape=[SDS((3, D, D), BF16), SDS((D, D), F32), SDS((3,) + packed, BF16), SDS(packed, F32),
                   SDS((4, D, D), BF16), SDS((4,) + packed, F32)],
        compiler_params=_params(58),
    )(walk, dza, dzb, dzm, h_all, g_m)


def _adamw(w, g, m, v):
    rows, cols = w.shape
    tr = _row_tile(rows)

    spec = pl.BlockSpec((tr, cols), lambda i: (i, 0))
    return pl.pallas_call(
        functools.partial(_adam_refs), name="adamw", grid=(rows // tr,), in_specs=[spec] * 4, out_specs=[spec] * 3,
        out_shape=[SDS((rows, cols), F32)] * 3, compiler_params=_params(32),
    )(w, g, m, v)


def _adam_refs(w_ref, g_ref, m_ref, v_ref, d_ref, nm_ref, nv_ref):
    gv = g_ref[...]
    nm = ADAM_B1 * m_ref[...] + (1.0 - ADAM_B1) * gv
    nv = ADAM_B2 * v_ref[...] + (1.0 - ADAM_B2) * (gv * gv)
    m_hat = nm / (1.0 - ADAM_B1 ** ADAM_STEP)
    v_hat = nv / (1.0 - ADAM_B2 ** ADAM_STEP)
    d_ref[...] = -ADAM_LR * (m_hat / (jnp.sqrt(v_hat) + ADAM_EPS) + ADAM_WD * w_ref[...])
    nm_ref[...] = nm
    nv_ref[...] = nv


def _adamw_small(ws, gs, ms, vs):
    k = len(ws)

    def body(*refs):
        ins, outs = refs[:4 * k], refs[4 * k:7 * k]
        vin, vout = refs[7 * k:11 * k], refs[11 * k:14 * k]
        load_sems, store_sems = refs[14 * k:]
        loads = [pltpu.make_async_copy(ins[i], vin[i], load_sems.at[i]) for i in range(4 * k)]
        for cp in loads:
            cp.start()
        for cp in loads:
            cp.wait()
        for i in range(k):
            _adam_refs(*[vin[part * k + i] for part in range(4)], *[vout[part * k + i] for part in range(3)])
        stores = [pltpu.make_async_copy(vout[i], outs[i], store_sems.at[i]) for i in range(3 * k)]
        for cp in stores:
            cp.start()
        for cp in stores:
            cp.wait()

    shapes = [SDS(w.shape, F32) for w in ws]
    vmem = [pltpu.VMEM(w.shape, F32) for w in ws]
    out = pl.pallas_call(
        body, name="adamw_small", out_shape=shapes * 3, in_specs=[HBM] * (4 * k), out_specs=[HBM] * (3 * k),
        scratch_shapes=vmem * 7 + [pltpu.SemaphoreType.DMA((4 * k,)), pltpu.SemaphoreType.DMA((3 * k,))],
        compiler_params=_params(32),
    )(*ws, *gs, *ms, *vs)
    return out[:k], out[k:2 * k], out[2 * k:]


def _allgather(blocks, dtypes, name):
    na = len(blocks)

    def body(*refs):
        ins, outs, stages = refs[:na], refs[na:2 * na], refs[2 * na:3 * na]
        send_sems, recv_sems, local_sems = refs[3 * na:]
        x, y, c = _place()
        me, sibling = (x, y, c), (x, y, 1 - c)
        chips = [(1 - x, y), (x, 1 - y), (1 - x, 1 - y)]
        blk = lambda p: 4 * p[0] + 2 * p[1] + p[2]

        def copy(a, k, block, to, src=None):
            return pltpu.make_async_remote_copy(
                src_ref=outs[a].at[blk(block)] if src is None else src, dst_ref=outs[a].at[blk(block)],
                send_sem=send_sems.at[7 * a + k], recv_sem=recv_sems.at[7 * a + k],
                device_id=to, device_id_type=MESH)

        mine, first, passed = [], [], []
        for a in range(na):
            stages[a][...] = ins[a][...].astype(dtypes[a])
            mine.append(pltpu.make_async_copy(stages[a], outs[a].at[blk(me)], local_sems.at[a]))
            mine[-1].start()
            first.append(copy(a, 0, me, sibling, src=stages[a]))
            first += [copy(a, 1 + j, me, (*chip, c), src=stages[a]) for j, chip in enumerate(chips)]
        for cp in first:
            cp.start()
        for j, chip in enumerate(chips):
            for a in range(na):
                copy(a, 1 + j, (*chip, c), me).wait_recv()
                passed.append(copy(a, 4 + j, (*chip, c), sibling))
                passed[-1].start()
        for a in range(na):
            copy(a, 0, sibling, me).wait_recv()
            for j, chip in enumerate(chips):
                copy(a, 4 + j, (*chip, 1 - c), me).wait_recv()
        for cp in first + passed:
            cp.wait_send()
        for cp in mine:
            cp.wait()

    return pl.pallas_call(
        body, name=name,
        in_specs=[pl.BlockSpec(memory_space=pltpu.VMEM)] * na, out_specs=[ANY] * na,
        out_shape=[SDS((NB,) + b.shape, dt) for b, dt in zip(blocks, dtypes)],
        scratch_shapes=[pltpu.VMEM(b.shape, dt) for b, dt in zip(blocks, dtypes)]
        + [pltpu.SemaphoreType.DMA((7 * na,)), pltpu.SemaphoreType.DMA((7 * na,)), pltpu.SemaphoreType.DMA((na,))],
        compiler_params=_params(40),
    )(*blocks)


HBM = pl.BlockSpec(memory_space=pltpu.HBM)
SEMS = pl.BlockSpec(memory_space=pltpu.SEMAPHORE)
EFFECT = pltpu.SideEffectType.DATAFLOW_SIDE_EFFECTING


def _chip_copies(srcs, lands, send_sems, recv_sems):
    x, y, c = _place()
    return [pltpu.make_async_remote_copy(
        src_ref=srcs[a].at[slot], dst_ref=lands[a].at[slot],
        send_sem=send_sems.at[3 * a + slot], recv_sem=recv_sems.at[3 * a + slot],
        device_id=(px, py, c), device_id_type=MESH)
        for a in range(len(srcs)) for slot, (px, py) in enumerate(_other_chips(x, y))]


def _split_start(name, copies, per_array, srcs, lands, after=None):
    na = len(srcs)

    def body(*refs):
        send_sems, recv_sems = refs[-2 * na - 3], refs[-2 * na - 2]
        for cp in copies(refs[:na], refs[na:2 * na], send_sems, recv_sems):
            cp.start()
        refs[-1][...] = jnp.zeros_like(refs[-1])

    hbm = lambda a: pltpu.HBM(a.shape, a.dtype)
    pin = lambda a: pltpu.with_memory_space_constraint(a, pltpu.HBM)
    out = pl.pallas_call(
        body, name=name,
        out_shape=(pltpu.SemaphoreType.DMA((per_array * na,)), pltpu.SemaphoreType.DMA((per_array * na,)),
                   *[hbm(a) for a in srcs], *[hbm(a) for a in lands], SDS((8, BD), F32)),
        in_specs=[HBM] * (2 * na) + ([] if after is None else [ANY]),
        out_specs=(SEMS, SEMS, *[HBM] * (2 * na), pl.BlockSpec(memory_space=pltpu.VMEM)),
        input_output_aliases={i: 2 + i for i in range(2 * na)},
        compiler_params=pltpu.CompilerParams(has_side_effects=EFFECT),
    )(*[pin(a) for a in srcs], *[pin(a) for a in lands], *([] if after is None else [after]))
    return out[0], out[1], out[2:2 + na], out[2 + na:2 + 2 * na], out[-1]


def _split_wait(name, copies, started, after):
    send_sems, recv_sems, srcs, lands, _ = started
    na = len(srcs)

    def body(*refs):
        waits = copies(refs[:na], refs[na:2 * na], refs[2 * na], refs[2 * na + 1])
        for cp in waits:
            cp.wait_send()
        for cp in waits:
            cp.wait_recv()

    hbm = lambda a: pltpu.HBM(a.shape, a.dtype)
    out = pl.pallas_call(
        body, name=name,
        out_shape=(*[hbm(a) for a in srcs], *[hbm(a) for a in lands]),
        in_specs=[HBM] * (2 * na) + [SEMS, SEMS, ANY],
        out_specs=tuple([HBM] * (2 * na)),
        input_output_aliases={i: i for i in range(2 * na)},
        compiler_params=pltpu.CompilerParams(has_side_effects=EFFECT),
    )(*srcs, *lands, send_sems, recv_sems, after)
    return out[na:]


def _add_chips(own, b_in):
    r, cols = own.shape
    tr = _row_tile(r)

    def body(p_ref, b0_ref, b1_ref, b2_ref, o_ref):
        o_ref[...] = ((p_ref[...] + b0_ref[0].astype(F32)) + b1_ref[0].astype(F32)) + b2_ref[0].astype(F32)

    slot = lambda k: pl.BlockSpec((1, tr, cols), lambda i: (k, i, 0))
    spec = pl.BlockSpec((tr, cols), lambda i: (i, 0))
    return pl.pallas_call(
        body, name="add_chips", grid=(r // tr,), in_specs=[spec, slot(0), slot(1), slot(2)], out_specs=spec,
        out_shape=SDS((r, cols), F32), compiler_params=_params(32),
    )(own, b_in, b_in, b_in)


VEC_NAMES = ("b_merge", "conv_b", "rg_bx", "rg_ba", "rg_lambda", "hg_lb_logits", "hg_norm_g", "final_norm_g")
REP_NAMES = ("rg_wx", "rg_wa", "norm_g") + VEC_NAMES
SMALL_AT = 3 * BD
SMALL_ROWS = 48
MID_ROWS = 448


def _sum_blocks(parts):
    def body(p_ref, o_ref):
        acc = p_ref[0]
        for k in range(1, NB):
            acc = acc + p_ref[k]
        o_ref[...] = acc

    return pl.pallas_call(body, name="sum_blocks", out_shape=SDS(parts.shape[1:], F32))(parts)


def _pack_rows(arrays, width, row_multiple=8):
    flat = jnp.concatenate([a.reshape(-1) for a in arrays])
    rows = -(-flat.shape[0] // width)
    rows = -(-rows // row_multiple) * row_multiple
    return jnp.pad(flat, (0, rows * width - flat.shape[0])).reshape(rows, width)


def _unpack(flat, like):
    out, off = [], 0
    for a in like:
        out.append(flat[off:off + a.size].reshape(a.shape))
        off += a.size
    return out


def kernel(x, w_in, b_merge, conv_w, conv_b, rg_wx, rg_bx, rg_wa, rg_ba, rg_lambda, hg_lb_logits, hg_norm_g, proj_a, proj_b, w_out, norm_g, final_norm_g, loss_target, m_w_in, m_b_merge, m_conv_w, m_conv_b, m_rg_wx, m_rg_bx, m_rg_wa, m_rg_ba, m_rg_lambda, m_hg_lb_logits, m_hg_norm_g, m_proj_a, m_proj_b, m_w_out, m_norm_g, m_final_norm_g, v_w_in, v_b_merge, v_conv_w, v_conv_b, v_rg_wx, v_rg_bx, v_rg_wa, v_rg_ba, v_rg_lambda, v_hg_lb_logits, v_hg_norm_g, v_proj_a, v_proj_b, v_w_out, v_norm_g, v_final_norm_g):
    weights = dict(w_in=w_in, b_merge=b_merge, conv_w=conv_w, conv_b=conv_b, rg_wx=rg_wx, rg_bx=rg_bx, rg_wa=rg_wa,
                   rg_ba=rg_ba, rg_lambda=rg_lambda, hg_lb_logits=hg_lb_logits, hg_norm_g=hg_norm_g, proj_a=proj_a,
                   proj_b=proj_b, w_out=w_out, norm_g=norm_g, final_norm_g=final_norm_g)
    mom1 = dict(w_in=m_w_in, b_merge=m_b_merge, conv_w=m_conv_w, conv_b=m_conv_b, rg_wx=m_rg_wx, rg_bx=m_rg_bx,
                rg_wa=m_rg_wa, rg_ba=m_rg_ba, rg_lambda=m_rg_lambda, hg_lb_logits=m_hg_lb_logits,
                hg_norm_g=m_hg_norm_g, proj_a=m_proj_a, proj_b=m_proj_b, w_out=m_w_out, norm_g=m_norm_g,
                final_norm_g=m_final_norm_g)
    mom2 = dict(w_in=v_w_in, b_merge=v_b_merge, conv_w=v_conv_w, conv_b=v_conv_b, rg_wx=v_rg_wx, rg_bx=v_rg_bx,
                rg_wa=v_rg_wa, rg_ba=v_rg_ba, rg_lambda=v_rg_lambda, hg_lb_logits=v_hg_lb_logits,
                hg_norm_g=v_hg_norm_g, proj_a=v_proj_a, proj_b=v_proj_b, w_out=v_w_out, norm_g=v_norm_g,
                final_norm_g=v_final_norm_g)
    order = list(weights)
    nb, s_len, _ = x.shape
    n = nb * s_len
    px, py, pc = _place()

    in_hbm = lambda a: pltpu.with_memory_space_constraint(a, pltpu.HBM)
    norm_gain = in_hbm(norm_g)

    x2 = x.reshape(n, D)
    cw_blk = jnp.pad(conv_w[0], ((0, 4), (0, 0)))
    order_ids = jnp.stack([_block_id(p) for p in _arrival_order(px, py, pc)]).astype(jnp.int32)
    z, h_all, w_all, pa_all, pb_all, wo_all, cw_all = _gather_inproj(
        order_ids, x2, norm_gain, [w_in[0], proj_a[0], proj_b[0], w_out[0], cw_blk], [BF16, BF16, BF16, BF16, F32])
    pa_full, pb_full, wo_full = (a.reshape(D, D) for a in (pa_all, pb_all, wo_all))
    cw8 = in_hbm(cw_all.transpose(1, 0, 2).reshape(8, D))
    wx_b, wa_b = in_hbm(rg_wx[0].astype(BF16)), in_hbm(rg_wa[0].astype(BF16))
    cb, bx, ba, lam = (in_hbm(a.reshape(1, D)) for a in (conv_b, rg_bx, rg_ba, rg_lambda))
    fin_g, b_mrg = in_hbm(final_norm_g.reshape(1, D)), in_hbm(b_merge)
    lb_lg, hg_g = in_hbm(hg_lb_logits), in_hbm(hg_norm_g)

    hlru, ya = _lru_fwd(z, cw8, cb, wx_b, wa_b, bx, ba, lam, nb, s_len)
    o_all, yb, st_all = _hgrn_fwd(z, lb_lg, hg_g, nb, s_len)

    (dx2, dya, dyb, dzm, loss_acc, g_fin, g_bm, g_mid) = _mid(
        ya, yb, z, b_mrg, x2, loss_target.reshape(n, D), fin_g, pa_full, pb_full, wo_full)
    dzb, g_lg, g_hg = _hgrn_bwd(z, o_all, st_all, dyb, lb_lg, hg_g, nb, s_len)
    dza, g_cw8, g_cb, g_wx, g_wa, g_bx, g_ba, g_lam = _lru_bwd(
        z, hlru, dya, cw8, cb, wx_b, wa_b, bx, ba, lam, nb, s_len)

    part = dict(b_merge=g_bm, conv_b=g_cb, rg_bx=g_bx, rg_ba=g_ba, rg_lambda=g_lam, hg_lb_logits=g_lg,
                hg_norm_g=g_hg, final_norm_g=g_fin)
    vec = _pack_rows([part[k] for k in VEC_NAMES], BD)
    vec = jnp.pad(vec, ((0, 16 * NB - vec.shape[0]), (0, 0))).reshape(NB, 2, D)
    rows8 = lambda a: jnp.pad(a, ((0, 0), (0, 8 - a.shape[1]), (0, 0)))
    small = jnp.concatenate([g_wx.reshape(NB, 16, D), g_wa.reshape(NB, 16, D),
                             rows8(g_cw8.reshape(8, NB, BD).transpose(1, 0, 2).reshape(NB, 1, D)), rows8(vec),
                             jnp.zeros((NB, MID_ROWS - SMALL_AT - SMALL_ROWS, D), F32)], axis=1)
    g_m = lax.dynamic_update_slice(g_mid, small, (0, SMALL_AT, 0))
    w_out_bf, w_own, m_out_bf, m_own, _, _ = _inproj_bwd_w(pc, dza, dzb, dzm, h_all, g_m)
    outgoing = [w_out_bf, m_out_bf]
    chip_sums = _split_start("rs_chips_start", _chip_copies, 3, outgoing, [lax.empty(a.shape, a.dtype) for a in outgoing])
    grad_x, g_ng = _inproj_bwd_x(dza, dzb, dzm, w_all, x2, dx2, norm_gain, chip_sums[-1])
    from_chips = _split_wait("rs_chips_wait", _chip_copies, chip_sums, grad_x)
    r_w = _add_chips(w_own, from_chips[0])
    r_m = _add_chips(m_own, from_chips[1])
    row = lax.broadcasted_iota(jnp.int32, (8, D), 0)
    mine = jnp.where(row == 0, g_ng, jnp.where(row == 1, loss_acc[0:1, 0:1], 0.0))
    tail = jnp.concatenate([r_m[SMALL_AT:SMALL_AT + SMALL_ROWS], mine], axis=0)
    (tail_all,) = _allgather([tail], [F32], "gather_small_grads")
    summed = _sum_blocks(tail_all[:, SMALL_ROWS:SMALL_ROWS + 8])

    grads = dict(w_in=r_w.reshape(1, D, D),
                 proj_a=r_m[0:BD].reshape(1, BD, D), proj_b=r_m[BD:2 * BD].reshape(1, BD, D),
                 w_out=r_m[2 * BD:3 * BD].reshape(1, BD, D),
                 conv_w=r_m[SMALL_AT + 32].reshape(8, BD)[0:4].reshape(1, 4, BD),
                 rg_wx=tail_all[:, 0:16].reshape(1, NB, BD, BD), rg_wa=tail_all[:, 16:32].reshape(1, NB, BD, BD),
                 norm_g=summed[0:1])
    vec_all = tail_all[:, 40:42].reshape(-1)
    for k, gk in zip(VEC_NAMES, _unpack(vec_all, [weights[k] for k in VEC_NAMES])):
        grads[k] = gk

    delta, new_m, new_v = {}, {}, {}
    flat2 = lambda a: a.reshape(-1, a.shape[-1])
    for k in ("w_in", "proj_a", "proj_b", "w_out"):
        outs = _adamw(*[flat2(t[k]) for t in (weights, grads, mom1, mom2)])
        delta[k], new_m[k], new_v[k] = (a.reshape(weights[k].shape) for a in outs)
    rep = list(REP_NAMES) + ["conv_w"]
    outs = _adamw_small(*[[flat2(t[k]) for k in rep] for t in (weights, grads, mom1, mom2)])
    for tgt, arrays in zip((delta, new_m, new_v), outs):
        for k, a in zip(rep, arrays):
            tgt[k] = a.reshape(weights[k].shape)

    return (summed[1, 0], grad_x.reshape(x.shape), *[grads[k] for k in order], *[delta[k] for k in order],
            *[new_m[k] for k in order], *[new_v[k] for k in order])
```

```python
import functools

import jax
import jax.numpy as jnp
from jax import lax
from jax.experimental import pallas as pl
from jax.experimental.pallas import tpu as pltpu

F32 = jnp.float32
BF16 = jnp.bfloat16
SDS = jax.ShapeDtypeStruct
MESH = pl.DeviceIdType.MESH
ANY = pl.BlockSpec(memory_space=pl.ANY)

D = 1024
NB = 8
BD = D // NB
CHUNK = 64
EPS = 1e-6
LRU_C = 8.0
HG_SCALE = BD ** -0.5
ADAM_LR, ADAM_B1, ADAM_B2, ADAM_EPS, ADAM_WD, ADAM_STEP = 0.001, 0.9, 0.999, 1e-08, 0.01, 10

NT_DIMS = (((1,), (1,)), ((), ()))
TN_DIMS = (((0,), (0,)), ((), ()))


def _params(vmem_mib):
    return pltpu.CompilerParams(vmem_limit_bytes=vmem_mib << 20)


def _row_tile(rows, most=256):
    assert rows % 8 == 0
    return max(t for t in range(8, min(rows, most) + 1, 8) if rows % t == 0)


def _sigmoid(v):
    return 0.5 * (jnp.tanh(0.5 * v) + 1.0)


def _groups(v):
    return v.reshape(v.shape[0] // 8, 8, v.shape[1])


def _softplus_neg(lam):
    t = -lam
    e = jnp.exp(-jnp.abs(t))
    w = 1.0 + e
    d = w - 1.0
    l1p = jnp.where(d == 0.0, e, jnp.log(w) * (e / jnp.where(d == 0.0, 1.0, d)))
    return jnp.maximum(t, 0.0) + l1p


def _place():
    return lax.axis_index("x"), lax.axis_index("y"), lax.axis_index("c")


def _other_chips(x, y):
    return [(1 - x, y), (x, 1 - y), (1 - x, 1 - y)]


def _block_id(p):
    return 4 * p[0] + 2 * p[1] + p[2]


def _core_chips(x, y, c):
    near, far, diag = _other_chips(x, y)
    pick = lambda a, b: (jnp.where(c == 0, a[0], b[0]), jnp.where(c == 0, a[1], b[1]))
    return [pick(near, far), pick(far, near), diag]


def _arrival_order(x, y, c):
    first, second, diag = _core_chips(x, y, c)
    return [(x, y, c), (x, y, 1 - c), (*first, c), (*second, 1 - c), (*second, c), (*first, 1 - c),
            (*diag, c), (*diag, 1 - c)]


def _gather_inproj(order_ids, x2, norm_g, blocks, dtypes):
    na = len(blocks)
    n = x2.shape[0]
    tm = min(n, 1024)
    ni = n // tm

    def body(order_ref, x_ref, g_ref, *refs):
        ins, (z_ref, h_ref), outs = refs[:na], refs[na:na + 2], refs[na + 2:2 * na + 2]
        stages = refs[2 * na + 2:3 * na + 2]
        h_full, wbuf, send_sems, recv_sems, local_sems, wsems, hsem = refs[3 * na + 2:]
        j, i = pl.program_id(0), pl.program_id(1)
        x, y, c = _place()
        me, sibling = (x, y, c), (x, y, 1 - c)
        chips = _core_chips(x, y, c)
        sibling_chips = [chips[1], chips[0], chips[2]]
        small = range(1, na)

        def copy(a, k, block, to, src=None):
            return pltpu.make_async_remote_copy(
                src_ref=outs[a].at[_block_id(block)] if src is None else src, dst_ref=outs[a].at[_block_id(block)],
                send_sem=send_sems.at[7 * a + k], recv_sem=recv_sems.at[7 * a + k],
                device_id=to, device_id_type=MESH)

        def local(a):
            return pltpu.make_async_copy(stages[a], outs[a].at[_block_id(me)], local_sems.at[a])

        def landed(a, slot):
            copy(a, 1 + slot, (*chips[slot], c), me).wait_recv()
            copy(a, 4 + slot, (*chips[slot], c), sibling).start()
            if slot == 0:
                copy(a, 3, (*chips[0], c), (*chips[1], c)).start()

        def diagonal_and_small():
            landed(0, 2)
            for a in small:
                landed(a, 0)
                landed(a, 1)

        def passed_on(a, slot):
            copy(a, 4 + slot, (*sibling_chips[slot], 1 - c), me).wait_recv()

        def sibling_here_send_second():
            copy(0, 0, sibling, me).wait_recv()
            for a in range(na):
                copy(a, 2, me, (*chips[1], c), src=stages[a]).start()

        @pl.when((j == 0) & (i == 0))
        def _():
            for a in range(na):
                stages[a][...] = ins[a][...].astype(dtypes[a])
                local(a).start()
            for a in range(na):
                copy(a, 0, me, sibling, src=stages[a]).start()
                copy(a, 1, me, (*chips[0], c), src=stages[a]).start()

        @pl.when(j == 0)
        def _():
            xv = x_ref[...]
            r = lax.rsqrt(jnp.mean(xv * xv, axis=-1, keepdims=True) + EPS)
            hb = ((xv * r) * g_ref[...]).astype(BF16)
            h_full[pl.ds(pl.multiple_of(i * tm, tm), tm), :] = hb

        save_h = pltpu.make_async_copy(h_full, h_ref, hsem)
        pl.when((j == 0) & (i == ni - 1))(save_h.start)

        steps = [
            lambda: local(0).wait(),
            sibling_here_send_second,
            lambda: landed(0, 0),
            lambda: passed_on(0, 0),
            lambda: landed(0, 1),
            lambda: passed_on(0, 1),
            diagonal_and_small,
            lambda: passed_on(0, 2),
        ]
        def w_load(k):
            return pltpu.make_async_copy(outs[0].at[order_ref[k]], wbuf.at[k % 2], wsems.at[k % 2])

        for k, step in enumerate(steps):
            @pl.when((j == 0) & (i == 0) if k == 0 else (j == k - 1) & (i == ni - 1))
            def _(k=k, step=step):
                step()
                w_load(k).start()

        pl.when(i == 0)(lambda: w_load(j).wait())
        z_ref[0] = jnp.dot(h_full[pl.ds(pl.multiple_of(i * tm, tm), tm), :], wbuf[j % 2], preferred_element_type=F32)

        @pl.when((j == NB - 1) & (i == ni - 1))
        def _():
            save_h.wait()
            for a in small:
                landed(a, 2)
            for a in small:
                local(a).wait()
                copy(a, 0, sibling, me).wait_recv()
                for slot in range(3):
                    passed_on(a, slot)
            for a in range(na):
                copy(a, 0, me, sibling, src=stages[a]).wait_send()
                for slot, chip in enumerate(chips):
                    copy(a, 1 + slot, me, (*chip, c), src=stages[a]).wait_send()
                    copy(a, 4 + slot, (*chip, c), sibling).wait_send()

    rows_once = lambda j, i, order: (jnp.where(j == 0, i, ni - 1), 0)
    vmem = pl.BlockSpec(memory_space=pltpu.VMEM)
    return pl.pallas_call(
        body, name="gather_inproj",
        grid_spec=pltpu.PrefetchScalarGridSpec(
            num_scalar_prefetch=1, grid=(NB, ni),
            in_specs=[pl.BlockSpec((tm, D), rows_once), pl.BlockSpec((1, D), lambda j, i, order: (0, 0))] + [vmem] * na,
            out_specs=[pl.BlockSpec((1, tm, D), lambda j, i, order: (order[j], i, 0)), ANY] + [ANY] * na,
            scratch_shapes=[pltpu.VMEM(b.shape, dt) for b, dt in zip(blocks, dtypes)]
            + [pltpu.VMEM((n, D), BF16), pltpu.VMEM((2, D, D), BF16),
               pltpu.SemaphoreType.DMA((7 * na,)), pltpu.SemaphoreType.DMA((7 * na,)),
               pltpu.SemaphoreType.DMA((na,)), pltpu.SemaphoreType.DMA((2,)), pltpu.SemaphoreType.DMA(())]),
        out_shape=[SDS((NB, n, D), F32), SDS((n, D), BF16)] + [SDS((NB,) + b.shape, dt) for b, dt in zip(blocks, dtypes)],
        compiler_params=_params(56),
    )(order_ids, x2, norm_g, *blocks)


LRU_T = 256


def _shifted(groups, shifts):
    row = lax.broadcasted_iota(jnp.int32, (groups.shape[0] - 1,) + groups.shape[1:], 1)
    out = []
    for s in shifts:
        y = pltpu.roll(groups, s % 8, 1)
        moved = jnp.where(row >= s, y[1:], y[:-1]) if s > 0 else jnp.where(row < 8 + s, y[:-1], y[1:])
        out.append(moved.reshape(-1, groups.shape[2]))
    return out


def _conv(taps, cw, cb):
    acc = taps[0] * cw[0:1, :] + taps[1] * cw[1:2, :]
    acc = acc + taps[2] * cw[2:3, :]
    acc = acc + taps[3] * cw[3:4, :]
    return cb + acc


def _lru_gates(xa, wx_ref, wa_ref, bx, ba, lam):
    xab = xa.astype(BF16)
    pis, prs = [], []
    for h in range(NB):
        xs = xab[:, h * BD:(h + 1) * BD]
        pis.append(jnp.dot(xs, wx_ref[h], preferred_element_type=F32))
        prs.append(jnp.dot(xs, wa_ref[h], preferred_element_type=F32))
    gi = _sigmoid(jnp.concatenate(pis, axis=1) + bx)
    gr = _sigmoid(jnp.concatenate(prs, axis=1) + ba)
    sp = _softplus_neg(lam)
    log_a = (-LRU_C * gr) * sp
    a = jnp.exp(log_a)
    mult = jnp.sqrt(-jnp.tanh(log_a) * (a * a + 1.0))
    return xab, gi, gr, sp, a, mult


def _lru_fwd(z, cw8, cb, wx, wa, bx, ba, lam, nb, s_len):
    n = nb * s_len
    t = LRU_T
    ns = s_len // t

    def body(xp_ref, ga_ref, cw_ref, cb_ref, wx_ref, wa_ref, bx_ref, ba_ref, lam_ref,
             h_ref, ya_ref, ext, a_s, u_s, carry):
        @pl.when(pl.program_id(1) == 0)
        def _():
            ext[0:8, :] = jnp.zeros((8, D), F32)
            carry[...] = jnp.zeros((8, D), F32)

        xp = xp_ref[0]
        ext[8:8 + t, :] = xp
        xa = _conv(_shifted(_groups(ext[...]), (3, 2, 1)) + [xp], cw_ref[...], cb_ref[...])
        ext[0:8, :] = xp[t - 8:t, :]
        _, gi, _, _, a, mult = _lru_gates(xa, wx_ref, wa_ref, bx_ref[...], ba_ref[...], lam_ref[...])
        u = (mult * gi) * xa
        a, u = _groups(a), _groups(u)
        row = lax.broadcasted_iota(jnp.int32, a.shape, 1)
        for sh in (1, 2, 4):
            a_sh = pltpu.roll(a, sh, 1)
            u_sh = pltpu.roll(u, sh, 1)
            m = row >= sh
            u = jnp.where(m, a * u_sh + u, u)
            a = jnp.where(m, a * a_sh, a)
        a_s[...] = a.reshape(t, D)
        u_s[...] = u.reshape(t, D)

        def step(g, c):
            r = pl.multiple_of(g * 8, 8)
            hg = u_s[pl.ds(r, 8), :] + a_s[pl.ds(r, 8), :] * c
            h_ref[pl.ds(r, 8), :] = hg
            return hg[7:8, :]

        c_out = lax.fori_loop(0, t // 8, step, carry[0:1, :], unroll=4)
        carry[0:1, :] = c_out
        ga = ga_ref[0]
        ya_ref[...] = (h_ref[...] * (ga * _sigmoid(ga))).astype(BF16)

    row_map = lambda b, s: (b * ns + s, 0)
    rep2 = lambda b, s: (0, 0)
    rep3 = lambda b, s: (0, 0, 0)
    return pl.pallas_call(
        body, name="lru_fwd", grid=(nb, ns),
        in_specs=[pl.BlockSpec((1, t, D), lambda b, s: (0, b * ns + s, 0)),
                  pl.BlockSpec((1, t, D), lambda b, s: (1, b * ns + s, 0)),
                  pl.BlockSpec((8, D), rep2), pl.BlockSpec((1, D), rep2),
                  pl.BlockSpec((NB, BD, BD), rep3), pl.BlockSpec((NB, BD, BD), rep3),
                  pl.BlockSpec((1, D), rep2), pl.BlockSpec((1, D), rep2), pl.BlockSpec((1, D), rep2)],
        out_specs=[pl.BlockSpec((t, D), row_map), pl.BlockSpec((t, D), row_map)],
        out_shape=[SDS((n, D), F32), SDS((n, D), BF16)],
        scratch_shapes=[pltpu.VMEM((t + 8, D), F32), pltpu.VMEM((t, D), F32), pltpu.VMEM((t, D), F32),
                        pltpu.VMEM((8, D), F32)],
        compiler_params=_params(48),
    )(z, z, cw8, cb, wx, wa, bx, ba, lam)


def _lru_bwd(z, h_all, dya, cw8, cb, wx, wa, bx, ba, lam, nb, s_len):
    n = nb * s_len
    t = LRU_T
    ns = s_len // t
    t8 = t // 8

    def body(xp_ref, xph_ref, ga_ref, h_ref, hh_ref, dya_ref, cw_ref, cb_ref, wx_ref, wa_ref, bx_ref, ba_ref,
             lam_ref, dz_ref, gcw_ref, gcb_ref, gwx_ref, gwa_ref, gbx_ref, gba_ref, glam_ref,
             ext, hext, dext, a_s, u_s, dh_s, carry):
        b, s = pl.program_id(0), pl.program_id(1)
        first_tile = s == ns - 1

        @pl.when((b == 0) & (s == 0))
        def _():
            for ref in (gcw_ref, gcb_ref, gwx_ref, gwa_ref, gbx_ref, gba_ref, glam_ref):
                ref[...] = jnp.zeros(ref.shape, F32)

        @pl.when(s == 0)
        def _():
            dext[t:t + 8, :] = jnp.zeros((8, D), F32)
            carry[...] = jnp.zeros((8, D), F32)

        keep = jnp.where(first_tile, 0.0, 1.0)
        xp = xp_ref[0]
        ext[0:8, :] = xph_ref[0] * keep
        ext[8:8 + t, :] = xp
        hext[0:8, :] = hh_ref[...] * keep
        hext[8:8 + t, :] = h_ref[...]
        cw = cw_ref[...]
        lam = lam_ref[...]
        taps = _shifted(_groups(ext[...]), (3, 2, 1)) + [xp]
        xa = _conv(taps, cw, cb_ref[...])
        xab, gi, gr, sp, a, mult = _lru_gates(xa, wx_ref, wa_ref, bx_ref[...], ba_ref[...], lam)
        (h_prev,) = _shifted(_groups(hext[...]), (1,))
        ga = ga_ref[0]
        sg = _sigmoid(ga)
        dya_v = dya_ref[...]
        d_ga = dya_v * h_ref[...] * (sg * (1.0 + ga * (1.0 - sg)))
        g_in = dya_v * (ga * sg)

        (an,) = _shifted(jnp.concatenate([_groups(a), jnp.ones((1, 8, D), F32)], axis=0), (-1,))
        an, u = _groups(an), _groups(g_in)
        row = lax.broadcasted_iota(jnp.int32, an.shape, 1)
        for sh in (1, 2, 4):
            a_sh = pltpu.roll(an, 8 - sh, 1)
            u_sh = pltpu.roll(u, 8 - sh, 1)
            m = row < 8 - sh
            u = jnp.where(m, u + an * u_sh, u)
            an = jnp.where(m, an * a_sh, an)
        a_s[...] = an.reshape(t, D)
        u_s[...] = u.reshape(t, D)

        def step(i, c):
            r = pl.multiple_of((t8 - 1 - i) * 8, 8)
            dg = u_s[pl.ds(r, 8), :] + a_s[pl.ds(r, 8), :] * c
            dh_s[pl.ds(r, 8), :] = dg
            return dg[0:1, :]

        lax.fori_loop(0, t8, step, carry[0:1, :], unroll=4)
        dh = dh_s[...]
        carry[0:1, :] = a[0:1, :] * dh[0:1, :]

        d_a = dh * h_prev
        dux = dh * xa
        d_mult = dux * gi
        d_gi = dux * mult
        d_xa = dh * (mult * gi)
        d_loga = d_a * a - d_mult * ((a * a) / mult)
        d_gr = d_loga * (-LRU_C * sp)
        d_sp = jnp.sum(d_loga * (-LRU_C * gr), axis=0, keepdims=True)
        glam_ref[...] += d_sp * (-_sigmoid(-lam))
        d_pi = d_gi * gi * (1.0 - gi)
        d_pr = d_gr * gr * (1.0 - gr)
        gbx_ref[...] += jnp.sum(d_pi, axis=0, keepdims=True)
        gba_ref[...] += jnp.sum(d_pr, axis=0, keepdims=True)
        dpib = d_pi.astype(BF16)
        dprb = d_pr.astype(BF16)
        back = []
        for h in range(NB):
            cs = slice(h * BD, (h + 1) * BD)
            gwx_ref[h] += lax.dot_general(xab[:, cs], dpib[:, cs], TN_DIMS, preferred_element_type=F32)
            gwa_ref[h] += lax.dot_general(xab[:, cs], dprb[:, cs], TN_DIMS, preferred_element_type=F32)
            back.append(lax.dot_general(dpib[:, cs], wx_ref[h], NT_DIMS, preferred_element_type=F32)
                        + lax.dot_general(dprb[:, cs], wa_ref[h], NT_DIMS, preferred_element_type=F32))
        d_xa = d_xa + jnp.concatenate(back, axis=1)

        dext[0:t, :] = d_xa
        later = _shifted(_groups(dext[...]), (-3, -2, -1))
        d_xp = later[0] * cw[0:1, :] + later[1] * cw[1:2, :]
        d_xp = d_xp + later[2] * cw[2:3, :]
        d_xp = d_xp + d_xa * cw[3:4, :]
        dext[t:t + 8, :] = d_xa[0:8, :]
        gcb_ref[...] += jnp.sum(d_xa, axis=0, keepdims=True)
        for k in range(4):
            gcw_ref[k:k + 1, :] += jnp.sum(d_xa * taps[k], axis=0, keepdims=True)
        dz_ref[0] = d_xp.astype(BF16)
        dz_ref[1] = d_ga.astype(BF16)

    rb = lambda b, s: b * ns + (ns - 1 - s)
    halo = lambda b, s: jnp.maximum(rb(b, s) * t8 - 1, 0)
    rep2 = lambda b, s: (0, 0)
    rep3 = lambda b, s: (0, 0, 0)
    return pl.pallas_call(
        body, name="lru_bwd", grid=(nb, ns),
        in_specs=[pl.BlockSpec((1, t, D), lambda b, s: (0, rb(b, s), 0)),
                  pl.BlockSpec((1, 8, D), lambda b, s: (0, halo(b, s), 0)),
                  pl.BlockSpec((1, t, D), lambda b, s: (1, rb(b, s), 0)),
                  pl.BlockSpec((t, D), lambda b, s: (rb(b, s), 0)),
                  pl.BlockSpec((8, D), lambda b, s: (halo(b, s), 0)),
                  pl.BlockSpec((t, D), lambda b, s: (rb(b, s), 0)),
                  pl.BlockSpec((8, D), rep2), pl.BlockSpec((1, D), rep2),
                  pl.BlockSpec((NB, BD, BD), rep3), pl.BlockSpec((NB, BD, BD), rep3),
                  pl.BlockSpec((1, D), rep2), pl.BlockSpec((1, D), rep2), pl.BlockSpec((1, D), rep2)],
        out_specs=[pl.BlockSpec((2, t, D), lambda b, s: (0, rb(b, s), 0)),
                   pl.BlockSpec((8, D), rep2), pl.BlockSpec((1, D), rep2),
                   pl.BlockSpec((NB, BD, BD), rep3), pl.BlockSpec((NB, BD, BD), rep3),
                   pl.BlockSpec((1, D), rep2), pl.BlockSpec((1, D), rep2), pl.BlockSpec((1, D), rep2)],
        out_shape=[SDS((2, n, D), BF16), SDS((8, D), F32), SDS((1, D), F32),
                   SDS((NB, BD, BD), F32), SDS((NB, BD, BD), F32),
                   SDS((1, D), F32), SDS((1, D), F32), SDS((1, D), F32)],
        scratch_shapes=[pltpu.VMEM((t + 8, D), F32), pltpu.VMEM((t + 8, D), F32), pltpu.VMEM((t + 8, D), F32),
                        pltpu.VMEM((t, D), F32), pltpu.VMEM((t, D), F32), pltpu.VMEM((t, D), F32),
                        pltpu.VMEM((8, D), F32)],
        compiler_params=_params(56),
    )(z, z, z, h_all, h_all, dya, cw8, cb, wx, wa, bx, ba, lam)


HG_T = 512
HG_NC = HG_T // CHUNK
BNT_DIMS = (((2,), (2,)), ((0,), (0,)))
BNN_DIMS = (((2,), (1,)), ((0,), (0,)))
BTN_DIMS = (((1,), (1,)), ((0,), (0,)))


def _lower_bound(lg):
    m = jnp.max(lg, axis=0, keepdims=True)
    e = jnp.exp(lg - m)
    return e[0:1, :] / jnp.sum(e, axis=0, keepdims=True)


def _tri(upper):
    r = lax.broadcasted_iota(jnp.int32, (HG_NC, CHUNK, CHUNK), 1)
    c = lax.broadcasted_iota(jnp.int32, (HG_NC, CHUNK, CHUNK), 2)
    return (c >= r) if upper else (r >= c)


def _bdot(a, b, dims):
    return lax.dot_general(a, b, dims, preferred_element_type=F32)


def _tri_sums(upper, a):
    tri = _tri(upper).astype(BF16)
    a1 = a.astype(BF16)
    r1 = a - a1.astype(F32)
    a2 = r1.astype(BF16)
    a3 = (r1 - a2.astype(F32)).astype(BF16)
    return _bdot(tri, a1, BNN_DIMS) + (_bdot(tri, a2, BNN_DIMS) + _bdot(tri, a3, BNN_DIMS))


def _chunks(a):
    return a.reshape(HG_NC, CHUNK, BD)


def _hg_tile(q, fp, lb):
    q, fp = _chunks(q), _chunks(fp)
    sig = _sigmoid(fp)
    f = lb + (1.0 - lb) * sig
    log_f = jnp.log(f)
    k = 1.0 - f
    b = _tri_sums(False, log_f)
    b_mid = b[:, CHUNK // 2:CHUNK // 2 + 1, :]
    b_last = b[:, CHUNK - 1:CHUNK, :]
    sq = _sigmoid(q)
    qh = q * sq
    e_qi = jnp.exp(b - b_mid)
    e_ki = jnp.exp(b_mid - b)
    e_qs = jnp.exp(b)
    e_ks = jnp.exp(b_last - b)
    dc = jnp.exp(b_last)
    q_in = (qh * e_qi) * HG_SCALE
    k_in = k * e_ki
    q_st = (qh * e_qs) * HG_SCALE
    k_st = k * e_ks
    att = _bdot(q_in.astype(BF16), k_in.astype(BF16), BNT_DIMS)
    att = jnp.where(_tri(False), att, 0.0)
    return dict(q=q, sig=sig, f=f, k=k, sq=sq, e_qi=e_qi, e_ki=e_ki, e_qs=e_qs, e_ks=e_ks, dc=dc,
                q_in=q_in, k_in=k_in, q_st=q_st, k_st=k_st, att=att)


def _hgrn_fwd(z, lb_logits, hg_g, nb, s_len):
    n = nb * s_len
    t = HG_T
    ns = s_len // t
    nchunk = s_len // CHUNK

    def body(q_ref, f_ref, v_ref, gb_ref, lg_ref, g_ref, o_ref, yb_ref, st_ref, st):
        @pl.when(pl.program_id(1) == 0)
        def _():
            st[...] = jnp.zeros((NB, BD, BD), F32)

        def head(h, carry):
            cols = pl.ds(pl.multiple_of(h * BD, BD), BD)
            lb = _lower_bound(lg_ref[:, cols])
            ck = _hg_tile(q_ref[0, :, cols], f_ref[0, :, cols], lb)
            vb = _chunks(v_ref[0, :, cols]).astype(BF16)
            kv = _bdot(vb, ck["k_st"].astype(BF16), BTN_DIMS)
            states = [st[h]]
            for c in range(HG_NC):
                states.append(states[c] * ck["dc"][c] + kv[c])
            st[h] = states[HG_NC]
            s_in = jnp.stack(states[:HG_NC], axis=0)
            st_ref[h] = s_in
            o = (_bdot(ck["att"].astype(BF16), vb, BNN_DIMS)
                 + _bdot(ck["q_st"].astype(BF16), s_in.astype(BF16), BNT_DIMS))
            o_ref[:, cols] = o.reshape(t, BD)
            r = lax.rsqrt(jnp.mean(o * o, axis=-1, keepdims=True) + EPS)
            gb = _chunks(gb_ref[0, :, cols])
            yb_ref[:, cols] = (((o * r) * g_ref[...]) * (gb * _sigmoid(gb))).astype(BF16).reshape(t, BD)
            return carry

        lax.fori_loop(0, NB, head, 0, unroll=4)

    seg = lambda j: pl.BlockSpec((1, t, D), lambda b, s: (j, b * ns + s, 0))
    tile = pl.BlockSpec((t, D), lambda b, s: (b * ns + s, 0))
    return pl.pallas_call(
        body, name="hgrn_fwd", grid=(nb, ns),
        in_specs=[seg(2), seg(3), seg(4), seg(5),
                  pl.BlockSpec((2, D), lambda b, s: (0, 0)), pl.BlockSpec((1, BD), lambda b, s: (0, 0))],
        out_specs=[tile, tile, pl.BlockSpec((NB, HG_NC, BD, BD), lambda b, s: (b, s, 0, 0))],
        out_shape=[SDS((n, D), F32), SDS((n, D), BF16), SDS((nb * NB, nchunk, BD, BD), F32)],
        scratch_shapes=[pltpu.VMEM((NB, BD, BD), F32)],
        compiler_params=_params(56),
    )(z, z, z, z, lb_logits, hg_g)


def _hgrn_bwd(z, o_all, st_all, dyb, lb_logits, hg_g, nb, s_len):
    n = nb * s_len
    t = HG_T
    ns = s_len // t

    def body(q_ref, f_ref, v_ref, gb_ref, o_ref, st_ref, dyb_ref, lg_ref, g_ref,
             dz_ref, glg_ref, ghg_ref, dst, dlb):
        b, s = pl.program_id(0), pl.program_id(1)

        @pl.when((b == 0) & (s == 0))
        def _():
            ghg_ref[...] = jnp.zeros((1, BD), F32)
            dlb[...] = jnp.zeros((8, D), F32)

        @pl.when(s == 0)
        def _():
            dst[...] = jnp.zeros((NB, BD, BD), F32)

        g = g_ref[...]

        def head(h, carry):
            cols = pl.ds(pl.multiple_of(h * BD, BD), BD)
            lb = _lower_bound(lg_ref[:, cols])
            ck = _hg_tile(q_ref[0, :, cols], f_ref[0, :, cols], lb)
            q = ck["q"]
            vb = _chunks(v_ref[0, :, cols]).astype(BF16)
            gb = _chunks(gb_ref[0, :, cols])
            o = _chunks(o_ref[:, cols])
            dyb_v = _chunks(dyb_ref[:, cols])
            s_in = st_ref[h]

            sgb = _sigmoid(gb)
            r = lax.rsqrt(jnp.mean(o * o, axis=-1, keepdims=True) + EPS)
            ohat = o * r
            d_on = dyb_v * (gb * sgb)
            d_gb = dyb_v * (ohat * g) * (sgb * (1.0 + gb * (1.0 - sgb)))
            ghg_ref[...] += jnp.sum(jnp.sum(d_on * ohat, axis=1), axis=0, keepdims=True)
            tt = d_on * g
            d_o = r * (tt - ohat * jnp.mean(tt * ohat, axis=-1, keepdims=True))
            dob = d_o.astype(BF16)

            attb = ck["att"].astype(BF16)
            q_inb, k_inb = ck["q_in"].astype(BF16), ck["k_in"].astype(BF16)
            q_stb, k_stb = ck["q_st"].astype(BF16), ck["k_st"].astype(BF16)
            d_att = jnp.where(_tri(False), _bdot(dob, vb, BNT_DIMS), 0.0).astype(BF16)
            d_q_in = _bdot(d_att, k_inb, BNN_DIMS)
            d_k_in = _bdot(d_att, q_inb, BTN_DIMS)
            d_q_st = _bdot(dob, s_in.astype(BF16), BNN_DIMS)
            qdo = _bdot(dob, q_stb, BTN_DIMS)
            d_states = [None] * HG_NC + [dst[h]]
            for c in reversed(range(HG_NC)):
                d_states[c] = d_states[c + 1] * ck["dc"][c] + qdo[c]
            dst[h] = d_states[0]
            ds_out = jnp.stack(d_states[1:], axis=0)
            dsb = ds_out.astype(BF16)
            d_v = _bdot(attb, dob, BTN_DIMS) + _bdot(k_stb, dsb, BNT_DIMS)
            d_k_st = _bdot(vb, dsb, BNN_DIMS)
            d_dc = jnp.sum(ds_out * s_in, axis=1, keepdims=True)

            p_qi = d_q_in * ck["q_in"]
            p_ki = d_k_in * ck["k_in"]
            p_qs = d_q_st * ck["q_st"]
            p_ks = d_k_st * ck["k_st"]
            d_qh = (d_q_in * ck["e_qi"] + d_q_st * ck["e_qs"]) * HG_SCALE
            d_k = d_k_in * ck["e_ki"] + d_k_st * ck["e_ks"]
            d_b = (p_qi - p_ki) + (p_qs - p_ks)
            d_b_mid = jnp.sum(p_ki - p_qi, axis=1, keepdims=True)
            d_b_last = jnp.sum(p_ks, axis=1, keepdims=True) + d_dc * ck["dc"]
            rowi = lax.broadcasted_iota(jnp.int32, (HG_NC, CHUNK, BD), 1)
            d_b = d_b + jnp.where(rowi == CHUNK // 2, d_b_mid, 0.0) + jnp.where(rowi == CHUNK - 1, d_b_last, 0.0)
            d_logf = _tri_sums(True, d_b)
            d_f = d_logf / ck["f"] - d_k
            sig, sq = ck["sig"], ck["sq"]
            d_fp = d_f * (1.0 - lb) * (sig * (1.0 - sig))
            dlb[0:1, cols] += jnp.sum(jnp.sum(d_f * (1.0 - sig), axis=1), axis=0, keepdims=True)
            d_q = d_qh * (sq * (1.0 + q * (1.0 - sq)))
            dz_ref[0, :, cols] = d_q.astype(BF16).reshape(t, BD)
            dz_ref[1, :, cols] = d_fp.astype(BF16).reshape(t, BD)
            dz_ref[2, :, cols] = d_v.astype(BF16).reshape(t, BD)
            dz_ref[3, :, cols] = d_gb.astype(BF16).reshape(t, BD)
            return carry

        lax.fori_loop(0, NB, head, 0, unroll=2)

        @pl.when((b == nb - 1) & (s == ns - 1))
        def _():
            lb = _lower_bound(lg_ref[...])
            dl = dlb[0:1, :] * (lb * (1.0 - lb))
            glg_ref[0:1, :] = dl
            glg_ref[1:2, :] = -dl

    rb = lambda b, s: b * ns + (ns - 1 - s)
    seg = lambda j: pl.BlockSpec((1, t, D), lambda b, s: (j, rb(b, s), 0))
    tile = pl.BlockSpec((t, D), lambda b, s: (rb(b, s), 0))
    return pl.pallas_call(
        body, name="hgrn_bwd", grid=(nb, ns),
        in_specs=[seg(2), seg(3), seg(4), seg(5), tile,
                  pl.BlockSpec((NB, HG_NC, BD, BD), lambda b, s: (b, ns - 1 - s, 0, 0)),
                  tile, pl.BlockSpec((2, D), lambda b, s: (0, 0)), pl.BlockSpec((1, BD), lambda b, s: (0, 0))],
        out_specs=[pl.BlockSpec((4, t, D), lambda b, s: (0, rb(b, s), 0)),
                   pl.BlockSpec((2, D), lambda b, s: (0, 0)), pl.BlockSpec((1, BD), lambda b, s: (0, 0))],
        out_shape=[SDS((4, n, D), BF16), SDS((2, D), F32), SDS((1, BD), F32)],
        scratch_shapes=[pltpu.VMEM((NB, BD, BD), F32), pltpu.VMEM((8, D), F32)],
        compiler_params=_params(60),
    )(z, z, z, z, o_all, st_all, dyb, lb_logits, hg_g)


def _mid(ya, yb, z, b_merge, x2, tgt, fin_g, pa, pb, wo):
    n = x2.shape[0]
    tm = 256
    ni = n // tm

    def body(ya_ref, yb_ref, gma_ref, gmb_ref, bm_ref, x_ref, t_ref, fg_ref, pa_hbm, pb_hbm, wo_hbm,
             dx2_ref, dya_ref, dyb_ref, dgm_ref, loss_ref, gfg_ref, gbm_ref, gm_hbm,
             pa_v, pb_v, wo_v, gpa_v, gpb_v, gwo_v, sem):
        i = pl.program_id(0)
        by_owner = lambda g: g.reshape(NB, BD, D)
        loads = [pltpu.make_async_copy(src, dst, sem.at[k])
                 for k, (src, dst) in enumerate(((pa_hbm, pa_v), (pb_hbm, pb_v), (wo_hbm, wo_v)))]
        stores = [pltpu.make_async_copy(src, dst, sem.at[k])
                  for k, (src, dst) in enumerate((g, gm_hbm.at[:, pl.ds(slot * BD, BD), :])
                                                 for slot, g in enumerate((gpa_v, gpb_v, gwo_v)))]

        @pl.when(i == 0)
        def _():
            for cp in loads:
                cp.start()
            for ref in (gpa_v, gpb_v, gwo_v, loss_ref, gfg_ref, gbm_ref):
                ref[...] = jnp.zeros(ref.shape, F32)
            for cp in loads:
                cp.wait()

        ya_v = ya_ref[...]
        yb_v = yb_ref[...]
        out_a = jnp.dot(ya_v, pa_v[...], preferred_element_type=F32)
        out_b = jnp.dot(yb_v, pb_v[...], preferred_element_type=F32)
        bm = bm_ref[...]
        g_a = _sigmoid(gma_ref[0] + bm[:, 0:D])
        g_b = _sigmoid(gmb_ref[0] + bm[:, D:2 * D])
        mixed = g_a * out_a + g_b * out_b
        mixb = mixed.astype(BF16)
        xo = x_ref[...] + jnp.dot(mixb, wo_v[...], preferred_element_type=F32)
        r = lax.rsqrt(jnp.mean(xo * xo, axis=-1, keepdims=True) + EPS)
        xn = xo * r
        fg = fg_ref[...]
        e = xn * fg - t_ref[...]
        loss_ref[...] += 0.5 * jnp.sum(jnp.mean(e * e, axis=-1, keepdims=True))
        dy = e * (1.0 / D)
        gfg_ref[...] += jnp.sum(dy * xn, axis=0, keepdims=True)
        dxn = dy * fg
        dx2 = r * (dxn - xn * jnp.mean(dxn * xn, axis=-1, keepdims=True))
        dx2_ref[...] = dx2
        dx2b = dx2.astype(BF16)
        d_mixed = lax.dot_general(dx2b, wo_v[...], NT_DIMS, preferred_element_type=F32)
        gwo_v[...] += by_owner(lax.dot_general(mixb, dx2b, TN_DIMS, preferred_element_type=F32))
        d_oa = (d_mixed * g_a).astype(BF16)
        d_ob = (d_mixed * g_b).astype(BF16)
        dgm_a = (d_mixed * out_a) * (g_a * (1.0 - g_a))
        dgm_b = (d_mixed * out_b) * (g_b * (1.0 - g_b))
        gbm_ref[:, 0:D] += jnp.sum(dgm_a, axis=0, keepdims=True)
        gbm_ref[:, D:2 * D] += jnp.sum(dgm_b, axis=0, keepdims=True)
        dgm_ref[0] = dgm_a.astype(BF16)
        dgm_ref[1] = dgm_b.astype(BF16)
        dya_ref[...] = lax.dot_general(d_oa, pa_v[...], NT_DIMS, preferred_element_type=F32)
        dyb_ref[...] = lax.dot_general(d_ob, pb_v[...], NT_DIMS, preferred_element_type=F32)
        gpa_v[...] += by_owner(lax.dot_general(ya_v, d_oa, TN_DIMS, preferred_element_type=F32))
        gpb_v[...] += by_owner(lax.dot_general(yb_v, d_ob, TN_DIMS, preferred_element_type=F32))

        @pl.when(i == ni - 1)
        def _():
            for cp in stores:
                cp.start()
            for cp in stores:
                cp.wait()

    rows = pl.BlockSpec((tm, D), lambda i: (i, 0))
    rep = lambda shape: pl.BlockSpec(shape, lambda i: (0,) * len(shape))
    return pl.pallas_call(
        body, name="mid", grid=(ni,),
        in_specs=[rows, rows,
                  pl.BlockSpec((1, tm, D), lambda i: (6, i, 0)), pl.BlockSpec((1, tm, D), lambda i: (7, i, 0)),
                  rep((1, 2 * D)), rows, rows, rep((1, D)), ANY, ANY, ANY],
        out_specs=[rows, rows, rows, pl.BlockSpec((2, tm, D), lambda i: (0, i, 0)),
                   rep((8, BD)), rep((1, D)), rep((1, 2 * D)), ANY],
        out_shape=[SDS((n, D), F32), SDS((n, D), F32), SDS((n, D), F32), SDS((2, n, D), BF16),
                   SDS((8, BD), F32), SDS((1, D), F32), SDS((1, 2 * D), F32),
                   SDS((NB, MID_ROWS, D), F32)],
        scratch_shapes=[pltpu.VMEM((D, D), BF16)] * 3 + [pltpu.VMEM((NB, BD, D), F32)] * 3 + [pltpu.SemaphoreType.DMA((3,))],
        compiler_params=_params(60),
    )(ya, yb, z, z, b_merge, x2, tgt, fin_g, pa, pb, wo)


def _dz_specs(tm, ni, row_major):
    if row_major:
        ia = lambda i, j: (jnp.minimum(j, 1), i, 0)
        ib = lambda i, j: (jnp.clip(j - 2, 0, 3), i, 0)
        im = lambda i, j: (jnp.clip(j - 6, 0, 1), i, 0)
    else:
        last = ni - 1
        ia = lambda j, i: (jnp.minimum(j, 1), jnp.where(j < 2, i, last), 0)
        ib = lambda j, i: (jnp.clip(j - 2, 0, 3), jnp.where(j < 2, 0, jnp.where(j < 6, i, last)), 0)
        im = lambda j, i: (jnp.clip(j - 6, 0, 1), jnp.where(j < 6, 0, i), 0)
    return [pl.BlockSpec((1, tm, D), f) for f in (ia, ib, im)]


def _inproj_bwd_x(dza, dzb, dzm, w_all, x2, dx2, norm_g, after):
    n = x2.shape[0]
    tm = 512
    ni = n // tm

    def body(dza_ref, dzb_ref, dzm_ref, w_ref, x_ref, dx2_ref, g_ref, after_ref, gx_ref, gg_ref, acc):
        i, j = pl.program_id(0), pl.program_id(1)

        @pl.when((i == 0) & (j == 0))
        def _():
            gg_ref[...] = jnp.zeros((1, D), F32)

        @pl.when(j == 0)
        def _():
            acc[...] = jnp.zeros((tm, D), F32)

        def add(ref):
            acc[...] += lax.dot_general(ref[0], w_ref[0], NT_DIMS, preferred_element_type=F32)

        pl.when(j < 2)(lambda: add(dza_ref))
        pl.when((j >= 2) & (j < 6))(lambda: add(dzb_ref))
        pl.when(j >= 6)(lambda: add(dzm_ref))

        @pl.when(j == NB - 1)
        def _():
            x = x_ref[...]
            r = lax.rsqrt(jnp.mean(x * x, axis=-1, keepdims=True) + EPS)
            xn = x * r
            dh = acc[...]
            gg_ref[...] += jnp.sum(dh * xn, axis=0, keepdims=True)
            dxn = dh * g_ref[...]
            gx_ref[...] = dx2_ref[...] + r * (dxn - xn * jnp.mean(dxn * xn, axis=-1, keepdims=True))

    rows = pl.BlockSpec((tm, D), lambda i, j: (i, 0))
    return pl.pallas_call(
        body, name="inproj_bwd_x", grid=(ni, NB),
        in_specs=_dz_specs(tm, ni, True) + [pl.BlockSpec((1, D, D), lambda i, j: (j, 0, 0)), rows, rows,
                                             pl.BlockSpec((1, D), lambda i, j: (0, 0)), ANY],
        out_specs=[rows, pl.BlockSpec((1, D), lambda i, j: (0, 0))],
        out_shape=[SDS((n, D), F32), SDS((1, D), F32)],
        scratch_shapes=[pltpu.VMEM((tm, D), F32)],
        compiler_params=_params(48),
    )(dza, dzb, dzm, w_all, x2, dx2, norm_g, after)


def _walk_tables(order, ni):
    rows = []
    for lo, hi in ((0, 2), (2, 6), (6, 8)):
        active = [j for j, g in enumerate(order) if lo <= g < hi]
        block, tile = [], []
        for j, g in enumerate(order):
            before = [a for a in active if a < j]
            if lo <= g < hi:
                block.append(g - lo), tile.append(-1)
            elif before:
                block.append(order[before[-1]] - lo), tile.append(ni - 1)
            else:
                block.append(order[active[0]] - lo), tile.append(0)
        rows += [block, tile]
    return rows


def _inproj_bwd_w(core, dza, dzb, dzm, h_all, g_m):
    n = h_all.shape[0]
    tm = min(n, 2048)
    ni = n // tm
    packed = g_m.shape[1:]
    orders = [[2 * q + 1 - c for q in range(4)] + [2 * q + c for q in range(4)] for c in (0, 1)]
    tables = jnp.asarray([[order] + _walk_tables(order, ni) for order in orders], jnp.int32)
    walk = jnp.where(core == 0, tables[0], tables[1])

    def body(walk_ref, dza_ref, dzb_ref, dzm_ref, h_ref, gm_hbm, out_bf, own_f32, m_out_bf, m_own_f32, got_w, got_m,
             acc, stage, theirs, m_mine, m_theirs, m_stage, send_sems, recv_sems, local_sems):
        j, i = pl.program_id(0), pl.program_id(1)
        group = walk_ref[0, j]
        x, y, c = _place()
        sibling = (x, y, 1 - c)

        def send_w(q):
            return pltpu.make_async_remote_copy(
                src_ref=stage.at[q % 2], dst_ref=got_w.at[q], send_sem=send_sems.at[q], recv_sem=recv_sems.at[q],
                device_id=sibling, device_id_type=MESH)

        def send_m(q):
            return pltpu.make_async_remote_copy(
                src_ref=gm_hbm.at[2 * q + (1 - c)], dst_ref=got_m.at[q], send_sem=send_sems.at[4 + q],
                recv_sem=recv_sems.at[4 + q], device_id=sibling, device_id_type=MESH)

        def fetch(q):
            return pltpu.make_async_copy(got_w.at[q], theirs, local_sems.at[0])

        def fetch_m(q):
            return (pltpu.make_async_copy(gm_hbm.at[2 * q + c], m_mine, local_sems.at[2]),
                    pltpu.make_async_copy(got_m.at[q], m_theirs, local_sems.at[3]))

        @pl.when((j == 0) & (i == 0))
        def _():
            for q in range(4):
                send_m(q).start()

        @pl.when(i == 0)
        def _():
            acc[...] = jnp.zeros((D, D), F32)

        def add(ref):
            acc[...] += lax.dot_general(h_ref[...], ref[0], TN_DIMS, preferred_element_type=F32)

        pl.when(group < 2)(lambda: add(dza_ref))
        pl.when((group >= 2) & (group < 6))(lambda: add(dzb_ref))
        pl.when(group >= 6)(lambda: add(dzm_ref))

        for q in range(4):
            @pl.when((i == ni - 1) & (j == q))
            def _(q=q):
                if q >= 2:
                    send_w(q - 2).wait_send()
                stage[q % 2] = acc[...].astype(BF16)
                send_w(q).start()

        def reducer(q):
            other_x, other_y = x != q // 2, y != q % 2
            return other_x | other_y, jnp.where(other_x & other_y, 2, jnp.where(other_x, 0, 1))

        def w_out(q):
            return pltpu.make_async_copy(stage.at[0], out_bf.at[reducer(q)[1]], local_sems.at[1])

        for q in range(4):
            @pl.when((j == 4 + q) & (i == 0))
            def _(q=q):
                send_w(q).wait_recv()
                send_m(q).wait_recv()
                fetch(q).start()
                for cp in fetch_m(q):
                    cp.start()
                if q > 0:
                    pl.when(reducer(q - 1)[0])(lambda: w_out(q - 1).wait())

            @pl.when((j == 4 + q) & (i == ni - 1))
            def _(q=q):
                if q == 0:
                    send_w(2).wait_send()
                    send_w(3).wait_send()
                other, slot = reducer(q)
                m_out = pltpu.make_async_copy(m_stage, m_out_bf.at[slot], local_sems.at[4])
                m_own = pltpu.make_async_copy(m_mine, m_own_f32, local_sems.at[4])

                for cp in fetch_m(q):
                    cp.wait()

                @pl.when(other)
                def _():
                    m_stage[...] = (m_mine[...] + m_theirs[...]).astype(BF16)
                    m_out.start()

                @pl.when(jnp.logical_not(other))
                def _():
                    m_mine[...] += m_theirs[...]
                    m_own.start()

                fetch(q).wait()

                @pl.when(other)
                def _():
                    stage[0] = (acc[...] + theirs[...].astype(F32)).astype(BF16)
                    w_out(q).start()
                    if q == 3:
                        w_out(q).wait()
                    m_out.wait()

                @pl.when(jnp.logical_not(other))
                def _():
                    acc[...] += theirs[...].astype(F32)
                    out = pltpu.make_async_copy(acc, own_f32, local_sems.at[1])
                    out.start()
                    out.wait()
                    m_own.wait()

        @pl.when((j == NB - 1) & (i == ni - 1))
        def _():
            for q in range(4):
                send_m(q).wait_send()

    def dz_spec(k):
        return pl.BlockSpec((1, tm, D), lambda j, i, w: (w[1 + 2 * k, j], jnp.where(w[2 + 2 * k, j] < 0, i, w[2 + 2 * k, j]), 0))

    return pl.pallas_call(
        body, name="inproj_bwd_w",
        grid_spec=pltpu.PrefetchScalarGridSpec(
            num_scalar_prefetch=1, grid=(NB, ni),
            in_specs=[dz_spec(0), dz_spec(1), dz_spec(2), pl.BlockSpec((tm, D), lambda j, i, w: (i, 0)), ANY],
            out_specs=[ANY] * 6,
            scratch_shapes=[pltpu.VMEM((D, D), F32), pltpu.VMEM((2, D, D), BF16), pltpu.VMEM((D, D), BF16),
                            pltpu.VMEM(packed, F32), pltpu.VMEM(packed, F32), pltpu.VMEM(packed, BF16),
                            pltpu.SemaphoreType.DMA((8,)), pltpu.SemaphoreType.DMA((8,)), pltpu.SemaphoreType.DMA((5,))]),
        out_shape=[SDS((3, D, D), BF16), SDS((D, D), F32), SDS((3,) + packed, BF16), SDS(packed, F32),
                   SDS((4, D, D), BF16), SDS((4,) + packed, F32)],
        compiler_params=_params(58),
    )(walk, dza, dzb, dzm, h_all, g_m)


def _adamw(w, g, m, v):
    rows, cols = w.shape
    tr = _row_tile(rows)

    spec = pl.BlockSpec((tr, cols), lambda i: (i, 0))
    return pl.pallas_call(
        functools.partial(_adam_refs), name="adamw", grid=(rows // tr,), in_specs=[spec] * 4, out_specs=[spec] * 3,
        out_shape=[SDS((rows, cols), F32)] * 3, compiler_params=_params(32),
    )(w, g, m, v)


def _adam_refs(w_ref, g_ref, m_ref, v_ref, d_ref, nm_ref, nv_ref):
    gv = g_ref[...]
    nm = ADAM_B1 * m_ref[...] + (1.0 - ADAM_B1) * gv
    nv = ADAM_B2 * v_ref[...] + (1.0 - ADAM_B2) * (gv * gv)
    m_hat = nm / (1.0 - ADAM_B1 ** ADAM_STEP)
    v_hat = nv / (1.0 - ADAM_B2 ** ADAM_STEP)
    d_ref[...] = -ADAM_LR * (m_hat / (jnp.sqrt(v_hat) + ADAM_EPS) + ADAM_WD * w_ref[...])
    nm_ref[...] = nm
    nv_ref[...] = nv


def _adamw_small(ws, gs, ms, vs):
    k = len(ws)

    def body(*refs):
        ins, outs = refs[:4 * k], refs[4 * k:7 * k]
        vin, vout = refs[7 * k:11 * k], refs[11 * k:14 * k]
        load_sems, store_sems = refs[14 * k:]
        loads = [pltpu.make_async_copy(ins[i], vin[i], load_sems.at[i]) for i in range(4 * k)]
        for cp in loads:
            cp.start()
        for cp in loads:
            cp.wait()
        for i in range(k):
            _adam_refs(*[vin[part * k + i] for part in range(4)], *[vout[part * k + i] for part in range(3)])
        stores = [pltpu.make_async_copy(vout[i], outs[i], store_sems.at[i]) for i in range(3 * k)]
        for cp in stores:
            cp.start()
        for cp in stores:
            cp.wait()

    shapes = [SDS(w.shape, F32) for w in ws]
    vmem = [pltpu.VMEM(w.shape, F32) for w in ws]
    out = pl.pallas_call(
        body, name="adamw_small", out_shape=shapes * 3, in_specs=[HBM] * (4 * k), out_specs=[HBM] * (3 * k),
        scratch_shapes=vmem * 7 + [pltpu.SemaphoreType.DMA((4 * k,)), pltpu.SemaphoreType.DMA((3 * k,))],
        compiler_params=_params(32),
    )(*ws, *gs, *ms, *vs)
    return out[:k], out[k:2 * k], out[2 * k:]


def _allgather(blocks, dtypes, name):
    na = len(blocks)

    def body(*refs):
        ins, outs, stages = refs[:na], refs[na:2 * na], refs[2 * na:3 * na]
        send_sems, recv_sems, local_sems = refs[3 * na:]
        x, y, c = _place()
        me, sibling = (x, y, c), (x, y, 1 - c)
        chips = [(1 - x, y), (x, 1 - y), (1 - x, 1 - y)]
        blk = lambda p: 4 * p[0] + 2 * p[1] + p[2]

        def copy(a, k, block, to, src=None):
            return pltpu.make_async_remote_copy(
                src_ref=outs[a].at[blk(block)] if src is None else src, dst_ref=outs[a].at[blk(block)],
                send_sem=send_sems.at[7 * a + k], recv_sem=recv_sems.at[7 * a + k],
                device_id=to, device_id_type=MESH)

        mine, first, passed = [], [], []
        for a in range(na):
            stages[a][...] = ins[a][...].astype(dtypes[a])
            mine.append(pltpu.make_async_copy(stages[a], outs[a].at[blk(me)], local_sems.at[a]))
            mine[-1].start()
            first.append(copy(a, 0, me, sibling, src=stages[a]))
            first += [copy(a, 1 + j, me, (*chip, c), src=stages[a]) for j, chip in enumerate(chips)]
        for cp in first:
            cp.start()
        for j, chip in enumerate(chips):
            for a in range(na):
                copy(a, 1 + j, (*chip, c), me).wait_recv()
                passed.append(copy(a, 4 + j, (*chip, c), sibling))
                passed[-1].start()
        for a in range(na):
            copy(a, 0, sibling, me).wait_recv()
            for j, chip in enumerate(chips):
                copy(a, 4 + j, (*chip, 1 - c), me).wait_recv()
        for cp in first + passed:
            cp.wait_send()
        for cp in mine:
            cp.wait()

    return pl.pallas_call(
        body, name=name,
        in_specs=[pl.BlockSpec(memory_space=pltpu.VMEM)] * na, out_specs=[ANY] * na,
        out_shape=[SDS((NB,) + b.shape, dt) for b, dt in zip(blocks, dtypes)],
        scratch_shapes=[pltpu.VMEM(b.shape, dt) for b, dt in zip(blocks, dtypes)]
        + [pltpu.SemaphoreType.DMA((7 * na,)), pltpu.SemaphoreType.DMA((7 * na,)), pltpu.SemaphoreType.DMA((na,))],
        compiler_params=_params(40),
    )(*blocks)


HBM = pl.BlockSpec(memory_space=pltpu.HBM)
SEMS = pl.BlockSpec(memory_space=pltpu.SEMAPHORE)
EFFECT = pltpu.SideEffectType.DATAFLOW_SIDE_EFFECTING


def _chip_copies(srcs, lands, send_sems, recv_sems):
    x, y, c = _place()
    return [pltpu.make_async_remote_copy(
        src_ref=srcs[a].at[slot], dst_ref=lands[a].at[slot],
        send_sem=send_sems.at[3 * a + slot], recv_sem=recv_sems.at[3 * a + slot],
        device_id=(px, py, c), device_id_type=MESH)
        for a in range(len(srcs)) for slot, (px, py) in enumerate(_other_chips(x, y))]


def _split_start(name, copies, per_array, srcs, lands, after=None):
    na = len(srcs)

    def body(*refs):
        send_sems, recv_sems = refs[-2 * na - 3], refs[-2 * na - 2]
        for cp in copies(refs[:na], refs[na:2 * na], send_sems, recv_sems):
            cp.start()
        refs[-1][...] = jnp.zeros_like(refs[-1])

    hbm = lambda a: pltpu.HBM(a.shape, a.dtype)
    pin = lambda a: pltpu.with_memory_space_constraint(a, pltpu.HBM)
    out = pl.pallas_call(
        body, name=name,
        out_shape=(pltpu.SemaphoreType.DMA((per_array * na,)), pltpu.SemaphoreType.DMA((per_array * na,)),
                   *[hbm(a) for a in srcs], *[hbm(a) for a in lands], SDS((8, BD), F32)),
        in_specs=[HBM] * (2 * na) + ([] if after is None else [ANY]),
        out_specs=(SEMS, SEMS, *[HBM] * (2 * na), pl.BlockSpec(memory_space=pltpu.VMEM)),
        input_output_aliases={i: 2 + i for i in range(2 * na)},
        compiler_params=pltpu.CompilerParams(has_side_effects=EFFECT),
    )(*[pin(a) for a in srcs], *[pin(a) for a in lands], *([] if after is None else [after]))
    return out[0], out[1], out[2:2 + na], out[2 + na:2 + 2 * na], out[-1]


def _split_wait(name, copies, started, after):
    send_sems, recv_sems, srcs, lands, _ = started
    na = len(srcs)

    def body(*refs):
        waits = copies(refs[:na], refs[na:2 * na], refs[2 * na], refs[2 * na + 1])
        for cp in waits:
            cp.wait_send()
        for cp in waits:
            cp.wait_recv()

    hbm = lambda a: pltpu.HBM(a.shape, a.dtype)
    out = pl.pallas_call(
        body, name=name,
        out_shape=(*[hbm(a) for a in srcs], *[hbm(a) for a in lands]),
        in_specs=[HBM] * (2 * na) + [SEMS, SEMS, ANY],
        out_specs=tuple([HBM] * (2 * na)),
        input_output_aliases={i: i for i in range(2 * na)},
        compiler_params=pltpu.CompilerParams(has_side_effects=EFFECT),
    )(*srcs, *lands, send_sems, recv_sems, after)
    return out[na:]


def _add_chips(own, b_in):
    r, cols = own.shape
    tr = _row_tile(r)

    def body(p_ref, b0_ref, b1_ref, b2_ref, o_ref):
        o_ref[...] = ((p_ref[...] + b0_ref[0].astype(F32)) + b1_ref[0].astype(F32)) + b2_ref[0].astype(F32)

    slot = lambda k: pl.BlockSpec((1, tr, cols), lambda i: (k, i, 0))
    spec = pl.BlockSpec((tr, cols), lambda i: (i, 0))
    return pl.pallas_call(
        body, name="add_chips", grid=(r // tr,), in_specs=[spec, slot(0), slot(1), slot(2)], out_specs=spec,
        out_shape=SDS((r, cols), F32), compiler_params=_params(32),
    )(own, b_in, b_in, b_in)


VEC_NAMES = ("b_merge", "conv_b", "rg_bx", "rg_ba", "rg_lambda", "hg_lb_logits", "hg_norm_g", "final_norm_g")
REP_NAMES = ("rg_wx", "rg_wa", "norm_g") + VEC_NAMES
SMALL_AT = 3 * BD
SMALL_ROWS = 48
MID_ROWS = 448


def _sum_blocks(parts):
    def body(p_ref, o_ref):
        acc = p_ref[0]
        for k in range(1, NB):
            acc = acc + p_ref[k]
        o_ref[...] = acc

    return pl.pallas_call(body, name="sum_blocks", out_shape=SDS(parts.shape[1:], F32))(parts)


def _pack_rows(arrays, width, row_multiple=8):
    flat = jnp.concatenate([a.reshape(-1) for a in arrays])
    rows = -(-flat.shape[0] // width)
    rows = -(-rows // row_multiple) * row_multiple
    return jnp.pad(flat, (0, rows * width - flat.shape[0])).reshape(rows, width)


def _unpack(flat, like):
    out, off = [], 0
    for a in like:
        out.append(flat[off:off + a.size].reshape(a.shape))
        off += a.size
    return out


def kernel(x, w_in, b_merge, conv_w, conv_b, rg_wx, rg_bx, rg_wa, rg_ba, rg_lambda, hg_lb_logits, hg_norm_g, proj_a, proj_b, w_out, norm_g, final_norm_g, loss_target, m_w_in, m_b_merge, m_conv_w, m_conv_b, m_rg_wx, m_rg_bx, m_rg_wa, m_rg_ba, m_rg_lambda, m_hg_lb_logits, m_hg_norm_g, m_proj_a, m_proj_b, m_w_out, m_norm_g, m_final_norm_g, v_w_in, v_b_merge, v_conv_w, v_conv_b, v_rg_wx, v_rg_bx, v_rg_wa, v_rg_ba, v_rg_lambda, v_hg_lb_logits, v_hg_norm_g, v_proj_a, v_proj_b, v_w_out, v_norm_g, v_final_norm_g):
    weights = dict(w_in=w_in, b_merge=b_merge, conv_w=conv_w, conv_b=conv_b, rg_wx=rg_wx, rg_bx=rg_bx, rg_wa=rg_wa,
                   rg_ba=rg_ba, rg_lambda=rg_lambda, hg_lb_logits=hg_lb_logits, hg_norm_g=hg_norm_g, proj_a=proj_a,
                   proj_b=proj_b, w_out=w_out, norm_g=norm_g, final_norm_g=final_norm_g)
    mom1 = dict(w_in=m_w_in, b_merge=m_b_merge, conv_w=m_conv_w, conv_b=m_conv_b, rg_wx=m_rg_wx, rg_bx=m_rg_bx,
                rg_wa=m_rg_wa, rg_ba=m_rg_ba, rg_lambda=m_rg_lambda, hg_lb_logits=m_hg_lb_logits,
                hg_norm_g=m_hg_norm_g, proj_a=m_proj_a, proj_b=m_proj_b, w_out=m_w_out, norm_g=m_norm_g,
                final_norm_g=m_final_norm_g)
    mom2 = dict(w_in=v_w_in, b_merge=v_b_merge, conv_w=v_conv_w, conv_b=v_conv_b, rg_wx=v_rg_wx, rg_bx=v_rg_bx,
                rg_wa=v_rg_wa, rg_ba=v_rg_ba, rg_lambda=v_rg_lambda, hg_lb_logits=v_hg_lb_logits,
                hg_norm_g=v_hg_norm_g, proj_a=v_proj_a, proj_b=v_proj_b, w_out=v_w_out, norm_g=v_norm_g,
                final_norm_g=v_final_norm_g)
    order = list(weights)
    nb, s_len, _ = x.shape
    n = nb * s_len
    px, py, pc = _place()

    in_hbm = lambda a: pltpu.with_memory_space_constraint(a, pltpu.HBM)
    norm_gain = in_hbm(norm_g)

    x2 = x.reshape(n, D)
    cw_blk = jnp.pad(conv_w[0], ((0, 4), (0, 0)))
    order_ids = jnp.stack([_block_id(p) for p in _arrival_order(px, py, pc)]).astype(jnp.int32)
    z, h_all, w_all, pa_all, pb_all, wo_all, cw_all = _gather_inproj(
        order_ids, x2, norm_gain, [w_in[0], proj_a[0], proj_b[0], w_out[0], cw_blk], [BF16, BF16, BF16, BF16, F32])
    pa_full, pb_full, wo_full = (a.reshape(D, D) for a in (pa_all, pb_all, wo_all))
    cw8 = in_hbm(cw_all.transpose(1, 0, 2).reshape(8, D))
    wx_b, wa_b = in_hbm(rg_wx[0].astype(BF16)), in_hbm(rg_wa[0].astype(BF16))
    cb, bx, ba, lam = (in_hbm(a.reshape(1, D)) for a in (conv_b, rg_bx, rg_ba, rg_lambda))
    fin_g, b_mrg = in_hbm(final_norm_g.reshape(1, D)), in_hbm(b_merge)
    lb_lg, hg_g = in_hbm(hg_lb_logits), in_hbm(hg_norm_g)

    hlru, ya = _lru_fwd(z, cw8, cb, wx_b, wa_b, bx, ba, lam, nb, s_len)
    o_all, yb, st_all = _hgrn_fwd(z, lb_lg, hg_g, nb, s_len)

    (dx2, dya, dyb, dzm, loss_acc, g_fin, g_bm, g_mid) = _mid(
        ya, yb, z, b_mrg, x2, loss_target.reshape(n, D), fin_g, pa_full, pb_full, wo_full)
    dzb, g_lg, g_hg = _hgrn_bwd(z, o_all, st_all, dyb, lb_lg, hg_g, nb, s_len)
    dza, g_cw8, g_cb, g_wx, g_wa, g_bx, g_ba, g_lam = _lru_bwd(
        z, hlru, dya, cw8, cb, wx_b, wa_b, bx, ba, lam, nb, s_len)

    part = dict(b_merge=g_bm, conv_b=g_cb, rg_bx=g_bx, rg_ba=g_ba, rg_lambda=g_lam, hg_lb_logits=g_lg,
                hg_norm_g=g_hg, final_norm_g=g_fin)
    vec = _pack_rows([part[k] for k in VEC_NAMES], BD)
    vec = jnp.pad(vec, ((0, 16 * NB - vec.shape[0]), (0, 0))).reshape(NB, 2, D)
    rows8 = lambda a: jnp.pad(a, ((0, 0), (0, 8 - a.shape[1]), (0, 0)))
    small = jnp.concatenate([g_wx.reshape(NB, 16, D), g_wa.reshape(NB, 16, D),
                             rows8(g_cw8.reshape(8, NB, BD).transpose(1, 0, 2).reshape(NB, 1, D)), rows8(vec),
                             jnp.zeros((NB, MID_ROWS - SMALL_AT - SMALL_ROWS, D), F32)], axis=1)
    g_m = lax.dynamic_update_slice(g_mid, small, (0, SMALL_AT, 0))
    w_out_bf, w_own, m_out_bf, m_own, _, _ = _inproj_bwd_w(pc, dza, dzb, dzm, h_all, g_m)
    outgoing = [w_out_bf, m_out_bf]
    chip_sums = _split_start("rs_chips_start", _chip_copies, 3, outgoing, [lax.empty(a.shape, a.dtype) for a in outgoing])
    grad_x, g_ng = _inproj_bwd_x(dza, dzb, dzm, w_all, x2, dx2, norm_gain, chip_sums[-1])
    from_chips = _split_wait("rs_chips_wait", _chip_copies, chip_sums, grad_x)
    r_w = _add_chips(w_own, from_chips[0])
    r_m = _add_chips(m_own, from_chips[1])
    row = lax.broadcasted_iota(jnp.int32, (8, D), 0)
    mine = jnp.where(row == 0, g_ng, jnp.where(row == 1, loss_acc[0:1, 0:1], 0.0))
    tail = jnp.concatenate([r_m[SMALL_AT:SMALL_AT + SMALL_ROWS], mine], axis=0)
    (tail_all,) = _allgather([tail], [F32], "gather_small_grads")
    summed = _sum_blocks(tail_all[:, SMALL_ROWS:SMALL_ROWS + 8])

    grads = dict(w_in=r_w.reshape(1, D, D),
                 proj_a=r_m[0:BD].reshape(1, BD, D), proj_b=r_m[BD:2 * BD].reshape(1, BD, D),
                 w_out=r_m[2 * BD:3 * BD].reshape(1, BD, D),
                 conv_w=r_m[SMALL_AT + 32].reshape(8, BD)[0:4].reshape(1, 4, BD),
                 rg_wx=tail_all[:, 0:16].reshape(1, NB, BD, BD), rg_wa=tail_all[:, 16:32].reshape(1, NB, BD, BD),
                 norm_g=summed[0:1])
    vec_all = tail_all[:, 40:42].reshape(-1)
    for k, gk in zip(VEC_NAMES, _unpack(vec_all, [weights[k] for k in VEC_NAMES])):
        grads[k] = gk

    delta, new_m, new_v = {}, {}, {}
    flat2 = lambda a: a.reshape(-1, a.shape[-1])
    for k in ("w_in", "proj_a", "proj_b", "w_out"):
        outs = _adamw(*[flat2(t[k]) for t in (weights, grads, mom1, mom2)])
        delta[k], new_m[k], new_v[k] = (a.reshape(weights[k].shape) for a in outs)
    rep = list(REP_NAMES) + ["conv_w"]
    outs = _adamw_small(*[[flat2(t[k]) for k in rep] for t in (weights, grads, mom1, mom2)])
    for tgt, arrays in zip((delta, new_m, new_v), outs):
        for k, a in zip(rep, arrays):
            tgt[k] = a.reshape(weights[k].shape)

    return (summed[1, 0], grad_x.reshape(x.shape), *[grads[k] for k in order], *[delta[k] for k in order],
            *[new_m[k] for k in order], *[new_v[k] for k in order])
```

```python
import functools

import jax
import jax.numpy as jnp
from jax import lax
from jax.experimental import pallas as pl
from jax.experimental.pallas import tpu as pltpu

F32 = jnp.float32
BF16 = jnp.bfloat16
SDS = jax.ShapeDtypeStruct
MESH = pl.DeviceIdType.MESH
ANY = pl.BlockSpec(memory_space=pl.ANY)

D = 1024
NB = 8
BD = D // NB
CHUNK = 64
EPS = 1e-6
LRU_C = 8.0
HG_SCALE = BD ** -0.5
ADAM_LR, ADAM_B1, ADAM_B2, ADAM_EPS, ADAM_WD, ADAM_STEP = 0.001, 0.9, 0.999, 1e-08, 0.01, 10

NT_DIMS = (((1,), (1,)), ((), ()))
TN_DIMS = (((0,), (0,)), ((), ()))


def _params(vmem_mib):
    return pltpu.CompilerParams(vmem_limit_bytes=vmem_mib << 20)


def _row_tile(rows, most=256):
    assert rows % 8 == 0
    return max(t for t in range(8, min(rows, most) + 1, 8) if rows % t == 0)


def _sigmoid(v):
    return 0.5 * (jnp.tanh(0.5 * v) + 1.0)


def _groups(v):
    return v.reshape(v.shape[0] // 8, 8, v.shape[1])


def _softplus_neg(lam):
    t = -lam
    e = jnp.exp(-jnp.abs(t))
    w = 1.0 + e
    d = w - 1.0
    l1p = jnp.where(d == 0.0, e, jnp.log(w) * (e / jnp.where(d == 0.0, 1.0, d)))
    return jnp.maximum(t, 0.0) + l1p


def _place():
    return lax.axis_index("x"), lax.axis_index("y"), lax.axis_index("c")


def _other_chips(x, y):
    return [(1 - x, y), (x, 1 - y), (1 - x, 1 - y)]


def _block_id(p):
    return 4 * p[0] + 2 * p[1] + p[2]


def _core_chips(x, y, c):
    near, far, diag = _other_chips(x, y)
    pick = lambda a, b: (jnp.where(c == 0, a[0], b[0]), jnp.where(c == 0, a[1], b[1]))
    return [pick(near, far), pick(far, near), diag]


def _arrival_order(x, y, c):
    first, second, diag = _core_chips(x, y, c)
    return [(x, y, c), (x, y, 1 - c), (*first, c), (*second, 1 - c), (*second, c), (*first, 1 - c),
            (*diag, c), (*diag, 1 - c)]


def _gather_inproj(order_ids, x2, norm_g, blocks, dtypes):
    na = len(blocks)
    n = x2.shape[0]
    tm = min(n, 1024)
    ni = n // tm

    def body(order_ref, x_ref, g_ref, *refs):
        ins, (z_ref, h_ref), outs = refs[:na], refs[na:na + 2], refs[na + 2:2 * na + 2]
        stages = refs[2 * na + 2:3 * na + 2]
        h_full, wbuf, send_sems, recv_sems, local_sems, wsems, hsem = refs[3 * na + 2:]
        j, i = pl.program_id(0), pl.program_id(1)
        x, y, c = _place()
        me, sibling = (x, y, c), (x, y, 1 - c)
        chips = _core_chips(x, y, c)
        sibling_chips = [chips[1], chips[0], chips[2]]
        small = range(1, na)

        def copy(a, k, block, to, src=None):
            return pltpu.make_async_remote_copy(
                src_ref=outs[a].at[_block_id(block)] if src is None else src, dst_ref=outs[a].at[_block_id(block)],
                send_sem=send_sems.at[7 * a + k], recv_sem=recv_sems.at[7 * a + k],
                device_id=to, device_id_type=MESH)

        def local(a):
            return pltpu.make_async_copy(stages[a], outs[a].at[_block_id(me)], local_sems.at[a])

        def landed(a, slot):
            copy(a, 1 + slot, (*chips[slot], c), me).wait_recv()
            copy(a, 4 + slot, (*chips[slot], c), sibling).start()
            if slot == 0:
                copy(a, 3, (*chips[0], c), (*chips[1], c)).start()

        def diagonal_and_small():
            landed(0, 2)
            for a in small:
                landed(a, 0)
                landed(a, 1)

        def passed_on(a, slot):
            copy(a, 4 + slot, (*sibling_chips[slot], 1 - c), me).wait_recv()

        def sibling_here_send_second():
            copy(0, 0, sibling, me).wait_recv()
            for a in range(na):
                copy(a, 2, me, (*chips[1], c), src=stages[a]).start()

        @pl.when((j == 0) & (i == 0))
        def _():
            for a in range(na):
                stages[a][...] = ins[a][...].astype(dtypes[a])
                local(a).start()
            for a in range(na):
                copy(a, 0, me, sibling, src=stages[a]).start()
                copy(a, 1, me, (*chips[0], c), src=stages[a]).start()

        @pl.when(j == 0)
        def _():
            xv = x_ref[...]
            r = lax.rsqrt(jnp.mean(xv * xv, axis=-1, keepdims=True) + EPS)
            hb = ((xv * r) * g_ref[...]).astype(BF16)
            h_full[pl.ds(pl.multiple_of(i * tm, tm), tm), :] = hb

        save_h = pltpu.make_async_copy(h_full, h_ref, hsem)
        pl.when((j == 0) & (i == ni - 1))(save_h.start)

        steps = [
            lambda: local(0).wait(),
            sibling_here_send_second,
            lambda: landed(0, 0),
            lambda: passed_on(0, 0),
            lambda: landed(0, 1),
            lambda: passed_on(0, 1),
            diagonal_and_small,
            lambda: passed_on(0, 2),
        ]
        def w_load(k):
            return pltpu.make_async_copy(outs[0].at[order_ref[k]], wbuf.at[k % 2], wsems.at[k % 2])

        for k, step in enumerate(steps):
            @pl.when((j == 0) & (i == 0) if k == 0 else (j == k - 1) & (i == ni - 1))
            def _(k=k, step=step):
                step()
                w_load(k).start()

        pl.when(i == 0)(lambda: w_load(j).wait())
        z_ref[0] = jnp.dot(h_full[pl.ds(pl.multiple_of(i * tm, tm), tm), :], wbuf[j % 2], preferred_element_type=F32)

        @pl.when((j == NB - 1) & (i == ni - 1))
        def _():
            save_h.wait()
            for a in small:
                landed(a, 2)
            for a in small:
                local(a).wait()
                copy(a, 0, sibling, me).wait_recv()
                for slot in range(3):
                    passed_on(a, slot)
            for a in range(na):
                copy(a, 0, me, sibling, src=stages[a]).wait_send()
                for slot, chip in enumerate(chips):
                    copy(a, 1 + slot, me, (*chip, c), src=stages[a]).wait_send()
                    copy(a, 4 + slot, (*chip, c), sibling).wait_send()

    rows_once = lambda j, i, order: (jnp.where(j == 0, i, ni - 1), 0)
    vmem = pl.BlockSpec(memory_space=pltpu.VMEM)
    return pl.pallas_call(
        body, name="gather_inproj",
        grid_spec=pltpu.PrefetchScalarGridSpec(
            num_scalar_prefetch=1, grid=(NB, ni),
            in_specs=[pl.BlockSpec((tm, D), rows_once), pl.BlockSpec((1, D), lambda j, i, order: (0, 0))] + [vmem] * na,
            out_specs=[pl.BlockSpec((1, tm, D), lambda j, i, order: (order[j], i, 0)), ANY] + [ANY] * na,
            scratch_shapes=[pltpu.VMEM(b.shape, dt) for b, dt in zip(blocks, dtypes)]
            + [pltpu.VMEM((n, D), BF16), pltpu.VMEM((2, D, D), BF16),
               pltpu.SemaphoreType.DMA((7 * na,)), pltpu.SemaphoreType.DMA((7 * na,)),
               pltpu.SemaphoreType.DMA((na,)), pltpu.SemaphoreType.DMA((2,)), pltpu.SemaphoreType.DMA(())]),
        out_shape=[SDS((NB, n, D), F32), SDS((n, D), BF16)] + [SDS((NB,) + b.shape, dt) for b, dt in zip(blocks, dtypes)],
        compiler_params=_params(56),
    )(order_ids, x2, norm_g, *blocks)


LRU_T = 256


def _shifted(groups, shifts):
    row = lax.broadcasted_iota(jnp.int32, (groups.shape[0] - 1,) + groups.shape[1:], 1)
    out = []
    for s in shifts:
        y = pltpu.roll(groups, s % 8, 1)
        moved = jnp.where(row >= s, y[1:], y[:-1]) if s > 0 else jnp.where(row < 8 + s, y[:-1], y[1:])
        out.append(moved.reshape(-1, groups.shape[2]))
    return out


def _conv(taps, cw, cb):
    acc = taps[0] * cw[0:1, :] + taps[1] * cw[1:2, :]
    acc = acc + taps[2] * cw[2:3, :]
    acc = acc + taps[3] * cw[3:4, :]
    return cb + acc


def _lru_gates(xa, wx_ref, wa_ref, bx, ba, lam):
    xab = xa.astype(BF16)
    pis, prs = [], []
    for h in range(NB):
        xs = xab[:, h * BD:(h + 1) * BD]
        pis.append(jnp.dot(xs, wx_ref[h], preferred_element_type=F32))
        prs.append(jnp.dot(xs, wa_ref[h], preferred_element_type=F32))
    gi = _sigmoid(jnp.concatenate(pis, axis=1) + bx)
    gr = _sigmoid(jnp.concatenate(prs, axis=1) + ba)
    sp = _softplus_neg(lam)
    log_a = (-LRU_C * gr) * sp
    a = jnp.exp(log_a)
    mult = jnp.sqrt(-jnp.tanh(log_a) * (a * a + 1.0))
    return xab, gi, gr, sp, a, mult


def _lru_fwd(z, cw8, cb, wx, wa, bx, ba, lam, nb, s_len):
    n = nb * s_len
    t = LRU_T
    ns = s_len // t

    def body(xp_ref, ga_ref, cw_ref, cb_ref, wx_ref, wa_ref, bx_ref, ba_ref, lam_ref,
             h_ref, ya_ref, ext, a_s, u_s, carry):
        @pl.when(pl.program_id(1) == 0)
        def _():
            ext[0:8, :] = jnp.zeros((8, D), F32)
            carry[...] = jnp.zeros((8, D), F32)

        xp = xp_ref[0]
        ext[8:8 + t, :] = xp
        xa = _conv(_shifted(_groups(ext[...]), (3, 2, 1)) + [xp], cw_ref[...], cb_ref[...])
        ext[0:8, :] = xp[t - 8:t, :]
        _, gi, _, _, a, mult = _lru_gates(xa, wx_ref, wa_ref, bx_ref[...], ba_ref[...], lam_ref[...])
        u = (mult * gi) * xa
        a, u = _groups(a), _groups(u)
        row = lax.broadcasted_iota(jnp.int32, a.shape, 1)
        for sh in (1, 2, 4):
            a_sh = pltpu.roll(a, sh, 1)
            u_sh = pltpu.roll(u, sh, 1)
            m = row >= sh
            u = jnp.where(m, a * u_sh + u, u)
            a = jnp.where(m, a * a_sh, a)
        a_s[...] = a.reshape(t, D)
        u_s[...] = u.reshape(t, D)

        def step(g, c):
            r = pl.multiple_of(g * 8, 8)
            hg = u_s[pl.ds(r, 8), :] + a_s[pl.ds(r, 8), :] * c
            h_ref[pl.ds(r, 8), :] = hg
            return hg[7:8, :]

        c_out = lax.fori_loop(0, t // 8, step, carry[0:1, :], unroll=4)
        carry[0:1, :] = c_out
        ga = ga_ref[0]
        ya_ref[...] = (h_ref[...] * (ga * _sigmoid(ga))).astype(BF16)

    row_map = lambda b, s: (b * ns + s, 0)
    rep2 = lambda b, s: (0, 0)
    rep3 = lambda b, s: (0, 0, 0)
    return pl.pallas_call(
        body, name="lru_fwd", grid=(nb, ns),
        in_specs=[pl.BlockSpec((1, t, D), lambda b, s: (0, b * ns + s, 0)),
                  pl.BlockSpec((1, t, D), lambda b, s: (1, b * ns + s, 0)),
                  pl.BlockSpec((8, D), rep2), pl.BlockSpec((1, D), rep2),
                  pl.BlockSpec((NB, BD, BD), rep3), pl.BlockSpec((NB, BD, BD), rep3),
                  pl.BlockSpec((1, D), rep2), pl.BlockSpec((1, D), rep2), pl.BlockSpec((1, D), rep2)],
        out_specs=[pl.BlockSpec((t, D), row_map), pl.BlockSpec((t, D), row_map)],
        out_shape=[SDS((n, D), F32), SDS((n, D), BF16)],
        scratch_shapes=[pltpu.VMEM((t + 8, D), F32), pltpu.VMEM((t, D), F32), pltpu.VMEM((t, D), F32),
                        pltpu.VMEM((8, D), F32)],
        compiler_params=_params(48),
    )(z, z, cw8, cb, wx, wa, bx, ba, lam)


def _lru_bwd(z, h_all, dya, cw8, cb, wx, wa, bx, ba, lam, nb, s_len):
    n = nb * s_len
    t = LRU_T
    ns = s_len // t
    t8 = t // 8

    def body(xp_ref, xph_ref, ga_ref, h_ref, hh_ref, dya_ref, cw_ref, cb_ref, wx_ref, wa_ref, bx_ref, ba_ref,
             lam_ref, dz_ref, gcw_ref, gcb_ref, gwx_ref, gwa_ref, gbx_ref, gba_ref, glam_ref,
             ext, hext, dext, a_s, u_s, dh_s, carry):
        b, s = pl.program_id(0), pl.program_id(1)
        first_tile = s == ns - 1

        @pl.when((b == 0) & (s == 0))
        def _():
            for ref in (gcw_ref, gcb_ref, gwx_ref, gwa_ref, gbx_ref, gba_ref, glam_ref):
                ref[...] = jnp.zeros(ref.shape, F32)

        @pl.when(s == 0)
        def _():
            dext[t:t + 8, :] = jnp.zeros((8, D), F32)
            carry[...] = jnp.zeros((8, D), F32)

        keep = jnp.where(first_tile, 0.0, 1.0)
        xp = xp_ref[0]
        ext[0:8, :] = xph_ref[0] * keep
        ext[8:8 + t, :] = xp
        hext[0:8, :] = hh_ref[...] * keep
        hext[8:8 + t, :] = h_ref[...]
        cw = cw_ref[...]
        lam = lam_ref[...]
        taps = _shifted(_groups(ext[...]), (3, 2, 1)) + [xp]
        xa = _conv(taps, cw, cb_ref[...])
        xab, gi, gr, sp, a, mult = _lru_gates(xa, wx_ref, wa_ref, bx_ref[...], ba_ref[...], lam)
        (h_prev,) = _shifted(_groups(hext[...]), (1,))
        ga = ga_ref[0]
        sg = _sigmoid(ga)
        dya_v = dya_ref[...]
        d_ga = dya_v * h_ref[...] * (sg * (1.0 + ga * (1.0 - sg)))
        g_in = dya_v * (ga * sg)

        (an,) = _shifted(jnp.concatenate([_groups(a), jnp.ones((1, 8, D), F32)], axis=0), (-1,))
        an, u = _groups(an), _groups(g_in)
        row = lax.broadcasted_iota(jnp.int32, an.shape, 1)
        for sh in (1, 2, 4):
            a_sh = pltpu.roll(an, 8 - sh, 1)
            u_sh = pltpu.roll(u, 8 - sh, 1)
            m = row < 8 - sh
            u = jnp.where(m, u + an * u_sh, u)
            an = jnp.where(m, an * a_sh, an)
        a_s[...] = an.reshape(t, D)
        u_s[...] = u.reshape(t, D)

        def step(i, c):
            r = pl.multiple_of((t8 - 1 - i) * 8, 8)
            dg = u_s[pl.ds(r, 8), :] + a_s[pl.ds(r, 8), :] * c
            dh_s[pl.ds(r, 8), :] = dg
            return dg[0:1, :]

        lax.fori_loop(0, t8, step, carry[0:1, :], unroll=4)
        dh = dh_s[...]
        carry[0:1, :] = a[0:1, :] * dh[0:1, :]

        d_a = dh * h_prev
        dux = dh * xa
        d_mult = dux * gi
        d_gi = dux * mult
        d_xa = dh * (mult * gi)
        d_loga = d_a * a - d_mult * ((a * a) / mult)
        d_gr = d_loga * (-LRU_C * sp)
        d_sp = jnp.sum(d_loga * (-LRU_C * gr), axis=0, keepdims=True)
        glam_ref[...] += d_sp * (-_sigmoid(-lam))
        d_pi = d_gi * gi * (1.0 - gi)
        d_pr = d_gr * gr * (1.0 - gr)
        gbx_ref[...] += jnp.sum(d_pi, axis=0, keepdims=True)
        gba_ref[...] += jnp.sum(d_pr, axis=0, keepdims=True)
        dpib = d_pi.astype(BF16)
        dprb = d_pr.astype(BF16)
        back = []
        for h in range(NB):
            cs = slice(h * BD, (h + 1) * BD)
            gwx_ref[h] += lax.dot_general(xab[:, cs], dpib[:, cs], TN_DIMS, preferred_element_type=F32)
            gwa_ref[h] += lax.dot_general(xab[:, cs], dprb[:, cs], TN_DIMS, preferred_element_type=F32)
            back.append(lax.dot_general(dpib[:, cs], wx_ref[h], NT_DIMS, preferred_element_type=F32)
                        + lax.dot_general(dprb[:, cs], wa_ref[h], NT_DIMS, preferred_element_type=F32))
        d_xa = d_xa + jnp.concatenate(back, axis=1)

        dext[0:t, :] = d_xa
        later = _shifted(_groups(dext[...]), (-3, -2, -1))
        d_xp = later[0] * cw[0:1, :] + later[1] * cw[1:2, :]
        d_xp = d_xp + later[2] * cw[2:3, :]
        d_xp = d_xp + d_xa * cw[3:4, :]
        dext[t:t + 8, :] = d_xa[0:8, :]
        gcb_ref[...] += jnp.sum(d_xa, axis=0, keepdims=True)
        for k in range(4):
            gcw_ref[k:k + 1, :] += jnp.sum(d_xa * taps[k], axis=0, keepdims=True)
        dz_ref[0] = d_xp.astype(BF16)
        dz_ref[1] = d_ga.astype(BF16)

    rb = lambda b, s: b * ns + (ns - 1 - s)
    halo = lambda b, s: jnp.maximum(rb(b, s) * t8 - 1, 0)
    rep2 = lambda b, s: (0, 0)
    rep3 = lambda b, s: (0, 0, 0)
    return pl.pallas_call(
        body, name="lru_bwd", grid=(nb, ns),
        in_specs=[pl.BlockSpec((1, t, D), lambda b, s: (0, rb(b, s), 0)),
                  pl.BlockSpec((1, 8, D), lambda b, s: (0, halo(b, s), 0)),
                  pl.BlockSpec((1, t, D), lambda b, s: (1, rb(b, s), 0)),
                  pl.BlockSpec((t, D), lambda b, s: (rb(b, s), 0)),
                  pl.BlockSpec((8, D), lambda b, s: (halo(b, s), 0)),
                  pl.BlockSpec((t, D), lambda b, s: (rb(b, s), 0)),
                  pl.BlockSpec((8, D), rep2), pl.BlockSpec((1, D), rep2),
                  pl.BlockSpec((NB, BD, BD), rep3), pl.BlockSpec((NB, BD, BD), rep3),
                  pl.BlockSpec((1, D), rep2), pl.BlockSpec((1, D), rep2), pl.BlockSpec((1, D), rep2)],
        out_specs=[pl.BlockSpec((2, t, D), lambda b, s: (0, rb(b, s), 0)),
                   pl.BlockSpec((8, D), rep2), pl.BlockSpec((1, D), rep2),
                   pl.BlockSpec((NB, BD, BD), rep3), pl.BlockSpec((NB, BD, BD), rep3),
                   pl.BlockSpec((1, D), rep2), pl.BlockSpec((1, D), rep2), pl.BlockSpec((1, D), rep2)],
        out_shape=[SDS((2, n, D), BF16), SDS((8, D), F32), SDS((1, D), F32),
                   SDS((NB, BD, BD), F32), SDS((NB, BD, BD), F32),
                   SDS((1, D), F32), SDS((1, D), F32), SDS((1, D), F32)],
        scratch_shapes=[pltpu.VMEM((t + 8, D), F32), pltpu.VMEM((t + 8, D), F32), pltpu.VMEM((t + 8, D), F32),
                        pltpu.VMEM((t, D), F32), pltpu.VMEM((t, D), F32), pltpu.VMEM((t, D), F32),
                        pltpu.VMEM((8, D), F32)],
        compiler_params=_params(56),
    )(z, z, z, h_all, h_all, dya, cw8, cb, wx, wa, bx, ba, lam)


HG_T = 512
HG_NC = HG_T // CHUNK
BNT_DIMS = (((2,), (2,)), ((0,), (0,)))
BNN_DIMS = (((2,), (1,)), ((0,), (0,)))
BTN_DIMS = (((1,), (1,)), ((0,), (0,)))


def _lower_bound(lg):
    m = jnp.max(lg, axis=0, keepdims=True)
    e = jnp.exp(lg - m)
    return e[0:1, :] / jnp.sum(e, axis=0, keepdims=True)


def _tri(upper):
    r = lax.broadcasted_iota(jnp.int32, (HG_NC, CHUNK, CHUNK), 1)
    c = lax.broadcasted_iota(jnp.int32, (HG_NC, CHUNK, CHUNK), 2)
    return (c >= r) if upper else (r >= c)


def _bdot(a, b, dims):
    return lax.dot_general(a, b, dims, preferred_element_type=F32)


def _tri_sums(upper, a):
    tri = _tri(upper).astype(BF16)
    a1 = a.astype(BF16)
    r1 = a - a1.astype(F32)
    a2 = r1.astype(BF16)
    a3 = (r1 - a2.astype(F32)).astype(BF16)
    return _bdot(tri, a1, BNN_DIMS) + (_bdot(tri, a2, BNN_DIMS) + _bdot(tri, a3, BNN_DIMS))


def _chunks(a):
    return a.reshape(HG_NC, CHUNK, BD)


def _hg_tile(q, fp, lb):
    q, fp = _chunks(q), _chunks(fp)
    sig = _sigmoid(fp)
    f = lb + (1.0 - lb) * sig
    log_f = jnp.log(f)
    k = 1.0 - f
    b = _tri_sums(False, log_f)
    b_mid = b[:, CHUNK // 2:CHUNK // 2 + 1, :]
    b_last = b[:, CHUNK - 1:CHUNK, :]
    sq = _sigmoid(q)
    qh = q * sq
    e_qi = jnp.exp(b - b_mid)
    e_ki = jnp.exp(b_mid - b)
    e_qs = jnp.exp(b)
    e_ks = jnp.exp(b_last - b)
    dc = jnp.exp(b_last)
    q_in = (qh * e_qi) * HG_SCALE
    k_in = k * e_ki
    q_st = (qh * e_qs) * HG_SCALE
    k_st = k * e_ks
    att = _bdot(q_in.astype(BF16), k_in.astype(BF16), BNT_DIMS)
    att = jnp.where(_tri(False), att, 0.0)
    return dict(q=q, sig=sig, f=f, k=k, sq=sq, e_qi=e_qi, e_ki=e_ki, e_qs=e_qs, e_ks=e_ks, dc=dc,
                q_in=q_in, k_in=k_in, q_st=q_st, k_st=k_st, att=att)


def _hgrn_fwd(z, lb_logits, hg_g, nb, s_len):
    n = nb * s_len
    t = HG_T
    ns = s_len // t
    nchunk = s_len // CHUNK

    def body(q_ref, f_ref, v_ref, gb_ref, lg_ref, g_ref, o_ref, yb_ref, st_ref, st):
        @pl.when(pl.program_id(1) == 0)
        def _():
            st[...] = jnp.zeros((NB, BD, BD), F32)

        def head(h, carry):
            cols = pl.ds(pl.multiple_of(h * BD, BD), BD)
            lb = _lower_bound(lg_ref[:, cols])
            ck = _hg_tile(q_ref[0, :, cols], f_ref[0, :, cols], lb)
            vb = _chunks(v_ref[0, :, cols]).astype(BF16)
            kv = _bdot(vb, ck["k_st"].astype(BF16), BTN_DIMS)
            states = [st[h]]
            for c in range(HG_NC):
                states.append(states[c] * ck["dc"][c] + kv[c])
            st[h] = states[HG_NC]
            s_in = jnp.stack(states[:HG_NC], axis=0)
            st_ref[h] = s_in
            o = (_bdot(ck["att"].astype(BF16), vb, BNN_DIMS)
                 + _bdot(ck["q_st"].astype(BF16), s_in.astype(BF16), BNT_DIMS))
            o_ref[:, cols] = o.reshape(t, BD)
            r = lax.rsqrt(jnp.mean(o * o, axis=-1, keepdims=True) + EPS)
            gb = _chunks(gb_ref[0, :, cols])
            yb_ref[:, cols] = (((o * r) * g_ref[...]) * (gb * _sigmoid(gb))).astype(BF16).reshape(t, BD)
            return carry

        lax.fori_loop(0, NB, head, 0, unroll=4)

    seg = lambda j: pl.BlockSpec((1, t, D), lambda b, s: (j, b * ns + s, 0))
    tile = pl.BlockSpec((t, D), lambda b, s: (b * ns + s, 0))
    return pl.pallas_call(
        body, name="hgrn_fwd", grid=(nb, ns),
        in_specs=[seg(2), seg(3), seg(4), seg(5),
                  pl.BlockSpec((2, D), lambda b, s: (0, 0)), pl.BlockSpec((1, BD), lambda b, s: (0, 0))],
        out_specs=[tile, tile, pl.BlockSpec((NB, HG_NC, BD, BD), lambda b, s: (b, s, 0, 0))],
        out_shape=[SDS((n, D), F32), SDS((n, D), BF16), SDS((nb * NB, nchunk, BD, BD), F32)],
        scratch_shapes=[pltpu.VMEM((NB, BD, BD), F32)],
        compiler_params=_params(56),
    )(z, z, z, z, lb_logits, hg_g)


def _hgrn_bwd(z, o_all, st_all, dyb, lb_logits, hg_g, nb, s_len):
    n = nb * s_len
    t = HG_T
    ns = s_len // t

    def body(q_ref, f_ref, v_ref, gb_ref, o_ref, st_ref, dyb_ref, lg_ref, g_ref,
             dz_ref, glg_ref, ghg_ref, dst, dlb):
        b, s = pl.program_id(0), pl.program_id(1)

        @pl.when((b == 0) & (s == 0))
        def _():
            ghg_ref[...] = jnp.zeros((1, BD), F32)
            dlb[...] = jnp.zeros((8, D), F32)

        @pl.when(s == 0)
        def _():
            dst[...] = jnp.zeros((NB, BD, BD), F32)

        g = g_ref[...]

        def head(h, carry):
            cols = pl.ds(pl.multiple_of(h * BD, BD), BD)
            lb = _lower_bound(lg_ref[:, cols])
            ck = _hg_tile(q_ref[0, :, cols], f_ref[0, :, cols], lb)
            q = ck["q"]
            vb = _chunks(v_ref[0, :, cols]).astype(BF16)
            gb = _chunks(gb_ref[0, :, cols])
            o = _chunks(o_ref[:, cols])
            dyb_v = _chunks(dyb_ref[:, cols])
            s_in = st_ref[h]

            sgb = _sigmoid(gb)
            r = lax.rsqrt(jnp.mean(o * o, axis=-1, keepdims=True) + EPS)
            ohat = o * r
            d_on = dyb_v * (gb * sgb)
            d_gb = dyb_v * (ohat * g) * (sgb * (1.0 + gb * (1.0 - sgb)))
            ghg_ref[...] += jnp.sum(jnp.sum(d_on * ohat, axis=1), axis=0, keepdims=True)
            tt = d_on * g
            d_o = r * (tt - ohat * jnp.mean(tt * ohat, axis=-1, keepdims=True))
            dob = d_o.astype(BF16)

            attb = ck["att"].astype(BF16)
            q_inb, k_inb = ck["q_in"].astype(BF16), ck["k_in"].astype(BF16)
            q_stb, k_stb = ck["q_st"].astype(BF16), ck["k_st"].astype(BF16)
            d_att = jnp.where(_tri(False), _bdot(dob, vb, BNT_DIMS), 0.0).astype(BF16)
            d_q_in = _bdot(d_att, k_inb, BNN_DIMS)
            d_k_in = _bdot(d_att, q_inb, BTN_DIMS)
            d_q_st = _bdot(dob, s_in.astype(BF16), BNN_DIMS)
            qdo = _bdot(dob, q_stb, BTN_DIMS)
            d_states = [None] * HG_NC + [dst[h]]
            for c in reversed(range(HG_NC)):
                d_states[c] = d_states[c + 1] * ck["dc"][c] + qdo[c]
            dst[h] = d_states[0]
            ds_out = jnp.stack(d_states[1:], axis=0)
            dsb = ds_out.astype(BF16)
            d_v = _bdot(attb, dob, BTN_DIMS) + _bdot(k_stb, dsb, BNT_DIMS)
            d_k_st = _bdot(vb, dsb, BNN_DIMS)
            d_dc = jnp.sum(ds_out * s_in, axis=1, keepdims=True)

            p_qi = d_q_in * ck["q_in"]
            p_ki = d_k_in * ck["k_in"]
            p_qs = d_q_st * ck["q_st"]
            p_ks = d_k_st * ck["k_st"]
            d_qh = (d_q_in * ck["e_qi"] + d_q_st * ck["e_qs"]) * HG_SCALE
            d_k = d_k_in * ck["e_ki"] + d_k_st * ck["e_ks"]
            d_b = (p_qi - p_ki) + (p_qs - p_ks)
            d_b_mid = jnp.sum(p_ki - p_qi, axis=1, keepdims=True)
            d_b_last = jnp.sum(p_ks, axis=1, keepdims=True) + d_dc * ck["dc"]
            rowi = lax.broadcasted_iota(jnp.int32, (HG_NC, CHUNK, BD), 1)
            d_b = d_b + jnp.where(rowi == CHUNK // 2, d_b_mid, 0.0) + jnp.where(rowi == CHUNK - 1, d_b_last, 0.0)
            d_logf = _tri_sums(True, d_b)
            d_f = d_logf / ck["f"] - d_k
            sig, sq = ck["sig"], ck["sq"]
            d_fp = d_f * (1.0 - lb) * (sig * (1.0 - sig))
            dlb[0:1, cols] += jnp.sum(jnp.sum(d_f * (1.0 - sig), axis=1), axis=0, keepdims=True)
            d_q = d_qh * (sq * (1.0 + q * (1.0 - sq)))
            dz_ref[0, :, cols] = d_q.astype(BF16).reshape(t, BD)
            dz_ref[1, :, cols] = d_fp.astype(BF16).reshape(t, BD)
            dz_ref[2, :, cols] = d_v.astype(BF16).reshape(t, BD)
            dz_ref[3, :, cols] = d_gb.astype(BF16).reshape(t, BD)
            return carry

        lax.fori_loop(0, NB, head, 0, unroll=2)

        @pl.when((b == nb - 1) & (s == ns - 1))
        def _():
            lb = _lower_bound(lg_ref[...])
            dl = dlb[0:1, :] * (lb * (1.0 - lb))
            glg_ref[0:1, :] = dl
            glg_ref[1:2, :] = -dl

    rb = lambda b, s: b * ns + (ns - 1 - s)
    seg = lambda j: pl.BlockSpec((1, t, D), lambda b, s: (j, rb(b, s), 0))
    tile = pl.BlockSpec((t, D), lambda b, s: (rb(b, s), 0))
    return pl.pallas_call(
        body, name="hgrn_bwd", grid=(nb, ns),
        in_specs=[seg(2), seg(3), seg(4), seg(5), tile,
                  pl.BlockSpec((NB, HG_NC, BD, BD), lambda b, s: (b, ns - 1 - s, 0, 0)),
                  tile, pl.BlockSpec((2, D), lambda b, s: (0, 0)), pl.BlockSpec((1, BD), lambda b, s: (0, 0))],
        out_specs=[pl.BlockSpec((4, t, D), lambda b, s: (0, rb(b, s), 0)),
                   pl.BlockSpec((2, D), lambda b, s: (0, 0)), pl.BlockSpec((1, BD), lambda b, s: (0, 0))],
        out_shape=[SDS((4, n, D), BF16), SDS((2, D), F32), SDS((1, BD), F32)],
        scratch_shapes=[pltpu.VMEM((NB, BD, BD), F32), pltpu.VMEM((8, D), F32)],
        compiler_params=_params(60),
    )(z, z, z, z, o_all, st_all, dyb, lb_logits, hg_g)


def _mid(ya, yb, z, b_merge, x2, tgt, fin_g, pa, pb, wo):
    n = x2.shape[0]
    tm = 256
    ni = n // tm

    def body(ya_ref, yb_ref, gma_ref, gmb_ref, bm_ref, x_ref, t_ref, fg_ref, pa_hbm, pb_hbm, wo_hbm,
             dx2_ref, dya_ref, dyb_ref, dgm_ref, loss_ref, gfg_ref, gbm_ref, gm_hbm,
             pa_v, pb_v, wo_v, gpa_v, gpb_v, gwo_v, sem):
        i = pl.program_id(0)
        by_owner = lambda g: g.reshape(NB, BD, D)
        loads = [pltpu.make_async_copy(src, dst, sem.at[k])
                 for k, (src, dst) in enumerate(((pa_hbm, pa_v), (pb_hbm, pb_v), (wo_hbm, wo_v)))]
        stores = [pltpu.make_async_copy(src, dst, sem.at[k])
                  for k, (src, dst) in enumerate((g, gm_hbm.at[:, pl.ds(slot * BD, BD), :])
                                                 for slot, g in enumerate((gpa_v, gpb_v, gwo_v)))]

        @pl.when(i == 0)
        def _():
            for cp in loads:
                cp.start()
            for ref in (gpa_v, gpb_v, gwo_v, loss_ref, gfg_ref, gbm_ref):
                ref[...] = jnp.zeros(ref.shape, F32)
            for cp in loads:
                cp.wait()

        ya_v = ya_ref[...]
        yb_v = yb_ref[...]
        out_a = jnp.dot(ya_v, pa_v[...], preferred_element_type=F32)
        out_b = jnp.dot(yb_v, pb_v[...], preferred_element_type=F32)
        bm = bm_ref[...]
        g_a = _sigmoid(gma_ref[0] + bm[:, 0:D])
        g_b = _sigmoid(gmb_ref[0] + bm[:, D:2 * D])
        mixed = g_a * out_a + g_b * out_b
        mixb = mixed.astype(BF16)
        xo = x_ref[...] + jnp.dot(mixb, wo_v[...], preferred_element_type=F32)
        r = lax.rsqrt(jnp.mean(xo * xo, axis=-1, keepdims=True) + EPS)
        xn = xo * r
        fg = fg_ref[...]
        e = xn * fg - t_ref[...]
        loss_ref[...] += 0.5 * jnp.sum(jnp.mean(e * e, axis=-1, keepdims=True))
        dy = e * (1.0 / D)
        gfg_ref[...] += jnp.sum(dy * xn, axis=0, keepdims=True)
        dxn = dy * fg
        dx2 = r * (dxn - xn * jnp.mean(dxn * xn, axis=-1, keepdims=True))
        dx2_ref[...] = dx2
        dx2b = dx2.astype(BF16)
        d_mixed = lax.dot_general(dx2b, wo_v[...], NT_DIMS, preferred_element_type=F32)
        gwo_v[...] += by_owner(lax.dot_general(mixb, dx2b, TN_DIMS, preferred_element_type=F32))
        d_oa = (d_mixed * g_a).astype(BF16)
        d_ob = (d_mixed * g_b).astype(BF16)
        dgm_a = (d_mixed * out_a) * (g_a * (1.0 - g_a))
        dgm_b = (d_mixed * out_b) * (g_b * (1.0 - g_b))
        gbm_ref[:, 0:D] += jnp.sum(dgm_a, axis=0, keepdims=True)
        gbm_ref[:, D:2 * D] += jnp.sum(dgm_b, axis=0, keepdims=True)
        dgm_ref[0] = dgm_a.astype(BF16)
        dgm_ref[1] = dgm_b.astype(BF16)
        dya_ref[...] = lax.dot_general(d_oa, pa_v[...], NT_DIMS, preferred_element_type=F32)
        dyb_ref[...] = lax.dot_general(d_ob, pb_v[...], NT_DIMS, preferred_element_type=F32)
        gpa_v[...] += by_owner(lax.dot_general(ya_v, d_oa, TN_DIMS, preferred_element_type=F32))
        gpb_v[...] += by_owner(lax.dot_general(yb_v, d_ob, TN_DIMS, preferred_element_type=F32))

        @pl.when(i == ni - 1)
        def _():
            for cp in stores:
                cp.start()
            for cp in stores:
                cp.wait()

    rows = pl.BlockSpec((tm, D), lambda i: (i, 0))
    rep = lambda shape: pl.BlockSpec(shape, lambda i: (0,) * len(shape))
    return pl.pallas_call(
        body, name="mid", grid=(ni,),
        in_specs=[rows, rows,
                  pl.BlockSpec((1, tm, D), lambda i: (6, i, 0)), pl.BlockSpec((1, tm, D), lambda i: (7, i, 0)),
                  rep((1, 2 * D)), rows, rows, rep((1, D)), ANY, ANY, ANY],
        out_specs=[rows, rows, rows, pl.BlockSpec((2, tm, D), lambda i: (0, i, 0)),
                   rep((8, BD)), rep((1, D)), rep((1, 2 * D)), ANY],
        out_shape=[SDS((n, D), F32), SDS((n, D), F32), SDS((n, D), F32), SDS((2, n, D), BF16),
                   SDS((8, BD), F32), SDS((1, D), F32), SDS((1, 2 * D), F32),
                   SDS((NB, MID_ROWS, D), F32)],
        scratch_shapes=[pltpu.VMEM((D, D), BF16)] * 3 + [pltpu.VMEM((NB, BD, D), F32)] * 3 + [pltpu.SemaphoreType.DMA((3,))],
        compiler_params=_params(60),
    )(ya, yb, z, z, b_merge, x2, tgt, fin_g, pa, pb, wo)


def _dz_specs(tm, ni, row_major):
    if row_major:
        ia = lambda i, j: (jnp.minimum(j, 1), i, 0)
        ib = lambda i, j: (jnp.clip(j - 2, 0, 3), i, 0)
        im = lambda i, j: (jnp.clip(j - 6, 0, 1), i, 0)
    else:
        last = ni - 1
        ia = lambda j, i: (jnp.minimum(j, 1), jnp.where(j < 2, i, last), 0)
        ib = lambda j, i: (jnp.clip(j - 2, 0, 3), jnp.where(j < 2, 0, jnp.where(j < 6, i, last)), 0)
        im = lambda j, i: (jnp.clip(j - 6, 0, 1), jnp.where(j < 6, 0, i), 0)
    return [pl.BlockSpec((1, tm, D), f) for f in (ia, ib, im)]


def _inproj_bwd_x(dza, dzb, dzm, w_all, x2, dx2, norm_g, after):
    n = x2.shape[0]
    tm = 512
    ni = n // tm

    def body(dza_ref, dzb_ref, dzm_ref, w_ref, x_ref, dx2_ref, g_ref, after_ref, gx_ref, gg_ref, acc):
        i, j = pl.program_id(0), pl.program_id(1)

        @pl.when((i == 0) & (j == 0))
        def _():
            gg_ref[...] = jnp.zeros((1, D), F32)

        @pl.when(j == 0)
        def _():
            acc[...] = jnp.zeros((tm, D), F32)

        def add(ref):
            acc[...] += lax.dot_general(ref[0], w_ref[0], NT_DIMS, preferred_element_type=F32)

        pl.when(j < 2)(lambda: add(dza_ref))
        pl.when((j >= 2) & (j < 6))(lambda: add(dzb_ref))
        pl.when(j >= 6)(lambda: add(dzm_ref))

        @pl.when(j == NB - 1)
        def _():
            x = x_ref[...]
            r = lax.rsqrt(jnp.mean(x * x, axis=-1, keepdims=True) + EPS)
            xn = x * r
            dh = acc[...]
            gg_ref[...] += jnp.sum(dh * xn, axis=0, keepdims=True)
            dxn = dh * g_ref[...]
            gx_ref[...] = dx2_ref[...] + r * (dxn - xn * jnp.mean(dxn * xn, axis=-1, keepdims=True))

    rows = pl.BlockSpec((tm, D), lambda i, j: (i, 0))
    return pl.pallas_call(
        body, name="inproj_bwd_x", grid=(ni, NB),
        in_specs=_dz_specs(tm, ni, True) + [pl.BlockSpec((1, D, D), lambda i, j: (j, 0, 0)), rows, rows,
                                             pl.BlockSpec((1, D), lambda i, j: (0, 0)), ANY],
        out_specs=[rows, pl.BlockSpec((1, D), lambda i, j: (0, 0))],
        out_shape=[SDS((n, D), F32), SDS((1, D), F32)],
        scratch_shapes=[pltpu.VMEM((tm, D), F32)],
        compiler_params=_params(48),
    )(dza, dzb, dzm, w_all, x2, dx2, norm_g, after)


def _walk_tables(order, ni):
    rows = []
    for lo, hi in ((0, 2), (2, 6), (6, 8)):
        active = [j for j, g in enumerate(order) if lo <= g < hi]
        block, tile = [], []
        for j, g in enumerate(order):
            before = [a for a in active if a < j]
            if lo <= g < hi:
                block.append(g - lo), tile.append(-1)
            elif before:
                block.append(order[before[-1]] - lo), tile.append(ni - 1)
            else:
                block.append(order[active[0]] - lo), tile.append(0)
        rows += [block, tile]
    return rows


def _inproj_bwd_w(core, dza, dzb, dzm, h_all, g_m):
    n = h_all.shape[0]
    tm = min(n, 2048)
    ni = n // tm
    packed = g_m.shape[1:]
    orders = [[2 * q + 1 - c for q in range(4)] + [2 * q + c for q in range(4)] for c in (0, 1)]
    tables = jnp.asarray([[order] + _walk_tables(order, ni) for order in orders], jnp.int32)
    walk = jnp.where(core == 0, tables[0], tables[1])

    def body(walk_ref, dza_ref, dzb_ref, dzm_ref, h_ref, gm_hbm, out_bf, own_f32, m_out_bf, m_own_f32, got_w, got_m,
             acc, stage, theirs, m_mine, m_theirs, m_stage, send_sems, recv_sems, local_sems):
        j, i = pl.program_id(0), pl.program_id(1)
        group = walk_ref[0, j]
        x, y, c = _place()
        sibling = (x, y, 1 - c)

        def send_w(q):
            return pltpu.make_async_remote_copy(
                src_ref=stage.at[q % 2], dst_ref=got_w.at[q], send_sem=send_sems.at[q], recv_sem=recv_sems.at[q],
                device_id=sibling, device_id_type=MESH)

        def send_m(q):
            return pltpu.make_async_remote_copy(
                src_ref=gm_hbm.at[2 * q + (1 - c)], dst_ref=got_m.at[q], send_sem=send_sems.at[4 + q],
                recv_sem=recv_sems.at[4 + q], device_id=sibling, device_id_type=MESH)

        def fetch(q):
            return pltpu.make_async_copy(got_w.at[q], theirs, local_sems.at[0])

        def fetch_m(q):
            return (pltpu.make_async_copy(gm_hbm.at[2 * q + c], m_mine, local_sems.at[2]),
                    pltpu.make_async_copy(got_m.at[q], m_theirs, local_sems.at[3]))

        @pl.when((j == 0) & (i == 0))
        def _():
            for q in range(4):
                send_m(q).start()

        @pl.when(i == 0)
        def _():
            acc[...] = jnp.zeros((D, D), F32)

        def add(ref):
            acc[...] += lax.dot_general(h_ref[...], ref[0], TN_DIMS, preferred_element_type=F32)

        pl.when(group < 2)(lambda: add(dza_ref))
        pl.when((group >= 2) & (group < 6))(lambda: add(dzb_ref))
        pl.when(group >= 6)(lambda: add(dzm_ref))

        for q in range(4):
            @pl.when((i == ni - 1) & (j == q))
            def _(q=q):
                if q >= 2:
                    send_w(q - 2).wait_send()
                stage[q % 2] = acc[...].astype(BF16)
                send_w(q).start()

        def reducer(q):
            other_x, other_y = x != q // 2, y != q % 2
            return other_x | other_y, jnp.where(other_x & other_y, 2, jnp.where(other_x, 0, 1))

        def w_out(q):
            return pltpu.make_async_copy(stage.at[0], out_bf.at[reducer(q)[1]], local_sems.at[1])

        for q in range(4):
            @pl.when((j == 4 + q) & (i == 0))
            def _(q=q):
                send_w(q).wait_recv()
                send_m(q).wait_recv()
                fetch(q).start()
                for cp in fetch_m(q):
                    cp.start()
                if q > 0:
                    pl.when(reducer(q - 1)[0])(lambda: w_out(q - 1).wait())

            @pl.when((j == 4 + q) & (i == ni - 1))
            def _(q=q):
                if q == 0:
                    send_w(2).wait_send()
                    send_w(3).wait_send()
                other, slot = reducer(q)
                m_out = pltpu.make_async_copy(m_stage, m_out_bf.at[slot], local_sems.at[4])
                m_own = pltpu.make_async_copy(m_mine, m_own_f32, local_sems.at[4])

                for cp in fetch_m(q):
                    cp.wait()

                @pl.when(other)
                def _():
                    m_stage[...] = (m_mine[...] + m_theirs[...]).astype(BF16)
                    m_out.start()

                @pl.when(jnp.logical_not(other))
                def _():
                    m_mine[...] += m_theirs[...]
                    m_own.start()

                fetch(q).wait()

                @pl.when(other)
                def _():
                    stage[0] = (acc[...] + theirs[...].astype(F32)).astype(BF16)
                    w_out(q).start()
                    if q == 3:
                        w_out(q).wait()
                    m_out.wait()

                @pl.when(jnp.logical_not(other))
                def _():
                    acc[...] += theirs[...].astype(F32)
                    out = pltpu.make_async_copy(acc, own_f32, local_sems.at[1])
                    out.start()
                    out.wait()
                    m_own.wait()

        @pl.when((j == NB - 1) & (i == ni - 1))
        def _():
            for q in range(4):
                send_m(q).wait_send()

    def dz_spec(k):
        return pl.BlockSpec((1, tm, D), lambda j, i, w: (w[1 + 2 * k, j], jnp.where(w[2 + 2 * k, j] < 0, i, w[2 + 2 * k, j]), 0))

    return pl.pallas_call(
        body, name="inproj_bwd_w",
        grid_spec=pltpu.PrefetchScalarGridSpec(
            num_scalar_prefetch=1, grid=(NB, ni),
            in_specs=[dz_spec(0), dz_spec(1), dz_spec(2), pl.BlockSpec((tm, D), lambda j, i, w: (i, 0)), ANY],
            out_specs=[ANY] * 6,
            scratch_shapes=[pltpu.VMEM((D, D), F32), pltpu.VMEM((2, D, D), BF16), pltpu.VMEM((D, D), BF16),
                            pltpu.VMEM(packed, F32), pltpu.VMEM(packed, F32), pltpu.VMEM(packed, BF16),
                            pltpu.SemaphoreType.DMA((8,)), pltpu.SemaphoreType.DMA((8,)), pltpu.SemaphoreType.DMA((5,))]),
        out_shape=[SDS((3, D, D), BF16), SDS((D, D), F32), SDS((3,) + packed, BF16), SDS(packed, F32),
                   SDS((4, D, D), BF16), SDS((4,) + packed, F32)],
        compiler_params=_params(58),
    )(walk, dza, dzb, dzm, h_all, g_m)


def _adamw(w, g, m, v):
    rows, cols = w.shape
    tr = _row_tile(rows)

    spec = pl.BlockSpec((tr, cols), lambda i: (i, 0))
    return pl.pallas_call(
        functools.partial(_adam_refs), name="adamw", grid=(rows // tr,), in_specs=[spec] * 4, out_specs=[spec] * 3,
        out_shape=[SDS((rows, cols), F32)] * 3, compiler_params=_params(32),
    )(w, g, m, v)


def _adamw_chips(w, own, b_in, m, v):
    rows, cols = w.shape
    tr = _row_tile(rows)

    def body(w_ref, p_ref, b0_ref, b1_ref, b2_ref, m_ref, v_ref, g_ref, d_ref, nm_ref, nv_ref):
        g_ref[...] = ((p_ref[...] + b0_ref[0].astype(F32)) + b1_ref[0].astype(F32)) + b2_ref[0].astype(F32)
        _adam_refs(w_ref, g_ref, m_ref, v_ref, d_ref, nm_ref, nv_ref)

    slot = lambda k: pl.BlockSpec((1, tr, cols), lambda i: (k, i, 0))
    spec = pl.BlockSpec((tr, cols), lambda i: (i, 0))
    return pl.pallas_call(
        body, name="adamw_chips", grid=(rows // tr,), in_specs=[spec, spec, slot(0), slot(1), slot(2), spec, spec],
        out_specs=[spec] * 4, out_shape=[SDS((rows, cols), F32)] * 4, compiler_params=_params(32),
    )(w, own, b_in, b_in, b_in, m, v)


def _adam_refs(w_ref, g_ref, m_ref, v_ref, d_ref, nm_ref, nv_ref):
    gv = g_ref[...]
    nm = ADAM_B1 * m_ref[...] + (1.0 - ADAM_B1) * gv
    nv = ADAM_B2 * v_ref[...] + (1.0 - ADAM_B2) * (gv * gv)
    m_hat = nm / (1.0 - ADAM_B1 ** ADAM_STEP)
    v_hat = nv / (1.0 - ADAM_B2 ** ADAM_STEP)
    d_ref[...] = -ADAM_LR * (m_hat / (jnp.sqrt(v_hat) + ADAM_EPS) + ADAM_WD * w_ref[...])
    nm_ref[...] = nm
    nv_ref[...] = nv


def _adamw_small(ws, gs, ms, vs):
    k = len(ws)

    def body(*refs):
        ins, outs = refs[:4 * k], refs[4 * k:7 * k]
        vin, vout = refs[7 * k:11 * k], refs[11 * k:14 * k]
        load_sems, store_sems = refs[14 * k:]
        loads = [pltpu.make_async_copy(ins[i], vin[i], load_sems.at[i]) for i in range(4 * k)]
        for cp in loads:
            cp.start()
        for cp in loads:
            cp.wait()
        for i in range(k):
            _adam_refs(*[vin[part * k + i] for part in range(4)], *[vout[part * k + i] for part in range(3)])
        stores = [pltpu.make_async_copy(vout[i], outs[i], store_sems.at[i]) for i in range(3 * k)]
        for cp in stores:
            cp.start()
        for cp in stores:
            cp.wait()

    shapes = [SDS(w.shape, F32) for w in ws]
    vmem = [pltpu.VMEM(w.shape, F32) for w in ws]
    out = pl.pallas_call(
        body, name="adamw_small", out_shape=shapes * 3, in_specs=[HBM] * (4 * k), out_specs=[HBM] * (3 * k),
        scratch_shapes=vmem * 7 + [pltpu.SemaphoreType.DMA((4 * k,)), pltpu.SemaphoreType.DMA((3 * k,))],
        compiler_params=_params(32),
    )(*ws, *gs, *ms, *vs)
    return out[:k], out[k:2 * k], out[2 * k:]


def _allgather(blocks, dtypes, name):
    na = len(blocks)

    def body(*refs):
        ins, outs, stages = refs[:na], refs[na:2 * na], refs[2 * na:3 * na]
        send_sems, recv_sems, local_sems = refs[3 * na:]
        x, y, c = _place()
        me, sibling = (x, y, c), (x, y, 1 - c)
        chips = [(1 - x, y), (x, 1 - y), (1 - x, 1 - y)]
        blk = lambda p: 4 * p[0] + 2 * p[1] + p[2]

        def copy(a, k, block, to, src=None):
            return pltpu.make_async_remote_copy(
                src_ref=outs[a].at[blk(block)] if src is None else src, dst_ref=outs[a].at[blk(block)],
                send_sem=send_sems.at[7 * a + k], recv_sem=recv_sems.at[7 * a + k],
                device_id=to, device_id_type=MESH)

        mine, first, passed = [], [], []
        for a in range(na):
            stages[a][...] = ins[a][...].astype(dtypes[a])
            mine.append(pltpu.make_async_copy(stages[a], outs[a].at[blk(me)], local_sems.at[a]))
            mine[-1].start()
            first.append(copy(a, 0, me, sibling, src=stages[a]))
            first += [copy(a, 1 + j, me, (*chip, c), src=stages[a]) for j, chip in enumerate(chips)]
        for cp in first:
            cp.start()
        for j, chip in enumerate(chips):
            for a in range(na):
                copy(a, 1 + j, (*chip, c), me).wait_recv()
                passed.append(copy(a, 4 + j, (*chip, c), sibling))
                passed[-1].start()
        for a in range(na):
            copy(a, 0, sibling, me).wait_recv()
            for j, chip in enumerate(chips):
                copy(a, 4 + j, (*chip, 1 - c), me).wait_recv()
        for cp in first + passed:
            cp.wait_send()
        for cp in mine:
            cp.wait()

    return pl.pallas_call(
        body, name=name,
        in_specs=[pl.BlockSpec(memory_space=pltpu.VMEM)] * na, out_specs=[ANY] * na,
        out_shape=[SDS((NB,) + b.shape, dt) for b, dt in zip(blocks, dtypes)],
        scratch_shapes=[pltpu.VMEM(b.shape, dt) for b, dt in zip(blocks, dtypes)]
        + [pltpu.SemaphoreType.DMA((7 * na,)), pltpu.SemaphoreType.DMA((7 * na,)), pltpu.SemaphoreType.DMA((na,))],
        compiler_params=_params(40),
    )(*blocks)


HBM = pl.BlockSpec(memory_space=pltpu.HBM)
SEMS = pl.BlockSpec(memory_space=pltpu.SEMAPHORE)
EFFECT = pltpu.SideEffectType.DATAFLOW_SIDE_EFFECTING


def _chip_copies(srcs, lands, send_sems, recv_sems):
    x, y, c = _place()
    return [pltpu.make_async_remote_copy(
        src_ref=srcs[a].at[slot], dst_ref=lands[a].at[slot],
        send_sem=send_sems.at[3 * a + slot], recv_sem=recv_sems.at[3 * a + slot],
        device_id=(px, py, c), device_id_type=MESH)
        for a in range(len(srcs)) for slot, (px, py) in enumerate(_other_chips(x, y))]


def _split_start(name, copies, per_array, srcs, lands, after=None):
    na = len(srcs)

    def body(*refs):
        send_sems, recv_sems = refs[-2 * na - 3], refs[-2 * na - 2]
        for cp in copies(refs[:na], refs[na:2 * na], send_sems, recv_sems):
            cp.start()
        refs[-1][...] = jnp.zeros_like(refs[-1])

    hbm = lambda a: pltpu.HBM(a.shape, a.dtype)
    pin = lambda a: pltpu.with_memory_space_constraint(a, pltpu.HBM)
    out = pl.pallas_call(
        body, name=name,
        out_shape=(pltpu.SemaphoreType.DMA((per_array * na,)), pltpu.SemaphoreType.DMA((per_array * na,)),
                   *[hbm(a) for a in srcs], *[hbm(a) for a in lands], SDS((8, BD), F32)),
        in_specs=[HBM] * (2 * na) + ([] if after is None else [ANY]),
        out_specs=(SEMS, SEMS, *[HBM] * (2 * na), pl.BlockSpec(memory_space=pltpu.VMEM)),
        input_output_aliases={i: 2 + i for i in range(2 * na)},
        compiler_params=pltpu.CompilerParams(has_side_effects=EFFECT),
    )(*[pin(a) for a in srcs], *[pin(a) for a in lands], *([] if after is None else [after]))
    return out[0], out[1], out[2:2 + na], out[2 + na:2 + 2 * na], out[-1]


def _split_wait(name, copies, started, after):
    send_sems, recv_sems, srcs, lands, _ = started
    na = len(srcs)

    def body(*refs):
        waits = copies(refs[:na], refs[na:2 * na], refs[2 * na], refs[2 * na + 1])
        for cp in waits:
            cp.wait_send()
        for cp in waits:
            cp.wait_recv()

    hbm = lambda a: pltpu.HBM(a.shape, a.dtype)
    out = pl.pallas_call(
        body, name=name,
        out_shape=(*[hbm(a) for a in srcs], *[hbm(a) for a in lands]),
        in_specs=[HBM] * (2 * na) + [SEMS, SEMS, ANY],
        out_specs=tuple([HBM] * (2 * na)),
        input_output_aliases={i: i for i in range(2 * na)},
        compiler_params=pltpu.CompilerParams(has_side_effects=EFFECT),
    )(*srcs, *lands, send_sems, recv_sems, after)
    return out[na:]


def _add_chips(own, b_in):
    r, cols = own.shape
    tr = _row_tile(r)

    def body(p_ref, b0_ref, b1_ref, b2_ref, o_ref):
        o_ref[...] = ((p_ref[...] + b0_ref[0].astype(F32)) + b1_ref[0].astype(F32)) + b2_ref[0].astype(F32)

    slot = lambda k: pl.BlockSpec((1, tr, cols), lambda i: (k, i, 0))
    spec = pl.BlockSpec((tr, cols), lambda i: (i, 0))
    return pl.pallas_call(
        body, name="add_chips", grid=(r // tr,), in_specs=[spec, slot(0), slot(1), slot(2)], out_specs=spec,
        out_shape=SDS((r, cols), F32), compiler_params=_params(32),
    )(own, b_in, b_in, b_in)


VEC_NAMES = ("b_merge", "conv_b", "rg_bx", "rg_ba", "rg_lambda", "hg_lb_logits", "hg_norm_g", "final_norm_g")
REP_NAMES = ("rg_wx", "rg_wa", "norm_g") + VEC_NAMES
SMALL_AT = 3 * BD
SMALL_ROWS = 48
MID_ROWS = 448


def _sum_blocks(parts):
    def body(p_ref, o_ref):
        acc = p_ref[0]
        for k in range(1, NB):
            acc = acc + p_ref[k]
        o_ref[...] = acc

    return pl.pallas_call(body, name="sum_blocks", out_shape=SDS(parts.shape[1:], F32))(parts)


def _pack_rows(arrays, width, row_multiple=8):
    flat = jnp.concatenate([a.reshape(-1) for a in arrays])
    rows = -(-flat.shape[0] // width)
    rows = -(-rows // row_multiple) * row_multiple
    return jnp.pad(flat, (0, rows * width - flat.shape[0])).reshape(rows, width)


def _unpack(flat, like):
    out, off = [], 0
    for a in like:
        out.append(flat[off:off + a.size].reshape(a.shape))
        off += a.size
    return out


def kernel(x, w_in, b_merge, conv_w, conv_b, rg_wx, rg_bx, rg_wa, rg_ba, rg_lambda, hg_lb_logits, hg_norm_g, proj_a, proj_b, w_out, norm_g, final_norm_g, loss_target, m_w_in, m_b_merge, m_conv_w, m_conv_b, m_rg_wx, m_rg_bx, m_rg_wa, m_rg_ba, m_rg_lambda, m_hg_lb_logits, m_hg_norm_g, m_proj_a, m_proj_b, m_w_out, m_norm_g, m_final_norm_g, v_w_in, v_b_merge, v_conv_w, v_conv_b, v_rg_wx, v_rg_bx, v_rg_wa, v_rg_ba, v_rg_lambda, v_hg_lb_logits, v_hg_norm_g, v_proj_a, v_proj_b, v_w_out, v_norm_g, v_final_norm_g):
    weights = dict(w_in=w_in, b_merge=b_merge, conv_w=conv_w, conv_b=conv_b, rg_wx=rg_wx, rg_bx=rg_bx, rg_wa=rg_wa,
                   rg_ba=rg_ba, rg_lambda=rg_lambda, hg_lb_logits=hg_lb_logits, hg_norm_g=hg_norm_g, proj_a=proj_a,
                   proj_b=proj_b, w_out=w_out, norm_g=norm_g, final_norm_g=final_norm_g)
    mom1 = dict(w_in=m_w_in, b_merge=m_b_merge, conv_w=m_conv_w, conv_b=m_conv_b, rg_wx=m_rg_wx, rg_bx=m_rg_bx,
                rg_wa=m_rg_wa, rg_ba=m_rg_ba, rg_lambda=m_rg_lambda, hg_lb_logits=m_hg_lb_logits,
                hg_norm_g=m_hg_norm_g, proj_a=m_proj_a, proj_b=m_proj_b, w_out=m_w_out, norm_g=m_norm_g,
                final_norm_g=m_final_norm_g)
    mom2 = dict(w_in=v_w_in, b_merge=v_b_merge, conv_w=v_conv_w, conv_b=v_conv_b, rg_wx=v_rg_wx, rg_bx=v_rg_bx,
                rg_wa=v_rg_wa, rg_ba=v_rg_ba, rg_lambda=v_rg_lambda, hg_lb_logits=v_hg_lb_logits,
                hg_norm_g=v_hg_norm_g, proj_a=v_proj_a, proj_b=v_proj_b, w_out=v_w_out, norm_g=v_norm_g,
                final_norm_g=v_final_norm_g)
    order = list(weights)
    nb, s_len, _ = x.shape
    n = nb * s_len
    px, py, pc = _place()

    in_hbm = lambda a: pltpu.with_memory_space_constraint(a, pltpu.HBM)
    norm_gain = in_hbm(norm_g)

    x2 = x.reshape(n, D)
    cw_blk = jnp.pad(conv_w[0], ((0, 4), (0, 0)))
    order_ids = jnp.stack([_block_id(p) for p in _arrival_order(px, py, pc)]).astype(jnp.int32)
    z, h_all, w_all, pa_all, pb_all, wo_all, cw_all = _gather_inproj(
        order_ids, x2, norm_gain, [w_in[0], proj_a[0], proj_b[0], w_out[0], cw_blk], [BF16, BF16, BF16, BF16, F32])
    pa_full, pb_full, wo_full = (a.reshape(D, D) for a in (pa_all, pb_all, wo_all))
    cw8 = in_hbm(cw_all.transpose(1, 0, 2).reshape(8, D))
    wx_b, wa_b = in_hbm(rg_wx[0].astype(BF16)), in_hbm(rg_wa[0].astype(BF16))
    cb, bx, ba, lam = (in_hbm(a.reshape(1, D)) for a in (conv_b, rg_bx, rg_ba, rg_lambda))
    fin_g, b_mrg = in_hbm(final_norm_g.reshape(1, D)), in_hbm(b_merge)
    lb_lg, hg_g = in_hbm(hg_lb_logits), in_hbm(hg_norm_g)

    hlru, ya = _lru_fwd(z, cw8, cb, wx_b, wa_b, bx, ba, lam, nb, s_len)
    o_all, yb, st_all = _hgrn_fwd(z, lb_lg, hg_g, nb, s_len)

    (dx2, dya, dyb, dzm, loss_acc, g_fin, g_bm, g_mid) = _mid(
        ya, yb, z, b_mrg, x2, loss_target.reshape(n, D), fin_g, pa_full, pb_full, wo_full)
    dzb, g_lg, g_hg = _hgrn_bwd(z, o_all, st_all, dyb, lb_lg, hg_g, nb, s_len)
    dza, g_cw8, g_cb, g_wx, g_wa, g_bx, g_ba, g_lam = _lru_bwd(
        z, hlru, dya, cw8, cb, wx_b, wa_b, bx, ba, lam, nb, s_len)

    part = dict(b_merge=g_bm, conv_b=g_cb, rg_bx=g_bx, rg_ba=g_ba, rg_lambda=g_lam, hg_lb_logits=g_lg,
                hg_norm_g=g_hg, final_norm_g=g_fin)
    vec = _pack_rows([part[k] for k in VEC_NAMES], BD)
    vec = jnp.pad(vec, ((0, 16 * NB - vec.shape[0]), (0, 0))).reshape(NB, 2, D)
    rows8 = lambda a: jnp.pad(a, ((0, 0), (0, 8 - a.shape[1]), (0, 0)))
    small = jnp.concatenate([g_wx.reshape(NB, 16, D), g_wa.reshape(NB, 16, D),
                             rows8(g_cw8.reshape(8, NB, BD).transpose(1, 0, 2).reshape(NB, 1, D)), rows8(vec),
                             jnp.zeros((NB, MID_ROWS - SMALL_AT - SMALL_ROWS, D), F32)], axis=1)
    g_m = lax.dynamic_update_slice(g_mid, small, (0, SMALL_AT, 0))
    w_out_bf, w_own, m_out_bf, m_own, _, _ = _inproj_bwd_w(pc, dza, dzb, dzm, h_all, g_m)
    outgoing = [w_out_bf, m_out_bf]
    chip_sums = _split_start("rs_chips_start", _chip_copies, 3, outgoing, [lax.empty(a.shape, a.dtype) for a in outgoing])
    grad_x, g_ng = _inproj_bwd_x(dza, dzb, dzm, w_all, x2, dx2, norm_gain, chip_sums[-1])
    from_chips = _split_wait("rs_chips_wait", _chip_copies, chip_sums, grad_x)
    flat2 = lambda a: a.reshape(-1, a.shape[-1])
    r_w, *w_in_step = _adamw_chips(flat2(weights["w_in"]), w_own, from_chips[0], flat2(mom1["w_in"]), flat2(mom2["w_in"]))
    r_m = _add_chips(m_own, from_chips[1])
    row = lax.broadcasted_iota(jnp.int32, (8, D), 0)
    mine = jnp.where(row == 0, g_ng, jnp.where(row == 1, loss_acc[0:1, 0:1], 0.0))
    tail = jnp.concatenate([r_m[SMALL_AT:SMALL_AT + SMALL_ROWS], mine], axis=0)
    (tail_all,) = _allgather([tail], [F32], "gather_small_grads")
    summed = _sum_blocks(tail_all[:, SMALL_ROWS:SMALL_ROWS + 8])

    grads = dict(w_in=r_w.reshape(1, D, D),
                 proj_a=r_m[0:BD].reshape(1, BD, D), proj_b=r_m[BD:2 * BD].reshape(1, BD, D),
                 w_out=r_m[2 * BD:3 * BD].reshape(1, BD, D),
                 conv_w=r_m[SMALL_AT + 32].reshape(8, BD)[0:4].reshape(1, 4, BD),
                 rg_wx=tail_all[:, 0:16].reshape(1, NB, BD, BD), rg_wa=tail_all[:, 16:32].reshape(1, NB, BD, BD),
                 norm_g=summed[0:1])
    vec_all = tail_all[:, 40:42].reshape(-1)
    for k, gk in zip(VEC_NAMES, _unpack(vec_all, [weights[k] for k in VEC_NAMES])):
        grads[k] = gk

    delta, new_m, new_v = {}, {}, {}
    for k in ("w_in", "proj_a", "proj_b", "w_out"):
        outs = w_in_step if k == "w_in" else _adamw(*[flat2(t[k]) for t in (weights, grads, mom1, mom2)])
        delta[k], new_m[k], new_v[k] = (a.reshape(weights[k].shape) for a in outs)
    rep = list(REP_NAMES) + ["conv_w"]
    outs = _adamw_small(*[[flat2(t[k]) for k in rep] for t in (weights, grads, mom1, mom2)])
    for tgt, arrays in zip((delta, new_m, new_v), outs):
        for k, a in zip(rep, arrays):
            tgt[k] = a.reshape(weights[k].shape)

    return (summed[1, 0], grad_x.reshape(x.shape), *[grads[k] for k in order], *[delta[k] for k in order],
            *[new_m[k] for k in order], *[new_v[k] for k in order])
```

```python
import functools

import jax
import jax.numpy as jnp
from jax import lax
from jax.experimental import pallas as pl
from jax.experimental.pallas import tpu as pltpu

F32 = jnp.float32
BF16 = jnp.bfloat16
SDS = jax.ShapeDtypeStruct
MESH = pl.DeviceIdType.MESH
ANY = pl.BlockSpec(memory_space=pl.ANY)

D = 1024
NB = 8
BD = D // NB
CHUNK = 64
EPS = 1e-6
LRU_C = 8.0
HG_SCALE = BD ** -0.5
ADAM_LR, ADAM_B1, ADAM_B2, ADAM_EPS, ADAM_WD, ADAM_STEP = 0.001, 0.9, 0.999, 1e-08, 0.01, 10

NT_DIMS = (((1,), (1,)), ((), ()))
TN_DIMS = (((0,), (0,)), ((), ()))


def _params(vmem_mib):
    return pltpu.CompilerParams(vmem_limit_bytes=vmem_mib << 20)


def _row_tile(rows, most=256):
    assert rows % 8 == 0
    return max(t for t in range(8, min(rows, most) + 1, 8) if rows % t == 0)


def _sigmoid(v):
    return 0.5 * (jnp.tanh(0.5 * v) + 1.0)


def _groups(v):
    return v.reshape(v.shape[0] // 8, 8, v.shape[1])


def _softplus_neg(lam):
    t = -lam
    e = jnp.exp(-jnp.abs(t))
    w = 1.0 + e
    d = w - 1.0
    l1p = jnp.where(d == 0.0, e, jnp.log(w) * (e / jnp.where(d == 0.0, 1.0, d)))
    return jnp.maximum(t, 0.0) + l1p


def _place():
    return lax.axis_index("x"), lax.axis_index("y"), lax.axis_index("c")


def _other_chips(x, y):
    return [(1 - x, y), (x, 1 - y), (1 - x, 1 - y)]


def _block_id(p):
    return 4 * p[0] + 2 * p[1] + p[2]


def _core_chips(x, y, c):
    near, far, diag = _other_chips(x, y)
    pick = lambda a, b: (jnp.where(c == 0, a[0], b[0]), jnp.where(c == 0, a[1], b[1]))
    return [pick(near, far), pick(far, near), diag]


def _arrival_order(x, y, c):
    first, second, diag = _core_chips(x, y, c)
    return [(x, y, c), (x, y, 1 - c), (*first, c), (*second, 1 - c), (*second, c), (*first, 1 - c),
            (*diag, c), (*diag, 1 - c)]


def _gather_inproj(order_ids, x2, norm_g, blocks, dtypes):
    na = len(blocks)
    n = x2.shape[0]
    tm = min(n, 1024)
    ni = n // tm

    def body(order_ref, x_ref, g_ref, *refs):
        ins, (z_ref, h_ref), outs = refs[:na], refs[na:na + 2], refs[na + 2:2 * na + 2]
        stages = refs[2 * na + 2:3 * na + 2]
        h_full, wbuf, send_sems, recv_sems, local_sems, wsems, hsem = refs[3 * na + 2:]
        j, i = pl.program_id(0), pl.program_id(1)
        x, y, c = _place()
        me, sibling = (x, y, c), (x, y, 1 - c)
        chips = _core_chips(x, y, c)
        sibling_chips = [chips[1], chips[0], chips[2]]
        small = range(1, na)

        def copy(a, k, block, to, src=None):
            return pltpu.make_async_remote_copy(
                src_ref=outs[a].at[_block_id(block)] if src is None else src, dst_ref=outs[a].at[_block_id(block)],
                send_sem=send_sems.at[7 * a + k], recv_sem=recv_sems.at[7 * a + k],
                device_id=to, device_id_type=MESH)

        def local(a):
            return pltpu.make_async_copy(stages[a], outs[a].at[_block_id(me)], local_sems.at[a])

        def landed(a, slot):
            copy(a, 1 + slot, (*chips[slot], c), me).wait_recv()
            copy(a, 4 + slot, (*chips[slot], c), sibling).start()
            if slot == 0:
                copy(a, 3, (*chips[0], c), (*chips[1], c)).start()

        def diagonal_and_small():
            landed(0, 2)
            for a in small:
                landed(a, 0)
                landed(a, 1)

        def passed_on(a, slot):
            copy(a, 4 + slot, (*sibling_chips[slot], 1 - c), me).wait_recv()

        def sibling_here_send_second():
            copy(0, 0, sibling, me).wait_recv()
            for a in range(na):
                copy(a, 2, me, (*chips[1], c), src=stages[a]).start()

        @pl.when((j == 0) & (i == 0))
        def _():
            for a in range(na):
                stages[a][...] = ins[a][...].astype(dtypes[a])
                local(a).start()
            for a in range(na):
                copy(a, 0, me, sibling, src=stages[a]).start()
                copy(a, 1, me, (*chips[0], c), src=stages[a]).start()

        @pl.when(j == 0)
        def _():
            xv = x_ref[...]
            r = lax.rsqrt(jnp.mean(xv * xv, axis=-1, keepdims=True) + EPS)
            hb = ((xv * r) * g_ref[...]).astype(BF16)
            h_full[pl.ds(pl.multiple_of(i * tm, tm), tm), :] = hb

        save_h = pltpu.make_async_copy(h_full, h_ref, hsem)
        pl.when((j == 0) & (i == ni - 1))(save_h.start)

        steps = [
            lambda: local(0).wait(),
            sibling_here_send_second,
            lambda: landed(0, 0),
            lambda: passed_on(0, 0),
            lambda: landed(0, 1),
            lambda: passed_on(0, 1),
            diagonal_and_small,
            lambda: passed_on(0, 2),
        ]
        def w_load(k):
            return pltpu.make_async_copy(outs[0].at[order_ref[k]], wbuf.at[k % 2], wsems.at[k % 2])

        for k, step in enumerate(steps):
            @pl.when((j == 0) & (i == 0) if k == 0 else (j == k - 1) & (i == ni - 1))
            def _(k=k, step=step):
                step()
                w_load(k).start()

        pl.when(i == 0)(lambda: w_load(j).wait())
        z_ref[0] = jnp.dot(h_full[pl.ds(pl.multiple_of(i * tm, tm), tm), :], wbuf[j % 2], preferred_element_type=F32)

        @pl.when((j == NB - 1) & (i == ni - 1))
        def _():
            save_h.wait()
            for a in small:
                landed(a, 2)
            for a in small:
                local(a).wait()
                copy(a, 0, sibling, me).wait_recv()
                for slot in range(3):
                    passed_on(a, slot)
            for a in range(na):
                copy(a, 0, me, sibling, src=stages[a]).wait_send()
                for slot, chip in enumerate(chips):
                    copy(a, 1 + slot, me, (*chip, c), src=stages[a]).wait_send()
                    copy(a, 4 + slot, (*chip, c), sibling).wait_send()

    rows_once = lambda j, i, order: (jnp.where(j == 0, i, ni - 1), 0)
    vmem = pl.BlockSpec(memory_space=pltpu.VMEM)
    return pl.pallas_call(
        body, name="gather_inproj",
        grid_spec=pltpu.PrefetchScalarGridSpec(
            num_scalar_prefetch=1, grid=(NB, ni),
            in_specs=[pl.BlockSpec((tm, D), rows_once), pl.BlockSpec((1, D), lambda j, i, order: (0, 0))] + [vmem] * na,
            out_specs=[pl.BlockSpec((1, tm, D), lambda j, i, order: (order[j], i, 0)), ANY] + [ANY] * na,
            scratch_shapes=[pltpu.VMEM(b.shape, dt) for b, dt in zip(blocks, dtypes)]
            + [pltpu.VMEM((n, D), BF16), pltpu.VMEM((2, D, D), BF16),
               pltpu.SemaphoreType.DMA((7 * na,)), pltpu.SemaphoreType.DMA((7 * na,)),
               pltpu.SemaphoreType.DMA((na,)), pltpu.SemaphoreType.DMA((2,)), pltpu.SemaphoreType.DMA(())]),
        out_shape=[SDS((NB, n, D), F32), SDS((n, D), BF16)] + [SDS((NB,) + b.shape, dt) for b, dt in zip(blocks, dtypes)],
        compiler_params=_params(56),
    )(order_ids, x2, norm_g, *blocks)


LRU_T = 256


def _shifted(groups, shifts):
    row = lax.broadcasted_iota(jnp.int32, (groups.shape[0] - 1,) + groups.shape[1:], 1)
    out = []
    for s in shifts:
        y = pltpu.roll(groups, s % 8, 1)
        moved = jnp.where(row >= s, y[1:], y[:-1]) if s > 0 else jnp.where(row < 8 + s, y[:-1], y[1:])
        out.append(moved.reshape(-1, groups.shape[2]))
    return out


def _conv(taps, cw, cb):
    acc = taps[0] * cw[0:1, :] + taps[1] * cw[1:2, :]
    acc = acc + taps[2] * cw[2:3, :]
    acc = acc + taps[3] * cw[3:4, :]
    return cb + acc


def _lru_gates(xa, wx_ref, wa_ref, bx, ba, lam):
    xab = xa.astype(BF16)
    pis, prs = [], []
    for h in range(NB):
        xs = xab[:, h * BD:(h + 1) * BD]
        pis.append(jnp.dot(xs, wx_ref[h], preferred_element_type=F32))
        prs.append(jnp.dot(xs, wa_ref[h], preferred_element_type=F32))
    gi = _sigmoid(jnp.concatenate(pis, axis=1) + bx)
    gr = _sigmoid(jnp.concatenate(prs, axis=1) + ba)
    sp = _softplus_neg(lam)
    log_a = (-LRU_C * gr) * sp
    a = jnp.exp(log_a)
    mult = jnp.sqrt(-jnp.tanh(log_a) * (a * a + 1.0))
    return xab, gi, gr, sp, a, mult


def _lru_fwd(z, cw8, cb, wx, wa, bx, ba, lam, nb, s_len):
    n = nb * s_len
    t = LRU_T
    ns = s_len // t

    def body(xp_ref, ga_ref, cw_ref, cb_ref, wx_ref, wa_ref, bx_ref, ba_ref, lam_ref,
             h_ref, ya_ref, ext, a_s, u_s, carry):
        @pl.when(pl.program_id(1) == 0)
        def _():
            ext[0:8, :] = jnp.zeros((8, D), F32)
            carry[...] = jnp.zeros((8, D), F32)

        xp = xp_ref[0]
        ext[8:8 + t, :] = xp
        xa = _conv(_shifted(_groups(ext[...]), (3, 2, 1)) + [xp], cw_ref[...], cb_ref[...])
        ext[0:8, :] = xp[t - 8:t, :]
        _, gi, _, _, a, mult = _lru_gates(xa, wx_ref, wa_ref, bx_ref[...], ba_ref[...], lam_ref[...])
        u = (mult * gi) * xa
        a, u = _groups(a), _groups(u)
        row = lax.broadcasted_iota(jnp.int32, a.shape, 1)
        for sh in (1, 2, 4):
            a_sh = pltpu.roll(a, sh, 1)
            u_sh = pltpu.roll(u, sh, 1)
            m = row >= sh
            u = jnp.where(m, a * u_sh + u, u)
            a = jnp.where(m, a * a_sh, a)
        a_s[...] = a.reshape(t, D)
        u_s[...] = u.reshape(t, D)

        def step(g, c):
            r = pl.multiple_of(g * 8, 8)
            hg = u_s[pl.ds(r, 8), :] + a_s[pl.ds(r, 8), :] * c
            h_ref[pl.ds(r, 8), :] = hg
            return hg[7:8, :]

        c_out = lax.fori_loop(0, t // 8, step, carry[0:1, :], unroll=4)
        carry[0:1, :] = c_out
        ga = ga_ref[0]
        ya_ref[...] = (h_ref[...] * (ga * _sigmoid(ga))).astype(BF16)

    row_map = lambda b, s: (b * ns + s, 0)
    rep2 = lambda b, s: (0, 0)
    rep3 = lambda b, s: (0, 0, 0)
    return pl.pallas_call(
        body, name="lru_fwd", grid=(nb, ns),
        in_specs=[pl.BlockSpec((1, t, D), lambda b, s: (0, b * ns + s, 0)),
                  pl.BlockSpec((1, t, D), lambda b, s: (1, b * ns + s, 0)),
                  pl.BlockSpec((8, D), rep2), pl.BlockSpec((1, D), rep2),
                  pl.BlockSpec((NB, BD, BD), rep3), pl.BlockSpec((NB, BD, BD), rep3),
                  pl.BlockSpec((1, D), rep2), pl.BlockSpec((1, D), rep2), pl.BlockSpec((1, D), rep2)],
        out_specs=[pl.BlockSpec((t, D), row_map), pl.BlockSpec((t, D), row_map)],
        out_shape=[SDS((n, D), F32), SDS((n, D), BF16)],
        scratch_shapes=[pltpu.VMEM((t + 8, D), F32), pltpu.VMEM((t, D), F32), pltpu.VMEM((t, D), F32),
                        pltpu.VMEM((8, D), F32)],
        compiler_params=_params(48),
    )(z, z, cw8, cb, wx, wa, bx, ba, lam)


def _lru_bwd(z, h_all, dya, cw8, cb, wx, wa, bx, ba, lam, nb, s_len):
    n = nb * s_len
    t = LRU_T
    ns = s_len // t
    t8 = t // 8

    def body(xp_ref, xph_ref, ga_ref, h_ref, hh_ref, dya_ref, cw_ref, cb_ref, wx_ref, wa_ref, bx_ref, ba_ref,
             lam_ref, dz_ref, gcw_ref, gcb_ref, gwx_ref, gwa_ref, gbx_ref, gba_ref, glam_ref,
             ext, hext, dext, a_s, u_s, dh_s, carry):
        b, s = pl.program_id(0), pl.program_id(1)
        first_tile = s == ns - 1

        @pl.when((b == 0) & (s == 0))
        def _():
            for ref in (gcw_ref, gcb_ref, gwx_ref, gwa_ref, gbx_ref, gba_ref, glam_ref):
                ref[...] = jnp.zeros(ref.shape, F32)

        @pl.when(s == 0)
        def _():
            dext[t:t + 8, :] = jnp.zeros((8, D), F32)
            carry[...] = jnp.zeros((8, D), F32)

        keep = jnp.where(first_tile, 0.0, 1.0)
        xp = xp_ref[0]
        ext[0:8, :] = xph_ref[0] * keep
        ext[8:8 + t, :] = xp
        hext[0:8, :] = hh_ref[...] * keep
        hext[8:8 + t, :] = h_ref[...]
        cw = cw_ref[...]
        lam = lam_ref[...]
        taps = _shifted(_groups(ext[...]), (3, 2, 1)) + [xp]
        xa = _conv(taps, cw, cb_ref[...])
        xab, gi, gr, sp, a, mult = _lru_gates(xa, wx_ref, wa_ref, bx_ref[...], ba_ref[...], lam)
        (h_prev,) = _shifted(_groups(hext[...]), (1,))
        ga = ga_ref[0]
        sg = _sigmoid(ga)
        dya_v = dya_ref[...]
        d_ga = dya_v * h_ref[...] * (sg * (1.0 + ga * (1.0 - sg)))
        g_in = dya_v * (ga * sg)

        (an,) = _shifted(jnp.concatenate([_groups(a), jnp.ones((1, 8, D), F32)], axis=0), (-1,))
        an, u = _groups(an), _groups(g_in)
        row = lax.broadcasted_iota(jnp.int32, an.shape, 1)
        for sh in (1, 2, 4):
            a_sh = pltpu.roll(an, 8 - sh, 1)
            u_sh = pltpu.roll(u, 8 - sh, 1)
            m = row < 8 - sh
            u = jnp.where(m, u + an * u_sh, u)
            an = jnp.where(m, an * a_sh, an)
        a_s[...] = an.reshape(t, D)
        u_s[...] = u.reshape(t, D)

        def step(i, c):
            r = pl.multiple_of((t8 - 1 - i) * 8, 8)
            dg = u_s[pl.ds(r, 8), :] + a_s[pl.ds(r, 8), :] * c
            dh_s[pl.ds(r, 8), :] = dg
            return dg[0:1, :]

        lax.fori_loop(0, t8, step, carry[0:1, :], unroll=4)
        dh = dh_s[...]
        carry[0:1, :] = a[0:1, :] * dh[0:1, :]

        d_a = dh * h_prev
        dux = dh * xa
        d_mult = dux * gi
        d_gi = dux * mult
        d_xa = dh * (mult * gi)
        d_loga = d_a * a - d_mult * ((a * a) / mult)
        d_gr = d_loga * (-LRU_C * sp)
        d_sp = jnp.sum(d_loga * (-LRU_C * gr), axis=0, keepdims=True)
        glam_ref[...] += d_sp * (-_sigmoid(-lam))
        d_pi = d_gi * gi * (1.0 - gi)
        d_pr = d_gr * gr * (1.0 - gr)
        gbx_ref[...] += jnp.sum(d_pi, axis=0, keepdims=True)
        gba_ref[...] += jnp.sum(d_pr, axis=0, keepdims=True)
        dpib = d_pi.astype(BF16)
        dprb = d_pr.astype(BF16)
        back = []
        for h in range(NB):
            cs = slice(h * BD, (h + 1) * BD)
            gwx_ref[h] += lax.dot_general(xab[:, cs], dpib[:, cs], TN_DIMS, preferred_element_type=F32)
            gwa_ref[h] += lax.dot_general(xab[:, cs], dprb[:, cs], TN_DIMS, preferred_element_type=F32)
            back.append(lax.dot_general(dpib[:, cs], wx_ref[h], NT_DIMS, preferred_element_type=F32)
                        + lax.dot_general(dprb[:, cs], wa_ref[h], NT_DIMS, preferred_element_type=F32))
        d_xa = d_xa + jnp.concatenate(back, axis=1)

        dext[0:t, :] = d_xa
        later = _shifted(_groups(dext[...]), (-3, -2, -1))
        d_xp = later[0] * cw[0:1, :] + later[1] * cw[1:2, :]
        d_xp = d_xp + later[2] * cw[2:3, :]
        d_xp = d_xp + d_xa * cw[3:4, :]
        dext[t:t + 8, :] = d_xa[0:8, :]
        gcb_ref[...] += jnp.sum(d_xa, axis=0, keepdims=True)
        for k in range(4):
            gcw_ref[k:k + 1, :] += jnp.sum(d_xa * taps[k], axis=0, keepdims=True)
        dz_ref[0] = d_xp.astype(BF16)
        dz_ref[1] = d_ga.astype(BF16)

    rb = lambda b, s: b * ns + (ns - 1 - s)
    halo = lambda b, s: jnp.maximum(rb(b, s) * t8 - 1, 0)
    rep2 = lambda b, s: (0, 0)
    rep3 = lambda b, s: (0, 0, 0)
    return pl.pallas_call(
        body, name="lru_bwd", grid=(nb, ns),
        in_specs=[pl.BlockSpec((1, t, D), lambda b, s: (0, rb(b, s), 0)),
                  pl.BlockSpec((1, 8, D), lambda b, s: (0, halo(b, s), 0)),
                  pl.BlockSpec((1, t, D), lambda b, s: (1, rb(b, s), 0)),
                  pl.BlockSpec((t, D), lambda b, s: (rb(b, s), 0)),
                  pl.BlockSpec((8, D), lambda b, s: (halo(b, s), 0)),
                  pl.BlockSpec((t, D), lambda b, s: (rb(b, s), 0)),
                  pl.BlockSpec((8, D), rep2), pl.BlockSpec((1, D), rep2),
                  pl.BlockSpec((NB, BD, BD), rep3), pl.BlockSpec((NB, BD, BD), rep3),
                  pl.BlockSpec((1, D), rep2), pl.BlockSpec((1, D), rep2), pl.BlockSpec((1, D), rep2)],
        out_specs=[pl.BlockSpec((2, t, D), lambda b, s: (0, rb(b, s), 0)),
                   pl.BlockSpec((8, D), rep2), pl.BlockSpec((1, D), rep2),
                   pl.BlockSpec((NB, BD, BD), rep3), pl.BlockSpec((NB, BD, BD), rep3),
                   pl.BlockSpec((1, D), rep2), pl.BlockSpec((1, D), rep2), pl.BlockSpec((1, D), rep2)],
        out_shape=[SDS((2, n, D), BF16), SDS((8, D), F32), SDS((1, D), F32),
                   SDS((NB, BD, BD), F32), SDS((NB, BD, BD), F32),
                   SDS((1, D), F32), SDS((1, D), F32), SDS((1, D), F32)],
        scratch_shapes=[pltpu.VMEM((t + 8, D), F32), pltpu.VMEM((t + 8, D), F32), pltpu.VMEM((t + 8, D), F32),
                        pltpu.VMEM((t, D), F32), pltpu.VMEM((t, D), F32), pltpu.VMEM((t, D), F32),
                        pltpu.VMEM((8, D), F32)],
        compiler_params=_params(56),
    )(z, z, z, h_all, h_all, dya, cw8, cb, wx, wa, bx, ba, lam)


HG_T = 512
HG_NC = HG_T // CHUNK
BNT_DIMS = (((2,), (2,)), ((0,), (0,)))
BNN_DIMS = (((2,), (1,)), ((0,), (0,)))
BTN_DIMS = (((1,), (1,)), ((0,), (0,)))


def _lower_bound(lg):
    m = jnp.max(lg, axis=0, keepdims=True)
    e = jnp.exp(lg - m)
    return e[0:1, :] / jnp.sum(e, axis=0, keepdims=True)


def _tri(upper):
    r = lax.broadcasted_iota(jnp.int32, (HG_NC, CHUNK, CHUNK), 1)
    c = lax.broadcasted_iota(jnp.int32, (HG_NC, CHUNK, CHUNK), 2)
    return (c >= r) if upper else (r >= c)


def _bdot(a, b, dims):
    return lax.dot_general(a, b, dims, preferred_element_type=F32)


def _tri_sums(upper, a):
    tri = _tri(upper).astype(BF16)
    a1 = a.astype(BF16)
    r1 = a - a1.astype(F32)
    a2 = r1.astype(BF16)
    a3 = (r1 - a2.astype(F32)).astype(BF16)
    return _bdot(tri, a1, BNN_DIMS) + (_bdot(tri, a2, BNN_DIMS) + _bdot(tri, a3, BNN_DIMS))


def _chunks(a):
    return a.reshape(HG_NC, CHUNK, BD)


def _hg_tile(q, fp, lb):
    q, fp = _chunks(q), _chunks(fp)
    sig = _sigmoid(fp)
    f = lb + (1.0 - lb) * sig
    log_f = jnp.log(f)
    k = 1.0 - f
    b = _tri_sums(False, log_f)
    b_mid = b[:, CHUNK // 2:CHUNK // 2 + 1, :]
    b_last = b[:, CHUNK - 1:CHUNK, :]
    sq = _sigmoid(q)
    qh = q * sq
    e_qi = jnp.exp(b - b_mid)
    e_ki = jnp.exp(b_mid - b)
    e_qs = jnp.exp(b)
    e_ks = jnp.exp(b_last - b)
    dc = jnp.exp(b_last)
    q_in = (qh * e_qi) * HG_SCALE
    k_in = k * e_ki
    q_st = (qh * e_qs) * HG_SCALE
    k_st = k * e_ks
    att = _bdot(q_in.astype(BF16), k_in.astype(BF16), BNT_DIMS)
    att = jnp.where(_tri(False), att, 0.0)
    return dict(q=q, sig=sig, f=f, k=k, sq=sq, e_qi=e_qi, e_ki=e_ki, e_qs=e_qs, e_ks=e_ks, dc=dc,
                q_in=q_in, k_in=k_in, q_st=q_st, k_st=k_st, att=att)


def _hgrn_fwd(z, lb_logits, hg_g, nb, s_len):
    n = nb * s_len
    t = HG_T
    ns = s_len // t
    nchunk = s_len // CHUNK

    def body(q_ref, f_ref, v_ref, gb_ref, lg_ref, g_ref, o_ref, yb_ref, st_ref, st):
        @pl.when(pl.program_id(1) == 0)
        def _():
            st[...] = jnp.zeros((NB, BD, BD), F32)

        def head(h, carry):
            cols = pl.ds(pl.multiple_of(h * BD, BD), BD)
            lb = _lower_bound(lg_ref[:, cols])
            ck = _hg_tile(q_ref[0, :, cols], f_ref[0, :, cols], lb)
            vb = _chunks(v_ref[0, :, cols]).astype(BF16)
            kv = _bdot(vb, ck["k_st"].astype(BF16), BTN_DIMS)
            states = [st[h]]
            for c in range(HG_NC):
                states.append(states[c] * ck["dc"][c] + kv[c])
            st[h] = states[HG_NC]
            s_in = jnp.stack(states[:HG_NC], axis=0)
            st_ref[h] = s_in
            o = (_bdot(ck["att"].astype(BF16), vb, BNN_DIMS)
                 + _bdot(ck["q_st"].astype(BF16), s_in.astype(BF16), BNT_DIMS))
            o_ref[:, cols] = o.reshape(t, BD)
            r = lax.rsqrt(jnp.mean(o * o, axis=-1, keepdims=True) + EPS)
            gb = _chunks(gb_ref[0, :, cols])
            yb_ref[:, cols] = (((o * r) * g_ref[...]) * (gb * _sigmoid(gb))).astype(BF16).reshape(t, BD)
            return carry

        lax.fori_loop(0, NB, head, 0, unroll=4)

    seg = lambda j: pl.BlockSpec((1, t, D), lambda b, s: (j, b * ns + s, 0))
    tile = pl.BlockSpec((t, D), lambda b, s: (b * ns + s, 0))
    return pl.pallas_call(
        body, name="hgrn_fwd", grid=(nb, ns),
        in_specs=[seg(2), seg(3), seg(4), seg(5),
                  pl.BlockSpec((2, D), lambda b, s: (0, 0)), pl.BlockSpec((1, BD), lambda b, s: (0, 0))],
        out_specs=[tile, tile, pl.BlockSpec((NB, HG_NC, BD, BD), lambda b, s: (b, s, 0, 0))],
        out_shape=[SDS((n, D), F32), SDS((n, D), BF16), SDS((nb * NB, nchunk, BD, BD), F32)],
        scratch_shapes=[pltpu.VMEM((NB, BD, BD), F32)],
        compiler_params=_params(56),
    )(z, z, z, z, lb_logits, hg_g)


def _hgrn_bwd(z, o_all, st_all, dyb, lb_logits, hg_g, nb, s_len):
    n = nb * s_len
    t = HG_T
    ns = s_len // t

    def body(q_ref, f_ref, v_ref, gb_ref, o_ref, st_ref, dyb_ref, lg_ref, g_ref,
             dz_ref, glg_ref, ghg_ref, dst, dlb):
        b, s = pl.program_id(0), pl.program_id(1)

        @pl.when((b == 0) & (s == 0))
        def _():
            ghg_ref[...] = jnp.zeros((1, BD), F32)
            dlb[...] = jnp.zeros((8, D), F32)

        @pl.when(s == 0)
        def _():
            dst[...] = jnp.zeros((NB, BD, BD), F32)

        g = g_ref[...]

        def head(h, carry):
            cols = pl.ds(pl.multiple_of(h * BD, BD), BD)
            lb = _lower_bound(lg_ref[:, cols])
            ck = _hg_tile(q_ref[0, :, cols], f_ref[0, :, cols], lb)
            q = ck["q"]
            vb = _chunks(v_ref[0, :, cols]).astype(BF16)
            gb = _chunks(gb_ref[0, :, cols])
            o = _chunks(o_ref[:, cols])
            dyb_v = _chunks(dyb_ref[:, cols])
            s_in = st_ref[h]

            sgb = _sigmoid(gb)
            r = lax.rsqrt(jnp.mean(o * o, axis=-1, keepdims=True) + EPS)
            ohat = o * r
            d_on = dyb_v * (gb * sgb)
            d_gb = dyb_v * (ohat * g) * (sgb * (1.0 + gb * (1.0 - sgb)))
            ghg_ref[...] += jnp.sum(jnp.sum(d_on * ohat, axis=1), axis=0, keepdims=True)
            tt = d_on * g
            d_o = r * (tt - ohat * jnp.mean(tt * ohat, axis=-1, keepdims=True))
            dob = d_o.astype(BF16)

            attb = ck["att"].astype(BF16)
            q_inb, k_inb = ck["q_in"].astype(BF16), ck["k_in"].astype(BF16)
            q_stb, k_stb = ck["q_st"].astype(BF16), ck["k_st"].astype(BF16)
            d_att = jnp.where(_tri(False), _bdot(dob, vb, BNT_DIMS), 0.0).astype(BF16)
            d_q_in = _bdot(d_att, k_inb, BNN_DIMS)
            d_k_in = _bdot(d_att, q_inb, BTN_DIMS)
            d_q_st = _bdot(dob, s_in.astype(BF16), BNN_DIMS)
            qdo = _bdot(dob, q_stb, BTN_DIMS)
            d_states = [None] * HG_NC + [dst[h]]
            for c in reversed(range(HG_NC)):
                d_states[c] = d_states[c + 1] * ck["dc"][c] + qdo[c]
            dst[h] = d_states[0]
            ds_out = jnp.stack(d_states[1:], axis=0)
            dsb = ds_out.astype(BF16)
            d_v = _bdot(attb, dob, BTN_DIMS) + _bdot(k_stb, dsb, BNT_DIMS)
            d_k_st = _bdot(vb, dsb, BNN_DIMS)
            d_dc = jnp.sum(ds_out * s_in, axis=1, keepdims=True)

            p_qi = d_q_in * ck["q_in"]
            p_ki = d_k_in * ck["k_in"]
            p_qs = d_q_st * ck["q_st"]
            p_ks = d_k_st * ck["k_st"]
            d_qh = (d_q_in * ck["e_qi"] + d_q_st * ck["e_qs"]) * HG_SCALE
            d_k = d_k_in * ck["e_ki"] + d_k_st * ck["e_ks"]
            d_b = (p_qi - p_ki) + (p_qs - p_ks)
            d_b_mid = jnp.sum(p_ki - p_qi, axis=1, keepdims=True)
            d_b_last = jnp.sum(p_ks, axis=1, keepdims=True) + d_dc * ck["dc"]
            rowi = lax.broadcasted_iota(jnp.int32, (HG_NC, CHUNK, BD), 1)
            d_b = d_b + jnp.where(rowi == CHUNK // 2, d_b_mid, 0.0) + jnp.where(rowi == CHUNK - 1, d_b_last, 0.0)
            d_logf = _tri_sums(True, d_b)
            d_f = d_logf / ck["f"] - d_k
            sig, sq = ck["sig"], ck["sq"]
            d_fp = d_f * (1.0 - lb) * (sig * (1.0 - sig))
            dlb[0:1, cols] += jnp.sum(jnp.sum(d_f * (1.0 - sig), axis=1), axis=0, keepdims=True)
            d_q = d_qh * (sq * (1.0 + q * (1.0 - sq)))
            dz_ref[0, :, cols] = d_q.astype(BF16).reshape(t, BD)
            dz_ref[1, :, cols] = d_fp.astype(BF16).reshape(t, BD)
            dz_ref[2, :, cols] = d_v.astype(BF16).reshape(t, BD)
            dz_ref[3, :, cols] = d_gb.astype(BF16).reshape(t, BD)
            return carry

        lax.fori_loop(0, NB, head, 0, unroll=2)

        @pl.when((b == nb - 1) & (s == ns - 1))
        def _():
            lb = _lower_bound(lg_ref[...])
            dl = dlb[0:1, :] * (lb * (1.0 - lb))
            glg_ref[0:1, :] = dl
            glg_ref[1:2, :] = -dl

    rb = lambda b, s: b * ns + (ns - 1 - s)
    seg = lambda j: pl.BlockSpec((1, t, D), lambda b, s: (j, rb(b, s), 0))
    tile = pl.BlockSpec((t, D), lambda b, s: (rb(b, s), 0))
    return pl.pallas_call(
        body, name="hgrn_bwd", grid=(nb, ns),
        in_specs=[seg(2), seg(3), seg(4), seg(5), tile,
                  pl.BlockSpec((NB, HG_NC, BD, BD), lambda b, s: (b, ns - 1 - s, 0, 0)),
                  tile, pl.BlockSpec((2, D), lambda b, s: (0, 0)), pl.BlockSpec((1, BD), lambda b, s: (0, 0))],
        out_specs=[pl.BlockSpec((4, t, D), lambda b, s: (0, rb(b, s), 0)),
                   pl.BlockSpec((2, D), lambda b, s: (0, 0)), pl.BlockSpec((1, BD), lambda b, s: (0, 0))],
        out_shape=[SDS((4, n, D), BF16), SDS((2, D), F32), SDS((1, BD), F32)],
        scratch_shapes=[pltpu.VMEM((NB, BD, BD), F32), pltpu.VMEM((8, D), F32)],
        compiler_params=_params(60),
    )(z, z, z, z, o_all, st_all, dyb, lb_logits, hg_g)


def _mid(ya, yb, z, b_merge, x2, tgt, fin_g, pa, pb, wo):
    n = x2.shape[0]
    tm = 256
    ni = n // tm

    def body(ya_ref, yb_ref, gma_ref, gmb_ref, bm_ref, x_ref, t_ref, fg_ref, pa_hbm, pb_hbm, wo_hbm,
             dx2_ref, dya_ref, dyb_ref, dgm_ref, loss_ref, gfg_ref, gbm_ref, gm_hbm,
             pa_v, pb_v, wo_v, gpa_v, gpb_v, gwo_v, sem):
        i = pl.program_id(0)
        by_owner = lambda g: g.reshape(NB, BD, D)
        loads = [pltpu.make_async_copy(src, dst, sem.at[k])
                 for k, (src, dst) in enumerate(((pa_hbm, pa_v), (pb_hbm, pb_v), (wo_hbm, wo_v)))]
        stores = [pltpu.make_async_copy(src, dst, sem.at[k])
                  for k, (src, dst) in enumerate((g, gm_hbm.at[:, pl.ds(slot * BD, BD), :])
                                                 for slot, g in enumerate((gpa_v, gpb_v, gwo_v)))]

        @pl.when(i == 0)
        def _():
            for cp in loads:
                cp.start()
            for ref in (gpa_v, gpb_v, gwo_v, loss_ref, gfg_ref, gbm_ref):
                ref[...] = jnp.zeros(ref.shape, F32)
            for cp in loads:
                cp.wait()

        ya_v = ya_ref[...]
        yb_v = yb_ref[...]
        out_a = jnp.dot(ya_v, pa_v[...], preferred_element_type=F32)
        out_b = jnp.dot(yb_v, pb_v[...], preferred_element_type=F32)
        bm = bm_ref[...]
        g_a = _sigmoid(gma_ref[0] + bm[:, 0:D])
        g_b = _sigmoid(gmb_ref[0] + bm[:, D:2 * D])
        mixed = g_a * out_a + g_b * out_b
        mixb = mixed.astype(BF16)
        xo = x_ref[...] + jnp.dot(mixb, wo_v[...], preferred_element_type=F32)
        r = lax.rsqrt(jnp.mean(xo * xo, axis=-1, keepdims=True) + EPS)
        xn = xo * r
        fg = fg_ref[...]
        e = xn * fg - t_ref[...]
        loss_ref[...] += 0.5 * jnp.sum(jnp.mean(e * e, axis=-1, keepdims=True))
        dy = e * (1.0 / D)
        gfg_ref[...] += jnp.sum(dy * xn, axis=0, keepdims=True)
        dxn = dy * fg
        dx2 = r * (dxn - xn * jnp.mean(dxn * xn, axis=-1, keepdims=True))
        dx2_ref[...] = dx2
        dx2b = dx2.astype(BF16)
        d_mixed = lax.dot_general(dx2b, wo_v[...], NT_DIMS, preferred_element_type=F32)
        gwo_v[...] += by_owner(lax.dot_general(mixb, dx2b, TN_DIMS, preferred_element_type=F32))
        d_oa = (d_mixed * g_a).astype(BF16)
        d_ob = (d_mixed * g_b).astype(BF16)
        dgm_a = (d_mixed * out_a) * (g_a * (1.0 - g_a))
        dgm_b = (d_mixed * out_b) * (g_b * (1.0 - g_b))
        gbm_ref[:, 0:D] += jnp.sum(dgm_a, axis=0, keepdims=True)
        gbm_ref[:, D:2 * D] += jnp.sum(dgm_b, axis=0, keepdims=True)
        dgm_ref[0] = dgm_a.astype(BF16)
        dgm_ref[1] = dgm_b.astype(BF16)
        dya_ref[...] = lax.dot_general(d_oa, pa_v[...], NT_DIMS, preferred_element_type=F32)
        dyb_ref[...] = lax.dot_general(d_ob, pb_v[...], NT_DIMS, preferred_element_type=F32)
        gpa_v[...] += by_owner(lax.dot_general(ya_v, d_oa, TN_DIMS, preferred_element_type=F32))
        gpb_v[...] += by_owner(lax.dot_general(yb_v, d_ob, TN_DIMS, preferred_element_type=F32))

        @pl.when(i == ni - 1)
        def _():
            for cp in stores:
                cp.start()
            for cp in stores:
                cp.wait()

    rows = pl.BlockSpec((tm, D), lambda i: (i, 0))
    rep = lambda shape: pl.BlockSpec(shape, lambda i: (0,) * len(shape))
    return pl.pallas_call(
        body, name="mid", grid=(ni,),
        in_specs=[rows, rows,
                  pl.BlockSpec((1, tm, D), lambda i: (6, i, 0)), pl.BlockSpec((1, tm, D), lambda i: (7, i, 0)),
                  rep((1, 2 * D)), rows, rows, rep((1, D)), ANY, ANY, ANY],
        out_specs=[rows, rows, rows, pl.BlockSpec((2, tm, D), lambda i: (0, i, 0)),
                   rep((8, BD)), rep((1, D)), rep((1, 2 * D)), ANY],
        out_shape=[SDS((n, D), F32), SDS((n, D), F32), SDS((n, D), F32), SDS((2, n, D), BF16),
                   SDS((8, BD), F32), SDS((1, D), F32), SDS((1, 2 * D), F32),
                   SDS((NB, MID_ROWS, D), F32)],
        scratch_shapes=[pltpu.VMEM((D, D), BF16)] * 3 + [pltpu.VMEM((NB, BD, D), F32)] * 3 + [pltpu.SemaphoreType.DMA((3,))],
        compiler_params=_params(60),
    )(ya, yb, z, z, b_merge, x2, tgt, fin_g, pa, pb, wo)


def _dz_specs(tm, ni, row_major):
    if row_major:
        ia = lambda i, j: (jnp.minimum(j, 1), i, 0)
        ib = lambda i, j: (jnp.clip(j - 2, 0, 3), i, 0)
        im = lambda i, j: (jnp.clip(j - 6, 0, 1), i, 0)
    else:
        last = ni - 1
        ia = lambda j, i: (jnp.minimum(j, 1), jnp.where(j < 2, i, last), 0)
        ib = lambda j, i: (jnp.clip(j - 2, 0, 3), jnp.where(j < 2, 0, jnp.where(j < 6, i, last)), 0)
        im = lambda j, i: (jnp.clip(j - 6, 0, 1), jnp.where(j < 6, 0, i), 0)
    return [pl.BlockSpec((1, tm, D), f) for f in (ia, ib, im)]


def _inproj_bwd_x(dza, dzb, dzm, w_all, x2, dx2, norm_g, after):
    n = x2.shape[0]
    tm = 512
    ni = n // tm

    def body(dza_ref, dzb_ref, dzm_ref, w_ref, x_ref, dx2_ref, g_ref, after_ref, gx_ref, gg_ref, acc):
        i, j = pl.program_id(0), pl.program_id(1)

        @pl.when((i == 0) & (j == 0))
        def _():
            gg_ref[...] = jnp.zeros((1, D), F32)

        @pl.when(j == 0)
        def _():
            acc[...] = jnp.zeros((tm, D), F32)

        def add(ref):
            acc[...] += lax.dot_general(ref[0], w_ref[0], NT_DIMS, preferred_element_type=F32)

        pl.when(j < 2)(lambda: add(dza_ref))
        pl.when((j >= 2) & (j < 6))(lambda: add(dzb_ref))
        pl.when(j >= 6)(lambda: add(dzm_ref))

        @pl.when(j == NB - 1)
        def _():
            x = x_ref[...]
            r = lax.rsqrt(jnp.mean(x * x, axis=-1, keepdims=True) + EPS)
            xn = x * r
            dh = acc[...]
            gg_ref[...] += jnp.sum(dh * xn, axis=0, keepdims=True)
            dxn = dh * g_ref[...]
            gx_ref[...] = dx2_ref[...] + r * (dxn - xn * jnp.mean(dxn * xn, axis=-1, keepdims=True))

    rows = pl.BlockSpec((tm, D), lambda i, j: (i, 0))
    return pl.pallas_call(
        body, name="inproj_bwd_x", grid=(ni, NB),
        in_specs=_dz_specs(tm, ni, True) + [pl.BlockSpec((1, D, D), lambda i, j: (j, 0, 0)), rows, rows,
                                             pl.BlockSpec((1, D), lambda i, j: (0, 0)), ANY],
        out_specs=[rows, pl.BlockSpec((1, D), lambda i, j: (0, 0))],
        out_shape=[SDS((n, D), F32), SDS((1, D), F32)],
        scratch_shapes=[pltpu.VMEM((tm, D), F32)],
        compiler_params=_params(48),
    )(dza, dzb, dzm, w_all, x2, dx2, norm_g, after)


def _walk_tables(order, ni):
    rows = []
    for lo, hi in ((0, 2), (2, 6), (6, 8)):
        active = [j for j, g in enumerate(order) if lo <= g < hi]
        block, tile = [], []
        for j, g in enumerate(order):
            before = [a for a in active if a < j]
            if lo <= g < hi:
                block.append(g - lo), tile.append(-1)
            elif before:
                block.append(order[before[-1]] - lo), tile.append(ni - 1)
            else:
                block.append(order[active[0]] - lo), tile.append(0)
        rows += [block, tile]
    return rows


def _inproj_bwd_w(core, dza, dzb, dzm, h_all, g_m):
    n = h_all.shape[0]
    tm = min(n, 2048)
    ni = n // tm
    packed = g_m.shape[1:]
    orders = [[2 * q + 1 - c for q in range(4)] + [2 * q + c for q in range(4)] for c in (0, 1)]
    tables = jnp.asarray([[order] + _walk_tables(order, ni) for order in orders], jnp.int32)
    walk = jnp.where(core == 0, tables[0], tables[1])

    def body(walk_ref, dza_ref, dzb_ref, dzm_ref, h_ref, gm_hbm, out_bf, own_f32, m_out_bf, m_own_f32, got_w, got_m,
             acc, stage, theirs, m_mine, m_theirs, m_stage, send_sems, recv_sems, local_sems):
        j, i = pl.program_id(0), pl.program_id(1)
        group = walk_ref[0, j]
        x, y, c = _place()
        sibling = (x, y, 1 - c)

        def send_w(q):
            return pltpu.make_async_remote_copy(
                src_ref=stage.at[q % 2], dst_ref=got_w.at[q], send_sem=send_sems.at[q], recv_sem=recv_sems.at[q],
                device_id=sibling, device_id_type=MESH)

        def send_m(q):
            return pltpu.make_async_remote_copy(
                src_ref=gm_hbm.at[2 * q + (1 - c)], dst_ref=got_m.at[q], send_sem=send_sems.at[4 + q],
                recv_sem=recv_sems.at[4 + q], device_id=sibling, device_id_type=MESH)

        def fetch(q):
            return pltpu.make_async_copy(got_w.at[q], theirs, local_sems.at[0])

        def fetch_m(q):
            return (pltpu.make_async_copy(gm_hbm.at[2 * q + c], m_mine, local_sems.at[2]),
                    pltpu.make_async_copy(got_m.at[q], m_theirs, local_sems.at[3]))

        @pl.when((j == 0) & (i == 0))
        def _():
            for q in range(4):
                send_m(q).start()

        @pl.when(i == 0)
        def _():
            acc[...] = jnp.zeros((D, D), F32)

        def add(ref):
            acc[...] += lax.dot_general(h_ref[...], ref[0], TN_DIMS, preferred_element_type=F32)

        pl.when(group < 2)(lambda: add(dza_ref))
        pl.when((group >= 2) & (group < 6))(lambda: add(dzb_ref))
        pl.when(group >= 6)(lambda: add(dzm_ref))

        for q in range(4):
            @pl.when((i == ni - 1) & (j == q))
            def _(q=q):
                if q >= 2:
                    send_w(q - 2).wait_send()
                stage[q % 2] = acc[...].astype(BF16)
                send_w(q).start()

        def reducer(q):
            other_x, other_y = x != q // 2, y != q % 2
            return other_x | other_y, jnp.where(other_x & other_y, 2, jnp.where(other_x, 0, 1))

        def w_out(q):
            return pltpu.make_async_copy(stage.at[0], out_bf.at[reducer(q)[1]], local_sems.at[1])

        for q in range(4):
            @pl.when((j == 4 + q) & (i == 0))
            def _(q=q):
                send_w(q).wait_recv()
                send_m(q).wait_recv()
                fetch(q).start()
                for cp in fetch_m(q):
                    cp.start()
                if q > 0:
                    pl.when(reducer(q - 1)[0])(lambda: w_out(q - 1).wait())

            @pl.when((j == 4 + q) & (i == ni - 1))
            def _(q=q):
                if q == 0:
                    send_w(2).wait_send()
                    send_w(3).wait_send()
                other, slot = reducer(q)
                m_out = pltpu.make_async_copy(m_stage, m_out_bf.at[slot], local_sems.at[4])
                m_own = pltpu.make_async_copy(m_mine, m_own_f32, local_sems.at[4])

                for cp in fetch_m(q):
                    cp.wait()

                @pl.when(other)
                def _():
                    m_stage[...] = (m_mine[...] + m_theirs[...]).astype(BF16)
                    m_out.start()

                @pl.when(jnp.logical_not(other))
                def _():
                    m_mine[...] += m_theirs[...]
                    m_own.start()

                fetch(q).wait()

                @pl.when(other)
                def _():
                    stage[0] = (acc[...] + theirs[...].astype(F32)).astype(BF16)
                    w_out(q).start()
                    if q == 3:
                        w_out(q).wait()
                    m_out.wait()

                @pl.when(jnp.logical_not(other))
                def _():
                    acc[...] += theirs[...].astype(F32)
                    out = pltpu.make_async_copy(acc, own_f32, local_sems.at[1])
                    out.start()
                    out.wait()
                    m_own.wait()

        @pl.when((j == NB - 1) & (i == ni - 1))
        def _():
            for q in range(4):
                send_m(q).wait_send()

    def dz_spec(k):
        return pl.BlockSpec((1, tm, D), lambda j, i, w: (w[1 + 2 * k, j], jnp.where(w[2 + 2 * k, j] < 0, i, w[2 + 2 * k, j]), 0))

    return pl.pallas_call(
        body, name="inproj_bwd_w",
        grid_spec=pltpu.PrefetchScalarGridSpec(
            num_scalar_prefetch=1, grid=(NB, ni),
            in_specs=[dz_spec(0), dz_spec(1), dz_spec(2), pl.BlockSpec((tm, D), lambda j, i, w: (i, 0)), ANY],
            out_specs=[ANY] * 6,
            scratch_shapes=[pltpu.VMEM((D, D), F32), pltpu.VMEM((2, D, D), BF16), pltpu.VMEM((D, D), BF16),
                            pltpu.VMEM(packed, F32), pltpu.VMEM(packed, F32), pltpu.VMEM(packed, BF16),
                            pltpu.SemaphoreType.DMA((8,)), pltpu.SemaphoreType.DMA((8,)), pltpu.SemaphoreType.DMA((5,))]),
        out_shape=[SDS((3, D, D), BF16), SDS((D, D), F32), SDS((3,) + packed, BF16), SDS(packed, F32),
                   SDS((4, D, D), BF16), SDS((4,) + packed, F32)],
        compiler_params=_params(58),
    )(walk, dza, dzb, dzm, h_all, g_m)


def _adamw(w, g, m, v):
    rows, cols = w.shape
    tr = _row_tile(rows)

    spec = pl.BlockSpec((tr, cols), lambda i: (i, 0))
    return pl.pallas_call(
        functools.partial(_adam_refs), name="adamw", grid=(rows // tr,), in_specs=[spec] * 4, out_specs=[spec] * 3,
        out_shape=[SDS((rows, cols), F32)] * 3, compiler_params=_params(32),
    )(w, g, m, v)


def _adam_refs(w_ref, g_ref, m_ref, v_ref, d_ref, nm_ref, nv_ref):
    gv = g_ref[...]
    nm = ADAM_B1 * m_ref[...] + (1.0 - ADAM_B1) * gv
    nv = ADAM_B2 * v_ref[...] + (1.0 - ADAM_B2) * (gv * gv)
    m_hat = nm / (1.0 - ADAM_B1 ** ADAM_STEP)
    v_hat = nv / (1.0 - ADAM_B2 ** ADAM_STEP)
    d_ref[...] = -ADAM_LR * (m_hat / (jnp.sqrt(v_hat) + ADAM_EPS) + ADAM_WD * w_ref[...])
    nm_ref[...] = nm
    nv_ref[...] = nv


def _adamw_small(ws, gs, ms, vs):
    k = len(ws)

    def body(*refs):
        ins, outs = refs[:4 * k], refs[4 * k:7 * k]
        vin, vout = refs[7 * k:11 * k], refs[11 * k:14 * k]
        load_sems, store_sems = refs[14 * k:]
        loads = [pltpu.make_async_copy(ins[i], vin[i], load_sems.at[i]) for i in range(4 * k)]
        for cp in loads:
            cp.start()
        for cp in loads:
            cp.wait()
        for i in range(k):
            _adam_refs(*[vin[part * k + i] for part in range(4)], *[vout[part * k + i] for part in range(3)])
        stores = [pltpu.make_async_copy(vout[i], outs[i], store_sems.at[i]) for i in range(3 * k)]
        for cp in stores:
            cp.start()
        for cp in stores:
            cp.wait()

    shapes = [SDS(w.shape, F32) for w in ws]
    vmem = [pltpu.VMEM(w.shape, F32) for w in ws]
    out = pl.pallas_call(
        body, name="adamw_small", out_shape=shapes * 3, in_specs=[HBM] * (4 * k), out_specs=[HBM] * (3 * k),
        scratch_shapes=vmem * 7 + [pltpu.SemaphoreType.DMA((4 * k,)), pltpu.SemaphoreType.DMA((3 * k,))],
        compiler_params=_params(32),
    )(*ws, *gs, *ms, *vs)
    return out[:k], out[k:2 * k], out[2 * k:]


def _allgather(blocks, dtypes, name):
    na = len(blocks)

    def body(*refs):
        ins, outs, stages = refs[:na], refs[na:2 * na], refs[2 * na:3 * na]
        send_sems, recv_sems, local_sems = refs[3 * na:]
        x, y, c = _place()
        me, sibling = (x, y, c), (x, y, 1 - c)
        chips = [(1 - x, y), (x, 1 - y), (1 - x, 1 - y)]
        blk = lambda p: 4 * p[0] + 2 * p[1] + p[2]

        def copy(a, k, block, to, src=None):
            return pltpu.make_async_remote_copy(
                src_ref=outs[a].at[blk(block)] if src is None else src, dst_ref=outs[a].at[blk(block)],
                send_sem=send_sems.at[7 * a + k], recv_sem=recv_sems.at[7 * a + k],
                device_id=to, device_id_type=MESH)

        mine, first, passed = [], [], []
        for a in range(na):
            stages[a][...] = ins[a][...].astype(dtypes[a])
            mine.append(pltpu.make_async_copy(stages[a], outs[a].at[blk(me)], local_sems.at[a]))
            mine[-1].start()
            first.append(copy(a, 0, me, sibling, src=stages[a]))
            first += [copy(a, 1 + j, me, (*chip, c), src=stages[a]) for j, chip in enumerate(chips)]
        for cp in first:
            cp.start()
        for j, chip in enumerate(chips):
            for a in range(na):
                copy(a, 1 + j, (*chip, c), me).wait_recv()
                passed.append(copy(a, 4 + j, (*chip, c), sibling))
                passed[-1].start()
        for a in range(na):
            copy(a, 0, sibling, me).wait_recv()
            for j, chip in enumerate(chips):
                copy(a, 4 + j, (*chip, 1 - c), me).wait_recv()
        for cp in first + passed:
            cp.wait_send()
        for cp in mine:
            cp.wait()

    return pl.pallas_call(
        body, name=name,
        in_specs=[pl.BlockSpec(memory_space=pltpu.VMEM)] * na, out_specs=[ANY] * na,
        out_shape=[SDS((NB,) + b.shape, dt) for b, dt in zip(blocks, dtypes)],
        scratch_shapes=[pltpu.VMEM(b.shape, dt) for b, dt in zip(blocks, dtypes)]
        + [pltpu.SemaphoreType.DMA((7 * na,)), pltpu.SemaphoreType.DMA((7 * na,)), pltpu.SemaphoreType.DMA((na,))],
        compiler_params=_params(40),
    )(*blocks)


HBM = pl.BlockSpec(memory_space=pltpu.HBM)
SEMS = pl.BlockSpec(memory_space=pltpu.SEMAPHORE)
EFFECT = pltpu.SideEffectType.DATAFLOW_SIDE_EFFECTING


def _chip_copies(srcs, lands, send_sems, recv_sems):
    x, y, c = _place()
    return [pltpu.make_async_remote_copy(
        src_ref=srcs[a].at[slot], dst_ref=lands[a].at[slot],
        send_sem=send_sems.at[3 * a + slot], recv_sem=recv_sems.at[3 * a + slot],
        device_id=(px, py, c), device_id_type=MESH)
        for a in range(len(srcs)) for slot, (px, py) in enumerate(_other_chips(x, y))]


def _split_start(name, copies, per_array, srcs, lands, after=None):
    na = len(srcs)

    def body(*refs):
        send_sems, recv_sems = refs[-2 * na - 3], refs[-2 * na - 2]
        for cp in copies(refs[:na], refs[na:2 * na], send_sems, recv_sems):
            cp.start()
        refs[-1][...] = jnp.zeros_like(refs[-1])

    hbm = lambda a: pltpu.HBM(a.shape, a.dtype)
    pin = lambda a: pltpu.with_memory_space_constraint(a, pltpu.HBM)
    out = pl.pallas_call(
        body, name=name,
        out_shape=(pltpu.SemaphoreType.DMA((per_array * na,)), pltpu.SemaphoreType.DMA((per_array * na,)),
                   *[hbm(a) for a in srcs], *[hbm(a) for a in lands], SDS((8, BD), F32)),
        in_specs=[HBM] * (2 * na) + ([] if after is None else [ANY]),
        out_specs=(SEMS, SEMS, *[HBM] * (2 * na), pl.BlockSpec(memory_space=pltpu.VMEM)),
        input_output_aliases={i: 2 + i for i in range(2 * na)},
        compiler_params=pltpu.CompilerParams(has_side_effects=EFFECT),
    )(*[pin(a) for a in srcs], *[pin(a) for a in lands], *([] if after is None else [after]))
    return out[0], out[1], out[2:2 + na], out[2 + na:2 + 2 * na], out[-1]


def _split_wait(name, copies, started, after):
    send_sems, recv_sems, srcs, lands, _ = started
    na = len(srcs)

    def body(*refs):
        waits = copies(refs[:na], refs[na:2 * na], refs[2 * na], refs[2 * na + 1])
        for cp in waits:
            cp.wait_send()
        for cp in waits:
            cp.wait_recv()

    hbm = lambda a: pltpu.HBM(a.shape, a.dtype)
    out = pl.pallas_call(
        body, name=name,
        out_shape=(*[hbm(a) for a in srcs], *[hbm(a) for a in lands]),
        in_specs=[HBM] * (2 * na) + [SEMS, SEMS, ANY],
        out_specs=tuple([HBM] * (2 * na)),
        input_output_aliases={i: i for i in range(2 * na)},
        compiler_params=pltpu.CompilerParams(has_side_effects=EFFECT),
    )(*srcs, *lands, send_sems, recv_sems, after)
    return out[na:]


def _add_chips(own, b_in):
    r, cols = own.shape
    tr = _row_tile(r)

    def body(p_ref, b0_ref, b1_ref, b2_ref, o_ref):
        o_ref[...] = ((p_ref[...] + b0_ref[0].astype(F32)) + b1_ref[0].astype(F32)) + b2_ref[0].astype(F32)

    slot = lambda k: pl.BlockSpec((1, tr, cols), lambda i: (k, i, 0))
    spec = pl.BlockSpec((tr, cols), lambda i: (i, 0))
    return pl.pallas_call(
        body, name="add_chips", grid=(r // tr,), in_specs=[spec, slot(0), slot(1), slot(2)], out_specs=spec,
        out_shape=SDS((r, cols), F32), compiler_params=_params(32),
    )(own, b_in, b_in, b_in)


VEC_NAMES = ("b_merge", "conv_b", "rg_bx", "rg_ba", "rg_lambda", "hg_lb_logits", "hg_norm_g", "final_norm_g")
REP_NAMES = ("rg_wx", "rg_wa", "norm_g") + VEC_NAMES
SMALL_AT = 3 * BD
SMALL_ROWS = 48
MID_ROWS = 448


def _sum_blocks(parts):
    def body(p_ref, o_ref):
        acc = p_ref[0]
        for k in range(1, NB):
            acc = acc + p_ref[k]
        o_ref[...] = acc

    return pl.pallas_call(body, name="sum_blocks", out_shape=SDS(parts.shape[1:], F32))(parts)


def _pack_rows(arrays, width, row_multiple=8):
    flat = jnp.concatenate([a.reshape(-1) for a in arrays])
    rows = -(-flat.shape[0] // width)
    rows = -(-rows // row_multiple) * row_multiple
    return jnp.pad(flat, (0, rows * width - flat.shape[0])).reshape(rows, width)


def _unpack(flat, like):
    out, off = [], 0
    for a in like:
        out.append(flat[off:off + a.size].reshape(a.shape))
        off += a.size
    return out


def kernel(x, w_in, b_merge, conv_w, conv_b, rg_wx, rg_bx, rg_wa, rg_ba, rg_lambda, hg_lb_logits, hg_norm_g, proj_a, proj_b, w_out, norm_g, final_norm_g, loss_target, m_w_in, m_b_merge, m_conv_w, m_conv_b, m_rg_wx, m_rg_bx, m_rg_wa, m_rg_ba, m_rg_lambda, m_hg_lb_logits, m_hg_norm_g, m_proj_a, m_proj_b, m_w_out, m_norm_g, m_final_norm_g, v_w_in, v_b_merge, v_conv_w, v_conv_b, v_rg_wx, v_rg_bx, v_rg_wa, v_rg_ba, v_rg_lambda, v_hg_lb_logits, v_hg_norm_g, v_proj_a, v_proj_b, v_w_out, v_norm_g, v_final_norm_g):
    weights = dict(w_in=w_in, b_merge=b_merge, conv_w=conv_w, conv_b=conv_b, rg_wx=rg_wx, rg_bx=rg_bx, rg_wa=rg_wa,
                   rg_ba=rg_ba, rg_lambda=rg_lambda, hg_lb_logits=hg_lb_logits, hg_norm_g=hg_norm_g, proj_a=proj_a,
                   proj_b=proj_b, w_out=w_out, norm_g=norm_g, final_norm_g=final_norm_g)
    mom1 = dict(w_in=m_w_in, b_merge=m_b_merge, conv_w=m_conv_w, conv_b=m_conv_b, rg_wx=m_rg_wx, rg_bx=m_rg_bx,
                rg_wa=m_rg_wa, rg_ba=m_rg_ba, rg_lambda=m_rg_lambda, hg_lb_logits=m_hg_lb_logits,
                hg_norm_g=m_hg_norm_g, proj_a=m_proj_a, proj_b=m_proj_b, w_out=m_w_out, norm_g=m_norm_g,
                final_norm_g=m_final_norm_g)
    mom2 = dict(w_in=v_w_in, b_merge=v_b_merge, conv_w=v_conv_w, conv_b=v_conv_b, rg_wx=v_rg_wx, rg_bx=v_rg_bx,
                rg_wa=v_rg_wa, rg_ba=v_rg_ba, rg_lambda=v_rg_lambda, hg_lb_logits=v_hg_lb_logits,
                hg_norm_g=v_hg_norm_g, proj_a=v_proj_a, proj_b=v_proj_b, w_out=v_w_out, norm_g=v_norm_g,
                final_norm_g=v_final_norm_g)
    order = list(weights)
    nb, s_len, _ = x.shape
    n = nb * s_len
    px, py, pc = _place()

    in_hbm = lambda a: pltpu.with_memory_space_constraint(a, pltpu.HBM)
    norm_gain = in_hbm(norm_g)

    x2 = x.reshape(n, D)
    cw_blk = jnp.pad(conv_w[0], ((0, 4), (0, 0)))
    order_ids = jnp.stack([_block_id(p) for p in _arrival_order(px, py, pc)]).astype(jnp.int32)
    z, h_all, w_all, pa_all, pb_all, wo_all, cw_all = _gather_inproj(
        order_ids, x2, norm_gain, [w_in[0], proj_a[0], proj_b[0], w_out[0], cw_blk], [BF16, BF16, BF16, BF16, F32])
    pa_full, pb_full, wo_full = (a.reshape(D, D) for a in (pa_all, pb_all, wo_all))
    cw8 = in_hbm(cw_all.transpose(1, 0, 2).reshape(8, D))
    wx_b, wa_b = in_hbm(rg_wx[0].astype(BF16)), in_hbm(rg_wa[0].astype(BF16))
    cb, bx, ba, lam = (in_hbm(a.reshape(1, D)) for a in (conv_b, rg_bx, rg_ba, rg_lambda))
    fin_g, b_mrg = in_hbm(final_norm_g.reshape(1, D)), in_hbm(b_merge)
    lb_lg, hg_g = in_hbm(hg_lb_logits), in_hbm(hg_norm_g)

    hlru, ya = _lru_fwd(z, cw8, cb, wx_b, wa_b, bx, ba, lam, nb, s_len)
    o_all, yb, st_all = _hgrn_fwd(z, lb_lg, hg_g, nb, s_len)

    (dx2, dya, dyb, dzm, loss_acc, g_fin, g_bm, g_mid) = _mid(
        ya, yb, z, b_mrg, x2, loss_target.reshape(n, D), fin_g, pa_full, pb_full, wo_full)
    dzb, g_lg, g_hg = _hgrn_bwd(z, o_all, st_all, dyb, lb_lg, hg_g, nb, s_len)
    dza, g_cw8, g_cb, g_wx, g_wa, g_bx, g_ba, g_lam = _lru_bwd(
        z, hlru, dya, cw8, cb, wx_b, wa_b, bx, ba, lam, nb, s_len)

    part = dict(b_merge=g_bm, conv_b=g_cb, rg_bx=g_bx, rg_ba=g_ba, rg_lambda=g_lam, hg_lb_logits=g_lg,
                hg_norm_g=g_hg, final_norm_g=g_fin)
    vec = _pack_rows([part[k] for k in VEC_NAMES], BD)
    vec = jnp.pad(vec, ((0, 16 * NB - vec.shape[0]), (0, 0))).reshape(NB, 2, D)
    rows8 = lambda a: jnp.pad(a, ((0, 0), (0, 8 - a.shape[1]), (0, 0)))
    small = jnp.concatenate([g_wx.reshape(NB, 16, D), g_wa.reshape(NB, 16, D),
                             rows8(g_cw8.reshape(8, NB, BD).transpose(1, 0, 2).reshape(NB, 1, D)), rows8(vec),
                             jnp.zeros((NB, MID_ROWS - SMALL_AT - SMALL_ROWS, D), F32)], axis=1)
    g_m = lax.dynamic_update_slice(g_mid, small, (0, SMALL_AT, 0))
    w_out_bf, w_own, m_out_bf, m_own, _, _ = _inproj_bwd_w(pc, dza, dzb, dzm, h_all, g_m)
    outgoing = [in_hbm(w_out_bf), in_hbm(m_out_bf)]
    chip_sums = _split_start("rs_chips_start", _chip_copies, 3, outgoing, [lax.empty(a.shape, a.dtype) for a in outgoing])
    grad_x, g_ng = _inproj_bwd_x(dza, dzb, dzm, w_all, x2, dx2, norm_gain, chip_sums[-1])
    from_chips = _split_wait("rs_chips_wait", _chip_copies, chip_sums, grad_x)
    r_w = _add_chips(w_own, from_chips[0])
    r_m = _add_chips(m_own, from_chips[1])
    row = lax.broadcasted_iota(jnp.int32, (8, D), 0)
    mine = jnp.where(row == 0, g_ng, jnp.where(row == 1, loss_acc[0:1, 0:1], 0.0))
    tail = jnp.concatenate([r_m[SMALL_AT:SMALL_AT + SMALL_ROWS], mine], axis=0)
    (tail_all,) = _allgather([tail], [F32], "gather_small_grads")
    summed = _sum_blocks(tail_all[:, SMALL_ROWS:SMALL_ROWS + 8])

    grads = dict(w_in=r_w.reshape(1, D, D),
                 proj_a=r_m[0:BD].reshape(1, BD, D), proj_b=r_m[BD:2 * BD].reshape(1, BD, D),
                 w_out=r_m[2 * BD:3 * BD].reshape(1, BD, D),
                 conv_w=r_m[SMALL_AT + 32].reshape(8, BD)[0:4].reshape(1, 4, BD),
                 rg_wx=tail_all[:, 0:16].reshape(1, NB, BD, BD), rg_wa=tail_all[:, 16:32].reshape(1, NB, BD, BD),
                 norm_g=summed[0:1])
    vec_all = tail_all[:, 40:42].reshape(-1)
    for k, gk in zip(VEC_NAMES, _unpack(vec_all, [weights[k] for k in VEC_NAMES])):
        grads[k] = gk

    delta, new_m, new_v = {}, {}, {}
    flat2 = lambda a: in_hbm(a.reshape(-1, a.shape[-1]))
    for k in ("w_in", "proj_a", "proj_b", "w_out"):
        outs = _adamw(*[flat2(t[k]) for t in (weights, grads, mom1, mom2)])
        delta[k], new_m[k], new_v[k] = (a.reshape(weights[k].shape) for a in outs)
    rep = list(REP_NAMES) + ["conv_w"]
    outs = _adamw_small(*[[flat2(t[k]) for k in rep] for t in (weights, grads, mom1, mom2)])
    for tgt, arrays in zip((delta, new_m, new_v), outs):
        for k, a in zip(rep, arrays):
            tgt[k] = a.reshape(weights[k].shape)

    return (summed[1, 0], grad_x.reshape(x.shape), *[grads[k] for k in order], *[delta[k] for k in order],
            *[new_m[k] for k in order], *[new_v[k] for k in order])
```

```python
import functools

import jax
import jax.numpy as jnp
from jax import lax
from jax.experimental import pallas as pl
from jax.experimental.pallas import tpu as pltpu

F32 = jnp.float32
BF16 = jnp.bfloat16
SDS = jax.ShapeDtypeStruct
MESH = pl.DeviceIdType.MESH
ANY = pl.BlockSpec(memory_space=pl.ANY)

D = 1024
NB = 8
BD = D // NB
CHUNK = 64
EPS = 1e-6
LRU_C = 8.0
HG_SCALE = BD ** -0.5
ADAM_LR, ADAM_B1, ADAM_B2, ADAM_EPS, ADAM_WD, ADAM_STEP = 0.001, 0.9, 0.999, 1e-08, 0.01, 10

NT_DIMS = (((1,), (1,)), ((), ()))
TN_DIMS = (((0,), (0,)), ((), ()))


def _params(vmem_mib):
    return pltpu.CompilerParams(vmem_limit_bytes=vmem_mib << 20)


def _row_tile(rows, most=256):
    assert rows % 8 == 0
    return max(t for t in range(8, min(rows, most) + 1, 8) if rows % t == 0)


def _sigmoid(v):
    return 0.5 * (jnp.tanh(0.5 * v) + 1.0)


def _groups(v):
    return v.reshape(v.shape[0] // 8, 8, v.shape[1])


def _softplus_neg(lam):
    t = -lam
    e = jnp.exp(-jnp.abs(t))
    w = 1.0 + e
    d = w - 1.0
    l1p = jnp.where(d == 0.0, e, jnp.log(w) * (e / jnp.where(d == 0.0, 1.0, d)))
    return jnp.maximum(t, 0.0) + l1p


def _place():
    return lax.axis_index("x"), lax.axis_index("y"), lax.axis_index("c")


def _other_chips(x, y):
    return [(1 - x, y), (x, 1 - y), (1 - x, 1 - y)]


def _block_id(p):
    return 4 * p[0] + 2 * p[1] + p[2]


def _core_chips(x, y, c):
    near, far, diag = _other_chips(x, y)
    pick = lambda a, b: (jnp.where(c == 0, a[0], b[0]), jnp.where(c == 0, a[1], b[1]))
    return [pick(near, far), pick(far, near), diag]


def _arrival_order(x, y, c):
    first, second, diag = _core_chips(x, y, c)
    return [(x, y, c), (x, y, 1 - c), (*first, c), (*second, 1 - c), (*second, c), (*first, 1 - c),
            (*diag, c), (*diag, 1 - c)]


def _gather_inproj(order_ids, x2, norm_g, blocks, dtypes):
    na = len(blocks)
    n = x2.shape[0]
    tm = min(n, 1024)
    ni = n // tm

    def body(order_ref, x_ref, g_ref, *refs):
        ins, (z_ref, h_ref), outs = refs[:na], refs[na:na + 2], refs[na + 2:2 * na + 2]
        stages = refs[2 * na + 2:3 * na + 2]
        h_full, wbuf, send_sems, recv_sems, local_sems, wsems, hsem = refs[3 * na + 2:]
        j, i = pl.program_id(0), pl.program_id(1)
        x, y, c = _place()
        me, sibling = (x, y, c), (x, y, 1 - c)
        chips = _core_chips(x, y, c)
        sibling_chips = [chips[1], chips[0], chips[2]]
        small = range(1, na)

        def copy(a, k, block, to, src=None):
            return pltpu.make_async_remote_copy(
                src_ref=outs[a].at[_block_id(block)] if src is None else src, dst_ref=outs[a].at[_block_id(block)],
                send_sem=send_sems.at[7 * a + k], recv_sem=recv_sems.at[7 * a + k],
                device_id=to, device_id_type=MESH)

        def local(a):
            return pltpu.make_async_copy(stages[a], outs[a].at[_block_id(me)], local_sems.at[a])

        def landed(a, slot):
            copy(a, 1 + slot, (*chips[slot], c), me).wait_recv()
            copy(a, 4 + slot, (*chips[slot], c), sibling).start()
            if slot == 0:
                copy(a, 3, (*chips[0], c), (*chips[1], c)).start()

        def diagonal_and_small():
            landed(0, 2)
            for a in small:
                landed(a, 0)
                landed(a, 1)

        def passed_on(a, slot):
            copy(a, 4 + slot, (*sibling_chips[slot], 1 - c), me).wait_recv()

        def sibling_here_send_second():
            copy(0, 0, sibling, me).wait_recv()
            for a in range(na):
                copy(a, 2, me, (*chips[1], c), src=stages[a]).start()

        @pl.when((j == 0) & (i == 0))
        def _():
            for a in range(na):
                stages[a][...] = ins[a][...].astype(dtypes[a])
                local(a).start()
            for a in range(na):
                copy(a, 0, me, sibling, src=stages[a]).start()
                copy(a, 1, me, (*chips[0], c), src=stages[a]).start()

        @pl.when(j == 0)
        def _():
            xv = x_ref[...]
            r = lax.rsqrt(jnp.mean(xv * xv, axis=-1, keepdims=True) + EPS)
            hb = ((xv * r) * g_ref[...]).astype(BF16)
            h_full[pl.ds(pl.multiple_of(i * tm, tm), tm), :] = hb

        save_h = pltpu.make_async_copy(h_full, h_ref, hsem)
        pl.when((j == 0) & (i == ni - 1))(save_h.start)

        steps = [
            lambda: local(0).wait(),
            sibling_here_send_second,
            lambda: landed(0, 0),
            lambda: passed_on(0, 0),
            lambda: landed(0, 1),
            lambda: passed_on(0, 1),
            diagonal_and_small,
            lambda: passed_on(0, 2),
        ]
        def w_load(k):
            return pltpu.make_async_copy(outs[0].at[order_ref[k]], wbuf.at[k % 2], wsems.at[k % 2])

        for k, step in enumerate(steps):
            @pl.when((j == 0) & (i == 0) if k == 0 else (j == k - 1) & (i == ni - 1))
            def _(k=k, step=step):
                step()
                w_load(k).start()

        pl.when(i == 0)(lambda: w_load(j).wait())
        z_ref[0] = jnp.dot(h_full[pl.ds(pl.multiple_of(i * tm, tm), tm), :], wbuf[j % 2], preferred_element_type=F32)

        @pl.when((j == NB - 1) & (i == ni - 1))
        def _():
            save_h.wait()
            for a in small:
                landed(a, 2)
            for a in small:
                local(a).wait()
                copy(a, 0, sibling, me).wait_recv()
                for slot in range(3):
                    passed_on(a, slot)
            for a in range(na):
                copy(a, 0, me, sibling, src=stages[a]).wait_send()
                for slot, chip in enumerate(chips):
                    copy(a, 1 + slot, me, (*chip, c), src=stages[a]).wait_send()
                    copy(a, 4 + slot, (*chip, c), sibling).wait_send()

    rows_once = lambda j, i, order: (jnp.where(j == 0, i, ni - 1), 0)
    vmem = pl.BlockSpec(memory_space=pltpu.VMEM)
    return pl.pallas_call(
        body, name="gather_inproj",
        grid_spec=pltpu.PrefetchScalarGridSpec(
            num_scalar_prefetch=1, grid=(NB, ni),
            in_specs=[pl.BlockSpec((tm, D), rows_once), pl.BlockSpec((1, D), lambda j, i, order: (0, 0))] + [vmem] * na,
            out_specs=[pl.BlockSpec((1, tm, D), lambda j, i, order: (order[j], i, 0)), ANY] + [ANY] * na,
            scratch_shapes=[pltpu.VMEM(b.shape, dt) for b, dt in zip(blocks, dtypes)]
            + [pltpu.VMEM((n, D), BF16), pltpu.VMEM((2, D, D), BF16),
               pltpu.SemaphoreType.DMA((7 * na,)), pltpu.SemaphoreType.DMA((7 * na,)),
               pltpu.SemaphoreType.DMA((na,)), pltpu.SemaphoreType.DMA((2,)), pltpu.SemaphoreType.DMA(())]),
        out_shape=[SDS((NB, n, D), F32), SDS((n, D), BF16)] + [SDS((NB,) + b.shape, dt) for b, dt in zip(blocks, dtypes)],
        compiler_params=_params(56),
    )(order_ids, x2, norm_g, *blocks)


LRU_T = 256


def _shifted(groups, shifts):
    row = lax.broadcasted_iota(jnp.int32, (groups.shape[0] - 1,) + groups.shape[1:], 1)
    out = []
    for s in shifts:
        y = pltpu.roll(groups, s % 8, 1)
        moved = jnp.where(row >= s, y[1:], y[:-1]) if s > 0 else jnp.where(row < 8 + s, y[:-1], y[1:])
        out.append(moved.reshape(-1, groups.shape[2]))
    return out


def _conv(taps, cw, cb):
    acc = taps[0] * cw[0:1, :] + taps[1] * cw[1:2, :]
    acc = acc + taps[2] * cw[2:3, :]
    acc = acc + taps[3] * cw[3:4, :]
    return cb + acc


def _lru_gates(xa, wx_ref, wa_ref, bx, ba, lam):
    xab = xa.astype(BF16)
    pis, prs = [], []
    for h in range(NB):
        xs = xab[:, h * BD:(h + 1) * BD]
        pis.append(jnp.dot(xs, wx_ref[h], preferred_element_type=F32))
        prs.append(jnp.dot(xs, wa_ref[h], preferred_element_type=F32))
    gi = _sigmoid(jnp.concatenate(pis, axis=1) + bx)
    gr = _sigmoid(jnp.concatenate(prs, axis=1) + ba)
    sp = _softplus_neg(lam)
    log_a = (-LRU_C * gr) * sp
    a = jnp.exp(log_a)
    mult = jnp.sqrt(-jnp.tanh(log_a) * (a * a + 1.0))
    return xab, gi, gr, sp, a, mult


def _lru_fwd(z, cw8, cb, wx, wa, bx, ba, lam, nb, s_len):
    n = nb * s_len
    t = LRU_T
    ns = s_len // t

    def body(xp_ref, ga_ref, cw_ref, cb_ref, wx_ref, wa_ref, bx_ref, ba_ref, lam_ref,
             h_ref, ya_ref, ext, a_s, u_s, carry):
        @pl.when(pl.program_id(1) == 0)
        def _():
            ext[0:8, :] = jnp.zeros((8, D), F32)
            carry[...] = jnp.zeros((8, D), F32)

        xp = xp_ref[0]
        ext[8:8 + t, :] = xp
        xa = _conv(_shifted(_groups(ext[...]), (3, 2, 1)) + [xp], cw_ref[...], cb_ref[...])
        ext[0:8, :] = xp[t - 8:t, :]
        _, gi, _, _, a, mult = _lru_gates(xa, wx_ref, wa_ref, bx_ref[...], ba_ref[...], lam_ref[...])
        u = (mult * gi) * xa
        a, u = _groups(a), _groups(u)
        row = lax.broadcasted_iota(jnp.int32, a.shape, 1)
        for sh in (1, 2, 4):
            a_sh = pltpu.roll(a, sh, 1)
            u_sh = pltpu.roll(u, sh, 1)
            m = row >= sh
            u = jnp.where(m, a * u_sh + u, u)
            a = jnp.where(m, a * a_sh, a)
        a_s[...] = a.reshape(t, D)
        u_s[...] = u.reshape(t, D)

        def step(g, c):
            r = pl.multiple_of(g * 8, 8)
            hg = u_s[pl.ds(r, 8), :] + a_s[pl.ds(r, 8), :] * c
            h_ref[pl.ds(r, 8), :] = hg
            return hg[7:8, :]

        c_out = lax.fori_loop(0, t // 8, step, carry[0:1, :], unroll=4)
        carry[0:1, :] = c_out
        ga = ga_ref[0]
        ya_ref[...] = (h_ref[...] * (ga * _sigmoid(ga))).astype(BF16)

    row_map = lambda b, s: (b * ns + s, 0)
    rep2 = lambda b, s: (0, 0)
    rep3 = lambda b, s: (0, 0, 0)
    return pl.pallas_call(
        body, name="lru_fwd", grid=(nb, ns),
        in_specs=[pl.BlockSpec((1, t, D), lambda b, s: (0, b * ns + s, 0)),
                  pl.BlockSpec((1, t, D), lambda b, s: (1, b * ns + s, 0)),
                  pl.BlockSpec((8, D), rep2), pl.BlockSpec((1, D), rep2),
                  pl.BlockSpec((NB, BD, BD), rep3), pl.BlockSpec((NB, BD, BD), rep3),
                  pl.BlockSpec((1, D), rep2), pl.BlockSpec((1, D), rep2), pl.BlockSpec((1, D), rep2)],
        out_specs=[pl.BlockSpec((t, D), row_map), pl.BlockSpec((t, D), row_map)],
        out_shape=[SDS((n, D), F32), SDS((n, D), BF16)],
        scratch_shapes=[pltpu.VMEM((t + 8, D), F32), pltpu.VMEM((t, D), F32), pltpu.VMEM((t, D), F32),
                        pltpu.VMEM((8, D), F32)],
        compiler_params=_params(48),
    )(z, z, cw8, cb, wx, wa, bx, ba, lam)


def _lru_bwd(z, h_all, dya, cw8, cb, wx, wa, bx, ba, lam, nb, s_len):
    n = nb * s_len
    t = LRU_T
    ns = s_len // t
    t8 = t // 8

    def body(xp_ref, xph_ref, ga_ref, h_ref, hh_ref, dya_ref, cw_ref, cb_ref, wx_ref, wa_ref, bx_ref, ba_ref,
             lam_ref, dz_ref, gcw_ref, gcb_ref, gwx_ref, gwa_ref, gbx_ref, gba_ref, glam_ref,
             ext, hext, dext, a_s, u_s, dh_s, carry):
        b, s = pl.program_id(0), pl.program_id(1)
        first_tile = s == ns - 1

        @pl.when((b == 0) & (s == 0))
        def _():
            for ref in (gcw_ref, gcb_ref, gwx_ref, gwa_ref, gbx_ref, gba_ref, glam_ref):
                ref[...] = jnp.zeros(ref.shape, F32)

        @pl.when(s == 0)
        def _():
            dext[t:t + 8, :] = jnp.zeros((8, D), F32)
            carry[...] = jnp.zeros((8, D), F32)

        keep = jnp.where(first_tile, 0.0, 1.0)
        xp = xp_ref[0]
        ext[0:8, :] = xph_ref[0] * keep
        ext[8:8 + t, :] = xp
        hext[0:8, :] = hh_ref[...] * keep
        hext[8:8 + t, :] = h_ref[...]
        cw = cw_ref[...]
        lam = lam_ref[...]
        taps = _shifted(_groups(ext[...]), (3, 2, 1)) + [xp]
        xa = _conv(taps, cw, cb_ref[...])
        xab, gi, gr, sp, a, mult = _lru_gates(xa, wx_ref, wa_ref, bx_ref[...], ba_ref[...], lam)
        (h_prev,) = _shifted(_groups(hext[...]), (1,))
        ga = ga_ref[0]
        sg = _sigmoid(ga)
        dya_v = dya_ref[...]
        d_ga = dya_v * h_ref[...] * (sg * (1.0 + ga * (1.0 - sg)))
        g_in = dya_v * (ga * sg)

        (an,) = _shifted(jnp.concatenate([_groups(a), jnp.ones((1, 8, D), F32)], axis=0), (-1,))
        an, u = _groups(an), _groups(g_in)
        row = lax.broadcasted_iota(jnp.int32, an.shape, 1)
        for sh in (1, 2, 4):
            a_sh = pltpu.roll(an, 8 - sh, 1)
            u_sh = pltpu.roll(u, 8 - sh, 1)
            m = row < 8 - sh
            u = jnp.where(m, u + an * u_sh, u)
            an = jnp.where(m, an * a_sh, an)
        a_s[...] = an.reshape(t, D)
        u_s[...] = u.reshape(t, D)

        def step(i, c):
            r = pl.multiple_of((t8 - 1 - i) * 8, 8)
            dg = u_s[pl.ds(r, 8), :] + a_s[pl.ds(r, 8), :] * c
            dh_s[pl.ds(r, 8), :] = dg
            return dg[0:1, :]

        lax.fori_loop(0, t8, step, carry[0:1, :], unroll=4)
        dh = dh_s[...]
        carry[0:1, :] = a[0:1, :] * dh[0:1, :]

        d_a = dh * h_prev
        dux = dh * xa
        d_mult = dux * gi
        d_gi = dux * mult
        d_xa = dh * (mult * gi)
        d_loga = d_a * a - d_mult * ((a * a) / mult)
        d_gr = d_loga * (-LRU_C * sp)
        d_sp = jnp.sum(d_loga * (-LRU_C * gr), axis=0, keepdims=True)
        glam_ref[...] += d_sp * (-_sigmoid(-lam))
        d_pi = d_gi * gi * (1.0 - gi)
        d_pr = d_gr * gr * (1.0 - gr)
        gbx_ref[...] += jnp.sum(d_pi, axis=0, keepdims=True)
        gba_ref[...] += jnp.sum(d_pr, axis=0, keepdims=True)
        dpib = d_pi.astype(BF16)
        dprb = d_pr.astype(BF16)
        back = []
        for h in range(NB):
            cs = slice(h * BD, (h + 1) * BD)
            gwx_ref[h] += lax.dot_general(xab[:, cs], dpib[:, cs], TN_DIMS, preferred_element_type=F32)
            gwa_ref[h] += lax.dot_general(xab[:, cs], dprb[:, cs], TN_DIMS, preferred_element_type=F32)
            back.append(lax.dot_general(dpib[:, cs], wx_ref[h], NT_DIMS, preferred_element_type=F32)
                        + lax.dot_general(dprb[:, cs], wa_ref[h], NT_DIMS, preferred_element_type=F32))
        d_xa = d_xa + jnp.concatenate(back, axis=1)

        dext[0:t, :] = d_xa
        later = _shifted(_groups(dext[...]), (-3, -2, -1))
        d_xp = later[0] * cw[0:1, :] + later[1] * cw[1:2, :]
        d_xp = d_xp + later[2] * cw[2:3, :]
        d_xp = d_xp + d_xa * cw[3:4, :]
        dext[t:t + 8, :] = d_xa[0:8, :]
        gcb_ref[...] += jnp.sum(d_xa, axis=0, keepdims=True)
        for k in range(4):
            gcw_ref[k:k + 1, :] += jnp.sum(d_xa * taps[k], axis=0, keepdims=True)
        dz_ref[0] = d_xp.astype(BF16)
        dz_ref[1] = d_ga.astype(BF16)

    rb = lambda b, s: b * ns + (ns - 1 - s)
    halo = lambda b, s: jnp.maximum(rb(b, s) * t8 - 1, 0)
    rep2 = lambda b, s: (0, 0)
    rep3 = lambda b, s: (0, 0, 0)
    return pl.pallas_call(
        body, name="lru_bwd", grid=(nb, ns),
        in_specs=[pl.BlockSpec((1, t, D), lambda b, s: (0, rb(b, s), 0)),
                  pl.BlockSpec((1, 8, D), lambda b, s: (0, halo(b, s), 0)),
                  pl.BlockSpec((1, t, D), lambda b, s: (1, rb(b, s), 0)),
                  pl.BlockSpec((t, D), lambda b, s: (rb(b, s), 0)),
                  pl.BlockSpec((8, D), lambda b, s: (halo(b, s), 0)),
                  pl.BlockSpec((t, D), lambda b, s: (rb(b, s), 0)),
                  pl.BlockSpec((8, D), rep2), pl.BlockSpec((1, D), rep2),
                  pl.BlockSpec((NB, BD, BD), rep3), pl.BlockSpec((NB, BD, BD), rep3),
                  pl.BlockSpec((1, D), rep2), pl.BlockSpec((1, D), rep2), pl.BlockSpec((1, D), rep2)],
        out_specs=[pl.BlockSpec((2, t, D), lambda b, s: (0, rb(b, s), 0)),
                   pl.BlockSpec((8, D), rep2), pl.BlockSpec((1, D), rep2),
                   pl.BlockSpec((NB, BD, BD), rep3), pl.BlockSpec((NB, BD, BD), rep3),
                   pl.BlockSpec((1, D), rep2), pl.BlockSpec((1, D), rep2), pl.BlockSpec((1, D), rep2)],
        out_shape=[SDS((2, n, D), BF16), SDS((8, D), F32), SDS((1, D), F32),
                   SDS((NB, BD, BD), F32), SDS((NB, BD, BD), F32),
                   SDS((1, D), F32), SDS((1, D), F32), SDS((1, D), F32)],
        scratch_shapes=[pltpu.VMEM((t + 8, D), F32), pltpu.VMEM((t + 8, D), F32), pltpu.VMEM((t + 8, D), F32),
                        pltpu.VMEM((t, D), F32), pltpu.VMEM((t, D), F32), pltpu.VMEM((t, D), F32),
                        pltpu.VMEM((8, D), F32)],
        compiler_params=_params(56),
    )(z, z, z, h_all, h_all, dya, cw8, cb, wx, wa, bx, ba, lam)


HG_T = 512
HG_NC = HG_T // CHUNK
BNT_DIMS = (((2,), (2,)), ((0,), (0,)))
BNN_DIMS = (((2,), (1,)), ((0,), (0,)))
BTN_DIMS = (((1,), (1,)), ((0,), (0,)))


def _lower_bound(lg):
    m = jnp.max(lg, axis=0, keepdims=True)
    e = jnp.exp(lg - m)
    return e[0:1, :] / jnp.sum(e, axis=0, keepdims=True)


def _tri(upper):
    r = lax.broadcasted_iota(jnp.int32, (HG_NC, CHUNK, CHUNK), 1)
    c = lax.broadcasted_iota(jnp.int32, (HG_NC, CHUNK, CHUNK), 2)
    return (c >= r) if upper else (r >= c)


def _bdot(a, b, dims):
    return lax.dot_general(a, b, dims, preferred_element_type=F32)


def _tri_sums(upper, a):
    tri = _tri(upper).astype(BF16)
    a1 = a.astype(BF16)
    r1 = a - a1.astype(F32)
    a2 = r1.astype(BF16)
    a3 = (r1 - a2.astype(F32)).astype(BF16)
    return _bdot(tri, a1, BNN_DIMS) + (_bdot(tri, a2, BNN_DIMS) + _bdot(tri, a3, BNN_DIMS))


def _chunks(a):
    return a.reshape(HG_NC, CHUNK, BD)


def _hg_tile(q, fp, lb):
    q, fp = _chunks(q), _chunks(fp)
    sig = _sigmoid(fp)
    f = lb + (1.0 - lb) * sig
    log_f = jnp.log(f)
    k = 1.0 - f
    b = _tri_sums(False, log_f)
    b_mid = b[:, CHUNK // 2:CHUNK // 2 + 1, :]
    b_last = b[:, CHUNK - 1:CHUNK, :]
    sq = _sigmoid(q)
    qh = q * sq
    e_qi = jnp.exp(b - b_mid)
    e_ki = jnp.exp(b_mid - b)
    e_qs = jnp.exp(b)
    e_ks = jnp.exp(b_last - b)
    dc = jnp.exp(b_last)
    q_in = (qh * e_qi) * HG_SCALE
    k_in = k * e_ki
    q_st = (qh * e_qs) * HG_SCALE
    k_st = k * e_ks
    att = _bdot(q_in.astype(BF16), k_in.astype(BF16), BNT_DIMS)
    att = jnp.where(_tri(False), att, 0.0)
    return dict(q=q, sig=sig, f=f, k=k, sq=sq, e_qi=e_qi, e_ki=e_ki, e_qs=e_qs, e_ks=e_ks, dc=dc,
                q_in=q_in, k_in=k_in, q_st=q_st, k_st=k_st, att=att)


def _hgrn_fwd(z, lb_logits, hg_g, nb, s_len):
    n = nb * s_len
    t = HG_T
    ns = s_len // t
    nchunk = s_len // CHUNK

    def body(q_ref, f_ref, v_ref, gb_ref, lg_ref, g_ref, o_ref, yb_ref, st_ref, st):
        @pl.when(pl.program_id(1) == 0)
        def _():
            st[...] = jnp.zeros((NB, BD, BD), F32)

        def head(h, carry):
            cols = pl.ds(pl.multiple_of(h * BD, BD), BD)
            lb = _lower_bound(lg_ref[:, cols])
            ck = _hg_tile(q_ref[0, :, cols], f_ref[0, :, cols], lb)
            vb = _chunks(v_ref[0, :, cols]).astype(BF16)
            kv = _bdot(vb, ck["k_st"].astype(BF16), BTN_DIMS)
            states = [st[h]]
            for c in range(HG_NC):
                states.append(states[c] * ck["dc"][c] + kv[c])
            st[h] = states[HG_NC]
            s_in = jnp.stack(states[:HG_NC], axis=0)
            st_ref[h] = s_in
            o = (_bdot(ck["att"].astype(BF16), vb, BNN_DIMS)
                 + _bdot(ck["q_st"].astype(BF16), s_in.astype(BF16), BNT_DIMS))
            o_ref[:, cols] = o.reshape(t, BD)
            r = lax.rsqrt(jnp.mean(o * o, axis=-1, keepdims=True) + EPS)
            gb = _chunks(gb_ref[0, :, cols])
            yb_ref[:, cols] = (((o * r) * g_ref[...]) * (gb * _sigmoid(gb))).astype(BF16).reshape(t, BD)
            return carry

        lax.fori_loop(0, NB, head, 0, unroll=4)

    seg = lambda j: pl.BlockSpec((1, t, D), lambda b, s: (j, b * ns + s, 0))
    tile = pl.BlockSpec((t, D), lambda b, s: (b * ns + s, 0))
    return pl.pallas_call(
        body, name="hgrn_fwd", grid=(nb, ns),
        in_specs=[seg(2), seg(3), seg(4), seg(5),
                  pl.BlockSpec((2, D), lambda b, s: (0, 0)), pl.BlockSpec((1, BD), lambda b, s: (0, 0))],
        out_specs=[tile, tile, pl.BlockSpec((NB, HG_NC, BD, BD), lambda b, s: (b, s, 0, 0))],
        out_shape=[SDS((n, D), F32), SDS((n, D), BF16), SDS((nb * NB, nchunk, BD, BD), F32)],
        scratch_shapes=[pltpu.VMEM((NB, BD, BD), F32)],
        compiler_params=_params(56),
    )(z, z, z, z, lb_logits, hg_g)


def _hgrn_bwd(z, o_all, st_all, dyb, lb_logits, hg_g, nb, s_len):
    n = nb * s_len
    t = HG_T
    ns = s_len // t

    def body(q_ref, f_ref, v_ref, gb_ref, o_ref, st_ref, dyb_ref, lg_ref, g_ref,
             dz_ref, glg_ref, ghg_ref, dst, dlb):
        b, s = pl.program_id(0), pl.program_id(1)

        @pl.when((b == 0) & (s == 0))
        def _():
            ghg_ref[...] = jnp.zeros((1, BD), F32)
            dlb[...] = jnp.zeros((8, D), F32)

        @pl.when(s == 0)
        def _():
            dst[...] = jnp.zeros((NB, BD, BD), F32)

        g = g_ref[...]

        def head(h, carry):
            cols = pl.ds(pl.multiple_of(h * BD, BD), BD)
            lb = _lower_bound(lg_ref[:, cols])
            ck = _hg_tile(q_ref[0, :, cols], f_ref[0, :, cols], lb)
            q = ck["q"]
            vb = _chunks(v_ref[0, :, cols]).astype(BF16)
            gb = _chunks(gb_ref[0, :, cols])
            o = _chunks(o_ref[:, cols])
            dyb_v = _chunks(dyb_ref[:, cols])
            s_in = st_ref[h]

            sgb = _sigmoid(gb)
            r = lax.rsqrt(jnp.mean(o * o, axis=-1, keepdims=True) + EPS)
            ohat = o * r
            d_on = dyb_v * (gb * sgb)
            d_gb = dyb_v * (ohat * g) * (sgb * (1.0 + gb * (1.0 - sgb)))
            ghg_ref[...] += jnp.sum(jnp.sum(d_on * ohat, axis=1), axis=0, keepdims=True)
            tt = d_on * g
            d_o = r * (tt - ohat * jnp.mean(tt * ohat, axis=-1, keepdims=True))
            dob = d_o.astype(BF16)

            attb = ck["att"].astype(BF16)
            q_inb, k_inb = ck["q_in"].astype(BF16), ck["k_in"].astype(BF16)
            q_stb, k_stb = ck["q_st"].astype(BF16), ck["k_st"].astype(BF16)
            d_att = jnp.where(_tri(False), _bdot(dob, vb, BNT_DIMS), 0.0).astype(BF16)
            d_q_in = _bdot(d_att, k_inb, BNN_DIMS)
            d_k_in = _bdot(d_att, q_inb, BTN_DIMS)
            d_q_st = _bdot(dob, s_in.astype(BF16), BNN_DIMS)
            qdo = _bdot(dob, q_stb, BTN_DIMS)
            d_states = [None] * HG_NC + [dst[h]]
            for c in reversed(range(HG_NC)):
                d_states[c] = d_states[c + 1] * ck["dc"][c] + qdo[c]
            dst[h] = d_states[0]
            ds_out = jnp.stack(d_states[1:], axis=0)
            dsb = ds_out.astype(BF16)
            d_v = _bdot(attb, dob, BTN_DIMS) + _bdot(k_stb, dsb, BNT_DIMS)
            d_k_st = _bdot(vb, dsb, BNN_DIMS)
            d_dc = jnp.sum(ds_out * s_in, axis=1, keepdims=True)

            p_qi = d_q_in * ck["q_in"]
            p_ki = d_k_in * ck["k_in"]
            p_qs = d_q_st * ck["q_st"]
            p_ks = d_k_st * ck["k_st"]
            d_qh = (d_q_in * ck["e_qi"] + d_q_st * ck["e_qs"]) * HG_SCALE
            d_k = d_k_in * ck["e_ki"] + d_k_st * ck["e_ks"]
            d_b = (p_qi - p_ki) + (p_qs - p_ks)
            d_b_mid = jnp.sum(p_ki - p_qi, axis=1, keepdims=True)
            d_b_last = jnp.sum(p_ks, axis=1, keepdims=True) + d_dc * ck["dc"]
            rowi = lax.broadcasted_iota(jnp.int32, (HG_NC, CHUNK, BD), 1)
            d_b = d_b + jnp.where(rowi == CHUNK // 2, d_b_mid, 0.0) + jnp.where(rowi == CHUNK - 1, d_b_last, 0.0)
            d_logf = _tri_sums(True, d_b)
            d_f = d_logf / ck["f"] - d_k
            sig, sq = ck["sig"], ck["sq"]
            d_fp = d_f * (1.0 - lb) * (sig * (1.0 - sig))
            dlb[0:1, cols] += jnp.sum(jnp.sum(d_f * (1.0 - sig), axis=1), axis=0, keepdims=True)
            d_q = d_qh * (sq * (1.0 + q * (1.0 - sq)))
            dz_ref[0, :, cols] = d_q.astype(BF16).reshape(t, BD)
            dz_ref[1, :, cols] = d_fp.astype(BF16).reshape(t, BD)
            dz_ref[2, :, cols] = d_v.astype(BF16).reshape(t, BD)
            dz_ref[3, :, cols] = d_gb.astype(BF16).reshape(t, BD)
            return carry

        lax.fori_loop(0, NB, head, 0, unroll=2)

        @pl.when((b == nb - 1) & (s == ns - 1))
        def _():
            lb = _lower_bound(lg_ref[...])
            dl = dlb[0:1, :] * (lb * (1.0 - lb))
            glg_ref[0:1, :] = dl
            glg_ref[1:2, :] = -dl

    rb = lambda b, s: b * ns + (ns - 1 - s)
    seg = lambda j: pl.BlockSpec((1, t, D), lambda b, s: (j, rb(b, s), 0))
    tile = pl.BlockSpec((t, D), lambda b, s: (rb(b, s), 0))
    return pl.pallas_call(
        body, name="hgrn_bwd", grid=(nb, ns),
        in_specs=[seg(2), seg(3), seg(4), seg(5), tile,
                  pl.BlockSpec((NB, HG_NC, BD, BD), lambda b, s: (b, ns - 1 - s, 0, 0)),
                  tile, pl.BlockSpec((2, D), lambda b, s: (0, 0)), pl.BlockSpec((1, BD), lambda b, s: (0, 0))],
        out_specs=[pl.BlockSpec((4, t, D), lambda b, s: (0, rb(b, s), 0)),
                   pl.BlockSpec((2, D), lambda b, s: (0, 0)), pl.BlockSpec((1, BD), lambda b, s: (0, 0))],
        out_shape=[SDS((4, n, D), BF16), SDS((2, D), F32), SDS((1, BD), F32)],
        scratch_shapes=[pltpu.VMEM((NB, BD, BD), F32), pltpu.VMEM((8, D), F32)],
        compiler_params=_params(60),
    )(z, z, z, z, o_all, st_all, dyb, lb_logits, hg_g)


def _mid(ya, yb, z, b_merge, x2, tgt, fin_g, pa, pb, wo):
    n = x2.shape[0]
    tm = 256
    ni = n // tm

    def body(ya_ref, yb_ref, gma_ref, gmb_ref, bm_ref, x_ref, t_ref, fg_ref, pa_hbm, pb_hbm, wo_hbm,
             dx2_ref, dya_ref, dyb_ref, dgm_ref, loss_ref, gfg_ref, gbm_ref, gm_hbm,
             pa_v, pb_v, wo_v, gpa_v, gpb_v, gwo_v, sem):
        i = pl.program_id(0)
        by_owner = lambda g: g.reshape(NB, BD, D)
        loads = [pltpu.make_async_copy(src, dst, sem.at[k])
                 for k, (src, dst) in enumerate(((pa_hbm, pa_v), (pb_hbm, pb_v), (wo_hbm, wo_v)))]
        stores = [pltpu.make_async_copy(src, dst, sem.at[k])
                  for k, (src, dst) in enumerate((g, gm_hbm.at[:, pl.ds(slot * BD, BD), :])
                                                 for slot, g in enumerate((gpa_v, gpb_v, gwo_v)))]

        @pl.when(i == 0)
        def _():
            for cp in loads:
                cp.start()
            for ref in (gpa_v, gpb_v, gwo_v, loss_ref, gfg_ref, gbm_ref):
                ref[...] = jnp.zeros(ref.shape, F32)
            for cp in loads:
                cp.wait()

        ya_v = ya_ref[...]
        yb_v = yb_ref[...]
        out_a = jnp.dot(ya_v, pa_v[...], preferred_element_type=F32)
        out_b = jnp.dot(yb_v, pb_v[...], preferred_element_type=F32)
        bm = bm_ref[...]
        g_a = _sigmoid(gma_ref[0] + bm[:, 0:D])
        g_b = _sigmoid(gmb_ref[0] + bm[:, D:2 * D])
        mixed = g_a * out_a + g_b * out_b
        mixb = mixed.astype(BF16)
        xo = x_ref[...] + jnp.dot(mixb, wo_v[...], preferred_element_type=F32)
        r = lax.rsqrt(jnp.mean(xo * xo, axis=-1, keepdims=True) + EPS)
        xn = xo * r
        fg = fg_ref[...]
        e = xn * fg - t_ref[...]
        loss_ref[...] += 0.5 * jnp.sum(jnp.mean(e * e, axis=-1, keepdims=True))
        dy = e * (1.0 / D)
        gfg_ref[...] += jnp.sum(dy * xn, axis=0, keepdims=True)
        dxn = dy * fg
        dx2 = r * (dxn - xn * jnp.mean(dxn * xn, axis=-1, keepdims=True))
        dx2_ref[...] = dx2
        dx2b = dx2.astype(BF16)
        d_mixed = lax.dot_general(dx2b, wo_v[...], NT_DIMS, preferred_element_type=F32)
        gwo_v[...] += by_owner(lax.dot_general(mixb, dx2b, TN_DIMS, preferred_element_type=F32))
        d_oa = (d_mixed * g_a).astype(BF16)
        d_ob = (d_mixed * g_b).astype(BF16)
        dgm_a = (d_mixed * out_a) * (g_a * (1.0 - g_a))
        dgm_b = (d_mixed * out_b) * (g_b * (1.0 - g_b))
        gbm_ref[:, 0:D] += jnp.sum(dgm_a, axis=0, keepdims=True)
        gbm_ref[:, D:2 * D] += jnp.sum(dgm_b, axis=0, keepdims=True)
        dgm_ref[0] = dgm_a.astype(BF16)
        dgm_ref[1] = dgm_b.astype(BF16)
        dya_ref[...] = lax.dot_general(d_oa, pa_v[...], NT_DIMS, preferred_element_type=F32)
        dyb_ref[...] = lax.dot_general(d_ob, pb_v[...], NT_DIMS, preferred_element_type=F32)
        gpa_v[...] += by_owner(lax.dot_general(ya_v, d_oa, TN_DIMS, preferred_element_type=F32))
        gpb_v[...] += by_owner(lax.dot_general(yb_v, d_ob, TN_DIMS, preferred_element_type=F32))

        @pl.when(i == ni - 1)
        def _():
            for cp in stores:
                cp.start()
            for cp in stores:
                cp.wait()

    rows = pl.BlockSpec((tm, D), lambda i: (i, 0))
    rep = lambda shape: pl.BlockSpec(shape, lambda i: (0,) * len(shape))
    return pl.pallas_call(
        body, name="mid", grid=(ni,),
        in_specs=[rows, rows,
                  pl.BlockSpec((1, tm, D), lambda i: (6, i, 0)), pl.BlockSpec((1, tm, D), lambda i: (7, i, 0)),
                  rep((1, 2 * D)), rows, rows, rep((1, D)), ANY, ANY, ANY],
        out_specs=[rows, rows, rows, pl.BlockSpec((2, tm, D), lambda i: (0, i, 0)),
                   rep((8, BD)), rep((1, D)), rep((1, 2 * D)), ANY],
        out_shape=[SDS((n, D), F32), SDS((n, D), F32), SDS((n, D), F32), SDS((2, n, D), BF16),
                   SDS((8, BD), F32), SDS((1, D), F32), SDS((1, 2 * D), F32),
                   SDS((NB, MID_ROWS, D), F32)],
        scratch_shapes=[pltpu.VMEM((D, D), BF16)] * 3 + [pltpu.VMEM((NB, BD, D), F32)] * 3 + [pltpu.SemaphoreType.DMA((3,))],
        compiler_params=_params(60),
    )(ya, yb, z, z, b_merge, x2, tgt, fin_g, pa, pb, wo)


def _dz_specs(tm, ni, row_major):
    if row_major:
        ia = lambda i, j: (jnp.minimum(j, 1), i, 0)
        ib = lambda i, j: (jnp.clip(j - 2, 0, 3), i, 0)
        im = lambda i, j: (jnp.clip(j - 6, 0, 1), i, 0)
    else:
        last = ni - 1
        ia = lambda j, i: (jnp.minimum(j, 1), jnp.where(j < 2, i, last), 0)
        ib = lambda j, i: (jnp.clip(j - 2, 0, 3), jnp.where(j < 2, 0, jnp.where(j < 6, i, last)), 0)
        im = lambda j, i: (jnp.clip(j - 6, 0, 1), jnp.where(j < 6, 0, i), 0)
    return [pl.BlockSpec((1, tm, D), f) for f in (ia, ib, im)]


def _inproj_bwd_x(dza, dzb, dzm, w_all, x2, dx2, norm_g, after):
    n = x2.shape[0]
    tm = 512
    ni = n // tm

    def body(dza_ref, dzb_ref, dzm_ref, w_ref, x_ref, dx2_ref, g_ref, after_ref, gx_ref, gg_ref, acc):
        i, j = pl.program_id(0), pl.program_id(1)

        @pl.when((i == 0) & (j == 0))
        def _():
            gg_ref[...] = jnp.zeros((1, D), F32)

        @pl.when(j == 0)
        def _():
            acc[...] = jnp.zeros((tm, D), F32)

        def add(ref):
            acc[...] += lax.dot_general(ref[0], w_ref[0], NT_DIMS, preferred_element_type=F32)

        pl.when(j < 2)(lambda: add(dza_ref))
        pl.when((j >= 2) & (j < 6))(lambda: add(dzb_ref))
        pl.when(j >= 6)(lambda: add(dzm_ref))

        @pl.when(j == NB - 1)
        def _():
            x = x_ref[...]
            r = lax.rsqrt(jnp.mean(x * x, axis=-1, keepdims=True) + EPS)
            xn = x * r
            dh = acc[...]
            gg_ref[...] += jnp.sum(dh * xn, axis=0, keepdims=True)
            dxn = dh * g_ref[...]
            gx_ref[...] = dx2_ref[...] + r * (dxn - xn * jnp.mean(dxn * xn, axis=-1, keepdims=True))

    rows = pl.BlockSpec((tm, D), lambda i, j: (i, 0))
    return pl.pallas_call(
        body, name="inproj_bwd_x", grid=(ni, NB),
        in_specs=_dz_specs(tm, ni, True) + [pl.BlockSpec((1, D, D), lambda i, j: (j, 0, 0)), rows, rows,
                                             pl.BlockSpec((1, D), lambda i, j: (0, 0)), ANY],
        out_specs=[rows, pl.BlockSpec((1, D), lambda i, j: (0, 0))],
        out_shape=[SDS((n, D), F32), SDS((1, D), F32)],
        scratch_shapes=[pltpu.VMEM((tm, D), F32)],
        compiler_params=_params(48),
    )(dza, dzb, dzm, w_all, x2, dx2, norm_g, after)


def _walk_tables(order, ni):
    rows = []
    for lo, hi in ((0, 2), (2, 6), (6, 8)):
        active = [j for j, g in enumerate(order) if lo <= g < hi]
        block, tile = [], []
        for j, g in enumerate(order):
            before = [a for a in active if a < j]
            if lo <= g < hi:
                block.append(g - lo), tile.append(-1)
            elif before:
                block.append(order[before[-1]] - lo), tile.append(ni - 1)
            else:
                block.append(order[active[0]] - lo), tile.append(0)
        rows += [block, tile]
    return rows


def _inproj_bwd_w(core, dza, dzb, dzm, h_all, g_m):
    n = h_all.shape[0]
    tm = min(n, 2048)
    ni = n // tm
    packed = g_m.shape[1:]
    orders = [[2 * q + 1 - c for q in range(4)] + [2 * q + c for q in range(4)] for c in (0, 1)]
    tables = jnp.asarray([[order] + _walk_tables(order, ni) for order in orders], jnp.int32)
    walk = jnp.where(core == 0, tables[0], tables[1])

    def body(walk_ref, dza_ref, dzb_ref, dzm_ref, h_ref, gm_hbm, out_bf, own_f32, m_out_bf, m_own_f32, got_w, got_m,
             acc, stage, theirs, m_mine, m_theirs, m_stage, send_sems, recv_sems, local_sems):
        j, i = pl.program_id(0), pl.program_id(1)
        group = walk_ref[0, j]
        x, y, c = _place()
        sibling = (x, y, 1 - c)

        def send_w(q):
            return pltpu.make_async_remote_copy(
                src_ref=stage.at[q % 2], dst_ref=got_w.at[q], send_sem=send_sems.at[q], recv_sem=recv_sems.at[q],
                device_id=sibling, device_id_type=MESH)

        def send_m(q):
            return pltpu.make_async_remote_copy(
                src_ref=gm_hbm.at[2 * q + (1 - c)], dst_ref=got_m.at[q], send_sem=send_sems.at[4 + q],
                recv_sem=recv_sems.at[4 + q], device_id=sibling, device_id_type=MESH)

        def fetch(q):
            return pltpu.make_async_copy(got_w.at[q], theirs, local_sems.at[0])

        def fetch_m(q):
            return (pltpu.make_async_copy(gm_hbm.at[2 * q + c], m_mine, local_sems.at[2]),
                    pltpu.make_async_copy(got_m.at[q], m_theirs, local_sems.at[3]))

        @pl.when((j == 0) & (i == 0))
        def _():
            for q in range(4):
                send_m(q).start()

        @pl.when(i == 0)
        def _():
            acc[...] = jnp.zeros((D, D), F32)

        def add(ref):
            acc[...] += lax.dot_general(h_ref[...], ref[0], TN_DIMS, preferred_element_type=F32)

        pl.when(group < 2)(lambda: add(dza_ref))
        pl.when((group >= 2) & (group < 6))(lambda: add(dzb_ref))
        pl.when(group >= 6)(lambda: add(dzm_ref))

        for q in range(4):
            @pl.when((i == ni - 1) & (j == q))
            def _(q=q):
                if q >= 2:
                    send_w(q - 2).wait_send()
                stage[q % 2] = acc[...].astype(BF16)
                send_w(q).start()

        def reducer(q):
            other_x, other_y = x != q // 2, y != q % 2
            return other_x | other_y, jnp.where(other_x & other_y, 2, jnp.where(other_x, 0, 1))

        def w_out(q):
            return pltpu.make_async_copy(stage.at[0], out_bf.at[reducer(q)[1]], local_sems.at[1])

        for q in range(4):
            @pl.when((j == 4 + q) & (i == 0))
            def _(q=q):
                send_w(q).wait_recv()
                send_m(q).wait_recv()
                fetch(q).start()
                for cp in fetch_m(q):
                    cp.start()
                if q > 0:
                    pl.when(reducer(q - 1)[0])(lambda: w_out(q - 1).wait())

            @pl.when((j == 4 + q) & (i == ni - 1))
            def _(q=q):
                if q == 0:
                    send_w(2).wait_send()
                    send_w(3).wait_send()
                other, slot = reducer(q)
                m_out = pltpu.make_async_copy(m_stage, m_out_bf.at[slot], local_sems.at[4])
                m_own = pltpu.make_async_copy(m_mine, m_own_f32, local_sems.at[4])

                for cp in fetch_m(q):
                    cp.wait()

                @pl.when(other)
                def _():
                    m_stage[...] = (m_mine[...] + m_theirs[...]).astype(BF16)
                    m_out.start()

                @pl.when(jnp.logical_not(other))
                def _():
                    m_mine[...] += m_theirs[...]
                    m_own.start()

                fetch(q).wait()

                @pl.when(other)
                def _():
                    stage[0] = (acc[...] + theirs[...].astype(F32)).astype(BF16)
                    w_out(q).start()
                    if q == 3:
                        w_out(q).wait()
                    m_out.wait()

                @pl.when(jnp.logical_not(other))
                def _():
                    acc[...] += theirs[...].astype(F32)
                    out = pltpu.make_async_copy(acc, own_f32, local_sems.at[1])
                    out.start()
                    out.wait()
                    m_own.wait()

        @pl.when((j == NB - 1) & (i == ni - 1))
        def _():
            for q in range(4):
                send_m(q).wait_send()

    def dz_spec(k):
        return pl.BlockSpec((1, tm, D), lambda j, i, w: (w[1 + 2 * k, j], jnp.where(w[2 + 2 * k, j] < 0, i, w[2 + 2 * k, j]), 0))

    return pl.pallas_call(
        body, name="inproj_bwd_w",
        grid_spec=pltpu.PrefetchScalarGridSpec(
            num_scalar_prefetch=1, grid=(NB, ni),
            in_specs=[dz_spec(0), dz_spec(1), dz_spec(2), pl.BlockSpec((tm, D), lambda j, i, w: (i, 0)), ANY],
            out_specs=[ANY] * 6,
            scratch_shapes=[pltpu.VMEM((D, D), F32), pltpu.VMEM((2, D, D), BF16), pltpu.VMEM((D, D), BF16),
                            pltpu.VMEM(packed, F32), pltpu.VMEM(packed, F32), pltpu.VMEM(packed, BF16),
                            pltpu.SemaphoreType.DMA((8,)), pltpu.SemaphoreType.DMA((8,)), pltpu.SemaphoreType.DMA((5,))]),
        out_shape=[SDS((3, D, D), BF16), SDS((D, D), F32), SDS((3,) + packed, BF16), SDS(packed, F32),
                   SDS((4, D, D), BF16), SDS((4,) + packed, F32)],
        compiler_params=_params(58),
    )(walk, dza, dzb, dzm, h_all, g_m)


def _adamw(w, g, m, v):
    rows, cols = w.shape
    tr = _row_tile(rows)

    spec = pl.BlockSpec((tr, cols), lambda i: (i, 0))
    return pl.pallas_call(
        functools.partial(_adam_refs), name="adamw", grid=(rows // tr,), in_specs=[spec] * 4, out_specs=[spec] * 3,
        out_shape=[SDS((rows, cols), F32)] * 3, compiler_params=_params(32),
    )(w, g, m, v)


def _adam_refs(w_ref, g_ref, m_ref, v_ref, d_ref, nm_ref, nv_ref):
    gv = g_ref[...]
    nm = ADAM_B1 * m_ref[...] + (1.0 - ADAM_B1) * gv
    nv = ADAM_B2 * v_ref[...] + (1.0 - ADAM_B2) * (gv * gv)
    m_hat = nm / (1.0 - ADAM_B1 ** ADAM_STEP)
    v_hat = nv / (1.0 - ADAM_B2 ** ADAM_STEP)
    d_ref[...] = -ADAM_LR * (m_hat / (jnp.sqrt(v_hat) + ADAM_EPS) + ADAM_WD * w_ref[...])
    nm_ref[...] = nm
    nv_ref[...] = nv


def _adamw_small(ws, gs, ms, vs):
    k = len(ws)

    def body(*refs):
        ins, outs = refs[:4 * k], refs[4 * k:7 * k]
        vin, vout = refs[7 * k:11 * k], refs[11 * k:14 * k]
        load_sems, store_sems = refs[14 * k:]
        loads = [pltpu.make_async_copy(ins[i], vin[i], load_sems.at[i]) for i in range(4 * k)]
        for cp in loads:
            cp.start()
        for cp in loads:
            cp.wait()
        for i in range(k):
            _adam_refs(*[vin[part * k + i] for part in range(4)], *[vout[part * k + i] for part in range(3)])
        stores = [pltpu.make_async_copy(vout[i], outs[i], store_sems.at[i]) for i in range(3 * k)]
        for cp in stores:
            cp.start()
        for cp in stores:
            cp.wait()

    shapes = [SDS(w.shape, F32) for w in ws]
    vmem = [pltpu.VMEM(w.shape, F32) for w in ws]
    out = pl.pallas_call(
        body, name="adamw_small", out_shape=shapes * 3, in_specs=[HBM] * (4 * k), out_specs=[HBM] * (3 * k),
        scratch_shapes=vmem * 7 + [pltpu.SemaphoreType.DMA((4 * k,)), pltpu.SemaphoreType.DMA((3 * k,))],
        compiler_params=_params(32),
    )(*ws, *gs, *ms, *vs)
    return out[:k], out[k:2 * k], out[2 * k:]


def _allgather(blocks, dtypes, name):
    na = len(blocks)

    def body(*refs):
        ins, outs, stages = refs[:na], refs[na:2 * na], refs[2 * na:3 * na]
        send_sems, recv_sems, local_sems = refs[3 * na:]
        x, y, c = _place()
        me, sibling = (x, y, c), (x, y, 1 - c)
        chips = [(1 - x, y), (x, 1 - y), (1 - x, 1 - y)]
        blk = lambda p: 4 * p[0] + 2 * p[1] + p[2]

        def copy(a, k, block, to, src=None):
            return pltpu.make_async_remote_copy(
                src_ref=outs[a].at[blk(block)] if src is None else src, dst_ref=outs[a].at[blk(block)],
                send_sem=send_sems.at[7 * a + k], recv_sem=recv_sems.at[7 * a + k],
                device_id=to, device_id_type=MESH)

        mine, first, passed = [], [], []
        for a in range(na):
            stages[a][...] = ins[a][...].astype(dtypes[a])
            mine.append(pltpu.make_async_copy(stages[a], outs[a].at[blk(me)], local_sems.at[a]))
            mine[-1].start()
            first.append(copy(a, 0, me, sibling, src=stages[a]))
            first += [copy(a, 1 + j, me, (*chip, c), src=stages[a]) for j, chip in enumerate(chips)]
        for cp in first:
            cp.start()
        for j, chip in enumerate(chips):
            for a in range(na):
                copy(a, 1 + j, (*chip, c), me).wait_recv()
                passed.append(copy(a, 4 + j, (*chip, c), sibling))
                passed[-1].start()
        for a in range(na):
            copy(a, 0, sibling, me).wait_recv()
            for j, chip in enumerate(chips):
                copy(a, 4 + j, (*chip, 1 - c), me).wait_recv()
        for cp in first + passed:
            cp.wait_send()
        for cp in mine:
            cp.wait()

    return pl.pallas_call(
        body, name=name,
        in_specs=[pl.BlockSpec(memory_space=pltpu.VMEM)] * na, out_specs=[ANY] * na,
        out_shape=[SDS((NB,) + b.shape, dt) for b, dt in zip(blocks, dtypes)],
        scratch_shapes=[pltpu.VMEM(b.shape, dt) for b, dt in zip(blocks, dtypes)]
        + [pltpu.SemaphoreType.DMA((7 * na,)), pltpu.SemaphoreType.DMA((7 * na,)), pltpu.SemaphoreType.DMA((na,))],
        compiler_params=_params(40),
    )(*blocks)


HBM = pl.BlockSpec(memory_space=pltpu.HBM)
SEMS = pl.BlockSpec(memory_space=pltpu.SEMAPHORE)
EFFECT = pltpu.SideEffectType.DATAFLOW_SIDE_EFFECTING


def _chip_copies(srcs, lands, send_sems, recv_sems):
    x, y, c = _place()
    return [pltpu.make_async_remote_copy(
        src_ref=srcs[a].at[slot], dst_ref=lands[a].at[slot],
        send_sem=send_sems.at[3 * a + slot], recv_sem=recv_sems.at[3 * a + slot],
        device_id=(px, py, c), device_id_type=MESH)
        for a in range(len(srcs)) for slot, (px, py) in enumerate(_other_chips(x, y))]


def _split_start(name, copies, per_array, srcs, lands, after=None):
    na = len(srcs)

    def body(*refs):
        send_sems, recv_sems = refs[-2 * na - 3], refs[-2 * na - 2]
        for cp in copies(refs[:na], refs[na:2 * na], send_sems, recv_sems):
            cp.start()
        refs[-1][...] = jnp.zeros_like(refs[-1])

    hbm = lambda a: pltpu.HBM(a.shape, a.dtype)
    pin = lambda a: pltpu.with_memory_space_constraint(a, pltpu.HBM)
    out = pl.pallas_call(
        body, name=name,
        out_shape=(pltpu.SemaphoreType.DMA((per_array * na,)), pltpu.SemaphoreType.DMA((per_array * na,)),
                   *[hbm(a) for a in srcs], *[hbm(a) for a in lands], SDS((8, BD), F32)),
        in_specs=[HBM] * (2 * na) + ([] if after is None else [ANY]),
        out_specs=(SEMS, SEMS, *[HBM] * (2 * na), pl.BlockSpec(memory_space=pltpu.VMEM)),
        input_output_aliases={i: 2 + i for i in range(2 * na)},
        compiler_params=pltpu.CompilerParams(has_side_effects=EFFECT),
    )(*[pin(a) for a in srcs], *[pin(a) for a in lands], *([] if after is None else [after]))
    return out[0], out[1], out[2:2 + na], out[2 + na:2 + 2 * na], out[-1]


def _split_wait(name, copies, started, after):
    send_sems, recv_sems, srcs, lands, _ = started
    na = len(srcs)

    def body(*refs):
        waits = copies(refs[:na], refs[na:2 * na], refs[2 * na], refs[2 * na + 1])
        for cp in waits:
            cp.wait_send()
        for cp in waits:
            cp.wait_recv()

    hbm = lambda a: pltpu.HBM(a.shape, a.dtype)
    out = pl.pallas_call(
        body, name=name,
        out_shape=(*[hbm(a) for a in srcs], *[hbm(a) for a in lands]),
        in_specs=[HBM] * (2 * na) + [SEMS, SEMS, ANY],
        out_specs=tuple([HBM] * (2 * na)),
        input_output_aliases={i: i for i in range(2 * na)},
        compiler_params=pltpu.CompilerParams(has_side_effects=EFFECT),
    )(*srcs, *lands, send_sems, recv_sems, after)
    return out[na:]


def _add_chips(own, b_in):
    r, cols = own.shape
    tr = _row_tile(r)

    def body(p_ref, b0_ref, b1_ref, b2_ref, o_ref):
        o_ref[...] = ((p_ref[...] + b0_ref[0].astype(F32)) + b1_ref[0].astype(F32)) + b2_ref[0].astype(F32)

    slot = lambda k: pl.BlockSpec((1, tr, cols), lambda i: (k, i, 0))
    spec = pl.BlockSpec((tr, cols), lambda i: (i, 0))
    return pl.pallas_call(
        body, name="add_chips", grid=(r // tr,), in_specs=[spec, slot(0), slot(1), slot(2)], out_specs=spec,
        out_shape=SDS((r, cols), F32), compiler_params=_params(32),
    )(own, b_in, b_in, b_in)


VEC_NAMES = ("b_merge", "conv_b", "rg_bx", "rg_ba", "rg_lambda", "hg_lb_logits", "hg_norm_g", "final_norm_g")
REP_NAMES = ("rg_wx", "rg_wa", "norm_g") + VEC_NAMES
SMALL_AT = 3 * BD
SMALL_ROWS = 48
MID_ROWS = 448


def _sum_blocks(parts):
    def body(p_ref, o_ref):
        acc = p_ref[0]
        for k in range(1, NB):
            acc = acc + p_ref[k]
        o_ref[...] = acc

    return pl.pallas_call(body, name="sum_blocks", out_shape=SDS(parts.shape[1:], F32))(parts)


def _pack_rows(arrays, width, row_multiple=8):
    flat = jnp.concatenate([a.reshape(-1) for a in arrays])
    rows = -(-flat.shape[0] // width)
    rows = -(-rows // row_multiple) * row_multiple
    return jnp.pad(flat, (0, rows * width - flat.shape[0])).reshape(rows, width)


def _unpack(flat, like):
    out, off = [], 0
    for a in like:
        out.append(flat[off:off + a.size].reshape(a.shape))
        off += a.size
    return out


def kernel(x, w_in, b_merge, conv_w, conv_b, rg_wx, rg_bx, rg_wa, rg_ba, rg_lambda, hg_lb_logits, hg_norm_g, proj_a, proj_b, w_out, norm_g, final_norm_g, loss_target, m_w_in, m_b_merge, m_conv_w, m_conv_b, m_rg_wx, m_rg_bx, m_rg_wa, m_rg_ba, m_rg_lambda, m_hg_lb_logits, m_hg_norm_g, m_proj_a, m_proj_b, m_w_out, m_norm_g, m_final_norm_g, v_w_in, v_b_merge, v_conv_w, v_conv_b, v_rg_wx, v_rg_bx, v_rg_wa, v_rg_ba, v_rg_lambda, v_hg_lb_logits, v_hg_norm_g, v_proj_a, v_proj_b, v_w_out, v_norm_g, v_final_norm_g):
    weights = dict(w_in=w_in, b_merge=b_merge, conv_w=conv_w, conv_b=conv_b, rg_wx=rg_wx, rg_bx=rg_bx, rg_wa=rg_wa,
                   rg_ba=rg_ba, rg_lambda=rg_lambda, hg_lb_logits=hg_lb_logits, hg_norm_g=hg_norm_g, proj_a=proj_a,
                   proj_b=proj_b, w_out=w_out, norm_g=norm_g, final_norm_g=final_norm_g)
    mom1 = dict(w_in=m_w_in, b_merge=m_b_merge, conv_w=m_conv_w, conv_b=m_conv_b, rg_wx=m_rg_wx, rg_bx=m_rg_bx,
                rg_wa=m_rg_wa, rg_ba=m_rg_ba, rg_lambda=m_rg_lambda, hg_lb_logits=m_hg_lb_logits,
                hg_norm_g=m_hg_norm_g, proj_a=m_proj_a, proj_b=m_proj_b, w_out=m_w_out, norm_g=m_norm_g,
                final_norm_g=m_final_norm_g)
    mom2 = dict(w_in=v_w_in, b_merge=v_b_merge, conv_w=v_conv_w, conv_b=v_conv_b, rg_wx=v_rg_wx, rg_bx=v_rg_bx,
                rg_wa=v_rg_wa, rg_ba=v_rg_ba, rg_lambda=v_rg_lambda, hg_lb_logits=v_hg_lb_logits,
                hg_norm_g=v_hg_norm_g, proj_a=v_proj_a, proj_b=v_proj_b, w_out=v_w_out, norm_g=v_norm_g,
                final_norm_g=v_final_norm_g)
    order = list(weights)
    nb, s_len, _ = x.shape
    n = nb * s_len
    px, py, pc = _place()

    in_hbm = lambda a: pltpu.with_memory_space_constraint(a, pltpu.HBM)
    norm_gain = in_hbm(norm_g)

    x2 = x.reshape(n, D)
    cw_blk = jnp.pad(conv_w[0], ((0, 4), (0, 0)))
    order_ids = jnp.stack([_block_id(p) for p in _arrival_order(px, py, pc)]).astype(jnp.int32)
    z, h_all, w_all, pa_all, pb_all, wo_all, cw_all = _gather_inproj(
        order_ids, x2, norm_gain, [w_in[0], proj_a[0], proj_b[0], w_out[0], cw_blk], [BF16, BF16, BF16, BF16, F32])
    pa_full, pb_full, wo_full = (a.reshape(D, D) for a in (pa_all, pb_all, wo_all))
    cw8 = in_hbm(cw_all.transpose(1, 0, 2).reshape(8, D))
    wx_b, wa_b = in_hbm(rg_wx[0].astype(BF16)), in_hbm(rg_wa[0].astype(BF16))
    cb, bx, ba, lam = (in_hbm(a.reshape(1, D)) for a in (conv_b, rg_bx, rg_ba, rg_lambda))
    fin_g, b_mrg = in_hbm(final_norm_g.reshape(1, D)), in_hbm(b_merge)
    lb_lg, hg_g = in_hbm(hg_lb_logits), in_hbm(hg_norm_g)

    hlru, ya = _lru_fwd(z, cw8, cb, wx_b, wa_b, bx, ba, lam, nb, s_len)
    o_all, yb, st_all = _hgrn_fwd(z, lb_lg, hg_g, nb, s_len)

    (dx2, dya, dyb, dzm, loss_acc, g_fin, g_bm, g_mid) = _mid(
        ya, yb, z, b_mrg, x2, loss_target.reshape(n, D), fin_g, pa_full, pb_full, wo_full)
    dzb, g_lg, g_hg = _hgrn_bwd(z, o_all, st_all, dyb, lb_lg, hg_g, nb, s_len)
    dza, g_cw8, g_cb, g_wx, g_wa, g_bx, g_ba, g_lam = _lru_bwd(
        z, hlru, dya, cw8, cb, wx_b, wa_b, bx, ba, lam, nb, s_len)

    part = dict(b_merge=g_bm, conv_b=g_cb, rg_bx=g_bx, rg_ba=g_ba, rg_lambda=g_lam, hg_lb_logits=g_lg,
                hg_norm_g=g_hg, final_norm_g=g_fin)
    vec = _pack_rows([part[k] for k in VEC_NAMES], BD)
    vec = jnp.pad(vec, ((0, 16 * NB - vec.shape[0]), (0, 0))).reshape(NB, 2, D)
    rows8 = lambda a: jnp.pad(a, ((0, 0), (0, 8 - a.shape[1]), (0, 0)))
    small = jnp.concatenate([g_wx.reshape(NB, 16, D), g_wa.reshape(NB, 16, D),
                             rows8(g_cw8.reshape(8, NB, BD).transpose(1, 0, 2).reshape(NB, 1, D)), rows8(vec),
                             jnp.zeros((NB, MID_ROWS - SMALL_AT - SMALL_ROWS, D), F32)], axis=1)
    g_m = lax.dynamic_update_slice(g_mid, small, (0, SMALL_AT, 0))
    w_out_bf, w_own, m_out_bf, m_own, _, _ = _inproj_bwd_w(pc, dza, dzb, dzm, h_all, g_m)
    outgoing = [in_hbm(w_out_bf), in_hbm(m_out_bf)]
    chip_sums = _split_start("rs_chips_start", _chip_copies, 3, outgoing, [lax.empty(a.shape, a.dtype) for a in outgoing])
    grad_x, g_ng = _inproj_bwd_x(dza, dzb, dzm, w_all, x2, dx2, norm_gain, chip_sums[-1])
    from_chips = _split_wait("rs_chips_wait", _chip_copies, chip_sums, grad_x)
    r_w = _add_chips(w_own, from_chips[0])
    r_m = _add_chips(m_own, from_chips[1])
    row = lax.broadcasted_iota(jnp.int32, (8, D), 0)
    mine = jnp.where(row == 0, g_ng, jnp.where(row == 1, loss_acc[0:1, 0:1], 0.0))
    tail = jnp.concatenate([r_m[SMALL_AT:SMALL_AT + SMALL_ROWS], mine], axis=0)
    (tail_all,) = _allgather([tail], [F32], "gather_small_grads")
    summed = _sum_blocks(tail_all[:, SMALL_ROWS:SMALL_ROWS + 8])

    grads = dict(w_in=r_w.reshape(1, D, D),
                 proj_a=r_m[0:BD].reshape(1, BD, D), proj_b=r_m[BD:2 * BD].reshape(1, BD, D),
                 w_out=r_m[2 * BD:3 * BD].reshape(1, BD, D),
                 conv_w=r_m[SMALL_AT + 32].reshape(8, BD)[0:4].reshape(1, 4, BD),
                 rg_wx=tail_all[:, 0:16].reshape(1, NB, BD, BD), rg_wa=tail_all[:, 16:32].reshape(1, NB, BD, BD),
                 norm_g=summed[0:1])
    vec_all = tail_all[:, 40:42].reshape(-1)
    for k, gk in zip(VEC_NAMES, _unpack(vec_all, [weights[k] for k in VEC_NAMES])):
        grads[k] = gk

    delta, new_m, new_v = {}, {}, {}
    flat2 = lambda a: a.reshape(-1, a.shape[-1])
    for k in ("w_in", "proj_a", "proj_b", "w_out"):
        pin = in_hbm if k == "w_in" else (lambda a: a)
        outs = _adamw(*[pin(flat2(t[k])) for t in (weights, grads, mom1, mom2)])
        delta[k], new_m[k], new_v[k] = (a.reshape(weights[k].shape) for a in outs)
    rep = list(REP_NAMES) + ["conv_w"]
    outs = _adamw_small(*[[in_hbm(flat2(t[k])) for k in rep] for t in (weights, grads, mom1, mom2)])
    for tgt, arrays in zip((delta, new_m, new_v), outs):
        for k, a in zip(rep, arrays):
            tgt[k] = a.reshape(weights[k].shape)

    return (summed[1, 0], grad_x.reshape(x.shape), *[grads[k] for k in order], *[delta[k] for k in order],
            *[new_m[k] for k in order], *[new_v[k] for k in order])
```

```python
import functools

import jax
import jax.numpy as jnp
from jax import lax
from jax.experimental import pallas as pl
from jax.experimental.pallas import tpu as pltpu

F32 = jnp.float32
BF16 = jnp.bfloat16
SDS = jax.ShapeDtypeStruct
MESH = pl.DeviceIdType.MESH
ANY = pl.BlockSpec(memory_space=pl.ANY)

D = 1024
NB = 8
BD = D // NB
CHUNK = 64
EPS = 1e-6
LRU_C = 8.0
HG_SCALE = BD ** -0.5
ADAM_LR, ADAM_B1, ADAM_B2, ADAM_EPS, ADAM_WD, ADAM_STEP = 0.001, 0.9, 0.999, 1e-08, 0.01, 10

NT_DIMS = (((1,), (1,)), ((), ()))
TN_DIMS = (((0,), (0,)), ((), ()))


def _params(vmem_mib):
    return pltpu.CompilerParams(vmem_limit_bytes=vmem_mib << 20)


def _row_tile(rows, most=256):
    assert rows % 8 == 0
    return max(t for t in range(8, min(rows, most) + 1, 8) if rows % t == 0)


def _sigmoid(v):
    return 0.5 * (jnp.tanh(0.5 * v) + 1.0)


def _groups(v):
    return v.reshape(v.shape[0] // 8, 8, v.shape[1])


def _softplus_neg(lam):
    t = -lam
    e = jnp.exp(-jnp.abs(t))
    w = 1.0 + e
    d = w - 1.0
    l1p = jnp.where(d == 0.0, e, jnp.log(w) * (e / jnp.where(d == 0.0, 1.0, d)))
    return jnp.maximum(t, 0.0) + l1p


def _place():
    return lax.axis_index("x"), lax.axis_index("y"), lax.axis_index("c")


def _other_chips(x, y):
    return [(1 - x, y), (x, 1 - y), (1 - x, 1 - y)]


def _block_id(p):
    return 4 * p[0] + 2 * p[1] + p[2]


def _core_chips(x, y, c):
    near, far, diag = _other_chips(x, y)
    pick = lambda a, b: (jnp.where(c == 0, a[0], b[0]), jnp.where(c == 0, a[1], b[1]))
    return [pick(near, far), pick(far, near), diag]


def _arrival_order(x, y, c):
    first, second, diag = _core_chips(x, y, c)
    return [(x, y, c), (x, y, 1 - c), (*first, c), (*second, 1 - c), (*second, c), (*first, 1 - c),
            (*diag, c), (*diag, 1 - c)]


def _gather_inproj(order_ids, x2, norm_g, blocks, dtypes):
    na = len(blocks)
    n = x2.shape[0]
    tm = min(n, 1024)
    ni = n // tm

    def body(order_ref, x_ref, g_ref, *refs):
        ins, (z_ref, h_ref), outs = refs[:na], refs[na:na + 2], refs[na + 2:2 * na + 2]
        stages = refs[2 * na + 2:3 * na + 2]
        h_full, wbuf, send_sems, recv_sems, local_sems, wsems, hsem = refs[3 * na + 2:]
        j, i = pl.program_id(0), pl.program_id(1)
        x, y, c = _place()
        me, sibling = (x, y, c), (x, y, 1 - c)
        chips = _core_chips(x, y, c)
        sibling_chips = [chips[1], chips[0], chips[2]]
        small = range(1, na)

        def copy(a, k, block, to, src=None):
            return pltpu.make_async_remote_copy(
                src_ref=outs[a].at[_block_id(block)] if src is None else src, dst_ref=outs[a].at[_block_id(block)],
                send_sem=send_sems.at[7 * a + k], recv_sem=recv_sems.at[7 * a + k],
                device_id=to, device_id_type=MESH)

        def local(a):
            return pltpu.make_async_copy(stages[a], outs[a].at[_block_id(me)], local_sems.at[a])

        def landed(a, slot):
            copy(a, 1 + slot, (*chips[slot], c), me).wait_recv()
            copy(a, 4 + slot, (*chips[slot], c), sibling).start()
            if slot == 0:
                copy(a, 3, (*chips[0], c), (*chips[1], c)).start()

        def diagonal_and_small():
            landed(0, 2)
            for a in small:
                landed(a, 0)
                landed(a, 1)

        def passed_on(a, slot):
            copy(a, 4 + slot, (*sibling_chips[slot], 1 - c), me).wait_recv()

        def sibling_here_send_second():
            copy(0, 0, sibling, me).wait_recv()
            for a in range(na):
                copy(a, 2, me, (*chips[1], c), src=stages[a]).start()

        @pl.when((j == 0) & (i == 0))
        def _():
            for a in range(na):
                stages[a][...] = ins[a][...].astype(dtypes[a])
                local(a).start()
            for a in range(na):
                copy(a, 0, me, sibling, src=stages[a]).start()
                copy(a, 1, me, (*chips[0], c), src=stages[a]).start()

        @pl.when(j == 0)
        def _():
            xv = x_ref[...]
            r = lax.rsqrt(jnp.mean(xv * xv, axis=-1, keepdims=True) + EPS)
            hb = ((xv * r) * g_ref[...]).astype(BF16)
            h_full[pl.ds(pl.multiple_of(i * tm, tm), tm), :] = hb

        save_h = pltpu.make_async_copy(h_full, h_ref, hsem)
        pl.when((j == 0) & (i == ni - 1))(save_h.start)

        steps = [
            lambda: local(0).wait(),
            sibling_here_send_second,
            lambda: landed(0, 0),
            lambda: passed_on(0, 0),
            lambda: landed(0, 1),
            lambda: passed_on(0, 1),
            diagonal_and_small,
            lambda: passed_on(0, 2),
        ]
        def w_load(k):
            return pltpu.make_async_copy(outs[0].at[order_ref[k]], wbuf.at[k % 2], wsems.at[k % 2])

        for k, step in enumerate(steps):
            @pl.when((j == 0) & (i == 0) if k == 0 else (j == k - 1) & (i == ni - 1))
            def _(k=k, step=step):
                step()
                w_load(k).start()

        pl.when(i == 0)(lambda: w_load(j).wait())
        z_ref[0] = jnp.dot(h_full[pl.ds(pl.multiple_of(i * tm, tm), tm), :], wbuf[j % 2], preferred_element_type=F32)

        @pl.when((j == NB - 1) & (i == ni - 1))
        def _():
            save_h.wait()
            for a in small:
                landed(a, 2)
            for a in small:
                local(a).wait()
                copy(a, 0, sibling, me).wait_recv()
                for slot in range(3):
                    passed_on(a, slot)
            for a in range(na):
                copy(a, 0, me, sibling, src=stages[a]).wait_send()
                for slot, chip in enumerate(chips):
                    copy(a, 1 + slot, me, (*chip, c), src=stages[a]).wait_send()
                    copy(a, 4 + slot, (*chip, c), sibling).wait_send()

    rows_once = lambda j, i, order: (jnp.where(j == 0, i, ni - 1), 0)
    vmem = pl.BlockSpec(memory_space=pltpu.VMEM)
    return pl.pallas_call(
        body, name="gather_inproj",
        grid_spec=pltpu.PrefetchScalarGridSpec(
            num_scalar_prefetch=1, grid=(NB, ni),
            in_specs=[pl.BlockSpec((tm, D), rows_once), pl.BlockSpec((1, D), lambda j, i, order: (0, 0))] + [vmem] * na,
            out_specs=[pl.BlockSpec((1, tm, D), lambda j, i, order: (order[j], i, 0)), ANY] + [ANY] * na,
            scratch_shapes=[pltpu.VMEM(b.shape, dt) for b, dt in zip(blocks, dtypes)]
            + [pltpu.VMEM((n, D), BF16), pltpu.VMEM((2, D, D), BF16),
               pltpu.SemaphoreType.DMA((7 * na,)), pltpu.SemaphoreType.DMA((7 * na,)),
               pltpu.SemaphoreType.DMA((na,)), pltpu.SemaphoreType.DMA((2,)), pltpu.SemaphoreType.DMA(())]),
        out_shape=[SDS((NB, n, D), F32), SDS((n, D), BF16)] + [SDS((NB,) + b.shape, dt) for b, dt in zip(blocks, dtypes)],
        compiler_params=_params(56),
    )(order_ids, x2, norm_g, *blocks)


LRU_T = 256


def _shifted(groups, shifts):
    row = lax.broadcasted_iota(jnp.int32, (groups.shape[0] - 1,) + groups.shape[1:], 1)
    out = []
    for s in shifts:
        y = pltpu.roll(groups, s % 8, 1)
        moved = jnp.where(row >= s, y[1:], y[:-1]) if s > 0 else jnp.where(row < 8 + s, y[:-1], y[1:])
        out.append(moved.reshape(-1, groups.shape[2]))
    return out


def _conv(taps, cw, cb):
    acc = taps[0] * cw[0:1, :] + taps[1] * cw[1:2, :]
    acc = acc + taps[2] * cw[2:3, :]
    acc = acc + taps[3] * cw[3:4, :]
    return cb + acc


def _lru_gates(xa, wx_ref, wa_ref, bx, ba, lam):
    xab = xa.astype(BF16)
    pis, prs = [], []
    for h in range(NB):
        xs = xab[:, h * BD:(h + 1) * BD]
        pis.append(jnp.dot(xs, wx_ref[h], preferred_element_type=F32))
        prs.append(jnp.dot(xs, wa_ref[h], preferred_element_type=F32))
    gi = _sigmoid(jnp.concatenate(pis, axis=1) + bx)
    gr = _sigmoid(jnp.concatenate(prs, axis=1) + ba)
    sp = _softplus_neg(lam)
    log_a = (-LRU_C * gr) * sp
    a = jnp.exp(log_a)
    mult = jnp.sqrt(-jnp.tanh(log_a) * (a * a + 1.0))
    return xab, gi, gr, sp, a, mult


def _lru_fwd(z, cw8, cb, wx, wa, bx, ba, lam, nb, s_len):
    n = nb * s_len
    t = LRU_T
    ns = s_len // t

    def body(xp_ref, ga_ref, cw_ref, cb_ref, wx_ref, wa_ref, bx_ref, ba_ref, lam_ref,
             h_ref, ya_ref, ext, a_s, u_s, carry):
        @pl.when(pl.program_id(1) == 0)
        def _():
            ext[0:8, :] = jnp.zeros((8, D), F32)
            carry[...] = jnp.zeros((8, D), F32)

        xp = xp_ref[0]
        ext[8:8 + t, :] = xp
        xa = _conv(_shifted(_groups(ext[...]), (3, 2, 1)) + [xp], cw_ref[...], cb_ref[...])
        ext[0:8, :] = xp[t - 8:t, :]
        _, gi, _, _, a, mult = _lru_gates(xa, wx_ref, wa_ref, bx_ref[...], ba_ref[...], lam_ref[...])
        u = (mult * gi) * xa
        a, u = _groups(a), _groups(u)
        row = lax.broadcasted_iota(jnp.int32, a.shape, 1)
        for sh in (1, 2, 4):
            a_sh = pltpu.roll(a, sh, 1)
            u_sh = pltpu.roll(u, sh, 1)
            m = row >= sh
            u = jnp.where(m, a * u_sh + u, u)
            a = jnp.where(m, a * a_sh, a)
        a_s[...] = a.reshape(t, D)
        u_s[...] = u.reshape(t, D)

        def step(g, c):
            r = pl.multiple_of(g * 8, 8)
            hg = u_s[pl.ds(r, 8), :] + a_s[pl.ds(r, 8), :] * c
            h_ref[pl.ds(r, 8), :] = hg
            return hg[7:8, :]

        c_out = lax.fori_loop(0, t // 8, step, carry[0:1, :], unroll=4)
        carry[0:1, :] = c_out
        ga = ga_ref[0]
        ya_ref[...] = (h_ref[...] * (ga * _sigmoid(ga))).astype(BF16)

    row_map = lambda b, s: (b * ns + s, 0)
    rep2 = lambda b, s: (0, 0)
    rep3 = lambda b, s: (0, 0, 0)
    return pl.pallas_call(
        body, name="lru_fwd", grid=(nb, ns),
        in_specs=[pl.BlockSpec((1, t, D), lambda b, s: (0, b * ns + s, 0)),
                  pl.BlockSpec((1, t, D), lambda b, s: (1, b * ns + s, 0)),
                  pl.BlockSpec((8, D), rep2), pl.BlockSpec((1, D), rep2),
                  pl.BlockSpec((NB, BD, BD), rep3), pl.BlockSpec((NB, BD, BD), rep3),
                  pl.BlockSpec((1, D), rep2), pl.BlockSpec((1, D), rep2), pl.BlockSpec((1, D), rep2)],
        out_specs=[pl.BlockSpec((t, D), row_map), pl.BlockSpec((t, D), row_map)],
        out_shape=[SDS((n, D), F32), SDS((n, D), BF16)],
        scratch_shapes=[pltpu.VMEM((t + 8, D), F32), pltpu.VMEM((t, D), F32), pltpu.VMEM((t, D), F32),
                        pltpu.VMEM((8, D), F32)],
        compiler_params=_params(48),
    )(z, z, cw8, cb, wx, wa, bx, ba, lam)


def _lru_bwd(z, h_all, dya, cw8, cb, wx, wa, bx, ba, lam, nb, s_len):
    n = nb * s_len
    t = LRU_T
    ns = s_len // t
    t8 = t // 8

    def body(xp_ref, xph_ref, ga_ref, h_ref, hh_ref, dya_ref, cw_ref, cb_ref, wx_ref, wa_ref, bx_ref, ba_ref,
             lam_ref, dz_ref, gcw_ref, gcb_ref, gwx_ref, gwa_ref, gbx_ref, gba_ref, glam_ref,
             ext, hext, dext, a_s, u_s, dh_s, carry):
        b, s = pl.program_id(0), pl.program_id(1)
        first_tile = s == ns - 1

        @pl.when((b == 0) & (s == 0))
        def _():
            for ref in (gcw_ref, gcb_ref, gwx_ref, gwa_ref, gbx_ref, gba_ref, glam_ref):
                ref[...] = jnp.zeros(ref.shape, F32)

        @pl.when(s == 0)
        def _():
            dext[t:t + 8, :] = jnp.zeros((8, D), F32)
            carry[...] = jnp.zeros((8, D), F32)

        keep = jnp.where(first_tile, 0.0, 1.0)
        xp = xp_ref[0]
        ext[0:8, :] = xph_ref[0] * keep
        ext[8:8 + t, :] = xp
        hext[0:8, :] = hh_ref[...] * keep
        hext[8:8 + t, :] = h_ref[...]
        cw = cw_ref[...]
        lam = lam_ref[...]
        taps = _shifted(_groups(ext[...]), (3, 2, 1)) + [xp]
        xa = _conv(taps, cw, cb_ref[...])
        xab, gi, gr, sp, a, mult = _lru_gates(xa, wx_ref, wa_ref, bx_ref[...], ba_ref[...], lam)
        (h_prev,) = _shifted(_groups(hext[...]), (1,))
        ga = ga_ref[0]
        sg = _sigmoid(ga)
        dya_v = dya_ref[...]
        d_ga = dya_v * h_ref[...] * (sg * (1.0 + ga * (1.0 - sg)))
        g_in = dya_v * (ga * sg)

        (an,) = _shifted(jnp.concatenate([_groups(a), jnp.ones((1, 8, D), F32)], axis=0), (-1,))
        an, u = _groups(an), _groups(g_in)
        row = lax.broadcasted_iota(jnp.int32, an.shape, 1)
        for sh in (1, 2, 4):
            a_sh = pltpu.roll(an, 8 - sh, 1)
            u_sh = pltpu.roll(u, 8 - sh, 1)
            m = row < 8 - sh
            u = jnp.where(m, u + an * u_sh, u)
            an = jnp.where(m, an * a_sh, an)
        a_s[...] = an.reshape(t, D)
        u_s[...] = u.reshape(t, D)

        def step(i, c):
            r = pl.multiple_of((t8 - 1 - i) * 8, 8)
            dg = u_s[pl.ds(r, 8), :] + a_s[pl.ds(r, 8), :] * c
            dh_s[pl.ds(r, 8), :] = dg
            return dg[0:1, :]

        lax.fori_loop(0, t8, step, carry[0:1, :], unroll=4)
        dh = dh_s[...]
        carry[0:1, :] = a[0:1, :] * dh[0:1, :]

        d_a = dh * h_prev
        dux = dh * xa
        d_mult = dux * gi
        d_gi = dux * mult
        d_xa = dh * (mult * gi)
        d_loga = d_a * a - d_mult * ((a * a) / mult)
        d_gr = d_loga * (-LRU_C * sp)
        d_sp = jnp.sum(d_loga * (-LRU_C * gr), axis=0, keepdims=True)
        glam_ref[...] += d_sp * (-_sigmoid(-lam))
        d_pi = d_gi * gi * (1.0 - gi)
        d_pr = d_gr * gr * (1.0 - gr)
        gbx_ref[...] += jnp.sum(d_pi, axis=0, keepdims=True)
        gba_ref[...] += jnp.sum(d_pr, axis=0, keepdims=True)
        dpib = d_pi.astype(BF16)
        dprb = d_pr.astype(BF16)
        back = []
        for h in range(NB):
            cs = slice(h * BD, (h + 1) * BD)
            gwx_ref[h] += lax.dot_general(xab[:, cs], dpib[:, cs], TN_DIMS, preferred_element_type=F32)
            gwa_ref[h] += lax.dot_general(xab[:, cs], dprb[:, cs], TN_DIMS, preferred_element_type=F32)
            back.append(lax.dot_general(dpib[:, cs], wx_ref[h], NT_DIMS, preferred_element_type=F32)
                        + lax.dot_general(dprb[:, cs], wa_ref[h], NT_DIMS, preferred_element_type=F32))
        d_xa = d_xa + jnp.concatenate(back, axis=1)

        dext[0:t, :] = d_xa
        later = _shifted(_groups(dext[...]), (-3, -2, -1))
        d_xp = later[0] * cw[0:1, :] + later[1] * cw[1:2, :]
        d_xp = d_xp + later[2] * cw[2:3, :]
        d_xp = d_xp + d_xa * cw[3:4, :]
        dext[t:t + 8, :] = d_xa[0:8, :]
        gcb_ref[...] += jnp.sum(d_xa, axis=0, keepdims=True)
        for k in range(4):
            gcw_ref[k:k + 1, :] += jnp.sum(d_xa * taps[k], axis=0, keepdims=True)
        dz_ref[0] = d_xp.astype(BF16)
        dz_ref[1] = d_ga.astype(BF16)

    rb = lambda b, s: b * ns + (ns - 1 - s)
    halo = lambda b, s: jnp.maximum(rb(b, s) * t8 - 1, 0)
    rep2 = lambda b, s: (0, 0)
    rep3 = lambda b, s: (0, 0, 0)
    return pl.pallas_call(
        body, name="lru_bwd", grid=(nb, ns),
        in_specs=[pl.BlockSpec((1, t, D), lambda b, s: (0, rb(b, s), 0)),
                  pl.BlockSpec((1, 8, D), lambda b, s: (0, halo(b, s), 0)),
                  pl.BlockSpec((1, t, D), lambda b, s: (1, rb(b, s), 0)),
                  pl.BlockSpec((t, D), lambda b, s: (rb(b, s), 0)),
                  pl.BlockSpec((8, D), lambda b, s: (halo(b, s), 0)),
                  pl.BlockSpec((t, D), lambda b, s: (rb(b, s), 0)),
                  pl.BlockSpec((8, D), rep2), pl.BlockSpec((1, D), rep2),
                  pl.BlockSpec((NB, BD, BD), rep3), pl.BlockSpec((NB, BD, BD), rep3),
                  pl.BlockSpec((1, D), rep2), pl.BlockSpec((1, D), rep2), pl.BlockSpec((1, D), rep2)],
        out_specs=[pl.BlockSpec((2, t, D), lambda b, s: (0, rb(b, s), 0)),
                   pl.BlockSpec((8, D), rep2), pl.BlockSpec((1, D), rep2),
                   pl.BlockSpec((NB, BD, BD), rep3), pl.BlockSpec((NB, BD, BD), rep3),
                   pl.BlockSpec((1, D), rep2), pl.BlockSpec((1, D), rep2), pl.BlockSpec((1, D), rep2)],
        out_shape=[SDS((2, n, D), BF16), SDS((8, D), F32), SDS((1, D), F32),
                   SDS((NB, BD, BD), F32), SDS((NB, BD, BD), F32),
                   SDS((1, D), F32), SDS((1, D), F32), SDS((1, D), F32)],
        scratch_shapes=[pltpu.VMEM((t + 8, D), F32), pltpu.VMEM((t + 8, D), F32), pltpu.VMEM((t + 8, D), F32),
                        pltpu.VMEM((t, D), F32), pltpu.VMEM((t, D), F32), pltpu.VMEM((t, D), F32),
                        pltpu.VMEM((8, D), F32)],
        compiler_params=_params(56),
    )(z, z, z, h_all, h_all, dya, cw8, cb, wx, wa, bx, ba, lam)


HG_T = 512
HG_NC = HG_T // CHUNK
BNT_DIMS = (((2,), (2,)), ((0,), (0,)))
BNN_DIMS = (((2,), (1,)), ((0,), (0,)))
BTN_DIMS = (((1,), (1,)), ((0,), (0,)))


def _lower_bound(lg):
    m = jnp.max(lg, axis=0, keepdims=True)
    e = jnp.exp(lg - m)
    return e[0:1, :] / jnp.sum(e, axis=0, keepdims=True)


def _tri(upper):
    r = lax.broadcasted_iota(jnp.int32, (HG_NC, CHUNK, CHUNK), 1)
    c = lax.broadcasted_iota(jnp.int32, (HG_NC, CHUNK, CHUNK), 2)
    return (c >= r) if upper else (r >= c)


def _bdot(a, b, dims):
    return lax.dot_general(a, b, dims, preferred_element_type=F32)


def _tri_sums(upper, a):
    tri = _tri(upper).astype(BF16)
    a1 = a.astype(BF16)
    r1 = a - a1.astype(F32)
    a2 = r1.astype(BF16)
    a3 = (r1 - a2.astype(F32)).astype(BF16)
    return _bdot(tri, a1, BNN_DIMS) + (_bdot(tri, a2, BNN_DIMS) + _bdot(tri, a3, BNN_DIMS))


def _chunks(a):
    return a.reshape(HG_NC, CHUNK, BD)


def _hg_tile(q, fp, lb):
    q, fp = _chunks(q), _chunks(fp)
    sig = _sigmoid(fp)
    f = lb + (1.0 - lb) * sig
    log_f = jnp.log(f)
    k = 1.0 - f
    b = _tri_sums(False, log_f)
    b_mid = b[:, CHUNK // 2:CHUNK // 2 + 1, :]
    b_last = b[:, CHUNK - 1:CHUNK, :]
    sq = _sigmoid(q)
    qh = q * sq
    e_qi = jnp.exp(b - b_mid)
    e_ki = jnp.exp(b_mid - b)
    e_qs = jnp.exp(b)
    e_ks = jnp.exp(b_last - b)
    dc = jnp.exp(b_last)
    q_in = (qh * e_qi) * HG_SCALE
    k_in = k * e_ki
    q_st = (qh * e_qs) * HG_SCALE
    k_st = k * e_ks
    att = _bdot(q_in.astype(BF16), k_in.astype(BF16), BNT_DIMS)
    att = jnp.where(_tri(False), att, 0.0)
    return dict(q=q, sig=sig, f=f, k=k, sq=sq, e_qi=e_qi, e_ki=e_ki, e_qs=e_qs, e_ks=e_ks, dc=dc,
                q_in=q_in, k_in=k_in, q_st=q_st, k_st=k_st, att=att)


def _hgrn_fwd(z, lb_logits, hg_g, nb, s_len):
    n = nb * s_len
    t = HG_T
    ns = s_len // t
    nchunk = s_len // CHUNK

    def body(q_ref, f_ref, v_ref, gb_ref, lg_ref, g_ref, o_ref, yb_ref, st_ref, st):
        @pl.when(pl.program_id(1) == 0)
        def _():
            st[...] = jnp.zeros((NB, BD, BD), F32)

        def head(h, carry):
            cols = pl.ds(pl.multiple_of(h * BD, BD), BD)
            lb = _lower_bound(lg_ref[:, cols])
            ck = _hg_tile(q_ref[0, :, cols], f_ref[0, :, cols], lb)
            vb = _chunks(v_ref[0, :, cols]).astype(BF16)
            kv = _bdot(vb, ck["k_st"].astype(BF16), BTN_DIMS)
            states = [st[h]]
            for c in range(HG_NC):
                states.append(states[c] * ck["dc"][c] + kv[c])
            st[h] = states[HG_NC]
            s_in = jnp.stack(states[:HG_NC], axis=0)
            st_ref[h] = s_in
            o = (_bdot(ck["att"].astype(BF16), vb, BNN_DIMS)
                 + _bdot(ck["q_st"].astype(BF16), s_in.astype(BF16), BNT_DIMS))
            o_ref[:, cols] = o.reshape(t, BD)
            r = lax.rsqrt(jnp.mean(o * o, axis=-1, keepdims=True) + EPS)
            gb = _chunks(gb_ref[0, :, cols])
            yb_ref[:, cols] = (((o * r) * g_ref[...]) * (gb * _sigmoid(gb))).astype(BF16).reshape(t, BD)
            return carry

        lax.fori_loop(0, NB, head, 0, unroll=4)

    seg = lambda j: pl.BlockSpec((1, t, D), lambda b, s: (j, b * ns + s, 0))
    tile = pl.BlockSpec((t, D), lambda b, s: (b * ns + s, 0))
    return pl.pallas_call(
        body, name="hgrn_fwd", grid=(nb, ns),
        in_specs=[seg(2), seg(3), seg(4), seg(5),
                  pl.BlockSpec((2, D), lambda b, s: (0, 0)), pl.BlockSpec((1, BD), lambda b, s: (0, 0))],
        out_specs=[tile, tile, pl.BlockSpec((NB, HG_NC, BD, BD), lambda b, s: (b, s, 0, 0))],
        out_shape=[SDS((n, D), F32), SDS((n, D), BF16), SDS((nb * NB, nchunk, BD, BD), F32)],
        scratch_shapes=[pltpu.VMEM((NB, BD, BD), F32)],
        compiler_params=_params(56),
    )(z, z, z, z, lb_logits, hg_g)


def _hgrn_bwd(z, o_all, st_all, dyb, lb_logits, hg_g, nb, s_len):
    n = nb * s_len
    t = HG_T
    ns = s_len // t

    def body(q_ref, f_ref, v_ref, gb_ref, o_ref, st_ref, dyb_ref, lg_ref, g_ref,
             dz_ref, glg_ref, ghg_ref, dst, dlb):
        b, s = pl.program_id(0), pl.program_id(1)

        @pl.when((b == 0) & (s == 0))
        def _():
            ghg_ref[...] = jnp.zeros((1, BD), F32)
            dlb[...] = jnp.zeros((8, D), F32)

        @pl.when(s == 0)
        def _():
            dst[...] = jnp.zeros((NB, BD, BD), F32)

        g = g_ref[...]

        def head(h, carry):
            cols = pl.ds(pl.multiple_of(h * BD, BD), BD)
            lb = _lower_bound(lg_ref[:, cols])
            ck = _hg_tile(q_ref[0, :, cols], f_ref[0, :, cols], lb)
            q = ck["q"]
            vb = _chunks(v_ref[0, :, cols]).astype(BF16)
            gb = _chunks(gb_ref[0, :, cols])
            o = _chunks(o_ref[:, cols])
            dyb_v = _chunks(dyb_ref[:, cols])
            s_in = st_ref[h]

            sgb = _sigmoid(gb)
            r = lax.rsqrt(jnp.mean(o * o, axis=-1, keepdims=True) + EPS)
            ohat = o * r
            d_on = dyb_v * (gb * sgb)
            d_gb = dyb_v * (ohat * g) * (sgb * (1.0 + gb * (1.0 - sgb)))
            ghg_ref[...] += jnp.sum(jnp.sum(d_on * ohat, axis=1), axis=0, keepdims=True)
            tt = d_on * g
            d_o = r * (tt - ohat * jnp.mean(tt * ohat, axis=-1, keepdims=True))
            dob = d_o.astype(BF16)

            attb = ck["att"].astype(BF16)
            q_inb, k_inb = ck["q_in"].astype(BF16), ck["k_in"].astype(BF16)
            q_stb, k_stb = ck["q_st"].astype(BF16), ck["k_st"].astype(BF16)
            d_att = jnp.where(_tri(False), _bdot(dob, vb, BNT_DIMS), 0.0).astype(BF16)
            d_q_in = _bdot(d_att, k_inb, BNN_DIMS)
            d_k_in = _bdot(d_att, q_inb, BTN_DIMS)
            d_q_st = _bdot(dob, s_in.astype(BF16), BNN_DIMS)
            qdo = _bdot(dob, q_stb, BTN_DIMS)
            d_states = [None] * HG_NC + [dst[h]]
            for c in reversed(range(HG_NC)):
                d_states[c] = d_states[c + 1] * ck["dc"][c] + qdo[c]
            dst[h] = d_states[0]
            ds_out = jnp.stack(d_states[1:], axis=0)
            dsb = ds_out.astype(BF16)
            d_v = _bdot(attb, dob, BTN_DIMS) + _bdot(k_stb, dsb, BNT_DIMS)
            d_k_st = _bdot(vb, dsb, BNN_DIMS)
            d_dc = jnp.sum(ds_out * s_in, axis=1, keepdims=True)

            p_qi = d_q_in * ck["q_in"]
            p_ki = d_k_in * ck["k_in"]
            p_qs = d_q_st * ck["q_st"]
            p_ks = d_k_st * ck["k_st"]
            d_qh = (d_q_in * ck["e_qi"] + d_q_st * ck["e_qs"]) * HG_SCALE
            d_k = d_k_in * ck["e_ki"] + d_k_st * ck["e_ks"]
            d_b = (p_qi - p_ki) + (p_qs - p_ks)
            d_b_mid = jnp.sum(p_ki - p_qi, axis=1, keepdims=True)
            d_b_last = jnp.sum(p_ks, axis=1, keepdims=True) + d_dc * ck["dc"]
            rowi = lax.broadcasted_iota(jnp.int32, (HG_NC, CHUNK, BD), 1)
            d_b = d_b + jnp.where(rowi == CHUNK // 2, d_b_mid, 0.0) + jnp.where(rowi == CHUNK - 1, d_b_last, 0.0)
            d_logf = _tri_sums(True, d_b)
            d_f = d_logf / ck["f"] - d_k
            sig, sq = ck["sig"], ck["sq"]
            d_fp = d_f * (1.0 - lb) * (sig * (1.0 - sig))
            dlb[0:1, cols] += jnp.sum(jnp.sum(d_f * (1.0 - sig), axis=1), axis=0, keepdims=True)
            d_q = d_qh * (sq * (1.0 + q * (1.0 - sq)))
            dz_ref[0, :, cols] = d_q.astype(BF16).reshape(t, BD)
            dz_ref[1, :, cols] = d_fp.astype(BF16).reshape(t, BD)
            dz_ref[2, :, cols] = d_v.astype(BF16).reshape(t, BD)
            dz_ref[3, :, cols] = d_gb.astype(BF16).reshape(t, BD)
            return carry

        lax.fori_loop(0, NB, head, 0, unroll=2)

        @pl.when((b == nb - 1) & (s == ns - 1))
        def _():
            lb = _lower_bound(lg_ref[...])
            dl = dlb[0:1, :] * (lb * (1.0 - lb))
            glg_ref[0:1, :] = dl
            glg_ref[1:2, :] = -dl

    rb = lambda b, s: b * ns + (ns - 1 - s)
    seg = lambda j: pl.BlockSpec((1, t, D), lambda b, s: (j, rb(b, s), 0))
    tile = pl.BlockSpec((t, D), lambda b, s: (rb(b, s), 0))
    return pl.pallas_call(
        body, name="hgrn_bwd", grid=(nb, ns),
        in_specs=[seg(2), seg(3), seg(4), seg(5), tile,
                  pl.BlockSpec((NB, HG_NC, BD, BD), lambda b, s: (b, ns - 1 - s, 0, 0)),
                  tile, pl.BlockSpec((2, D), lambda b, s: (0, 0)), pl.BlockSpec((1, BD), lambda b, s: (0, 0))],
        out_specs=[pl.BlockSpec((4, t, D), lambda b, s: (0, rb(b, s), 0)),
                   pl.BlockSpec((2, D), lambda b, s: (0, 0)), pl.BlockSpec((1, BD), lambda b, s: (0, 0))],
        out_shape=[SDS((4, n, D), BF16), SDS((2, D), F32), SDS((1, BD), F32)],
        scratch_shapes=[pltpu.VMEM((NB, BD, BD), F32), pltpu.VMEM((8, D), F32)],
        compiler_params=_params(60),
    )(z, z, z, z, o_all, st_all, dyb, lb_logits, hg_g)


def _mid(ya, yb, z, b_merge, x2, tgt, fin_g, pa, pb, wo):
    n = x2.shape[0]
    tm = 256
    ni = n // tm

    def body(ya_ref, yb_ref, gma_ref, gmb_ref, bm_ref, x_ref, t_ref, fg_ref, pa_hbm, pb_hbm, wo_hbm,
             dx2_ref, dya_ref, dyb_ref, dgm_ref, loss_ref, gfg_ref, gbm_ref, gm_hbm,
             pa_v, pb_v, wo_v, gpa_v, gpb_v, gwo_v, sem):
        i = pl.program_id(0)
        by_owner = lambda g: g.reshape(NB, BD, D)
        loads = [pltpu.make_async_copy(src, dst, sem.at[k])
                 for k, (src, dst) in enumerate(((pa_hbm, pa_v), (pb_hbm, pb_v), (wo_hbm, wo_v)))]
        stores = [pltpu.make_async_copy(src, dst, sem.at[k])
                  for k, (src, dst) in enumerate((g, gm_hbm.at[:, pl.ds(slot * BD, BD), :])
                                                 for slot, g in enumerate((gpa_v, gpb_v, gwo_v)))]

        @pl.when(i == 0)
        def _():
            for cp in loads:
                cp.start()
            for ref in (gpa_v, gpb_v, gwo_v, loss_ref, gfg_ref, gbm_ref):
                ref[...] = jnp.zeros(ref.shape, F32)
            for cp in loads:
                cp.wait()

        ya_v = ya_ref[...]
        yb_v = yb_ref[...]
        out_a = jnp.dot(ya_v, pa_v[...], preferred_element_type=F32)
        out_b = jnp.dot(yb_v, pb_v[...], preferred_element_type=F32)
        bm = bm_ref[...]
        g_a = _sigmoid(gma_ref[0] + bm[:, 0:D])
        g_b = _sigmoid(gmb_ref[0] + bm[:, D:2 * D])
        mixed = g_a * out_a + g_b * out_b
        mixb = mixed.astype(BF16)
        xo = x_ref[...] + jnp.dot(mixb, wo_v[...], preferred_element_type=F32)
        r = lax.rsqrt(jnp.mean(xo * xo, axis=-1, keepdims=True) + EPS)
        xn = xo * r
        fg = fg_ref[...]
        e = xn * fg - t_ref[...]
        loss_ref[...] += 0.5 * jnp.sum(jnp.mean(e * e, axis=-1, keepdims=True))
        dy = e * (1.0 / D)
        gfg_ref[...] += jnp.sum(dy * xn, axis=0, keepdims=True)
        dxn = dy * fg
        dx2 = r * (dxn - xn * jnp.mean(dxn * xn, axis=-1, keepdims=True))
        dx2_ref[...] = dx2
        dx2b = dx2.astype(BF16)
        d_mixed = lax.dot_general(dx2b, wo_v[...], NT_DIMS, preferred_element_type=F32)
        gwo_v[...] += by_owner(lax.dot_general(mixb, dx2b, TN_DIMS, preferred_element_type=F32))
        d_oa = (d_mixed * g_a).astype(BF16)
        d_ob = (d_mixed * g_b).astype(BF16)
        dgm_a = (d_mixed * out_a) * (g_a * (1.0 - g_a))
        dgm_b = (d_mixed * out_b) * (g_b * (1.0 - g_b))
        gbm_ref[:, 0:D] += jnp.sum(dgm_a, axis=0, keepdims=True)
        gbm_ref[:, D:2 * D] += jnp.sum(dgm_b, axis=0, keepdims=True)
        dgm_ref[0] = dgm_a.astype(BF16)
        dgm_ref[1] = dgm_b.astype(BF16)
        dya_ref[...] = lax.dot_general(d_oa, pa_v[...], NT_DIMS, preferred_element_type=F32)
        dyb_ref[...] = lax.dot_general(d_ob, pb_v[...], NT_DIMS, preferred_element_type=F32)
        gpa_v[...] += by_owner(lax.dot_general(ya_v, d_oa, TN_DIMS, preferred_element_type=F32))
        gpb_v[...] += by_owner(lax.dot_general(yb_v, d_ob, TN_DIMS, preferred_element_type=F32))

        @pl.when(i == ni - 1)
        def _():
            for cp in stores:
                cp.start()
            for cp in stores:
                cp.wait()

    rows = pl.BlockSpec((tm, D), lambda i: (i, 0))
    rep = lambda shape: pl.BlockSpec(shape, lambda i: (0,) * len(shape))
    return pl.pallas_call(
        body, name="mid", grid=(ni,),
        in_specs=[rows, rows,
                  pl.BlockSpec((1, tm, D), lambda i: (6, i, 0)), pl.BlockSpec((1, tm, D), lambda i: (7, i, 0)),
                  rep((1, 2 * D)), rows, rows, rep((1, D)), ANY, ANY, ANY],
        out_specs=[rows, rows, rows, pl.BlockSpec((2, tm, D), lambda i: (0, i, 0)),
                   rep((8, BD)), rep((1, D)), rep((1, 2 * D)), ANY],
        out_shape=[SDS((n, D), F32), SDS((n, D), F32), SDS((n, D), F32), SDS((2, n, D), BF16),
                   SDS((8, BD), F32), SDS((1, D), F32), SDS((1, 2 * D), F32),
                   SDS((NB, MID_ROWS, D), F32)],
        scratch_shapes=[pltpu.VMEM((D, D), BF16)] * 3 + [pltpu.VMEM((NB, BD, D), F32)] * 3 + [pltpu.SemaphoreType.DMA((3,))],
        compiler_params=_params(60),
    )(ya, yb, z, z, b_merge, x2, tgt, fin_g, pa, pb, wo)


def _dz_specs(tm, ni, row_major):
    if row_major:
        ia = lambda i, j: (jnp.minimum(j, 1), i, 0)
        ib = lambda i, j: (jnp.clip(j - 2, 0, 3), i, 0)
        im = lambda i, j: (jnp.clip(j - 6, 0, 1), i, 0)
    else:
        last = ni - 1
        ia = lambda j, i: (jnp.minimum(j, 1), jnp.where(j < 2, i, last), 0)
        ib = lambda j, i: (jnp.clip(j - 2, 0, 3), jnp.where(j < 2, 0, jnp.where(j < 6, i, last)), 0)
        im = lambda j, i: (jnp.clip(j - 6, 0, 1), jnp.where(j < 6, 0, i), 0)
    return [pl.BlockSpec((1, tm, D), f) for f in (ia, ib, im)]


def _inproj_bwd_x(dza, dzb, dzm, w_all, x2, dx2, norm_g, after):
    n = x2.shape[0]
    tm = 512
    ni = n // tm

    def body(dza_ref, dzb_ref, dzm_ref, w_ref, x_ref, dx2_ref, g_ref, after_ref, gx_ref, gg_ref, acc):
        i, j = pl.program_id(0), pl.program_id(1)

        @pl.when((i == 0) & (j == 0))
        def _():
            gg_ref[...] = jnp.zeros((1, D), F32)

        @pl.when(j == 0)
        def _():
            acc[...] = jnp.zeros((tm, D), F32)

        def add(ref):
            acc[...] += lax.dot_general(ref[0], w_ref[0], NT_DIMS, preferred_element_type=F32)

        pl.when(j < 2)(lambda: add(dza_ref))
        pl.when((j >= 2) & (j < 6))(lambda: add(dzb_ref))
        pl.when(j >= 6)(lambda: add(dzm_ref))

        @pl.when(j == NB - 1)
        def _():
            x = x_ref[...]
            r = lax.rsqrt(jnp.mean(x * x, axis=-1, keepdims=True) + EPS)
            xn = x * r
            dh = acc[...]
            gg_ref[...] += jnp.sum(dh * xn, axis=0, keepdims=True)
            dxn = dh * g_ref[...]
            gx_ref[...] = dx2_ref[...] + r * (dxn - xn * jnp.mean(dxn * xn, axis=-1, keepdims=True))

    rows = pl.BlockSpec((tm, D), lambda i, j: (i, 0))
    return pl.pallas_call(
        body, name="inproj_bwd_x", grid=(ni, NB),
        in_specs=_dz_specs(tm, ni, True) + [pl.BlockSpec((1, D, D), lambda i, j: (j, 0, 0)), rows, rows,
                                             pl.BlockSpec((1, D), lambda i, j: (0, 0)), ANY],
        out_specs=[rows, pl.BlockSpec((1, D), lambda i, j: (0, 0))],
        out_shape=[SDS((n, D), F32), SDS((1, D), F32)],
        scratch_shapes=[pltpu.VMEM((tm, D), F32)],
        compiler_params=_params(48),
    )(dza, dzb, dzm, w_all, x2, dx2, norm_g, after)


def _walk_tables(order, ni):
    rows = []
    for lo, hi in ((0, 2), (2, 6), (6, 8)):
        active = [j for j, g in enumerate(order) if lo <= g < hi]
        block, tile = [], []
        for j, g in enumerate(order):
            before = [a for a in active if a < j]
            if lo <= g < hi:
                block.append(g - lo), tile.append(-1)
            elif before:
                block.append(order[before[-1]] - lo), tile.append(ni - 1)
            else:
                block.append(order[active[0]] - lo), tile.append(0)
        rows += [block, tile]
    return rows


def _inproj_bwd_w(core, dza, dzb, dzm, h_all, g_m):
    n = h_all.shape[0]
    tm = min(n, 2048)
    ni = n // tm
    packed = g_m.shape[1:]
    orders = [[2 * q + 1 - c for q in range(4)] + [2 * q + c for q in range(4)] for c in (0, 1)]
    tables = jnp.asarray([[order] + _walk_tables(order, ni) for order in orders], jnp.int32)
    walk = jnp.where(core == 0, tables[0], tables[1])

    def body(walk_ref, dza_ref, dzb_ref, dzm_ref, h_ref, gm_hbm, out_bf, own_f32, m_out_bf, m_own_f32, got_w, got_m,
             acc, stage, theirs, m_mine, m_theirs, m_stage, send_sems, recv_sems, local_sems):
        j, i = pl.program_id(0), pl.program_id(1)
        group = walk_ref[0, j]
        x, y, c = _place()
        sibling = (x, y, 1 - c)

        def send_w(q):
            return pltpu.make_async_remote_copy(
                src_ref=stage.at[q % 2], dst_ref=got_w.at[q], send_sem=send_sems.at[q], recv_sem=recv_sems.at[q],
                device_id=sibling, device_id_type=MESH)

        def send_m(q):
            return pltpu.make_async_remote_copy(
                src_ref=gm_hbm.at[2 * q + (1 - c)], dst_ref=got_m.at[q], send_sem=send_sems.at[4 + q],
                recv_sem=recv_sems.at[4 + q], device_id=sibling, device_id_type=MESH)

        def fetch(q):
            return pltpu.make_async_copy(got_w.at[q], theirs, local_sems.at[0])

        def fetch_m(q):
            return (pltpu.make_async_copy(gm_hbm.at[2 * q + c], m_mine, local_sems.at[2]),
                    pltpu.make_async_copy(got_m.at[q], m_theirs, local_sems.at[3]))

        @pl.when((j == 0) & (i == 0))
        def _():
            for q in range(4):
                send_m(q).start()

        @pl.when(i == 0)
        def _():
            acc[...] = jnp.zeros((D, D), F32)

        def add(ref):
            acc[...] += lax.dot_general(h_ref[...], ref[0], TN_DIMS, preferred_element_type=F32)

        pl.when(group < 2)(lambda: add(dza_ref))
        pl.when((group >= 2) & (group < 6))(lambda: add(dzb_ref))
        pl.when(group >= 6)(lambda: add(dzm_ref))

        for q in range(4):
            @pl.when((i == ni - 1) & (j == q))
            def _(q=q):
                if q >= 2:
                    send_w(q - 2).wait_send()
                stage[q % 2] = acc[...].astype(BF16)
                send_w(q).start()

        def reducer(q):
            other_x, other_y = x != q // 2, y != q % 2
            return other_x | other_y, jnp.where(other_x & other_y, 2, jnp.where(other_x, 0, 1))

        def w_out(q):
            return pltpu.make_async_copy(stage.at[0], out_bf.at[reducer(q)[1]], local_sems.at[1])

        for q in range(4):
            @pl.when((j == 4 + q) & (i == 0))
            def _(q=q):
                send_w(q).wait_recv()
                send_m(q).wait_recv()
                fetch(q).start()
                for cp in fetch_m(q):
                    cp.start()
                if q > 0:
                    pl.when(reducer(q - 1)[0])(lambda: w_out(q - 1).wait())

            @pl.when((j == 4 + q) & (i == ni - 1))
            def _(q=q):
                if q == 0:
                    send_w(2).wait_send()
                    send_w(3).wait_send()
                other, slot = reducer(q)
                m_out = pltpu.make_async_copy(m_stage, m_out_bf.at[slot], local_sems.at[4])
                m_own = pltpu.make_async_copy(m_mine, m_own_f32, local_sems.at[4])

                for cp in fetch_m(q):
                    cp.wait()

                @pl.when(other)
                def _():
                    m_stage[...] = (m_mine[...] + m_theirs[...]).astype(BF16)
                    m_out.start()

                @pl.when(jnp.logical_not(other))
                def _():
                    m_mine[...] += m_theirs[...]
                    m_own.start()

                fetch(q).wait()

                @pl.when(other)
                def _():
                    stage[0] = (acc[...] + theirs[...].astype(F32)).astype(BF16)
                    w_out(q).start()
                    if q == 3:
                        w_out(q).wait()
                    m_out.wait()

                @pl.when(jnp.logical_not(other))
                def _():
                    acc[...] += theirs[...].astype(F32)
                    out = pltpu.make_async_copy(acc, own_f32, local_sems.at[1])
                    out.start()
                    out.wait()
                    m_own.wait()

        @pl.when((j == NB - 1) & (i == ni - 1))
        def _():
            for q in range(4):
                send_m(q).wait_send()

    def dz_spec(k):
        return pl.BlockSpec((1, tm, D), lambda j, i, w: (w[1 + 2 * k, j], jnp.where(w[2 + 2 * k, j] < 0, i, w[2 + 2 * k, j]), 0))

    return pl.pallas_call(
        body, name="inproj_bwd_w",
        grid_spec=pltpu.PrefetchScalarGridSpec(
            num_scalar_prefetch=1, grid=(NB, ni),
            in_specs=[dz_spec(0), dz_spec(1), dz_spec(2), pl.BlockSpec((tm, D), lambda j, i, w: (i, 0)), ANY],
            out_specs=[ANY] * 6,
            scratch_shapes=[pltpu.VMEM((D, D), F32), pltpu.VMEM((2, D, D), BF16), pltpu.VMEM((D, D), BF16),
                            pltpu.VMEM(packed, F32), pltpu.VMEM(packed, F32), pltpu.VMEM(packed, BF16),
                            pltpu.SemaphoreType.DMA((8,)), pltpu.SemaphoreType.DMA((8,)), pltpu.SemaphoreType.DMA((5,))]),
        out_shape=[SDS((3, D, D), BF16), SDS((D, D), F32), SDS((3,) + packed, BF16), SDS(packed, F32),
                   SDS((4, D, D), BF16), SDS((4,) + packed, F32)],
        compiler_params=_params(58),
    )(walk, dza, dzb, dzm, h_all, g_m)


def _adamw(w, g, m, v):
    rows, cols = w.shape
    tr = _row_tile(rows)

    spec = pl.BlockSpec((tr, cols), lambda i: (i, 0))
    return pl.pallas_call(
        functools.partial(_adam_refs), name="adamw", grid=(rows // tr,), in_specs=[spec] * 4, out_specs=[spec] * 3,
        out_shape=[SDS((rows, cols), F32)] * 3, compiler_params=_params(32),
    )(w, g, m, v)


def _adam_refs(w_ref, g_ref, m_ref, v_ref, d_ref, nm_ref, nv_ref):
    gv = g_ref[...]
    nm = ADAM_B1 * m_ref[...] + (1.0 - ADAM_B1) * gv
    nv = ADAM_B2 * v_ref[...] + (1.0 - ADAM_B2) * (gv * gv)
    m_hat = nm / (1.0 - ADAM_B1 ** ADAM_STEP)
    v_hat = nv / (1.0 - ADAM_B2 ** ADAM_STEP)
    d_ref[...] = -ADAM_LR * (m_hat / (jnp.sqrt(v_hat) + ADAM_EPS) + ADAM_WD * w_ref[...])
    nm_ref[...] = nm
    nv_ref[...] = nv


def _adamw_small(ws, gs, ms, vs):
    k = len(ws)

    def body(*refs):
        ins, outs = refs[:4 * k], refs[4 * k:7 * k]
        vin, vout = refs[7 * k:11 * k], refs[11 * k:14 * k]
        load_sems, store_sems = refs[14 * k:]
        loads = [pltpu.make_async_copy(ins[i], vin[i], load_sems.at[i]) for i in range(4 * k)]
        for cp in loads:
            cp.start()
        for cp in loads:
            cp.wait()
        for i in range(k):
            _adam_refs(*[vin[part * k + i] for part in range(4)], *[vout[part * k + i] for part in range(3)])
        stores = [pltpu.make_async_copy(vout[i], outs[i], store_sems.at[i]) for i in range(3 * k)]
        for cp in stores:
            cp.start()
        for cp in stores:
            cp.wait()

    shapes = [SDS(w.shape, F32) for w in ws]
    vmem = [pltpu.VMEM(w.shape, F32) for w in ws]
    out = pl.pallas_call(
        body, name="adamw_small", out_shape=shapes * 3, in_specs=[HBM] * (4 * k), out_specs=[HBM] * (3 * k),
        scratch_shapes=vmem * 7 + [pltpu.SemaphoreType.DMA((4 * k,)), pltpu.SemaphoreType.DMA((3 * k,))],
        compiler_params=_params(32),
    )(*ws, *gs, *ms, *vs)
    return out[:k], out[k:2 * k], out[2 * k:]


def _allgather(blocks, dtypes, name):
    na = len(blocks)

    def body(*refs):
        ins, outs, stages = refs[:na], refs[na:2 * na], refs[2 * na:3 * na]
        send_sems, recv_sems, local_sems = refs[3 * na:]
        x, y, c = _place()
        me, sibling = (x, y, c), (x, y, 1 - c)
        chips = [(1 - x, y), (x, 1 - y), (1 - x, 1 - y)]
        blk = lambda p: 4 * p[0] + 2 * p[1] + p[2]

        def copy(a, k, block, to, src=None):
            return pltpu.make_async_remote_copy(
                src_ref=outs[a].at[blk(block)] if src is None else src, dst_ref=outs[a].at[blk(block)],
                send_sem=send_sems.at[7 * a + k], recv_sem=recv_sems.at[7 * a + k],
                device_id=to, device_id_type=MESH)

        mine, first, passed = [], [], []
        for a in range(na):
            stages[a][...] = ins[a][...].astype(dtypes[a])
            mine.append(pltpu.make_async_copy(stages[a], outs[a].at[blk(me)], local_sems.at[a]))
            mine[-1].start()
            first.append(copy(a, 0, me, sibling, src=stages[a]))
            first += [copy(a, 1 + j, me, (*chip, c), src=stages[a]) for j, chip in enumerate(chips)]
        for cp in first:
            cp.start()
        for j, chip in enumerate(chips):
            for a in range(na):
                copy(a, 1 + j, (*chip, c), me).wait_recv()
                passed.append(copy(a, 4 + j, (*chip, c), sibling))
                passed[-1].start()
        for a in range(na):
            copy(a, 0, sibling, me).wait_recv()
            for j, chip in enumerate(chips):
                copy(a, 4 + j, (*chip, 1 - c), me).wait_recv()
        for cp in first + passed:
            cp.wait_send()
        for cp in mine:
            cp.wait()

    return pl.pallas_call(
        body, name=name,
        in_specs=[pl.BlockSpec(memory_space=pltpu.VMEM)] * na, out_specs=[ANY] * na,
        out_shape=[SDS((NB,) + b.shape, dt) for b, dt in zip(blocks, dtypes)],
        scratch_shapes=[pltpu.VMEM(b.shape, dt) for b, dt in zip(blocks, dtypes)]
        + [pltpu.SemaphoreType.DMA((7 * na,)), pltpu.SemaphoreType.DMA((7 * na,)), pltpu.SemaphoreType.DMA((na,))],
        compiler_params=_params(40),
    )(*blocks)


HBM = pl.BlockSpec(memory_space=pltpu.HBM)
SEMS = pl.BlockSpec(memory_space=pltpu.SEMAPHORE)
EFFECT = pltpu.SideEffectType.DATAFLOW_SIDE_EFFECTING


def _chip_copies(srcs, lands, send_sems, recv_sems):
    x, y, c = _place()
    return [pltpu.make_async_remote_copy(
        src_ref=srcs[a].at[slot], dst_ref=lands[a].at[slot],
        send_sem=send_sems.at[3 * a + slot], recv_sem=recv_sems.at[3 * a + slot],
        device_id=(px, py, c), device_id_type=MESH)
        for a in range(len(srcs)) for slot, (px, py) in enumerate(_other_chips(x, y))]


def _split_start(name, copies, per_array, srcs, lands, after=None):
    na = len(srcs)

    def body(*refs):
        send_sems, recv_sems = refs[-2 * na - 3], refs[-2 * na - 2]
        for cp in copies(refs[:na], refs[na:2 * na], send_sems, recv_sems):
            cp.start()
        refs[-1][...] = jnp.zeros_like(refs[-1])

    hbm = lambda a: pltpu.HBM(a.shape, a.dtype)
    pin = lambda a: pltpu.with_memory_space_constraint(a, pltpu.HBM)
    out = pl.pallas_call(
        body, name=name,
        out_shape=(pltpu.SemaphoreType.DMA((per_array * na,)), pltpu.SemaphoreType.DMA((per_array * na,)),
                   *[hbm(a) for a in srcs], *[hbm(a) for a in lands], SDS((8, BD), F32)),
        in_specs=[HBM] * (2 * na) + ([] if after is None else [ANY]),
        out_specs=(SEMS, SEMS, *[HBM] * (2 * na), pl.BlockSpec(memory_space=pltpu.VMEM)),
        input_output_aliases={i: 2 + i for i in range(2 * na)},
        compiler_params=pltpu.CompilerParams(has_side_effects=EFFECT),
    )(*[pin(a) for a in srcs], *[pin(a) for a in lands], *([] if after is None else [after]))
    return out[0], out[1], out[2:2 + na], out[2 + na:2 + 2 * na], out[-1]


def _split_wait(name, copies, started, after):
    send_sems, recv_sems, srcs, lands, _ = started
    na = len(srcs)

    def body(*refs):
        waits = copies(refs[:na], refs[na:2 * na], refs[2 * na], refs[2 * na + 1])
        for cp in waits:
            cp.wait_send()
        for cp in waits:
            cp.wait_recv()

    hbm = lambda a: pltpu.HBM(a.shape, a.dtype)
    out = pl.pallas_call(
        body, name=name,
        out_shape=(*[hbm(a) for a in srcs], *[hbm(a) for a in lands]),
        in_specs=[HBM] * (2 * na) + [SEMS, SEMS, ANY],
        out_specs=tuple([HBM] * (2 * na)),
        input_output_aliases={i: i for i in range(2 * na)},
        compiler_params=pltpu.CompilerParams(has_side_effects=EFFECT),
    )(*srcs, *lands, send_sems, recv_sems, after)
    return out[na:]


def _add_chips(own, b_in):
    r, cols = own.shape
    tr = _row_tile(r)

    def body(p_ref, b0_ref, b1_ref, b2_ref, o_ref):
        o_ref[...] = ((p_ref[...] + b0_ref[0].astype(F32)) + b1_ref[0].astype(F32)) + b2_ref[0].astype(F32)

    slot = lambda k: pl.BlockSpec((1, tr, cols), lambda i: (k, i, 0))
    spec = pl.BlockSpec((tr, cols), lambda i: (i, 0))
    return pl.pallas_call(
        body, name="add_chips", grid=(r // tr,), in_specs=[spec, slot(0), slot(1), slot(2)], out_specs=spec,
        out_shape=SDS((r, cols), F32), compiler_params=_params(32),
    )(own, b_in, b_in, b_in)


VEC_NAMES = ("b_merge", "conv_b", "rg_bx", "rg_ba", "rg_lambda", "hg_lb_logits", "hg_norm_g", "final_norm_g")
REP_NAMES = ("rg_wx", "rg_wa", "norm_g") + VEC_NAMES
SMALL_AT = 3 * BD
SMALL_ROWS = 48
MID_ROWS = 448


def _sum_blocks(parts):
    def body(p_ref, o_ref):
        acc = p_ref[0]
        for k in range(1, NB):
            acc = acc + p_ref[k]
        o_ref[...] = acc

    return pl.pallas_call(body, name="sum_blocks", out_shape=SDS(parts.shape[1:], F32))(parts)


def _pack_rows(arrays, width, row_multiple=8):
    flat = jnp.concatenate([a.reshape(-1) for a in arrays])
    rows = -(-flat.shape[0] // width)
    rows = -(-rows // row_multiple) * row_multiple
    return jnp.pad(flat, (0, rows * width - flat.shape[0])).reshape(rows, width)


def _unpack(flat, like):
    out, off = [], 0
    for a in like:
        out.append(flat[off:off + a.size].reshape(a.shape))
        off += a.size
    return out


def kernel(x, w_in, b_merge, conv_w, conv_b, rg_wx, rg_bx, rg_wa, rg_ba, rg_lambda, hg_lb_logits, hg_norm_g, proj_a, proj_b, w_out, norm_g, final_norm_g, loss_target, m_w_in, m_b_merge, m_conv_w, m_conv_b, m_rg_wx, m_rg_bx, m_rg_wa, m_rg_ba, m_rg_lambda, m_hg_lb_logits, m_hg_norm_g, m_proj_a, m_proj_b, m_w_out, m_norm_g, m_final_norm_g, v_w_in, v_b_merge, v_conv_w, v_conv_b, v_rg_wx, v_rg_bx, v_rg_wa, v_rg_ba, v_rg_lambda, v_hg_lb_logits, v_hg_norm_g, v_proj_a, v_proj_b, v_w_out, v_norm_g, v_final_norm_g):
    weights = dict(w_in=w_in, b_merge=b_merge, conv_w=conv_w, conv_b=conv_b, rg_wx=rg_wx, rg_bx=rg_bx, rg_wa=rg_wa,
                   rg_ba=rg_ba, rg_lambda=rg_lambda, hg_lb_logits=hg_lb_logits, hg_norm_g=hg_norm_g, proj_a=proj_a,
                   proj_b=proj_b, w_out=w_out, norm_g=norm_g, final_norm_g=final_norm_g)
    mom1 = dict(w_in=m_w_in, b_merge=m_b_merge, conv_w=m_conv_w, conv_b=m_conv_b, rg_wx=m_rg_wx, rg_bx=m_rg_bx,
                rg_wa=m_rg_wa, rg_ba=m_rg_ba, rg_lambda=m_rg_lambda, hg_lb_logits=m_hg_lb_logits,
                hg_norm_g=m_hg_norm_g, proj_a=m_proj_a, proj_b=m_proj_b, w_out=m_w_out, norm_g=m_norm_g,
                final_norm_g=m_final_norm_g)
    mom2 = dict(w_in=v_w_in, b_merge=v_b_merge, conv_w=v_conv_w, conv_b=v_conv_b, rg_wx=v_rg_wx, rg_bx=v_rg_bx,
                rg_wa=v_rg_wa, rg_ba=v_rg_ba, rg_lambda=v_rg_lambda, hg_lb_logits=v_hg_lb_logits,
                hg_norm_g=v_hg_norm_g, proj_a=v_proj_a, proj_b=v_proj_b, w_out=v_w_out, norm_g=v_norm_g,
                final_norm_g=v_final_norm_g)
    order = list(weights)
    nb, s_len, _ = x.shape
    n = nb * s_len
    px, py, pc = _place()

    in_hbm = lambda a: pltpu.with_memory_space_constraint(a, pltpu.HBM)
    norm_gain = in_hbm(norm_g)

    x2 = x.reshape(n, D)
    cw_blk = jnp.pad(conv_w[0], ((0, 4), (0, 0)))
    order_ids = jnp.stack([_block_id(p) for p in _arrival_order(px, py, pc)]).astype(jnp.int32)
    z, h_all, w_all, pa_all, pb_all, wo_all, cw_all = _gather_inproj(
        order_ids, x2, norm_gain, [w_in[0], proj_a[0], proj_b[0], w_out[0], cw_blk], [BF16, BF16, BF16, BF16, F32])
    pa_full, pb_full, wo_full = (a.reshape(D, D) for a in (pa_all, pb_all, wo_all))
    cw8 = in_hbm(cw_all.transpose(1, 0, 2).reshape(8, D))
    wx_b, wa_b = in_hbm(rg_wx[0].astype(BF16)), in_hbm(rg_wa[0].astype(BF16))
    cb, bx, ba, lam = (in_hbm(a.reshape(1, D)) for a in (conv_b, rg_bx, rg_ba, rg_lambda))
    fin_g, b_mrg = in_hbm(final_norm_g.reshape(1, D)), in_hbm(b_merge)
    lb_lg, hg_g = in_hbm(hg_lb_logits), in_hbm(hg_norm_g)

    hlru, ya = _lru_fwd(z, cw8, cb, wx_b, wa_b, bx, ba, lam, nb, s_len)
    o_all, yb, st_all = _hgrn_fwd(z, lb_lg, hg_g, nb, s_len)

    (dx2, dya, dyb, dzm, loss_acc, g_fin, g_bm, g_mid) = _mid(
        ya, yb, z, b_mrg, x2, loss_target.reshape(n, D), fin_g, pa_full, pb_full, wo_full)
    dzb, g_lg, g_hg = _hgrn_bwd(z, o_all, st_all, dyb, lb_lg, hg_g, nb, s_len)
    dza, g_cw8, g_cb, g_wx, g_wa, g_bx, g_ba, g_lam = _lru_bwd(
        z, hlru, dya, cw8, cb, wx_b, wa_b, bx, ba, lam, nb, s_len)

    part = dict(b_merge=g_bm, conv_b=g_cb, rg_bx=g_bx, rg_ba=g_ba, rg_lambda=g_lam, hg_lb_logits=g_lg,
                hg_norm_g=g_hg, final_norm_g=g_fin)
    vec = _pack_rows([part[k] for k in VEC_NAMES], BD)
    vec = jnp.pad(vec, ((0, 16 * NB - vec.shape[0]), (0, 0))).reshape(NB, 2, D)
    rows8 = lambda a: jnp.pad(a, ((0, 0), (0, 8 - a.shape[1]), (0, 0)))
    small = jnp.concatenate([g_wx.reshape(NB, 16, D), g_wa.reshape(NB, 16, D),
                             rows8(g_cw8.reshape(8, NB, BD).transpose(1, 0, 2).reshape(NB, 1, D)), rows8(vec),
                             jnp.zeros((NB, MID_ROWS - SMALL_AT - SMALL_ROWS, D), F32)], axis=1)
    g_m = lax.dynamic_update_slice(g_mid, small, (0, SMALL_AT, 0))
    w_out_bf, w_own, m_out_bf, m_own, _, _ = _inproj_bwd_w(pc, dza, dzb, dzm, h_all, g_m)
    outgoing = [in_hbm(w_out_bf), in_hbm(m_out_bf)]
    chip_sums = _split_start("rs_chips_start", _chip_copies, 3, outgoing, [lax.empty(a.shape, a.dtype) for a in outgoing])
    grad_x, g_ng = _inproj_bwd_x(dza, dzb, dzm, w_all, x2, dx2, norm_gain, chip_sums[-1])
    from_chips = _split_wait("rs_chips_wait", _chip_copies, chip_sums, grad_x)
    r_w = _add_chips(w_own, in_hbm(from_chips[0]))
    r_m = _add_chips(m_own, in_hbm(from_chips[1]))
    row = lax.broadcasted_iota(jnp.int32, (8, D), 0)
    mine = jnp.where(row == 0, g_ng, jnp.where(row == 1, loss_acc[0:1, 0:1], 0.0))
    tail = jnp.concatenate([r_m[SMALL_AT:SMALL_AT + SMALL_ROWS], mine], axis=0)
    (tail_all,) = _allgather([tail], [F32], "gather_small_grads")
    summed = _sum_blocks(tail_all[:, SMALL_ROWS:SMALL_ROWS + 8])

    grads = dict(w_in=r_w.reshape(1, D, D),
                 proj_a=r_m[0:BD].reshape(1, BD, D), proj_b=r_m[BD:2 * BD].reshape(1, BD, D),
                 w_out=r_m[2 * BD:3 * BD].reshape(1, BD, D),
                 conv_w=r_m[SMALL_AT + 32].reshape(8, BD)[0:4].reshape(1, 4, BD),
                 rg_wx=tail_all[:, 0:16].reshape(1, NB, BD, BD), rg_wa=tail_all[:, 16:32].reshape(1, NB, BD, BD),
                 norm_g=summed[0:1])
    vec_all = tail_all[:, 40:42].reshape(-1)
    for k, gk in zip(VEC_NAMES, _unpack(vec_all, [weights[k] for k in VEC_NAMES])):
        grads[k] = gk

    delta, new_m, new_v = {}, {}, {}
    flat2 = lambda a: a.reshape(-1, a.shape[-1])
    for k in ("w_in", "proj_a", "proj_b", "w_out"):
        pin = in_hbm if k == "w_in" else (lambda a: a)
        outs = _adamw(*[pin(flat2(t[k])) for t in (weights, grads, mom1, mom2)])
        delta[k], new_m[k], new_v[k] = (a.reshape(weights[k].shape) for a in outs)
    rep = list(REP_NAMES) + ["conv_w"]
    outs = _adamw_small(*[[in_hbm(flat2(t[k])) for k in rep] for t in (weights, grads, mom1, mom2)])
    for tgt, arrays in zip((delta, new_m, new_v), outs):
        for k, a in zip(rep, arrays):
            tgt[k] = a.reshape(weights[k].shape)

    return (summed[1, 0], grad_x.reshape(x.shape), *[grads[k] for k in order], *[delta[k] for k in order],
            *[new_m[k] for k in order], *[new_v[k] for k in order])
```

```python
import functools

import jax
import jax.numpy as jnp
from jax import lax
from jax.experimental import pallas as pl
from jax.experimental.pallas import tpu as pltpu

F32 = jnp.float32
BF16 = jnp.bfloat16
SDS = jax.ShapeDtypeStruct
MESH = pl.DeviceIdType.MESH
ANY = pl.BlockSpec(memory_space=pl.ANY)

D = 1024
NB = 8
BD = D // NB
CHUNK = 64
EPS = 1e-6
LRU_C = 8.0
HG_SCALE = BD ** -0.5
ADAM_LR, ADAM_B1, ADAM_B2, ADAM_EPS, ADAM_WD, ADAM_STEP = 0.001, 0.9, 0.999, 1e-08, 0.01, 10

NT_DIMS = (((1,), (1,)), ((), ()))
TN_DIMS = (((0,), (0,)), ((), ()))


def _params(vmem_mib):
    return pltpu.CompilerParams(vmem_limit_bytes=vmem_mib << 20)


def _row_tile(rows, most=256):
    assert rows % 8 == 0
    return max(t for t in range(8, min(rows, most) + 1, 8) if rows % t == 0)


def _sigmoid(v):
    return 0.5 * (jnp.tanh(0.5 * v) + 1.0)


def _groups(v):
    return v.reshape(v.shape[0] // 8, 8, v.shape[1])


def _softplus_neg(lam):
    t = -lam
    e = jnp.exp(-jnp.abs(t))
    w = 1.0 + e
    d = w - 1.0
    l1p = jnp.where(d == 0.0, e, jnp.log(w) * (e / jnp.where(d == 0.0, 1.0, d)))
    return jnp.maximum(t, 0.0) + l1p


def _place():
    return lax.axis_index("x"), lax.axis_index("y"), lax.axis_index("c")


def _other_chips(x, y):
    return [(1 - x, y), (x, 1 - y), (1 - x, 1 - y)]


def _block_id(p):
    return 4 * p[0] + 2 * p[1] + p[2]


def _core_chips(x, y, c):
    near, far, diag = _other_chips(x, y)
    pick = lambda a, b: (jnp.where(c == 0, a[0], b[0]), jnp.where(c == 0, a[1], b[1]))
    return [pick(near, far), pick(far, near), diag]


def _arrival_order(x, y, c):
    first, second, diag = _core_chips(x, y, c)
    return [(x, y, c), (x, y, 1 - c), (*first, c), (*second, 1 - c), (*second, c), (*first, 1 - c),
            (*diag, c), (*diag, 1 - c)]


def _gather_inproj(order_ids, x2, norm_g, blocks, dtypes):
    na = len(blocks)
    n = x2.shape[0]
    tm = min(n, 1024)
    ni = n // tm

    def body(order_ref, x_ref, g_ref, *refs):
        ins, (z_ref, h_ref), outs = refs[:na], refs[na:na + 2], refs[na + 2:2 * na + 2]
        stages = refs[2 * na + 2:3 * na + 2]
        h_full, wbuf, send_sems, recv_sems, local_sems, wsems, hsem = refs[3 * na + 2:]
        j, i = pl.program_id(0), pl.program_id(1)
        x, y, c = _place()
        me, sibling = (x, y, c), (x, y, 1 - c)
        chips = _core_chips(x, y, c)
        sibling_chips = [chips[1], chips[0], chips[2]]
        small = range(1, na)

        def copy(a, k, block, to, src=None):
            return pltpu.make_async_remote_copy(
                src_ref=outs[a].at[_block_id(block)] if src is None else src, dst_ref=outs[a].at[_block_id(block)],
                send_sem=send_sems.at[7 * a + k], recv_sem=recv_sems.at[7 * a + k],
                device_id=to, device_id_type=MESH)

        def local(a):
            return pltpu.make_async_copy(stages[a], outs[a].at[_block_id(me)], local_sems.at[a])

        def landed(a, slot):
            copy(a, 1 + slot, (*chips[slot], c), me).wait_recv()
            copy(a, 4 + slot, (*chips[slot], c), sibling).start()
            if slot == 0:
                copy(a, 3, (*chips[0], c), (*chips[1], c)).start()

        def diagonal_and_small():
            landed(0, 2)
            for a in small:
                landed(a, 0)
                landed(a, 1)

        def passed_on(a, slot):
            copy(a, 4 + slot, (*sibling_chips[slot], 1 - c), me).wait_recv()

        def sibling_here_send_second():
            copy(0, 0, sibling, me).wait_recv()
            for a in range(na):
                copy(a, 2, me, (*chips[1], c), src=stages[a]).start()

        @pl.when((j == 0) & (i == 0))
        def _():
            for a in range(na):
                stages[a][...] = ins[a][...].astype(dtypes[a])
                local(a).start()
            for a in range(na):
                copy(a, 0, me, sibling, src=stages[a]).start()
                copy(a, 1, me, (*chips[0], c), src=stages[a]).start()

        @pl.when(j == 0)
        def _():
            xv = x_ref[...]
            r = lax.rsqrt(jnp.mean(xv * xv, axis=-1, keepdims=True) + EPS)
            hb = ((xv * r) * g_ref[...]).astype(BF16)
            h_full[pl.ds(pl.multiple_of(i * tm, tm), tm), :] = hb

        save_h = pltpu.make_async_copy(h_full, h_ref, hsem)
        pl.when((j == 0) & (i == ni - 1))(save_h.start)

        steps = [
            lambda: local(0).wait(),
            sibling_here_send_second,
            lambda: landed(0, 0),
            lambda: passed_on(0, 0),
            lambda: landed(0, 1),
            lambda: passed_on(0, 1),
            diagonal_and_small,
            lambda: passed_on(0, 2),
        ]
        def w_load(k):
            return pltpu.make_async_copy(outs[0].at[order_ref[k]], wbuf.at[k % 2], wsems.at[k % 2])

        for k, step in enumerate(steps):
            @pl.when((j == 0) & (i == 0) if k == 0 else (j == k - 1) & (i == ni - 1))
            def _(k=k, step=step):
                step()
                w_load(k).start()

        pl.when(i == 0)(lambda: w_load(j).wait())
        z_ref[0] = jnp.dot(h_full[pl.ds(pl.multiple_of(i * tm, tm), tm), :], wbuf[j % 2], preferred_element_type=F32)

        @pl.when((j == NB - 1) & (i == ni - 1))
        def _():
            save_h.wait()
            for a in small:
                landed(a, 2)
            for a in small:
                local(a).wait()
                copy(a, 0, sibling, me).wait_recv()
                for slot in range(3):
                    passed_on(a, slot)
            for a in range(na):
                copy(a, 0, me, sibling, src=stages[a]).wait_send()
                for slot, chip in enumerate(chips):
                    copy(a, 1 + slot, me, (*chip, c), src=stages[a]).wait_send()
                    copy(a, 4 + slot, (*chip, c), sibling).wait_send()

    rows_once = lambda j, i, order: (jnp.where(j == 0, i, ni - 1), 0)
    vmem = pl.BlockSpec(memory_space=pltpu.VMEM)
    return pl.pallas_call(
        body, name="gather_inproj",
        grid_spec=pltpu.PrefetchScalarGridSpec(
            num_scalar_prefetch=1, grid=(NB, ni),
            in_specs=[pl.BlockSpec((tm, D), rows_once), pl.BlockSpec((1, D), lambda j, i, order: (0, 0))] + [vmem] * na,
            out_specs=[pl.BlockSpec((1, tm, D), lambda j, i, order: (order[j], i, 0)), ANY] + [ANY] * na,
            scratch_shapes=[pltpu.VMEM(b.shape, dt) for b, dt in zip(blocks, dtypes)]
            + [pltpu.VMEM((n, D), BF16), pltpu.VMEM((2, D, D), BF16),
               pltpu.SemaphoreType.DMA((7 * na,)), pltpu.SemaphoreType.DMA((7 * na,)),
               pltpu.SemaphoreType.DMA((na,)), pltpu.SemaphoreType.DMA((2,)), pltpu.SemaphoreType.DMA(())]),
        out_shape=[SDS((NB, n, D), F32), SDS((n, D), BF16)] + [SDS((NB,) + b.shape, dt) for b, dt in zip(blocks, dtypes)],
        compiler_params=_params(56),
    )(order_ids, x2, norm_g, *blocks)


LRU_T = 256


def _shifted(groups, shifts):
    row = lax.broadcasted_iota(jnp.int32, (groups.shape[0] - 1,) + groups.shape[1:], 1)
    out = []
    for s in shifts:
        y = pltpu.roll(groups, s % 8, 1)
        moved = jnp.where(row >= s, y[1:], y[:-1]) if s > 0 else jnp.where(row < 8 + s, y[:-1], y[1:])
        out.append(moved.reshape(-1, groups.shape[2]))
    return out


def _conv(taps, cw, cb):
    acc = taps[0] * cw[0:1, :] + taps[1] * cw[1:2, :]
    acc = acc + taps[2] * cw[2:3, :]
    acc = acc + taps[3] * cw[3:4, :]
    return cb + acc


def _lru_gates(xa, wx_ref, wa_ref, bx, ba, lam):
    xab = xa.astype(BF16)
    pis, prs = [], []
    for h in range(NB):
        xs = xab[:, h * BD:(h + 1) * BD]
        pis.append(jnp.dot(xs, wx_ref[h], preferred_element_type=F32))
        prs.append(jnp.dot(xs, wa_ref[h], preferred_element_type=F32))
    gi = _sigmoid(jnp.concatenate(pis, axis=1) + bx)
    gr = _sigmoid(jnp.concatenate(prs, axis=1) + ba)
    sp = _softplus_neg(lam)
    log_a = (-LRU_C * gr) * sp
    a = jnp.exp(log_a)
    mult = jnp.sqrt(-jnp.tanh(log_a) * (a * a + 1.0))
    return xab, gi, gr, sp, a, mult


def _lru_fwd(z, cw8, cb, wx, wa, bx, ba, lam, nb, s_len):
    n = nb * s_len
    t = LRU_T
    ns = s_len // t

    def body(xp_ref, ga_ref, cw_ref, cb_ref, wx_ref, wa_ref, bx_ref, ba_ref, lam_ref,
             h_ref, ya_ref, ext, a_s, u_s, carry):
        @pl.when(pl.program_id(1) == 0)
        def _():
            ext[0:8, :] = jnp.zeros((8, D), F32)
            carry[...] = jnp.zeros((8, D), F32)

        xp = xp_ref[0]
        ext[8:8 + t, :] = xp
        xa = _conv(_shifted(_groups(ext[...]), (3, 2, 1)) + [xp], cw_ref[...], cb_ref[...])
        ext[0:8, :] = xp[t - 8:t, :]
        _, gi, _, _, a, mult = _lru_gates(xa, wx_ref, wa_ref, bx_ref[...], ba_ref[...], lam_ref[...])
        u = (mult * gi) * xa
        a, u = _groups(a), _groups(u)
        row = lax.broadcasted_iota(jnp.int32, a.shape, 1)
        for sh in (1, 2, 4):
            a_sh = pltpu.roll(a, sh, 1)
            u_sh = pltpu.roll(u, sh, 1)
            m = row >= sh
            u = jnp.where(m, a * u_sh + u, u)
            a = jnp.where(m, a * a_sh, a)
        a_s[...] = a.reshape(t, D)
        u_s[...] = u.reshape(t, D)

        def step(g, c):
            r = pl.multiple_of(g * 8, 8)
            hg = u_s[pl.ds(r, 8), :] + a_s[pl.ds(r, 8), :] * c
            h_ref[pl.ds(r, 8), :] = hg
            return hg[7:8, :]

        c_out = lax.fori_loop(0, t // 8, step, carry[0:1, :], unroll=4)
        carry[0:1, :] = c_out
        ga = ga_ref[0]
        ya_ref[...] = (h_ref[...] * (ga * _sigmoid(ga))).astype(BF16)

    row_map = lambda b, s: (b * ns + s, 0)
    rep2 = lambda b, s: (0, 0)
    rep3 = lambda b, s: (0, 0, 0)
    return pl.pallas_call(
        body, name="lru_fwd", grid=(nb, ns),
        in_specs=[pl.BlockSpec((1, t, D), lambda b, s: (0, b * ns + s, 0)),
                  pl.BlockSpec((1, t, D), lambda b, s: (1, b * ns + s, 0)),
                  pl.BlockSpec((8, D), rep2), pl.BlockSpec((1, D), rep2),
                  pl.BlockSpec((NB, BD, BD), rep3), pl.BlockSpec((NB, BD, BD), rep3),
                  pl.BlockSpec((1, D), rep2), pl.BlockSpec((1, D), rep2), pl.BlockSpec((1, D), rep2)],
        out_specs=[pl.BlockSpec((t, D), row_map), pl.BlockSpec((t, D), row_map)],
        out_shape=[SDS((n, D), F32), SDS((n, D), BF16)],
        scratch_shapes=[pltpu.VMEM((t + 8, D), F32), pltpu.VMEM((t, D), F32), pltpu.VMEM((t, D), F32),
                        pltpu.VMEM((8, D), F32)],
        compiler_params=_params(48),
    )(z, z, cw8, cb, wx, wa, bx, ba, lam)


def _lru_bwd(z, h_all, dya, cw8, cb, wx, wa, bx, ba, lam, nb, s_len):
    n = nb * s_len
    t = LRU_T
    ns = s_len // t
    t8 = t // 8

    def body(xp_ref, xph_ref, ga_ref, h_ref, hh_ref, dya_ref, cw_ref, cb_ref, wx_ref, wa_ref, bx_ref, ba_ref,
             lam_ref, dz_ref, gcw_ref, gcb_ref, gwx_ref, gwa_ref, gbx_ref, gba_ref, glam_ref,
             ext, hext, dext, a_s, u_s, dh_s, carry):
        b, s = pl.program_id(0), pl.program_id(1)
        first_tile = s == ns - 1

        @pl.when((b == 0) & (s == 0))
        def _():
            for ref in (gcw_ref, gcb_ref, gwx_ref, gwa_ref, gbx_ref, gba_ref, glam_ref):
                ref[...] = jnp.zeros(ref.shape, F32)

        @pl.when(s == 0)
        def _():
            dext[t:t + 8, :] = jnp.zeros((8, D), F32)
            carry[...] = jnp.zeros((8, D), F32)

        keep = jnp.where(first_tile, 0.0, 1.0)
        xp = xp_ref[0]
        ext[0:8, :] = xph_ref[0] * keep
        ext[8:8 + t, :] = xp
        hext[0:8, :] = hh_ref[...] * keep
        hext[8:8 + t, :] = h_ref[...]
        cw = cw_ref[...]
        lam = lam_ref[...]
        taps = _shifted(_groups(ext[...]), (3, 2, 1)) + [xp]
        xa = _conv(taps, cw, cb_ref[...])
        xab, gi, gr, sp, a, mult = _lru_gates(xa, wx_ref, wa_ref, bx_ref[...], ba_ref[...], lam)
        (h_prev,) = _shifted(_groups(hext[...]), (1,))
        ga = ga_ref[0]
        sg = _sigmoid(ga)
        dya_v = dya_ref[...]
        d_ga = dya_v * h_ref[...] * (sg * (1.0 + ga * (1.0 - sg)))
        g_in = dya_v * (ga * sg)

        (an,) = _shifted(jnp.concatenate([_groups(a), jnp.ones((1, 8, D), F32)], axis=0), (-1,))
        an, u = _groups(an), _groups(g_in)
        row = lax.broadcasted_iota(jnp.int32, an.shape, 1)
        for sh in (1, 2, 4):
            a_sh = pltpu.roll(an, 8 - sh, 1)
            u_sh = pltpu.roll(u, 8 - sh, 1)
            m = row < 8 - sh
            u = jnp.where(m, u + an * u_sh, u)
            an = jnp.where(m, an * a_sh, an)
        a_s[...] = an.reshape(t, D)
        u_s[...] = u.reshape(t, D)

        def step(i, c):
            r = pl.multiple_of((t8 - 1 - i) * 8, 8)
            dg = u_s[pl.ds(r, 8), :] + a_s[pl.ds(r, 8), :] * c
            dh_s[pl.ds(r, 8), :] = dg
            return dg[0:1, :]

        lax.fori_loop(0, t8, step, carry[0:1, :], unroll=4)
        dh = dh_s[...]
        carry[0:1, :] = a[0:1, :] * dh[0:1, :]

        d_a = dh * h_prev
        dux = dh * xa
        d_mult = dux * gi
        d_gi = dux * mult
        d_xa = dh * (mult * gi)
        d_loga = d_a * a - d_mult * ((a * a) / mult)
        d_gr = d_loga * (-LRU_C * sp)
        d_sp = jnp.sum(d_loga * (-LRU_C * gr), axis=0, keepdims=True)
        glam_ref[...] += d_sp * (-_sigmoid(-lam))
        d_pi = d_gi * gi * (1.0 - gi)
        d_pr = d_gr * gr * (1.0 - gr)
        gbx_ref[...] += jnp.sum(d_pi, axis=0, keepdims=True)
        gba_ref[...] += jnp.sum(d_pr, axis=0, keepdims=True)
        dpib = d_pi.astype(BF16)
        dprb = d_pr.astype(BF16)
        back = []
        for h in range(NB):
            cs = slice(h * BD, (h + 1) * BD)
            gwx_ref[h] += lax.dot_general(xab[:, cs], dpib[:, cs], TN_DIMS, preferred_element_type=F32)
            gwa_ref[h] += lax.dot_general(xab[:, cs], dprb[:, cs], TN_DIMS, preferred_element_type=F32)
            back.append(lax.dot_general(dpib[:, cs], wx_ref[h], NT_DIMS, preferred_element_type=F32)
                        + lax.dot_general(dprb[:, cs], wa_ref[h], NT_DIMS, preferred_element_type=F32))
        d_xa = d_xa + jnp.concatenate(back, axis=1)

        dext[0:t, :] = d_xa
        later = _shifted(_groups(dext[...]), (-3, -2, -1))
        d_xp = later[0] * cw[0:1, :] + later[1] * cw[1:2, :]
        d_xp = d_xp + later[2] * cw[2:3, :]
        d_xp = d_xp + d_xa * cw[3:4, :]
        dext[t:t + 8, :] = d_xa[0:8, :]
        gcb_ref[...] += jnp.sum(d_xa, axis=0, keepdims=True)
        for k in range(4):
            gcw_ref[k:k + 1, :] += jnp.sum(d_xa * taps[k], axis=0, keepdims=True)
        dz_ref[0] = d_xp.astype(BF16)
        dz_ref[1] = d_ga.astype(BF16)

    rb = lambda b, s: b * ns + (ns - 1 - s)
    halo = lambda b, s: jnp.maximum(rb(b, s) * t8 - 1, 0)
    rep2 = lambda b, s: (0, 0)
    rep3 = lambda b, s: (0, 0, 0)
    return pl.pallas_call(
        body, name="lru_bwd", grid=(nb, ns),
        in_specs=[pl.BlockSpec((1, t, D), lambda b, s: (0, rb(b, s), 0)),
                  pl.BlockSpec((1, 8, D), lambda b, s: (0, halo(b, s), 0)),
                  pl.BlockSpec((1, t, D), lambda b, s: (1, rb(b, s), 0)),
                  pl.BlockSpec((t, D), lambda b, s: (rb(b, s), 0)),
                  pl.BlockSpec((8, D), lambda b, s: (halo(b, s), 0)),
                  pl.BlockSpec((t, D), lambda b, s: (rb(b, s), 0)),
                  pl.BlockSpec((8, D), rep2), pl.BlockSpec((1, D), rep2),
                  pl.BlockSpec((NB, BD, BD), rep3), pl.BlockSpec((NB, BD, BD), rep3),
                  pl.BlockSpec((1, D), rep2), pl.BlockSpec((1, D), rep2), pl.BlockSpec((1, D), rep2)],
        out_specs=[pl.BlockSpec((2, t, D), lambda b, s: (0, rb(b, s), 0)),
                   pl.BlockSpec((8, D), rep2), pl.BlockSpec((1, D), rep2),
                   pl.BlockSpec((NB, BD, BD), rep3), pl.BlockSpec((NB, BD, BD), rep3),
                   pl.BlockSpec((1, D), rep2), pl.BlockSpec((1, D), rep2), pl.BlockSpec((1, D), rep2)],
        out_shape=[SDS((2, n, D), BF16), SDS((8, D), F32), SDS((1, D), F32),
                   SDS((NB, BD, BD), F32), SDS((NB, BD, BD), F32),
                   SDS((1, D), F32), SDS((1, D), F32), SDS((1, D), F32)],
        scratch_shapes=[pltpu.VMEM((t + 8, D), F32), pltpu.VMEM((t + 8, D), F32), pltpu.VMEM((t + 8, D), F32),
                        pltpu.VMEM((t, D), F32), pltpu.VMEM((t, D), F32), pltpu.VMEM((t, D), F32),
                        pltpu.VMEM((8, D), F32)],
        compiler_params=_params(56),
    )(z, z, z, h_all, h_all, dya, cw8, cb, wx, wa, bx, ba, lam)


HG_T = 512
HG_NC = HG_T // CHUNK
BNT_DIMS = (((2,), (2,)), ((0,), (0,)))
BNN_DIMS = (((2,), (1,)), ((0,), (0,)))
BTN_DIMS = (((1,), (1,)), ((0,), (0,)))


def _lower_bound(lg):
    m = jnp.max(lg, axis=0, keepdims=True)
    e = jnp.exp(lg - m)
    return e[0:1, :] / jnp.sum(e, axis=0, keepdims=True)


def _tri(upper):
    r = lax.broadcasted_iota(jnp.int32, (HG_NC, CHUNK, CHUNK), 1)
    c = lax.broadcasted_iota(jnp.int32, (HG_NC, CHUNK, CHUNK), 2)
    return (c >= r) if upper else (r >= c)


def _bdot(a, b, dims):
    return lax.dot_general(a, b, dims, preferred_element_type=F32)


def _tri_sums(upper, a):
    tri = _tri(upper).astype(BF16)
    a1 = a.astype(BF16)
    r1 = a - a1.astype(F32)
    a2 = r1.astype(BF16)
    a3 = (r1 - a2.astype(F32)).astype(BF16)
    return _bdot(tri, a1, BNN_DIMS) + (_bdot(tri, a2, BNN_DIMS) + _bdot(tri, a3, BNN_DIMS))


def _chunks(a):
    return a.reshape(HG_NC, CHUNK, BD)


def _hg_tile(q, fp, lb):
    q, fp = _chunks(q), _chunks(fp)
    sig = _sigmoid(fp)
    f = lb + (1.0 - lb) * sig
    log_f = jnp.log(f)
    k = 1.0 - f
    b = _tri_sums(False, log_f)
    b_mid = b[:, CHUNK // 2:CHUNK // 2 + 1, :]
    b_last = b[:, CHUNK - 1:CHUNK, :]
    sq = _sigmoid(q)
    qh = q * sq
    e_qi = jnp.exp(b - b_mid)
    e_ki = jnp.exp(b_mid - b)
    e_qs = jnp.exp(b)
    e_ks = jnp.exp(b_last - b)
    dc = jnp.exp(b_last)
    q_in = (qh * e_qi) * HG_SCALE
    k_in = k * e_ki
    q_st = (qh * e_qs) * HG_SCALE
    k_st = k * e_ks
    att = _bdot(q_in.astype(BF16), k_in.astype(BF16), BNT_DIMS)
    att = jnp.where(_tri(False), att, 0.0)
    return dict(q=q, sig=sig, f=f, k=k, sq=sq, e_qi=e_qi, e_ki=e_ki, e_qs=e_qs, e_ks=e_ks, dc=dc,
                q_in=q_in, k_in=k_in, q_st=q_st, k_st=k_st, att=att)


def _hgrn_fwd(z, lb_logits, hg_g, nb, s_len):
    n = nb * s_len
    t = HG_T
    ns = s_len // t
    nchunk = s_len // CHUNK

    def body(q_ref, f_ref, v_ref, gb_ref, lg_ref, g_ref, o_ref, yb_ref, st_ref, st):
        @pl.when(pl.program_id(1) == 0)
        def _():
            st[...] = jnp.zeros((NB, BD, BD), F32)

        def head(h, carry):
            cols = pl.ds(pl.multiple_of(h * BD, BD), BD)
            lb = _lower_bound(lg_ref[:, cols])
            ck = _hg_tile(q_ref[0, :, cols], f_ref[0, :, cols], lb)
            vb = _chunks(v_ref[0, :, cols]).astype(BF16)
            kv = _bdot(vb, ck["k_st"].astype(BF16), BTN_DIMS)
            states = [st[h]]
            for c in range(HG_NC):
                states.append(states[c] * ck["dc"][c] + kv[c])
            st[h] = states[HG_NC]
            s_in = jnp.stack(states[:HG_NC], axis=0)
            st_ref[h] = s_in
            o = (_bdot(ck["att"].astype(BF16), vb, BNN_DIMS)
                 + _bdot(ck["q_st"].astype(BF16), s_in.astype(BF16), BNT_DIMS))
            o_ref[:, cols] = o.reshape(t, BD)
            r = lax.rsqrt(jnp.mean(o * o, axis=-1, keepdims=True) + EPS)
            gb = _chunks(gb_ref[0, :, cols])
            yb_ref[:, cols] = (((o * r) * g_ref[...]) * (gb * _sigmoid(gb))).astype(BF16).reshape(t, BD)
            return carry

        lax.fori_loop(0, NB, head, 0, unroll=4)

    seg = lambda j: pl.BlockSpec((1, t, D), lambda b, s: (j, b * ns + s, 0))
    tile = pl.BlockSpec((t, D), lambda b, s: (b * ns + s, 0))
    return pl.pallas_call(
        body, name="hgrn_fwd", grid=(nb, ns),
        in_specs=[seg(2), seg(3), seg(4), seg(5),
                  pl.BlockSpec((2, D), lambda b, s: (0, 0)), pl.BlockSpec((1, BD), lambda b, s: (0, 0))],
        out_specs=[tile, tile, pl.BlockSpec((NB, HG_NC, BD, BD), lambda b, s: (b, s, 0, 0))],
        out_shape=[SDS((n, D), F32), SDS((n, D), BF16), SDS((nb * NB, nchunk, BD, BD), F32)],
        scratch_shapes=[pltpu.VMEM((NB, BD, BD), F32)],
        compiler_params=_params(56),
    )(z, z, z, z, lb_logits, hg_g)


def _hgrn_bwd(z, o_all, st_all, dyb, lb_logits, hg_g, nb, s_len):
    n = nb * s_len
    t = HG_T
    ns = s_len // t

    def body(q_ref, f_ref, v_ref, gb_ref, o_ref, st_ref, dyb_ref, lg_ref, g_ref,
             dz_ref, glg_ref, ghg_ref, dst, dlb):
        b, s = pl.program_id(0), pl.program_id(1)

        @pl.when((b == 0) & (s == 0))
        def _():
            ghg_ref[...] = jnp.zeros((1, BD), F32)
            dlb[...] = jnp.zeros((8, D), F32)

        @pl.when(s == 0)
        def _():
            dst[...] = jnp.zeros((NB, BD, BD), F32)

        g = g_ref[...]

        def head(h, carry):
            cols = pl.ds(pl.multiple_of(h * BD, BD), BD)
            lb = _lower_bound(lg_ref[:, cols])
            ck = _hg_tile(q_ref[0, :, cols], f_ref[0, :, cols], lb)
            q = ck["q"]
            vb = _chunks(v_ref[0, :, cols]).astype(BF16)
            gb = _chunks(gb_ref[0, :, cols])
            o = _chunks(o_ref[:, cols])
            dyb_v = _chunks(dyb_ref[:, cols])
            s_in = st_ref[h]

            sgb = _sigmoid(gb)
            r = lax.rsqrt(jnp.mean(o * o, axis=-1, keepdims=True) + EPS)
            ohat = o * r
            d_on = dyb_v * (gb * sgb)
            d_gb = dyb_v * (ohat * g) * (sgb * (1.0 + gb * (1.0 - sgb)))
            ghg_ref[...] += jnp.sum(jnp.sum(d_on * ohat, axis=1), axis=0, keepdims=True)
            tt = d_on * g
            d_o = r * (tt - ohat * jnp.mean(tt * ohat, axis=-1, keepdims=True))
            dob = d_o.astype(BF16)

            attb = ck["att"].astype(BF16)
            q_inb, k_inb = ck["q_in"].astype(BF16), ck["k_in"].astype(BF16)
            q_stb, k_stb = ck["q_st"].astype(BF16), ck["k_st"].astype(BF16)
            d_att = jnp.where(_tri(False), _bdot(dob, vb, BNT_DIMS), 0.0).astype(BF16)
            d_q_in = _bdot(d_att, k_inb, BNN_DIMS)
            d_k_in = _bdot(d_att, q_inb, BTN_DIMS)
            d_q_st = _bdot(dob, s_in.astype(BF16), BNN_DIMS)
            qdo = _bdot(dob, q_stb, BTN_DIMS)
            d_states = [None] * HG_NC + [dst[h]]
            for c in reversed(range(HG_NC)):
                d_states[c] = d_states[c + 1] * ck["dc"][c] + qdo[c]
            dst[h] = d_states[0]
            ds_out = jnp.stack(d_states[1:], axis=0)
            dsb = ds_out.astype(BF16)
            d_v = _bdot(attb, dob, BTN_DIMS) + _bdot(k_stb, dsb, BNT_DIMS)
            d_k_st = _bdot(vb, dsb, BNN_DIMS)
            d_dc = jnp.sum(ds_out * s_in, axis=1, keepdims=True)

            p_qi = d_q_in * ck["q_in"]
            p_ki = d_k_in * ck["k_in"]
            p_qs = d_q_st * ck["q_st"]
            p_ks = d_k_st * ck["k_st"]
            d_qh = (d_q_in * ck["e_qi"] + d_q_st * ck["e_qs"]) * HG_SCALE
            d_k = d_k_in * ck["e_ki"] + d_k_st * ck["e_ks"]
            d_b = (p_qi - p_ki) + (p_qs - p_ks)
            d_b_mid = jnp.sum(p_ki - p_qi, axis=1, keepdims=True)
            d_b_last = jnp.sum(p_ks, axis=1, keepdims=True) + d_dc * ck["dc"]
            rowi = lax.broadcasted_iota(jnp.int32, (HG_NC, CHUNK, BD), 1)
            d_b = d_b + jnp.where(rowi == CHUNK // 2, d_b_mid, 0.0) + jnp.where(rowi == CHUNK - 1, d_b_last, 0.0)
            d_logf = _tri_sums(True, d_b)
            d_f = d_logf / ck["f"] - d_k
            sig, sq = ck["sig"], ck["sq"]
            d_fp = d_f * (1.0 - lb) * (sig * (1.0 - sig))
            dlb[0:1, cols] += jnp.sum(jnp.sum(d_f * (1.0 - sig), axis=1), axis=0, keepdims=True)
            d_q = d_qh * (sq * (1.0 + q * (1.0 - sq)))
            dz_ref[0, :, cols] = d_q.astype(BF16).reshape(t, BD)
            dz_ref[1, :, cols] = d_fp.astype(BF16).reshape(t, BD)
            dz_ref[2, :, cols] = d_v.astype(BF16).reshape(t, BD)
            dz_ref[3, :, cols] = d_gb.astype(BF16).reshape(t, BD)
            return carry

        lax.fori_loop(0, NB, head, 0, unroll=2)

        @pl.when((b == nb - 1) & (s == ns - 1))
        def _():
            lb = _lower_bound(lg_ref[...])
            dl = dlb[0:1, :] * (lb * (1.0 - lb))
            glg_ref[0:1, :] = dl
            glg_ref[1:2, :] = -dl

    rb = lambda b, s: b * ns + (ns - 1 - s)
    seg = lambda j: pl.BlockSpec((1, t, D), lambda b, s: (j, rb(b, s), 0))
    tile = pl.BlockSpec((t, D), lambda b, s: (rb(b, s), 0))
    return pl.pallas_call(
        body, name="hgrn_bwd", grid=(nb, ns),
        in_specs=[seg(2), seg(3), seg(4), seg(5), tile,
                  pl.BlockSpec((NB, HG_NC, BD, BD), lambda b, s: (b, ns - 1 - s, 0, 0)),
                  tile, pl.BlockSpec((2, D), lambda b, s: (0, 0)), pl.BlockSpec((1, BD), lambda b, s: (0, 0))],
        out_specs=[pl.BlockSpec((4, t, D), lambda b, s: (0, rb(b, s), 0)),
                   pl.BlockSpec((2, D), lambda b, s: (0, 0)), pl.BlockSpec((1, BD), lambda b, s: (0, 0))],
        out_shape=[SDS((4, n, D), BF16), SDS((2, D), F32), SDS((1, BD), F32)],
        scratch_shapes=[pltpu.VMEM((NB, BD, BD), F32), pltpu.VMEM((8, D), F32)],
        compiler_params=_params(60),
    )(z, z, z, z, o_all, st_all, dyb, lb_logits, hg_g)


def _mid(ya, yb, z, b_merge, x2, tgt, fin_g, pa, pb, wo):
    n = x2.shape[0]
    tm = 256
    ni = n // tm

    def body(ya_ref, yb_ref, gma_ref, gmb_ref, bm_ref, x_ref, t_ref, fg_ref, pa_hbm, pb_hbm, wo_hbm,
             dx2_ref, dya_ref, dyb_ref, dgm_ref, loss_ref, gfg_ref, gbm_ref, gm_hbm,
             pa_v, pb_v, wo_v, gpa_v, gpb_v, gwo_v, sem):
        i = pl.program_id(0)
        by_owner = lambda g: g.reshape(NB, BD, D)
        loads = [pltpu.make_async_copy(src, dst, sem.at[k])
                 for k, (src, dst) in enumerate(((pa_hbm, pa_v), (pb_hbm, pb_v), (wo_hbm, wo_v)))]
        stores = [pltpu.make_async_copy(src, dst, sem.at[k])
                  for k, (src, dst) in enumerate((g, gm_hbm.at[:, pl.ds(slot * BD, BD), :])
                                                 for slot, g in enumerate((gpa_v, gpb_v, gwo_v)))]

        @pl.when(i == 0)
        def _():
            for cp in loads:
                cp.start()
            for ref in (gpa_v, gpb_v, gwo_v, loss_ref, gfg_ref, gbm_ref):
                ref[...] = jnp.zeros(ref.shape, F32)
            for cp in loads:
                cp.wait()

        ya_v = ya_ref[...]
        yb_v = yb_ref[...]
        out_a = jnp.dot(ya_v, pa_v[...], preferred_element_type=F32)
        out_b = jnp.dot(yb_v, pb_v[...], preferred_element_type=F32)
        bm = bm_ref[...]
        g_a = _sigmoid(gma_ref[0] + bm[:, 0:D])
        g_b = _sigmoid(gmb_ref[0] + bm[:, D:2 * D])
        mixed = g_a * out_a + g_b * out_b
        mixb = mixed.astype(BF16)
        xo = x_ref[...] + jnp.dot(mixb, wo_v[...], preferred_element_type=F32)
        r = lax.rsqrt(jnp.mean(xo * xo, axis=-1, keepdims=True) + EPS)
        xn = xo * r
        fg = fg_ref[...]
        e = xn * fg - t_ref[...]
        loss_ref[...] += 0.5 * jnp.sum(jnp.mean(e * e, axis=-1, keepdims=True))
        dy = e * (1.0 / D)
        gfg_ref[...] += jnp.sum(dy * xn, axis=0, keepdims=True)
        dxn = dy * fg
        dx2 = r * (dxn - xn * jnp.mean(dxn * xn, axis=-1, keepdims=True))
        dx2_ref[...] = dx2
        dx2b = dx2.astype(BF16)
        d_mixed = lax.dot_general(dx2b, wo_v[...], NT_DIMS, preferred_element_type=F32)
        gwo_v[...] += by_owner(lax.dot_general(mixb, dx2b, TN_DIMS, preferred_element_type=F32))
        d_oa = (d_mixed * g_a).astype(BF16)
        d_ob = (d_mixed * g_b).astype(BF16)
        dgm_a = (d_mixed * out_a) * (g_a * (1.0 - g_a))
        dgm_b = (d_mixed * out_b) * (g_b * (1.0 - g_b))
        gbm_ref[:, 0:D] += jnp.sum(dgm_a, axis=0, keepdims=True)
        gbm_ref[:, D:2 * D] += jnp.sum(dgm_b, axis=0, keepdims=True)
        dgm_ref[0] = dgm_a.astype(BF16)
        dgm_ref[1] = dgm_b.astype(BF16)
        dya_ref[...] = lax.dot_general(d_oa, pa_v[...], NT_DIMS, preferred_element_type=F32)
        dyb_ref[...] = lax.dot_general(d_ob, pb_v[...], NT_DIMS, preferred_element_type=F32)
        gpa_v[...] += by_owner(lax.dot_general(ya_v, d_oa, TN_DIMS, preferred_element_type=F32))
        gpb_v[...] += by_owner(lax.dot_general(yb_v, d_ob, TN_DIMS, preferred_element_type=F32))

        @pl.when(i == ni - 1)
        def _():
            for cp in stores:
                cp.start()
            for cp in stores:
                cp.wait()

    rows = pl.BlockSpec((tm, D), lambda i: (i, 0))
    rep = lambda shape: pl.BlockSpec(shape, lambda i: (0,) * len(shape))
    return pl.pallas_call(
        body, name="mid", grid=(ni,),
        in_specs=[rows, rows,
                  pl.BlockSpec((1, tm, D), lambda i: (6, i, 0)), pl.BlockSpec((1, tm, D), lambda i: (7, i, 0)),
                  rep((1, 2 * D)), rows, rows, rep((1, D)), ANY, ANY, ANY],
        out_specs=[rows, rows, rows, pl.BlockSpec((2, tm, D), lambda i: (0, i, 0)),
                   rep((8, BD)), rep((1, D)), rep((1, 2 * D)), ANY],
        out_shape=[SDS((n, D), F32), SDS((n, D), F32), SDS((n, D), F32), SDS((2, n, D), BF16),
                   SDS((8, BD), F32), SDS((1, D), F32), SDS((1, 2 * D), F32),
                   SDS((NB, MID_ROWS, D), F32)],
        scratch_shapes=[pltpu.VMEM((D, D), BF16)] * 3 + [pltpu.VMEM((NB, BD, D), F32)] * 3 + [pltpu.SemaphoreType.DMA((3,))],
        compiler_params=_params(60),
    )(ya, yb, z, z, b_merge, x2, tgt, fin_g, pa, pb, wo)


def _dz_specs(tm, ni, row_major):
    if row_major:
        ia = lambda i, j: (jnp.minimum(j, 1), i, 0)
        ib = lambda i, j: (jnp.clip(j - 2, 0, 3), i, 0)
        im = lambda i, j: (jnp.clip(j - 6, 0, 1), i, 0)
    else:
        last = ni - 1
        ia = lambda j, i: (jnp.minimum(j, 1), jnp.where(j < 2, i, last), 0)
        ib = lambda j, i: (jnp.clip(j - 2, 0, 3), jnp.where(j < 2, 0, jnp.where(j < 6, i, last)), 0)
        im = lambda j, i: (jnp.clip(j - 6, 0, 1), jnp.where(j < 6, 0, i), 0)
    return [pl.BlockSpec((1, tm, D), f) for f in (ia, ib, im)]


def _inproj_bwd_x(dza, dzb, dzm, w_all, x2, dx2, norm_g, after):
    n = x2.shape[0]
    tm = 512
    ni = n // tm

    def body(dza_ref, dzb_ref, dzm_ref, w_ref, x_ref, dx2_ref, g_ref, after_ref, gx_ref, gg_ref, acc):
        i, j = pl.program_id(0), pl.program_id(1)

        @pl.when((i == 0) & (j == 0))
        def _():
            gg_ref[...] = jnp.zeros((1, D), F32)

        @pl.when(j == 0)
        def _():
            acc[...] = jnp.zeros((tm, D), F32)

        def add(ref):
            acc[...] += lax.dot_general(ref[0], w_ref[0], NT_DIMS, preferred_element_type=F32)

        pl.when(j < 2)(lambda: add(dza_ref))
        pl.when((j >= 2) & (j < 6))(lambda: add(dzb_ref))
        pl.when(j >= 6)(lambda: add(dzm_ref))

        @pl.when(j == NB - 1)
        def _():
            x = x_ref[...]
            r = lax.rsqrt(jnp.mean(x * x, axis=-1, keepdims=True) + EPS)
            xn = x * r
            dh = acc[...]
            gg_ref[...] += jnp.sum(dh * xn, axis=0, keepdims=True)
            dxn = dh * g_ref[...]
            gx_ref[...] = dx2_ref[...] + r * (dxn - xn * jnp.mean(dxn * xn, axis=-1, keepdims=True))

    rows = pl.BlockSpec((tm, D), lambda i, j: (i, 0))
    return pl.pallas_call(
        body, name="inproj_bwd_x", grid=(ni, NB),
        in_specs=_dz_specs(tm, ni, True) + [pl.BlockSpec((1, D, D), lambda i, j: (j, 0, 0)), rows, rows,
                                             pl.BlockSpec((1, D), lambda i, j: (0, 0)), ANY],
        out_specs=[rows, pl.BlockSpec((1, D), lambda i, j: (0, 0))],
        out_shape=[SDS((n, D), F32), SDS((1, D), F32)],
        scratch_shapes=[pltpu.VMEM((tm, D), F32)],
        compiler_params=_params(48),
    )(dza, dzb, dzm, w_all, x2, dx2, norm_g, after)


def _walk_tables(order, ni):
    rows = []
    for lo, hi in ((0, 2), (2, 6), (6, 8)):
        active = [j for j, g in enumerate(order) if lo <= g < hi]
        block, tile = [], []
        for j, g in enumerate(order):
            before = [a for a in active if a < j]
            if lo <= g < hi:
                block.append(g - lo), tile.append(-1)
            elif before:
                block.append(order[before[-1]] - lo), tile.append(ni - 1)
            else:
                block.append(order[active[0]] - lo), tile.append(0)
        rows += [block, tile]
    return rows


def _inproj_bwd_w(core, dza, dzb, dzm, h_all, g_m):
    n = h_all.shape[0]
    tm = min(n, 2048)
    ni = n // tm
    packed = g_m.shape[1:]
    orders = [[2 * q + 1 - c for q in range(4)] + [2 * q + c for q in range(4)] for c in (0, 1)]
    tables = jnp.asarray([[order] + _walk_tables(order, ni) for order in orders], jnp.int32)
    walk = jnp.where(core == 0, tables[0], tables[1])

    def body(walk_ref, dza_ref, dzb_ref, dzm_ref, h_ref, gm_hbm, out_bf, own_f32, m_out_bf, m_own_f32, got_w, got_m,
             acc, stage, theirs, m_mine, m_theirs, m_stage, send_sems, recv_sems, local_sems):
        j, i = pl.program_id(0), pl.program_id(1)
        group = walk_ref[0, j]
        x, y, c = _place()
        sibling = (x, y, 1 - c)

        def send_w(q):
            return pltpu.make_async_remote_copy(
                src_ref=stage.at[q % 2], dst_ref=got_w.at[q], send_sem=send_sems.at[q], recv_sem=recv_sems.at[q],
                device_id=sibling, device_id_type=MESH)

        def send_m(q):
            return pltpu.make_async_remote_copy(
                src_ref=gm_hbm.at[2 * q + (1 - c)], dst_ref=got_m.at[q], send_sem=send_sems.at[4 + q],
                recv_sem=recv_sems.at[4 + q], device_id=sibling, device_id_type=MESH)

        def fetch(q):
            return pltpu.make_async_copy(got_w.at[q], theirs, local_sems.at[0])

        def fetch_m(q):
            return (pltpu.make_async_copy(gm_hbm.at[2 * q + c], m_mine, local_sems.at[2]),
                    pltpu.make_async_copy(got_m.at[q], m_theirs, local_sems.at[3]))

        @pl.when((j == 0) & (i == 0))
        def _():
            for q in range(4):
                send_m(q).start()

        @pl.when(i == 0)
        def _():
            acc[...] = jnp.zeros((D, D), F32)

        def add(ref):
            acc[...] += lax.dot_general(h_ref[...], ref[0], TN_DIMS, preferred_element_type=F32)

        pl.when(group < 2)(lambda: add(dza_ref))
        pl.when((group >= 2) & (group < 6))(lambda: add(dzb_ref))
        pl.when(group >= 6)(lambda: add(dzm_ref))

        for q in range(4):
            @pl.when((i == ni - 1) & (j == q))
            def _(q=q):
                if q >= 2:
                    send_w(q - 2).wait_send()
                stage[q % 2] = acc[...].astype(BF16)
                send_w(q).start()

        def reducer(q):
            other_x, other_y = x != q // 2, y != q % 2
            return other_x | other_y, jnp.where(other_x & other_y, 2, jnp.where(other_x, 0, 1))

        def w_out(q):
            return pltpu.make_async_copy(stage.at[0], out_bf.at[reducer(q)[1]], local_sems.at[1])

        for q in range(4):
            @pl.when((j == 4 + q) & (i == 0))
            def _(q=q):
                send_w(q).wait_recv()
                send_m(q).wait_recv()
                fetch(q).start()
                for cp in fetch_m(q):
                    cp.start()
                if q > 0:
                    pl.when(reducer(q - 1)[0])(lambda: w_out(q - 1).wait())

            @pl.when((j == 4 + q) & (i == ni - 1))
            def _(q=q):
                if q == 0:
                    send_w(2).wait_send()
                    send_w(3).wait_send()
                other, slot = reducer(q)
                m_out = pltpu.make_async_copy(m_stage, m_out_bf.at[slot], local_sems.at[4])
                m_own = pltpu.make_async_copy(m_mine, m_own_f32, local_sems.at[4])

                for cp in fetch_m(q):
                    cp.wait()

                @pl.when(other)
                def _():
                    m_stage[...] = (m_mine[...] + m_theirs[...]).astype(BF16)
                    m_out.start()

                @pl.when(jnp.logical_not(other))
                def _():
                    m_mine[...] += m_theirs[...]
                    m_own.start()

                fetch(q).wait()

                @pl.when(other)
                def _():
                    stage[0] = (acc[...] + theirs[...].astype(F32)).astype(BF16)
                    w_out(q).start()
                    if q == 3:
                        w_out(q).wait()
                    m_out.wait()

                @pl.when(jnp.logical_not(other))
                def _():
                    acc[...] += theirs[...].astype(F32)
                    out = pltpu.make_async_copy(acc, own_f32, local_sems.at[1])
                    out.start()
                    out.wait()
                    m_own.wait()

        @pl.when((j == NB - 1) & (i == ni - 1))
        def _():
            for q in range(4):
                send_m(q).wait_send()

    def dz_spec(k):
        return pl.BlockSpec((1, tm, D), lambda j, i, w: (w[1 + 2 * k, j], jnp.where(w[2 + 2 * k, j] < 0, i, w[2 + 2 * k, j]), 0))

    return pl.pallas_call(
        body, name="inproj_bwd_w",
        grid_spec=pltpu.PrefetchScalarGridSpec(
            num_scalar_prefetch=1, grid=(NB, ni),
            in_specs=[dz_spec(0), dz_spec(1), dz_spec(2), pl.BlockSpec((tm, D), lambda j, i, w: (i, 0)), ANY],
            out_specs=[ANY] * 6,
            scratch_shapes=[pltpu.VMEM((D, D), F32), pltpu.VMEM((2, D, D), BF16), pltpu.VMEM((D, D), BF16),
                            pltpu.VMEM(packed, F32), pltpu.VMEM(packed, F32), pltpu.VMEM(packed, BF16),
                            pltpu.SemaphoreType.DMA((8,)), pltpu.SemaphoreType.DMA((8,)), pltpu.SemaphoreType.DMA((5,))]),
        out_shape=[SDS((3, D, D), BF16), SDS((D, D), F32), SDS((3,) + packed, BF16), SDS(packed, F32),
                   SDS((4, D, D), BF16), SDS((4,) + packed, F32)],
        compiler_params=_params(58),
    )(walk, dza, dzb, dzm, h_all, g_m)


def _adamw(w, g, m, v):
    rows, cols = w.shape
    tr = _row_tile(rows)

    spec = pl.BlockSpec((tr, cols), lambda i: (i, 0))
    if rows // tr >= 4:
        ahead = pl.BlockSpec((tr, cols), lambda i: (i, 0), pipeline_mode=pl.Buffered(3))

        def streamed(*refs):
            pltpu.emit_pipeline(_adam_refs, grid=(rows // tr,), in_specs=[ahead] * 4, out_specs=[spec] * 3)(*refs)

        return pl.pallas_call(
            streamed, name="adamw_streamed", in_specs=[ANY] * 4, out_specs=[ANY] * 3,
            out_shape=[SDS((rows, cols), F32)] * 3, compiler_params=_params(32),
        )(w, g, m, v)
    return pl.pallas_call(
        functools.partial(_adam_refs), name="adamw", grid=(rows // tr,), in_specs=[spec] * 4, out_specs=[spec] * 3,
        out_shape=[SDS((rows, cols), F32)] * 3, compiler_params=_params(32),
    )(w, g, m, v)


def _adam_refs(w_ref, g_ref, m_ref, v_ref, d_ref, nm_ref, nv_ref):
    gv = g_ref[...]
    nm = ADAM_B1 * m_ref[...] + (1.0 - ADAM_B1) * gv
    nv = ADAM_B2 * v_ref[...] + (1.0 - ADAM_B2) * (gv * gv)
    m_hat = nm / (1.0 - ADAM_B1 ** ADAM_STEP)
    v_hat = nv / (1.0 - ADAM_B2 ** ADAM_STEP)
    d_ref[...] = -ADAM_LR * (m_hat / (jnp.sqrt(v_hat) + ADAM_EPS) + ADAM_WD * w_ref[...])
    nm_ref[...] = nm
    nv_ref[...] = nv


def _adamw_small(ws, gs, ms, vs):
    k = len(ws)

    def body(*refs):
        ins, outs = refs[:4 * k], refs[4 * k:7 * k]
        vin, vout = refs[7 * k:11 * k], refs[11 * k:14 * k]
        load_sems, store_sems = refs[14 * k:]
        loads = [pltpu.make_async_copy(ins[i], vin[i], load_sems.at[i]) for i in range(4 * k)]
        for cp in loads:
            cp.start()
        for cp in loads:
            cp.wait()
        for i in range(k):
            _adam_refs(*[vin[part * k + i] for part in range(4)], *[vout[part * k + i] for part in range(3)])
        stores = [pltpu.make_async_copy(vout[i], outs[i], store_sems.at[i]) for i in range(3 * k)]
        for cp in stores:
            cp.start()
        for cp in stores:
            cp.wait()

    shapes = [SDS(w.shape, F32) for w in ws]
    vmem = [pltpu.VMEM(w.shape, F32) for w in ws]
    out = pl.pallas_call(
        body, name="adamw_small", out_shape=shapes * 3, in_specs=[HBM] * (4 * k), out_specs=[HBM] * (3 * k),
        scratch_shapes=vmem * 7 + [pltpu.SemaphoreType.DMA((4 * k,)), pltpu.SemaphoreType.DMA((3 * k,))],
        compiler_params=_params(32),
    )(*ws, *gs, *ms, *vs)
    return out[:k], out[k:2 * k], out[2 * k:]


def _allgather(blocks, dtypes, name):
    na = len(blocks)

    def body(*refs):
        ins, outs, stages = refs[:na], refs[na:2 * na], refs[2 * na:3 * na]
        send_sems, recv_sems, local_sems = refs[3 * na:]
        x, y, c = _place()
        me, sibling = (x, y, c), (x, y, 1 - c)
        chips = [(1 - x, y), (x, 1 - y), (1 - x, 1 - y)]
        blk = lambda p: 4 * p[0] + 2 * p[1] + p[2]

        def copy(a, k, block, to, src=None):
            return pltpu.make_async_remote_copy(
                src_ref=outs[a].at[blk(block)] if src is None else src, dst_ref=outs[a].at[blk(block)],
                send_sem=send_sems.at[7 * a + k], recv_sem=recv_sems.at[7 * a + k],
                device_id=to, device_id_type=MESH)

        mine, first, passed = [], [], []
        for a in range(na):
            stages[a][...] = ins[a][...].astype(dtypes[a])
            mine.append(pltpu.make_async_copy(stages[a], outs[a].at[blk(me)], local_sems.at[a]))
            mine[-1].start()
            first.append(copy(a, 0, me, sibling, src=stages[a]))
            first += [copy(a, 1 + j, me, (*chip, c), src=stages[a]) for j, chip in enumerate(chips)]
        for cp in first:
            cp.start()
        for j, chip in enumerate(chips):
            for a in range(na):
                copy(a, 1 + j, (*chip, c), me).wait_recv()
                passed.append(copy(a, 4 + j, (*chip, c), sibling))
                passed[-1].start()
        for a in range(na):
            copy(a, 0, sibling, me).wait_recv()
            for j, chip in enumerate(chips):
                copy(a, 4 + j, (*chip, 1 - c), me).wait_recv()
        for cp in first + passed:
            cp.wait_send()
        for cp in mine:
            cp.wait()

    return pl.pallas_call(
        body, name=name,
        in_specs=[pl.BlockSpec(memory_space=pltpu.VMEM)] * na, out_specs=[ANY] * na,
        out_shape=[SDS((NB,) + b.shape, dt) for b, dt in zip(blocks, dtypes)],
        scratch_shapes=[pltpu.VMEM(b.shape, dt) for b, dt in zip(blocks, dtypes)]
        + [pltpu.SemaphoreType.DMA((7 * na,)), pltpu.SemaphoreType.DMA((7 * na,)), pltpu.SemaphoreType.DMA((na,))],
        compiler_params=_params(40),
    )(*blocks)


HBM = pl.BlockSpec(memory_space=pltpu.HBM)
SEMS = pl.BlockSpec(memory_space=pltpu.SEMAPHORE)
EFFECT = pltpu.SideEffectType.DATAFLOW_SIDE_EFFECTING


def _chip_copies(srcs, lands, send_sems, recv_sems):
    x, y, c = _place()
    return [pltpu.make_async_remote_copy(
        src_ref=srcs[a].at[slot], dst_ref=lands[a].at[slot],
        send_sem=send_sems.at[3 * a + slot], recv_sem=recv_sems.at[3 * a + slot],
        device_id=(px, py, c), device_id_type=MESH)
        for a in range(len(srcs)) for slot, (px, py) in enumerate(_other_chips(x, y))]


def _split_start(name, copies, per_array, srcs, lands, after=None):
    na = len(srcs)

    def body(*refs):
        send_sems, recv_sems = refs[-2 * na - 3], refs[-2 * na - 2]
        for cp in copies(refs[:na], refs[na:2 * na], send_sems, recv_sems):
            cp.start()
        refs[-1][...] = jnp.zeros_like(refs[-1])

    hbm = lambda a: pltpu.HBM(a.shape, a.dtype)
    pin = lambda a: pltpu.with_memory_space_constraint(a, pltpu.HBM)
    out = pl.pallas_call(
        body, name=name,
        out_shape=(pltpu.SemaphoreType.DMA((per_array * na,)), pltpu.SemaphoreType.DMA((per_array * na,)),
                   *[hbm(a) for a in srcs], *[hbm(a) for a in lands], SDS((8, BD), F32)),
        in_specs=[HBM] * (2 * na) + ([] if after is None else [ANY]),
        out_specs=(SEMS, SEMS, *[HBM] * (2 * na), pl.BlockSpec(memory_space=pltpu.VMEM)),
        input_output_aliases={i: 2 + i for i in range(2 * na)},
        compiler_params=pltpu.CompilerParams(has_side_effects=EFFECT),
    )(*[pin(a) for a in srcs], *[pin(a) for a in lands], *([] if after is None else [after]))
    return out[0], out[1], out[2:2 + na], out[2 + na:2 + 2 * na], out[-1]


def _split_wait(name, copies, started, after):
    send_sems, recv_sems, srcs, lands, _ = started
    na = len(srcs)

    def body(*refs):
        waits = copies(refs[:na], refs[na:2 * na], refs[2 * na], refs[2 * na + 1])
        for cp in waits:
            cp.wait_send()
        for cp in waits:
            cp.wait_recv()

    hbm = lambda a: pltpu.HBM(a.shape, a.dtype)
    out = pl.pallas_call(
        body, name=name,
        out_shape=(*[hbm(a) for a in srcs], *[hbm(a) for a in lands]),
        in_specs=[HBM] * (2 * na) + [SEMS, SEMS, ANY],
        out_specs=tuple([HBM] * (2 * na)),
        input_output_aliases={i: i for i in range(2 * na)},
        compiler_params=pltpu.CompilerParams(has_side_effects=EFFECT),
    )(*srcs, *lands, send_sems, recv_sems, after)
    return out[na:]


def _add_chips(own, b_in):
    r, cols = own.shape
    tr = _row_tile(r)

    def body(p_ref, b0_ref, b1_ref, b2_ref, o_ref):
        o_ref[...] = ((p_ref[...] + b0_ref[0].astype(F32)) + b1_ref[0].astype(F32)) + b2_ref[0].astype(F32)

    slot = lambda k: pl.BlockSpec((1, tr, cols), lambda i: (k, i, 0))
    spec = pl.BlockSpec((tr, cols), lambda i: (i, 0))
    return pl.pallas_call(
        body, name="add_chips", grid=(r // tr,), in_specs=[spec, slot(0), slot(1), slot(2)], out_specs=spec,
        out_shape=SDS((r, cols), F32), compiler_params=_params(32),
    )(own, b_in, b_in, b_in)


VEC_NAMES = ("b_merge", "conv_b", "rg_bx", "rg_ba", "rg_lambda", "hg_lb_logits", "hg_norm_g", "final_norm_g")
REP_NAMES = ("rg_wx", "rg_wa", "norm_g") + VEC_NAMES
SMALL_AT = 3 * BD
SMALL_ROWS = 48
MID_ROWS = 448


def _sum_blocks(parts):
    def body(p_ref, o_ref):
        acc = p_ref[0]
        for k in range(1, NB):
            acc = acc + p_ref[k]
        o_ref[...] = acc

    return pl.pallas_call(body, name="sum_blocks", out_shape=SDS(parts.shape[1:], F32))(parts)


def _pack_rows(arrays, width, row_multiple=8):
    flat = jnp.concatenate([a.reshape(-1) for a in arrays])
    rows = -(-flat.shape[0] // width)
    rows = -(-rows // row_multiple) * row_multiple
    return jnp.pad(flat, (0, rows * width - flat.shape[0])).reshape(rows, width)


def _unpack(flat, like):
    out, off = [], 0
    for a in like:
        out.append(flat[off:off + a.size].reshape(a.shape))
        off += a.size
    return out


def kernel(x, w_in, b_merge, conv_w, conv_b, rg_wx, rg_bx, rg_wa, rg_ba, rg_lambda, hg_lb_logits, hg_norm_g, proj_a, proj_b, w_out, norm_g, final_norm_g, loss_target, m_w_in, m_b_merge, m_conv_w, m_conv_b, m_rg_wx, m_rg_bx, m_rg_wa, m_rg_ba, m_rg_lambda, m_hg_lb_logits, m_hg_norm_g, m_proj_a, m_proj_b, m_w_out, m_norm_g, m_final_norm_g, v_w_in, v_b_merge, v_conv_w, v_conv_b, v_rg_wx, v_rg_bx, v_rg_wa, v_rg_ba, v_rg_lambda, v_hg_lb_logits, v_hg_norm_g, v_proj_a, v_proj_b, v_w_out, v_norm_g, v_final_norm_g):
    weights = dict(w_in=w_in, b_merge=b_merge, conv_w=conv_w, conv_b=conv_b, rg_wx=rg_wx, rg_bx=rg_bx, rg_wa=rg_wa,
                   rg_ba=rg_ba, rg_lambda=rg_lambda, hg_lb_logits=hg_lb_logits, hg_norm_g=hg_norm_g, proj_a=proj_a,
                   proj_b=proj_b, w_out=w_out, norm_g=norm_g, final_norm_g=final_norm_g)
    mom1 = dict(w_in=m_w_in, b_merge=m_b_merge, conv_w=m_conv_w, conv_b=m_conv_b, rg_wx=m_rg_wx, rg_bx=m_rg_bx,
                rg_wa=m_rg_wa, rg_ba=m_rg_ba, rg_lambda=m_rg_lambda, hg_lb_logits=m_hg_lb_logits,
                hg_norm_g=m_hg_norm_g, proj_a=m_proj_a, proj_b=m_proj_b, w_out=m_w_out, norm_g=m_norm_g,
                final_norm_g=m_final_norm_g)
    mom2 = dict(w_in=v_w_in, b_merge=v_b_merge, conv_w=v_conv_w, conv_b=v_conv_b, rg_wx=v_rg_wx, rg_bx=v_rg_bx,
                rg_wa=v_rg_wa, rg_ba=v_rg_ba, rg_lambda=v_rg_lambda, hg_lb_logits=v_hg_lb_logits,
                hg_norm_g=v_hg_norm_g, proj_a=v_proj_a, proj_b=v_proj_b, w_out=v_w_out, norm_g=v_norm_g,
                final_norm_g=v_final_norm_g)
    order = list(weights)
    nb, s_len, _ = x.shape
    n = nb * s_len
    px, py, pc = _place()

    in_hbm = lambda a: pltpu.with_memory_space_constraint(a, pltpu.HBM)
    norm_gain = in_hbm(norm_g)

    x2 = x.reshape(n, D)
    cw_blk = jnp.pad(conv_w[0], ((0, 4), (0, 0)))
    order_ids = jnp.stack([_block_id(p) for p in _arrival_order(px, py, pc)]).astype(jnp.int32)
    z, h_all, w_all, pa_all, pb_all, wo_all, cw_all = _gather_inproj(
        order_ids, x2, norm_gain, [w_in[0], proj_a[0], proj_b[0], w_out[0], cw_blk], [BF16, BF16, BF16, BF16, F32])
    pa_full, pb_full, wo_full = (a.reshape(D, D) for a in (pa_all, pb_all, wo_all))
    cw8 = in_hbm(cw_all.transpose(1, 0, 2).reshape(8, D))
    wx_b, wa_b = in_hbm(rg_wx[0].astype(BF16)), in_hbm(rg_wa[0].astype(BF16))
    cb, bx, ba, lam = (in_hbm(a.reshape(1, D)) for a in (conv_b, rg_bx, rg_ba, rg_lambda))
    fin_g, b_mrg = in_hbm(final_norm_g.reshape(1, D)), in_hbm(b_merge)
    lb_lg, hg_g = in_hbm(hg_lb_logits), in_hbm(hg_norm_g)

    hlru, ya = _lru_fwd(z, cw8, cb, wx_b, wa_b, bx, ba, lam, nb, s_len)
    o_all, yb, st_all = _hgrn_fwd(z, lb_lg, hg_g, nb, s_len)

    (dx2, dya, dyb, dzm, loss_acc, g_fin, g_bm, g_mid) = _mid(
        ya, yb, z, b_mrg, x2, loss_target.reshape(n, D), fin_g, pa_full, pb_full, wo_full)
    dzb, g_lg, g_hg = _hgrn_bwd(z, o_all, st_all, dyb, lb_lg, hg_g, nb, s_len)
    dza, g_cw8, g_cb, g_wx, g_wa, g_bx, g_ba, g_lam = _lru_bwd(
        z, hlru, dya, cw8, cb, wx_b, wa_b, bx, ba, lam, nb, s_len)

    part = dict(b_merge=g_bm, conv_b=g_cb, rg_bx=g_bx, rg_ba=g_ba, rg_lambda=g_lam, hg_lb_logits=g_lg,
                hg_norm_g=g_hg, final_norm_g=g_fin)
    vec = _pack_rows([part[k] for k in VEC_NAMES], BD)
    vec = jnp.pad(vec, ((0, 16 * NB - vec.shape[0]), (0, 0))).reshape(NB, 2, D)
    rows8 = lambda a: jnp.pad(a, ((0, 0), (0, 8 - a.shape[1]), (0, 0)))
    small = jnp.concatenate([g_wx.reshape(NB, 16, D), g_wa.reshape(NB, 16, D),
                             rows8(g_cw8.reshape(8, NB, BD).transpose(1, 0, 2).reshape(NB, 1, D)), rows8(vec),
                             jnp.zeros((NB, MID_ROWS - SMALL_AT - SMALL_ROWS, D), F32)], axis=1)
    g_m = lax.dynamic_update_slice(g_mid, small, (0, SMALL_AT, 0))
    w_out_bf, w_own, m_out_bf, m_own, _, _ = _inproj_bwd_w(pc, dza, dzb, dzm, h_all, g_m)
    outgoing = [in_hbm(w_out_bf), in_hbm(m_out_bf)]
    chip_sums = _split_start("rs_chips_start", _chip_copies, 3, outgoing, [lax.empty(a.shape, a.dtype) for a in outgoing])
    grad_x, g_ng = _inproj_bwd_x(dza, dzb, dzm, w_all, x2, dx2, norm_gain, chip_sums[-1])
    from_chips = _split_wait("rs_chips_wait", _chip_copies, chip_sums, grad_x)
    r_w = _add_chips(w_own, from_chips[0])
    r_m = _add_chips(m_own, from_chips[1])
    row = lax.broadcasted_iota(jnp.int32, (8, D), 0)
    mine = jnp.where(row == 0, g_ng, jnp.where(row == 1, loss_acc[0:1, 0:1], 0.0))
    tail = jnp.concatenate([r_m[SMALL_AT:SMALL_AT + SMALL_ROWS], mine], axis=0)
    (tail_all,) = _allgather([tail], [F32], "gather_small_grads")
    summed = _sum_blocks(tail_all[:, SMALL_ROWS:SMALL_ROWS + 8])

    grads = dict(w_in=r_w.reshape(1, D, D),
                 proj_a=r_m[0:BD].reshape(1, BD, D), proj_b=r_m[BD:2 * BD].reshape(1, BD, D),
                 w_out=r_m[2 * BD:3 * BD].reshape(1, BD, D),
                 conv_w=r_m[SMALL_AT + 32].reshape(8, BD)[0:4].reshape(1, 4, BD),
                 rg_wx=tail_all[:, 0:16].reshape(1, NB, BD, BD), rg_wa=tail_all[:, 16:32].reshape(1, NB, BD, BD),
                 norm_g=summed[0:1])
    vec_all = tail_all[:, 40:42].reshape(-1)
    for k, gk in zip(VEC_NAMES, _unpack(vec_all, [weights[k] for k in VEC_NAMES])):
        grads[k] = gk

    delta, new_m, new_v = {}, {}, {}
    flat2 = lambda a: a.reshape(-1, a.shape[-1])
    for k in ("w_in", "proj_a", "proj_b", "w_out"):
        pin = in_hbm if k == "w_in" else (lambda a: a)
        outs = _adamw(*[pin(flat2(t[k])) for t in (weights, grads, mom1, mom2)])
        delta[k], new_m[k], new_v[k] = (a.reshape(weights[k].shape) for a in outs)
    rep = list(REP_NAMES) + ["conv_w"]
    outs = _adamw_small(*[[in_hbm(flat2(t[k])) for k in rep] for t in (weights, grads, mom1, mom2)])
    for tgt, arrays in zip((delta, new_m, new_v), outs):
        for k, a in zip(rep, arrays):
            tgt[k] = a.reshape(weights[k].shape)

    return (summed[1, 0], grad_x.reshape(x.shape), *[grads[k] for k in order], *[delta[k] for k in order],
            *[new_m[k] for k in order], *[new_v[k] for k in order])
```
